```python
import jax, jax.numpy as jnp
from jax import lax
import numpy as np

D_MODEL = 1024
BATCH = 8
SEQ = 2048
DEPTH = 1

HG_HEADS = 4
HG_KEY_DIM = 128
HG_VAL_DIM = 128
HG_KEY_WIDTH = HG_HEADS * HG_KEY_DIM
HG_VAL_WIDTH = HG_HEADS * HG_VAL_DIM
HG_CHUNK = 64
ATT_GROUPS = ((128, 1), (512, 4), (2048, 16))
N_ATT_GROUPS = 3
ATT_HEADS = 8
ATT_HEAD_DIM = 64
ATT_WIDTH = ATT_HEADS * ATT_HEAD_DIM
ATT_BLOCK = 128
ALIBI_MAX = 8.0
D_FF = 2816
IN_SIZES = (HG_KEY_WIDTH, HG_KEY_WIDTH, HG_VAL_WIDTH, HG_VAL_WIDTH, N_ATT_GROUPS * 3 * ATT_WIDTH, D_MODEL, D_MODEL)
IN_COLS = sum(IN_SIZES)
EPS = 1e-6
NEG_INF = -1e30

kernel_name = "hybrid_hgrn2_dilated_alibi_macaron"


def _rmsnorm(x, gain):
    xf = x.astype(jnp.float32)
    xf = xf * lax.rsqrt(jnp.mean(xf * xf, axis=-1, keepdims=True) + EPS)
    return xf.astype(x.dtype) * gain


def _swiglu(x, w_gate_up, w_down):
    a, b = jnp.split(x @ w_gate_up, 2, axis=-1)
    return (jax.nn.silu(a) * b) @ w_down


def _hgrn2(q, f_pre, i, og, lower_bound, out_gain):
    B, S, _ = q.shape
    dt = q.dtype
    H, K, V, C = HG_HEADS, HG_KEY_DIM, HG_VAL_DIM, HG_CHUNK
    lb = lower_bound.reshape(H, K)
    f = lb + (1.0 - lb) * jax.nn.sigmoid(f_pre.astype(jnp.float32).reshape(B, S, H, K))
    log_f = jnp.log(f)
    k = 1.0 - f
    qf = q.astype(jnp.float32).reshape(B, S, H, K)
    v = i.astype(jnp.float32).reshape(B, S, H, V)
    Sp = -(-S // C) * C
    pad = Sp - S

    def to_chunks(a):
        a = jnp.pad(a, ((0, 0), (0, pad), (0, 0), (0, 0)))
        return a.reshape(B, Sp // C, C, H, a.shape[-1]).transpose(1, 0, 3, 2, 4)

    qc, kc, vc, gc = to_chunks(qf), to_chunks(k), to_chunks(v), to_chunks(log_f)
    Gc = jnp.cumsum(gc, axis=3)
    causal = jnp.tril(jnp.ones((C, C), dtype=bool))[:, :, None]

    def step(state, inp):
        q_, k_, v_, G = inp
        diff = G[:, :, :, None, :] - G[:, :, None, :, :]
        decay = jnp.where(causal, jnp.exp(jnp.where(causal, diff, 0.0)), 0.0)
        scores = jnp.einsum('bhtk,bhsk,bhtsk->bhts', q_, k_, decay)
        o = (jnp.einsum('bhts,bhsv->bhtv', scores, v_)
             + jnp.einsum('bhtk,bhkv->bhtv', q_ * jnp.exp(G), state))
        G_last = G[:, :, -1:, :]
        new_state = (jnp.exp(G_last[:, :, 0, :])[..., None] * state
                     + jnp.einsum('bhsk,bhsv->bhkv', k_ * jnp.exp(G_last - G), v_))
        return new_state, o

    state0 = jnp.zeros((B, H, K, V), jnp.float32)
    _, oc = lax.scan(step, state0, (qc, kc, vc, Gc))
    o = oc.transpose(1, 0, 3, 2, 4).reshape(B, Sp, H, V)[:, :S]
    o = o * lax.rsqrt(jnp.mean(o * o, axis=-1, keepdims=True) + EPS)
    o = o.reshape(B, S, H * V).astype(dt) * out_gain
    return o * jax.nn.silu(og)


def _dilated_group(q, k, v, window, dilation, slopes):
    B, S, H, E = q.shape
    BLK = ATT_BLOCK
    span = window // dilation
    unit = dilation * BLK
    Sp = -(-S // unit) * unit
    L = Sp // dilation
    nb = L // BLK

    def to_blocks(a):
        a = jnp.pad(a, ((0, 0), (0, Sp - S), (0, 0), (0, 0)))
        return a.reshape(B, nb, BLK, dilation, H, E)

    def band(a):
        prev = jnp.pad(a, ((0, 0), (1, 0), (0, 0), (0, 0), (0, 0), (0, 0)))[:, :-1]
        return jnp.concatenate([prev, a], axis=2)

    qb = to_blocks(q)
    kband, vband = band(to_blocks(k)), band(to_blocks(v))
    scores = jnp.einsum('bnqrhe,bnkrhe->bnrhqk', qb, kband).astype(jnp.float32) * (E ** -0.5)
    qi = jnp.arange(BLK)[:, None]
    kj = jnp.arange(2 * BLK)[None, :]
    delta = qi + BLK - kj
    blk_idx = jnp.arange(nb)[:, None, None]
    valid = (delta >= 0) & (delta <= span) & ((blk_idx > 0) | (kj >= BLK))
    bias = -slopes[:, None, None] * (dilation * delta).astype(jnp.float32)
    scores = scores + bias[None, None, None]
    scores = jnp.where(valid[None, :, None, None], scores, NEG_INF)
    lse = jax.nn.logsumexp(scores, axis=-1)
    p = jnp.exp(scores - lse[..., None]).astype(v.dtype)
    o = jnp.einsum('bnrhqk,bnkrhe->bnqrhe', p, vband).reshape(B, Sp, H, E)[:, :S]
    lse = lse.transpose(0, 1, 4, 2, 3).reshape(B, Sp, H)[:, :S]
    return o, lse


def _mixer(u, w_in, lower_bound, hg_out_norm, w_branch_hg, w_branch_att, w_out):
    B, S, _ = u.shape
    dt = u.dtype
    points = [int(p) for p in np.cumsum(IN_SIZES)[:-1]]
    hg_q, hg_f, hg_i, hg_og, att, gate_hg, gate_att = jnp.split(u @ w_in, points, axis=-1)
    y_hg = _hgrn2(hg_q, hg_f, hg_i, hg_og, lower_bound, hg_out_norm)
    n_heads = N_ATT_GROUPS * ATT_HEADS
    slopes = jnp.exp2(-ALIBI_MAX * jnp.arange(1, n_heads + 1, dtype=jnp.float32) / n_heads)
    qkv = att.reshape(B, S, N_ATT_GROUPS, 3, ATT_HEADS, ATT_HEAD_DIM)
    outs, lses = [], []
    for g, (window, dilation) in enumerate(ATT_GROUPS):
        o, lse = _dilated_group(qkv[:, :, g, 0], qkv[:, :, g, 1], qkv[:, :, g, 2], window, dilation,
                                slopes[g * ATT_HEADS:(g + 1) * ATT_HEADS])
        outs.append(o)
        lses.append(lse)
    wts = jax.nn.softmax(jnp.stack(lses), axis=0).astype(dt)
    y_att = jnp.einsum('gbshe,gbsh->bshe', jnp.stack(outs), wts).reshape(B, S, ATT_WIDTH)
    merged = (jax.nn.sigmoid(gate_hg) * (y_hg @ w_branch_hg)
              + jax.nn.sigmoid(gate_att) * (y_att @ w_branch_att))
    return merged @ w_out


def _fwd_setup_inputs(seed: int = 0) -> dict:
    key = jax.random.key(seed)
    ks = jax.random.split(key, 16)

    def nrm(k, shape, scale):
        return jax.random.normal(k, shape, jnp.float32) * scale

    return {
        "x": nrm(ks[0], (BATCH, SEQ, D_MODEL), 1.0),
        "ffn1_norm": 1.0 + nrm(ks[1], (DEPTH, D_MODEL), 0.02),
        "ffn1_w_gate_up": nrm(ks[2], (DEPTH, D_MODEL, 2 * D_FF), D_MODEL ** -0.5),
        "ffn1_w_down": nrm(ks[3], (DEPTH, D_FF, D_MODEL), D_FF ** -0.5),
        "mix_norm": 1.0 + nrm(ks[4], (DEPTH, D_MODEL), 0.02),
        "w_in": nrm(ks[5], (DEPTH, D_MODEL, IN_COLS), D_MODEL ** -0.5),
        "hg_lower_bounds": nrm(ks[6], (DEPTH + 1, HG_KEY_WIDTH), 0.1),
        "hg_out_norm": 1.0 + nrm(ks[7], (DEPTH, HG_VAL_WIDTH), 0.02),
        "w_branch_hg": nrm(ks[8], (DEPTH, HG_VAL_WIDTH, D_MODEL), HG_VAL_WIDTH ** -0.5),
        "w_branch_att": nrm(ks[9], (DEPTH, ATT_WIDTH, D_MODEL), ATT_WIDTH ** -0.5),
        "w_out": nrm(ks[10], (DEPTH, D_MODEL, D_MODEL), D_MODEL ** -0.5),
        "ffn2_norm": 1.0 + nrm(ks[11], (DEPTH, D_MODEL), 0.02),
        "ffn2_w_gate_up": nrm(ks[12], (DEPTH, D_MODEL, 2 * D_FF), D_MODEL ** -0.5),
        "ffn2_w_down": nrm(ks[13], (DEPTH, D_FF, D_MODEL), D_FF ** -0.5),
        "final_norm": 1.0 + nrm(ks[14], (D_MODEL,), 0.02),
    }


def _fwd_reference(x, ffn1_norm, ffn1_w_gate_up, ffn1_w_down, mix_norm, w_in, hg_lower_bounds, hg_out_norm,
              w_branch_hg, w_branch_att, w_out, ffn2_norm, ffn2_w_gate_up, ffn2_w_down, final_norm):
    lower_bounds = jnp.cumsum(jax.nn.softmax(hg_lower_bounds.astype(jnp.float32), axis=0), axis=0)
    h = x
    for l in range(DEPTH):
        h = h + 0.5 * _swiglu(_rmsnorm(h, ffn1_norm[l]), ffn1_w_gate_up[l], ffn1_w_down[l])
        h = h + _mixer(_rmsnorm(h, mix_norm[l]), w_in[l], lower_bounds[l], hg_out_norm[l],
                       w_branch_hg[l], w_branch_att[l], w_out[l])
        h = h + 0.5 * _swiglu(_rmsnorm(h, ffn2_norm[l]), ffn2_w_gate_up[l], ffn2_w_down[l])
    return _rmsnorm(h, final_norm)


import jax as _jax
import jax.numpy as _jnp

TWIN_FORMAT = 'train_step'
FWD_PARAMS = ['x', 'ffn1_norm', 'ffn1_w_gate_up', 'ffn1_w_down', 'mix_norm', 'w_in', 'hg_lower_bounds', 'hg_out_norm', 'w_branch_hg', 'w_branch_att', 'w_out', 'ffn2_norm', 'ffn2_w_gate_up', 'ffn2_w_down', 'final_norm']
TWIN_WEIGHTS = ['ffn1_norm', 'ffn1_w_gate_up', 'ffn1_w_down', 'mix_norm', 'w_in', 'hg_lower_bounds', 'hg_out_norm', 'w_branch_hg', 'w_branch_att', 'w_out', 'ffn2_norm', 'ffn2_w_gate_up', 'ffn2_w_down', 'final_norm']
TWIN_DIFF_INPUT = 'x'
TWIN_INPUTS = ['x', 'ffn1_norm', 'ffn1_w_gate_up', 'ffn1_w_down', 'mix_norm', 'w_in', 'hg_lower_bounds', 'hg_out_norm', 'w_branch_hg', 'w_branch_att', 'w_out', 'ffn2_norm', 'ffn2_w_gate_up', 'ffn2_w_down', 'final_norm', 'loss_target', 'm_ffn1_norm', 'm_ffn1_w_gate_up', 'm_ffn1_w_down', 'm_mix_norm', 'm_w_in', 'm_hg_lower_bounds', 'm_hg_out_norm', 'm_w_branch_hg', 'm_w_branch_att', 'm_w_out', 'm_ffn2_norm', 'm_ffn2_w_gate_up', 'm_ffn2_w_down', 'm_final_norm', 'v_ffn1_norm', 'v_ffn1_w_gate_up', 'v_ffn1_w_down', 'v_mix_norm', 'v_w_in', 'v_hg_lower_bounds', 'v_hg_out_norm', 'v_w_branch_hg', 'v_w_branch_att', 'v_w_out', 'v_ffn2_norm', 'v_ffn2_w_gate_up', 'v_ffn2_w_down', 'v_final_norm']
TWIN_OUTPUTS = ['loss', 'grad_x', 'grad_ffn1_norm', 'grad_ffn1_w_gate_up', 'grad_ffn1_w_down', 'grad_mix_norm', 'grad_w_in', 'grad_hg_lower_bounds', 'grad_hg_out_norm', 'grad_w_branch_hg', 'grad_w_branch_att', 'grad_w_out', 'grad_ffn2_norm', 'grad_ffn2_w_gate_up', 'grad_ffn2_w_down', 'grad_final_norm', 'delta_ffn1_norm', 'delta_ffn1_w_gate_up', 'delta_ffn1_w_down', 'delta_mix_norm', 'delta_w_in', 'delta_hg_lower_bounds', 'delta_hg_out_norm', 'delta_w_branch_hg', 'delta_w_branch_att', 'delta_w_out', 'delta_ffn2_norm', 'delta_ffn2_w_gate_up', 'delta_ffn2_w_down', 'delta_final_norm', 'new_m_ffn1_norm', 'new_m_ffn1_w_gate_up', 'new_m_ffn1_w_down', 'new_m_mix_norm', 'new_m_w_in', 'new_m_hg_lower_bounds', 'new_m_hg_out_norm', 'new_m_w_branch_hg', 'new_m_w_branch_att', 'new_m_w_out', 'new_m_ffn2_norm', 'new_m_ffn2_w_gate_up', 'new_m_ffn2_w_down', 'new_m_final_norm', 'new_v_ffn1_norm', 'new_v_ffn1_w_gate_up', 'new_v_ffn1_w_down', 'new_v_mix_norm', 'new_v_w_in', 'new_v_hg_lower_bounds', 'new_v_hg_out_norm', 'new_v_w_branch_hg', 'new_v_w_branch_att', 'new_v_w_out', 'new_v_ffn2_norm', 'new_v_ffn2_w_gate_up', 'new_v_ffn2_w_down', 'new_v_final_norm']
TWIN_LEAF_KINDS = {'loss': 'loss', 'grad_x': 'grad_x', 'grad_ffn1_norm': 'grad_w', 'grad_ffn1_w_gate_up': 'grad_w', 'grad_ffn1_w_down': 'grad_w', 'grad_mix_norm': 'grad_w', 'grad_w_in': 'grad_w', 'grad_hg_lower_bounds': 'grad_w', 'grad_hg_out_norm': 'grad_w', 'grad_w_branch_hg': 'grad_w', 'grad_w_branch_att': 'grad_w', 'grad_w_out': 'grad_w', 'grad_ffn2_norm': 'grad_w', 'grad_ffn2_w_gate_up': 'grad_w', 'grad_ffn2_w_down': 'grad_w', 'grad_final_norm': 'grad_w', 'delta_ffn1_norm': 'delta_w', 'delta_ffn1_w_gate_up': 'delta_w', 'delta_ffn1_w_down': 'delta_w', 'delta_mix_norm': 'delta_w', 'delta_w_in': 'delta_w', 'delta_hg_lower_bounds': 'delta_w', 'delta_hg_out_norm': 'delta_w', 'delta_w_branch_hg': 'delta_w', 'delta_w_branch_att': 'delta_w', 'delta_w_out': 'delta_w', 'delta_ffn2_norm': 'delta_w', 'delta_ffn2_w_gate_up': 'delta_w', 'delta_ffn2_w_down': 'delta_w', 'delta_final_norm': 'delta_w', 'new_m_ffn1_norm': 'new_m', 'new_m_ffn1_w_gate_up': 'new_m', 'new_m_ffn1_w_down': 'new_m', 'new_m_mix_norm': 'new_m', 'new_m_w_in': 'new_m', 'new_m_hg_lower_bounds': 'new_m', 'new_m_hg_out_norm': 'new_m', 'new_m_w_branch_hg': 'new_m', 'new_m_w_branch_att': 'new_m', 'new_m_w_out': 'new_m', 'new_m_ffn2_norm': 'new_m', 'new_m_ffn2_w_gate_up': 'new_m', 'new_m_ffn2_w_down': 'new_m', 'new_m_final_norm': 'new_m', 'new_v_ffn1_norm': 'new_v', 'new_v_ffn1_w_gate_up': 'new_v', 'new_v_ffn1_w_down': 'new_v', 'new_v_mix_norm': 'new_v', 'new_v_w_in': 'new_v', 'new_v_hg_lower_bounds': 'new_v', 'new_v_hg_out_norm': 'new_v', 'new_v_w_branch_hg': 'new_v', 'new_v_w_branch_att': 'new_v', 'new_v_w_out': 'new_v', 'new_v_ffn2_norm': 'new_v', 'new_v_ffn2_w_gate_up': 'new_v', 'new_v_ffn2_w_down': 'new_v', 'new_v_final_norm': 'new_v'}


def _forward(args):
    return _fwd_reference(*[args[k] for k in FWD_PARAMS])


def _output_shape():
    out = _jax.eval_shape(lambda: _forward(_fwd_setup_inputs(0)))
    return out.shape, out.dtype

N_MICROBATCH = 1
ADAM_LR = 0.001
ADAM_B1 = 0.9
ADAM_B2 = 0.999
ADAM_EPS = 1e-08
ADAM_WD = 0.01
ADAM_STEP = 10
PER_EXAMPLE_BATCH_AXIS = {'x': 0, 'loss_target': 0}
SHARED_INPUTS = []
_WEIGHT_DTYPES = {'ffn1_norm': _jnp.float32, 'ffn1_w_gate_up': _jnp.float32, 'ffn1_w_down': _jnp.float32, 'mix_norm': _jnp.float32, 'w_in': _jnp.float32, 'hg_lower_bounds': _jnp.float32, 'hg_out_norm': _jnp.float32, 'w_branch_hg': _jnp.float32, 'w_branch_att': _jnp.float32, 'w_out': _jnp.float32, 'ffn2_norm': _jnp.float32, 'ffn2_w_gate_up': _jnp.float32, 'ffn2_w_down': _jnp.float32, 'final_norm': _jnp.float32}
MOMENT_SCALE = {'ffn1_norm': 6.373863e-02, 'ffn1_w_gate_up': 2.662192e-02, 'ffn1_w_down': 4.343800e-02, 'mix_norm': 8.872311e-02, 'w_in': 3.095431e-02, 'hg_lower_bounds': 3.715380e-02, 'hg_out_norm': 5.541462e-02, 'w_branch_hg': 3.800416e-02, 'w_branch_att': 2.152880e-02, 'w_out': 4.330337e-02, 'ffn2_norm': 4.691811e-02, 'ffn2_w_gate_up': 2.008451e-02, 'ffn2_w_down': 3.274936e-02, 'final_norm': 1.599200e+01}


def _to_microbatches(a, axis):
    t = _jnp.moveaxis(a, axis, 0)
    t = t.reshape((N_MICROBATCH, t.shape[0] // N_MICROBATCH) + t.shape[1:])
    return _jnp.moveaxis(t, 1, axis + 1)


def setup_inputs(seed: int = 0) -> dict:
    inp = _fwd_setup_inputs(seed)
    key = _jax.random.fold_in(_jax.random.key(seed), 7919)
    shape, _ = _output_shape()
    out = dict(inp)
    out["loss_target"] = _jax.random.normal(_jax.random.fold_in(key, 0), shape, _jnp.float32)
    for i, name in enumerate(TWIN_WEIGHTS):
        w = inp[name].astype(_jnp.float32)
        if MOMENT_SCALE is None:
            s = _jnp.sqrt(_jnp.mean(_jnp.square(w)) + 1e-30)
        else:
            s = MOMENT_SCALE[name]
        km, kv = _jax.random.split(_jax.random.fold_in(key, i + 1))
        out[name] = w
        out["m_" + name] = s * _jax.random.normal(km, w.shape, _jnp.float32)
        out["v_" + name] = (s * s) * _jax.random.uniform(kv, w.shape, _jnp.float32, 0.5, 1.5)
    if N_MICROBATCH > 1:
        for name, axis in PER_EXAMPLE_BATCH_AXIS.items():
            out[name] = _to_microbatches(out[name], axis)
    return {'x': out['x'], 'ffn1_norm': out['ffn1_norm'], 'ffn1_w_gate_up': out['ffn1_w_gate_up'], 'ffn1_w_down': out['ffn1_w_down'], 'mix_norm': out['mix_norm'], 'w_in': out['w_in'], 'hg_lower_bounds': out['hg_lower_bounds'], 'hg_out_norm': out['hg_out_norm'], 'w_branch_hg': out['w_branch_hg'], 'w_branch_att': out['w_branch_att'], 'w_out': out['w_out'], 'ffn2_norm': out['ffn2_norm'], 'ffn2_w_gate_up': out['ffn2_w_gate_up'], 'ffn2_w_down': out['ffn2_w_down'], 'final_norm': out['final_norm'], 'loss_target': out['loss_target'], 'm_ffn1_norm': out['m_ffn1_norm'], 'm_ffn1_w_gate_up': out['m_ffn1_w_gate_up'], 'm_ffn1_w_down': out['m_ffn1_w_down'], 'm_mix_norm': out['m_mix_norm'], 'm_w_in': out['m_w_in'], 'm_hg_lower_bounds': out['m_hg_lower_bounds'], 'm_hg_out_norm': out['m_hg_out_norm'], 'm_w_branch_hg': out['m_w_branch_hg'], 'm_w_branch_att': out['m_w_branch_att'], 'm_w_out': out['m_w_out'], 'm_ffn2_norm': out['m_ffn2_norm'], 'm_ffn2_w_gate_up': out['m_ffn2_w_gate_up'], 'm_ffn2_w_down': out['m_ffn2_w_down'], 'm_final_norm': out['m_final_norm'], 'v_ffn1_norm': out['v_ffn1_norm'], 'v_ffn1_w_gate_up': out['v_ffn1_w_gate_up'], 'v_ffn1_w_down': out['v_ffn1_w_down'], 'v_mix_norm': out['v_mix_norm'], 'v_w_in': out['v_w_in'], 'v_hg_lower_bounds': out['v_hg_lower_bounds'], 'v_hg_out_norm': out['v_hg_out_norm'], 'v_w_branch_hg': out['v_w_branch_hg'], 'v_w_branch_att': out['v_w_branch_att'], 'v_w_out': out['v_w_out'], 'v_ffn2_norm': out['v_ffn2_norm'], 'v_ffn2_w_gate_up': out['v_ffn2_w_gate_up'], 'v_ffn2_w_down': out['v_ffn2_w_down'], 'v_final_norm': out['v_final_norm']}


def _loss(weights, diff, rest, loss_target):
    with _jax.named_scope("forward"):
        args = {**rest, TWIN_DIFF_INPUT: diff, **{k: w.astype(_WEIGHT_DTYPES[k]) for k, w in weights.items()}}
        y = _forward(args)
    with _jax.named_scope("loss_head"):
        err = _jnp.square(y.astype(_jnp.float32) - loss_target)
        return 0.5 * _jnp.sum(_jnp.mean(err, axis=-1)) if err.ndim else 0.5 * err


def _adamw(w, g, m, v):
    m = ADAM_B1 * m + (1.0 - ADAM_B1) * g
    v = ADAM_B2 * v + (1.0 - ADAM_B2) * _jnp.square(g)
    m_hat = m / (1.0 - ADAM_B1 ** ADAM_STEP)
    v_hat = v / (1.0 - ADAM_B2 ** ADAM_STEP)
    delta = -ADAM_LR * (m_hat / (_jnp.sqrt(v_hat) + ADAM_EPS) + ADAM_WD * w)
    return delta, m, v


def reference(x, ffn1_norm, ffn1_w_gate_up, ffn1_w_down, mix_norm, w_in, hg_lower_bounds, hg_out_norm, w_branch_hg, w_branch_att, w_out, ffn2_norm, ffn2_w_gate_up, ffn2_w_down, final_norm, loss_target, m_ffn1_norm, m_ffn1_w_gate_up, m_ffn1_w_down, m_mix_norm, m_w_in, m_hg_lower_bounds, m_hg_out_norm, m_w_branch_hg, m_w_branch_att, m_w_out, m_ffn2_norm, m_ffn2_w_gate_up, m_ffn2_w_down, m_final_norm, v_ffn1_norm, v_ffn1_w_gate_up, v_ffn1_w_down, v_mix_norm, v_w_in, v_hg_lower_bounds, v_hg_out_norm, v_w_branch_hg, v_w_branch_att, v_w_out, v_ffn2_norm, v_ffn2_w_gate_up, v_ffn2_w_down, v_final_norm):
    given = dict(x=x, ffn1_norm=ffn1_norm, ffn1_w_gate_up=ffn1_w_gate_up, ffn1_w_down=ffn1_w_down, mix_norm=mix_norm, w_in=w_in, hg_lower_bounds=hg_lower_bounds, hg_out_norm=hg_out_norm, w_branch_hg=w_branch_hg, w_branch_att=w_branch_att, w_out=w_out, ffn2_norm=ffn2_norm, ffn2_w_gate_up=ffn2_w_gate_up, ffn2_w_down=ffn2_w_down, final_norm=final_norm, loss_target=loss_target, m_ffn1_norm=m_ffn1_norm, m_ffn1_w_gate_up=m_ffn1_w_gate_up, m_ffn1_w_down=m_ffn1_w_down, m_mix_norm=m_mix_norm, m_w_in=m_w_in, m_hg_lower_bounds=m_hg_lower_bounds, m_hg_out_norm=m_hg_out_norm, m_w_branch_hg=m_w_branch_hg, m_w_branch_att=m_w_branch_att, m_w_out=m_w_out, m_ffn2_norm=m_ffn2_norm, m_ffn2_w_gate_up=m_ffn2_w_gate_up, m_ffn2_w_down=m_ffn2_w_down, m_final_norm=m_final_norm, v_ffn1_norm=v_ffn1_norm, v_ffn1_w_gate_up=v_ffn1_w_gate_up, v_ffn1_w_down=v_ffn1_w_down, v_mix_norm=v_mix_norm, v_w_in=v_w_in, v_hg_lower_bounds=v_hg_lower_bounds, v_hg_out_norm=v_hg_out_norm, v_w_branch_hg=v_w_branch_hg, v_w_branch_att=v_w_branch_att, v_w_out=v_w_out, v_ffn2_norm=v_ffn2_norm, v_ffn2_w_gate_up=v_ffn2_w_gate_up, v_ffn2_w_down=v_ffn2_w_down, v_final_norm=v_final_norm)
    weights = {n: given[n] for n in TWIN_WEIGHTS}
    shared = {n: given[n] for n in SHARED_INPUTS}
    per_example = {n: given[n] for n in ['x']}
    grad_fn = _jax.value_and_grad(_loss, argnums=(0, 1))

    def one_microbatch(ex, loss_target):
        ex = dict(ex)
        diff = ex.pop(TWIN_DIFF_INPUT)
        return grad_fn(weights, diff, {**shared, **ex}, loss_target)

    if N_MICROBATCH == 1:
        loss, (grad_w, grad_x) = one_microbatch(per_example, given["loss_target"])
    else:
        def body(carry, xs):
            loss_sum, grad_sum = carry
            l_k, (gw_k, gx_k) = one_microbatch(xs[0], xs[1])
            with _jax.named_scope("update"):
                return (loss_sum + l_k, _jax.tree.map(_jnp.add, grad_sum, gw_k)), gx_k

        init = (_jnp.zeros((), _jnp.float32), _jax.tree.map(_jnp.zeros_like, weights))
        (loss, grad_w), grad_x = _jax.lax.scan(body, init, (per_example, given["loss_target"]))
    with _jax.named_scope("update"):
        delta_w, new_m, new_v = {}, {}, {}
        for n in TWIN_WEIGHTS:
            delta_w[n], new_m[n], new_v[n] = _adamw(weights[n], grad_w[n], given["m_" + n], given["v_" + n])
    return (loss, grad_x, *[grad_w[n] for n in TWIN_WEIGHTS], *[delta_w[n] for n in TWIN_WEIGHTS],
            *[new_m[n] for n in TWIN_WEIGHTS], *[new_v[n] for n in TWIN_WEIGHTS])
```

```python
import numpy as np
import jax
import jax.numpy as jnp
from jax import lax
from jax.experimental import pallas as pl
from jax.experimental.pallas import tpu as pltpu

SEQ = 2048
D_MODEL = 1024
D_FF = 2816
HG_HEADS = 4
HG_DIM = 128
HG_WIDTH = 512
HG_CHUNK = 64
ATT_GROUPS = ((128, 1), (512, 4), (2048, 16))
ATT_HEADS = 8
ATT_WIDTH = 512
ATT_BLOCK = 128
ALIBI_MAX = 8.0
IN_COLS = 8704
EPS = 1e-6
NEG_INF = -1e30
ADAM_LR = 0.001
ADAM_B1 = 0.9
ADAM_B2 = 0.999
ADAM_EPS = 1e-08
ADAM_WD = 0.01
ADAM_STEP = 10

N_CHIPS = 4
MXU_DTYPE = jnp.bfloat16
HG_DOT_DTYPE = jnp.float32
WEIGHT_COMM_DTYPE = jnp.bfloat16
GRAD_COMM_DTYPE = jnp.float32
MESH = pl.DeviceIdType.MESH
F32 = jnp.float32
HIGHEST = lax.Precision.HIGHEST


def _pick(n, cands):
    for c in cands:
        if n % c == 0:
            return c
    return n


def _sigmoid(x):
    return 1.0 / (1.0 + jnp.exp(-x))


def _dot(a, b, ta=False, tb=False):
    dn = (((0 if ta else 1,), (1 if tb else 0,)), ((), ()))
    return lax.dot_general(a.astype(MXU_DTYPE), b.astype(MXU_DTYPE), dn, preferred_element_type=F32)


def _dot_f32(a, b):
    return jnp.dot(a, b, precision=HIGHEST, preferred_element_type=F32)


def _hdot(a, b, ta=False, tb=False):
    if HG_DOT_DTYPE == F32:
        dn = (((0 if ta else 1,), (1 if tb else 0,)), ((), ()))
        return lax.dot_general(a, b, dn, precision=HIGHEST, preferred_element_type=F32)
    return _dot(a, b, ta, tb)


MATMUL_VMEM_BYTES = 40 * 1024 * 1024


def matmul(a, b, *, ta=False, tb=False, out_dtype=F32, res=None, scale=1.0, name):
    if ta:
        K, M = a.shape
    else:
        M, K = a.shape
    if tb:
        N, K2 = b.shape
    else:
        K2, N = b.shape
    assert K == K2
    tm = _pick(M, (1024, 512, 256, 128))
    tn = _pick(N, (512, 256, 128))
    tk = _pick(K, (512, 256, 128))
    nk = K // tk

    def body(*refs):
        if res is None:
            a_ref, b_ref, o_ref, acc = refs
        else:
            a_ref, b_ref, r_ref, o_ref, acc = refs
        k = pl.program_id(2)

        @pl.when(k == 0)
        def _():
            acc[...] = jnp.zeros_like(acc)

        acc[...] += _dot(a_ref[...], b_ref[...], ta, tb)

        @pl.when(k == nk - 1)
        def _():
            r = acc[...]
            if scale != 1.0:
                r = r * scale
            if res is not None:
                r = r_ref[...] + r
            o_ref[...] = r.astype(out_dtype)

    a_spec = pl.BlockSpec((tk, tm), lambda i, j, k: (k, i)) if ta else pl.BlockSpec((tm, tk), lambda i, j, k: (i, k))
    b_spec = pl.BlockSpec((tn, tk), lambda i, j, k: (j, k)) if tb else pl.BlockSpec((tk, tn), lambda i, j, k: (k, j))
    in_specs = [a_spec, b_spec]
    args = [a, b]
    if res is not None:
        in_specs.append(pl.BlockSpec((tm, tn), lambda i, j, k: (i, j)))
        args.append(res)
    return pl.pallas_call(
        body, name=name, grid=(M // tm, N // tn, nk), in_specs=in_specs,
        out_specs=pl.BlockSpec((tm, tn), lambda i, j, k: (i, j)),
        out_shape=jax.ShapeDtypeStruct((M, N), out_dtype),
        scratch_shapes=[pltpu.VMEM((tm, tn), F32)],
        compiler_params=pltpu.CompilerParams(dimension_semantics=("parallel", "parallel", "arbitrary"),
                                             vmem_limit_bytes=MATMUL_VMEM_BYTES),
    )(*args)


ROW_TILE = 256


def rmsnorm_fwd(x, g, name):
    def body(x_ref, g_ref, n_ref):
        xv = x_ref[...]
        r = lax.rsqrt(jnp.mean(xv * xv, axis=-1, keepdims=True) + EPS)
        n_ref[...] = ((xv * r) * g_ref[...]).astype(n_ref.dtype)

    return pl.pallas_call(
        body, name=name, grid=(SEQ // ROW_TILE,),
        in_specs=[pl.BlockSpec((ROW_TILE, D_MODEL), lambda i: (i, 0)), pl.BlockSpec((1, D_MODEL), lambda i: (0, 0))],
        out_specs=pl.BlockSpec((ROW_TILE, D_MODEL), lambda i: (i, 0)),
        out_shape=jax.ShapeDtypeStruct((SEQ, D_MODEL), MXU_DTYPE),
    )(x, g)


def rmsnorm_bwd(x, g, dn, dres, name):
    def body(x_ref, g_ref, dn_ref, dr_ref, dx_ref, dg_ref):
        xv = x_ref[...]
        r = lax.rsqrt(jnp.mean(xv * xv, axis=-1, keepdims=True) + EPS)
        xh = xv * r
        dnv = dn_ref[...]

        @pl.when(pl.program_id(0) == 0)
        def _():
            dg_ref[...] = jnp.zeros_like(dg_ref)

        dg_ref[...] += jnp.sum(dnv * xh, axis=0, keepdims=True)
        dxh = dnv * g_ref[...]
        dx_ref[...] = dr_ref[...] + r * (dxh - xh * jnp.mean(dxh * xh, axis=-1, keepdims=True))

    row = pl.BlockSpec((ROW_TILE, D_MODEL), lambda i: (i, 0))
    vec = pl.BlockSpec((1, D_MODEL), lambda i: (0, 0))
    return pl.pallas_call(
        body, name=name, grid=(SEQ // ROW_TILE,), in_specs=[row, vec, row, row], out_specs=[row, vec],
        out_shape=[jax.ShapeDtypeStruct((SEQ, D_MODEL), F32), jax.ShapeDtypeStruct((1, D_MODEL), F32)],
        compiler_params=pltpu.CompilerParams(dimension_semantics=("arbitrary",)),
    )(x, g, dn, dres)


def final_norm_loss(h, g, target, name):
    def body(h_ref, g_ref, t_ref, dh_ref, dg_ref, loss_ref):
        xv = h_ref[...]
        r = lax.rsqrt(jnp.mean(xv * xv, axis=-1, keepdims=True) + EPS)
        xh = xv * r
        gv = g_ref[...]
        e = xh * gv - t_ref[...]

        @pl.when(pl.program_id(0) == 0)
        def _():
            dg_ref[...] = jnp.zeros_like(dg_ref)
            loss_ref[...] = jnp.zeros_like(loss_ref)

        part = 0.5 * jnp.sum(jnp.sum(e * e, axis=-1, keepdims=True) * (1.0 / D_MODEL), axis=0, keepdims=True)
        loss_ref[...] += jnp.broadcast_to(part, loss_ref.shape)
        dout = e * (1.0 / D_MODEL)
        dg_ref[...] += jnp.sum(dout * xh, axis=0, keepdims=True)
        dxh = dout * gv
        dh_ref[...] = r * (dxh - xh * jnp.mean(dxh * xh, axis=-1, keepdims=True))

    row = pl.BlockSpec((ROW_TILE, D_MODEL), lambda i: (i, 0))
    vec = pl.BlockSpec((1, D_MODEL), lambda i: (0, 0))
    return pl.pallas_call(
        body, name=name, grid=(SEQ // ROW_TILE,), in_specs=[row, vec, row],
        out_specs=[row, vec, pl.BlockSpec((8, 128), lambda i: (0, 0))],
        out_shape=[jax.ShapeDtypeStruct((SEQ, D_MODEL), F32), jax.ShapeDtypeStruct((1, D_MODEL), F32),
                   jax.ShapeDtypeStruct((8, 128), F32)],
        compiler_params=pltpu.CompilerParams(dimension_semantics=("arbitrary",)),
    )(h, g, target)


FF_TILE = D_FF // 2


def swiglu_fwd(gu, name):
    def body(a_ref, b_ref, s_ref):
        a = a_ref[...]
        s_ref[...] = (a * _sigmoid(a) * b_ref[...]).astype(s_ref.dtype)

    return pl.pallas_call(
        body, name=name, grid=(SEQ // ROW_TILE, 2),
        in_specs=[pl.BlockSpec((ROW_TILE, FF_TILE), lambda i, j: (i, j)),
                  pl.BlockSpec((ROW_TILE, FF_TILE), lambda i, j: (i, j + 2))],
        out_specs=pl.BlockSpec((ROW_TILE, FF_TILE), lambda i, j: (i, j)),
        out_shape=jax.ShapeDtypeStruct((SEQ, D_FF), MXU_DTYPE),
    )(gu, gu)


def swiglu_bwd(gu, ds, name):
    def body(a_ref, b_ref, ds_ref, o_ref):
        a = a_ref[...]
        sg = _sigmoid(a)
        dsv = ds_ref[...]

        @pl.when(pl.program_id(1) < 2)
        def _():
            o_ref[...] = (dsv * b_ref[...] * (sg * (1.0 + a * (1.0 - sg)))).astype(o_ref.dtype)

        @pl.when(pl.program_id(1) >= 2)
        def _():
            o_ref[...] = (dsv * a * sg).astype(o_ref.dtype)

    return pl.pallas_call(
        body, name=name, grid=(SEQ // ROW_TILE, 4),
        in_specs=[pl.BlockSpec((ROW_TILE, FF_TILE), lambda i, j: (i, j % 2)),
                  pl.BlockSpec((ROW_TILE, FF_TILE), lambda i, j: (i, j % 2 + 2)),
                  pl.BlockSpec((ROW_TILE, FF_TILE), lambda i, j: (i, j % 2))],
        out_specs=pl.BlockSpec((ROW_TILE, FF_TILE), lambda i, j: (i, j)),
        out_shape=jax.ShapeDtypeStruct((SEQ, 2 * D_FF), MXU_DTYPE),
    )(gu, gu, ds)


GATE_HG_BLK = 6656 // 512
GATE_ATT_BLK = 7680 // 512


def merge_fwd(z, bh, ba, name):
    def body(gh_ref, ga_ref, bh_ref, ba_ref, o_ref):
        o_ref[...] = (_sigmoid(gh_ref[...]) * bh_ref[...] + _sigmoid(ga_ref[...]) * ba_ref[...]).astype(o_ref.dtype)

    blk = pl.BlockSpec((ROW_TILE, 512), lambda i, j: (i, j))
    return pl.pallas_call(
        body, name=name, grid=(SEQ // ROW_TILE, 2),
        in_specs=[pl.BlockSpec((ROW_TILE, 512), lambda i, j: (i, GATE_HG_BLK + j)),
                  pl.BlockSpec((ROW_TILE, 512), lambda i, j: (i, GATE_ATT_BLK + j)), blk, blk],
        out_specs=blk, out_shape=jax.ShapeDtypeStruct((SEQ, D_MODEL), MXU_DTYPE),
    )(z, z, bh, ba)


def merge_bwd(z, bh, ba, dm, name):
    def body(gh_ref, ga_ref, bh_ref, ba_ref, dm_ref, dbh_ref, dba_ref, dgh_ref, dga_ref):
        dmv = dm_ref[...]
        sh = _sigmoid(gh_ref[...])
        sa = _sigmoid(ga_ref[...])
        dbh_ref[...] = (dmv * sh).astype(dbh_ref.dtype)
        dba_ref[...] = (dmv * sa).astype(dba_ref.dtype)
        dgh_ref[...] = (dmv * bh_ref[...] * (sh * (1.0 - sh))).astype(dgh_ref.dtype)
        dga_ref[...] = (dmv * ba_ref[...] * (sa * (1.0 - sa))).astype(dga_ref.dtype)

    blk = pl.BlockSpec((ROW_TILE, 512), lambda i, j: (i, j))
    out = jax.ShapeDtypeStruct((SEQ, D_MODEL), MXU_DTYPE)
    return pl.pallas_call(
        body, name=name, grid=(SEQ // ROW_TILE, 2),
        in_specs=[pl.BlockSpec((ROW_TILE, 512), lambda i, j: (i, GATE_HG_BLK + j)),
                  pl.BlockSpec((ROW_TILE, 512), lambda i, j: (i, GATE_ATT_BLK + j)), blk, blk, blk],
        out_specs=[blk, blk, blk, blk], out_shape=[out, out, out, out],
    )(z, z, bh, ba, dm)


N_CHUNKS = SEQ // HG_CHUNK


def _hgrn_gates(q, fp, lb):
    C = HG_CHUNK
    sg = _sigmoid(fp)
    f = lb + (1.0 - lb) * sg
    lf = jnp.log(f)
    row = lax.broadcasted_iota(jnp.int32, (C, C), 0)
    col = lax.broadcasted_iota(jnp.int32, (C, C), 1)
    causal = row >= col
    G = _dot_f32(causal.astype(F32), lf)
    eG = jnp.exp(G)
    enG = jnp.exp(-G)
    qg = q * eG
    kg = (1.0 - f) * enG
    A = jnp.where(causal, _hdot(qg, kg, tb=True), 0.0)
    egl = jnp.exp(jnp.sum(lf, axis=0, keepdims=True))
    return sg, f, causal, eG, enG, qg, kg, A, egl


def hgrn_fwd(z, lb, gain, name):
    C, K = HG_CHUNK, HG_DIM

    def body(q_ref, f_ref, v_ref, og_ref, p_ref, g_ref, y_ref, o_ref, st_ref, state):
        @pl.when(pl.program_id(1) == 0)
        def _():
            state[...] = jnp.zeros_like(state)

        v = v_ref[...]
        _, _, _, _, _, qg, kg, A, egl = _hgrn_gates(q_ref[...], f_ref[...], p_ref[...])
        st = state[...]
        st_ref[0, 0] = st
        o = _hdot(A, v) + _hdot(qg, st, tb=True)
        state[...] = st * egl + _hdot(v, kg * egl, ta=True)
        o_ref[...] = o
        rs = lax.rsqrt(jnp.mean(o * o, axis=-1, keepdims=True) + EPS)
        og = og_ref[...]
        y_ref[...] = (((o * rs) * g_ref[...]) * (og * _sigmoid(og))).astype(y_ref.dtype)

    def zcol(section):
        return pl.BlockSpec((C, K), lambda h, c: (c, section * HG_HEADS + h))

    vec = pl.BlockSpec((1, K), lambda h, c: (0, h))
    blk = pl.BlockSpec((C, K), lambda h, c: (c, h))
    return pl.pallas_call(
        body, name=name, grid=(HG_HEADS, N_CHUNKS),
        in_specs=[zcol(0), zcol(1), zcol(2), zcol(3), vec, vec],
        out_specs=[blk, blk, pl.BlockSpec((1, 1, K, K), lambda h, c: (h, c, 0, 0))],
        out_shape=[jax.ShapeDtypeStruct((SEQ, HG_WIDTH), MXU_DTYPE), jax.ShapeDtypeStruct((SEQ, HG_WIDTH), F32),
                   jax.ShapeDtypeStruct((HG_HEADS, N_CHUNKS, K, K), F32)],
        scratch_shapes=[pltpu.VMEM((K, K), F32)],
        compiler_params=pltpu.CompilerParams(dimension_semantics=("parallel", "arbitrary")),
    )(z, z, z, z, lb, gain)


def hgrn_bwd(z, lb, gain, o_raw, states, dy, name):
    C, K = HG_CHUNK, HG_DIM

    def body(q_ref, f_ref, v_ref, og_ref, p_ref, g_ref, o_ref, st_ref, dy_ref,
             dq_ref, dfp_ref, dv_ref, dog_ref, dlb_ref, dgain_ref, dstate):
        @pl.when(pl.program_id(1) == 0)
        def _():
            dstate[...] = jnp.zeros_like(dstate)
            dlb_ref[...] = jnp.zeros_like(dlb_ref)
            dgain_ref[...] = jnp.zeros_like(dgain_ref)

        v = v_ref[...]
        lb = p_ref[...]
        sg, f, causal, eG, enG, qg, kg, A, egl = _hgrn_gates(q_ref[...], f_ref[...], lb)
        kd = kg * egl
        st = st_ref[0, 0]
        dst = dstate[...]
        o = o_ref[...]
        og = og_ref[...]
        gain_v = g_ref[...]
        dyv = dy_ref[...]
        rs = lax.rsqrt(jnp.mean(o * o, axis=-1, keepdims=True) + EPS)
        on = o * rs
        sgo = _sigmoid(og)
        silu = og * sgo
        dog_ref[...] = (dyv * (on * gain_v) * (sgo * (1.0 + og * (1.0 - sgo)))).astype(dog_ref.dtype)
        dgain_ref[...] += jnp.sum(dyv * silu * on, axis=0, keepdims=True)
        don = dyv * gain_v * silu
        do = rs * (don - on * jnp.mean(don * on, axis=-1, keepdims=True))
        dA = jnp.where(causal, _hdot(do, v, tb=True), 0.0)
        dv_ref[...] = (_hdot(A, do, ta=True) + _hdot(kd, dst, tb=True)).astype(dv_ref.dtype)
        dqg = _hdot(dA, kg) + _hdot(do, st)
        dkg = _hdot(dA, qg, ta=True)
        dkd = _hdot(v, dst)
        dstate[...] = dst * egl + _hdot(do, qg, ta=True)
        dgl = jnp.sum(st * dst, axis=0, keepdims=True) * egl
        dq_ref[...] = (dqg * eG).astype(dq_ref.dtype)
        dk = dkg * enG + dkd * (enG * egl)
        dG = dqg * qg - dkg * kg - dkd * kd
        extra = jnp.sum(dkd * kd, axis=0, keepdims=True) + dgl
        last = lax.broadcasted_iota(jnp.int32, (C, K), 0) == C - 1
        dG = dG + jnp.where(last, extra, 0.0)
        row = lax.broadcasted_iota(jnp.int32, (C, C), 0)
        col = lax.broadcasted_iota(jnp.int32, (C, C), 1)
        dlf = _dot_f32((col >= row).astype(F32), dG)
        df = dlf / f - dk
        dfp_ref[...] = (df * (1.0 - lb) * (sg * (1.0 - sg))).astype(dfp_ref.dtype)
        dlb_ref[...] += jnp.sum(df * (1.0 - sg), axis=0, keepdims=True)

    def rc(c):
        return N_CHUNKS - 1 - c

    def zcol(section):
        return pl.BlockSpec((C, K), lambda h, c: (rc(c), section * HG_HEADS + h))

    vec = pl.BlockSpec((1, K), lambda h, c: (0, h))
    blk = pl.BlockSpec((C, K), lambda h, c: (rc(c), h))
    out = jax.ShapeDtypeStruct((SEQ, HG_WIDTH), MXU_DTYPE)
    small = jax.ShapeDtypeStruct((1, HG_WIDTH), F32)
    return pl.pallas_call(
        body, name=name, grid=(HG_HEADS, N_CHUNKS),
        in_specs=[zcol(0), zcol(1), zcol(2), zcol(3), vec, vec, blk,
                  pl.BlockSpec((1, 1, K, K), lambda h, c: (h, rc(c), 0, 0)), blk],
        out_specs=[blk, blk, blk, blk, vec, vec],
        out_shape=[out, out, out, out, small, small],
        scratch_shapes=[pltpu.VMEM((K, K), F32)],
        compiler_params=pltpu.CompilerParams(dimension_semantics=("parallel", "arbitrary")),
    )(z, z, z, z, lb, gain, o_raw, states, dy)


N_GROUPS = len(ATT_GROUPS)
N_BLOCKS = SEQ // ATT_BLOCK
HEAD_PAIRS = ATT_WIDTH // 128


def _alibi_coef():
    n = N_GROUPS * ATT_HEADS
    slopes = np.exp2(-ALIBI_MAX * np.arange(1, n + 1, dtype=np.float32) / n).astype(np.float32)
    dil = np.repeat(np.array([d for _, d in ATT_GROUPS], np.float32), ATT_HEADS)
    return jnp.asarray(slopes * dil, F32)


def _blocks_per_seq(g):
    return jnp.where(g == 0, N_BLOCKS // ATT_GROUPS[0][1],
                     jnp.where(g == 1, N_BLOCKS // ATT_GROUPS[1][1], N_BLOCKS // ATT_GROUPS[2][1]))


def _att_masks(off_cur):
    B = ATT_BLOCK
    qi = lax.broadcasted_iota(jnp.int32, (B, B), 0)
    kj = lax.broadcasted_iota(jnp.int32, (B, B), 1)
    return qi, kj, (qi - kj).astype(F32), (qi + B - kj).astype(F32)


def _head_lanes(j):
    lane = lax.broadcasted_iota(jnp.int32, (ATT_BLOCK, 128), 1)
    return (lane >= 64 * j) & (lane < 64 * (j + 1))


def _lane_value(x, sel):
    return jnp.max(jnp.where(sel, x, -3e38), axis=-1, keepdims=True)


def att_fwd(qp, kp, vp, name):
    B = ATT_BLOCK

    def body(coef_ref, q_ref, kc_ref, kp_ref, vc_ref, vp_ref, o_ref, l_ref):
        g, hp, nb = pl.program_id(0), pl.program_id(1), pl.program_id(2)
        first = jnp.where(nb % _blocks_per_seq(g) == 0, 4 * B, 0)
        qi, kj, d_cur, d_prev = _att_masks(0)
        m_cur = kj <= qi
        m_prev = kj >= qi + first
        q, kc, kpv, vc, vpv = q_ref[0], kc_ref[0], kp_ref[0], vc_ref[0], vp_ref[0]
        o_acc = jnp.zeros((B, 128), F32)
        l_acc = jnp.zeros((B, 128), F32)
        for j in range(2):
            sel = _head_lanes(j)
            cf = coef_ref[g * ATT_HEADS + hp * 2 + j]
            qh = jnp.where(sel, q, 0.0)
            s_cur = jnp.where(m_cur, _dot(qh, kc, tb=True) * 0.125 - cf * d_cur, NEG_INF)
            s_prev = jnp.where(m_prev, _dot(qh, kpv, tb=True) * 0.125 - cf * d_prev, NEG_INF)
            mx = jnp.maximum(jnp.max(s_cur, axis=-1, keepdims=True), jnp.max(s_prev, axis=-1, keepdims=True))
            e_cur = jnp.exp(s_cur - mx)
            e_prev = jnp.exp(s_prev - mx)
            den = jnp.sum(e_cur, axis=-1, keepdims=True) + jnp.sum(e_prev, axis=-1, keepdims=True)
            inv = 1.0 / den
            oh = _dot(e_cur * inv, vc) + _dot(e_prev * inv, vpv)
            o_acc = jnp.where(sel, oh, o_acc)
            l_acc = jnp.where(sel, mx + jnp.log(den), l_acc)
        o_ref[0] = o_acc
        l_ref[0] = l_acc

    cur = pl.BlockSpec((1, B, 128), lambda g, hp, nb: (g, nb, hp))
    prev = pl.BlockSpec((1, B, 128), lambda g, hp, nb: (g, jnp.maximum(nb - 1, 0), hp))
    out = jax.ShapeDtypeStruct((N_GROUPS, SEQ, ATT_WIDTH), F32)
    return pl.pallas_call(
        body, name=name, grid=(N_GROUPS, HEAD_PAIRS, N_BLOCKS),
        in_specs=[pl.BlockSpec(memory_space=pltpu.SMEM), cur, cur, prev, cur, prev],
        out_specs=[cur, cur], out_shape=[out, out],
        compiler_params=pltpu.CompilerParams(dimension_semantics=("parallel", "parallel", "arbitrary")),
    )(_alibi_coef(), qp, kp, kp, vp, vp)


def att_bwd(qp, kp, vp, lp, dop, corrp, name):
    B = ATT_BLOCK

    def body(coef_ref, q_ref, qn_ref, kc_ref, kp_ref, vc_ref, vp_ref, l_ref, ln_ref, do_ref, don_ref, cr_ref, crn_ref,
             dq_ref, dk_ref, dv_ref):
        g, hp, nb = pl.program_id(0), pl.program_id(1), pl.program_id(2)
        bps = _blocks_per_seq(g)
        first = jnp.where(nb % bps == 0, 4 * B, 0)
        no_next = jnp.where((nb + 1) % bps == 0, 4 * B, 0)
        qi, kj, d_cur, d_prev = _att_masks(0)
        m_cc = kj <= qi
        m_cp = kj >= qi + first
        m_nc = kj >= qi + no_next
        q, qn, kc, kpv, vc, vpv = q_ref[0], qn_ref[0], kc_ref[0], kp_ref[0], vc_ref[0], vp_ref[0]
        lv, lnv, dov, donv, crv, crnv = l_ref[0], ln_ref[0], do_ref[0], don_ref[0], cr_ref[0], crn_ref[0]
        dq_acc = jnp.zeros((B, 128), F32)
        dk_acc = jnp.zeros((B, 128), F32)
        dv_acc = jnp.zeros((B, 128), F32)
        for j in range(2):
            sel = _head_lanes(j)
            cf = coef_ref[g * ATT_HEADS + hp * 2 + j]
            qh = jnp.where(sel, q, 0.0)
            qnh = jnp.where(sel, qn, 0.0)
            doh = jnp.where(sel, dov, 0.0)
            donh = jnp.where(sel, donv, 0.0)
            lse, lse_n = _lane_value(lv, sel), _lane_value(lnv, sel)
            cr, cr_n = _lane_value(crv, sel), _lane_value(crnv, sel)
            p_cc = jnp.exp(jnp.where(m_cc, _dot(qh, kc, tb=True) * 0.125 - cf * d_cur, NEG_INF) - lse)
            p_cp = jnp.exp(jnp.where(m_cp, _dot(qh, kpv, tb=True) * 0.125 - cf * d_prev, NEG_INF) - lse)
            p_nc = jnp.exp(jnp.where(m_nc, _dot(qnh, kc, tb=True) * 0.125 - cf * d_prev, NEG_INF) - lse_n)
            ds_cc = p_cc * (_dot(doh, vc, tb=True) + cr)
            ds_cp = p_cp * (_dot(doh, vpv, tb=True) + cr)
            ds_nc = p_nc * (_dot(donh, vc, tb=True) + cr_n)
            dqh = (_dot(ds_cc, kc) + _dot(ds_cp, kpv)) * 0.125
            dkh = (_dot(ds_cc, qh, ta=True) + _dot(ds_nc, qnh, ta=True)) * 0.125
            dvh = _dot(p_cc, doh, ta=True) + _dot(p_nc, donh, ta=True)
            dq_acc = jnp.where(sel, dqh, dq_acc)
            dk_acc = jnp.where(sel, dkh, dk_acc)
            dv_acc = jnp.where(sel, dvh, dv_acc)
        dq_ref[0] = dq_acc.astype(dq_ref.dtype)
        dk_ref[0] = dk_acc.astype(dk_ref.dtype)
        dv_ref[0] = dv_acc.astype(dv_ref.dtype)

    cur = pl.BlockSpec((1, B, 128), lambda g, hp, nb: (g, nb, hp))
    prev = pl.BlockSpec((1, B, 128), lambda g, hp, nb: (g, jnp.maximum(nb - 1, 0), hp))
    nxt = pl.BlockSpec((1, B, 128), lambda g, hp, nb: (g, jnp.minimum(nb + 1, N_BLOCKS - 1), hp))
    out = jax.ShapeDtypeStruct((N_GROUPS, SEQ, ATT_WIDTH), MXU_DTYPE)
    return pl.pallas_call(
        body, name=name, grid=(N_GROUPS, HEAD_PAIRS, N_BLOCKS),
        in_specs=[pl.BlockSpec(memory_space=pltpu.SMEM), cur, nxt, cur, prev, cur, prev, cur, nxt, cur, nxt, cur, nxt],
        out_specs=[cur, cur, cur], out_shape=[out, out, out],
        compiler_params=pltpu.CompilerParams(dimension_semantics=("parallel", "parallel", "arbitrary")),
    )(_alibi_coef(), qp, qp, kp, kp, vp, vp, lp, lp, dop, dop, corrp, corrp)


def _head_sum(x):
    i = lax.broadcasted_iota(jnp.int32, (128, 128), 0) // 64
    j = lax.broadcasted_iota(jnp.int32, (128, 128), 1) // 64
    return _dot_f32(x, (i == j).astype(F32))


def _group_weights(l0, l1, l2):
    mx = jnp.maximum(jnp.maximum(l0, l1), l2)
    e0, e1, e2 = jnp.exp(l0 - mx), jnp.exp(l1 - mx), jnp.exp(l2 - mx)
    inv = 1.0 / (e0 + e1 + e2)
    return e0 * inv, e1 * inv, e2 * inv


def att_combine_fwd(o, l, name):
    def body(o_ref, l_ref, y_ref):
        w0, w1, w2 = _group_weights(l_ref[0], l_ref[1], l_ref[2])
        y_ref[...] = (o_ref[0] * w0 + o_ref[1] * w1 + o_ref[2] * w2).astype(y_ref.dtype)

    blk3 = pl.BlockSpec((N_GROUPS, ROW_TILE, 128), lambda i, j: (0, i, j))
    return pl.pallas_call(
        body, name=name, grid=(SEQ // ROW_TILE, HEAD_PAIRS), in_specs=[blk3, blk3],
        out_specs=pl.BlockSpec((ROW_TILE, 128), lambda i, j: (i, j)),
        out_shape=jax.ShapeDtypeStruct((SEQ, ATT_WIDTH), MXU_DTYPE),
    )(o, l)


def att_combine_bwd(o, l, dy, name):
    def body(o_ref, l_ref, dy_ref, do_ref, cr_ref):
        w = _group_weights(l_ref[0], l_ref[1], l_ref[2])
        dyv = dy_ref[...]
        dw = [_head_sum(dyv * o_ref[g]) for g in range(N_GROUPS)]
        tot = w[0] * dw[0] + w[1] * dw[1] + w[2] * dw[2]
        for g in range(N_GROUPS):
            do_ref[g] = dyv * w[g]
            cr_ref[g] = -w[g] * tot

    blk3 = pl.BlockSpec((N_GROUPS, ROW_TILE, 128), lambda i, j: (0, i, j))
    out = jax.ShapeDtypeStruct((N_GROUPS, SEQ, ATT_WIDTH), F32)
    return pl.pallas_call(
        body, name=name, grid=(SEQ // ROW_TILE, HEAD_PAIRS),
        in_specs=[blk3, blk3, pl.BlockSpec((ROW_TILE, 128), lambda i, j: (i, j))],
        out_specs=[blk3, blk3], out_shape=[out, out],
    )(o, l, dy)


def _permute(a, d):
    if d == 1:
        return a
    return a.reshape(SEQ // d, d, a.shape[-1]).transpose(1, 0, 2).reshape(SEQ, a.shape[-1])


def _unpermute(a, d):
    if d == 1:
        return a
    return a.reshape(d, SEQ // d, a.shape[-1]).transpose(1, 0, 2).reshape(SEQ, a.shape[-1])


def _permute_groups(a):
    return jnp.stack([_permute(a[g], d) for g, (_, d) in enumerate(ATT_GROUPS)])


def _unpermute_groups(a):
    return jnp.stack([_unpermute(a[g], d) for g, (_, d) in enumerate(ATT_GROUPS)])


def add_n(parts, out_dtype, name):
    R, Cc = parts[0].shape
    tr = _pick(R, (256, 128, 64, 32, 16))

    def body(*refs):
        acc = refs[0][...].astype(F32)
        for r in refs[1:-1]:
            acc = acc + r[...].astype(F32)
        refs[-1][...] = acc.astype(out_dtype)

    blk = pl.BlockSpec((tr, Cc), lambda i: (i, 0))
    return pl.pallas_call(
        body, name=name, grid=(R // tr,), in_specs=[blk] * len(parts), out_specs=blk,
        out_shape=jax.ShapeDtypeStruct((R, Cc), out_dtype),
    )(*parts)


def adamw(w, g, m, v, name):
    R, Cc = w.shape
    tr = _pick(R, (256, 128, 64, 8))

    def body(w_ref, g_ref, m_ref, v_ref, d_ref, nm_ref, nv_ref):
        gv = g_ref[...]
        nm = ADAM_B1 * m_ref[...] + (1.0 - ADAM_B1) * gv
        nv = ADAM_B2 * v_ref[...] + (1.0 - ADAM_B2) * (gv * gv)
        m_hat = nm / (1.0 - ADAM_B1 ** ADAM_STEP)
        v_hat = nv / (1.0 - ADAM_B2 ** ADAM_STEP)
        d_ref[...] = -ADAM_LR * (m_hat / (jnp.sqrt(v_hat) + ADAM_EPS) + ADAM_WD * w_ref[...])
        nm_ref[...] = nm
        nv_ref[...] = nv

    blk = pl.BlockSpec((tr, Cc), lambda i: (i, 0))
    out = jax.ShapeDtypeStruct((R, Cc), F32)
    return pl.pallas_call(
        body, name=name, grid=(R // tr,), in_specs=[blk] * 4, out_specs=[blk] * 3, out_shape=[out, out, out],
    )(w, g, m, v)


BIG = (
    ("ffn1_w_gate_up", D_MODEL, 2 * D_FF, 1),
    ("ffn1_w_down", D_FF, D_MODEL, 0),
    ("w_in", D_MODEL, IN_COLS, 1),
    ("w_branch_hg", HG_WIDTH, D_MODEL, 1),
    ("w_branch_att", ATT_WIDTH, D_MODEL, 1),
    ("w_out", D_MODEL, D_MODEL, 0),
    ("ffn2_w_gate_up", D_MODEL, 2 * D_FF, 1),
    ("ffn2_w_down", D_FF, D_MODEL, 0),
)
N_BIG = len(BIG)
ANY = pl.BlockSpec(memory_space=pl.ANY)


def _place():
    return lax.axis_index("x"), lax.axis_index("y"), lax.axis_index("c")


def _other_chips(x, y):
    return ((1 - x, y), (x, 1 - y), (1 - x, 1 - y))


def _shard_shape(rows, cols, axis):
    return (rows // N_CHIPS, cols) if axis == 0 else (rows, cols // N_CHIPS)


def all_gather_weights(shards):
    def half_of_shard(ref, rows, cols, axis, c):
        sr, _ = _shard_shape(rows, cols, axis)
        return ref.at[pl.ds(c * (sr // 2), sr // 2), :]

    def piece(ref, rows, cols, axis, chip, c):
        sr, sc = _shard_shape(rows, cols, axis)
        j = 2 * chip[0] + chip[1]
        if axis == 0:
            return ref.at[pl.ds(j * sr + c * (sr // 2), sr // 2), :]
        return ref.at[pl.ds(c * (sr // 2), sr // 2), pl.ds(pl.multiple_of(j * sc, 128), sc)]

    def whole_block(ref, rows, cols, axis, chip):
        sr, sc = _shard_shape(rows, cols, axis)
        j = 2 * chip[0] + chip[1]
        if axis == 0:
            return ref.at[pl.ds(j * sr, sr), :]
        return ref.at[:, pl.ds(pl.multiple_of(j * sc, 128), sc)]

    def body(*refs):
        ins, outs = refs[:N_BIG], refs[N_BIG:2 * N_BIG]
        send_sems, recv_sems, local_sems = refs[2 * N_BIG:]
        x, y, c = _place()
        chips = _other_chips(x, y)
        local = [pltpu.make_async_copy(ins[w], whole_block(outs[w], r, cc, ax, (x, y)), local_sems.at[w])
                 for w, (_, r, cc, ax) in enumerate(BIG)]
        for cp in local:
            cp.start()
        sends = []
        for w, (_, r, cc, ax) in enumerate(BIG):
            for k, chip in enumerate(chips):
                sends.append(pltpu.make_async_remote_copy(
                    src_ref=half_of_shard(ins[w], r, cc, ax, c), dst_ref=piece(outs[w], r, cc, ax, (x, y), c),
                    send_sem=send_sems.at[w, k], recv_sem=recv_sems.at[w, k], device_id=(*chip, c), device_id_type=MESH))
        for cp in sends:
            cp.start()
        passed = []
        for w, (_, r, cc, ax) in enumerate(BIG):
            for k, chip in enumerate(chips):
                got = piece(outs[w], r, cc, ax, chip, c)
                pltpu.make_async_remote_copy(
                    src_ref=got, dst_ref=got, send_sem=send_sems.at[w, k], recv_sem=recv_sems.at[w, k],
                    device_id=(x, y, c), device_id_type=MESH).wait_recv()
                cp = pltpu.make_async_remote_copy(
                    src_ref=got, dst_ref=got, send_sem=send_sems.at[w, 3 + k], recv_sem=recv_sems.at[w, 3 + k],
                    device_id=(x, y, 1 - c), device_id_type=MESH)
                cp.start()
                passed.append(cp)
        for w, (_, r, cc, ax) in enumerate(BIG):
            for k, chip in enumerate(chips):
                got = piece(outs[w], r, cc, ax, chip, 1 - c)
                pltpu.make_async_remote_copy(
                    src_ref=got, dst_ref=got, send_sem=send_sems.at[w, 3 + k], recv_sem=recv_sems.at[w, 3 + k],
                    device_id=(x, y, c), device_id_type=MESH).wait_recv()
        for cp in sends + passed:
            cp.wait_send()
        for cp in local:
            cp.wait()

    return pl.pallas_call(
        body, name="all_gather_weights", in_specs=[ANY] * N_BIG, out_specs=[ANY] * N_BIG,
        out_shape=[jax.ShapeDtypeStruct((r, cc), WEIGHT_COMM_DTYPE) for _, r, cc, _ in BIG],
        scratch_shapes=[pltpu.SemaphoreType.DMA((N_BIG, 6)), pltpu.SemaphoreType.DMA((N_BIG, 6)),
                        pltpu.SemaphoreType.DMA((N_BIG,))],
    )(*shards)


def _half_shape(rows, cols, axis):
    return (rows, cols // 2) if axis == 0 else (rows // 2, cols)


def _half(ref, rows, cols, axis, c):
    if axis == 0:
        return ref.at[:, pl.ds(pl.multiple_of(c * (cols // 2), 128), cols // 2)]
    return ref.at[pl.ds(c * (rows // 2), rows // 2), :]


def _piece_shape(rows, cols, axis):
    return (rows // N_CHIPS, cols // 2) if axis == 0 else (rows // 2, cols // N_CHIPS)


def _piece_of_half(ref, rows, cols, axis, chip):
    j = 2 * chip[0] + chip[1]
    pr, pc = _piece_shape(rows, cols, axis)
    if axis == 0:
        return ref.at[pl.ds(j * pr, pr), :]
    return ref.at[:, pl.ds(pl.multiple_of(j * pc, 128), pc)]


def exchange_halves(grads):
    def body(*refs):
        ins, mine, theirs = refs[:N_BIG], refs[N_BIG:2 * N_BIG], refs[2 * N_BIG:3 * N_BIG]
        send_sems, recv_sems, local_sems = refs[3 * N_BIG:]
        x, y, c = _place()
        copies = []
        for w, (_, r, cc, ax) in enumerate(BIG):
            lc = pltpu.make_async_copy(_half(ins[w], r, cc, ax, c), mine[w], local_sems.at[w])
            lc.start()
            rc = pltpu.make_async_remote_copy(
                src_ref=_half(ins[w], r, cc, ax, 1 - c), dst_ref=theirs[w], send_sem=send_sems.at[w],
                recv_sem=recv_sems.at[w], device_id=(x, y, 1 - c), device_id_type=MESH)
            rc.start()
            copies.append((lc, rc))
        for lc, rc in copies:
            rc.wait()
            lc.wait()

    half = [jax.ShapeDtypeStruct(_half_shape(r, cc, ax), GRAD_COMM_DTYPE) for _, r, cc, ax in BIG]
    out = pl.pallas_call(
        body, name="exchange_halves", in_specs=[ANY] * N_BIG, out_specs=[ANY] * (2 * N_BIG), out_shape=half + half,
        scratch_shapes=[pltpu.SemaphoreType.DMA((N_BIG,)), pltpu.SemaphoreType.DMA((N_BIG,)),
                        pltpu.SemaphoreType.DMA((N_BIG,))],
    )(*grads)
    return out[:N_BIG], out[N_BIG:]


def scatter_pieces(halves):
    def body(*refs):
        ins, own, got = refs[:N_BIG], refs[N_BIG:2 * N_BIG], refs[2 * N_BIG:3 * N_BIG]
        send_sems, recv_sems, local_sems = refs[3 * N_BIG:]
        x, y, c = _place()
        chips = _other_chips(x, y)
        copies = []
        for w, (_, r, cc, ax) in enumerate(BIG):
            lc = pltpu.make_async_copy(_piece_of_half(ins[w], r, cc, ax, (x, y)), own[w], local_sems.at[w])
            lc.start()
            copies.append(lc)
            for k, chip in enumerate(chips):
                rc = pltpu.make_async_remote_copy(
                    src_ref=_piece_of_half(ins[w], r, cc, ax, chip), dst_ref=got[w].at[k], send_sem=send_sems.at[w, k],
                    recv_sem=recv_sems.at[w, k], device_id=(*chip, c), device_id_type=MESH)
                rc.start()
                copies.append(rc)
        for cp in copies:
            cp.wait()

    piece = [jax.ShapeDtypeStruct(_piece_shape(r, cc, ax), GRAD_COMM_DTYPE) for _, r, cc, ax in BIG]
    three = [jax.ShapeDtypeStruct((3,) + _piece_shape(r, cc, ax), GRAD_COMM_DTYPE) for _, r, cc, ax in BIG]
    out = pl.pallas_call(
        body, name="scatter_pieces", in_specs=[ANY] * N_BIG, out_specs=[ANY] * (2 * N_BIG), out_shape=piece + three,
        scratch_shapes=[pltpu.SemaphoreType.DMA((N_BIG, 3)), pltpu.SemaphoreType.DMA((N_BIG, 3)),
                        pltpu.SemaphoreType.DMA((N_BIG,))],
    )(*halves)
    return out[:N_BIG], out[N_BIG:]


def join_halves(pieces):
    def where(ref, rows, cols, axis, c):
        pr, pc = _piece_shape(rows, cols, axis)
        if axis == 0:
            return ref.at[:, pl.ds(pl.multiple_of(c * pc, 128), pc)]
        return ref.at[pl.ds(c * pr, pr), :]

    def body(*refs):
        ins, outs = refs[:N_BIG], refs[N_BIG:2 * N_BIG]
        send_sems, recv_sems, local_sems = refs[2 * N_BIG:]
        x, y, c = _place()
        copies = []
        for w, (_, r, cc, ax) in enumerate(BIG):
            lc = pltpu.make_async_copy(ins[w], where(outs[w], r, cc, ax, c), local_sems.at[w])
            lc.start()
            rc = pltpu.make_async_remote_copy(
                src_ref=ins[w], dst_ref=where(outs[w], r, cc, ax, c), send_sem=send_sems.at[w],
                recv_sem=recv_sems.at[w], device_id=(x, y, 1 - c), device_id_type=MESH)
            rc.start()
            copies.append((lc, rc))
        for lc, rc in copies:
            rc.wait()
            lc.wait()

    return pl.pallas_call(
        body, name="join_halves", in_specs=[ANY] * N_BIG, out_specs=[ANY] * N_BIG,
        out_shape=[jax.ShapeDtypeStruct(_shard_shape(r, cc, ax), F32) for _, r, cc, ax in BIG],
        scratch_shapes=[pltpu.SemaphoreType.DMA((N_BIG,)), pltpu.SemaphoreType.DMA((N_BIG,)),
                        pltpu.SemaphoreType.DMA((N_BIG,))],
    )(*pieces)


N_DEV = 8
SMALL_ROWS = 8


def all_reduce_small(packed):
    def body(x_ref, o_ref, gathered, send_sems, recv_sems):
        x, y, c = _place()
        me = 4 * x + 2 * y + c
        gathered[me] = x_ref[...]
        copies = []
        for k in range(1, N_DEV):
            peer = (x ^ (k >> 2), y ^ ((k >> 1) & 1), c ^ (k & 1))
            cp = pltpu.make_async_remote_copy(
                src_ref=x_ref, dst_ref=gathered.at[me], send_sem=send_sems.at[k - 1], recv_sem=recv_sems.at[k - 1],
                device_id=peer, device_id_type=MESH)
            cp.start()
            copies.append(cp)
        for cp in copies:
            cp.wait()
        acc = gathered[0]
        for k in range(1, N_DEV):
            acc = acc + gathered[k]
        o_ref[...] = acc

    vm = pl.BlockSpec(memory_space=pltpu.VMEM)
    return pl.pallas_call(
        body, name="all_reduce_small", in_specs=[vm], out_specs=vm,
        out_shape=jax.ShapeDtypeStruct((SMALL_ROWS, D_MODEL), F32),
        scratch_shapes=[pltpu.VMEM((N_DEV, SMALL_ROWS, D_MODEL), F32), pltpu.SemaphoreType.DMA((N_DEV - 1,)),
                        pltpu.SemaphoreType.DMA((N_DEV - 1,))],
    )(packed)


def _swiglu_block_fwd(h, norm_g, w_gu, w_down, tag):
    n = rmsnorm_fwd(h, norm_g, f"{tag}_norm")
    gu = matmul(n, w_gu, name=f"{tag}_gate_up")
    s = swiglu_fwd(gu, f"{tag}_swiglu")
    h_out = matmul(s, w_down, res=h, scale=0.5, name=f"{tag}_down")
    return h_out, (n, gu, s)


def _swiglu_block_bwd(h, norm_g, w_gu, w_down, saved, dh_out, tag):
    n, gu, s = saved
    df = dh_out.astype(MXU_DTYPE)
    d_down = matmul(s, df, ta=True, scale=0.5, out_dtype=GRAD_COMM_DTYPE, name=f"{tag}_d_w_down")
    ds = matmul(df, w_down, tb=True, scale=0.5, name=f"{tag}_d_s")
    dgu = swiglu_bwd(gu, ds, f"{tag}_swiglu_bwd")
    d_gu = matmul(n, dgu, ta=True, out_dtype=GRAD_COMM_DTYPE, name=f"{tag}_d_w_gate_up")
    dn = matmul(dgu, w_gu, tb=True, name=f"{tag}_d_n")
    dh, dg = rmsnorm_bwd(h, norm_g, dn, dh_out, f"{tag}_norm_bwd")
    return dh, dg, d_gu, d_down


ATT_COL0 = 4 * HG_WIDTH


def local_step(x, target, small, big):
    h1, saved1 = _swiglu_block_fwd(x, small["ffn1_norm"], big["ffn1_w_gate_up"], big["ffn1_w_down"], "ffn1")
    u = rmsnorm_fwd(h1, small["mix_norm"], "mix_norm")
    z = matmul(u, big["w_in"], name="w_in")
    p = small["hg_lower_bounds"]
    lb = 1.0 / (1.0 + jnp.exp(p[1:2] - p[0:1]))
    y_hg, o_raw, states = hgrn_fwd(z, lb, small["hg_out_norm"], "hgrn_fwd")
    att = z[:, ATT_COL0:ATT_COL0 + 3 * N_GROUPS * ATT_WIDTH].reshape(SEQ, N_GROUPS, 3, ATT_WIDTH)
    qp = _permute_groups(att[:, :, 0].transpose(1, 0, 2))
    kp = _permute_groups(att[:, :, 1].transpose(1, 0, 2))
    vp = _permute_groups(att[:, :, 2].transpose(1, 0, 2))
    op, lp = att_fwd(qp, kp, vp, "att_fwd")
    o_att, l_att = _unpermute_groups(op), _unpermute_groups(lp)
    y_att = att_combine_fwd(o_att, l_att, "att_combine")
    bh = matmul(y_hg, big["w_branch_hg"], name="branch_hg")
    ba = matmul(y_att, big["w_branch_att"], name="branch_att")
    merged = merge_fwd(z, bh, ba, "merge")
    h2 = matmul(merged, big["w_out"], res=h1, name="w_out")
    h3, saved2 = _swiglu_block_fwd(h2, small["ffn2_norm"], big["ffn2_w_gate_up"], big["ffn2_w_down"], "ffn2")
    dh3, d_final, loss = final_norm_loss(h3, small["final_norm"], target, "final_norm_loss")

    gs, gb = {"final_norm": d_final}, {}
    dh2, gs["ffn2_norm"], gb["ffn2_w_gate_up"], gb["ffn2_w_down"] = _swiglu_block_bwd(
        h2, small["ffn2_norm"], big["ffn2_w_gate_up"], big["ffn2_w_down"], saved2, dh3, "ffn2")
    dh2_m = dh2.astype(MXU_DTYPE)
    gb["w_out"] = matmul(merged, dh2_m, ta=True, out_dtype=GRAD_COMM_DTYPE, name="d_w_out")
    dmerged = matmul(dh2_m, big["w_out"], tb=True, name="d_merged")
    dbh, dba, dgh, dga = merge_bwd(z, bh, ba, dmerged, "merge_bwd")
    gb["w_branch_hg"] = matmul(y_hg, dbh, ta=True, out_dtype=GRAD_COMM_DTYPE, name="d_w_branch_hg")
    gb["w_branch_att"] = matmul(y_att, dba, ta=True, out_dtype=GRAD_COMM_DTYPE, name="d_w_branch_att")
    dy_hg = matmul(dbh, big["w_branch_hg"], tb=True, name="d_y_hg")
    dy_att = matmul(dba, big["w_branch_att"], tb=True, name="d_y_att")
    dq, dfp, di, dog, d_lb, gs["hg_out_norm"] = hgrn_bwd(z, lb, small["hg_out_norm"], o_raw, states, dy_hg, "hgrn_bwd")
    do_att, corr = att_combine_bwd(o_att, l_att, dy_att, "att_combine_bwd")
    dqp, dkp, dvp = att_bwd(qp, kp, vp, lp, _permute_groups(do_att), _permute_groups(corr), "att_bwd")
    dqa, dka, dva = _unpermute_groups(dqp), _unpermute_groups(dkp), _unpermute_groups(dvp)
    d_att = jnp.stack([dqa, dka, dva], axis=1).transpose(2, 0, 1, 3).reshape(SEQ, 3 * N_GROUPS * ATT_WIDTH)
    dz = jnp.concatenate([dq, dfp, di, dog, d_att, dgh, dga], axis=1)
    gb["w_in"] = matmul(u, dz, ta=True, out_dtype=GRAD_COMM_DTYPE, name="d_w_in")
    du = matmul(dz, big["w_in"], tb=True, name="d_u")
    dh1, gs["mix_norm"] = rmsnorm_bwd(h1, small["mix_norm"], du, dh2, "mix_norm_bwd")
    dp0 = d_lb * lb * (1.0 - lb)
    gs["hg_lower_bounds"] = jnp.concatenate([dp0, -dp0], axis=0)
    dx, gs["ffn1_norm"], gb["ffn1_w_gate_up"], gb["ffn1_w_down"] = _swiglu_block_bwd(
        x, small["ffn1_norm"], big["ffn1_w_gate_up"], big["ffn1_w_down"], saved1, dh1, "ffn1")
    return loss[0, 0], dx, gs, gb


SMALL = ("ffn1_norm", "mix_norm", "hg_lower_bounds", "hg_out_norm", "ffn2_norm", "final_norm")
WEIGHTS = ("ffn1_norm", "ffn1_w_gate_up", "ffn1_w_down", "mix_norm", "w_in", "hg_lower_bounds", "hg_out_norm",
           "w_branch_hg", "w_branch_att", "w_out", "ffn2_norm", "ffn2_w_gate_up", "ffn2_w_down", "final_norm")
SMALL_SHAPE = {"ffn1_norm": (1, 1024), "mix_norm": (1, 1024), "hg_lower_bounds": (2, 512), "hg_out_norm": (1, 512),
               "ffn2_norm": (1, 1024), "final_norm": (1024,)}
LOSS_ROW = 6


def _pack_small(vals):
    rows = []
    for n in SMALL:
        r = vals[n].reshape(1, -1).astype(F32)
        rows.append(jnp.pad(r, ((0, 0), (0, D_MODEL - r.shape[1]))))
    rows.append(jnp.zeros((SMALL_ROWS - len(SMALL), D_MODEL), F32))
    return jnp.concatenate(rows, axis=0)


def _unpack_small(packed):
    out = {}
    for i, n in enumerate(SMALL):
        size = int(np.prod(SMALL_SHAPE[n]))
        out[n] = packed[i, :size].reshape(SMALL_SHAPE[n])
    return out


def reduce_big(grads):
    mine, theirs = exchange_halves([grads[n] for n, *_ in BIG])
    halves = [add_n([a, b], GRAD_COMM_DTYPE, f"add_halves_{n}") for (n, *_), a, b in zip(BIG, mine, theirs)]
    own, got = scatter_pieces(halves)
    pieces = [add_n([a, b[0], b[1], b[2]], F32, f"add_pieces_{n}") for (n, *_), a, b in zip(BIG, own, got)]
    blocks = join_halves(pieces)
    return {n: b for (n, *_), b in zip(BIG, blocks)}


def kernel(x, ffn1_norm, ffn1_w_gate_up, ffn1_w_down, mix_norm, w_in, hg_lower_bounds, hg_out_norm, w_branch_hg, w_branch_att, w_out, ffn2_norm, ffn2_w_gate_up, ffn2_w_down, final_norm, loss_target, m_ffn1_norm, m_ffn1_w_gate_up, m_ffn1_w_down, m_mix_norm, m_w_in, m_hg_lower_bounds, m_hg_out_norm, m_w_branch_hg, m_w_branch_att, m_w_out, m_ffn2_norm, m_ffn2_w_gate_up, m_ffn2_w_down, m_final_norm, v_ffn1_norm, v_ffn1_w_gate_up, v_ffn1_w_down, v_mix_norm, v_w_in, v_hg_lower_bounds, v_hg_out_norm, v_w_branch_hg, v_w_branch_att, v_w_out, v_ffn2_norm, v_ffn2_w_gate_up, v_ffn2_w_down, v_final_norm):
    w = dict(ffn1_norm=ffn1_norm, ffn1_w_gate_up=ffn1_w_gate_up, ffn1_w_down=ffn1_w_down, mix_norm=mix_norm, w_in=w_in,
             hg_lower_bounds=hg_lower_bounds, hg_out_norm=hg_out_norm, w_branch_hg=w_branch_hg, w_branch_att=w_branch_att,
             w_out=w_out, ffn2_norm=ffn2_norm, ffn2_w_gate_up=ffn2_w_gate_up, ffn2_w_down=ffn2_w_down, final_norm=final_norm)
    m = dict(ffn1_norm=m_ffn1_norm, ffn1_w_gate_up=m_ffn1_w_gate_up, ffn1_w_down=m_ffn1_w_down, mix_norm=m_mix_norm,
             w_in=m_w_in, hg_lower_bounds=m_hg_lower_bounds, hg_out_norm=m_hg_out_norm, w_branch_hg=m_w_branch_hg,
             w_branch_att=m_w_branch_att, w_out=m_w_out, ffn2_norm=m_ffn2_norm, ffn2_w_gate_up=m_ffn2_w_gate_up,
             ffn2_w_down=m_ffn2_w_down, final_norm=m_final_norm)
    v = dict(ffn1_norm=v_ffn1_norm, ffn1_w_gate_up=v_ffn1_w_gate_up, ffn1_w_down=v_ffn1_w_down, mix_norm=v_mix_norm,
             w_in=v_w_in, hg_lower_bounds=v_hg_lower_bounds, hg_out_norm=v_hg_out_norm, w_branch_hg=v_w_branch_hg,
             w_branch_att=v_w_branch_att, w_out=v_w_out, ffn2_norm=v_ffn2_norm, ffn2_w_gate_up=v_ffn2_w_gate_up,
             ffn2_w_down=v_ffn2_w_down, final_norm=v_final_norm)

    gathered = all_gather_weights([w[n][0].astype(WEIGHT_COMM_DTYPE) for n, *_ in BIG])
    big = {n: a for (n, *_), a in zip(BIG, gathered)}
    small = {n: w[n] for n in SMALL}
    small["final_norm"] = final_norm.reshape(1, D_MODEL)

    loss, dx, gs, gb = local_step(x[0], loss_target[0], small, big)

    packed = _pack_small(gs)
    packed = packed.at[LOSS_ROW].set(jnp.full((D_MODEL,), loss, F32))
    total = all_reduce_small(packed)
    grads = _unpack_small(total)
    loss_total = total[LOSS_ROW, 0]
    blocks = reduce_big(gb)
    for n, *_ in BIG:
        grads[n] = blocks[n][None]

    delta, new_m, new_v = {}, {}, {}
    pd, pm, pv = adamw(_pack_small({n: w[n] for n in SMALL}), total.at[LOSS_ROW].set(0.0),
                       _pack_small({n: m[n] for n in SMALL}), _pack_small({n: v[n] for n in SMALL}), "adamw_small")
    delta.update(_unpack_small(pd))
    new_m.update(_unpack_small(pm))
    new_v.update(_unpack_small(pv))
    for n, *_ in BIG:
        d, nm, nv = adamw(w[n][0], blocks[n], m[n][0], v[n][0], f"adamw_{n}")
        delta[n], new_m[n], new_v[n] = d[None], nm[None], nv[None]

    return (loss_total, dx[None], *[grads[n] for n in WEIGHTS], *[delta[n] for n in WEIGHTS],
            *[new_m[n] for n in WEIGHTS], *[new_v[n] for n in WEIGHTS])
```

```python
import numpy as np
import jax
import jax.numpy as jnp
from jax import lax
from jax.experimental import pallas as pl
from jax.experimental.pallas import tpu as pltpu

SEQ = 2048
D_MODEL = 1024
D_FF = 2816
HG_HEADS = 4
HG_DIM = 128
HG_WIDTH = 512
HG_CHUNK = 64
ATT_GROUPS = ((128, 1), (512, 4), (2048, 16))
ATT_HEADS = 8
ATT_WIDTH = 512
ATT_BLOCK = 128
ALIBI_MAX = 8.0
IN_COLS = 8704
EPS = 1e-6
NEG_INF = -1e30
ADAM_LR = 0.001
ADAM_B1 = 0.9
ADAM_B2 = 0.999
ADAM_EPS = 1e-08
ADAM_WD = 0.01
ADAM_STEP = 10

N_CHIPS = 4
MXU_DTYPE = jnp.bfloat16
HG_DOT_DTYPE = jnp.float32
WEIGHT_COMM_DTYPE = jnp.bfloat16
GRAD_COMM_DTYPE = jnp.bfloat16
MESH = pl.DeviceIdType.MESH
F32 = jnp.float32
HIGHEST = lax.Precision.HIGHEST


def _pick(n, cands):
    for c in cands:
        if n % c == 0:
            return c
    return n


def _sigmoid(x):
    return 1.0 / (1.0 + jnp.exp(-x))


def _dot(a, b, ta=False, tb=False):
    dn = (((0 if ta else 1,), (1 if tb else 0,)), ((), ()))
    return lax.dot_general(a.astype(MXU_DTYPE), b.astype(MXU_DTYPE), dn, preferred_element_type=F32)


def _dot_f32(a, b):
    return jnp.dot(a, b, precision=HIGHEST, preferred_element_type=F32)


def _hdot(a, b, ta=False, tb=False):
    if HG_DOT_DTYPE == F32:
        dn = (((0 if ta else 1,), (1 if tb else 0,)), ((), ()))
        return lax.dot_general(a, b, dn, precision=HIGHEST, preferred_element_type=F32)
    return _dot(a, b, ta, tb)


MATMUL_VMEM_BYTES = 40 * 1024 * 1024


def matmul(a, b, *, ta=False, tb=False, out_dtype=F32, res=None, scale=1.0, name):
    if ta:
        K, M = a.shape
    else:
        M, K = a.shape
    if tb:
        N, K2 = b.shape
    else:
        K2, N = b.shape
    assert K == K2
    tm = _pick(M, (1024, 512, 256, 128))
    tn = _pick(N, (512, 256, 128))
    tk = _pick(K, (512, 256, 128))
    nk = K // tk

    def body(*refs):
        if res is None:
            a_ref, b_ref, o_ref, acc = refs
        else:
            a_ref, b_ref, r_ref, o_ref, acc = refs
        k = pl.program_id(2)

        @pl.when(k == 0)
        def _():
            acc[...] = jnp.zeros_like(acc)

        acc[...] += _dot(a_ref[...], b_ref[...], ta, tb)

        @pl.when(k == nk - 1)
        def _():
            r = acc[...]
            if scale != 1.0:
                r = r * scale
            if res is not None:
                r = r_ref[...] + r
            o_ref[...] = r.astype(out_dtype)

    a_spec = pl.BlockSpec((tk, tm), lambda i, j, k: (k, i)) if ta else pl.BlockSpec((tm, tk), lambda i, j, k: (i, k))
    b_spec = pl.BlockSpec((tn, tk), lambda i, j, k: (j, k)) if tb else pl.BlockSpec((tk, tn), lambda i, j, k: (k, j))
    in_specs = [a_spec, b_spec]
    args = [a, b]
    if res is not None:
        in_specs.append(pl.BlockSpec((tm, tn), lambda i, j, k: (i, j)))
        args.append(res)
    return pl.pallas_call(
        body, name=name, grid=(M // tm, N // tn, nk), in_specs=in_specs,
        out_specs=pl.BlockSpec((tm, tn), lambda i, j, k: (i, j)),
        out_shape=jax.ShapeDtypeStruct((M, N), out_dtype),
        scratch_shapes=[pltpu.VMEM((tm, tn), F32)],
        compiler_params=pltpu.CompilerParams(dimension_semantics=("parallel", "parallel", "arbitrary"),
                                             vmem_limit_bytes=MATMUL_VMEM_BYTES),
    )(*args)


ROW_TILE = 256


def rmsnorm_fwd(x, g, name):
    def body(x_ref, g_ref, n_ref):
        xv = x_ref[...]
        r = lax.rsqrt(jnp.mean(xv * xv, axis=-1, keepdims=True) + EPS)
        n_ref[...] = ((xv * r) * g_ref[...]).astype(n_ref.dtype)

    return pl.pallas_call(
        body, name=name, grid=(SEQ // ROW_TILE,),
        in_specs=[pl.BlockSpec((ROW_TILE, D_MODEL), lambda i: (i, 0)), pl.BlockSpec((1, D_MODEL), lambda i: (0, 0))],
        out_specs=pl.BlockSpec((ROW_TILE, D_MODEL), lambda i: (i, 0)),
        out_shape=jax.ShapeDtypeStruct((SEQ, D_MODEL), MXU_DTYPE),
    )(x, g)


def rmsnorm_bwd(x, g, dn, dres, name):
    def body(x_ref, g_ref, dn_ref, dr_ref, dx_ref, dg_ref):
        xv = x_ref[...]
        r = lax.rsqrt(jnp.mean(xv * xv, axis=-1, keepdims=True) + EPS)
        xh = xv * r
        dnv = dn_ref[...]

        @pl.when(pl.program_id(0) == 0)
        def _():
            dg_ref[...] = jnp.zeros_like(dg_ref)

        dg_ref[...] += jnp.sum(dnv * xh, axis=0, keepdims=True)
        dxh = dnv * g_ref[...]
        dx_ref[...] = dr_ref[...] + r * (dxh - xh * jnp.mean(dxh * xh, axis=-1, keepdims=True))

    row = pl.BlockSpec((ROW_TILE, D_MODEL), lambda i: (i, 0))
    vec = pl.BlockSpec((1, D_MODEL), lambda i: (0, 0))
    return pl.pallas_call(
        body, name=name, grid=(SEQ // ROW_TILE,), in_specs=[row, vec, row, row], out_specs=[row, vec],
        out_shape=[jax.ShapeDtypeStruct((SEQ, D_MODEL), F32), jax.ShapeDtypeStruct((1, D_MODEL), F32)],
        compiler_params=pltpu.CompilerParams(dimension_semantics=("arbitrary",)),
    )(x, g, dn, dres)


def final_norm_loss(h, g, target, name):
    def body(h_ref, g_ref, t_ref, dh_ref, dg_ref, loss_ref):
        xv = h_ref[...]
        r = lax.rsqrt(jnp.mean(xv * xv, axis=-1, keepdims=True) + EPS)
        xh = xv * r
        gv = g_ref[...]
        e = xh * gv - t_ref[...]

        @pl.when(pl.program_id(0) == 0)
        def _():
            dg_ref[...] = jnp.zeros_like(dg_ref)
            loss_ref[...] = jnp.zeros_like(loss_ref)

        part = 0.5 * jnp.sum(jnp.sum(e * e, axis=-1, keepdims=True) * (1.0 / D_MODEL), axis=0, keepdims=True)
        loss_ref[...] += jnp.broadcast_to(part, loss_ref.shape)
        dout = e * (1.0 / D_MODEL)
        dg_ref[...] += jnp.sum(dout * xh, axis=0, keepdims=True)
        dxh = dout * gv
        dh_ref[...] = r * (dxh - xh * jnp.mean(dxh * xh, axis=-1, keepdims=True))

    row = pl.BlockSpec((ROW_TILE, D_MODEL), lambda i: (i, 0))
    vec = pl.BlockSpec((1, D_MODEL), lambda i: (0, 0))
    return pl.pallas_call(
        body, name=name, grid=(SEQ // ROW_TILE,), in_specs=[row, vec, row],
        out_specs=[row, vec, pl.BlockSpec((8, 128), lambda i: (0, 0))],
        out_shape=[jax.ShapeDtypeStruct((SEQ, D_MODEL), F32), jax.ShapeDtypeStruct((1, D_MODEL), F32),
                   jax.ShapeDtypeStruct((8, 128), F32)],
        compiler_params=pltpu.CompilerParams(dimension_semantics=("arbitrary",)),
    )(h, g, target)


FF_TILE = D_FF // 2


def swiglu_fwd(gu, name):
    def body(a_ref, b_ref, s_ref):
        a = a_ref[...]
        s_ref[...] = (a * _sigmoid(a) * b_ref[...]).astype(s_ref.dtype)

    return pl.pallas_call(
        body, name=name, grid=(SEQ // ROW_TILE, 2),
        in_specs=[pl.BlockSpec((ROW_TILE, FF_TILE), lambda i, j: (i, j)),
                  pl.BlockSpec((ROW_TILE, FF_TILE), lambda i, j: (i, j + 2))],
        out_specs=pl.BlockSpec((ROW_TILE, FF_TILE), lambda i, j: (i, j)),
        out_shape=jax.ShapeDtypeStruct((SEQ, D_FF), MXU_DTYPE),
    )(gu, gu)


def swiglu_bwd(gu, ds, name):
    def body(a_ref, b_ref, ds_ref, o_ref):
        a = a_ref[...]
        sg = _sigmoid(a)
        dsv = ds_ref[...]

        @pl.when(pl.program_id(1) < 2)
        def _():
            o_ref[...] = (dsv * b_ref[...] * (sg * (1.0 + a * (1.0 - sg)))).astype(o_ref.dtype)

        @pl.when(pl.program_id(1) >= 2)
        def _():
            o_ref[...] = (dsv * a * sg).astype(o_ref.dtype)

    return pl.pallas_call(
        body, name=name, grid=(SEQ // ROW_TILE, 4),
        in_specs=[pl.BlockSpec((ROW_TILE, FF_TILE), lambda i, j: (i, j % 2)),
                  pl.BlockSpec((ROW_TILE, FF_TILE), lambda i, j: (i, j % 2 + 2)),
                  pl.BlockSpec((ROW_TILE, FF_TILE), lambda i, j: (i, j % 2))],
        out_specs=pl.BlockSpec((ROW_TILE, FF_TILE), lambda i, j: (i, j)),
        out_shape=jax.ShapeDtypeStruct((SEQ, 2 * D_FF), MXU_DTYPE),
    )(gu, gu, ds)


GATE_HG_BLK = 6656 // 512
GATE_ATT_BLK = 7680 // 512


def merge_fwd(z, bh, ba, name):
    def body(gh_ref, ga_ref, bh_ref, ba_ref, o_ref):
        o_ref[...] = (_sigmoid(gh_ref[...]) * bh_ref[...] + _sigmoid(ga_ref[...]) * ba_ref[...]).astype(o_ref.dtype)

    blk = pl.BlockSpec((ROW_TILE, 512), lambda i, j: (i, j))
    return pl.pallas_call(
        body, name=name, grid=(SEQ // ROW_TILE, 2),
        in_specs=[pl.BlockSpec((ROW_TILE, 512), lambda i, j: (i, GATE_HG_BLK + j)),
                  pl.BlockSpec((ROW_TILE, 512), lambda i, j: (i, GATE_ATT_BLK + j)), blk, blk],
        out_specs=blk, out_shape=jax.ShapeDtypeStruct((SEQ, D_MODEL), MXU_DTYPE),
    )(z, z, bh, ba)


def merge_bwd(z, bh, ba, dm, name):
    def body(gh_ref, ga_ref, bh_ref, ba_ref, dm_ref, dbh_ref, dba_ref, dgh_ref, dga_ref):
        dmv = dm_ref[...]
        sh = _sigmoid(gh_ref[...])
        sa = _sigmoid(ga_ref[...])
        dbh_ref[...] = (dmv * sh).astype(dbh_ref.dtype)
        dba_ref[...] = (dmv * sa).astype(dba_ref.dtype)
        dgh_ref[...] = (dmv * bh_ref[...] * (sh * (1.0 - sh))).astype(dgh_ref.dtype)
        dga_ref[...] = (dmv * ba_ref[...] * (sa * (1.0 - sa))).astype(dga_ref.dtype)

    blk = pl.BlockSpec((ROW_TILE, 512), lambda i, j: (i, j))
    out = jax.ShapeDtypeStruct((SEQ, D_MODEL), MXU_DTYPE)
    return pl.pallas_call(
        body, name=name, grid=(SEQ // ROW_TILE, 2),
        in_specs=[pl.BlockSpec((ROW_TILE, 512), lambda i, j: (i, GATE_HG_BLK + j)),
                  pl.BlockSpec((ROW_TILE, 512), lambda i, j: (i, GATE_ATT_BLK + j)), blk, blk, blk],
        out_specs=[blk, blk, blk, blk], out_shape=[out, out, out, out],
    )(z, z, bh, ba, dm)


N_CHUNKS = SEQ // HG_CHUNK


def _hgrn_gates(q, fp, lb):
    C = HG_CHUNK
    sg = _sigmoid(fp)
    f = lb + (1.0 - lb) * sg
    lf = jnp.log(f)
    row = lax.broadcasted_iota(jnp.int32, (C, C), 0)
    col = lax.broadcasted_iota(jnp.int32, (C, C), 1)
    causal = row >= col
    G = _dot_f32(causal.astype(F32), lf)
    eG = jnp.exp(G)
    enG = jnp.exp(-G)
    qg = q * eG
    kg = (1.0 - f) * enG
    A = jnp.where(causal, _hdot(qg, kg, tb=True), 0.0)
    egl = jnp.exp(jnp.sum(lf, axis=0, keepdims=True))
    return sg, f, causal, eG, enG, qg, kg, A, egl


def hgrn_fwd(z, lb, gain, name):
    C, K = HG_CHUNK, HG_DIM

    def body(q_ref, f_ref, v_ref, og_ref, p_ref, g_ref, y_ref, o_ref, st_ref, state):
        @pl.when(pl.program_id(1) == 0)
        def _():
            state[...] = jnp.zeros_like(state)

        v = v_ref[...]
        _, _, _, _, _, qg, kg, A, egl = _hgrn_gates(q_ref[...], f_ref[...], p_ref[...])
        st = state[...]
        st_ref[0, 0] = st
        o = _hdot(A, v) + _hdot(qg, st, tb=True)
        state[...] = st * egl + _hdot(v, kg * egl, ta=True)
        o_ref[...] = o
        rs = lax.rsqrt(jnp.mean(o * o, axis=-1, keepdims=True) + EPS)
        og = og_ref[...]
        y_ref[...] = (((o * rs) * g_ref[...]) * (og * _sigmoid(og))).astype(y_ref.dtype)

    def zcol(section):
        return pl.BlockSpec((C, K), lambda h, c: (c, section * HG_HEADS + h))

    vec = pl.BlockSpec((1, K), lambda h, c: (0, h))
    blk = pl.BlockSpec((C, K), lambda h, c: (c, h))
    return pl.pallas_call(
        body, name=name, grid=(HG_HEADS, N_CHUNKS),
        in_specs=[zcol(0), zcol(1), zcol(2), zcol(3), vec, vec],
        out_specs=[blk, blk, pl.BlockSpec((1, 1, K, K), lambda h, c: (h, c, 0, 0))],
        out_shape=[jax.ShapeDtypeStruct((SEQ, HG_WIDTH), MXU_DTYPE), jax.ShapeDtypeStruct((SEQ, HG_WIDTH), F32),
                   jax.ShapeDtypeStruct((HG_HEADS, N_CHUNKS, K, K), F32)],
        scratch_shapes=[pltpu.VMEM((K, K), F32)],
        compiler_params=pltpu.CompilerParams(dimension_semantics=("parallel", "arbitrary")),
    )(z, z, z, z, lb, gain)


def hgrn_bwd(z, lb, gain, o_raw, states, dy, name):
    C, K = HG_CHUNK, HG_DIM

    def body(q_ref, f_ref, v_ref, og_ref, p_ref, g_ref, o_ref, st_ref, dy_ref,
             dq_ref, dfp_ref, dv_ref, dog_ref, dlb_ref, dgain_ref, dstate):
        @pl.when(pl.program_id(1) == 0)
        def _():
            dstate[...] = jnp.zeros_like(dstate)
            dlb_ref[...] = jnp.zeros_like(dlb_ref)
            dgain_ref[...] = jnp.zeros_like(dgain_ref)

        v = v_ref[...]
        lb = p_ref[...]
        sg, f, causal, eG, enG, qg, kg, A, egl = _hgrn_gates(q_ref[...], f_ref[...], lb)
        kd = kg * egl
        st = st_ref[0, 0]
        dst = dstate[...]
        o = o_ref[...]
        og = og_ref[...]
        gain_v = g_ref[...]
        dyv = dy_ref[...]
        rs = lax.rsqrt(jnp.mean(o * o, axis=-1, keepdims=True) + EPS)
        on = o * rs
        sgo = _sigmoid(og)
        silu = og * sgo
        dog_ref[...] = (dyv * (on * gain_v) * (sgo * (1.0 + og * (1.0 - sgo)))).astype(dog_ref.dtype)
        dgain_ref[...] += jnp.sum(dyv * silu * on, axis=0, keepdims=True)
        don = dyv * gain_v * silu
        do = rs * (don - on * jnp.mean(don * on, axis=-1, keepdims=True))
        dA = jnp.where(causal, _hdot(do, v, tb=True), 0.0)
        dv_ref[...] = (_hdot(A, do, ta=True) + _hdot(kd, dst, tb=True)).astype(dv_ref.dtype)
        dqg = _hdot(dA, kg) + _hdot(do, st)
        dkg = _hdot(dA, qg, ta=True)
        dkd = _hdot(v, dst)
        dstate[...] = dst * egl + _hdot(do, qg, ta=True)
        dgl = jnp.sum(st * dst, axis=0, keepdims=True) * egl
        dq_ref[...] = (dqg * eG).astype(dq_ref.dtype)
        dk = dkg * enG + dkd * (enG * egl)
        dG = dqg * qg - dkg * kg - dkd * kd
        extra = jnp.sum(dkd * kd, axis=0, keepdims=True) + dgl
        last = lax.broadcasted_iota(jnp.int32, (C, K), 0) == C - 1
        dG = dG + jnp.where(last, extra, 0.0)
        row = lax.broadcasted_iota(jnp.int32, (C, C), 0)
        col = lax.broadcasted_iota(jnp.int32, (C, C), 1)
        dlf = _dot_f32((col >= row).astype(F32), dG)
        df = dlf / f - dk
        dfp_ref[...] = (df * (1.0 - lb) * (sg * (1.0 - sg))).astype(dfp_ref.dtype)
        dlb_ref[...] += jnp.sum(df * (1.0 - sg), axis=0, keepdims=True)

    def rc(c):
        return N_CHUNKS - 1 - c

    def zcol(section):
        return pl.BlockSpec((C, K), lambda h, c: (rc(c), section * HG_HEADS + h))

    vec = pl.BlockSpec((1, K), lambda h, c: (0, h))
    blk = pl.BlockSpec((C, K), lambda h, c: (rc(c), h))
    out = jax.ShapeDtypeStruct((SEQ, HG_WIDTH), MXU_DTYPE)
    small = jax.ShapeDtypeStruct((1, HG_WIDTH), F32)
    return pl.pallas_call(
        body, name=name, grid=(HG_HEADS, N_CHUNKS),
        in_specs=[zcol(0), zcol(1), zcol(2), zcol(3), vec, vec, blk,
                  pl.BlockSpec((1, 1, K, K), lambda h, c: (h, rc(c), 0, 0)), blk],
        out_specs=[blk, blk, blk, blk, vec, vec],
        out_shape=[out, out, out, out, small, small],
        scratch_shapes=[pltpu.VMEM((K, K), F32)],
        compiler_params=pltpu.CompilerParams(dimension_semantics=("parallel", "arbitrary")),
    )(z, z, z, z, lb, gain, o_raw, states, dy)


N_GROUPS = len(ATT_GROUPS)
N_BLOCKS = SEQ // ATT_BLOCK
HEAD_PAIRS = ATT_WIDTH // 128


def _alibi_coef():
    n = N_GROUPS * ATT_HEADS
    slopes = np.exp2(-ALIBI_MAX * np.arange(1, n + 1, dtype=np.float32) / n).astype(np.float32)
    dil = np.repeat(np.array([d for _, d in ATT_GROUPS], np.float32), ATT_HEADS)
    return jnp.asarray(slopes * dil, F32)


def _blocks_per_seq(g):
    return jnp.where(g == 0, N_BLOCKS // ATT_GROUPS[0][1],
                     jnp.where(g == 1, N_BLOCKS // ATT_GROUPS[1][1], N_BLOCKS // ATT_GROUPS[2][1]))


def _att_masks(off_cur):
    B = ATT_BLOCK
    qi = lax.broadcasted_iota(jnp.int32, (B, B), 0)
    kj = lax.broadcasted_iota(jnp.int32, (B, B), 1)
    return qi, kj, (qi - kj).astype(F32), (qi + B - kj).astype(F32)


def _head_lanes(j):
    lane = lax.broadcasted_iota(jnp.int32, (ATT_BLOCK, 128), 1)
    return (lane >= 64 * j) & (lane < 64 * (j + 1))


def _lane_value(x, sel):
    return jnp.max(jnp.where(sel, x, -3e38), axis=-1, keepdims=True)


def att_fwd(qp, kp, vp, name):
    B = ATT_BLOCK

    def body(coef_ref, q_ref, kc_ref, kp_ref, vc_ref, vp_ref, o_ref, l_ref):
        g, hp, nb = pl.program_id(0), pl.program_id(1), pl.program_id(2)
        first = jnp.where(nb % _blocks_per_seq(g) == 0, 4 * B, 0)
        qi, kj, d_cur, d_prev = _att_masks(0)
        m_cur = kj <= qi
        m_prev = kj >= qi + first
        q, kc, kpv, vc, vpv = q_ref[0], kc_ref[0], kp_ref[0], vc_ref[0], vp_ref[0]
        o_acc = jnp.zeros((B, 128), F32)
        l_acc = jnp.zeros((B, 128), F32)
        for j in range(2):
            sel = _head_lanes(j)
            cf = coef_ref[g * ATT_HEADS + hp * 2 + j]
            qh = jnp.where(sel, q, 0.0)
            s_cur = jnp.where(m_cur, _dot(qh, kc, tb=True) * 0.125 - cf * d_cur, NEG_INF)
            s_prev = jnp.where(m_prev, _dot(qh, kpv, tb=True) * 0.125 - cf * d_prev, NEG_INF)
            mx = jnp.maximum(jnp.max(s_cur, axis=-1, keepdims=True), jnp.max(s_prev, axis=-1, keepdims=True))
            e_cur = jnp.exp(s_cur - mx)
            e_prev = jnp.exp(s_prev - mx)
            den = jnp.sum(e_cur, axis=-1, keepdims=True) + jnp.sum(e_prev, axis=-1, keepdims=True)
            inv = 1.0 / den
            oh = _dot(e_cur * inv, vc) + _dot(e_prev * inv, vpv)
            o_acc = jnp.where(sel, oh, o_acc)
            l_acc = jnp.where(sel, mx + jnp.log(den), l_acc)
        o_ref[0] = o_acc
        l_ref[0] = l_acc

    cur = pl.BlockSpec((1, B, 128), lambda g, hp, nb: (g, nb, hp))
    prev = pl.BlockSpec((1, B, 128), lambda g, hp, nb: (g, jnp.maximum(nb - 1, 0), hp))
    out = jax.ShapeDtypeStruct((N_GROUPS, SEQ, ATT_WIDTH), F32)
    return pl.pallas_call(
        body, name=name, grid=(N_GROUPS, HEAD_PAIRS, N_BLOCKS),
        in_specs=[pl.BlockSpec(memory_space=pltpu.SMEM), cur, cur, prev, cur, prev],
        out_specs=[cur, cur], out_shape=[out, out],
        compiler_params=pltpu.CompilerParams(dimension_semantics=("parallel", "parallel", "arbitrary")),
    )(_alibi_coef(), qp, kp, kp, vp, vp)


def att_bwd(qp, kp, vp, lp, dop, corrp, name):
    B = ATT_BLOCK

    def body(coef_ref, q_ref, qn_ref, kc_ref, kp_ref, vc_ref, vp_ref, l_ref, ln_ref, do_ref, don_ref, cr_ref, crn_ref,
             dq_ref, dk_ref, dv_ref):
        g, hp, nb = pl.program_id(0), pl.program_id(1), pl.program_id(2)
        bps = _blocks_per_seq(g)
        first = jnp.where(nb % bps == 0, 4 * B, 0)
        no_next = jnp.where((nb + 1) % bps == 0, 4 * B, 0)
        qi, kj, d_cur, d_prev = _att_masks(0)
        m_cc = kj <= qi
        m_cp = kj >= qi + first
        m_nc = kj >= qi + no_next
        q, qn, kc, kpv, vc, vpv = q_ref[0], qn_ref[0], kc_ref[0], kp_ref[0], vc_ref[0], vp_ref[0]
        lv, lnv, dov, donv, crv, crnv = l_ref[0], ln_ref[0], do_ref[0], don_ref[0], cr_ref[0], crn_ref[0]
        dq_acc = jnp.zeros((B, 128), F32)
        dk_acc = jnp.zeros((B, 128), F32)
        dv_acc = jnp.zeros((B, 128), F32)
        for j in range(2):
            sel = _head_lanes(j)
            cf = coef_ref[g * ATT_HEADS + hp * 2 + j]
            qh = jnp.where(sel, q, 0.0)
            qnh = jnp.where(sel, qn, 0.0)
            doh = jnp.where(sel, dov, 0.0)
            donh = jnp.where(sel, donv, 0.0)
            lse, lse_n = _lane_value(lv, sel), _lane_value(lnv, sel)
            cr, cr_n = _lane_value(crv, sel), _lane_value(crnv, sel)
            p_cc = jnp.exp(jnp.where(m_cc, _dot(qh, kc, tb=True) * 0.125 - cf * d_cur, NEG_INF) - lse)
            p_cp = jnp.exp(jnp.where(m_cp, _dot(qh, kpv, tb=True) * 0.125 - cf * d_prev, NEG_INF) - lse)
            p_nc = jnp.exp(jnp.where(m_nc, _dot(qnh, kc, tb=True) * 0.125 - cf * d_prev, NEG_INF) - lse_n)
            ds_cc = p_cc * (_dot(doh, vc, tb=True) + cr)
            ds_cp = p_cp * (_dot(doh, vpv, tb=True) + cr)
            ds_nc = p_nc * (_dot(donh, vc, tb=True) + cr_n)
            dqh = (_dot(ds_cc, kc) + _dot(ds_cp, kpv)) * 0.125
            dkh = (_dot(ds_cc, qh, ta=True) + _dot(ds_nc, qnh, ta=True)) * 0.125
            dvh = _dot(p_cc, doh, ta=True) + _dot(p_nc, donh, ta=True)
            dq_acc = jnp.where(sel, dqh, dq_acc)
            dk_acc = jnp.where(sel, dkh, dk_acc)
            dv_acc = jnp.where(sel, dvh, dv_acc)
        dq_ref[0] = dq_acc.astype(dq_ref.dtype)
        dk_ref[0] = dk_acc.astype(dk_ref.dtype)
        dv_ref[0] = dv_acc.astype(dv_ref.dtype)

    cur = pl.BlockSpec((1, B, 128), lambda g, hp, nb: (g, nb, hp))
    prev = pl.BlockSpec((1, B, 128), lambda g, hp, nb: (g, jnp.maximum(nb - 1, 0), hp))
    nxt = pl.BlockSpec((1, B, 128), lambda g, hp, nb: (g, jnp.minimum(nb + 1, N_BLOCKS - 1), hp))
    out = jax.ShapeDtypeStruct((N_GROUPS, SEQ, ATT_WIDTH), MXU_DTYPE)
    return pl.pallas_call(
        body, name=name, grid=(N_GROUPS, HEAD_PAIRS, N_BLOCKS),
        in_specs=[pl.BlockSpec(memory_space=pltpu.SMEM), cur, nxt, cur, prev, cur, prev, cur, nxt, cur, nxt, cur, nxt],
        out_specs=[cur, cur, cur], out_shape=[out, out, out],
        compiler_params=pltpu.CompilerParams(dimension_semantics=("parallel", "parallel", "arbitrary")),
    )(_alibi_coef(), qp, qp, kp, kp, vp, vp, lp, lp, dop, dop, corrp, corrp)


def _head_sum(x):
    i = lax.broadcasted_iota(jnp.int32, (128, 128), 0) // 64
    j = lax.broadcasted_iota(jnp.int32, (128, 128), 1) // 64
    return _dot_f32(x, (i == j).astype(F32))


def _group_weights(l0, l1, l2):
    mx = jnp.maximum(jnp.maximum(l0, l1), l2)
    e0, e1, e2 = jnp.exp(l0 - mx), jnp.exp(l1 - mx), jnp.exp(l2 - mx)
    inv = 1.0 / (e0 + e1 + e2)
    return e0 * inv, e1 * inv, e2 * inv


def att_combine_fwd(o, l, name):
    def body(o_ref, l_ref, y_ref):
        w0, w1, w2 = _group_weights(l_ref[0], l_ref[1], l_ref[2])
        y_ref[...] = (o_ref[0] * w0 + o_ref[1] * w1 + o_ref[2] * w2).astype(y_ref.dtype)

    blk3 = pl.BlockSpec((N_GROUPS, ROW_TILE, 128), lambda i, j: (0, i, j))
    return pl.pallas_call(
        body, name=name, grid=(SEQ // ROW_TILE, HEAD_PAIRS), in_specs=[blk3, blk3],
        out_specs=pl.BlockSpec((ROW_TILE, 128), lambda i, j: (i, j)),
        out_shape=jax.ShapeDtypeStruct((SEQ, ATT_WIDTH), MXU_DTYPE),
    )(o, l)


def att_combine_bwd(o, l, dy, name):
    def body(o_ref, l_ref, dy_ref, do_ref, cr_ref):
        w = _group_weights(l_ref[0], l_ref[1], l_ref[2])
        dyv = dy_ref[...]
        dw = [_head_sum(dyv * o_ref[g]) for g in range(N_GROUPS)]
        tot = w[0] * dw[0] + w[1] * dw[1] + w[2] * dw[2]
        for g in range(N_GROUPS):
            do_ref[g] = dyv * w[g]
            cr_ref[g] = -w[g] * tot

    blk3 = pl.BlockSpec((N_GROUPS, ROW_TILE, 128), lambda i, j: (0, i, j))
    out = jax.ShapeDtypeStruct((N_GROUPS, SEQ, ATT_WIDTH), F32)
    return pl.pallas_call(
        body, name=name, grid=(SEQ // ROW_TILE, HEAD_PAIRS),
        in_specs=[blk3, blk3, pl.BlockSpec((ROW_TILE, 128), lambda i, j: (i, j))],
        out_specs=[blk3, blk3], out_shape=[out, out],
    )(o, l, dy)


def _permute(a, d):
    if d == 1:
        return a
    return a.reshape(SEQ // d, d, a.shape[-1]).transpose(1, 0, 2).reshape(SEQ, a.shape[-1])


def _unpermute(a, d):
    if d == 1:
        return a
    return a.reshape(d, SEQ // d, a.shape[-1]).transpose(1, 0, 2).reshape(SEQ, a.shape[-1])


def _permute_groups(a):
    return jnp.stack([_permute(a[g], d) for g, (_, d) in enumerate(ATT_GROUPS)])


def _unpermute_groups(a):
    return jnp.stack([_unpermute(a[g], d) for g, (_, d) in enumerate(ATT_GROUPS)])


def add_n(parts, out_dtype, name):
    R, Cc = parts[0].shape
    tr = _pick(R, (256, 128, 64, 32, 16))

    def body(*refs):
        acc = refs[0][...].astype(F32)
        for r in refs[1:-1]:
            acc = acc + r[...].astype(F32)
        refs[-1][...] = acc.astype(out_dtype)

    blk = pl.BlockSpec((tr, Cc), lambda i: (i, 0))
    return pl.pallas_call(
        body, name=name, grid=(R // tr,), in_specs=[blk] * len(parts), out_specs=blk,
        out_shape=jax.ShapeDtypeStruct((R, Cc), out_dtype),
    )(*parts)


def adamw(w, g, m, v, name):
    R, Cc = w.shape
    tr = _pick(R, (256, 128, 64, 8))

    def body(w_ref, g_ref, m_ref, v_ref, d_ref, nm_ref, nv_ref):
        gv = g_ref[...]
        nm = ADAM_B1 * m_ref[...] + (1.0 - ADAM_B1) * gv
        nv = ADAM_B2 * v_ref[...] + (1.0 - ADAM_B2) * (gv * gv)
        m_hat = nm / (1.0 - ADAM_B1 ** ADAM_STEP)
        v_hat = nv / (1.0 - ADAM_B2 ** ADAM_STEP)
        d_ref[...] = -ADAM_LR * (m_hat / (jnp.sqrt(v_hat) + ADAM_EPS) + ADAM_WD * w_ref[...])
        nm_ref[...] = nm
        nv_ref[...] = nv

    blk = pl.BlockSpec((tr, Cc), lambda i: (i, 0))
    out = jax.ShapeDtypeStruct((R, Cc), F32)
    return pl.pallas_call(
        body, name=name, grid=(R // tr,), in_specs=[blk] * 4, out_specs=[blk] * 3, out_shape=[out, out, out],
    )(w, g, m, v)


BIG = (
    ("ffn1_w_gate_up", D_MODEL, 2 * D_FF, 1),
    ("ffn1_w_down", D_FF, D_MODEL, 0),
    ("w_in", D_MODEL, IN_COLS, 1),
    ("w_branch_hg", HG_WIDTH, D_MODEL, 1),
    ("w_branch_att", ATT_WIDTH, D_MODEL, 1),
    ("w_out", D_MODEL, D_MODEL, 0),
    ("ffn2_w_gate_up", D_MODEL, 2 * D_FF, 1),
    ("ffn2_w_down", D_FF, D_MODEL, 0),
)
N_BIG = len(BIG)
ANY = pl.BlockSpec(memory_space=pl.ANY)


def _place():
    return lax.axis_index("x"), lax.axis_index("y"), lax.axis_index("c")


def _other_chips(x, y):
    return ((1 - x, y), (x, 1 - y), (1 - x, 1 - y))


def _shard_shape(rows, cols, axis):
    return (rows // N_CHIPS, cols) if axis == 0 else (rows, cols // N_CHIPS)


MAX_COPY_CHUNKS = 16
CHUNK_ROW_ALIGN = 16


def _row_chunks(view):
    rows = view.shape[0]
    n = next(n for n in range(MAX_COPY_CHUNKS, 0, -1) if rows % (CHUNK_ROW_ALIGN * n) == 0 or n == 1)
    step = rows // n
    return [pl.ds(i * step, step) for i in range(n)]


def _start_local(src, dst, sem):
    for rows in _row_chunks(src):
        pltpu.make_async_copy(src.at[rows, :], dst.at[rows, :], sem).start()
    return pltpu.make_async_copy(src, dst, sem)


def _remote(src, dst, send_sem, recv_sem, device):
    return pltpu.make_async_remote_copy(src_ref=src, dst_ref=dst, send_sem=send_sem, recv_sem=recv_sem,
                                        device_id=device, device_id_type=MESH)


def _start_remote(src, dst, send_sem, recv_sem, device):
    for rows in _row_chunks(src):
        _remote(src.at[rows, :], dst.at[rows, :], send_sem, recv_sem, device).start()
    return _remote(src, dst, send_sem, recv_sem, device)


def all_gather_weights(shards):
    def half_of_shard(ref, rows, cols, axis, c):
        sr, _ = _shard_shape(rows, cols, axis)
        return ref.at[pl.ds(c * (sr // 2), sr // 2), :]

    def piece(ref, rows, cols, axis, chip, c):
        sr, sc = _shard_shape(rows, cols, axis)
        j = 2 * chip[0] + chip[1]
        if axis == 0:
            return ref.at[pl.ds(j * sr + c * (sr // 2), sr // 2), :]
        return ref.at[pl.ds(c * (sr // 2), sr // 2), pl.ds(pl.multiple_of(j * sc, 128), sc)]

    def whole_block(ref, rows, cols, axis, chip):
        sr, sc = _shard_shape(rows, cols, axis)
        j = 2 * chip[0] + chip[1]
        if axis == 0:
            return ref.at[pl.ds(j * sr, sr), :]
        return ref.at[:, pl.ds(pl.multiple_of(j * sc, 128), sc)]

    def body(*refs):
        ins, outs = refs[:N_BIG], refs[N_BIG:2 * N_BIG]
        send_sems, recv_sems, local_sems = refs[2 * N_BIG:]
        x, y, c = _place()
        chips = _other_chips(x, y)
        sends = []
        for w, (_, r, cc, ax) in enumerate(BIG):
            for k, chip in enumerate(chips):
                sends.append(_start_remote(half_of_shard(ins[w], r, cc, ax, c), piece(outs[w], r, cc, ax, (x, y), c),
                                           send_sems.at[w, k], recv_sems.at[w, k], (*chip, c)))
        local = [_start_local(ins[w], whole_block(outs[w], r, cc, ax, (x, y)), local_sems.at[w])
                 for w, (_, r, cc, ax) in enumerate(BIG)]
        passed = []
        for w, (_, r, cc, ax) in enumerate(BIG):
            for k, chip in enumerate(chips):
                got = piece(outs[w], r, cc, ax, chip, c)
                _remote(got, got, send_sems.at[w, k], recv_sems.at[w, k], (x, y, c)).wait_recv()
                passed.append(_start_remote(got, got, send_sems.at[w, 3 + k], recv_sems.at[w, 3 + k], (x, y, 1 - c)))
        for w, (_, r, cc, ax) in enumerate(BIG):
            for k, chip in enumerate(chips):
                got = piece(outs[w], r, cc, ax, chip, 1 - c)
                _remote(got, got, send_sems.at[w, 3 + k], recv_sems.at[w, 3 + k], (x, y, c)).wait_recv()
        for cp in sends + passed:
            cp.wait_send()
        for cp in local:
            cp.wait()

    return pl.pallas_call(
        body, name="all_gather_weights", in_specs=[ANY] * N_BIG, out_specs=[ANY] * N_BIG,
        out_shape=[jax.ShapeDtypeStruct((r, cc), WEIGHT_COMM_DTYPE) for _, r, cc, _ in BIG],
        scratch_shapes=[pltpu.SemaphoreType.DMA((N_BIG, 6)), pltpu.SemaphoreType.DMA((N_BIG, 6)),
                        pltpu.SemaphoreType.DMA((N_BIG,))],
    )(*shards)


def _half_shape(rows, cols, axis):
    return (rows, cols // 2) if axis == 0 else (rows // 2, cols)


def _half(ref, rows, cols, axis, c):
    if axis == 0:
        return ref.at[:, pl.ds(pl.multiple_of(c * (cols // 2), 128), cols // 2)]
    return ref.at[pl.ds(c * (rows // 2), rows // 2), :]


def _piece_shape(rows, cols, axis):
    return (rows // N_CHIPS, cols // 2) if axis == 0 else (rows // 2, cols // N_CHIPS)


def _piece_of_half(ref, rows, cols, axis, chip):
    j = 2 * chip[0] + chip[1]
    pr, pc = _piece_shape(rows, cols, axis)
    if axis == 0:
        return ref.at[pl.ds(j * pr, pr), :]
    return ref.at[:, pl.ds(pl.multiple_of(j * pc, 128), pc)]


def exchange_halves(grads):
    def body(*refs):
        ins, mine, theirs = refs[:N_BIG], refs[N_BIG:2 * N_BIG], refs[2 * N_BIG:3 * N_BIG]
        send_sems, recv_sems, local_sems = refs[3 * N_BIG:]
        x, y, c = _place()
        copies = []
        for w, (_, r, cc, ax) in enumerate(BIG):
            rc = _start_remote(_half(ins[w], r, cc, ax, 1 - c), theirs[w], send_sems.at[w], recv_sems.at[w], (x, y, 1 - c))
            lc = _start_local(_half(ins[w], r, cc, ax, c), mine[w], local_sems.at[w])
            copies.append((lc, rc))
        for lc, rc in copies:
            rc.wait()
            lc.wait()

    half = [jax.ShapeDtypeStruct(_half_shape(r, cc, ax), GRAD_COMM_DTYPE) for _, r, cc, ax in BIG]
    out = pl.pallas_call(
        body, name="exchange_halves", in_specs=[ANY] * N_BIG, out_specs=[ANY] * (2 * N_BIG), out_shape=half + half,
        scratch_shapes=[pltpu.SemaphoreType.DMA((N_BIG,)), pltpu.SemaphoreType.DMA((N_BIG,)),
                        pltpu.SemaphoreType.DMA((N_BIG,))],
    )(*grads)
    return out[:N_BIG], out[N_BIG:]


def scatter_pieces(halves):
    def body(*refs):
        ins, own, got = refs[:N_BIG], refs[N_BIG:2 * N_BIG], refs[2 * N_BIG:3 * N_BIG]
        send_sems, recv_sems, local_sems = refs[3 * N_BIG:]
        x, y, c = _place()
        chips = _other_chips(x, y)
        copies = []
        for w, (_, r, cc, ax) in enumerate(BIG):
            for k, chip in enumerate(chips):
                copies.append(_start_remote(_piece_of_half(ins[w], r, cc, ax, chip), got[w].at[k], send_sems.at[w, k],
                                            recv_sems.at[w, k], (*chip, c)))
            copies.append(_start_local(_piece_of_half(ins[w], r, cc, ax, (x, y)), own[w], local_sems.at[w]))
        for cp in copies:
            cp.wait()

    piece = [jax.ShapeDtypeStruct(_piece_shape(r, cc, ax), GRAD_COMM_DTYPE) for _, r, cc, ax in BIG]
    three = [jax.ShapeDtypeStruct((3,) + _piece_shape(r, cc, ax), GRAD_COMM_DTYPE) for _, r, cc, ax in BIG]
    out = pl.pallas_call(
        body, name="scatter_pieces", in_specs=[ANY] * N_BIG, out_specs=[ANY] * (2 * N_BIG), out_shape=piece + three,
        scratch_shapes=[pltpu.SemaphoreType.DMA((N_BIG, 3)), pltpu.SemaphoreType.DMA((N_BIG, 3)),
                        pltpu.SemaphoreType.DMA((N_BIG,))],
    )(*halves)
    return out[:N_BIG], out[N_BIG:]


def join_halves(pieces):
    def where(ref, rows, cols, axis, c):
        pr, pc = _piece_shape(rows, cols, axis)
        if axis == 0:
            return ref.at[:, pl.ds(pl.multiple_of(c * pc, 128), pc)]
        return ref.at[pl.ds(c * pr, pr), :]

    def body(*refs):
        ins, outs = refs[:N_BIG], refs[N_BIG:2 * N_BIG]
        send_sems, recv_sems, local_sems = refs[2 * N_BIG:]
        x, y, c = _place()
        copies = []
        for w, (_, r, cc, ax) in enumerate(BIG):
            rc = _start_remote(ins[w], where(outs[w], r, cc, ax, c), send_sems.at[w], recv_sems.at[w], (x, y, 1 - c))
            lc = _start_local(ins[w], where(outs[w], r, cc, ax, c), local_sems.at[w])
            copies.append((lc, rc))
        for lc, rc in copies:
            rc.wait()
            lc.wait()

    return pl.pallas_call(
        body, name="join_halves", in_specs=[ANY] * N_BIG, out_specs=[ANY] * N_BIG,
        out_shape=[jax.ShapeDtypeStruct(_shard_shape(r, cc, ax), F32) for _, r, cc, ax in BIG],
        scratch_shapes=[pltpu.SemaphoreType.DMA((N_BIG,)), pltpu.SemaphoreType.DMA((N_BIG,)),
                        pltpu.SemaphoreType.DMA((N_BIG,))],
    )(*pieces)


N_DEV = 8
SMALL_ROWS = 8


def all_reduce_small(packed):
    def body(x_ref, o_ref, gathered, send_sems, recv_sems):
        x, y, c = _place()
        me = 4 * x + 2 * y + c
        gathered[me] = x_ref[...]
        copies = []
        for k in range(1, N_DEV):
            peer = (x ^ (k >> 2), y ^ ((k >> 1) & 1), c ^ (k & 1))
            cp = pltpu.make_async_remote_copy(
                src_ref=x_ref, dst_ref=gathered.at[me], send_sem=send_sems.at[k - 1], recv_sem=recv_sems.at[k - 1],
                device_id=peer, device_id_type=MESH)
            cp.start()
            copies.append(cp)
        for cp in copies:
            cp.wait()
        acc = gathered[0]
        for k in range(1, N_DEV):
            acc = acc + gathered[k]
        o_ref[...] = acc

    vm = pl.BlockSpec(memory_space=pltpu.VMEM)
    return pl.pallas_call(
        body, name="all_reduce_small", in_specs=[vm], out_specs=vm,
        out_shape=jax.ShapeDtypeStruct((SMALL_ROWS, D_MODEL), F32),
        scratch_shapes=[pltpu.VMEM((N_DEV, SMALL_ROWS, D_MODEL), F32), pltpu.SemaphoreType.DMA((N_DEV - 1,)),
                        pltpu.SemaphoreType.DMA((N_DEV - 1,))],
    )(packed)


def _swiglu_block_fwd(h, norm_g, w_gu, w_down, tag):
    n = rmsnorm_fwd(h, norm_g, f"{tag}_norm")
    gu = matmul(n, w_gu, name=f"{tag}_gate_up")
    s = swiglu_fwd(gu, f"{tag}_swiglu")
    h_out = matmul(s, w_down, res=h, scale=0.5, name=f"{tag}_down")
    return h_out, (n, gu, s)


def _swiglu_block_bwd(h, norm_g, w_gu, w_down, saved, dh_out, tag):
    n, gu, s = saved
    df = dh_out.astype(MXU_DTYPE)
    d_down = matmul(s, df, ta=True, scale=0.5, out_dtype=GRAD_COMM_DTYPE, name=f"{tag}_d_w_down")
    ds = matmul(df, w_down, tb=True, scale=0.5, name=f"{tag}_d_s")
    dgu = swiglu_bwd(gu, ds, f"{tag}_swiglu_bwd")
    d_gu = matmul(n, dgu, ta=True, out_dtype=GRAD_COMM_DTYPE, name=f"{tag}_d_w_gate_up")
    dn = matmul(dgu, w_gu, tb=True, name=f"{tag}_d_n")
    dh, dg = rmsnorm_bwd(h, norm_g, dn, dh_out, f"{tag}_norm_bwd")
    return dh, dg, d_gu, d_down


ATT_COL0 = 4 * HG_WIDTH


def local_step(x, target, small, big):
    h1, saved1 = _swiglu_block_fwd(x, small["ffn1_norm"], big["ffn1_w_gate_up"], big["ffn1_w_down"], "ffn1")
    u = rmsnorm_fwd(h1, small["mix_norm"], "mix_norm")
    z = matmul(u, big["w_in"], name="w_in")
    p = small["hg_lower_bounds"]
    lb = 1.0 / (1.0 + jnp.exp(p[1:2] - p[0:1]))
    y_hg, o_raw, states = hgrn_fwd(z, lb, small["hg_out_norm"], "hgrn_fwd")
    att = z[:, ATT_COL0:ATT_COL0 + 3 * N_GROUPS * ATT_WIDTH].reshape(SEQ, N_GROUPS, 3, ATT_WIDTH)
    qp = _permute_groups(att[:, :, 0].transpose(1, 0, 2))
    kp = _permute_groups(att[:, :, 1].transpose(1, 0, 2))
    vp = _permute_groups(att[:, :, 2].transpose(1, 0, 2))
    op, lp = att_fwd(qp, kp, vp, "att_fwd")
    o_att, l_att = _unpermute_groups(op), _unpermute_groups(lp)
    y_att = att_combine_fwd(o_att, l_att, "att_combine")
    bh = matmul(y_hg, big["w_branch_hg"], name="branch_hg")
    ba = matmul(y_att, big["w_branch_att"], name="branch_att")
    merged = merge_fwd(z, bh, ba, "merge")
    h2 = matmul(merged, big["w_out"], res=h1, name="w_out")
    h3, saved2 = _swiglu_block_fwd(h2, small["ffn2_norm"], big["ffn2_w_gate_up"], big["ffn2_w_down"], "ffn2")
    dh3, d_final, loss = final_norm_loss(h3, small["final_norm"], target, "final_norm_loss")

    gs, gb = {"final_norm": d_final}, {}
    dh2, gs["ffn2_norm"], gb["ffn2_w_gate_up"], gb["ffn2_w_down"] = _swiglu_block_bwd(
        h2, small["ffn2_norm"], big["ffn2_w_gate_up"], big["ffn2_w_down"], saved2, dh3, "ffn2")
    dh2_m = dh2.astype(MXU_DTYPE)
    gb["w_out"] = matmul(merged, dh2_m, ta=True, out_dtype=GRAD_COMM_DTYPE, name="d_w_out")
    dmerged = matmul(dh2_m, big["w_out"], tb=True, name="d_merged")
    dbh, dba, dgh, dga = merge_bwd(z, bh, ba, dmerged, "merge_bwd")
    gb["w_branch_hg"] = matmul(y_hg, dbh, ta=True, out_dtype=GRAD_COMM_DTYPE, name="d_w_branch_hg")
    gb["w_branch_att"] = matmul(y_att, dba, ta=True, out_dtype=GRAD_COMM_DTYPE, name="d_w_branch_att")
    dy_hg = matmul(dbh, big["w_branch_hg"], tb=True, name="d_y_hg")
    dy_att = matmul(dba, big["w_branch_att"], tb=True, name="d_y_att")
    dq, dfp, di, dog, d_lb, gs["hg_out_norm"] = hgrn_bwd(z, lb, small["hg_out_norm"], o_raw, states, dy_hg, "hgrn_bwd")
    do_att, corr = att_combine_bwd(o_att, l_att, dy_att, "att_combine_bwd")
    dqp, dkp, dvp = att_bwd(qp, kp, vp, lp, _permute_groups(do_att), _permute_groups(corr), "att_bwd")
    dqa, dka, dva = _unpermute_groups(dqp), _unpermute_groups(dkp), _unpermute_groups(dvp)
    d_att = jnp.stack([dqa, dka, dva], axis=1).transpose(2, 0, 1, 3).reshape(SEQ, 3 * N_GROUPS * ATT_WIDTH)
    dz = jnp.concatenate([dq, dfp, di, dog, d_att, dgh, dga], axis=1)
    gb["w_in"] = matmul(u, dz, ta=True, out_dtype=GRAD_COMM_DTYPE, name="d_w_in")
    du = matmul(dz, big["w_in"], tb=True, name="d_u")
    dh1, gs["mix_norm"] = rmsnorm_bwd(h1, small["mix_norm"], du, dh2, "mix_norm_bwd")
    dp0 = d_lb * lb * (1.0 - lb)
    gs["hg_lower_bounds"] = jnp.concatenate([dp0, -dp0], axis=0)
    dx, gs["ffn1_norm"], gb["ffn1_w_gate_up"], gb["ffn1_w_down"] = _swiglu_block_bwd(
        x, small["ffn1_norm"], big["ffn1_w_gate_up"], big["ffn1_w_down"], saved1, dh1, "ffn1")
    return loss[0, 0], dx, gs, gb


SMALL = ("ffn1_norm", "mix_norm", "hg_lower_bounds", "hg_out_norm", "ffn2_norm", "final_norm")
WEIGHTS = ("ffn1_norm", "ffn1_w_gate_up", "ffn1_w_down", "mix_norm", "w_in", "hg_lower_bounds", "hg_out_norm",
           "w_branch_hg", "w_branch_att", "w_out", "ffn2_norm", "ffn2_w_gate_up", "ffn2_w_down", "final_norm")
SMALL_SHAPE = {"ffn1_norm": (1, 1024), "mix_norm": (1, 1024), "hg_lower_bounds": (2, 512), "hg_out_norm": (1, 512),
               "ffn2_norm": (1, 1024), "final_norm": (1024,)}
LOSS_ROW = 6


def _pack_small(vals):
    rows = []
    for n in SMALL:
        r = vals[n].reshape(1, -1).astype(F32)
        rows.append(jnp.pad(r, ((0, 0), (0, D_MODEL - r.shape[1]))))
    rows.append(jnp.zeros((SMALL_ROWS - len(SMALL), D_MODEL), F32))
    return jnp.concatenate(rows, axis=0)


def _unpack_small(packed):
    out = {}
    for i, n in enumerate(SMALL):
        size = int(np.prod(SMALL_SHAPE[n]))
        out[n] = packed[i, :size].reshape(SMALL_SHAPE[n])
    return out


def reduce_big(grads):
    mine, theirs = exchange_halves([grads[n] for n, *_ in BIG])
    halves = [add_n([a, b], GRAD_COMM_DTYPE, f"add_halves_{n}") for (n, *_), a, b in zip(BIG, mine, theirs)]
    own, got = scatter_pieces(halves)
    pieces = [add_n([a, b[0], b[1], b[2]], F32, f"add_pieces_{n}") for (n, *_), a, b in zip(BIG, own, got)]
    blocks = join_halves(pieces)
    return {n: b for (n, *_), b in zip(BIG, blocks)}


def kernel(x, ffn1_norm, ffn1_w_gate_up, ffn1_w_down, mix_norm, w_in, hg_lower_bounds, hg_out_norm, w_branch_hg, w_branch_att, w_out, ffn2_norm, ffn2_w_gate_up, ffn2_w_down, final_norm, loss_target, m_ffn1_norm, m_ffn1_w_gate_up, m_ffn1_w_down, m_mix_norm, m_w_in, m_hg_lower_bounds, m_hg_out_norm, m_w_branch_hg, m_w_branch_att, m_w_out, m_ffn2_norm, m_ffn2_w_gate_up, m_ffn2_w_down, m_final_norm, v_ffn1_norm, v_ffn1_w_gate_up, v_ffn1_w_down, v_mix_norm, v_w_in, v_hg_lower_bounds, v_hg_out_norm, v_w_branch_hg, v_w_branch_att, v_w_out, v_ffn2_norm, v_ffn2_w_gate_up, v_ffn2_w_down, v_final_norm):
    w = dict(ffn1_norm=ffn1_norm, ffn1_w_gate_up=ffn1_w_gate_up, ffn1_w_down=ffn1_w_down, mix_norm=mix_norm, w_in=w_in,
             hg_lower_bounds=hg_lower_bounds, hg_out_norm=hg_out_norm, w_branch_hg=w_branch_hg, w_branch_att=w_branch_att,
             w_out=w_out, ffn2_norm=ffn2_norm, ffn2_w_gate_up=ffn2_w_gate_up, ffn2_w_down=ffn2_w_down, final_norm=final_norm)
    m = dict(ffn1_norm=m_ffn1_norm, ffn1_w_gate_up=m_ffn1_w_gate_up, ffn1_w_down=m_ffn1_w_down, mix_norm=m_mix_norm,
             w_in=m_w_in, hg_lower_bounds=m_hg_lower_bounds, hg_out_norm=m_hg_out_norm, w_branch_hg=m_w_branch_hg,
             w_branch_att=m_w_branch_att, w_out=m_w_out, ffn2_norm=m_ffn2_norm, ffn2_w_gate_up=m_ffn2_w_gate_up,
             ffn2_w_down=m_ffn2_w_down, final_norm=m_final_norm)
    v = dict(ffn1_norm=v_ffn1_norm, ffn1_w_gate_up=v_ffn1_w_gate_up, ffn1_w_down=v_ffn1_w_down, mix_norm=v_mix_norm,
             w_in=v_w_in, hg_lower_bounds=v_hg_lower_bounds, hg_out_norm=v_hg_out_norm, w_branch_hg=v_w_branch_hg,
             w_branch_att=v_w_branch_att, w_out=v_w_out, ffn2_norm=v_ffn2_norm, ffn2_w_gate_up=v_ffn2_w_gate_up,
             ffn2_w_down=v_ffn2_w_down, final_norm=v_final_norm)

    gathered = all_gather_weights([w[n][0].astype(WEIGHT_COMM_DTYPE) for n, *_ in BIG])
    big = {n: a for (n, *_), a in zip(BIG, gathered)}
    small = {n: w[n] for n in SMALL}
    small["final_norm"] = final_norm.reshape(1, D_MODEL)

    loss, dx, gs, gb = local_step(x[0], loss_target[0], small, big)

    packed = _pack_small(gs)
    packed = packed.at[LOSS_ROW].set(jnp.full((D_MODEL,), loss, F32))
    total = all_reduce_small(packed)
    grads = _unpack_small(total)
    loss_total = total[LOSS_ROW, 0]
    blocks = reduce_big(gb)
    for n, *_ in BIG:
        grads[n] = blocks[n][None]

    delta, new_m, new_v = {}, {}, {}
    pd, pm, pv = adamw(_pack_small({n: w[n] for n in SMALL}), total.at[LOSS_ROW].set(0.0),
                       _pack_small({n: m[n] for n in SMALL}), _pack_small({n: v[n] for n in SMALL}), "adamw_small")
    delta.update(_unpack_small(pd))
    new_m.update(_unpack_small(pm))
    new_v.update(_unpack_small(pv))
    for n, *_ in BIG:
        d, nm, nv = adamw(w[n][0], blocks[n], m[n][0], v[n][0], f"adamw_{n}")
        delta[n], new_m[n], new_v[n] = d[None], nm[None], nv[None]

    return (loss_total, dx[None], *[grads[n] for n in WEIGHTS], *[delta[n] for n in WEIGHTS],
            *[new_m[n] for n in WEIGHTS], *[new_v[n] for n in WEIGHTS])
```

```python
import numpy as np
import jax
import jax.numpy as jnp
from jax import lax
from jax.experimental import pallas as pl
from jax.experimental.pallas import tpu as pltpu

SEQ = 2048
D_MODEL = 1024
D_FF = 2816
HG_HEADS = 4
HG_DIM = 128
HG_WIDTH = 512
HG_CHUNK = 64
ATT_GROUPS = ((128, 1), (512, 4), (2048, 16))
ATT_HEADS = 8
ATT_WIDTH = 512
ATT_BLOCK = 128
ALIBI_MAX = 8.0
IN_COLS = 8704
EPS = 1e-6
NEG_INF = -1e30
ADAM_LR = 0.001
ADAM_B1 = 0.9
ADAM_B2 = 0.999
ADAM_EPS = 1e-08
ADAM_WD = 0.01
ADAM_STEP = 10

N_CHIPS = 4
MXU_DTYPE = jnp.bfloat16
HG_DOT_DTYPE = jnp.float32
WEIGHT_COMM_DTYPE = jnp.bfloat16
GRAD_COMM_DTYPE = jnp.bfloat16
MESH = pl.DeviceIdType.MESH
F32 = jnp.float32
HIGHEST = lax.Precision.HIGHEST


def _pick(n, cands):
    for c in cands:
        if n % c == 0:
            return c
    return n


def _sigmoid(x):
    return 1.0 / (1.0 + jnp.exp(-x))


def _dot(a, b, ta=False, tb=False):
    dn = (((0 if ta else 1,), (1 if tb else 0,)), ((), ()))
    return lax.dot_general(a.astype(MXU_DTYPE), b.astype(MXU_DTYPE), dn, preferred_element_type=F32)


def _dot_f32(a, b):
    return jnp.dot(a, b, precision=HIGHEST, preferred_element_type=F32)


def _hdot(a, b, ta=False, tb=False):
    if HG_DOT_DTYPE == F32:
        dn = (((0 if ta else 1,), (1 if tb else 0,)), ((), ()))
        return lax.dot_general(a, b, dn, precision=HIGHEST, preferred_element_type=F32)
    return _dot(a, b, ta, tb)


MATMUL_VMEM_BYTES = 40 * 1024 * 1024


def matmul(a, b, *, ta=False, tb=False, out_dtype=F32, res=None, scale=1.0, name):
    if ta:
        K, M = a.shape
    else:
        M, K = a.shape
    if tb:
        N, K2 = b.shape
    else:
        K2, N = b.shape
    assert K == K2
    tm = _pick(M, (1024, 512, 256, 128))
    tn = _pick(N, (512, 256, 128))
    tk = _pick(K, (512, 256, 128))
    nk = K // tk

    def body(*refs):
        if res is None:
            a_ref, b_ref, o_ref, acc = refs
        else:
            a_ref, b_ref, r_ref, o_ref, acc = refs
        k = pl.program_id(2)

        @pl.when(k == 0)
        def _():
            acc[...] = jnp.zeros_like(acc)

        acc[...] += _dot(a_ref[...], b_ref[...], ta, tb)

        @pl.when(k == nk - 1)
        def _():
            r = acc[...]
            if scale != 1.0:
                r = r * scale
            if res is not None:
                r = r_ref[...] + r
            o_ref[...] = r.astype(out_dtype)

    a_spec = pl.BlockSpec((tk, tm), lambda i, j, k: (k, i)) if ta else pl.BlockSpec((tm, tk), lambda i, j, k: (i, k))
    b_spec = pl.BlockSpec((tn, tk), lambda i, j, k: (j, k)) if tb else pl.BlockSpec((tk, tn), lambda i, j, k: (k, j))
    in_specs = [a_spec, b_spec]
    args = [a, b]
    if res is not None:
        in_specs.append(pl.BlockSpec((tm, tn), lambda i, j, k: (i, j)))
        args.append(res)
    return pl.pallas_call(
        body, name=name, grid=(M // tm, N // tn, nk), in_specs=in_specs,
        out_specs=pl.BlockSpec((tm, tn), lambda i, j, k: (i, j)),
        out_shape=jax.ShapeDtypeStruct((M, N), out_dtype),
        scratch_shapes=[pltpu.VMEM((tm, tn), F32)],
        compiler_params=pltpu.CompilerParams(dimension_semantics=("parallel", "parallel", "arbitrary"),
                                             vmem_limit_bytes=MATMUL_VMEM_BYTES),
    )(*args)


ROW_TILE = 256


def rmsnorm_fwd(x, g, name):
    def body(x_ref, g_ref, n_ref):
        xv = x_ref[...]
        r = lax.rsqrt(jnp.mean(xv * xv, axis=-1, keepdims=True) + EPS)
        n_ref[...] = ((xv * r) * g_ref[...]).astype(n_ref.dtype)

    return pl.pallas_call(
        body, name=name, grid=(SEQ // ROW_TILE,),
        in_specs=[pl.BlockSpec((ROW_TILE, D_MODEL), lambda i: (i, 0)), pl.BlockSpec((1, D_MODEL), lambda i: (0, 0))],
        out_specs=pl.BlockSpec((ROW_TILE, D_MODEL), lambda i: (i, 0)),
        out_shape=jax.ShapeDtypeStruct((SEQ, D_MODEL), MXU_DTYPE),
    )(x, g)


def rmsnorm_bwd(x, g, dn, dres, name):
    def body(x_ref, g_ref, dn_ref, dr_ref, dx_ref, dg_ref):
        xv = x_ref[...]
        r = lax.rsqrt(jnp.mean(xv * xv, axis=-1, keepdims=True) + EPS)
        xh = xv * r
        dnv = dn_ref[...]

        @pl.when(pl.program_id(0) == 0)
        def _():
            dg_ref[...] = jnp.zeros_like(dg_ref)

        dg_ref[...] += jnp.sum(dnv * xh, axis=0, keepdims=True)
        dxh = dnv * g_ref[...]
        dx_ref[...] = dr_ref[...] + r * (dxh - xh * jnp.mean(dxh * xh, axis=-1, keepdims=True))

    row = pl.BlockSpec((ROW_TILE, D_MODEL), lambda i: (i, 0))
    vec = pl.BlockSpec((1, D_MODEL), lambda i: (0, 0))
    return pl.pallas_call(
        body, name=name, grid=(SEQ // ROW_TILE,), in_specs=[row, vec, row, row], out_specs=[row, vec],
        out_shape=[jax.ShapeDtypeStruct((SEQ, D_MODEL), F32), jax.ShapeDtypeStruct((1, D_MODEL), F32)],
        compiler_params=pltpu.CompilerParams(dimension_semantics=("arbitrary",)),
    )(x, g, dn, dres)


def final_norm_loss(h, g, target, name):
    def body(h_ref, g_ref, t_ref, dh_ref, dg_ref, loss_ref):
        xv = h_ref[...]
        r = lax.rsqrt(jnp.mean(xv * xv, axis=-1, keepdims=True) + EPS)
        xh = xv * r
        gv = g_ref[...]
        e = xh * gv - t_ref[...]

        @pl.when(pl.program_id(0) == 0)
        def _():
            dg_ref[...] = jnp.zeros_like(dg_ref)
            loss_ref[...] = jnp.zeros_like(loss_ref)

        part = 0.5 * jnp.sum(jnp.sum(e * e, axis=-1, keepdims=True) * (1.0 / D_MODEL), axis=0, keepdims=True)
        loss_ref[...] += jnp.broadcast_to(part, loss_ref.shape)
        dout = e * (1.0 / D_MODEL)
        dg_ref[...] += jnp.sum(dout * xh, axis=0, keepdims=True)
        dxh = dout * gv
        dh_ref[...] = r * (dxh - xh * jnp.mean(dxh * xh, axis=-1, keepdims=True))

    row = pl.BlockSpec((ROW_TILE, D_MODEL), lambda i: (i, 0))
    vec = pl.BlockSpec((1, D_MODEL), lambda i: (0, 0))
    return pl.pallas_call(
        body, name=name, grid=(SEQ // ROW_TILE,), in_specs=[row, vec, row],
        out_specs=[row, vec, pl.BlockSpec((8, 128), lambda i: (0, 0))],
        out_shape=[jax.ShapeDtypeStruct((SEQ, D_MODEL), F32), jax.ShapeDtypeStruct((1, D_MODEL), F32),
                   jax.ShapeDtypeStruct((8, 128), F32)],
        compiler_params=pltpu.CompilerParams(dimension_semantics=("arbitrary",)),
    )(h, g, target)


FF_TILE = D_FF // 2


def swiglu_fwd(gu, name):
    def body(a_ref, b_ref, s_ref):
        a = a_ref[...]
        s_ref[...] = (a * _sigmoid(a) * b_ref[...]).astype(s_ref.dtype)

    return pl.pallas_call(
        body, name=name, grid=(SEQ // ROW_TILE, 2),
        in_specs=[pl.BlockSpec((ROW_TILE, FF_TILE), lambda i, j: (i, j)),
                  pl.BlockSpec((ROW_TILE, FF_TILE), lambda i, j: (i, j + 2))],
        out_specs=pl.BlockSpec((ROW_TILE, FF_TILE), lambda i, j: (i, j)),
        out_shape=jax.ShapeDtypeStruct((SEQ, D_FF), MXU_DTYPE),
    )(gu, gu)


def swiglu_bwd(gu, ds, name):
    def body(a_ref, b_ref, ds_ref, o_ref):
        a = a_ref[...]
        sg = _sigmoid(a)
        dsv = ds_ref[...]

        @pl.when(pl.program_id(1) < 2)
        def _():
            o_ref[...] = (dsv * b_ref[...] * (sg * (1.0 + a * (1.0 - sg)))).astype(o_ref.dtype)

        @pl.when(pl.program_id(1) >= 2)
        def _():
            o_ref[...] = (dsv * a * sg).astype(o_ref.dtype)

    return pl.pallas_call(
        body, name=name, grid=(SEQ // ROW_TILE, 4),
        in_specs=[pl.BlockSpec((ROW_TILE, FF_TILE), lambda i, j: (i, j % 2)),
                  pl.BlockSpec((ROW_TILE, FF_TILE), lambda i, j: (i, j % 2 + 2)),
                  pl.BlockSpec((ROW_TILE, FF_TILE), lambda i, j: (i, j % 2))],
        out_specs=pl.BlockSpec((ROW_TILE, FF_TILE), lambda i, j: (i, j)),
        out_shape=jax.ShapeDtypeStruct((SEQ, 2 * D_FF), MXU_DTYPE),
    )(gu, gu, ds)


GATE_HG_BLK = 6656 // 512
GATE_ATT_BLK = 7680 // 512


def merge_fwd(z, bh, ba, name):
    def body(gh_ref, ga_ref, bh_ref, ba_ref, o_ref):
        o_ref[...] = (_sigmoid(gh_ref[...]) * bh_ref[...] + _sigmoid(ga_ref[...]) * ba_ref[...]).astype(o_ref.dtype)

    blk = pl.BlockSpec((ROW_TILE, 512), lambda i, j: (i, j))
    return pl.pallas_call(
        body, name=name, grid=(SEQ // ROW_TILE, 2),
        in_specs=[pl.BlockSpec((ROW_TILE, 512), lambda i, j: (i, GATE_HG_BLK + j)),
                  pl.BlockSpec((ROW_TILE, 512), lambda i, j: (i, GATE_ATT_BLK + j)), blk, blk],
        out_specs=blk, out_shape=jax.ShapeDtypeStruct((SEQ, D_MODEL), MXU_DTYPE),
    )(z, z, bh, ba)


def merge_bwd(z, bh, ba, dm, name):
    def body(gh_ref, ga_ref, bh_ref, ba_ref, dm_ref, dbh_ref, dba_ref, dgh_ref, dga_ref):
        dmv = dm_ref[...]
        sh = _sigmoid(gh_ref[...])
        sa = _sigmoid(ga_ref[...])
        dbh_ref[...] = (dmv * sh).astype(dbh_ref.dtype)
        dba_ref[...] = (dmv * sa).astype(dba_ref.dtype)
        dgh_ref[...] = (dmv * bh_ref[...] * (sh * (1.0 - sh))).astype(dgh_ref.dtype)
        dga_ref[...] = (dmv * ba_ref[...] * (sa * (1.0 - sa))).astype(dga_ref.dtype)

    blk = pl.BlockSpec((ROW_TILE, 512), lambda i, j: (i, j))
    out = jax.ShapeDtypeStruct((SEQ, D_MODEL), MXU_DTYPE)
    return pl.pallas_call(
        body, name=name, grid=(SEQ // ROW_TILE, 2),
        in_specs=[pl.BlockSpec((ROW_TILE, 512), lambda i, j: (i, GATE_HG_BLK + j)),
                  pl.BlockSpec((ROW_TILE, 512), lambda i, j: (i, GATE_ATT_BLK + j)), blk, blk, blk],
        out_specs=[blk, blk, blk, blk], out_shape=[out, out, out, out],
    )(z, z, bh, ba, dm)


N_CHUNKS = SEQ // HG_CHUNK


def _hgrn_gates(q, fp, lb):
    C = HG_CHUNK
    sg = _sigmoid(fp)
    f = lb + (1.0 - lb) * sg
    lf = jnp.log(f)
    row = lax.broadcasted_iota(jnp.int32, (C, C), 0)
    col = lax.broadcasted_iota(jnp.int32, (C, C), 1)
    causal = row >= col
    G = _dot_f32(causal.astype(F32), lf)
    eG = jnp.exp(G)
    enG = jnp.exp(-G)
    qg = q * eG
    kg = (1.0 - f) * enG
    A = jnp.where(causal, _hdot(qg, kg, tb=True), 0.0)
    egl = jnp.exp(jnp.sum(lf, axis=0, keepdims=True))
    return sg, f, causal, eG, enG, qg, kg, A, egl


def hgrn_fwd(z, lb, gain, name):
    C, K = HG_CHUNK, HG_DIM

    def body(q_ref, f_ref, v_ref, og_ref, p_ref, g_ref, y_ref, o_ref, st_ref, state):
        @pl.when(pl.program_id(1) == 0)
        def _():
            state[...] = jnp.zeros_like(state)

        v = v_ref[...]
        _, _, _, _, _, qg, kg, A, egl = _hgrn_gates(q_ref[...], f_ref[...], p_ref[...])
        st = state[...]
        st_ref[0, 0] = st
        o = _hdot(A, v) + _hdot(qg, st, tb=True)
        state[...] = st * egl + _hdot(v, kg * egl, ta=True)
        o_ref[...] = o
        rs = lax.rsqrt(jnp.mean(o * o, axis=-1, keepdims=True) + EPS)
        og = og_ref[...]
        y_ref[...] = (((o * rs) * g_ref[...]) * (og * _sigmoid(og))).astype(y_ref.dtype)

    def zcol(section):
        return pl.BlockSpec((C, K), lambda h, c: (c, section * HG_HEADS + h))

    vec = pl.BlockSpec((1, K), lambda h, c: (0, h))
    blk = pl.BlockSpec((C, K), lambda h, c: (c, h))
    return pl.pallas_call(
        body, name=name, grid=(HG_HEADS, N_CHUNKS),
        in_specs=[zcol(0), zcol(1), zcol(2), zcol(3), vec, vec],
        out_specs=[blk, blk, pl.BlockSpec((1, 1, K, K), lambda h, c: (h, c, 0, 0))],
        out_shape=[jax.ShapeDtypeStruct((SEQ, HG_WIDTH), MXU_DTYPE), jax.ShapeDtypeStruct((SEQ, HG_WIDTH), F32),
                   jax.ShapeDtypeStruct((HG_HEADS, N_CHUNKS, K, K), F32)],
        scratch_shapes=[pltpu.VMEM((K, K), F32)],
        compiler_params=pltpu.CompilerParams(dimension_semantics=("parallel", "arbitrary")),
    )(z, z, z, z, lb, gain)


def hgrn_bwd(z, lb, gain, o_raw, states, dy, name):
    C, K = HG_CHUNK, HG_DIM

    def body(q_ref, f_ref, v_ref, og_ref, p_ref, g_ref, o_ref, st_ref, dy_ref,
             dq_ref, dfp_ref, dv_ref, dog_ref, dlb_ref, dgain_ref, dstate):
        @pl.when(pl.program_id(1) == 0)
        def _():
            dstate[...] = jnp.zeros_like(dstate)
            dlb_ref[...] = jnp.zeros_like(dlb_ref)
            dgain_ref[...] = jnp.zeros_like(dgain_ref)

        v = v_ref[...]
        lb = p_ref[...]
        sg, f, causal, eG, enG, qg, kg, A, egl = _hgrn_gates(q_ref[...], f_ref[...], lb)
        kd = kg * egl
        st = st_ref[0, 0]
        dst = dstate[...]
        o = o_ref[...]
        og = og_ref[...]
        gain_v = g_ref[...]
        dyv = dy_ref[...]
        rs = lax.rsqrt(jnp.mean(o * o, axis=-1, keepdims=True) + EPS)
        on = o * rs
        sgo = _sigmoid(og)
        silu = og * sgo
        dog_ref[...] = (dyv * (on * gain_v) * (sgo * (1.0 + og * (1.0 - sgo)))).astype(dog_ref.dtype)
        dgain_ref[...] += jnp.sum(dyv * silu * on, axis=0, keepdims=True)
        don = dyv * gain_v * silu
        do = rs * (don - on * jnp.mean(don * on, axis=-1, keepdims=True))
        dA = jnp.where(causal, _hdot(do, v, tb=True), 0.0)
        dv_ref[...] = (_hdot(A, do, ta=True) + _hdot(kd, dst, tb=True)).astype(dv_ref.dtype)
        dqg = _hdot(dA, kg) + _hdot(do, st)
        dkg = _hdot(dA, qg, ta=True)
        dkd = _hdot(v, dst)
        dstate[...] = dst * egl + _hdot(do, qg, ta=True)
        dgl = jnp.sum(st * dst, axis=0, keepdims=True) * egl
        dq_ref[...] = (dqg * eG).astype(dq_ref.dtype)
        dk = dkg * enG + dkd * (enG * egl)
        dG = dqg * qg - dkg * kg - dkd * kd
        extra = jnp.sum(dkd * kd, axis=0, keepdims=True) + dgl
        last = lax.broadcasted_iota(jnp.int32, (C, K), 0) == C - 1
        dG = dG + jnp.where(last, extra, 0.0)
        row = lax.broadcasted_iota(jnp.int32, (C, C), 0)
        col = lax.broadcasted_iota(jnp.int32, (C, C), 1)
        dlf = _dot_f32((col >= row).astype(F32), dG)
        df = dlf / f - dk
        dfp_ref[...] = (df * (1.0 - lb) * (sg * (1.0 - sg))).astype(dfp_ref.dtype)
        dlb_ref[...] += jnp.sum(df * (1.0 - sg), axis=0, keepdims=True)

    def rc(c):
        return N_CHUNKS - 1 - c

    def zcol(section):
        return pl.BlockSpec((C, K), lambda h, c: (rc(c), section * HG_HEADS + h))

    vec = pl.BlockSpec((1, K), lambda h, c: (0, h))
    blk = pl.BlockSpec((C, K), lambda h, c: (rc(c), h))
    out = jax.ShapeDtypeStruct((SEQ, HG_WIDTH), MXU_DTYPE)
    small = jax.ShapeDtypeStruct((1, HG_WIDTH), F32)
    return pl.pallas_call(
        body, name=name, grid=(HG_HEADS, N_CHUNKS),
        in_specs=[zcol(0), zcol(1), zcol(2), zcol(3), vec, vec, blk,
                  pl.BlockSpec((1, 1, K, K), lambda h, c: (h, rc(c), 0, 0)), blk],
        out_specs=[blk, blk, blk, blk, vec, vec],
        out_shape=[out, out, out, out, small, small],
        scratch_shapes=[pltpu.VMEM((K, K), F32)],
        compiler_params=pltpu.CompilerParams(dimension_semantics=("parallel", "arbitrary")),
    )(z, z, z, z, lb, gain, o_raw, states, dy)


N_GROUPS = len(ATT_GROUPS)
N_BLOCKS = SEQ // ATT_BLOCK
HEAD_PAIRS = ATT_WIDTH // 128


def _alibi_coef():
    n = N_GROUPS * ATT_HEADS
    slopes = np.exp2(-ALIBI_MAX * np.arange(1, n + 1, dtype=np.float32) / n).astype(np.float32)
    dil = np.repeat(np.array([d for _, d in ATT_GROUPS], np.float32), ATT_HEADS)
    return jnp.asarray(slopes * dil, F32)


def _blocks_per_seq(g):
    return jnp.where(g == 0, N_BLOCKS // ATT_GROUPS[0][1],
                     jnp.where(g == 1, N_BLOCKS // ATT_GROUPS[1][1], N_BLOCKS // ATT_GROUPS[2][1]))


def _att_masks(off_cur):
    B = ATT_BLOCK
    qi = lax.broadcasted_iota(jnp.int32, (B, B), 0)
    kj = lax.broadcasted_iota(jnp.int32, (B, B), 1)
    return qi, kj, (qi - kj).astype(F32), (qi + B - kj).astype(F32)


def _head_lanes(j):
    lane = lax.broadcasted_iota(jnp.int32, (ATT_BLOCK, 128), 1)
    return (lane >= 64 * j) & (lane < 64 * (j + 1))


def _lane_value(x, sel):
    return jnp.max(jnp.where(sel, x, -3e38), axis=-1, keepdims=True)


def att_fwd(qp, kp, vp, name):
    B = ATT_BLOCK

    def body(coef_ref, q_ref, kc_ref, kp_ref, vc_ref, vp_ref, o_ref, l_ref):
        g, hp, nb = pl.program_id(0), pl.program_id(1), pl.program_id(2)
        first = jnp.where(nb % _blocks_per_seq(g) == 0, 4 * B, 0)
        qi, kj, d_cur, d_prev = _att_masks(0)
        m_cur = kj <= qi
        m_prev = kj >= qi + first
        q, kc, kpv, vc, vpv = q_ref[0], kc_ref[0], kp_ref[0], vc_ref[0], vp_ref[0]
        o_acc = jnp.zeros((B, 128), F32)
        l_acc = jnp.zeros((B, 128), F32)
        for j in range(2):
            sel = _head_lanes(j)
            cf = coef_ref[g * ATT_HEADS + hp * 2 + j]
            qh = jnp.where(sel, q, 0.0)
            s_cur = jnp.where(m_cur, _dot(qh, kc, tb=True) * 0.125 - cf * d_cur, NEG_INF)
            s_prev = jnp.where(m_prev, _dot(qh, kpv, tb=True) * 0.125 - cf * d_prev, NEG_INF)
            mx = jnp.maximum(jnp.max(s_cur, axis=-1, keepdims=True), jnp.max(s_prev, axis=-1, keepdims=True))
            e_cur = jnp.exp(s_cur - mx)
            e_prev = jnp.exp(s_prev - mx)
            den = jnp.sum(e_cur, axis=-1, keepdims=True) + jnp.sum(e_prev, axis=-1, keepdims=True)
            inv = 1.0 / den
            oh = _dot(e_cur * inv, vc) + _dot(e_prev * inv, vpv)
            o_acc = jnp.where(sel, oh, o_acc)
            l_acc = jnp.where(sel, mx + jnp.log(den), l_acc)
        o_ref[0] = o_acc
        l_ref[0] = l_acc

    cur = pl.BlockSpec((1, B, 128), lambda g, hp, nb: (g, nb, hp))
    prev = pl.BlockSpec((1, B, 128), lambda g, hp, nb: (g, jnp.maximum(nb - 1, 0), hp))
    out = jax.ShapeDtypeStruct((N_GROUPS, SEQ, ATT_WIDTH), F32)
    return pl.pallas_call(
        body, name=name, grid=(N_GROUPS, HEAD_PAIRS, N_BLOCKS),
        in_specs=[pl.BlockSpec(memory_space=pltpu.SMEM), cur, cur, prev, cur, prev],
        out_specs=[cur, cur], out_shape=[out, out],
        compiler_params=pltpu.CompilerParams(dimension_semantics=("parallel", "parallel", "arbitrary")),
    )(_alibi_coef(), qp, kp, kp, vp, vp)


def att_bwd(qp, kp, vp, lp, dop, corrp, name):
    B = ATT_BLOCK

    def body(coef_ref, q_ref, qn_ref, kc_ref, kp_ref, vc_ref, vp_ref, l_ref, ln_ref, do_ref, don_ref, cr_ref, crn_ref,
             dq_ref, dk_ref, dv_ref):
        g, hp, nb = pl.program_id(0), pl.program_id(1), pl.program_id(2)
        bps = _blocks_per_seq(g)
        first = jnp.where(nb % bps == 0, 4 * B, 0)
        no_next = jnp.where((nb + 1) % bps == 0, 4 * B, 0)
        qi, kj, d_cur, d_prev = _att_masks(0)
        m_cc = kj <= qi
        m_cp = kj >= qi + first
        m_nc = kj >= qi + no_next
        q, qn, kc, kpv, vc, vpv = q_ref[0], qn_ref[0], kc_ref[0], kp_ref[0], vc_ref[0], vp_ref[0]
        lv, lnv, dov, donv, crv, crnv = l_ref[0], ln_ref[0], do_ref[0], don_ref[0], cr_ref[0], crn_ref[0]
        dq_acc = jnp.zeros((B, 128), F32)
        dk_acc = jnp.zeros((B, 128), F32)
        dv_acc = jnp.zeros((B, 128), F32)
        for j in range(2):
            sel = _head_lanes(j)
            cf = coef_ref[g * ATT_HEADS + hp * 2 + j]
            qh = jnp.where(sel, q, 0.0)
            qnh = jnp.where(sel, qn, 0.0)
            doh = jnp.where(sel, dov, 0.0)
            donh = jnp.where(sel, donv, 0.0)
            lse, lse_n = _lane_value(lv, sel), _lane_value(lnv, sel)
            cr, cr_n = _lane_value(crv, sel), _lane_value(crnv, sel)
            p_cc = jnp.exp(jnp.where(m_cc, _dot(qh, kc, tb=True) * 0.125 - cf * d_cur, NEG_INF) - lse)
            p_cp = jnp.exp(jnp.where(m_cp, _dot(qh, kpv, tb=True) * 0.125 - cf * d_prev, NEG_INF) - lse)
            p_nc = jnp.exp(jnp.where(m_nc, _dot(qnh, kc, tb=True) * 0.125 - cf * d_prev, NEG_INF) - lse_n)
            ds_cc = p_cc * (_dot(doh, vc, tb=True) + cr)
            ds_cp = p_cp * (_dot(doh, vpv, tb=True) + cr)
            ds_nc = p_nc * (_dot(donh, vc, tb=True) + cr_n)
            dqh = (_dot(ds_cc, kc) + _dot(ds_cp, kpv)) * 0.125
            dkh = (_dot(ds_cc, qh, ta=True) + _dot(ds_nc, qnh, ta=True)) * 0.125
            dvh = _dot(p_cc, doh, ta=True) + _dot(p_nc, donh, ta=True)
            dq_acc = jnp.where(sel, dqh, dq_acc)
            dk_acc = jnp.where(sel, dkh, dk_acc)
            dv_acc = jnp.where(sel, dvh, dv_acc)
        dq_ref[0] = dq_acc.astype(dq_ref.dtype)
        dk_ref[0] = dk_acc.astype(dk_ref.dtype)
        dv_ref[0] = dv_acc.astype(dv_ref.dtype)

    cur = pl.BlockSpec((1, B, 128), lambda g, hp, nb: (g, nb, hp))
    prev = pl.BlockSpec((1, B, 128), lambda g, hp, nb: (g, jnp.maximum(nb - 1, 0), hp))
    nxt = pl.BlockSpec((1, B, 128), lambda g, hp, nb: (g, jnp.minimum(nb + 1, N_BLOCKS - 1), hp))
    out = jax.ShapeDtypeStruct((N_GROUPS, SEQ, ATT_WIDTH), MXU_DTYPE)
    return pl.pallas_call(
        body, name=name, grid=(N_GROUPS, HEAD_PAIRS, N_BLOCKS),
        in_specs=[pl.BlockSpec(memory_space=pltpu.SMEM), cur, nxt, cur, prev, cur, prev, cur, nxt, cur, nxt, cur, nxt],
        out_specs=[cur, cur, cur], out_shape=[out, out, out],
        compiler_params=pltpu.CompilerParams(dimension_semantics=("parallel", "parallel", "arbitrary")),
    )(_alibi_coef(), qp, qp, kp, kp, vp, vp, lp, lp, dop, dop, corrp, corrp)


def _head_sum(x):
    i = lax.broadcasted_iota(jnp.int32, (128, 128), 0) // 64
    j = lax.broadcasted_iota(jnp.int32, (128, 128), 1) // 64
    return _dot_f32(x, (i == j).astype(F32))


def _group_weights(l0, l1, l2):
    mx = jnp.maximum(jnp.maximum(l0, l1), l2)
    e0, e1, e2 = jnp.exp(l0 - mx), jnp.exp(l1 - mx), jnp.exp(l2 - mx)
    inv = 1.0 / (e0 + e1 + e2)
    return e0 * inv, e1 * inv, e2 * inv


def att_combine_fwd(o, l, name):
    def body(o_ref, l_ref, y_ref):
        w0, w1, w2 = _group_weights(l_ref[0], l_ref[1], l_ref[2])
        y_ref[...] = (o_ref[0] * w0 + o_ref[1] * w1 + o_ref[2] * w2).astype(y_ref.dtype)

    blk3 = pl.BlockSpec((N_GROUPS, ROW_TILE, 128), lambda i, j: (0, i, j))
    return pl.pallas_call(
        body, name=name, grid=(SEQ // ROW_TILE, HEAD_PAIRS), in_specs=[blk3, blk3],
        out_specs=pl.BlockSpec((ROW_TILE, 128), lambda i, j: (i, j)),
        out_shape=jax.ShapeDtypeStruct((SEQ, ATT_WIDTH), MXU_DTYPE),
    )(o, l)


def att_combine_bwd(o, l, dy, name):
    def body(o_ref, l_ref, dy_ref, do_ref, cr_ref):
        w = _group_weights(l_ref[0], l_ref[1], l_ref[2])
        dyv = dy_ref[...]
        dw = [_head_sum(dyv * o_ref[g]) for g in range(N_GROUPS)]
        tot = w[0] * dw[0] + w[1] * dw[1] + w[2] * dw[2]
        for g in range(N_GROUPS):
            do_ref[g] = dyv * w[g]
            cr_ref[g] = -w[g] * tot

    blk3 = pl.BlockSpec((N_GROUPS, ROW_TILE, 128), lambda i, j: (0, i, j))
    out = jax.ShapeDtypeStruct((N_GROUPS, SEQ, ATT_WIDTH), F32)
    return pl.pallas_call(
        body, name=name, grid=(SEQ // ROW_TILE, HEAD_PAIRS),
        in_specs=[blk3, blk3, pl.BlockSpec((ROW_TILE, 128), lambda i, j: (i, j))],
        out_specs=[blk3, blk3], out_shape=[out, out],
    )(o, l, dy)


def _permute(a, d):
    if d == 1:
        return a
    return a.reshape(SEQ // d, d, a.shape[-1]).transpose(1, 0, 2).reshape(SEQ, a.shape[-1])


def _unpermute(a, d):
    if d == 1:
        return a
    return a.reshape(d, SEQ // d, a.shape[-1]).transpose(1, 0, 2).reshape(SEQ, a.shape[-1])


def _permute_groups(a):
    return jnp.stack([_permute(a[g], d) for g, (_, d) in enumerate(ATT_GROUPS)])


def _unpermute_groups(a):
    return jnp.stack([_unpermute(a[g], d) for g, (_, d) in enumerate(ATT_GROUPS)])


SUM_ROW_TILES = (256, 128, 64, 32, 16)
SUM_TILE_ELEMS = 128 * 1024


def _row_tile(rows, cols):
    fit = [t for t in SUM_ROW_TILES if rows % t == 0]
    return next((t for t in fit if t * cols <= SUM_TILE_ELEMS), fit[-1])


def _shard_shape(rows, cols, axis):
    return (rows // N_CHIPS, cols) if axis == 0 else (rows, cols // N_CHIPS)


def _half_shape(rows, cols, axis):
    return (rows, cols // 2) if axis == 0 else (rows // 2, cols)


def _piece_shape(rows, cols, axis):
    return (rows // N_CHIPS, cols // 2) if axis == 0 else (rows // 2, cols // N_CHIPS)


def place_own_block(shard, chip, rows, cols, axis, name):
    sr, sc = _shard_shape(rows, cols, axis)
    tr = _row_tile(sr, sc)

    def body(chip_ref, s_ref, o_ref):
        o_ref[...] = s_ref[...].astype(o_ref.dtype)

    if axis == 0:
        out_map = lambda i, chip_ref: (chip_ref[0] * (sr // tr) + i, 0)
    else:
        out_map = lambda i, chip_ref: (i, chip_ref[0])
    return pl.pallas_call(
        body, name=name, out_shape=jax.ShapeDtypeStruct((rows, cols), WEIGHT_COMM_DTYPE),
        grid_spec=pltpu.PrefetchScalarGridSpec(
            num_scalar_prefetch=1, grid=(sr // tr,), in_specs=[pl.BlockSpec((tr, sc), lambda i, chip_ref: (i, 0))],
            out_specs=pl.BlockSpec((tr, sc), out_map)),
    )(chip, shard)


def add_halves(g, theirs, core, rows, cols, axis, name):
    hr, hc = _half_shape(rows, cols, axis)
    tr = _row_tile(hr, hc)

    def body(core_ref, g_ref, t_ref, o_ref):
        o_ref[...] = (g_ref[...].astype(F32) + t_ref[...].astype(F32)).astype(o_ref.dtype)

    if axis == 0:
        g_map = lambda i, core_ref: (i, core_ref[0])
    else:
        g_map = lambda i, core_ref: (core_ref[0] * (hr // tr) + i, 0)
    blk = pl.BlockSpec((tr, hc), lambda i, core_ref: (i, 0))
    return pl.pallas_call(
        body, name=name, out_shape=jax.ShapeDtypeStruct((hr, hc), GRAD_COMM_DTYPE),
        grid_spec=pltpu.PrefetchScalarGridSpec(
            num_scalar_prefetch=1, grid=(hr // tr,), in_specs=[pl.BlockSpec((tr, hc), g_map), blk], out_specs=blk),
    )(core, g, theirs)


def add_pieces(half, got, chip, rows, cols, axis, name):
    hr, _ = _half_shape(rows, cols, axis)
    pr, pc = _piece_shape(rows, cols, axis)
    tr = _row_tile(pr, pc)

    def body(chip_ref, h_ref, got_ref, o_ref):
        o_ref[...] = (h_ref[...].astype(F32) + got_ref[0].astype(F32) + got_ref[1].astype(F32) + got_ref[2].astype(F32))

    if axis == 0:
        h_map = lambda i, chip_ref: (chip_ref[0] * (pr // tr) + i, 0)
    else:
        h_map = lambda i, chip_ref: (i, chip_ref[0])
    return pl.pallas_call(
        body, name=name, out_shape=jax.ShapeDtypeStruct((pr, pc), F32),
        grid_spec=pltpu.PrefetchScalarGridSpec(
            num_scalar_prefetch=1, grid=(pr // tr,),
            in_specs=[pl.BlockSpec((tr, pc), h_map), pl.BlockSpec((3, tr, pc), lambda i, chip_ref: (0, i, 0))],
            out_specs=pl.BlockSpec((tr, pc), lambda i, chip_ref: (i, 0))),
    )(chip, half, got)


def _adamw_math(w, g, m, v):
    nm = ADAM_B1 * m + (1.0 - ADAM_B1) * g
    nv = ADAM_B2 * v + (1.0 - ADAM_B2) * (g * g)
    m_hat = nm / (1.0 - ADAM_B1 ** ADAM_STEP)
    v_hat = nv / (1.0 - ADAM_B2 ** ADAM_STEP)
    return -ADAM_LR * (m_hat / (jnp.sqrt(v_hat) + ADAM_EPS) + ADAM_WD * w), nm, nv


def adamw(w, g, m, v, name):
    R, Cc = w.shape
    tr = _pick(R, (256, 128, 64, 8))

    def body(w_ref, g_ref, m_ref, v_ref, d_ref, nm_ref, nv_ref):
        d_ref[...], nm_ref[...], nv_ref[...] = _adamw_math(w_ref[...], g_ref[...], m_ref[...], v_ref[...])

    blk = pl.BlockSpec((tr, Cc), lambda i: (i, 0))
    out = jax.ShapeDtypeStruct((R, Cc), F32)
    return pl.pallas_call(
        body, name=name, grid=(R // tr,), in_specs=[blk] * 4, out_specs=[blk] * 3, out_shape=[out, out, out],
    )(w, g, m, v)


def adamw_halves(w, mine, theirs, m, v, core, rows, cols, axis, name):
    sr, sc = _shard_shape(rows, cols, axis)
    pr, pc = _piece_shape(rows, cols, axis)
    tr = _row_tile(pr, pc)
    nt = pr // tr

    def body(core_ref, w_ref, a_ref, b_ref, m_ref, v_ref, g_ref, d_ref, nm_ref, nv_ref):
        g = jnp.where(pl.program_id(0) == core_ref[0], a_ref[...], b_ref[...])
        g_ref[...] = g
        d_ref[...], nm_ref[...], nv_ref[...] = _adamw_math(w_ref[...], g, m_ref[...], v_ref[...])

    if axis == 0:
        full = pl.BlockSpec((tr, pc), lambda h, i, core_ref: (i, h))
    else:
        full = pl.BlockSpec((tr, pc), lambda h, i, core_ref: (h * nt + i, 0))
    part = pl.BlockSpec((tr, pc), lambda h, i, core_ref: (i, 0))
    out = jax.ShapeDtypeStruct((sr, sc), F32)
    return pl.pallas_call(
        body, name=name, out_shape=[out, out, out, out],
        grid_spec=pltpu.PrefetchScalarGridSpec(
            num_scalar_prefetch=1, grid=(2, nt), in_specs=[full, part, part, full, full], out_specs=[full] * 4),
    )(core, w, mine, theirs, m, v)


BIG = (
    ("ffn1_w_gate_up", D_MODEL, 2 * D_FF, 1),
    ("ffn1_w_down", D_FF, D_MODEL, 0),
    ("w_in", D_MODEL, IN_COLS, 1),
    ("w_branch_hg", HG_WIDTH, D_MODEL, 1),
    ("w_branch_att", ATT_WIDTH, D_MODEL, 1),
    ("w_out", D_MODEL, D_MODEL, 0),
    ("ffn2_w_gate_up", D_MODEL, 2 * D_FF, 1),
    ("ffn2_w_down", D_FF, D_MODEL, 0),
)
N_BIG = len(BIG)
ANY = pl.BlockSpec(memory_space=pl.ANY)


def _place():
    return lax.axis_index("x"), lax.axis_index("y"), lax.axis_index("c")


def _other_chips(x, y):
    return ((1 - x, y), (x, 1 - y), (1 - x, 1 - y))


MAX_COPY_CHUNKS = 16
CHUNK_ROW_ALIGN = 16


def _row_chunks(view):
    rows = view.shape[0]
    n = next(n for n in range(MAX_COPY_CHUNKS, 0, -1) if rows % (CHUNK_ROW_ALIGN * n) == 0 or n == 1)
    step = rows // n
    return [pl.ds(i * step, step) for i in range(n)]


def _remote(src, dst, send_sem, recv_sem, device):
    return pltpu.make_async_remote_copy(src_ref=src, dst_ref=dst, send_sem=send_sem, recv_sem=recv_sem,
                                        device_id=device, device_id_type=MESH)


def _start_remote(src, dst, send_sem, recv_sem, device):
    for rows in _row_chunks(src):
        _remote(src.at[rows, :], dst.at[rows, :], send_sem, recv_sem, device).start()
    return _remote(src, dst, send_sem, recv_sem, device)


def all_gather_weights(placed):
    def piece(ref, rows, cols, axis, chip, c):
        sr, sc = _shard_shape(rows, cols, axis)
        j = 2 * chip[0] + chip[1]
        if axis == 0:
            return ref.at[pl.ds(j * sr + c * (sr // 2), sr // 2), :]
        return ref.at[pl.ds(c * (sr // 2), sr // 2), pl.ds(pl.multiple_of(j * sc, 128), sc)]

    def body(*refs):
        outs = refs[N_BIG:2 * N_BIG]
        send_sems, recv_sems = refs[2 * N_BIG:]
        x, y, c = _place()
        chips = _other_chips(x, y)
        sends = []
        for w, (_, r, cc, ax) in enumerate(BIG):
            mine = piece(outs[w], r, cc, ax, (x, y), c)
            for k, chip in enumerate(chips):
                sends.append(_start_remote(mine, mine, send_sems.at[w, k], recv_sems.at[w, k], (*chip, c)))
        passed = []
        for w, (_, r, cc, ax) in enumerate(BIG):
            for k, chip in enumerate(chips):
                got = piece(outs[w], r, cc, ax, chip, c)
                _remote(got, got, send_sems.at[w, k], recv_sems.at[w, k], (x, y, c)).wait_recv()
                passed.append(_start_remote(got, got, send_sems.at[w, 3 + k], recv_sems.at[w, 3 + k], (x, y, 1 - c)))
        for w, (_, r, cc, ax) in enumerate(BIG):
            for k, chip in enumerate(chips):
                got = piece(outs[w], r, cc, ax, chip, 1 - c)
                _remote(got, got, send_sems.at[w, 3 + k], recv_sems.at[w, 3 + k], (x, y, c)).wait_recv()
        for cp in sends + passed:
            cp.wait_send()

    return pl.pallas_call(
        body, name="all_gather_weights", in_specs=[ANY] * N_BIG, out_specs=[ANY] * N_BIG,
        out_shape=[jax.ShapeDtypeStruct((r, cc), WEIGHT_COMM_DTYPE) for _, r, cc, _ in BIG],
        input_output_aliases={w: w for w in range(N_BIG)},
        scratch_shapes=[pltpu.SemaphoreType.DMA((N_BIG, 6)), pltpu.SemaphoreType.DMA((N_BIG, 6))],
    )(*placed)


def _half(ref, rows, cols, axis, c):
    if axis == 0:
        return ref.at[:, pl.ds(pl.multiple_of(c * (cols // 2), 128), cols // 2)]
    return ref.at[pl.ds(c * (rows // 2), rows // 2), :]


def _piece_of_half(ref, rows, cols, axis, chip):
    j = 2 * chip[0] + chip[1]
    pr, pc = _piece_shape(rows, cols, axis)
    if axis == 0:
        return ref.at[pl.ds(j * pr, pr), :]
    return ref.at[:, pl.ds(pl.multiple_of(j * pc, 128), pc)]


def exchange_halves(grads):
    def body(*refs):
        ins, theirs = refs[:N_BIG], refs[N_BIG:2 * N_BIG]
        send_sems, recv_sems = refs[2 * N_BIG:]
        x, y, c = _place()
        copies = [_start_remote(_half(ins[w], r, cc, ax, 1 - c), theirs[w], send_sems.at[w], recv_sems.at[w], (x, y, 1 - c))
                  for w, (_, r, cc, ax) in enumerate(BIG)]
        for cp in copies:
            cp.wait()

    return pl.pallas_call(
        body, name="exchange_halves", in_specs=[ANY] * N_BIG, out_specs=[ANY] * N_BIG,
        out_shape=[jax.ShapeDtypeStruct(_half_shape(r, cc, ax), GRAD_COMM_DTYPE) for _, r, cc, ax in BIG],
        scratch_shapes=[pltpu.SemaphoreType.DMA((N_BIG,)), pltpu.SemaphoreType.DMA((N_BIG,))],
    )(*grads)


def scatter_pieces(halves):
    def body(*refs):
        ins, got = refs[:N_BIG], refs[N_BIG:2 * N_BIG]
        send_sems, recv_sems = refs[2 * N_BIG:]
        x, y, c = _place()
        copies = []
        for w, (_, r, cc, ax) in enumerate(BIG):
            for k, chip in enumerate(_other_chips(x, y)):
                copies.append(_start_remote(_piece_of_half(ins[w], r, cc, ax, chip), got[w].at[k], send_sems.at[w, k],
                                            recv_sems.at[w, k], (*chip, c)))
        for cp in copies:
            cp.wait()

    return pl.pallas_call(
        body, name="scatter_pieces", in_specs=[ANY] * N_BIG, out_specs=[ANY] * N_BIG,
        out_shape=[jax.ShapeDtypeStruct((3,) + _piece_shape(r, cc, ax), GRAD_COMM_DTYPE) for _, r, cc, ax in BIG],
        scratch_shapes=[pltpu.SemaphoreType.DMA((N_BIG, 3)), pltpu.SemaphoreType.DMA((N_BIG, 3))],
    )(*halves)


def exchange_reduced(pieces):
    def body(*refs):
        ins, theirs = refs[:N_BIG], refs[N_BIG:2 * N_BIG]
        send_sems, recv_sems = refs[2 * N_BIG:]
        x, y, c = _place()
        copies = [_start_remote(ins[w], theirs[w], send_sems.at[w], recv_sems.at[w], (x, y, 1 - c)) for w in range(N_BIG)]
        for cp in copies:
            cp.wait()

    return pl.pallas_call(
        body, name="exchange_reduced", in_specs=[ANY] * N_BIG, out_specs=[ANY] * N_BIG,
        out_shape=[jax.ShapeDtypeStruct(_piece_shape(r, cc, ax), F32) for _, r, cc, ax in BIG],
        scratch_shapes=[pltpu.SemaphoreType.DMA((N_BIG,)), pltpu.SemaphoreType.DMA((N_BIG,))],
    )(*pieces)


N_DEV = 8
SMALL_ROWS = 8


def all_reduce_small(packed):
    def body(x_ref, o_ref, gathered, send_sems, recv_sems):
        x, y, c = _place()
        me = 4 * x + 2 * y + c
        gathered[me] = x_ref[...]
        copies = []
        for k in range(1, N_DEV):
            peer = (x ^ (k >> 2), y ^ ((k >> 1) & 1), c ^ (k & 1))
            cp = pltpu.make_async_remote_copy(
                src_ref=x_ref, dst_ref=gathered.at[me], send_sem=send_sems.at[k - 1], recv_sem=recv_sems.at[k - 1],
                device_id=peer, device_id_type=MESH)
            cp.start()
            copies.append(cp)
        for cp in copies:
            cp.wait()
        acc = gathered[0]
        for k in range(1, N_DEV):
            acc = acc + gathered[k]
        o_ref[...] = acc

    vm = pl.BlockSpec(memory_space=pltpu.VMEM)
    return pl.pallas_call(
        body, name="all_reduce_small", in_specs=[vm], out_specs=vm,
        out_shape=jax.ShapeDtypeStruct((SMALL_ROWS, D_MODEL), F32),
        scratch_shapes=[pltpu.VMEM((N_DEV, SMALL_ROWS, D_MODEL), F32), pltpu.SemaphoreType.DMA((N_DEV - 1,)),
                        pltpu.SemaphoreType.DMA((N_DEV - 1,))],
    )(packed)


def _swiglu_block_fwd(h, norm_g, w_gu, w_down, tag):
    n = rmsnorm_fwd(h, norm_g, f"{tag}_norm")
    gu = matmul(n, w_gu, name=f"{tag}_gate_up")
    s = swiglu_fwd(gu, f"{tag}_swiglu")
    h_out = matmul(s, w_down, res=h, scale=0.5, name=f"{tag}_down")
    return h_out, (n, gu, s)


def _swiglu_block_bwd(h, norm_g, w_gu, w_down, saved, dh_out, tag):
    n, gu, s = saved
    df = dh_out.astype(MXU_DTYPE)
    d_down = matmul(s, df, ta=True, scale=0.5, out_dtype=GRAD_COMM_DTYPE, name=f"{tag}_d_w_down")
    ds = matmul(df, w_down, tb=True, scale=0.5, name=f"{tag}_d_s")
    dgu = swiglu_bwd(gu, ds, f"{tag}_swiglu_bwd")
    d_gu = matmul(n, dgu, ta=True, out_dtype=GRAD_COMM_DTYPE, name=f"{tag}_d_w_gate_up")
    dn = matmul(dgu, w_gu, tb=True, name=f"{tag}_d_n")
    dh, dg = rmsnorm_bwd(h, norm_g, dn, dh_out, f"{tag}_norm_bwd")
    return dh, dg, d_gu, d_down


ATT_COL0 = 4 * HG_WIDTH


def local_step(x, target, small, big):
    h1, saved1 = _swiglu_block_fwd(x, small["ffn1_norm"], big["ffn1_w_gate_up"], big["ffn1_w_down"], "ffn1")
    u = rmsnorm_fwd(h1, small["mix_norm"], "mix_norm")
    z = matmul(u, big["w_in"], name="w_in")
    p = small["hg_lower_bounds"]
    lb = 1.0 / (1.0 + jnp.exp(p[1:2] - p[0:1]))
    y_hg, o_raw, states = hgrn_fwd(z, lb, small["hg_out_norm"], "hgrn_fwd")
    att = z[:, ATT_COL0:ATT_COL0 + 3 * N_GROUPS * ATT_WIDTH].reshape(SEQ, N_GROUPS, 3, ATT_WIDTH)
    qp = _permute_groups(att[:, :, 0].transpose(1, 0, 2))
    kp = _permute_groups(att[:, :, 1].transpose(1, 0, 2))
    vp = _permute_groups(att[:, :, 2].transpose(1, 0, 2))
    op, lp = att_fwd(qp, kp, vp, "att_fwd")
    o_att, l_att = _unpermute_groups(op), _unpermute_groups(lp)
    y_att = att_combine_fwd(o_att, l_att, "att_combine")
    bh = matmul(y_hg, big["w_branch_hg"], name="branch_hg")
    ba = matmul(y_att, big["w_branch_att"], name="branch_att")
    merged = merge_fwd(z, bh, ba, "merge")
    h2 = matmul(merged, big["w_out"], res=h1, name="w_out")
    h3, saved2 = _swiglu_block_fwd(h2, small["ffn2_norm"], big["ffn2_w_gate_up"], big["ffn2_w_down"], "ffn2")
    dh3, d_final, loss = final_norm_loss(h3, small["final_norm"], target, "final_norm_loss")

    gs, gb = {"final_norm": d_final}, {}
    dh2, gs["ffn2_norm"], gb["ffn2_w_gate_up"], gb["ffn2_w_down"] = _swiglu_block_bwd(
        h2, small["ffn2_norm"], big["ffn2_w_gate_up"], big["ffn2_w_down"], saved2, dh3, "ffn2")
    dh2_m = dh2.astype(MXU_DTYPE)
    gb["w_out"] = matmul(merged, dh2_m, ta=True, out_dtype=GRAD_COMM_DTYPE, name="d_w_out")
    dmerged = matmul(dh2_m, big["w_out"], tb=True, name="d_merged")
    dbh, dba, dgh, dga = merge_bwd(z, bh, ba, dmerged, "merge_bwd")
    gb["w_branch_hg"] = matmul(y_hg, dbh, ta=True, out_dtype=GRAD_COMM_DTYPE, name="d_w_branch_hg")
    gb["w_branch_att"] = matmul(y_att, dba, ta=True, out_dtype=GRAD_COMM_DTYPE, name="d_w_branch_att")
    dy_hg = matmul(dbh, big["w_branch_hg"], tb=True, name="d_y_hg")
    dy_att = matmul(dba, big["w_branch_att"], tb=True, name="d_y_att")
    dq, dfp, di, dog, d_lb, gs["hg_out_norm"] = hgrn_bwd(z, lb, small["hg_out_norm"], o_raw, states, dy_hg, "hgrn_bwd")
    do_att, corr = att_combine_bwd(o_att, l_att, dy_att, "att_combine_bwd")
    dqp, dkp, dvp = att_bwd(qp, kp, vp, lp, _permute_groups(do_att), _permute_groups(corr), "att_bwd")
    dqa, dka, dva = _unpermute_groups(dqp), _unpermute_groups(dkp), _unpermute_groups(dvp)
    d_att = jnp.stack([dqa, dka, dva], axis=1).transpose(2, 0, 1, 3).reshape(SEQ, 3 * N_GROUPS * ATT_WIDTH)
    dz = jnp.concatenate([dq, dfp, di, dog, d_att, dgh, dga], axis=1)
    gb["w_in"] = matmul(u, dz, ta=True, out_dtype=GRAD_COMM_DTYPE, name="d_w_in")
    du = matmul(dz, big["w_in"], tb=True, name="d_u")
    dh1, gs["mix_norm"] = rmsnorm_bwd(h1, small["mix_norm"], du, dh2, "mix_norm_bwd")
    dp0 = d_lb * lb * (1.0 - lb)
    gs["hg_lower_bounds"] = jnp.concatenate([dp0, -dp0], axis=0)
    dx, gs["ffn1_norm"], gb["ffn1_w_gate_up"], gb["ffn1_w_down"] = _swiglu_block_bwd(
        x, small["ffn1_norm"], big["ffn1_w_gate_up"], big["ffn1_w_down"], saved1, dh1, "ffn1")
    return loss[0, 0], dx, gs, gb


SMALL = ("ffn1_norm", "mix_norm", "hg_lower_bounds", "hg_out_norm", "ffn2_norm", "final_norm")
WEIGHTS = ("ffn1_norm", "ffn1_w_gate_up", "ffn1_w_down", "mix_norm", "w_in", "hg_lower_bounds", "hg_out_norm",
           "w_branch_hg", "w_branch_att", "w_out", "ffn2_norm", "ffn2_w_gate_up", "ffn2_w_down", "final_norm")
SMALL_SHAPE = {"ffn1_norm": (1, 1024), "mix_norm": (1, 1024), "hg_lower_bounds": (2, 512), "hg_out_norm": (1, 512),
               "ffn2_norm": (1, 1024), "final_norm": (1024,)}
LOSS_ROW = 6


def _pack_small(vals):
    rows = []
    for n in SMALL:
        r = vals[n].reshape(1, -1).astype(F32)
        rows.append(jnp.pad(r, ((0, 0), (0, D_MODEL - r.shape[1]))))
    rows.append(jnp.zeros((SMALL_ROWS - len(SMALL), D_MODEL), F32))
    return jnp.concatenate(rows, axis=0)


def _unpack_small(packed):
    out = {}
    for i, n in enumerate(SMALL):
        size = int(np.prod(SMALL_SHAPE[n]))
        out[n] = packed[i, :size].reshape(SMALL_SHAPE[n])
    return out


def reduce_big(grads, core, chip):
    theirs = exchange_halves([grads[n] for n, *_ in BIG])
    halves = [add_halves(grads[n], t, core, r, cc, ax, f"add_halves_{n}") for (n, r, cc, ax), t in zip(BIG, theirs)]
    got = scatter_pieces(halves)
    mine = [add_pieces(h, g, chip, r, cc, ax, f"add_pieces_{n}") for (n, r, cc, ax), h, g in zip(BIG, halves, got)]
    return mine, exchange_reduced(mine)


def kernel(x, ffn1_norm, ffn1_w_gate_up, ffn1_w_down, mix_norm, w_in, hg_lower_bounds, hg_out_norm, w_branch_hg, w_branch_att, w_out, ffn2_norm, ffn2_w_gate_up, ffn2_w_down, final_norm, loss_target, m_ffn1_norm, m_ffn1_w_gate_up, m_ffn1_w_down, m_mix_norm, m_w_in, m_hg_lower_bounds, m_hg_out_norm, m_w_branch_hg, m_w_branch_att, m_w_out, m_ffn2_norm, m_ffn2_w_gate_up, m_ffn2_w_down, m_final_norm, v_ffn1_norm, v_ffn1_w_gate_up, v_ffn1_w_down, v_mix_norm, v_w_in, v_hg_lower_bounds, v_hg_out_norm, v_w_branch_hg, v_w_branch_att, v_w_out, v_ffn2_norm, v_ffn2_w_gate_up, v_ffn2_w_down, v_final_norm):
    w = dict(ffn1_norm=ffn1_norm, ffn1_w_gate_up=ffn1_w_gate_up, ffn1_w_down=ffn1_w_down, mix_norm=mix_norm, w_in=w_in,
             hg_lower_bounds=hg_lower_bounds, hg_out_norm=hg_out_norm, w_branch_hg=w_branch_hg, w_branch_att=w_branch_att,
             w_out=w_out, ffn2_norm=ffn2_norm, ffn2_w_gate_up=ffn2_w_gate_up, ffn2_w_down=ffn2_w_down, final_norm=final_norm)
    m = dict(ffn1_norm=m_ffn1_norm, ffn1_w_gate_up=m_ffn1_w_gate_up, ffn1_w_down=m_ffn1_w_down, mix_norm=m_mix_norm,
             w_in=m_w_in, hg_lower_bounds=m_hg_lower_bounds, hg_out_norm=m_hg_out_norm, w_branch_hg=m_w_branch_hg,
             w_branch_att=m_w_branch_att, w_out=m_w_out, ffn2_norm=m_ffn2_norm, ffn2_w_gate_up=m_ffn2_w_gate_up,
             ffn2_w_down=m_ffn2_w_down, final_norm=m_final_norm)
    v = dict(ffn1_norm=v_ffn1_norm, ffn1_w_gate_up=v_ffn1_w_gate_up, ffn1_w_down=v_ffn1_w_down, mix_norm=v_mix_norm,
             w_in=v_w_in, hg_lower_bounds=v_hg_lower_bounds, hg_out_norm=v_hg_out_norm, w_branch_hg=v_w_branch_hg,
             w_branch_att=v_w_branch_att, w_out=v_w_out, ffn2_norm=v_ffn2_norm, ffn2_w_gate_up=v_ffn2_w_gate_up,
             ffn2_w_down=v_ffn2_w_down, final_norm=v_final_norm)

    core = lax.axis_index("c").astype(jnp.int32).reshape(1)
    chip = (2 * lax.axis_index("x") + lax.axis_index("y")).astype(jnp.int32).reshape(1)
    gathered = all_gather_weights([place_own_block(w[n][0], chip, r, cc, ax, f"place_{n}") for n, r, cc, ax in BIG])
    big = {n: a for (n, *_), a in zip(BIG, gathered)}
    small = {n: w[n] for n in SMALL}
    small["final_norm"] = final_norm.reshape(1, D_MODEL)

    loss, dx, gs, gb = local_step(x[0], loss_target[0], small, big)

    packed = _pack_small(gs)
    packed = packed.at[LOSS_ROW].set(jnp.full((D_MODEL,), loss, F32))
    total = all_reduce_small(packed)
    grads = _unpack_small(total)
    loss_total = total[LOSS_ROW, 0]
    mine, theirs = reduce_big(gb, core, chip)

    delta, new_m, new_v = {}, {}, {}
    pd, pm, pv = adamw(_pack_small({n: w[n] for n in SMALL}), total.at[LOSS_ROW].set(0.0),
                       _pack_small({n: m[n] for n in SMALL}), _pack_small({n: v[n] for n in SMALL}), "adamw_small")
    delta.update(_unpack_small(pd))
    new_m.update(_unpack_small(pm))
    new_v.update(_unpack_small(pv))
    for (n, r, cc, ax), a, b in zip(BIG, mine, theirs):
        g, d, nm, nv = adamw_halves(w[n][0], a, b, m[n][0], v[n][0], core, r, cc, ax, f"adamw_{n}")
        grads[n], delta[n], new_m[n], new_v[n] = g[None], d[None], nm[None], nv[None]

    return (loss_total, dx[None], *[grads[n] for n in WEIGHTS], *[delta[n] for n in WEIGHTS],
            *[new_m[n] for n in WEIGHTS], *[new_v[n] for n in WEIGHTS])
```

```python
import numpy as np
import jax
import jax.numpy as jnp
from jax import lax
from jax.experimental import pallas as pl
from jax.experimental.pallas import tpu as pltpu

SEQ = 2048
D_MODEL = 1024
D_FF = 2816
HG_HEADS = 4
HG_DIM = 128
HG_WIDTH = 512
HG_CHUNK = 64
ATT_GROUPS = ((128, 1), (512, 4), (2048, 16))
ATT_HEADS = 8
ATT_WIDTH = 512
ATT_BLOCK = 128
ALIBI_MAX = 8.0
IN_COLS = 8704
EPS = 1e-6
NEG_INF = -1e30
ADAM_LR = 0.001
ADAM_B1 = 0.9
ADAM_B2 = 0.999
ADAM_EPS = 1e-08
ADAM_WD = 0.01
ADAM_STEP = 10

N_CHIPS = 4
MXU_DTYPE = jnp.bfloat16
HG_DOT_DTYPE = jnp.float32
WEIGHT_COMM_DTYPE = jnp.bfloat16
GRAD_COMM_DTYPE = jnp.bfloat16
MESH = pl.DeviceIdType.MESH
F32 = jnp.float32
HIGHEST = lax.Precision.HIGHEST


def _pick(n, cands):
    for c in cands:
        if n % c == 0:
            return c
    return n


def _sigmoid(x):
    return 1.0 / (1.0 + jnp.exp(-x))


def _dot(a, b, ta=False, tb=False):
    dn = (((0 if ta else 1,), (1 if tb else 0,)), ((), ()))
    return lax.dot_general(a.astype(MXU_DTYPE), b.astype(MXU_DTYPE), dn, preferred_element_type=F32)


def _dot_f32(a, b):
    return jnp.dot(a, b, precision=HIGHEST, preferred_element_type=F32)


def _hdot(a, b, ta=False, tb=False):
    if HG_DOT_DTYPE == F32:
        dn = (((0 if ta else 1,), (1 if tb else 0,)), ((), ()))
        return lax.dot_general(a, b, dn, precision=HIGHEST, preferred_element_type=F32)
    return _dot(a, b, ta, tb)


MATMUL_VMEM_BYTES = 40 * 1024 * 1024


def matmul(a, b, *, ta=False, tb=False, out_dtype=F32, res=None, scale=1.0, name):
    if ta:
        K, M = a.shape
    else:
        M, K = a.shape
    if tb:
        N, K2 = b.shape
    else:
        K2, N = b.shape
    assert K == K2
    tm = _pick(M, (1024, 512, 256, 128))
    tn = _pick(N, (512, 256, 128))
    tk = _pick(K, (512, 256, 128))
    nk = K // tk

    def body(*refs):
        if res is None:
            a_ref, b_ref, o_ref, acc = refs
        else:
            a_ref, b_ref, r_ref, o_ref, acc = refs
        k = pl.program_id(2)

        @pl.when(k == 0)
        def _():
            acc[...] = jnp.zeros_like(acc)

        acc[...] += _dot(a_ref[...], b_ref[...], ta, tb)

        @pl.when(k == nk - 1)
        def _():
            r = acc[...]
            if scale != 1.0:
                r = r * scale
            if res is not None:
                r = r_ref[...] + r
            o_ref[...] = r.astype(out_dtype)

    a_spec = pl.BlockSpec((tk, tm), lambda i, j, k: (k, i)) if ta else pl.BlockSpec((tm, tk), lambda i, j, k: (i, k))
    b_spec = pl.BlockSpec((tn, tk), lambda i, j, k: (j, k)) if tb else pl.BlockSpec((tk, tn), lambda i, j, k: (k, j))
    in_specs = [a_spec, b_spec]
    args = [a, b]
    if res is not None:
        in_specs.append(pl.BlockSpec((tm, tn), lambda i, j, k: (i, j)))
        args.append(res)
    return pl.pallas_call(
        body, name=name, grid=(M // tm, N // tn, nk), in_specs=in_specs,
        out_specs=pl.BlockSpec((tm, tn), lambda i, j, k: (i, j)),
        out_shape=jax.ShapeDtypeStruct((M, N), out_dtype),
        scratch_shapes=[pltpu.VMEM((tm, tn), F32)],
        compiler_params=pltpu.CompilerParams(dimension_semantics=("parallel", "parallel", "arbitrary"),
                                             vmem_limit_bytes=MATMUL_VMEM_BYTES),
    )(*args)


ROW_TILE = 256


def rmsnorm_fwd(x, g, name):
    def body(x_ref, g_ref, n_ref):
        xv = x_ref[...]
        r = lax.rsqrt(jnp.mean(xv * xv, axis=-1, keepdims=True) + EPS)
        n_ref[...] = ((xv * r) * g_ref[...]).astype(n_ref.dtype)

    return pl.pallas_call(
        body, name=name, grid=(SEQ // ROW_TILE,),
        in_specs=[pl.BlockSpec((ROW_TILE, D_MODEL), lambda i: (i, 0)), pl.BlockSpec((1, D_MODEL), lambda i: (0, 0))],
        out_specs=pl.BlockSpec((ROW_TILE, D_MODEL), lambda i: (i, 0)),
        out_shape=jax.ShapeDtypeStruct((SEQ, D_MODEL), MXU_DTYPE),
    )(x, g)


def rmsnorm_bwd(x, g, dn, dres, name):
    def body(x_ref, g_ref, dn_ref, dr_ref, dx_ref, dg_ref):
        xv = x_ref[...]
        r = lax.rsqrt(jnp.mean(xv * xv, axis=-1, keepdims=True) + EPS)
        xh = xv * r
        dnv = dn_ref[...]

        @pl.when(pl.program_id(0) == 0)
        def _():
            dg_ref[...] = jnp.zeros_like(dg_ref)

        dg_ref[...] += jnp.sum(dnv * xh, axis=0, keepdims=True)
        dxh = dnv * g_ref[...]
        dx_ref[...] = dr_ref[...] + r * (dxh - xh * jnp.mean(dxh * xh, axis=-1, keepdims=True))

    row = pl.BlockSpec((ROW_TILE, D_MODEL), lambda i: (i, 0))
    vec = pl.BlockSpec((1, D_MODEL), lambda i: (0, 0))
    return pl.pallas_call(
        body, name=name, grid=(SEQ // ROW_TILE,), in_specs=[row, vec, row, row], out_specs=[row, vec],
        out_shape=[jax.ShapeDtypeStruct((SEQ, D_MODEL), F32), jax.ShapeDtypeStruct((1, D_MODEL), F32)],
        compiler_params=pltpu.CompilerParams(dimension_semantics=("arbitrary",)),
    )(x, g, dn, dres)


def final_norm_loss(h, g, target, name):
    def body(h_ref, g_ref, t_ref, dh_ref, dg_ref, loss_ref):
        xv = h_ref[...]
        r = lax.rsqrt(jnp.mean(xv * xv, axis=-1, keepdims=True) + EPS)
        xh = xv * r
        gv = g_ref[...]
        e = xh * gv - t_ref[...]

        @pl.when(pl.program_id(0) == 0)
        def _():
            dg_ref[...] = jnp.zeros_like(dg_ref)
            loss_ref[...] = jnp.zeros_like(loss_ref)

        part = 0.5 * jnp.sum(jnp.sum(e * e, axis=-1, keepdims=True) * (1.0 / D_MODEL), axis=0, keepdims=True)
        loss_ref[...] += jnp.broadcast_to(part, loss_ref.shape)
        dout = e * (1.0 / D_MODEL)
        dg_ref[...] += jnp.sum(dout * xh, axis=0, keepdims=True)
        dxh = dout * gv
        dh_ref[...] = r * (dxh - xh * jnp.mean(dxh * xh, axis=-1, keepdims=True))

    row = pl.BlockSpec((ROW_TILE, D_MODEL), lambda i: (i, 0))
    vec = pl.BlockSpec((1, D_MODEL), lambda i: (0, 0))
    return pl.pallas_call(
        body, name=name, grid=(SEQ // ROW_TILE,), in_specs=[row, vec, row],
        out_specs=[row, vec, pl.BlockSpec((8, 128), lambda i: (0, 0))],
        out_shape=[jax.ShapeDtypeStruct((SEQ, D_MODEL), F32), jax.ShapeDtypeStruct((1, D_MODEL), F32),
                   jax.ShapeDtypeStruct((8, 128), F32)],
        compiler_params=pltpu.CompilerParams(dimension_semantics=("arbitrary",)),
    )(h, g, target)


FF_TILE = D_FF // 2


def swiglu_fwd(gu, name):
    def body(a_ref, b_ref, s_ref):
        a = a_ref[...]
        s_ref[...] = (a * _sigmoid(a) * b_ref[...]).astype(s_ref.dtype)

    return pl.pallas_call(
        body, name=name, grid=(SEQ // ROW_TILE, 2),
        in_specs=[pl.BlockSpec((ROW_TILE, FF_TILE), lambda i, j: (i, j)),
                  pl.BlockSpec((ROW_TILE, FF_TILE), lambda i, j: (i, j + 2))],
        out_specs=pl.BlockSpec((ROW_TILE, FF_TILE), lambda i, j: (i, j)),
        out_shape=jax.ShapeDtypeStruct((SEQ, D_FF), MXU_DTYPE),
    )(gu, gu)


def swiglu_bwd(gu, ds, name):
    def body(a_ref, b_ref, ds_ref, o_ref):
        a = a_ref[...]
        sg = _sigmoid(a)
        dsv = ds_ref[...]

        @pl.when(pl.program_id(1) < 2)
        def _():
            o_ref[...] = (dsv * b_ref[...] * (sg * (1.0 + a * (1.0 - sg)))).astype(o_ref.dtype)

        @pl.when(pl.program_id(1) >= 2)
        def _():
            o_ref[...] = (dsv * a * sg).astype(o_ref.dtype)

    return pl.pallas_call(
        body, name=name, grid=(SEQ // ROW_TILE, 4),
        in_specs=[pl.BlockSpec((ROW_TILE, FF_TILE), lambda i, j: (i, j % 2)),
                  pl.BlockSpec((ROW_TILE, FF_TILE), lambda i, j: (i, j % 2 + 2)),
                  pl.BlockSpec((ROW_TILE, FF_TILE), lambda i, j: (i, j % 2))],
        out_specs=pl.BlockSpec((ROW_TILE, FF_TILE), lambda i, j: (i, j)),
        out_shape=jax.ShapeDtypeStruct((SEQ, 2 * D_FF), MXU_DTYPE),
    )(gu, gu, ds)


GATE_HG_BLK = 6656 // 512
GATE_ATT_BLK = 7680 // 512


def merge_fwd(z, bh, ba, name):
    def body(gh_ref, ga_ref, bh_ref, ba_ref, o_ref):
        o_ref[...] = (_sigmoid(gh_ref[...]) * bh_ref[...] + _sigmoid(ga_ref[...]) * ba_ref[...]).astype(o_ref.dtype)

    blk = pl.BlockSpec((ROW_TILE, 512), lambda i, j: (i, j))
    return pl.pallas_call(
        body, name=name, grid=(SEQ // ROW_TILE, 2),
        in_specs=[pl.BlockSpec((ROW_TILE, 512), lambda i, j: (i, GATE_HG_BLK + j)),
                  pl.BlockSpec((ROW_TILE, 512), lambda i, j: (i, GATE_ATT_BLK + j)), blk, blk],
        out_specs=blk, out_shape=jax.ShapeDtypeStruct((SEQ, D_MODEL), MXU_DTYPE),
    )(z, z, bh, ba)


def merge_bwd(z, bh, ba, dm, name):
    def body(gh_ref, ga_ref, bh_ref, ba_ref, dm_ref, dbh_ref, dba_ref, dgh_ref, dga_ref):
        dmv = dm_ref[...]
        sh = _sigmoid(gh_ref[...])
        sa = _sigmoid(ga_ref[...])
        dbh_ref[...] = (dmv * sh).astype(dbh_ref.dtype)
        dba_ref[...] = (dmv * sa).astype(dba_ref.dtype)
        dgh_ref[...] = (dmv * bh_ref[...] * (sh * (1.0 - sh))).astype(dgh_ref.dtype)
        dga_ref[...] = (dmv * ba_ref[...] * (sa * (1.0 - sa))).astype(dga_ref.dtype)

    blk = pl.BlockSpec((ROW_TILE, 512), lambda i, j: (i, j))
    out = jax.ShapeDtypeStruct((SEQ, D_MODEL), MXU_DTYPE)
    return pl.pallas_call(
        body, name=name, grid=(SEQ // ROW_TILE, 2),
        in_specs=[pl.BlockSpec((ROW_TILE, 512), lambda i, j: (i, GATE_HG_BLK + j)),
                  pl.BlockSpec((ROW_TILE, 512), lambda i, j: (i, GATE_ATT_BLK + j)), blk, blk, blk],
        out_specs=[blk, blk, blk, blk], out_shape=[out, out, out, out],
    )(z, z, bh, ba, dm)


N_CHUNKS = SEQ // HG_CHUNK


def _hgrn_gates(q, fp, lb):
    C = HG_CHUNK
    sg = _sigmoid(fp)
    f = lb + (1.0 - lb) * sg
    lf = jnp.log(f)
    row = lax.broadcasted_iota(jnp.int32, (C, C), 0)
    col = lax.broadcasted_iota(jnp.int32, (C, C), 1)
    causal = row >= col
    G = _dot_f32(causal.astype(F32), lf)
    eG = jnp.exp(G)
    enG = jnp.exp(-G)
    qg = q * eG
    kg = (1.0 - f) * enG
    A = jnp.where(causal, _hdot(qg, kg, tb=True), 0.0)
    egl = jnp.exp(jnp.sum(lf, axis=0, keepdims=True))
    return sg, f, causal, eG, enG, qg, kg, A, egl


def hgrn_fwd(z, lb, gain, name):
    C, K = HG_CHUNK, HG_DIM

    def body(q_ref, f_ref, v_ref, og_ref, p_ref, g_ref, y_ref, o_ref, st_ref, state):
        @pl.when(pl.program_id(1) == 0)
        def _():
            state[...] = jnp.zeros_like(state)

        v = v_ref[...]
        _, _, _, _, _, qg, kg, A, egl = _hgrn_gates(q_ref[...], f_ref[...], p_ref[...])
        st = state[...]
        st_ref[0, 0] = st
        o = _hdot(A, v) + _hdot(qg, st, tb=True)
        state[...] = st * egl + _hdot(v, kg * egl, ta=True)
        o_ref[...] = o
        rs = lax.rsqrt(jnp.mean(o * o, axis=-1, keepdims=True) + EPS)
        og = og_ref[...]
        y_ref[...] = (((o * rs) * g_ref[...]) * (og * _sigmoid(og))).astype(y_ref.dtype)

    def zcol(section):
        return pl.BlockSpec((C, K), lambda h, c: (c, section * HG_HEADS + h))

    vec = pl.BlockSpec((1, K), lambda h, c: (0, h))
    blk = pl.BlockSpec((C, K), lambda h, c: (c, h))
    return pl.pallas_call(
        body, name=name, grid=(HG_HEADS, N_CHUNKS),
        in_specs=[zcol(0), zcol(1), zcol(2), zcol(3), vec, vec],
        out_specs=[blk, blk, pl.BlockSpec((1, 1, K, K), lambda h, c: (h, c, 0, 0))],
        out_shape=[jax.ShapeDtypeStruct((SEQ, HG_WIDTH), MXU_DTYPE), jax.ShapeDtypeStruct((SEQ, HG_WIDTH), F32),
                   jax.ShapeDtypeStruct((HG_HEADS, N_CHUNKS, K, K), F32)],
        scratch_shapes=[pltpu.VMEM((K, K), F32)],
        compiler_params=pltpu.CompilerParams(dimension_semantics=("parallel", "arbitrary")),
    )(z, z, z, z, lb, gain)


def hgrn_bwd(z, lb, gain, o_raw, states, dy, name):
    C, K = HG_CHUNK, HG_DIM

    def body(q_ref, f_ref, v_ref, og_ref, p_ref, g_ref, o_ref, st_ref, dy_ref,
             dq_ref, dfp_ref, dv_ref, dog_ref, dlb_ref, dgain_ref, dstate):
        @pl.when(pl.program_id(1) == 0)
        def _():
            dstate[...] = jnp.zeros_like(dstate)
            dlb_ref[...] = jnp.zeros_like(dlb_ref)
            dgain_ref[...] = jnp.zeros_like(dgain_ref)

        v = v_ref[...]
        lb = p_ref[...]
        sg, f, causal, eG, enG, qg, kg, A, egl = _hgrn_gates(q_ref[...], f_ref[...], lb)
        kd = kg * egl
        st = st_ref[0, 0]
        dst = dstate[...]
        o = o_ref[...]
        og = og_ref[...]
        gain_v = g_ref[...]
        dyv = dy_ref[...]
        rs = lax.rsqrt(jnp.mean(o * o, axis=-1, keepdims=True) + EPS)
        on = o * rs
        sgo = _sigmoid(og)
        silu = og * sgo
        dog_ref[...] = (dyv * (on * gain_v) * (sgo * (1.0 + og * (1.0 - sgo)))).astype(dog_ref.dtype)
        dgain_ref[...] += jnp.sum(dyv * silu * on, axis=0, keepdims=True)
        don = dyv * gain_v * silu
        do = rs * (don - on * jnp.mean(don * on, axis=-1, keepdims=True))
        dA = jnp.where(causal, _hdot(do, v, tb=True), 0.0)
        dv_ref[...] = (_hdot(A, do, ta=True) + _hdot(kd, dst, tb=True)).astype(dv_ref.dtype)
        dqg = _hdot(dA, kg) + _hdot(do, st)
        dkg = _hdot(dA, qg, ta=True)
        dkd = _hdot(v, dst)
        dstate[...] = dst * egl + _hdot(do, qg, ta=True)
        dgl = jnp.sum(st * dst, axis=0, keepdims=True) * egl
        dq_ref[...] = (dqg * eG).astype(dq_ref.dtype)
        dk = dkg * enG + dkd * (enG * egl)
        dG = dqg * qg - dkg * kg - dkd * kd
        extra = jnp.sum(dkd * kd, axis=0, keepdims=True) + dgl
        last = lax.broadcasted_iota(jnp.int32, (C, K), 0) == C - 1
        dG = dG + jnp.where(last, extra, 0.0)
        row = lax.broadcasted_iota(jnp.int32, (C, C), 0)
        col = lax.broadcasted_iota(jnp.int32, (C, C), 1)
        dlf = _dot_f32((col >= row).astype(F32), dG)
        df = dlf / f - dk
        dfp_ref[...] = (df * (1.0 - lb) * (sg * (1.0 - sg))).astype(dfp_ref.dtype)
        dlb_ref[...] += jnp.sum(df * (1.0 - sg), axis=0, keepdims=True)

    def rc(c):
        return N_CHUNKS - 1 - c

    def zcol(section):
        return pl.BlockSpec((C, K), lambda h, c: (rc(c), section * HG_HEADS + h))

    vec = pl.BlockSpec((1, K), lambda h, c: (0, h))
    blk = pl.BlockSpec((C, K), lambda h, c: (rc(c), h))
    out = jax.ShapeDtypeStruct((SEQ, HG_WIDTH), MXU_DTYPE)
    small = jax.ShapeDtypeStruct((1, HG_WIDTH), F32)
    return pl.pallas_call(
        body, name=name, grid=(HG_HEADS, N_CHUNKS),
        in_specs=[zcol(0), zcol(1), zcol(2), zcol(3), vec, vec, blk,
                  pl.BlockSpec((1, 1, K, K), lambda h, c: (h, rc(c), 0, 0)), blk],
        out_specs=[blk, blk, blk, blk, vec, vec],
        out_shape=[out, out, out, out, small, small],
        scratch_shapes=[pltpu.VMEM((K, K), F32)],
        compiler_params=pltpu.CompilerParams(dimension_semantics=("parallel", "arbitrary")),
    )(z, z, z, z, lb, gain, o_raw, states, dy)


N_GROUPS = len(ATT_GROUPS)
HEAD_PAIRS = ATT_WIDTH // 128
ATT_COL0 = 4 * HG_WIDTH
UNROLLED_SUBSEQS = 4


def _alibi_coef():
    n = N_GROUPS * ATT_HEADS
    slopes = np.exp2(-ALIBI_MAX * np.arange(1, n + 1, dtype=np.float32) / n).astype(np.float32)
    dil = np.repeat(np.array([d for _, d in ATT_GROUPS], np.float32), ATT_HEADS)
    return jnp.asarray(slopes * dil, F32)


def _att_masks():
    B = ATT_BLOCK
    qi = lax.broadcasted_iota(jnp.int32, (B, B), 0)
    kj = lax.broadcasted_iota(jnp.int32, (B, B), 1)
    return qi, kj, (qi - kj).astype(F32), (qi + B - kj).astype(F32)


def _subseq_rows(r, d):
    return pl.ds(r, ATT_BLOCK, stride=d) if d > 1 else pl.ds(0, ATT_BLOCK)


def _for_each_subseq(d, fn):
    if d <= UNROLLED_SUBSEQS:
        for r in range(d):
            fn(r)
    else:
        lax.fori_loop(0, d, lambda r, carry: (fn(r), carry)[1], 0)


def _att_specs(g):
    d = ATT_GROUPS[g][1]
    R = ATT_BLOCK * d
    n_slabs = SEQ // R
    col0 = (ATT_COL0 + g * 3 * ATT_WIDTH) // 128

    def cur(col):
        return pl.BlockSpec((R, 128), lambda hp, s: (s, col + hp))

    def prev(col):
        return pl.BlockSpec((R, 128), lambda hp, s: (jnp.maximum(s - 1, 0), col + hp))

    def nxt(col):
        return pl.BlockSpec((R, 128), lambda hp, s: (jnp.minimum(s + 1, n_slabs - 1), col + hp))

    return d, R, n_slabs, col0, cur, prev, nxt


def _head_lanes(j):
    lane = lax.broadcasted_iota(jnp.int32, (ATT_BLOCK, 128), 1)
    return (lane >= 64 * j) & (lane < 64 * (j + 1))


def _lane_value(x, sel):
    return jnp.max(jnp.where(sel, x, -3e38), axis=-1, keepdims=True)


def att_fwd(z, g, name):
    B = ATT_BLOCK
    d, R, n_slabs, col0, cur, prev, _ = _att_specs(g)
    has_prev = n_slabs > 1

    def body(coef_ref, *refs):
        if has_prev:
            q_ref, kc_ref, vc_ref, kp_ref, vp_ref, o_ref, l_ref = refs
        else:
            q_ref, kc_ref, vc_ref, o_ref, l_ref = refs
        hp, s = pl.program_id(0), pl.program_id(1)
        qi, kj, d_cur, d_prev = _att_masks()
        m_cur = kj <= qi
        m_prev = kj >= qi + jnp.where(s == 0, 4 * B, 0)

        def one(r):
            rows = _subseq_rows(r, d)
            q, kc, vc = q_ref[rows, :], kc_ref[rows, :], vc_ref[rows, :]
            if has_prev:
                kpv, vpv = kp_ref[rows, :], vp_ref[rows, :]
            o_acc = jnp.zeros((B, 128), F32)
            l_acc = jnp.zeros((B, 128), F32)
            for j in range(2):
                sel = _head_lanes(j)
                cf = coef_ref[g * ATT_HEADS + hp * 2 + j]
                qh = jnp.where(sel, q, 0.0)
                s_cur = jnp.where(m_cur, _dot(qh, kc, tb=True) * 0.125 - cf * d_cur, NEG_INF)
                mx = jnp.max(s_cur, axis=-1, keepdims=True)
                if has_prev:
                    s_prev = jnp.where(m_prev, _dot(qh, kpv, tb=True) * 0.125 - cf * d_prev, NEG_INF)
                    mx = jnp.maximum(mx, jnp.max(s_prev, axis=-1, keepdims=True))
                e_cur = jnp.exp(s_cur - mx)
                den = jnp.sum(e_cur, axis=-1, keepdims=True)
                if has_prev:
                    e_prev = jnp.exp(s_prev - mx)
                    den = den + jnp.sum(e_prev, axis=-1, keepdims=True)
                inv = 1.0 / den
                oh = _dot(e_cur * inv, vc)
                if has_prev:
                    oh = oh + _dot(e_prev * inv, vpv)
                o_acc = jnp.where(sel, oh, o_acc)
                l_acc = jnp.where(sel, mx + jnp.log(den), l_acc)
            o_ref[rows, :] = o_acc
            l_ref[rows, :] = l_acc

        _for_each_subseq(d, one)

    in_specs = [pl.BlockSpec(memory_space=pltpu.SMEM), cur(col0), cur(col0 + 4), cur(col0 + 8)]
    args = [_alibi_coef(), z, z, z]
    if has_prev:
        in_specs += [prev(col0 + 4), prev(col0 + 8)]
        args += [z, z]
    out = jax.ShapeDtypeStruct((SEQ, ATT_WIDTH), F32)
    return pl.pallas_call(
        body, name=name, grid=(HEAD_PAIRS, n_slabs), in_specs=in_specs,
        out_specs=[cur(0), cur(0)], out_shape=[out, out],
        compiler_params=pltpu.CompilerParams(dimension_semantics=("parallel", "arbitrary")),
    )(*args)


def att_bwd(z, l, do, corr, g, name):
    B = ATT_BLOCK
    d, R, n_slabs, col0, cur, prev, nxt = _att_specs(g)
    neighbours = n_slabs > 1

    def body(coef_ref, *refs):
        if neighbours:
            (q_ref, kc_ref, vc_ref, l_ref, do_ref, cr_ref, kp_ref, vp_ref, qn_ref, ln_ref, don_ref, crn_ref,
             dq_ref, dk_ref, dv_ref, dq_sc, dk_sc, dv_sc) = refs
        else:
            q_ref, kc_ref, vc_ref, l_ref, do_ref, cr_ref, dq_ref, dk_ref, dv_ref, dq_sc, dk_sc, dv_sc = refs
        hp, s = pl.program_id(0), pl.program_id(1)
        qi, kj, d_cur, d_prev = _att_masks()
        m_cc = kj <= qi
        m_cp = kj >= qi + jnp.where(s == 0, 4 * B, 0)
        m_nc = kj >= qi + jnp.where(s == n_slabs - 1, 4 * B, 0)

        def one(r):
            rows = _subseq_rows(r, d)
            q, kc, vc, lv, dov, crv = (ref[rows, :] for ref in (q_ref, kc_ref, vc_ref, l_ref, do_ref, cr_ref))
            if neighbours:
                kpv, vpv, qn, lnv, donv, crnv = (ref[rows, :] for ref in (kp_ref, vp_ref, qn_ref, ln_ref, don_ref, crn_ref))
            dq_acc = jnp.zeros((B, 128), F32)
            dk_acc = jnp.zeros((B, 128), F32)
            dv_acc = jnp.zeros((B, 128), F32)
            for j in range(2):
                sel = _head_lanes(j)
                cf = coef_ref[g * ATT_HEADS + hp * 2 + j]
                qh = jnp.where(sel, q, 0.0)
                doh = jnp.where(sel, dov, 0.0)
                lse, cr = _lane_value(lv, sel), _lane_value(crv, sel)
                p_cc = jnp.exp(jnp.where(m_cc, _dot(qh, kc, tb=True) * 0.125 - cf * d_cur, NEG_INF) - lse)
                ds_cc = p_cc * (_dot(doh, vc, tb=True) + cr)
                dqh = _dot(ds_cc, kc)
                dkh = _dot(ds_cc, qh, ta=True)
                dvh = _dot(p_cc, doh, ta=True)
                if neighbours:
                    qnh = jnp.where(sel, qn, 0.0)
                    donh = jnp.where(sel, donv, 0.0)
                    lse_n, cr_n = _lane_value(lnv, sel), _lane_value(crnv, sel)
                    p_cp = jnp.exp(jnp.where(m_cp, _dot(qh, kpv, tb=True) * 0.125 - cf * d_prev, NEG_INF) - lse)
                    p_nc = jnp.exp(jnp.where(m_nc, _dot(qnh, kc, tb=True) * 0.125 - cf * d_prev, NEG_INF) - lse_n)
                    ds_cp = p_cp * (_dot(doh, vpv, tb=True) + cr)
                    ds_nc = p_nc * (_dot(donh, vc, tb=True) + cr_n)
                    dqh = dqh + _dot(ds_cp, kpv)
                    dkh = dkh + _dot(ds_nc, qnh, ta=True)
                    dvh = dvh + _dot(p_nc, donh, ta=True)
                dq_acc = jnp.where(sel, dqh * 0.125, dq_acc)
                dk_acc = jnp.where(sel, dkh * 0.125, dk_acc)
                dv_acc = jnp.where(sel, dvh, dv_acc)
            dq_sc[rows, :] = dq_acc
            dk_sc[rows, :] = dk_acc
            dv_sc[rows, :] = dv_acc

        _for_each_subseq(d, one)
        dq_ref[...] = dq_sc[...].astype(dq_ref.dtype)
        dk_ref[...] = dk_sc[...].astype(dk_ref.dtype)
        dv_ref[...] = dv_sc[...].astype(dv_ref.dtype)

    in_specs = [pl.BlockSpec(memory_space=pltpu.SMEM), cur(col0), cur(col0 + 4), cur(col0 + 8), cur(0), cur(0), cur(0)]
    args = [_alibi_coef(), z, z, z, l, do, corr]
    if neighbours:
        in_specs += [prev(col0 + 4), prev(col0 + 8), nxt(col0), nxt(0), nxt(0), nxt(0)]
        args += [z, z, z, l, do, corr]
    out = jax.ShapeDtypeStruct((SEQ, ATT_WIDTH), MXU_DTYPE)
    return pl.pallas_call(
        body, name=name, grid=(HEAD_PAIRS, n_slabs), in_specs=in_specs,
        out_specs=[cur(0)] * 3, out_shape=[out] * 3,
        scratch_shapes=[pltpu.VMEM((R, 128), F32)] * 3,
        compiler_params=pltpu.CompilerParams(dimension_semantics=("parallel", "arbitrary"),
                                             vmem_limit_bytes=MATMUL_VMEM_BYTES),
    )(*args)


def _head_sum(x):
    i = lax.broadcasted_iota(jnp.int32, (128, 128), 0) // 64
    j = lax.broadcasted_iota(jnp.int32, (128, 128), 1) // 64
    return _dot_f32(x, (i == j).astype(F32))


def _group_weights(l0, l1, l2):
    mx = jnp.maximum(jnp.maximum(l0, l1), l2)
    e0, e1, e2 = jnp.exp(l0 - mx), jnp.exp(l1 - mx), jnp.exp(l2 - mx)
    inv = 1.0 / (e0 + e1 + e2)
    return e0 * inv, e1 * inv, e2 * inv


def att_combine_fwd(o, l, name):
    def body(o0, o1, o2, l0, l1, l2, y_ref):
        w0, w1, w2 = _group_weights(l0[...], l1[...], l2[...])
        y_ref[...] = (o0[...] * w0 + o1[...] * w1 + o2[...] * w2).astype(y_ref.dtype)

    blk = pl.BlockSpec((ROW_TILE, ATT_WIDTH), lambda i: (i, 0))
    return pl.pallas_call(
        body, name=name, grid=(SEQ // ROW_TILE,), in_specs=[blk] * 6, out_specs=blk,
        out_shape=jax.ShapeDtypeStruct((SEQ, ATT_WIDTH), MXU_DTYPE),
    )(*o, *l)


def att_combine_bwd(o, l, dy, name):
    def body(o0, o1, o2, l0, l1, l2, dy_ref, do0, do1, do2, cr0, cr1, cr2):
        w = _group_weights(l0[...], l1[...], l2[...])
        dyv = dy_ref[...]
        dw = [_head_sum(dyv * o_ref[...]) for o_ref in (o0, o1, o2)]
        tot = w[0] * dw[0] + w[1] * dw[1] + w[2] * dw[2]
        for g, (do_ref, cr_ref) in enumerate(((do0, cr0), (do1, cr1), (do2, cr2))):
            do_ref[...] = dyv * w[g]
            cr_ref[...] = -w[g] * tot

    blk = pl.BlockSpec((ROW_TILE, 128), lambda i, j: (i, j))
    out = jax.ShapeDtypeStruct((SEQ, ATT_WIDTH), F32)
    res = pl.pallas_call(
        body, name=name, grid=(SEQ // ROW_TILE, HEAD_PAIRS), in_specs=[blk] * 7, out_specs=[blk] * 6, out_shape=[out] * 6,
    )(*o, *l, dy)
    return res[:N_GROUPS], res[N_GROUPS:]


SUM_ROW_TILES = (256, 128, 64, 32, 16)
SUM_TILE_ELEMS = 128 * 1024


def _row_tile(rows, cols):
    fit = [t for t in SUM_ROW_TILES if rows % t == 0]
    return next((t for t in fit if t * cols <= SUM_TILE_ELEMS), fit[-1])


def _shard_shape(rows, cols, axis):
    return (rows // N_CHIPS, cols) if axis == 0 else (rows, cols // N_CHIPS)


def _half_shape(rows, cols, axis):
    return (rows, cols // 2) if axis == 0 else (rows // 2, cols)


def _piece_shape(rows, cols, axis):
    return (rows // N_CHIPS, cols // 2) if axis == 0 else (rows // 2, cols // N_CHIPS)


def place_own_block(shard, chip, rows, cols, axis, name):
    sr, sc = _shard_shape(rows, cols, axis)
    tr = _row_tile(sr, sc)

    def body(chip_ref, s_ref, o_ref):
        o_ref[...] = s_ref[...].astype(o_ref.dtype)

    if axis == 0:
        out_map = lambda i, chip_ref: (chip_ref[0] * (sr // tr) + i, 0)
    else:
        out_map = lambda i, chip_ref: (i, chip_ref[0])
    return pl.pallas_call(
        body, name=name, out_shape=jax.ShapeDtypeStruct((rows, cols), WEIGHT_COMM_DTYPE),
        grid_spec=pltpu.PrefetchScalarGridSpec(
            num_scalar_prefetch=1, grid=(sr // tr,), in_specs=[pl.BlockSpec((tr, sc), lambda i, chip_ref: (i, 0))],
            out_specs=pl.BlockSpec((tr, sc), out_map)),
    )(chip, shard)


def add_halves(g, theirs, core, rows, cols, axis, name):
    hr, hc = _half_shape(rows, cols, axis)
    tr = _row_tile(hr, hc)

    def body(core_ref, g_ref, t_ref, o_ref):
        o_ref[...] = (g_ref[...].astype(F32) + t_ref[...].astype(F32)).astype(o_ref.dtype)

    if axis == 0:
        g_map = lambda i, core_ref: (i, core_ref[0])
    else:
        g_map = lambda i, core_ref: (core_ref[0] * (hr // tr) + i, 0)
    blk = pl.BlockSpec((tr, hc), lambda i, core_ref: (i, 0))
    return pl.pallas_call(
        body, name=name, out_shape=jax.ShapeDtypeStruct((hr, hc), GRAD_COMM_DTYPE),
        grid_spec=pltpu.PrefetchScalarGridSpec(
            num_scalar_prefetch=1, grid=(hr // tr,), in_specs=[pl.BlockSpec((tr, hc), g_map), blk], out_specs=blk),
    )(core, g, theirs)


def add_pieces(half, got, chip, rows, cols, axis, name):
    hr, _ = _half_shape(rows, cols, axis)
    pr, pc = _piece_shape(rows, cols, axis)
    tr = _row_tile(pr, pc)

    def body(chip_ref, h_ref, got_ref, o_ref):
        o_ref[...] = (h_ref[...].astype(F32) + got_ref[0].astype(F32) + got_ref[1].astype(F32) + got_ref[2].astype(F32))

    if axis == 0:
        h_map = lambda i, chip_ref: (chip_ref[0] * (pr // tr) + i, 0)
    else:
        h_map = lambda i, chip_ref: (i, chip_ref[0])
    return pl.pallas_call(
        body, name=name, out_shape=jax.ShapeDtypeStruct((pr, pc), F32),
        grid_spec=pltpu.PrefetchScalarGridSpec(
            num_scalar_prefetch=1, grid=(pr // tr,),
            in_specs=[pl.BlockSpec((tr, pc), h_map), pl.BlockSpec((3, tr, pc), lambda i, chip_ref: (0, i, 0))],
            out_specs=pl.BlockSpec((tr, pc), lambda i, chip_ref: (i, 0))),
    )(chip, half, got)


def _adamw_math(w, g, m, v):
    nm = ADAM_B1 * m + (1.0 - ADAM_B1) * g
    nv = ADAM_B2 * v + (1.0 - ADAM_B2) * (g * g)
    m_hat = nm / (1.0 - ADAM_B1 ** ADAM_STEP)
    v_hat = nv / (1.0 - ADAM_B2 ** ADAM_STEP)
    return -ADAM_LR * (m_hat / (jnp.sqrt(v_hat) + ADAM_EPS) + ADAM_WD * w), nm, nv


def adamw(w, g, m, v, name):
    R, Cc = w.shape
    tr = _pick(R, (256, 128, 64, 8))

    def body(w_ref, g_ref, m_ref, v_ref, d_ref, nm_ref, nv_ref):
        d_ref[...], nm_ref[...], nv_ref[...] = _adamw_math(w_ref[...], g_ref[...], m_ref[...], v_ref[...])

    blk = pl.BlockSpec((tr, Cc), lambda i: (i, 0))
    out = jax.ShapeDtypeStruct((R, Cc), F32)
    return pl.pallas_call(
        body, name=name, grid=(R // tr,), in_specs=[blk] * 4, out_specs=[blk] * 3, out_shape=[out, out, out],
    )(w, g, m, v)


def adamw_halves(w, mine, theirs, m, v, core, rows, cols, axis, name):
    sr, sc = _shard_shape(rows, cols, axis)
    pr, pc = _piece_shape(rows, cols, axis)
    tr = _row_tile(pr, pc)
    nt = pr // tr

    def body(core_ref, w_ref, a_ref, b_ref, m_ref, v_ref, g_ref, d_ref, nm_ref, nv_ref):
        g = jnp.where(pl.program_id(0) == core_ref[0], a_ref[...], b_ref[...])
        g_ref[...] = g
        d_ref[...], nm_ref[...], nv_ref[...] = _adamw_math(w_ref[...], g, m_ref[...], v_ref[...])

    if axis == 0:
        full = pl.BlockSpec((tr, pc), lambda h, i, core_ref: (i, h))
    else:
        full = pl.BlockSpec((tr, pc), lambda h, i, core_ref: (h * nt + i, 0))
    part = pl.BlockSpec((tr, pc), lambda h, i, core_ref: (i, 0))
    out = jax.ShapeDtypeStruct((sr, sc), F32)
    return pl.pallas_call(
        body, name=name, out_shape=[out, out, out, out],
        grid_spec=pltpu.PrefetchScalarGridSpec(
            num_scalar_prefetch=1, grid=(2, nt), in_specs=[full, part, part, full, full], out_specs=[full] * 4),
    )(core, w, mine, theirs, m, v)


BIG = (
    ("ffn1_w_gate_up", D_MODEL, 2 * D_FF, 1),
    ("ffn1_w_down", D_FF, D_MODEL, 0),
    ("w_in", D_MODEL, IN_COLS, 1),
    ("w_branch_hg", HG_WIDTH, D_MODEL, 1),
    ("w_branch_att", ATT_WIDTH, D_MODEL, 1),
    ("w_out", D_MODEL, D_MODEL, 0),
    ("ffn2_w_gate_up", D_MODEL, 2 * D_FF, 1),
    ("ffn2_w_down", D_FF, D_MODEL, 0),
)
N_BIG = len(BIG)
ANY = pl.BlockSpec(memory_space=pl.ANY)


def _place():
    return lax.axis_index("x"), lax.axis_index("y"), lax.axis_index("c")


def _other_chips(x, y):
    return ((1 - x, y), (x, 1 - y), (1 - x, 1 - y))


MAX_COPY_CHUNKS = 16
CHUNK_ROW_ALIGN = 16


def _row_chunks(view):
    rows = view.shape[0]
    n = next(n for n in range(MAX_COPY_CHUNKS, 0, -1) if rows % (CHUNK_ROW_ALIGN * n) == 0 or n == 1)
    step = rows // n
    return [pl.ds(i * step, step) for i in range(n)]


def _remote(src, dst, send_sem, recv_sem, device):
    return pltpu.make_async_remote_copy(src_ref=src, dst_ref=dst, send_sem=send_sem, recv_sem=recv_sem,
                                        device_id=device, device_id_type=MESH)


def _start_remote(src, dst, send_sem, recv_sem, device):
    for rows in _row_chunks(src):
        _remote(src.at[rows, :], dst.at[rows, :], send_sem, recv_sem, device).start()
    return _remote(src, dst, send_sem, recv_sem, device)


def all_gather_weights(placed):
    def piece(ref, rows, cols, axis, chip, c):
        sr, sc = _shard_shape(rows, cols, axis)
        j = 2 * chip[0] + chip[1]
        if axis == 0:
            return ref.at[pl.ds(j * sr + c * (sr // 2), sr // 2), :]
        return ref.at[pl.ds(c * (sr // 2), sr // 2), pl.ds(pl.multiple_of(j * sc, 128), sc)]

    def body(*refs):
        outs = refs[N_BIG:2 * N_BIG]
        send_sems, recv_sems = refs[2 * N_BIG:]
        x, y, c = _place()
        chips = _other_chips(x, y)
        sends = []
        for w, (_, r, cc, ax) in enumerate(BIG):
            mine = piece(outs[w], r, cc, ax, (x, y), c)
            for k, chip in enumerate(chips):
                sends.append(_start_remote(mine, mine, send_sems.at[w, k], recv_sems.at[w, k], (*chip, c)))
        passed = []
        for w, (_, r, cc, ax) in enumerate(BIG):
            for k, chip in enumerate(chips):
                got = piece(outs[w], r, cc, ax, chip, c)
                _remote(got, got, send_sems.at[w, k], recv_sems.at[w, k], (x, y, c)).wait_recv()
                passed.append(_start_remote(got, got, send_sems.at[w, 3 + k], recv_sems.at[w, 3 + k], (x, y, 1 - c)))
        for w, (_, r, cc, ax) in enumerate(BIG):
            for k, chip in enumerate(chips):
                got = piece(outs[w], r, cc, ax, chip, 1 - c)
                _remote(got, got, send_sems.at[w, 3 + k], recv_sems.at[w, 3 + k], (x, y, c)).wait_recv()
        for cp in sends + passed:
            cp.wait_send()

    return pl.pallas_call(
        body, name="all_gather_weights", in_specs=[ANY] * N_BIG, out_specs=[ANY] * N_BIG,
        out_shape=[jax.ShapeDtypeStruct((r, cc), WEIGHT_COMM_DTYPE) for _, r, cc, _ in BIG],
        input_output_aliases={w: w for w in range(N_BIG)},
        scratch_shapes=[pltpu.SemaphoreType.DMA((N_BIG, 6)), pltpu.SemaphoreType.DMA((N_BIG, 6))],
    )(*placed)


def _half(ref, rows, cols, axis, c):
    if axis == 0:
        return ref.at[:, pl.ds(pl.multiple_of(c * (cols // 2), 128), cols // 2)]
    return ref.at[pl.ds(c * (rows // 2), rows // 2), :]


def _piece_of_half(ref, rows, cols, axis, chip):
    j = 2 * chip[0] + chip[1]
    pr, pc = _piece_shape(rows, cols, axis)
    if axis == 0:
        return ref.at[pl.ds(j * pr, pr), :]
    return ref.at[:, pl.ds(pl.multiple_of(j * pc, 128), pc)]


def exchange_halves(grads):
    def body(*refs):
        ins, theirs = refs[:N_BIG], refs[N_BIG:2 * N_BIG]
        send_sems, recv_sems = refs[2 * N_BIG:]
        x, y, c = _place()
        copies = [_start_remote(_half(ins[w], r, cc, ax, 1 - c), theirs[w], send_sems.at[w], recv_sems.at[w], (x, y, 1 - c))
                  for w, (_, r, cc, ax) in enumerate(BIG)]
        for cp in copies:
            cp.wait()

    return pl.pallas_call(
        body, name="exchange_halves", in_specs=[ANY] * N_BIG, out_specs=[ANY] * N_BIG,
        out_shape=[jax.ShapeDtypeStruct(_half_shape(r, cc, ax), GRAD_COMM_DTYPE) for _, r, cc, ax in BIG],
        scratch_shapes=[pltpu.SemaphoreType.DMA((N_BIG,)), pltpu.SemaphoreType.DMA((N_BIG,))],
    )(*grads)


def scatter_pieces(halves):
    def body(*refs):
        ins, got = refs[:N_BIG], refs[N_BIG:2 * N_BIG]
        send_sems, recv_sems = refs[2 * N_BIG:]
        x, y, c = _place()
        copies = []
        for w, (_, r, cc, ax) in enumerate(BIG):
            for k, chip in enumerate(_other_chips(x, y)):
                copies.append(_start_remote(_piece_of_half(ins[w], r, cc, ax, chip), got[w].at[k], send_sems.at[w, k],
                                            recv_sems.at[w, k], (*chip, c)))
        for cp in copies:
            cp.wait()

    return pl.pallas_call(
        body, name="scatter_pieces", in_specs=[ANY] * N_BIG, out_specs=[ANY] * N_BIG,
        out_shape=[jax.ShapeDtypeStruct((3,) + _piece_shape(r, cc, ax), GRAD_COMM_DTYPE) for _, r, cc, ax in BIG],
        scratch_shapes=[pltpu.SemaphoreType.DMA((N_BIG, 3)), pltpu.SemaphoreType.DMA((N_BIG, 3))],
    )(*halves)


def exchange_reduced(pieces):
    def body(*refs):
        ins, theirs = refs[:N_BIG], refs[N_BIG:2 * N_BIG]
        send_sems, recv_sems = refs[2 * N_BIG:]
        x, y, c = _place()
        copies = [_start_remote(ins[w], theirs[w], send_sems.at[w], recv_sems.at[w], (x, y, 1 - c)) for w in range(N_BIG)]
        for cp in copies:
            cp.wait()

    return pl.pallas_call(
        body, name="exchange_reduced", in_specs=[ANY] * N_BIG, out_specs=[ANY] * N_BIG,
        out_shape=[jax.ShapeDtypeStruct(_piece_shape(r, cc, ax), F32) for _, r, cc, ax in BIG],
        scratch_shapes=[pltpu.SemaphoreType.DMA((N_BIG,)), pltpu.SemaphoreType.DMA((N_BIG,))],
    )(*pieces)


N_DEV = 8
SMALL_ROWS = 8


def all_reduce_small(packed):
    def body(x_ref, o_ref, gathered, send_sems, recv_sems):
        x, y, c = _place()
        me = 4 * x + 2 * y + c
        gathered[me] = x_ref[...]
        copies = []
        for k in range(1, N_DEV):
            peer = (x ^ (k >> 2), y ^ ((k >> 1) & 1), c ^ (k & 1))
            cp = pltpu.make_async_remote_copy(
                src_ref=x_ref, dst_ref=gathered.at[me], send_sem=send_sems.at[k - 1], recv_sem=recv_sems.at[k - 1],
                device_id=peer, device_id_type=MESH)
            cp.start()
            copies.append(cp)
        for cp in copies:
            cp.wait()
        acc = gathered[0]
        for k in range(1, N_DEV):
            acc = acc + gathered[k]
        o_ref[...] = acc

    vm = pl.BlockSpec(memory_space=pltpu.VMEM)
    return pl.pallas_call(
        body, name="all_reduce_small", in_specs=[vm], out_specs=vm,
        out_shape=jax.ShapeDtypeStruct((SMALL_ROWS, D_MODEL), F32),
        scratch_shapes=[pltpu.VMEM((N_DEV, SMALL_ROWS, D_MODEL), F32), pltpu.SemaphoreType.DMA((N_DEV - 1,)),
                        pltpu.SemaphoreType.DMA((N_DEV - 1,))],
    )(packed)


def _swiglu_block_fwd(h, norm_g, w_gu, w_down, tag):
    n = rmsnorm_fwd(h, norm_g, f"{tag}_norm")
    gu = matmul(n, w_gu, name=f"{tag}_gate_up")
    s = swiglu_fwd(gu, f"{tag}_swiglu")
    h_out = matmul(s, w_down, res=h, scale=0.5, name=f"{tag}_down")
    return h_out, (n, gu, s)


def _swiglu_block_bwd(h, norm_g, w_gu, w_down, saved, dh_out, tag):
    n, gu, s = saved
    df = dh_out.astype(MXU_DTYPE)
    d_down = matmul(s, df, ta=True, scale=0.5, out_dtype=GRAD_COMM_DTYPE, name=f"{tag}_d_w_down")
    ds = matmul(df, w_down, tb=True, scale=0.5, name=f"{tag}_d_s")
    dgu = swiglu_bwd(gu, ds, f"{tag}_swiglu_bwd")
    d_gu = matmul(n, dgu, ta=True, out_dtype=GRAD_COMM_DTYPE, name=f"{tag}_d_w_gate_up")
    dn = matmul(dgu, w_gu, tb=True, name=f"{tag}_d_n")
    dh, dg = rmsnorm_bwd(h, norm_g, dn, dh_out, f"{tag}_norm_bwd")
    return dh, dg, d_gu, d_down


def local_step(x, target, small, big):
    h1, saved1 = _swiglu_block_fwd(x, small["ffn1_norm"], big["ffn1_w_gate_up"], big["ffn1_w_down"], "ffn1")
    u = rmsnorm_fwd(h1, small["mix_norm"], "mix_norm")
    z = matmul(u, big["w_in"], name="w_in")
    p = small["hg_lower_bounds"]
    lb = 1.0 / (1.0 + jnp.exp(p[1:2] - p[0:1]))
    y_hg, o_raw, states = hgrn_fwd(z, lb, small["hg_out_norm"], "hgrn_fwd")
    o_att, l_att = zip(*[att_fwd(z, g, f"att_fwd_{g}") for g in range(N_GROUPS)])
    y_att = att_combine_fwd(o_att, l_att, "att_combine")
    bh = matmul(y_hg, big["w_branch_hg"], name="branch_hg")
    ba = matmul(y_att, big["w_branch_att"], name="branch_att")
    merged = merge_fwd(z, bh, ba, "merge")
    h2 = matmul(merged, big["w_out"], res=h1, name="w_out")
    h3, saved2 = _swiglu_block_fwd(h2, small["ffn2_norm"], big["ffn2_w_gate_up"], big["ffn2_w_down"], "ffn2")
    dh3, d_final, loss = final_norm_loss(h3, small["final_norm"], target, "final_norm_loss")

    gs, gb = {"final_norm": d_final}, {}
    dh2, gs["ffn2_norm"], gb["ffn2_w_gate_up"], gb["ffn2_w_down"] = _swiglu_block_bwd(
        h2, small["ffn2_norm"], big["ffn2_w_gate_up"], big["ffn2_w_down"], saved2, dh3, "ffn2")
    dh2_m = dh2.astype(MXU_DTYPE)
    gb["w_out"] = matmul(merged, dh2_m, ta=True, out_dtype=GRAD_COMM_DTYPE, name="d_w_out")
    dmerged = matmul(dh2_m, big["w_out"], tb=True, name="d_merged")
    dbh, dba, dgh, dga = merge_bwd(z, bh, ba, dmerged, "merge_bwd")
    gb["w_branch_hg"] = matmul(y_hg, dbh, ta=True, out_dtype=GRAD_COMM_DTYPE, name="d_w_branch_hg")
    gb["w_branch_att"] = matmul(y_att, dba, ta=True, out_dtype=GRAD_COMM_DTYPE, name="d_w_branch_att")
    dy_hg = matmul(dbh, big["w_branch_hg"], tb=True, name="d_y_hg")
    dy_att = matmul(dba, big["w_branch_att"], tb=True, name="d_y_att")
    dq, dfp, di, dog, d_lb, gs["hg_out_norm"] = hgrn_bwd(z, lb, small["hg_out_norm"], o_raw, states, dy_hg, "hgrn_bwd")
    do_att, corr = att_combine_bwd(o_att, l_att, dy_att, "att_combine_bwd")
    d_att = [part for g in range(N_GROUPS) for part in att_bwd(z, l_att[g], do_att[g], corr[g], g, f"att_bwd_{g}")]
    dz = jnp.concatenate([dq, dfp, di, dog, *d_att, dgh, dga], axis=1)
    gb["w_in"] = matmul(u, dz, ta=True, out_dtype=GRAD_COMM_DTYPE, name="d_w_in")
    du = matmul(dz, big["w_in"], tb=True, name="d_u")
    dh1, gs["mix_norm"] = rmsnorm_bwd(h1, small["mix_norm"], du, dh2, "mix_norm_bwd")
    dp0 = d_lb * lb * (1.0 - lb)
    gs["hg_lower_bounds"] = jnp.concatenate([dp0, -dp0], axis=0)
    dx, gs["ffn1_norm"], gb["ffn1_w_gate_up"], gb["ffn1_w_down"] = _swiglu_block_bwd(
        x, small["ffn1_norm"], big["ffn1_w_gate_up"], big["ffn1_w_down"], saved1, dh1, "ffn1")
    return loss[0, 0], dx, gs, gb


SMALL = ("ffn1_norm", "mix_norm", "hg_lower_bounds", "hg_out_norm", "ffn2_norm", "final_norm")
WEIGHTS = ("ffn1_norm", "ffn1_w_gate_up", "ffn1_w_down", "mix_norm", "w_in", "hg_lower_bounds", "hg_out_norm",
           "w_branch_hg", "w_branch_att", "w_out", "ffn2_norm", "ffn2_w_gate_up", "ffn2_w_down", "final_norm")
SMALL_SHAPE = {"ffn1_norm": (1, 1024), "mix_norm": (1, 1024), "hg_lower_bounds": (2, 512), "hg_out_norm": (1, 512),
               "ffn2_norm": (1, 1024), "final_norm": (1024,)}
LOSS_ROW = 6


def _pack_small(vals):
    rows = []
    for n in SMALL:
        r = vals[n].reshape(1, -1).astype(F32)
        rows.append(jnp.pad(r, ((0, 0), (0, D_MODEL - r.shape[1]))))
    rows.append(jnp.zeros((SMALL_ROWS - len(SMALL), D_MODEL), F32))
    return jnp.concatenate(rows, axis=0)


def _unpack_small(packed):
    out = {}
    for i, n in enumerate(SMALL):
        size = int(np.prod(SMALL_SHAPE[n]))
        out[n] = packed[i, :size].reshape(SMALL_SHAPE[n])
    return out


def reduce_big(grads, core, chip):
    theirs = exchange_halves([grads[n] for n, *_ in BIG])
    halves = [add_halves(grads[n], t, core, r, cc, ax, f"add_halves_{n}") for (n, r, cc, ax), t in zip(BIG, theirs)]
    got = scatter_pieces(halves)
    mine = [add_pieces(h, g, chip, r, cc, ax, f"add_pieces_{n}") for (n, r, cc, ax), h, g in zip(BIG, halves, got)]
    return mine, exchange_reduced(mine)


def kernel(x, ffn1_norm, ffn1_w_gate_up, ffn1_w_down, mix_norm, w_in, hg_lower_bounds, hg_out_norm, w_branch_hg, w_branch_att, w_out, ffn2_norm, ffn2_w_gate_up, ffn2_w_down, final_norm, loss_target, m_ffn1_norm, m_ffn1_w_gate_up, m_ffn1_w_down, m_mix_norm, m_w_in, m_hg_lower_bounds, m_hg_out_norm, m_w_branch_hg, m_w_branch_att, m_w_out, m_ffn2_norm, m_ffn2_w_gate_up, m_ffn2_w_down, m_final_norm, v_ffn1_norm, v_ffn1_w_gate_up, v_ffn1_w_down, v_mix_norm, v_w_in, v_hg_lower_bounds, v_hg_out_norm, v_w_branch_hg, v_w_branch_att, v_w_out, v_ffn2_norm, v_ffn2_w_gate_up, v_ffn2_w_down, v_final_norm):
    w = dict(ffn1_norm=ffn1_norm, ffn1_w_gate_up=ffn1_w_gate_up, ffn1_w_down=ffn1_w_down, mix_norm=mix_norm, w_in=w_in,
             hg_lower_bounds=hg_lower_bounds, hg_out_norm=hg_out_norm, w_branch_hg=w_branch_hg, w_branch_att=w_branch_att,
             w_out=w_out, ffn2_norm=ffn2_norm, ffn2_w_gate_up=ffn2_w_gate_up, ffn2_w_down=ffn2_w_down, final_norm=final_norm)
    m = dict(ffn1_norm=m_ffn1_norm, ffn1_w_gate_up=m_ffn1_w_gate_up, ffn1_w_down=m_ffn1_w_down, mix_norm=m_mix_norm,
             w_in=m_w_in, hg_lower_bounds=m_hg_lower_bounds, hg_out_norm=m_hg_out_norm, w_branch_hg=m_w_branch_hg,
             w_branch_att=m_w_branch_att, w_out=m_w_out, ffn2_norm=m_ffn2_norm, ffn2_w_gate_up=m_ffn2_w_gate_up,
             ffn2_w_down=m_ffn2_w_down, final_norm=m_final_norm)
    v = dict(ffn1_norm=v_ffn1_norm, ffn1_w_gate_up=v_ffn1_w_gate_up, ffn1_w_down=v_ffn1_w_down, mix_norm=v_mix_norm,
             w_in=v_w_in, hg_lower_bounds=v_hg_lower_bounds, hg_out_norm=v_hg_out_norm, w_branch_hg=v_w_branch_hg,
             w_branch_att=v_w_branch_att, w_out=v_w_out, ffn2_norm=v_ffn2_norm, ffn2_w_gate_up=v_ffn2_w_gate_up,
             ffn2_w_down=v_ffn2_w_down, final_norm=v_final_norm)

    core = lax.axis_index("c").astype(jnp.int32).reshape(1)
    chip = (2 * lax.axis_index("x") + lax.axis_index("y")).astype(jnp.int32).reshape(1)
    gathered = all_gather_weights([place_own_block(w[n][0], chip, r, cc, ax, f"place_{n}") for n, r, cc, ax in BIG])
    big = {n: a for (n, *_), a in zip(BIG, gathered)}
    small = {n: w[n] for n in SMALL}
    small["final_norm"] = final_norm.reshape(1, D_MODEL)

    loss, dx, gs, gb = local_step(x[0], loss_target[0], small, big)

    packed = _pack_small(gs)
    packed = packed.at[LOSS_ROW].set(jnp.full((D_MODEL,), loss, F32))
    total = all_reduce_small(packed)
    grads = _unpack_small(total)
    loss_total = total[LOSS_ROW, 0]
    mine, theirs = reduce_big(gb, core, chip)

    delta, new_m, new_v = {}, {}, {}
    pd, pm, pv = adamw(_pack_small({n: w[n] for n in SMALL}), total.at[LOSS_ROW].set(0.0),
                       _pack_small({n: m[n] for n in SMALL}), _pack_small({n: v[n] for n in SMALL}), "adamw_small")
    delta.update(_unpack_small(pd))
    new_m.update(_unpack_small(pm))
    new_v.update(_unpack_small(pv))
    for (n, r, cc, ax), a, b in zip(BIG, mine, theirs):
        g, d, nm, nv = adamw_halves(w[n][0], a, b, m[n][0], v[n][0], core, r, cc, ax, f"adamw_{n}")
        grads[n], delta[n], new_m[n], new_v[n] = g[None], d[None], nm[None], nv[None]

    return (loss_total, dx[None], *[grads[n] for n in WEIGHTS], *[delta[n] for n in WEIGHTS],
            *[new_m[n] for n in WEIGHTS], *[new_v[n] for n in WEIGHTS])
```

```python
import numpy as np
import jax
import jax.numpy as jnp
from jax import lax
from jax.experimental import pallas as pl
from jax.experimental.pallas import tpu as pltpu

SEQ = 2048
D_MODEL = 1024
D_FF = 2816
HG_HEADS = 4
HG_DIM = 128
HG_WIDTH = 512
HG_CHUNK = 64
ATT_GROUPS = ((128, 1), (512, 4), (2048, 16))
ATT_HEADS = 8
ATT_WIDTH = 512
ATT_BLOCK = 128
ALIBI_MAX = 8.0
IN_COLS = 8704
EPS = 1e-6
NEG_INF = -1e30
ADAM_LR = 0.001
ADAM_B1 = 0.9
ADAM_B2 = 0.999
ADAM_EPS = 1e-08
ADAM_WD = 0.01
ADAM_STEP = 10

N_CHIPS = 4
MXU_DTYPE = jnp.bfloat16
HG_DOT_DTYPE = jnp.float32
WEIGHT_COMM_DTYPE = jnp.bfloat16
GRAD_COMM_DTYPE = jnp.bfloat16
MESH = pl.DeviceIdType.MESH
F32 = jnp.float32
HIGHEST = lax.Precision.HIGHEST


def _pick(n, cands):
    for c in cands:
        if n % c == 0:
            return c
    return n


def _sigmoid(x):
    return 1.0 / (1.0 + jnp.exp(-x))


def _dot(a, b, ta=False, tb=False):
    dn = (((0 if ta else 1,), (1 if tb else 0,)), ((), ()))
    return lax.dot_general(a.astype(MXU_DTYPE), b.astype(MXU_DTYPE), dn, preferred_element_type=F32)


def _dot_f32(a, b):
    return jnp.dot(a, b, precision=HIGHEST, preferred_element_type=F32)


def _hdot(a, b, ta=False, tb=False):
    if HG_DOT_DTYPE == F32:
        dn = (((0 if ta else 1,), (1 if tb else 0,)), ((), ()))
        return lax.dot_general(a, b, dn, precision=HIGHEST, preferred_element_type=F32)
    return _dot(a, b, ta, tb)


MATMUL_VMEM_BYTES = 48 * 1024 * 1024
MATMUL_TILE_BYTES = 36 * 1024 * 1024
MXU_ALIGN = 128


def _divisors(n, most):
    return [t for t in range(min(n, most), 0, -MXU_ALIGN) if n % t == 0 and t % MXU_ALIGN == 0]


def _matmul_tiles(M, N, K, in_bytes, out_bytes, has_res):
    best = None
    for tk in _divisors(K, K):
        nk = K // tk
        for tm in _divisors(M, 2048):
            for tn in _divisors(N, 512):
                tiles = 2 * in_bytes * (tm * tk + tk * tn) + 2 * out_bytes * tm * tn
                tiles += 4 * tm * tn * ((nk > 1) + 2 * has_res)
                if tiles > MATMUL_TILE_BYTES:
                    continue
                traffic = in_bytes * (M * K * (1 if nk == 1 else N // tn) + K * N * (M // tm))
                key = (traffic, -tm * tn * tk)
                if best is None or key < best[0]:
                    best = (key, (tm, tn, tk))
    return best[1]


def matmul(a, b, *, ta=False, tb=False, out_dtype=F32, res=None, scale=1.0, name):
    if ta:
        K, M = a.shape
    else:
        M, K = a.shape
    if tb:
        N, K2 = b.shape
    else:
        K2, N = b.shape
    assert K == K2 and a.dtype == b.dtype
    tm, tn, tk = _matmul_tiles(M, N, K, a.dtype.itemsize, jnp.dtype(out_dtype).itemsize, res is not None)
    nk = K // tk

    def finish(r, r_ref, o_ref):
        if scale != 1.0:
            r = r * scale
        if res is not None:
            r = r_ref[...] + r
        o_ref[...] = r.astype(out_dtype)

    def body(*refs):
        a_ref, b_ref = refs[:2]
        r_ref = refs[2] if res is not None else None
        o_ref = refs[3] if res is not None else refs[2]
        if nk == 1:
            finish(_dot(a_ref[...], b_ref[...], ta, tb), r_ref, o_ref)
            return
        acc = refs[-1]
        k = pl.program_id(2)

        @pl.when(k == 0)
        def _():
            acc[...] = jnp.zeros_like(acc)

        acc[...] += _dot(a_ref[...], b_ref[...], ta, tb)

        @pl.when(k == nk - 1)
        def _():
            finish(acc[...], r_ref, o_ref)

    a_spec = pl.BlockSpec((tk, tm), lambda i, j, k: (k, i)) if ta else pl.BlockSpec((tm, tk), lambda i, j, k: (i, k))
    b_spec = pl.BlockSpec((tn, tk), lambda i, j, k: (j, k)) if tb else pl.BlockSpec((tk, tn), lambda i, j, k: (k, j))
    in_specs = [a_spec, b_spec]
    args = [a, b]
    if res is not None:
        in_specs.append(pl.BlockSpec((tm, tn), lambda i, j, k: (i, j)))
        args.append(res)
    return pl.pallas_call(
        body, name=name, grid=(M // tm, N // tn, nk), in_specs=in_specs,
        out_specs=pl.BlockSpec((tm, tn), lambda i, j, k: (i, j)),
        out_shape=jax.ShapeDtypeStruct((M, N), out_dtype),
        scratch_shapes=[pltpu.VMEM((tm, tn), F32)] if nk > 1 else [],
        compiler_params=pltpu.CompilerParams(dimension_semantics=("parallel", "parallel", "arbitrary"),
                                             vmem_limit_bytes=MATMUL_VMEM_BYTES),
    )(*args)


ROW_TILE = 256


def rmsnorm_fwd(x, g, name):
    def body(x_ref, g_ref, n_ref):
        xv = x_ref[...]
        r = lax.rsqrt(jnp.mean(xv * xv, axis=-1, keepdims=True) + EPS)
        n_ref[...] = ((xv * r) * g_ref[...]).astype(n_ref.dtype)

    return pl.pallas_call(
        body, name=name, grid=(SEQ // ROW_TILE,),
        in_specs=[pl.BlockSpec((ROW_TILE, D_MODEL), lambda i: (i, 0)), pl.BlockSpec((1, D_MODEL), lambda i: (0, 0))],
        out_specs=pl.BlockSpec((ROW_TILE, D_MODEL), lambda i: (i, 0)),
        out_shape=jax.ShapeDtypeStruct((SEQ, D_MODEL), MXU_DTYPE),
    )(x, g)


def rmsnorm_bwd(x, g, dn, dres, name):
    def body(x_ref, g_ref, dn_ref, dr_ref, dx_ref, dg_ref):
        xv = x_ref[...]
        r = lax.rsqrt(jnp.mean(xv * xv, axis=-1, keepdims=True) + EPS)
        xh = xv * r
        dnv = dn_ref[...]

        @pl.when(pl.program_id(0) == 0)
        def _():
            dg_ref[...] = jnp.zeros_like(dg_ref)

        dg_ref[...] += jnp.sum(dnv * xh, axis=0, keepdims=True)
        dxh = dnv * g_ref[...]
        dx_ref[...] = dr_ref[...] + r * (dxh - xh * jnp.mean(dxh * xh, axis=-1, keepdims=True))

    row = pl.BlockSpec((ROW_TILE, D_MODEL), lambda i: (i, 0))
    vec = pl.BlockSpec((1, D_MODEL), lambda i: (0, 0))
    return pl.pallas_call(
        body, name=name, grid=(SEQ // ROW_TILE,), in_specs=[row, vec, row, row], out_specs=[row, vec],
        out_shape=[jax.ShapeDtypeStruct((SEQ, D_MODEL), F32), jax.ShapeDtypeStruct((1, D_MODEL), F32)],
        compiler_params=pltpu.CompilerParams(dimension_semantics=("arbitrary",)),
    )(x, g, dn, dres)


def final_norm_loss(h, g, target, name):
    def body(h_ref, g_ref, t_ref, dh_ref, dg_ref, loss_ref):
        xv = h_ref[...]
        r = lax.rsqrt(jnp.mean(xv * xv, axis=-1, keepdims=True) + EPS)
        xh = xv * r
        gv = g_ref[...]
        e = xh * gv - t_ref[...]

        @pl.when(pl.program_id(0) == 0)
        def _():
            dg_ref[...] = jnp.zeros_like(dg_ref)
            loss_ref[...] = jnp.zeros_like(loss_ref)

        part = 0.5 * jnp.sum(jnp.sum(e * e, axis=-1, keepdims=True) * (1.0 / D_MODEL), axis=0, keepdims=True)
        loss_ref[...] += jnp.broadcast_to(part, loss_ref.shape)
        dout = e * (1.0 / D_MODEL)
        dg_ref[...] += jnp.sum(dout * xh, axis=0, keepdims=True)
        dxh = dout * gv
        dh_ref[...] = r * (dxh - xh * jnp.mean(dxh * xh, axis=-1, keepdims=True))

    row = pl.BlockSpec((ROW_TILE, D_MODEL), lambda i: (i, 0))
    vec = pl.BlockSpec((1, D_MODEL), lambda i: (0, 0))
    return pl.pallas_call(
        body, name=name, grid=(SEQ // ROW_TILE,), in_specs=[row, vec, row],
        out_specs=[row, vec, pl.BlockSpec((8, 128), lambda i: (0, 0))],
        out_shape=[jax.ShapeDtypeStruct((SEQ, D_MODEL), F32), jax.ShapeDtypeStruct((1, D_MODEL), F32),
                   jax.ShapeDtypeStruct((8, 128), F32)],
        compiler_params=pltpu.CompilerParams(dimension_semantics=("arbitrary",)),
    )(h, g, target)


FF_TILE = D_FF // 2


def swiglu_fwd(gu, name):
    def body(a_ref, b_ref, s_ref):
        a = a_ref[...]
        s_ref[...] = (a * _sigmoid(a) * b_ref[...]).astype(s_ref.dtype)

    return pl.pallas_call(
        body, name=name, grid=(SEQ // ROW_TILE, 2),
        in_specs=[pl.BlockSpec((ROW_TILE, FF_TILE), lambda i, j: (i, j)),
                  pl.BlockSpec((ROW_TILE, FF_TILE), lambda i, j: (i, j + 2))],
        out_specs=pl.BlockSpec((ROW_TILE, FF_TILE), lambda i, j: (i, j)),
        out_shape=jax.ShapeDtypeStruct((SEQ, D_FF), MXU_DTYPE),
    )(gu, gu)


def swiglu_bwd(gu, ds, name):
    def body(a_ref, b_ref, ds_ref, o_ref):
        a = a_ref[...]
        sg = _sigmoid(a)
        dsv = ds_ref[...]

        @pl.when(pl.program_id(1) < 2)
        def _():
            o_ref[...] = (dsv * b_ref[...] * (sg * (1.0 + a * (1.0 - sg)))).astype(o_ref.dtype)

        @pl.when(pl.program_id(1) >= 2)
        def _():
            o_ref[...] = (dsv * a * sg).astype(o_ref.dtype)

    return pl.pallas_call(
        body, name=name, grid=(SEQ // ROW_TILE, 4),
        in_specs=[pl.BlockSpec((ROW_TILE, FF_TILE), lambda i, j: (i, j % 2)),
                  pl.BlockSpec((ROW_TILE, FF_TILE), lambda i, j: (i, j % 2 + 2)),
                  pl.BlockSpec((ROW_TILE, FF_TILE), lambda i, j: (i, j % 2))],
        out_specs=pl.BlockSpec((ROW_TILE, FF_TILE), lambda i, j: (i, j)),
        out_shape=jax.ShapeDtypeStruct((SEQ, 2 * D_FF), MXU_DTYPE),
    )(gu, gu, ds)


GATE_HG_BLK = 6656 // 512
GATE_ATT_BLK = 7680 // 512


def merge_fwd(z, bh, ba, name):
    def body(gh_ref, ga_ref, bh_ref, ba_ref, o_ref):
        o_ref[...] = (_sigmoid(gh_ref[...]) * bh_ref[...] + _sigmoid(ga_ref[...]) * ba_ref[...]).astype(o_ref.dtype)

    blk = pl.BlockSpec((ROW_TILE, 512), lambda i, j: (i, j))
    return pl.pallas_call(
        body, name=name, grid=(SEQ // ROW_TILE, 2),
        in_specs=[pl.BlockSpec((ROW_TILE, 512), lambda i, j: (i, GATE_HG_BLK + j)),
                  pl.BlockSpec((ROW_TILE, 512), lambda i, j: (i, GATE_ATT_BLK + j)), blk, blk],
        out_specs=blk, out_shape=jax.ShapeDtypeStruct((SEQ, D_MODEL), MXU_DTYPE),
    )(z, z, bh, ba)


def merge_bwd(z, bh, ba, dm, name):
    def body(gh_ref, ga_ref, bh_ref, ba_ref, dm_ref, dbh_ref, dba_ref, dgh_ref, dga_ref):
        dmv = dm_ref[...]
        sh = _sigmoid(gh_ref[...])
        sa = _sigmoid(ga_ref[...])
        dbh_ref[...] = (dmv * sh).astype(dbh_ref.dtype)
        dba_ref[...] = (dmv * sa).astype(dba_ref.dtype)
        dgh_ref[...] = (dmv * bh_ref[...] * (sh * (1.0 - sh))).astype(dgh_ref.dtype)
        dga_ref[...] = (dmv * ba_ref[...] * (sa * (1.0 - sa))).astype(dga_ref.dtype)

    blk = pl.BlockSpec((ROW_TILE, 512), lambda i, j: (i, j))
    out = jax.ShapeDtypeStruct((SEQ, D_MODEL), MXU_DTYPE)
    return pl.pallas_call(
        body, name=name, grid=(SEQ // ROW_TILE, 2),
        in_specs=[pl.BlockSpec((ROW_TILE, 512), lambda i, j: (i, GATE_HG_BLK + j)),
                  pl.BlockSpec((ROW_TILE, 512), lambda i, j: (i, GATE_ATT_BLK + j)), blk, blk, blk],
        out_specs=[blk, blk, blk, blk], out_shape=[out, out, out, out],
    )(z, z, bh, ba, dm)


N_CHUNKS = SEQ // HG_CHUNK


def _hgrn_gates(q, fp, lb):
    C = HG_CHUNK
    sg = _sigmoid(fp)
    f = lb + (1.0 - lb) * sg
    lf = jnp.log(f)
    row = lax.broadcasted_iota(jnp.int32, (C, C), 0)
    col = lax.broadcasted_iota(jnp.int32, (C, C), 1)
    causal = row >= col
    G = _dot_f32(causal.astype(F32), lf)
    eG = jnp.exp(G)
    enG = jnp.exp(-G)
    qg = q * eG
    kg = (1.0 - f) * enG
    A = jnp.where(causal, _hdot(qg, kg, tb=True), 0.0)
    egl = jnp.exp(jnp.sum(lf, axis=0, keepdims=True))
    return sg, f, causal, eG, enG, qg, kg, A, egl


def hgrn_fwd(z, lb, gain, name):
    C, K = HG_CHUNK, HG_DIM

    def body(q_ref, f_ref, v_ref, og_ref, p_ref, g_ref, y_ref, o_ref, st_ref, state):
        @pl.when(pl.program_id(0) == 0)
        def _():
            state[...] = jnp.zeros_like(state)

        for h in range(HG_HEADS):
            hd = pl.ds(h * K, K)
            v = v_ref[:, hd]
            _, _, _, _, _, qg, kg, A, egl = _hgrn_gates(q_ref[:, hd], f_ref[:, hd], p_ref[:, hd])
            st = state[h]
            st_ref[h, 0] = st
            o = _hdot(A, v) + _hdot(qg, st, tb=True)
            state[h] = st * egl + _hdot(v, kg * egl, ta=True)
            o_ref[:, hd] = o
            rs = lax.rsqrt(jnp.mean(o * o, axis=-1, keepdims=True) + EPS)
            og = og_ref[:, hd]
            y_ref[:, hd] = (((o * rs) * g_ref[:, hd]) * (og * _sigmoid(og))).astype(y_ref.dtype)

    def zcol(section):
        return pl.BlockSpec((C, HG_WIDTH), lambda c: (c, section))

    vec = pl.BlockSpec((1, HG_WIDTH), lambda c: (0, 0))
    blk = pl.BlockSpec((C, HG_WIDTH), lambda c: (c, 0))
    return pl.pallas_call(
        body, name=name, grid=(N_CHUNKS,),
        in_specs=[zcol(0), zcol(1), zcol(2), zcol(3), vec, vec],
        out_specs=[blk, blk, pl.BlockSpec((HG_HEADS, 1, K, K), lambda c: (0, c, 0, 0))],
        out_shape=[jax.ShapeDtypeStruct((SEQ, HG_WIDTH), MXU_DTYPE), jax.ShapeDtypeStruct((SEQ, HG_WIDTH), F32),
                   jax.ShapeDtypeStruct((HG_HEADS, N_CHUNKS, K, K), F32)],
        scratch_shapes=[pltpu.VMEM((HG_HEADS, K, K), F32)],
        compiler_params=pltpu.CompilerParams(dimension_semantics=("arbitrary",)),
    )(z, z, z, z, lb, gain)


def hgrn_bwd(z, lb, gain, o_raw, states, dy, name):
    C, K = HG_CHUNK, HG_DIM

    def body(q_ref, f_ref, v_ref, og_ref, p_ref, g_ref, o_ref, st_ref, dy_ref,
             dq_ref, dfp_ref, dv_ref, dog_ref, dlb_ref, dgain_ref, dstate):
        @pl.when(pl.program_id(0) == 0)
        def _():
            dstate[...] = jnp.zeros_like(dstate)
            dlb_ref[...] = jnp.zeros_like(dlb_ref)
            dgain_ref[...] = jnp.zeros_like(dgain_ref)

        last = lax.broadcasted_iota(jnp.int32, (C, K), 0) == C - 1
        row = lax.broadcasted_iota(jnp.int32, (C, C), 0)
        col = lax.broadcasted_iota(jnp.int32, (C, C), 1)
        anti_causal = (col >= row).astype(F32)
        for h in range(HG_HEADS):
            hd = pl.ds(h * K, K)
            v = v_ref[:, hd]
            lb = p_ref[:, hd]
            sg, f, causal, eG, enG, qg, kg, A, egl = _hgrn_gates(q_ref[:, hd], f_ref[:, hd], lb)
            kd = kg * egl
            st = st_ref[h, 0]
            dst = dstate[h]
            o = o_ref[:, hd]
            og = og_ref[:, hd]
            gain_v = g_ref[:, hd]
            dyv = dy_ref[:, hd]
            rs = lax.rsqrt(jnp.mean(o * o, axis=-1, keepdims=True) + EPS)
            on = o * rs
            sgo = _sigmoid(og)
            silu = og * sgo
            dog_ref[:, hd] = (dyv * (on * gain_v) * (sgo * (1.0 + og * (1.0 - sgo)))).astype(dog_ref.dtype)
            dgain_ref[:, hd] += jnp.sum(dyv * silu * on, axis=0, keepdims=True)
            don = dyv * gain_v * silu
            do = rs * (don - on * jnp.mean(don * on, axis=-1, keepdims=True))
            dA = jnp.where(causal, _hdot(do, v, tb=True), 0.0)
            dv_ref[:, hd] = (_hdot(A, do, ta=True) + _hdot(kd, dst, tb=True)).astype(dv_ref.dtype)
            dqg = _hdot(dA, kg) + _hdot(do, st)
            dkg = _hdot(dA, qg, ta=True)
            dkd = _hdot(v, dst)
            dstate[h] = dst * egl + _hdot(do, qg, ta=True)
            dgl = jnp.sum(st * dst, axis=0, keepdims=True) * egl
            dq_ref[:, hd] = (dqg * eG).astype(dq_ref.dtype)
            dk = dkg * enG + dkd * (enG * egl)
            dG = dqg * qg - dkg * kg - dkd * kd
            extra = jnp.sum(dkd * kd, axis=0, keepdims=True) + dgl
            dG = dG + jnp.where(last, extra, 0.0)
            dlf = _dot_f32(anti_causal, dG)
            df = dlf / f - dk
            dfp_ref[:, hd] = (df * (1.0 - lb) * (sg * (1.0 - sg))).astype(dfp_ref.dtype)
            dlb_ref[:, hd] += jnp.sum(df * (1.0 - sg), axis=0, keepdims=True)

    def rc(c):
        return N_CHUNKS - 1 - c

    def zcol(section):
        return pl.BlockSpec((C, HG_WIDTH), lambda c: (rc(c), section))

    vec = pl.BlockSpec((1, HG_WIDTH), lambda c: (0, 0))
    blk = pl.BlockSpec((C, HG_WIDTH), lambda c: (rc(c), 0))
    out = jax.ShapeDtypeStruct((SEQ, HG_WIDTH), MXU_DTYPE)
    small = jax.ShapeDtypeStruct((1, HG_WIDTH), F32)
    return pl.pallas_call(
        body, name=name, grid=(N_CHUNKS,),
        in_specs=[zcol(0), zcol(1), zcol(2), zcol(3), vec, vec, blk,
                  pl.BlockSpec((HG_HEADS, 1, K, K), lambda c: (0, rc(c), 0, 0)), blk],
        out_specs=[blk, blk, blk, blk, vec, vec],
        out_shape=[out, out, out, out, small, small],
        scratch_shapes=[pltpu.VMEM((HG_HEADS, K, K), F32)],
        compiler_params=pltpu.CompilerParams(dimension_semantics=("arbitrary",)),
    )(z, z, z, z, lb, gain, o_raw, states, dy)


N_GROUPS = len(ATT_GROUPS)
HEAD_PAIRS = ATT_WIDTH // 128
ATT_COL0 = 4 * HG_WIDTH
UNROLLED_SUBSEQS = 4


def _alibi_coef():
    n = N_GROUPS * ATT_HEADS
    slopes = np.exp2(-ALIBI_MAX * np.arange(1, n + 1, dtype=np.float32) / n).astype(np.float32)
    dil = np.repeat(np.array([d for _, d in ATT_GROUPS], np.float32), ATT_HEADS)
    return jnp.asarray(slopes * dil, F32)


def _att_masks():
    B = ATT_BLOCK
    qi = lax.broadcasted_iota(jnp.int32, (B, B), 0)
    kj = lax.broadcasted_iota(jnp.int32, (B, B), 1)
    return qi, kj, (qi - kj).astype(F32), (qi + B - kj).astype(F32)


def _subseq_rows(r, d):
    return pl.ds(r, ATT_BLOCK, stride=d) if d > 1 else pl.ds(0, ATT_BLOCK)


def _for_each_subseq(d, fn):
    if d <= UNROLLED_SUBSEQS:
        for r in range(d):
            fn(r)
    else:
        lax.fori_loop(0, d, lambda r, carry: (fn(r), carry)[1], 0)


def _att_specs(g):
    d = ATT_GROUPS[g][1]
    R = ATT_BLOCK * d
    n_slabs = SEQ // R
    col0 = (ATT_COL0 + g * 3 * ATT_WIDTH) // 128

    def cur(col):
        return pl.BlockSpec((R, 128), lambda hp, s: (s, col + hp))

    def prev(col):
        return pl.BlockSpec((R, 128), lambda hp, s: (jnp.maximum(s - 1, 0), col + hp))

    def nxt(col):
        return pl.BlockSpec((R, 128), lambda hp, s: (jnp.minimum(s + 1, n_slabs - 1), col + hp))

    return d, R, n_slabs, col0, cur, prev, nxt


def _head_lanes(j):
    lane = lax.broadcasted_iota(jnp.int32, (ATT_BLOCK, 128), 1)
    return (lane >= 64 * j) & (lane < 64 * (j + 1))


def _lane_value(x, sel):
    return jnp.max(jnp.where(sel, x, -3e38), axis=-1, keepdims=True)


def att_fwd(z, g, name):
    B = ATT_BLOCK
    d, R, n_slabs, col0, cur, prev, _ = _att_specs(g)
    has_prev = n_slabs > 1

    def body(coef_ref, *refs):
        if has_prev:
            q_ref, kc_ref, vc_ref, kp_ref, vp_ref, o_ref, l_ref = refs
        else:
            q_ref, kc_ref, vc_ref, o_ref, l_ref = refs
        hp, s = pl.program_id(0), pl.program_id(1)
        qi, kj, d_cur, d_prev = _att_masks()
        m_cur = kj <= qi
        m_prev = kj >= qi + jnp.where(s == 0, 4 * B, 0)

        def one(r):
            rows = _subseq_rows(r, d)
            q, kc, vc = q_ref[rows, :], kc_ref[rows, :], vc_ref[rows, :]
            if has_prev:
                kpv, vpv = kp_ref[rows, :], vp_ref[rows, :]
            o_acc = jnp.zeros((B, 128), F32)
            l_acc = jnp.zeros((B, 128), F32)
            for j in range(2):
                sel = _head_lanes(j)
                cf = coef_ref[g * ATT_HEADS + hp * 2 + j]
                qh = jnp.where(sel, q, 0.0)
                s_cur = jnp.where(m_cur, _dot(qh, kc, tb=True) * 0.125 - cf * d_cur, NEG_INF)
                mx = jnp.max(s_cur, axis=-1, keepdims=True)
                if has_prev:
                    s_prev = jnp.where(m_prev, _dot(qh, kpv, tb=True) * 0.125 - cf * d_prev, NEG_INF)
                    mx = jnp.maximum(mx, jnp.max(s_prev, axis=-1, keepdims=True))
                e_cur = jnp.exp(s_cur - mx)
                den = jnp.sum(e_cur, axis=-1, keepdims=True)
                if has_prev:
                    e_prev = jnp.exp(s_prev - mx)
                    den = den + jnp.sum(e_prev, axis=-1, keepdims=True)
                inv = 1.0 / den
                oh = _dot(e_cur * inv, vc)
                if has_prev:
                    oh = oh + _dot(e_prev * inv, vpv)
                o_acc = jnp.where(sel, oh, o_acc)
                l_acc = jnp.where(sel, mx + jnp.log(den), l_acc)
            o_ref[rows, :] = o_acc
            l_ref[rows, :] = l_acc

        _for_each_subseq(d, one)

    in_specs = [pl.BlockSpec(memory_space=pltpu.SMEM), cur(col0), cur(col0 + 4), cur(col0 + 8)]
    args = [_alibi_coef(), z, z, z]
    if has_prev:
        in_specs += [prev(col0 + 4), prev(col0 + 8)]
        args += [z, z]
    out = jax.ShapeDtypeStruct((SEQ, ATT_WIDTH), F32)
    return pl.pallas_call(
        body, name=name, grid=(HEAD_PAIRS, n_slabs), in_specs=in_specs,
        out_specs=[cur(0), cur(0)], out_shape=[out, out],
        compiler_params=pltpu.CompilerParams(dimension_semantics=("parallel", "arbitrary")),
    )(*args)


def att_bwd(z, l, do, corr, g, name):
    B = ATT_BLOCK
    d, R, n_slabs, col0, cur, prev, nxt = _att_specs(g)
    neighbours = n_slabs > 1

    def body(coef_ref, *refs):
        if neighbours:
            (q_ref, kc_ref, vc_ref, l_ref, do_ref, cr_ref, kp_ref, vp_ref, qn_ref, ln_ref, don_ref, crn_ref,
             dq_ref, dk_ref, dv_ref, dq_sc, dk_sc, dv_sc) = refs
        else:
            q_ref, kc_ref, vc_ref, l_ref, do_ref, cr_ref, dq_ref, dk_ref, dv_ref, dq_sc, dk_sc, dv_sc = refs
        hp, s = pl.program_id(0), pl.program_id(1)
        qi, kj, d_cur, d_prev = _att_masks()
        m_cc = kj <= qi
        m_cp = kj >= qi + jnp.where(s == 0, 4 * B, 0)
        m_nc = kj >= qi + jnp.where(s == n_slabs - 1, 4 * B, 0)

        def one(r):
            rows = _subseq_rows(r, d)
            q, kc, vc, lv, dov, crv = (ref[rows, :] for ref in (q_ref, kc_ref, vc_ref, l_ref, do_ref, cr_ref))
            if neighbours:
                kpv, vpv, qn, lnv, donv, crnv = (ref[rows, :] for ref in (kp_ref, vp_ref, qn_ref, ln_ref, don_ref, crn_ref))
            dq_acc = jnp.zeros((B, 128), F32)
            dk_acc = jnp.zeros((B, 128), F32)
            dv_acc = jnp.zeros((B, 128), F32)
            for j in range(2):
                sel = _head_lanes(j)
                cf = coef_ref[g * ATT_HEADS + hp * 2 + j]
                qh = jnp.where(sel, q, 0.0)
                doh = jnp.where(sel, dov, 0.0)
                lse, cr = _lane_value(lv, sel), _lane_value(crv, sel)
                p_cc = jnp.exp(jnp.where(m_cc, _dot(qh, kc, tb=True) * 0.125 - cf * d_cur, NEG_INF) - lse)
                ds_cc = p_cc * (_dot(doh, vc, tb=True) + cr)
                dqh = _dot(ds_cc, kc)
                dkh = _dot(ds_cc, qh, ta=True)
                dvh = _dot(p_cc, doh, ta=True)
                if neighbours:
                    qnh = jnp.where(sel, qn, 0.0)
                    donh = jnp.where(sel, donv, 0.0)
                    lse_n, cr_n = _lane_value(lnv, sel), _lane_value(crnv, sel)
                    p_cp = jnp.exp(jnp.where(m_cp, _dot(qh, kpv, tb=True) * 0.125 - cf * d_prev, NEG_INF) - lse)
                    p_nc = jnp.exp(jnp.where(m_nc, _dot(qnh, kc, tb=True) * 0.125 - cf * d_prev, NEG_INF) - lse_n)
                    ds_cp = p_cp * (_dot(doh, vpv, tb=True) + cr)
                    ds_nc = p_nc * (_dot(donh, vc, tb=True) + cr_n)
                    dqh = dqh + _dot(ds_cp, kpv)
                    dkh = dkh + _dot(ds_nc, qnh, ta=True)
                    dvh = dvh + _dot(p_nc, donh, ta=True)
                dq_acc = jnp.where(sel, dqh * 0.125, dq_acc)
                dk_acc = jnp.where(sel, dkh * 0.125, dk_acc)
                dv_acc = jnp.where(sel, dvh, dv_acc)
            dq_sc[rows, :] = dq_acc
            dk_sc[rows, :] = dk_acc
            dv_sc[rows, :] = dv_acc

        _for_each_subseq(d, one)
        dq_ref[...] = dq_sc[...].astype(dq_ref.dtype)
        dk_ref[...] = dk_sc[...].astype(dk_ref.dtype)
        dv_ref[...] = dv_sc[...].astype(dv_ref.dtype)

    in_specs = [pl.BlockSpec(memory_space=pltpu.SMEM), cur(col0), cur(col0 + 4), cur(col0 + 8), cur(0), cur(0), cur(0)]
    args = [_alibi_coef(), z, z, z, l, do, corr]
    if neighbours:
        in_specs += [prev(col0 + 4), prev(col0 + 8), nxt(col0), nxt(0), nxt(0), nxt(0)]
        args += [z, z, z, l, do, corr]
    out = jax.ShapeDtypeStruct((SEQ, ATT_WIDTH), MXU_DTYPE)
    return pl.pallas_call(
        body, name=name, grid=(HEAD_PAIRS, n_slabs), in_specs=in_specs,
        out_specs=[cur(0)] * 3, out_shape=[out] * 3,
        scratch_shapes=[pltpu.VMEM((R, 128), F32)] * 3,
        compiler_params=pltpu.CompilerParams(dimension_semantics=("parallel", "arbitrary"),
                                             vmem_limit_bytes=MATMUL_VMEM_BYTES),
    )(*args)


def _head_sum(x):
    i = lax.broadcasted_iota(jnp.int32, (128, 128), 0) // 64
    j = lax.broadcasted_iota(jnp.int32, (128, 128), 1) // 64
    return _dot_f32(x, (i == j).astype(F32))


def _group_weights(l0, l1, l2):
    mx = jnp.maximum(jnp.maximum(l0, l1), l2)
    e0, e1, e2 = jnp.exp(l0 - mx), jnp.exp(l1 - mx), jnp.exp(l2 - mx)
    inv = 1.0 / (e0 + e1 + e2)
    return e0 * inv, e1 * inv, e2 * inv


def att_combine_fwd(o, l, name):
    def body(o0, o1, o2, l0, l1, l2, y_ref):
        w0, w1, w2 = _group_weights(l0[...], l1[...], l2[...])
        y_ref[...] = (o0[...] * w0 + o1[...] * w1 + o2[...] * w2).astype(y_ref.dtype)

    blk = pl.BlockSpec((ROW_TILE, ATT_WIDTH), lambda i: (i, 0))
    return pl.pallas_call(
        body, name=name, grid=(SEQ // ROW_TILE,), in_specs=[blk] * 6, out_specs=blk,
        out_shape=jax.ShapeDtypeStruct((SEQ, ATT_WIDTH), MXU_DTYPE),
    )(*o, *l)


def att_combine_bwd(o, l, dy, name):
    def body(o0, o1, o2, l0, l1, l2, dy_ref, do0, do1, do2, cr0, cr1, cr2):
        w = _group_weights(l0[...], l1[...], l2[...])
        dyv = dy_ref[...]
        dw = [_head_sum(dyv * o_ref[...]) for o_ref in (o0, o1, o2)]
        tot = w[0] * dw[0] + w[1] * dw[1] + w[2] * dw[2]
        for g, (do_ref, cr_ref) in enumerate(((do0, cr0), (do1, cr1), (do2, cr2))):
            do_ref[...] = dyv * w[g]
            cr_ref[...] = -w[g] * tot

    blk = pl.BlockSpec((ROW_TILE, 128), lambda i, j: (i, j))
    out = jax.ShapeDtypeStruct((SEQ, ATT_WIDTH), F32)
    res = pl.pallas_call(
        body, name=name, grid=(SEQ // ROW_TILE, HEAD_PAIRS), in_specs=[blk] * 7, out_specs=[blk] * 6, out_shape=[out] * 6,
    )(*o, *l, dy)
    return res[:N_GROUPS], res[N_GROUPS:]


SUM_ROW_TILES = (256, 128, 64, 32, 16)
SUM_TILE_ELEMS = 128 * 1024


def _row_tile(rows, cols):
    fit = [t for t in SUM_ROW_TILES if rows % t == 0]
    return next((t for t in fit if t * cols <= SUM_TILE_ELEMS), fit[-1])


def _shard_shape(rows, cols, axis):
    return (rows // N_CHIPS, cols) if axis == 0 else (rows, cols // N_CHIPS)


def _half_shape(rows, cols, axis):
    return (rows, cols // 2) if axis == 0 else (rows // 2, cols)


def _piece_shape(rows, cols, axis):
    return (rows // N_CHIPS, cols // 2) if axis == 0 else (rows // 2, cols // N_CHIPS)


def place_own_block(shard, chip, rows, cols, axis, name):
    sr, sc = _shard_shape(rows, cols, axis)
    tr = _row_tile(sr, sc)

    def body(chip_ref, s_ref, o_ref):
        o_ref[...] = s_ref[...].astype(o_ref.dtype)

    if axis == 0:
        out_map = lambda i, chip_ref: (chip_ref[0] * (sr // tr) + i, 0)
    else:
        out_map = lambda i, chip_ref: (i, chip_ref[0])
    return pl.pallas_call(
        body, name=name, out_shape=jax.ShapeDtypeStruct((rows, cols), WEIGHT_COMM_DTYPE),
        grid_spec=pltpu.PrefetchScalarGridSpec(
            num_scalar_prefetch=1, grid=(sr // tr,), in_specs=[pl.BlockSpec((tr, sc), lambda i, chip_ref: (i, 0))],
            out_specs=pl.BlockSpec((tr, sc), out_map)),
    )(chip, shard)


def add_halves(g, theirs, core, rows, cols, axis, name):
    hr, hc = _half_shape(rows, cols, axis)
    tr = _row_tile(hr, hc)

    def body(core_ref, g_ref, t_ref, o_ref):
        o_ref[...] = (g_ref[...].astype(F32) + t_ref[...].astype(F32)).astype(o_ref.dtype)

    if axis == 0:
        g_map = lambda i, core_ref: (i, core_ref[0])
    else:
        g_map = lambda i, core_ref: (core_ref[0] * (hr // tr) + i, 0)
    blk = pl.BlockSpec((tr, hc), lambda i, core_ref: (i, 0))
    return pl.pallas_call(
        body, name=name, out_shape=jax.ShapeDtypeStruct((hr, hc), GRAD_COMM_DTYPE),
        grid_spec=pltpu.PrefetchScalarGridSpec(
            num_scalar_prefetch=1, grid=(hr // tr,), in_specs=[pl.BlockSpec((tr, hc), g_map), blk], out_specs=blk),
    )(core, g, theirs)


def add_pieces(half, got, chip, rows, cols, axis, name):
    hr, _ = _half_shape(rows, cols, axis)
    pr, pc = _piece_shape(rows, cols, axis)
    tr = _row_tile(pr, pc)

    def body(chip_ref, h_ref, got_ref, o_ref):
        o_ref[...] = (h_ref[...].astype(F32) + got_ref[0].astype(F32) + got_ref[1].astype(F32) + got_ref[2].astype(F32))

    if axis == 0:
        h_map = lambda i, chip_ref: (chip_ref[0] * (pr // tr) + i, 0)
    else:
        h_map = lambda i, chip_ref: (i, chip_ref[0])
    return pl.pallas_call(
        body, name=name, out_shape=jax.ShapeDtypeStruct((pr, pc), F32),
        grid_spec=pltpu.PrefetchScalarGridSpec(
            num_scalar_prefetch=1, grid=(pr // tr,),
            in_specs=[pl.BlockSpec((tr, pc), h_map), pl.BlockSpec((3, tr, pc), lambda i, chip_ref: (0, i, 0))],
            out_specs=pl.BlockSpec((tr, pc), lambda i, chip_ref: (i, 0))),
    )(chip, half, got)


def _adamw_math(w, g, m, v):
    nm = ADAM_B1 * m + (1.0 - ADAM_B1) * g
    nv = ADAM_B2 * v + (1.0 - ADAM_B2) * (g * g)
    m_hat = nm / (1.0 - ADAM_B1 ** ADAM_STEP)
    v_hat = nv / (1.0 - ADAM_B2 ** ADAM_STEP)
    return -ADAM_LR * (m_hat / (jnp.sqrt(v_hat) + ADAM_EPS) + ADAM_WD * w), nm, nv


def adamw(w, g, m, v, name):
    R, Cc = w.shape
    tr = _pick(R, (256, 128, 64, 8))

    def body(w_ref, g_ref, m_ref, v_ref, d_ref, nm_ref, nv_ref):
        d_ref[...], nm_ref[...], nv_ref[...] = _adamw_math(w_ref[...], g_ref[...], m_ref[...], v_ref[...])

    blk = pl.BlockSpec((tr, Cc), lambda i: (i, 0))
    out = jax.ShapeDtypeStruct((R, Cc), F32)
    return pl.pallas_call(
        body, name=name, grid=(R // tr,), in_specs=[blk] * 4, out_specs=[blk] * 3, out_shape=[out, out, out],
    )(w, g, m, v)


def adamw_halves(w, mine, theirs, m, v, core, rows, cols, axis, name):
    sr, sc = _shard_shape(rows, cols, axis)
    pr, pc = _piece_shape(rows, cols, axis)
    tr = _row_tile(pr, pc)
    nt = pr // tr

    def body(core_ref, w_ref, a_ref, b_ref, m_ref, v_ref, g_ref, d_ref, nm_ref, nv_ref):
        g = jnp.where(pl.program_id(0) == core_ref[0], a_ref[...], b_ref[...])
        g_ref[...] = g
        d_ref[...], nm_ref[...], nv_ref[...] = _adamw_math(w_ref[...], g, m_ref[...], v_ref[...])

    if axis == 0:
        full = pl.BlockSpec((tr, pc), lambda h, i, core_ref: (i, h))
    else:
        full = pl.BlockSpec((tr, pc), lambda h, i, core_ref: (h * nt + i, 0))
    part = pl.BlockSpec((tr, pc), lambda h, i, core_ref: (i, 0))
    out = jax.ShapeDtypeStruct((sr, sc), F32)
    return pl.pallas_call(
        body, name=name, out_shape=[out, out, out, out],
        grid_spec=pltpu.PrefetchScalarGridSpec(
            num_scalar_prefetch=1, grid=(2, nt), in_specs=[full, part, part, full, full], out_specs=[full] * 4),
    )(core, w, mine, theirs, m, v)


BIG = (
    ("ffn1_w_gate_up", D_MODEL, 2 * D_FF, 1),
    ("ffn1_w_down", D_FF, D_MODEL, 0),
    ("w_in", D_MODEL, IN_COLS, 1),
    ("w_branch_hg", HG_WIDTH, D_MODEL, 1),
    ("w_branch_att", ATT_WIDTH, D_MODEL, 1),
    ("w_out", D_MODEL, D_MODEL, 0),
    ("ffn2_w_gate_up", D_MODEL, 2 * D_FF, 1),
    ("ffn2_w_down", D_FF, D_MODEL, 0),
)
N_BIG = len(BIG)
ANY = pl.BlockSpec(memory_space=pl.ANY)


def _place():
    return lax.axis_index("x"), lax.axis_index("y"), lax.axis_index("c")


def _other_chips(x, y):
    return ((1 - x, y), (x, 1 - y), (1 - x, 1 - y))


MAX_COPY_CHUNKS = 16
CHUNK_ROW_ALIGN = 16


def _row_chunks(view):
    rows = view.shape[0]
    n = next(n for n in range(MAX_COPY_CHUNKS, 0, -1) if rows % (CHUNK_ROW_ALIGN * n) == 0 or n == 1)
    step = rows // n
    return [pl.ds(i * step, step) for i in range(n)]


def _remote(src, dst, send_sem, recv_sem, device):
    return pltpu.make_async_remote_copy(src_ref=src, dst_ref=dst, send_sem=send_sem, recv_sem=recv_sem,
                                        device_id=device, device_id_type=MESH)


def _start_remote(src, dst, send_sem, recv_sem, device):
    for rows in _row_chunks(src):
        _remote(src.at[rows, :], dst.at[rows, :], send_sem, recv_sem, device).start()
    return _remote(src, dst, send_sem, recv_sem, device)


def all_gather_weights(placed):
    def piece(ref, rows, cols, axis, chip, c):
        sr, sc = _shard_shape(rows, cols, axis)
        j = 2 * chip[0] + chip[1]
        if axis == 0:
            return ref.at[pl.ds(j * sr + c * (sr // 2), sr // 2), :]
        return ref.at[pl.ds(c * (sr // 2), sr // 2), pl.ds(pl.multiple_of(j * sc, 128), sc)]

    def body(*refs):
        outs = refs[N_BIG:2 * N_BIG]
        send_sems, recv_sems = refs[2 * N_BIG:]
        x, y, c = _place()
        chips = _other_chips(x, y)
        sends = []
        for w, (_, r, cc, ax) in enumerate(BIG):
            mine = piece(outs[w], r, cc, ax, (x, y), c)
            for k, chip in enumerate(chips):
                sends.append(_start_remote(mine, mine, send_sems.at[w, k], recv_sems.at[w, k], (*chip, c)))
        passed = []
        for w, (_, r, cc, ax) in enumerate(BIG):
            for k, chip in enumerate(chips):
                got = piece(outs[w], r, cc, ax, chip, c)
                _remote(got, got, send_sems.at[w, k], recv_sems.at[w, k], (x, y, c)).wait_recv()
                passed.append(_start_remote(got, got, send_sems.at[w, 3 + k], recv_sems.at[w, 3 + k], (x, y, 1 - c)))
        for w, (_, r, cc, ax) in enumerate(BIG):
            for k, chip in enumerate(chips):
                got = piece(outs[w], r, cc, ax, chip, 1 - c)
                _remote(got, got, send_sems.at[w, 3 + k], recv_sems.at[w, 3 + k], (x, y, c)).wait_recv()
        for cp in sends + passed:
            cp.wait_send()

    return pl.pallas_call(
        body, name="all_gather_weights", in_specs=[ANY] * N_BIG, out_specs=[ANY] * N_BIG,
        out_shape=[jax.ShapeDtypeStruct((r, cc), WEIGHT_COMM_DTYPE) for _, r, cc, _ in BIG],
        input_output_aliases={w: w for w in range(N_BIG)},
        scratch_shapes=[pltpu.SemaphoreType.DMA((N_BIG, 6)), pltpu.SemaphoreType.DMA((N_BIG, 6))],
    )(*placed)


def _half(ref, rows, cols, axis, c):
    if axis == 0:
        return ref.at[:, pl.ds(pl.multiple_of(c * (cols // 2), 128), cols // 2)]
    return ref.at[pl.ds(c * (rows // 2), rows // 2), :]


def _piece_of_half(ref, rows, cols, axis, chip):
    j = 2 * chip[0] + chip[1]
    pr, pc = _piece_shape(rows, cols, axis)
    if axis == 0:
        return ref.at[pl.ds(j * pr, pr), :]
    return ref.at[:, pl.ds(pl.multiple_of(j * pc, 128), pc)]


def exchange_halves(grads):
    def body(*refs):
        ins, theirs = refs[:N_BIG], refs[N_BIG:2 * N_BIG]
        send_sems, recv_sems = refs[2 * N_BIG:]
        x, y, c = _place()
        copies = [_start_remote(_half(ins[w], r, cc, ax, 1 - c), theirs[w], send_sems.at[w], recv_sems.at[w], (x, y, 1 - c))
                  for w, (_, r, cc, ax) in enumerate(BIG)]
        for cp in copies:
            cp.wait()

    return pl.pallas_call(
        body, name="exchange_halves", in_specs=[ANY] * N_BIG, out_specs=[ANY] * N_BIG,
        out_shape=[jax.ShapeDtypeStruct(_half_shape(r, cc, ax), GRAD_COMM_DTYPE) for _, r, cc, ax in BIG],
        scratch_shapes=[pltpu.SemaphoreType.DMA((N_BIG,)), pltpu.SemaphoreType.DMA((N_BIG,))],
    )(*grads)


def scatter_pieces(halves):
    def body(*refs):
        ins, got = refs[:N_BIG], refs[N_BIG:2 * N_BIG]
        send_sems, recv_sems = refs[2 * N_BIG:]
        x, y, c = _place()
        copies = []
        for w, (_, r, cc, ax) in enumerate(BIG):
            for k, chip in enumerate(_other_chips(x, y)):
                copies.append(_start_remote(_piece_of_half(ins[w], r, cc, ax, chip), got[w].at[k], send_sems.at[w, k],
                                            recv_sems.at[w, k], (*chip, c)))
        for cp in copies:
            cp.wait()

    return pl.pallas_call(
        body, name="scatter_pieces", in_specs=[ANY] * N_BIG, out_specs=[ANY] * N_BIG,
        out_shape=[jax.ShapeDtypeStruct((3,) + _piece_shape(r, cc, ax), GRAD_COMM_DTYPE) for _, r, cc, ax in BIG],
        scratch_shapes=[pltpu.SemaphoreType.DMA((N_BIG, 3)), pltpu.SemaphoreType.DMA((N_BIG, 3))],
    )(*halves)


def exchange_reduced(pieces):
    def body(*refs):
        ins, theirs = refs[:N_BIG], refs[N_BIG:2 * N_BIG]
        send_sems, recv_sems = refs[2 * N_BIG:]
        x, y, c = _place()
        copies = [_start_remote(ins[w], theirs[w], send_sems.at[w], recv_sems.at[w], (x, y, 1 - c)) for w in range(N_BIG)]
        for cp in copies:
            cp.wait()

    return pl.pallas_call(
        body, name="exchange_reduced", in_specs=[ANY] * N_BIG, out_specs=[ANY] * N_BIG,
        out_shape=[jax.ShapeDtypeStruct(_piece_shape(r, cc, ax), F32) for _, r, cc, ax in BIG],
        scratch_shapes=[pltpu.SemaphoreType.DMA((N_BIG,)), pltpu.SemaphoreType.DMA((N_BIG,))],
    )(*pieces)


N_DEV = 8
SMALL_ROWS = 8


def all_reduce_small(packed):
    def body(x_ref, o_ref, gathered, send_sems, recv_sems):
        x, y, c = _place()
        me = 4 * x + 2 * y + c
        gathered[me] = x_ref[...]
        copies = []
        for k in range(1, N_DEV):
            peer = (x ^ (k >> 2), y ^ ((k >> 1) & 1), c ^ (k & 1))
            cp = pltpu.make_async_remote_copy(
                src_ref=x_ref, dst_ref=gathered.at[me], send_sem=send_sems.at[k - 1], recv_sem=recv_sems.at[k - 1],
                device_id=peer, device_id_type=MESH)
            cp.start()
            copies.append(cp)
        for cp in copies:
            cp.wait()
        acc = gathered[0]
        for k in range(1, N_DEV):
            acc = acc + gathered[k]
        o_ref[...] = acc

    vm = pl.BlockSpec(memory_space=pltpu.VMEM)
    return pl.pallas_call(
        body, name="all_reduce_small", in_specs=[vm], out_specs=vm,
        out_shape=jax.ShapeDtypeStruct((SMALL_ROWS, D_MODEL), F32),
        scratch_shapes=[pltpu.VMEM((N_DEV, SMALL_ROWS, D_MODEL), F32), pltpu.SemaphoreType.DMA((N_DEV - 1,)),
                        pltpu.SemaphoreType.DMA((N_DEV - 1,))],
    )(packed)


def _swiglu_block_fwd(h, norm_g, w_gu, w_down, tag):
    n = rmsnorm_fwd(h, norm_g, f"{tag}_norm")
    gu = matmul(n, w_gu, name=f"{tag}_gate_up")
    s = swiglu_fwd(gu, f"{tag}_swiglu")
    h_out = matmul(s, w_down, res=h, scale=0.5, name=f"{tag}_down")
    return h_out, (n, gu, s)


def _swiglu_block_bwd(h, norm_g, w_gu, w_down, saved, dh_out, tag):
    n, gu, s = saved
    df = dh_out.astype(MXU_DTYPE)
    d_down = matmul(s, df, ta=True, scale=0.5, out_dtype=GRAD_COMM_DTYPE, name=f"{tag}_d_w_down")
    ds = matmul(df, w_down, tb=True, scale=0.5, name=f"{tag}_d_s")
    dgu = swiglu_bwd(gu, ds, f"{tag}_swiglu_bwd")
    d_gu = matmul(n, dgu, ta=True, out_dtype=GRAD_COMM_DTYPE, name=f"{tag}_d_w_gate_up")
    dn = matmul(dgu, w_gu, tb=True, name=f"{tag}_d_n")
    dh, dg = rmsnorm_bwd(h, norm_g, dn, dh_out, f"{tag}_norm_bwd")
    return dh, dg, d_gu, d_down


def local_step(x, target, small, big):
    h1, saved1 = _swiglu_block_fwd(x, small["ffn1_norm"], big["ffn1_w_gate_up"], big["ffn1_w_down"], "ffn1")
    u = rmsnorm_fwd(h1, small["mix_norm"], "mix_norm")
    z = matmul(u, big["w_in"], name="w_in")
    p = small["hg_lower_bounds"]
    lb = 1.0 / (1.0 + jnp.exp(p[1:2] - p[0:1]))
    y_hg, o_raw, states = hgrn_fwd(z, lb, small["hg_out_norm"], "hgrn_fwd")
    o_att, l_att = zip(*[att_fwd(z, g, f"att_fwd_{g}") for g in range(N_GROUPS)])
    y_att = att_combine_fwd(o_att, l_att, "att_combine")
    bh = matmul(y_hg, big["w_branch_hg"], name="branch_hg")
    ba = matmul(y_att, big["w_branch_att"], name="branch_att")
    merged = merge_fwd(z, bh, ba, "merge")
    h2 = matmul(merged, big["w_out"], res=h1, name="w_out")
    h3, saved2 = _swiglu_block_fwd(h2, small["ffn2_norm"], big["ffn2_w_gate_up"], big["ffn2_w_down"], "ffn2")
    dh3, d_final, loss = final_norm_loss(h3, small["final_norm"], target, "final_norm_loss")

    gs, gb = {"final_norm": d_final}, {}
    dh2, gs["ffn2_norm"], gb["ffn2_w_gate_up"], gb["ffn2_w_down"] = _swiglu_block_bwd(
        h2, small["ffn2_norm"], big["ffn2_w_gate_up"], big["ffn2_w_down"], saved2, dh3, "ffn2")
    dh2_m = dh2.astype(MXU_DTYPE)
    gb["w_out"] = matmul(merged, dh2_m, ta=True, out_dtype=GRAD_COMM_DTYPE, name="d_w_out")
    dmerged = matmul(dh2_m, big["w_out"], tb=True, name="d_merged")
    dbh, dba, dgh, dga = merge_bwd(z, bh, ba, dmerged, "merge_bwd")
    gb["w_branch_hg"] = matmul(y_hg, dbh, ta=True, out_dtype=GRAD_COMM_DTYPE, name="d_w_branch_hg")
    gb["w_branch_att"] = matmul(y_att, dba, ta=True, out_dtype=GRAD_COMM_DTYPE, name="d_w_branch_att")
    dy_hg = matmul(dbh, big["w_branch_hg"], tb=True, name="d_y_hg")
    dy_att = matmul(dba, big["w_branch_att"], tb=True, name="d_y_att")
    dq, dfp, di, dog, d_lb, gs["hg_out_norm"] = hgrn_bwd(z, lb, small["hg_out_norm"], o_raw, states, dy_hg, "hgrn_bwd")
    do_att, corr = att_combine_bwd(o_att, l_att, dy_att, "att_combine_bwd")
    d_att = [part for g in range(N_GROUPS) for part in att_bwd(z, l_att[g], do_att[g], corr[g], g, f"att_bwd_{g}")]
    dz = jnp.concatenate([dq, dfp, di, dog, *d_att, dgh, dga], axis=1)
    gb["w_in"] = matmul(u, dz, ta=True, out_dtype=GRAD_COMM_DTYPE, name="d_w_in")
    du = matmul(dz, big["w_in"], tb=True, name="d_u")
    dh1, gs["mix_norm"] = rmsnorm_bwd(h1, small["mix_norm"], du, dh2, "mix_norm_bwd")
    dp0 = d_lb * lb * (1.0 - lb)
    gs["hg_lower_bounds"] = jnp.concatenate([dp0, -dp0], axis=0)
    dx, gs["ffn1_norm"], gb["ffn1_w_gate_up"], gb["ffn1_w_down"] = _swiglu_block_bwd(
        x, small["ffn1_norm"], big["ffn1_w_gate_up"], big["ffn1_w_down"], saved1, dh1, "ffn1")
    return loss[0, 0], dx, gs, gb


SMALL = ("ffn1_norm", "mix_norm", "hg_lower_bounds", "hg_out_norm", "ffn2_norm", "final_norm")
WEIGHTS = ("ffn1_norm", "ffn1_w_gate_up", "ffn1_w_down", "mix_norm", "w_in", "hg_lower_bounds", "hg_out_norm",
           "w_branch_hg", "w_branch_att", "w_out", "ffn2_norm", "ffn2_w_gate_up", "ffn2_w_down", "final_norm")
SMALL_SHAPE = {"ffn1_norm": (1, 1024), "mix_norm": (1, 1024), "hg_lower_bounds": (2, 512), "hg_out_norm": (1, 512),
               "ffn2_norm": (1, 1024), "final_norm": (1024,)}
LOSS_ROW = 6


def _pack_small(vals):
    rows = []
    for n in SMALL:
        r = vals[n].reshape(1, -1).astype(F32)
        rows.append(jnp.pad(r, ((0, 0), (0, D_MODEL - r.shape[1]))))
    rows.append(jnp.zeros((SMALL_ROWS - len(SMALL), D_MODEL), F32))
    return jnp.concatenate(rows, axis=0)


def _unpack_small(packed):
    out = {}
    for i, n in enumerate(SMALL):
        size = int(np.prod(SMALL_SHAPE[n]))
        out[n] = packed[i, :size].reshape(SMALL_SHAPE[n])
    return out


def reduce_big(grads, core, chip):
    theirs = exchange_halves([grads[n] for n, *_ in BIG])
    halves = [add_halves(grads[n], t, core, r, cc, ax, f"add_halves_{n}") for (n, r, cc, ax), t in zip(BIG, theirs)]
    got = scatter_pieces(halves)
    mine = [add_pieces(h, g, chip, r, cc, ax, f"add_pieces_{n}") for (n, r, cc, ax), h, g in zip(BIG, halves, got)]
    return mine, exchange_reduced(mine)


def kernel(x, ffn1_norm, ffn1_w_gate_up, ffn1_w_down, mix_norm, w_in, hg_lower_bounds, hg_out_norm, w_branch_hg, w_branch_att, w_out, ffn2_norm, ffn2_w_gate_up, ffn2_w_down, final_norm, loss_target, m_ffn1_norm, m_ffn1_w_gate_up, m_ffn1_w_down, m_mix_norm, m_w_in, m_hg_lower_bounds, m_hg_out_norm, m_w_branch_hg, m_w_branch_att, m_w_out, m_ffn2_norm, m_ffn2_w_gate_up, m_ffn2_w_down, m_final_norm, v_ffn1_norm, v_ffn1_w_gate_up, v_ffn1_w_down, v_mix_norm, v_w_in, v_hg_lower_bounds, v_hg_out_norm, v_w_branch_hg, v_w_branch_att, v_w_out, v_ffn2_norm, v_ffn2_w_gate_up, v_ffn2_w_down, v_final_norm):
    w = dict(ffn1_norm=ffn1_norm, ffn1_w_gate_up=ffn1_w_gate_up, ffn1_w_down=ffn1_w_down, mix_norm=mix_norm, w_in=w_in,
             hg_lower_bounds=hg_lower_bounds, hg_out_norm=hg_out_norm, w_branch_hg=w_branch_hg, w_branch_att=w_branch_att,
             w_out=w_out, ffn2_norm=ffn2_norm, ffn2_w_gate_up=ffn2_w_gate_up, ffn2_w_down=ffn2_w_down, final_norm=final_norm)
    m = dict(ffn1_norm=m_ffn1_norm, ffn1_w_gate_up=m_ffn1_w_gate_up, ffn1_w_down=m_ffn1_w_down, mix_norm=m_mix_norm,
             w_in=m_w_in, hg_lower_bounds=m_hg_lower_bounds, hg_out_norm=m_hg_out_norm, w_branch_hg=m_w_branch_hg,
             w_branch_att=m_w_branch_att, w_out=m_w_out, ffn2_norm=m_ffn2_norm, ffn2_w_gate_up=m_ffn2_w_gate_up,
             ffn2_w_down=m_ffn2_w_down, final_norm=m_final_norm)
    v = dict(ffn1_norm=v_ffn1_norm, ffn1_w_gate_up=v_ffn1_w_gate_up, ffn1_w_down=v_ffn1_w_down, mix_norm=v_mix_norm,
             w_in=v_w_in, hg_lower_bounds=v_hg_lower_bounds, hg_out_norm=v_hg_out_norm, w_branch_hg=v_w_branch_hg,
             w_branch_att=v_w_branch_att, w_out=v_w_out, ffn2_norm=v_ffn2_norm, ffn2_w_gate_up=v_ffn2_w_gate_up,
             ffn2_w_down=v_ffn2_w_down, final_norm=v_final_norm)

    core = lax.axis_index("c").astype(jnp.int32).reshape(1)
    chip = (2 * lax.axis_index("x") + lax.axis_index("y")).astype(jnp.int32).reshape(1)
    gathered = all_gather_weights([place_own_block(w[n][0], chip, r, cc, ax, f"place_{n}") for n, r, cc, ax in BIG])
    big = {n: a for (n, *_), a in zip(BIG, gathered)}
    small = {n: w[n] for n in SMALL}
    small["final_norm"] = final_norm.reshape(1, D_MODEL)

    loss, dx, gs, gb = local_step(x[0], loss_target[0], small, big)

    packed = _pack_small(gs)
    packed = packed.at[LOSS_ROW].set(jnp.full((D_MODEL,), loss, F32))
    total = all_reduce_small(packed)
    grads = _unpack_small(total)
    loss_total = total[LOSS_ROW, 0]
    mine, theirs = reduce_big(gb, core, chip)

    delta, new_m, new_v = {}, {}, {}
    pd, pm, pv = adamw(_pack_small({n: w[n] for n in SMALL}), total.at[LOSS_ROW].set(0.0),
                       _pack_small({n: m[n] for n in SMALL}), _pack_small({n: v[n] for n in SMALL}), "adamw_small")
    delta.update(_unpack_small(pd))
    new_m.update(_unpack_small(pm))
    new_v.update(_unpack_small(pv))
    for (n, r, cc, ax), a, b in zip(BIG, mine, theirs):
        g, d, nm, nv = adamw_halves(w[n][0], a, b, m[n][0], v[n][0], core, r, cc, ax, f"adamw_{n}")
        grads[n], delta[n], new_m[n], new_v[n] = g[None], d[None], nm[None], nv[None]

    return (loss_total, dx[None], *[grads[n] for n in WEIGHTS], *[delta[n] for n in WEIGHTS],
            *[new_m[n] for n in WEIGHTS], *[new_v[n] for n in WEIGHTS])
```

```python
import numpy as np
import jax
import jax.numpy as jnp
from jax import lax
from jax.experimental import pallas as pl
from jax.experimental.pallas import tpu as pltpu

SEQ = 2048
D_MODEL = 1024
D_FF = 2816
HG_HEADS = 4
HG_DIM = 128
HG_WIDTH = 512
HG_CHUNK = 64
ATT_GROUPS = ((128, 1), (512, 4), (2048, 16))
ATT_HEADS = 8
ATT_WIDTH = 512
ATT_BLOCK = 128
ALIBI_MAX = 8.0
IN_COLS = 8704
EPS = 1e-6
NEG_INF = -1e30
ADAM_LR = 0.001
ADAM_B1 = 0.9
ADAM_B2 = 0.999
ADAM_EPS = 1e-08
ADAM_WD = 0.01
ADAM_STEP = 10

N_CHIPS = 4
MXU_DTYPE = jnp.bfloat16
HG_DOT_DTYPE = jnp.float32
WEIGHT_COMM_DTYPE = jnp.bfloat16
GRAD_COMM_DTYPE = jnp.bfloat16
MESH = pl.DeviceIdType.MESH
F32 = jnp.float32
HIGHEST = lax.Precision.HIGHEST


def _pick(n, cands):
    for c in cands:
        if n % c == 0:
            return c
    return n


def _sigmoid(x):
    return 1.0 / (1.0 + jnp.exp(-x))


def _dot(a, b, ta=False, tb=False):
    dn = (((0 if ta else 1,), (1 if tb else 0,)), ((), ()))
    return lax.dot_general(a.astype(MXU_DTYPE), b.astype(MXU_DTYPE), dn, preferred_element_type=F32)


def _dot_f32(a, b):
    return jnp.dot(a, b, precision=HIGHEST, preferred_element_type=F32)


def _hdot(a, b, ta=False, tb=False):
    if HG_DOT_DTYPE == F32:
        dn = (((0 if ta else 1,), (1 if tb else 0,)), ((), ()))
        return lax.dot_general(a, b, dn, precision=HIGHEST, preferred_element_type=F32)
    return _dot(a, b, ta, tb)


MATMUL_VMEM_BYTES = 48 * 1024 * 1024
MATMUL_TILE_BYTES = 36 * 1024 * 1024
MXU_ALIGN = 128


def _divisors(n, most):
    return [t for t in range(min(n, most), 0, -MXU_ALIGN) if n % t == 0 and t % MXU_ALIGN == 0]


def _matmul_tiles(M, N, K, in_bytes, out_bytes, has_res):
    best = None
    for tk in _divisors(K, K):
        nk = K // tk
        for tm in _divisors(M, 2048):
            for tn in _divisors(N, 512):
                tiles = 2 * in_bytes * (tm * tk + tk * tn) + 2 * out_bytes * tm * tn
                tiles += 4 * tm * tn * ((nk > 1) + 2 * has_res)
                if tiles > MATMUL_TILE_BYTES:
                    continue
                traffic = in_bytes * (M * K * (1 if nk == 1 else N // tn) + K * N * (M // tm))
                key = (traffic, -tm * tn * tk)
                if best is None or key < best[0]:
                    best = (key, (tm, tn, tk))
    return best[1]


def matmul(a, b, *, ta=False, tb=False, out_dtype=F32, res=None, scale=1.0, name):
    if ta:
        K, M = a.shape
    else:
        M, K = a.shape
    if tb:
        N, K2 = b.shape
    else:
        K2, N = b.shape
    assert K == K2 and a.dtype == b.dtype
    tm, tn, tk = _matmul_tiles(M, N, K, a.dtype.itemsize, jnp.dtype(out_dtype).itemsize, res is not None)
    nk = K // tk

    def finish(r, r_ref, o_ref):
        if scale != 1.0:
            r = r * scale
        if res is not None:
            r = r_ref[...] + r
        o_ref[...] = r.astype(out_dtype)

    def body(*refs):
        a_ref, b_ref = refs[:2]
        r_ref = refs[2] if res is not None else None
        o_ref = refs[3] if res is not None else refs[2]
        if nk == 1:
            finish(_dot(a_ref[...], b_ref[...], ta, tb), r_ref, o_ref)
            return
        acc = refs[-1]
        k = pl.program_id(2)

        @pl.when(k == 0)
        def _():
            acc[...] = jnp.zeros_like(acc)

        acc[...] += _dot(a_ref[...], b_ref[...], ta, tb)

        @pl.when(k == nk - 1)
        def _():
            finish(acc[...], r_ref, o_ref)

    a_spec = pl.BlockSpec((tk, tm), lambda i, j, k: (k, i)) if ta else pl.BlockSpec((tm, tk), lambda i, j, k: (i, k))
    b_spec = pl.BlockSpec((tn, tk), lambda i, j, k: (j, k)) if tb else pl.BlockSpec((tk, tn), lambda i, j, k: (k, j))
    in_specs = [a_spec, b_spec]
    args = [a, b]
    if res is not None:
        in_specs.append(pl.BlockSpec((tm, tn), lambda i, j, k: (i, j)))
        args.append(res)
    return pl.pallas_call(
        body, name=name, grid=(M // tm, N // tn, nk), in_specs=in_specs,
        out_specs=pl.BlockSpec((tm, tn), lambda i, j, k: (i, j)),
        out_shape=jax.ShapeDtypeStruct((M, N), out_dtype),
        scratch_shapes=[pltpu.VMEM((tm, tn), F32)] if nk > 1 else [],
        compiler_params=pltpu.CompilerParams(dimension_semantics=("parallel", "parallel", "arbitrary"),
                                             vmem_limit_bytes=MATMUL_VMEM_BYTES),
    )(*args)


ROW_TILE = 256


def rmsnorm_fwd(x, g, name):
    def body(x_ref, g_ref, n_ref):
        xv = x_ref[...]
        r = lax.rsqrt(jnp.mean(xv * xv, axis=-1, keepdims=True) + EPS)
        n_ref[...] = ((xv * r) * g_ref[...]).astype(n_ref.dtype)

    return pl.pallas_call(
        body, name=name, grid=(SEQ // ROW_TILE,),
        in_specs=[pl.BlockSpec((ROW_TILE, D_MODEL), lambda i: (i, 0)), pl.BlockSpec((1, D_MODEL), lambda i: (0, 0))],
        out_specs=pl.BlockSpec((ROW_TILE, D_MODEL), lambda i: (i, 0)),
        out_shape=jax.ShapeDtypeStruct((SEQ, D_MODEL), MXU_DTYPE),
    )(x, g)


def rmsnorm_bwd(x, g, dn, dres, name):
    def body(x_ref, g_ref, dn_ref, dr_ref, dx_ref, dg_ref):
        xv = x_ref[...]
        r = lax.rsqrt(jnp.mean(xv * xv, axis=-1, keepdims=True) + EPS)
        xh = xv * r
        dnv = dn_ref[...]

        @pl.when(pl.program_id(0) == 0)
        def _():
            dg_ref[...] = jnp.zeros_like(dg_ref)

        dg_ref[...] += jnp.sum(dnv * xh, axis=0, keepdims=True)
        dxh = dnv * g_ref[...]
        dx_ref[...] = dr_ref[...] + r * (dxh - xh * jnp.mean(dxh * xh, axis=-1, keepdims=True))

    row = pl.BlockSpec((ROW_TILE, D_MODEL), lambda i: (i, 0))
    vec = pl.BlockSpec((1, D_MODEL), lambda i: (0, 0))
    return pl.pallas_call(
        body, name=name, grid=(SEQ // ROW_TILE,), in_specs=[row, vec, row, row], out_specs=[row, vec],
        out_shape=[jax.ShapeDtypeStruct((SEQ, D_MODEL), F32), jax.ShapeDtypeStruct((1, D_MODEL), F32)],
        compiler_params=pltpu.CompilerParams(dimension_semantics=("arbitrary",)),
    )(x, g, dn, dres)


def final_norm_loss(h, g, target, name):
    def body(h_ref, g_ref, t_ref, dh_ref, dg_ref, loss_ref):
        xv = h_ref[...]
        r = lax.rsqrt(jnp.mean(xv * xv, axis=-1, keepdims=True) + EPS)
        xh = xv * r
        gv = g_ref[...]
        e = xh * gv - t_ref[...]

        @pl.when(pl.program_id(0) == 0)
        def _():
            dg_ref[...] = jnp.zeros_like(dg_ref)
            loss_ref[...] = jnp.zeros_like(loss_ref)

        part = 0.5 * jnp.sum(jnp.sum(e * e, axis=-1, keepdims=True) * (1.0 / D_MODEL), axis=0, keepdims=True)
        loss_ref[...] += jnp.broadcast_to(part, loss_ref.shape)
        dout = e * (1.0 / D_MODEL)
        dg_ref[...] += jnp.sum(dout * xh, axis=0, keepdims=True)
        dxh = dout * gv
        dh_ref[...] = r * (dxh - xh * jnp.mean(dxh * xh, axis=-1, keepdims=True))

    row = pl.BlockSpec((ROW_TILE, D_MODEL), lambda i: (i, 0))
    vec = pl.BlockSpec((1, D_MODEL), lambda i: (0, 0))
    return pl.pallas_call(
        body, name=name, grid=(SEQ // ROW_TILE,), in_specs=[row, vec, row],
        out_specs=[row, vec, pl.BlockSpec((8, 128), lambda i: (0, 0))],
        out_shape=[jax.ShapeDtypeStruct((SEQ, D_MODEL), F32), jax.ShapeDtypeStruct((1, D_MODEL), F32),
                   jax.ShapeDtypeStruct((8, 128), F32)],
        compiler_params=pltpu.CompilerParams(dimension_semantics=("arbitrary",)),
    )(h, g, target)


FF_TILE = D_FF // 2


def swiglu_fwd(gu, name):
    def body(a_ref, b_ref, s_ref):
        a = a_ref[...]
        s_ref[...] = (a * _sigmoid(a) * b_ref[...]).astype(s_ref.dtype)

    return pl.pallas_call(
        body, name=name, grid=(SEQ // ROW_TILE, 2),
        in_specs=[pl.BlockSpec((ROW_TILE, FF_TILE), lambda i, j: (i, j)),
                  pl.BlockSpec((ROW_TILE, FF_TILE), lambda i, j: (i, j + 2))],
        out_specs=pl.BlockSpec((ROW_TILE, FF_TILE), lambda i, j: (i, j)),
        out_shape=jax.ShapeDtypeStruct((SEQ, D_FF), MXU_DTYPE),
    )(gu, gu)


def swiglu_bwd(gu, ds, name):
    def body(a_ref, b_ref, ds_ref, o_ref):
        a = a_ref[...]
        sg = _sigmoid(a)
        dsv = ds_ref[...]

        @pl.when(pl.program_id(1) < 2)
        def _():
            o_ref[...] = (dsv * b_ref[...] * (sg * (1.0 + a * (1.0 - sg)))).astype(o_ref.dtype)

        @pl.when(pl.program_id(1) >= 2)
        def _():
            o_ref[...] = (dsv * a * sg).astype(o_ref.dtype)

    return pl.pallas_call(
        body, name=name, grid=(SEQ // ROW_TILE, 4),
        in_specs=[pl.BlockSpec((ROW_TILE, FF_TILE), lambda i, j: (i, j % 2)),
                  pl.BlockSpec((ROW_TILE, FF_TILE), lambda i, j: (i, j % 2 + 2)),
                  pl.BlockSpec((ROW_TILE, FF_TILE), lambda i, j: (i, j % 2))],
        out_specs=pl.BlockSpec((ROW_TILE, FF_TILE), lambda i, j: (i, j)),
        out_shape=jax.ShapeDtypeStruct((SEQ, 2 * D_FF), MXU_DTYPE),
    )(gu, gu, ds)


GATE_HG_BLK = 6656 // 512
GATE_ATT_BLK = 7680 // 512


def merge_fwd(z, bh, ba, name):
    def body(gh_ref, ga_ref, bh_ref, ba_ref, o_ref):
        o_ref[...] = (_sigmoid(gh_ref[...]) * bh_ref[...] + _sigmoid(ga_ref[...]) * ba_ref[...]).astype(o_ref.dtype)

    blk = pl.BlockSpec((ROW_TILE, 512), lambda i, j: (i, j))
    return pl.pallas_call(
        body, name=name, grid=(SEQ // ROW_TILE, 2),
        in_specs=[pl.BlockSpec((ROW_TILE, 512), lambda i, j: (i, GATE_HG_BLK + j)),
                  pl.BlockSpec((ROW_TILE, 512), lambda i, j: (i, GATE_ATT_BLK + j)), blk, blk],
        out_specs=blk, out_shape=jax.ShapeDtypeStruct((SEQ, D_MODEL), MXU_DTYPE),
    )(z, z, bh, ba)


def merge_bwd(z, bh, ba, dm, name):
    def body(gh_ref, ga_ref, bh_ref, ba_ref, dm_ref, dbh_ref, dba_ref, dgh_ref, dga_ref):
        dmv = dm_ref[...]
        sh = _sigmoid(gh_ref[...])
        sa = _sigmoid(ga_ref[...])
        dbh_ref[...] = (dmv * sh).astype(dbh_ref.dtype)
        dba_ref[...] = (dmv * sa).astype(dba_ref.dtype)
        dgh_ref[...] = (dmv * bh_ref[...] * (sh * (1.0 - sh))).astype(dgh_ref.dtype)
        dga_ref[...] = (dmv * ba_ref[...] * (sa * (1.0 - sa))).astype(dga_ref.dtype)

    blk = pl.BlockSpec((ROW_TILE, 512), lambda i, j: (i, j))
    out = jax.ShapeDtypeStruct((SEQ, D_MODEL), MXU_DTYPE)
    return pl.pallas_call(
        body, name=name, grid=(SEQ // ROW_TILE, 2),
        in_specs=[pl.BlockSpec((ROW_TILE, 512), lambda i, j: (i, GATE_HG_BLK + j)),
                  pl.BlockSpec((ROW_TILE, 512), lambda i, j: (i, GATE_ATT_BLK + j)), blk, blk, blk],
        out_specs=[blk, blk, blk, blk], out_shape=[out, out, out, out],
    )(z, z, bh, ba, dm)


N_CHUNKS = SEQ // HG_CHUNK


def _hgrn_gates(q, fp, lb):
    C = HG_CHUNK
    sg = _sigmoid(fp)
    f = lb + (1.0 - lb) * sg
    lf = jnp.log(f)
    row = lax.broadcasted_iota(jnp.int32, (C, C), 0)
    col = lax.broadcasted_iota(jnp.int32, (C, C), 1)
    causal = row >= col
    G = _dot_f32(causal.astype(F32), lf)
    eG = jnp.exp(G)
    enG = jnp.exp(-G)
    qg = q * eG
    kg = (1.0 - f) * enG
    A = jnp.where(causal, _hdot(qg, kg, tb=True), 0.0)
    egl = jnp.exp(jnp.sum(lf, axis=0, keepdims=True))
    return sg, f, causal, eG, enG, qg, kg, A, egl


def hgrn_fwd(z, lb, gain, name):
    C, K = HG_CHUNK, HG_DIM

    def body(q_ref, f_ref, v_ref, og_ref, p_ref, g_ref, y_ref, o_ref, st_ref, state):
        @pl.when(pl.program_id(0) == 0)
        def _():
            state[...] = jnp.zeros_like(state)

        for h in range(HG_HEADS):
            hd = pl.ds(h * K, K)
            v = v_ref[:, hd]
            _, _, _, _, _, qg, kg, A, egl = _hgrn_gates(q_ref[:, hd], f_ref[:, hd], p_ref[:, hd])
            st = state[h]
            st_ref[h, 0] = st
            o = _hdot(A, v) + _hdot(qg, st, tb=True)
            state[h] = st * egl + _hdot(v, kg * egl, ta=True)
            o_ref[:, hd] = o
            rs = lax.rsqrt(jnp.mean(o * o, axis=-1, keepdims=True) + EPS)
            og = og_ref[:, hd]
            y_ref[:, hd] = (((o * rs) * g_ref[:, hd]) * (og * _sigmoid(og))).astype(y_ref.dtype)

    def zcol(section):
        return pl.BlockSpec((C, HG_WIDTH), lambda c: (c, section))

    vec = pl.BlockSpec((1, HG_WIDTH), lambda c: (0, 0))
    blk = pl.BlockSpec((C, HG_WIDTH), lambda c: (c, 0))
    return pl.pallas_call(
        body, name=name, grid=(N_CHUNKS,),
        in_specs=[zcol(0), zcol(1), zcol(2), zcol(3), vec, vec],
        out_specs=[blk, blk, pl.BlockSpec((HG_HEADS, 1, K, K), lambda c: (0, c, 0, 0))],
        out_shape=[jax.ShapeDtypeStruct((SEQ, HG_WIDTH), MXU_DTYPE), jax.ShapeDtypeStruct((SEQ, HG_WIDTH), F32),
                   jax.ShapeDtypeStruct((HG_HEADS, N_CHUNKS, K, K), F32)],
        scratch_shapes=[pltpu.VMEM((HG_HEADS, K, K), F32)],
        compiler_params=pltpu.CompilerParams(dimension_semantics=("arbitrary",)),
    )(z, z, z, z, lb, gain)


def hgrn_bwd(z, lb, gain, o_raw, states, dy, name):
    C, K = HG_CHUNK, HG_DIM

    def body(q_ref, f_ref, v_ref, og_ref, p_ref, g_ref, o_ref, st_ref, dy_ref,
             dq_ref, dfp_ref, dv_ref, dog_ref, dlb_ref, dgain_ref, dstate):
        @pl.when(pl.program_id(0) == 0)
        def _():
            dstate[...] = jnp.zeros_like(dstate)
            dlb_ref[...] = jnp.zeros_like(dlb_ref)
            dgain_ref[...] = jnp.zeros_like(dgain_ref)

        last = lax.broadcasted_iota(jnp.int32, (C, K), 0) == C - 1
        row = lax.broadcasted_iota(jnp.int32, (C, C), 0)
        col = lax.broadcasted_iota(jnp.int32, (C, C), 1)
        anti_causal = (col >= row).astype(F32)
        for h in range(HG_HEADS):
            hd = pl.ds(h * K, K)
            v = v_ref[:, hd]
            lb = p_ref[:, hd]
            sg, f, causal, eG, enG, qg, kg, A, egl = _hgrn_gates(q_ref[:, hd], f_ref[:, hd], lb)
            kd = kg * egl
            st = st_ref[h, 0]
            dst = dstate[h]
            o = o_ref[:, hd]
            og = og_ref[:, hd]
            gain_v = g_ref[:, hd]
            dyv = dy_ref[:, hd]
            rs = lax.rsqrt(jnp.mean(o * o, axis=-1, keepdims=True) + EPS)
            on = o * rs
            sgo = _sigmoid(og)
            silu = og * sgo
            dog_ref[:, hd] = (dyv * (on * gain_v) * (sgo * (1.0 + og * (1.0 - sgo)))).astype(dog_ref.dtype)
            dgain_ref[:, hd] += jnp.sum(dyv * silu * on, axis=0, keepdims=True)
            don = dyv * gain_v * silu
            do = rs * (don - on * jnp.mean(don * on, axis=-1, keepdims=True))
            dA = jnp.where(causal, _hdot(do, v, tb=True), 0.0)
            dv_ref[:, hd] = (_hdot(A, do, ta=True) + _hdot(kd, dst, tb=True)).astype(dv_ref.dtype)
            dqg = _hdot(dA, kg) + _hdot(do, st)
            dkg = _hdot(dA, qg, ta=True)
            dkd = _hdot(v, dst)
            dstate[h] = dst * egl + _hdot(do, qg, ta=True)
            dgl = jnp.sum(st * dst, axis=0, keepdims=True) * egl
            dq_ref[:, hd] = (dqg * eG).astype(dq_ref.dtype)
            dk = dkg * enG + dkd * (enG * egl)
            dG = dqg * qg - dkg * kg - dkd * kd
            extra = jnp.sum(dkd * kd, axis=0, keepdims=True) + dgl
            dG = dG + jnp.where(last, extra, 0.0)
            dlf = _dot_f32(anti_causal, dG)
            df = dlf / f - dk
            dfp_ref[:, hd] = (df * (1.0 - lb) * (sg * (1.0 - sg))).astype(dfp_ref.dtype)
            dlb_ref[:, hd] += jnp.sum(df * (1.0 - sg), axis=0, keepdims=True)

    def rc(c):
        return N_CHUNKS - 1 - c

    def zcol(section):
        return pl.BlockSpec((C, HG_WIDTH), lambda c: (rc(c), section))

    vec = pl.BlockSpec((1, HG_WIDTH), lambda c: (0, 0))
    blk = pl.BlockSpec((C, HG_WIDTH), lambda c: (rc(c), 0))
    out = jax.ShapeDtypeStruct((SEQ, HG_WIDTH), MXU_DTYPE)
    small = jax.ShapeDtypeStruct((1, HG_WIDTH), F32)
    return pl.pallas_call(
        body, name=name, grid=(N_CHUNKS,),
        in_specs=[zcol(0), zcol(1), zcol(2), zcol(3), vec, vec, blk,
                  pl.BlockSpec((HG_HEADS, 1, K, K), lambda c: (0, rc(c), 0, 0)), blk],
        out_specs=[blk, blk, blk, blk, vec, vec],
        out_shape=[out, out, out, out, small, small],
        scratch_shapes=[pltpu.VMEM((HG_HEADS, K, K), F32)],
        compiler_params=pltpu.CompilerParams(dimension_semantics=("arbitrary",)),
    )(z, z, z, z, lb, gain, o_raw, states, dy)


N_GROUPS = len(ATT_GROUPS)
HEAD_PAIRS = ATT_WIDTH // 128
ATT_COL0 = 4 * HG_WIDTH
UNROLLED_SUBSEQS = 4


def _alibi_coef():
    n = N_GROUPS * ATT_HEADS
    slopes = np.exp2(-ALIBI_MAX * np.arange(1, n + 1, dtype=np.float32) / n).astype(np.float32)
    dil = np.repeat(np.array([d for _, d in ATT_GROUPS], np.float32), ATT_HEADS)
    return jnp.asarray(slopes * dil, F32)


def _att_masks():
    B = ATT_BLOCK
    qi = lax.broadcasted_iota(jnp.int32, (B, B), 0)
    kj = lax.broadcasted_iota(jnp.int32, (B, B), 1)
    return qi, kj, (qi - kj).astype(F32), (qi + B - kj).astype(F32)


def _subseq_rows(r, d):
    return pl.ds(r, ATT_BLOCK, stride=d) if d > 1 else pl.ds(0, ATT_BLOCK)


def _for_each_subseq(d, fn):
    if d <= UNROLLED_SUBSEQS:
        for r in range(d):
            fn(r)
    else:
        lax.fori_loop(0, d, lambda r, carry: (fn(r), carry)[1], 0)


def _att_specs(g):
    d = ATT_GROUPS[g][1]
    R = ATT_BLOCK * d
    n_slabs = SEQ // R
    col0 = (ATT_COL0 + g * 3 * ATT_WIDTH) // 128

    def cur(col):
        return pl.BlockSpec((R, 128), lambda hp, s: (s, col + hp))

    def prev(col):
        return pl.BlockSpec((R, 128), lambda hp, s: (jnp.maximum(s - 1, 0), col + hp))

    def nxt(col):
        return pl.BlockSpec((R, 128), lambda hp, s: (jnp.minimum(s + 1, n_slabs - 1), col + hp))

    return d, R, n_slabs, col0, cur, prev, nxt


def _head_lanes(j):
    lane = lax.broadcasted_iota(jnp.int32, (ATT_BLOCK, 128), 1)
    return (lane >= 64 * j) & (lane < 64 * (j + 1))


def _lane_value(x, sel):
    return jnp.max(jnp.where(sel, x, -3e38), axis=-1, keepdims=True)


def att_fwd(z, g, name):
    B = ATT_BLOCK
    d, R, n_slabs, col0, cur, prev, _ = _att_specs(g)
    has_prev = n_slabs > 1

    def body(coef_ref, *refs):
        if has_prev:
            q_ref, kc_ref, vc_ref, kp_ref, vp_ref, o_ref, l_ref = refs
        else:
            q_ref, kc_ref, vc_ref, o_ref, l_ref = refs
        hp, s = pl.program_id(0), pl.program_id(1)
        qi, kj, d_cur, d_prev = _att_masks()
        m_cur = kj <= qi
        m_prev = kj >= qi + jnp.where(s == 0, 4 * B, 0)

        def one(r):
            rows = _subseq_rows(r, d)
            q, kc, vc = q_ref[rows, :], kc_ref[rows, :], vc_ref[rows, :]
            if has_prev:
                kpv, vpv = kp_ref[rows, :], vp_ref[rows, :]
            o_acc = jnp.zeros((B, 128), F32)
            l_acc = jnp.zeros((B, 128), F32)
            for j in range(2):
                sel = _head_lanes(j)
                cf = coef_ref[g * ATT_HEADS + hp * 2 + j]
                qh = jnp.where(sel, q, 0.0)
                s_cur = jnp.where(m_cur, _dot(qh, kc, tb=True) * 0.125 - cf * d_cur, NEG_INF)
                mx = jnp.max(s_cur, axis=-1, keepdims=True)
                if has_prev:
                    s_prev = jnp.where(m_prev, _dot(qh, kpv, tb=True) * 0.125 - cf * d_prev, NEG_INF)
                    mx = jnp.maximum(mx, jnp.max(s_prev, axis=-1, keepdims=True))
                e_cur = jnp.exp(s_cur - mx)
                den = jnp.sum(e_cur, axis=-1, keepdims=True)
                if has_prev:
                    e_prev = jnp.exp(s_prev - mx)
                    den = den + jnp.sum(e_prev, axis=-1, keepdims=True)
                inv = 1.0 / den
                oh = _dot(e_cur * inv, vc)
                if has_prev:
                    oh = oh + _dot(e_prev * inv, vpv)
                o_acc = jnp.where(sel, oh, o_acc)
                l_acc = jnp.where(sel, mx + jnp.log(den), l_acc)
            o_ref[rows, :] = o_acc
            l_ref[rows, :] = l_acc

        _for_each_subseq(d, one)

    in_specs = [pl.BlockSpec(memory_space=pltpu.SMEM), cur(col0), cur(col0 + 4), cur(col0 + 8)]
    args = [_alibi_coef(), z, z, z]
    if has_prev:
        in_specs += [prev(col0 + 4), prev(col0 + 8)]
        args += [z, z]
    out = jax.ShapeDtypeStruct((SEQ, ATT_WIDTH), F32)
    return pl.pallas_call(
        body, name=name, grid=(HEAD_PAIRS, n_slabs), in_specs=in_specs,
        out_specs=[cur(0), cur(0)], out_shape=[out, out],
        compiler_params=pltpu.CompilerParams(dimension_semantics=("parallel", "arbitrary")),
    )(*args)


def att_bwd(z, l, do, corr, g, name):
    B = ATT_BLOCK
    d, R, n_slabs, col0, cur, prev, nxt = _att_specs(g)
    neighbours = n_slabs > 1

    def body(coef_ref, *refs):
        if neighbours:
            (q_ref, kc_ref, vc_ref, l_ref, do_ref, cr_ref, kp_ref, vp_ref, qn_ref, ln_ref, don_ref, crn_ref,
             dq_ref, dk_ref, dv_ref, dq_sc, dk_sc, dv_sc) = refs
        else:
            q_ref, kc_ref, vc_ref, l_ref, do_ref, cr_ref, dq_ref, dk_ref, dv_ref, dq_sc, dk_sc, dv_sc = refs
        hp, s = pl.program_id(0), pl.program_id(1)
        qi, kj, d_cur, d_prev = _att_masks()
        m_cc = kj <= qi
        m_cp = kj >= qi + jnp.where(s == 0, 4 * B, 0)
        m_nc = kj >= qi + jnp.where(s == n_slabs - 1, 4 * B, 0)

        def one(r):
            rows = _subseq_rows(r, d)
            q, kc, vc, lv, dov, crv = (ref[rows, :] for ref in (q_ref, kc_ref, vc_ref, l_ref, do_ref, cr_ref))
            if neighbours:
                kpv, vpv, qn, lnv, donv, crnv = (ref[rows, :] for ref in (kp_ref, vp_ref, qn_ref, ln_ref, don_ref, crn_ref))
            dq_acc = jnp.zeros((B, 128), F32)
            dk_acc = jnp.zeros((B, 128), F32)
            dv_acc = jnp.zeros((B, 128), F32)
            for j in range(2):
                sel = _head_lanes(j)
                cf = coef_ref[g * ATT_HEADS + hp * 2 + j]
                qh = jnp.where(sel, q, 0.0)
                doh = jnp.where(sel, dov, 0.0)
                lse, cr = _lane_value(lv, sel), _lane_value(crv, sel)
                p_cc = jnp.exp(jnp.where(m_cc, _dot(qh, kc, tb=True) * 0.125 - cf * d_cur, NEG_INF) - lse)
                ds_cc = p_cc * (_dot(doh, vc, tb=True) + cr)
                dqh = _dot(ds_cc, kc)
                dkh = _dot(ds_cc, qh, ta=True)
                dvh = _dot(p_cc, doh, ta=True)
                if neighbours:
                    qnh = jnp.where(sel, qn, 0.0)
                    donh = jnp.where(sel, donv, 0.0)
                    lse_n, cr_n = _lane_value(lnv, sel), _lane_value(crnv, sel)
                    p_cp = jnp.exp(jnp.where(m_cp, _dot(qh, kpv, tb=True) * 0.125 - cf * d_prev, NEG_INF) - lse)
                    p_nc = jnp.exp(jnp.where(m_nc, _dot(qnh, kc, tb=True) * 0.125 - cf * d_prev, NEG_INF) - lse_n)
                    ds_cp = p_cp * (_dot(doh, vpv, tb=True) + cr)
                    ds_nc = p_nc * (_dot(donh, vc, tb=True) + cr_n)
                    dqh = dqh + _dot(ds_cp, kpv)
                    dkh = dkh + _dot(ds_nc, qnh, ta=True)
                    dvh = dvh + _dot(p_nc, donh, ta=True)
                dq_acc = jnp.where(sel, dqh * 0.125, dq_acc)
                dk_acc = jnp.where(sel, dkh * 0.125, dk_acc)
                dv_acc = jnp.where(sel, dvh, dv_acc)
            dq_sc[rows, :] = dq_acc
            dk_sc[rows, :] = dk_acc
            dv_sc[rows, :] = dv_acc

        _for_each_subseq(d, one)
        dq_ref[...] = dq_sc[...].astype(dq_ref.dtype)
        dk_ref[...] = dk_sc[...].astype(dk_ref.dtype)
        dv_ref[...] = dv_sc[...].astype(dv_ref.dtype)

    in_specs = [pl.BlockSpec(memory_space=pltpu.SMEM), cur(col0), cur(col0 + 4), cur(col0 + 8), cur(0), cur(0), cur(0)]
    args = [_alibi_coef(), z, z, z, l, do, corr]
    if neighbours:
        in_specs += [prev(col0 + 4), prev(col0 + 8), nxt(col0), nxt(0), nxt(0), nxt(0)]
        args += [z, z, z, l, do, corr]
    out = jax.ShapeDtypeStruct((SEQ, ATT_WIDTH), MXU_DTYPE)
    return pl.pallas_call(
        body, name=name, grid=(HEAD_PAIRS, n_slabs), in_specs=in_specs,
        out_specs=[cur(0)] * 3, out_shape=[out] * 3,
        scratch_shapes=[pltpu.VMEM((R, 128), F32)] * 3,
        compiler_params=pltpu.CompilerParams(dimension_semantics=("parallel", "arbitrary"),
                                             vmem_limit_bytes=MATMUL_VMEM_BYTES),
    )(*args)


def _head_sum(x):
    i = lax.broadcasted_iota(jnp.int32, (128, 128), 0) // 64
    j = lax.broadcasted_iota(jnp.int32, (128, 128), 1) // 64
    return _dot_f32(x, (i == j).astype(F32))


def _group_weights(l0, l1, l2):
    mx = jnp.maximum(jnp.maximum(l0, l1), l2)
    e0, e1, e2 = jnp.exp(l0 - mx), jnp.exp(l1 - mx), jnp.exp(l2 - mx)
    inv = 1.0 / (e0 + e1 + e2)
    return e0 * inv, e1 * inv, e2 * inv


def att_combine_fwd(o, l, name):
    def body(o0, o1, o2, l0, l1, l2, y_ref):
        w0, w1, w2 = _group_weights(l0[...], l1[...], l2[...])
        y_ref[...] = (o0[...] * w0 + o1[...] * w1 + o2[...] * w2).astype(y_ref.dtype)

    blk = pl.BlockSpec((ROW_TILE, ATT_WIDTH), lambda i: (i, 0))
    return pl.pallas_call(
        body, name=name, grid=(SEQ // ROW_TILE,), in_specs=[blk] * 6, out_specs=blk,
        out_shape=jax.ShapeDtypeStruct((SEQ, ATT_WIDTH), MXU_DTYPE),
    )(*o, *l)


def att_combine_bwd(o, l, dy, name):
    def body(o0, o1, o2, l0, l1, l2, dy_ref, do0, do1, do2, cr0, cr1, cr2):
        w = _group_weights(l0[...], l1[...], l2[...])
        dyv = dy_ref[...]
        dw = [_head_sum(dyv * o_ref[...]) for o_ref in (o0, o1, o2)]
        tot = w[0] * dw[0] + w[1] * dw[1] + w[2] * dw[2]
        for g, (do_ref, cr_ref) in enumerate(((do0, cr0), (do1, cr1), (do2, cr2))):
            do_ref[...] = dyv * w[g]
            cr_ref[...] = -w[g] * tot

    blk = pl.BlockSpec((ROW_TILE, 128), lambda i, j: (i, j))
    out = jax.ShapeDtypeStruct((SEQ, ATT_WIDTH), F32)
    res = pl.pallas_call(
        body, name=name, grid=(SEQ // ROW_TILE, HEAD_PAIRS), in_specs=[blk] * 7, out_specs=[blk] * 6, out_shape=[out] * 6,
    )(*o, *l, dy)
    return res[:N_GROUPS], res[N_GROUPS:]


SUM_ROW_TILES = (256, 128, 64, 32, 16)
SUM_TILE_ELEMS = 128 * 1024


def _row_tile(rows, cols):
    fit = [t for t in SUM_ROW_TILES if rows % t == 0]
    return next((t for t in fit if t * cols <= SUM_TILE_ELEMS), fit[-1])


def _shard_shape(rows, cols, axis):
    return (rows // N_CHIPS, cols) if axis == 0 else (rows, cols // N_CHIPS)


def _half_shape(rows, cols, axis):
    return (rows, cols // 2) if axis == 0 else (rows // 2, cols)


def _piece_shape(rows, cols, axis):
    return (rows // N_CHIPS, cols // 2) if axis == 0 else (rows // 2, cols // N_CHIPS)


def place_own_block(shard, chip, rows, cols, axis, name):
    sr, sc = _shard_shape(rows, cols, axis)
    tr = _row_tile(sr, sc)

    def body(chip_ref, s_ref, o_ref):
        o_ref[...] = s_ref[...].astype(o_ref.dtype)

    if axis == 0:
        out_map = lambda i, chip_ref: (chip_ref[0] * (sr // tr) + i, 0)
    else:
        out_map = lambda i, chip_ref: (i, chip_ref[0])
    return pl.pallas_call(
        body, name=name, out_shape=jax.ShapeDtypeStruct((rows, cols), WEIGHT_COMM_DTYPE),
        grid_spec=pltpu.PrefetchScalarGridSpec(
            num_scalar_prefetch=1, grid=(sr // tr,), in_specs=[pl.BlockSpec((tr, sc), lambda i, chip_ref: (i, 0))],
            out_specs=pl.BlockSpec((tr, sc), out_map)),
    )(chip, shard)


def add_halves(g, theirs, core, rows, cols, axis, name):
    hr, hc = _half_shape(rows, cols, axis)
    tr = _row_tile(hr, hc)

    def body(core_ref, g_ref, t_ref, o_ref):
        o_ref[...] = (g_ref[...].astype(F32) + t_ref[...].astype(F32)).astype(o_ref.dtype)

    if axis == 0:
        g_map = lambda i, core_ref: (i, core_ref[0])
    else:
        g_map = lambda i, core_ref: (core_ref[0] * (hr // tr) + i, 0)
    blk = pl.BlockSpec((tr, hc), lambda i, core_ref: (i, 0))
    return pl.pallas_call(
        body, name=name, out_shape=jax.ShapeDtypeStruct((hr, hc), GRAD_COMM_DTYPE),
        grid_spec=pltpu.PrefetchScalarGridSpec(
            num_scalar_prefetch=1, grid=(hr // tr,), in_specs=[pl.BlockSpec((tr, hc), g_map), blk], out_specs=blk),
    )(core, g, theirs)


def add_pieces(half, got, chip, rows, cols, axis, name):
    hr, _ = _half_shape(rows, cols, axis)
    pr, pc = _piece_shape(rows, cols, axis)
    tr = _row_tile(pr, pc)

    def body(chip_ref, h_ref, got_ref, o_ref):
        o_ref[...] = (h_ref[...].astype(F32) + got_ref[0].astype(F32) + got_ref[1].astype(F32) + got_ref[2].astype(F32))

    if axis == 0:
        h_map = lambda i, chip_ref: (chip_ref[0] * (pr // tr) + i, 0)
    else:
        h_map = lambda i, chip_ref: (i, chip_ref[0])
    return pl.pallas_call(
        body, name=name, out_shape=jax.ShapeDtypeStruct((pr, pc), F32),
        grid_spec=pltpu.PrefetchScalarGridSpec(
            num_scalar_prefetch=1, grid=(pr // tr,),
            in_specs=[pl.BlockSpec((tr, pc), h_map), pl.BlockSpec((3, tr, pc), lambda i, chip_ref: (0, i, 0))],
            out_specs=pl.BlockSpec((tr, pc), lambda i, chip_ref: (i, 0))),
    )(chip, half, got)


def _adamw_math(w, g, m, v):
    nm = ADAM_B1 * m + (1.0 - ADAM_B1) * g
    nv = ADAM_B2 * v + (1.0 - ADAM_B2) * (g * g)
    m_hat = nm / (1.0 - ADAM_B1 ** ADAM_STEP)
    v_hat = nv / (1.0 - ADAM_B2 ** ADAM_STEP)
    return -ADAM_LR * (m_hat / (jnp.sqrt(v_hat) + ADAM_EPS) + ADAM_WD * w), nm, nv


def adamw(w, g, m, v, name):
    R, Cc = w.shape
    tr = _pick(R, (256, 128, 64, 8))

    def body(w_ref, g_ref, m_ref, v_ref, d_ref, nm_ref, nv_ref):
        d_ref[...], nm_ref[...], nv_ref[...] = _adamw_math(w_ref[...], g_ref[...], m_ref[...], v_ref[...])

    blk = pl.BlockSpec((tr, Cc), lambda i: (i, 0))
    out = jax.ShapeDtypeStruct((R, Cc), F32)
    return pl.pallas_call(
        body, name=name, grid=(R // tr,), in_specs=[blk] * 4, out_specs=[blk] * 3, out_shape=[out, out, out],
    )(w, g, m, v)


def adamw_halves(w, mine, theirs, m, v, core, rows, cols, axis, name):
    sr, sc = _shard_shape(rows, cols, axis)
    pr, pc = _piece_shape(rows, cols, axis)
    tr = _row_tile(pr, pc)
    nt = pr // tr

    def body(core_ref, w_ref, a_ref, b_ref, m_ref, v_ref, g_ref, d_ref, nm_ref, nv_ref):
        g = jnp.where(pl.program_id(0) == core_ref[0], a_ref[...], b_ref[...])
        g_ref[...] = g
        d_ref[...], nm_ref[...], nv_ref[...] = _adamw_math(w_ref[...], g, m_ref[...], v_ref[...])

    if axis == 0:
        full = pl.BlockSpec((tr, pc), lambda h, i, core_ref: (i, h))
    else:
        full = pl.BlockSpec((tr, pc), lambda h, i, core_ref: (h * nt + i, 0))
    part = pl.BlockSpec((tr, pc), lambda h, i, core_ref: (i, 0))
    out = jax.ShapeDtypeStruct((sr, sc), F32)
    return pl.pallas_call(
        body, name=name, out_shape=[out, out, out, out],
        grid_spec=pltpu.PrefetchScalarGridSpec(
            num_scalar_prefetch=1, grid=(2, nt), in_specs=[full, part, part, full, full], out_specs=[full] * 4),
    )(core, w, mine, theirs, m, v)


BIG = (
    ("ffn1_w_gate_up", D_MODEL, 2 * D_FF, 1),
    ("ffn1_w_down", D_FF, D_MODEL, 0),
    ("w_in", D_MODEL, IN_COLS, 1),
    ("w_branch_hg", HG_WIDTH, D_MODEL, 1),
    ("w_branch_att", ATT_WIDTH, D_MODEL, 1),
    ("w_out", D_MODEL, D_MODEL, 0),
    ("ffn2_w_gate_up", D_MODEL, 2 * D_FF, 1),
    ("ffn2_w_down", D_FF, D_MODEL, 0),
)
N_BIG = len(BIG)
ANY = pl.BlockSpec(memory_space=pl.ANY)


def _place():
    return lax.axis_index("x"), lax.axis_index("y"), lax.axis_index("c")


def _other_chips(x, y):
    return ((1 - x, y), (x, 1 - y), (1 - x, 1 - y))


MAX_COPY_CHUNKS = 16
CHUNK_ROW_ALIGN = 16


def _row_chunks(view):
    rows = view.shape[0]
    n = next(n for n in range(MAX_COPY_CHUNKS, 0, -1) if rows % (CHUNK_ROW_ALIGN * n) == 0 or n == 1)
    step = rows // n
    return [pl.ds(i * step, step) for i in range(n)]


def _remote(src, dst, send_sem, recv_sem, device):
    return pltpu.make_async_remote_copy(src_ref=src, dst_ref=dst, send_sem=send_sem, recv_sem=recv_sem,
                                        device_id=device, device_id_type=MESH)


def _start_remote(src, dst, send_sem, recv_sem, device):
    for rows in _row_chunks(src):
        _remote(src.at[rows, :], dst.at[rows, :], send_sem, recv_sem, device).start()
    return _remote(src, dst, send_sem, recv_sem, device)


HBM = pl.BlockSpec(memory_space=pltpu.HBM)
SEM = pl.BlockSpec(memory_space=pltpu.SEMAPHORE)
SPLIT_COPY_EFFECT = pltpu.SideEffectType.DATAFLOW_SIDE_EFFECTING
GROUPS = {"ffn1": (0, 1), "mix": (2, 3, 4, 5), "ffn2": (6, 7)}


def _in_hbm(a):
    return pltpu.with_memory_space_constraint(a, pltpu.HBM)


class _SemList:
    def __init__(self, refs):
        self.refs = refs
        self.at = self

    def __getitem__(self, index):
        w, k = index
        return self.refs[3 * w + k]


def _gather_piece(ref, rows, cols, axis, chip, c):
    sr, sc = _shard_shape(rows, cols, axis)
    j = 2 * chip[0] + chip[1]
    if axis == 0:
        return ref.at[pl.ds(j * sr + c * (sr // 2), sr // 2), :]
    return ref.at[pl.ds(c * (sr // 2), sr // 2), pl.ds(pl.multiple_of(j * sc, 128), sc)]


def _start_gather_sends(bufs, ws, send_sems, recv_sems):
    x, y, c = _place()
    for w, (_, r, cc, ax) in enumerate(ws):
        mine = _gather_piece(bufs[w], r, cc, ax, (x, y), c)
        for k, chip in enumerate(_other_chips(x, y)):
            _start_remote(mine, mine, send_sems.at[w, k], recv_sems.at[w, k], (*chip, c))


def _wait_gather_sends(bufs, ws, send_sems, recv_sems, forward=None):
    x, y, c = _place()
    for w, (_, r, cc, ax) in enumerate(ws):
        for k, chip in enumerate(_other_chips(x, y)):
            got = _gather_piece(bufs[w], r, cc, ax, chip, c)
            _remote(got, got, send_sems.at[w, k], recv_sems.at[w, k], (x, y, c)).wait_recv()
            if forward is not None:
                forward(w, k, got)
    for w, (_, r, cc, ax) in enumerate(ws):
        mine = _gather_piece(bufs[w], r, cc, ax, (x, y), c)
        for k in range(3):
            _remote(mine, mine, send_sems.at[w, k], recv_sems.at[w, k], (x, y, c)).wait_send()


def _forward_halves(bufs, ws, send_sems, recv_sems, first=None):
    x, y, c = _place()
    passed = []

    def forward(w, k, got):
        passed.append(_start_remote(got, got, send_sems.at[w, k], recv_sems.at[w, k], (x, y, 1 - c)))

    if first is not None:
        first(forward)
    else:
        for w, (_, r, cc, ax) in enumerate(ws):
            for k, chip in enumerate(_other_chips(x, y)):
                forward(w, k, _gather_piece(bufs[w], r, cc, ax, chip, c))
    for w, (_, r, cc, ax) in enumerate(ws):
        for k, chip in enumerate(_other_chips(x, y)):
            got = _gather_piece(bufs[w], r, cc, ax, chip, 1 - c)
            _remote(got, got, send_sems.at[w, k], recv_sems.at[w, k], (x, y, c)).wait_recv()
    for cp in passed:
        cp.wait_send()


def all_gather_weights(placed, group):
    ws = [BIG[i] for i in GROUPS[group]]
    n = len(ws)

    def body(*refs):
        bufs = refs[n:2 * n]
        ici_send, ici_recv, d2d_send, d2d_recv = refs[2 * n:]
        _start_gather_sends(bufs, ws, ici_send, ici_recv)
        _forward_halves(bufs, ws, d2d_send, d2d_recv,
                        first=lambda forward: _wait_gather_sends(bufs, ws, ici_send, ici_recv, forward))

    return pl.pallas_call(
        body, name=f"all_gather_{group}", in_specs=[ANY] * n, out_specs=[ANY] * n,
        out_shape=[jax.ShapeDtypeStruct((r, cc), WEIGHT_COMM_DTYPE) for _, r, cc, _ in ws],
        input_output_aliases={w: w for w in range(n)},
        scratch_shapes=[pltpu.SemaphoreType.DMA((n, 3))] * 4,
    )(*placed)


def gather_start(placed, group):
    ws = [BIG[i] for i in GROUPS[group]]
    n = len(ws)

    def body(*refs):
        bufs = refs[:n]
        send_sems, recv_sems = _SemList(refs[n:n + 3 * n]), _SemList(refs[n + 3 * n:n + 6 * n])
        token = refs[-1]
        _start_gather_sends(bufs, ws, send_sems, recv_sems)
        token[...] = jnp.zeros_like(token)

    out = pl.pallas_call(
        body, name=f"gather_start_{group}", in_specs=[HBM] * n,
        out_specs=[SEM] * (6 * n) + [HBM] * n + [pl.BlockSpec(memory_space=pltpu.VMEM)],
        out_shape=[pltpu.SemaphoreType.DMA(())] * (6 * n)
        + [pltpu.HBM((r, cc), WEIGHT_COMM_DTYPE) for _, r, cc, _ in ws] + [jax.ShapeDtypeStruct((8, 128), F32)],
        input_output_aliases={w: 6 * n + w for w in range(n)},
        compiler_params=pltpu.CompilerParams(has_side_effects=SPLIT_COPY_EFFECT),
    )(*[_in_hbm(p) for p in placed])
    return out[:3 * n], out[3 * n:6 * n], out[6 * n:7 * n], out[-1]


def gather_wait(bufs, send_sems, recv_sems, after, group):
    ws = [BIG[i] for i in GROUPS[group]]
    n = len(ws)

    def body(*refs):
        _wait_gather_sends(refs[:n], ws, _SemList(refs[n:n + 3 * n]), _SemList(refs[n + 3 * n:n + 6 * n]))

    return pl.pallas_call(
        body, name=f"gather_wait_{group}", in_specs=[HBM] * n + [SEM] * (6 * n) + [ANY], out_specs=[HBM] * n,
        out_shape=[pltpu.HBM((r, cc), WEIGHT_COMM_DTYPE) for _, r, cc, _ in ws],
        input_output_aliases={w: w for w in range(n)},
        compiler_params=pltpu.CompilerParams(has_side_effects=SPLIT_COPY_EFFECT),
    )(*bufs, *send_sems, *recv_sems, after)


def gather_forward(bufs, group):
    ws = [BIG[i] for i in GROUPS[group]]
    n = len(ws)

    def body(*refs):
        _forward_halves(refs[n:2 * n], ws, refs[2 * n], refs[2 * n + 1])

    return pl.pallas_call(
        body, name=f"gather_forward_{group}", in_specs=[ANY] * n, out_specs=[ANY] * n,
        out_shape=[jax.ShapeDtypeStruct((r, cc), WEIGHT_COMM_DTYPE) for _, r, cc, _ in ws],
        input_output_aliases={w: w for w in range(n)},
        scratch_shapes=[pltpu.SemaphoreType.DMA((n, 3))] * 2,
    )(*bufs)


def _half(ref, rows, cols, axis, c):
    if axis == 0:
        return ref.at[:, pl.ds(pl.multiple_of(c * (cols // 2), 128), cols // 2)]
    return ref.at[pl.ds(c * (rows // 2), rows // 2), :]


def _piece_of_half(ref, rows, cols, axis, chip):
    j = 2 * chip[0] + chip[1]
    pr, pc = _piece_shape(rows, cols, axis)
    if axis == 0:
        return ref.at[pl.ds(j * pr, pr), :]
    return ref.at[:, pl.ds(pl.multiple_of(j * pc, 128), pc)]


def exchange_halves(grads, group):
    ws = [BIG[i] for i in GROUPS[group]]
    n = len(ws)

    def body(*refs):
        ins, theirs = refs[:n], refs[n:2 * n]
        send_sems, recv_sems = refs[2 * n:]
        x, y, c = _place()
        copies = [_start_remote(_half(ins[w], r, cc, ax, 1 - c), theirs[w], send_sems.at[w], recv_sems.at[w], (x, y, 1 - c))
                  for w, (_, r, cc, ax) in enumerate(ws)]
        for cp in copies:
            cp.wait()

    return pl.pallas_call(
        body, name=f"exchange_halves_{group}", in_specs=[ANY] * n, out_specs=[ANY] * n,
        out_shape=[jax.ShapeDtypeStruct(_half_shape(r, cc, ax), GRAD_COMM_DTYPE) for _, r, cc, ax in ws],
        scratch_shapes=[pltpu.SemaphoreType.DMA((n,)), pltpu.SemaphoreType.DMA((n,))],
    )(*grads)


def _scatter_copies(halves, got, ws, send_sems, recv_sems, start):
    x, y, c = _place()
    copies = []
    for w, (_, r, cc, ax) in enumerate(ws):
        for k, chip in enumerate(_other_chips(x, y)):
            args = (_piece_of_half(halves[w], r, cc, ax, chip), got[w].at[k], send_sems.at[w, k], recv_sems.at[w, k], (*chip, c))
            copies.append(_start_remote(*args) if start else _remote(*args))
    return copies


def scatter_start(halves, group):
    ws = [BIG[i] for i in GROUPS[group]]
    n = len(ws)

    def body(*refs):
        sems = refs[2 * n:8 * n]
        _scatter_copies(refs[:n], refs[n:2 * n], ws, _SemList(sems[:3 * n]), _SemList(sems[3 * n:]), start=True)
        refs[-1][...] = jnp.zeros_like(refs[-1])

    landing = [lax.empty((3,) + _piece_shape(r, cc, ax), GRAD_COMM_DTYPE) for _, r, cc, ax in ws]
    out = pl.pallas_call(
        body, name=f"scatter_start_{group}", in_specs=[HBM] * (2 * n),
        out_specs=[SEM] * (6 * n) + [HBM] * (2 * n) + [pl.BlockSpec(memory_space=pltpu.VMEM)],
        out_shape=[pltpu.SemaphoreType.DMA(())] * (6 * n)
        + [pltpu.HBM(_half_shape(r, cc, ax), GRAD_COMM_DTYPE) for _, r, cc, ax in ws]
        + [pltpu.HBM((3,) + _piece_shape(r, cc, ax), GRAD_COMM_DTYPE) for _, r, cc, ax in ws]
        + [jax.ShapeDtypeStruct((8, 128), F32)],
        input_output_aliases={i: 6 * n + i for i in range(2 * n)},
        compiler_params=pltpu.CompilerParams(has_side_effects=SPLIT_COPY_EFFECT),
    )(*[_in_hbm(h) for h in halves], *[_in_hbm(b) for b in landing])
    return out[:3 * n], out[3 * n:6 * n], out[6 * n:7 * n], out[7 * n:8 * n], out[-1]


def scatter_wait(halves, got, send_sems, recv_sems, after, group):
    ws = [BIG[i] for i in GROUPS[group]]
    n = len(ws)

    def body(*refs):
        sems = refs[2 * n:8 * n]
        for cp in _scatter_copies(refs[:n], refs[n:2 * n], ws, _SemList(sems[:3 * n]), _SemList(sems[3 * n:]), start=False):
            cp.wait_send()
            cp.wait_recv()

    out = pl.pallas_call(
        body, name=f"scatter_wait_{group}", in_specs=[HBM] * (2 * n) + [SEM] * (6 * n) + [ANY], out_specs=[HBM] * (2 * n),
        out_shape=[pltpu.HBM(_half_shape(r, cc, ax), GRAD_COMM_DTYPE) for _, r, cc, ax in ws]
        + [pltpu.HBM((3,) + _piece_shape(r, cc, ax), GRAD_COMM_DTYPE) for _, r, cc, ax in ws],
        input_output_aliases={i: i for i in range(2 * n)},
        compiler_params=pltpu.CompilerParams(has_side_effects=SPLIT_COPY_EFFECT),
    )(*halves, *got, *send_sems, *recv_sems, after)
    return out[:n], out[n:]


def exchange_reduced(pieces, group):
    ws = [BIG[i] for i in GROUPS[group]]
    n = len(ws)

    def body(*refs):
        ins, theirs = refs[:n], refs[n:2 * n]
        send_sems, recv_sems = refs[2 * n:]
        x, y, c = _place()
        copies = [_start_remote(ins[w], theirs[w], send_sems.at[w], recv_sems.at[w], (x, y, 1 - c)) for w in range(n)]
        for cp in copies:
            cp.wait()

    return pl.pallas_call(
        body, name=f"exchange_reduced_{group}", in_specs=[ANY] * n, out_specs=[ANY] * n,
        out_shape=[jax.ShapeDtypeStruct(_piece_shape(r, cc, ax), F32) for _, r, cc, ax in ws],
        scratch_shapes=[pltpu.SemaphoreType.DMA((n,)), pltpu.SemaphoreType.DMA((n,))],
    )(*pieces)


N_DEV = 8
SMALL_ROWS = 8


def all_reduce_small(packed):
    def body(x_ref, o_ref, gathered, send_sems, recv_sems):
        x, y, c = _place()
        me = 4 * x + 2 * y + c
        gathered[me] = x_ref[...]
        copies = []
        for k in range(1, N_DEV):
            peer = (x ^ (k >> 2), y ^ ((k >> 1) & 1), c ^ (k & 1))
            cp = pltpu.make_async_remote_copy(
                src_ref=x_ref, dst_ref=gathered.at[me], send_sem=send_sems.at[k - 1], recv_sem=recv_sems.at[k - 1],
                device_id=peer, device_id_type=MESH)
            cp.start()
            copies.append(cp)
        for cp in copies:
            cp.wait()
        acc = gathered[0]
        for k in range(1, N_DEV):
            acc = acc + gathered[k]
        o_ref[...] = acc

    vm = pl.BlockSpec(memory_space=pltpu.VMEM)
    return pl.pallas_call(
        body, name="all_reduce_small", in_specs=[vm], out_specs=vm,
        out_shape=jax.ShapeDtypeStruct((SMALL_ROWS, D_MODEL), F32),
        scratch_shapes=[pltpu.VMEM((N_DEV, SMALL_ROWS, D_MODEL), F32), pltpu.SemaphoreType.DMA((N_DEV - 1,)),
                        pltpu.SemaphoreType.DMA((N_DEV - 1,))],
    )(packed)


def _swiglu_block_fwd(h, norm_g, w_gu, w_down, tag):
    n = rmsnorm_fwd(h, norm_g, f"{tag}_norm")
    gu = matmul(n, w_gu, name=f"{tag}_gate_up")
    s = swiglu_fwd(gu, f"{tag}_swiglu")
    h_out = matmul(s, w_down, res=h, scale=0.5, name=f"{tag}_down")
    return h_out, (n, gu, s)


def _swiglu_block_bwd(h, norm_g, w_gu, w_down, saved, dh_out, tag):
    n, gu, s = saved
    df = dh_out.astype(MXU_DTYPE)
    d_down = matmul(s, df, ta=True, scale=0.5, out_dtype=GRAD_COMM_DTYPE, name=f"{tag}_d_w_down")
    ds = matmul(df, w_down, tb=True, scale=0.5, name=f"{tag}_d_s")
    dgu = swiglu_bwd(gu, ds, f"{tag}_swiglu_bwd")
    d_gu = matmul(n, dgu, ta=True, out_dtype=GRAD_COMM_DTYPE, name=f"{tag}_d_w_gate_up")
    dn = matmul(dgu, w_gu, tb=True, name=f"{tag}_d_n")
    dh, dg = rmsnorm_bwd(h, norm_g, dn, dh_out, f"{tag}_norm_bwd")
    return dh, dg, d_gu, d_down


def local_step(x, target, small, exchange):
    big = {}
    x, big_ffn1 = exchange.weights("ffn1", x)
    big.update(big_ffn1)
    h1, saved1 = _swiglu_block_fwd(x, small["ffn1_norm"], big["ffn1_w_gate_up"], big["ffn1_w_down"], "ffn1")
    h1, big_mix = exchange.weights("mix", h1)
    big.update(big_mix)
    u = rmsnorm_fwd(h1, small["mix_norm"], "mix_norm")
    z = matmul(u, big["w_in"], name="w_in")
    p = small["hg_lower_bounds"]
    lb = 1.0 / (1.0 + jnp.exp(p[1:2] - p[0:1]))
    y_hg, o_raw, states = hgrn_fwd(z, lb, small["hg_out_norm"], "hgrn_fwd")
    o_att, l_att = zip(*[att_fwd(z, g, f"att_fwd_{g}") for g in range(N_GROUPS)])
    y_att = att_combine_fwd(o_att, l_att, "att_combine")
    bh = matmul(y_hg, big["w_branch_hg"], name="branch_hg")
    ba = matmul(y_att, big["w_branch_att"], name="branch_att")
    merged = merge_fwd(z, bh, ba, "merge")
    h2 = matmul(merged, big["w_out"], res=h1, name="w_out")
    h2, big_ffn2 = exchange.weights("ffn2", h2)
    big.update(big_ffn2)
    h3, saved2 = _swiglu_block_fwd(h2, small["ffn2_norm"], big["ffn2_w_gate_up"], big["ffn2_w_down"], "ffn2")
    dh3, d_final, loss = final_norm_loss(h3, small["final_norm"], target, "final_norm_loss")

    gs, gb = {"final_norm": d_final}, {}
    dh2, gs["ffn2_norm"], gb["ffn2_w_gate_up"], gb["ffn2_w_down"] = _swiglu_block_bwd(
        h2, small["ffn2_norm"], big["ffn2_w_gate_up"], big["ffn2_w_down"], saved2, dh3, "ffn2")
    dh2 = exchange.gradients("ffn2", gb, dh2)
    dh2_m = dh2.astype(MXU_DTYPE)
    gb["w_out"] = matmul(merged, dh2_m, ta=True, out_dtype=GRAD_COMM_DTYPE, name="d_w_out")
    dmerged = matmul(dh2_m, big["w_out"], tb=True, name="d_merged")
    dbh, dba, dgh, dga = merge_bwd(z, bh, ba, dmerged, "merge_bwd")
    gb["w_branch_hg"] = matmul(y_hg, dbh, ta=True, out_dtype=GRAD_COMM_DTYPE, name="d_w_branch_hg")
    gb["w_branch_att"] = matmul(y_att, dba, ta=True, out_dtype=GRAD_COMM_DTYPE, name="d_w_branch_att")
    dy_hg = matmul(dbh, big["w_branch_hg"], tb=True, name="d_y_hg")
    dy_att = matmul(dba, big["w_branch_att"], tb=True, name="d_y_att")
    dq, dfp, di, dog, d_lb, gs["hg_out_norm"] = hgrn_bwd(z, lb, small["hg_out_norm"], o_raw, states, dy_hg, "hgrn_bwd")
    do_att, corr = att_combine_bwd(o_att, l_att, dy_att, "att_combine_bwd")
    d_att = [part for g in range(N_GROUPS) for part in att_bwd(z, l_att[g], do_att[g], corr[g], g, f"att_bwd_{g}")]
    dz = jnp.concatenate([dq, dfp, di, dog, *d_att, dgh, dga], axis=1)
    gb["w_in"] = matmul(u, dz, ta=True, out_dtype=GRAD_COMM_DTYPE, name="d_w_in")
    du = matmul(dz, big["w_in"], tb=True, name="d_u")
    dh1, gs["mix_norm"] = rmsnorm_bwd(h1, small["mix_norm"], du, dh2, "mix_norm_bwd")
    dh1 = exchange.gradients("mix", gb, dh1)
    dp0 = d_lb * lb * (1.0 - lb)
    gs["hg_lower_bounds"] = jnp.concatenate([dp0, -dp0], axis=0)
    dx, gs["ffn1_norm"], gb["ffn1_w_gate_up"], gb["ffn1_w_down"] = _swiglu_block_bwd(
        x, small["ffn1_norm"], big["ffn1_w_gate_up"], big["ffn1_w_down"], saved1, dh1, "ffn1")
    dx = exchange.gradients("ffn1", gb, dx)
    return loss[0, 0], dx, gs


SMALL = ("ffn1_norm", "mix_norm", "hg_lower_bounds", "hg_out_norm", "ffn2_norm", "final_norm")
WEIGHTS = ("ffn1_norm", "ffn1_w_gate_up", "ffn1_w_down", "mix_norm", "w_in", "hg_lower_bounds", "hg_out_norm",
           "w_branch_hg", "w_branch_att", "w_out", "ffn2_norm", "ffn2_w_gate_up", "ffn2_w_down", "final_norm")
SMALL_SHAPE = {"ffn1_norm": (1, 1024), "mix_norm": (1, 1024), "hg_lower_bounds": (2, 512), "hg_out_norm": (1, 512),
               "ffn2_norm": (1, 1024), "final_norm": (1024,)}
LOSS_ROW = 6


def _pack_small(vals):
    rows = []
    for n in SMALL:
        r = vals[n].reshape(1, -1).astype(F32)
        rows.append(jnp.pad(r, ((0, 0), (0, D_MODEL - r.shape[1]))))
    rows.append(jnp.zeros((SMALL_ROWS - len(SMALL), D_MODEL), F32))
    return jnp.concatenate(rows, axis=0)


def _unpack_small(packed):
    out = {}
    for i, n in enumerate(SMALL):
        size = int(np.prod(SMALL_SHAPE[n]))
        out[n] = packed[i, :size].reshape(SMALL_SHAPE[n])
    return out


def _behind(a, token):
    return lax.optimization_barrier((a, token))[0]


class WeightExchange:
    ORDER = ("ffn1", "mix", "ffn2")

    def __init__(self, shards, core, chip):
        self.core, self.chip = core, chip
        self.placed = {n: place_own_block(shards[n], chip, r, cc, ax, f"place_{n}") for n, r, cc, ax in BIG}
        self.gathering = None
        self.scattering = None
        self.reduced = {}
        self.token = None

    def _names(self, group):
        return [BIG[i][0] for i in GROUPS[group]]

    def _start_gather(self, group):
        send_sems, recv_sems, bufs, self.token = gather_start([self.placed[n] for n in self._names(group)], group)
        self.gathering = (group, send_sems, recv_sems, bufs)

    def weights(self, group, h):
        if self.gathering is None:
            whole = all_gather_weights([self.placed[n] for n in self._names(group)], group)
        else:
            pending, send_sems, recv_sems, bufs = self.gathering
            assert pending == group
            whole = gather_forward(gather_wait(bufs, send_sems, recv_sems, h, group), group)
            self.gathering = None
        later = self.ORDER.index(group) + 1
        if later < len(self.ORDER):
            self._start_gather(self.ORDER[later])
            h = _behind(h, self.token)
        return h, dict(zip(self._names(group), whole))

    def _finish_scatter(self, after):
        group, send_sems, recv_sems, halves, got = self.scattering
        halves, got = scatter_wait(halves, got, send_sems, recv_sems, after, group)
        ws = [BIG[i] for i in GROUPS[group]]
        mine = [add_pieces(h, g, self.chip, r, cc, ax, f"add_pieces_{n}") for (n, r, cc, ax), h, g in zip(ws, halves, got)]
        theirs = exchange_reduced(mine, group)
        self.reduced.update({n: (a, b) for (n, *_), a, b in zip(ws, mine, theirs)})
        self.scattering = None

    def gradients(self, group, grads, dh):
        if self.scattering is not None:
            self._finish_scatter(dh)
        ws = [BIG[i] for i in GROUPS[group]]
        theirs = exchange_halves([grads[n] for n, *_ in ws], group)
        halves = [add_halves(grads[n], t, self.core, r, cc, ax, f"add_halves_{n}") for (n, r, cc, ax), t in zip(ws, theirs)]
        send_sems, recv_sems, halves, got, self.token = scatter_start(halves, group)
        self.scattering = (group, send_sems, recv_sems, halves, got)
        return _behind(dh, self.token)

    def finish(self, after):
        self._finish_scatter(after)
        return self.reduced


def kernel(x, ffn1_norm, ffn1_w_gate_up, ffn1_w_down, mix_norm, w_in, hg_lower_bounds, hg_out_norm, w_branch_hg, w_branch_att, w_out, ffn2_norm, ffn2_w_gate_up, ffn2_w_down, final_norm, loss_target, m_ffn1_norm, m_ffn1_w_gate_up, m_ffn1_w_down, m_mix_norm, m_w_in, m_hg_lower_bounds, m_hg_out_norm, m_w_branch_hg, m_w_branch_att, m_w_out, m_ffn2_norm, m_ffn2_w_gate_up, m_ffn2_w_down, m_final_norm, v_ffn1_norm, v_ffn1_w_gate_up, v_ffn1_w_down, v_mix_norm, v_w_in, v_hg_lower_bounds, v_hg_out_norm, v_w_branch_hg, v_w_branch_att, v_w_out, v_ffn2_norm, v_ffn2_w_gate_up, v_ffn2_w_down, v_final_norm):
    w = dict(ffn1_norm=ffn1_norm, ffn1_w_gate_up=ffn1_w_gate_up, ffn1_w_down=ffn1_w_down, mix_norm=mix_norm, w_in=w_in,
             hg_lower_bounds=hg_lower_bounds, hg_out_norm=hg_out_norm, w_branch_hg=w_branch_hg, w_branch_att=w_branch_att,
             w_out=w_out, ffn2_norm=ffn2_norm, ffn2_w_gate_up=ffn2_w_gate_up, ffn2_w_down=ffn2_w_down, final_norm=final_norm)
    m = dict(ffn1_norm=m_ffn1_norm, ffn1_w_gate_up=m_ffn1_w_gate_up, ffn1_w_down=m_ffn1_w_down, mix_norm=m_mix_norm,
             w_in=m_w_in, hg_lower_bounds=m_hg_lower_bounds, hg_out_norm=m_hg_out_norm, w_branch_hg=m_w_branch_hg,
             w_branch_att=m_w_branch_att, w_out=m_w_out, ffn2_norm=m_ffn2_norm, ffn2_w_gate_up=m_ffn2_w_gate_up,
             ffn2_w_down=m_ffn2_w_down, final_norm=m_final_norm)
    v = dict(ffn1_norm=v_ffn1_norm, ffn1_w_gate_up=v_ffn1_w_gate_up, ffn1_w_down=v_ffn1_w_down, mix_norm=v_mix_norm,
             w_in=v_w_in, hg_lower_bounds=v_hg_lower_bounds, hg_out_norm=v_hg_out_norm, w_branch_hg=v_w_branch_hg,
             w_branch_att=v_w_branch_att, w_out=v_w_out, ffn2_norm=v_ffn2_norm, ffn2_w_gate_up=v_ffn2_w_gate_up,
             ffn2_w_down=v_ffn2_w_down, final_norm=v_final_norm)

    core = lax.axis_index("c").astype(jnp.int32).reshape(1)
    chip = (2 * lax.axis_index("x") + lax.axis_index("y")).astype(jnp.int32).reshape(1)
    exchange = WeightExchange({n: w[n][0] for n, *_ in BIG}, core, chip)
    small = {n: w[n] for n in SMALL}
    small["final_norm"] = final_norm.reshape(1, D_MODEL)

    loss, dx, gs = local_step(x[0], loss_target[0], small, exchange)

    grads, delta, new_m, new_v = {}, {}, {}, {}

    def update(group, core):
        for i in GROUPS[group]:
            n, r, cc, ax = BIG[i]
            a, b = exchange.reduced[n]
            g, d, nm, nv = adamw_halves(w[n][0], a, b, m[n][0], v[n][0], core, r, cc, ax, f"adamw_{n}")
            grads[n], delta[n], new_m[n], new_v[n] = g[None], d[None], nm[None], nv[None]

    core_behind = _behind(core, exchange.token)
    update("ffn2", core_behind)
    update("mix", core_behind)
    packed = _pack_small(gs)
    packed = _behind(packed.at[LOSS_ROW].set(jnp.full((D_MODEL,), loss, F32)), exchange.token)
    total = all_reduce_small(packed)
    grads.update(_unpack_small(total))
    loss_total = total[LOSS_ROW, 0]
    pd, pm, pv = adamw(_pack_small({n: w[n] for n in SMALL}), total.at[LOSS_ROW].set(0.0),
                       _pack_small({n: m[n] for n in SMALL}), _pack_small({n: v[n] for n in SMALL}), "adamw_small")
    delta.update(_unpack_small(pd))
    new_m.update(_unpack_small(pm))
    new_v.update(_unpack_small(pv))
    exchange.finish(after=pd)
    update("ffn1", core)

    return (loss_total, dx[None], *[grads[n] for n in WEIGHTS], *[delta[n] for n in WEIGHTS],
            *[new_m[n] for n in WEIGHTS], *[new_v[n] for n in WEIGHTS])
```

```python
import numpy as np
import jax
import jax.numpy as jnp
from jax import lax
from jax.experimental import pallas as pl
from jax.experimental.pallas import tpu as pltpu

SEQ = 2048
D_MODEL = 1024
D_FF = 2816
HG_HEADS = 4
HG_DIM = 128
HG_WIDTH = 512
HG_CHUNK = 64
ATT_GROUPS = ((128, 1), (512, 4), (2048, 16))
ATT_HEADS = 8
ATT_WIDTH = 512
ATT_BLOCK = 128
ALIBI_MAX = 8.0
IN_COLS = 8704
EPS = 1e-6
NEG_INF = -1e30
ADAM_LR = 0.001
ADAM_B1 = 0.9
ADAM_B2 = 0.999
ADAM_EPS = 1e-08
ADAM_WD = 0.01
ADAM_STEP = 10

N_CHIPS = 4
MXU_DTYPE = jnp.bfloat16
HG_DOT_DTYPE = jnp.float32
WEIGHT_COMM_DTYPE = jnp.bfloat16
GRAD_COMM_DTYPE = jnp.bfloat16
MESH = pl.DeviceIdType.MESH
F32 = jnp.float32
HIGHEST = lax.Precision.HIGHEST


def _pick(n, cands):
    for c in cands:
        if n % c == 0:
            return c
    return n


def _sigmoid(x):
    return 1.0 / (1.0 + jnp.exp(-x))


def _dot(a, b, ta=False, tb=False):
    dn = (((0 if ta else 1,), (1 if tb else 0,)), ((), ()))
    return lax.dot_general(a.astype(MXU_DTYPE), b.astype(MXU_DTYPE), dn, preferred_element_type=F32)


def _dot_f32(a, b):
    return jnp.dot(a, b, precision=HIGHEST, preferred_element_type=F32)


def _hdot(a, b, ta=False, tb=False):
    if HG_DOT_DTYPE == F32:
        dn = (((0 if ta else 1,), (1 if tb else 0,)), ((), ()))
        return lax.dot_general(a, b, dn, precision=HIGHEST, preferred_element_type=F32)
    return _dot(a, b, ta, tb)


MATMUL_VMEM_BYTES = 48 * 1024 * 1024
MATMUL_TILE_BYTES = 36 * 1024 * 1024
MXU_ALIGN = 128


def _divisors(n, most):
    return [t for t in range(min(n, most), 0, -MXU_ALIGN) if n % t == 0 and t % MXU_ALIGN == 0]


def _matmul_tiles(M, N, K, in_bytes, out_bytes, has_res):
    best = None
    for tk in _divisors(K, K):
        nk = K // tk
        for tm in _divisors(M, 2048):
            for tn in _divisors(N, 512):
                tiles = 2 * in_bytes * (tm * tk + tk * tn) + 2 * out_bytes * tm * tn
                tiles += 4 * tm * tn * ((nk > 1) + 2 * has_res)
                if tiles > MATMUL_TILE_BYTES:
                    continue
                traffic = in_bytes * (M * K * (1 if nk == 1 else N // tn) + K * N * (M // tm))
                key = (traffic, -tm * tn * tk)
                if best is None or key < best[0]:
                    best = (key, (tm, tn, tk))
    return best[1]


def matmul(a, b, *, ta=False, tb=False, out_dtype=F32, res=None, scale=1.0, behind=None, name):
    if ta:
        K, M = a.shape
    else:
        M, K = a.shape
    if tb:
        N, K2 = b.shape
    else:
        K2, N = b.shape
    assert K == K2 and a.dtype == b.dtype
    tm, tn, tk = _matmul_tiles(M, N, K, a.dtype.itemsize, jnp.dtype(out_dtype).itemsize, res is not None)
    nk = K // tk

    def finish(r, r_ref, o_ref):
        if scale != 1.0:
            r = r * scale
        if res is not None:
            r = r_ref[...] + r
        o_ref[...] = r.astype(out_dtype)

    def body(*refs):
        a_ref, b_ref = refs[:2]
        r_ref = refs[2] if res is not None else None
        o_ref = refs[2 + (res is not None) + (behind is not None)]
        if nk == 1:
            finish(_dot(a_ref[...], b_ref[...], ta, tb), r_ref, o_ref)
            return
        acc = refs[-1]
        k = pl.program_id(2)

        @pl.when(k == 0)
        def _():
            acc[...] = jnp.zeros_like(acc)

        acc[...] += _dot(a_ref[...], b_ref[...], ta, tb)

        @pl.when(k == nk - 1)
        def _():
            finish(acc[...], r_ref, o_ref)

    a_spec = pl.BlockSpec((tk, tm), lambda i, j, k: (k, i)) if ta else pl.BlockSpec((tm, tk), lambda i, j, k: (i, k))
    b_spec = pl.BlockSpec((tn, tk), lambda i, j, k: (j, k)) if tb else pl.BlockSpec((tk, tn), lambda i, j, k: (k, j))
    in_specs = [a_spec, b_spec]
    args = [a, b]
    if res is not None:
        in_specs.append(pl.BlockSpec((tm, tn), lambda i, j, k: (i, j)))
        args.append(res)
    if behind is not None:
        in_specs.append(pl.BlockSpec(memory_space=pl.ANY))
        args.append(behind)
    return pl.pallas_call(
        body, name=name, grid=(M // tm, N // tn, nk), in_specs=in_specs,
        out_specs=pl.BlockSpec((tm, tn), lambda i, j, k: (i, j)),
        out_shape=jax.ShapeDtypeStruct((M, N), out_dtype),
        scratch_shapes=[pltpu.VMEM((tm, tn), F32)] if nk > 1 else [],
        compiler_params=pltpu.CompilerParams(dimension_semantics=("parallel", "parallel", "arbitrary"),
                                             vmem_limit_bytes=MATMUL_VMEM_BYTES),
    )(*args)


ROW_TILE = 256


def rmsnorm_fwd(x, g, name, behind=None):
    def body(x_ref, g_ref, *refs):
        n_ref = refs[-1]
        xv = x_ref[...]
        r = lax.rsqrt(jnp.mean(xv * xv, axis=-1, keepdims=True) + EPS)
        n_ref[...] = ((xv * r) * g_ref[...]).astype(n_ref.dtype)

    order = [] if behind is None else [behind]
    return pl.pallas_call(
        body, name=name, grid=(SEQ // ROW_TILE,),
        in_specs=[pl.BlockSpec((ROW_TILE, D_MODEL), lambda i: (i, 0)), pl.BlockSpec((1, D_MODEL), lambda i: (0, 0))]
        + [pl.BlockSpec(memory_space=pl.ANY)] * len(order),
        out_specs=pl.BlockSpec((ROW_TILE, D_MODEL), lambda i: (i, 0)),
        out_shape=jax.ShapeDtypeStruct((SEQ, D_MODEL), MXU_DTYPE),
    )(x, g, *order)


def rmsnorm_bwd(x, g, dn, dres, name):
    def body(x_ref, g_ref, dn_ref, dr_ref, dx_ref, dg_ref):
        xv = x_ref[...]
        r = lax.rsqrt(jnp.mean(xv * xv, axis=-1, keepdims=True) + EPS)
        xh = xv * r
        dnv = dn_ref[...]

        @pl.when(pl.program_id(0) == 0)
        def _():
            dg_ref[...] = jnp.zeros_like(dg_ref)

        dg_ref[...] += jnp.sum(dnv * xh, axis=0, keepdims=True)
        dxh = dnv * g_ref[...]
        dx_ref[...] = dr_ref[...] + r * (dxh - xh * jnp.mean(dxh * xh, axis=-1, keepdims=True))

    row = pl.BlockSpec((ROW_TILE, D_MODEL), lambda i: (i, 0))
    vec = pl.BlockSpec((1, D_MODEL), lambda i: (0, 0))
    return pl.pallas_call(
        body, name=name, grid=(SEQ // ROW_TILE,), in_specs=[row, vec, row, row], out_specs=[row, vec],
        out_shape=[jax.ShapeDtypeStruct((SEQ, D_MODEL), F32), jax.ShapeDtypeStruct((1, D_MODEL), F32)],
        compiler_params=pltpu.CompilerParams(dimension_semantics=("arbitrary",)),
    )(x, g, dn, dres)


def final_norm_loss(h, g, target, name):
    def body(h_ref, g_ref, t_ref, dh_ref, dg_ref, loss_ref):
        xv = h_ref[...]
        r = lax.rsqrt(jnp.mean(xv * xv, axis=-1, keepdims=True) + EPS)
        xh = xv * r
        gv = g_ref[...]
        e = xh * gv - t_ref[...]

        @pl.when(pl.program_id(0) == 0)
        def _():
            dg_ref[...] = jnp.zeros_like(dg_ref)
            loss_ref[...] = jnp.zeros_like(loss_ref)

        part = 0.5 * jnp.sum(jnp.sum(e * e, axis=-1, keepdims=True) * (1.0 / D_MODEL), axis=0, keepdims=True)
        loss_ref[...] += jnp.broadcast_to(part, loss_ref.shape)
        dout = e * (1.0 / D_MODEL)
        dg_ref[...] += jnp.sum(dout * xh, axis=0, keepdims=True)
        dxh = dout * gv
        dh_ref[...] = r * (dxh - xh * jnp.mean(dxh * xh, axis=-1, keepdims=True))

    row = pl.BlockSpec((ROW_TILE, D_MODEL), lambda i: (i, 0))
    vec = pl.BlockSpec((1, D_MODEL), lambda i: (0, 0))
    return pl.pallas_call(
        body, name=name, grid=(SEQ // ROW_TILE,), in_specs=[row, vec, row],
        out_specs=[row, vec, pl.BlockSpec((8, 128), lambda i: (0, 0))],
        out_shape=[jax.ShapeDtypeStruct((SEQ, D_MODEL), F32), jax.ShapeDtypeStruct((1, D_MODEL), F32),
                   jax.ShapeDtypeStruct((8, 128), F32)],
        compiler_params=pltpu.CompilerParams(dimension_semantics=("arbitrary",)),
    )(h, g, target)


FF_TILE = D_FF // 2


def swiglu_fwd(gu, name):
    def body(a_ref, b_ref, s_ref):
        a = a_ref[...]
        s_ref[...] = (a * _sigmoid(a) * b_ref[...]).astype(s_ref.dtype)

    return pl.pallas_call(
        body, name=name, grid=(SEQ // ROW_TILE, 2),
        in_specs=[pl.BlockSpec((ROW_TILE, FF_TILE), lambda i, j: (i, j)),
                  pl.BlockSpec((ROW_TILE, FF_TILE), lambda i, j: (i, j + 2))],
        out_specs=pl.BlockSpec((ROW_TILE, FF_TILE), lambda i, j: (i, j)),
        out_shape=jax.ShapeDtypeStruct((SEQ, D_FF), MXU_DTYPE),
    )(gu, gu)


def swiglu_bwd(gu, ds, name):
    def body(a_ref, b_ref, ds_ref, o_ref):
        a = a_ref[...]
        sg = _sigmoid(a)
        dsv = ds_ref[...]

        @pl.when(pl.program_id(1) < 2)
        def _():
            o_ref[...] = (dsv * b_ref[...] * (sg * (1.0 + a * (1.0 - sg)))).astype(o_ref.dtype)

        @pl.when(pl.program_id(1) >= 2)
        def _():
            o_ref[...] = (dsv * a * sg).astype(o_ref.dtype)

    return pl.pallas_call(
        body, name=name, grid=(SEQ // ROW_TILE, 4),
        in_specs=[pl.BlockSpec((ROW_TILE, FF_TILE), lambda i, j: (i, j % 2)),
                  pl.BlockSpec((ROW_TILE, FF_TILE), lambda i, j: (i, j % 2 + 2)),
                  pl.BlockSpec((ROW_TILE, FF_TILE), lambda i, j: (i, j % 2))],
        out_specs=pl.BlockSpec((ROW_TILE, FF_TILE), lambda i, j: (i, j)),
        out_shape=jax.ShapeDtypeStruct((SEQ, 2 * D_FF), MXU_DTYPE),
    )(gu, gu, ds)


GATE_HG_BLK = 6656 // 512
GATE_ATT_BLK = 7680 // 512


def merge_fwd(z, bh, ba, name):
    def body(gh_ref, ga_ref, bh_ref, ba_ref, o_ref):
        o_ref[...] = (_sigmoid(gh_ref[...]) * bh_ref[...] + _sigmoid(ga_ref[...]) * ba_ref[...]).astype(o_ref.dtype)

    blk = pl.BlockSpec((ROW_TILE, 512), lambda i, j: (i, j))
    return pl.pallas_call(
        body, name=name, grid=(SEQ // ROW_TILE, 2),
        in_specs=[pl.BlockSpec((ROW_TILE, 512), lambda i, j: (i, GATE_HG_BLK + j)),
                  pl.BlockSpec((ROW_TILE, 512), lambda i, j: (i, GATE_ATT_BLK + j)), blk, blk],
        out_specs=blk, out_shape=jax.ShapeDtypeStruct((SEQ, D_MODEL), MXU_DTYPE),
    )(z, z, bh, ba)


def merge_bwd(z, bh, ba, dm, name):
    def body(gh_ref, ga_ref, bh_ref, ba_ref, dm_ref, dbh_ref, dba_ref, dgh_ref, dga_ref):
        dmv = dm_ref[...]
        sh = _sigmoid(gh_ref[...])
        sa = _sigmoid(ga_ref[...])
        dbh_ref[...] = (dmv * sh).astype(dbh_ref.dtype)
        dba_ref[...] = (dmv * sa).astype(dba_ref.dtype)
        dgh_ref[...] = (dmv * bh_ref[...] * (sh * (1.0 - sh))).astype(dgh_ref.dtype)
        dga_ref[...] = (dmv * ba_ref[...] * (sa * (1.0 - sa))).astype(dga_ref.dtype)

    blk = pl.BlockSpec((ROW_TILE, 512), lambda i, j: (i, j))
    out = jax.ShapeDtypeStruct((SEQ, D_MODEL), MXU_DTYPE)
    return pl.pallas_call(
        body, name=name, grid=(SEQ // ROW_TILE, 2),
        in_specs=[pl.BlockSpec((ROW_TILE, 512), lambda i, j: (i, GATE_HG_BLK + j)),
                  pl.BlockSpec((ROW_TILE, 512), lambda i, j: (i, GATE_ATT_BLK + j)), blk, blk, blk],
        out_specs=[blk, blk, blk, blk], out_shape=[out, out, out, out],
    )(z, z, bh, ba, dm)


N_CHUNKS = SEQ // HG_CHUNK


def _hgrn_gates(q, fp, lb):
    C = HG_CHUNK
    sg = _sigmoid(fp)
    f = lb + (1.0 - lb) * sg
    lf = jnp.log(f)
    row = lax.broadcasted_iota(jnp.int32, (C, C), 0)
    col = lax.broadcasted_iota(jnp.int32, (C, C), 1)
    causal = row >= col
    G = _dot_f32(causal.astype(F32), lf)
    eG = jnp.exp(G)
    enG = jnp.exp(-G)
    qg = q * eG
    kg = (1.0 - f) * enG
    A = jnp.where(causal, _hdot(qg, kg, tb=True), 0.0)
    egl = jnp.exp(jnp.sum(lf, axis=0, keepdims=True))
    return sg, f, causal, eG, enG, qg, kg, A, egl


def hgrn_fwd(z, lb, gain, name):
    C, K = HG_CHUNK, HG_DIM

    def body(q_ref, f_ref, v_ref, og_ref, p_ref, g_ref, y_ref, o_ref, st_ref, state):
        @pl.when(pl.program_id(0) == 0)
        def _():
            state[...] = jnp.zeros_like(state)

        for h in range(HG_HEADS):
            hd = pl.ds(h * K, K)
            v = v_ref[:, hd]
            _, _, _, _, _, qg, kg, A, egl = _hgrn_gates(q_ref[:, hd], f_ref[:, hd], p_ref[:, hd])
            st = state[h]
            st_ref[h, 0] = st
            o = _hdot(A, v) + _hdot(qg, st, tb=True)
            state[h] = st * egl + _hdot(v, kg * egl, ta=True)
            o_ref[:, hd] = o
            rs = lax.rsqrt(jnp.mean(o * o, axis=-1, keepdims=True) + EPS)
            og = og_ref[:, hd]
            y_ref[:, hd] = (((o * rs) * g_ref[:, hd]) * (og * _sigmoid(og))).astype(y_ref.dtype)

    def zcol(section):
        return pl.BlockSpec((C, HG_WIDTH), lambda c: (c, section))

    vec = pl.BlockSpec((1, HG_WIDTH), lambda c: (0, 0))
    blk = pl.BlockSpec((C, HG_WIDTH), lambda c: (c, 0))
    return pl.pallas_call(
        body, name=name, grid=(N_CHUNKS,),
        in_specs=[zcol(0), zcol(1), zcol(2), zcol(3), vec, vec],
        out_specs=[blk, blk, pl.BlockSpec((HG_HEADS, 1, K, K), lambda c: (0, c, 0, 0))],
        out_shape=[jax.ShapeDtypeStruct((SEQ, HG_WIDTH), MXU_DTYPE), jax.ShapeDtypeStruct((SEQ, HG_WIDTH), F32),
                   jax.ShapeDtypeStruct((HG_HEADS, N_CHUNKS, K, K), F32)],
        scratch_shapes=[pltpu.VMEM((HG_HEADS, K, K), F32)],
        compiler_params=pltpu.CompilerParams(dimension_semantics=("arbitrary",)),
    )(z, z, z, z, lb, gain)


def hgrn_bwd(z, lb, gain, o_raw, states, dy, name):
    C, K = HG_CHUNK, HG_DIM

    def body(q_ref, f_ref, v_ref, og_ref, p_ref, g_ref, o_ref, st_ref, dy_ref,
             dq_ref, dfp_ref, dv_ref, dog_ref, dlb_ref, dgain_ref, dstate):
        @pl.when(pl.program_id(0) == 0)
        def _():
            dstate[...] = jnp.zeros_like(dstate)
            dlb_ref[...] = jnp.zeros_like(dlb_ref)
            dgain_ref[...] = jnp.zeros_like(dgain_ref)

        last = lax.broadcasted_iota(jnp.int32, (C, K), 0) == C - 1
        row = lax.broadcasted_iota(jnp.int32, (C, C), 0)
        col = lax.broadcasted_iota(jnp.int32, (C, C), 1)
        anti_causal = (col >= row).astype(F32)
        for h in range(HG_HEADS):
            hd = pl.ds(h * K, K)
            v = v_ref[:, hd]
            lb = p_ref[:, hd]
            sg, f, causal, eG, enG, qg, kg, A, egl = _hgrn_gates(q_ref[:, hd], f_ref[:, hd], lb)
            kd = kg * egl
            st = st_ref[h, 0]
            dst = dstate[h]
            o = o_ref[:, hd]
            og = og_ref[:, hd]
            gain_v = g_ref[:, hd]
            dyv = dy_ref[:, hd]
            rs = lax.rsqrt(jnp.mean(o * o, axis=-1, keepdims=True) + EPS)
            on = o * rs
            sgo = _sigmoid(og)
            silu = og * sgo
            dog_ref[:, hd] = (dyv * (on * gain_v) * (sgo * (1.0 + og * (1.0 - sgo)))).astype(dog_ref.dtype)
            dgain_ref[:, hd] += jnp.sum(dyv * silu * on, axis=0, keepdims=True)
            don = dyv * gain_v * silu
            do = rs * (don - on * jnp.mean(don * on, axis=-1, keepdims=True))
            dA = jnp.where(causal, _hdot(do, v, tb=True), 0.0)
            dv_ref[:, hd] = (_hdot(A, do, ta=True) + _hdot(kd, dst, tb=True)).astype(dv_ref.dtype)
            dqg = _hdot(dA, kg) + _hdot(do, st)
            dkg = _hdot(dA, qg, ta=True)
            dkd = _hdot(v, dst)
            dstate[h] = dst * egl + _hdot(do, qg, ta=True)
            dgl = jnp.sum(st * dst, axis=0, keepdims=True) * egl
            dq_ref[:, hd] = (dqg * eG).astype(dq_ref.dtype)
            dk = dkg * enG + dkd * (enG * egl)
            dG = dqg * qg - dkg * kg - dkd * kd
            extra = jnp.sum(dkd * kd, axis=0, keepdims=True) + dgl
            dG = dG + jnp.where(last, extra, 0.0)
            dlf = _dot_f32(anti_causal, dG)
            df = dlf / f - dk
            dfp_ref[:, hd] = (df * (1.0 - lb) * (sg * (1.0 - sg))).astype(dfp_ref.dtype)
            dlb_ref[:, hd] += jnp.sum(df * (1.0 - sg), axis=0, keepdims=True)

    def rc(c):
        return N_CHUNKS - 1 - c

    def zcol(section):
        return pl.BlockSpec((C, HG_WIDTH), lambda c: (rc(c), section))

    vec = pl.BlockSpec((1, HG_WIDTH), lambda c: (0, 0))
    blk = pl.BlockSpec((C, HG_WIDTH), lambda c: (rc(c), 0))
    out = jax.ShapeDtypeStruct((SEQ, HG_WIDTH), MXU_DTYPE)
    small = jax.ShapeDtypeStruct((1, HG_WIDTH), F32)
    return pl.pallas_call(
        body, name=name, grid=(N_CHUNKS,),
        in_specs=[zcol(0), zcol(1), zcol(2), zcol(3), vec, vec, blk,
                  pl.BlockSpec((HG_HEADS, 1, K, K), lambda c: (0, rc(c), 0, 0)), blk],
        out_specs=[blk, blk, blk, blk, vec, vec],
        out_shape=[out, out, out, out, small, small],
        scratch_shapes=[pltpu.VMEM((HG_HEADS, K, K), F32)],
        compiler_params=pltpu.CompilerParams(dimension_semantics=("arbitrary",)),
    )(z, z, z, z, lb, gain, o_raw, states, dy)


N_GROUPS = len(ATT_GROUPS)
HEAD_PAIRS = ATT_WIDTH // 128
ATT_COL0 = 4 * HG_WIDTH
UNROLLED_SUBSEQS = 4


def _alibi_coef():
    n = N_GROUPS * ATT_HEADS
    slopes = np.exp2(-ALIBI_MAX * np.arange(1, n + 1, dtype=np.float32) / n).astype(np.float32)
    dil = np.repeat(np.array([d for _, d in ATT_GROUPS], np.float32), ATT_HEADS)
    return jnp.asarray(slopes * dil, F32)


def _att_masks():
    B = ATT_BLOCK
    qi = lax.broadcasted_iota(jnp.int32, (B, B), 0)
    kj = lax.broadcasted_iota(jnp.int32, (B, B), 1)
    return qi, kj, (qi - kj).astype(F32), (qi + B - kj).astype(F32)


def _subseq_rows(r, d):
    return pl.ds(r, ATT_BLOCK, stride=d) if d > 1 else pl.ds(0, ATT_BLOCK)


def _for_each_subseq(d, fn):
    if d <= UNROLLED_SUBSEQS:
        for r in range(d):
            fn(r)
    else:
        lax.fori_loop(0, d, lambda r, carry: (fn(r), carry)[1], 0)


def _att_specs(g):
    d = ATT_GROUPS[g][1]
    R = ATT_BLOCK * d
    n_slabs = SEQ // R
    col0 = (ATT_COL0 + g * 3 * ATT_WIDTH) // 128

    def cur(col):
        return pl.BlockSpec((R, 128), lambda hp, s: (s, col + hp))

    def prev(col):
        return pl.BlockSpec((R, 128), lambda hp, s: (jnp.maximum(s - 1, 0), col + hp))

    def nxt(col):
        return pl.BlockSpec((R, 128), lambda hp, s: (jnp.minimum(s + 1, n_slabs - 1), col + hp))

    return d, R, n_slabs, col0, cur, prev, nxt


def _head_lanes(j):
    lane = lax.broadcasted_iota(jnp.int32, (ATT_BLOCK, 128), 1)
    return (lane >= 64 * j) & (lane < 64 * (j + 1))


def _lane_value(x, sel):
    return jnp.max(jnp.where(sel, x, -3e38), axis=-1, keepdims=True)


def att_fwd(z, g, name):
    B = ATT_BLOCK
    d, R, n_slabs, col0, cur, prev, _ = _att_specs(g)
    has_prev = n_slabs > 1

    def body(coef_ref, *refs):
        if has_prev:
            q_ref, kc_ref, vc_ref, kp_ref, vp_ref, o_ref, l_ref = refs
        else:
            q_ref, kc_ref, vc_ref, o_ref, l_ref = refs
        hp, s = pl.program_id(0), pl.program_id(1)
        qi, kj, d_cur, d_prev = _att_masks()
        m_cur = kj <= qi
        m_prev = kj >= qi + jnp.where(s == 0, 4 * B, 0)

        def one(r):
            rows = _subseq_rows(r, d)
            q, kc, vc = q_ref[rows, :], kc_ref[rows, :], vc_ref[rows, :]
            if has_prev:
                kpv, vpv = kp_ref[rows, :], vp_ref[rows, :]
            o_acc = jnp.zeros((B, 128), F32)
            l_acc = jnp.zeros((B, 128), F32)
            for j in range(2):
                sel = _head_lanes(j)
                cf = coef_ref[g * ATT_HEADS + hp * 2 + j]
                qh = jnp.where(sel, q, 0.0)
                s_cur = jnp.where(m_cur, _dot(qh, kc, tb=True) * 0.125 - cf * d_cur, NEG_INF)
                mx = jnp.max(s_cur, axis=-1, keepdims=True)
                if has_prev:
                    s_prev = jnp.where(m_prev, _dot(qh, kpv, tb=True) * 0.125 - cf * d_prev, NEG_INF)
                    mx = jnp.maximum(mx, jnp.max(s_prev, axis=-1, keepdims=True))
                e_cur = jnp.exp(s_cur - mx)
                den = jnp.sum(e_cur, axis=-1, keepdims=True)
                if has_prev:
                    e_prev = jnp.exp(s_prev - mx)
                    den = den + jnp.sum(e_prev, axis=-1, keepdims=True)
                inv = 1.0 / den
                oh = _dot(e_cur * inv, vc)
                if has_prev:
                    oh = oh + _dot(e_prev * inv, vpv)
                o_acc = jnp.where(sel, oh, o_acc)
                l_acc = jnp.where(sel, mx + jnp.log(den), l_acc)
            o_ref[rows, :] = o_acc
            l_ref[rows, :] = l_acc

        _for_each_subseq(d, one)

    in_specs = [pl.BlockSpec(memory_space=pltpu.SMEM), cur(col0), cur(col0 + 4), cur(col0 + 8)]
    args = [_alibi_coef(), z, z, z]
    if has_prev:
        in_specs += [prev(col0 + 4), prev(col0 + 8)]
        args += [z, z]
    out = jax.ShapeDtypeStruct((SEQ, ATT_WIDTH), F32)
    return pl.pallas_call(
        body, name=name, grid=(HEAD_PAIRS, n_slabs), in_specs=in_specs,
        out_specs=[cur(0), cur(0)], out_shape=[out, out],
        compiler_params=pltpu.CompilerParams(dimension_semantics=("parallel", "arbitrary")),
    )(*args)


def att_bwd(z, l, do, corr, g, name):
    B = ATT_BLOCK
    d, R, n_slabs, col0, cur, prev, nxt = _att_specs(g)
    neighbours = n_slabs > 1

    def body(coef_ref, *refs):
        if neighbours:
            (q_ref, kc_ref, vc_ref, l_ref, do_ref, cr_ref, kp_ref, vp_ref, qn_ref, ln_ref, don_ref, crn_ref,
             dq_ref, dk_ref, dv_ref, dq_sc, dk_sc, dv_sc) = refs
        else:
            q_ref, kc_ref, vc_ref, l_ref, do_ref, cr_ref, dq_ref, dk_ref, dv_ref, dq_sc, dk_sc, dv_sc = refs
        hp, s = pl.program_id(0), pl.program_id(1)
        qi, kj, d_cur, d_prev = _att_masks()
        m_cc = kj <= qi
        m_cp = kj >= qi + jnp.where(s == 0, 4 * B, 0)
        m_nc = kj >= qi + jnp.where(s == n_slabs - 1, 4 * B, 0)

        def one(r):
            rows = _subseq_rows(r, d)
            q, kc, vc, lv, dov, crv = (ref[rows, :] for ref in (q_ref, kc_ref, vc_ref, l_ref, do_ref, cr_ref))
            if neighbours:
                kpv, vpv, qn, lnv, donv, crnv = (ref[rows, :] for ref in (kp_ref, vp_ref, qn_ref, ln_ref, don_ref, crn_ref))
            dq_acc = jnp.zeros((B, 128), F32)
            dk_acc = jnp.zeros((B, 128), F32)
            dv_acc = jnp.zeros((B, 128), F32)
            for j in range(2):
                sel = _head_lanes(j)
                cf = coef_ref[g * ATT_HEADS + hp * 2 + j]
                qh = jnp.where(sel, q, 0.0)
                doh = jnp.where(sel, dov, 0.0)
                lse, cr = _lane_value(lv, sel), _lane_value(crv, sel)
                p_cc = jnp.exp(jnp.where(m_cc, _dot(qh, kc, tb=True) * 0.125 - cf * d_cur, NEG_INF) - lse)
                ds_cc = p_cc * (_dot(doh, vc, tb=True) + cr)
                dqh = _dot(ds_cc, kc)
                dkh = _dot(ds_cc, qh, ta=True)
                dvh = _dot(p_cc, doh, ta=True)
                if neighbours:
                    qnh = jnp.where(sel, qn, 0.0)
                    donh = jnp.where(sel, donv, 0.0)
                    lse_n, cr_n = _lane_value(lnv, sel), _lane_value(crnv, sel)
                    p_cp = jnp.exp(jnp.where(m_cp, _dot(qh, kpv, tb=True) * 0.125 - cf * d_prev, NEG_INF) - lse)
                    p_nc = jnp.exp(jnp.where(m_nc, _dot(qnh, kc, tb=True) * 0.125 - cf * d_prev, NEG_INF) - lse_n)
                    ds_cp = p_cp * (_dot(doh, vpv, tb=True) + cr)
                    ds_nc = p_nc * (_dot(donh, vc, tb=True) + cr_n)
                    dqh = dqh + _dot(ds_cp, kpv)
                    dkh = dkh + _dot(ds_nc, qnh, ta=True)
                    dvh = dvh + _dot(p_nc, donh, ta=True)
                dq_acc = jnp.where(sel, dqh * 0.125, dq_acc)
                dk_acc = jnp.where(sel, dkh * 0.125, dk_acc)
                dv_acc = jnp.where(sel, dvh, dv_acc)
            dq_sc[rows, :] = dq_acc
            dk_sc[rows, :] = dk_acc
            dv_sc[rows, :] = dv_acc

        _for_each_subseq(d, one)
        dq_ref[...] = dq_sc[...].astype(dq_ref.dtype)
        dk_ref[...] = dk_sc[...].astype(dk_ref.dtype)
        dv_ref[...] = dv_sc[...].astype(dv_ref.dtype)

    in_specs = [pl.BlockSpec(memory_space=pltpu.SMEM), cur(col0), cur(col0 + 4), cur(col0 + 8), cur(0), cur(0), cur(0)]
    args = [_alibi_coef(), z, z, z, l, do, corr]
    if neighbours:
        in_specs += [prev(col0 + 4), prev(col0 + 8), nxt(col0), nxt(0), nxt(0), nxt(0)]
        args += [z, z, z, l, do, corr]
    out = jax.ShapeDtypeStruct((SEQ, ATT_WIDTH), MXU_DTYPE)
    return pl.pallas_call(
        body, name=name, grid=(HEAD_PAIRS, n_slabs), in_specs=in_specs,
        out_specs=[cur(0)] * 3, out_shape=[out] * 3,
        scratch_shapes=[pltpu.VMEM((R, 128), F32)] * 3,
        compiler_params=pltpu.CompilerParams(dimension_semantics=("parallel", "arbitrary"),
                                             vmem_limit_bytes=MATMUL_VMEM_BYTES),
    )(*args)


def _head_sum(x):
    i = lax.broadcasted_iota(jnp.int32, (128, 128), 0) // 64
    j = lax.broadcasted_iota(jnp.int32, (128, 128), 1) // 64
    return _dot_f32(x, (i == j).astype(F32))


def _group_weights(l0, l1, l2):
    mx = jnp.maximum(jnp.maximum(l0, l1), l2)
    e0, e1, e2 = jnp.exp(l0 - mx), jnp.exp(l1 - mx), jnp.exp(l2 - mx)
    inv = 1.0 / (e0 + e1 + e2)
    return e0 * inv, e1 * inv, e2 * inv


def att_combine_fwd(o, l, name):
    def body(o0, o1, o2, l0, l1, l2, y_ref):
        w0, w1, w2 = _group_weights(l0[...], l1[...], l2[...])
        y_ref[...] = (o0[...] * w0 + o1[...] * w1 + o2[...] * w2).astype(y_ref.dtype)

    blk = pl.BlockSpec((ROW_TILE, ATT_WIDTH), lambda i: (i, 0))
    return pl.pallas_call(
        body, name=name, grid=(SEQ // ROW_TILE,), in_specs=[blk] * 6, out_specs=blk,
        out_shape=jax.ShapeDtypeStruct((SEQ, ATT_WIDTH), MXU_DTYPE),
    )(*o, *l)


def att_combine_bwd(o, l, dy, name):
    def body(o0, o1, o2, l0, l1, l2, dy_ref, do0, do1, do2, cr0, cr1, cr2):
        w = _group_weights(l0[...], l1[...], l2[...])
        dyv = dy_ref[...]
        dw = [_head_sum(dyv * o_ref[...]) for o_ref in (o0, o1, o2)]
        tot = w[0] * dw[0] + w[1] * dw[1] + w[2] * dw[2]
        for g, (do_ref, cr_ref) in enumerate(((do0, cr0), (do1, cr1), (do2, cr2))):
            do_ref[...] = dyv * w[g]
            cr_ref[...] = -w[g] * tot

    blk = pl.BlockSpec((ROW_TILE, 128), lambda i, j: (i, j))
    out = jax.ShapeDtypeStruct((SEQ, ATT_WIDTH), F32)
    res = pl.pallas_call(
        body, name=name, grid=(SEQ // ROW_TILE, HEAD_PAIRS), in_specs=[blk] * 7, out_specs=[blk] * 6, out_shape=[out] * 6,
    )(*o, *l, dy)
    return res[:N_GROUPS], res[N_GROUPS:]


SUM_ROW_TILES = (256, 128, 64, 32, 16)
SUM_TILE_ELEMS = 128 * 1024


def _row_tile(rows, cols):
    fit = [t for t in SUM_ROW_TILES if rows % t == 0]
    return next((t for t in fit if t * cols <= SUM_TILE_ELEMS), fit[-1])


def _shard_shape(rows, cols, axis):
    return (rows // N_CHIPS, cols) if axis == 0 else (rows, cols // N_CHIPS)


def _half_shape(rows, cols, axis):
    return (rows, cols // 2) if axis == 0 else (rows // 2, cols)


def _piece_shape(rows, cols, axis):
    return (rows // N_CHIPS, cols // 2) if axis == 0 else (rows // 2, cols // N_CHIPS)


def place_own_block(shard, chip, rows, cols, axis, name):
    sr, sc = _shard_shape(rows, cols, axis)
    tr = _row_tile(sr, sc)

    def body(chip_ref, s_ref, o_ref):
        o_ref[...] = s_ref[...].astype(o_ref.dtype)

    if axis == 0:
        out_map = lambda i, chip_ref: (chip_ref[0] * (sr // tr) + i, 0)
    else:
        out_map = lambda i, chip_ref: (i, chip_ref[0])
    return pl.pallas_call(
        body, name=name, out_shape=jax.ShapeDtypeStruct((rows, cols), WEIGHT_COMM_DTYPE),
        grid_spec=pltpu.PrefetchScalarGridSpec(
            num_scalar_prefetch=1, grid=(sr // tr,), in_specs=[pl.BlockSpec((tr, sc), lambda i, chip_ref: (i, 0))],
            out_specs=pl.BlockSpec((tr, sc), out_map)),
    )(chip, shard)


def add_halves(g, theirs, core, rows, cols, axis, name):
    hr, hc = _half_shape(rows, cols, axis)
    tr = _row_tile(hr, hc)

    def body(core_ref, g_ref, t_ref, o_ref):
        o_ref[...] = (g_ref[...].astype(F32) + t_ref[...].astype(F32)).astype(o_ref.dtype)

    if axis == 0:
        g_map = lambda i, core_ref: (i, core_ref[0])
    else:
        g_map = lambda i, core_ref: (core_ref[0] * (hr // tr) + i, 0)
    blk = pl.BlockSpec((tr, hc), lambda i, core_ref: (i, 0))
    return pl.pallas_call(
        body, name=name, out_shape=jax.ShapeDtypeStruct((hr, hc), GRAD_COMM_DTYPE),
        grid_spec=pltpu.PrefetchScalarGridSpec(
            num_scalar_prefetch=1, grid=(hr // tr,), in_specs=[pl.BlockSpec((tr, hc), g_map), blk], out_specs=blk),
    )(core, g, theirs)


def add_pieces(half, got, chip, rows, cols, axis, name):
    hr, _ = _half_shape(rows, cols, axis)
    pr, pc = _piece_shape(rows, cols, axis)
    tr = _row_tile(pr, pc)

    def body(chip_ref, h_ref, got_ref, o_ref):
        o_ref[...] = (h_ref[...].astype(F32) + got_ref[0].astype(F32) + got_ref[1].astype(F32) + got_ref[2].astype(F32))

    if axis == 0:
        h_map = lambda i, chip_ref: (chip_ref[0] * (pr // tr) + i, 0)
    else:
        h_map = lambda i, chip_ref: (i, chip_ref[0])
    return pl.pallas_call(
        body, name=name, out_shape=jax.ShapeDtypeStruct((pr, pc), F32),
        grid_spec=pltpu.PrefetchScalarGridSpec(
            num_scalar_prefetch=1, grid=(pr // tr,),
            in_specs=[pl.BlockSpec((tr, pc), h_map), pl.BlockSpec((3, tr, pc), lambda i, chip_ref: (0, i, 0))],
            out_specs=pl.BlockSpec((tr, pc), lambda i, chip_ref: (i, 0))),
    )(chip, half, got)


def _adamw_math(w, g, m, v):
    nm = ADAM_B1 * m + (1.0 - ADAM_B1) * g
    nv = ADAM_B2 * v + (1.0 - ADAM_B2) * (g * g)
    m_hat = nm / (1.0 - ADAM_B1 ** ADAM_STEP)
    v_hat = nv / (1.0 - ADAM_B2 ** ADAM_STEP)
    return -ADAM_LR * (m_hat / (jnp.sqrt(v_hat) + ADAM_EPS) + ADAM_WD * w), nm, nv


def adamw(w, g, m, v, name):
    R, Cc = w.shape
    tr = _pick(R, (256, 128, 64, 8))

    def body(w_ref, g_ref, m_ref, v_ref, d_ref, nm_ref, nv_ref):
        d_ref[...], nm_ref[...], nv_ref[...] = _adamw_math(w_ref[...], g_ref[...], m_ref[...], v_ref[...])

    blk = pl.BlockSpec((tr, Cc), lambda i: (i, 0))
    out = jax.ShapeDtypeStruct((R, Cc), F32)
    return pl.pallas_call(
        body, name=name, grid=(R // tr,), in_specs=[blk] * 4, out_specs=[blk] * 3, out_shape=[out, out, out],
    )(w, g, m, v)


def adamw_halves(w, mine, theirs, m, v, core, rows, cols, axis, name):
    sr, sc = _shard_shape(rows, cols, axis)
    pr, pc = _piece_shape(rows, cols, axis)
    tr = _row_tile(pr, pc)
    nt = pr // tr

    def body(core_ref, w_ref, a_ref, b_ref, m_ref, v_ref, g_ref, d_ref, nm_ref, nv_ref):
        g = jnp.where(pl.program_id(0) == core_ref[0], a_ref[...], b_ref[...])
        g_ref[...] = g
        d_ref[...], nm_ref[...], nv_ref[...] = _adamw_math(w_ref[...], g, m_ref[...], v_ref[...])

    if axis == 0:
        full = pl.BlockSpec((tr, pc), lambda h, i, core_ref: (i, h))
    else:
        full = pl.BlockSpec((tr, pc), lambda h, i, core_ref: (h * nt + i, 0))
    part = pl.BlockSpec((tr, pc), lambda h, i, core_ref: (i, 0))
    out = jax.ShapeDtypeStruct((sr, sc), F32)
    return pl.pallas_call(
        body, name=name, out_shape=[out, out, out, out],
        grid_spec=pltpu.PrefetchScalarGridSpec(
            num_scalar_prefetch=1, grid=(2, nt), in_specs=[full, part, part, full, full], out_specs=[full] * 4),
    )(core, w, mine, theirs, m, v)


BIG = (
    ("ffn1_w_gate_up", D_MODEL, 2 * D_FF, 1),
    ("ffn1_w_down", D_FF, D_MODEL, 0),
    ("w_in", D_MODEL, IN_COLS, 1),
    ("w_branch_hg", HG_WIDTH, D_MODEL, 1),
    ("w_branch_att", ATT_WIDTH, D_MODEL, 1),
    ("w_out", D_MODEL, D_MODEL, 0),
    ("ffn2_w_gate_up", D_MODEL, 2 * D_FF, 1),
    ("ffn2_w_down", D_FF, D_MODEL, 0),
)
N_BIG = len(BIG)
ANY = pl.BlockSpec(memory_space=pl.ANY)


def _place():
    return lax.axis_index("x"), lax.axis_index("y"), lax.axis_index("c")


def _other_chips(x, y):
    return ((1 - x, y), (x, 1 - y), (1 - x, 1 - y))


MAX_COPY_CHUNKS = 16
CHUNK_ROW_ALIGN = 16


def _row_chunks(view):
    rows = view.shape[0]
    n = next(n for n in range(MAX_COPY_CHUNKS, 0, -1) if rows % (CHUNK_ROW_ALIGN * n) == 0 or n == 1)
    step = rows // n
    return [pl.ds(i * step, step) for i in range(n)]


def _remote(src, dst, send_sem, recv_sem, device):
    return pltpu.make_async_remote_copy(src_ref=src, dst_ref=dst, send_sem=send_sem, recv_sem=recv_sem,
                                        device_id=device, device_id_type=MESH)


def _start_remote(src, dst, send_sem, recv_sem, device):
    for rows in _row_chunks(src):
        _remote(src.at[rows, :], dst.at[rows, :], send_sem, recv_sem, device).start()
    return _remote(src, dst, send_sem, recv_sem, device)


HBM = pl.BlockSpec(memory_space=pltpu.HBM)
SEM = pl.BlockSpec(memory_space=pltpu.SEMAPHORE)
SPLIT_COPY_EFFECT = pltpu.SideEffectType.DATAFLOW_SIDE_EFFECTING
GROUPS = {"ffn1": (0, 1), "mix": (2, 3, 4, 5), "ffn2": (6, 7)}


def _in_hbm(a):
    return pltpu.with_memory_space_constraint(a, pltpu.HBM)


class _SemList:
    def __init__(self, refs):
        self.refs = refs
        self.at = self

    def __getitem__(self, index):
        w, k = index
        return self.refs[3 * w + k]


def _gather_piece(ref, rows, cols, axis, chip, c):
    sr, sc = _shard_shape(rows, cols, axis)
    j = 2 * chip[0] + chip[1]
    if axis == 0:
        return ref.at[pl.ds(j * sr + c * (sr // 2), sr // 2), :]
    return ref.at[pl.ds(c * (sr // 2), sr // 2), pl.ds(pl.multiple_of(j * sc, 128), sc)]


def _start_gather_sends(bufs, ws, send_sems, recv_sems):
    x, y, c = _place()
    for w, (_, r, cc, ax) in enumerate(ws):
        mine = _gather_piece(bufs[w], r, cc, ax, (x, y), c)
        for k, chip in enumerate(_other_chips(x, y)):
            _start_remote(mine, mine, send_sems.at[w, k], recv_sems.at[w, k], (*chip, c))


def _wait_gather_sends(bufs, ws, send_sems, recv_sems, forward=None):
    x, y, c = _place()
    for w, (_, r, cc, ax) in enumerate(ws):
        for k, chip in enumerate(_other_chips(x, y)):
            got = _gather_piece(bufs[w], r, cc, ax, chip, c)
            _remote(got, got, send_sems.at[w, k], recv_sems.at[w, k], (x, y, c)).wait_recv()
            if forward is not None:
                forward(w, k, got)
    for w, (_, r, cc, ax) in enumerate(ws):
        mine = _gather_piece(bufs[w], r, cc, ax, (x, y), c)
        for k in range(3):
            _remote(mine, mine, send_sems.at[w, k], recv_sems.at[w, k], (x, y, c)).wait_send()


def _forward_halves(bufs, ws, send_sems, recv_sems, first=None):
    x, y, c = _place()
    passed = []

    def forward(w, k, got):
        passed.append(_start_remote(got, got, send_sems.at[w, k], recv_sems.at[w, k], (x, y, 1 - c)))

    if first is not None:
        first(forward)
    else:
        for w, (_, r, cc, ax) in enumerate(ws):
            for k, chip in enumerate(_other_chips(x, y)):
                forward(w, k, _gather_piece(bufs[w], r, cc, ax, chip, c))
    for w, (_, r, cc, ax) in enumerate(ws):
        for k, chip in enumerate(_other_chips(x, y)):
            got = _gather_piece(bufs[w], r, cc, ax, chip, 1 - c)
            _remote(got, got, send_sems.at[w, k], recv_sems.at[w, k], (x, y, c)).wait_recv()
    for cp in passed:
        cp.wait_send()


def all_gather_weights(placed, group):
    ws = [BIG[i] for i in GROUPS[group]]
    n = len(ws)

    def body(*refs):
        bufs = refs[n:2 * n]
        ici_send, ici_recv, d2d_send, d2d_recv = refs[2 * n:]
        _start_gather_sends(bufs, ws, ici_send, ici_recv)
        _forward_halves(bufs, ws, d2d_send, d2d_recv,
                        first=lambda forward: _wait_gather_sends(bufs, ws, ici_send, ici_recv, forward))

    return pl.pallas_call(
        body, name=f"all_gather_{group}", in_specs=[ANY] * n, out_specs=[ANY] * n,
        out_shape=[jax.ShapeDtypeStruct((r, cc), WEIGHT_COMM_DTYPE) for _, r, cc, _ in ws],
        input_output_aliases={w: w for w in range(n)},
        scratch_shapes=[pltpu.SemaphoreType.DMA((n, 3))] * 4,
    )(*placed)


def gather_start(placed, after, group):
    ws = [BIG[i] for i in GROUPS[group]]
    n = len(ws)

    def body(*refs):
        bufs = refs[:n]
        send_sems, recv_sems = _SemList(refs[n + 1:4 * n + 1]), _SemList(refs[4 * n + 1:7 * n + 1])
        token = refs[-1]
        _start_gather_sends(bufs, ws, send_sems, recv_sems)
        token[...] = jnp.zeros_like(token)

    out = pl.pallas_call(
        body, name=f"gather_start_{group}", in_specs=[HBM] * n + [ANY],
        out_specs=[SEM] * (6 * n) + [HBM] * n + [pl.BlockSpec(memory_space=pltpu.VMEM)],
        out_shape=[pltpu.SemaphoreType.DMA(())] * (6 * n)
        + [pltpu.HBM((r, cc), WEIGHT_COMM_DTYPE) for _, r, cc, _ in ws] + [jax.ShapeDtypeStruct((8, 128), F32)],
        input_output_aliases={w: 6 * n + w for w in range(n)},
        compiler_params=pltpu.CompilerParams(has_side_effects=SPLIT_COPY_EFFECT),
    )(*[_in_hbm(p) for p in placed], after)
    return out[:3 * n], out[3 * n:6 * n], out[6 * n:7 * n], out[-1]


def gather_wait(bufs, send_sems, recv_sems, after, group):
    ws = [BIG[i] for i in GROUPS[group]]
    n = len(ws)

    def body(*refs):
        _wait_gather_sends(refs[:n], ws, _SemList(refs[n:n + 3 * n]), _SemList(refs[n + 3 * n:n + 6 * n]))

    return pl.pallas_call(
        body, name=f"gather_wait_{group}", in_specs=[HBM] * n + [SEM] * (6 * n) + [ANY], out_specs=[HBM] * n,
        out_shape=[pltpu.HBM((r, cc), WEIGHT_COMM_DTYPE) for _, r, cc, _ in ws],
        input_output_aliases={w: w for w in range(n)},
        compiler_params=pltpu.CompilerParams(has_side_effects=SPLIT_COPY_EFFECT),
    )(*bufs, *send_sems, *recv_sems, after)


def gather_forward(bufs, group):
    ws = [BIG[i] for i in GROUPS[group]]
    n = len(ws)

    def body(*refs):
        _forward_halves(refs[n:2 * n], ws, refs[2 * n], refs[2 * n + 1])

    return pl.pallas_call(
        body, name=f"gather_forward_{group}", in_specs=[ANY] * n, out_specs=[ANY] * n,
        out_shape=[jax.ShapeDtypeStruct((r, cc), WEIGHT_COMM_DTYPE) for _, r, cc, _ in ws],
        input_output_aliases={w: w for w in range(n)},
        scratch_shapes=[pltpu.SemaphoreType.DMA((n, 3))] * 2,
    )(*bufs)


def _half(ref, rows, cols, axis, c):
    if axis == 0:
        return ref.at[:, pl.ds(pl.multiple_of(c * (cols // 2), 128), cols // 2)]
    return ref.at[pl.ds(c * (rows // 2), rows // 2), :]


def _piece_of_half(ref, rows, cols, axis, chip):
    j = 2 * chip[0] + chip[1]
    pr, pc = _piece_shape(rows, cols, axis)
    if axis == 0:
        return ref.at[pl.ds(j * pr, pr), :]
    return ref.at[:, pl.ds(pl.multiple_of(j * pc, 128), pc)]


def exchange_halves(grads, group):
    ws = [BIG[i] for i in GROUPS[group]]
    n = len(ws)

    def body(*refs):
        ins, theirs = refs[:n], refs[n:2 * n]
        send_sems, recv_sems = refs[2 * n:]
        x, y, c = _place()
        copies = [_start_remote(_half(ins[w], r, cc, ax, 1 - c), theirs[w], send_sems.at[w], recv_sems.at[w], (x, y, 1 - c))
                  for w, (_, r, cc, ax) in enumerate(ws)]
        for cp in copies:
            cp.wait()

    return pl.pallas_call(
        body, name=f"exchange_halves_{group}", in_specs=[ANY] * n, out_specs=[ANY] * n,
        out_shape=[jax.ShapeDtypeStruct(_half_shape(r, cc, ax), GRAD_COMM_DTYPE) for _, r, cc, ax in ws],
        scratch_shapes=[pltpu.SemaphoreType.DMA((n,)), pltpu.SemaphoreType.DMA((n,))],
    )(*grads)


def _scatter_copies(halves, got, ws, send_sems, recv_sems, start):
    x, y, c = _place()
    copies = []
    for w, (_, r, cc, ax) in enumerate(ws):
        for k, chip in enumerate(_other_chips(x, y)):
            args = (_piece_of_half(halves[w], r, cc, ax, chip), got[w].at[k], send_sems.at[w, k], recv_sems.at[w, k], (*chip, c))
            copies.append(_start_remote(*args) if start else _remote(*args))
    return copies


def scatter_start(halves, group):
    ws = [BIG[i] for i in GROUPS[group]]
    n = len(ws)

    def body(*refs):
        sems = refs[2 * n:8 * n]
        _scatter_copies(refs[:n], refs[n:2 * n], ws, _SemList(sems[:3 * n]), _SemList(sems[3 * n:]), start=True)
        refs[-1][...] = jnp.zeros_like(refs[-1])

    landing = [lax.empty((3,) + _piece_shape(r, cc, ax), GRAD_COMM_DTYPE) for _, r, cc, ax in ws]
    out = pl.pallas_call(
        body, name=f"scatter_start_{group}", in_specs=[HBM] * (2 * n),
        out_specs=[SEM] * (6 * n) + [HBM] * (2 * n) + [pl.BlockSpec(memory_space=pltpu.VMEM)],
        out_shape=[pltpu.SemaphoreType.DMA(())] * (6 * n)
        + [pltpu.HBM(_half_shape(r, cc, ax), GRAD_COMM_DTYPE) for _, r, cc, ax in ws]
        + [pltpu.HBM((3,) + _piece_shape(r, cc, ax), GRAD_COMM_DTYPE) for _, r, cc, ax in ws]
        + [jax.ShapeDtypeStruct((8, 128), F32)],
        input_output_aliases={i: 6 * n + i for i in range(2 * n)},
        compiler_params=pltpu.CompilerParams(has_side_effects=SPLIT_COPY_EFFECT),
    )(*[_in_hbm(h) for h in halves], *[_in_hbm(b) for b in landing])
    return out[:3 * n], out[3 * n:6 * n], out[6 * n:7 * n], out[7 * n:8 * n], out[-1]


def scatter_wait(halves, got, send_sems, recv_sems, after, group):
    ws = [BIG[i] for i in GROUPS[group]]
    n = len(ws)

    def body(*refs):
        sems = refs[2 * n:8 * n]
        for cp in _scatter_copies(refs[:n], refs[n:2 * n], ws, _SemList(sems[:3 * n]), _SemList(sems[3 * n:]), start=False):
            cp.wait_send()
            cp.wait_recv()

    out = pl.pallas_call(
        body, name=f"scatter_wait_{group}", in_specs=[HBM] * (2 * n) + [SEM] * (6 * n) + [ANY], out_specs=[HBM] * (2 * n),
        out_shape=[pltpu.HBM(_half_shape(r, cc, ax), GRAD_COMM_DTYPE) for _, r, cc, ax in ws]
        + [pltpu.HBM((3,) + _piece_shape(r, cc, ax), GRAD_COMM_DTYPE) for _, r, cc, ax in ws],
        input_output_aliases={i: i for i in range(2 * n)},
        compiler_params=pltpu.CompilerParams(has_side_effects=SPLIT_COPY_EFFECT),
    )(*halves, *got, *send_sems, *recv_sems, after)
    return out[:n], out[n:]


def exchange_reduced(pieces, group):
    ws = [BIG[i] for i in GROUPS[group]]
    n = len(ws)

    def body(*refs):
        ins, theirs = refs[:n], refs[n:2 * n]
        send_sems, recv_sems = refs[2 * n:]
        x, y, c = _place()
        copies = [_start_remote(ins[w], theirs[w], send_sems.at[w], recv_sems.at[w], (x, y, 1 - c)) for w in range(n)]
        for cp in copies:
            cp.wait()

    return pl.pallas_call(
        body, name=f"exchange_reduced_{group}", in_specs=[ANY] * n, out_specs=[ANY] * n,
        out_shape=[jax.ShapeDtypeStruct(_piece_shape(r, cc, ax), F32) for _, r, cc, ax in ws],
        scratch_shapes=[pltpu.SemaphoreType.DMA((n,)), pltpu.SemaphoreType.DMA((n,))],
    )(*pieces)


N_DEV = 8
SMALL_ROWS = 8


def all_reduce_small(packed, behind):
    def body(x_ref, behind_ref, o_ref, gathered, send_sems, recv_sems):
        x, y, c = _place()
        me = 4 * x + 2 * y + c
        gathered[me] = x_ref[...]
        copies = []
        for k in range(1, N_DEV):
            peer = (x ^ (k >> 2), y ^ ((k >> 1) & 1), c ^ (k & 1))
            cp = pltpu.make_async_remote_copy(
                src_ref=x_ref, dst_ref=gathered.at[me], send_sem=send_sems.at[k - 1], recv_sem=recv_sems.at[k - 1],
                device_id=peer, device_id_type=MESH)
            cp.start()
            copies.append(cp)
        for cp in copies:
            cp.wait()
        acc = gathered[0]
        for k in range(1, N_DEV):
            acc = acc + gathered[k]
        o_ref[...] = acc

    vm = pl.BlockSpec(memory_space=pltpu.VMEM)
    return pl.pallas_call(
        body, name="all_reduce_small", in_specs=[vm, ANY], out_specs=vm,
        out_shape=jax.ShapeDtypeStruct((SMALL_ROWS, D_MODEL), F32),
        scratch_shapes=[pltpu.VMEM((N_DEV, SMALL_ROWS, D_MODEL), F32), pltpu.SemaphoreType.DMA((N_DEV - 1,)),
                        pltpu.SemaphoreType.DMA((N_DEV - 1,))],
    )(packed, behind)


def _swiglu_block_fwd(h, norm_g, w_gu, w_down, tag, behind=None):
    n = rmsnorm_fwd(h, norm_g, f"{tag}_norm", behind=behind)
    gu = matmul(n, w_gu, name=f"{tag}_gate_up")
    s = swiglu_fwd(gu, f"{tag}_swiglu")
    h_out = matmul(s, w_down, res=h, scale=0.5, name=f"{tag}_down")
    return h_out, (n, gu, s)


def _swiglu_block_bwd(h, norm_g, w_gu, w_down, saved, dh_out, tag, behind=None):
    n, gu, s = saved
    df = dh_out.astype(MXU_DTYPE)
    d_down = matmul(s, df, ta=True, scale=0.5, out_dtype=GRAD_COMM_DTYPE, name=f"{tag}_d_w_down")
    ds = matmul(df, w_down, tb=True, scale=0.5, behind=behind, name=f"{tag}_d_s")
    dgu = swiglu_bwd(gu, ds, f"{tag}_swiglu_bwd")
    d_gu = matmul(n, dgu, ta=True, out_dtype=GRAD_COMM_DTYPE, name=f"{tag}_d_w_gate_up")
    dn = matmul(dgu, w_gu, tb=True, name=f"{tag}_d_n")
    dh, dg = rmsnorm_bwd(h, norm_g, dn, dh_out, f"{tag}_norm_bwd")
    return dh, dg, d_gu, d_down


def local_step(x, target, small, exchange):
    big = {}
    token, big_ffn1 = exchange.weights("ffn1", x)
    big.update(big_ffn1)
    h1, saved1 = _swiglu_block_fwd(x, small["ffn1_norm"], big["ffn1_w_gate_up"], big["ffn1_w_down"], "ffn1", token)
    token, big_mix = exchange.weights("mix", h1)
    big.update(big_mix)
    u = rmsnorm_fwd(h1, small["mix_norm"], "mix_norm", behind=token)
    z = matmul(u, big["w_in"], name="w_in")
    p = small["hg_lower_bounds"]
    lb = 1.0 / (1.0 + jnp.exp(p[1:2] - p[0:1]))
    y_hg, o_raw, states = hgrn_fwd(z, lb, small["hg_out_norm"], "hgrn_fwd")
    o_att, l_att = zip(*[att_fwd(z, g, f"att_fwd_{g}") for g in range(N_GROUPS)])
    y_att = att_combine_fwd(o_att, l_att, "att_combine")
    bh = matmul(y_hg, big["w_branch_hg"], name="branch_hg")
    ba = matmul(y_att, big["w_branch_att"], name="branch_att")
    merged = merge_fwd(z, bh, ba, "merge")
    h2 = matmul(merged, big["w_out"], res=h1, name="w_out")
    token, big_ffn2 = exchange.weights("ffn2", h2)
    big.update(big_ffn2)
    h3, saved2 = _swiglu_block_fwd(h2, small["ffn2_norm"], big["ffn2_w_gate_up"], big["ffn2_w_down"], "ffn2", token)
    dh3, d_final, loss = final_norm_loss(h3, small["final_norm"], target, "final_norm_loss")

    gs, gb = {"final_norm": d_final}, {}
    dh2, gs["ffn2_norm"], gb["ffn2_w_gate_up"], gb["ffn2_w_down"] = _swiglu_block_bwd(
        h2, small["ffn2_norm"], big["ffn2_w_gate_up"], big["ffn2_w_down"], saved2, dh3, "ffn2")
    token = exchange.gradients("ffn2", gb, dh2)
    dh2_m = dh2.astype(MXU_DTYPE)
    gb["w_out"] = matmul(merged, dh2_m, ta=True, out_dtype=GRAD_COMM_DTYPE, name="d_w_out")
    dmerged = matmul(dh2_m, big["w_out"], tb=True, behind=token, name="d_merged")
    dbh, dba, dgh, dga = merge_bwd(z, bh, ba, dmerged, "merge_bwd")
    gb["w_branch_hg"] = matmul(y_hg, dbh, ta=True, out_dtype=GRAD_COMM_DTYPE, name="d_w_branch_hg")
    gb["w_branch_att"] = matmul(y_att, dba, ta=True, out_dtype=GRAD_COMM_DTYPE, name="d_w_branch_att")
    dy_hg = matmul(dbh, big["w_branch_hg"], tb=True, name="d_y_hg")
    dy_att = matmul(dba, big["w_branch_att"], tb=True, name="d_y_att")
    dq, dfp, di, dog, d_lb, gs["hg_out_norm"] = hgrn_bwd(z, lb, small["hg_out_norm"], o_raw, states, dy_hg, "hgrn_bwd")
    do_att, corr = att_combine_bwd(o_att, l_att, dy_att, "att_combine_bwd")
    d_att = [part for g in range(N_GROUPS) for part in att_bwd(z, l_att[g], do_att[g], corr[g], g, f"att_bwd_{g}")]
    dz = jnp.concatenate([dq, dfp, di, dog, *d_att, dgh, dga], axis=1)
    gb["w_in"] = matmul(u, dz, ta=True, out_dtype=GRAD_COMM_DTYPE, name="d_w_in")
    du = matmul(dz, big["w_in"], tb=True, name="d_u")
    dh1, gs["mix_norm"] = rmsnorm_bwd(h1, small["mix_norm"], du, dh2, "mix_norm_bwd")
    token = exchange.gradients("mix", gb, dh1)
    dp0 = d_lb * lb * (1.0 - lb)
    gs["hg_lower_bounds"] = jnp.concatenate([dp0, -dp0], axis=0)
    dx, gs["ffn1_norm"], gb["ffn1_w_gate_up"], gb["ffn1_w_down"] = _swiglu_block_bwd(
        x, small["ffn1_norm"], big["ffn1_w_gate_up"], big["ffn1_w_down"], saved1, dh1, "ffn1", token)
    exchange.gradients("ffn1", gb, dx)
    return loss[0, 0], dx, gs


SMALL = ("ffn1_norm", "mix_norm", "hg_lower_bounds", "hg_out_norm", "ffn2_norm", "final_norm")
WEIGHTS = ("ffn1_norm", "ffn1_w_gate_up", "ffn1_w_down", "mix_norm", "w_in", "hg_lower_bounds", "hg_out_norm",
           "w_branch_hg", "w_branch_att", "w_out", "ffn2_norm", "ffn2_w_gate_up", "ffn2_w_down", "final_norm")
SMALL_SHAPE = {"ffn1_norm": (1, 1024), "mix_norm": (1, 1024), "hg_lower_bounds": (2, 512), "hg_out_norm": (1, 512),
               "ffn2_norm": (1, 1024), "final_norm": (1024,)}
LOSS_ROW = 6


def _pack_small(vals):
    rows = []
    for n in SMALL:
        r = vals[n].reshape(1, -1).astype(F32)
        rows.append(jnp.pad(r, ((0, 0), (0, D_MODEL - r.shape[1]))))
    rows.append(jnp.zeros((SMALL_ROWS - len(SMALL), D_MODEL), F32))
    return jnp.concatenate(rows, axis=0)


def _unpack_small(packed):
    out = {}
    for i, n in enumerate(SMALL):
        size = int(np.prod(SMALL_SHAPE[n]))
        out[n] = packed[i, :size].reshape(SMALL_SHAPE[n])
    return out


class WeightExchange:
    ORDER = ("ffn1", "mix", "ffn2")

    def __init__(self, shards, core, chip):
        self.core, self.chip = core, chip
        self.placed = {n: place_own_block(shards[n], chip, r, cc, ax, f"place_{n}") for n, r, cc, ax in BIG}
        self.gathering = None
        self.scattering = None
        self.reduced = {}
        self.token = None

    def _names(self, group):
        return [BIG[i][0] for i in GROUPS[group]]

    def _start_gather(self, group, after):
        send_sems, recv_sems, bufs, self.token = gather_start([self.placed[n] for n in self._names(group)], after, group)
        self.gathering = (group, send_sems, recv_sems, bufs)

    def weights(self, group, h):
        if self.gathering is None:
            whole = all_gather_weights([self.placed[n] for n in self._names(group)], group)
        else:
            pending, send_sems, recv_sems, bufs = self.gathering
            assert pending == group
            whole = gather_forward(gather_wait(bufs, send_sems, recv_sems, h, group), group)
            self.gathering = None
        later = self.ORDER.index(group) + 1
        token = None
        if later < len(self.ORDER):
            self._start_gather(self.ORDER[later], whole[0])
            token = self.token
        return token, dict(zip(self._names(group), whole))

    def _finish_scatter(self, after):
        group, send_sems, recv_sems, halves, got = self.scattering
        halves, got = scatter_wait(halves, got, send_sems, recv_sems, after, group)
        ws = [BIG[i] for i in GROUPS[group]]
        mine = [add_pieces(h, g, self.chip, r, cc, ax, f"add_pieces_{n}") for (n, r, cc, ax), h, g in zip(ws, halves, got)]
        theirs = exchange_reduced(mine, group)
        self.reduced.update({n: (a, b) for (n, *_), a, b in zip(ws, mine, theirs)})
        self.scattering = None

    def gradients(self, group, grads, dh):
        if self.scattering is not None:
            self._finish_scatter(dh)
        ws = [BIG[i] for i in GROUPS[group]]
        theirs = exchange_halves([grads[n] for n, *_ in ws], group)
        halves = [add_halves(grads[n], t, self.core, r, cc, ax, f"add_halves_{n}") for (n, r, cc, ax), t in zip(ws, theirs)]
        send_sems, recv_sems, halves, got, self.token = scatter_start(halves, group)
        self.scattering = (group, send_sems, recv_sems, halves, got)
        return self.token

    def finish(self, after):
        self._finish_scatter(after)
        return self.reduced


def kernel(x, ffn1_norm, ffn1_w_gate_up, ffn1_w_down, mix_norm, w_in, hg_lower_bounds, hg_out_norm, w_branch_hg, w_branch_att, w_out, ffn2_norm, ffn2_w_gate_up, ffn2_w_down, final_norm, loss_target, m_ffn1_norm, m_ffn1_w_gate_up, m_ffn1_w_down, m_mix_norm, m_w_in, m_hg_lower_bounds, m_hg_out_norm, m_w_branch_hg, m_w_branch_att, m_w_out, m_ffn2_norm, m_ffn2_w_gate_up, m_ffn2_w_down, m_final_norm, v_ffn1_norm, v_ffn1_w_gate_up, v_ffn1_w_down, v_mix_norm, v_w_in, v_hg_lower_bounds, v_hg_out_norm, v_w_branch_hg, v_w_branch_att, v_w_out, v_ffn2_norm, v_ffn2_w_gate_up, v_ffn2_w_down, v_final_norm):
    w = dict(ffn1_norm=ffn1_norm, ffn1_w_gate_up=ffn1_w_gate_up, ffn1_w_down=ffn1_w_down, mix_norm=mix_norm, w_in=w_in,
             hg_lower_bounds=hg_lower_bounds, hg_out_norm=hg_out_norm, w_branch_hg=w_branch_hg, w_branch_att=w_branch_att,
             w_out=w_out, ffn2_norm=ffn2_norm, ffn2_w_gate_up=ffn2_w_gate_up, ffn2_w_down=ffn2_w_down, final_norm=final_norm)
    m = dict(ffn1_norm=m_ffn1_norm, ffn1_w_gate_up=m_ffn1_w_gate_up, ffn1_w_down=m_ffn1_w_down, mix_norm=m_mix_norm,
             w_in=m_w_in, hg_lower_bounds=m_hg_lower_bounds, hg_out_norm=m_hg_out_norm, w_branch_hg=m_w_branch_hg,
             w_branch_att=m_w_branch_att, w_out=m_w_out, ffn2_norm=m_ffn2_norm, ffn2_w_gate_up=m_ffn2_w_gate_up,
             ffn2_w_down=m_ffn2_w_down, final_norm=m_final_norm)
    v = dict(ffn1_norm=v_ffn1_norm, ffn1_w_gate_up=v_ffn1_w_gate_up, ffn1_w_down=v_ffn1_w_down, mix_norm=v_mix_norm,
             w_in=v_w_in, hg_lower_bounds=v_hg_lower_bounds, hg_out_norm=v_hg_out_norm, w_branch_hg=v_w_branch_hg,
             w_branch_att=v_w_branch_att, w_out=v_w_out, ffn2_norm=v_ffn2_norm, ffn2_w_gate_up=v_ffn2_w_gate_up,
             ffn2_w_down=v_ffn2_w_down, final_norm=v_final_norm)

    core = lax.axis_index("c").astype(jnp.int32).reshape(1)
    chip = (2 * lax.axis_index("x") + lax.axis_index("y")).astype(jnp.int32).reshape(1)
    exchange = WeightExchange({n: w[n][0] for n, *_ in BIG}, core, chip)
    small = {n: w[n] for n in SMALL}
    small["final_norm"] = final_norm.reshape(1, D_MODEL)

    loss, dx, gs = local_step(x[0], loss_target[0], small, exchange)

    grads, delta, new_m, new_v = {}, {}, {}, {}

    def update(group, core):
        for i in GROUPS[group]:
            n, r, cc, ax = BIG[i]
            a, b = exchange.reduced[n]
            g, d, nm, nv = adamw_halves(w[n][0], a, b, m[n][0], v[n][0], core, r, cc, ax, f"adamw_{n}")
            grads[n], delta[n], new_m[n], new_v[n] = g[None], d[None], nm[None], nv[None]

    core_behind = core + exchange.token[0, :1].astype(jnp.int32)
    update("ffn2", core_behind)
    update("mix", core_behind)
    exchange.finish(after=delta["w_in"])
    update("ffn1", core)
    packed = _pack_small(gs)
    packed = packed.at[LOSS_ROW].set(jnp.full((D_MODEL,), loss, F32))
    total = all_reduce_small(packed, behind=delta["ffn1_w_down"])
    grads.update(_unpack_small(total))
    loss_total = total[LOSS_ROW, 0]
    pd, pm, pv = adamw(_pack_small({n: w[n] for n in SMALL}), total.at[LOSS_ROW].set(0.0),
                       _pack_small({n: m[n] for n in SMALL}), _pack_small({n: v[n] for n in SMALL}), "adamw_small")
    delta.update(_unpack_small(pd))
    new_m.update(_unpack_small(pm))
    new_v.update(_unpack_small(pv))

    return (loss_total, dx[None], *[grads[n] for n in WEIGHTS], *[delta[n] for n in WEIGHTS],
            *[new_m[n] for n in WEIGHTS], *[new_v[n] for n in WEIGHTS])
```

```python
import numpy as np
import jax
import jax.numpy as jnp
from jax import lax
from jax.experimental import pallas as pl
from jax.experimental.pallas import tpu as pltpu

SEQ = 2048
D_MODEL = 1024
D_FF = 2816
HG_HEADS = 4
HG_DIM = 128
HG_WIDTH = 512
HG_CHUNK = 64
ATT_GROUPS = ((128, 1), (512, 4), (2048, 16))
ATT_HEADS = 8
ATT_WIDTH = 512
ATT_BLOCK = 128
ALIBI_MAX = 8.0
IN_COLS = 8704
EPS = 1e-6
NEG_INF = -1e30
ADAM_LR = 0.001
ADAM_B1 = 0.9
ADAM_B2 = 0.999
ADAM_EPS = 1e-08
ADAM_WD = 0.01
ADAM_STEP = 10

N_CHIPS = 4
MXU_DTYPE = jnp.bfloat16
HG_DOT_DTYPE = jnp.float32
WEIGHT_COMM_DTYPE = jnp.bfloat16
GRAD_COMM_DTYPE = jnp.bfloat16
MESH = pl.DeviceIdType.MESH
F32 = jnp.float32
HIGHEST = lax.Precision.HIGHEST


def _pick(n, cands):
    for c in cands:
        if n % c == 0:
            return c
    return n


def _sigmoid(x):
    return 1.0 / (1.0 + jnp.exp(-x))


def _dot(a, b, ta=False, tb=False):
    dn = (((0 if ta else 1,), (1 if tb else 0,)), ((), ()))
    return lax.dot_general(a.astype(MXU_DTYPE), b.astype(MXU_DTYPE), dn, preferred_element_type=F32)


def _dot_f32(a, b):
    return jnp.dot(a, b, precision=HIGHEST, preferred_element_type=F32)


def _hdot(a, b, ta=False, tb=False):
    if HG_DOT_DTYPE == F32:
        dn = (((0 if ta else 1,), (1 if tb else 0,)), ((), ()))
        return lax.dot_general(a, b, dn, precision=HIGHEST, preferred_element_type=F32)
    return _dot(a, b, ta, tb)


MATMUL_VMEM_BYTES = 48 * 1024 * 1024
MATMUL_TILE_BYTES = 36 * 1024 * 1024
MXU_ALIGN = 128


def _divisors(n, most):
    return [t for t in range(min(n, most), 0, -MXU_ALIGN) if n % t == 0 and t % MXU_ALIGN == 0]


def _matmul_tiles(M, N, K, in_bytes, out_bytes, has_res):
    best = None
    for tk in _divisors(K, K):
        nk = K // tk
        for tm in _divisors(M, 2048):
            for tn in _divisors(N, 512):
                tiles = 2 * in_bytes * (tm * tk + tk * tn) + 2 * out_bytes * tm * tn
                tiles += 4 * tm * tn * ((nk > 1) + 2 * has_res)
                if tiles > MATMUL_TILE_BYTES:
                    continue
                traffic = in_bytes * (M * K * (1 if nk == 1 else N // tn) + K * N * (M // tm))
                key = (traffic, -tm * tn * tk)
                if best is None or key < best[0]:
                    best = (key, (tm, tn, tk))
    return best[1]


def matmul(a, b, *, ta=False, tb=False, out_dtype=F32, res=None, scale=1.0, behind=(), name):
    if ta:
        K, M = a.shape
    else:
        M, K = a.shape
    if tb:
        N, K2 = b.shape
    else:
        K2, N = b.shape
    assert K == K2 and a.dtype == b.dtype
    tm, tn, tk = _matmul_tiles(M, N, K, a.dtype.itemsize, jnp.dtype(out_dtype).itemsize, res is not None)
    nk = K // tk

    def finish(r, r_ref, o_ref):
        if scale != 1.0:
            r = r * scale
        if res is not None:
            r = r_ref[...] + r
        o_ref[...] = r.astype(out_dtype)

    def body(*refs):
        a_ref, b_ref = refs[:2]
        r_ref = refs[2] if res is not None else None
        o_ref = refs[2 + (res is not None) + len(behind)]
        if nk == 1:
            finish(_dot(a_ref[...], b_ref[...], ta, tb), r_ref, o_ref)
            return
        acc = refs[-1]
        k = pl.program_id(2)

        @pl.when(k == 0)
        def _():
            acc[...] = jnp.zeros_like(acc)

        acc[...] += _dot(a_ref[...], b_ref[...], ta, tb)

        @pl.when(k == nk - 1)
        def _():
            finish(acc[...], r_ref, o_ref)

    a_spec = pl.BlockSpec((tk, tm), lambda i, j, k: (k, i)) if ta else pl.BlockSpec((tm, tk), lambda i, j, k: (i, k))
    b_spec = pl.BlockSpec((tn, tk), lambda i, j, k: (j, k)) if tb else pl.BlockSpec((tk, tn), lambda i, j, k: (k, j))
    in_specs = [a_spec, b_spec]
    args = [a, b]
    if res is not None:
        in_specs.append(pl.BlockSpec((tm, tn), lambda i, j, k: (i, j)))
        args.append(res)
    for earlier in behind:
        in_specs.append(pl.BlockSpec(memory_space=pl.ANY))
        args.append(earlier)
    return pl.pallas_call(
        body, name=name, grid=(M // tm, N // tn, nk), in_specs=in_specs,
        out_specs=pl.BlockSpec((tm, tn), lambda i, j, k: (i, j)),
        out_shape=jax.ShapeDtypeStruct((M, N), out_dtype),
        scratch_shapes=[pltpu.VMEM((tm, tn), F32)] if nk > 1 else [],
        compiler_params=pltpu.CompilerParams(dimension_semantics=("parallel", "parallel", "arbitrary"),
                                             vmem_limit_bytes=MATMUL_VMEM_BYTES),
    )(*args)


ROW_TILE = 256


def rmsnorm_fwd(x, g, name, behind=()):
    def body(x_ref, g_ref, *refs):
        n_ref = refs[-1]
        xv = x_ref[...]
        r = lax.rsqrt(jnp.mean(xv * xv, axis=-1, keepdims=True) + EPS)
        n_ref[...] = ((xv * r) * g_ref[...]).astype(n_ref.dtype)

    order = list(behind)
    return pl.pallas_call(
        body, name=name, grid=(SEQ // ROW_TILE,),
        in_specs=[pl.BlockSpec((ROW_TILE, D_MODEL), lambda i: (i, 0)), pl.BlockSpec((1, D_MODEL), lambda i: (0, 0))]
        + [pl.BlockSpec(memory_space=pl.ANY)] * len(order),
        out_specs=pl.BlockSpec((ROW_TILE, D_MODEL), lambda i: (i, 0)),
        out_shape=jax.ShapeDtypeStruct((SEQ, D_MODEL), MXU_DTYPE),
    )(x, g, *order)


def rmsnorm_bwd(x, g, dn, dres, name):
    def body(x_ref, g_ref, dn_ref, dr_ref, dx_ref, dg_ref):
        xv = x_ref[...]
        r = lax.rsqrt(jnp.mean(xv * xv, axis=-1, keepdims=True) + EPS)
        xh = xv * r
        dnv = dn_ref[...]

        @pl.when(pl.program_id(0) == 0)
        def _():
            dg_ref[...] = jnp.zeros_like(dg_ref)

        dg_ref[...] += jnp.sum(dnv * xh, axis=0, keepdims=True)
        dxh = dnv * g_ref[...]
        dx_ref[...] = dr_ref[...] + r * (dxh - xh * jnp.mean(dxh * xh, axis=-1, keepdims=True))

    row = pl.BlockSpec((ROW_TILE, D_MODEL), lambda i: (i, 0))
    vec = pl.BlockSpec((1, D_MODEL), lambda i: (0, 0))
    return pl.pallas_call(
        body, name=name, grid=(SEQ // ROW_TILE,), in_specs=[row, vec, row, row], out_specs=[row, vec],
        out_shape=[jax.ShapeDtypeStruct((SEQ, D_MODEL), F32), jax.ShapeDtypeStruct((1, D_MODEL), F32)],
        compiler_params=pltpu.CompilerParams(dimension_semantics=("arbitrary",)),
    )(x, g, dn, dres)


def final_norm_loss(h, g, target, name):
    def body(h_ref, g_ref, t_ref, dh_ref, dg_ref, loss_ref):
        xv = h_ref[...]
        r = lax.rsqrt(jnp.mean(xv * xv, axis=-1, keepdims=True) + EPS)
        xh = xv * r
        gv = g_ref[...]
        e = xh * gv - t_ref[...]

        @pl.when(pl.program_id(0) == 0)
        def _():
            dg_ref[...] = jnp.zeros_like(dg_ref)
            loss_ref[...] = jnp.zeros_like(loss_ref)

        part = 0.5 * jnp.sum(jnp.sum(e * e, axis=-1, keepdims=True) * (1.0 / D_MODEL), axis=0, keepdims=True)
        loss_ref[...] += jnp.broadcast_to(part, loss_ref.shape)
        dout = e * (1.0 / D_MODEL)
        dg_ref[...] += jnp.sum(dout * xh, axis=0, keepdims=True)
        dxh = dout * gv
        dh_ref[...] = r * (dxh - xh * jnp.mean(dxh * xh, axis=-1, keepdims=True))

    row = pl.BlockSpec((ROW_TILE, D_MODEL), lambda i: (i, 0))
    vec = pl.BlockSpec((1, D_MODEL), lambda i: (0, 0))
    return pl.pallas_call(
        body, name=name, grid=(SEQ // ROW_TILE,), in_specs=[row, vec, row],
        out_specs=[row, vec, pl.BlockSpec((8, 128), lambda i: (0, 0))],
        out_shape=[jax.ShapeDtypeStruct((SEQ, D_MODEL), F32), jax.ShapeDtypeStruct((1, D_MODEL), F32),
                   jax.ShapeDtypeStruct((8, 128), F32)],
        compiler_params=pltpu.CompilerParams(dimension_semantics=("arbitrary",)),
    )(h, g, target)


FF_TILE = D_FF // 2


def swiglu_fwd(gu, name):
    def body(a_ref, b_ref, s_ref):
        a = a_ref[...]
        s_ref[...] = (a * _sigmoid(a) * b_ref[...]).astype(s_ref.dtype)

    return pl.pallas_call(
        body, name=name, grid=(SEQ // ROW_TILE, 2),
        in_specs=[pl.BlockSpec((ROW_TILE, FF_TILE), lambda i, j: (i, j)),
                  pl.BlockSpec((ROW_TILE, FF_TILE), lambda i, j: (i, j + 2))],
        out_specs=pl.BlockSpec((ROW_TILE, FF_TILE), lambda i, j: (i, j)),
        out_shape=jax.ShapeDtypeStruct((SEQ, D_FF), MXU_DTYPE),
    )(gu, gu)


def swiglu_bwd(gu, ds, name):
    def body(a_ref, b_ref, ds_ref, o_ref):
        a = a_ref[...]
        sg = _sigmoid(a)
        dsv = ds_ref[...]

        @pl.when(pl.program_id(1) < 2)
        def _():
            o_ref[...] = (dsv * b_ref[...] * (sg * (1.0 + a * (1.0 - sg)))).astype(o_ref.dtype)

        @pl.when(pl.program_id(1) >= 2)
        def _():
            o_ref[...] = (dsv * a * sg).astype(o_ref.dtype)

    return pl.pallas_call(
        body, name=name, grid=(SEQ // ROW_TILE, 4),
        in_specs=[pl.BlockSpec((ROW_TILE, FF_TILE), lambda i, j: (i, j % 2)),
                  pl.BlockSpec((ROW_TILE, FF_TILE), lambda i, j: (i, j % 2 + 2)),
                  pl.BlockSpec((ROW_TILE, FF_TILE), lambda i, j: (i, j % 2))],
        out_specs=pl.BlockSpec((ROW_TILE, FF_TILE), lambda i, j: (i, j)),
        out_shape=jax.ShapeDtypeStruct((SEQ, 2 * D_FF), MXU_DTYPE),
    )(gu, gu, ds)


GATE_HG_BLK = 6656 // 512
GATE_ATT_BLK = 7680 // 512


def merge_fwd(z, bh, ba, name):
    def body(gh_ref, ga_ref, bh_ref, ba_ref, o_ref):
        o_ref[...] = (_sigmoid(gh_ref[...]) * bh_ref[...] + _sigmoid(ga_ref[...]) * ba_ref[...]).astype(o_ref.dtype)

    blk = pl.BlockSpec((ROW_TILE, 512), lambda i, j: (i, j))
    return pl.pallas_call(
        body, name=name, grid=(SEQ // ROW_TILE, 2),
        in_specs=[pl.BlockSpec((ROW_TILE, 512), lambda i, j: (i, GATE_HG_BLK + j)),
                  pl.BlockSpec((ROW_TILE, 512), lambda i, j: (i, GATE_ATT_BLK + j)), blk, blk],
        out_specs=blk, out_shape=jax.ShapeDtypeStruct((SEQ, D_MODEL), MXU_DTYPE),
    )(z, z, bh, ba)


def merge_bwd(z, bh, ba, dm, name):
    def body(gh_ref, ga_ref, bh_ref, ba_ref, dm_ref, dbh_ref, dba_ref, dgh_ref, dga_ref):
        dmv = dm_ref[...]
        sh = _sigmoid(gh_ref[...])
        sa = _sigmoid(ga_ref[...])
        dbh_ref[...] = (dmv * sh).astype(dbh_ref.dtype)
        dba_ref[...] = (dmv * sa).astype(dba_ref.dtype)
        dgh_ref[...] = (dmv * bh_ref[...] * (sh * (1.0 - sh))).astype(dgh_ref.dtype)
        dga_ref[...] = (dmv * ba_ref[...] * (sa * (1.0 - sa))).astype(dga_ref.dtype)

    blk = pl.BlockSpec((ROW_TILE, 512), lambda i, j: (i, j))
    out = jax.ShapeDtypeStruct((SEQ, D_MODEL), MXU_DTYPE)
    return pl.pallas_call(
        body, name=name, grid=(SEQ // ROW_TILE, 2),
        in_specs=[pl.BlockSpec((ROW_TILE, 512), lambda i, j: (i, GATE_HG_BLK + j)),
                  pl.BlockSpec((ROW_TILE, 512), lambda i, j: (i, GATE_ATT_BLK + j)), blk, blk, blk],
        out_specs=[blk, blk, blk, blk], out_shape=[out, out, out, out],
    )(z, z, bh, ba, dm)


N_CHUNKS = SEQ // HG_CHUNK


def _hgrn_gates(q, fp, lb):
    C = HG_CHUNK
    sg = _sigmoid(fp)
    f = lb + (1.0 - lb) * sg
    lf = jnp.log(f)
    row = lax.broadcasted_iota(jnp.int32, (C, C), 0)
    col = lax.broadcasted_iota(jnp.int32, (C, C), 1)
    causal = row >= col
    G = _dot_f32(causal.astype(F32), lf)
    eG = jnp.exp(G)
    enG = jnp.exp(-G)
    qg = q * eG
    kg = (1.0 - f) * enG
    A = jnp.where(causal, _hdot(qg, kg, tb=True), 0.0)
    egl = jnp.exp(jnp.sum(lf, axis=0, keepdims=True))
    return sg, f, causal, eG, enG, qg, kg, A, egl


def hgrn_fwd(z, lb, gain, name):
    C, K = HG_CHUNK, HG_DIM

    def body(q_ref, f_ref, v_ref, og_ref, p_ref, g_ref, y_ref, o_ref, st_ref, state):
        @pl.when(pl.program_id(0) == 0)
        def _():
            state[...] = jnp.zeros_like(state)

        for h in range(HG_HEADS):
            hd = pl.ds(h * K, K)
            v = v_ref[:, hd]
            _, _, _, _, _, qg, kg, A, egl = _hgrn_gates(q_ref[:, hd], f_ref[:, hd], p_ref[:, hd])
            st = state[h]
            st_ref[h, 0] = st
            o = _hdot(A, v) + _hdot(qg, st, tb=True)
            state[h] = st * egl + _hdot(v, kg * egl, ta=True)
            o_ref[:, hd] = o
            rs = lax.rsqrt(jnp.mean(o * o, axis=-1, keepdims=True) + EPS)
            og = og_ref[:, hd]
            y_ref[:, hd] = (((o * rs) * g_ref[:, hd]) * (og * _sigmoid(og))).astype(y_ref.dtype)

    def zcol(section):
        return pl.BlockSpec((C, HG_WIDTH), lambda c: (c, section))

    vec = pl.BlockSpec((1, HG_WIDTH), lambda c: (0, 0))
    blk = pl.BlockSpec((C, HG_WIDTH), lambda c: (c, 0))
    return pl.pallas_call(
        body, name=name, grid=(N_CHUNKS,),
        in_specs=[zcol(0), zcol(1), zcol(2), zcol(3), vec, vec],
        out_specs=[blk, blk, pl.BlockSpec((HG_HEADS, 1, K, K), lambda c: (0, c, 0, 0))],
        out_shape=[jax.ShapeDtypeStruct((SEQ, HG_WIDTH), MXU_DTYPE), jax.ShapeDtypeStruct((SEQ, HG_WIDTH), F32),
                   jax.ShapeDtypeStruct((HG_HEADS, N_CHUNKS, K, K), F32)],
        scratch_shapes=[pltpu.VMEM((HG_HEADS, K, K), F32)],
        compiler_params=pltpu.CompilerParams(dimension_semantics=("arbitrary",)),
    )(z, z, z, z, lb, gain)


def hgrn_bwd(z, lb, gain, o_raw, states, dy, name):
    C, K = HG_CHUNK, HG_DIM

    def body(q_ref, f_ref, v_ref, og_ref, p_ref, g_ref, o_ref, st_ref, dy_ref,
             dq_ref, dfp_ref, dv_ref, dog_ref, dlb_ref, dgain_ref, dstate):
        @pl.when(pl.program_id(0) == 0)
        def _():
            dstate[...] = jnp.zeros_like(dstate)
            dlb_ref[...] = jnp.zeros_like(dlb_ref)
            dgain_ref[...] = jnp.zeros_like(dgain_ref)

        last = lax.broadcasted_iota(jnp.int32, (C, K), 0) == C - 1
        row = lax.broadcasted_iota(jnp.int32, (C, C), 0)
        col = lax.broadcasted_iota(jnp.int32, (C, C), 1)
        anti_causal = (col >= row).astype(F32)
        for h in range(HG_HEADS):
            hd = pl.ds(h * K, K)
            v = v_ref[:, hd]
            lb = p_ref[:, hd]
            sg, f, causal, eG, enG, qg, kg, A, egl = _hgrn_gates(q_ref[:, hd], f_ref[:, hd], lb)
            kd = kg * egl
            st = st_ref[h, 0]
            dst = dstate[h]
            o = o_ref[:, hd]
            og = og_ref[:, hd]
            gain_v = g_ref[:, hd]
            dyv = dy_ref[:, hd]
            rs = lax.rsqrt(jnp.mean(o * o, axis=-1, keepdims=True) + EPS)
            on = o * rs
            sgo = _sigmoid(og)
            silu = og * sgo
            dog_ref[:, hd] = (dyv * (on * gain_v) * (sgo * (1.0 + og * (1.0 - sgo)))).astype(dog_ref.dtype)
            dgain_ref[:, hd] += jnp.sum(dyv * silu * on, axis=0, keepdims=True)
            don = dyv * gain_v * silu
            do = rs * (don - on * jnp.mean(don * on, axis=-1, keepdims=True))
            dA = jnp.where(causal, _hdot(do, v, tb=True), 0.0)
            dv_ref[:, hd] = (_hdot(A, do, ta=True) + _hdot(kd, dst, tb=True)).astype(dv_ref.dtype)
            dqg = _hdot(dA, kg) + _hdot(do, st)
            dkg = _hdot(dA, qg, ta=True)
            dkd = _hdot(v, dst)
            dstate[h] = dst * egl + _hdot(do, qg, ta=True)
            dgl = jnp.sum(st * dst, axis=0, keepdims=True) * egl
            dq_ref[:, hd] = (dqg * eG).astype(dq_ref.dtype)
            dk = dkg * enG + dkd * (enG * egl)
            dG = dqg * qg - dkg * kg - dkd * kd
            extra = jnp.sum(dkd * kd, axis=0, keepdims=True) + dgl
            dG = dG + jnp.where(last, extra, 0.0)
            dlf = _dot_f32(anti_causal, dG)
            df = dlf / f - dk
            dfp_ref[:, hd] = (df * (1.0 - lb) * (sg * (1.0 - sg))).astype(dfp_ref.dtype)
            dlb_ref[:, hd] += jnp.sum(df * (1.0 - sg), axis=0, keepdims=True)

    def rc(c):
        return N_CHUNKS - 1 - c

    def zcol(section):
        return pl.BlockSpec((C, HG_WIDTH), lambda c: (rc(c), section))

    vec = pl.BlockSpec((1, HG_WIDTH), lambda c: (0, 0))
    blk = pl.BlockSpec((C, HG_WIDTH), lambda c: (rc(c), 0))
    out = jax.ShapeDtypeStruct((SEQ, HG_WIDTH), MXU_DTYPE)
    small = jax.ShapeDtypeStruct((1, HG_WIDTH), F32)
    return pl.pallas_call(
        body, name=name, grid=(N_CHUNKS,),
        in_specs=[zcol(0), zcol(1), zcol(2), zcol(3), vec, vec, blk,
                  pl.BlockSpec((HG_HEADS, 1, K, K), lambda c: (0, rc(c), 0, 0)), blk],
        out_specs=[blk, blk, blk, blk, vec, vec],
        out_shape=[out, out, out, out, small, small],
        scratch_shapes=[pltpu.VMEM((HG_HEADS, K, K), F32)],
        compiler_params=pltpu.CompilerParams(dimension_semantics=("arbitrary",)),
    )(z, z, z, z, lb, gain, o_raw, states, dy)


N_GROUPS = len(ATT_GROUPS)
HEAD_PAIRS = ATT_WIDTH // 128
ATT_COL0 = 4 * HG_WIDTH
UNROLLED_SUBSEQS = 4


def _alibi_coef():
    n = N_GROUPS * ATT_HEADS
    slopes = np.exp2(-ALIBI_MAX * np.arange(1, n + 1, dtype=np.float32) / n).astype(np.float32)
    dil = np.repeat(np.array([d for _, d in ATT_GROUPS], np.float32), ATT_HEADS)
    return jnp.asarray(slopes * dil, F32)


def _subseq_rows(r, d):
    return pl.ds(r, ATT_BLOCK, stride=d) if d > 1 else pl.ds(0, ATT_BLOCK)


def _for_each_subseq(d, fn):
    if d <= UNROLLED_SUBSEQS:
        for r in range(d):
            fn(r)
    else:
        lax.fori_loop(0, d, lambda r, carry: (fn(r), carry)[1], 0)


def _att_specs(g):
    d = ATT_GROUPS[g][1]
    R = ATT_BLOCK * d
    n_slabs = SEQ // R
    col0 = (ATT_COL0 + g * 3 * ATT_WIDTH) // 128

    def cur(col):
        return pl.BlockSpec((R, 128), lambda hp, s: (s, col + hp))

    def prev(col):
        return pl.BlockSpec((R, 128), lambda hp, s: (jnp.maximum(s - 1, 0), col + hp))

    def nxt(col):
        return pl.BlockSpec((R, 128), lambda hp, s: (jnp.minimum(s + 1, n_slabs - 1), col + hp))

    return d, R, n_slabs, col0, cur, prev, nxt


def _head_lanes(j):
    lane = lax.broadcasted_iota(jnp.int32, (ATT_BLOCK, 128), 1)
    return (lane >= 64 * j) & (lane < 64 * (j + 1))


def _lane_value(x, sel):
    return jnp.max(jnp.where(sel, x, -3e38), axis=-1, keepdims=True)


def _stack_heads(x, sel0):
    return jnp.concatenate([jnp.where(sel0, x, 0.0), jnp.where(sel0, 0.0, x)], axis=0)


def _stack_values(x, sel0):
    return jnp.concatenate([_lane_value(x, sel0), _lane_value(x, jnp.logical_not(sel0))], axis=0)


def _pair_coef(coef_ref, g, hp):
    row = lax.broadcasted_iota(jnp.int32, (2 * ATT_BLOCK, 1), 0)
    first = g * ATT_HEADS + hp * 2
    return jnp.where(row < ATT_BLOCK, coef_ref[first], coef_ref[first + 1])


def _band(with_prev, first_key):
    B = ATT_BLOCK
    keys = 2 * B if with_prev else B
    qi = jnp.bitwise_and(lax.broadcasted_iota(jnp.int32, (2 * B, keys), 0), B - 1)
    kj = lax.broadcasted_iota(jnp.int32, (2 * B, keys), 1)
    delta = qi + (B if with_prev else 0) - kj
    valid = (delta >= 0) & (delta <= B)
    if with_prev:
        valid = valid & (kj >= first_key)
    return valid, delta.astype(F32)


def _band_next(last_slab):
    B = ATT_BLOCK
    qi = jnp.bitwise_and(lax.broadcasted_iota(jnp.int32, (2 * B, B), 0), B - 1)
    kj = lax.broadcasted_iota(jnp.int32, (2 * B, B), 1)
    delta = qi + B - kj
    return (delta <= B) & (kj >= jnp.where(last_slab, B, 0)), delta.astype(F32)


def att_fwd(z, g, name):
    B = ATT_BLOCK
    d, R, n_slabs, col0, cur, prev, _ = _att_specs(g)
    has_prev = n_slabs > 1

    def body(coef_ref, *refs):
        if has_prev:
            q_ref, kc_ref, vc_ref, kp_ref, vp_ref, o_ref, l_ref = refs
        else:
            q_ref, kc_ref, vc_ref, o_ref, l_ref = refs
        hp, s = pl.program_id(0), pl.program_id(1)
        valid, dist = _band(has_prev, jnp.where(s == 0, B, 0))
        cf2 = _pair_coef(coef_ref, g, hp)
        sel0 = _head_lanes(0)

        def one(r):
            rows = _subseq_rows(r, d)
            q2 = _stack_heads(q_ref[rows, :], sel0)
            kk, vv = kc_ref[rows, :], vc_ref[rows, :]
            if has_prev:
                kk = jnp.concatenate([kp_ref[rows, :], kk], axis=0)
                vv = jnp.concatenate([vp_ref[rows, :], vv], axis=0)
            sc = jnp.where(valid, _dot(q2, kk, tb=True) * 0.125 - cf2 * dist, NEG_INF)
            mx = jnp.max(sc, axis=-1, keepdims=True)
            e = jnp.exp(sc - mx)
            den = jnp.sum(e, axis=-1, keepdims=True)
            o2 = _dot(e * (1.0 / den), vv)
            lse2 = mx + jnp.log(den)
            o_ref[rows, :] = jnp.where(sel0, o2[:B], o2[B:])
            l_ref[rows, :] = jnp.where(sel0, lse2[:B], lse2[B:])

        _for_each_subseq(d, one)

    in_specs = [pl.BlockSpec(memory_space=pltpu.SMEM), cur(col0), cur(col0 + 4), cur(col0 + 8)]
    args = [_alibi_coef(), z, z, z]
    if has_prev:
        in_specs += [prev(col0 + 4), prev(col0 + 8)]
        args += [z, z]
    out = jax.ShapeDtypeStruct((SEQ, ATT_WIDTH), F32)
    return pl.pallas_call(
        body, name=name, grid=(HEAD_PAIRS, n_slabs), in_specs=in_specs,
        out_specs=[cur(0), cur(0)], out_shape=[out, out],
        compiler_params=pltpu.CompilerParams(dimension_semantics=("parallel", "arbitrary")),
    )(*args)


def att_bwd(z, l, do, corr, g, name):
    B = ATT_BLOCK
    d, R, n_slabs, col0, cur, prev, nxt = _att_specs(g)
    neighbours = n_slabs > 1

    def body(coef_ref, *refs):
        if neighbours:
            (q_ref, kc_ref, vc_ref, l_ref, do_ref, cr_ref, kp_ref, vp_ref, qn_ref, ln_ref, don_ref, crn_ref,
             dq_ref, dk_ref, dv_ref, dq_sc, dk_sc, dv_sc) = refs
        else:
            q_ref, kc_ref, vc_ref, l_ref, do_ref, cr_ref, dq_ref, dk_ref, dv_ref, dq_sc, dk_sc, dv_sc = refs
        hp, s = pl.program_id(0), pl.program_id(1)
        valid, dist = _band(neighbours, jnp.where(s == 0, B, 0))
        if neighbours:
            valid_n, dist_n = _band_next(s == n_slabs - 1)
        cf2 = _pair_coef(coef_ref, g, hp)
        sel0 = _head_lanes(0)
        own = slice(B, 2 * B) if neighbours else slice(0, B)

        def one(r):
            rows = _subseq_rows(r, d)
            kc, vc = kc_ref[rows, :], vc_ref[rows, :]
            kk, vv = kc, vc
            if neighbours:
                kk = jnp.concatenate([kp_ref[rows, :], kc], axis=0)
                vv = jnp.concatenate([vp_ref[rows, :], vc], axis=0)
            q2, do2 = _stack_heads(q_ref[rows, :], sel0), _stack_heads(do_ref[rows, :], sel0)
            lse2, cr2 = _stack_values(l_ref[rows, :], sel0), _stack_values(cr_ref[rows, :], sel0)
            p = jnp.exp(jnp.where(valid, _dot(q2, kk, tb=True) * 0.125 - cf2 * dist, NEG_INF) - lse2)
            ds = p * (_dot(do2, vv, tb=True) + cr2)
            dq2 = _dot(ds, kk)
            dk = _dot(ds, q2, ta=True)[own]
            dv = _dot(p, do2, ta=True)[own]
            if neighbours:
                qn2, don2 = _stack_heads(qn_ref[rows, :], sel0), _stack_heads(don_ref[rows, :], sel0)
                lse_n2, cr_n2 = _stack_values(ln_ref[rows, :], sel0), _stack_values(crn_ref[rows, :], sel0)
                p_n = jnp.exp(jnp.where(valid_n, _dot(qn2, kc, tb=True) * 0.125 - cf2 * dist_n, NEG_INF) - lse_n2)
                ds_n = p_n * (_dot(don2, vc, tb=True) + cr_n2)
                dk = dk + _dot(ds_n, qn2, ta=True)
                dv = dv + _dot(p_n, don2, ta=True)
            dq_sc[rows, :] = jnp.where(sel0, dq2[:B], dq2[B:]) * 0.125
            dk_sc[rows, :] = dk * 0.125
            dv_sc[rows, :] = dv

        _for_each_subseq(d, one)
        dq_ref[...] = dq_sc[...].astype(dq_ref.dtype)
        dk_ref[...] = dk_sc[...].astype(dk_ref.dtype)
        dv_ref[...] = dv_sc[...].astype(dv_ref.dtype)

    in_specs = [pl.BlockSpec(memory_space=pltpu.SMEM), cur(col0), cur(col0 + 4), cur(col0 + 8), cur(0), cur(0), cur(0)]
    args = [_alibi_coef(), z, z, z, l, do, corr]
    if neighbours:
        in_specs += [prev(col0 + 4), prev(col0 + 8), nxt(col0), nxt(0), nxt(0), nxt(0)]
        args += [z, z, z, l, do, corr]
    out = jax.ShapeDtypeStruct((SEQ, ATT_WIDTH), MXU_DTYPE)
    return pl.pallas_call(
        body, name=name, grid=(HEAD_PAIRS, n_slabs), in_specs=in_specs,
        out_specs=[cur(0)] * 3, out_shape=[out] * 3,
        scratch_shapes=[pltpu.VMEM((R, 128), F32)] * 3,
        compiler_params=pltpu.CompilerParams(dimension_semantics=("parallel", "arbitrary"),
                                             vmem_limit_bytes=MATMUL_VMEM_BYTES),
    )(*args)


def _head_sum(x):
    i = lax.broadcasted_iota(jnp.int32, (128, 128), 0) // 64
    j = lax.broadcasted_iota(jnp.int32, (128, 128), 1) // 64
    return _dot_f32(x, (i == j).astype(F32))


def _group_weights(l0, l1, l2):
    mx = jnp.maximum(jnp.maximum(l0, l1), l2)
    e0, e1, e2 = jnp.exp(l0 - mx), jnp.exp(l1 - mx), jnp.exp(l2 - mx)
    inv = 1.0 / (e0 + e1 + e2)
    return e0 * inv, e1 * inv, e2 * inv


def att_combine_fwd(o, l, name):
    def body(o0, o1, o2, l0, l1, l2, y_ref):
        w0, w1, w2 = _group_weights(l0[...], l1[...], l2[...])
        y_ref[...] = (o0[...] * w0 + o1[...] * w1 + o2[...] * w2).astype(y_ref.dtype)

    blk = pl.BlockSpec((ROW_TILE, ATT_WIDTH), lambda i: (i, 0))
    return pl.pallas_call(
        body, name=name, grid=(SEQ // ROW_TILE,), in_specs=[blk] * 6, out_specs=blk,
        out_shape=jax.ShapeDtypeStruct((SEQ, ATT_WIDTH), MXU_DTYPE),
    )(*o, *l)


def att_combine_bwd(o, l, dy, name):
    def body(o0, o1, o2, l0, l1, l2, dy_ref, do0, do1, do2, cr0, cr1, cr2):
        w = _group_weights(l0[...], l1[...], l2[...])
        dyv = dy_ref[...]
        dw = [_head_sum(dyv * o_ref[...]) for o_ref in (o0, o1, o2)]
        tot = w[0] * dw[0] + w[1] * dw[1] + w[2] * dw[2]
        for g, (do_ref, cr_ref) in enumerate(((do0, cr0), (do1, cr1), (do2, cr2))):
            do_ref[...] = dyv * w[g]
            cr_ref[...] = -w[g] * tot

    blk = pl.BlockSpec((ROW_TILE, 128), lambda i, j: (i, j))
    out = jax.ShapeDtypeStruct((SEQ, ATT_WIDTH), F32)
    res = pl.pallas_call(
        body, name=name, grid=(SEQ // ROW_TILE, HEAD_PAIRS), in_specs=[blk] * 7, out_specs=[blk] * 6, out_shape=[out] * 6,
    )(*o, *l, dy)
    return res[:N_GROUPS], res[N_GROUPS:]


SUM_ROW_TILES = (256, 128, 64, 32, 16)
SUM_TILE_ELEMS = 128 * 1024


def _row_tile(rows, cols):
    fit = [t for t in SUM_ROW_TILES if rows % t == 0]
    return next((t for t in fit if t * cols <= SUM_TILE_ELEMS), fit[-1])


def _shard_shape(rows, cols, axis):
    return (rows // N_CHIPS, cols) if axis == 0 else (rows, cols // N_CHIPS)


def _half_shape(rows, cols, axis):
    return (rows, cols // 2) if axis == 0 else (rows // 2, cols)


def _piece_shape(rows, cols, axis):
    return (rows // N_CHIPS, cols // 2) if axis == 0 else (rows // 2, cols // N_CHIPS)


def place_own_block(shard, chip, rows, cols, axis, name):
    sr, sc = _shard_shape(rows, cols, axis)
    tr = _row_tile(sr, sc)

    def body(chip_ref, s_ref, o_ref):
        o_ref[...] = s_ref[...].astype(o_ref.dtype)

    if axis == 0:
        out_map = lambda i, chip_ref: (chip_ref[0] * (sr // tr) + i, 0)
    else:
        out_map = lambda i, chip_ref: (i, chip_ref[0])
    return pl.pallas_call(
        body, name=name, out_shape=jax.ShapeDtypeStruct((rows, cols), WEIGHT_COMM_DTYPE),
        grid_spec=pltpu.PrefetchScalarGridSpec(
            num_scalar_prefetch=1, grid=(sr // tr,), in_specs=[pl.BlockSpec((tr, sc), lambda i, chip_ref: (i, 0))],
            out_specs=pl.BlockSpec((tr, sc), out_map)),
    )(chip, shard)


def add_halves(g, theirs, core, rows, cols, axis, name):
    hr, hc = _half_shape(rows, cols, axis)
    tr = _row_tile(hr, hc)

    def body(core_ref, g_ref, t_ref, o_ref):
        o_ref[...] = (g_ref[...].astype(F32) + t_ref[...].astype(F32)).astype(o_ref.dtype)

    if axis == 0:
        g_map = lambda i, core_ref: (i, core_ref[0])
    else:
        g_map = lambda i, core_ref: (core_ref[0] * (hr // tr) + i, 0)
    blk = pl.BlockSpec((tr, hc), lambda i, core_ref: (i, 0))
    return pl.pallas_call(
        body, name=name, out_shape=jax.ShapeDtypeStruct((hr, hc), GRAD_COMM_DTYPE),
        grid_spec=pltpu.PrefetchScalarGridSpec(
            num_scalar_prefetch=1, grid=(hr // tr,), in_specs=[pl.BlockSpec((tr, hc), g_map), blk], out_specs=blk),
    )(core, g, theirs)


def add_pieces(half, got, chip, rows, cols, axis, name):
    hr, _ = _half_shape(rows, cols, axis)
    pr, pc = _piece_shape(rows, cols, axis)
    tr = _row_tile(pr, pc)

    def body(chip_ref, h_ref, got_ref, o_ref):
        o_ref[...] = (h_ref[...].astype(F32) + got_ref[0].astype(F32) + got_ref[1].astype(F32) + got_ref[2].astype(F32))

    if axis == 0:
        h_map = lambda i, chip_ref: (chip_ref[0] * (pr // tr) + i, 0)
    else:
        h_map = lambda i, chip_ref: (i, chip_ref[0])
    return pl.pallas_call(
        body, name=name, out_shape=jax.ShapeDtypeStruct((pr, pc), F32),
        grid_spec=pltpu.PrefetchScalarGridSpec(
            num_scalar_prefetch=1, grid=(pr // tr,),
            in_specs=[pl.BlockSpec((tr, pc), h_map), pl.BlockSpec((3, tr, pc), lambda i, chip_ref: (0, i, 0))],
            out_specs=pl.BlockSpec((tr, pc), lambda i, chip_ref: (i, 0))),
    )(chip, half, got)


def _adamw_math(w, g, m, v):
    nm = ADAM_B1 * m + (1.0 - ADAM_B1) * g
    nv = ADAM_B2 * v + (1.0 - ADAM_B2) * (g * g)
    m_hat = nm / (1.0 - ADAM_B1 ** ADAM_STEP)
    v_hat = nv / (1.0 - ADAM_B2 ** ADAM_STEP)
    return -ADAM_LR * (m_hat / (jnp.sqrt(v_hat) + ADAM_EPS) + ADAM_WD * w), nm, nv


def adamw(w, g, m, v, name):
    R, Cc = w.shape
    tr = _pick(R, (256, 128, 64, 8))

    def body(w_ref, g_ref, m_ref, v_ref, d_ref, nm_ref, nv_ref):
        d_ref[...], nm_ref[...], nv_ref[...] = _adamw_math(w_ref[...], g_ref[...], m_ref[...], v_ref[...])

    blk = pl.BlockSpec((tr, Cc), lambda i: (i, 0))
    out = jax.ShapeDtypeStruct((R, Cc), F32)
    return pl.pallas_call(
        body, name=name, grid=(R // tr,), in_specs=[blk] * 4, out_specs=[blk] * 3, out_shape=[out, out, out],
    )(w, g, m, v)


def adamw_halves(w, mine, theirs, m, v, core, rows, cols, axis, name):
    sr, sc = _shard_shape(rows, cols, axis)
    pr, pc = _piece_shape(rows, cols, axis)
    tr = _row_tile(pr, pc)
    nt = pr // tr

    def body(core_ref, w_ref, a_ref, b_ref, m_ref, v_ref, g_ref, d_ref, nm_ref, nv_ref):
        g = jnp.where(pl.program_id(0) == core_ref[0], a_ref[...], b_ref[...])
        g_ref[...] = g
        d_ref[...], nm_ref[...], nv_ref[...] = _adamw_math(w_ref[...], g, m_ref[...], v_ref[...])

    if axis == 0:
        full = pl.BlockSpec((tr, pc), lambda h, i, core_ref: (i, h))
    else:
        full = pl.BlockSpec((tr, pc), lambda h, i, core_ref: (h * nt + i, 0))
    part = pl.BlockSpec((tr, pc), lambda h, i, core_ref: (i, 0))
    out = jax.ShapeDtypeStruct((sr, sc), F32)
    return pl.pallas_call(
        body, name=name, out_shape=[out, out, out, out],
        grid_spec=pltpu.PrefetchScalarGridSpec(
            num_scalar_prefetch=1, grid=(2, nt), in_specs=[full, part, part, full, full], out_specs=[full] * 4),
    )(core, w, mine, theirs, m, v)


BIG = (
    ("ffn1_w_gate_up", D_MODEL, 2 * D_FF, 1),
    ("ffn1_w_down", D_FF, D_MODEL, 0),
    ("w_in", D_MODEL, IN_COLS, 1),
    ("w_branch_hg", HG_WIDTH, D_MODEL, 1),
    ("w_branch_att", ATT_WIDTH, D_MODEL, 1),
    ("w_out", D_MODEL, D_MODEL, 0),
    ("ffn2_w_gate_up", D_MODEL, 2 * D_FF, 1),
    ("ffn2_w_down", D_FF, D_MODEL, 0),
)
N_BIG = len(BIG)
ANY = pl.BlockSpec(memory_space=pl.ANY)


def _place():
    return lax.axis_index("x"), lax.axis_index("y"), lax.axis_index("c")


def _other_chips(x, y):
    return ((1 - x, y), (x, 1 - y), (1 - x, 1 - y))


MAX_COPY_CHUNKS = 16
CHUNK_ROW_ALIGN = 16


def _row_chunks(view):
    rows = view.shape[0]
    n = next(n for n in range(MAX_COPY_CHUNKS, 0, -1) if rows % (CHUNK_ROW_ALIGN * n) == 0 or n == 1)
    step = rows // n
    return [pl.ds(i * step, step) for i in range(n)]


def _remote(src, dst, send_sem, recv_sem, device):
    return pltpu.make_async_remote_copy(src_ref=src, dst_ref=dst, send_sem=send_sem, recv_sem=recv_sem,
                                        device_id=device, device_id_type=MESH)


def _start_remote(src, dst, send_sem, recv_sem, device):
    for rows in _row_chunks(src):
        _remote(src.at[rows, :], dst.at[rows, :], send_sem, recv_sem, device).start()
    return _remote(src, dst, send_sem, recv_sem, device)


HBM = pl.BlockSpec(memory_space=pltpu.HBM)
SEM = pl.BlockSpec(memory_space=pltpu.SEMAPHORE)
SPLIT_COPY_EFFECT = pltpu.SideEffectType.DATAFLOW_SIDE_EFFECTING
GROUPS = {"ffn1": (0, 1), "mix": (2, 3, 4, 5), "ffn2": (6, 7)}


def _in_hbm(a):
    return pltpu.with_memory_space_constraint(a, pltpu.HBM)


class _SemList:
    def __init__(self, refs):
        self.refs = refs
        self.at = self

    def __getitem__(self, index):
        w, k = index
        return self.refs[3 * w + k]


def _gather_piece(ref, rows, cols, axis, chip, c):
    sr, sc = _shard_shape(rows, cols, axis)
    j = 2 * chip[0] + chip[1]
    if axis == 0:
        return ref.at[pl.ds(j * sr + c * (sr // 2), sr // 2), :]
    return ref.at[pl.ds(c * (sr // 2), sr // 2), pl.ds(pl.multiple_of(j * sc, 128), sc)]


def _start_gather_sends(bufs, ws, send_sems, recv_sems):
    x, y, c = _place()
    for w, (_, r, cc, ax) in enumerate(ws):
        mine = _gather_piece(bufs[w], r, cc, ax, (x, y), c)
        for k, chip in enumerate(_other_chips(x, y)):
            _start_remote(mine, mine, send_sems.at[w, k], recv_sems.at[w, k], (*chip, c))


def _wait_gather_sends(bufs, ws, send_sems, recv_sems):
    x, y, c = _place()
    for w, (_, r, cc, ax) in enumerate(ws):
        for k, chip in enumerate(_other_chips(x, y)):
            got = _gather_piece(bufs[w], r, cc, ax, chip, c)
            _remote(got, got, send_sems.at[w, k], recv_sems.at[w, k], (x, y, c)).wait_recv()
    for w, (_, r, cc, ax) in enumerate(ws):
        mine = _gather_piece(bufs[w], r, cc, ax, (x, y), c)
        for k in range(3):
            _remote(mine, mine, send_sems.at[w, k], recv_sems.at[w, k], (x, y, c)).wait_send()


def _forward_halves(bufs, ws, send_sems, recv_sems):
    x, y, c = _place()
    passed = []
    for w, (_, r, cc, ax) in enumerate(ws):
        for k, chip in enumerate(_other_chips(x, y)):
            got = _gather_piece(bufs[w], r, cc, ax, chip, c)
            passed.append(_start_remote(got, got, send_sems.at[w, k], recv_sems.at[w, k], (x, y, 1 - c)))
    for w, (_, r, cc, ax) in enumerate(ws):
        for k, chip in enumerate(_other_chips(x, y)):
            got = _gather_piece(bufs[w], r, cc, ax, chip, 1 - c)
            _remote(got, got, send_sems.at[w, k], recv_sems.at[w, k], (x, y, c)).wait_recv()
    for cp in passed:
        cp.wait_send()


def gather_start(placed, after, group):
    ws = [BIG[i] for i in GROUPS[group]]
    n = len(ws)

    def body(*refs):
        bufs = refs[:n]
        send_sems, recv_sems = _SemList(refs[n + 1:4 * n + 1]), _SemList(refs[4 * n + 1:7 * n + 1])
        token = refs[-1]
        _start_gather_sends(bufs, ws, send_sems, recv_sems)
        token[...] = jnp.zeros_like(token)

    out = pl.pallas_call(
        body, name=f"gather_start_{group}", in_specs=[HBM] * n + [ANY],
        out_specs=[SEM] * (6 * n) + [HBM] * n + [pl.BlockSpec(memory_space=pltpu.VMEM)],
        out_shape=[pltpu.SemaphoreType.DMA(())] * (6 * n)
        + [pltpu.HBM((r, cc), WEIGHT_COMM_DTYPE) for _, r, cc, _ in ws] + [jax.ShapeDtypeStruct((8, 128), F32)],
        input_output_aliases={w: 6 * n + w for w in range(n)},
        compiler_params=pltpu.CompilerParams(has_side_effects=SPLIT_COPY_EFFECT),
    )(*[_in_hbm(p) for p in placed], after)
    return out[:3 * n], out[3 * n:6 * n], out[6 * n:7 * n], out[-1]


def gather_wait(bufs, send_sems, recv_sems, after, group):
    ws = [BIG[i] for i in GROUPS[group]]
    n = len(ws)

    def body(*refs):
        _wait_gather_sends(refs[:n], ws, _SemList(refs[n:n + 3 * n]), _SemList(refs[n + 3 * n:n + 6 * n]))

    return pl.pallas_call(
        body, name=f"gather_wait_{group}", in_specs=[HBM] * n + [SEM] * (6 * n) + [ANY] * len(after), out_specs=[HBM] * n,
        out_shape=[pltpu.HBM((r, cc), WEIGHT_COMM_DTYPE) for _, r, cc, _ in ws],
        input_output_aliases={w: w for w in range(n)},
        compiler_params=pltpu.CompilerParams(has_side_effects=SPLIT_COPY_EFFECT),
    )(*bufs, *send_sems, *recv_sems, *after)


def gather_forward(bufs, group):
    ws = [BIG[i] for i in GROUPS[group]]
    n = len(ws)

    def body(*refs):
        _forward_halves(refs[n:2 * n], ws, refs[2 * n], refs[2 * n + 1])

    return pl.pallas_call(
        body, name=f"gather_forward_{group}", in_specs=[ANY] * n, out_specs=[ANY] * n,
        out_shape=[jax.ShapeDtypeStruct((r, cc), WEIGHT_COMM_DTYPE) for _, r, cc, _ in ws],
        input_output_aliases={w: w for w in range(n)},
        scratch_shapes=[pltpu.SemaphoreType.DMA((n, 3))] * 2,
    )(*bufs)


def _half(ref, rows, cols, axis, c):
    if axis == 0:
        return ref.at[:, pl.ds(pl.multiple_of(c * (cols // 2), 128), cols // 2)]
    return ref.at[pl.ds(c * (rows // 2), rows // 2), :]


def _piece_of_half(ref, rows, cols, axis, chip):
    j = 2 * chip[0] + chip[1]
    pr, pc = _piece_shape(rows, cols, axis)
    if axis == 0:
        return ref.at[pl.ds(j * pr, pr), :]
    return ref.at[:, pl.ds(pl.multiple_of(j * pc, 128), pc)]


def exchange_halves(grads, group):
    ws = [BIG[i] for i in GROUPS[group]]
    n = len(ws)

    def body(*refs):
        ins, theirs = refs[:n], refs[n:2 * n]
        send_sems, recv_sems = refs[2 * n:]
        x, y, c = _place()
        copies = [_start_remote(_half(ins[w], r, cc, ax, 1 - c), theirs[w], send_sems.at[w], recv_sems.at[w], (x, y, 1 - c))
                  for w, (_, r, cc, ax) in enumerate(ws)]
        for cp in copies:
            cp.wait()

    return pl.pallas_call(
        body, name=f"exchange_halves_{group}", in_specs=[ANY] * n, out_specs=[ANY] * n,
        out_shape=[jax.ShapeDtypeStruct(_half_shape(r, cc, ax), GRAD_COMM_DTYPE) for _, r, cc, ax in ws],
        scratch_shapes=[pltpu.SemaphoreType.DMA((n,)), pltpu.SemaphoreType.DMA((n,))],
    )(*grads)


def _scatter_copies(halves, got, ws, send_sems, recv_sems, start):
    x, y, c = _place()
    copies = []
    for w, (_, r, cc, ax) in enumerate(ws):
        for k, chip in enumerate(_other_chips(x, y)):
            args = (_piece_of_half(halves[w], r, cc, ax, chip), got[w].at[k], send_sems.at[w, k], recv_sems.at[w, k], (*chip, c))
            copies.append(_start_remote(*args) if start else _remote(*args))
    return copies


def scatter_start(halves, group):
    ws = [BIG[i] for i in GROUPS[group]]
    n = len(ws)

    def body(*refs):
        sems = refs[2 * n:8 * n]
        _scatter_copies(refs[:n], refs[n:2 * n], ws, _SemList(sems[:3 * n]), _SemList(sems[3 * n:]), start=True)
        refs[-1][...] = jnp.zeros_like(refs[-1])

    landing = [lax.empty((3,) + _piece_shape(r, cc, ax), GRAD_COMM_DTYPE) for _, r, cc, ax in ws]
    out = pl.pallas_call(
        body, name=f"scatter_start_{group}", in_specs=[HBM] * (2 * n),
        out_specs=[SEM] * (6 * n) + [HBM] * (2 * n) + [pl.BlockSpec(memory_space=pltpu.VMEM)],
        out_shape=[pltpu.SemaphoreType.DMA(())] * (6 * n)
        + [pltpu.HBM(_half_shape(r, cc, ax), GRAD_COMM_DTYPE) for _, r, cc, ax in ws]
        + [pltpu.HBM((3,) + _piece_shape(r, cc, ax), GRAD_COMM_DTYPE) for _, r, cc, ax in ws]
        + [jax.ShapeDtypeStruct((8, 128), F32)],
        input_output_aliases={i: 6 * n + i for i in range(2 * n)},
        compiler_params=pltpu.CompilerParams(has_side_effects=SPLIT_COPY_EFFECT),
    )(*[_in_hbm(h) for h in halves], *[_in_hbm(b) for b in landing])
    return out[:3 * n], out[3 * n:6 * n], out[6 * n:7 * n], out[7 * n:8 * n], out[-1]


def scatter_wait(halves, got, send_sems, recv_sems, after, group):
    ws = [BIG[i] for i in GROUPS[group]]
    n = len(ws)

    def body(*refs):
        sems = refs[2 * n:8 * n]
        for cp in _scatter_copies(refs[:n], refs[n:2 * n], ws, _SemList(sems[:3 * n]), _SemList(sems[3 * n:]), start=False):
            cp.wait_send()
            cp.wait_recv()

    out = pl.pallas_call(
        body, name=f"scatter_wait_{group}", in_specs=[HBM] * (2 * n) + [SEM] * (6 * n) + [ANY] * len(after),
        out_specs=[HBM] * (2 * n),
        out_shape=[pltpu.HBM(_half_shape(r, cc, ax), GRAD_COMM_DTYPE) for _, r, cc, ax in ws]
        + [pltpu.HBM((3,) + _piece_shape(r, cc, ax), GRAD_COMM_DTYPE) for _, r, cc, ax in ws],
        input_output_aliases={i: i for i in range(2 * n)},
        compiler_params=pltpu.CompilerParams(has_side_effects=SPLIT_COPY_EFFECT),
    )(*halves, *got, *send_sems, *recv_sems, *after)
    return out[:n], out[n:]


def exchange_reduced(pieces, group):
    ws = [BIG[i] for i in GROUPS[group]]
    n = len(ws)

    def body(*refs):
        ins, theirs = refs[:n], refs[n:2 * n]
        send_sems, recv_sems = refs[2 * n:]
        x, y, c = _place()
        copies = [_start_remote(ins[w], theirs[w], send_sems.at[w], recv_sems.at[w], (x, y, 1 - c)) for w in range(n)]
        for cp in copies:
            cp.wait()

    return pl.pallas_call(
        body, name=f"exchange_reduced_{group}", in_specs=[ANY] * n, out_specs=[ANY] * n,
        out_shape=[jax.ShapeDtypeStruct(_piece_shape(r, cc, ax), F32) for _, r, cc, ax in ws],
        scratch_shapes=[pltpu.SemaphoreType.DMA((n,)), pltpu.SemaphoreType.DMA((n,))],
    )(*pieces)


N_DEV = 8
SMALL_ROWS = 8


def all_reduce_small(packed, behind):
    def body(x_ref, behind_ref, o_ref, gathered, send_sems, recv_sems):
        x, y, c = _place()
        me = 4 * x + 2 * y + c
        gathered[me] = x_ref[...]
        copies = []
        for k in range(1, N_DEV):
            peer = (x ^ (k >> 2), y ^ ((k >> 1) & 1), c ^ (k & 1))
            cp = pltpu.make_async_remote_copy(
                src_ref=x_ref, dst_ref=gathered.at[me], send_sem=send_sems.at[k - 1], recv_sem=recv_sems.at[k - 1],
                device_id=peer, device_id_type=MESH)
            cp.start()
            copies.append(cp)
        for cp in copies:
            cp.wait()
        acc = gathered[0]
        for k in range(1, N_DEV):
            acc = acc + gathered[k]
        o_ref[...] = acc

    vm = pl.BlockSpec(memory_space=pltpu.VMEM)
    return pl.pallas_call(
        body, name="all_reduce_small", in_specs=[vm, ANY], out_specs=vm,
        out_shape=jax.ShapeDtypeStruct((SMALL_ROWS, D_MODEL), F32),
        scratch_shapes=[pltpu.VMEM((N_DEV, SMALL_ROWS, D_MODEL), F32), pltpu.SemaphoreType.DMA((N_DEV - 1,)),
                        pltpu.SemaphoreType.DMA((N_DEV - 1,))],
    )(packed, behind)


def _swiglu_block_fwd(h, norm_g, w_gu, w_down, tag, behind=()):
    n = rmsnorm_fwd(h, norm_g, f"{tag}_norm", behind=behind)
    gu = matmul(n, w_gu, name=f"{tag}_gate_up")
    s = swiglu_fwd(gu, f"{tag}_swiglu")
    h_out = matmul(s, w_down, res=h, scale=0.5, name=f"{tag}_down")
    return h_out, (n, gu, s)


def _swiglu_block_bwd(h, norm_g, w_gu, w_down, saved, dh_out, tag, behind=()):
    n, gu, s = saved
    df = dh_out.astype(MXU_DTYPE)
    d_down = matmul(s, df, ta=True, scale=0.5, out_dtype=GRAD_COMM_DTYPE, name=f"{tag}_d_w_down")
    ds = matmul(df, w_down, tb=True, scale=0.5, behind=behind, name=f"{tag}_d_s")
    dgu = swiglu_bwd(gu, ds, f"{tag}_swiglu_bwd")
    d_gu = matmul(n, dgu, ta=True, out_dtype=GRAD_COMM_DTYPE, name=f"{tag}_d_w_gate_up")
    dn = matmul(dgu, w_gu, tb=True, name=f"{tag}_d_n")
    dh, dg = rmsnorm_bwd(h, norm_g, dn, dh_out, f"{tag}_norm_bwd")
    return dh, dg, d_gu, d_down


def local_step(x, target, small, exchange):
    big = {}
    token, big_ffn1 = exchange.weights("ffn1", x)
    big.update(big_ffn1)
    h1, saved1 = _swiglu_block_fwd(x, small["ffn1_norm"], big["ffn1_w_gate_up"], big["ffn1_w_down"], "ffn1", token)
    token, big_mix = exchange.weights("mix", h1)
    big.update(big_mix)
    u = rmsnorm_fwd(h1, small["mix_norm"], "mix_norm", behind=token)
    z = matmul(u, big["w_in"], name="w_in")
    p = small["hg_lower_bounds"]
    lb = 1.0 / (1.0 + jnp.exp(p[1:2] - p[0:1]))
    y_hg, o_raw, states = hgrn_fwd(z, lb, small["hg_out_norm"], "hgrn_fwd")
    o_att, l_att = zip(*[att_fwd(z, g, f"att_fwd_{g}") for g in range(N_GROUPS)])
    y_att = att_combine_fwd(o_att, l_att, "att_combine")
    bh = matmul(y_hg, big["w_branch_hg"], name="branch_hg")
    ba = matmul(y_att, big["w_branch_att"], name="branch_att")
    merged = merge_fwd(z, bh, ba, "merge")
    h2 = matmul(merged, big["w_out"], res=h1, name="w_out")
    token, big_ffn2 = exchange.weights("ffn2", h2)
    big.update(big_ffn2)
    h3, saved2 = _swiglu_block_fwd(h2, small["ffn2_norm"], big["ffn2_w_gate_up"], big["ffn2_w_down"], "ffn2", token)
    dh3, d_final, loss = final_norm_loss(h3, small["final_norm"], target, "final_norm_loss")

    gs, gb = {"final_norm": d_final}, {}
    dh2, gs["ffn2_norm"], gb["ffn2_w_gate_up"], gb["ffn2_w_down"] = _swiglu_block_bwd(
        h2, small["ffn2_norm"], big["ffn2_w_gate_up"], big["ffn2_w_down"], saved2, dh3, "ffn2")
    token = exchange.gradients("ffn2", gb, dh2)
    dh2_m = dh2.astype(MXU_DTYPE)
    gb["w_out"] = matmul(merged, dh2_m, ta=True, out_dtype=GRAD_COMM_DTYPE, name="d_w_out")
    dmerged = matmul(dh2_m, big["w_out"], tb=True, behind=token, name="d_merged")
    dbh, dba, dgh, dga = merge_bwd(z, bh, ba, dmerged, "merge_bwd")
    gb["w_branch_hg"] = matmul(y_hg, dbh, ta=True, out_dtype=GRAD_COMM_DTYPE, name="d_w_branch_hg")
    gb["w_branch_att"] = matmul(y_att, dba, ta=True, out_dtype=GRAD_COMM_DTYPE, name="d_w_branch_att")
    dy_hg = matmul(dbh, big["w_branch_hg"], tb=True, name="d_y_hg")
    dy_att = matmul(dba, big["w_branch_att"], tb=True, name="d_y_att")
    dq, dfp, di, dog, d_lb, gs["hg_out_norm"] = hgrn_bwd(z, lb, small["hg_out_norm"], o_raw, states, dy_hg, "hgrn_bwd")
    do_att, corr = att_combine_bwd(o_att, l_att, dy_att, "att_combine_bwd")
    d_att = [part for g in range(N_GROUPS) for part in att_bwd(z, l_att[g], do_att[g], corr[g], g, f"att_bwd_{g}")]
    dz = jnp.concatenate([dq, dfp, di, dog, *d_att, dgh, dga], axis=1)
    gb["w_in"] = matmul(u, dz, ta=True, out_dtype=GRAD_COMM_DTYPE, name="d_w_in")
    du = matmul(dz, big["w_in"], tb=True, name="d_u")
    dh1, gs["mix_norm"] = rmsnorm_bwd(h1, small["mix_norm"], du, dh2, "mix_norm_bwd")
    token = exchange.gradients("mix", gb, dh1)
    dp0 = d_lb * lb * (1.0 - lb)
    gs["hg_lower_bounds"] = jnp.concatenate([dp0, -dp0], axis=0)
    dx, gs["ffn1_norm"], gb["ffn1_w_gate_up"], gb["ffn1_w_down"] = _swiglu_block_bwd(
        x, small["ffn1_norm"], big["ffn1_w_gate_up"], big["ffn1_w_down"], saved1, dh1, "ffn1", token)
    exchange.gradients("ffn1", gb, dx)
    return loss[0, 0], dx, gs


SMALL = ("ffn1_norm", "mix_norm", "hg_lower_bounds", "hg_out_norm", "ffn2_norm", "final_norm")
WEIGHTS = ("ffn1_norm", "ffn1_w_gate_up", "ffn1_w_down", "mix_norm", "w_in", "hg_lower_bounds", "hg_out_norm",
           "w_branch_hg", "w_branch_att", "w_out", "ffn2_norm", "ffn2_w_gate_up", "ffn2_w_down", "final_norm")
SMALL_SHAPE = {"ffn1_norm": (1, 1024), "mix_norm": (1, 1024), "hg_lower_bounds": (2, 512), "hg_out_norm": (1, 512),
               "ffn2_norm": (1, 1024), "final_norm": (1024,)}
LOSS_ROW = 6


def _pack_small(vals):
    rows = []
    for n in SMALL:
        r = vals[n].reshape(1, -1).astype(F32)
        rows.append(jnp.pad(r, ((0, 0), (0, D_MODEL - r.shape[1]))))
    rows.append(jnp.zeros((SMALL_ROWS - len(SMALL), D_MODEL), F32))
    return jnp.concatenate(rows, axis=0)


def _unpack_small(packed):
    out = {}
    for i, n in enumerate(SMALL):
        size = int(np.prod(SMALL_SHAPE[n]))
        out[n] = packed[i, :size].reshape(SMALL_SHAPE[n])
    return out


class WeightExchange:
    ORDER = ("ffn1", "mix", "ffn2")

    def __init__(self, shards, core, chip):
        self.core, self.chip = core, chip
        self.scattering = None
        self.reduced = {}
        first = self.ORDER[0]
        self.placed = {BIG[i][0]: place_own_block(shards[BIG[i][0]], chip, *BIG[i][1:], f"place_{BIG[i][0]}")
                       for i in GROUPS[first]}
        self._start_gather(first, self.placed[self._names(first)[0]])
        chip_behind = chip + self.token[0, :1].astype(jnp.int32)
        for group in self.ORDER[1:]:
            for i in GROUPS[group]:
                n, r, cc, ax = BIG[i]
                self.placed[n] = place_own_block(shards[n], chip_behind, r, cc, ax, f"place_{n}")
        self.placed_behind = [self.placed[n] for group in self.ORDER[1:] for n in self._names(group)]

    def _names(self, group):
        return [BIG[i][0] for i in GROUPS[group]]

    def _start_gather(self, group, after):
        send_sems, recv_sems, bufs, self.token = gather_start([self.placed[n] for n in self._names(group)], after, group)
        self.gathering = (group, send_sems, recv_sems, bufs)

    def weights(self, group, h):
        pending, send_sems, recv_sems, bufs = self.gathering
        assert pending == group
        after = self.placed_behind if group == self.ORDER[0] else [h]
        whole = gather_forward(gather_wait(bufs, send_sems, recv_sems, after, group), group)
        later = self.ORDER.index(group) + 1
        behind = []
        if later < len(self.ORDER):
            self._start_gather(self.ORDER[later], whole[0])
            behind = [self.token]
        return behind, dict(zip(self._names(group), whole))

    def _finish_scatter(self, after):
        group, send_sems, recv_sems, halves, got = self.scattering
        halves, got = scatter_wait(halves, got, send_sems, recv_sems, after, group)
        ws = [BIG[i] for i in GROUPS[group]]
        mine = [add_pieces(h, g, self.chip, r, cc, ax, f"add_pieces_{n}") for (n, r, cc, ax), h, g in zip(ws, halves, got)]
        theirs = exchange_reduced(mine, group)
        self.reduced.update({n: (a, b) for (n, *_), a, b in zip(ws, mine, theirs)})
        self.scattering = None
        return theirs[0]

    def gradients(self, group, grads, dh):
        behind = [self._finish_scatter([dh])] if self.scattering is not None else []
        ws = [BIG[i] for i in GROUPS[group]]
        theirs = exchange_halves([grads[n] for n, *_ in ws], group)
        halves = [add_halves(grads[n], t, self.core, r, cc, ax, f"add_halves_{n}") for (n, r, cc, ax), t in zip(ws, theirs)]
        send_sems, recv_sems, halves, got, self.token = scatter_start(halves, group)
        self.scattering = (group, send_sems, recv_sems, halves, got)
        return behind + [self.token]

    def finish(self, after):
        self._finish_scatter(after)
        return self.reduced


def kernel(x, ffn1_norm, ffn1_w_gate_up, ffn1_w_down, mix_norm, w_in, hg_lower_bounds, hg_out_norm, w_branch_hg, w_branch_att, w_out, ffn2_norm, ffn2_w_gate_up, ffn2_w_down, final_norm, loss_target, m_ffn1_norm, m_ffn1_w_gate_up, m_ffn1_w_down, m_mix_norm, m_w_in, m_hg_lower_bounds, m_hg_out_norm, m_w_branch_hg, m_w_branch_att, m_w_out, m_ffn2_norm, m_ffn2_w_gate_up, m_ffn2_w_down, m_final_norm, v_ffn1_norm, v_ffn1_w_gate_up, v_ffn1_w_down, v_mix_norm, v_w_in, v_hg_lower_bounds, v_hg_out_norm, v_w_branch_hg, v_w_branch_att, v_w_out, v_ffn2_norm, v_ffn2_w_gate_up, v_ffn2_w_down, v_final_norm):
    w = dict(ffn1_norm=ffn1_norm, ffn1_w_gate_up=ffn1_w_gate_up, ffn1_w_down=ffn1_w_down, mix_norm=mix_norm, w_in=w_in,
             hg_lower_bounds=hg_lower_bounds, hg_out_norm=hg_out_norm, w_branch_hg=w_branch_hg, w_branch_att=w_branch_att,
             w_out=w_out, ffn2_norm=ffn2_norm, ffn2_w_gate_up=ffn2_w_gate_up, ffn2_w_down=ffn2_w_down, final_norm=final_norm)
    m = dict(ffn1_norm=m_ffn1_norm, ffn1_w_gate_up=m_ffn1_w_gate_up, ffn1_w_down=m_ffn1_w_down, mix_norm=m_mix_norm,
             w_in=m_w_in, hg_lower_bounds=m_hg_lower_bounds, hg_out_norm=m_hg_out_norm, w_branch_hg=m_w_branch_hg,
             w_branch_att=m_w_branch_att, w_out=m_w_out, ffn2_norm=m_ffn2_norm, ffn2_w_gate_up=m_ffn2_w_gate_up,
             ffn2_w_down=m_ffn2_w_down, final_norm=m_final_norm)
    v = dict(ffn1_norm=v_ffn1_norm, ffn1_w_gate_up=v_ffn1_w_gate_up, ffn1_w_down=v_ffn1_w_down, mix_norm=v_mix_norm,
             w_in=v_w_in, hg_lower_bounds=v_hg_lower_bounds, hg_out_norm=v_hg_out_norm, w_branch_hg=v_w_branch_hg,
             w_branch_att=v_w_branch_att, w_out=v_w_out, ffn2_norm=v_ffn2_norm, ffn2_w_gate_up=v_ffn2_w_gate_up,
             ffn2_w_down=v_ffn2_w_down, final_norm=v_final_norm)

    core = lax.axis_index("c").astype(jnp.int32).reshape(1)
    chip = (2 * lax.axis_index("x") + lax.axis_index("y")).astype(jnp.int32).reshape(1)
    exchange = WeightExchange({n: w[n][0] for n, *_ in BIG}, core, chip)
    small = {n: w[n] for n in SMALL}
    small["final_norm"] = final_norm.reshape(1, D_MODEL)

    loss, dx, gs = local_step(x[0], loss_target[0], small, exchange)

    grads, delta, new_m, new_v = {}, {}, {}, {}

    def update(group, core):
        for i in GROUPS[group]:
            n, r, cc, ax = BIG[i]
            a, b = exchange.reduced[n]
            g, d, nm, nv = adamw_halves(w[n][0], a, b, m[n][0], v[n][0], core, r, cc, ax, f"adamw_{n}")
            grads[n], delta[n], new_m[n], new_v[n] = g[None], d[None], nm[None], nv[None]

    core_behind = core + exchange.token[0, :1].astype(jnp.int32)
    update("ffn2", core_behind)
    update("mix", core_behind)
    exchange.finish(after=[delta[BIG[i][0]] for group in ("ffn2", "mix") for i in GROUPS[group]])
    update("ffn1", core)
    packed = _pack_small(gs)
    packed = packed.at[LOSS_ROW].set(jnp.full((D_MODEL,), loss, F32))
    total = all_reduce_small(packed, behind=delta["ffn1_w_down"])
    grads.update(_unpack_small(total))
    loss_total = total[LOSS_ROW, 0]
    pd, pm, pv = adamw(_pack_small({n: w[n] for n in SMALL}), total.at[LOSS_ROW].set(0.0),
                       _pack_small({n: m[n] for n in SMALL}), _pack_small({n: v[n] for n in SMALL}), "adamw_small")
    delta.update(_unpack_small(pd))
    new_m.update(_unpack_small(pm))
    new_v.update(_unpack_small(pv))

    return (loss_total, dx[None], *[grads[n] for n in WEIGHTS], *[delta[n] for n in WEIGHTS],
            *[new_m[n] for n in WEIGHTS], *[new_v[n] for n in WEIGHTS])
```

```python
import numpy as np
import jax
import jax.numpy as jnp
from jax import lax
from jax.experimental import pallas as pl
from jax.experimental.pallas import tpu as pltpu

SEQ = 2048
D_MODEL = 1024
D_FF = 2816
HG_HEADS = 4
HG_DIM = 128
HG_WIDTH = 512
HG_CHUNK = 64
ATT_GROUPS = ((128, 1), (512, 4), (2048, 16))
ATT_HEADS = 8
ATT_WIDTH = 512
ATT_BLOCK = 128
ALIBI_MAX = 8.0
IN_COLS = 8704
EPS = 1e-6
NEG_INF = -1e30
ADAM_LR = 0.001
ADAM_B1 = 0.9
ADAM_B2 = 0.999
ADAM_EPS = 1e-08
ADAM_WD = 0.01
ADAM_STEP = 10

N_CHIPS = 4
MXU_DTYPE = jnp.bfloat16
HG_DOT_DTYPE = jnp.float32
WEIGHT_COMM_DTYPE = jnp.bfloat16
GRAD_COMM_DTYPE = jnp.bfloat16
MESH = pl.DeviceIdType.MESH
F32 = jnp.float32
HIGHEST = lax.Precision.HIGHEST


def _pick(n, cands):
    for c in cands:
        if n % c == 0:
            return c
    return n


def _sigmoid(x):
    return 1.0 / (1.0 + jnp.exp(-x))


def _dot(a, b, ta=False, tb=False):
    dn = (((0 if ta else 1,), (1 if tb else 0,)), ((), ()))
    return lax.dot_general(a.astype(MXU_DTYPE), b.astype(MXU_DTYPE), dn, preferred_element_type=F32)


def _dot_f32(a, b):
    return jnp.dot(a, b, precision=HIGHEST, preferred_element_type=F32)


def _hdot(a, b, ta=False, tb=False):
    if HG_DOT_DTYPE == F32:
        dn = (((0 if ta else 1,), (1 if tb else 0,)), ((), ()))
        return lax.dot_general(a, b, dn, precision=HIGHEST, preferred_element_type=F32)
    return _dot(a, b, ta, tb)


MATMUL_VMEM_BYTES = 48 * 1024 * 1024
MATMUL_TILE_BYTES = 36 * 1024 * 1024
MXU_ALIGN = 128


def _divisors(n, most):
    return [t for t in range(min(n, most), 0, -MXU_ALIGN) if n % t == 0 and t % MXU_ALIGN == 0]


def _matmul_tiles(M, N, K, in_bytes, out_bytes, has_res):
    best = None
    for tk in _divisors(K, K):
        nk = K // tk
        for tm in _divisors(M, 2048):
            for tn in _divisors(N, 512):
                tiles = 2 * in_bytes * (tm * tk + tk * tn) + 2 * out_bytes * tm * tn
                tiles += 4 * tm * tn * ((nk > 1) + 2 * has_res)
                if tiles > MATMUL_TILE_BYTES:
                    continue
                traffic = in_bytes * (M * K * (1 if nk == 1 else N // tn) + K * N * (M // tm))
                key = (traffic, -tm * tn * tk)
                if best is None or key < best[0]:
                    best = (key, (tm, tn, tk))
    return best[1]


def matmul(a, b, *, ta=False, tb=False, out_dtype=F32, res=None, scale=1.0, behind=(), name):
    if ta:
        K, M = a.shape
    else:
        M, K = a.shape
    if tb:
        N, K2 = b.shape
    else:
        K2, N = b.shape
    assert K == K2 and a.dtype == b.dtype
    tm, tn, tk = _matmul_tiles(M, N, K, a.dtype.itemsize, jnp.dtype(out_dtype).itemsize, res is not None)
    nk = K // tk

    def finish(r, r_ref, o_ref):
        if scale != 1.0:
            r = r * scale
        if res is not None:
            r = r_ref[...] + r
        o_ref[...] = r.astype(out_dtype)

    def body(*refs):
        a_ref, b_ref = refs[:2]
        r_ref = refs[2] if res is not None else None
        o_ref = refs[2 + (res is not None) + len(behind)]
        if nk == 1:
            finish(_dot(a_ref[...], b_ref[...], ta, tb), r_ref, o_ref)
            return
        acc = refs[-1]
        k = pl.program_id(2)

        @pl.when(k == 0)
        def _():
            acc[...] = jnp.zeros_like(acc)

        acc[...] += _dot(a_ref[...], b_ref[...], ta, tb)

        @pl.when(k == nk - 1)
        def _():
            finish(acc[...], r_ref, o_ref)

    a_spec = pl.BlockSpec((tk, tm), lambda i, j, k: (k, i)) if ta else pl.BlockSpec((tm, tk), lambda i, j, k: (i, k))
    b_spec = pl.BlockSpec((tn, tk), lambda i, j, k: (j, k)) if tb else pl.BlockSpec((tk, tn), lambda i, j, k: (k, j))
    in_specs = [a_spec, b_spec]
    args = [a, b]
    if res is not None:
        in_specs.append(pl.BlockSpec((tm, tn), lambda i, j, k: (i, j)))
        args.append(res)
    for earlier in behind:
        in_specs.append(pl.BlockSpec(memory_space=pl.ANY))
        args.append(earlier)
    return pl.pallas_call(
        body, name=name, grid=(M // tm, N // tn, nk), in_specs=in_specs,
        out_specs=pl.BlockSpec((tm, tn), lambda i, j, k: (i, j)),
        out_shape=jax.ShapeDtypeStruct((M, N), out_dtype),
        scratch_shapes=[pltpu.VMEM((tm, tn), F32)] if nk > 1 else [],
        compiler_params=pltpu.CompilerParams(dimension_semantics=("parallel", "parallel", "arbitrary"),
                                             vmem_limit_bytes=MATMUL_VMEM_BYTES),
    )(*args)


ROW_TILE = 256


def rmsnorm_fwd(x, g, name, behind=()):
    def body(x_ref, g_ref, *refs):
        n_ref = refs[-1]
        xv = x_ref[...]
        r = lax.rsqrt(jnp.mean(xv * xv, axis=-1, keepdims=True) + EPS)
        n_ref[...] = ((xv * r) * g_ref[...]).astype(n_ref.dtype)

    order = list(behind)
    return pl.pallas_call(
        body, name=name, grid=(SEQ // ROW_TILE,),
        in_specs=[pl.BlockSpec((ROW_TILE, D_MODEL), lambda i: (i, 0)), pl.BlockSpec((1, D_MODEL), lambda i: (0, 0))]
        + [pl.BlockSpec(memory_space=pl.ANY)] * len(order),
        out_specs=pl.BlockSpec((ROW_TILE, D_MODEL), lambda i: (i, 0)),
        out_shape=jax.ShapeDtypeStruct((SEQ, D_MODEL), MXU_DTYPE),
    )(x, g, *order)


def rmsnorm_bwd(x, g, dn, dres, name):
    def body(x_ref, g_ref, dn_ref, dr_ref, dx_ref, dg_ref):
        xv = x_ref[...]
        r = lax.rsqrt(jnp.mean(xv * xv, axis=-1, keepdims=True) + EPS)
        xh = xv * r
        dnv = dn_ref[...]

        @pl.when(pl.program_id(0) == 0)
        def _():
            dg_ref[...] = jnp.zeros_like(dg_ref)

        dg_ref[...] += jnp.sum(dnv * xh, axis=0, keepdims=True)
        dxh = dnv * g_ref[...]
        dx_ref[...] = dr_ref[...] + r * (dxh - xh * jnp.mean(dxh * xh, axis=-1, keepdims=True))

    row = pl.BlockSpec((ROW_TILE, D_MODEL), lambda i: (i, 0))
    vec = pl.BlockSpec((1, D_MODEL), lambda i: (0, 0))
    return pl.pallas_call(
        body, name=name, grid=(SEQ // ROW_TILE,), in_specs=[row, vec, row, row], out_specs=[row, vec],
        out_shape=[jax.ShapeDtypeStruct((SEQ, D_MODEL), F32), jax.ShapeDtypeStruct((1, D_MODEL), F32)],
        compiler_params=pltpu.CompilerParams(dimension_semantics=("arbitrary",)),
    )(x, g, dn, dres)


def final_norm_loss(h, g, target, name):
    def body(h_ref, g_ref, t_ref, dh_ref, dg_ref, loss_ref):
        xv = h_ref[...]
        r = lax.rsqrt(jnp.mean(xv * xv, axis=-1, keepdims=True) + EPS)
        xh = xv * r
        gv = g_ref[...]
        e = xh * gv - t_ref[...]

        @pl.when(pl.program_id(0) == 0)
        def _():
            dg_ref[...] = jnp.zeros_like(dg_ref)
            loss_ref[...] = jnp.zeros_like(loss_ref)

        part = 0.5 * jnp.sum(jnp.sum(e * e, axis=-1, keepdims=True) * (1.0 / D_MODEL), axis=0, keepdims=True)
        loss_ref[...] += jnp.broadcast_to(part, loss_ref.shape)
        dout = e * (1.0 / D_MODEL)
        dg_ref[...] += jnp.sum(dout * xh, axis=0, keepdims=True)
        dxh = dout * gv
        dh_ref[...] = r * (dxh - xh * jnp.mean(dxh * xh, axis=-1, keepdims=True))

    row = pl.BlockSpec((ROW_TILE, D_MODEL), lambda i: (i, 0))
    vec = pl.BlockSpec((1, D_MODEL), lambda i: (0, 0))
    return pl.pallas_call(
        body, name=name, grid=(SEQ // ROW_TILE,), in_specs=[row, vec, row],
        out_specs=[row, vec, pl.BlockSpec((8, 128), lambda i: (0, 0))],
        out_shape=[jax.ShapeDtypeStruct((SEQ, D_MODEL), F32), jax.ShapeDtypeStruct((1, D_MODEL), F32),
                   jax.ShapeDtypeStruct((8, 128), F32)],
        compiler_params=pltpu.CompilerParams(dimension_semantics=("arbitrary",)),
    )(h, g, target)


FF_TILE = D_FF // 2


def swiglu_fwd(gu, name):
    def body(a_ref, b_ref, s_ref):
        a = a_ref[...]
        s_ref[...] = (a * _sigmoid(a) * b_ref[...]).astype(s_ref.dtype)

    return pl.pallas_call(
        body, name=name, grid=(SEQ // ROW_TILE, 2),
        in_specs=[pl.BlockSpec((ROW_TILE, FF_TILE), lambda i, j: (i, j)),
                  pl.BlockSpec((ROW_TILE, FF_TILE), lambda i, j: (i, j + 2))],
        out_specs=pl.BlockSpec((ROW_TILE, FF_TILE), lambda i, j: (i, j)),
        out_shape=jax.ShapeDtypeStruct((SEQ, D_FF), MXU_DTYPE),
    )(gu, gu)


def swiglu_bwd(gu, ds, name):
    rows = ROW_TILE // 2

    def body(a_ref, b_ref, ds_ref, o_ref):
        a = a_ref[...]
        sg = _sigmoid(a)
        dsv = ds_ref[...]
        o_ref[:, :D_FF] = (dsv * b_ref[...] * (sg * (1.0 + a * (1.0 - sg)))).astype(o_ref.dtype)
        o_ref[:, D_FF:] = (dsv * a * sg).astype(o_ref.dtype)

    return pl.pallas_call(
        body, name=name, grid=(SEQ // rows,),
        in_specs=[pl.BlockSpec((rows, D_FF), lambda i: (i, 0)), pl.BlockSpec((rows, D_FF), lambda i: (i, 1)),
                  pl.BlockSpec((rows, D_FF), lambda i: (i, 0))],
        out_specs=pl.BlockSpec((rows, 2 * D_FF), lambda i: (i, 0)),
        out_shape=jax.ShapeDtypeStruct((SEQ, 2 * D_FF), MXU_DTYPE), compiler_params=SUM_PARAMS,
    )(gu, gu, ds)


GATE_HG_BLK = 6656 // 512
GATE_ATT_BLK = 7680 // 512


def merge_fwd(z, bh, ba, name):
    def body(gh_ref, ga_ref, bh_ref, ba_ref, o_ref):
        o_ref[...] = (_sigmoid(gh_ref[...]) * bh_ref[...] + _sigmoid(ga_ref[...]) * ba_ref[...]).astype(o_ref.dtype)

    blk = pl.BlockSpec((ROW_TILE, 512), lambda i, j: (i, j))
    return pl.pallas_call(
        body, name=name, grid=(SEQ // ROW_TILE, 2),
        in_specs=[pl.BlockSpec((ROW_TILE, 512), lambda i, j: (i, GATE_HG_BLK + j)),
                  pl.BlockSpec((ROW_TILE, 512), lambda i, j: (i, GATE_ATT_BLK + j)), blk, blk],
        out_specs=blk, out_shape=jax.ShapeDtypeStruct((SEQ, D_MODEL), MXU_DTYPE),
    )(z, z, bh, ba)


def merge_bwd(z, bh, ba, dm, name):
    def body(gh_ref, ga_ref, bh_ref, ba_ref, dm_ref, dbh_ref, dba_ref, dgh_ref, dga_ref):
        dmv = dm_ref[...]
        sh = _sigmoid(gh_ref[...])
        sa = _sigmoid(ga_ref[...])
        dbh_ref[...] = (dmv * sh).astype(dbh_ref.dtype)
        dba_ref[...] = (dmv * sa).astype(dba_ref.dtype)
        dgh_ref[...] = (dmv * bh_ref[...] * (sh * (1.0 - sh))).astype(dgh_ref.dtype)
        dga_ref[...] = (dmv * ba_ref[...] * (sa * (1.0 - sa))).astype(dga_ref.dtype)

    blk = pl.BlockSpec((ROW_TILE, 512), lambda i, j: (i, j))
    out = jax.ShapeDtypeStruct((SEQ, D_MODEL), MXU_DTYPE)
    return pl.pallas_call(
        body, name=name, grid=(SEQ // ROW_TILE, 2),
        in_specs=[pl.BlockSpec((ROW_TILE, 512), lambda i, j: (i, GATE_HG_BLK + j)),
                  pl.BlockSpec((ROW_TILE, 512), lambda i, j: (i, GATE_ATT_BLK + j)), blk, blk, blk],
        out_specs=[blk, blk, blk, blk], out_shape=[out, out, out, out],
    )(z, z, bh, ba, dm)


N_CHUNKS = SEQ // HG_CHUNK


def _hgrn_gates(q, fp, lb):
    C = HG_CHUNK
    sg = _sigmoid(fp)
    f = lb + (1.0 - lb) * sg
    lf = jnp.log(f)
    row = lax.broadcasted_iota(jnp.int32, (C, C), 0)
    col = lax.broadcasted_iota(jnp.int32, (C, C), 1)
    causal = row >= col
    G = _dot_f32(causal.astype(F32), lf)
    eG = jnp.exp(G)
    enG = jnp.exp(-G)
    qg = q * eG
    kg = (1.0 - f) * enG
    A = jnp.where(causal, _hdot(qg, kg, tb=True), 0.0)
    egl = jnp.exp(jnp.sum(lf, axis=0, keepdims=True))
    return sg, f, causal, eG, enG, qg, kg, A, egl


def hgrn_fwd(z, lb, gain, name):
    C, K = HG_CHUNK, HG_DIM

    def body(q_ref, f_ref, v_ref, og_ref, p_ref, g_ref, y_ref, o_ref, st_ref, state):
        @pl.when(pl.program_id(0) == 0)
        def _():
            state[...] = jnp.zeros_like(state)

        for h in range(HG_HEADS):
            hd = pl.ds(h * K, K)
            v = v_ref[:, hd]
            _, _, _, _, _, qg, kg, A, egl = _hgrn_gates(q_ref[:, hd], f_ref[:, hd], p_ref[:, hd])
            st = state[h]
            st_ref[h, 0] = st
            o = _hdot(A, v) + _hdot(qg, st, tb=True)
            state[h] = st * egl + _hdot(v, kg * egl, ta=True)
            o_ref[:, hd] = o
            rs = lax.rsqrt(jnp.mean(o * o, axis=-1, keepdims=True) + EPS)
            og = og_ref[:, hd]
            y_ref[:, hd] = (((o * rs) * g_ref[:, hd]) * (og * _sigmoid(og))).astype(y_ref.dtype)

    def zcol(section):
        return pl.BlockSpec((C, HG_WIDTH), lambda c: (c, section))

    vec = pl.BlockSpec((1, HG_WIDTH), lambda c: (0, 0))
    blk = pl.BlockSpec((C, HG_WIDTH), lambda c: (c, 0))
    return pl.pallas_call(
        body, name=name, grid=(N_CHUNKS,),
        in_specs=[zcol(0), zcol(1), zcol(2), zcol(3), vec, vec],
        out_specs=[blk, blk, pl.BlockSpec((HG_HEADS, 1, K, K), lambda c: (0, c, 0, 0))],
        out_shape=[jax.ShapeDtypeStruct((SEQ, HG_WIDTH), MXU_DTYPE), jax.ShapeDtypeStruct((SEQ, HG_WIDTH), F32),
                   jax.ShapeDtypeStruct((HG_HEADS, N_CHUNKS, K, K), F32)],
        scratch_shapes=[pltpu.VMEM((HG_HEADS, K, K), F32)],
        compiler_params=pltpu.CompilerParams(dimension_semantics=("arbitrary",)),
    )(z, z, z, z, lb, gain)


def hgrn_bwd(z, lb, gain, o_raw, states, dy, name):
    C, K = HG_CHUNK, HG_DIM

    def body(q_ref, f_ref, v_ref, og_ref, p_ref, g_ref, o_ref, st_ref, dy_ref,
             dq_ref, dfp_ref, dv_ref, dog_ref, dlb_ref, dgain_ref, dstate):
        @pl.when(pl.program_id(0) == 0)
        def _():
            dstate[...] = jnp.zeros_like(dstate)
            dlb_ref[...] = jnp.zeros_like(dlb_ref)
            dgain_ref[...] = jnp.zeros_like(dgain_ref)

        last = lax.broadcasted_iota(jnp.int32, (C, K), 0) == C - 1
        row = lax.broadcasted_iota(jnp.int32, (C, C), 0)
        col = lax.broadcasted_iota(jnp.int32, (C, C), 1)
        anti_causal = (col >= row).astype(F32)
        for h in range(HG_HEADS):
            hd = pl.ds(h * K, K)
            v = v_ref[:, hd]
            lb = p_ref[:, hd]
            sg, f, causal, eG, enG, qg, kg, A, egl = _hgrn_gates(q_ref[:, hd], f_ref[:, hd], lb)
            kd = kg * egl
            st = st_ref[h, 0]
            dst = dstate[h]
            o = o_ref[:, hd]
            og = og_ref[:, hd]
            gain_v = g_ref[:, hd]
            dyv = dy_ref[:, hd]
            rs = lax.rsqrt(jnp.mean(o * o, axis=-1, keepdims=True) + EPS)
            on = o * rs
            sgo = _sigmoid(og)
            silu = og * sgo
            dog_ref[:, hd] = (dyv * (on * gain_v) * (sgo * (1.0 + og * (1.0 - sgo)))).astype(dog_ref.dtype)
            dgain_ref[:, hd] += jnp.sum(dyv * silu * on, axis=0, keepdims=True)
            don = dyv * gain_v * silu
            do = rs * (don - on * jnp.mean(don * on, axis=-1, keepdims=True))
            dA = jnp.where(causal, _hdot(do, v, tb=True), 0.0)
            dv_ref[:, hd] = (_hdot(A, do, ta=True) + _hdot(kd, dst, tb=True)).astype(dv_ref.dtype)
            dqg = _hdot(dA, kg) + _hdot(do, st)
            dkg = _hdot(dA, qg, ta=True)
            dkd = _hdot(v, dst)
            dstate[h] = dst * egl + _hdot(do, qg, ta=True)
            dgl = jnp.sum(st * dst, axis=0, keepdims=True) * egl
            dq_ref[:, hd] = (dqg * eG).astype(dq_ref.dtype)
            dk = dkg * enG + dkd * (enG * egl)
            dG = dqg * qg - dkg * kg - dkd * kd
            extra = jnp.sum(dkd * kd, axis=0, keepdims=True) + dgl
            dG = dG + jnp.where(last, extra, 0.0)
            dlf = _dot_f32(anti_causal, dG)
            df = dlf / f - dk
            dfp_ref[:, hd] = (df * (1.0 - lb) * (sg * (1.0 - sg))).astype(dfp_ref.dtype)
            dlb_ref[:, hd] += jnp.sum(df * (1.0 - sg), axis=0, keepdims=True)

    def rc(c):
        return N_CHUNKS - 1 - c

    def zcol(section):
        return pl.BlockSpec((C, HG_WIDTH), lambda c: (rc(c), section))

    vec = pl.BlockSpec((1, HG_WIDTH), lambda c: (0, 0))
    blk = pl.BlockSpec((C, HG_WIDTH), lambda c: (rc(c), 0))
    out = jax.ShapeDtypeStruct((SEQ, HG_WIDTH), MXU_DTYPE)
    small = jax.ShapeDtypeStruct((1, HG_WIDTH), F32)
    return pl.pallas_call(
        body, name=name, grid=(N_CHUNKS,),
        in_specs=[zcol(0), zcol(1), zcol(2), zcol(3), vec, vec, blk,
                  pl.BlockSpec((HG_HEADS, 1, K, K), lambda c: (0, rc(c), 0, 0)), blk],
        out_specs=[blk, blk, blk, blk, vec, vec],
        out_shape=[out, out, out, out, small, small],
        scratch_shapes=[pltpu.VMEM((HG_HEADS, K, K), F32)],
        compiler_params=pltpu.CompilerParams(dimension_semantics=("arbitrary",)),
    )(z, z, z, z, lb, gain, o_raw, states, dy)


N_GROUPS = len(ATT_GROUPS)
HEAD_PAIRS = ATT_WIDTH // 128
ATT_COL0 = 4 * HG_WIDTH
UNROLLED_SUBSEQS = 4


def _alibi_coef():
    n = N_GROUPS * ATT_HEADS
    slopes = np.exp2(-ALIBI_MAX * np.arange(1, n + 1, dtype=np.float32) / n).astype(np.float32)
    dil = np.repeat(np.array([d for _, d in ATT_GROUPS], np.float32), ATT_HEADS)
    return jnp.asarray(slopes * dil, F32)


def _subseq_rows(r, d):
    return pl.ds(r, ATT_BLOCK, stride=d) if d > 1 else pl.ds(0, ATT_BLOCK)


def _for_each_subseq(d, fn):
    if d <= UNROLLED_SUBSEQS:
        for r in range(d):
            fn(r)
    else:
        lax.fori_loop(0, d, lambda r, carry: (fn(r), carry)[1], 0)


def _att_specs(g):
    d = ATT_GROUPS[g][1]
    R = ATT_BLOCK * d
    n_slabs = SEQ // R
    col0 = (ATT_COL0 + g * 3 * ATT_WIDTH) // 128

    def cur(col):
        return pl.BlockSpec((R, 128), lambda hp, s: (s, col + hp))

    def prev(col):
        return pl.BlockSpec((R, 128), lambda hp, s: (jnp.maximum(s - 1, 0), col + hp))

    def nxt(col):
        return pl.BlockSpec((R, 128), lambda hp, s: (jnp.minimum(s + 1, n_slabs - 1), col + hp))

    return d, R, n_slabs, col0, cur, prev, nxt


def _head_lanes(j):
    lane = lax.broadcasted_iota(jnp.int32, (ATT_BLOCK, 128), 1)
    return (lane >= 64 * j) & (lane < 64 * (j + 1))


def _lane_value(x, sel):
    return jnp.max(jnp.where(sel, x, -3e38), axis=-1, keepdims=True)


def _stack_heads(x, sel0):
    return jnp.concatenate([jnp.where(sel0, x, 0.0), jnp.where(sel0, 0.0, x)], axis=0)


def _stack_values(x, sel0, lanes):
    swapped = pltpu.roll(x, 64, 1)
    stacked = jnp.concatenate([jnp.where(sel0, x, swapped), jnp.where(sel0, swapped, x)], axis=0)
    return stacked if lanes == 128 else jnp.concatenate([stacked] * (lanes // 128), axis=1)


def _pair_coef(coef_ref, g, hp):
    row = lax.broadcasted_iota(jnp.int32, (2 * ATT_BLOCK, 1), 0)
    first = g * ATT_HEADS + hp * 2
    return jnp.where(row < ATT_BLOCK, coef_ref[first], coef_ref[first + 1])


def _band(with_prev, first_key):
    B = ATT_BLOCK
    keys = 2 * B if with_prev else B
    qi = jnp.bitwise_and(lax.broadcasted_iota(jnp.int32, (2 * B, keys), 0), B - 1)
    kj = lax.broadcasted_iota(jnp.int32, (2 * B, keys), 1)
    delta = qi + (B if with_prev else 0) - kj
    valid = (delta >= 0) & (delta <= B)
    if with_prev:
        valid = valid & (kj >= first_key)
    return valid, delta.astype(F32)


def _band_next(last_slab):
    B = ATT_BLOCK
    qi = jnp.bitwise_and(lax.broadcasted_iota(jnp.int32, (2 * B, B), 0), B - 1)
    kj = lax.broadcasted_iota(jnp.int32, (2 * B, B), 1)
    delta = qi + B - kj
    return (delta <= B) & (kj >= jnp.where(last_slab, B, 0)), delta.astype(F32)


def att_fwd(z, g, name):
    B = ATT_BLOCK
    d, R, n_slabs, col0, cur, prev, _ = _att_specs(g)
    has_prev = n_slabs > 1

    def body(coef_ref, *refs):
        if has_prev:
            q_ref, kc_ref, vc_ref, kp_ref, vp_ref, o_ref, l_ref = refs
        else:
            q_ref, kc_ref, vc_ref, o_ref, l_ref = refs
        hp, s = pl.program_id(0), pl.program_id(1)
        valid, dist = _band(has_prev, jnp.where(s == 0, B, 0))
        cf2 = _pair_coef(coef_ref, g, hp)
        sel0 = _head_lanes(0)

        def one(r):
            rows = _subseq_rows(r, d)
            q2 = _stack_heads(q_ref[rows, :], sel0)
            kk, vv = kc_ref[rows, :], vc_ref[rows, :]
            if has_prev:
                kk = jnp.concatenate([kp_ref[rows, :], kk], axis=0)
                vv = jnp.concatenate([vp_ref[rows, :], vv], axis=0)
            sc = jnp.where(valid, _dot(q2, kk, tb=True) * 0.125 - cf2 * dist, NEG_INF)
            mx = jnp.max(sc, axis=-1, keepdims=True)
            e = jnp.exp(sc - mx)
            den = jnp.sum(e, axis=-1, keepdims=True)
            o2 = _dot(e * (1.0 / den), vv)
            lse2 = mx + jnp.log(den)
            o_ref[rows, :] = jnp.where(sel0, o2[:B], o2[B:])
            l_ref[rows, :] = jnp.where(sel0, lse2[:B], lse2[B:])

        _for_each_subseq(d, one)

    in_specs = [pl.BlockSpec(memory_space=pltpu.SMEM), cur(col0), cur(col0 + 4), cur(col0 + 8)]
    args = [_alibi_coef(), z, z, z]
    if has_prev:
        in_specs += [prev(col0 + 4), prev(col0 + 8)]
        args += [z, z]
    out = jax.ShapeDtypeStruct((SEQ, ATT_WIDTH), F32)
    return pl.pallas_call(
        body, name=name, grid=(HEAD_PAIRS, n_slabs), in_specs=in_specs,
        out_specs=[cur(0), cur(0)], out_shape=[out, out],
        compiler_params=pltpu.CompilerParams(dimension_semantics=("parallel", "arbitrary")),
    )(*args)


def att_bwd(z, l, do, corr, g, name):
    B = ATT_BLOCK
    d, R, n_slabs, col0, cur, prev, nxt = _att_specs(g)
    neighbours = n_slabs > 1

    def body(coef_ref, *refs):
        if neighbours:
            (q_ref, kc_ref, vc_ref, l_ref, do_ref, cr_ref, kp_ref, vp_ref, qn_ref, ln_ref, don_ref, crn_ref,
             dq_ref, dk_ref, dv_ref, dq_sc, dk_sc, dv_sc) = refs
        else:
            q_ref, kc_ref, vc_ref, l_ref, do_ref, cr_ref, dq_ref, dk_ref, dv_ref, dq_sc, dk_sc, dv_sc = refs
        hp, s = pl.program_id(0), pl.program_id(1)
        valid, dist = _band(neighbours, jnp.where(s == 0, B, 0))
        if neighbours:
            valid_n, dist_n = _band_next(s == n_slabs - 1)
        cf2 = _pair_coef(coef_ref, g, hp)
        sel0 = _head_lanes(0)
        own = slice(B, 2 * B) if neighbours else slice(0, B)

        def one(r):
            rows = _subseq_rows(r, d)
            kc, vc = kc_ref[rows, :], vc_ref[rows, :]
            kk, vv = kc, vc
            if neighbours:
                kk = jnp.concatenate([kp_ref[rows, :], kc], axis=0)
                vv = jnp.concatenate([vp_ref[rows, :], vc], axis=0)
            q2, do2 = _stack_heads(q_ref[rows, :], sel0), _stack_heads(do_ref[rows, :], sel0)
            keys = kk.shape[0]
            lse2, cr2 = _stack_values(l_ref[rows, :], sel0, keys), _stack_values(cr_ref[rows, :], sel0, keys)
            p = jnp.exp(jnp.where(valid, _dot(q2, kk, tb=True) * 0.125 - cf2 * dist, NEG_INF) - lse2)
            ds = p * (_dot(do2, vv, tb=True) + cr2)
            dq2 = _dot(ds, kk)
            dk = _dot(ds, q2, ta=True)[own]
            dv = _dot(p, do2, ta=True)[own]
            if neighbours:
                qn2, don2 = _stack_heads(qn_ref[rows, :], sel0), _stack_heads(don_ref[rows, :], sel0)
                lse_n2, cr_n2 = _stack_values(ln_ref[rows, :], sel0, B), _stack_values(crn_ref[rows, :], sel0, B)
                p_n = jnp.exp(jnp.where(valid_n, _dot(qn2, kc, tb=True) * 0.125 - cf2 * dist_n, NEG_INF) - lse_n2)
                ds_n = p_n * (_dot(don2, vc, tb=True) + cr_n2)
                dk = dk + _dot(ds_n, qn2, ta=True)
                dv = dv + _dot(p_n, don2, ta=True)
            dq_sc[rows, :] = jnp.where(sel0, dq2[:B], dq2[B:]) * 0.125
            dk_sc[rows, :] = dk * 0.125
            dv_sc[rows, :] = dv

        _for_each_subseq(d, one)
        dq_ref[...] = dq_sc[...].astype(dq_ref.dtype)
        dk_ref[...] = dk_sc[...].astype(dk_ref.dtype)
        dv_ref[...] = dv_sc[...].astype(dv_ref.dtype)

    in_specs = [pl.BlockSpec(memory_space=pltpu.SMEM), cur(col0), cur(col0 + 4), cur(col0 + 8), cur(0), cur(0), cur(0)]
    args = [_alibi_coef(), z, z, z, l, do, corr]
    if neighbours:
        in_specs += [prev(col0 + 4), prev(col0 + 8), nxt(col0), nxt(0), nxt(0), nxt(0)]
        args += [z, z, z, l, do, corr]
    out = jax.ShapeDtypeStruct((SEQ, ATT_WIDTH), MXU_DTYPE)
    return pl.pallas_call(
        body, name=name, grid=(HEAD_PAIRS, n_slabs), in_specs=in_specs,
        out_specs=[cur(0)] * 3, out_shape=[out] * 3,
        scratch_shapes=[pltpu.VMEM((R, 128), F32)] * 3,
        compiler_params=pltpu.CompilerParams(dimension_semantics=("parallel", "arbitrary"),
                                             vmem_limit_bytes=MATMUL_VMEM_BYTES),
    )(*args)


def _head_sum(x):
    i = lax.broadcasted_iota(jnp.int32, (128, 128), 0) // 64
    j = lax.broadcasted_iota(jnp.int32, (128, 128), 1) // 64
    return _dot_f32(x, (i == j).astype(F32))


def _group_weights(l0, l1, l2):
    mx = jnp.maximum(jnp.maximum(l0, l1), l2)
    e0, e1, e2 = jnp.exp(l0 - mx), jnp.exp(l1 - mx), jnp.exp(l2 - mx)
    inv = 1.0 / (e0 + e1 + e2)
    return e0 * inv, e1 * inv, e2 * inv


def att_combine_fwd(o, l, name):
    def body(o0, o1, o2, l0, l1, l2, y_ref):
        w0, w1, w2 = _group_weights(l0[...], l1[...], l2[...])
        y_ref[...] = (o0[...] * w0 + o1[...] * w1 + o2[...] * w2).astype(y_ref.dtype)

    blk = pl.BlockSpec((ROW_TILE, ATT_WIDTH), lambda i: (i, 0))
    return pl.pallas_call(
        body, name=name, grid=(SEQ // ROW_TILE,), in_specs=[blk] * 6, out_specs=blk,
        out_shape=jax.ShapeDtypeStruct((SEQ, ATT_WIDTH), MXU_DTYPE),
    )(*o, *l)


def att_combine_bwd(o, l, dy, name):
    def body(o0, o1, o2, l0, l1, l2, dy_ref, do0, do1, do2, cr0, cr1, cr2):
        w = _group_weights(l0[...], l1[...], l2[...])
        dyv = dy_ref[...]
        dw = [_head_sum(dyv * o_ref[...]) for o_ref in (o0, o1, o2)]
        tot = w[0] * dw[0] + w[1] * dw[1] + w[2] * dw[2]
        for g, (do_ref, cr_ref) in enumerate(((do0, cr0), (do1, cr1), (do2, cr2))):
            do_ref[...] = dyv * w[g]
            cr_ref[...] = -w[g] * tot

    blk = pl.BlockSpec((ROW_TILE, 128), lambda i, j: (i, j))
    out = jax.ShapeDtypeStruct((SEQ, ATT_WIDTH), F32)
    res = pl.pallas_call(
        body, name=name, grid=(SEQ // ROW_TILE, HEAD_PAIRS), in_specs=[blk] * 7, out_specs=[blk] * 6, out_shape=[out] * 6,
    )(*o, *l, dy)
    return res[:N_GROUPS], res[N_GROUPS:]


SUM_ROW_TILES = (1024, 512, 256, 128, 64, 32, 16)
SUM_TILE_BYTES = 24 * 1024 * 1024
SUM_PARAMS = pltpu.CompilerParams(vmem_limit_bytes=MATMUL_VMEM_BYTES)


def _row_tile(rows, cols, operands):
    fit = [t for t in SUM_ROW_TILES if rows % t == 0]
    return next((t for t in fit if 2 * 4 * operands * t * cols <= SUM_TILE_BYTES), fit[-1])


def _shard_shape(rows, cols, axis):
    return (rows // N_CHIPS, cols) if axis == 0 else (rows, cols // N_CHIPS)


def _half_shape(rows, cols, axis):
    return (rows, cols // 2) if axis == 0 else (rows // 2, cols)


def _piece_shape(rows, cols, axis):
    return (rows // N_CHIPS, cols // 2) if axis == 0 else (rows // 2, cols // N_CHIPS)


def place_own_block(shard, chip, rows, cols, axis, name):
    sr, sc = _shard_shape(rows, cols, axis)
    tr = _row_tile(sr, sc, 2)

    def body(chip_ref, s_ref, o_ref):
        o_ref[...] = s_ref[...].astype(o_ref.dtype)

    if axis == 0:
        out_map = lambda i, chip_ref: (chip_ref[0] * (sr // tr) + i, 0)
    else:
        out_map = lambda i, chip_ref: (i, chip_ref[0])
    return pl.pallas_call(
        body, name=name, out_shape=jax.ShapeDtypeStruct((rows, cols), WEIGHT_COMM_DTYPE), compiler_params=SUM_PARAMS,
        grid_spec=pltpu.PrefetchScalarGridSpec(
            num_scalar_prefetch=1, grid=(sr // tr,), in_specs=[pl.BlockSpec((tr, sc), lambda i, chip_ref: (i, 0))],
            out_specs=pl.BlockSpec((tr, sc), out_map)),
    )(chip, shard)


def add_halves(g, theirs, core, rows, cols, axis, name):
    hr, hc = _half_shape(rows, cols, axis)
    tr = _row_tile(hr, hc, 3)

    def body(core_ref, g_ref, t_ref, o_ref):
        o_ref[...] = (g_ref[...].astype(F32) + t_ref[...].astype(F32)).astype(o_ref.dtype)

    if axis == 0:
        g_map = lambda i, core_ref: (i, core_ref[0])
    else:
        g_map = lambda i, core_ref: (core_ref[0] * (hr // tr) + i, 0)
    blk = pl.BlockSpec((tr, hc), lambda i, core_ref: (i, 0))
    return pl.pallas_call(
        body, name=name, out_shape=jax.ShapeDtypeStruct((hr, hc), GRAD_COMM_DTYPE), compiler_params=SUM_PARAMS,
        grid_spec=pltpu.PrefetchScalarGridSpec(
            num_scalar_prefetch=1, grid=(hr // tr,), in_specs=[pl.BlockSpec((tr, hc), g_map), blk], out_specs=blk),
    )(core, g, theirs)


def add_pieces(half, got, chip, rows, cols, axis, name):
    hr, _ = _half_shape(rows, cols, axis)
    pr, pc = _piece_shape(rows, cols, axis)
    tr = _row_tile(pr, pc, 5)

    def body(chip_ref, h_ref, got_ref, o_ref):
        o_ref[...] = (h_ref[...].astype(F32) + got_ref[0].astype(F32) + got_ref[1].astype(F32) + got_ref[2].astype(F32))

    if axis == 0:
        h_map = lambda i, chip_ref: (chip_ref[0] * (pr // tr) + i, 0)
    else:
        h_map = lambda i, chip_ref: (i, chip_ref[0])
    return pl.pallas_call(
        body, name=name, out_shape=jax.ShapeDtypeStruct((pr, pc), F32), compiler_params=SUM_PARAMS,
        grid_spec=pltpu.PrefetchScalarGridSpec(
            num_scalar_prefetch=1, grid=(pr // tr,),
            in_specs=[pl.BlockSpec((tr, pc), h_map), pl.BlockSpec((3, tr, pc), lambda i, chip_ref: (0, i, 0))],
            out_specs=pl.BlockSpec((tr, pc), lambda i, chip_ref: (i, 0))),
    )(chip, half, got)


def _adamw_math(w, g, m, v):
    nm = ADAM_B1 * m + (1.0 - ADAM_B1) * g
    nv = ADAM_B2 * v + (1.0 - ADAM_B2) * (g * g)
    m_hat = nm / (1.0 - ADAM_B1 ** ADAM_STEP)
    v_hat = nv / (1.0 - ADAM_B2 ** ADAM_STEP)
    return -ADAM_LR * (m_hat / (jnp.sqrt(v_hat) + ADAM_EPS) + ADAM_WD * w), nm, nv


def adamw(w, g, m, v, name):
    R, Cc = w.shape
    tr = _pick(R, (256, 128, 64, 8))

    def body(w_ref, g_ref, m_ref, v_ref, d_ref, nm_ref, nv_ref):
        d_ref[...], nm_ref[...], nv_ref[...] = _adamw_math(w_ref[...], g_ref[...], m_ref[...], v_ref[...])

    blk = pl.BlockSpec((tr, Cc), lambda i: (i, 0))
    out = jax.ShapeDtypeStruct((R, Cc), F32)
    return pl.pallas_call(
        body, name=name, grid=(R // tr,), in_specs=[blk] * 4, out_specs=[blk] * 3, out_shape=[out, out, out],
    )(w, g, m, v)


def adamw_halves(w, mine, theirs, m, v, core, rows, cols, axis, name):
    sr, sc = _shard_shape(rows, cols, axis)
    pr, pc = _piece_shape(rows, cols, axis)
    tr = _row_tile(pr, pc, 9)
    nt = pr // tr

    def body(core_ref, w_ref, a_ref, b_ref, m_ref, v_ref, g_ref, d_ref, nm_ref, nv_ref):
        g = jnp.where(pl.program_id(0) == core_ref[0], a_ref[...], b_ref[...])
        g_ref[...] = g
        d_ref[...], nm_ref[...], nv_ref[...] = _adamw_math(w_ref[...], g, m_ref[...], v_ref[...])

    if axis == 0:
        full = pl.BlockSpec((tr, pc), lambda h, i, core_ref: (i, h))
    else:
        full = pl.BlockSpec((tr, pc), lambda h, i, core_ref: (h * nt + i, 0))
    part = pl.BlockSpec((tr, pc), lambda h, i, core_ref: (i, 0))
    out = jax.ShapeDtypeStruct((sr, sc), F32)
    return pl.pallas_call(
        body, name=name, out_shape=[out, out, out, out], compiler_params=SUM_PARAMS,
        grid_spec=pltpu.PrefetchScalarGridSpec(
            num_scalar_prefetch=1, grid=(2, nt), in_specs=[full, part, part, full, full], out_specs=[full] * 4),
    )(core, w, mine, theirs, m, v)


BIG = (
    ("ffn1_w_gate_up", D_MODEL, 2 * D_FF, 1),
    ("ffn1_w_down", D_FF, D_MODEL, 0),
    ("w_in", D_MODEL, IN_COLS, 1),
    ("w_branch_hg", HG_WIDTH, D_MODEL, 1),
    ("w_branch_att", ATT_WIDTH, D_MODEL, 1),
    ("w_out", D_MODEL, D_MODEL, 0),
    ("ffn2_w_gate_up", D_MODEL, 2 * D_FF, 1),
    ("ffn2_w_down", D_FF, D_MODEL, 0),
)
N_BIG = len(BIG)
ANY = pl.BlockSpec(memory_space=pl.ANY)


def _place():
    return lax.axis_index("x"), lax.axis_index("y"), lax.axis_index("c")


def _other_chips(x, y):
    return ((1 - x, y), (x, 1 - y), (1 - x, 1 - y))


MAX_COPY_CHUNKS = 16
CHUNK_ROW_ALIGN = 16


def _row_chunks(view):
    rows = view.shape[0]
    n = next(n for n in range(MAX_COPY_CHUNKS, 0, -1) if rows % (CHUNK_ROW_ALIGN * n) == 0 or n == 1)
    step = rows // n
    return [pl.ds(i * step, step) for i in range(n)]


def _remote(src, dst, send_sem, recv_sem, device):
    return pltpu.make_async_remote_copy(src_ref=src, dst_ref=dst, send_sem=send_sem, recv_sem=recv_sem,
                                        device_id=device, device_id_type=MESH)


def _start_remote(src, dst, send_sem, recv_sem, device):
    for rows in _row_chunks(src):
        _remote(src.at[rows, :], dst.at[rows, :], send_sem, recv_sem, device).start()
    return _remote(src, dst, send_sem, recv_sem, device)


HBM = pl.BlockSpec(memory_space=pltpu.HBM)
SEM = pl.BlockSpec(memory_space=pltpu.SEMAPHORE)
SPLIT_COPY_EFFECT = pltpu.SideEffectType.DATAFLOW_SIDE_EFFECTING
GROUPS = {"ffn1": (0, 1), "mix": (2, 3, 4, 5), "ffn2": (6, 7)}


def _in_hbm(a):
    return pltpu.with_memory_space_constraint(a, pltpu.HBM)


class _SemList:
    def __init__(self, refs):
        self.refs = refs
        self.at = self

    def __getitem__(self, index):
        w, k = index
        return self.refs[3 * w + k]


def _gather_piece(ref, rows, cols, axis, chip, c):
    sr, sc = _shard_shape(rows, cols, axis)
    j = 2 * chip[0] + chip[1]
    if axis == 0:
        return ref.at[pl.ds(j * sr + c * (sr // 2), sr // 2), :]
    return ref.at[pl.ds(c * (sr // 2), sr // 2), pl.ds(pl.multiple_of(j * sc, 128), sc)]


def _start_gather_sends(bufs, ws, send_sems, recv_sems):
    x, y, c = _place()
    for w, (_, r, cc, ax) in enumerate(ws):
        mine = _gather_piece(bufs[w], r, cc, ax, (x, y), c)
        for k, chip in enumerate(_other_chips(x, y)):
            _start_remote(mine, mine, send_sems.at[w, k], recv_sems.at[w, k], (*chip, c))


def _wait_gather_sends(bufs, ws, send_sems, recv_sems):
    x, y, c = _place()
    for w, (_, r, cc, ax) in enumerate(ws):
        for k, chip in enumerate(_other_chips(x, y)):
            got = _gather_piece(bufs[w], r, cc, ax, chip, c)
            _remote(got, got, send_sems.at[w, k], recv_sems.at[w, k], (x, y, c)).wait_recv()
    for w, (_, r, cc, ax) in enumerate(ws):
        mine = _gather_piece(bufs[w], r, cc, ax, (x, y), c)
        for k in range(3):
            _remote(mine, mine, send_sems.at[w, k], recv_sems.at[w, k], (x, y, c)).wait_send()


def _forward_halves(bufs, ws, send_sems, recv_sems):
    x, y, c = _place()
    passed = []
    for w, (_, r, cc, ax) in enumerate(ws):
        for k, chip in enumerate(_other_chips(x, y)):
            got = _gather_piece(bufs[w], r, cc, ax, chip, c)
            passed.append(_start_remote(got, got, send_sems.at[w, k], recv_sems.at[w, k], (x, y, 1 - c)))
    for w, (_, r, cc, ax) in enumerate(ws):
        for k, chip in enumerate(_other_chips(x, y)):
            got = _gather_piece(bufs[w], r, cc, ax, chip, 1 - c)
            _remote(got, got, send_sems.at[w, k], recv_sems.at[w, k], (x, y, c)).wait_recv()
    for cp in passed:
        cp.wait_send()


def gather_start(placed, after, group):
    ws = [BIG[i] for i in GROUPS[group]]
    n = len(ws)

    def body(*refs):
        bufs = refs[:n]
        send_sems, recv_sems = _SemList(refs[n + 1:4 * n + 1]), _SemList(refs[4 * n + 1:7 * n + 1])
        token = refs[-1]
        _start_gather_sends(bufs, ws, send_sems, recv_sems)
        token[...] = jnp.zeros_like(token)

    out = pl.pallas_call(
        body, name=f"gather_start_{group}", in_specs=[HBM] * n + [ANY],
        out_specs=[SEM] * (6 * n) + [HBM] * n + [pl.BlockSpec(memory_space=pltpu.VMEM)],
        out_shape=[pltpu.SemaphoreType.DMA(())] * (6 * n)
        + [pltpu.HBM((r, cc), WEIGHT_COMM_DTYPE) for _, r, cc, _ in ws] + [jax.ShapeDtypeStruct((8, 128), F32)],
        input_output_aliases={w: 6 * n + w for w in range(n)},
        compiler_params=pltpu.CompilerParams(has_side_effects=SPLIT_COPY_EFFECT),
    )(*[_in_hbm(p) for p in placed], after)
    return out[:3 * n], out[3 * n:6 * n], out[6 * n:7 * n], out[-1]


def gather_wait(bufs, send_sems, recv_sems, after, group):
    ws = [BIG[i] for i in GROUPS[group]]
    n = len(ws)

    def body(*refs):
        _wait_gather_sends(refs[:n], ws, _SemList(refs[n:n + 3 * n]), _SemList(refs[n + 3 * n:n + 6 * n]))

    return pl.pallas_call(
        body, name=f"gather_wait_{group}", in_specs=[HBM] * n + [SEM] * (6 * n) + [ANY] * len(after), out_specs=[HBM] * n,
        out_shape=[pltpu.HBM((r, cc), WEIGHT_COMM_DTYPE) for _, r, cc, _ in ws],
        input_output_aliases={w: w for w in range(n)},
        compiler_params=pltpu.CompilerParams(has_side_effects=SPLIT_COPY_EFFECT),
    )(*bufs, *send_sems, *recv_sems, *after)


def gather_forward(bufs, group):
    ws = [BIG[i] for i in GROUPS[group]]
    n = len(ws)

    def body(*refs):
        _forward_halves(refs[n:2 * n], ws, refs[2 * n], refs[2 * n + 1])

    return pl.pallas_call(
        body, name=f"gather_forward_{group}", in_specs=[ANY] * n, out_specs=[ANY] * n,
        out_shape=[jax.ShapeDtypeStruct((r, cc), WEIGHT_COMM_DTYPE) for _, r, cc, _ in ws],
        input_output_aliases={w: w for w in range(n)},
        scratch_shapes=[pltpu.SemaphoreType.DMA((n, 3))] * 2,
    )(*bufs)


def _half(ref, rows, cols, axis, c):
    if axis == 0:
        return ref.at[:, pl.ds(pl.multiple_of(c * (cols // 2), 128), cols // 2)]
    return ref.at[pl.ds(c * (rows // 2), rows // 2), :]


def _piece_of_half(ref, rows, cols, axis, chip):
    j = 2 * chip[0] + chip[1]
    pr, pc = _piece_shape(rows, cols, axis)
    if axis == 0:
        return ref.at[pl.ds(j * pr, pr), :]
    return ref.at[:, pl.ds(pl.multiple_of(j * pc, 128), pc)]


def exchange_halves(grads, group):
    ws = [BIG[i] for i in GROUPS[group]]
    n = len(ws)

    def body(*refs):
        ins, theirs = refs[:n], refs[n:2 * n]
        send_sems, recv_sems = refs[2 * n:]
        x, y, c = _place()
        copies = [_start_remote(_half(ins[w], r, cc, ax, 1 - c), theirs[w], send_sems.at[w], recv_sems.at[w], (x, y, 1 - c))
                  for w, (_, r, cc, ax) in enumerate(ws)]
        for cp in copies:
            cp.wait()

    return pl.pallas_call(
        body, name=f"exchange_halves_{group}", in_specs=[ANY] * n, out_specs=[ANY] * n,
        out_shape=[jax.ShapeDtypeStruct(_half_shape(r, cc, ax), GRAD_COMM_DTYPE) for _, r, cc, ax in ws],
        scratch_shapes=[pltpu.SemaphoreType.DMA((n,)), pltpu.SemaphoreType.DMA((n,))],
    )(*grads)


def _scatter_copies(halves, got, ws, send_sems, recv_sems, start):
    x, y, c = _place()
    copies = []
    for w, (_, r, cc, ax) in enumerate(ws):
        for k, chip in enumerate(_other_chips(x, y)):
            args = (_piece_of_half(halves[w], r, cc, ax, chip), got[w].at[k], send_sems.at[w, k], recv_sems.at[w, k], (*chip, c))
            copies.append(_start_remote(*args) if start else _remote(*args))
    return copies


def scatter_start(halves, group):
    ws = [BIG[i] for i in GROUPS[group]]
    n = len(ws)

    def body(*refs):
        sems = refs[2 * n:8 * n]
        _scatter_copies(refs[:n], refs[n:2 * n], ws, _SemList(sems[:3 * n]), _SemList(sems[3 * n:]), start=True)
        refs[-1][...] = jnp.zeros_like(refs[-1])

    landing = [lax.empty((3,) + _piece_shape(r, cc, ax), GRAD_COMM_DTYPE) for _, r, cc, ax in ws]
    out = pl.pallas_call(
        body, name=f"scatter_start_{group}", in_specs=[HBM] * (2 * n),
        out_specs=[SEM] * (6 * n) + [HBM] * (2 * n) + [pl.BlockSpec(memory_space=pltpu.VMEM)],
        out_shape=[pltpu.SemaphoreType.DMA(())] * (6 * n)
        + [pltpu.HBM(_half_shape(r, cc, ax), GRAD_COMM_DTYPE) for _, r, cc, ax in ws]
        + [pltpu.HBM((3,) + _piece_shape(r, cc, ax), GRAD_COMM_DTYPE) for _, r, cc, ax in ws]
        + [jax.ShapeDtypeStruct((8, 128), F32)],
        input_output_aliases={i: 6 * n + i for i in range(2 * n)},
        compiler_params=pltpu.CompilerParams(has_side_effects=SPLIT_COPY_EFFECT),
    )(*[_in_hbm(h) for h in halves], *[_in_hbm(b) for b in landing])
    return out[:3 * n], out[3 * n:6 * n], out[6 * n:7 * n], out[7 * n:8 * n], out[-1]


def scatter_wait(halves, got, send_sems, recv_sems, after, group):
    ws = [BIG[i] for i in GROUPS[group]]
    n = len(ws)

    def body(*refs):
        sems = refs[2 * n:8 * n]
        for cp in _scatter_copies(refs[:n], refs[n:2 * n], ws, _SemList(sems[:3 * n]), _SemList(sems[3 * n:]), start=False):
            cp.wait_send()
            cp.wait_recv()

    out = pl.pallas_call(
        body, name=f"scatter_wait_{group}", in_specs=[HBM] * (2 * n) + [SEM] * (6 * n) + [ANY] * len(after),
        out_specs=[HBM] * (2 * n),
        out_shape=[pltpu.HBM(_half_shape(r, cc, ax), GRAD_COMM_DTYPE) for _, r, cc, ax in ws]
        + [pltpu.HBM((3,) + _piece_shape(r, cc, ax), GRAD_COMM_DTYPE) for _, r, cc, ax in ws],
        input_output_aliases={i: i for i in range(2 * n)},
        compiler_params=pltpu.CompilerParams(has_side_effects=SPLIT_COPY_EFFECT),
    )(*halves, *got, *send_sems, *recv_sems, *after)
    return out[:n], out[n:]


def exchange_reduced(pieces, group):
    ws = [BIG[i] for i in GROUPS[group]]
    n = len(ws)

    def body(*refs):
        ins, theirs = refs[:n], refs[n:2 * n]
        send_sems, recv_sems = refs[2 * n:]
        x, y, c = _place()
        copies = [_start_remote(ins[w], theirs[w], send_sems.at[w], recv_sems.at[w], (x, y, 1 - c)) for w in range(n)]
        for cp in copies:
            cp.wait()

    return pl.pallas_call(
        body, name=f"exchange_reduced_{group}", in_specs=[ANY] * n, out_specs=[ANY] * n,
        out_shape=[jax.ShapeDtypeStruct(_piece_shape(r, cc, ax), F32) for _, r, cc, ax in ws],
        scratch_shapes=[pltpu.SemaphoreType.DMA((n,)), pltpu.SemaphoreType.DMA((n,))],
    )(*pieces)


N_DEV = 8
SMALL_ROWS = 8


def all_reduce_small(packed, behind):
    def body(x_ref, behind_ref, o_ref, gathered, send_sems, recv_sems):
        x, y, c = _place()
        me = 4 * x + 2 * y + c
        gathered[me] = x_ref[...]
        copies = []
        for k in range(1, N_DEV):
            peer = (x ^ (k >> 2), y ^ ((k >> 1) & 1), c ^ (k & 1))
            cp = pltpu.make_async_remote_copy(
                src_ref=x_ref, dst_ref=gathered.at[me], send_sem=send_sems.at[k - 1], recv_sem=recv_sems.at[k - 1],
                device_id=peer, device_id_type=MESH)
            cp.start()
            copies.append(cp)
        for cp in copies:
            cp.wait()
        acc = gathered[0]
        for k in range(1, N_DEV):
            acc = acc + gathered[k]
        o_ref[...] = acc

    vm = pl.BlockSpec(memory_space=pltpu.VMEM)
    return pl.pallas_call(
        body, name="all_reduce_small", in_specs=[vm, ANY], out_specs=vm,
        out_shape=jax.ShapeDtypeStruct((SMALL_ROWS, D_MODEL), F32),
        scratch_shapes=[pltpu.VMEM((N_DEV, SMALL_ROWS, D_MODEL), F32), pltpu.SemaphoreType.DMA((N_DEV - 1,)),
                        pltpu.SemaphoreType.DMA((N_DEV - 1,))],
    )(packed, behind)


def _swiglu_block_fwd(h, norm_g, w_gu, w_down, tag, behind=()):
    n = rmsnorm_fwd(h, norm_g, f"{tag}_norm", behind=behind)
    gu = matmul(n, w_gu, name=f"{tag}_gate_up")
    s = swiglu_fwd(gu, f"{tag}_swiglu")
    h_out = matmul(s, w_down, res=h, scale=0.5, name=f"{tag}_down")
    return h_out, (n, gu, s)


def _swiglu_block_bwd(h, norm_g, w_gu, w_down, saved, dh_out, tag, behind=()):
    n, gu, s = saved
    df = dh_out.astype(MXU_DTYPE)
    d_down = matmul(s, df, ta=True, scale=0.5, out_dtype=GRAD_COMM_DTYPE, name=f"{tag}_d_w_down")
    ds = matmul(df, w_down, tb=True, scale=0.5, behind=behind, name=f"{tag}_d_s")
    dgu = swiglu_bwd(gu, ds, f"{tag}_swiglu_bwd")
    d_gu = matmul(n, dgu, ta=True, out_dtype=GRAD_COMM_DTYPE, name=f"{tag}_d_w_gate_up")
    dn = matmul(dgu, w_gu, tb=True, name=f"{tag}_d_n")
    dh, dg = rmsnorm_bwd(h, norm_g, dn, dh_out, f"{tag}_norm_bwd")
    return dh, dg, d_gu, d_down


def local_step(x, target, small, exchange):
    big = {}
    token, big_ffn1 = exchange.weights("ffn1", x)
    big.update(big_ffn1)
    h1, saved1 = _swiglu_block_fwd(x, small["ffn1_norm"], big["ffn1_w_gate_up"], big["ffn1_w_down"], "ffn1", token)
    token, big_mix = exchange.weights("mix", h1)
    big.update(big_mix)
    u = rmsnorm_fwd(h1, small["mix_norm"], "mix_norm", behind=token)
    z = matmul(u, big["w_in"], name="w_in")
    p = small["hg_lower_bounds"]
    lb = 1.0 / (1.0 + jnp.exp(p[1:2] - p[0:1]))
    y_hg, o_raw, states = hgrn_fwd(z, lb, small["hg_out_norm"], "hgrn_fwd")
    o_att, l_att = zip(*[att_fwd(z, g, f"att_fwd_{g}") for g in range(N_GROUPS)])
    y_att = att_combine_fwd(o_att, l_att, "att_combine")
    bh = matmul(y_hg, big["w_branch_hg"], name="branch_hg")
    ba = matmul(y_att, big["w_branch_att"], name="branch_att")
    merged = merge_fwd(z, bh, ba, "merge")
    h2 = matmul(merged, big["w_out"], res=h1, name="w_out")
    token, big_ffn2 = exchange.weights("ffn2", h2)
    big.update(big_ffn2)
    h3, saved2 = _swiglu_block_fwd(h2, small["ffn2_norm"], big["ffn2_w_gate_up"], big["ffn2_w_down"], "ffn2", token)
    dh3, d_final, loss = final_norm_loss(h3, small["final_norm"], target, "final_norm_loss")

    gs, gb = {"final_norm": d_final}, {}
    dh2, gs["ffn2_norm"], gb["ffn2_w_gate_up"], gb["ffn2_w_down"] = _swiglu_block_bwd(
        h2, small["ffn2_norm"], big["ffn2_w_gate_up"], big["ffn2_w_down"], saved2, dh3, "ffn2")
    token = exchange.gradients("ffn2", gb, dh2)
    dh2_m = dh2.astype(MXU_DTYPE)
    gb["w_out"] = matmul(merged, dh2_m, ta=True, out_dtype=GRAD_COMM_DTYPE, name="d_w_out")
    dmerged = matmul(dh2_m, big["w_out"], tb=True, behind=token, name="d_merged")
    dbh, dba, dgh, dga = merge_bwd(z, bh, ba, dmerged, "merge_bwd")
    gb["w_branch_hg"] = matmul(y_hg, dbh, ta=True, out_dtype=GRAD_COMM_DTYPE, name="d_w_branch_hg")
    gb["w_branch_att"] = matmul(y_att, dba, ta=True, out_dtype=GRAD_COMM_DTYPE, name="d_w_branch_att")
    dy_hg = matmul(dbh, big["w_branch_hg"], tb=True, name="d_y_hg")
    dy_att = matmul(dba, big["w_branch_att"], tb=True, name="d_y_att")
    dq, dfp, di, dog, d_lb, gs["hg_out_norm"] = hgrn_bwd(z, lb, small["hg_out_norm"], o_raw, states, dy_hg, "hgrn_bwd")
    do_att, corr = att_combine_bwd(o_att, l_att, dy_att, "att_combine_bwd")
    d_att = [part for g in range(N_GROUPS) for part in att_bwd(z, l_att[g], do_att[g], corr[g], g, f"att_bwd_{g}")]
    dz = jnp.concatenate([dq, dfp, di, dog, *d_att, dgh, dga], axis=1)
    gb["w_in"] = matmul(u, dz, ta=True, out_dtype=GRAD_COMM_DTYPE, name="d_w_in")
    du = matmul(dz, big["w_in"], tb=True, name="d_u")
    dh1, gs["mix_norm"] = rmsnorm_bwd(h1, small["mix_norm"], du, dh2, "mix_norm_bwd")
    token = exchange.gradients("mix", gb, dh1)
    dp0 = d_lb * lb * (1.0 - lb)
    gs["hg_lower_bounds"] = jnp.concatenate([dp0, -dp0], axis=0)
    dx, gs["ffn1_norm"], gb["ffn1_w_gate_up"], gb["ffn1_w_down"] = _swiglu_block_bwd(
        x, small["ffn1_norm"], big["ffn1_w_gate_up"], big["ffn1_w_down"], saved1, dh1, "ffn1", token)
    exchange.gradients("ffn1", gb, dx)
    return loss[0, 0], dx, gs


SMALL = ("ffn1_norm", "mix_norm", "hg_lower_bounds", "hg_out_norm", "ffn2_norm", "final_norm")
WEIGHTS = ("ffn1_norm", "ffn1_w_gate_up", "ffn1_w_down", "mix_norm", "w_in", "hg_lower_bounds", "hg_out_norm",
           "w_branch_hg", "w_branch_att", "w_out", "ffn2_norm", "ffn2_w_gate_up", "ffn2_w_down", "final_norm")
SMALL_SHAPE = {"ffn1_norm": (1, 1024), "mix_norm": (1, 1024), "hg_lower_bounds": (2, 512), "hg_out_norm": (1, 512),
               "ffn2_norm": (1, 1024), "final_norm": (1024,)}
LOSS_ROW = 6


def _pack_small(vals):
    rows = []
    for n in SMALL:
        r = vals[n].reshape(1, -1).astype(F32)
        rows.append(jnp.pad(r, ((0, 0), (0, D_MODEL - r.shape[1]))))
    rows.append(jnp.zeros((SMALL_ROWS - len(SMALL), D_MODEL), F32))
    return jnp.concatenate(rows, axis=0)


def _unpack_small(packed):
    out = {}
    for i, n in enumerate(SMALL):
        size = int(np.prod(SMALL_SHAPE[n]))
        out[n] = packed[i, :size].reshape(SMALL_SHAPE[n])
    return out


class WeightExchange:
    ORDER = ("ffn1", "mix", "ffn2")

    def __init__(self, shards, core, chip):
        self.core, self.chip = core, chip
        self.scattering = None
        self.reduced = {}
        first = self.ORDER[0]
        self.placed = {BIG[i][0]: place_own_block(shards[BIG[i][0]], chip, *BIG[i][1:], f"place_{BIG[i][0]}")
                       for i in GROUPS[first]}
        self._start_gather(first, self.placed[self._names(first)[0]])
        chip_behind = chip + self.token[0, :1].astype(jnp.int32)
        for group in self.ORDER[1:]:
            for i in GROUPS[group]:
                n, r, cc, ax = BIG[i]
                self.placed[n] = place_own_block(shards[n], chip_behind, r, cc, ax, f"place_{n}")
        self.placed_behind = [self.placed[n] for group in self.ORDER[1:] for n in self._names(group)]

    def _names(self, group):
        return [BIG[i][0] for i in GROUPS[group]]

    def _start_gather(self, group, after):
        send_sems, recv_sems, bufs, self.token = gather_start([self.placed[n] for n in self._names(group)], after, group)
        self.gathering = (group, send_sems, recv_sems, bufs)

    def weights(self, group, h):
        pending, send_sems, recv_sems, bufs = self.gathering
        assert pending == group
        after = self.placed_behind if group == self.ORDER[0] else [h]
        whole = gather_forward(gather_wait(bufs, send_sems, recv_sems, after, group), group)
        later = self.ORDER.index(group) + 1
        behind = []
        if later < len(self.ORDER):
            self._start_gather(self.ORDER[later], whole[0])
            behind = [self.token]
        return behind, dict(zip(self._names(group), whole))

    def _finish_scatter(self, after):
        group, send_sems, recv_sems, halves, got = self.scattering
        halves, got = scatter_wait(halves, got, send_sems, recv_sems, after, group)
        ws = [BIG[i] for i in GROUPS[group]]
        mine = [add_pieces(h, g, self.chip, r, cc, ax, f"add_pieces_{n}") for (n, r, cc, ax), h, g in zip(ws, halves, got)]
        theirs = exchange_reduced(mine, group)
        self.reduced.update({n: (a, b) for (n, *_), a, b in zip(ws, mine, theirs)})
        self.scattering = None
        return theirs[0]

    def gradients(self, group, grads, dh):
        behind = [self._finish_scatter([dh])] if self.scattering is not None else []
        ws = [BIG[i] for i in GROUPS[group]]
        theirs = exchange_halves([grads[n] for n, *_ in ws], group)
        halves = [add_halves(grads[n], t, self.core, r, cc, ax, f"add_halves_{n}") for (n, r, cc, ax), t in zip(ws, theirs)]
        send_sems, recv_sems, halves, got, self.token = scatter_start(halves, group)
        self.scattering = (group, send_sems, recv_sems, halves, got)
        return behind + [self.token]

    def finish(self, after):
        self._finish_scatter(after)
        return self.reduced


def kernel(x, ffn1_norm, ffn1_w_gate_up, ffn1_w_down, mix_norm, w_in, hg_lower_bounds, hg_out_norm, w_branch_hg, w_branch_att, w_out, ffn2_norm, ffn2_w_gate_up, ffn2_w_down, final_norm, loss_target, m_ffn1_norm, m_ffn1_w_gate_up, m_ffn1_w_down, m_mix_norm, m_w_in, m_hg_lower_bounds, m_hg_out_norm, m_w_branch_hg, m_w_branch_att, m_w_out, m_ffn2_norm, m_ffn2_w_gate_up, m_ffn2_w_down, m_final_norm, v_ffn1_norm, v_ffn1_w_gate_up, v_ffn1_w_down, v_mix_norm, v_w_in, v_hg_lower_bounds, v_hg_out_norm, v_w_branch_hg, v_w_branch_att, v_w_out, v_ffn2_norm, v_ffn2_w_gate_up, v_ffn2_w_down, v_final_norm):
    w = dict(ffn1_norm=ffn1_norm, ffn1_w_gate_up=ffn1_w_gate_up, ffn1_w_down=ffn1_w_down, mix_norm=mix_norm, w_in=w_in,
             hg_lower_bounds=hg_lower_bounds, hg_out_norm=hg_out_norm, w_branch_hg=w_branch_hg, w_branch_att=w_branch_att,
             w_out=w_out, ffn2_norm=ffn2_norm, ffn2_w_gate_up=ffn2_w_gate_up, ffn2_w_down=ffn2_w_down, final_norm=final_norm)
    m = dict(ffn1_norm=m_ffn1_norm, ffn1_w_gate_up=m_ffn1_w_gate_up, ffn1_w_down=m_ffn1_w_down, mix_norm=m_mix_norm,
             w_in=m_w_in, hg_lower_bounds=m_hg_lower_bounds, hg_out_norm=m_hg_out_norm, w_branch_hg=m_w_branch_hg,
             w_branch_att=m_w_branch_att, w_out=m_w_out, ffn2_norm=m_ffn2_norm, ffn2_w_gate_up=m_ffn2_w_gate_up,
             ffn2_w_down=m_ffn2_w_down, final_norm=m_final_norm)
    v = dict(ffn1_norm=v_ffn1_norm, ffn1_w_gate_up=v_ffn1_w_gate_up, ffn1_w_down=v_ffn1_w_down, mix_norm=v_mix_norm,
             w_in=v_w_in, hg_lower_bounds=v_hg_lower_bounds, hg_out_norm=v_hg_out_norm, w_branch_hg=v_w_branch_hg,
             w_branch_att=v_w_branch_att, w_out=v_w_out, ffn2_norm=v_ffn2_norm, ffn2_w_gate_up=v_ffn2_w_gate_up,
             ffn2_w_down=v_ffn2_w_down, final_norm=v_final_norm)

    core = lax.axis_index("c").astype(jnp.int32).reshape(1)
    chip = (2 * lax.axis_index("x") + lax.axis_index("y")).astype(jnp.int32).reshape(1)
    exchange = WeightExchange({n: w[n][0] for n, *_ in BIG}, core, chip)
    small = {n: w[n] for n in SMALL}
    small["final_norm"] = final_norm.reshape(1, D_MODEL)

    loss, dx, gs = local_step(x[0], loss_target[0], small, exchange)

    grads, delta, new_m, new_v = {}, {}, {}, {}

    def update(group, core):
        for i in GROUPS[group]:
            n, r, cc, ax = BIG[i]
            a, b = exchange.reduced[n]
            g, d, nm, nv = adamw_halves(w[n][0], a, b, m[n][0], v[n][0], core, r, cc, ax, f"adamw_{n}")
            grads[n], delta[n], new_m[n], new_v[n] = g[None], d[None], nm[None], nv[None]

    core_behind = core + exchange.token[0, :1].astype(jnp.int32)
    update("ffn2", core_behind)
    update("mix", core_behind)
    exchange.finish(after=[delta[BIG[i][0]] for group in ("ffn2", "mix") for i in GROUPS[group]])
    update("ffn1", core)
    packed = _pack_small(gs)
    packed = packed.at[LOSS_ROW].set(jnp.full((D_MODEL,), loss, F32))
    total = all_reduce_small(packed, behind=delta["ffn1_w_down"])
    grads.update(_unpack_small(total))
    loss_total = total[LOSS_ROW, 0]
    pd, pm, pv = adamw(_pack_small({n: w[n] for n in SMALL}), total.at[LOSS_ROW].set(0.0),
                       _pack_small({n: m[n] for n in SMALL}), _pack_small({n: v[n] for n in SMALL}), "adamw_small")
    delta.update(_unpack_small(pd))
    new_m.update(_unpack_small(pm))
    new_v.update(_unpack_small(pv))

    return (loss_total, dx[None], *[grads[n] for n in WEIGHTS], *[delta[n] for n in WEIGHTS],
            *[new_m[n] for n in WEIGHTS], *[new_v[n] for n in WEIGHTS])
```

```python
import numpy as np
import jax
import jax.numpy as jnp
from jax import lax
from jax.experimental import pallas as pl
from jax.experimental.pallas import tpu as pltpu

SEQ = 2048
D_MODEL = 1024
D_FF = 2816
HG_HEADS = 4
HG_DIM = 128
HG_WIDTH = 512
HG_CHUNK = 64
ATT_GROUPS = ((128, 1), (512, 4), (2048, 16))
ATT_HEADS = 8
ATT_WIDTH = 512
ATT_BLOCK = 128
ALIBI_MAX = 8.0
IN_COLS = 8704
EPS = 1e-6
NEG_INF = -1e30
ADAM_LR = 0.001
ADAM_B1 = 0.9
ADAM_B2 = 0.999
ADAM_EPS = 1e-08
ADAM_WD = 0.01
ADAM_STEP = 10

N_CHIPS = 4
MXU_DTYPE = jnp.bfloat16
WEIGHT_COMM_DTYPE = jnp.bfloat16
GRAD_COMM_DTYPE = jnp.bfloat16
MESH = pl.DeviceIdType.MESH
F32 = jnp.float32
HIGHEST = lax.Precision.HIGHEST


def _pick(n, cands):
    for c in cands:
        if n % c == 0:
            return c
    return n


def _sigmoid(x):
    return 1.0 / (1.0 + jnp.exp(-x))


def _dot(a, b, ta=False, tb=False):
    dn = (((0 if ta else 1,), (1 if tb else 0,)), ((), ()))
    return lax.dot_general(a.astype(MXU_DTYPE), b.astype(MXU_DTYPE), dn, preferred_element_type=F32)


def _dot_f32(a, b, ones_on_right=False):
    x = a if ones_on_right else b
    hi = x.astype(jnp.bfloat16)
    rest = x - hi.astype(F32)
    mid = rest.astype(jnp.bfloat16)
    lo = (rest - mid.astype(F32)).astype(jnp.bfloat16)
    if ones_on_right:
        dot = lambda q: jnp.dot(q, b.astype(jnp.bfloat16), preferred_element_type=F32)
    else:
        dot = lambda q: jnp.dot(a.astype(jnp.bfloat16), q, preferred_element_type=F32)
    return dot(hi) + (dot(mid) + dot(lo))


def _split_bf16(x):
    hi = x.astype(jnp.bfloat16)
    return hi, (x - hi.astype(F32)).astype(jnp.bfloat16)


def _hdot(a, b, ta=False, tb=False):
    dn =(((0 if ta else 1,), (1 if tb else 0,)), ((), ()))
    (a_hi, a_lo), (b_hi, b_lo) = _split_bf16(a), _split_bf16(b)
    dot = lambda p, q: lax.dot_general(p, q, dn, preferred_element_type=F32)
    return dot(a_hi, b_hi) + (dot(a_lo, b_hi) + dot(a_hi, b_lo))


MATMUL_VMEM_BYTES = 48 * 1024 * 1024
MATMUL_TILE_BYTES = 36 * 1024 * 1024
MXU_ALIGN = 128


def _divisors(n, most):
    return [t for t in range(min(n, most), 0, -MXU_ALIGN) if n % t == 0 and t % MXU_ALIGN == 0]


def _matmul_tiles(M, N, K, in_bytes, out_bytes, has_res):
    best = None
    for tk in _divisors(K, K):
        nk = K // tk
        for tm in _divisors(M, 2048):
            for tn in _divisors(N, 512):
                tiles = 2 * in_bytes * (tm * tk + tk * tn) + 2 * out_bytes * tm * tn
                tiles += 4 * tm * tn * ((nk > 1) + 2 * has_res)
                if tiles > MATMUL_TILE_BYTES:
                    continue
                traffic = in_bytes * (M * K * (1 if nk == 1 else N // tn) + K * N * (M // tm))
                key = (traffic, -tm * tn * tk)
                if best is None or key < best[0]:
                    best = (key, (tm, tn, tk))
    return best[1]


def matmul(a, b, *, ta=False, tb=False, out_dtype=F32, res=None, scale=1.0, behind=(), name):
    if ta:
        K, M = a.shape
    else:
        M, K = a.shape
    if tb:
        N, K2 = b.shape
    else:
        K2, N = b.shape
    assert K == K2 and a.dtype == b.dtype
    tm, tn, tk = _matmul_tiles(M, N, K, a.dtype.itemsize, jnp.dtype(out_dtype).itemsize, res is not None)
    nk = K // tk

    def finish(r, r_ref, o_ref):
        if scale != 1.0:
            r = r * scale
        if res is not None:
            r = r_ref[...] + r
        o_ref[...] = r.astype(out_dtype)

    def body(*refs):
        a_ref, b_ref = refs[:2]
        r_ref = refs[2] if res is not None else None
        o_ref = refs[2 + (res is not None) + len(behind)]
        if nk == 1:
            finish(_dot(a_ref[...], b_ref[...], ta, tb), r_ref, o_ref)
            return
        acc = refs[-1]
        k = pl.program_id(2)

        @pl.when(k == 0)
        def _():
            acc[...] = jnp.zeros_like(acc)

        acc[...] += _dot(a_ref[...], b_ref[...], ta, tb)

        @pl.when(k == nk - 1)
        def _():
            finish(acc[...], r_ref, o_ref)

    a_spec = pl.BlockSpec((tk, tm), lambda i, j, k: (k, i)) if ta else pl.BlockSpec((tm, tk), lambda i, j, k: (i, k))
    b_spec = pl.BlockSpec((tn, tk), lambda i, j, k: (j, k)) if tb else pl.BlockSpec((tk, tn), lambda i, j, k: (k, j))
    in_specs = [a_spec, b_spec]
    args = [a, b]
    if res is not None:
        in_specs.append(pl.BlockSpec((tm, tn), lambda i, j, k: (i, j)))
        args.append(res)
    for earlier in behind:
        in_specs.append(pl.BlockSpec(memory_space=pl.ANY))
        args.append(earlier)
    return pl.pallas_call(
        body, name=name, grid=(M // tm, N // tn, nk), in_specs=in_specs,
        out_specs=pl.BlockSpec((tm, tn), lambda i, j, k: (i, j)),
        out_shape=jax.ShapeDtypeStruct((M, N), out_dtype),
        scratch_shapes=[pltpu.VMEM((tm, tn), F32)] if nk > 1 else [],
        compiler_params=pltpu.CompilerParams(dimension_semantics=("parallel", "parallel", "arbitrary"),
                                             vmem_limit_bytes=MATMUL_VMEM_BYTES),
    )(*args)


ROW_TILE = 256


def rmsnorm_fwd(x, g, name, behind=()):
    def body(x_ref, g_ref, *refs):
        n_ref = refs[-1]
        xv = x_ref[...]
        r = lax.rsqrt(jnp.mean(xv * xv, axis=-1, keepdims=True) + EPS)
        n_ref[...] = ((xv * r) * g_ref[...]).astype(n_ref.dtype)

    order = list(behind)
    return pl.pallas_call(
        body, name=name, grid=(SEQ // ROW_TILE,),
        in_specs=[pl.BlockSpec((ROW_TILE, D_MODEL), lambda i: (i, 0)), pl.BlockSpec((1, D_MODEL), lambda i: (0, 0))]
        + [pl.BlockSpec(memory_space=pl.ANY)] * len(order),
        out_specs=pl.BlockSpec((ROW_TILE, D_MODEL), lambda i: (i, 0)),
        out_shape=jax.ShapeDtypeStruct((SEQ, D_MODEL), MXU_DTYPE),
    )(x, g, *order)


def rmsnorm_bwd(x, g, dn, dres, name):
    def body(x_ref, g_ref, dn_ref, dr_ref, dx_ref, dg_ref):
        xv = x_ref[...]
        r = lax.rsqrt(jnp.mean(xv * xv, axis=-1, keepdims=True) + EPS)
        xh = xv * r
        dnv = dn_ref[...]

        @pl.when(pl.program_id(0) == 0)
        def _():
            dg_ref[...] = jnp.zeros_like(dg_ref)

        dg_ref[...] += jnp.sum(dnv * xh, axis=0, keepdims=True)
        dxh = dnv * g_ref[...]
        dx_ref[...] = dr_ref[...] + r * (dxh - xh * jnp.mean(dxh * xh, axis=-1, keepdims=True))

    row = pl.BlockSpec((ROW_TILE, D_MODEL), lambda i: (i, 0))
    vec = pl.BlockSpec((1, D_MODEL), lambda i: (0, 0))
    return pl.pallas_call(
        body, name=name, grid=(SEQ // ROW_TILE,), in_specs=[row, vec, row, row], out_specs=[row, vec],
        out_shape=[jax.ShapeDtypeStruct((SEQ, D_MODEL), F32), jax.ShapeDtypeStruct((1, D_MODEL), F32)],
        compiler_params=pltpu.CompilerParams(dimension_semantics=("arbitrary",)),
    )(x, g, dn, dres)


def final_norm_loss(h, g, target, name):
    def body(h_ref, g_ref, t_ref, dh_ref, dg_ref, loss_ref):
        xv = h_ref[...]
        r = lax.rsqrt(jnp.mean(xv * xv, axis=-1, keepdims=True) + EPS)
        xh = xv * r
        gv = g_ref[...]
        e = xh * gv - t_ref[...]

        @pl.when(pl.program_id(0) == 0)
        def _():
            dg_ref[...] = jnp.zeros_like(dg_ref)
            loss_ref[...] = jnp.zeros_like(loss_ref)

        part = 0.5 * jnp.sum(jnp.sum(e * e, axis=-1, keepdims=True) * (1.0 / D_MODEL), axis=0, keepdims=True)
        loss_ref[...] += jnp.broadcast_to(part, loss_ref.shape)
        dout = e * (1.0 / D_MODEL)
        dg_ref[...] += jnp.sum(dout * xh, axis=0, keepdims=True)
        dxh = dout * gv
        dh_ref[...] = r * (dxh - xh * jnp.mean(dxh * xh, axis=-1, keepdims=True))

    row = pl.BlockSpec((ROW_TILE, D_MODEL), lambda i: (i, 0))
    vec = pl.BlockSpec((1, D_MODEL), lambda i: (0, 0))
    return pl.pallas_call(
        body, name=name, grid=(SEQ // ROW_TILE,), in_specs=[row, vec, row],
        out_specs=[row, vec, pl.BlockSpec((8, 128), lambda i: (0, 0))],
        out_shape=[jax.ShapeDtypeStruct((SEQ, D_MODEL), F32), jax.ShapeDtypeStruct((1, D_MODEL), F32),
                   jax.ShapeDtypeStruct((8, 128), F32)],
        compiler_params=pltpu.CompilerParams(dimension_semantics=("arbitrary",)),
    )(h, g, target)


FF_TILE = D_FF // 2


def swiglu_fwd(gu, name):
    def body(a_ref, b_ref, s_ref):
        a = a_ref[...]
        s_ref[...] = (a * _sigmoid(a) * b_ref[...]).astype(s_ref.dtype)

    return pl.pallas_call(
        body, name=name, grid=(SEQ // ROW_TILE, 2),
        in_specs=[pl.BlockSpec((ROW_TILE, FF_TILE), lambda i, j: (i, j)),
                  pl.BlockSpec((ROW_TILE, FF_TILE), lambda i, j: (i, j + 2))],
        out_specs=pl.BlockSpec((ROW_TILE, FF_TILE), lambda i, j: (i, j)),
        out_shape=jax.ShapeDtypeStruct((SEQ, D_FF), MXU_DTYPE),
    )(gu, gu)


def swiglu_bwd(gu, ds, name):
    rows = ROW_TILE // 2

    def body(a_ref, b_ref, ds_ref, o_ref):
        a = a_ref[...]
        sg = _sigmoid(a)
        dsv = ds_ref[...]
        o_ref[:, :D_FF] = (dsv * b_ref[...] * (sg * (1.0 + a * (1.0 - sg)))).astype(o_ref.dtype)
        o_ref[:, D_FF:] = (dsv * a * sg).astype(o_ref.dtype)

    return pl.pallas_call(
        body, name=name, grid=(SEQ // rows,),
        in_specs=[pl.BlockSpec((rows, D_FF), lambda i: (i, 0)), pl.BlockSpec((rows, D_FF), lambda i: (i, 1)),
                  pl.BlockSpec((rows, D_FF), lambda i: (i, 0))],
        out_specs=pl.BlockSpec((rows, 2 * D_FF), lambda i: (i, 0)),
        out_shape=jax.ShapeDtypeStruct((SEQ, 2 * D_FF), MXU_DTYPE), compiler_params=SUM_PARAMS,
    )(gu, gu, ds)


GATE_HG_BLK = 6656 // 512
GATE_ATT_BLK = 7680 // 512


def merge_fwd(z, bh, ba, name):
    def body(gh_ref, ga_ref, bh_ref, ba_ref, o_ref):
        o_ref[...] = (_sigmoid(gh_ref[...]) * bh_ref[...] + _sigmoid(ga_ref[...]) * ba_ref[...]).astype(o_ref.dtype)

    blk = pl.BlockSpec((ROW_TILE, 512), lambda i, j: (i, j))
    return pl.pallas_call(
        body, name=name, grid=(SEQ // ROW_TILE, 2),
        in_specs=[pl.BlockSpec((ROW_TILE, 512), lambda i, j: (i, GATE_HG_BLK + j)),
                  pl.BlockSpec((ROW_TILE, 512), lambda i, j: (i, GATE_ATT_BLK + j)), blk, blk],
        out_specs=blk, out_shape=jax.ShapeDtypeStruct((SEQ, D_MODEL), MXU_DTYPE),
    )(z, z, bh, ba)


def merge_bwd(z, bh, ba, dm, name):
    def body(gh_ref, ga_ref, bh_ref, ba_ref, dm_ref, dbh_ref, dba_ref, dgh_ref, dga_ref):
        dmv = dm_ref[...]
        sh = _sigmoid(gh_ref[...])
        sa = _sigmoid(ga_ref[...])
        dbh_ref[...] = (dmv * sh).astype(dbh_ref.dtype)
        dba_ref[...] = (dmv * sa).astype(dba_ref.dtype)
        dgh_ref[...] = (dmv * bh_ref[...] * (sh * (1.0 - sh))).astype(dgh_ref.dtype)
        dga_ref[...] = (dmv * ba_ref[...] * (sa * (1.0 - sa))).astype(dga_ref.dtype)

    blk = pl.BlockSpec((ROW_TILE, 512), lambda i, j: (i, j))
    out = jax.ShapeDtypeStruct((SEQ, D_MODEL), MXU_DTYPE)
    return pl.pallas_call(
        body, name=name, grid=(SEQ // ROW_TILE, 2),
        in_specs=[pl.BlockSpec((ROW_TILE, 512), lambda i, j: (i, GATE_HG_BLK + j)),
                  pl.BlockSpec((ROW_TILE, 512), lambda i, j: (i, GATE_ATT_BLK + j)), blk, blk, blk],
        out_specs=[blk, blk, blk, blk], out_shape=[out, out, out, out],
    )(z, z, bh, ba, dm)


N_CHUNKS = SEQ // HG_CHUNK
HG_STEP_CHUNKS = 4


def _hgrn_gates(q, fp, lb):
    C = HG_CHUNK
    sg = _sigmoid(fp)
    f = lb + (1.0 - lb) * sg
    lf = jnp.log(f)
    row = lax.broadcasted_iota(jnp.int32, (C, C), 0)
    col = lax.broadcasted_iota(jnp.int32, (C, C), 1)
    causal = row >= col
    G = _dot_f32(causal.astype(F32), lf)
    eG = jnp.exp(G)
    enG = jnp.exp(-G)
    qg = q * eG
    kg = (1.0 - f) * enG
    A = jnp.where(causal, _hdot(qg, kg, tb=True), 0.0)
    egl = jnp.exp(jnp.sum(lf, axis=0, keepdims=True))
    return sg, f, causal, eG, enG, qg, kg, A, egl


def hgrn_fwd(z, lb, gain, name):
    C, K = HG_CHUNK, HG_DIM

    def body(q_ref, f_ref, v_ref, og_ref, p_ref, g_ref, y_ref, o_ref, st_ref, state):
        @pl.when(pl.program_id(0) == 0)
        def _():
            state[...] = jnp.zeros_like(state)

        for cc in range(HG_STEP_CHUNKS):
            rows = pl.ds(cc * C, C)
            for h in range(HG_HEADS):
                hd = pl.ds(h * K, K)
                v = v_ref[rows, hd]
                _, _, _, _, _, qg, kg, A, egl = _hgrn_gates(q_ref[rows, hd], f_ref[rows, hd], p_ref[:, hd])
                st = state[h]
                st_ref[h, cc] = st
                o = _hdot(A, v) + _hdot(qg, st, tb=True)
                state[h] = st * egl + _hdot(v, kg * egl, ta=True)
                o_ref[rows, hd] = o
                rs = lax.rsqrt(jnp.mean(o * o, axis=-1, keepdims=True) + EPS)
                og = og_ref[rows, hd]
                y_ref[rows, hd] = (((o * rs) * g_ref[:, hd]) * (og * _sigmoid(og))).astype(y_ref.dtype)

    R = HG_STEP_CHUNKS * C

    def zcol(section):
        return pl.BlockSpec((R, HG_WIDTH), lambda c: (c, section))

    vec = pl.BlockSpec((1, HG_WIDTH), lambda c: (0, 0))
    blk = pl.BlockSpec((R, HG_WIDTH), lambda c: (c, 0))
    return pl.pallas_call(
        body, name=name, grid=(N_CHUNKS // HG_STEP_CHUNKS,),
        in_specs=[zcol(0), zcol(1), zcol(2), zcol(3), vec, vec],
        out_specs=[blk, blk, pl.BlockSpec((HG_HEADS, HG_STEP_CHUNKS, K, K), lambda c: (0, c, 0, 0))],
        out_shape=[jax.ShapeDtypeStruct((SEQ, HG_WIDTH), MXU_DTYPE), jax.ShapeDtypeStruct((SEQ, HG_WIDTH), F32),
                   jax.ShapeDtypeStruct((HG_HEADS, N_CHUNKS, K, K), F32)],
        scratch_shapes=[pltpu.VMEM((HG_HEADS, K, K), F32)],
        compiler_params=pltpu.CompilerParams(dimension_semantics=("arbitrary",)),
    )(z, z, z, z, lb, gain)


def hgrn_bwd(z, lb, gain, o_raw, states, dy, name):
    C, K = HG_CHUNK, HG_DIM

    def body(q_ref, f_ref, v_ref, og_ref, p_ref, g_ref, o_ref, st_ref, dy_ref,
             dq_ref, dfp_ref, dv_ref, dog_ref, dlb_ref, dgain_ref, dstate):
        @pl.when(pl.program_id(0) == 0)
        def _():
            dstate[...] = jnp.zeros_like(dstate)
            dlb_ref[...] = jnp.zeros_like(dlb_ref)
            dgain_ref[...] = jnp.zeros_like(dgain_ref)

        last = lax.broadcasted_iota(jnp.int32, (C, K), 0) == C - 1
        row = lax.broadcasted_iota(jnp.int32, (C, C), 0)
        col = lax.broadcasted_iota(jnp.int32, (C, C), 1)
        anti_causal = (col >= row).astype(F32)
        for cc in reversed(range(HG_STEP_CHUNKS)):
            rows = pl.ds(cc * C, C)
            for h in range(HG_HEADS):
                hd = pl.ds(h * K, K)
                v = v_ref[rows, hd]
                lb = p_ref[:, hd]
                sg, f, causal, eG, enG, qg, kg, A, egl = _hgrn_gates(q_ref[rows, hd], f_ref[rows, hd], lb)
                kd = kg * egl
                st = st_ref[h, cc]
                dst = dstate[h]
                o = o_ref[rows, hd]
                og = og_ref[rows, hd]
                gain_v = g_ref[:, hd]
                dyv = dy_ref[rows, hd]
                rs = lax.rsqrt(jnp.mean(o * o, axis=-1, keepdims=True) + EPS)
                on = o * rs
                sgo = _sigmoid(og)
                silu = og * sgo
                dog_ref[rows, hd] = (dyv * (on * gain_v) * (sgo * (1.0 + og * (1.0 - sgo)))).astype(dog_ref.dtype)
                dgain_ref[:, hd] += jnp.sum(dyv * silu * on, axis=0, keepdims=True)
                don = dyv * gain_v * silu
                do = rs * (don - on * jnp.mean(don * on, axis=-1, keepdims=True))
                dA = jnp.where(causal, _hdot(do, v, tb=True), 0.0)
                dv_ref[rows, hd] = (_hdot(A, do, ta=True) + _hdot(kd, dst, tb=True)).astype(dv_ref.dtype)
                dqg = _hdot(dA, kg) + _hdot(do, st)
                dkg = _hdot(dA, qg, ta=True)
                dkd = _hdot(v, dst)
                dstate[h] = dst * egl + _hdot(do, qg, ta=True)
                dgl = jnp.sum(st * dst, axis=0, keepdims=True) * egl
                dq_ref[rows, hd] = (dqg * eG).astype(dq_ref.dtype)
                dk = dkg * enG + dkd * (enG * egl)
                dG = dqg * qg - dkg * kg - dkd * kd
                extra = jnp.sum(dkd * kd, axis=0, keepdims=True) + dgl
                dG = dG + jnp.where(last, extra, 0.0)
                dlf = _dot_f32(anti_causal, dG)
                df = dlf / f - dk
                dfp_ref[rows, hd] = (df * (1.0 - lb) * (sg * (1.0 - sg))).astype(dfp_ref.dtype)
                dlb_ref[:, hd] += jnp.sum(df * (1.0 - sg), axis=0, keepdims=True)

    R = HG_STEP_CHUNKS * C
    n_steps = N_CHUNKS // HG_STEP_CHUNKS

    def rc(c):
        return n_steps - 1 - c

    def zcol(section):
        return pl.BlockSpec((R, HG_WIDTH), lambda c: (rc(c), section))

    vec = pl.BlockSpec((1, HG_WIDTH), lambda c: (0, 0))
    blk = pl.BlockSpec((R, HG_WIDTH), lambda c: (rc(c), 0))
    out = jax.ShapeDtypeStruct((SEQ, HG_WIDTH), MXU_DTYPE)
    small = jax.ShapeDtypeStruct((1, HG_WIDTH), F32)
    return pl.pallas_call(
        body, name=name, grid=(n_steps,),
        in_specs=[zcol(0), zcol(1), zcol(2), zcol(3), vec, vec, blk,
                  pl.BlockSpec((HG_HEADS, HG_STEP_CHUNKS, K, K), lambda c: (0, rc(c), 0, 0)), blk],
        out_specs=[blk, blk, blk, blk, vec, vec],
        out_shape=[out, out, out, out, small, small],
        scratch_shapes=[pltpu.VMEM((HG_HEADS, K, K), F32)],
        compiler_params=pltpu.CompilerParams(dimension_semantics=("arbitrary",)),
    )(z, z, z, z, lb, gain, o_raw, states, dy)


N_GROUPS = len(ATT_GROUPS)
HEAD_PAIRS = ATT_WIDTH // 128
ATT_COL0 = 4 * HG_WIDTH
UNROLLED_SUBSEQS = 4


def _alibi_coef():
    n = N_GROUPS * ATT_HEADS
    slopes = np.exp2(-ALIBI_MAX * np.arange(1, n + 1, dtype=np.float32) / n).astype(np.float32)
    dil = np.repeat(np.array([d for _, d in ATT_GROUPS], np.float32), ATT_HEADS)
    return jnp.asarray(slopes * dil, F32)


def _subseq_rows(r, d):
    return pl.ds(r, ATT_BLOCK, stride=d) if d > 1 else pl.ds(0, ATT_BLOCK)


def _for_each_subseq(d, fn):
    if d <= UNROLLED_SUBSEQS:
        for r in range(d):
            fn(r)
    else:
        lax.fori_loop(0, d, lambda r, carry: (fn(r), carry)[1], 0)


def _att_specs(g):
    d = ATT_GROUPS[g][1]
    R = ATT_BLOCK * d
    n_slabs = SEQ // R
    col0 = (ATT_COL0 + g * 3 * ATT_WIDTH) // 128

    def cur(col):
        return pl.BlockSpec((R, 128), lambda hp, s: (s, col + hp))

    def prev(col):
        return pl.BlockSpec((R, 128), lambda hp, s: (jnp.maximum(s - 1, 0), col + hp))

    def nxt(col):
        return pl.BlockSpec((R, 128), lambda hp, s: (jnp.minimum(s + 1, n_slabs - 1), col + hp))

    return d, R, n_slabs, col0, cur, prev, nxt


def _head_lanes(j):
    lane = lax.broadcasted_iota(jnp.int32, (ATT_BLOCK, 128), 1)
    return (lane >= 64 * j) & (lane < 64 * (j + 1))


def _lane_value(x, sel):
    return jnp.max(jnp.where(sel, x, -3e38), axis=-1, keepdims=True)


def _stack_heads(x, sel0):
    return jnp.concatenate([jnp.where(sel0, x, 0.0), jnp.where(sel0, 0.0, x)], axis=0)


def _stack_values(x, sel0, lanes):
    swapped = pltpu.roll(x, 64, 1)
    stacked = jnp.concatenate([jnp.where(sel0, x, swapped), jnp.where(sel0, swapped, x)], axis=0)
    return stacked if lanes == 128 else jnp.concatenate([stacked] * (lanes // 128), axis=1)


def _pair_coef(coef_ref, g, hp):
    row = lax.broadcasted_iota(jnp.int32, (2 * ATT_BLOCK, 1), 0)
    first = g * ATT_HEADS + hp * 2
    return jnp.where(row < ATT_BLOCK, coef_ref[first], coef_ref[first + 1])


def _band(with_prev, first_key):
    B = ATT_BLOCK
    keys = 2 * B if with_prev else B
    qi = jnp.bitwise_and(lax.broadcasted_iota(jnp.int32, (2 * B, keys), 0), B - 1)
    kj = lax.broadcasted_iota(jnp.int32, (2 * B, keys), 1)
    delta = qi + (B if with_prev else 0) - kj
    valid = (delta >= 0) & (delta <= B)
    if with_prev:
        valid = valid & (kj >= first_key)
    return valid, delta.astype(F32)


def _band_next(last_slab):
    B = ATT_BLOCK
    qi = jnp.bitwise_and(lax.broadcasted_iota(jnp.int32, (2 * B, B), 0), B - 1)
    kj = lax.broadcasted_iota(jnp.int32, (2 * B, B), 1)
    delta = qi + B - kj
    return (delta <= B) & (kj >= jnp.where(last_slab, B, 0)), delta.astype(F32)


def att_fwd(z, g, name):
    B = ATT_BLOCK
    d, R, n_slabs, col0, cur, prev, _ = _att_specs(g)
    has_prev = n_slabs > 1

    def body(coef_ref, *refs):
        if has_prev:
            q_ref, kc_ref, vc_ref, kp_ref, vp_ref, o_ref, l_ref = refs
        else:
            q_ref, kc_ref, vc_ref, o_ref, l_ref = refs
        hp, s = pl.program_id(0), pl.program_id(1)
        valid, dist = _band(has_prev, jnp.where(s == 0, B, 0))
        cf2 = _pair_coef(coef_ref, g, hp)
        sel0 = _head_lanes(0)

        def one(r):
            rows = _subseq_rows(r, d)
            q2 = _stack_heads(q_ref[rows, :], sel0)
            kk, vv = kc_ref[rows, :], vc_ref[rows, :]
            if has_prev:
                kk = jnp.concatenate([kp_ref[rows, :], kk], axis=0)
                vv = jnp.concatenate([vp_ref[rows, :], vv], axis=0)
            sc = jnp.where(valid, _dot(q2, kk, tb=True) * 0.125 - cf2 * dist, NEG_INF)
            mx = jnp.max(sc, axis=-1, keepdims=True)
            e = jnp.exp(sc - mx)
            den = jnp.sum(e, axis=-1, keepdims=True)
            o2 = _dot(e * (1.0 / den), vv)
            lse2 = mx + jnp.log(den)
            o_ref[rows, :] = jnp.where(sel0, o2[:B], o2[B:])
            l_ref[rows, :] = jnp.where(sel0, lse2[:B], lse2[B:])

        _for_each_subseq(d, one)

    in_specs = [pl.BlockSpec(memory_space=pltpu.SMEM), cur(col0), cur(col0 + 4), cur(col0 + 8)]
    args = [_alibi_coef(), z, z, z]
    if has_prev:
        in_specs += [prev(col0 + 4), prev(col0 + 8)]
        args += [z, z]
    out = jax.ShapeDtypeStruct((SEQ, ATT_WIDTH), F32)
    return pl.pallas_call(
        body, name=name, grid=(HEAD_PAIRS, n_slabs), in_specs=in_specs,
        out_specs=[cur(0), cur(0)], out_shape=[out, out],
        compiler_params=pltpu.CompilerParams(dimension_semantics=("parallel", "arbitrary")),
    )(*args)


def att_bwd(z, l, do, corr, g, name):
    B = ATT_BLOCK
    d, R, n_slabs, col0, cur, prev, nxt = _att_specs(g)
    neighbours = n_slabs > 1

    def body(coef_ref, *refs):
        if neighbours:
            (q_ref, kc_ref, vc_ref, l_ref, do_ref, cr_ref, kp_ref, vp_ref, qn_ref, ln_ref, don_ref, crn_ref,
             dq_ref, dk_ref, dv_ref, dq_sc, dk_sc, dv_sc) = refs
        else:
            q_ref, kc_ref, vc_ref, l_ref, do_ref, cr_ref, dq_ref, dk_ref, dv_ref, dq_sc, dk_sc, dv_sc = refs
        hp, s = pl.program_id(0), pl.program_id(1)
        valid, dist = _band(neighbours, jnp.where(s == 0, B, 0))
        if neighbours:
            valid_n, dist_n = _band_next(s == n_slabs - 1)
        cf2 = _pair_coef(coef_ref, g, hp)
        sel0 = _head_lanes(0)
        own = slice(B, 2 * B) if neighbours else slice(0, B)

        def one(r):
            rows = _subseq_rows(r, d)
            kc, vc = kc_ref[rows, :], vc_ref[rows, :]
            kk, vv = kc, vc
            if neighbours:
                kk = jnp.concatenate([kp_ref[rows, :], kc], axis=0)
                vv = jnp.concatenate([vp_ref[rows, :], vc], axis=0)
            q2, do2 = _stack_heads(q_ref[rows, :], sel0), _stack_heads(do_ref[rows, :], sel0)
            keys = kk.shape[0]
            lse2, cr2 = _stack_values(l_ref[rows, :], sel0, keys), _stack_values(cr_ref[rows, :], sel0, keys)
            p = jnp.exp(jnp.where(valid, _dot(q2, kk, tb=True) * 0.125 - cf2 * dist, NEG_INF) - lse2)
            ds = p * (_dot(do2, vv, tb=True) + cr2)
            dq2 = _dot(ds, kk)
            dk = _dot(ds, q2, ta=True)[own]
            dv = _dot(p, do2, ta=True)[own]
            if neighbours:
                qn2, don2 = _stack_heads(qn_ref[rows, :], sel0), _stack_heads(don_ref[rows, :], sel0)
                lse_n2, cr_n2 = _stack_values(ln_ref[rows, :], sel0, B), _stack_values(crn_ref[rows, :], sel0, B)
                p_n = jnp.exp(jnp.where(valid_n, _dot(qn2, kc, tb=True) * 0.125 - cf2 * dist_n, NEG_INF) - lse_n2)
                ds_n = p_n * (_dot(don2, vc, tb=True) + cr_n2)
                dk = dk + _dot(ds_n, qn2, ta=True)
                dv = dv + _dot(p_n, don2, ta=True)
            dq_sc[rows, :] = jnp.where(sel0, dq2[:B], dq2[B:]) * 0.125
            dk_sc[rows, :] = dk * 0.125
            dv_sc[rows, :] = dv

        _for_each_subseq(d, one)
        dq_ref[...] = dq_sc[...].astype(dq_ref.dtype)
        dk_ref[...] = dk_sc[...].astype(dk_ref.dtype)
        dv_ref[...] = dv_sc[...].astype(dv_ref.dtype)

    in_specs = [pl.BlockSpec(memory_space=pltpu.SMEM), cur(col0), cur(col0 + 4), cur(col0 + 8), cur(0), cur(0), cur(0)]
    args = [_alibi_coef(), z, z, z, l, do, corr]
    if neighbours:
        in_specs += [prev(col0 + 4), prev(col0 + 8), nxt(col0), nxt(0), nxt(0), nxt(0)]
        args += [z, z, z, l, do, corr]
    out = jax.ShapeDtypeStruct((SEQ, ATT_WIDTH), MXU_DTYPE)
    return pl.pallas_call(
        body, name=name, grid=(HEAD_PAIRS, n_slabs), in_specs=in_specs,
        out_specs=[cur(0)] * 3, out_shape=[out] * 3,
        scratch_shapes=[pltpu.VMEM((R, 128), F32)] * 3,
        compiler_params=pltpu.CompilerParams(dimension_semantics=("parallel", "arbitrary"),
                                             vmem_limit_bytes=MATMUL_VMEM_BYTES),
    )(*args)


def _head_sum(x):
    i = lax.broadcasted_iota(jnp.int32, (128, 128), 0) // 64
    j = lax.broadcasted_iota(jnp.int32, (128, 128), 1) // 64
    return _dot_f32(x, (i == j).astype(F32), ones_on_right=True)


def _group_weights(l0, l1, l2):
    mx = jnp.maximum(jnp.maximum(l0, l1), l2)
    e0, e1, e2 = jnp.exp(l0 - mx), jnp.exp(l1 - mx), jnp.exp(l2 - mx)
    inv = 1.0 / (e0 + e1 + e2)
    return e0 * inv, e1 * inv, e2 * inv


def att_combine_fwd(o, l, name):
    def body(o0, o1, o2, l0, l1, l2, y_ref):
        w0, w1, w2 = _group_weights(l0[...], l1[...], l2[...])
        y_ref[...] = (o0[...] * w0 + o1[...] * w1 + o2[...] * w2).astype(y_ref.dtype)

    blk = pl.BlockSpec((ROW_TILE, ATT_WIDTH), lambda i: (i, 0))
    return pl.pallas_call(
        body, name=name, grid=(SEQ // ROW_TILE,), in_specs=[blk] * 6, out_specs=blk,
        out_shape=jax.ShapeDtypeStruct((SEQ, ATT_WIDTH), MXU_DTYPE),
    )(*o, *l)


def att_combine_bwd(o, l, dy, name):
    def body(o0, o1, o2, l0, l1, l2, dy_ref, do0, do1, do2, cr0, cr1, cr2):
        w = _group_weights(l0[...], l1[...], l2[...])
        dyv = dy_ref[...]
        dw = [_head_sum(dyv * o_ref[...]) for o_ref in (o0, o1, o2)]
        tot = w[0] * dw[0] + w[1] * dw[1] + w[2] * dw[2]
        for g, (do_ref, cr_ref) in enumerate(((do0, cr0), (do1, cr1), (do2, cr2))):
            do_ref[...] = dyv * w[g]
            cr_ref[...] = -w[g] * tot

    blk = pl.BlockSpec((ROW_TILE, 128), lambda i, j: (i, j))
    out = jax.ShapeDtypeStruct((SEQ, ATT_WIDTH), F32)
    res = pl.pallas_call(
        body, name=name, grid=(SEQ // ROW_TILE, HEAD_PAIRS), in_specs=[blk] * 7, out_specs=[blk] * 6, out_shape=[out] * 6,
    )(*o, *l, dy)
    return res[:N_GROUPS], res[N_GROUPS:]


SUM_ROW_TILES = (1024, 512, 256, 128, 64, 32, 16)
SUM_TILE_BYTES = 24 * 1024 * 1024
SUM_PARAMS = pltpu.CompilerParams(vmem_limit_bytes=MATMUL_VMEM_BYTES)


def _row_tile(rows, cols, operands):
    fit = [t for t in SUM_ROW_TILES if rows % t == 0]
    return next((t for t in fit if 2 * 4 * operands * t * cols <= SUM_TILE_BYTES), fit[-1])


def _shard_shape(rows, cols, axis):
    return (rows // N_CHIPS, cols) if axis == 0 else (rows, cols // N_CHIPS)


def _half_shape(rows, cols, axis):
    return (rows, cols // 2) if axis == 0 else (rows // 2, cols)


def _piece_shape(rows, cols, axis):
    return (rows // N_CHIPS, cols // 2) if axis == 0 else (rows // 2, cols // N_CHIPS)


def place_own_block(shard, chip, rows, cols, axis, name):
    sr, sc = _shard_shape(rows, cols, axis)
    tr = _row_tile(sr, sc, 2)

    def body(chip_ref, s_ref, o_ref):
        o_ref[...] = s_ref[...].astype(o_ref.dtype)

    if axis == 0:
        out_map = lambda i, chip_ref: (chip_ref[0] * (sr // tr) + i, 0)
    else:
        out_map = lambda i, chip_ref: (i, chip_ref[0])
    return pl.pallas_call(
        body, name=name, out_shape=jax.ShapeDtypeStruct((rows, cols), WEIGHT_COMM_DTYPE), compiler_params=SUM_PARAMS,
        grid_spec=pltpu.PrefetchScalarGridSpec(
            num_scalar_prefetch=1, grid=(sr // tr,), in_specs=[pl.BlockSpec((tr, sc), lambda i, chip_ref: (i, 0))],
            out_specs=pl.BlockSpec((tr, sc), out_map)),
    )(chip, shard)


def add_halves(g, theirs, core, rows, cols, axis, name):
    hr, hc = _half_shape(rows, cols, axis)
    tr = _row_tile(hr, hc, 3)

    def body(core_ref, g_ref, t_ref, o_ref):
        o_ref[...] = (g_ref[...].astype(F32) + t_ref[...].astype(F32)).astype(o_ref.dtype)

    if axis == 0:
        g_map = lambda i, core_ref: (i, core_ref[0])
    else:
        g_map = lambda i, core_ref: (core_ref[0] * (hr // tr) + i, 0)
    blk = pl.BlockSpec((tr, hc), lambda i, core_ref: (i, 0))
    return pl.pallas_call(
        body, name=name, out_shape=jax.ShapeDtypeStruct((hr, hc), GRAD_COMM_DTYPE), compiler_params=SUM_PARAMS,
        grid_spec=pltpu.PrefetchScalarGridSpec(
            num_scalar_prefetch=1, grid=(hr // tr,), in_specs=[pl.BlockSpec((tr, hc), g_map), blk], out_specs=blk),
    )(core, g, theirs)


def add_pieces(half, got, chip, rows, cols, axis, name):
    hr, _ = _half_shape(rows, cols, axis)
    pr, pc = _piece_shape(rows, cols, axis)
    tr = _row_tile(pr, pc, 5)

    def body(chip_ref, h_ref, got_ref, o_ref):
        o_ref[...] = (h_ref[...].astype(F32) + got_ref[0].astype(F32) + got_ref[1].astype(F32) + got_ref[2].astype(F32))

    if axis == 0:
        h_map = lambda i, chip_ref: (chip_ref[0] * (pr // tr) + i, 0)
    else:
        h_map = lambda i, chip_ref: (i, chip_ref[0])
    return pl.pallas_call(
        body, name=name, out_shape=jax.ShapeDtypeStruct((pr, pc), F32), compiler_params=SUM_PARAMS,
        grid_spec=pltpu.PrefetchScalarGridSpec(
            num_scalar_prefetch=1, grid=(pr // tr,),
            in_specs=[pl.BlockSpec((tr, pc), h_map), pl.BlockSpec((3, tr, pc), lambda i, chip_ref: (0, i, 0))],
            out_specs=pl.BlockSpec((tr, pc), lambda i, chip_ref: (i, 0))),
    )(chip, half, got)


def _adamw_math(w, g, m, v):
    nm = ADAM_B1 * m + (1.0 - ADAM_B1) * g
    nv = ADAM_B2 * v + (1.0 - ADAM_B2) * (g * g)
    m_hat = nm / (1.0 - ADAM_B1 ** ADAM_STEP)
    v_hat = nv / (1.0 - ADAM_B2 ** ADAM_STEP)
    return -ADAM_LR * (m_hat / (jnp.sqrt(v_hat) + ADAM_EPS) + ADAM_WD * w), nm, nv


def adamw(w, g, m, v, name):
    R, Cc = w.shape
    tr = _pick(R, (256, 128, 64, 8))

    def body(w_ref, g_ref, m_ref, v_ref, d_ref, nm_ref, nv_ref):
        d_ref[...], nm_ref[...], nv_ref[...] = _adamw_math(w_ref[...], g_ref[...], m_ref[...], v_ref[...])

    blk = pl.BlockSpec((tr, Cc), lambda i: (i, 0))
    out = jax.ShapeDtypeStruct((R, Cc), F32)
    return pl.pallas_call(
        body, name=name, grid=(R // tr,), in_specs=[blk] * 4, out_specs=[blk] * 3, out_shape=[out, out, out],
    )(w, g, m, v)


def adamw_halves(w, mine, theirs, m, v, core, rows, cols, axis, name):
    sr, sc = _shard_shape(rows, cols, axis)
    pr, pc = _piece_shape(rows, cols, axis)
    tr = _row_tile(pr, pc, 9)
    nt = pr // tr

    def body(core_ref, w_ref, a_ref, b_ref, m_ref, v_ref, g_ref, d_ref, nm_ref, nv_ref):
        g = jnp.where(pl.program_id(0) == core_ref[0], a_ref[...], b_ref[...])
        g_ref[...] = g
        d_ref[...], nm_ref[...], nv_ref[...] = _adamw_math(w_ref[...], g, m_ref[...], v_ref[...])

    if axis == 0:
        full = pl.BlockSpec((tr, pc), lambda h, i, core_ref: (i, h))
    else:
        full = pl.BlockSpec((tr, pc), lambda h, i, core_ref: (h * nt + i, 0))
    part = pl.BlockSpec((tr, pc), lambda h, i, core_ref: (i, 0))
    out = jax.ShapeDtypeStruct((sr, sc), F32)
    return pl.pallas_call(
        body, name=name, out_shape=[out, out, out, out], compiler_params=SUM_PARAMS,
        grid_spec=pltpu.PrefetchScalarGridSpec(
            num_scalar_prefetch=1, grid=(2, nt), in_specs=[full, part, part, full, full], out_specs=[full] * 4),
    )(core, w, mine, theirs, m, v)


BIG = (
    ("ffn1_w_gate_up", D_MODEL, 2 * D_FF, 1),
    ("ffn1_w_down", D_FF, D_MODEL, 0),
    ("w_in", D_MODEL, IN_COLS, 1),
    ("w_branch_hg", HG_WIDTH, D_MODEL, 1),
    ("w_branch_att", ATT_WIDTH, D_MODEL, 1),
    ("w_out", D_MODEL, D_MODEL, 0),
    ("ffn2_w_gate_up", D_MODEL, 2 * D_FF, 1),
    ("ffn2_w_down", D_FF, D_MODEL, 0),
)
N_BIG = len(BIG)
ANY = pl.BlockSpec(memory_space=pl.ANY)


def _place():
    return lax.axis_index("x"), lax.axis_index("y"), lax.axis_index("c")


def _other_chips(x, y):
    return ((1 - x, y), (x, 1 - y), (1 - x, 1 - y))


MAX_COPY_CHUNKS = 16
CHUNK_ROW_ALIGN = 16


def _row_chunks(view):
    rows = view.shape[0]
    n = next(n for n in range(MAX_COPY_CHUNKS, 0, -1) if rows % (CHUNK_ROW_ALIGN * n) == 0 or n == 1)
    step = rows // n
    return [pl.ds(i * step, step) for i in range(n)]


def _remote(src, dst, send_sem, recv_sem, device):
    return pltpu.make_async_remote_copy(src_ref=src, dst_ref=dst, send_sem=send_sem, recv_sem=recv_sem,
                                        device_id=device, device_id_type=MESH)


def _start_remote(src, dst, send_sem, recv_sem, device):
    for rows in _row_chunks(src):
        _remote(src.at[rows, :], dst.at[rows, :], send_sem, recv_sem, device).start()
    return _remote(src, dst, send_sem, recv_sem, device)


HBM = pl.BlockSpec(memory_space=pltpu.HBM)
SEM = pl.BlockSpec(memory_space=pltpu.SEMAPHORE)
SPLIT_COPY_EFFECT = pltpu.SideEffectType.DATAFLOW_SIDE_EFFECTING
GROUPS = {"ffn1": (0, 1), "mix": (2, 3, 4, 5), "ffn2": (6, 7)}


def _in_hbm(a):
    return pltpu.with_memory_space_constraint(a, pltpu.HBM)


class _SemList:
    def __init__(self, refs):
        self.refs = refs
        self.at = self

    def __getitem__(self, index):
        w, k = index
        return self.refs[3 * w + k]


def _gather_piece(ref, rows, cols, axis, chip, c):
    sr, sc = _shard_shape(rows, cols, axis)
    j = 2 * chip[0] + chip[1]
    if axis == 0:
        return ref.at[pl.ds(j * sr + c * (sr // 2), sr // 2), :]
    return ref.at[pl.ds(c * (sr // 2), sr // 2), pl.ds(pl.multiple_of(j * sc, 128), sc)]


def _start_gather_sends(bufs, ws, send_sems, recv_sems):
    x, y, c = _place()
    for w, (_, r, cc, ax) in enumerate(ws):
        mine = _gather_piece(bufs[w], r, cc, ax, (x, y), c)
        for k, chip in enumerate(_other_chips(x, y)):
            _start_remote(mine, mine, send_sems.at[w, k], recv_sems.at[w, k], (*chip, c))


def _wait_gather_sends(bufs, ws, send_sems, recv_sems):
    x, y, c = _place()
    for w, (_, r, cc, ax) in enumerate(ws):
        for k, chip in enumerate(_other_chips(x, y)):
            got = _gather_piece(bufs[w], r, cc, ax, chip, c)
            _remote(got, got, send_sems.at[w, k], recv_sems.at[w, k], (x, y, c)).wait_recv()
    for w, (_, r, cc, ax) in enumerate(ws):
        mine = _gather_piece(bufs[w], r, cc, ax, (x, y), c)
        for k in range(3):
            _remote(mine, mine, send_sems.at[w, k], recv_sems.at[w, k], (x, y, c)).wait_send()


def _forward_halves(bufs, ws, send_sems, recv_sems):
    x, y, c = _place()
    passed = []
    for w, (_, r, cc, ax) in enumerate(ws):
        for k, chip in enumerate(_other_chips(x, y)):
            got = _gather_piece(bufs[w], r, cc, ax, chip, c)
            passed.append(_start_remote(got, got, send_sems.at[w, k], recv_sems.at[w, k], (x, y, 1 - c)))
    for w, (_, r, cc, ax) in enumerate(ws):
        for k, chip in enumerate(_other_chips(x, y)):
            got = _gather_piece(bufs[w], r, cc, ax, chip, 1 - c)
            _remote(got, got, send_sems.at[w, k], recv_sems.at[w, k], (x, y, c)).wait_recv()
    for cp in passed:
        cp.wait_send()


def gather_start(placed, after, group):
    ws = [BIG[i] for i in GROUPS[group]]
    n = len(ws)

    def body(*refs):
        bufs = refs[:n]
        send_sems, recv_sems = _SemList(refs[n + 1:4 * n + 1]), _SemList(refs[4 * n + 1:7 * n + 1])
        token = refs[-1]
        _start_gather_sends(bufs, ws, send_sems, recv_sems)
        token[...] = jnp.zeros_like(token)

    out = pl.pallas_call(
        body, name=f"gather_start_{group}", in_specs=[HBM] * n + [ANY],
        out_specs=[SEM] * (6 * n) + [HBM] * n + [pl.BlockSpec(memory_space=pltpu.VMEM)],
        out_shape=[pltpu.SemaphoreType.DMA(())] * (6 * n)
        + [pltpu.HBM((r, cc), WEIGHT_COMM_DTYPE) for _, r, cc, _ in ws] + [jax.ShapeDtypeStruct((8, 128), F32)],
        input_output_aliases={w: 6 * n + w for w in range(n)},
        compiler_params=pltpu.CompilerParams(has_side_effects=SPLIT_COPY_EFFECT),
    )(*[_in_hbm(p) for p in placed], after)
    return out[:3 * n], out[3 * n:6 * n], out[6 * n:7 * n], out[-1]


def gather_wait(bufs, send_sems, recv_sems, after, group):
    ws = [BIG[i] for i in GROUPS[group]]
    n = len(ws)

    def body(*refs):
        _wait_gather_sends(refs[:n], ws, _SemList(refs[n:n + 3 * n]), _SemList(refs[n + 3 * n:n + 6 * n]))

    return pl.pallas_call(
        body, name=f"gather_wait_{group}", in_specs=[HBM] * n + [SEM] * (6 * n) + [ANY] * len(after), out_specs=[HBM] * n,
        out_shape=[pltpu.HBM((r, cc), WEIGHT_COMM_DTYPE) for _, r, cc, _ in ws],
        input_output_aliases={w: w for w in range(n)},
        compiler_params=pltpu.CompilerParams(has_side_effects=SPLIT_COPY_EFFECT),
    )(*bufs, *send_sems, *recv_sems, *after)


def gather_forward(bufs, group):
    ws = [BIG[i] for i in GROUPS[group]]
    n = len(ws)

    def body(*refs):
        _forward_halves(refs[n:2 * n], ws, refs[2 * n], refs[2 * n + 1])

    return pl.pallas_call(
        body, name=f"gather_forward_{group}", in_specs=[ANY] * n, out_specs=[ANY] * n,
        out_shape=[jax.ShapeDtypeStruct((r, cc), WEIGHT_COMM_DTYPE) for _, r, cc, _ in ws],
        input_output_aliases={w: w for w in range(n)},
        scratch_shapes=[pltpu.SemaphoreType.DMA((n, 3))] * 2,
    )(*bufs)


def _half(ref, rows, cols, axis, c):
    if axis == 0:
        return ref.at[:, pl.ds(pl.multiple_of(c * (cols // 2), 128), cols // 2)]
    return ref.at[pl.ds(c * (rows // 2), rows // 2), :]


def _piece_of_half(ref, rows, cols, axis, chip):
    j = 2 * chip[0] + chip[1]
    pr, pc = _piece_shape(rows, cols, axis)
    if axis == 0:
        return ref.at[pl.ds(j * pr, pr), :]
    return ref.at[:, pl.ds(pl.multiple_of(j * pc, 128), pc)]


def exchange_halves(grads, group):
    ws = [BIG[i] for i in GROUPS[group]]
    n = len(ws)

    def body(*refs):
        ins, theirs = refs[:n], refs[n:2 * n]
        send_sems, recv_sems = refs[2 * n:]
        x, y, c = _place()
        copies = [_start_remote(_half(ins[w], r, cc, ax, 1 - c), theirs[w], send_sems.at[w], recv_sems.at[w], (x, y, 1 - c))
                  for w, (_, r, cc, ax) in enumerate(ws)]
        for cp in copies:
            cp.wait()

    return pl.pallas_call(
        body, name=f"exchange_halves_{group}", in_specs=[ANY] * n, out_specs=[ANY] * n,
        out_shape=[jax.ShapeDtypeStruct(_half_shape(r, cc, ax), GRAD_COMM_DTYPE) for _, r, cc, ax in ws],
        scratch_shapes=[pltpu.SemaphoreType.DMA((n,)), pltpu.SemaphoreType.DMA((n,))],
    )(*grads)


def _scatter_copies(halves, got, ws, send_sems, recv_sems, start):
    x, y, c = _place()
    copies = []
    for w, (_, r, cc, ax) in enumerate(ws):
        for k, chip in enumerate(_other_chips(x, y)):
            args = (_piece_of_half(halves[w], r, cc, ax, chip), got[w].at[k], send_sems.at[w, k], recv_sems.at[w, k], (*chip, c))
            copies.append(_start_remote(*args) if start else _remote(*args))
    return copies


def scatter_start(halves, group):
    ws = [BIG[i] for i in GROUPS[group]]
    n = len(ws)

    def body(*refs):
        sems = refs[2 * n:8 * n]
        _scatter_copies(refs[:n], refs[n:2 * n], ws, _SemList(sems[:3 * n]), _SemList(sems[3 * n:]), start=True)
        refs[-1][...] = jnp.zeros_like(refs[-1])

    landing = [lax.empty((3,) + _piece_shape(r, cc, ax), GRAD_COMM_DTYPE) for _, r, cc, ax in ws]
    out = pl.pallas_call(
        body, name=f"scatter_start_{group}", in_specs=[HBM] * (2 * n),
        out_specs=[SEM] * (6 * n) + [HBM] * (2 * n) + [pl.BlockSpec(memory_space=pltpu.VMEM)],
        out_shape=[pltpu.SemaphoreType.DMA(())] * (6 * n)
        + [pltpu.HBM(_half_shape(r, cc, ax), GRAD_COMM_DTYPE) for _, r, cc, ax in ws]
        + [pltpu.HBM((3,) + _piece_shape(r, cc, ax), GRAD_COMM_DTYPE) for _, r, cc, ax in ws]
        + [jax.ShapeDtypeStruct((8, 128), F32)],
        input_output_aliases={i: 6 * n + i for i in range(2 * n)},
        compiler_params=pltpu.CompilerParams(has_side_effects=SPLIT_COPY_EFFECT),
    )(*[_in_hbm(h) for h in halves], *[_in_hbm(b) for b in landing])
    return out[:3 * n], out[3 * n:6 * n], out[6 * n:7 * n], out[7 * n:8 * n], out[-1]


def scatter_wait(halves, got, send_sems, recv_sems, after, group):
    ws = [BIG[i] for i in GROUPS[group]]
    n = len(ws)

    def body(*refs):
        sems = refs[2 * n:8 * n]
        for cp in _scatter_copies(refs[:n], refs[n:2 * n], ws, _SemList(sems[:3 * n]), _SemList(sems[3 * n:]), start=False):
            cp.wait_send()
            cp.wait_recv()

    out = pl.pallas_call(
        body, name=f"scatter_wait_{group}", in_specs=[HBM] * (2 * n) + [SEM] * (6 * n) + [ANY] * len(after),
        out_specs=[HBM] * (2 * n),
        out_shape=[pltpu.HBM(_half_shape(r, cc, ax), GRAD_COMM_DTYPE) for _, r, cc, ax in ws]
        + [pltpu.HBM((3,) + _piece_shape(r, cc, ax), GRAD_COMM_DTYPE) for _, r, cc, ax in ws],
        input_output_aliases={i: i for i in range(2 * n)},
        compiler_params=pltpu.CompilerParams(has_side_effects=SPLIT_COPY_EFFECT),
    )(*halves, *got, *send_sems, *recv_sems, *after)
    return out[:n], out[n:]


def exchange_reduced(pieces, group):
    ws = [BIG[i] for i in GROUPS[group]]
    n = len(ws)

    def body(*refs):
        ins, theirs = refs[:n], refs[n:2 * n]
        send_sems, recv_sems = refs[2 * n:]
        x, y, c = _place()
        copies = [_start_remote(ins[w], theirs[w], send_sems.at[w], recv_sems.at[w], (x, y, 1 - c)) for w in range(n)]
        for cp in copies:
            cp.wait()

    return pl.pallas_call(
        body, name=f"exchange_reduced_{group}", in_specs=[ANY] * n, out_specs=[ANY] * n,
        out_shape=[jax.ShapeDtypeStruct(_piece_shape(r, cc, ax), F32) for _, r, cc, ax in ws],
        scratch_shapes=[pltpu.SemaphoreType.DMA((n,)), pltpu.SemaphoreType.DMA((n,))],
    )(*pieces)


N_DEV = 8
SMALL_ROWS = 8


def all_reduce_small(packed, behind):
    def body(x_ref, behind_ref, o_ref, gathered, send_sems, recv_sems):
        x, y, c = _place()
        me = 4 * x + 2 * y + c
        gathered[me] = x_ref[...]
        copies = []
        for k in range(1, N_DEV):
            peer = (x ^ (k >> 2), y ^ ((k >> 1) & 1), c ^ (k & 1))
            cp = pltpu.make_async_remote_copy(
                src_ref=x_ref, dst_ref=gathered.at[me], send_sem=send_sems.at[k - 1], recv_sem=recv_sems.at[k - 1],
                device_id=peer, device_id_type=MESH)
            cp.start()
            copies.append(cp)
        for cp in copies:
            cp.wait()
        acc = gathered[0]
        for k in range(1, N_DEV):
            acc = acc + gathered[k]
        o_ref[...] = acc

    vm = pl.BlockSpec(memory_space=pltpu.VMEM)
    return pl.pallas_call(
        body, name="all_reduce_small", in_specs=[vm, ANY], out_specs=vm,
        out_shape=jax.ShapeDtypeStruct((SMALL_ROWS, D_MODEL), F32),
        scratch_shapes=[pltpu.VMEM((N_DEV, SMALL_ROWS, D_MODEL), F32), pltpu.SemaphoreType.DMA((N_DEV - 1,)),
                        pltpu.SemaphoreType.DMA((N_DEV - 1,))],
    )(packed, behind)


def _swiglu_block_fwd(h, norm_g, w_gu, w_down, tag, behind=()):
    n = rmsnorm_fwd(h, norm_g, f"{tag}_norm", behind=behind)
    gu = matmul(n, w_gu, name=f"{tag}_gate_up")
    s = swiglu_fwd(gu, f"{tag}_swiglu")
    h_out = matmul(s, w_down, res=h, scale=0.5, name=f"{tag}_down")
    return h_out, (n, gu, s)


def _swiglu_block_bwd(h, norm_g, w_gu, w_down, saved, dh_out, tag, behind=()):
    n, gu, s = saved
    df = dh_out.astype(MXU_DTYPE)
    d_down = matmul(s, df, ta=True, scale=0.5, out_dtype=GRAD_COMM_DTYPE, name=f"{tag}_d_w_down")
    ds = matmul(df, w_down, tb=True, scale=0.5, behind=behind, name=f"{tag}_d_s")
    dgu = swiglu_bwd(gu, ds, f"{tag}_swiglu_bwd")
    d_gu = matmul(n, dgu, ta=True, out_dtype=GRAD_COMM_DTYPE, name=f"{tag}_d_w_gate_up")
    dn = matmul(dgu, w_gu, tb=True, name=f"{tag}_d_n")
    dh, dg = rmsnorm_bwd(h, norm_g, dn, dh_out, f"{tag}_norm_bwd")
    return dh, dg, d_gu, d_down


def local_step(x, target, small, exchange):
    big = {}
    token, big_ffn1 = exchange.weights("ffn1", x)
    big.update(big_ffn1)
    h1, saved1 = _swiglu_block_fwd(x, small["ffn1_norm"], big["ffn1_w_gate_up"], big["ffn1_w_down"], "ffn1", token)
    token, big_mix = exchange.weights("mix", h1)
    big.update(big_mix)
    u = rmsnorm_fwd(h1, small["mix_norm"], "mix_norm", behind=token)
    z = matmul(u, big["w_in"], name="w_in")
    p = small["hg_lower_bounds"]
    lb = 1.0 / (1.0 + jnp.exp(p[1:2] - p[0:1]))
    y_hg, o_raw, states = hgrn_fwd(z, lb, small["hg_out_norm"], "hgrn_fwd")
    o_att, l_att = zip(*[att_fwd(z, g, f"att_fwd_{g}") for g in range(N_GROUPS)])
    y_att = att_combine_fwd(o_att, l_att, "att_combine")
    bh = matmul(y_hg, big["w_branch_hg"], name="branch_hg")
    ba = matmul(y_att, big["w_branch_att"], name="branch_att")
    merged = merge_fwd(z, bh, ba, "merge")
    h2 = matmul(merged, big["w_out"], res=h1, name="w_out")
    token, big_ffn2 = exchange.weights("ffn2", h2)
    big.update(big_ffn2)
    h3, saved2 = _swiglu_block_fwd(h2, small["ffn2_norm"], big["ffn2_w_gate_up"], big["ffn2_w_down"], "ffn2", token)
    dh3, d_final, loss = final_norm_loss(h3, small["final_norm"], target, "final_norm_loss")

    gs, gb = {"final_norm": d_final}, {}
    dh2, gs["ffn2_norm"], gb["ffn2_w_gate_up"], gb["ffn2_w_down"] = _swiglu_block_bwd(
        h2, small["ffn2_norm"], big["ffn2_w_gate_up"], big["ffn2_w_down"], saved2, dh3, "ffn2")
    token = exchange.gradients("ffn2", gb, dh2)
    dh2_m = dh2.astype(MXU_DTYPE)
    gb["w_out"] = matmul(merged, dh2_m, ta=True, out_dtype=GRAD_COMM_DTYPE, name="d_w_out")
    dmerged = matmul(dh2_m, big["w_out"], tb=True, behind=token, name="d_merged")
    dbh, dba, dgh, dga = merge_bwd(z, bh, ba, dmerged, "merge_bwd")
    gb["w_branch_hg"] = matmul(y_hg, dbh, ta=True, out_dtype=GRAD_COMM_DTYPE, name="d_w_branch_hg")
    gb["w_branch_att"] = matmul(y_att, dba, ta=True, out_dtype=GRAD_COMM_DTYPE, name="d_w_branch_att")
    dy_hg = matmul(dbh, big["w_branch_hg"], tb=True, name="d_y_hg")
    dy_att = matmul(dba, big["w_branch_att"], tb=True, name="d_y_att")
    dq, dfp, di, dog, d_lb, gs["hg_out_norm"] = hgrn_bwd(z, lb, small["hg_out_norm"], o_raw, states, dy_hg, "hgrn_bwd")
    do_att, corr = att_combine_bwd(o_att, l_att, dy_att, "att_combine_bwd")
    d_att = [part for g in range(N_GROUPS) for part in att_bwd(z, l_att[g], do_att[g], corr[g], g, f"att_bwd_{g}")]
    dz = jnp.concatenate([dq, dfp, di, dog, *d_att, dgh, dga], axis=1)
    gb["w_in"] = matmul(u, dz, ta=True, out_dtype=GRAD_COMM_DTYPE, name="d_w_in")
    du = matmul(dz, big["w_in"], tb=True, name="d_u")
    dh1, gs["mix_norm"] = rmsnorm_bwd(h1, small["mix_norm"], du, dh2, "mix_norm_bwd")
    token = exchange.gradients("mix", gb, dh1)
    dp0 = d_lb * lb * (1.0 - lb)
    gs["hg_lower_bounds"] = jnp.concatenate([dp0, -dp0], axis=0)
    dx, gs["ffn1_norm"], gb["ffn1_w_gate_up"], gb["ffn1_w_down"] = _swiglu_block_bwd(
        x, small["ffn1_norm"], big["ffn1_w_gate_up"], big["ffn1_w_down"], saved1, dh1, "ffn1", token)
    exchange.gradients("ffn1", gb, dx)
    return loss[0, 0], dx, gs


SMALL = ("ffn1_norm", "mix_norm", "hg_lower_bounds", "hg_out_norm", "ffn2_norm", "final_norm")
WEIGHTS = ("ffn1_norm", "ffn1_w_gate_up", "ffn1_w_down", "mix_norm", "w_in", "hg_lower_bounds", "hg_out_norm",
           "w_branch_hg", "w_branch_att", "w_out", "ffn2_norm", "ffn2_w_gate_up", "ffn2_w_down", "final_norm")
SMALL_SHAPE = {"ffn1_norm": (1, 1024), "mix_norm": (1, 1024), "hg_lower_bounds": (2, 512), "hg_out_norm": (1, 512),
               "ffn2_norm": (1, 1024), "final_norm": (1024,)}
LOSS_ROW = 6


def _pack_small(vals):
    rows = []
    for n in SMALL:
        r = vals[n].reshape(1, -1).astype(F32)
        rows.append(jnp.pad(r, ((0, 0), (0, D_MODEL - r.shape[1]))))
    rows.append(jnp.zeros((SMALL_ROWS - len(SMALL), D_MODEL), F32))
    return jnp.concatenate(rows, axis=0)


def _unpack_small(packed):
    out = {}
    for i, n in enumerate(SMALL):
        size = int(np.prod(SMALL_SHAPE[n]))
        out[n] = packed[i, :size].reshape(SMALL_SHAPE[n])
    return out


class WeightExchange:
    ORDER = ("ffn1", "mix", "ffn2")

    def __init__(self, shards, core, chip):
        self.core, self.chip = core, chip
        self.scattering = None
        self.reduced = {}
        first = self.ORDER[0]
        self.placed = {BIG[i][0]: place_own_block(shards[BIG[i][0]], chip, *BIG[i][1:], f"place_{BIG[i][0]}")
                       for i in GROUPS[first]}
        self._start_gather(first, self.placed[self._names(first)[0]])
        chip_behind = chip + self.token[0, :1].astype(jnp.int32)
        for group in self.ORDER[1:]:
            for i in GROUPS[group]:
                n, r, cc, ax = BIG[i]
                self.placed[n] = place_own_block(shards[n], chip_behind, r, cc, ax, f"place_{n}")
        self.placed_behind = [self.placed[n] for group in self.ORDER[1:] for n in self._names(group)]

    def _names(self, group):
        return [BIG[i][0] for i in GROUPS[group]]

    def _start_gather(self, group, after):
        send_sems, recv_sems, bufs, self.token = gather_start([self.placed[n] for n in self._names(group)], after, group)
        self.gathering = (group, send_sems, recv_sems, bufs)

    def weights(self, group, h):
        pending, send_sems, recv_sems, bufs = self.gathering
        assert pending == group
        after = self.placed_behind if group == self.ORDER[0] else [h]
        whole = gather_forward(gather_wait(bufs, send_sems, recv_sems, after, group), group)
        later = self.ORDER.index(group) + 1
        behind = []
        if later < len(self.ORDER):
            self._start_gather(self.ORDER[later], whole[0])
            behind = [self.token]
        return behind, dict(zip(self._names(group), whole))

    def _finish_scatter(self, after):
        group, send_sems, recv_sems, halves, got = self.scattering
        halves, got = scatter_wait(halves, got, send_sems, recv_sems, after, group)
        ws = [BIG[i] for i in GROUPS[group]]
        mine = [add_pieces(h, g, self.chip, r, cc, ax, f"add_pieces_{n}") for (n, r, cc, ax), h, g in zip(ws, halves, got)]
        theirs = exchange_reduced(mine, group)
        self.reduced.update({n: (a, b) for (n, *_), a, b in zip(ws, mine, theirs)})
        self.scattering = None
        return theirs[0]

    def gradients(self, group, grads, dh):
        behind = [self._finish_scatter([dh])] if self.scattering is not None else []
        ws = [BIG[i] for i in GROUPS[group]]
        theirs = exchange_halves([grads[n] for n, *_ in ws], group)
        halves = [add_halves(grads[n], t, self.core, r, cc, ax, f"add_halves_{n}") for (n, r, cc, ax), t in zip(ws, theirs)]
        send_sems, recv_sems, halves, got, self.token = scatter_start(halves, group)
        self.scattering = (group, send_sems, recv_sems, halves, got)
        return behind + [self.token]

    def finish(self, after):
        self._finish_scatter(after)
        return self.reduced


def kernel(x, ffn1_norm, ffn1_w_gate_up, ffn1_w_down, mix_norm, w_in, hg_lower_bounds, hg_out_norm, w_branch_hg, w_branch_att, w_out, ffn2_norm, ffn2_w_gate_up, ffn2_w_down, final_norm, loss_target, m_ffn1_norm, m_ffn1_w_gate_up, m_ffn1_w_down, m_mix_norm, m_w_in, m_hg_lower_bounds, m_hg_out_norm, m_w_branch_hg, m_w_branch_att, m_w_out, m_ffn2_norm, m_ffn2_w_gate_up, m_ffn2_w_down, m_final_norm, v_ffn1_norm, v_ffn1_w_gate_up, v_ffn1_w_down, v_mix_norm, v_w_in, v_hg_lower_bounds, v_hg_out_norm, v_w_branch_hg, v_w_branch_att, v_w_out, v_ffn2_norm, v_ffn2_w_gate_up, v_ffn2_w_down, v_final_norm):
    w = dict(ffn1_norm=ffn1_norm, ffn1_w_gate_up=ffn1_w_gate_up, ffn1_w_down=ffn1_w_down, mix_norm=mix_norm, w_in=w_in,
             hg_lower_bounds=hg_lower_bounds, hg_out_norm=hg_out_norm, w_branch_hg=w_branch_hg, w_branch_att=w_branch_att,
             w_out=w_out, ffn2_norm=ffn2_norm, ffn2_w_gate_up=ffn2_w_gate_up, ffn2_w_down=ffn2_w_down, final_norm=final_norm)
    m = dict(ffn1_norm=m_ffn1_norm, ffn1_w_gate_up=m_ffn1_w_gate_up, ffn1_w_down=m_ffn1_w_down, mix_norm=m_mix_norm,
             w_in=m_w_in, hg_lower_bounds=m_hg_lower_bounds, hg_out_norm=m_hg_out_norm, w_branch_hg=m_w_branch_hg,
             w_branch_att=m_w_branch_att, w_out=m_w_out, ffn2_norm=m_ffn2_norm, ffn2_w_gate_up=m_ffn2_w_gate_up,
             ffn2_w_down=m_ffn2_w_down, final_norm=m_final_norm)
    v = dict(ffn1_norm=v_ffn1_norm, ffn1_w_gate_up=v_ffn1_w_gate_up, ffn1_w_down=v_ffn1_w_down, mix_norm=v_mix_norm,
             w_in=v_w_in, hg_lower_bounds=v_hg_lower_bounds, hg_out_norm=v_hg_out_norm, w_branch_hg=v_w_branch_hg,
             w_branch_att=v_w_branch_att, w_out=v_w_out, ffn2_norm=v_ffn2_norm, ffn2_w_gate_up=v_ffn2_w_gate_up,
             ffn2_w_down=v_ffn2_w_down, final_norm=v_final_norm)

    core = lax.axis_index("c").astype(jnp.int32).reshape(1)
    chip = (2 * lax.axis_index("x") + lax.axis_index("y")).astype(jnp.int32).reshape(1)
    exchange = WeightExchange({n: w[n][0] for n, *_ in BIG}, core, chip)
    small = {n: w[n] for n in SMALL}
    small["final_norm"] = final_norm.reshape(1, D_MODEL)

    loss, dx, gs = local_step(x[0], loss_target[0], small, exchange)

    grads, delta, new_m, new_v = {}, {}, {}, {}

    def update(group, core):
        for i in GROUPS[group]:
            n, r, cc, ax = BIG[i]
            a, b = exchange.reduced[n]
            g, d, nm, nv = adamw_halves(w[n][0], a, b, m[n][0], v[n][0], core, r, cc, ax, f"adamw_{n}")
            grads[n], delta[n], new_m[n], new_v[n] = g[None], d[None], nm[None], nv[None]

    core_behind = core + exchange.token[0, :1].astype(jnp.int32)
    update("ffn2", core_behind)
    update("mix", core_behind)
    exchange.finish(after=[delta[BIG[i][0]] for group in ("ffn2", "mix") for i in GROUPS[group]])
    update("ffn1", core)
    packed = _pack_small(gs)
    packed = packed.at[LOSS_ROW].set(jnp.full((D_MODEL,), loss, F32))
    total = all_reduce_small(packed, behind=delta["ffn1_w_down"])
    grads.update(_unpack_small(total))
    loss_total = total[LOSS_ROW, 0]
    pd, pm, pv = adamw(_pack_small({n: w[n] for n in SMALL}), total.at[LOSS_ROW].set(0.0),
                       _pack_small({n: m[n] for n in SMALL}), _pack_small({n: v[n] for n in SMALL}), "adamw_small")
    delta.update(_unpack_small(pd))
    new_m.update(_unpack_small(pm))
    new_v.update(_unpack_small(pv))

    return (loss_total, dx[None], *[grads[n] for n in WEIGHTS], *[delta[n] for n in WEIGHTS],
            *[new_m[n] for n in WEIGHTS], *[new_v[n] for n in WEIGHTS])
```

```python
import numpy as np
import jax
import jax.numpy as jnp
from jax import lax
from jax.experimental import pallas as pl
from jax.experimental.pallas import tpu as pltpu

SEQ = 2048
D_MODEL = 1024
D_FF = 2816
HG_HEADS = 4
HG_DIM = 128
HG_WIDTH = 512
HG_CHUNK = 64
ATT_GROUPS = ((128, 1), (512, 4), (2048, 16))
ATT_HEADS = 8
ATT_WIDTH = 512
ATT_BLOCK = 128
ALIBI_MAX = 8.0
IN_COLS = 8704
EPS = 1e-6
NEG_INF = -1e30
ADAM_LR = 0.001
ADAM_B1 = 0.9
ADAM_B2 = 0.999
ADAM_EPS = 1e-08
ADAM_WD = 0.01
ADAM_STEP = 10

N_CHIPS = 4
MXU_DTYPE = jnp.bfloat16
WEIGHT_COMM_DTYPE = jnp.bfloat16
GRAD_COMM_DTYPE = jnp.bfloat16
MESH = pl.DeviceIdType.MESH
F32 = jnp.float32
HIGHEST = lax.Precision.HIGHEST


def _pick(n, cands):
    for c in cands:
        if n % c == 0:
            return c
    return n


def _sigmoid(x):
    return 1.0 / (1.0 + jnp.exp(-x))


def _dot(a, b, ta=False, tb=False):
    dn = (((0 if ta else 1,), (1 if tb else 0,)), ((), ()))
    return lax.dot_general(a.astype(MXU_DTYPE), b.astype(MXU_DTYPE), dn, preferred_element_type=F32)


def _dot_f32(a, b, ones_on_right=False):
    x = a if ones_on_right else b
    hi = x.astype(jnp.bfloat16)
    rest = x - hi.astype(F32)
    mid = rest.astype(jnp.bfloat16)
    lo = (rest - mid.astype(F32)).astype(jnp.bfloat16)
    if ones_on_right:
        dot = lambda q: jnp.dot(q, b.astype(jnp.bfloat16), preferred_element_type=F32)
    else:
        dot = lambda q: jnp.dot(a.astype(jnp.bfloat16), q, preferred_element_type=F32)
    return dot(hi) + (dot(mid) + dot(lo))


def _split_bf16(x):
    hi = x.astype(jnp.bfloat16)
    return hi, (x - hi.astype(F32)).astype(jnp.bfloat16)


def _hdot(a, b, ta=False, tb=False):
    dn =(((0 if ta else 1,), (1 if tb else 0,)), ((), ()))
    (a_hi, a_lo), (b_hi, b_lo) = _split_bf16(a), _split_bf16(b)
    dot = lambda p, q: lax.dot_general(p, q, dn, preferred_element_type=F32)
    return dot(a_hi, b_hi) + (dot(a_lo, b_hi) + dot(a_hi, b_lo))


MATMUL_VMEM_BYTES = 48 * 1024 * 1024
MATMUL_TILE_BYTES = 36 * 1024 * 1024
MXU_ALIGN = 128


def _divisors(n, most):
    return [t for t in range(min(n, most), 0, -MXU_ALIGN) if n % t == 0 and t % MXU_ALIGN == 0]


def _matmul_tiles(M, N, K, in_bytes, out_bytes, has_res):
    best = None
    for tk in _divisors(K, K):
        nk = K // tk
        for tm in _divisors(M, 2048):
            for tn in _divisors(N, 512):
                tiles = 2 * in_bytes * (tm * tk + tk * tn) + 2 * out_bytes * tm * tn
                tiles += 4 * tm * tn * ((nk > 1) + 2 * has_res)
                if tiles > MATMUL_TILE_BYTES:
                    continue
                traffic = in_bytes * (M * K * (1 if nk == 1 else N // tn) + K * N * (M // tm))
                key = (traffic, -tm * tn * tk)
                if best is None or key < best[0]:
                    best = (key, (tm, tn, tk))
    return best[1]


def matmul(a, b, *, ta=False, tb=False, out_dtype=F32, res=None, scale=1.0, behind=(), name):
    if ta:
        K, M = a.shape
    else:
        M, K = a.shape
    if tb:
        N, K2 = b.shape
    else:
        K2, N = b.shape
    assert K == K2 and a.dtype == b.dtype
    tm, tn, tk = _matmul_tiles(M, N, K, a.dtype.itemsize, jnp.dtype(out_dtype).itemsize, res is not None)
    nk = K // tk

    def finish(r, r_ref, o_ref):
        if scale != 1.0:
            r = r * scale
        if res is not None:
            r = r_ref[...] + r
        o_ref[...] = r.astype(out_dtype)

    def body(*refs):
        a_ref, b_ref = refs[:2]
        r_ref = refs[2] if res is not None else None
        o_ref = refs[2 + (res is not None) + len(behind)]
        if nk == 1:
            finish(_dot(a_ref[...], b_ref[...], ta, tb), r_ref, o_ref)
            return
        acc = refs[-1]
        k = pl.program_id(2)

        @pl.when(k == 0)
        def _():
            acc[...] = jnp.zeros_like(acc)

        acc[...] += _dot(a_ref[...], b_ref[...], ta, tb)

        @pl.when(k == nk - 1)
        def _():
            finish(acc[...], r_ref, o_ref)

    a_spec = pl.BlockSpec((tk, tm), lambda i, j, k: (k, i)) if ta else pl.BlockSpec((tm, tk), lambda i, j, k: (i, k))
    b_spec = pl.BlockSpec((tn, tk), lambda i, j, k: (j, k)) if tb else pl.BlockSpec((tk, tn), lambda i, j, k: (k, j))
    in_specs = [a_spec, b_spec]
    args = [a, b]
    if res is not None:
        in_specs.append(pl.BlockSpec((tm, tn), lambda i, j, k: (i, j)))
        args.append(res)
    for earlier in behind:
        in_specs.append(pl.BlockSpec(memory_space=pl.ANY))
        args.append(earlier)
    return pl.pallas_call(
        body, name=name, grid=(M // tm, N // tn, nk), in_specs=in_specs,
        out_specs=pl.BlockSpec((tm, tn), lambda i, j, k: (i, j)),
        out_shape=jax.ShapeDtypeStruct((M, N), out_dtype),
        scratch_shapes=[pltpu.VMEM((tm, tn), F32)] if nk > 1 else [],
        compiler_params=pltpu.CompilerParams(dimension_semantics=("parallel", "parallel", "arbitrary"),
                                             vmem_limit_bytes=MATMUL_VMEM_BYTES),
    )(*args)


ROW_TILE = 256


def rmsnorm_fwd(x, g, name, behind=()):
    def body(x_ref, g_ref, *refs):
        n_ref = refs[-1]
        xv = x_ref[...]
        r = lax.rsqrt(jnp.mean(xv * xv, axis=-1, keepdims=True) + EPS)
        n_ref[...] = ((xv * r) * g_ref[...]).astype(n_ref.dtype)

    order = list(behind)
    return pl.pallas_call(
        body, name=name, grid=(SEQ // ROW_TILE,),
        in_specs=[pl.BlockSpec((ROW_TILE, D_MODEL), lambda i: (i, 0)), pl.BlockSpec((1, D_MODEL), lambda i: (0, 0))]
        + [pl.BlockSpec(memory_space=pl.ANY)] * len(order),
        out_specs=pl.BlockSpec((ROW_TILE, D_MODEL), lambda i: (i, 0)),
        out_shape=jax.ShapeDtypeStruct((SEQ, D_MODEL), MXU_DTYPE),
    )(x, g, *order)


def rmsnorm_bwd(x, g, dn, dres, name):
    def body(x_ref, g_ref, dn_ref, dr_ref, dx_ref, dg_ref):
        xv = x_ref[...]
        r = lax.rsqrt(jnp.mean(xv * xv, axis=-1, keepdims=True) + EPS)
        xh = xv * r
        dnv = dn_ref[...]

        @pl.when(pl.program_id(0) == 0)
        def _():
            dg_ref[...] = jnp.zeros_like(dg_ref)

        dg_ref[...] += jnp.sum(dnv * xh, axis=0, keepdims=True)
        dxh = dnv * g_ref[...]
        dx_ref[...] = dr_ref[...] + r * (dxh - xh * jnp.mean(dxh * xh, axis=-1, keepdims=True))

    row = pl.BlockSpec((ROW_TILE, D_MODEL), lambda i: (i, 0))
    vec = pl.BlockSpec((1, D_MODEL), lambda i: (0, 0))
    return pl.pallas_call(
        body, name=name, grid=(SEQ // ROW_TILE,), in_specs=[row, vec, row, row], out_specs=[row, vec],
        out_shape=[jax.ShapeDtypeStruct((SEQ, D_MODEL), F32), jax.ShapeDtypeStruct((1, D_MODEL), F32)],
        compiler_params=pltpu.CompilerParams(dimension_semantics=("arbitrary",)),
    )(x, g, dn, dres)


def final_norm_loss(h, g, target, name):
    def body(h_ref, g_ref, t_ref, dh_ref, dg_ref, loss_ref):
        xv = h_ref[...]
        r = lax.rsqrt(jnp.mean(xv * xv, axis=-1, keepdims=True) + EPS)
        xh = xv * r
        gv = g_ref[...]
        e = xh * gv - t_ref[...]

        @pl.when(pl.program_id(0) == 0)
        def _():
            dg_ref[...] = jnp.zeros_like(dg_ref)
            loss_ref[...] = jnp.zeros_like(loss_ref)

        part = 0.5 * jnp.sum(jnp.sum(e * e, axis=-1, keepdims=True) * (1.0 / D_MODEL), axis=0, keepdims=True)
        loss_ref[...] += jnp.broadcast_to(part, loss_ref.shape)
        dout = e * (1.0 / D_MODEL)
        dg_ref[...] += jnp.sum(dout * xh, axis=0, keepdims=True)
        dxh = dout * gv
        dh_ref[...] = r * (dxh - xh * jnp.mean(dxh * xh, axis=-1, keepdims=True))

    row = pl.BlockSpec((ROW_TILE, D_MODEL), lambda i: (i, 0))
    vec = pl.BlockSpec((1, D_MODEL), lambda i: (0, 0))
    return pl.pallas_call(
        body, name=name, grid=(SEQ // ROW_TILE,), in_specs=[row, vec, row],
        out_specs=[row, vec, pl.BlockSpec((8, 128), lambda i: (0, 0))],
        out_shape=[jax.ShapeDtypeStruct((SEQ, D_MODEL), F32), jax.ShapeDtypeStruct((1, D_MODEL), F32),
                   jax.ShapeDtypeStruct((8, 128), F32)],
        compiler_params=pltpu.CompilerParams(dimension_semantics=("arbitrary",)),
    )(h, g, target)


FF_TILE = D_FF // 2


def swiglu_fwd(gu, name):
    def body(a_ref, b_ref, s_ref):
        a = a_ref[...]
        s_ref[...] = (a * _sigmoid(a) * b_ref[...]).astype(s_ref.dtype)

    return pl.pallas_call(
        body, name=name, grid=(SEQ // ROW_TILE, 2),
        in_specs=[pl.BlockSpec((ROW_TILE, FF_TILE), lambda i, j: (i, j)),
                  pl.BlockSpec((ROW_TILE, FF_TILE), lambda i, j: (i, j + 2))],
        out_specs=pl.BlockSpec((ROW_TILE, FF_TILE), lambda i, j: (i, j)),
        out_shape=jax.ShapeDtypeStruct((SEQ, D_FF), MXU_DTYPE),
    )(gu, gu)


def swiglu_bwd(gu, ds, name):
    rows = ROW_TILE // 2

    def body(a_ref, b_ref, ds_ref, o_ref):
        a = a_ref[...]
        sg = _sigmoid(a)
        dsv = ds_ref[...]
        o_ref[:, :D_FF] = (dsv * b_ref[...] * (sg * (1.0 + a * (1.0 - sg)))).astype(o_ref.dtype)
        o_ref[:, D_FF:] = (dsv * a * sg).astype(o_ref.dtype)

    return pl.pallas_call(
        body, name=name, grid=(SEQ // rows,),
        in_specs=[pl.BlockSpec((rows, D_FF), lambda i: (i, 0)), pl.BlockSpec((rows, D_FF), lambda i: (i, 1)),
                  pl.BlockSpec((rows, D_FF), lambda i: (i, 0))],
        out_specs=pl.BlockSpec((rows, 2 * D_FF), lambda i: (i, 0)),
        out_shape=jax.ShapeDtypeStruct((SEQ, 2 * D_FF), MXU_DTYPE), compiler_params=SUM_PARAMS,
    )(gu, gu, ds)


GATE_HG_BLK = 6656 // 512
GATE_ATT_BLK = 7680 // 512


def merge_fwd(z, bh, ba, name):
    def body(gh_ref, ga_ref, bh_ref, ba_ref, o_ref):
        o_ref[...] = (_sigmoid(gh_ref[...]) * bh_ref[...] + _sigmoid(ga_ref[...]) * ba_ref[...]).astype(o_ref.dtype)

    blk = pl.BlockSpec((ROW_TILE, 512), lambda i, j: (i, j))
    return pl.pallas_call(
        body, name=name, grid=(SEQ // ROW_TILE, 2),
        in_specs=[pl.BlockSpec((ROW_TILE, 512), lambda i, j: (i, GATE_HG_BLK + j)),
                  pl.BlockSpec((ROW_TILE, 512), lambda i, j: (i, GATE_ATT_BLK + j)), blk, blk],
        out_specs=blk, out_shape=jax.ShapeDtypeStruct((SEQ, D_MODEL), MXU_DTYPE),
    )(z, z, bh, ba)


def merge_bwd(z, bh, ba, dm, name):
    def body(gh_ref, ga_ref, bh_ref, ba_ref, dm_ref, dbh_ref, dba_ref, dgh_ref, dga_ref):
        dmv = dm_ref[...]
        sh = _sigmoid(gh_ref[...])
        sa = _sigmoid(ga_ref[...])
        dbh_ref[...] = (dmv * sh).astype(dbh_ref.dtype)
        dba_ref[...] = (dmv * sa).astype(dba_ref.dtype)
        dgh_ref[...] = (dmv * bh_ref[...] * (sh * (1.0 - sh))).astype(dgh_ref.dtype)
        dga_ref[...] = (dmv * ba_ref[...] * (sa * (1.0 - sa))).astype(dga_ref.dtype)

    blk = pl.BlockSpec((ROW_TILE, 512), lambda i, j: (i, j))
    out = jax.ShapeDtypeStruct((SEQ, D_MODEL), MXU_DTYPE)
    return pl.pallas_call(
        body, name=name, grid=(SEQ // ROW_TILE, 2),
        in_specs=[pl.BlockSpec((ROW_TILE, 512), lambda i, j: (i, GATE_HG_BLK + j)),
                  pl.BlockSpec((ROW_TILE, 512), lambda i, j: (i, GATE_ATT_BLK + j)), blk, blk, blk],
        out_specs=[blk, blk, blk, blk], out_shape=[out, out, out, out],
    )(z, z, bh, ba, dm)


N_CHUNKS = SEQ // HG_CHUNK
HG_STEP_CHUNKS = 4


def _hgrn_gates(q, fp, lb):
    C = HG_CHUNK
    sg = _sigmoid(fp)
    f = lb + (1.0 - lb) * sg
    lf = jnp.log(f)
    row = lax.broadcasted_iota(jnp.int32, (C, C), 0)
    col = lax.broadcasted_iota(jnp.int32, (C, C), 1)
    causal = row >= col
    G = _dot_f32(causal.astype(F32), lf)
    eG = jnp.exp(G)
    enG = jnp.exp(-G)
    qg = q * eG
    kg = (1.0 - f) * enG
    A = jnp.where(causal, _hdot(qg, kg, tb=True), 0.0)
    egl = jnp.exp(jnp.sum(lf, axis=0, keepdims=True))
    return sg, f, causal, eG, enG, qg, kg, A, egl


def hgrn_fwd(z, lb, gain, name):
    C, K = HG_CHUNK, HG_DIM

    def body(q_ref, f_ref, v_ref, og_ref, p_ref, g_ref, y_ref, o_ref, st_ref, state):
        @pl.when(pl.program_id(0) == 0)
        def _():
            state[...] = jnp.zeros_like(state)

        for cc in range(HG_STEP_CHUNKS):
            rows = pl.ds(cc * C, C)
            for h in range(HG_HEADS):
                hd = pl.ds(h * K, K)
                v = v_ref[rows, hd]
                _, _, _, _, _, qg, kg, A, egl = _hgrn_gates(q_ref[rows, hd], f_ref[rows, hd], p_ref[:, hd])
                st = state[h]
                st_ref[h, cc] = st
                o = _hdot(A, v) + _hdot(qg, st, tb=True)
                state[h] = st * egl + _hdot(v, kg * egl, ta=True)
                o_ref[rows, hd] = o
                rs = lax.rsqrt(jnp.mean(o * o, axis=-1, keepdims=True) + EPS)
                og = og_ref[rows, hd]
                y_ref[rows, hd] = (((o * rs) * g_ref[:, hd]) * (og * _sigmoid(og))).astype(y_ref.dtype)

    R = HG_STEP_CHUNKS * C

    def zcol(section):
        return pl.BlockSpec((R, HG_WIDTH), lambda c: (c, section))

    vec = pl.BlockSpec((1, HG_WIDTH), lambda c: (0, 0))
    blk = pl.BlockSpec((R, HG_WIDTH), lambda c: (c, 0))
    return pl.pallas_call(
        body, name=name, grid=(N_CHUNKS // HG_STEP_CHUNKS,),
        in_specs=[zcol(0), zcol(1), zcol(2), zcol(3), vec, vec],
        out_specs=[blk, blk, pl.BlockSpec((HG_HEADS, HG_STEP_CHUNKS, K, K), lambda c: (0, c, 0, 0))],
        out_shape=[jax.ShapeDtypeStruct((SEQ, HG_WIDTH), MXU_DTYPE), jax.ShapeDtypeStruct((SEQ, HG_WIDTH), F32),
                   jax.ShapeDtypeStruct((HG_HEADS, N_CHUNKS, K, K), F32)],
        scratch_shapes=[pltpu.VMEM((HG_HEADS, K, K), F32)],
        compiler_params=pltpu.CompilerParams(dimension_semantics=("arbitrary",)),
    )(z, z, z, z, lb, gain)


def hgrn_bwd(z, lb, gain, o_raw, states, dy, name):
    C, K = HG_CHUNK, HG_DIM

    def body(q_ref, f_ref, v_ref, og_ref, p_ref, g_ref, o_ref, st_ref, dy_ref,
             dq_ref, dfp_ref, dv_ref, dog_ref, dlb_ref, dgain_ref, dstate):
        @pl.when(pl.program_id(0) == 0)
        def _():
            dstate[...] = jnp.zeros_like(dstate)
            dlb_ref[...] = jnp.zeros_like(dlb_ref)
            dgain_ref[...] = jnp.zeros_like(dgain_ref)

        last = lax.broadcasted_iota(jnp.int32, (C, K), 0) == C - 1
        row = lax.broadcasted_iota(jnp.int32, (C, C), 0)
        col = lax.broadcasted_iota(jnp.int32, (C, C), 1)
        anti_causal = (col >= row).astype(F32)
        for cc in reversed(range(HG_STEP_CHUNKS)):
            rows = pl.ds(cc * C, C)
            for h in range(HG_HEADS):
                hd = pl.ds(h * K, K)
                v = v_ref[rows, hd]
                lb = p_ref[:, hd]
                sg, f, causal, eG, enG, qg, kg, A, egl = _hgrn_gates(q_ref[rows, hd], f_ref[rows, hd], lb)
                kd = kg * egl
                st = st_ref[h, cc]
                dst = dstate[h]
                o = o_ref[rows, hd]
                og = og_ref[rows, hd]
                gain_v = g_ref[:, hd]
                dyv = dy_ref[rows, hd]
                rs = lax.rsqrt(jnp.mean(o * o, axis=-1, keepdims=True) + EPS)
                on = o * rs
                sgo = _sigmoid(og)
                silu = og * sgo
                dog_ref[rows, hd] = (dyv * (on * gain_v) * (sgo * (1.0 + og * (1.0 - sgo)))).astype(dog_ref.dtype)
                dgain_ref[:, hd] += jnp.sum(dyv * silu * on, axis=0, keepdims=True)
                don = dyv * gain_v * silu
                do = rs * (don - on * jnp.mean(don * on, axis=-1, keepdims=True))
                dA = jnp.where(causal, _hdot(do, v, tb=True), 0.0)
                dv_ref[rows, hd] = (_hdot(A, do, ta=True) + _hdot(kd, dst, tb=True)).astype(dv_ref.dtype)
                dqg = _hdot(dA, kg) + _hdot(do, st)
                dkg = _hdot(dA, qg, ta=True)
                dkd = _hdot(v, dst)
                dstate[h] = dst * egl + _hdot(do, qg, ta=True)
                dgl = jnp.sum(st * dst, axis=0, keepdims=True) * egl
                dq_ref[rows, hd] = (dqg * eG).astype(dq_ref.dtype)
                dk = dkg * enG + dkd * (enG * egl)
                dG = dqg * qg - dkg * kg - dkd * kd
                extra = jnp.sum(dkd * kd, axis=0, keepdims=True) + dgl
                dG = dG + jnp.where(last, extra, 0.0)
                dlf = _dot_f32(anti_causal, dG)
                df = dlf / f - dk
                dfp_ref[rows, hd] = (df * (1.0 - lb) * (sg * (1.0 - sg))).astype(dfp_ref.dtype)
                dlb_ref[:, hd] += jnp.sum(df * (1.0 - sg), axis=0, keepdims=True)

    R = HG_STEP_CHUNKS * C
    n_steps = N_CHUNKS // HG_STEP_CHUNKS

    def rc(c):
        return n_steps - 1 - c

    def zcol(section):
        return pl.BlockSpec((R, HG_WIDTH), lambda c: (rc(c), section))

    vec = pl.BlockSpec((1, HG_WIDTH), lambda c: (0, 0))
    blk = pl.BlockSpec((R, HG_WIDTH), lambda c: (rc(c), 0))
    out = jax.ShapeDtypeStruct((SEQ, HG_WIDTH), MXU_DTYPE)
    small = jax.ShapeDtypeStruct((1, HG_WIDTH), F32)
    return pl.pallas_call(
        body, name=name, grid=(n_steps,),
        in_specs=[zcol(0), zcol(1), zcol(2), zcol(3), vec, vec, blk,
                  pl.BlockSpec((HG_HEADS, HG_STEP_CHUNKS, K, K), lambda c: (0, rc(c), 0, 0)), blk],
        out_specs=[blk, blk, blk, blk, vec, vec],
        out_shape=[out, out, out, out, small, small],
        scratch_shapes=[pltpu.VMEM((HG_HEADS, K, K), F32)],
        compiler_params=pltpu.CompilerParams(dimension_semantics=("arbitrary",)),
    )(z, z, z, z, lb, gain, o_raw, states, dy)


N_GROUPS = len(ATT_GROUPS)
HEAD_PAIRS = ATT_WIDTH // 128
ATT_COL0 = 4 * HG_WIDTH
UNROLLED_UNITS = 4
ATT_SLAB_BLOCKS = 4


def _alibi_coef():
    n = N_GROUPS * ATT_HEADS
    slopes = np.exp2(-ALIBI_MAX * np.arange(1, n + 1, dtype=np.float32) / n).astype(np.float32)
    dil = np.repeat(np.array([d for _, d in ATT_GROUPS], np.float32), ATT_HEADS)
    return jnp.asarray(slopes * dil, F32)


def _for_each_unit(n, fn):
    if n <= UNROLLED_UNITS:
        for u in range(n):
            fn(u)
    else:
        def group(i, carry):
            for j in range(UNROLLED_UNITS):
                fn(i * UNROLLED_UNITS + j)
            return carry
        lax.fori_loop(0, n // UNROLLED_UNITS, group, 0)


def _att_specs(g):
    B = ATT_BLOCK
    d = ATT_GROUPS[g][1]
    blocks = ATT_SLAB_BLOCKS if d == 1 else 1
    R = B * d * blocks
    n_slabs = SEQ // R
    multi = SEQ // d > B
    col0 = (ATT_COL0 + g * 3 * ATT_WIDTH) // 128

    def cur(col):
        return pl.BlockSpec((R, 128), lambda hp, s: (s, col + hp))

    def prev(col):
        return pl.BlockSpec((R, 128), lambda hp, s: (jnp.maximum(s - 1, 0), col + hp))

    def nxt(col):
        return pl.BlockSpec((R, 128), lambda hp, s: (jnp.minimum(s + 1, n_slabs - 1), col + hp))

    def unit(u, s):
        if d > 1:
            rows = pl.ds(u, B, stride=d)
            return rows, False, rows, jnp.where(s == 0, B, 0), False, rows, jnp.where(s == n_slabs - 1, B, 0)
        rows = pl.ds(u * B, B)
        inner_prev, inner_next = u > 0, u < blocks - 1
        return (rows, inner_prev, pl.ds((u - 1) * B if inner_prev else (blocks - 1) * B, B),
                0 if inner_prev else jnp.where(s == 0, B, 0),
                inner_next, pl.ds((u + 1) * B if inner_next else 0, B),
                0 if inner_next else jnp.where(s == n_slabs - 1, B, 0))

    return d * blocks, R, n_slabs, multi, col0, cur, prev, nxt, unit


def _head_lanes(j):
    lane = lax.broadcasted_iota(jnp.int32, (ATT_BLOCK, 128), 1)
    return (lane >= 64 * j) & (lane < 64 * (j + 1))


def _lane_value(x, sel):
    return jnp.max(jnp.where(sel, x, -3e38), axis=-1, keepdims=True)


def _stack_heads(x, sel0):
    return jnp.concatenate([jnp.where(sel0, x, 0.0), jnp.where(sel0, 0.0, x)], axis=0)


def _stack_values(x, sel0, lanes):
    swapped = pltpu.roll(x, 64, 1)
    stacked = jnp.concatenate([jnp.where(sel0, x, swapped), jnp.where(sel0, swapped, x)], axis=0)
    return stacked if lanes == 128 else jnp.concatenate([stacked] * (lanes // 128), axis=1)


def _pair_coef(coef_ref, g, hp):
    row = lax.broadcasted_iota(jnp.int32, (2 * ATT_BLOCK, 1), 0)
    first = g * ATT_HEADS + hp * 2
    return jnp.where(row < ATT_BLOCK, coef_ref[first], coef_ref[first + 1])


def _band(with_prev, first_key):
    B = ATT_BLOCK
    keys = 2 * B if with_prev else B
    qi = jnp.bitwise_and(lax.broadcasted_iota(jnp.int32, (2 * B, keys), 0), B - 1)
    kj = lax.broadcasted_iota(jnp.int32, (2 * B, keys), 1)
    delta = qi + (B if with_prev else 0) - kj
    valid = (delta >= 0) & (delta <= B)
    if with_prev:
        valid = valid & (kj >= first_key)
    return valid, delta.astype(F32)


def _band_next(first_key):
    B = ATT_BLOCK
    qi = jnp.bitwise_and(lax.broadcasted_iota(jnp.int32, (2 * B, B), 0), B - 1)
    kj = lax.broadcasted_iota(jnp.int32, (2 * B, B), 1)
    delta = qi + B - kj
    return (delta <= B) & (kj >= first_key), delta.astype(F32)


def att_fwd(z, g, name):
    B = ATT_BLOCK
    n_units, R, n_slabs, has_prev, col0, cur, prev, _, unit = _att_specs(g)

    def body(coef_ref, *refs):
        if has_prev:
            q_ref, kc_ref, vc_ref, kp_ref, vp_ref, o_ref, l_ref = refs
        else:
            q_ref, kc_ref, vc_ref, o_ref, l_ref = refs
        hp, s = pl.program_id(0), pl.program_id(1)
        cf2 = _pair_coef(coef_ref, g, hp)
        sel0 = _head_lanes(0)

        def one(u):
            rows, inner_prev, prev_rows, first_key, _, _, _ = unit(u, s)
            valid, dist = _band(has_prev, first_key)
            q2 = _stack_heads(q_ref[rows, :], sel0)
            kk, vv = kc_ref[rows, :], vc_ref[rows, :]
            if has_prev:
                k_from, v_from = (kc_ref, vc_ref) if inner_prev else (kp_ref, vp_ref)
                kk = jnp.concatenate([k_from[prev_rows, :], kk], axis=0)
                vv = jnp.concatenate([v_from[prev_rows, :], vv], axis=0)
            sc = jnp.where(valid, _dot(q2, kk, tb=True) * 0.125 - cf2 * dist, NEG_INF)
            mx = jnp.max(sc, axis=-1, keepdims=True)
            e = jnp.exp(sc - mx)
            den = jnp.sum(e, axis=-1, keepdims=True)
            o2 = _dot(e * (1.0 / den), vv)
            lse2 = mx + jnp.log(den)
            o_ref[rows, :] = jnp.where(sel0, o2[:B], o2[B:])
            l_ref[rows, :] = jnp.where(sel0, lse2[:B], lse2[B:])

        _for_each_unit(n_units, one)

    in_specs = [pl.BlockSpec(memory_space=pltpu.SMEM), cur(col0), cur(col0 + 4), cur(col0 + 8)]
    args = [_alibi_coef(), z, z, z]
    if has_prev:
        in_specs += [prev(col0 + 4), prev(col0 + 8)]
        args += [z, z]
    out = jax.ShapeDtypeStruct((SEQ, ATT_WIDTH), F32)
    return pl.pallas_call(
        body, name=name, grid=(HEAD_PAIRS, n_slabs), in_specs=in_specs,
        out_specs=[cur(0), cur(0)], out_shape=[out, out],
        compiler_params=pltpu.CompilerParams(dimension_semantics=("parallel", "arbitrary")),
    )(*args)


def att_bwd(z, l, do, corr, g, name):
    B = ATT_BLOCK
    n_units, R, n_slabs, neighbours, col0, cur, prev, nxt, unit = _att_specs(g)

    def body(coef_ref, *refs):
        if neighbours:
            (q_ref, kc_ref, vc_ref, l_ref, do_ref, cr_ref, kp_ref, vp_ref, qn_ref, ln_ref, don_ref, crn_ref,
             dq_ref, dk_ref, dv_ref, dq_sc, dk_sc, dv_sc) = refs
        else:
            q_ref, kc_ref, vc_ref, l_ref, do_ref, cr_ref, dq_ref, dk_ref, dv_ref, dq_sc, dk_sc, dv_sc = refs
        hp, s = pl.program_id(0), pl.program_id(1)
        cf2 = _pair_coef(coef_ref, g, hp)
        sel0 = _head_lanes(0)
        own = slice(B, 2 * B) if neighbours else slice(0, B)

        def one(u):
            rows, inner_prev, prev_rows, first_key, inner_next, next_rows, first_key_n = unit(u, s)
            valid, dist = _band(neighbours, first_key)
            kc, vc = kc_ref[rows, :], vc_ref[rows, :]
            kk, vv = kc, vc
            if neighbours:
                k_from, v_from = (kc_ref, vc_ref) if inner_prev else (kp_ref, vp_ref)
                kk = jnp.concatenate([k_from[prev_rows, :], kc], axis=0)
                vv = jnp.concatenate([v_from[prev_rows, :], vc], axis=0)
            q2, do2 = _stack_heads(q_ref[rows, :], sel0), _stack_heads(do_ref[rows, :], sel0)
            keys = kk.shape[0]
            lse2, cr2 = _stack_values(l_ref[rows, :], sel0, keys), _stack_values(cr_ref[rows, :], sel0, keys)
            p = jnp.exp(jnp.where(valid, _dot(q2, kk, tb=True) * 0.125 - cf2 * dist, NEG_INF) - lse2)
            ds = p * (_dot(do2, vv, tb=True) + cr2)
            dq2 = _dot(ds, kk)
            dk = _dot(ds, q2, ta=True)[own]
            dv = _dot(p, do2, ta=True)[own]
            if neighbours:
                valid_n, dist_n = _band_next(first_key_n)
                q_from, l_from, do_from, cr_from = ((q_ref, l_ref, do_ref, cr_ref) if inner_next
                                                    else (qn_ref, ln_ref, don_ref, crn_ref))
                qn2, don2 = _stack_heads(q_from[next_rows, :], sel0), _stack_heads(do_from[next_rows, :], sel0)
                lse_n2 = _stack_values(l_from[next_rows, :], sel0, B)
                cr_n2 = _stack_values(cr_from[next_rows, :], sel0, B)
                p_n = jnp.exp(jnp.where(valid_n, _dot(qn2, kc, tb=True) * 0.125 - cf2 * dist_n, NEG_INF) - lse_n2)
                ds_n = p_n * (_dot(don2, vc, tb=True) + cr_n2)
                dk = dk + _dot(ds_n, qn2, ta=True)
                dv = dv + _dot(p_n, don2, ta=True)
            dq_sc[rows, :] = jnp.where(sel0, dq2[:B], dq2[B:]) * 0.125
            dk_sc[rows, :] = dk * 0.125
            dv_sc[rows, :] = dv

        _for_each_unit(n_units, one)
        dq_ref[...] = dq_sc[...].astype(dq_ref.dtype)
        dk_ref[...] = dk_sc[...].astype(dk_ref.dtype)
        dv_ref[...] = dv_sc[...].astype(dv_ref.dtype)

    in_specs = [pl.BlockSpec(memory_space=pltpu.SMEM), cur(col0), cur(col0 + 4), cur(col0 + 8), cur(0), cur(0), cur(0)]
    args = [_alibi_coef(), z, z, z, l, do, corr]
    if neighbours:
        in_specs += [prev(col0 + 4), prev(col0 + 8), nxt(col0), nxt(0), nxt(0), nxt(0)]
        args += [z, z, z, l, do, corr]
    out = jax.ShapeDtypeStruct((SEQ, ATT_WIDTH), MXU_DTYPE)
    return pl.pallas_call(
        body, name=name, grid=(HEAD_PAIRS, n_slabs), in_specs=in_specs,
        out_specs=[cur(0)] * 3, out_shape=[out] * 3,
        scratch_shapes=[pltpu.VMEM((R, 128), F32)] * 3,
        compiler_params=pltpu.CompilerParams(dimension_semantics=("parallel", "arbitrary"),
                                             vmem_limit_bytes=MATMUL_VMEM_BYTES),
    )(*args)


def _head_sum(x):
    i = lax.broadcasted_iota(jnp.int32, (128, 128), 0) // 64
    j = lax.broadcasted_iota(jnp.int32, (128, 128), 1) // 64
    return _dot_f32(x, (i == j).astype(F32), ones_on_right=True)


def _group_weights(l0, l1, l2):
    mx = jnp.maximum(jnp.maximum(l0, l1), l2)
    e0, e1, e2 = jnp.exp(l0 - mx), jnp.exp(l1 - mx), jnp.exp(l2 - mx)
    inv = 1.0 / (e0 + e1 + e2)
    return e0 * inv, e1 * inv, e2 * inv


def att_combine_fwd(o, l, name):
    def body(o0, o1, o2, l0, l1, l2, y_ref):
        w0, w1, w2 = _group_weights(l0[...], l1[...], l2[...])
        y_ref[...] = (o0[...] * w0 + o1[...] * w1 + o2[...] * w2).astype(y_ref.dtype)

    blk = pl.BlockSpec((ROW_TILE, ATT_WIDTH), lambda i: (i, 0))
    return pl.pallas_call(
        body, name=name, grid=(SEQ // ROW_TILE,), in_specs=[blk] * 6, out_specs=blk,
        out_shape=jax.ShapeDtypeStruct((SEQ, ATT_WIDTH), MXU_DTYPE),
    )(*o, *l)


def att_combine_bwd(o, l, dy, name):
    def body(o0, o1, o2, l0, l1, l2, dy_ref, do0, do1, do2, cr0, cr1, cr2):
        w = _group_weights(l0[...], l1[...], l2[...])
        dyv = dy_ref[...]
        dw = [_head_sum(dyv * o_ref[...]) for o_ref in (o0, o1, o2)]
        tot = w[0] * dw[0] + w[1] * dw[1] + w[2] * dw[2]
        for g, (do_ref, cr_ref) in enumerate(((do0, cr0), (do1, cr1), (do2, cr2))):
            do_ref[...] = dyv * w[g]
            cr_ref[...] = -w[g] * tot

    blk = pl.BlockSpec((ROW_TILE, 128), lambda i, j: (i, j))
    out = jax.ShapeDtypeStruct((SEQ, ATT_WIDTH), F32)
    res = pl.pallas_call(
        body, name=name, grid=(SEQ // ROW_TILE, HEAD_PAIRS), in_specs=[blk] * 7, out_specs=[blk] * 6, out_shape=[out] * 6,
    )(*o, *l, dy)
    return res[:N_GROUPS], res[N_GROUPS:]


SUM_ROW_TILES = (1024, 512, 256, 128, 64, 32, 16)
SUM_TILE_BYTES = 24 * 1024 * 1024
SUM_PARAMS = pltpu.CompilerParams(vmem_limit_bytes=MATMUL_VMEM_BYTES)


def _row_tile(rows, cols, operands):
    fit = [t for t in SUM_ROW_TILES if rows % t == 0]
    return next((t for t in fit if 2 * 4 * operands * t * cols <= SUM_TILE_BYTES), fit[-1])


def _shard_shape(rows, cols, axis):
    return (rows // N_CHIPS, cols) if axis == 0 else (rows, cols // N_CHIPS)


def _half_shape(rows, cols, axis):
    return (rows, cols // 2) if axis == 0 else (rows // 2, cols)


def _piece_shape(rows, cols, axis):
    return (rows // N_CHIPS, cols // 2) if axis == 0 else (rows // 2, cols // N_CHIPS)


def place_own_block(shard, chip, rows, cols, axis, name):
    sr, sc = _shard_shape(rows, cols, axis)
    tr = _row_tile(sr, sc, 2)

    def body(chip_ref, s_ref, o_ref):
        o_ref[...] = s_ref[...].astype(o_ref.dtype)

    if axis == 0:
        out_map = lambda i, chip_ref: (chip_ref[0] * (sr // tr) + i, 0)
    else:
        out_map = lambda i, chip_ref: (i, chip_ref[0])
    return pl.pallas_call(
        body, name=name, out_shape=jax.ShapeDtypeStruct((rows, cols), WEIGHT_COMM_DTYPE), compiler_params=SUM_PARAMS,
        grid_spec=pltpu.PrefetchScalarGridSpec(
            num_scalar_prefetch=1, grid=(sr // tr,), in_specs=[pl.BlockSpec((tr, sc), lambda i, chip_ref: (i, 0))],
            out_specs=pl.BlockSpec((tr, sc), out_map)),
    )(chip, shard)


def add_halves(g, theirs, core, rows, cols, axis, name):
    hr, hc = _half_shape(rows, cols, axis)
    tr = _row_tile(hr, hc, 3)

    def body(core_ref, g_ref, t_ref, o_ref):
        o_ref[...] = (g_ref[...].astype(F32) + t_ref[...].astype(F32)).astype(o_ref.dtype)

    if axis == 0:
        g_map = lambda i, core_ref: (i, core_ref[0])
    else:
        g_map = lambda i, core_ref: (core_ref[0] * (hr // tr) + i, 0)
    blk = pl.BlockSpec((tr, hc), lambda i, core_ref: (i, 0))
    return pl.pallas_call(
        body, name=name, out_shape=jax.ShapeDtypeStruct((hr, hc), GRAD_COMM_DTYPE), compiler_params=SUM_PARAMS,
        grid_spec=pltpu.PrefetchScalarGridSpec(
            num_scalar_prefetch=1, grid=(hr // tr,), in_specs=[pl.BlockSpec((tr, hc), g_map), blk], out_specs=blk),
    )(core, g, theirs)


def add_pieces(half, got, chip, rows, cols, axis, name):
    hr, _ = _half_shape(rows, cols, axis)
    pr, pc = _piece_shape(rows, cols, axis)
    tr = _row_tile(pr, pc, 5)

    def body(chip_ref, h_ref, got_ref, o_ref):
        o_ref[...] = (h_ref[...].astype(F32) + got_ref[0].astype(F32) + got_ref[1].astype(F32) + got_ref[2].astype(F32))

    if axis == 0:
        h_map = lambda i, chip_ref: (chip_ref[0] * (pr // tr) + i, 0)
    else:
        h_map = lambda i, chip_ref: (i, chip_ref[0])
    return pl.pallas_call(
        body, name=name, out_shape=jax.ShapeDtypeStruct((pr, pc), F32), compiler_params=SUM_PARAMS,
        grid_spec=pltpu.PrefetchScalarGridSpec(
            num_scalar_prefetch=1, grid=(pr // tr,),
            in_specs=[pl.BlockSpec((tr, pc), h_map), pl.BlockSpec((3, tr, pc), lambda i, chip_ref: (0, i, 0))],
            out_specs=pl.BlockSpec((tr, pc), lambda i, chip_ref: (i, 0))),
    )(chip, half, got)


def _adamw_math(w, g, m, v):
    nm = ADAM_B1 * m + (1.0 - ADAM_B1) * g
    nv = ADAM_B2 * v + (1.0 - ADAM_B2) * (g * g)
    m_hat = nm / (1.0 - ADAM_B1 ** ADAM_STEP)
    v_hat = nv / (1.0 - ADAM_B2 ** ADAM_STEP)
    return -ADAM_LR * (m_hat / (jnp.sqrt(v_hat) + ADAM_EPS) + ADAM_WD * w), nm, nv


def adamw(w, g, m, v, name):
    R, Cc = w.shape
    tr = _pick(R, (256, 128, 64, 8))

    def body(w_ref, g_ref, m_ref, v_ref, d_ref, nm_ref, nv_ref):
        d_ref[...], nm_ref[...], nv_ref[...] = _adamw_math(w_ref[...], g_ref[...], m_ref[...], v_ref[...])

    blk = pl.BlockSpec((tr, Cc), lambda i: (i, 0))
    out = jax.ShapeDtypeStruct((R, Cc), F32)
    return pl.pallas_call(
        body, name=name, grid=(R // tr,), in_specs=[blk] * 4, out_specs=[blk] * 3, out_shape=[out, out, out],
    )(w, g, m, v)


def adamw_halves(w, mine, theirs, m, v, core, rows, cols, axis, name):
    sr, sc = _shard_shape(rows, cols, axis)
    pr, pc = _piece_shape(rows, cols, axis)
    tr = _row_tile(pr, pc, 9)
    nt = pr // tr

    def body(core_ref, w_ref, a_ref, b_ref, m_ref, v_ref, g_ref, d_ref, nm_ref, nv_ref):
        g = jnp.where(pl.program_id(0) == core_ref[0], a_ref[...], b_ref[...])
        g_ref[...] = g
        d_ref[...], nm_ref[...], nv_ref[...] = _adamw_math(w_ref[...], g, m_ref[...], v_ref[...])

    if axis == 0:
        full = pl.BlockSpec((tr, pc), lambda h, i, core_ref: (i, h))
    else:
        full = pl.BlockSpec((tr, pc), lambda h, i, core_ref: (h * nt + i, 0))
    part = pl.BlockSpec((tr, pc), lambda h, i, core_ref: (i, 0))
    out = jax.ShapeDtypeStruct((sr, sc), F32)
    return pl.pallas_call(
        body, name=name, out_shape=[out, out, out, out], compiler_params=SUM_PARAMS,
        grid_spec=pltpu.PrefetchScalarGridSpec(
            num_scalar_prefetch=1, grid=(2, nt), in_specs=[full, part, part, full, full], out_specs=[full] * 4),
    )(core, w, mine, theirs, m, v)


BIG = (
    ("ffn1_w_gate_up", D_MODEL, 2 * D_FF, 1),
    ("ffn1_w_down", D_FF, D_MODEL, 0),
    ("w_in", D_MODEL, IN_COLS, 1),
    ("w_branch_hg", HG_WIDTH, D_MODEL, 1),
    ("w_branch_att", ATT_WIDTH, D_MODEL, 1),
    ("w_out", D_MODEL, D_MODEL, 0),
    ("ffn2_w_gate_up", D_MODEL, 2 * D_FF, 1),
    ("ffn2_w_down", D_FF, D_MODEL, 0),
)
N_BIG = len(BIG)
ANY = pl.BlockSpec(memory_space=pl.ANY)


def _place():
    return lax.axis_index("x"), lax.axis_index("y"), lax.axis_index("c")


def _other_chips(x, y):
    return ((1 - x, y), (x, 1 - y), (1 - x, 1 - y))


MAX_COPY_CHUNKS = 16
CHUNK_ROW_ALIGN = 16


def _row_chunks(view):
    rows = view.shape[0]
    n = next(n for n in range(MAX_COPY_CHUNKS, 0, -1) if rows % (CHUNK_ROW_ALIGN * n) == 0 or n == 1)
    step = rows // n
    return [pl.ds(i * step, step) for i in range(n)]


def _remote(src, dst, send_sem, recv_sem, device):
    return pltpu.make_async_remote_copy(src_ref=src, dst_ref=dst, send_sem=send_sem, recv_sem=recv_sem,
                                        device_id=device, device_id_type=MESH)


def _start_remote(src, dst, send_sem, recv_sem, device):
    for rows in _row_chunks(src):
        _remote(src.at[rows, :], dst.at[rows, :], send_sem, recv_sem, device).start()
    return _remote(src, dst, send_sem, recv_sem, device)


HBM = pl.BlockSpec(memory_space=pltpu.HBM)
SEM = pl.BlockSpec(memory_space=pltpu.SEMAPHORE)
SPLIT_COPY_EFFECT = pltpu.SideEffectType.DATAFLOW_SIDE_EFFECTING
GROUPS = {"ffn1": (0, 1), "mix": (2, 3, 4, 5), "ffn2": (6, 7)}


def _in_hbm(a):
    return pltpu.with_memory_space_constraint(a, pltpu.HBM)


class _SemList:
    def __init__(self, refs):
        self.refs = refs
        self.at = self

    def __getitem__(self, index):
        w, k = index
        return self.refs[3 * w + k]


def _gather_piece(ref, rows, cols, axis, chip, c):
    sr, sc = _shard_shape(rows, cols, axis)
    j = 2 * chip[0] + chip[1]
    if axis == 0:
        return ref.at[pl.ds(j * sr + c * (sr // 2), sr // 2), :]
    return ref.at[pl.ds(c * (sr // 2), sr // 2), pl.ds(pl.multiple_of(j * sc, 128), sc)]


def _start_gather_sends(bufs, ws, send_sems, recv_sems):
    x, y, c = _place()
    for w, (_, r, cc, ax) in enumerate(ws):
        mine = _gather_piece(bufs[w], r, cc, ax, (x, y), c)
        for k, chip in enumerate(_other_chips(x, y)):
            _start_remote(mine, mine, send_sems.at[w, k], recv_sems.at[w, k], (*chip, c))


def _wait_gather_sends(bufs, ws, send_sems, recv_sems):
    x, y, c = _place()
    for w, (_, r, cc, ax) in enumerate(ws):
        for k, chip in enumerate(_other_chips(x, y)):
            got = _gather_piece(bufs[w], r, cc, ax, chip, c)
            _remote(got, got, send_sems.at[w, k], recv_sems.at[w, k], (x, y, c)).wait_recv()
    for w, (_, r, cc, ax) in enumerate(ws):
        mine = _gather_piece(bufs[w], r, cc, ax, (x, y), c)
        for k in range(3):
            _remote(mine, mine, send_sems.at[w, k], recv_sems.at[w, k], (x, y, c)).wait_send()


def _forward_halves(bufs, ws, send_sems, recv_sems):
    x, y, c = _place()
    passed = []
    for w, (_, r, cc, ax) in enumerate(ws):
        for k, chip in enumerate(_other_chips(x, y)):
            got = _gather_piece(bufs[w], r, cc, ax, chip, c)
            passed.append(_start_remote(got, got, send_sems.at[w, k], recv_sems.at[w, k], (x, y, 1 - c)))
    for w, (_, r, cc, ax) in enumerate(ws):
        for k, chip in enumerate(_other_chips(x, y)):
            got = _gather_piece(bufs[w], r, cc, ax, chip, 1 - c)
            _remote(got, got, send_sems.at[w, k], recv_sems.at[w, k], (x, y, c)).wait_recv()
    for cp in passed:
        cp.wait_send()


def gather_start(placed, after, group):
    ws = [BIG[i] for i in GROUPS[group]]
    n = len(ws)

    def body(*refs):
        bufs = refs[:n]
        send_sems, recv_sems = _SemList(refs[n + 1:4 * n + 1]), _SemList(refs[4 * n + 1:7 * n + 1])
        token = refs[-1]
        _start_gather_sends(bufs, ws, send_sems, recv_sems)
        token[...] = jnp.zeros_like(token)

    out = pl.pallas_call(
        body, name=f"gather_start_{group}", in_specs=[HBM] * n + [ANY],
        out_specs=[SEM] * (6 * n) + [HBM] * n + [pl.BlockSpec(memory_space=pltpu.VMEM)],
        out_shape=[pltpu.SemaphoreType.DMA(())] * (6 * n)
        + [pltpu.HBM((r, cc), WEIGHT_COMM_DTYPE) for _, r, cc, _ in ws] + [jax.ShapeDtypeStruct((8, 128), F32)],
        input_output_aliases={w: 6 * n + w for w in range(n)},
        compiler_params=pltpu.CompilerParams(has_side_effects=SPLIT_COPY_EFFECT),
    )(*[_in_hbm(p) for p in placed], after)
    return out[:3 * n], out[3 * n:6 * n], out[6 * n:7 * n], out[-1]


def gather_wait(bufs, send_sems, recv_sems, after, group):
    ws = [BIG[i] for i in GROUPS[group]]
    n = len(ws)

    def body(*refs):
        _wait_gather_sends(refs[:n], ws, _SemList(refs[n:n + 3 * n]), _SemList(refs[n + 3 * n:n + 6 * n]))

    return pl.pallas_call(
        body, name=f"gather_wait_{group}", in_specs=[HBM] * n + [SEM] * (6 * n) + [ANY] * len(after), out_specs=[HBM] * n,
        out_shape=[pltpu.HBM((r, cc), WEIGHT_COMM_DTYPE) for _, r, cc, _ in ws],
        input_output_aliases={w: w for w in range(n)},
        compiler_params=pltpu.CompilerParams(has_side_effects=SPLIT_COPY_EFFECT),
    )(*bufs, *send_sems, *recv_sems, *after)


def gather_forward(bufs, group):
    ws = [BIG[i] for i in GROUPS[group]]
    n = len(ws)

    def body(*refs):
        _forward_halves(refs[n:2 * n], ws, refs[2 * n], refs[2 * n + 1])

    return pl.pallas_call(
        body, name=f"gather_forward_{group}", in_specs=[ANY] * n, out_specs=[ANY] * n,
        out_shape=[jax.ShapeDtypeStruct((r, cc), WEIGHT_COMM_DTYPE) for _, r, cc, _ in ws],
        input_output_aliases={w: w for w in range(n)},
        scratch_shapes=[pltpu.SemaphoreType.DMA((n, 3))] * 2,
    )(*bufs)


def _half(ref, rows, cols, axis, c):
    if axis == 0:
        return ref.at[:, pl.ds(pl.multiple_of(c * (cols // 2), 128), cols // 2)]
    return ref.at[pl.ds(c * (rows // 2), rows // 2), :]


def _piece_of_half(ref, rows, cols, axis, chip):
    j = 2 * chip[0] + chip[1]
    pr, pc = _piece_shape(rows, cols, axis)
    if axis == 0:
        return ref.at[pl.ds(j * pr, pr), :]
    return ref.at[:, pl.ds(pl.multiple_of(j * pc, 128), pc)]


def exchange_halves(grads, group):
    ws = [BIG[i] for i in GROUPS[group]]
    n = len(ws)

    def body(*refs):
        ins, theirs = refs[:n], refs[n:2 * n]
        send_sems, recv_sems = refs[2 * n:]
        x, y, c = _place()
        copies = [_start_remote(_half(ins[w], r, cc, ax, 1 - c), theirs[w], send_sems.at[w], recv_sems.at[w], (x, y, 1 - c))
                  for w, (_, r, cc, ax) in enumerate(ws)]
        for cp in copies:
            cp.wait()

    return pl.pallas_call(
        body, name=f"exchange_halves_{group}", in_specs=[ANY] * n, out_specs=[ANY] * n,
        out_shape=[jax.ShapeDtypeStruct(_half_shape(r, cc, ax), GRAD_COMM_DTYPE) for _, r, cc, ax in ws],
        scratch_shapes=[pltpu.SemaphoreType.DMA((n,)), pltpu.SemaphoreType.DMA((n,))],
    )(*grads)


def _scatter_copies(halves, got, ws, send_sems, recv_sems, start):
    x, y, c = _place()
    copies = []
    for w, (_, r, cc, ax) in enumerate(ws):
        for k, chip in enumerate(_other_chips(x, y)):
            args = (_piece_of_half(halves[w], r, cc, ax, chip), got[w].at[k], send_sems.at[w, k], recv_sems.at[w, k], (*chip, c))
            copies.append(_start_remote(*args) if start else _remote(*args))
    return copies


def scatter_start(halves, group):
    ws = [BIG[i] for i in GROUPS[group]]
    n = len(ws)

    def body(*refs):
        sems = refs[2 * n:8 * n]
        _scatter_copies(refs[:n], refs[n:2 * n], ws, _SemList(sems[:3 * n]), _SemList(sems[3 * n:]), start=True)
        refs[-1][...] = jnp.zeros_like(refs[-1])

    landing = [lax.empty((3,) + _piece_shape(r, cc, ax), GRAD_COMM_DTYPE) for _, r, cc, ax in ws]
    out = pl.pallas_call(
        body, name=f"scatter_start_{group}", in_specs=[HBM] * (2 * n),
        out_specs=[SEM] * (6 * n) + [HBM] * (2 * n) + [pl.BlockSpec(memory_space=pltpu.VMEM)],
        out_shape=[pltpu.SemaphoreType.DMA(())] * (6 * n)
        + [pltpu.HBM(_half_shape(r, cc, ax), GRAD_COMM_DTYPE) for _, r, cc, ax in ws]
        + [pltpu.HBM((3,) + _piece_shape(r, cc, ax), GRAD_COMM_DTYPE) for _, r, cc, ax in ws]
        + [jax.ShapeDtypeStruct((8, 128), F32)],
        input_output_aliases={i: 6 * n + i for i in range(2 * n)},
        compiler_params=pltpu.CompilerParams(has_side_effects=SPLIT_COPY_EFFECT),
    )(*[_in_hbm(h) for h in halves], *[_in_hbm(b) for b in landing])
    return out[:3 * n], out[3 * n:6 * n], out[6 * n:7 * n], out[7 * n:8 * n], out[-1]


def scatter_wait(halves, got, send_sems, recv_sems, after, group):
    ws = [BIG[i] for i in GROUPS[group]]
    n = len(ws)

    def body(*refs):
        sems = refs[2 * n:8 * n]
        for cp in _scatter_copies(refs[:n], refs[n:2 * n], ws, _SemList(sems[:3 * n]), _SemList(sems[3 * n:]), start=False):
            cp.wait_send()
            cp.wait_recv()

    out = pl.pallas_call(
        body, name=f"scatter_wait_{group}", in_specs=[HBM] * (2 * n) + [SEM] * (6 * n) + [ANY] * len(after),
        out_specs=[HBM] * (2 * n),
        out_shape=[pltpu.HBM(_half_shape(r, cc, ax), GRAD_COMM_DTYPE) for _, r, cc, ax in ws]
        + [pltpu.HBM((3,) + _piece_shape(r, cc, ax), GRAD_COMM_DTYPE) for _, r, cc, ax in ws],
        input_output_aliases={i: i for i in range(2 * n)},
        compiler_params=pltpu.CompilerParams(has_side_effects=SPLIT_COPY_EFFECT),
    )(*halves, *got, *send_sems, *recv_sems, *after)
    return out[:n], out[n:]


def exchange_reduced(pieces, group):
    ws = [BIG[i] for i in GROUPS[group]]
    n = len(ws)

    def body(*refs):
        ins, theirs = refs[:n], refs[n:2 * n]
        send_sems, recv_sems = refs[2 * n:]
        x, y, c = _place()
        copies = [_start_remote(ins[w], theirs[w], send_sems.at[w], recv_sems.at[w], (x, y, 1 - c)) for w in range(n)]
        for cp in copies:
            cp.wait()

    return pl.pallas_call(
        body, name=f"exchange_reduced_{group}", in_specs=[ANY] * n, out_specs=[ANY] * n,
        out_shape=[jax.ShapeDtypeStruct(_piece_shape(r, cc, ax), F32) for _, r, cc, ax in ws],
        scratch_shapes=[pltpu.SemaphoreType.DMA((n,)), pltpu.SemaphoreType.DMA((n,))],
    )(*pieces)


N_DEV = 8
SMALL_ROWS = 8


def all_reduce_small(packed, behind):
    def body(x_ref, behind_ref, o_ref, gathered, send_sems, recv_sems):
        x, y, c = _place()
        me = 4 * x + 2 * y + c
        gathered[me] = x_ref[...]
        copies = []
        for k in range(1, N_DEV):
            peer = (x ^ (k >> 2), y ^ ((k >> 1) & 1), c ^ (k & 1))
            cp = pltpu.make_async_remote_copy(
                src_ref=x_ref, dst_ref=gathered.at[me], send_sem=send_sems.at[k - 1], recv_sem=recv_sems.at[k - 1],
                device_id=peer, device_id_type=MESH)
            cp.start()
            copies.append(cp)
        for cp in copies:
            cp.wait()
        acc = gathered[0]
        for k in range(1, N_DEV):
            acc = acc + gathered[k]
        o_ref[...] = acc

    vm = pl.BlockSpec(memory_space=pltpu.VMEM)
    return pl.pallas_call(
        body, name="all_reduce_small", in_specs=[vm, ANY], out_specs=vm,
        out_shape=jax.ShapeDtypeStruct((SMALL_ROWS, D_MODEL), F32),
        scratch_shapes=[pltpu.VMEM((N_DEV, SMALL_ROWS, D_MODEL), F32), pltpu.SemaphoreType.DMA((N_DEV - 1,)),
                        pltpu.SemaphoreType.DMA((N_DEV - 1,))],
    )(packed, behind)


def _swiglu_block_fwd(h, norm_g, w_gu, w_down, tag, behind=()):
    n = rmsnorm_fwd(h, norm_g, f"{tag}_norm", behind=behind)
    gu = matmul(n, w_gu, name=f"{tag}_gate_up")
    s = swiglu_fwd(gu, f"{tag}_swiglu")
    h_out = matmul(s, w_down, res=h, scale=0.5, name=f"{tag}_down")
    return h_out, (n, gu, s)


def _swiglu_block_bwd(h, norm_g, w_gu, w_down, saved, dh_out, tag, behind=()):
    n, gu, s = saved
    df = dh_out.astype(MXU_DTYPE)
    d_down = matmul(s, df, ta=True, scale=0.5, out_dtype=GRAD_COMM_DTYPE, name=f"{tag}_d_w_down")
    ds = matmul(df, w_down, tb=True, scale=0.5, behind=behind, name=f"{tag}_d_s")
    dgu = swiglu_bwd(gu, ds, f"{tag}_swiglu_bwd")
    d_gu = matmul(n, dgu, ta=True, out_dtype=GRAD_COMM_DTYPE, name=f"{tag}_d_w_gate_up")
    dn = matmul(dgu, w_gu, tb=True, name=f"{tag}_d_n")
    dh, dg = rmsnorm_bwd(h, norm_g, dn, dh_out, f"{tag}_norm_bwd")
    return dh, dg, d_gu, d_down


def local_step(x, target, small, exchange):
    big = {}
    token, big_ffn1 = exchange.weights("ffn1", x)
    big.update(big_ffn1)
    h1, saved1 = _swiglu_block_fwd(x, small["ffn1_norm"], big["ffn1_w_gate_up"], big["ffn1_w_down"], "ffn1", token)
    token, big_mix = exchange.weights("mix", h1)
    big.update(big_mix)
    u = rmsnorm_fwd(h1, small["mix_norm"], "mix_norm", behind=token)
    z = matmul(u, big["w_in"], name="w_in")
    p = small["hg_lower_bounds"]
    lb = 1.0 / (1.0 + jnp.exp(p[1:2] - p[0:1]))
    y_hg, o_raw, states = hgrn_fwd(z, lb, small["hg_out_norm"], "hgrn_fwd")
    o_att, l_att = zip(*[att_fwd(z, g, f"att_fwd_{g}") for g in range(N_GROUPS)])
    y_att = att_combine_fwd(o_att, l_att, "att_combine")
    bh = matmul(y_hg, big["w_branch_hg"], name="branch_hg")
    ba = matmul(y_att, big["w_branch_att"], name="branch_att")
    merged = merge_fwd(z, bh, ba, "merge")
    h2 = matmul(merged, big["w_out"], res=h1, name="w_out")
    token, big_ffn2 = exchange.weights("ffn2", h2)
    big.update(big_ffn2)
    h3, saved2 = _swiglu_block_fwd(h2, small["ffn2_norm"], big["ffn2_w_gate_up"], big["ffn2_w_down"], "ffn2", token)
    dh3, d_final, loss = final_norm_loss(h3, small["final_norm"], target, "final_norm_loss")

    gs, gb = {"final_norm": d_final}, {}
    dh2, gs["ffn2_norm"], gb["ffn2_w_gate_up"], gb["ffn2_w_down"] = _swiglu_block_bwd(
        h2, small["ffn2_norm"], big["ffn2_w_gate_up"], big["ffn2_w_down"], saved2, dh3, "ffn2")
    token = exchange.gradients("ffn2", gb, dh2)
    dh2_m = dh2.astype(MXU_DTYPE)
    gb["w_out"] = matmul(merged, dh2_m, ta=True, out_dtype=GRAD_COMM_DTYPE, name="d_w_out")
    dmerged = matmul(dh2_m, big["w_out"], tb=True, behind=token, name="d_merged")
    dbh, dba, dgh, dga = merge_bwd(z, bh, ba, dmerged, "merge_bwd")
    gb["w_branch_hg"] = matmul(y_hg, dbh, ta=True, out_dtype=GRAD_COMM_DTYPE, name="d_w_branch_hg")
    gb["w_branch_att"] = matmul(y_att, dba, ta=True, out_dtype=GRAD_COMM_DTYPE, name="d_w_branch_att")
    dy_hg = matmul(dbh, big["w_branch_hg"], tb=True, name="d_y_hg")
    dy_att = matmul(dba, big["w_branch_att"], tb=True, name="d_y_att")
    dq, dfp, di, dog, d_lb, gs["hg_out_norm"] = hgrn_bwd(z, lb, small["hg_out_norm"], o_raw, states, dy_hg, "hgrn_bwd")
    do_att, corr = att_combine_bwd(o_att, l_att, dy_att, "att_combine_bwd")
    d_att = [part for g in range(N_GROUPS) for part in att_bwd(z, l_att[g], do_att[g], corr[g], g, f"att_bwd_{g}")]
    dz = jnp.concatenate([dq, dfp, di, dog, *d_att, dgh, dga], axis=1)
    gb["w_in"] = matmul(u, dz, ta=True, out_dtype=GRAD_COMM_DTYPE, name="d_w_in")
    du = matmul(dz, big["w_in"], tb=True, name="d_u")
    dh1, gs["mix_norm"] = rmsnorm_bwd(h1, small["mix_norm"], du, dh2, "mix_norm_bwd")
    token = exchange.gradients("mix", gb, dh1)
    dp0 = d_lb * lb * (1.0 - lb)
    gs["hg_lower_bounds"] = jnp.concatenate([dp0, -dp0], axis=0)
    dx, gs["ffn1_norm"], gb["ffn1_w_gate_up"], gb["ffn1_w_down"] = _swiglu_block_bwd(
        x, small["ffn1_norm"], big["ffn1_w_gate_up"], big["ffn1_w_down"], saved1, dh1, "ffn1", token)
    exchange.gradients("ffn1", gb, dx)
    return loss[0, 0], dx, gs


SMALL = ("ffn1_norm", "mix_norm", "hg_lower_bounds", "hg_out_norm", "ffn2_norm", "final_norm")
WEIGHTS = ("ffn1_norm", "ffn1_w_gate_up", "ffn1_w_down", "mix_norm", "w_in", "hg_lower_bounds", "hg_out_norm",
           "w_branch_hg", "w_branch_att", "w_out", "ffn2_norm", "ffn2_w_gate_up", "ffn2_w_down", "final_norm")
SMALL_SHAPE = {"ffn1_norm": (1, 1024), "mix_norm": (1, 1024), "hg_lower_bounds": (2, 512), "hg_out_norm": (1, 512),
               "ffn2_norm": (1, 1024), "final_norm": (1024,)}
LOSS_ROW = 6


def _pack_small(vals):
    rows = []
    for n in SMALL:
        r = vals[n].reshape(1, -1).astype(F32)
        rows.append(jnp.pad(r, ((0, 0), (0, D_MODEL - r.shape[1]))))
    rows.append(jnp.zeros((SMALL_ROWS - len(SMALL), D_MODEL), F32))
    return jnp.concatenate(rows, axis=0)


def _unpack_small(packed):
    out = {}
    for i, n in enumerate(SMALL):
        size = int(np.prod(SMALL_SHAPE[n]))
        out[n] = packed[i, :size].reshape(SMALL_SHAPE[n])
    return out


class WeightExchange:
    ORDER = ("ffn1", "mix", "ffn2")

    def __init__(self, shards, core, chip):
        self.core, self.chip = core, chip
        self.scattering = None
        self.reduced = {}
        first = self.ORDER[0]
        self.placed = {BIG[i][0]: place_own_block(shards[BIG[i][0]], chip, *BIG[i][1:], f"place_{BIG[i][0]}")
                       for i in GROUPS[first]}
        self._start_gather(first, self.placed[self._names(first)[0]])
        chip_behind = chip + self.token[0, :1].astype(jnp.int32)
        for group in self.ORDER[1:]:
            for i in GROUPS[group]:
                n, r, cc, ax = BIG[i]
                self.placed[n] = place_own_block(shards[n], chip_behind, r, cc, ax, f"place_{n}")
        self.placed_behind = [self.placed[n] for group in self.ORDER[1:] for n in self._names(group)]

    def _names(self, group):
        return [BIG[i][0] for i in GROUPS[group]]

    def _start_gather(self, group, after):
        send_sems, recv_sems, bufs, self.token = gather_start([self.placed[n] for n in self._names(group)], after, group)
        self.gathering = (group, send_sems, recv_sems, bufs)

    def weights(self, group, h):
        pending, send_sems, recv_sems, bufs = self.gathering
        assert pending == group
        after = self.placed_behind if group == self.ORDER[0] else [h]
        whole = gather_forward(gather_wait(bufs, send_sems, recv_sems, after, group), group)
        later = self.ORDER.index(group) + 1
        behind = []
        if later < len(self.ORDER):
            self._start_gather(self.ORDER[later], whole[0])
            behind = [self.token]
        return behind, dict(zip(self._names(group), whole))

    def _finish_scatter(self, after):
        group, send_sems, recv_sems, halves, got = self.scattering
        halves, got = scatter_wait(halves, got, send_sems, recv_sems, after, group)
        ws = [BIG[i] for i in GROUPS[group]]
        mine = [add_pieces(h, g, self.chip, r, cc, ax, f"add_pieces_{n}") for (n, r, cc, ax), h, g in zip(ws, halves, got)]
        theirs = exchange_reduced(mine, group)
        self.reduced.update({n: (a, b) for (n, *_), a, b in zip(ws, mine, theirs)})
        self.scattering = None
        return theirs[0]

    def gradients(self, group, grads, dh):
        behind = [self._finish_scatter([dh])] if self.scattering is not None else []
        ws = [BIG[i] for i in GROUPS[group]]
        theirs = exchange_halves([grads[n] for n, *_ in ws], group)
        halves = [add_halves(grads[n], t, self.core, r, cc, ax, f"add_halves_{n}") for (n, r, cc, ax), t in zip(ws, theirs)]
        send_sems, recv_sems, halves, got, self.token = scatter_start(halves, group)
        self.scattering = (group, send_sems, recv_sems, halves, got)
        return behind + [self.token]

    def finish(self, after):
        self._finish_scatter(after)
        return self.reduced


def kernel(x, ffn1_norm, ffn1_w_gate_up, ffn1_w_down, mix_norm, w_in, hg_lower_bounds, hg_out_norm, w_branch_hg, w_branch_att, w_out, ffn2_norm, ffn2_w_gate_up, ffn2_w_down, final_norm, loss_target, m_ffn1_norm, m_ffn1_w_gate_up, m_ffn1_w_down, m_mix_norm, m_w_in, m_hg_lower_bounds, m_hg_out_norm, m_w_branch_hg, m_w_branch_att, m_w_out, m_ffn2_norm, m_ffn2_w_gate_up, m_ffn2_w_down, m_final_norm, v_ffn1_norm, v_ffn1_w_gate_up, v_ffn1_w_down, v_mix_norm, v_w_in, v_hg_lower_bounds, v_hg_out_norm, v_w_branch_hg, v_w_branch_att, v_w_out, v_ffn2_norm, v_ffn2_w_gate_up, v_ffn2_w_down, v_final_norm):
    w = dict(ffn1_norm=ffn1_norm, ffn1_w_gate_up=ffn1_w_gate_up, ffn1_w_down=ffn1_w_down, mix_norm=mix_norm, w_in=w_in,
             hg_lower_bounds=hg_lower_bounds, hg_out_norm=hg_out_norm, w_branch_hg=w_branch_hg, w_branch_att=w_branch_att,
             w_out=w_out, ffn2_norm=ffn2_norm, ffn2_w_gate_up=ffn2_w_gate_up, ffn2_w_down=ffn2_w_down, final_norm=final_norm)
    m = dict(ffn1_norm=m_ffn1_norm, ffn1_w_gate_up=m_ffn1_w_gate_up, ffn1_w_down=m_ffn1_w_down, mix_norm=m_mix_norm,
             w_in=m_w_in, hg_lower_bounds=m_hg_lower_bounds, hg_out_norm=m_hg_out_norm, w_branch_hg=m_w_branch_hg,
             w_branch_att=m_w_branch_att, w_out=m_w_out, ffn2_norm=m_ffn2_norm, ffn2_w_gate_up=m_ffn2_w_gate_up,
             ffn2_w_down=m_ffn2_w_down, final_norm=m_final_norm)
    v = dict(ffn1_norm=v_ffn1_norm, ffn1_w_gate_up=v_ffn1_w_gate_up, ffn1_w_down=v_ffn1_w_down, mix_norm=v_mix_norm,
             w_in=v_w_in, hg_lower_bounds=v_hg_lower_bounds, hg_out_norm=v_hg_out_norm, w_branch_hg=v_w_branch_hg,
             w_branch_att=v_w_branch_att, w_out=v_w_out, ffn2_norm=v_ffn2_norm, ffn2_w_gate_up=v_ffn2_w_gate_up,
             ffn2_w_down=v_ffn2_w_down, final_norm=v_final_norm)

    core = lax.axis_index("c").astype(jnp.int32).reshape(1)
    chip = (2 * lax.axis_index("x") + lax.axis_index("y")).astype(jnp.int32).reshape(1)
    exchange = WeightExchange({n: w[n][0] for n, *_ in BIG}, core, chip)
    small = {n: w[n] for n in SMALL}
    small["final_norm"] = final_norm.reshape(1, D_MODEL)

    loss, dx, gs = local_step(x[0], loss_target[0], small, exchange)

    grads, delta, new_m, new_v = {}, {}, {}, {}

    def update(group, core):
        for i in GROUPS[group]:
            n, r, cc, ax = BIG[i]
            a, b = exchange.reduced[n]
            g, d, nm, nv = adamw_halves(w[n][0], a, b, m[n][0], v[n][0], core, r, cc, ax, f"adamw_{n}")
            grads[n], delta[n], new_m[n], new_v[n] = g[None], d[None], nm[None], nv[None]

    core_behind = core + exchange.token[0, :1].astype(jnp.int32)
    update("ffn2", core_behind)
    update("mix", core_behind)
    exchange.finish(after=[delta[BIG[i][0]] for group in ("ffn2", "mix") for i in GROUPS[group]])
    update("ffn1", core)
    packed = _pack_small(gs)
    packed = packed.at[LOSS_ROW].set(jnp.full((D_MODEL,), loss, F32))
    total = all_reduce_small(packed, behind=delta["ffn1_w_down"])
    grads.update(_unpack_small(total))
    loss_total = total[LOSS_ROW, 0]
    pd, pm, pv = adamw(_pack_small({n: w[n] for n in SMALL}), total.at[LOSS_ROW].set(0.0),
                       _pack_small({n: m[n] for n in SMALL}), _pack_small({n: v[n] for n in SMALL}), "adamw_small")
    delta.update(_unpack_small(pd))
    new_m.update(_unpack_small(pm))
    new_v.update(_unpack_small(pv))

    return (loss_total, dx[None], *[grads[n] for n in WEIGHTS], *[delta[n] for n in WEIGHTS],
            *[new_m[n] for n in WEIGHTS], *[new_v[n] for n in WEIGHTS])
```

```python
import numpy as np
import jax
import jax.numpy as jnp
from jax import lax
from jax.experimental import pallas as pl
from jax.experimental.pallas import tpu as pltpu

SEQ = 2048
D_MODEL = 1024
D_FF = 2816
HG_HEADS = 4
HG_DIM = 128
HG_WIDTH = 512
HG_CHUNK = 64
ATT_GROUPS = ((128, 1), (512, 4), (2048, 16))
ATT_HEADS = 8
ATT_WIDTH = 512
ATT_BLOCK = 128
ALIBI_MAX = 8.0
IN_COLS = 8704
EPS = 1e-6
NEG_INF = -1e30
ADAM_LR = 0.001
ADAM_B1 = 0.9
ADAM_B2 = 0.999
ADAM_EPS = 1e-08
ADAM_WD = 0.01
ADAM_STEP = 10

N_CHIPS = 4
MXU_DTYPE = jnp.bfloat16
WEIGHT_COMM_DTYPE = jnp.bfloat16
GRAD_COMM_DTYPE = jnp.bfloat16
MESH = pl.DeviceIdType.MESH
F32 = jnp.float32
HIGHEST = lax.Precision.HIGHEST


def _sigmoid(x):
    return 1.0 / (1.0 + jnp.exp(-x))


def _dot(a, b, ta=False, tb=False):
    dn = (((0 if ta else 1,), (1 if tb else 0,)), ((), ()))
    return lax.dot_general(a.astype(MXU_DTYPE), b.astype(MXU_DTYPE), dn, preferred_element_type=F32)


def _dot_f32(a, b, ones_on_right=False):
    x = a if ones_on_right else b
    hi = x.astype(jnp.bfloat16)
    rest = x - hi.astype(F32)
    mid = rest.astype(jnp.bfloat16)
    lo = (rest - mid.astype(F32)).astype(jnp.bfloat16)
    if ones_on_right:
        dot = lambda q: jnp.dot(q, b.astype(jnp.bfloat16), preferred_element_type=F32)
    else:
        dot = lambda q: jnp.dot(a.astype(jnp.bfloat16), q, preferred_element_type=F32)
    return dot(hi) + (dot(mid) + dot(lo))


def _split_bf16(x):
    hi = x.astype(jnp.bfloat16)
    return hi, (x - hi.astype(F32)).astype(jnp.bfloat16)


def _hdot(a, b, ta=False, tb=False):
    dn =(((0 if ta else 1,), (1 if tb else 0,)), ((), ()))
    (a_hi, a_lo), (b_hi, b_lo) = _split_bf16(a), _split_bf16(b)
    dot = lambda p, q: lax.dot_general(p, q, dn, preferred_element_type=F32)
    return dot(a_hi, b_hi) + (dot(a_lo, b_hi) + dot(a_hi, b_lo))


MATMUL_VMEM_BYTES = 48 * 1024 * 1024
MATMUL_TILE_BYTES = 36 * 1024 * 1024
MXU_ALIGN = 128


def _divisors(n, most):
    return [t for t in range(min(n, most), 0, -MXU_ALIGN) if n % t == 0 and t % MXU_ALIGN == 0]


def _matmul_tiles(M, N, K, in_bytes, out_bytes, has_res):
    best = None
    for tk in _divisors(K, K):
        nk = K // tk
        for tm in _divisors(M, 2048):
            for tn in _divisors(N, 512):
                tiles = 2 * in_bytes * (tm * tk + tk * tn) + 2 * out_bytes * tm * tn
                tiles += 4 * tm * tn * ((nk > 1) + 2 * has_res)
                if tiles > MATMUL_TILE_BYTES:
                    continue
                traffic = in_bytes * (M * K * (1 if nk == 1 else N // tn) + K * N * (M // tm))
                key = (traffic, -tm * tn * tk)
                if best is None or key < best[0]:
                    best = (key, (tm, tn, tk))
    return best[1]


def matmul(a, b, *, ta=False, tb=False, out_dtype=F32, res=None, scale=1.0, behind=(), name):
    if ta:
        K, M = a.shape
    else:
        M, K = a.shape
    if tb:
        N, K2 = b.shape
    else:
        K2, N = b.shape
    assert K == K2 and a.dtype == b.dtype
    tm, tn, tk = _matmul_tiles(M, N, K, a.dtype.itemsize, jnp.dtype(out_dtype).itemsize, res is not None)
    nk = K // tk

    def finish(r, r_ref, o_ref):
        if scale != 1.0:
            r = r * scale
        if res is not None:
            r = r_ref[...] + r
        o_ref[...] = r.astype(out_dtype)

    def body(*refs):
        a_ref, b_ref = refs[:2]
        r_ref = refs[2] if res is not None else None
        o_ref = refs[2 + (res is not None) + len(behind)]
        if nk == 1:
            finish(_dot(a_ref[...], b_ref[...], ta, tb), r_ref, o_ref)
            return
        acc = refs[-1]
        k = pl.program_id(2)

        @pl.when(k == 0)
        def _():
            acc[...] = jnp.zeros_like(acc)

        acc[...] += _dot(a_ref[...], b_ref[...], ta, tb)

        @pl.when(k == nk - 1)
        def _():
            finish(acc[...], r_ref, o_ref)

    a_spec = pl.BlockSpec((tk, tm), lambda i, j, k: (k, i)) if ta else pl.BlockSpec((tm, tk), lambda i, j, k: (i, k))
    b_spec = pl.BlockSpec((tn, tk), lambda i, j, k: (j, k)) if tb else pl.BlockSpec((tk, tn), lambda i, j, k: (k, j))
    in_specs = [a_spec, b_spec]
    args = [a, b]
    if res is not None:
        in_specs.append(pl.BlockSpec((tm, tn), lambda i, j, k: (i, j)))
        args.append(res)
    for earlier in behind:
        in_specs.append(pl.BlockSpec(memory_space=pl.ANY))
        args.append(earlier)
    return pl.pallas_call(
        body, name=name, grid=(M // tm, N // tn, nk), in_specs=in_specs,
        out_specs=pl.BlockSpec((tm, tn), lambda i, j, k: (i, j)),
        out_shape=jax.ShapeDtypeStruct((M, N), out_dtype),
        scratch_shapes=[pltpu.VMEM((tm, tn), F32)] if nk > 1 else [],
        compiler_params=pltpu.CompilerParams(dimension_semantics=("parallel", "parallel", "arbitrary"),
                                             vmem_limit_bytes=MATMUL_VMEM_BYTES),
    )(*args)


ROW_TILE = 256


def rmsnorm_fwd(x, g, name, behind=()):
    def body(x_ref, g_ref, *refs):
        n_ref = refs[-1]
        xv = x_ref[...]
        r = lax.rsqrt(jnp.mean(xv * xv, axis=-1, keepdims=True) + EPS)
        n_ref[...] = ((xv * r) * g_ref[...]).astype(n_ref.dtype)

    order = list(behind)
    return pl.pallas_call(
        body, name=name, grid=(SEQ // ROW_TILE,),
        in_specs=[pl.BlockSpec((ROW_TILE, D_MODEL), lambda i: (i, 0)), pl.BlockSpec((1, D_MODEL), lambda i: (0, 0))]
        + [pl.BlockSpec(memory_space=pl.ANY)] * len(order),
        out_specs=pl.BlockSpec((ROW_TILE, D_MODEL), lambda i: (i, 0)),
        out_shape=jax.ShapeDtypeStruct((SEQ, D_MODEL), MXU_DTYPE),
    )(x, g, *order)


def rmsnorm_bwd(x, g, dn, dres, name):
    def body(x_ref, g_ref, dn_ref, dr_ref, dx_ref, dg_ref):
        xv = x_ref[...]
        r = lax.rsqrt(jnp.mean(xv * xv, axis=-1, keepdims=True) + EPS)
        xh = xv * r
        dnv = dn_ref[...]

        @pl.when(pl.program_id(0) == 0)
        def _():
            dg_ref[...] = jnp.zeros_like(dg_ref)

        dg_ref[...] += jnp.sum(dnv * xh, axis=0, keepdims=True)
        dxh = dnv * g_ref[...]
        dx_ref[...] = dr_ref[...] + r * (dxh - xh * jnp.mean(dxh * xh, axis=-1, keepdims=True))

    row = pl.BlockSpec((ROW_TILE, D_MODEL), lambda i: (i, 0))
    vec = pl.BlockSpec((1, D_MODEL), lambda i: (0, 0))
    return pl.pallas_call(
        body, name=name, grid=(SEQ // ROW_TILE,), in_specs=[row, vec, row, row], out_specs=[row, vec],
        out_shape=[jax.ShapeDtypeStruct((SEQ, D_MODEL), F32), jax.ShapeDtypeStruct((1, D_MODEL), F32)],
        compiler_params=pltpu.CompilerParams(dimension_semantics=("arbitrary",)),
    )(x, g, dn, dres)


def final_norm_loss(h, g, target, name):
    def body(h_ref, g_ref, t_ref, dh_ref, dg_ref, loss_ref):
        xv = h_ref[...]
        r = lax.rsqrt(jnp.mean(xv * xv, axis=-1, keepdims=True) + EPS)
        xh = xv * r
        gv = g_ref[...]
        e = xh * gv - t_ref[...]

        @pl.when(pl.program_id(0) == 0)
        def _():
            dg_ref[...] = jnp.zeros_like(dg_ref)
            loss_ref[...] = jnp.zeros_like(loss_ref)

        part = 0.5 * jnp.sum(jnp.sum(e * e, axis=-1, keepdims=True) * (1.0 / D_MODEL), axis=0, keepdims=True)
        loss_ref[...] += jnp.broadcast_to(part, loss_ref.shape)
        dout = e * (1.0 / D_MODEL)
        dg_ref[...] += jnp.sum(dout * xh, axis=0, keepdims=True)
        dxh = dout * gv
        dh_ref[...] = r * (dxh - xh * jnp.mean(dxh * xh, axis=-1, keepdims=True))

    row = pl.BlockSpec((ROW_TILE, D_MODEL), lambda i: (i, 0))
    vec = pl.BlockSpec((1, D_MODEL), lambda i: (0, 0))
    return pl.pallas_call(
        body, name=name, grid=(SEQ // ROW_TILE,), in_specs=[row, vec, row],
        out_specs=[row, vec, pl.BlockSpec((8, 128), lambda i: (0, 0))],
        out_shape=[jax.ShapeDtypeStruct((SEQ, D_MODEL), F32), jax.ShapeDtypeStruct((1, D_MODEL), F32),
                   jax.ShapeDtypeStruct((8, 128), F32)],
        compiler_params=pltpu.CompilerParams(dimension_semantics=("arbitrary",)),
    )(h, g, target)


FF_TILE = D_FF // 2


def swiglu_fwd(gu, name):
    def body(a_ref, b_ref, s_ref):
        a = a_ref[...]
        s_ref[...] = (a * _sigmoid(a) * b_ref[...]).astype(s_ref.dtype)

    return pl.pallas_call(
        body, name=name, grid=(SEQ // ROW_TILE, 2),
        in_specs=[pl.BlockSpec((ROW_TILE, FF_TILE), lambda i, j: (i, j)),
                  pl.BlockSpec((ROW_TILE, FF_TILE), lambda i, j: (i, j + 2))],
        out_specs=pl.BlockSpec((ROW_TILE, FF_TILE), lambda i, j: (i, j)),
        out_shape=jax.ShapeDtypeStruct((SEQ, D_FF), MXU_DTYPE),
    )(gu, gu)


def swiglu_bwd(gu, ds, name):
    rows = ROW_TILE // 2

    def body(a_ref, b_ref, ds_ref, o_ref):
        a = a_ref[...]
        sg = _sigmoid(a)
        dsv = ds_ref[...]
        o_ref[:, :D_FF] = (dsv * b_ref[...] * (sg * (1.0 + a * (1.0 - sg)))).astype(o_ref.dtype)
        o_ref[:, D_FF:] = (dsv * a * sg).astype(o_ref.dtype)

    return pl.pallas_call(
        body, name=name, grid=(SEQ // rows,),
        in_specs=[pl.BlockSpec((rows, D_FF), lambda i: (i, 0)), pl.BlockSpec((rows, D_FF), lambda i: (i, 1)),
                  pl.BlockSpec((rows, D_FF), lambda i: (i, 0))],
        out_specs=pl.BlockSpec((rows, 2 * D_FF), lambda i: (i, 0)),
        out_shape=jax.ShapeDtypeStruct((SEQ, 2 * D_FF), MXU_DTYPE), compiler_params=SUM_PARAMS,
    )(gu, gu, ds)


GATE_HG_BLK = 6656 // 512
GATE_ATT_BLK = 7680 // 512


def merge_fwd(z, bh, ba, name):
    def body(gh_ref, ga_ref, bh_ref, ba_ref, o_ref):
        o_ref[...] = (_sigmoid(gh_ref[...]) * bh_ref[...] + _sigmoid(ga_ref[...]) * ba_ref[...]).astype(o_ref.dtype)

    blk = pl.BlockSpec((ROW_TILE, 512), lambda i, j: (i, j))
    return pl.pallas_call(
        body, name=name, grid=(SEQ // ROW_TILE, 2),
        in_specs=[pl.BlockSpec((ROW_TILE, 512), lambda i, j: (i, GATE_HG_BLK + j)),
                  pl.BlockSpec((ROW_TILE, 512), lambda i, j: (i, GATE_ATT_BLK + j)), blk, blk],
        out_specs=blk, out_shape=jax.ShapeDtypeStruct((SEQ, D_MODEL), MXU_DTYPE),
    )(z, z, bh, ba)


def merge_bwd(z, bh, ba, dm, name):
    def body(gh_ref, ga_ref, bh_ref, ba_ref, dm_ref, dbh_ref, dba_ref, dgh_ref, dga_ref):
        dmv = dm_ref[...]
        sh = _sigmoid(gh_ref[...])
        sa = _sigmoid(ga_ref[...])
        dbh_ref[...] = (dmv * sh).astype(dbh_ref.dtype)
        dba_ref[...] = (dmv * sa).astype(dba_ref.dtype)
        dgh_ref[...] = (dmv * bh_ref[...] * (sh * (1.0 - sh))).astype(dgh_ref.dtype)
        dga_ref[...] = (dmv * ba_ref[...] * (sa * (1.0 - sa))).astype(dga_ref.dtype)

    blk = pl.BlockSpec((ROW_TILE, 512), lambda i, j: (i, j))
    out = jax.ShapeDtypeStruct((SEQ, D_MODEL), MXU_DTYPE)
    return pl.pallas_call(
        body, name=name, grid=(SEQ // ROW_TILE, 2),
        in_specs=[pl.BlockSpec((ROW_TILE, 512), lambda i, j: (i, GATE_HG_BLK + j)),
                  pl.BlockSpec((ROW_TILE, 512), lambda i, j: (i, GATE_ATT_BLK + j)), blk, blk, blk],
        out_specs=[blk, blk, blk, blk], out_shape=[out, out, out, out],
    )(z, z, bh, ba, dm)


N_CHUNKS = SEQ // HG_CHUNK
HG_STEP_CHUNKS = 4


def _hgrn_gates(q, fp, lb):
    C = HG_CHUNK
    sg = _sigmoid(fp)
    f = lb + (1.0 - lb) * sg
    lf = jnp.log(f)
    row = lax.broadcasted_iota(jnp.int32, (C, C), 0)
    col = lax.broadcasted_iota(jnp.int32, (C, C), 1)
    causal = row >= col
    G = _dot_f32(causal.astype(F32), lf)
    eG = jnp.exp(G)
    enG = jnp.exp(-G)
    qg = q * eG
    kg = (1.0 - f) * enG
    A = jnp.where(causal, _hdot(qg, kg, tb=True), 0.0)
    egl = jnp.exp(jnp.sum(lf, axis=0, keepdims=True))
    return sg, f, causal, eG, enG, qg, kg, A, egl


def hgrn_fwd(z, lb, gain, name):
    C, K = HG_CHUNK, HG_DIM

    def body(q_ref, f_ref, v_ref, og_ref, p_ref, g_ref, y_ref, o_ref, st_ref, state):
        @pl.when(pl.program_id(0) == 0)
        def _():
            state[...] = jnp.zeros_like(state)

        for cc in range(HG_STEP_CHUNKS):
            rows = pl.ds(cc * C, C)
            for h in range(HG_HEADS):
                hd = pl.ds(h * K, K)
                v = v_ref[rows, hd]
                _, _, _, _, _, qg, kg, A, egl = _hgrn_gates(q_ref[rows, hd], f_ref[rows, hd], p_ref[:, hd])
                st = state[h]
                st_ref[h, cc] = st
                o = _hdot(A, v) + _hdot(qg, st, tb=True)
                state[h] = st * egl + _hdot(v, kg * egl, ta=True)
                o_ref[rows, hd] = o
                rs = lax.rsqrt(jnp.mean(o * o, axis=-1, keepdims=True) + EPS)
                og = og_ref[rows, hd]
                y_ref[rows, hd] = (((o * rs) * g_ref[:, hd]) * (og * _sigmoid(og))).astype(y_ref.dtype)

    R = HG_STEP_CHUNKS * C

    def zcol(section):
        return pl.BlockSpec((R, HG_WIDTH), lambda c: (c, section))

    vec = pl.BlockSpec((1, HG_WIDTH), lambda c: (0, 0))
    blk = pl.BlockSpec((R, HG_WIDTH), lambda c: (c, 0))
    return pl.pallas_call(
        body, name=name, grid=(N_CHUNKS // HG_STEP_CHUNKS,),
        in_specs=[zcol(0), zcol(1), zcol(2), zcol(3), vec, vec],
        out_specs=[blk, blk, pl.BlockSpec((HG_HEADS, HG_STEP_CHUNKS, K, K), lambda c: (0, c, 0, 0))],
        out_shape=[jax.ShapeDtypeStruct((SEQ, HG_WIDTH), MXU_DTYPE), jax.ShapeDtypeStruct((SEQ, HG_WIDTH), F32),
                   jax.ShapeDtypeStruct((HG_HEADS, N_CHUNKS, K, K), F32)],
        scratch_shapes=[pltpu.VMEM((HG_HEADS, K, K), F32)],
        compiler_params=pltpu.CompilerParams(dimension_semantics=("arbitrary",)),
    )(z, z, z, z, lb, gain)


def hgrn_bwd(z, lb, gain, o_raw, states, dy, name):
    C, K = HG_CHUNK, HG_DIM

    def body(q_ref, f_ref, v_ref, og_ref, p_ref, g_ref, o_ref, st_ref, dy_ref,
             dq_ref, dfp_ref, dv_ref, dog_ref, dlb_ref, dgain_ref, dstate):
        @pl.when(pl.program_id(0) == 0)
        def _():
            dstate[...] = jnp.zeros_like(dstate)
            dlb_ref[...] = jnp.zeros_like(dlb_ref)
            dgain_ref[...] = jnp.zeros_like(dgain_ref)

        last = lax.broadcasted_iota(jnp.int32, (C, K), 0) == C - 1
        row = lax.broadcasted_iota(jnp.int32, (C, C), 0)
        col = lax.broadcasted_iota(jnp.int32, (C, C), 1)
        anti_causal = (col >= row).astype(F32)
        for cc in reversed(range(HG_STEP_CHUNKS)):
            rows = pl.ds(cc * C, C)
            for h in range(HG_HEADS):
                hd = pl.ds(h * K, K)
                v = v_ref[rows, hd]
                lb = p_ref[:, hd]
                sg, f, causal, eG, enG, qg, kg, A, egl = _hgrn_gates(q_ref[rows, hd], f_ref[rows, hd], lb)
                kd = kg * egl
                st = st_ref[h, cc]
                dst = dstate[h]
                o = o_ref[rows, hd]
                og = og_ref[rows, hd]
                gain_v = g_ref[:, hd]
                dyv = dy_ref[rows, hd]
                rs = lax.rsqrt(jnp.mean(o * o, axis=-1, keepdims=True) + EPS)
                on = o * rs
                sgo = _sigmoid(og)
                silu = og * sgo
                dog_ref[rows, hd] = (dyv * (on * gain_v) * (sgo * (1.0 + og * (1.0 - sgo)))).astype(dog_ref.dtype)
                dgain_ref[:, hd] += jnp.sum(dyv * silu * on, axis=0, keepdims=True)
                don = dyv * gain_v * silu
                do = rs * (don - on * jnp.mean(don * on, axis=-1, keepdims=True))
                dA = jnp.where(causal, _hdot(do, v, tb=True), 0.0)
                dv_ref[rows, hd] = (_hdot(A, do, ta=True) + _hdot(kd, dst, tb=True)).astype(dv_ref.dtype)
                dqg = _hdot(dA, kg) + _hdot(do, st)
                dkg = _hdot(dA, qg, ta=True)
                dkd = _hdot(v, dst)
                dstate[h] = dst * egl + _hdot(do, qg, ta=True)
                dgl = jnp.sum(st * dst, axis=0, keepdims=True) * egl
                dq_ref[rows, hd] = (dqg * eG).astype(dq_ref.dtype)
                dk = dkg * enG + dkd * (enG * egl)
                dG = dqg * qg - dkg * kg - dkd * kd
                extra = jnp.sum(dkd * kd, axis=0, keepdims=True) + dgl
                dG = dG + jnp.where(last, extra, 0.0)
                dlf = _dot_f32(anti_causal, dG)
                df = dlf / f - dk
                dfp_ref[rows, hd] = (df * (1.0 - lb) * (sg * (1.0 - sg))).astype(dfp_ref.dtype)
                dlb_ref[:, hd] += jnp.sum(df * (1.0 - sg), axis=0, keepdims=True)

    R = HG_STEP_CHUNKS * C
    n_steps = N_CHUNKS // HG_STEP_CHUNKS

    def rc(c):
        return n_steps - 1 - c

    def zcol(section):
        return pl.BlockSpec((R, HG_WIDTH), lambda c: (rc(c), section))

    vec = pl.BlockSpec((1, HG_WIDTH), lambda c: (0, 0))
    blk = pl.BlockSpec((R, HG_WIDTH), lambda c: (rc(c), 0))
    out = jax.ShapeDtypeStruct((SEQ, HG_WIDTH), MXU_DTYPE)
    small = jax.ShapeDtypeStruct((1, HG_WIDTH), F32)
    return pl.pallas_call(
        body, name=name, grid=(n_steps,),
        in_specs=[zcol(0), zcol(1), zcol(2), zcol(3), vec, vec, blk,
                  pl.BlockSpec((HG_HEADS, HG_STEP_CHUNKS, K, K), lambda c: (0, rc(c), 0, 0)), blk],
        out_specs=[blk, blk, blk, blk, vec, vec],
        out_shape=[out, out, out, out, small, small],
        scratch_shapes=[pltpu.VMEM((HG_HEADS, K, K), F32)],
        compiler_params=pltpu.CompilerParams(dimension_semantics=("arbitrary",)),
    )(z, z, z, z, lb, gain, o_raw, states, dy)


N_GROUPS = len(ATT_GROUPS)
HEAD_PAIRS = ATT_WIDTH // 128
ATT_COL0 = 4 * HG_WIDTH
UNROLLED_UNITS = 4
ATT_SLAB_BLOCKS = 4


def _alibi_coef():
    n = N_GROUPS * ATT_HEADS
    slopes = np.exp2(-ALIBI_MAX * np.arange(1, n + 1, dtype=np.float32) / n).astype(np.float32)
    dil = np.repeat(np.array([d for _, d in ATT_GROUPS], np.float32), ATT_HEADS)
    return jnp.asarray(slopes * dil, F32)


def _for_each_unit(n, fn):
    if n <= UNROLLED_UNITS:
        for u in range(n):
            fn(u)
    else:
        def group(i, carry):
            for j in range(UNROLLED_UNITS):
                fn(i * UNROLLED_UNITS + j)
            return carry
        lax.fori_loop(0, n // UNROLLED_UNITS, group, 0)


def _att_specs(g):
    B = ATT_BLOCK
    d = ATT_GROUPS[g][1]
    blocks = ATT_SLAB_BLOCKS if d == 1 else 1
    R = B * d * blocks
    n_slabs = SEQ // R
    multi = SEQ // d > B
    col0 = (ATT_COL0 + g * 3 * ATT_WIDTH) // 128

    def cur(col):
        return pl.BlockSpec((R, 128), lambda hp, s: (s, col + hp))

    def prev(col):
        return pl.BlockSpec((R, 128), lambda hp, s: (jnp.maximum(s - 1, 0), col + hp))

    def nxt(col):
        return pl.BlockSpec((R, 128), lambda hp, s: (jnp.minimum(s + 1, n_slabs - 1), col + hp))

    def unit(u, s):
        if d > 1:
            rows = pl.ds(u, B, stride=d)
            return rows, False, rows, jnp.where(s == 0, B, 0), False, rows, jnp.where(s == n_slabs - 1, B, 0)
        rows = pl.ds(u * B, B)
        inner_prev, inner_next = u > 0, u < blocks - 1
        return (rows, inner_prev, pl.ds((u - 1) * B if inner_prev else (blocks - 1) * B, B),
                0 if inner_prev else jnp.where(s == 0, B, 0),
                inner_next, pl.ds((u + 1) * B if inner_next else 0, B),
                0 if inner_next else jnp.where(s == n_slabs - 1, B, 0))

    return d * blocks, R, n_slabs, multi, col0, cur, prev, nxt, unit


def _head_lanes(j):
    lane = lax.broadcasted_iota(jnp.int32, (ATT_BLOCK, 128), 1)
    return (lane >= 64 * j) & (lane < 64 * (j + 1))


def _lane_value(x, sel):
    return jnp.max(jnp.where(sel, x, -3e38), axis=-1, keepdims=True)


def _stack_heads(x, sel0):
    return jnp.concatenate([jnp.where(sel0, x, 0.0), jnp.where(sel0, 0.0, x)], axis=0)


def _stack_values(x, sel0, lanes):
    swapped = pltpu.roll(x, 64, 1)
    stacked = jnp.concatenate([jnp.where(sel0, x, swapped), jnp.where(sel0, swapped, x)], axis=0)
    return stacked if lanes == 128 else jnp.concatenate([stacked] * (lanes // 128), axis=1)


def _pair_coef(coef_ref, g, hp):
    row = lax.broadcasted_iota(jnp.int32, (2 * ATT_BLOCK, 1), 0)
    first = g * ATT_HEADS + hp * 2
    return jnp.where(row < ATT_BLOCK, coef_ref[first], coef_ref[first + 1])


def _band(with_prev, first_key):
    B = ATT_BLOCK
    keys = 2 * B if with_prev else B
    qi = jnp.bitwise_and(lax.broadcasted_iota(jnp.int32, (2 * B, keys), 0), B - 1)
    kj = lax.broadcasted_iota(jnp.int32, (2 * B, keys), 1)
    delta = qi + (B if with_prev else 0) - kj
    valid = (delta >= 0) & (delta <= B)
    if with_prev:
        valid = valid & (kj >= first_key)
    return valid, delta.astype(F32)


def _band_next(first_key):
    B = ATT_BLOCK
    qi = jnp.bitwise_and(lax.broadcasted_iota(jnp.int32, (2 * B, B), 0), B - 1)
    kj = lax.broadcasted_iota(jnp.int32, (2 * B, B), 1)
    delta = qi + B - kj
    return (delta <= B) & (kj >= first_key), delta.astype(F32)


def att_fwd(z, g, name):
    B = ATT_BLOCK
    n_units, R, n_slabs, has_prev, col0, cur, prev, _, unit = _att_specs(g)

    def body(coef_ref, *refs):
        if has_prev:
            q_ref, kc_ref, vc_ref, kp_ref, vp_ref, o_ref, l_ref = refs
        else:
            q_ref, kc_ref, vc_ref, o_ref, l_ref = refs
        hp, s = pl.program_id(0), pl.program_id(1)
        cf2 = _pair_coef(coef_ref, g, hp)
        sel0 = _head_lanes(0)

        def one(u):
            rows, inner_prev, prev_rows, first_key, _, _, _ = unit(u, s)
            valid, dist = _band(has_prev, first_key)
            q2 = _stack_heads(q_ref[rows, :], sel0)
            kk, vv = kc_ref[rows, :], vc_ref[rows, :]
            if has_prev:
                k_from, v_from = (kc_ref, vc_ref) if inner_prev else (kp_ref, vp_ref)
                kk = jnp.concatenate([k_from[prev_rows, :], kk], axis=0)
                vv = jnp.concatenate([v_from[prev_rows, :], vv], axis=0)
            sc = jnp.where(valid, _dot(q2, kk, tb=True) * 0.125 - cf2 * dist, NEG_INF)
            mx = jnp.max(sc, axis=-1, keepdims=True)
            e = jnp.exp(sc - mx)
            den = jnp.sum(e, axis=-1, keepdims=True)
            o2 = _dot(e * (1.0 / den), vv)
            lse2 = mx + jnp.log(den)
            o_ref[rows, :] = jnp.where(sel0, o2[:B], o2[B:])
            l_ref[rows, :] = jnp.where(sel0, lse2[:B], lse2[B:])

        _for_each_unit(n_units, one)

    in_specs = [pl.BlockSpec(memory_space=pltpu.SMEM), cur(col0), cur(col0 + 4), cur(col0 + 8)]
    args = [_alibi_coef(), z, z, z]
    if has_prev:
        in_specs += [prev(col0 + 4), prev(col0 + 8)]
        args += [z, z]
    out = jax.ShapeDtypeStruct((SEQ, ATT_WIDTH), F32)
    return pl.pallas_call(
        body, name=name, grid=(HEAD_PAIRS, n_slabs), in_specs=in_specs,
        out_specs=[cur(0), cur(0)], out_shape=[out, out],
        compiler_params=pltpu.CompilerParams(dimension_semantics=("parallel", "arbitrary")),
    )(*args)


def att_bwd(z, l, do, corr, g, name):
    B = ATT_BLOCK
    n_units, R, n_slabs, neighbours, col0, cur, prev, nxt, unit = _att_specs(g)

    def body(coef_ref, *refs):
        if neighbours:
            (q_ref, kc_ref, vc_ref, l_ref, do_ref, cr_ref, kp_ref, vp_ref, qn_ref, ln_ref, don_ref, crn_ref,
             dq_ref, dk_ref, dv_ref, dq_sc, dk_sc, dv_sc) = refs
        else:
            q_ref, kc_ref, vc_ref, l_ref, do_ref, cr_ref, dq_ref, dk_ref, dv_ref, dq_sc, dk_sc, dv_sc = refs
        hp, s = pl.program_id(0), pl.program_id(1)
        cf2 = _pair_coef(coef_ref, g, hp)
        sel0 = _head_lanes(0)
        own = slice(B, 2 * B) if neighbours else slice(0, B)

        def one(u):
            rows, inner_prev, prev_rows, first_key, inner_next, next_rows, first_key_n = unit(u, s)
            valid, dist = _band(neighbours, first_key)
            kc, vc = kc_ref[rows, :], vc_ref[rows, :]
            kk, vv = kc, vc
            if neighbours:
                k_from, v_from = (kc_ref, vc_ref) if inner_prev else (kp_ref, vp_ref)
                kk = jnp.concatenate([k_from[prev_rows, :], kc], axis=0)
                vv = jnp.concatenate([v_from[prev_rows, :], vc], axis=0)
            q2, do2 = _stack_heads(q_ref[rows, :], sel0), _stack_heads(do_ref[rows, :], sel0)
            keys = kk.shape[0]
            lse2, cr2 = _stack_values(l_ref[rows, :], sel0, keys), _stack_values(cr_ref[rows, :], sel0, keys)
            p = jnp.exp(jnp.where(valid, _dot(q2, kk, tb=True) * 0.125 - cf2 * dist, NEG_INF) - lse2)
            ds = p * (_dot(do2, vv, tb=True) + cr2)
            dq2 = _dot(ds, kk)
            dk = _dot(ds, q2, ta=True)[own]
            dv = _dot(p, do2, ta=True)[own]
            if neighbours:
                valid_n, dist_n = _band_next(first_key_n)
                q_from, l_from, do_from, cr_from = ((q_ref, l_ref, do_ref, cr_ref) if inner_next
                                                    else (qn_ref, ln_ref, don_ref, crn_ref))
                qn2, don2 = _stack_heads(q_from[next_rows, :], sel0), _stack_heads(do_from[next_rows, :], sel0)
                lse_n2 = _stack_values(l_from[next_rows, :], sel0, B)
                cr_n2 = _stack_values(cr_from[next_rows, :], sel0, B)
                p_n = jnp.exp(jnp.where(valid_n, _dot(qn2, kc, tb=True) * 0.125 - cf2 * dist_n, NEG_INF) - lse_n2)
                ds_n = p_n * (_dot(don2, vc, tb=True) + cr_n2)
                dk = dk + _dot(ds_n, qn2, ta=True)
                dv = dv + _dot(p_n, don2, ta=True)
            dq_sc[rows, :] = jnp.where(sel0, dq2[:B], dq2[B:]) * 0.125
            dk_sc[rows, :] = dk * 0.125
            dv_sc[rows, :] = dv

        _for_each_unit(n_units, one)
        dq_ref[...] = dq_sc[...].astype(dq_ref.dtype)
        dk_ref[...] = dk_sc[...].astype(dk_ref.dtype)
        dv_ref[...] = dv_sc[...].astype(dv_ref.dtype)

    in_specs = [pl.BlockSpec(memory_space=pltpu.SMEM), cur(col0), cur(col0 + 4), cur(col0 + 8), cur(0), cur(0), cur(0)]
    args = [_alibi_coef(), z, z, z, l, do, corr]
    if neighbours:
        in_specs += [prev(col0 + 4), prev(col0 + 8), nxt(col0), nxt(0), nxt(0), nxt(0)]
        args += [z, z, z, l, do, corr]
    out = jax.ShapeDtypeStruct((SEQ, ATT_WIDTH), MXU_DTYPE)
    return pl.pallas_call(
        body, name=name, grid=(HEAD_PAIRS, n_slabs), in_specs=in_specs,
        out_specs=[cur(0)] * 3, out_shape=[out] * 3,
        scratch_shapes=[pltpu.VMEM((R, 128), F32)] * 3,
        compiler_params=pltpu.CompilerParams(dimension_semantics=("parallel", "arbitrary"),
                                             vmem_limit_bytes=MATMUL_VMEM_BYTES),
    )(*args)


def _head_sum(x):
    i = lax.broadcasted_iota(jnp.int32, (128, 128), 0) // 64
    j = lax.broadcasted_iota(jnp.int32, (128, 128), 1) // 64
    return _dot_f32(x, (i == j).astype(F32), ones_on_right=True)


def _group_weights(l0, l1, l2):
    mx = jnp.maximum(jnp.maximum(l0, l1), l2)
    e0, e1, e2 = jnp.exp(l0 - mx), jnp.exp(l1 - mx), jnp.exp(l2 - mx)
    inv = 1.0 / (e0 + e1 + e2)
    return e0 * inv, e1 * inv, e2 * inv


def att_combine_fwd(o, l, name):
    def body(o0, o1, o2, l0, l1, l2, y_ref):
        w0, w1, w2 = _group_weights(l0[...], l1[...], l2[...])
        y_ref[...] = (o0[...] * w0 + o1[...] * w1 + o2[...] * w2).astype(y_ref.dtype)

    blk = pl.BlockSpec((ROW_TILE, ATT_WIDTH), lambda i: (i, 0))
    return pl.pallas_call(
        body, name=name, grid=(SEQ // ROW_TILE,), in_specs=[blk] * 6, out_specs=blk,
        out_shape=jax.ShapeDtypeStruct((SEQ, ATT_WIDTH), MXU_DTYPE),
    )(*o, *l)


def att_combine_bwd(o, l, dy, name):
    def body(o0, o1, o2, l0, l1, l2, dy_ref, do0, do1, do2, cr0, cr1, cr2):
        w = _group_weights(l0[...], l1[...], l2[...])
        dyv = dy_ref[...]
        dw = [_head_sum(dyv * o_ref[...]) for o_ref in (o0, o1, o2)]
        tot = w[0] * dw[0] + w[1] * dw[1] + w[2] * dw[2]
        for g, (do_ref, cr_ref) in enumerate(((do0, cr0), (do1, cr1), (do2, cr2))):
            do_ref[...] = dyv * w[g]
            cr_ref[...] = -w[g] * tot

    blk = pl.BlockSpec((ROW_TILE, 128), lambda i, j: (i, j))
    out = jax.ShapeDtypeStruct((SEQ, ATT_WIDTH), F32)
    res = pl.pallas_call(
        body, name=name, grid=(SEQ // ROW_TILE, HEAD_PAIRS), in_specs=[blk] * 7, out_specs=[blk] * 6, out_shape=[out] * 6,
    )(*o, *l, dy)
    return res[:N_GROUPS], res[N_GROUPS:]


SUM_ROW_TILES = (1024, 512, 256, 128, 64, 32, 16)
SUM_TILE_BYTES = 24 * 1024 * 1024
SUM_PARAMS = pltpu.CompilerParams(vmem_limit_bytes=MATMUL_VMEM_BYTES)


def _row_tile(rows, cols, operands):
    fit = [t for t in SUM_ROW_TILES if rows % t == 0]
    return next((t for t in fit if 2 * 4 * operands * t * cols <= SUM_TILE_BYTES), fit[-1])


def _shard_shape(rows, cols, axis):
    return (rows // N_CHIPS, cols) if axis == 0 else (rows, cols // N_CHIPS)


def _half_shape(rows, cols, axis):
    return (rows, cols // 2) if axis == 0 else (rows // 2, cols)


def _piece_shape(rows, cols, axis):
    return (rows // N_CHIPS, cols // 2) if axis == 0 else (rows // 2, cols // N_CHIPS)


def place_own_block(shard, chip, rows, cols, axis, name):
    sr, sc = _shard_shape(rows, cols, axis)
    tr = _row_tile(sr, sc, 2)

    def body(chip_ref, s_ref, o_ref):
        o_ref[...] = s_ref[...].astype(o_ref.dtype)

    if axis == 0:
        out_map = lambda i, chip_ref: (chip_ref[0] * (sr // tr) + i, 0)
    else:
        out_map = lambda i, chip_ref: (i, chip_ref[0])
    return pl.pallas_call(
        body, name=name, out_shape=jax.ShapeDtypeStruct((rows, cols), WEIGHT_COMM_DTYPE), compiler_params=SUM_PARAMS,
        grid_spec=pltpu.PrefetchScalarGridSpec(
            num_scalar_prefetch=1, grid=(sr // tr,), in_specs=[pl.BlockSpec((tr, sc), lambda i, chip_ref: (i, 0))],
            out_specs=pl.BlockSpec((tr, sc), out_map)),
    )(chip, shard)


def add_halves(g, theirs, core, rows, cols, axis, name):
    hr, hc = _half_shape(rows, cols, axis)
    tr = _row_tile(hr, hc, 3)

    def body(core_ref, g_ref, t_ref, o_ref):
        o_ref[...] = (g_ref[...].astype(F32) + t_ref[...].astype(F32)).astype(o_ref.dtype)

    if axis == 0:
        g_map = lambda i, core_ref: (i, core_ref[0])
    else:
        g_map = lambda i, core_ref: (core_ref[0] * (hr // tr) + i, 0)
    blk = pl.BlockSpec((tr, hc), lambda i, core_ref: (i, 0))
    return pl.pallas_call(
        body, name=name, out_shape=jax.ShapeDtypeStruct((hr, hc), GRAD_COMM_DTYPE), compiler_params=SUM_PARAMS,
        grid_spec=pltpu.PrefetchScalarGridSpec(
            num_scalar_prefetch=1, grid=(hr // tr,), in_specs=[pl.BlockSpec((tr, hc), g_map), blk], out_specs=blk),
    )(core, g, theirs)


def add_pieces(half, got, chip, rows, cols, axis, name):
    hr, _ = _half_shape(rows, cols, axis)
    pr, pc = _piece_shape(rows, cols, axis)
    tr = _row_tile(pr, pc, 5)

    def body(chip_ref, h_ref, got_ref, o_ref):
        o_ref[...] = (h_ref[...].astype(F32) + got_ref[0].astype(F32) + got_ref[1].astype(F32) + got_ref[2].astype(F32))

    if axis == 0:
        h_map = lambda i, chip_ref: (chip_ref[0] * (pr // tr) + i, 0)
    else:
        h_map = lambda i, chip_ref: (i, chip_ref[0])
    return pl.pallas_call(
        body, name=name, out_shape=jax.ShapeDtypeStruct((pr, pc), F32), compiler_params=SUM_PARAMS,
        grid_spec=pltpu.PrefetchScalarGridSpec(
            num_scalar_prefetch=1, grid=(pr // tr,),
            in_specs=[pl.BlockSpec((tr, pc), h_map), pl.BlockSpec((3, tr, pc), lambda i, chip_ref: (0, i, 0))],
            out_specs=pl.BlockSpec((tr, pc), lambda i, chip_ref: (i, 0))),
    )(chip, half, got)


def _adamw_math(w, g, m, v):
    nm = ADAM_B1 * m + (1.0 - ADAM_B1) * g
    nv = ADAM_B2 * v + (1.0 - ADAM_B2) * (g * g)
    m_hat = nm / (1.0 - ADAM_B1 ** ADAM_STEP)
    v_hat = nv / (1.0 - ADAM_B2 ** ADAM_STEP)
    return -ADAM_LR * (m_hat / (jnp.sqrt(v_hat) + ADAM_EPS) + ADAM_WD * w), nm, nv


def adamw_halves(w, mine, theirs, m, v, core, rows, cols, axis, name):
    sr, sc = _shard_shape(rows, cols, axis)
    pr, pc = _piece_shape(rows, cols, axis)
    tr = _row_tile(pr, pc, 9)
    nt = pr // tr

    def body(core_ref, w_ref, a_ref, b_ref, m_ref, v_ref, g_ref, d_ref, nm_ref, nv_ref):
        g = jnp.where(pl.program_id(0) == core_ref[0], a_ref[...], b_ref[...])
        g_ref[...] = g
        d_ref[...], nm_ref[...], nv_ref[...] = _adamw_math(w_ref[...], g, m_ref[...], v_ref[...])

    if axis == 0:
        full = pl.BlockSpec((tr, pc), lambda h, i, core_ref: (i, h))
    else:
        full = pl.BlockSpec((tr, pc), lambda h, i, core_ref: (h * nt + i, 0))
    part = pl.BlockSpec((tr, pc), lambda h, i, core_ref: (i, 0))
    out = jax.ShapeDtypeStruct((sr, sc), F32)
    return pl.pallas_call(
        body, name=name, out_shape=[out, out, out, out], compiler_params=SUM_PARAMS,
        grid_spec=pltpu.PrefetchScalarGridSpec(
            num_scalar_prefetch=1, grid=(2, nt), in_specs=[full, part, part, full, full], out_specs=[full] * 4),
    )(core, w, mine, theirs, m, v)


BIG = (
    ("ffn1_w_gate_up", D_MODEL, 2 * D_FF, 1),
    ("ffn1_w_down", D_FF, D_MODEL, 0),
    ("w_in", D_MODEL, IN_COLS, 1),
    ("w_branch_hg", HG_WIDTH, D_MODEL, 1),
    ("w_branch_att", ATT_WIDTH, D_MODEL, 1),
    ("w_out", D_MODEL, D_MODEL, 0),
    ("ffn2_w_gate_up", D_MODEL, 2 * D_FF, 1),
    ("ffn2_w_down", D_FF, D_MODEL, 0),
)
N_BIG = len(BIG)
ANY = pl.BlockSpec(memory_space=pl.ANY)


def _place():
    return lax.axis_index("x"), lax.axis_index("y"), lax.axis_index("c")


def _other_chips(x, y):
    return ((1 - x, y), (x, 1 - y), (1 - x, 1 - y))


MAX_COPY_CHUNKS = 16
CHUNK_ROW_ALIGN = 16


def _row_chunks(view):
    rows = view.shape[0]
    n = next(n for n in range(MAX_COPY_CHUNKS, 0, -1) if rows % (CHUNK_ROW_ALIGN * n) == 0 or n == 1)
    step = rows // n
    return [pl.ds(i * step, step) for i in range(n)]


def _remote(src, dst, send_sem, recv_sem, device):
    return pltpu.make_async_remote_copy(src_ref=src, dst_ref=dst, send_sem=send_sem, recv_sem=recv_sem,
                                        device_id=device, device_id_type=MESH)


def _start_remote(src, dst, send_sem, recv_sem, device):
    for rows in _row_chunks(src):
        _remote(src.at[rows, :], dst.at[rows, :], send_sem, recv_sem, device).start()
    return _remote(src, dst, send_sem, recv_sem, device)


HBM = pl.BlockSpec(memory_space=pltpu.HBM)
SEM = pl.BlockSpec(memory_space=pltpu.SEMAPHORE)
SPLIT_COPY_EFFECT = pltpu.SideEffectType.DATAFLOW_SIDE_EFFECTING
GROUPS = {"ffn1": (0, 1), "mix": (2, 3, 4, 5), "ffn2": (6, 7)}


def _in_hbm(a):
    return pltpu.with_memory_space_constraint(a, pltpu.HBM)


class _SemList:
    def __init__(self, refs):
        self.refs = refs
        self.at = self

    def __getitem__(self, index):
        w, k = index
        return self.refs[3 * w + k]


def _gather_piece(ref, rows, cols, axis, chip, c):
    sr, sc = _shard_shape(rows, cols, axis)
    j = 2 * chip[0] + chip[1]
    if axis == 0:
        return ref.at[pl.ds(j * sr + c * (sr // 2), sr // 2), :]
    return ref.at[pl.ds(c * (sr // 2), sr // 2), pl.ds(pl.multiple_of(j * sc, 128), sc)]


def _start_gather_sends(bufs, ws, send_sems, recv_sems):
    x, y, c = _place()
    for w, (_, r, cc, ax) in enumerate(ws):
        mine = _gather_piece(bufs[w], r, cc, ax, (x, y), c)
        for k, chip in enumerate(_other_chips(x, y)):
            _start_remote(mine, mine, send_sems.at[w, k], recv_sems.at[w, k], (*chip, c))


def _wait_gather_sends(bufs, ws, send_sems, recv_sems):
    x, y, c = _place()
    for w, (_, r, cc, ax) in enumerate(ws):
        for k, chip in enumerate(_other_chips(x, y)):
            got = _gather_piece(bufs[w], r, cc, ax, chip, c)
            _remote(got, got, send_sems.at[w, k], recv_sems.at[w, k], (x, y, c)).wait_recv()
    for w, (_, r, cc, ax) in enumerate(ws):
        mine = _gather_piece(bufs[w], r, cc, ax, (x, y), c)
        for k in range(3):
            _remote(mine, mine, send_sems.at[w, k], recv_sems.at[w, k], (x, y, c)).wait_send()


def _forward_halves(bufs, ws, send_sems, recv_sems):
    x, y, c = _place()
    passed = []
    for w, (_, r, cc, ax) in enumerate(ws):
        for k, chip in enumerate(_other_chips(x, y)):
            got = _gather_piece(bufs[w], r, cc, ax, chip, c)
            passed.append(_start_remote(got, got, send_sems.at[w, k], recv_sems.at[w, k], (x, y, 1 - c)))
    for w, (_, r, cc, ax) in enumerate(ws):
        for k, chip in enumerate(_other_chips(x, y)):
            got = _gather_piece(bufs[w], r, cc, ax, chip, 1 - c)
            _remote(got, got, send_sems.at[w, k], recv_sems.at[w, k], (x, y, c)).wait_recv()
    for cp in passed:
        cp.wait_send()


def gather_start(placed, after, group):
    ws = [BIG[i] for i in GROUPS[group]]
    n = len(ws)

    def body(*refs):
        bufs = refs[:n]
        send_sems, recv_sems = _SemList(refs[n + 1:4 * n + 1]), _SemList(refs[4 * n + 1:7 * n + 1])
        token = refs[-1]
        _start_gather_sends(bufs, ws, send_sems, recv_sems)
        token[...] = jnp.zeros_like(token)

    out = pl.pallas_call(
        body, name=f"gather_start_{group}", in_specs=[HBM] * n + [ANY],
        out_specs=[SEM] * (6 * n) + [HBM] * n + [pl.BlockSpec(memory_space=pltpu.VMEM)],
        out_shape=[pltpu.SemaphoreType.DMA(())] * (6 * n)
        + [pltpu.HBM((r, cc), WEIGHT_COMM_DTYPE) for _, r, cc, _ in ws] + [jax.ShapeDtypeStruct((8, 128), F32)],
        input_output_aliases={w: 6 * n + w for w in range(n)},
        compiler_params=pltpu.CompilerParams(has_side_effects=SPLIT_COPY_EFFECT),
    )(*[_in_hbm(p) for p in placed], after)
    return out[:3 * n], out[3 * n:6 * n], out[6 * n:7 * n], out[-1]


def gather_wait(bufs, send_sems, recv_sems, after, group):
    ws = [BIG[i] for i in GROUPS[group]]
    n = len(ws)

    def body(*refs):
        _wait_gather_sends(refs[:n], ws, _SemList(refs[n:n + 3 * n]), _SemList(refs[n + 3 * n:n + 6 * n]))

    return pl.pallas_call(
        body, name=f"gather_wait_{group}", in_specs=[HBM] * n + [SEM] * (6 * n) + [ANY] * len(after), out_specs=[HBM] * n,
        out_shape=[pltpu.HBM((r, cc), WEIGHT_COMM_DTYPE) for _, r, cc, _ in ws],
        input_output_aliases={w: w for w in range(n)},
        compiler_params=pltpu.CompilerParams(has_side_effects=SPLIT_COPY_EFFECT),
    )(*bufs, *send_sems, *recv_sems, *after)


def gather_forward(bufs, group):
    ws = [BIG[i] for i in GROUPS[group]]
    n = len(ws)

    def body(*refs):
        _forward_halves(refs[n:2 * n], ws, refs[2 * n], refs[2 * n + 1])

    return pl.pallas_call(
        body, name=f"gather_forward_{group}", in_specs=[ANY] * n, out_specs=[ANY] * n,
        out_shape=[jax.ShapeDtypeStruct((r, cc), WEIGHT_COMM_DTYPE) for _, r, cc, _ in ws],
        input_output_aliases={w: w for w in range(n)},
        scratch_shapes=[pltpu.SemaphoreType.DMA((n, 3))] * 2,
    )(*bufs)


def _half(ref, rows, cols, axis, c):
    if axis == 0:
        return ref.at[:, pl.ds(pl.multiple_of(c * (cols // 2), 128), cols // 2)]
    return ref.at[pl.ds(c * (rows // 2), rows // 2), :]


def _piece_of_half(ref, rows, cols, axis, chip):
    j = 2 * chip[0] + chip[1]
    pr, pc = _piece_shape(rows, cols, axis)
    if axis == 0:
        return ref.at[pl.ds(j * pr, pr), :]
    return ref.at[:, pl.ds(pl.multiple_of(j * pc, 128), pc)]


def exchange_halves(grads, group):
    ws = [BIG[i] for i in GROUPS[group]]
    n = len(ws)

    def body(*refs):
        ins, theirs = refs[:n], refs[n:2 * n]
        send_sems, recv_sems = refs[2 * n:]
        x, y, c = _place()
        copies = [_start_remote(_half(ins[w], r, cc, ax, 1 - c), theirs[w], send_sems.at[w], recv_sems.at[w], (x, y, 1 - c))
                  for w, (_, r, cc, ax) in enumerate(ws)]
        for cp in copies:
            cp.wait()

    return pl.pallas_call(
        body, name=f"exchange_halves_{group}", in_specs=[ANY] * n, out_specs=[ANY] * n,
        out_shape=[jax.ShapeDtypeStruct(_half_shape(r, cc, ax), GRAD_COMM_DTYPE) for _, r, cc, ax in ws],
        scratch_shapes=[pltpu.SemaphoreType.DMA((n,)), pltpu.SemaphoreType.DMA((n,))],
    )(*grads)


def _scatter_copies(halves, got, ws, send_sems, recv_sems, start):
    x, y, c = _place()
    copies = []
    for w, (_, r, cc, ax) in enumerate(ws):
        for k, chip in enumerate(_other_chips(x, y)):
            args = (_piece_of_half(halves[w], r, cc, ax, chip), got[w].at[k], send_sems.at[w, k], recv_sems.at[w, k], (*chip, c))
            copies.append(_start_remote(*args) if start else _remote(*args))
    return copies


def scatter_start(halves, group):
    ws = [BIG[i] for i in GROUPS[group]]
    n = len(ws)

    def body(*refs):
        sems = refs[2 * n:8 * n]
        _scatter_copies(refs[:n], refs[n:2 * n], ws, _SemList(sems[:3 * n]), _SemList(sems[3 * n:]), start=True)
        refs[-1][...] = jnp.zeros_like(refs[-1])

    landing = [lax.empty((3,) + _piece_shape(r, cc, ax), GRAD_COMM_DTYPE) for _, r, cc, ax in ws]
    out = pl.pallas_call(
        body, name=f"scatter_start_{group}", in_specs=[HBM] * (2 * n),
        out_specs=[SEM] * (6 * n) + [HBM] * (2 * n) + [pl.BlockSpec(memory_space=pltpu.VMEM)],
        out_shape=[pltpu.SemaphoreType.DMA(())] * (6 * n)
        + [pltpu.HBM(_half_shape(r, cc, ax), GRAD_COMM_DTYPE) for _, r, cc, ax in ws]
        + [pltpu.HBM((3,) + _piece_shape(r, cc, ax), GRAD_COMM_DTYPE) for _, r, cc, ax in ws]
        + [jax.ShapeDtypeStruct((8, 128), F32)],
        input_output_aliases={i: 6 * n + i for i in range(2 * n)},
        compiler_params=pltpu.CompilerParams(has_side_effects=SPLIT_COPY_EFFECT),
    )(*[_in_hbm(h) for h in halves], *[_in_hbm(b) for b in landing])
    return out[:3 * n], out[3 * n:6 * n], out[6 * n:7 * n], out[7 * n:8 * n], out[-1]


def scatter_wait(halves, got, send_sems, recv_sems, after, group):
    ws = [BIG[i] for i in GROUPS[group]]
    n = len(ws)

    def body(*refs):
        sems = refs[2 * n:8 * n]
        for cp in _scatter_copies(refs[:n], refs[n:2 * n], ws, _SemList(sems[:3 * n]), _SemList(sems[3 * n:]), start=False):
            cp.wait_send()
            cp.wait_recv()

    out = pl.pallas_call(
        body, name=f"scatter_wait_{group}", in_specs=[HBM] * (2 * n) + [SEM] * (6 * n) + [ANY] * len(after),
        out_specs=[HBM] * (2 * n),
        out_shape=[pltpu.HBM(_half_shape(r, cc, ax), GRAD_COMM_DTYPE) for _, r, cc, ax in ws]
        + [pltpu.HBM((3,) + _piece_shape(r, cc, ax), GRAD_COMM_DTYPE) for _, r, cc, ax in ws],
        input_output_aliases={i: i for i in range(2 * n)},
        compiler_params=pltpu.CompilerParams(has_side_effects=SPLIT_COPY_EFFECT),
    )(*halves, *got, *send_sems, *recv_sems, *after)
    return out[:n], out[n:]


def exchange_reduced(pieces, group):
    ws = [BIG[i] for i in GROUPS[group]]
    n = len(ws)

    def body(*refs):
        ins, theirs = refs[:n], refs[n:2 * n]
        send_sems, recv_sems = refs[2 * n:]
        x, y, c = _place()
        copies = [_start_remote(ins[w], theirs[w], send_sems.at[w], recv_sems.at[w], (x, y, 1 - c)) for w in range(n)]
        for cp in copies:
            cp.wait()

    return pl.pallas_call(
        body, name=f"exchange_reduced_{group}", in_specs=[ANY] * n, out_specs=[ANY] * n,
        out_shape=[jax.ShapeDtypeStruct(_piece_shape(r, cc, ax), F32) for _, r, cc, ax in ws],
        scratch_shapes=[pltpu.SemaphoreType.DMA((n,)), pltpu.SemaphoreType.DMA((n,))],
    )(*pieces)


N_DEV = 8
SMALL = ("ffn1_norm", "mix_norm", "hg_lower_bounds", "hg_out_norm", "ffn2_norm", "final_norm")
SMALL_SLOT_ROWS = 8


def small_step(loss, grads, w, m, v, behind):
    n = len(SMALL)
    slots = n + 1
    shapes = [g.shape for g in grads]

    def body(*refs):
        loss_ref, g_refs, w_refs, m_refs, v_refs = refs[0], refs[1:1 + n], refs[1 + n:1 + 2 * n], refs[1 + 2 * n:1 + 3 * n], refs[1 + 3 * n:1 + 4 * n]
        outs = refs[2 + 4 * n:3 + 8 * n]
        loss_out, dg_refs, d_refs, nm_refs, nv_refs = outs[0], outs[1:1 + n], outs[1 + n:1 + 2 * n], outs[1 + 2 * n:1 + 3 * n], outs[1 + 3 * n:]
        stage, gathered, send_sems, recv_sems = refs[3 + 8 * n:]
        x, y, c = _place()
        me = 4 * x + 2 * y + c

        def slot(i, shape):
            return pl.ds(i * SMALL_SLOT_ROWS, shape[0]), pl.ds(0, shape[1])

        stage[...] = jnp.zeros_like(stage)
        for i, g_ref in enumerate(g_refs):
            stage[slot(i, shapes[i])] = g_ref[...]
        stage[slot(n, loss_ref.shape)] = loss_ref[...]
        gathered[me] = stage[...]
        copies = []
        for k in range(1, N_DEV):
            peer = (x ^ (k >> 2), y ^ ((k >> 1) & 1), c ^ (k & 1))
            cp = pltpu.make_async_remote_copy(
                src_ref=stage, dst_ref=gathered.at[me], send_sem=send_sems.at[k - 1], recv_sem=recv_sems.at[k - 1],
                device_id=peer, device_id_type=MESH)
            cp.start()
            copies.append(cp)
        for cp in copies:
            cp.wait()
        acc = gathered[0]
        for k in range(1, N_DEV):
            acc = acc + gathered[k]
        stage[...] = acc
        loss_out[...] = stage[slot(n, loss_ref.shape)]
        for i in range(n):
            g = stage[slot(i, shapes[i])]
            dg_refs[i][...] = g
            d_refs[i][...], nm_refs[i][...], nv_refs[i][...] = _adamw_math(w_refs[i][...], g, m_refs[i][...], v_refs[i][...])

    vm = pl.BlockSpec(memory_space=pltpu.VMEM)
    per_param = [jax.ShapeDtypeStruct(s, F32) for s in shapes]
    out = pl.pallas_call(
        body, name="small_step", in_specs=[vm] * (1 + 4 * n) + [ANY], out_specs=[vm] * (1 + 4 * n),
        out_shape=[jax.ShapeDtypeStruct(loss.shape, F32)] + per_param * 4,
        scratch_shapes=[pltpu.VMEM((slots * SMALL_SLOT_ROWS, D_MODEL), F32),
                        pltpu.VMEM((N_DEV, slots * SMALL_SLOT_ROWS, D_MODEL), F32),
                        pltpu.SemaphoreType.DMA((N_DEV - 1,)), pltpu.SemaphoreType.DMA((N_DEV - 1,))],
    )(loss, *grads, *w, *m, *v, behind)
    return out[0], out[1:1 + n], out[1 + n:1 + 2 * n], out[1 + 2 * n:1 + 3 * n], out[1 + 3 * n:]


def _swiglu_block_fwd(h, norm_g, w_gu, w_down, tag, behind=()):
    n = rmsnorm_fwd(h, norm_g, f"{tag}_norm", behind=behind)
    gu = matmul(n, w_gu, name=f"{tag}_gate_up")
    s = swiglu_fwd(gu, f"{tag}_swiglu")
    h_out = matmul(s, w_down, res=h, scale=0.5, name=f"{tag}_down")
    return h_out, (n, gu, s)


def _swiglu_block_bwd(h, norm_g, w_gu, w_down, saved, dh_out, tag, behind=()):
    n, gu, s = saved
    df = dh_out.astype(MXU_DTYPE)
    d_down = matmul(s, df, ta=True, scale=0.5, out_dtype=GRAD_COMM_DTYPE, name=f"{tag}_d_w_down")
    ds = matmul(df, w_down, tb=True, scale=0.5, behind=behind, name=f"{tag}_d_s")
    dgu = swiglu_bwd(gu, ds, f"{tag}_swiglu_bwd")
    d_gu = matmul(n, dgu, ta=True, out_dtype=GRAD_COMM_DTYPE, name=f"{tag}_d_w_gate_up")
    dn = matmul(dgu, w_gu, tb=True, name=f"{tag}_d_n")
    dh, dg = rmsnorm_bwd(h, norm_g, dn, dh_out, f"{tag}_norm_bwd")
    return dh, dg, d_gu, d_down


def local_step(x, target, small, exchange):
    big = {}
    token, big_ffn1 = exchange.weights("ffn1", x)
    big.update(big_ffn1)
    h1, saved1 = _swiglu_block_fwd(x, small["ffn1_norm"], big["ffn1_w_gate_up"], big["ffn1_w_down"], "ffn1", token)
    token, big_mix = exchange.weights("mix", h1)
    big.update(big_mix)
    u = rmsnorm_fwd(h1, small["mix_norm"], "mix_norm", behind=token)
    z = matmul(u, big["w_in"], name="w_in")
    p = small["hg_lower_bounds"]
    lb = 1.0 / (1.0 + jnp.exp(p[1:2] - p[0:1]))
    y_hg, o_raw, states = hgrn_fwd(z, lb, small["hg_out_norm"], "hgrn_fwd")
    o_att, l_att = zip(*[att_fwd(z, g, f"att_fwd_{g}") for g in range(N_GROUPS)])
    y_att = att_combine_fwd(o_att, l_att, "att_combine")
    bh = matmul(y_hg, big["w_branch_hg"], name="branch_hg")
    ba = matmul(y_att, big["w_branch_att"], name="branch_att")
    merged = merge_fwd(z, bh, ba, "merge")
    h2 = matmul(merged, big["w_out"], res=h1, name="w_out")
    token, big_ffn2 = exchange.weights("ffn2", h2)
    big.update(big_ffn2)
    h3, saved2 = _swiglu_block_fwd(h2, small["ffn2_norm"], big["ffn2_w_gate_up"], big["ffn2_w_down"], "ffn2", token)
    dh3, d_final, loss = final_norm_loss(h3, small["final_norm"], target, "final_norm_loss")

    gs, gb = {"final_norm": d_final}, {}
    dh2, gs["ffn2_norm"], gb["ffn2_w_gate_up"], gb["ffn2_w_down"] = _swiglu_block_bwd(
        h2, small["ffn2_norm"], big["ffn2_w_gate_up"], big["ffn2_w_down"], saved2, dh3, "ffn2")
    token = exchange.gradients("ffn2", gb, dh2)
    dh2_m = dh2.astype(MXU_DTYPE)
    gb["w_out"] = matmul(merged, dh2_m, ta=True, out_dtype=GRAD_COMM_DTYPE, name="d_w_out")
    dmerged = matmul(dh2_m, big["w_out"], tb=True, behind=token, name="d_merged")
    dbh, dba, dgh, dga = merge_bwd(z, bh, ba, dmerged, "merge_bwd")
    gb["w_branch_hg"] = matmul(y_hg, dbh, ta=True, out_dtype=GRAD_COMM_DTYPE, name="d_w_branch_hg")
    gb["w_branch_att"] = matmul(y_att, dba, ta=True, out_dtype=GRAD_COMM_DTYPE, name="d_w_branch_att")
    dy_hg = matmul(dbh, big["w_branch_hg"], tb=True, name="d_y_hg")
    dy_att = matmul(dba, big["w_branch_att"], tb=True, name="d_y_att")
    dq, dfp, di, dog, d_lb, gs["hg_out_norm"] = hgrn_bwd(z, lb, small["hg_out_norm"], o_raw, states, dy_hg, "hgrn_bwd")
    do_att, corr = att_combine_bwd(o_att, l_att, dy_att, "att_combine_bwd")
    d_att = [part for g in range(N_GROUPS) for part in att_bwd(z, l_att[g], do_att[g], corr[g], g, f"att_bwd_{g}")]
    dz = jnp.concatenate([dq, dfp, di, dog, *d_att, dgh, dga], axis=1)
    gb["w_in"] = matmul(u, dz, ta=True, out_dtype=GRAD_COMM_DTYPE, name="d_w_in")
    du = matmul(dz, big["w_in"], tb=True, name="d_u")
    dh1, gs["mix_norm"] = rmsnorm_bwd(h1, small["mix_norm"], du, dh2, "mix_norm_bwd")
    token = exchange.gradients("mix", gb, dh1)
    dp0 = d_lb * lb * (1.0 - lb)
    gs["hg_lower_bounds"] = jnp.concatenate([dp0, -dp0], axis=0)
    dx, gs["ffn1_norm"], gb["ffn1_w_gate_up"], gb["ffn1_w_down"] = _swiglu_block_bwd(
        x, small["ffn1_norm"], big["ffn1_w_gate_up"], big["ffn1_w_down"], saved1, dh1, "ffn1", token)
    exchange.gradients("ffn1", gb, dx)
    return loss, dx, gs


WEIGHTS = ("ffn1_norm", "ffn1_w_gate_up", "ffn1_w_down", "mix_norm", "w_in", "hg_lower_bounds", "hg_out_norm",
           "w_branch_hg", "w_branch_att", "w_out", "ffn2_norm", "ffn2_w_gate_up", "ffn2_w_down", "final_norm")


class WeightExchange:
    ORDER = ("ffn1", "mix", "ffn2")

    def __init__(self, shards, core, chip):
        self.core, self.chip = core, chip
        self.scattering = None
        self.reduced = {}
        first = self.ORDER[0]
        self.placed = {BIG[i][0]: place_own_block(shards[BIG[i][0]], chip, *BIG[i][1:], f"place_{BIG[i][0]}")
                       for i in GROUPS[first]}
        self._start_gather(first, self.placed[self._names(first)[0]])
        chip_behind = chip + self.token[0, :1].astype(jnp.int32)
        for group in self.ORDER[1:]:
            for i in GROUPS[group]:
                n, r, cc, ax = BIG[i]
                self.placed[n] = place_own_block(shards[n], chip_behind, r, cc, ax, f"place_{n}")
        self.placed_behind = [self.placed[n] for group in self.ORDER[1:] for n in self._names(group)]

    def _names(self, group):
        return [BIG[i][0] for i in GROUPS[group]]

    def _start_gather(self, group, after):
        send_sems, recv_sems, bufs, self.token = gather_start([self.placed[n] for n in self._names(group)], after, group)
        self.gathering = (group, send_sems, recv_sems, bufs)

    def weights(self, group, h):
        pending, send_sems, recv_sems, bufs = self.gathering
        assert pending == group
        after = self.placed_behind if group == self.ORDER[0] else [h]
        whole = gather_forward(gather_wait(bufs, send_sems, recv_sems, after, group), group)
        later = self.ORDER.index(group) + 1
        behind = []
        if later < len(self.ORDER):
            self._start_gather(self.ORDER[later], whole[0])
            behind = [self.token]
        return behind, dict(zip(self._names(group), whole))

    def _finish_scatter(self, after):
        group, send_sems, recv_sems, halves, got = self.scattering
        halves, got = scatter_wait(halves, got, send_sems, recv_sems, after, group)
        ws = [BIG[i] for i in GROUPS[group]]
        mine = [add_pieces(h, g, self.chip, r, cc, ax, f"add_pieces_{n}") for (n, r, cc, ax), h, g in zip(ws, halves, got)]
        theirs = exchange_reduced(mine, group)
        self.reduced.update({n: (a, b) for (n, *_), a, b in zip(ws, mine, theirs)})
        self.scattering = None
        return theirs[0]

    def gradients(self, group, grads, dh):
        behind = [self._finish_scatter([dh])] if self.scattering is not None else []
        ws = [BIG[i] for i in GROUPS[group]]
        theirs = exchange_halves([grads[n] for n, *_ in ws], group)
        halves = [add_halves(grads[n], t, self.core, r, cc, ax, f"add_halves_{n}") for (n, r, cc, ax), t in zip(ws, theirs)]
        send_sems, recv_sems, halves, got, self.token = scatter_start(halves, group)
        self.scattering = (group, send_sems, recv_sems, halves, got)
        return behind + [self.token]

    def finish(self, after):
        self._finish_scatter(after)
        return self.reduced


def kernel(x, ffn1_norm, ffn1_w_gate_up, ffn1_w_down, mix_norm, w_in, hg_lower_bounds, hg_out_norm, w_branch_hg, w_branch_att, w_out, ffn2_norm, ffn2_w_gate_up, ffn2_w_down, final_norm, loss_target, m_ffn1_norm, m_ffn1_w_gate_up, m_ffn1_w_down, m_mix_norm, m_w_in, m_hg_lower_bounds, m_hg_out_norm, m_w_branch_hg, m_w_branch_att, m_w_out, m_ffn2_norm, m_ffn2_w_gate_up, m_ffn2_w_down, m_final_norm, v_ffn1_norm, v_ffn1_w_gate_up, v_ffn1_w_down, v_mix_norm, v_w_in, v_hg_lower_bounds, v_hg_out_norm, v_w_branch_hg, v_w_branch_att, v_w_out, v_ffn2_norm, v_ffn2_w_gate_up, v_ffn2_w_down, v_final_norm):
    w = dict(ffn1_norm=ffn1_norm, ffn1_w_gate_up=ffn1_w_gate_up, ffn1_w_down=ffn1_w_down, mix_norm=mix_norm, w_in=w_in,
             hg_lower_bounds=hg_lower_bounds, hg_out_norm=hg_out_norm, w_branch_hg=w_branch_hg, w_branch_att=w_branch_att,
             w_out=w_out, ffn2_norm=ffn2_norm, ffn2_w_gate_up=ffn2_w_gate_up, ffn2_w_down=ffn2_w_down, final_norm=final_norm)
    m = dict(ffn1_norm=m_ffn1_norm, ffn1_w_gate_up=m_ffn1_w_gate_up, ffn1_w_down=m_ffn1_w_down, mix_norm=m_mix_norm,
             w_in=m_w_in, hg_lower_bounds=m_hg_lower_bounds, hg_out_norm=m_hg_out_norm, w_branch_hg=m_w_branch_hg,
             w_branch_att=m_w_branch_att, w_out=m_w_out, ffn2_norm=m_ffn2_norm, ffn2_w_gate_up=m_ffn2_w_gate_up,
             ffn2_w_down=m_ffn2_w_down, final_norm=m_final_norm)
    v = dict(ffn1_norm=v_ffn1_norm, ffn1_w_gate_up=v_ffn1_w_gate_up, ffn1_w_down=v_ffn1_w_down, mix_norm=v_mix_norm,
             w_in=v_w_in, hg_lower_bounds=v_hg_lower_bounds, hg_out_norm=v_hg_out_norm, w_branch_hg=v_w_branch_hg,
             w_branch_att=v_w_branch_att, w_out=v_w_out, ffn2_norm=v_ffn2_norm, ffn2_w_gate_up=v_ffn2_w_gate_up,
             ffn2_w_down=v_ffn2_w_down, final_norm=v_final_norm)

    core = lax.axis_index("c").astype(jnp.int32).reshape(1)
    chip = (2 * lax.axis_index("x") + lax.axis_index("y")).astype(jnp.int32).reshape(1)
    exchange = WeightExchange({n: w[n][0] for n, *_ in BIG}, core, chip)
    small = {n: w[n] for n in SMALL}
    small["final_norm"] = final_norm.reshape(1, D_MODEL)

    loss, dx, gs = local_step(x[0], loss_target[0], small, exchange)

    grads, delta, new_m, new_v = {}, {}, {}, {}

    def update(group, core):
        for i in GROUPS[group]:
            n, r, cc, ax = BIG[i]
            a, b = exchange.reduced[n]
            g, d, nm, nv = adamw_halves(w[n][0], a, b, m[n][0], v[n][0], core, r, cc, ax, f"adamw_{n}")
            grads[n], delta[n], new_m[n], new_v[n] = g[None], d[None], nm[None], nv[None]

    core_behind = core + exchange.token[0, :1].astype(jnp.int32)
    update("ffn2", core_behind)
    update("mix", core_behind)
    exchange.finish(after=[delta[BIG[i][0]] for group in ("ffn2", "mix") for i in GROUPS[group]])
    update("ffn1", core)
    two_d = lambda a: a.reshape(1, D_MODEL) if a.ndim == 1 else a
    loss_sum, *small_out = small_step(loss, [gs[n] for n in SMALL], *[[two_d(p[n]) for n in SMALL] for p in (w, m, v)],
                                      behind=delta["ffn1_w_down"])
    for result, parts in zip((grads, delta, new_m, new_v), small_out):
        result.update({n: a.reshape(w[n].shape) for n, a in zip(SMALL, parts)})

    return (loss_sum[0, 0], dx[None], *[grads[n] for n in WEIGHTS], *[delta[n] for n in WEIGHTS],
            *[new_m[n] for n in WEIGHTS], *[new_v[n] for n in WEIGHTS])
```

```python
import numpy as np
import jax
import jax.numpy as jnp
from jax import lax
from jax.experimental import pallas as pl
from jax.experimental.pallas import tpu as pltpu

SEQ = 2048
D_MODEL = 1024
D_FF = 2816
HG_HEADS = 4
HG_DIM = 128
HG_WIDTH = 512
HG_CHUNK = 64
ATT_GROUPS = ((128, 1), (512, 4), (2048, 16))
ATT_HEADS = 8
ATT_WIDTH = 512
ATT_BLOCK = 128
ALIBI_MAX = 8.0
IN_COLS = 8704
EPS = 1e-6
NEG_INF = -1e30
ADAM_LR = 0.001
ADAM_B1 = 0.9
ADAM_B2 = 0.999
ADAM_EPS = 1e-08
ADAM_WD = 0.01
ADAM_STEP = 10

N_CHIPS = 4
MXU_DTYPE = jnp.bfloat16
WEIGHT_COMM_DTYPE = jnp.bfloat16
GRAD_COMM_DTYPE = jnp.bfloat16
MESH = pl.DeviceIdType.MESH
F32 = jnp.float32
HIGHEST = lax.Precision.HIGHEST


def _sigmoid(x):
    return 1.0 / (1.0 + jnp.exp(-x))


def _dot(a, b, ta=False, tb=False):
    dn = (((0 if ta else 1,), (1 if tb else 0,)), ((), ()))
    return lax.dot_general(a.astype(MXU_DTYPE), b.astype(MXU_DTYPE), dn, preferred_element_type=F32)


def _dot_f32(a, b, ones_on_right=False):
    x = a if ones_on_right else b
    hi = x.astype(jnp.bfloat16)
    rest = x - hi.astype(F32)
    mid = rest.astype(jnp.bfloat16)
    lo = (rest - mid.astype(F32)).astype(jnp.bfloat16)
    if ones_on_right:
        dot = lambda q: jnp.dot(q, b.astype(jnp.bfloat16), preferred_element_type=F32)
    else:
        dot = lambda q: jnp.dot(a.astype(jnp.bfloat16), q, preferred_element_type=F32)
    return dot(hi) + (dot(mid) + dot(lo))


def _split_bf16(x):
    hi = x.astype(jnp.bfloat16)
    return hi, (x - hi.astype(F32)).astype(jnp.bfloat16)


def _hdot(a, b, ta=False, tb=False):
    dn =(((0 if ta else 1,), (1 if tb else 0,)), ((), ()))
    (a_hi, a_lo), (b_hi, b_lo) = _split_bf16(a), _split_bf16(b)
    dot = lambda p, q: lax.dot_general(p, q, dn, preferred_element_type=F32)
    return dot(a_hi, b_hi) + (dot(a_lo, b_hi) + dot(a_hi, b_lo))


MATMUL_VMEM_BYTES = 48 * 1024 * 1024
MATMUL_TILE_BYTES = 36 * 1024 * 1024
MXU_ALIGN = 128


def _divisors(n, most):
    return [t for t in range(min(n, most), 0, -MXU_ALIGN) if n % t == 0 and t % MXU_ALIGN == 0]


def _matmul_tiles(M, N, K, in_bytes, out_bytes, has_res):
    best = None
    for tk in _divisors(K, K):
        nk = K // tk
        for tm in _divisors(M, 2048):
            for tn in _divisors(N, 512):
                tiles = 2 * in_bytes * (tm * tk + tk * tn) + 2 * out_bytes * tm * tn
                tiles += 4 * tm * tn * ((nk > 1) + 2 * has_res)
                if tiles > MATMUL_TILE_BYTES:
                    continue
                traffic = in_bytes * (M * K * (1 if nk == 1 else N // tn) + K * N * (M // tm))
                key = (traffic, -tm * tn * tk)
                if best is None or key < best[0]:
                    best = (key, (tm, tn, tk))
    return best[1]


def matmul(a, b, *, ta=False, tb=False, out_dtype=F32, res=None, scale=1.0, behind=(), name):
    if ta:
        K, M = a.shape
    else:
        M, K = a.shape
    if tb:
        N, K2 = b.shape
    else:
        K2, N = b.shape
    assert K == K2 and a.dtype == b.dtype
    tm, tn, tk = _matmul_tiles(M, N, K, a.dtype.itemsize, jnp.dtype(out_dtype).itemsize, res is not None)
    nk = K // tk

    def finish(r, r_ref, o_ref):
        if scale != 1.0:
            r = r * scale
        if res is not None:
            r = r_ref[...] + r
        o_ref[...] = r.astype(out_dtype)

    def body(*refs):
        a_ref, b_ref = refs[:2]
        r_ref = refs[2] if res is not None else None
        o_ref = refs[2 + (res is not None) + len(behind)]
        if nk == 1:
            finish(_dot(a_ref[...], b_ref[...], ta, tb), r_ref, o_ref)
            return
        acc = refs[-1]
        k = pl.program_id(2)

        @pl.when(k == 0)
        def _():
            acc[...] = jnp.zeros_like(acc)

        acc[...] += _dot(a_ref[...], b_ref[...], ta, tb)

        @pl.when(k == nk - 1)
        def _():
            finish(acc[...], r_ref, o_ref)

    a_spec = pl.BlockSpec((tk, tm), lambda i, j, k: (k, i)) if ta else pl.BlockSpec((tm, tk), lambda i, j, k: (i, k))
    b_spec = pl.BlockSpec((tn, tk), lambda i, j, k: (j, k)) if tb else pl.BlockSpec((tk, tn), lambda i, j, k: (k, j))
    in_specs = [a_spec, b_spec]
    args = [a, b]
    if res is not None:
        in_specs.append(pl.BlockSpec((tm, tn), lambda i, j, k: (i, j)))
        args.append(res)
    for earlier in behind:
        in_specs.append(pl.BlockSpec(memory_space=pl.ANY))
        args.append(earlier)
    return pl.pallas_call(
        body, name=name, grid=(M // tm, N // tn, nk), in_specs=in_specs,
        out_specs=pl.BlockSpec((tm, tn), lambda i, j, k: (i, j)),
        out_shape=jax.ShapeDtypeStruct((M, N), out_dtype),
        scratch_shapes=[pltpu.VMEM((tm, tn), F32)] if nk > 1 else [],
        compiler_params=pltpu.CompilerParams(dimension_semantics=("parallel", "parallel", "arbitrary"),
                                             vmem_limit_bytes=MATMUL_VMEM_BYTES),
    )(*args)


ROW_TILE = 256


def rmsnorm_fwd(x, g, name, behind=()):
    def body(x_ref, g_ref, *refs):
        n_ref = refs[-1]
        xv = x_ref[...]
        r = lax.rsqrt(jnp.mean(xv * xv, axis=-1, keepdims=True) + EPS)
        n_ref[...] = ((xv * r) * g_ref[...]).astype(n_ref.dtype)

    order = list(behind)
    return pl.pallas_call(
        body, name=name, grid=(SEQ // ROW_TILE,),
        in_specs=[pl.BlockSpec((ROW_TILE, D_MODEL), lambda i: (i, 0)), pl.BlockSpec((1, D_MODEL), lambda i: (0, 0))]
        + [pl.BlockSpec(memory_space=pl.ANY)] * len(order),
        out_specs=pl.BlockSpec((ROW_TILE, D_MODEL), lambda i: (i, 0)),
        out_shape=jax.ShapeDtypeStruct((SEQ, D_MODEL), MXU_DTYPE),
    )(x, g, *order)


def rmsnorm_bwd(x, g, dn, dres, name):
    def body(x_ref, g_ref, dn_ref, dr_ref, dx_ref, dxm_ref, dg_ref):
        xv = x_ref[...]
        r = lax.rsqrt(jnp.mean(xv * xv, axis=-1, keepdims=True) + EPS)
        xh = xv * r
        dnv = dn_ref[...]

        @pl.when(pl.program_id(0) == 0)
        def _():
            dg_ref[...] = jnp.zeros_like(dg_ref)

        dg_ref[...] += jnp.sum(dnv * xh, axis=0, keepdims=True)
        dxh = dnv * g_ref[...]
        dx = dr_ref[...] + r * (dxh - xh * jnp.mean(dxh * xh, axis=-1, keepdims=True))
        dx_ref[...] = dx
        dxm_ref[...] = dx.astype(dxm_ref.dtype)

    row = pl.BlockSpec((ROW_TILE, D_MODEL), lambda i: (i, 0))
    vec = pl.BlockSpec((1, D_MODEL), lambda i: (0, 0))
    return pl.pallas_call(
        body, name=name, grid=(SEQ // ROW_TILE,), in_specs=[row, vec, row, row], out_specs=[row, row, vec],
        out_shape=[jax.ShapeDtypeStruct((SEQ, D_MODEL), F32), jax.ShapeDtypeStruct((SEQ, D_MODEL), MXU_DTYPE),
                   jax.ShapeDtypeStruct((1, D_MODEL), F32)],
        compiler_params=pltpu.CompilerParams(dimension_semantics=("arbitrary",)),
    )(x, g, dn, dres)


def final_norm_loss(h, g, target, name):
    def body(h_ref, g_ref, t_ref, dh_ref, dhm_ref, dg_ref, loss_ref):
        xv = h_ref[...]
        r = lax.rsqrt(jnp.mean(xv * xv, axis=-1, keepdims=True) + EPS)
        xh = xv * r
        gv = g_ref[...]
        e = xh * gv - t_ref[...]

        @pl.when(pl.program_id(0) == 0)
        def _():
            dg_ref[...] = jnp.zeros_like(dg_ref)
            loss_ref[...] = jnp.zeros_like(loss_ref)

        part = 0.5 * jnp.sum(jnp.sum(e * e, axis=-1, keepdims=True) * (1.0 / D_MODEL), axis=0, keepdims=True)
        loss_ref[...] += jnp.broadcast_to(part, loss_ref.shape)
        dout = e * (1.0 / D_MODEL)
        dg_ref[...] += jnp.sum(dout * xh, axis=0, keepdims=True)
        dxh = dout * gv
        dh = r * (dxh - xh * jnp.mean(dxh * xh, axis=-1, keepdims=True))
        dh_ref[...] = dh
        dhm_ref[...] = dh.astype(dhm_ref.dtype)

    row = pl.BlockSpec((ROW_TILE, D_MODEL), lambda i: (i, 0))
    vec = pl.BlockSpec((1, D_MODEL), lambda i: (0, 0))
    return pl.pallas_call(
        body, name=name, grid=(SEQ // ROW_TILE,), in_specs=[row, vec, row],
        out_specs=[row, row, vec, pl.BlockSpec((8, 128), lambda i: (0, 0))],
        out_shape=[jax.ShapeDtypeStruct((SEQ, D_MODEL), F32), jax.ShapeDtypeStruct((SEQ, D_MODEL), MXU_DTYPE),
                   jax.ShapeDtypeStruct((1, D_MODEL), F32), jax.ShapeDtypeStruct((8, 128), F32)],
        compiler_params=pltpu.CompilerParams(dimension_semantics=("arbitrary",)),
    )(h, g, target)


FF_TILE = D_FF // 2


def swiglu_fwd(gu, name):
    def body(a_ref, b_ref, s_ref):
        a = a_ref[...]
        s_ref[...] = (a * _sigmoid(a) * b_ref[...]).astype(s_ref.dtype)

    return pl.pallas_call(
        body, name=name, grid=(SEQ // ROW_TILE, 2),
        in_specs=[pl.BlockSpec((ROW_TILE, FF_TILE), lambda i, j: (i, j)),
                  pl.BlockSpec((ROW_TILE, FF_TILE), lambda i, j: (i, j + 2))],
        out_specs=pl.BlockSpec((ROW_TILE, FF_TILE), lambda i, j: (i, j)),
        out_shape=jax.ShapeDtypeStruct((SEQ, D_FF), MXU_DTYPE),
    )(gu, gu)


def swiglu_bwd(gu, ds, name):
    rows = ROW_TILE // 2

    def body(a_ref, b_ref, ds_ref, o_ref):
        a = a_ref[...]
        sg = _sigmoid(a)
        dsv = ds_ref[...]
        o_ref[:, :D_FF] = (dsv * b_ref[...] * (sg * (1.0 + a * (1.0 - sg)))).astype(o_ref.dtype)
        o_ref[:, D_FF:] = (dsv * a * sg).astype(o_ref.dtype)

    return pl.pallas_call(
        body, name=name, grid=(SEQ // rows,),
        in_specs=[pl.BlockSpec((rows, D_FF), lambda i: (i, 0)), pl.BlockSpec((rows, D_FF), lambda i: (i, 1)),
                  pl.BlockSpec((rows, D_FF), lambda i: (i, 0))],
        out_specs=pl.BlockSpec((rows, 2 * D_FF), lambda i: (i, 0)),
        out_shape=jax.ShapeDtypeStruct((SEQ, 2 * D_FF), MXU_DTYPE), compiler_params=SUM_PARAMS,
    )(gu, gu, ds)


GATE_HG_BLK = 6656 // 512
GATE_ATT_BLK = 7680 // 512


def merge_fwd(z, bh, ba, name):
    def body(gh_ref, ga_ref, bh_ref, ba_ref, o_ref):
        o_ref[...] = (_sigmoid(gh_ref[...]) * bh_ref[...] + _sigmoid(ga_ref[...]) * ba_ref[...]).astype(o_ref.dtype)

    blk = pl.BlockSpec((ROW_TILE, 512), lambda i, j: (i, j))
    return pl.pallas_call(
        body, name=name, grid=(SEQ // ROW_TILE, 2),
        in_specs=[pl.BlockSpec((ROW_TILE, 512), lambda i, j: (i, GATE_HG_BLK + j)),
                  pl.BlockSpec((ROW_TILE, 512), lambda i, j: (i, GATE_ATT_BLK + j)), blk, blk],
        out_specs=blk, out_shape=jax.ShapeDtypeStruct((SEQ, D_MODEL), MXU_DTYPE),
    )(z, z, bh, ba)


def merge_bwd(z, bh, ba, dm, name):
    def body(gh_ref, ga_ref, bh_ref, ba_ref, dm_ref, dbh_ref, dba_ref, dgh_ref, dga_ref):
        dmv = dm_ref[...]
        sh = _sigmoid(gh_ref[...])
        sa = _sigmoid(ga_ref[...])
        dbh_ref[...] = (dmv * sh).astype(dbh_ref.dtype)
        dba_ref[...] = (dmv * sa).astype(dba_ref.dtype)
        dgh_ref[...] = (dmv * bh_ref[...] * (sh * (1.0 - sh))).astype(dgh_ref.dtype)
        dga_ref[...] = (dmv * ba_ref[...] * (sa * (1.0 - sa))).astype(dga_ref.dtype)

    blk = pl.BlockSpec((ROW_TILE, 512), lambda i, j: (i, j))
    out = jax.ShapeDtypeStruct((SEQ, D_MODEL), MXU_DTYPE)
    return pl.pallas_call(
        body, name=name, grid=(SEQ // ROW_TILE, 2),
        in_specs=[pl.BlockSpec((ROW_TILE, 512), lambda i, j: (i, GATE_HG_BLK + j)),
                  pl.BlockSpec((ROW_TILE, 512), lambda i, j: (i, GATE_ATT_BLK + j)), blk, blk, blk],
        out_specs=[blk, blk, blk, blk], out_shape=[out, out, out, out],
    )(z, z, bh, ba, dm)


N_CHUNKS = SEQ // HG_CHUNK
HG_STEP_CHUNKS = 4


def _hgrn_gates(q, fp, lb):
    C = HG_CHUNK
    sg = _sigmoid(fp)
    f = lb + (1.0 - lb) * sg
    lf = jnp.log(f)
    row = lax.broadcasted_iota(jnp.int32, (C, C), 0)
    col = lax.broadcasted_iota(jnp.int32, (C, C), 1)
    causal = row >= col
    G = _dot_f32(causal.astype(F32), lf)
    eG = jnp.exp(G)
    enG = jnp.exp(-G)
    qg = q * eG
    kg = (1.0 - f) * enG
    A = jnp.where(causal, _hdot(qg, kg, tb=True), 0.0)
    egl = jnp.exp(jnp.sum(lf, axis=0, keepdims=True))
    return sg, f, causal, eG, enG, qg, kg, A, egl


def hgrn_fwd(z, lb, gain, name):
    C, K = HG_CHUNK, HG_DIM

    def body(q_ref, f_ref, v_ref, og_ref, p_ref, g_ref, y_ref, o_ref, st_ref, state):
        @pl.when(pl.program_id(0) == 0)
        def _():
            state[...] = jnp.zeros_like(state)

        for cc in range(HG_STEP_CHUNKS):
            rows = pl.ds(cc * C, C)
            for h in range(HG_HEADS):
                hd = pl.ds(h * K, K)
                v = v_ref[rows, hd]
                _, _, _, _, _, qg, kg, A, egl = _hgrn_gates(q_ref[rows, hd], f_ref[rows, hd], p_ref[:, hd])
                st = state[h]
                st_ref[h, cc] = st
                o = _hdot(A, v) + _hdot(qg, st, tb=True)
                state[h] = st * egl + _hdot(v, kg * egl, ta=True)
                o_ref[rows, hd] = o
                rs = lax.rsqrt(jnp.mean(o * o, axis=-1, keepdims=True) + EPS)
                og = og_ref[rows, hd]
                y_ref[rows, hd] = (((o * rs) * g_ref[:, hd]) * (og * _sigmoid(og))).astype(y_ref.dtype)

    R = HG_STEP_CHUNKS * C

    def zcol(section):
        return pl.BlockSpec((R, HG_WIDTH), lambda c: (c, section))

    vec = pl.BlockSpec((1, HG_WIDTH), lambda c: (0, 0))
    blk = pl.BlockSpec((R, HG_WIDTH), lambda c: (c, 0))
    return pl.pallas_call(
        body, name=name, grid=(N_CHUNKS // HG_STEP_CHUNKS,),
        in_specs=[zcol(0), zcol(1), zcol(2), zcol(3), vec, vec],
        out_specs=[blk, blk, pl.BlockSpec((HG_HEADS, HG_STEP_CHUNKS, K, K), lambda c: (0, c, 0, 0))],
        out_shape=[jax.ShapeDtypeStruct((SEQ, HG_WIDTH), MXU_DTYPE), jax.ShapeDtypeStruct((SEQ, HG_WIDTH), F32),
                   jax.ShapeDtypeStruct((HG_HEADS, N_CHUNKS, K, K), F32)],
        scratch_shapes=[pltpu.VMEM((HG_HEADS, K, K), F32)],
        compiler_params=pltpu.CompilerParams(dimension_semantics=("arbitrary",)),
    )(z, z, z, z, lb, gain)


def hgrn_bwd(z, lb, gain, o_raw, states, dy, name):
    C, K = HG_CHUNK, HG_DIM

    def body(q_ref, f_ref, v_ref, og_ref, p_ref, g_ref, o_ref, st_ref, dy_ref,
             dq_ref, dfp_ref, dv_ref, dog_ref, dlb_ref, dgain_ref, dstate):
        @pl.when(pl.program_id(0) == 0)
        def _():
            dstate[...] = jnp.zeros_like(dstate)
            dlb_ref[...] = jnp.zeros_like(dlb_ref)
            dgain_ref[...] = jnp.zeros_like(dgain_ref)

        last = lax.broadcasted_iota(jnp.int32, (C, K), 0) == C - 1
        row = lax.broadcasted_iota(jnp.int32, (C, C), 0)
        col = lax.broadcasted_iota(jnp.int32, (C, C), 1)
        anti_causal = (col >= row).astype(F32)
        for cc in reversed(range(HG_STEP_CHUNKS)):
            rows = pl.ds(cc * C, C)
            for h in range(HG_HEADS):
                hd = pl.ds(h * K, K)
                v = v_ref[rows, hd]
                lb = p_ref[:, hd]
                sg, f, causal, eG, enG, qg, kg, A, egl = _hgrn_gates(q_ref[rows, hd], f_ref[rows, hd], lb)
                kd = kg * egl
                st = st_ref[h, cc]
                dst = dstate[h]
                o = o_ref[rows, hd]
                og = og_ref[rows, hd]
                gain_v = g_ref[:, hd]
                dyv = dy_ref[rows, hd]
                rs = lax.rsqrt(jnp.mean(o * o, axis=-1, keepdims=True) + EPS)
                on = o * rs
                sgo = _sigmoid(og)
                silu = og * sgo
                dog_ref[rows, hd] = (dyv * (on * gain_v) * (sgo * (1.0 + og * (1.0 - sgo)))).astype(dog_ref.dtype)
                dgain_ref[:, hd] += jnp.sum(dyv * silu * on, axis=0, keepdims=True)
                don = dyv * gain_v * silu
                do = rs * (don - on * jnp.mean(don * on, axis=-1, keepdims=True))
                dA = jnp.where(causal, _hdot(do, v, tb=True), 0.0)
                dv_ref[rows, hd] = (_hdot(A, do, ta=True) + _hdot(kd, dst, tb=True)).astype(dv_ref.dtype)
                dqg = _hdot(dA, kg) + _hdot(do, st)
                dkg = _hdot(dA, qg, ta=True)
                dkd = _hdot(v, dst)
                dstate[h] = dst * egl + _hdot(do, qg, ta=True)
                dgl = jnp.sum(st * dst, axis=0, keepdims=True) * egl
                dq_ref[rows, hd] = (dqg * eG).astype(dq_ref.dtype)
                dk = dkg * enG + dkd * (enG * egl)
                dG = dqg * qg - dkg * kg - dkd * kd
                extra = jnp.sum(dkd * kd, axis=0, keepdims=True) + dgl
                dG = dG + jnp.where(last, extra, 0.0)
                dlf = _dot_f32(anti_causal, dG)
                df = dlf / f - dk
                dfp_ref[rows, hd] = (df * (1.0 - lb) * (sg * (1.0 - sg))).astype(dfp_ref.dtype)
                dlb_ref[:, hd] += jnp.sum(df * (1.0 - sg), axis=0, keepdims=True)

    R = HG_STEP_CHUNKS * C
    n_steps = N_CHUNKS // HG_STEP_CHUNKS

    def rc(c):
        return n_steps - 1 - c

    def zcol(section):
        return pl.BlockSpec((R, HG_WIDTH), lambda c: (rc(c), section))

    vec = pl.BlockSpec((1, HG_WIDTH), lambda c: (0, 0))
    blk = pl.BlockSpec((R, HG_WIDTH), lambda c: (rc(c), 0))
    out = jax.ShapeDtypeStruct((SEQ, HG_WIDTH), MXU_DTYPE)
    small = jax.ShapeDtypeStruct((1, HG_WIDTH), F32)
    return pl.pallas_call(
        body, name=name, grid=(n_steps,),
        in_specs=[zcol(0), zcol(1), zcol(2), zcol(3), vec, vec, blk,
                  pl.BlockSpec((HG_HEADS, HG_STEP_CHUNKS, K, K), lambda c: (0, rc(c), 0, 0)), blk],
        out_specs=[blk, blk, blk, blk, vec, vec],
        out_shape=[out, out, out, out, small, small],
        scratch_shapes=[pltpu.VMEM((HG_HEADS, K, K), F32)],
        compiler_params=pltpu.CompilerParams(dimension_semantics=("arbitrary",)),
    )(z, z, z, z, lb, gain, o_raw, states, dy)


N_GROUPS = len(ATT_GROUPS)
HEAD_PAIRS = ATT_WIDTH // 128
ATT_COL0 = 4 * HG_WIDTH
UNROLLED_UNITS = 4
ATT_SLAB_BLOCKS = 4


def _alibi_coef():
    n = N_GROUPS * ATT_HEADS
    slopes = np.exp2(-ALIBI_MAX * np.arange(1, n + 1, dtype=np.float32) / n).astype(np.float32)
    dil = np.repeat(np.array([d for _, d in ATT_GROUPS], np.float32), ATT_HEADS)
    return jnp.asarray(slopes * dil, F32)


def _for_each_unit(n, fn):
    if n <= UNROLLED_UNITS:
        for u in range(n):
            fn(u)
    else:
        def group(i, carry):
            for j in range(UNROLLED_UNITS):
                fn(i * UNROLLED_UNITS + j)
            return carry
        lax.fori_loop(0, n // UNROLLED_UNITS, group, 0)


def _att_specs(g):
    B = ATT_BLOCK
    d = ATT_GROUPS[g][1]
    blocks = ATT_SLAB_BLOCKS if d == 1 else 1
    R = B * d * blocks
    n_slabs = SEQ // R
    multi = SEQ // d > B
    col0 = (ATT_COL0 + g * 3 * ATT_WIDTH) // 128

    def cur(col):
        return pl.BlockSpec((R, 128), lambda hp, s: (s, col + hp))

    def prev(col):
        return pl.BlockSpec((R, 128), lambda hp, s: (jnp.maximum(s - 1, 0), col + hp))

    def nxt(col):
        return pl.BlockSpec((R, 128), lambda hp, s: (jnp.minimum(s + 1, n_slabs - 1), col + hp))

    def unit(u, s):
        if d > 1:
            rows = pl.ds(u, B, stride=d)
            return rows, False, rows, jnp.where(s == 0, B, 0), False, rows, jnp.where(s == n_slabs - 1, B, 0)
        rows = pl.ds(u * B, B)
        inner_prev, inner_next = u > 0, u < blocks - 1
        return (rows, inner_prev, pl.ds((u - 1) * B if inner_prev else (blocks - 1) * B, B),
                0 if inner_prev else jnp.where(s == 0, B, 0),
                inner_next, pl.ds((u + 1) * B if inner_next else 0, B),
                0 if inner_next else jnp.where(s == n_slabs - 1, B, 0))

    return d * blocks, R, n_slabs, multi, col0, cur, prev, nxt, unit


def _head_lanes(j):
    lane = lax.broadcasted_iota(jnp.int32, (ATT_BLOCK, 128), 1)
    return (lane >= 64 * j) & (lane < 64 * (j + 1))


def _lane_value(x, sel):
    return jnp.max(jnp.where(sel, x, -3e38), axis=-1, keepdims=True)


def _stack_heads(x, sel0):
    return jnp.concatenate([jnp.where(sel0, x, 0.0), jnp.where(sel0, 0.0, x)], axis=0)


def _stack_values(x, sel0, lanes):
    swapped = pltpu.roll(x, 64, 1)
    stacked = jnp.concatenate([jnp.where(sel0, x, swapped), jnp.where(sel0, swapped, x)], axis=0)
    return stacked if lanes == 128 else jnp.concatenate([stacked] * (lanes // 128), axis=1)


def _pair_coef(coef_ref, g, hp):
    row = lax.broadcasted_iota(jnp.int32, (2 * ATT_BLOCK, 1), 0)
    first = g * ATT_HEADS + hp * 2
    return jnp.where(row < ATT_BLOCK, coef_ref[first], coef_ref[first + 1])


def _band(with_prev, first_key):
    B = ATT_BLOCK
    keys = 2 * B if with_prev else B
    qi = jnp.bitwise_and(lax.broadcasted_iota(jnp.int32, (2 * B, keys), 0), B - 1)
    kj = lax.broadcasted_iota(jnp.int32, (2 * B, keys), 1)
    delta = qi + (B if with_prev else 0) - kj
    valid = (delta >= 0) & (delta <= B)
    if with_prev:
        valid = valid & (kj >= first_key)
    return valid, delta.astype(F32)


def _band_next(first_key):
    B = ATT_BLOCK
    qi = jnp.bitwise_and(lax.broadcasted_iota(jnp.int32, (2 * B, B), 0), B - 1)
    kj = lax.broadcasted_iota(jnp.int32, (2 * B, B), 1)
    delta = qi + B - kj
    return (delta <= B) & (kj >= first_key), delta.astype(F32)


def att_fwd(z, g, name):
    B = ATT_BLOCK
    n_units, R, n_slabs, has_prev, col0, cur, prev, _, unit = _att_specs(g)

    def body(coef_ref, *refs):
        if has_prev:
            q_ref, kc_ref, vc_ref, kp_ref, vp_ref, o_ref, l_ref = refs
        else:
            q_ref, kc_ref, vc_ref, o_ref, l_ref = refs
        hp, s = pl.program_id(0), pl.program_id(1)
        cf2 = _pair_coef(coef_ref, g, hp)
        sel0 = _head_lanes(0)

        def one(u):
            rows, inner_prev, prev_rows, first_key, _, _, _ = unit(u, s)
            valid, dist = _band(has_prev, first_key)
            q2 = _stack_heads(q_ref[rows, :], sel0)
            kk, vv = kc_ref[rows, :], vc_ref[rows, :]
            if has_prev:
                k_from, v_from = (kc_ref, vc_ref) if inner_prev else (kp_ref, vp_ref)
                kk = jnp.concatenate([k_from[prev_rows, :], kk], axis=0)
                vv = jnp.concatenate([v_from[prev_rows, :], vv], axis=0)
            sc = jnp.where(valid, _dot(q2, kk, tb=True) * 0.125 - cf2 * dist, NEG_INF)
            mx = jnp.max(sc, axis=-1, keepdims=True)
            e = jnp.exp(sc - mx)
            den = jnp.sum(e, axis=-1, keepdims=True)
            o2 = _dot(e * (1.0 / den), vv)
            lse2 = mx + jnp.log(den)
            o_ref[rows, :] = jnp.where(sel0, o2[:B], o2[B:])
            l_ref[rows, :] = jnp.where(sel0, lse2[:B], lse2[B:])

        _for_each_unit(n_units, one)

    in_specs = [pl.BlockSpec(memory_space=pltpu.SMEM), cur(col0), cur(col0 + 4), cur(col0 + 8)]
    args = [_alibi_coef(), z, z, z]
    if has_prev:
        in_specs += [prev(col0 + 4), prev(col0 + 8)]
        args += [z, z]
    out = jax.ShapeDtypeStruct((SEQ, ATT_WIDTH), F32)
    return pl.pallas_call(
        body, name=name, grid=(HEAD_PAIRS, n_slabs), in_specs=in_specs,
        out_specs=[cur(0), cur(0)], out_shape=[out, out],
        compiler_params=pltpu.CompilerParams(dimension_semantics=("parallel", "arbitrary")),
    )(*args)


def att_bwd(z, l, do, corr, g, name):
    B = ATT_BLOCK
    n_units, R, n_slabs, neighbours, col0, cur, prev, nxt, unit = _att_specs(g)

    def body(coef_ref, *refs):
        if neighbours:
            (q_ref, kc_ref, vc_ref, l_ref, do_ref, cr_ref, kp_ref, vp_ref, qn_ref, ln_ref, don_ref, crn_ref,
             dq_ref, dk_ref, dv_ref, dq_sc, dk_sc, dv_sc) = refs
        else:
            q_ref, kc_ref, vc_ref, l_ref, do_ref, cr_ref, dq_ref, dk_ref, dv_ref, dq_sc, dk_sc, dv_sc = refs
        hp, s = pl.program_id(0), pl.program_id(1)
        cf2 = _pair_coef(coef_ref, g, hp)
        sel0 = _head_lanes(0)
        own = slice(B, 2 * B) if neighbours else slice(0, B)

        def one(u):
            rows, inner_prev, prev_rows, first_key, inner_next, next_rows, first_key_n = unit(u, s)
            valid, dist = _band(neighbours, first_key)
            kc, vc = kc_ref[rows, :], vc_ref[rows, :]
            kk, vv = kc, vc
            if neighbours:
                k_from, v_from = (kc_ref, vc_ref) if inner_prev else (kp_ref, vp_ref)
                kk = jnp.concatenate([k_from[prev_rows, :], kc], axis=0)
                vv = jnp.concatenate([v_from[prev_rows, :], vc], axis=0)
            q2, do2 = _stack_heads(q_ref[rows, :], sel0), _stack_heads(do_ref[rows, :], sel0)
            keys = kk.shape[0]
            lse2, cr2 = _stack_values(l_ref[rows, :], sel0, keys), _stack_values(cr_ref[rows, :], sel0, keys)
            p = jnp.exp(jnp.where(valid, _dot(q2, kk, tb=True) * 0.125 - cf2 * dist, NEG_INF) - lse2)
            ds = p * (_dot(do2, vv, tb=True) + cr2)
            dq2 = _dot(ds, kk)
            dk = _dot(ds, q2, ta=True)[own]
            dv = _dot(p, do2, ta=True)[own]
            if neighbours:
                valid_n, dist_n = _band_next(first_key_n)
                q_from, l_from, do_from, cr_from = ((q_ref, l_ref, do_ref, cr_ref) if inner_next
                                                    else (qn_ref, ln_ref, don_ref, crn_ref))
                qn2, don2 = _stack_heads(q_from[next_rows, :], sel0), _stack_heads(do_from[next_rows, :], sel0)
                lse_n2 = _stack_values(l_from[next_rows, :], sel0, B)
                cr_n2 = _stack_values(cr_from[next_rows, :], sel0, B)
                p_n = jnp.exp(jnp.where(valid_n, _dot(qn2, kc, tb=True) * 0.125 - cf2 * dist_n, NEG_INF) - lse_n2)
                ds_n = p_n * (_dot(don2, vc, tb=True) + cr_n2)
                dk = dk + _dot(ds_n, qn2, ta=True)
                dv = dv + _dot(p_n, don2, ta=True)
            dq_sc[rows, :] = jnp.where(sel0, dq2[:B], dq2[B:]) * 0.125
            dk_sc[rows, :] = dk * 0.125
            dv_sc[rows, :] = dv

        _for_each_unit(n_units, one)
        dq_ref[...] = dq_sc[...].astype(dq_ref.dtype)
        dk_ref[...] = dk_sc[...].astype(dk_ref.dtype)
        dv_ref[...] = dv_sc[...].astype(dv_ref.dtype)

    in_specs = [pl.BlockSpec(memory_space=pltpu.SMEM), cur(col0), cur(col0 + 4), cur(col0 + 8), cur(0), cur(0), cur(0)]
    args = [_alibi_coef(), z, z, z, l, do, corr]
    if neighbours:
        in_specs += [prev(col0 + 4), prev(col0 + 8), nxt(col0), nxt(0), nxt(0), nxt(0)]
        args += [z, z, z, l, do, corr]
    out = jax.ShapeDtypeStruct((SEQ, ATT_WIDTH), MXU_DTYPE)
    return pl.pallas_call(
        body, name=name, grid=(HEAD_PAIRS, n_slabs), in_specs=in_specs,
        out_specs=[cur(0)] * 3, out_shape=[out] * 3,
        scratch_shapes=[pltpu.VMEM((R, 128), F32)] * 3,
        compiler_params=pltpu.CompilerParams(dimension_semantics=("parallel", "arbitrary"),
                                             vmem_limit_bytes=MATMUL_VMEM_BYTES),
    )(*args)


def _head_sum(x):
    i = lax.broadcasted_iota(jnp.int32, (128, 128), 0) // 64
    j = lax.broadcasted_iota(jnp.int32, (128, 128), 1) // 64
    return _dot_f32(x, (i == j).astype(F32), ones_on_right=True)


def _group_weights(l0, l1, l2):
    mx = jnp.maximum(jnp.maximum(l0, l1), l2)
    e0, e1, e2 = jnp.exp(l0 - mx), jnp.exp(l1 - mx), jnp.exp(l2 - mx)
    inv = 1.0 / (e0 + e1 + e2)
    return e0 * inv, e1 * inv, e2 * inv


def att_combine_fwd(o, l, name):
    def body(o0, o1, o2, l0, l1, l2, y_ref):
        w0, w1, w2 = _group_weights(l0[...], l1[...], l2[...])
        y_ref[...] = (o0[...] * w0 + o1[...] * w1 + o2[...] * w2).astype(y_ref.dtype)

    blk = pl.BlockSpec((ROW_TILE, ATT_WIDTH), lambda i: (i, 0))
    return pl.pallas_call(
        body, name=name, grid=(SEQ // ROW_TILE,), in_specs=[blk] * 6, out_specs=blk,
        out_shape=jax.ShapeDtypeStruct((SEQ, ATT_WIDTH), MXU_DTYPE),
    )(*o, *l)


def att_combine_bwd(o, l, dy, name):
    def body(o0, o1, o2, l0, l1, l2, dy_ref, do0, do1, do2, cr0, cr1, cr2):
        w = _group_weights(l0[...], l1[...], l2[...])
        dyv = dy_ref[...]
        dw = [_head_sum(dyv * o_ref[...]) for o_ref in (o0, o1, o2)]
        tot = w[0] * dw[0] + w[1] * dw[1] + w[2] * dw[2]
        for g, (do_ref, cr_ref) in enumerate(((do0, cr0), (do1, cr1), (do2, cr2))):
            do_ref[...] = dyv * w[g]
            cr_ref[...] = -w[g] * tot

    blk = pl.BlockSpec((ROW_TILE, 128), lambda i, j: (i, j))
    out = jax.ShapeDtypeStruct((SEQ, ATT_WIDTH), F32)
    res = pl.pallas_call(
        body, name=name, grid=(SEQ // ROW_TILE, HEAD_PAIRS), in_specs=[blk] * 7, out_specs=[blk] * 6, out_shape=[out] * 6,
    )(*o, *l, dy)
    return res[:N_GROUPS], res[N_GROUPS:]


SUM_ROW_TILES = (1024, 512, 256, 128, 64, 32, 16)
SUM_TILE_BYTES = 24 * 1024 * 1024
SUM_PARAMS = pltpu.CompilerParams(vmem_limit_bytes=MATMUL_VMEM_BYTES)


def _row_tile(rows, cols, operands):
    fit = [t for t in SUM_ROW_TILES if rows % t == 0]
    return next((t for t in fit if 2 * 4 * operands * t * cols <= SUM_TILE_BYTES), fit[-1])


def _shard_shape(rows, cols, axis):
    return (rows // N_CHIPS, cols) if axis == 0 else (rows, cols // N_CHIPS)


def _half_shape(rows, cols, axis):
    return (rows, cols // 2) if axis == 0 else (rows // 2, cols)


def _piece_shape(rows, cols, axis):
    return (rows // N_CHIPS, cols // 2) if axis == 0 else (rows // 2, cols // N_CHIPS)


def place_own_block(shard, chip, rows, cols, axis, name):
    sr, sc = _shard_shape(rows, cols, axis)
    tr = _row_tile(sr, sc, 2)

    def body(chip_ref, s_ref, o_ref):
        o_ref[...] = s_ref[...].astype(o_ref.dtype)

    if axis == 0:
        out_map = lambda i, chip_ref: (chip_ref[0] * (sr // tr) + i, 0)
    else:
        out_map = lambda i, chip_ref: (i, chip_ref[0])
    return pl.pallas_call(
        body, name=name, out_shape=jax.ShapeDtypeStruct((rows, cols), WEIGHT_COMM_DTYPE), compiler_params=SUM_PARAMS,
        grid_spec=pltpu.PrefetchScalarGridSpec(
            num_scalar_prefetch=1, grid=(sr // tr,), in_specs=[pl.BlockSpec((tr, sc), lambda i, chip_ref: (i, 0))],
            out_specs=pl.BlockSpec((tr, sc), out_map)),
    )(chip, shard)


def add_halves(g, theirs, core, rows, cols, axis, name):
    hr, hc = _half_shape(rows, cols, axis)
    tr = _row_tile(hr, hc, 3)

    def body(core_ref, g_ref, t_ref, o_ref):
        o_ref[...] = (g_ref[...].astype(F32) + t_ref[...].astype(F32)).astype(o_ref.dtype)

    if axis == 0:
        g_map = lambda i, core_ref: (i, core_ref[0])
    else:
        g_map = lambda i, core_ref: (core_ref[0] * (hr // tr) + i, 0)
    blk = pl.BlockSpec((tr, hc), lambda i, core_ref: (i, 0))
    return pl.pallas_call(
        body, name=name, out_shape=jax.ShapeDtypeStruct((hr, hc), GRAD_COMM_DTYPE), compiler_params=SUM_PARAMS,
        grid_spec=pltpu.PrefetchScalarGridSpec(
            num_scalar_prefetch=1, grid=(hr // tr,), in_specs=[pl.BlockSpec((tr, hc), g_map), blk], out_specs=blk),
    )(core, g, theirs)


def add_pieces(half, got, chip, rows, cols, axis, name):
    hr, _ = _half_shape(rows, cols, axis)
    pr, pc = _piece_shape(rows, cols, axis)
    tr = _row_tile(pr, pc, 5)

    def body(chip_ref, h_ref, got_ref, o_ref):
        o_ref[...] = (h_ref[...].astype(F32) + got_ref[0].astype(F32) + got_ref[1].astype(F32) + got_ref[2].astype(F32))

    if axis == 0:
        h_map = lambda i, chip_ref: (chip_ref[0] * (pr // tr) + i, 0)
    else:
        h_map = lambda i, chip_ref: (i, chip_ref[0])
    return pl.pallas_call(
        body, name=name, out_shape=jax.ShapeDtypeStruct((pr, pc), F32), compiler_params=SUM_PARAMS,
        grid_spec=pltpu.PrefetchScalarGridSpec(
            num_scalar_prefetch=1, grid=(pr // tr,),
            in_specs=[pl.BlockSpec((tr, pc), h_map), pl.BlockSpec((3, tr, pc), lambda i, chip_ref: (0, i, 0))],
            out_specs=pl.BlockSpec((tr, pc), lambda i, chip_ref: (i, 0))),
    )(chip, half, got)


def _adamw_math(w, g, m, v):
    nm = ADAM_B1 * m + (1.0 - ADAM_B1) * g
    nv = ADAM_B2 * v + (1.0 - ADAM_B2) * (g * g)
    m_hat = nm / (1.0 - ADAM_B1 ** ADAM_STEP)
    v_hat = nv / (1.0 - ADAM_B2 ** ADAM_STEP)
    return -ADAM_LR * (m_hat / (jnp.sqrt(v_hat) + ADAM_EPS) + ADAM_WD * w), nm, nv


def adamw_halves(w, mine, theirs, m, v, core, rows, cols, axis, name):
    sr, sc = _shard_shape(rows, cols, axis)
    pr, pc = _piece_shape(rows, cols, axis)
    tr = _row_tile(pr, pc, 9)
    nt = pr // tr

    def body(core_ref, w_ref, a_ref, b_ref, m_ref, v_ref, g_ref, d_ref, nm_ref, nv_ref):
        g = jnp.where(pl.program_id(0) == core_ref[0], a_ref[...], b_ref[...])
        g_ref[...] = g
        d_ref[...], nm_ref[...], nv_ref[...] = _adamw_math(w_ref[...], g, m_ref[...], v_ref[...])

    if axis == 0:
        full = pl.BlockSpec((tr, pc), lambda h, i, core_ref: (i, h))
    else:
        full = pl.BlockSpec((tr, pc), lambda h, i, core_ref: (h * nt + i, 0))
    part = pl.BlockSpec((tr, pc), lambda h, i, core_ref: (i, 0))
    out = jax.ShapeDtypeStruct((sr, sc), F32)
    return pl.pallas_call(
        body, name=name, out_shape=[out, out, out, out], compiler_params=SUM_PARAMS,
        grid_spec=pltpu.PrefetchScalarGridSpec(
            num_scalar_prefetch=1, grid=(2, nt), in_specs=[full, part, part, full, full], out_specs=[full] * 4),
    )(core, w, mine, theirs, m, v)


BIG = (
    ("ffn1_w_gate_up", D_MODEL, 2 * D_FF, 1),
    ("ffn1_w_down", D_FF, D_MODEL, 0),
    ("w_in", D_MODEL, IN_COLS, 1),
    ("w_branch_hg", HG_WIDTH, D_MODEL, 1),
    ("w_branch_att", ATT_WIDTH, D_MODEL, 1),
    ("w_out", D_MODEL, D_MODEL, 0),
    ("ffn2_w_gate_up", D_MODEL, 2 * D_FF, 1),
    ("ffn2_w_down", D_FF, D_MODEL, 0),
)
N_BIG = len(BIG)
ANY = pl.BlockSpec(memory_space=pl.ANY)


def _place():
    return lax.axis_index("x"), lax.axis_index("y"), lax.axis_index("c")


def _other_chips(x, y):
    return ((1 - x, y), (x, 1 - y), (1 - x, 1 - y))


MAX_COPY_CHUNKS = 16
CHUNK_ROW_ALIGN = 16


def _row_chunks(view):
    rows = view.shape[0]
    n = next(n for n in range(MAX_COPY_CHUNKS, 0, -1) if rows % (CHUNK_ROW_ALIGN * n) == 0 or n == 1)
    step = rows // n
    return [pl.ds(i * step, step) for i in range(n)]


def _remote(src, dst, send_sem, recv_sem, device):
    return pltpu.make_async_remote_copy(src_ref=src, dst_ref=dst, send_sem=send_sem, recv_sem=recv_sem,
                                        device_id=device, device_id_type=MESH)


def _start_remote(src, dst, send_sem, recv_sem, device):
    for rows in _row_chunks(src):
        _remote(src.at[rows, :], dst.at[rows, :], send_sem, recv_sem, device).start()
    return _remote(src, dst, send_sem, recv_sem, device)


HBM = pl.BlockSpec(memory_space=pltpu.HBM)
SEM = pl.BlockSpec(memory_space=pltpu.SEMAPHORE)
SPLIT_COPY_EFFECT = pltpu.SideEffectType.DATAFLOW_SIDE_EFFECTING
GROUPS = {"ffn1": (0, 1), "mix": (2, 3, 4, 5), "ffn2": (6, 7)}


def _in_hbm(a):
    return pltpu.with_memory_space_constraint(a, pltpu.HBM)


class _SemList:
    def __init__(self, refs):
        self.refs = refs
        self.at = self

    def __getitem__(self, index):
        w, k = index
        return self.refs[3 * w + k]


def _gather_piece(ref, rows, cols, axis, chip, c):
    sr, sc = _shard_shape(rows, cols, axis)
    j = 2 * chip[0] + chip[1]
    if axis == 0:
        return ref.at[pl.ds(j * sr + c * (sr // 2), sr // 2), :]
    return ref.at[pl.ds(c * (sr // 2), sr // 2), pl.ds(pl.multiple_of(j * sc, 128), sc)]


def _start_gather_sends(bufs, ws, send_sems, recv_sems):
    x, y, c = _place()
    for w, (_, r, cc, ax) in enumerate(ws):
        mine = _gather_piece(bufs[w], r, cc, ax, (x, y), c)
        for k, chip in enumerate(_other_chips(x, y)):
            _start_remote(mine, mine, send_sems.at[w, k], recv_sems.at[w, k], (*chip, c))


def _wait_gather_sends(bufs, ws, send_sems, recv_sems):
    x, y, c = _place()
    for w, (_, r, cc, ax) in enumerate(ws):
        for k, chip in enumerate(_other_chips(x, y)):
            got = _gather_piece(bufs[w], r, cc, ax, chip, c)
            _remote(got, got, send_sems.at[w, k], recv_sems.at[w, k], (x, y, c)).wait_recv()
    for w, (_, r, cc, ax) in enumerate(ws):
        mine = _gather_piece(bufs[w], r, cc, ax, (x, y), c)
        for k in range(3):
            _remote(mine, mine, send_sems.at[w, k], recv_sems.at[w, k], (x, y, c)).wait_send()


def _forward_halves(bufs, ws, send_sems, recv_sems):
    x, y, c = _place()
    passed = []
    for w, (_, r, cc, ax) in enumerate(ws):
        for k, chip in enumerate(_other_chips(x, y)):
            got = _gather_piece(bufs[w], r, cc, ax, chip, c)
            passed.append(_start_remote(got, got, send_sems.at[w, k], recv_sems.at[w, k], (x, y, 1 - c)))
    for w, (_, r, cc, ax) in enumerate(ws):
        for k, chip in enumerate(_other_chips(x, y)):
            got = _gather_piece(bufs[w], r, cc, ax, chip, 1 - c)
            _remote(got, got, send_sems.at[w, k], recv_sems.at[w, k], (x, y, c)).wait_recv()
    for cp in passed:
        cp.wait_send()


def gather_start(placed, after, group):
    ws = [BIG[i] for i in GROUPS[group]]
    n = len(ws)

    def body(*refs):
        bufs = refs[:n]
        send_sems, recv_sems = _SemList(refs[n + 1:4 * n + 1]), _SemList(refs[4 * n + 1:7 * n + 1])
        token = refs[-1]
        _start_gather_sends(bufs, ws, send_sems, recv_sems)
        token[...] = jnp.zeros_like(token)

    out = pl.pallas_call(
        body, name=f"gather_start_{group}", in_specs=[HBM] * n + [ANY],
        out_specs=[SEM] * (6 * n) + [HBM] * n + [pl.BlockSpec(memory_space=pltpu.VMEM)],
        out_shape=[pltpu.SemaphoreType.DMA(())] * (6 * n)
        + [pltpu.HBM((r, cc), WEIGHT_COMM_DTYPE) for _, r, cc, _ in ws] + [jax.ShapeDtypeStruct((8, 128), F32)],
        input_output_aliases={w: 6 * n + w for w in range(n)},
        compiler_params=pltpu.CompilerParams(has_side_effects=SPLIT_COPY_EFFECT),
    )(*[_in_hbm(p) for p in placed], after)
    return out[:3 * n], out[3 * n:6 * n], out[6 * n:7 * n], out[-1]


def gather_wait(bufs, send_sems, recv_sems, after, group):
    ws = [BIG[i] for i in GROUPS[group]]
    n = len(ws)

    def body(*refs):
        _wait_gather_sends(refs[:n], ws, _SemList(refs[n:n + 3 * n]), _SemList(refs[n + 3 * n:n + 6 * n]))

    return pl.pallas_call(
        body, name=f"gather_wait_{group}", in_specs=[HBM] * n + [SEM] * (6 * n) + [ANY] * len(after), out_specs=[HBM] * n,
        out_shape=[pltpu.HBM((r, cc), WEIGHT_COMM_DTYPE) for _, r, cc, _ in ws],
        input_output_aliases={w: w for w in range(n)},
        compiler_params=pltpu.CompilerParams(has_side_effects=SPLIT_COPY_EFFECT),
    )(*bufs, *send_sems, *recv_sems, *after)


def gather_forward(bufs, group):
    ws = [BIG[i] for i in GROUPS[group]]
    n = len(ws)

    def body(*refs):
        _forward_halves(refs[n:2 * n], ws, refs[2 * n], refs[2 * n + 1])

    return pl.pallas_call(
        body, name=f"gather_forward_{group}", in_specs=[ANY] * n, out_specs=[ANY] * n,
        out_shape=[jax.ShapeDtypeStruct((r, cc), WEIGHT_COMM_DTYPE) for _, r, cc, _ in ws],
        input_output_aliases={w: w for w in range(n)},
        scratch_shapes=[pltpu.SemaphoreType.DMA((n, 3))] * 2,
    )(*bufs)


def _half(ref, rows, cols, axis, c):
    if axis == 0:
        return ref.at[:, pl.ds(pl.multiple_of(c * (cols // 2), 128), cols // 2)]
    return ref.at[pl.ds(c * (rows // 2), rows // 2), :]


def _piece_of_half(ref, rows, cols, axis, chip):
    j = 2 * chip[0] + chip[1]
    pr, pc = _piece_shape(rows, cols, axis)
    if axis == 0:
        return ref.at[pl.ds(j * pr, pr), :]
    return ref.at[:, pl.ds(pl.multiple_of(j * pc, 128), pc)]


def exchange_halves(grads, group):
    ws = [BIG[i] for i in GROUPS[group]]
    n = len(ws)

    def body(*refs):
        ins, theirs = refs[:n], refs[n:2 * n]
        send_sems, recv_sems = refs[2 * n:]
        x, y, c = _place()
        copies = [_start_remote(_half(ins[w], r, cc, ax, 1 - c), theirs[w], send_sems.at[w], recv_sems.at[w], (x, y, 1 - c))
                  for w, (_, r, cc, ax) in enumerate(ws)]
        for cp in copies:
            cp.wait()

    return pl.pallas_call(
        body, name=f"exchange_halves_{group}", in_specs=[ANY] * n, out_specs=[ANY] * n,
        out_shape=[jax.ShapeDtypeStruct(_half_shape(r, cc, ax), GRAD_COMM_DTYPE) for _, r, cc, ax in ws],
        scratch_shapes=[pltpu.SemaphoreType.DMA((n,)), pltpu.SemaphoreType.DMA((n,))],
    )(*grads)


def _scatter_copies(halves, got, ws, send_sems, recv_sems, start):
    x, y, c = _place()
    copies = []
    for w, (_, r, cc, ax) in enumerate(ws):
        for k, chip in enumerate(_other_chips(x, y)):
            args = (_piece_of_half(halves[w], r, cc, ax, chip), got[w].at[k], send_sems.at[w, k], recv_sems.at[w, k], (*chip, c))
            copies.append(_start_remote(*args) if start else _remote(*args))
    return copies


def scatter_start(halves, group):
    ws = [BIG[i] for i in GROUPS[group]]
    n = len(ws)

    def body(*refs):
        sems = refs[2 * n:8 * n]
        _scatter_copies(refs[:n], refs[n:2 * n], ws, _SemList(sems[:3 * n]), _SemList(sems[3 * n:]), start=True)
        refs[-1][...] = jnp.zeros_like(refs[-1])

    landing = [lax.empty((3,) + _piece_shape(r, cc, ax), GRAD_COMM_DTYPE) for _, r, cc, ax in ws]
    out = pl.pallas_call(
        body, name=f"scatter_start_{group}", in_specs=[HBM] * (2 * n),
        out_specs=[SEM] * (6 * n) + [HBM] * (2 * n) + [pl.BlockSpec(memory_space=pltpu.VMEM)],
        out_shape=[pltpu.SemaphoreType.DMA(())] * (6 * n)
        + [pltpu.HBM(_half_shape(r, cc, ax), GRAD_COMM_DTYPE) for _, r, cc, ax in ws]
        + [pltpu.HBM((3,) + _piece_shape(r, cc, ax), GRAD_COMM_DTYPE) for _, r, cc, ax in ws]
        + [jax.ShapeDtypeStruct((8, 128), F32)],
        input_output_aliases={i: 6 * n + i for i in range(2 * n)},
        compiler_params=pltpu.CompilerParams(has_side_effects=SPLIT_COPY_EFFECT),
    )(*[_in_hbm(h) for h in halves], *[_in_hbm(b) for b in landing])
    return out[:3 * n], out[3 * n:6 * n], out[6 * n:7 * n], out[7 * n:8 * n], out[-1]


def scatter_wait(halves, got, send_sems, recv_sems, after, group):
    ws = [BIG[i] for i in GROUPS[group]]
    n = len(ws)

    def body(*refs):
        sems = refs[2 * n:8 * n]
        for cp in _scatter_copies(refs[:n], refs[n:2 * n], ws, _SemList(sems[:3 * n]), _SemList(sems[3 * n:]), start=False):
            cp.wait_send()
            cp.wait_recv()

    out = pl.pallas_call(
        body, name=f"scatter_wait_{group}", in_specs=[HBM] * (2 * n) + [SEM] * (6 * n) + [ANY] * len(after),
        out_specs=[HBM] * (2 * n),
        out_shape=[pltpu.HBM(_half_shape(r, cc, ax), GRAD_COMM_DTYPE) for _, r, cc, ax in ws]
        + [pltpu.HBM((3,) + _piece_shape(r, cc, ax), GRAD_COMM_DTYPE) for _, r, cc, ax in ws],
        input_output_aliases={i: i for i in range(2 * n)},
        compiler_params=pltpu.CompilerParams(has_side_effects=SPLIT_COPY_EFFECT),
    )(*halves, *got, *send_sems, *recv_sems, *after)
    return out[:n], out[n:]


def exchange_reduced(pieces, group):
    ws = [BIG[i] for i in GROUPS[group]]
    n = len(ws)

    def body(*refs):
        ins, theirs = refs[:n], refs[n:2 * n]
        send_sems, recv_sems = refs[2 * n:]
        x, y, c = _place()
        copies = [_start_remote(ins[w], theirs[w], send_sems.at[w], recv_sems.at[w], (x, y, 1 - c)) for w in range(n)]
        for cp in copies:
            cp.wait()

    return pl.pallas_call(
        body, name=f"exchange_reduced_{group}", in_specs=[ANY] * n, out_specs=[ANY] * n,
        out_shape=[jax.ShapeDtypeStruct(_piece_shape(r, cc, ax), F32) for _, r, cc, ax in ws],
        scratch_shapes=[pltpu.SemaphoreType.DMA((n,)), pltpu.SemaphoreType.DMA((n,))],
    )(*pieces)


N_DEV = 8
SMALL = ("ffn1_norm", "mix_norm", "hg_lower_bounds", "hg_out_norm", "ffn2_norm", "final_norm")
SMALL_STAGE_ROWS = 8


def small_step(loss, grads, w, m, v, behind):
    n = len(SMALL)
    shapes = [g.shape for g in grads]
    first_row = [sum(s[0] for s in shapes[:i]) for i in range(n + 1)]
    assert first_row[n] < SMALL_STAGE_ROWS
    loss_row = (pl.ds(first_row[n], 1), pl.ds(0, loss.shape[1]))

    def body(*refs):
        loss_ref, g_refs, w_refs, m_refs, v_refs = refs[0], refs[1:1 + n], refs[1 + n:1 + 2 * n], refs[1 + 2 * n:1 + 3 * n], refs[1 + 3 * n:1 + 4 * n]
        outs = refs[2 + 4 * n:3 + 8 * n]
        loss_out, dg_refs, d_refs, nm_refs, nv_refs = outs[0], outs[1:1 + n], outs[1 + n:1 + 2 * n], outs[1 + 2 * n:1 + 3 * n], outs[1 + 3 * n:]
        stage, gathered, send_sems, recv_sems = refs[3 + 8 * n:]
        x, y, c = _place()
        me = 4 * x + 2 * y + c

        def slot(i, shape):
            return pl.ds(first_row[i], shape[0]), pl.ds(0, shape[1])

        stage[...] = jnp.zeros_like(stage)
        for i, g_ref in enumerate(g_refs):
            stage[slot(i, shapes[i])] = g_ref[...]
        stage[loss_row] = loss_ref[pl.ds(0, 1), :]
        gathered[me] = stage[...]
        copies = []
        for k in range(1, N_DEV):
            peer = (x ^ (k >> 2), y ^ ((k >> 1) & 1), c ^ (k & 1))
            cp = pltpu.make_async_remote_copy(
                src_ref=stage, dst_ref=gathered.at[me], send_sem=send_sems.at[k - 1], recv_sem=recv_sems.at[k - 1],
                device_id=peer, device_id_type=MESH)
            cp.start()
            copies.append(cp)
        for cp in copies:
            cp.wait()
        acc = gathered[0]
        for k in range(1, N_DEV):
            acc = acc + gathered[k]
        stage[...] = acc
        loss_out[...] = jnp.broadcast_to(stage[loss_row], loss_out.shape)
        for i in range(n):
            g = stage[slot(i, shapes[i])]
            dg_refs[i][...] = g
            d_refs[i][...], nm_refs[i][...], nv_refs[i][...] = _adamw_math(w_refs[i][...], g, m_refs[i][...], v_refs[i][...])

    vm = pl.BlockSpec(memory_space=pltpu.VMEM)
    per_param = [jax.ShapeDtypeStruct(s, F32) for s in shapes]
    out = pl.pallas_call(
        body, name="small_step", in_specs=[vm] * (1 + 4 * n) + [ANY], out_specs=[vm] * (1 + 4 * n),
        out_shape=[jax.ShapeDtypeStruct(loss.shape, F32)] + per_param * 4,
        scratch_shapes=[pltpu.VMEM((SMALL_STAGE_ROWS, D_MODEL), F32),
                        pltpu.VMEM((N_DEV, SMALL_STAGE_ROWS, D_MODEL), F32),
                        pltpu.SemaphoreType.DMA((N_DEV - 1,)), pltpu.SemaphoreType.DMA((N_DEV - 1,))],
    )(loss, *grads, *w, *m, *v, behind)
    return out[0], out[1:1 + n], out[1 + n:1 + 2 * n], out[1 + 2 * n:1 + 3 * n], out[1 + 3 * n:]


def _swiglu_block_fwd(h, norm_g, w_gu, w_down, tag, behind=()):
    n = rmsnorm_fwd(h, norm_g, f"{tag}_norm", behind=behind)
    gu = matmul(n, w_gu, name=f"{tag}_gate_up")
    s = swiglu_fwd(gu, f"{tag}_swiglu")
    h_out = matmul(s, w_down, res=h, scale=0.5, name=f"{tag}_down")
    return h_out, (n, gu, s)


def _swiglu_block_bwd(h, norm_g, w_gu, w_down, saved, dh_out, df, tag, behind=()):
    n, gu, s = saved
    d_down = matmul(s, df, ta=True, scale=0.5, out_dtype=GRAD_COMM_DTYPE, name=f"{tag}_d_w_down")
    ds = matmul(df, w_down, tb=True, scale=0.5, behind=behind, name=f"{tag}_d_s")
    dgu = swiglu_bwd(gu, ds, f"{tag}_swiglu_bwd")
    d_gu = matmul(n, dgu, ta=True, out_dtype=GRAD_COMM_DTYPE, name=f"{tag}_d_w_gate_up")
    dn = matmul(dgu, w_gu, tb=True, name=f"{tag}_d_n")
    dh, dh_m, dg = rmsnorm_bwd(h, norm_g, dn, dh_out, f"{tag}_norm_bwd")
    return dh, dh_m, dg, d_gu, d_down


def local_step(x, target, small, exchange):
    big = {}
    token, big_ffn1 = exchange.weights("ffn1", x)
    big.update(big_ffn1)
    h1, saved1 = _swiglu_block_fwd(x, small["ffn1_norm"], big["ffn1_w_gate_up"], big["ffn1_w_down"], "ffn1", token)
    token, big_mix = exchange.weights("mix", h1)
    big.update(big_mix)
    u = rmsnorm_fwd(h1, small["mix_norm"], "mix_norm", behind=token)
    z = matmul(u, big["w_in"], name="w_in")
    p = small["hg_lower_bounds"]
    lb = 1.0 / (1.0 + jnp.exp(p[1:2] - p[0:1]))
    y_hg, o_raw, states = hgrn_fwd(z, lb, small["hg_out_norm"], "hgrn_fwd")
    o_att, l_att = zip(*[att_fwd(z, g, f"att_fwd_{g}") for g in range(N_GROUPS)])
    y_att = att_combine_fwd(o_att, l_att, "att_combine")
    bh = matmul(y_hg, big["w_branch_hg"], name="branch_hg")
    ba = matmul(y_att, big["w_branch_att"], name="branch_att")
    merged = merge_fwd(z, bh, ba, "merge")
    h2 = matmul(merged, big["w_out"], res=h1, name="w_out")
    token, big_ffn2 = exchange.weights("ffn2", h2)
    big.update(big_ffn2)
    h3, saved2 = _swiglu_block_fwd(h2, small["ffn2_norm"], big["ffn2_w_gate_up"], big["ffn2_w_down"], "ffn2", token)
    dh3, dh3_m, d_final, loss = final_norm_loss(h3, small["final_norm"], target, "final_norm_loss")

    gs, gb = {"final_norm": d_final}, {}
    dh2, dh2_m, gs["ffn2_norm"], gb["ffn2_w_gate_up"], gb["ffn2_w_down"] = _swiglu_block_bwd(
        h2, small["ffn2_norm"], big["ffn2_w_gate_up"], big["ffn2_w_down"], saved2, dh3, dh3_m, "ffn2")
    token = exchange.gradients("ffn2", gb, dh2)
    gb["w_out"] = matmul(merged, dh2_m, ta=True, out_dtype=GRAD_COMM_DTYPE, name="d_w_out")
    dmerged = matmul(dh2_m, big["w_out"], tb=True, behind=token, name="d_merged")
    dbh, dba, dgh, dga = merge_bwd(z, bh, ba, dmerged, "merge_bwd")
    gb["w_branch_hg"] = matmul(y_hg, dbh, ta=True, out_dtype=GRAD_COMM_DTYPE, name="d_w_branch_hg")
    gb["w_branch_att"] = matmul(y_att, dba, ta=True, out_dtype=GRAD_COMM_DTYPE, name="d_w_branch_att")
    dy_hg = matmul(dbh, big["w_branch_hg"], tb=True, name="d_y_hg")
    dy_att = matmul(dba, big["w_branch_att"], tb=True, name="d_y_att")
    dq, dfp, di, dog, d_lb, gs["hg_out_norm"] = hgrn_bwd(z, lb, small["hg_out_norm"], o_raw, states, dy_hg, "hgrn_bwd")
    do_att, corr = att_combine_bwd(o_att, l_att, dy_att, "att_combine_bwd")
    d_att = [part for g in range(N_GROUPS) for part in att_bwd(z, l_att[g], do_att[g], corr[g], g, f"att_bwd_{g}")]
    dz = jnp.concatenate([dq, dfp, di, dog, *d_att, dgh, dga], axis=1)
    gb["w_in"] = matmul(u, dz, ta=True, out_dtype=GRAD_COMM_DTYPE, name="d_w_in")
    du = matmul(dz, big["w_in"], tb=True, name="d_u")
    dh1, dh1_m, gs["mix_norm"] = rmsnorm_bwd(h1, small["mix_norm"], du, dh2, "mix_norm_bwd")
    token = exchange.gradients("mix", gb, dh1)
    dp0 = d_lb * lb * (1.0 - lb)
    gs["hg_lower_bounds"] = jnp.concatenate([dp0, -dp0], axis=0)
    dx, _, gs["ffn1_norm"], gb["ffn1_w_gate_up"], gb["ffn1_w_down"] = _swiglu_block_bwd(
        x, small["ffn1_norm"], big["ffn1_w_gate_up"], big["ffn1_w_down"], saved1, dh1, dh1_m, "ffn1", token)
    exchange.gradients("ffn1", gb, dx)
    return loss, dx, gs


WEIGHTS = ("ffn1_norm", "ffn1_w_gate_up", "ffn1_w_down", "mix_norm", "w_in", "hg_lower_bounds", "hg_out_norm",
           "w_branch_hg", "w_branch_att", "w_out", "ffn2_norm", "ffn2_w_gate_up", "ffn2_w_down", "final_norm")


class WeightExchange:
    ORDER = ("ffn1", "mix", "ffn2")

    def __init__(self, shards, core, chip):
        self.core, self.chip = core, chip
        self.scattering = None
        self.reduced = {}
        first = self.ORDER[0]
        self.placed = {BIG[i][0]: place_own_block(shards[BIG[i][0]], chip, *BIG[i][1:], f"place_{BIG[i][0]}")
                       for i in GROUPS[first]}
        self._start_gather(first, self.placed[self._names(first)[0]])
        chip_behind = chip + self.token[0, :1].astype(jnp.int32)
        for group in self.ORDER[1:]:
            for i in GROUPS[group]:
                n, r, cc, ax = BIG[i]
                self.placed[n] = place_own_block(shards[n], chip_behind, r, cc, ax, f"place_{n}")
        self.placed_behind = [self.placed[n] for group in self.ORDER[1:] for n in self._names(group)]

    def _names(self, group):
        return [BIG[i][0] for i in GROUPS[group]]

    def _start_gather(self, group, after):
        send_sems, recv_sems, bufs, self.token = gather_start([self.placed[n] for n in self._names(group)], after, group)
        self.gathering = (group, send_sems, recv_sems, bufs)

    def weights(self, group, h):
        pending, send_sems, recv_sems, bufs = self.gathering
        assert pending == group
        after = self.placed_behind if group == self.ORDER[0] else [h]
        whole = gather_forward(gather_wait(bufs, send_sems, recv_sems, after, group), group)
        later = self.ORDER.index(group) + 1
        behind = []
        if later < len(self.ORDER):
            self._start_gather(self.ORDER[later], whole[0])
            behind = [self.token]
        return behind, dict(zip(self._names(group), whole))

    def _finish_scatter(self, after):
        group, send_sems, recv_sems, halves, got = self.scattering
        halves, got = scatter_wait(halves, got, send_sems, recv_sems, after, group)
        ws = [BIG[i] for i in GROUPS[group]]
        mine = [add_pieces(h, g, self.chip, r, cc, ax, f"add_pieces_{n}") for (n, r, cc, ax), h, g in zip(ws, halves, got)]
        theirs = exchange_reduced(mine, group)
        self.reduced.update({n: (a, b) for (n, *_), a, b in zip(ws, mine, theirs)})
        self.scattering = None
        return theirs[0]

    def gradients(self, group, grads, dh):
        behind = [self._finish_scatter([dh])] if self.scattering is not None else []
        ws = [BIG[i] for i in GROUPS[group]]
        theirs = exchange_halves([grads[n] for n, *_ in ws], group)
        halves = [add_halves(grads[n], t, self.core, r, cc, ax, f"add_halves_{n}") for (n, r, cc, ax), t in zip(ws, theirs)]
        send_sems, recv_sems, halves, got, self.token = scatter_start(halves, group)
        self.scattering = (group, send_sems, recv_sems, halves, got)
        return behind + [self.token]

    def finish(self, after):
        self._finish_scatter(after)
        return self.reduced


def kernel(x, ffn1_norm, ffn1_w_gate_up, ffn1_w_down, mix_norm, w_in, hg_lower_bounds, hg_out_norm, w_branch_hg, w_branch_att, w_out, ffn2_norm, ffn2_w_gate_up, ffn2_w_down, final_norm, loss_target, m_ffn1_norm, m_ffn1_w_gate_up, m_ffn1_w_down, m_mix_norm, m_w_in, m_hg_lower_bounds, m_hg_out_norm, m_w_branch_hg, m_w_branch_att, m_w_out, m_ffn2_norm, m_ffn2_w_gate_up, m_ffn2_w_down, m_final_norm, v_ffn1_norm, v_ffn1_w_gate_up, v_ffn1_w_down, v_mix_norm, v_w_in, v_hg_lower_bounds, v_hg_out_norm, v_w_branch_hg, v_w_branch_att, v_w_out, v_ffn2_norm, v_ffn2_w_gate_up, v_ffn2_w_down, v_final_norm):
    w = dict(ffn1_norm=ffn1_norm, ffn1_w_gate_up=ffn1_w_gate_up, ffn1_w_down=ffn1_w_down, mix_norm=mix_norm, w_in=w_in,
             hg_lower_bounds=hg_lower_bounds, hg_out_norm=hg_out_norm, w_branch_hg=w_branch_hg, w_branch_att=w_branch_att,
             w_out=w_out, ffn2_norm=ffn2_norm, ffn2_w_gate_up=ffn2_w_gate_up, ffn2_w_down=ffn2_w_down, final_norm=final_norm)
    m = dict(ffn1_norm=m_ffn1_norm, ffn1_w_gate_up=m_ffn1_w_gate_up, ffn1_w_down=m_ffn1_w_down, mix_norm=m_mix_norm,
             w_in=m_w_in, hg_lower_bounds=m_hg_lower_bounds, hg_out_norm=m_hg_out_norm, w_branch_hg=m_w_branch_hg,
             w_branch_att=m_w_branch_att, w_out=m_w_out, ffn2_norm=m_ffn2_norm, ffn2_w_gate_up=m_ffn2_w_gate_up,
             ffn2_w_down=m_ffn2_w_down, final_norm=m_final_norm)
    v = dict(ffn1_norm=v_ffn1_norm, ffn1_w_gate_up=v_ffn1_w_gate_up, ffn1_w_down=v_ffn1_w_down, mix_norm=v_mix_norm,
             w_in=v_w_in, hg_lower_bounds=v_hg_lower_bounds, hg_out_norm=v_hg_out_norm, w_branch_hg=v_w_branch_hg,
             w_branch_att=v_w_branch_att, w_out=v_w_out, ffn2_norm=v_ffn2_norm, ffn2_w_gate_up=v_ffn2_w_gate_up,
             ffn2_w_down=v_ffn2_w_down, final_norm=v_final_norm)

    core = lax.axis_index("c").astype(jnp.int32).reshape(1)
    chip = (2 * lax.axis_index("x") + lax.axis_index("y")).astype(jnp.int32).reshape(1)
    exchange = WeightExchange({n: w[n][0] for n, *_ in BIG}, core, chip)
    small = {n: w[n] for n in SMALL}
    small["final_norm"] = final_norm.reshape(1, D_MODEL)

    loss, dx, gs = local_step(x[0], loss_target[0], small, exchange)

    grads, delta, new_m, new_v = {}, {}, {}, {}

    def update(group, core):
        for i in GROUPS[group]:
            n, r, cc, ax = BIG[i]
            a, b = exchange.reduced[n]
            g, d, nm, nv = adamw_halves(w[n][0], a, b, m[n][0], v[n][0], core, r, cc, ax, f"adamw_{n}")
            grads[n], delta[n], new_m[n], new_v[n] = g[None], d[None], nm[None], nv[None]

    core_behind = core + exchange.token[0, :1].astype(jnp.int32)
    update("ffn2", core_behind)
    update("mix", core_behind)
    exchange.finish(after=[delta[BIG[i][0]] for group in ("ffn2", "mix") for i in GROUPS[group]])
    update("ffn1", core)
    two_d = lambda a: a.reshape(1, D_MODEL) if a.ndim == 1 else a
    loss_sum, *small_out = small_step(loss, [gs[n] for n in SMALL], *[[two_d(p[n]) for n in SMALL] for p in (w, m, v)],
                                      behind=delta["ffn1_w_down"])
    for result, parts in zip((grads, delta, new_m, new_v), small_out):
        result.update({n: a.reshape(w[n].shape) for n, a in zip(SMALL, parts)})

    return (loss_sum[0, 0], dx[None], *[grads[n] for n in WEIGHTS], *[delta[n] for n in WEIGHTS],
            *[new_m[n] for n in WEIGHTS], *[new_v[n] for n in WEIGHTS])
```

```python
import numpy as np
import jax
import jax.numpy as jnp
from jax import lax
from jax.experimental import pallas as pl
from jax.experimental.pallas import tpu as pltpu

SEQ = 2048
D_MODEL = 1024
D_FF = 2816
HG_HEADS = 4
HG_DIM = 128
HG_WIDTH = 512
HG_CHUNK = 64
ATT_GROUPS = ((128, 1), (512, 4), (2048, 16))
ATT_HEADS = 8
ATT_WIDTH = 512
ATT_BLOCK = 128
ALIBI_MAX = 8.0
IN_COLS = 8704
EPS = 1e-6
NEG_INF = -1e30
ADAM_LR = 0.001
ADAM_B1 = 0.9
ADAM_B2 = 0.999
ADAM_EPS = 1e-08
ADAM_WD = 0.01
ADAM_STEP = 10

N_CHIPS = 4
MXU_DTYPE = jnp.bfloat16
WEIGHT_COMM_DTYPE = jnp.bfloat16
GRAD_COMM_DTYPE = jnp.bfloat16
ACT_DTYPE = jnp.bfloat16
MESH = pl.DeviceIdType.MESH
F32 = jnp.float32
HIGHEST = lax.Precision.HIGHEST


def _sigmoid(x):
    return 1.0 / (1.0 + jnp.exp(-x))


def _dot(a, b, ta=False, tb=False):
    dn = (((0 if ta else 1,), (1 if tb else 0,)), ((), ()))
    return lax.dot_general(a.astype(MXU_DTYPE), b.astype(MXU_DTYPE), dn, preferred_element_type=F32)


def _dot_f32(a, b, ones_on_right=False):
    x = a if ones_on_right else b
    hi = x.astype(jnp.bfloat16)
    rest = x - hi.astype(F32)
    mid = rest.astype(jnp.bfloat16)
    lo = (rest - mid.astype(F32)).astype(jnp.bfloat16)
    if ones_on_right:
        dot = lambda q: jnp.dot(q, b.astype(jnp.bfloat16), preferred_element_type=F32)
    else:
        dot = lambda q: jnp.dot(a.astype(jnp.bfloat16), q, preferred_element_type=F32)
    return dot(hi) + (dot(mid) + dot(lo))


def _split_bf16(x):
    hi = x.astype(jnp.bfloat16)
    return hi, (x - hi.astype(F32)).astype(jnp.bfloat16)


def _hdot(a, b, ta=False, tb=False):
    dn =(((0 if ta else 1,), (1 if tb else 0,)), ((), ()))
    (a_hi, a_lo), (b_hi, b_lo) = _split_bf16(a), _split_bf16(b)
    dot = lambda p, q: lax.dot_general(p, q, dn, preferred_element_type=F32)
    return dot(a_hi, b_hi) + (dot(a_lo, b_hi) + dot(a_hi, b_lo))


MATMUL_VMEM_BYTES = 48 * 1024 * 1024
MATMUL_TILE_BYTES = 36 * 1024 * 1024
MXU_ALIGN = 128


def _divisors(n, most):
    return [t for t in range(min(n, most), 0, -MXU_ALIGN) if n % t == 0 and t % MXU_ALIGN == 0]


def _matmul_tiles(M, N, K, in_bytes, out_bytes, has_res):
    best = None
    for tk in _divisors(K, K):
        nk = K // tk
        for tm in _divisors(M, 2048):
            for tn in _divisors(N, 512):
                tiles = 2 * in_bytes * (tm * tk + tk * tn) + 2 * out_bytes * tm * tn
                tiles += 4 * tm * tn * ((nk > 1) + 2 * has_res)
                if tiles > MATMUL_TILE_BYTES:
                    continue
                traffic = in_bytes * (M * K * (1 if nk == 1 else N // tn) + K * N * (M // tm))
                key = (traffic, -tm * tn * tk)
                if best is None or key < best[0]:
                    best = (key, (tm, tn, tk))
    return best[1]


def matmul(a, b, *, ta=False, tb=False, out_dtype=F32, res=None, scale=1.0, behind=(), name):
    if ta:
        K, M = a.shape
    else:
        M, K = a.shape
    if tb:
        N, K2 = b.shape
    else:
        K2, N = b.shape
    assert K == K2 and a.dtype == b.dtype
    tm, tn, tk = _matmul_tiles(M, N, K, a.dtype.itemsize, jnp.dtype(out_dtype).itemsize, res is not None)
    nk = K // tk

    def finish(r, r_ref, o_ref):
        if scale != 1.0:
            r = r * scale
        if res is not None:
            r = r_ref[...] + r
        o_ref[...] = r.astype(out_dtype)

    def body(*refs):
        a_ref, b_ref = refs[:2]
        r_ref = refs[2] if res is not None else None
        o_ref = refs[2 + (res is not None) + len(behind)]
        if nk == 1:
            finish(_dot(a_ref[...], b_ref[...], ta, tb), r_ref, o_ref)
            return
        acc = refs[-1]
        k = pl.program_id(2)

        @pl.when(k == 0)
        def _():
            acc[...] = jnp.zeros_like(acc)

        acc[...] += _dot(a_ref[...], b_ref[...], ta, tb)

        @pl.when(k == nk - 1)
        def _():
            finish(acc[...], r_ref, o_ref)

    a_spec = pl.BlockSpec((tk, tm), lambda i, j, k: (k, i)) if ta else pl.BlockSpec((tm, tk), lambda i, j, k: (i, k))
    b_spec = pl.BlockSpec((tn, tk), lambda i, j, k: (j, k)) if tb else pl.BlockSpec((tk, tn), lambda i, j, k: (k, j))
    in_specs = [a_spec, b_spec]
    args = [a, b]
    if res is not None:
        in_specs.append(pl.BlockSpec((tm, tn), lambda i, j, k: (i, j)))
        args.append(res)
    for earlier in behind:
        in_specs.append(pl.BlockSpec(memory_space=pl.ANY))
        args.append(earlier)
    return pl.pallas_call(
        body, name=name, grid=(M // tm, N // tn, nk), in_specs=in_specs,
        out_specs=pl.BlockSpec((tm, tn), lambda i, j, k: (i, j)),
        out_shape=jax.ShapeDtypeStruct((M, N), out_dtype),
        scratch_shapes=[pltpu.VMEM((tm, tn), F32)] if nk > 1 else [],
        compiler_params=pltpu.CompilerParams(dimension_semantics=("parallel", "parallel", "arbitrary"),
                                             vmem_limit_bytes=MATMUL_VMEM_BYTES),
    )(*args)


ROW_TILE = 256


def rmsnorm_fwd(x, g, name, behind=()):
    def body(x_ref, g_ref, *refs):
        n_ref = refs[-1]
        xv = x_ref[...]
        r = lax.rsqrt(jnp.mean(xv * xv, axis=-1, keepdims=True) + EPS)
        n_ref[...] = ((xv * r) * g_ref[...]).astype(n_ref.dtype)

    order = list(behind)
    return pl.pallas_call(
        body, name=name, grid=(SEQ // ROW_TILE,),
        in_specs=[pl.BlockSpec((ROW_TILE, D_MODEL), lambda i: (i, 0)), pl.BlockSpec((1, D_MODEL), lambda i: (0, 0))]
        + [pl.BlockSpec(memory_space=pl.ANY)] * len(order),
        out_specs=pl.BlockSpec((ROW_TILE, D_MODEL), lambda i: (i, 0)),
        out_shape=jax.ShapeDtypeStruct((SEQ, D_MODEL), MXU_DTYPE),
    )(x, g, *order)


def rmsnorm_bwd(x, g, dn, dres, name):
    def body(x_ref, g_ref, dn_ref, dr_ref, dx_ref, dxm_ref, dg_ref):
        xv = x_ref[...]
        r = lax.rsqrt(jnp.mean(xv * xv, axis=-1, keepdims=True) + EPS)
        xh = xv * r
        dnv = dn_ref[...]

        @pl.when(pl.program_id(0) == 0)
        def _():
            dg_ref[...] = jnp.zeros_like(dg_ref)

        dg_ref[...] += jnp.sum(dnv * xh, axis=0, keepdims=True)
        dxh = dnv * g_ref[...]
        dx = dr_ref[...] + r * (dxh - xh * jnp.mean(dxh * xh, axis=-1, keepdims=True))
        dx_ref[...] = dx
        dxm_ref[...] = dx.astype(dxm_ref.dtype)

    row = pl.BlockSpec((ROW_TILE, D_MODEL), lambda i: (i, 0))
    vec = pl.BlockSpec((1, D_MODEL), lambda i: (0, 0))
    return pl.pallas_call(
        body, name=name, grid=(SEQ // ROW_TILE,), in_specs=[row, vec, row, row], out_specs=[row, row, vec],
        out_shape=[jax.ShapeDtypeStruct((SEQ, D_MODEL), F32), jax.ShapeDtypeStruct((SEQ, D_MODEL), MXU_DTYPE),
                   jax.ShapeDtypeStruct((1, D_MODEL), F32)],
        compiler_params=pltpu.CompilerParams(dimension_semantics=("arbitrary",)),
    )(x, g, dn, dres)


def final_norm_loss(h, g, target, name):
    def body(h_ref, g_ref, t_ref, dh_ref, dhm_ref, dg_ref, loss_ref):
        xv = h_ref[...]
        r = lax.rsqrt(jnp.mean(xv * xv, axis=-1, keepdims=True) + EPS)
        xh = xv * r
        gv = g_ref[...]
        e = xh * gv - t_ref[...]

        @pl.when(pl.program_id(0) == 0)
        def _():
            dg_ref[...] = jnp.zeros_like(dg_ref)
            loss_ref[...] = jnp.zeros_like(loss_ref)

        part = 0.5 * jnp.sum(jnp.sum(e * e, axis=-1, keepdims=True) * (1.0 / D_MODEL), axis=0, keepdims=True)
        loss_ref[...] += jnp.broadcast_to(part, loss_ref.shape)
        dout = e * (1.0 / D_MODEL)
        dg_ref[...] += jnp.sum(dout * xh, axis=0, keepdims=True)
        dxh = dout * gv
        dh = r * (dxh - xh * jnp.mean(dxh * xh, axis=-1, keepdims=True))
        dh_ref[...] = dh
        dhm_ref[...] = dh.astype(dhm_ref.dtype)

    row = pl.BlockSpec((ROW_TILE, D_MODEL), lambda i: (i, 0))
    vec = pl.BlockSpec((1, D_MODEL), lambda i: (0, 0))
    return pl.pallas_call(
        body, name=name, grid=(SEQ // ROW_TILE,), in_specs=[row, vec, row],
        out_specs=[row, row, vec, pl.BlockSpec((8, 128), lambda i: (0, 0))],
        out_shape=[jax.ShapeDtypeStruct((SEQ, D_MODEL), F32), jax.ShapeDtypeStruct((SEQ, D_MODEL), MXU_DTYPE),
                   jax.ShapeDtypeStruct((1, D_MODEL), F32), jax.ShapeDtypeStruct((8, 128), F32)],
        compiler_params=pltpu.CompilerParams(dimension_semantics=("arbitrary",)),
    )(h, g, target)


FF_TILE = D_FF // 2


def swiglu_fwd(gu, name):
    def body(a_ref, b_ref, s_ref):
        a = a_ref[...].astype(F32)
        s_ref[...] = (a * _sigmoid(a) * b_ref[...].astype(F32)).astype(s_ref.dtype)

    return pl.pallas_call(
        body, name=name, grid=(SEQ // ROW_TILE, 2),
        in_specs=[pl.BlockSpec((ROW_TILE, FF_TILE), lambda i, j: (i, j)),
                  pl.BlockSpec((ROW_TILE, FF_TILE), lambda i, j: (i, j + 2))],
        out_specs=pl.BlockSpec((ROW_TILE, FF_TILE), lambda i, j: (i, j)),
        out_shape=jax.ShapeDtypeStruct((SEQ, D_FF), MXU_DTYPE),
    )(gu, gu)


def swiglu_bwd(gu, ds, name):
    rows = ROW_TILE // 2

    def body(a_ref, b_ref, ds_ref, o_ref):
        a = a_ref[...].astype(F32)
        sg = _sigmoid(a)
        dsv = ds_ref[...].astype(F32)
        o_ref[:, :D_FF] = (dsv * b_ref[...].astype(F32) * (sg * (1.0 + a * (1.0 - sg)))).astype(o_ref.dtype)
        o_ref[:, D_FF:] = (dsv * a * sg).astype(o_ref.dtype)

    return pl.pallas_call(
        body, name=name, grid=(SEQ // rows,),
        in_specs=[pl.BlockSpec((rows, D_FF), lambda i: (i, 0)), pl.BlockSpec((rows, D_FF), lambda i: (i, 1)),
                  pl.BlockSpec((rows, D_FF), lambda i: (i, 0))],
        out_specs=pl.BlockSpec((rows, 2 * D_FF), lambda i: (i, 0)),
        out_shape=jax.ShapeDtypeStruct((SEQ, 2 * D_FF), MXU_DTYPE), compiler_params=SUM_PARAMS,
    )(gu, gu, ds)


GATE_HG_BLK = 6656 // 512
GATE_ATT_BLK = 7680 // 512


def merge_fwd(z, bh, ba, name):
    def body(gh_ref, ga_ref, bh_ref, ba_ref, o_ref):
        o_ref[...] = (_sigmoid(gh_ref[...]) * bh_ref[...] + _sigmoid(ga_ref[...]) * ba_ref[...]).astype(o_ref.dtype)

    blk = pl.BlockSpec((ROW_TILE, 512), lambda i, j: (i, j))
    return pl.pallas_call(
        body, name=name, grid=(SEQ // ROW_TILE, 2),
        in_specs=[pl.BlockSpec((ROW_TILE, 512), lambda i, j: (i, GATE_HG_BLK + j)),
                  pl.BlockSpec((ROW_TILE, 512), lambda i, j: (i, GATE_ATT_BLK + j)), blk, blk],
        out_specs=blk, out_shape=jax.ShapeDtypeStruct((SEQ, D_MODEL), MXU_DTYPE),
    )(z, z, bh, ba)


def merge_bwd(z, bh, ba, dm, name):
    def body(gh_ref, ga_ref, bh_ref, ba_ref, dm_ref, dbh_ref, dba_ref, dgh_ref, dga_ref):
        dmv = dm_ref[...]
        sh = _sigmoid(gh_ref[...])
        sa = _sigmoid(ga_ref[...])
        dbh_ref[...] = (dmv * sh).astype(dbh_ref.dtype)
        dba_ref[...] = (dmv * sa).astype(dba_ref.dtype)
        dgh_ref[...] = (dmv * bh_ref[...] * (sh * (1.0 - sh))).astype(dgh_ref.dtype)
        dga_ref[...] = (dmv * ba_ref[...] * (sa * (1.0 - sa))).astype(dga_ref.dtype)

    blk = pl.BlockSpec((ROW_TILE, 512), lambda i, j: (i, j))
    out = jax.ShapeDtypeStruct((SEQ, D_MODEL), MXU_DTYPE)
    return pl.pallas_call(
        body, name=name, grid=(SEQ // ROW_TILE, 2),
        in_specs=[pl.BlockSpec((ROW_TILE, 512), lambda i, j: (i, GATE_HG_BLK + j)),
                  pl.BlockSpec((ROW_TILE, 512), lambda i, j: (i, GATE_ATT_BLK + j)), blk, blk, blk],
        out_specs=[blk, blk, blk, blk], out_shape=[out, out, out, out],
    )(z, z, bh, ba, dm)


N_CHUNKS = SEQ // HG_CHUNK
HG_STEP_CHUNKS = 4


def _hgrn_gates(q, fp, lb):
    C = HG_CHUNK
    sg = _sigmoid(fp)
    f = lb + (1.0 - lb) * sg
    lf = jnp.log(f)
    row = lax.broadcasted_iota(jnp.int32, (C, C), 0)
    col = lax.broadcasted_iota(jnp.int32, (C, C), 1)
    causal = row >= col
    G = _dot_f32(causal.astype(F32), lf)
    eG = jnp.exp(G)
    enG = jnp.exp(-G)
    qg = q * eG
    kg = (1.0 - f) * enG
    A = jnp.where(causal, _hdot(qg, kg, tb=True), 0.0)
    egl = jnp.exp(jnp.sum(lf, axis=0, keepdims=True))
    return sg, f, causal, eG, enG, qg, kg, A, egl


def hgrn_fwd(z, lb, gain, name):
    C, K = HG_CHUNK, HG_DIM

    def body(q_ref, f_ref, v_ref, og_ref, p_ref, g_ref, y_ref, o_ref, st_ref, state):
        @pl.when(pl.program_id(0) == 0)
        def _():
            state[...] = jnp.zeros_like(state)

        for cc in range(HG_STEP_CHUNKS):
            rows = pl.ds(cc * C, C)
            for h in range(HG_HEADS):
                hd = pl.ds(h * K, K)
                v = v_ref[rows, hd]
                _, _, _, _, _, qg, kg, A, egl = _hgrn_gates(q_ref[rows, hd], f_ref[rows, hd], p_ref[:, hd])
                st = state[h]
                st_ref[h, cc] = st
                o = _hdot(A, v) + _hdot(qg, st, tb=True)
                state[h] = st * egl + _hdot(v, kg * egl, ta=True)
                o_ref[rows, hd] = o
                rs = lax.rsqrt(jnp.mean(o * o, axis=-1, keepdims=True) + EPS)
                og = og_ref[rows, hd]
                y_ref[rows, hd] = (((o * rs) * g_ref[:, hd]) * (og * _sigmoid(og))).astype(y_ref.dtype)

    R = HG_STEP_CHUNKS * C

    def zcol(section):
        return pl.BlockSpec((R, HG_WIDTH), lambda c: (c, section))

    vec = pl.BlockSpec((1, HG_WIDTH), lambda c: (0, 0))
    blk = pl.BlockSpec((R, HG_WIDTH), lambda c: (c, 0))
    return pl.pallas_call(
        body, name=name, grid=(N_CHUNKS // HG_STEP_CHUNKS,),
        in_specs=[zcol(0), zcol(1), zcol(2), zcol(3), vec, vec],
        out_specs=[blk, blk, pl.BlockSpec((HG_HEADS, HG_STEP_CHUNKS, K, K), lambda c: (0, c, 0, 0))],
        out_shape=[jax.ShapeDtypeStruct((SEQ, HG_WIDTH), MXU_DTYPE), jax.ShapeDtypeStruct((SEQ, HG_WIDTH), F32),
                   jax.ShapeDtypeStruct((HG_HEADS, N_CHUNKS, K, K), F32)],
        scratch_shapes=[pltpu.VMEM((HG_HEADS, K, K), F32)],
        compiler_params=pltpu.CompilerParams(dimension_semantics=("arbitrary",)),
    )(z, z, z, z, lb, gain)


def hgrn_bwd(z, lb, gain, o_raw, states, dy, name):
    C, K = HG_CHUNK, HG_DIM

    def body(q_ref, f_ref, v_ref, og_ref, p_ref, g_ref, o_ref, st_ref, dy_ref,
             dq_ref, dfp_ref, dv_ref, dog_ref, dlb_ref, dgain_ref, dstate):
        @pl.when(pl.program_id(0) == 0)
        def _():
            dstate[...] = jnp.zeros_like(dstate)
            dlb_ref[...] = jnp.zeros_like(dlb_ref)
            dgain_ref[...] = jnp.zeros_like(dgain_ref)

        last = lax.broadcasted_iota(jnp.int32, (C, K), 0) == C - 1
        row = lax.broadcasted_iota(jnp.int32, (C, C), 0)
        col = lax.broadcasted_iota(jnp.int32, (C, C), 1)
        anti_causal = (col >= row).astype(F32)
        for cc in reversed(range(HG_STEP_CHUNKS)):
            rows = pl.ds(cc * C, C)
            for h in range(HG_HEADS):
                hd = pl.ds(h * K, K)
                v = v_ref[rows, hd]
                lb = p_ref[:, hd]
                sg, f, causal, eG, enG, qg, kg, A, egl = _hgrn_gates(q_ref[rows, hd], f_ref[rows, hd], lb)
                kd = kg * egl
                st = st_ref[h, cc]
                dst = dstate[h]
                o = o_ref[rows, hd]
                og = og_ref[rows, hd]
                gain_v = g_ref[:, hd]
                dyv = dy_ref[rows, hd]
                rs = lax.rsqrt(jnp.mean(o * o, axis=-1, keepdims=True) + EPS)
                on = o * rs
                sgo = _sigmoid(og)
                silu = og * sgo
                dog_ref[rows, hd] = (dyv * (on * gain_v) * (sgo * (1.0 + og * (1.0 - sgo)))).astype(dog_ref.dtype)
                dgain_ref[:, hd] += jnp.sum(dyv * silu * on, axis=0, keepdims=True)
                don = dyv * gain_v * silu
                do = rs * (don - on * jnp.mean(don * on, axis=-1, keepdims=True))
                dA = jnp.where(causal, _hdot(do, v, tb=True), 0.0)
                dv_ref[rows, hd] = (_hdot(A, do, ta=True) + _hdot(kd, dst, tb=True)).astype(dv_ref.dtype)
                dqg = _hdot(dA, kg) + _hdot(do, st)
                dkg = _hdot(dA, qg, ta=True)
                dkd = _hdot(v, dst)
                dstate[h] = dst * egl + _hdot(do, qg, ta=True)
                dgl = jnp.sum(st * dst, axis=0, keepdims=True) * egl
                dq_ref[rows, hd] = (dqg * eG).astype(dq_ref.dtype)
                dk = dkg * enG + dkd * (enG * egl)
                dG = dqg * qg - dkg * kg - dkd * kd
                extra = jnp.sum(dkd * kd, axis=0, keepdims=True) + dgl
                dG = dG + jnp.where(last, extra, 0.0)
                dlf = _dot_f32(anti_causal, dG)
                df = dlf / f - dk
                dfp_ref[rows, hd] = (df * (1.0 - lb) * (sg * (1.0 - sg))).astype(dfp_ref.dtype)
                dlb_ref[:, hd] += jnp.sum(df * (1.0 - sg), axis=0, keepdims=True)

    R = HG_STEP_CHUNKS * C
    n_steps = N_CHUNKS // HG_STEP_CHUNKS

    def rc(c):
        return n_steps - 1 - c

    def zcol(section):
        return pl.BlockSpec((R, HG_WIDTH), lambda c: (rc(c), section))

    vec = pl.BlockSpec((1, HG_WIDTH), lambda c: (0, 0))
    blk = pl.BlockSpec((R, HG_WIDTH), lambda c: (rc(c), 0))
    out = jax.ShapeDtypeStruct((SEQ, HG_WIDTH), MXU_DTYPE)
    small = jax.ShapeDtypeStruct((1, HG_WIDTH), F32)
    return pl.pallas_call(
        body, name=name, grid=(n_steps,),
        in_specs=[zcol(0), zcol(1), zcol(2), zcol(3), vec, vec, blk,
                  pl.BlockSpec((HG_HEADS, HG_STEP_CHUNKS, K, K), lambda c: (0, rc(c), 0, 0)), blk],
        out_specs=[blk, blk, blk, blk, vec, vec],
        out_shape=[out, out, out, out, small, small],
        scratch_shapes=[pltpu.VMEM((HG_HEADS, K, K), F32)],
        compiler_params=pltpu.CompilerParams(dimension_semantics=("arbitrary",)),
    )(z, z, z, z, lb, gain, o_raw, states, dy)


N_GROUPS = len(ATT_GROUPS)
HEAD_PAIRS = ATT_WIDTH // 128
ATT_COL0 = 4 * HG_WIDTH
UNROLLED_UNITS = 4
ATT_SLAB_BLOCKS = 4


def _alibi_coef():
    n = N_GROUPS * ATT_HEADS
    slopes = np.exp2(-ALIBI_MAX * np.arange(1, n + 1, dtype=np.float32) / n).astype(np.float32)
    dil = np.repeat(np.array([d for _, d in ATT_GROUPS], np.float32), ATT_HEADS)
    return jnp.asarray(slopes * dil, F32)


def _for_each_unit(n, fn):
    if n <= UNROLLED_UNITS:
        for u in range(n):
            fn(u)
    else:
        def group(i, carry):
            for j in range(UNROLLED_UNITS):
                fn(i * UNROLLED_UNITS + j)
            return carry
        lax.fori_loop(0, n // UNROLLED_UNITS, group, 0)


def _att_specs(g):
    B = ATT_BLOCK
    d = ATT_GROUPS[g][1]
    blocks = ATT_SLAB_BLOCKS if d == 1 else 1
    R = B * d * blocks
    n_slabs = SEQ // R
    multi = SEQ // d > B
    col0 = (ATT_COL0 + g * 3 * ATT_WIDTH) // 128

    def cur(col):
        return pl.BlockSpec((R, 128), lambda hp, s: (s, col + hp))

    def prev(col):
        return pl.BlockSpec((R, 128), lambda hp, s: (jnp.maximum(s - 1, 0), col + hp))

    def nxt(col):
        return pl.BlockSpec((R, 128), lambda hp, s: (jnp.minimum(s + 1, n_slabs - 1), col + hp))

    def unit(u, s):
        if d > 1:
            rows = pl.ds(u, B, stride=d)
            return rows, False, rows, jnp.where(s == 0, B, 0), False, rows, jnp.where(s == n_slabs - 1, B, 0)
        rows = pl.ds(u * B, B)
        inner_prev, inner_next = u > 0, u < blocks - 1
        return (rows, inner_prev, pl.ds((u - 1) * B if inner_prev else (blocks - 1) * B, B),
                0 if inner_prev else jnp.where(s == 0, B, 0),
                inner_next, pl.ds((u + 1) * B if inner_next else 0, B),
                0 if inner_next else jnp.where(s == n_slabs - 1, B, 0))

    return d * blocks, R, n_slabs, multi, col0, cur, prev, nxt, unit


def _head_lanes(j):
    lane = lax.broadcasted_iota(jnp.int32, (ATT_BLOCK, 128), 1)
    return (lane >= 64 * j) & (lane < 64 * (j + 1))


def _lane_value(x, sel):
    return jnp.max(jnp.where(sel, x, -3e38), axis=-1, keepdims=True)


def _stack_heads(x, sel0):
    return jnp.concatenate([jnp.where(sel0, x, 0.0), jnp.where(sel0, 0.0, x)], axis=0)


def _stack_values(x, sel0, lanes):
    swapped = pltpu.roll(x, 64, 1)
    stacked = jnp.concatenate([jnp.where(sel0, x, swapped), jnp.where(sel0, swapped, x)], axis=0)
    return stacked if lanes == 128 else jnp.concatenate([stacked] * (lanes // 128), axis=1)


def _pair_coef(coef_ref, g, hp):
    row = lax.broadcasted_iota(jnp.int32, (2 * ATT_BLOCK, 1), 0)
    first = g * ATT_HEADS + hp * 2
    return jnp.where(row < ATT_BLOCK, coef_ref[first], coef_ref[first + 1])


def _band(with_prev, first_key):
    B = ATT_BLOCK
    keys = 2 * B if with_prev else B
    qi = jnp.bitwise_and(lax.broadcasted_iota(jnp.int32, (2 * B, keys), 0), B - 1)
    kj = lax.broadcasted_iota(jnp.int32, (2 * B, keys), 1)
    delta = qi + (B if with_prev else 0) - kj
    valid = (delta >= 0) & (delta <= B)
    if with_prev:
        valid = valid & (kj >= first_key)
    return valid, delta.astype(F32)


def _band_next(first_key):
    B = ATT_BLOCK
    qi = jnp.bitwise_and(lax.broadcasted_iota(jnp.int32, (2 * B, B), 0), B - 1)
    kj = lax.broadcasted_iota(jnp.int32, (2 * B, B), 1)
    delta = qi + B - kj
    return (delta <= B) & (kj >= first_key), delta.astype(F32)


def att_fwd(z, g, name):
    B = ATT_BLOCK
    n_units, R, n_slabs, has_prev, col0, cur, prev, _, unit = _att_specs(g)

    def body(coef_ref, *refs):
        if has_prev:
            q_ref, kc_ref, vc_ref, kp_ref, vp_ref, o_ref, l_ref = refs
        else:
            q_ref, kc_ref, vc_ref, o_ref, l_ref = refs
        hp, s = pl.program_id(0), pl.program_id(1)
        cf2 = _pair_coef(coef_ref, g, hp)
        sel0 = _head_lanes(0)

        def one(u):
            rows, inner_prev, prev_rows, first_key, _, _, _ = unit(u, s)
            valid, dist = _band(has_prev, first_key)
            q2 = _stack_heads(q_ref[rows, :], sel0)
            kk, vv = kc_ref[rows, :], vc_ref[rows, :]
            if has_prev:
                k_from, v_from = (kc_ref, vc_ref) if inner_prev else (kp_ref, vp_ref)
                kk = jnp.concatenate([k_from[prev_rows, :], kk], axis=0)
                vv = jnp.concatenate([v_from[prev_rows, :], vv], axis=0)
            sc = jnp.where(valid, _dot(q2, kk, tb=True) * 0.125 - cf2 * dist, NEG_INF)
            mx = jnp.max(sc, axis=-1, keepdims=True)
            e = jnp.exp(sc - mx)
            den = jnp.sum(e, axis=-1, keepdims=True)
            o2 = _dot(e * (1.0 / den), vv)
            lse2 = mx + jnp.log(den)
            o_ref[rows, :] = jnp.where(sel0, o2[:B], o2[B:])
            l_ref[rows, :] = jnp.where(sel0, lse2[:B], lse2[B:])

        _for_each_unit(n_units, one)

    in_specs = [pl.BlockSpec(memory_space=pltpu.SMEM), cur(col0), cur(col0 + 4), cur(col0 + 8)]
    args = [_alibi_coef(), z, z, z]
    if has_prev:
        in_specs += [prev(col0 + 4), prev(col0 + 8)]
        args += [z, z]
    out = jax.ShapeDtypeStruct((SEQ, ATT_WIDTH), F32)
    return pl.pallas_call(
        body, name=name, grid=(HEAD_PAIRS, n_slabs), in_specs=in_specs,
        out_specs=[cur(0), cur(0)], out_shape=[out, out],
        compiler_params=pltpu.CompilerParams(dimension_semantics=("parallel", "arbitrary")),
    )(*args)


def att_bwd(z, l, do, corr, g, name):
    B = ATT_BLOCK
    n_units, R, n_slabs, neighbours, col0, cur, prev, nxt, unit = _att_specs(g)

    def body(coef_ref, *refs):
        if neighbours:
            (q_ref, kc_ref, vc_ref, l_ref, do_ref, cr_ref, kp_ref, vp_ref, qn_ref, ln_ref, don_ref, crn_ref,
             dq_ref, dk_ref, dv_ref, dq_sc, dk_sc, dv_sc) = refs
        else:
            q_ref, kc_ref, vc_ref, l_ref, do_ref, cr_ref, dq_ref, dk_ref, dv_ref, dq_sc, dk_sc, dv_sc = refs
        hp, s = pl.program_id(0), pl.program_id(1)
        cf2 = _pair_coef(coef_ref, g, hp)
        sel0 = _head_lanes(0)
        own = slice(B, 2 * B) if neighbours else slice(0, B)

        def one(u):
            rows, inner_prev, prev_rows, first_key, inner_next, next_rows, first_key_n = unit(u, s)
            valid, dist = _band(neighbours, first_key)
            kc, vc = kc_ref[rows, :], vc_ref[rows, :]
            kk, vv = kc, vc
            if neighbours:
                k_from, v_from = (kc_ref, vc_ref) if inner_prev else (kp_ref, vp_ref)
                kk = jnp.concatenate([k_from[prev_rows, :], kc], axis=0)
                vv = jnp.concatenate([v_from[prev_rows, :], vc], axis=0)
            q2, do2 = _stack_heads(q_ref[rows, :], sel0), _stack_heads(do_ref[rows, :], sel0)
            keys = kk.shape[0]
            lse2, cr2 = _stack_values(l_ref[rows, :], sel0, keys), _stack_values(cr_ref[rows, :], sel0, keys)
            p = jnp.exp(jnp.where(valid, _dot(q2, kk, tb=True) * 0.125 - cf2 * dist, NEG_INF) - lse2)
            ds = p * (_dot(do2, vv, tb=True) + cr2)
            dq2 = _dot(ds, kk)
            dk = _dot(ds, q2, ta=True)[own]
            dv = _dot(p, do2, ta=True)[own]
            if neighbours:
                valid_n, dist_n = _band_next(first_key_n)
                q_from, l_from, do_from, cr_from = ((q_ref, l_ref, do_ref, cr_ref) if inner_next
                                                    else (qn_ref, ln_ref, don_ref, crn_ref))
                qn2, don2 = _stack_heads(q_from[next_rows, :], sel0), _stack_heads(do_from[next_rows, :], sel0)
                lse_n2 = _stack_values(l_from[next_rows, :], sel0, B)
                cr_n2 = _stack_values(cr_from[next_rows, :], sel0, B)
                p_n = jnp.exp(jnp.where(valid_n, _dot(qn2, kc, tb=True) * 0.125 - cf2 * dist_n, NEG_INF) - lse_n2)
                ds_n = p_n * (_dot(don2, vc, tb=True) + cr_n2)
                dk = dk + _dot(ds_n, qn2, ta=True)
                dv = dv + _dot(p_n, don2, ta=True)
            dq_sc[rows, :] = jnp.where(sel0, dq2[:B], dq2[B:]) * 0.125
            dk_sc[rows, :] = dk * 0.125
            dv_sc[rows, :] = dv

        _for_each_unit(n_units, one)
        dq_ref[...] = dq_sc[...].astype(dq_ref.dtype)
        dk_ref[...] = dk_sc[...].astype(dk_ref.dtype)
        dv_ref[...] = dv_sc[...].astype(dv_ref.dtype)

    in_specs = [pl.BlockSpec(memory_space=pltpu.SMEM), cur(col0), cur(col0 + 4), cur(col0 + 8), cur(0), cur(0), cur(0)]
    args = [_alibi_coef(), z, z, z, l, do, corr]
    if neighbours:
        in_specs += [prev(col0 + 4), prev(col0 + 8), nxt(col0), nxt(0), nxt(0), nxt(0)]
        args += [z, z, z, l, do, corr]
    out = jax.ShapeDtypeStruct((SEQ, ATT_WIDTH), MXU_DTYPE)
    return pl.pallas_call(
        body, name=name, grid=(HEAD_PAIRS, n_slabs), in_specs=in_specs,
        out_specs=[cur(0)] * 3, out_shape=[out] * 3,
        scratch_shapes=[pltpu.VMEM((R, 128), F32)] * 3,
        compiler_params=pltpu.CompilerParams(dimension_semantics=("parallel", "arbitrary"),
                                             vmem_limit_bytes=MATMUL_VMEM_BYTES),
    )(*args)


def _head_sum(x):
    i = lax.broadcasted_iota(jnp.int32, (128, 128), 0) // 64
    j = lax.broadcasted_iota(jnp.int32, (128, 128), 1) // 64
    return _dot_f32(x, (i == j).astype(F32), ones_on_right=True)


def _group_weights(l0, l1, l2):
    mx = jnp.maximum(jnp.maximum(l0, l1), l2)
    e0, e1, e2 = jnp.exp(l0 - mx), jnp.exp(l1 - mx), jnp.exp(l2 - mx)
    inv = 1.0 / (e0 + e1 + e2)
    return e0 * inv, e1 * inv, e2 * inv


def att_combine_fwd(o, l, name):
    def body(o0, o1, o2, l0, l1, l2, y_ref):
        w0, w1, w2 = _group_weights(l0[...], l1[...], l2[...])
        y_ref[...] = (o0[...] * w0 + o1[...] * w1 + o2[...] * w2).astype(y_ref.dtype)

    blk = pl.BlockSpec((ROW_TILE, ATT_WIDTH), lambda i: (i, 0))
    return pl.pallas_call(
        body, name=name, grid=(SEQ // ROW_TILE,), in_specs=[blk] * 6, out_specs=blk,
        out_shape=jax.ShapeDtypeStruct((SEQ, ATT_WIDTH), MXU_DTYPE),
    )(*o, *l)


def att_combine_bwd(o, l, dy, name):
    def body(o0, o1, o2, l0, l1, l2, dy_ref, do0, do1, do2, cr0, cr1, cr2):
        w = _group_weights(l0[...], l1[...], l2[...])
        dyv = dy_ref[...]
        dw = [_head_sum(dyv * o_ref[...]) for o_ref in (o0, o1, o2)]
        tot = w[0] * dw[0] + w[1] * dw[1] + w[2] * dw[2]
        for g, (do_ref, cr_ref) in enumerate(((do0, cr0), (do1, cr1), (do2, cr2))):
            do_ref[...] = dyv * w[g]
            cr_ref[...] = -w[g] * tot

    blk = pl.BlockSpec((ROW_TILE, 128), lambda i, j: (i, j))
    out = jax.ShapeDtypeStruct((SEQ, ATT_WIDTH), F32)
    res = pl.pallas_call(
        body, name=name, grid=(SEQ // ROW_TILE, HEAD_PAIRS), in_specs=[blk] * 7, out_specs=[blk] * 6, out_shape=[out] * 6,
    )(*o, *l, dy)
    return res[:N_GROUPS], res[N_GROUPS:]


SUM_ROW_TILES = (1024, 512, 256, 128, 64, 32, 16)
SUM_TILE_BYTES = 24 * 1024 * 1024
SUM_PARAMS = pltpu.CompilerParams(vmem_limit_bytes=MATMUL_VMEM_BYTES)


def _row_tile(rows, cols, operands):
    fit = [t for t in SUM_ROW_TILES if rows % t == 0]
    return next((t for t in fit if 2 * 4 * operands * t * cols <= SUM_TILE_BYTES), fit[-1])


def _shard_shape(rows, cols, axis):
    return (rows // N_CHIPS, cols) if axis == 0 else (rows, cols // N_CHIPS)


def _half_shape(rows, cols, axis):
    return (rows, cols // 2) if axis == 0 else (rows // 2, cols)


def _piece_shape(rows, cols, axis):
    return (rows // N_CHIPS, cols // 2) if axis == 0 else (rows // 2, cols // N_CHIPS)


def place_own_block(shard, chip, rows, cols, axis, name):
    sr, sc = _shard_shape(rows, cols, axis)
    tr = _row_tile(sr, sc, 2)

    def body(chip_ref, s_ref, o_ref):
        o_ref[...] = s_ref[...].astype(o_ref.dtype)

    if axis == 0:
        out_map = lambda i, chip_ref: (chip_ref[0] * (sr // tr) + i, 0)
    else:
        out_map = lambda i, chip_ref: (i, chip_ref[0])
    return pl.pallas_call(
        body, name=name, out_shape=jax.ShapeDtypeStruct((rows, cols), WEIGHT_COMM_DTYPE), compiler_params=SUM_PARAMS,
        grid_spec=pltpu.PrefetchScalarGridSpec(
            num_scalar_prefetch=1, grid=(sr // tr,), in_specs=[pl.BlockSpec((tr, sc), lambda i, chip_ref: (i, 0))],
            out_specs=pl.BlockSpec((tr, sc), out_map)),
    )(chip, shard)


def add_halves(g, theirs, core, rows, cols, axis, name):
    hr, hc = _half_shape(rows, cols, axis)
    tr = _row_tile(hr, hc, 3)

    def body(core_ref, g_ref, t_ref, o_ref):
        o_ref[...] = (g_ref[...].astype(F32) + t_ref[...].astype(F32)).astype(o_ref.dtype)

    if axis == 0:
        g_map = lambda i, core_ref: (i, core_ref[0])
    else:
        g_map = lambda i, core_ref: (core_ref[0] * (hr // tr) + i, 0)
    blk = pl.BlockSpec((tr, hc), lambda i, core_ref: (i, 0))
    return pl.pallas_call(
        body, name=name, out_shape=jax.ShapeDtypeStruct((hr, hc), GRAD_COMM_DTYPE), compiler_params=SUM_PARAMS,
        grid_spec=pltpu.PrefetchScalarGridSpec(
            num_scalar_prefetch=1, grid=(hr // tr,), in_specs=[pl.BlockSpec((tr, hc), g_map), blk], out_specs=blk),
    )(core, g, theirs)


def add_pieces(half, got, chip, rows, cols, axis, name):
    hr, _ = _half_shape(rows, cols, axis)
    pr, pc = _piece_shape(rows, cols, axis)
    tr = _row_tile(pr, pc, 5)

    def body(chip_ref, h_ref, got_ref, o_ref):
        o_ref[...] = (h_ref[...].astype(F32) + got_ref[0].astype(F32) + got_ref[1].astype(F32) + got_ref[2].astype(F32))

    if axis == 0:
        h_map = lambda i, chip_ref: (chip_ref[0] * (pr // tr) + i, 0)
    else:
        h_map = lambda i, chip_ref: (i, chip_ref[0])
    return pl.pallas_call(
        body, name=name, out_shape=jax.ShapeDtypeStruct((pr, pc), F32), compiler_params=SUM_PARAMS,
        grid_spec=pltpu.PrefetchScalarGridSpec(
            num_scalar_prefetch=1, grid=(pr // tr,),
            in_specs=[pl.BlockSpec((tr, pc), h_map), pl.BlockSpec((3, tr, pc), lambda i, chip_ref: (0, i, 0))],
            out_specs=pl.BlockSpec((tr, pc), lambda i, chip_ref: (i, 0))),
    )(chip, half, got)


def _adamw_math(w, g, m, v):
    nm = ADAM_B1 * m + (1.0 - ADAM_B1) * g
    nv = ADAM_B2 * v + (1.0 - ADAM_B2) * (g * g)
    m_hat = nm / (1.0 - ADAM_B1 ** ADAM_STEP)
    v_hat = nv / (1.0 - ADAM_B2 ** ADAM_STEP)
    return -ADAM_LR * (m_hat / (jnp.sqrt(v_hat) + ADAM_EPS) + ADAM_WD * w), nm, nv


def adamw_halves(w, mine, theirs, m, v, core, rows, cols, axis, name):
    sr, sc = _shard_shape(rows, cols, axis)
    pr, pc = _piece_shape(rows, cols, axis)
    tr = _row_tile(pr, pc, 9)
    nt = pr // tr

    def body(core_ref, w_ref, a_ref, b_ref, m_ref, v_ref, g_ref, d_ref, nm_ref, nv_ref):
        g = jnp.where(pl.program_id(0) == core_ref[0], a_ref[...], b_ref[...])
        g_ref[...] = g
        d_ref[...], nm_ref[...], nv_ref[...] = _adamw_math(w_ref[...], g, m_ref[...], v_ref[...])

    if axis == 0:
        full = pl.BlockSpec((tr, pc), lambda h, i, core_ref: (i, h))
    else:
        full = pl.BlockSpec((tr, pc), lambda h, i, core_ref: (h * nt + i, 0))
    part = pl.BlockSpec((tr, pc), lambda h, i, core_ref: (i, 0))
    out = jax.ShapeDtypeStruct((sr, sc), F32)
    return pl.pallas_call(
        body, name=name, out_shape=[out, out, out, out], compiler_params=SUM_PARAMS,
        grid_spec=pltpu.PrefetchScalarGridSpec(
            num_scalar_prefetch=1, grid=(2, nt), in_specs=[full, part, part, full, full], out_specs=[full] * 4),
    )(core, w, mine, theirs, m, v)


BIG = (
    ("ffn1_w_gate_up", D_MODEL, 2 * D_FF, 1),
    ("ffn1_w_down", D_FF, D_MODEL, 0),
    ("w_in", D_MODEL, IN_COLS, 1),
    ("w_branch_hg", HG_WIDTH, D_MODEL, 1),
    ("w_branch_att", ATT_WIDTH, D_MODEL, 1),
    ("w_out", D_MODEL, D_MODEL, 0),
    ("ffn2_w_gate_up", D_MODEL, 2 * D_FF, 1),
    ("ffn2_w_down", D_FF, D_MODEL, 0),
)
N_BIG = len(BIG)
ANY = pl.BlockSpec(memory_space=pl.ANY)


def _place():
    return lax.axis_index("x"), lax.axis_index("y"), lax.axis_index("c")


def _other_chips(x, y):
    return ((1 - x, y), (x, 1 - y), (1 - x, 1 - y))


MAX_COPY_CHUNKS = 16
CHUNK_ROW_ALIGN = 16


def _row_chunks(view):
    rows = view.shape[0]
    n = next(n for n in range(MAX_COPY_CHUNKS, 0, -1) if rows % (CHUNK_ROW_ALIGN * n) == 0 or n == 1)
    step = rows // n
    return [pl.ds(i * step, step) for i in range(n)]


def _remote(src, dst, send_sem, recv_sem, device):
    return pltpu.make_async_remote_copy(src_ref=src, dst_ref=dst, send_sem=send_sem, recv_sem=recv_sem,
                                        device_id=device, device_id_type=MESH)


def _start_remote(src, dst, send_sem, recv_sem, device):
    for rows in _row_chunks(src):
        _remote(src.at[rows, :], dst.at[rows, :], send_sem, recv_sem, device).start()
    return _remote(src, dst, send_sem, recv_sem, device)


HBM = pl.BlockSpec(memory_space=pltpu.HBM)
SEM = pl.BlockSpec(memory_space=pltpu.SEMAPHORE)
SPLIT_COPY_EFFECT = pltpu.SideEffectType.DATAFLOW_SIDE_EFFECTING
GROUPS = {"ffn1": (0, 1), "mix": (2, 3, 4, 5), "ffn2": (6, 7)}


def _in_hbm(a):
    return pltpu.with_memory_space_constraint(a, pltpu.HBM)


class _SemList:
    def __init__(self, refs):
        self.refs = refs
        self.at = self

    def __getitem__(self, index):
        w, k = index
        return self.refs[3 * w + k]


def _gather_piece(ref, rows, cols, axis, chip, c):
    sr, sc = _shard_shape(rows, cols, axis)
    j = 2 * chip[0] + chip[1]
    if axis == 0:
        return ref.at[pl.ds(j * sr + c * (sr // 2), sr // 2), :]
    return ref.at[pl.ds(c * (sr // 2), sr // 2), pl.ds(pl.multiple_of(j * sc, 128), sc)]


def _start_gather_sends(bufs, ws, send_sems, recv_sems):
    x, y, c = _place()
    for w, (_, r, cc, ax) in enumerate(ws):
        mine = _gather_piece(bufs[w], r, cc, ax, (x, y), c)
        for k, chip in enumerate(_other_chips(x, y)):
            _start_remote(mine, mine, send_sems.at[w, k], recv_sems.at[w, k], (*chip, c))


def _wait_gather_sends(bufs, ws, send_sems, recv_sems):
    x, y, c = _place()
    for w, (_, r, cc, ax) in enumerate(ws):
        for k, chip in enumerate(_other_chips(x, y)):
            got = _gather_piece(bufs[w], r, cc, ax, chip, c)
            _remote(got, got, send_sems.at[w, k], recv_sems.at[w, k], (x, y, c)).wait_recv()
    for w, (_, r, cc, ax) in enumerate(ws):
        mine = _gather_piece(bufs[w], r, cc, ax, (x, y), c)
        for k in range(3):
            _remote(mine, mine, send_sems.at[w, k], recv_sems.at[w, k], (x, y, c)).wait_send()


def _forward_halves(bufs, ws, send_sems, recv_sems):
    x, y, c = _place()
    passed = []
    for w, (_, r, cc, ax) in enumerate(ws):
        for k, chip in enumerate(_other_chips(x, y)):
            got = _gather_piece(bufs[w], r, cc, ax, chip, c)
            passed.append(_start_remote(got, got, send_sems.at[w, k], recv_sems.at[w, k], (x, y, 1 - c)))
    for w, (_, r, cc, ax) in enumerate(ws):
        for k, chip in enumerate(_other_chips(x, y)):
            got = _gather_piece(bufs[w], r, cc, ax, chip, 1 - c)
            _remote(got, got, send_sems.at[w, k], recv_sems.at[w, k], (x, y, c)).wait_recv()
    for cp in passed:
        cp.wait_send()


def gather_start(placed, after, group):
    ws = [BIG[i] for i in GROUPS[group]]
    n = len(ws)

    def body(*refs):
        bufs = refs[:n]
        send_sems, recv_sems = _SemList(refs[n + 1:4 * n + 1]), _SemList(refs[4 * n + 1:7 * n + 1])
        token = refs[-1]
        _start_gather_sends(bufs, ws, send_sems, recv_sems)
        token[...] = jnp.zeros_like(token)

    out = pl.pallas_call(
        body, name=f"gather_start_{group}", in_specs=[HBM] * n + [ANY],
        out_specs=[SEM] * (6 * n) + [HBM] * n + [pl.BlockSpec(memory_space=pltpu.VMEM)],
        out_shape=[pltpu.SemaphoreType.DMA(())] * (6 * n)
        + [pltpu.HBM((r, cc), WEIGHT_COMM_DTYPE) for _, r, cc, _ in ws] + [jax.ShapeDtypeStruct((8, 128), F32)],
        input_output_aliases={w: 6 * n + w for w in range(n)},
        compiler_params=pltpu.CompilerParams(has_side_effects=SPLIT_COPY_EFFECT),
    )(*[_in_hbm(p) for p in placed], after)
    return out[:3 * n], out[3 * n:6 * n], out[6 * n:7 * n], out[-1]


def gather_wait(bufs, send_sems, recv_sems, after, group):
    ws = [BIG[i] for i in GROUPS[group]]
    n = len(ws)

    def body(*refs):
        _wait_gather_sends(refs[:n], ws, _SemList(refs[n:n + 3 * n]), _SemList(refs[n + 3 * n:n + 6 * n]))

    return pl.pallas_call(
        body, name=f"gather_wait_{group}", in_specs=[HBM] * n + [SEM] * (6 * n) + [ANY] * len(after), out_specs=[HBM] * n,
        out_shape=[pltpu.HBM((r, cc), WEIGHT_COMM_DTYPE) for _, r, cc, _ in ws],
        input_output_aliases={w: w for w in range(n)},
        compiler_params=pltpu.CompilerParams(has_side_effects=SPLIT_COPY_EFFECT),
    )(*bufs, *send_sems, *recv_sems, *after)


def gather_forward(bufs, group):
    ws = [BIG[i] for i in GROUPS[group]]
    n = len(ws)

    def body(*refs):
        _forward_halves(refs[n:2 * n], ws, refs[2 * n], refs[2 * n + 1])

    return pl.pallas_call(
        body, name=f"gather_forward_{group}", in_specs=[ANY] * n, out_specs=[ANY] * n,
        out_shape=[jax.ShapeDtypeStruct((r, cc), WEIGHT_COMM_DTYPE) for _, r, cc, _ in ws],
        input_output_aliases={w: w for w in range(n)},
        scratch_shapes=[pltpu.SemaphoreType.DMA((n, 3))] * 2,
    )(*bufs)


def _half(ref, rows, cols, axis, c):
    if axis == 0:
        return ref.at[:, pl.ds(pl.multiple_of(c * (cols // 2), 128), cols // 2)]
    return ref.at[pl.ds(c * (rows // 2), rows // 2), :]


def _piece_of_half(ref, rows, cols, axis, chip):
    j = 2 * chip[0] + chip[1]
    pr, pc = _piece_shape(rows, cols, axis)
    if axis == 0:
        return ref.at[pl.ds(j * pr, pr), :]
    return ref.at[:, pl.ds(pl.multiple_of(j * pc, 128), pc)]


def exchange_halves(grads, group):
    ws = [BIG[i] for i in GROUPS[group]]
    n = len(ws)

    def body(*refs):
        ins, theirs = refs[:n], refs[n:2 * n]
        send_sems, recv_sems = refs[2 * n:]
        x, y, c = _place()
        copies = [_start_remote(_half(ins[w], r, cc, ax, 1 - c), theirs[w], send_sems.at[w], recv_sems.at[w], (x, y, 1 - c))
                  for w, (_, r, cc, ax) in enumerate(ws)]
        for cp in copies:
            cp.wait()

    return pl.pallas_call(
        body, name=f"exchange_halves_{group}", in_specs=[ANY] * n, out_specs=[ANY] * n,
        out_shape=[jax.ShapeDtypeStruct(_half_shape(r, cc, ax), GRAD_COMM_DTYPE) for _, r, cc, ax in ws],
        scratch_shapes=[pltpu.SemaphoreType.DMA((n,)), pltpu.SemaphoreType.DMA((n,))],
    )(*grads)


def _scatter_copies(halves, got, ws, send_sems, recv_sems, start):
    x, y, c = _place()
    copies = []
    for w, (_, r, cc, ax) in enumerate(ws):
        for k, chip in enumerate(_other_chips(x, y)):
            args = (_piece_of_half(halves[w], r, cc, ax, chip), got[w].at[k], send_sems.at[w, k], recv_sems.at[w, k], (*chip, c))
            copies.append(_start_remote(*args) if start else _remote(*args))
    return copies


def scatter_start(halves, group):
    ws = [BIG[i] for i in GROUPS[group]]
    n = len(ws)

    def body(*refs):
        sems = refs[2 * n:8 * n]
        _scatter_copies(refs[:n], refs[n:2 * n], ws, _SemList(sems[:3 * n]), _SemList(sems[3 * n:]), start=True)
        refs[-1][...] = jnp.zeros_like(refs[-1])

    landing = [lax.empty((3,) + _piece_shape(r, cc, ax), GRAD_COMM_DTYPE) for _, r, cc, ax in ws]
    out = pl.pallas_call(
        body, name=f"scatter_start_{group}", in_specs=[HBM] * (2 * n),
        out_specs=[SEM] * (6 * n) + [HBM] * (2 * n) + [pl.BlockSpec(memory_space=pltpu.VMEM)],
        out_shape=[pltpu.SemaphoreType.DMA(())] * (6 * n)
        + [pltpu.HBM(_half_shape(r, cc, ax), GRAD_COMM_DTYPE) for _, r, cc, ax in ws]
        + [pltpu.HBM((3,) + _piece_shape(r, cc, ax), GRAD_COMM_DTYPE) for _, r, cc, ax in ws]
        + [jax.ShapeDtypeStruct((8, 128), F32)],
        input_output_aliases={i: 6 * n + i for i in range(2 * n)},
        compiler_params=pltpu.CompilerParams(has_side_effects=SPLIT_COPY_EFFECT),
    )(*[_in_hbm(h) for h in halves], *[_in_hbm(b) for b in landing])
    return out[:3 * n], out[3 * n:6 * n], out[6 * n:7 * n], out[7 * n:8 * n], out[-1]


def scatter_wait(halves, got, send_sems, recv_sems, after, group):
    ws = [BIG[i] for i in GROUPS[group]]
    n = len(ws)

    def body(*refs):
        sems = refs[2 * n:8 * n]
        for cp in _scatter_copies(refs[:n], refs[n:2 * n], ws, _SemList(sems[:3 * n]), _SemList(sems[3 * n:]), start=False):
            cp.wait_send()
            cp.wait_recv()

    out = pl.pallas_call(
        body, name=f"scatter_wait_{group}", in_specs=[HBM] * (2 * n) + [SEM] * (6 * n) + [ANY] * len(after),
        out_specs=[HBM] * (2 * n),
        out_shape=[pltpu.HBM(_half_shape(r, cc, ax), GRAD_COMM_DTYPE) for _, r, cc, ax in ws]
        + [pltpu.HBM((3,) + _piece_shape(r, cc, ax), GRAD_COMM_DTYPE) for _, r, cc, ax in ws],
        input_output_aliases={i: i for i in range(2 * n)},
        compiler_params=pltpu.CompilerParams(has_side_effects=SPLIT_COPY_EFFECT),
    )(*halves, *got, *send_sems, *recv_sems, *after)
    return out[:n], out[n:]


def exchange_reduced(pieces, group):
    ws = [BIG[i] for i in GROUPS[group]]
    n = len(ws)

    def body(*refs):
        ins, theirs = refs[:n], refs[n:2 * n]
        send_sems, recv_sems = refs[2 * n:]
        x, y, c = _place()
        copies = [_start_remote(ins[w], theirs[w], send_sems.at[w], recv_sems.at[w], (x, y, 1 - c)) for w in range(n)]
        for cp in copies:
            cp.wait()

    return pl.pallas_call(
        body, name=f"exchange_reduced_{group}", in_specs=[ANY] * n, out_specs=[ANY] * n,
        out_shape=[jax.ShapeDtypeStruct(_piece_shape(r, cc, ax), F32) for _, r, cc, ax in ws],
        scratch_shapes=[pltpu.SemaphoreType.DMA((n,)), pltpu.SemaphoreType.DMA((n,))],
    )(*pieces)


N_DEV = 8
SMALL = ("ffn1_norm", "mix_norm", "hg_lower_bounds", "hg_out_norm", "ffn2_norm", "final_norm")
SMALL_STAGE_ROWS = 8


def small_step(loss, grads, w, m, v, behind):
    n = len(SMALL)
    shapes = [g.shape for g in grads]
    first_row = [sum(s[0] for s in shapes[:i]) for i in range(n + 1)]
    assert first_row[n] < SMALL_STAGE_ROWS
    loss_row = (pl.ds(first_row[n], 1), pl.ds(0, loss.shape[1]))

    def body(*refs):
        loss_ref, g_refs, w_refs, m_refs, v_refs = refs[0], refs[1:1 + n], refs[1 + n:1 + 2 * n], refs[1 + 2 * n:1 + 3 * n], refs[1 + 3 * n:1 + 4 * n]
        outs = refs[2 + 4 * n:3 + 8 * n]
        loss_out, dg_refs, d_refs, nm_refs, nv_refs = outs[0], outs[1:1 + n], outs[1 + n:1 + 2 * n], outs[1 + 2 * n:1 + 3 * n], outs[1 + 3 * n:]
        stage, gathered, send_sems, recv_sems = refs[3 + 8 * n:]
        x, y, c = _place()
        me = 4 * x + 2 * y + c

        def slot(i, shape):
            return pl.ds(first_row[i], shape[0]), pl.ds(0, shape[1])

        stage[...] = jnp.zeros_like(stage)
        for i, g_ref in enumerate(g_refs):
            stage[slot(i, shapes[i])] = g_ref[...]
        stage[loss_row] = loss_ref[pl.ds(0, 1), :]
        gathered[me] = stage[...]
        copies = []
        for k in range(1, N_DEV):
            peer = (x ^ (k >> 2), y ^ ((k >> 1) & 1), c ^ (k & 1))
            cp = pltpu.make_async_remote_copy(
                src_ref=stage, dst_ref=gathered.at[me], send_sem=send_sems.at[k - 1], recv_sem=recv_sems.at[k - 1],
                device_id=peer, device_id_type=MESH)
            cp.start()
            copies.append(cp)
        for cp in copies:
            cp.wait()
        acc = gathered[0]
        for k in range(1, N_DEV):
            acc = acc + gathered[k]
        stage[...] = acc
        loss_out[...] = jnp.broadcast_to(stage[loss_row], loss_out.shape)
        for i in range(n):
            g = stage[slot(i, shapes[i])]
            dg_refs[i][...] = g
            d_refs[i][...], nm_refs[i][...], nv_refs[i][...] = _adamw_math(w_refs[i][...], g, m_refs[i][...], v_refs[i][...])

    vm = pl.BlockSpec(memory_space=pltpu.VMEM)
    per_param = [jax.ShapeDtypeStruct(s, F32) for s in shapes]
    out = pl.pallas_call(
        body, name="small_step", in_specs=[vm] * (1 + 4 * n) + [ANY], out_specs=[vm] * (1 + 4 * n),
        out_shape=[jax.ShapeDtypeStruct(loss.shape, F32)] + per_param * 4,
        scratch_shapes=[pltpu.VMEM((SMALL_STAGE_ROWS, D_MODEL), F32),
                        pltpu.VMEM((N_DEV, SMALL_STAGE_ROWS, D_MODEL), F32),
                        pltpu.SemaphoreType.DMA((N_DEV - 1,)), pltpu.SemaphoreType.DMA((N_DEV - 1,))],
    )(loss, *grads, *w, *m, *v, behind)
    return out[0], out[1:1 + n], out[1 + n:1 + 2 * n], out[1 + 2 * n:1 + 3 * n], out[1 + 3 * n:]


def _swiglu_block_fwd(h, norm_g, w_gu, w_down, tag, behind=()):
    n = rmsnorm_fwd(h, norm_g, f"{tag}_norm", behind=behind)
    gu = matmul(n, w_gu, out_dtype=ACT_DTYPE, name=f"{tag}_gate_up")
    s = swiglu_fwd(gu, f"{tag}_swiglu")
    h_out = matmul(s, w_down, res=h, scale=0.5, name=f"{tag}_down")
    return h_out, (n, gu, s)


def _swiglu_block_bwd(h, norm_g, w_gu, w_down, saved, dh_out, df, tag, behind=()):
    n, gu, s = saved
    d_down = matmul(s, df, ta=True, scale=0.5, out_dtype=GRAD_COMM_DTYPE, name=f"{tag}_d_w_down")
    ds = matmul(df, w_down, tb=True, scale=0.5, out_dtype=ACT_DTYPE, behind=behind, name=f"{tag}_d_s")
    dgu = swiglu_bwd(gu, ds, f"{tag}_swiglu_bwd")
    d_gu = matmul(n, dgu, ta=True, out_dtype=GRAD_COMM_DTYPE, name=f"{tag}_d_w_gate_up")
    dn = matmul(dgu, w_gu, tb=True, name=f"{tag}_d_n")
    dh, dh_m, dg = rmsnorm_bwd(h, norm_g, dn, dh_out, f"{tag}_norm_bwd")
    return dh, dh_m, dg, d_gu, d_down


def local_step(x, target, small, exchange):
    big = {}
    token, big_ffn1 = exchange.weights("ffn1", x)
    big.update(big_ffn1)
    h1, saved1 = _swiglu_block_fwd(x, small["ffn1_norm"], big["ffn1_w_gate_up"], big["ffn1_w_down"], "ffn1", token)
    token, big_mix = exchange.weights("mix", h1)
    big.update(big_mix)
    u = rmsnorm_fwd(h1, small["mix_norm"], "mix_norm", behind=token)
    z = matmul(u, big["w_in"], name="w_in")
    p = small["hg_lower_bounds"]
    lb = 1.0 / (1.0 + jnp.exp(p[1:2] - p[0:1]))
    y_hg, o_raw, states = hgrn_fwd(z, lb, small["hg_out_norm"], "hgrn_fwd")
    o_att, l_att = zip(*[att_fwd(z, g, f"att_fwd_{g}") for g in range(N_GROUPS)])
    y_att = att_combine_fwd(o_att, l_att, "att_combine")
    bh = matmul(y_hg, big["w_branch_hg"], name="branch_hg")
    ba = matmul(y_att, big["w_branch_att"], name="branch_att")
    merged = merge_fwd(z, bh, ba, "merge")
    h2 = matmul(merged, big["w_out"], res=h1, name="w_out")
    token, big_ffn2 = exchange.weights("ffn2", h2)
    big.update(big_ffn2)
    h3, saved2 = _swiglu_block_fwd(h2, small["ffn2_norm"], big["ffn2_w_gate_up"], big["ffn2_w_down"], "ffn2", token)
    dh3, dh3_m, d_final, loss = final_norm_loss(h3, small["final_norm"], target, "final_norm_loss")

    gs, gb = {"final_norm": d_final}, {}
    dh2, dh2_m, gs["ffn2_norm"], gb["ffn2_w_gate_up"], gb["ffn2_w_down"] = _swiglu_block_bwd(
        h2, small["ffn2_norm"], big["ffn2_w_gate_up"], big["ffn2_w_down"], saved2, dh3, dh3_m, "ffn2")
    token = exchange.gradients("ffn2", gb, dh2)
    gb["w_out"] = matmul(merged, dh2_m, ta=True, out_dtype=GRAD_COMM_DTYPE, name="d_w_out")
    dmerged = matmul(dh2_m, big["w_out"], tb=True, behind=token, name="d_merged")
    dbh, dba, dgh, dga = merge_bwd(z, bh, ba, dmerged, "merge_bwd")
    gb["w_branch_hg"] = matmul(y_hg, dbh, ta=True, out_dtype=GRAD_COMM_DTYPE, name="d_w_branch_hg")
    gb["w_branch_att"] = matmul(y_att, dba, ta=True, out_dtype=GRAD_COMM_DTYPE, name="d_w_branch_att")
    dy_hg = matmul(dbh, big["w_branch_hg"], tb=True, name="d_y_hg")
    dy_att = matmul(dba, big["w_branch_att"], tb=True, name="d_y_att")
    dq, dfp, di, dog, d_lb, gs["hg_out_norm"] = hgrn_bwd(z, lb, small["hg_out_norm"], o_raw, states, dy_hg, "hgrn_bwd")
    do_att, corr = att_combine_bwd(o_att, l_att, dy_att, "att_combine_bwd")
    d_att = [part for g in range(N_GROUPS) for part in att_bwd(z, l_att[g], do_att[g], corr[g], g, f"att_bwd_{g}")]
    dz = jnp.concatenate([dq, dfp, di, dog, *d_att, dgh, dga], axis=1)
    gb["w_in"] = matmul(u, dz, ta=True, out_dtype=GRAD_COMM_DTYPE, name="d_w_in")
    du = matmul(dz, big["w_in"], tb=True, name="d_u")
    dh1, dh1_m, gs["mix_norm"] = rmsnorm_bwd(h1, small["mix_norm"], du, dh2, "mix_norm_bwd")
    token = exchange.gradients("mix", gb, dh1)
    dp0 = d_lb * lb * (1.0 - lb)
    gs["hg_lower_bounds"] = jnp.concatenate([dp0, -dp0], axis=0)
    dx, _, gs["ffn1_norm"], gb["ffn1_w_gate_up"], gb["ffn1_w_down"] = _swiglu_block_bwd(
        x, small["ffn1_norm"], big["ffn1_w_gate_up"], big["ffn1_w_down"], saved1, dh1, dh1_m, "ffn1", token)
    exchange.gradients("ffn1", gb, dx)
    return loss, dx, gs


WEIGHTS = ("ffn1_norm", "ffn1_w_gate_up", "ffn1_w_down", "mix_norm", "w_in", "hg_lower_bounds", "hg_out_norm",
           "w_branch_hg", "w_branch_att", "w_out", "ffn2_norm", "ffn2_w_gate_up", "ffn2_w_down", "final_norm")


class WeightExchange:
    ORDER = ("ffn1", "mix", "ffn2")

    def __init__(self, shards, core, chip):
        self.core, self.chip = core, chip
        self.scattering = None
        self.reduced = {}
        first = self.ORDER[0]
        self.placed = {BIG[i][0]: place_own_block(shards[BIG[i][0]], chip, *BIG[i][1:], f"place_{BIG[i][0]}")
                       for i in GROUPS[first]}
        self._start_gather(first, self.placed[self._names(first)[0]])
        chip_behind = chip + self.token[0, :1].astype(jnp.int32)
        for group in self.ORDER[1:]:
            for i in GROUPS[group]:
                n, r, cc, ax = BIG[i]
                self.placed[n] = place_own_block(shards[n], chip_behind, r, cc, ax, f"place_{n}")
        self.placed_behind = [self.placed[n] for group in self.ORDER[1:] for n in self._names(group)]

    def _names(self, group):
        return [BIG[i][0] for i in GROUPS[group]]

    def _start_gather(self, group, after):
        send_sems, recv_sems, bufs, self.token = gather_start([self.placed[n] for n in self._names(group)], after, group)
        self.gathering = (group, send_sems, recv_sems, bufs)

    def weights(self, group, h):
        pending, send_sems, recv_sems, bufs = self.gathering
        assert pending == group
        after = self.placed_behind if group == self.ORDER[0] else [h]
        whole = gather_forward(gather_wait(bufs, send_sems, recv_sems, after, group), group)
        later = self.ORDER.index(group) + 1
        behind = []
        if later < len(self.ORDER):
            self._start_gather(self.ORDER[later], whole[0])
            behind = [self.token]
        return behind, dict(zip(self._names(group), whole))

    def _finish_scatter(self, after):
        group, send_sems, recv_sems, halves, got = self.scattering
        halves, got = scatter_wait(halves, got, send_sems, recv_sems, after, group)
        ws = [BIG[i] for i in GROUPS[group]]
        mine = [add_pieces(h, g, self.chip, r, cc, ax, f"add_pieces_{n}") for (n, r, cc, ax), h, g in zip(ws, halves, got)]
        theirs = exchange_reduced(mine, group)
        self.reduced.update({n: (a, b) for (n, *_), a, b in zip(ws, mine, theirs)})
        self.scattering = None
        return theirs[0]

    def gradients(self, group, grads, dh):
        behind = [self._finish_scatter([dh])] if self.scattering is not None else []
        ws = [BIG[i] for i in GROUPS[group]]
        theirs = exchange_halves([grads[n] for n, *_ in ws], group)
        halves = [add_halves(grads[n], t, self.core, r, cc, ax, f"add_halves_{n}") for (n, r, cc, ax), t in zip(ws, theirs)]
        send_sems, recv_sems, halves, got, self.token = scatter_start(halves, group)
        self.scattering = (group, send_sems, recv_sems, halves, got)
        return behind + [self.token]

    def finish(self, after):
        self._finish_scatter(after)
        return self.reduced


def kernel(x, ffn1_norm, ffn1_w_gate_up, ffn1_w_down, mix_norm, w_in, hg_lower_bounds, hg_out_norm, w_branch_hg, w_branch_att, w_out, ffn2_norm, ffn2_w_gate_up, ffn2_w_down, final_norm, loss_target, m_ffn1_norm, m_ffn1_w_gate_up, m_ffn1_w_down, m_mix_norm, m_w_in, m_hg_lower_bounds, m_hg_out_norm, m_w_branch_hg, m_w_branch_att, m_w_out, m_ffn2_norm, m_ffn2_w_gate_up, m_ffn2_w_down, m_final_norm, v_ffn1_norm, v_ffn1_w_gate_up, v_ffn1_w_down, v_mix_norm, v_w_in, v_hg_lower_bounds, v_hg_out_norm, v_w_branch_hg, v_w_branch_att, v_w_out, v_ffn2_norm, v_ffn2_w_gate_up, v_ffn2_w_down, v_final_norm):
    w = dict(ffn1_norm=ffn1_norm, ffn1_w_gate_up=ffn1_w_gate_up, ffn1_w_down=ffn1_w_down, mix_norm=mix_norm, w_in=w_in,
             hg_lower_bounds=hg_lower_bounds, hg_out_norm=hg_out_norm, w_branch_hg=w_branch_hg, w_branch_att=w_branch_att,
             w_out=w_out, ffn2_norm=ffn2_norm, ffn2_w_gate_up=ffn2_w_gate_up, ffn2_w_down=ffn2_w_down, final_norm=final_norm)
    m = dict(ffn1_norm=m_ffn1_norm, ffn1_w_gate_up=m_ffn1_w_gate_up, ffn1_w_down=m_ffn1_w_down, mix_norm=m_mix_norm,
             w_in=m_w_in, hg_lower_bounds=m_hg_lower_bounds, hg_out_norm=m_hg_out_norm, w_branch_hg=m_w_branch_hg,
             w_branch_att=m_w_branch_att, w_out=m_w_out, ffn2_norm=m_ffn2_norm, ffn2_w_gate_up=m_ffn2_w_gate_up,
             ffn2_w_down=m_ffn2_w_down, final_norm=m_final_norm)
    v = dict(ffn1_norm=v_ffn1_norm, ffn1_w_gate_up=v_ffn1_w_gate_up, ffn1_w_down=v_ffn1_w_down, mix_norm=v_mix_norm,
             w_in=v_w_in, hg_lower_bounds=v_hg_lower_bounds, hg_out_norm=v_hg_out_norm, w_branch_hg=v_w_branch_hg,
             w_branch_att=v_w_branch_att, w_out=v_w_out, ffn2_norm=v_ffn2_norm, ffn2_w_gate_up=v_ffn2_w_gate_up,
             ffn2_w_down=v_ffn2_w_down, final_norm=v_final_norm)

    core = lax.axis_index("c").astype(jnp.int32).reshape(1)
    chip = (2 * lax.axis_index("x") + lax.axis_index("y")).astype(jnp.int32).reshape(1)
    exchange = WeightExchange({n: w[n][0] for n, *_ in BIG}, core, chip)
    small = {n: w[n] for n in SMALL}
    small["final_norm"] = final_norm.reshape(1, D_MODEL)

    loss, dx, gs = local_step(x[0], loss_target[0], small, exchange)

    grads, delta, new_m, new_v = {}, {}, {}, {}

    def update(group, core):
        for i in GROUPS[group]:
            n, r, cc, ax = BIG[i]
            a, b = exchange.reduced[n]
            g, d, nm, nv = adamw_halves(w[n][0], a, b, m[n][0], v[n][0], core, r, cc, ax, f"adamw_{n}")
            grads[n], delta[n], new_m[n], new_v[n] = g[None], d[None], nm[None], nv[None]

    core_behind = core + exchange.token[0, :1].astype(jnp.int32)
    update("ffn2", core_behind)
    update("mix", core_behind)
    exchange.finish(after=[delta[BIG[i][0]] for group in ("ffn2", "mix") for i in GROUPS[group]])
    update("ffn1", core)
    two_d = lambda a: a.reshape(1, D_MODEL) if a.ndim == 1 else a
    loss_sum, *small_out = small_step(loss, [gs[n] for n in SMALL], *[[two_d(p[n]) for n in SMALL] for p in (w, m, v)],
                                      behind=delta["ffn1_w_down"])
    for result, parts in zip((grads, delta, new_m, new_v), small_out):
        result.update({n: a.reshape(w[n].shape) for n, a in zip(SMALL, parts)})

    return (loss_sum[0, 0], dx[None], *[grads[n] for n in WEIGHTS], *[delta[n] for n in WEIGHTS],
            *[new_m[n] for n in WEIGHTS], *[new_v[n] for n in WEIGHTS])
```

```python
import numpy as np
import jax
import jax.numpy as jnp
from jax import lax
from jax.experimental import pallas as pl
from jax.experimental.pallas import tpu as pltpu

SEQ = 2048
D_MODEL = 1024
D_FF = 2816
HG_HEADS = 4
HG_DIM = 128
HG_WIDTH = 512
HG_CHUNK = 64
ATT_GROUPS = ((128, 1), (512, 4), (2048, 16))
ATT_HEADS = 8
ATT_WIDTH = 512
ATT_BLOCK = 128
ALIBI_MAX = 8.0
IN_COLS = 8704
EPS = 1e-6
NEG_INF = -1e30
ADAM_LR = 0.001
ADAM_B1 = 0.9
ADAM_B2 = 0.999
ADAM_EPS = 1e-08
ADAM_WD = 0.01
ADAM_STEP = 10

N_CHIPS = 4
MXU_DTYPE = jnp.bfloat16
WEIGHT_COMM_DTYPE = jnp.bfloat16
GRAD_COMM_DTYPE = jnp.bfloat16
ACT_DTYPE = jnp.bfloat16
MESH = pl.DeviceIdType.MESH
F32 = jnp.float32
HIGHEST = lax.Precision.HIGHEST


def _sigmoid(x):
    return 1.0 / (1.0 + jnp.exp(-x))


def _dot(a, b, ta=False, tb=False):
    dn = (((0 if ta else 1,), (1 if tb else 0,)), ((), ()))
    return lax.dot_general(a.astype(MXU_DTYPE), b.astype(MXU_DTYPE), dn, preferred_element_type=F32)


def _dot_f32(a, b, ones_on_right=False):
    x = a if ones_on_right else b
    hi = x.astype(jnp.bfloat16)
    rest = x - hi.astype(F32)
    mid = rest.astype(jnp.bfloat16)
    lo = (rest - mid.astype(F32)).astype(jnp.bfloat16)
    if ones_on_right:
        dot = lambda q: jnp.dot(q, b.astype(jnp.bfloat16), preferred_element_type=F32)
    else:
        dot = lambda q: jnp.dot(a.astype(jnp.bfloat16), q, preferred_element_type=F32)
    return dot(hi) + (dot(mid) + dot(lo))


def _split_bf16(x):
    hi = x.astype(jnp.bfloat16)
    return hi, (x - hi.astype(F32)).astype(jnp.bfloat16)


def _hdot(a, b, ta=False, tb=False):
    dn =(((0 if ta else 1,), (1 if tb else 0,)), ((), ()))
    (a_hi, a_lo), (b_hi, b_lo) = _split_bf16(a), _split_bf16(b)
    dot = lambda p, q: lax.dot_general(p, q, dn, preferred_element_type=F32)
    return dot(a_hi, b_hi) + (dot(a_lo, b_hi) + dot(a_hi, b_lo))


MATMUL_VMEM_BYTES = 48 * 1024 * 1024
MATMUL_TILE_BYTES = 36 * 1024 * 1024
MXU_ALIGN = 128


def _divisors(n, most):
    return [t for t in range(min(n, most), 0, -MXU_ALIGN) if n % t == 0 and t % MXU_ALIGN == 0]


def _matmul_tiles(M, N, K, in_bytes, out_bytes, has_res):
    best = None
    for tk in _divisors(K, K):
        nk = K // tk
        for tm in _divisors(M, 2048):
            for tn in _divisors(N, 512):
                tiles = 2 * in_bytes * (tm * tk + tk * tn) + 2 * out_bytes * tm * tn
                tiles += 4 * tm * tn * ((nk > 1) + 2 * has_res)
                if tiles > MATMUL_TILE_BYTES:
                    continue
                traffic = in_bytes * (M * K * (1 if nk == 1 else N // tn) + K * N * (M // tm))
                key = (traffic, -tm * tn * tk)
                if best is None or key < best[0]:
                    best = (key, (tm, tn, tk))
    return best[1]


def matmul(a, b, *, ta=False, tb=False, out_dtype=F32, res=None, scale=1.0, behind=(), name):
    if ta:
        K, M = a.shape
    else:
        M, K = a.shape
    if tb:
        N, K2 = b.shape
    else:
        K2, N = b.shape
    assert K == K2 and a.dtype == b.dtype
    tm, tn, tk = _matmul_tiles(M, N, K, a.dtype.itemsize, jnp.dtype(out_dtype).itemsize, res is not None)
    nk = K // tk

    def finish(r, r_ref, o_ref):
        if scale != 1.0:
            r = r * scale
        if res is not None:
            r = r_ref[...] + r
        o_ref[...] = r.astype(out_dtype)

    def body(*refs):
        a_ref, b_ref = refs[:2]
        r_ref = refs[2] if res is not None else None
        o_ref = refs[2 + (res is not None) + len(behind)]
        if nk == 1:
            finish(_dot(a_ref[...], b_ref[...], ta, tb), r_ref, o_ref)
            return
        acc = refs[-1]
        k = pl.program_id(2)

        @pl.when(k == 0)
        def _():
            acc[...] = jnp.zeros_like(acc)

        acc[...] += _dot(a_ref[...], b_ref[...], ta, tb)

        @pl.when(k == nk - 1)
        def _():
            finish(acc[...], r_ref, o_ref)

    a_spec = pl.BlockSpec((tk, tm), lambda i, j, k: (k, i)) if ta else pl.BlockSpec((tm, tk), lambda i, j, k: (i, k))
    b_spec = pl.BlockSpec((tn, tk), lambda i, j, k: (j, k)) if tb else pl.BlockSpec((tk, tn), lambda i, j, k: (k, j))
    in_specs = [a_spec, b_spec]
    args = [a, b]
    if res is not None:
        in_specs.append(pl.BlockSpec((tm, tn), lambda i, j, k: (i, j)))
        args.append(res)
    for earlier in behind:
        in_specs.append(pl.BlockSpec(memory_space=pl.ANY))
        args.append(earlier)
    return pl.pallas_call(
        body, name=name, grid=(M // tm, N // tn, nk), in_specs=in_specs,
        out_specs=pl.BlockSpec((tm, tn), lambda i, j, k: (i, j)),
        out_shape=jax.ShapeDtypeStruct((M, N), out_dtype),
        scratch_shapes=[pltpu.VMEM((tm, tn), F32)] if nk > 1 else [],
        compiler_params=pltpu.CompilerParams(dimension_semantics=("parallel", "parallel", "arbitrary"),
                                             vmem_limit_bytes=MATMUL_VMEM_BYTES),
    )(*args)


ROW_TILE = 256


def rmsnorm_fwd(x, g, name, behind=()):
    def body(x_ref, g_ref, *refs):
        n_ref = refs[-1]
        xv = x_ref[...]
        r = lax.rsqrt(jnp.mean(xv * xv, axis=-1, keepdims=True) + EPS)
        n_ref[...] = ((xv * r) * g_ref[...]).astype(n_ref.dtype)

    order = list(behind)
    return pl.pallas_call(
        body, name=name, grid=(SEQ // ROW_TILE,),
        in_specs=[pl.BlockSpec((ROW_TILE, D_MODEL), lambda i: (i, 0)), pl.BlockSpec((1, D_MODEL), lambda i: (0, 0))]
        + [pl.BlockSpec(memory_space=pl.ANY)] * len(order),
        out_specs=pl.BlockSpec((ROW_TILE, D_MODEL), lambda i: (i, 0)),
        out_shape=jax.ShapeDtypeStruct((SEQ, D_MODEL), MXU_DTYPE),
    )(x, g, *order)


def rmsnorm_bwd(x, g, dn, dres, name):
    def body(x_ref, g_ref, dn_ref, dr_ref, dx_ref, dxm_ref, dg_ref):
        xv = x_ref[...]
        r = lax.rsqrt(jnp.mean(xv * xv, axis=-1, keepdims=True) + EPS)
        xh = xv * r
        dnv = dn_ref[...]

        @pl.when(pl.program_id(0) == 0)
        def _():
            dg_ref[...] = jnp.zeros_like(dg_ref)

        dg_ref[...] += jnp.sum(dnv * xh, axis=0, keepdims=True)
        dxh = dnv * g_ref[...]
        dx = dr_ref[...] + r * (dxh - xh * jnp.mean(dxh * xh, axis=-1, keepdims=True))
        dx_ref[...] = dx
        dxm_ref[...] = dx.astype(dxm_ref.dtype)

    row = pl.BlockSpec((ROW_TILE, D_MODEL), lambda i: (i, 0))
    vec = pl.BlockSpec((1, D_MODEL), lambda i: (0, 0))
    return pl.pallas_call(
        body, name=name, grid=(SEQ // ROW_TILE,), in_specs=[row, vec, row, row], out_specs=[row, row, vec],
        out_shape=[jax.ShapeDtypeStruct((SEQ, D_MODEL), F32), jax.ShapeDtypeStruct((SEQ, D_MODEL), MXU_DTYPE),
                   jax.ShapeDtypeStruct((1, D_MODEL), F32)],
        compiler_params=pltpu.CompilerParams(dimension_semantics=("arbitrary",)),
    )(x, g, dn, dres)


def final_norm_loss(h, g, target, name):
    def body(h_ref, g_ref, t_ref, dh_ref, dhm_ref, dg_ref, loss_ref):
        xv = h_ref[...]
        r = lax.rsqrt(jnp.mean(xv * xv, axis=-1, keepdims=True) + EPS)
        xh = xv * r
        gv = g_ref[...]
        e = xh * gv - t_ref[...]

        @pl.when(pl.program_id(0) == 0)
        def _():
            dg_ref[...] = jnp.zeros_like(dg_ref)
            loss_ref[...] = jnp.zeros_like(loss_ref)

        part = 0.5 * jnp.sum(jnp.sum(e * e, axis=-1, keepdims=True) * (1.0 / D_MODEL), axis=0, keepdims=True)
        loss_ref[...] += jnp.broadcast_to(part, loss_ref.shape)
        dout = e * (1.0 / D_MODEL)
        dg_ref[...] += jnp.sum(dout * xh, axis=0, keepdims=True)
        dxh = dout * gv
        dh = r * (dxh - xh * jnp.mean(dxh * xh, axis=-1, keepdims=True))
        dh_ref[...] = dh
        dhm_ref[...] = dh.astype(dhm_ref.dtype)

    row = pl.BlockSpec((ROW_TILE, D_MODEL), lambda i: (i, 0))
    vec = pl.BlockSpec((1, D_MODEL), lambda i: (0, 0))
    return pl.pallas_call(
        body, name=name, grid=(SEQ // ROW_TILE,), in_specs=[row, vec, row],
        out_specs=[row, row, vec, pl.BlockSpec((8, 128), lambda i: (0, 0))],
        out_shape=[jax.ShapeDtypeStruct((SEQ, D_MODEL), F32), jax.ShapeDtypeStruct((SEQ, D_MODEL), MXU_DTYPE),
                   jax.ShapeDtypeStruct((1, D_MODEL), F32), jax.ShapeDtypeStruct((8, 128), F32)],
        compiler_params=pltpu.CompilerParams(dimension_semantics=("arbitrary",)),
    )(h, g, target)


FF_TILE = D_FF // 2


def swiglu_fwd(gu, name):
    def body(a_ref, b_ref, s_ref):
        a = a_ref[...].astype(F32)
        s_ref[...] = (a * _sigmoid(a) * b_ref[...].astype(F32)).astype(s_ref.dtype)

    return pl.pallas_call(
        body, name=name, grid=(SEQ // ROW_TILE, 2),
        in_specs=[pl.BlockSpec((ROW_TILE, FF_TILE), lambda i, j: (i, j)),
                  pl.BlockSpec((ROW_TILE, FF_TILE), lambda i, j: (i, j + 2))],
        out_specs=pl.BlockSpec((ROW_TILE, FF_TILE), lambda i, j: (i, j)),
        out_shape=jax.ShapeDtypeStruct((SEQ, D_FF), MXU_DTYPE),
    )(gu, gu)


def swiglu_bwd(gu, ds, name):
    rows = ROW_TILE // 2

    def body(a_ref, b_ref, ds_ref, o_ref):
        a = a_ref[...].astype(F32)
        sg = _sigmoid(a)
        dsv = ds_ref[...].astype(F32)
        o_ref[:, :D_FF] = (dsv * b_ref[...].astype(F32) * (sg * (1.0 + a * (1.0 - sg)))).astype(o_ref.dtype)
        o_ref[:, D_FF:] = (dsv * a * sg).astype(o_ref.dtype)

    return pl.pallas_call(
        body, name=name, grid=(SEQ // rows,),
        in_specs=[pl.BlockSpec((rows, D_FF), lambda i: (i, 0)), pl.BlockSpec((rows, D_FF), lambda i: (i, 1)),
                  pl.BlockSpec((rows, D_FF), lambda i: (i, 0))],
        out_specs=pl.BlockSpec((rows, 2 * D_FF), lambda i: (i, 0)),
        out_shape=jax.ShapeDtypeStruct((SEQ, 2 * D_FF), MXU_DTYPE), compiler_params=SUM_PARAMS,
    )(gu, gu, ds)


GATE_HG_BLK = 6656 // 512
GATE_ATT_BLK = 7680 // 512


def merge_fwd(z, bh, ba, name):
    def body(gh_ref, ga_ref, bh_ref, ba_ref, o_ref):
        o_ref[...] = (_sigmoid(gh_ref[...]) * bh_ref[...] + _sigmoid(ga_ref[...]) * ba_ref[...]).astype(o_ref.dtype)

    blk = pl.BlockSpec((ROW_TILE, 512), lambda i, j: (i, j))
    return pl.pallas_call(
        body, name=name, grid=(SEQ // ROW_TILE, 2),
        in_specs=[pl.BlockSpec((ROW_TILE, 512), lambda i, j: (i, GATE_HG_BLK + j)),
                  pl.BlockSpec((ROW_TILE, 512), lambda i, j: (i, GATE_ATT_BLK + j)), blk, blk],
        out_specs=blk, out_shape=jax.ShapeDtypeStruct((SEQ, D_MODEL), MXU_DTYPE),
    )(z, z, bh, ba)


def merge_bwd(z, bh, ba, dm, name):
    def body(gh_ref, ga_ref, bh_ref, ba_ref, dm_ref, dbh_ref, dba_ref, dgh_ref, dga_ref):
        dmv = dm_ref[...]
        sh = _sigmoid(gh_ref[...])
        sa = _sigmoid(ga_ref[...])
        dbh_ref[...] = (dmv * sh).astype(dbh_ref.dtype)
        dba_ref[...] = (dmv * sa).astype(dba_ref.dtype)
        dgh_ref[...] = (dmv * bh_ref[...] * (sh * (1.0 - sh))).astype(dgh_ref.dtype)
        dga_ref[...] = (dmv * ba_ref[...] * (sa * (1.0 - sa))).astype(dga_ref.dtype)

    blk = pl.BlockSpec((ROW_TILE, 512), lambda i, j: (i, j))
    out = jax.ShapeDtypeStruct((SEQ, D_MODEL), MXU_DTYPE)
    return pl.pallas_call(
        body, name=name, grid=(SEQ // ROW_TILE, 2),
        in_specs=[pl.BlockSpec((ROW_TILE, 512), lambda i, j: (i, GATE_HG_BLK + j)),
                  pl.BlockSpec((ROW_TILE, 512), lambda i, j: (i, GATE_ATT_BLK + j)), blk, blk, blk],
        out_specs=[blk, blk, blk, blk], out_shape=[out, out, out, out],
    )(z, z, bh, ba, dm)


N_CHUNKS = SEQ // HG_CHUNK
HG_STEP_CHUNKS = 4


def _hgrn_gates(q, fp, lb):
    C = HG_CHUNK
    sg = _sigmoid(fp)
    f = lb + (1.0 - lb) * sg
    lf = jnp.log(f)
    row = lax.broadcasted_iota(jnp.int32, (C, C), 0)
    col = lax.broadcasted_iota(jnp.int32, (C, C), 1)
    causal = row >= col
    G = _dot_f32(causal.astype(F32), lf)
    eG = jnp.exp(G)
    enG = jnp.exp(-G)
    qg = q * eG
    kg = (1.0 - f) * enG
    A = jnp.where(causal, _hdot(qg, kg, tb=True), 0.0)
    egl = jnp.exp(jnp.sum(lf, axis=0, keepdims=True))
    return sg, f, causal, eG, enG, qg, kg, A, egl


def hgrn_fwd(z, lb, gain, name):
    C, K = HG_CHUNK, HG_DIM

    def body(q_ref, f_ref, v_ref, og_ref, p_ref, g_ref, y_ref, o_ref, st_ref, state):
        @pl.when(pl.program_id(0) == 0)
        def _():
            state[...] = jnp.zeros_like(state)

        for cc in range(HG_STEP_CHUNKS):
            rows = pl.ds(cc * C, C)
            for h in range(HG_HEADS):
                hd = pl.ds(h * K, K)
                v = v_ref[rows, hd]
                _, _, _, _, _, qg, kg, A, egl = _hgrn_gates(q_ref[rows, hd], f_ref[rows, hd], p_ref[:, hd])
                st = state[h]
                st_ref[h, cc] = st
                o = _hdot(A, v) + _hdot(qg, st, tb=True)
                state[h] = st * egl + _hdot(v, kg * egl, ta=True)
                o_ref[rows, hd] = o
                rs = lax.rsqrt(jnp.mean(o * o, axis=-1, keepdims=True) + EPS)
                og = og_ref[rows, hd]
                y_ref[rows, hd] = (((o * rs) * g_ref[:, hd]) * (og * _sigmoid(og))).astype(y_ref.dtype)

    R = HG_STEP_CHUNKS * C

    def zcol(section):
        return pl.BlockSpec((R, HG_WIDTH), lambda c: (c, section))

    vec = pl.BlockSpec((1, HG_WIDTH), lambda c: (0, 0))
    blk = pl.BlockSpec((R, HG_WIDTH), lambda c: (c, 0))
    return pl.pallas_call(
        body, name=name, grid=(N_CHUNKS // HG_STEP_CHUNKS,),
        in_specs=[zcol(0), zcol(1), zcol(2), zcol(3), vec, vec],
        out_specs=[blk, blk, pl.BlockSpec((HG_HEADS, HG_STEP_CHUNKS, K, K), lambda c: (0, c, 0, 0))],
        out_shape=[jax.ShapeDtypeStruct((SEQ, HG_WIDTH), MXU_DTYPE), jax.ShapeDtypeStruct((SEQ, HG_WIDTH), F32),
                   jax.ShapeDtypeStruct((HG_HEADS, N_CHUNKS, K, K), F32)],
        scratch_shapes=[pltpu.VMEM((HG_HEADS, K, K), F32)],
        compiler_params=pltpu.CompilerParams(dimension_semantics=("arbitrary",)),
    )(z, z, z, z, lb, gain)


def hgrn_bwd(z, lb, gain, o_raw, states, dy, name):
    C, K = HG_CHUNK, HG_DIM

    def body(q_ref, f_ref, v_ref, og_ref, p_ref, g_ref, o_ref, st_ref, dy_ref,
             dq_ref, dfp_ref, dv_ref, dog_ref, dlb_ref, dgain_ref, dstate):
        @pl.when(pl.program_id(0) == 0)
        def _():
            dstate[...] = jnp.zeros_like(dstate)
            dlb_ref[...] = jnp.zeros_like(dlb_ref)
            dgain_ref[...] = jnp.zeros_like(dgain_ref)

        last = lax.broadcasted_iota(jnp.int32, (C, K), 0) == C - 1
        row = lax.broadcasted_iota(jnp.int32, (C, C), 0)
        col = lax.broadcasted_iota(jnp.int32, (C, C), 1)
        anti_causal = (col >= row).astype(F32)
        for cc in reversed(range(HG_STEP_CHUNKS)):
            rows = pl.ds(cc * C, C)
            for h in range(HG_HEADS):
                hd = pl.ds(h * K, K)
                v = v_ref[rows, hd]
                lb = p_ref[:, hd]
                sg, f, causal, eG, enG, qg, kg, A, egl = _hgrn_gates(q_ref[rows, hd], f_ref[rows, hd], lb)
                kd = kg * egl
                st = st_ref[h, cc]
                dst = dstate[h]
                o = o_ref[rows, hd]
                og = og_ref[rows, hd]
                gain_v = g_ref[:, hd]
                dyv = dy_ref[rows, hd]
                rs = lax.rsqrt(jnp.mean(o * o, axis=-1, keepdims=True) + EPS)
                on = o * rs
                sgo = _sigmoid(og)
                silu = og * sgo
                dog_ref[rows, hd] = (dyv * (on * gain_v) * (sgo * (1.0 + og * (1.0 - sgo)))).astype(dog_ref.dtype)
                dgain_ref[:, hd] += jnp.sum(dyv * silu * on, axis=0, keepdims=True)
                don = dyv * gain_v * silu
                do = rs * (don - on * jnp.mean(don * on, axis=-1, keepdims=True))
                dA = jnp.where(causal, _hdot(do, v, tb=True), 0.0)
                dv_ref[rows, hd] = (_hdot(A, do, ta=True) + _hdot(kd, dst, tb=True)).astype(dv_ref.dtype)
                dqg = _hdot(dA, kg) + _hdot(do, st)
                dkg = _hdot(dA, qg, ta=True)
                dkd = _hdot(v, dst)
                dstate[h] = dst * egl + _hdot(do, qg, ta=True)
                dgl = jnp.sum(st * dst, axis=0, keepdims=True) * egl
                dq_ref[rows, hd] = (dqg * eG).astype(dq_ref.dtype)
                dk = dkg * enG + dkd * (enG * egl)
                dG = dqg * qg - dkg * kg - dkd * kd
                extra = jnp.sum(dkd * kd, axis=0, keepdims=True) + dgl
                dG = dG + jnp.where(last, extra, 0.0)
                dlf = _dot_f32(anti_causal, dG)
                df = dlf / f - dk
                dfp_ref[rows, hd] = (df * (1.0 - lb) * (sg * (1.0 - sg))).astype(dfp_ref.dtype)
                dlb_ref[:, hd] += jnp.sum(df * (1.0 - sg), axis=0, keepdims=True)

    R = HG_STEP_CHUNKS * C
    n_steps = N_CHUNKS // HG_STEP_CHUNKS

    def rc(c):
        return n_steps - 1 - c

    def zcol(section):
        return pl.BlockSpec((R, HG_WIDTH), lambda c: (rc(c), section))

    vec = pl.BlockSpec((1, HG_WIDTH), lambda c: (0, 0))
    blk = pl.BlockSpec((R, HG_WIDTH), lambda c: (rc(c), 0))
    out = jax.ShapeDtypeStruct((SEQ, HG_WIDTH), MXU_DTYPE)
    small = jax.ShapeDtypeStruct((1, HG_WIDTH), F32)
    return pl.pallas_call(
        body, name=name, grid=(n_steps,),
        in_specs=[zcol(0), zcol(1), zcol(2), zcol(3), vec, vec, blk,
                  pl.BlockSpec((HG_HEADS, HG_STEP_CHUNKS, K, K), lambda c: (0, rc(c), 0, 0)), blk],
        out_specs=[blk, blk, blk, blk, vec, vec],
        out_shape=[out, out, out, out, small, small],
        scratch_shapes=[pltpu.VMEM((HG_HEADS, K, K), F32)],
        compiler_params=pltpu.CompilerParams(dimension_semantics=("arbitrary",)),
    )(z, z, z, z, lb, gain, o_raw, states, dy)


N_GROUPS = len(ATT_GROUPS)
HEAD_PAIRS = ATT_WIDTH // 128
ATT_COL0 = 4 * HG_WIDTH
UNROLLED_UNITS = 4
ATT_SLAB_BLOCKS = 4


def _alibi_coef():
    n = N_GROUPS * ATT_HEADS
    slopes = np.exp2(-ALIBI_MAX * np.arange(1, n + 1, dtype=np.float32) / n).astype(np.float32)
    dil = np.repeat(np.array([d for _, d in ATT_GROUPS], np.float32), ATT_HEADS)
    return jnp.asarray(slopes * dil, F32)


def _for_each_unit(n, fn):
    if n <= UNROLLED_UNITS:
        for u in range(n):
            fn(u)
    else:
        def group(i, carry):
            for j in range(UNROLLED_UNITS):
                fn(i * UNROLLED_UNITS + j)
            return carry
        lax.fori_loop(0, n // UNROLLED_UNITS, group, 0)


def _att_specs(g):
    B = ATT_BLOCK
    d = ATT_GROUPS[g][1]
    blocks = ATT_SLAB_BLOCKS if d == 1 else 1
    R = B * d * blocks
    n_slabs = SEQ // R
    multi = SEQ // d > B
    col0 = (ATT_COL0 + g * 3 * ATT_WIDTH) // 128

    def cur(col):
        return pl.BlockSpec((R, 128), lambda hp, s: (s, col + hp))

    def prev(col):
        return pl.BlockSpec((R, 128), lambda hp, s: (jnp.maximum(s - 1, 0), col + hp))

    def nxt(col):
        return pl.BlockSpec((R, 128), lambda hp, s: (jnp.minimum(s + 1, n_slabs - 1), col + hp))

    def unit(u, s):
        if d > 1:
            rows = pl.ds(u, B, stride=d)
            return rows, False, rows, jnp.where(s == 0, B, 0), False, rows, jnp.where(s == n_slabs - 1, B, 0)
        rows = pl.ds(u * B, B)
        inner_prev, inner_next = u > 0, u < blocks - 1
        return (rows, inner_prev, pl.ds((u - 1) * B if inner_prev else (blocks - 1) * B, B),
                0 if inner_prev else jnp.where(s == 0, B, 0),
                inner_next, pl.ds((u + 1) * B if inner_next else 0, B),
                0 if inner_next else jnp.where(s == n_slabs - 1, B, 0))

    return d * blocks, R, n_slabs, multi, col0, cur, prev, nxt, unit


def _head_lanes(j):
    lane = lax.broadcasted_iota(jnp.int32, (ATT_BLOCK, 128), 1)
    return (lane >= 64 * j) & (lane < 64 * (j + 1))


def _lane_value(x, sel):
    return jnp.max(jnp.where(sel, x, -3e38), axis=-1, keepdims=True)


def _stack_heads(x, sel0):
    return jnp.concatenate([jnp.where(sel0, x, 0.0), jnp.where(sel0, 0.0, x)], axis=0)


def _stack_values(x, sel0, lanes):
    swapped = pltpu.roll(x, 64, 1)
    stacked = jnp.concatenate([jnp.where(sel0, x, swapped), jnp.where(sel0, swapped, x)], axis=0)
    return stacked if lanes == 128 else jnp.concatenate([stacked] * (lanes // 128), axis=1)


def _pair_coef(coef_ref, g, hp):
    row = lax.broadcasted_iota(jnp.int32, (2 * ATT_BLOCK, 1), 0)
    first = g * ATT_HEADS + hp * 2
    return jnp.where(row < ATT_BLOCK, coef_ref[first], coef_ref[first + 1])


def _band(with_prev, first_key):
    B = ATT_BLOCK
    keys = 2 * B if with_prev else B
    qi = jnp.bitwise_and(lax.broadcasted_iota(jnp.int32, (2 * B, keys), 0), B - 1)
    kj = lax.broadcasted_iota(jnp.int32, (2 * B, keys), 1)
    delta = qi + (B if with_prev else 0) - kj
    valid = (delta >= 0) & (delta <= B)
    if with_prev:
        valid = valid & (kj >= first_key)
    return valid, delta.astype(F32)


def _band_next(first_key):
    B = ATT_BLOCK
    qi = jnp.bitwise_and(lax.broadcasted_iota(jnp.int32, (2 * B, B), 0), B - 1)
    kj = lax.broadcasted_iota(jnp.int32, (2 * B, B), 1)
    delta = qi + B - kj
    return (delta <= B) & (kj >= first_key), delta.astype(F32)


def att_fwd(z, g, name):
    B = ATT_BLOCK
    n_units, R, n_slabs, has_prev, col0, cur, prev, _, unit = _att_specs(g)

    def body(coef_ref, *refs):
        if has_prev:
            q_ref, kc_ref, vc_ref, kp_ref, vp_ref, o_ref, l_ref = refs
        else:
            q_ref, kc_ref, vc_ref, o_ref, l_ref = refs
        hp, s = pl.program_id(0), pl.program_id(1)
        cf2 = _pair_coef(coef_ref, g, hp)
        sel0 = _head_lanes(0)

        def one(u):
            rows, inner_prev, prev_rows, first_key, _, _, _ = unit(u, s)
            valid, dist = _band(has_prev, first_key)
            q2 = _stack_heads(q_ref[rows, :], sel0)
            kk, vv = kc_ref[rows, :], vc_ref[rows, :]
            if has_prev:
                k_from, v_from = (kc_ref, vc_ref) if inner_prev else (kp_ref, vp_ref)
                kk = jnp.concatenate([k_from[prev_rows, :], kk], axis=0)
                vv = jnp.concatenate([v_from[prev_rows, :], vv], axis=0)
            sc = jnp.where(valid, _dot(q2, kk, tb=True) * 0.125 - cf2 * dist, NEG_INF)
            mx = jnp.max(sc, axis=-1, keepdims=True)
            e = jnp.exp(sc - mx)
            den = jnp.sum(e, axis=-1, keepdims=True)
            o2 = _dot(e * (1.0 / den), vv)
            lse2 = mx + jnp.log(den)
            o_ref[rows, :] = jnp.where(sel0, o2[:B], o2[B:])
            l_ref[rows, :] = jnp.where(sel0, lse2[:B], lse2[B:])

        _for_each_unit(n_units, one)

    in_specs = [pl.BlockSpec(memory_space=pltpu.SMEM), cur(col0), cur(col0 + 4), cur(col0 + 8)]
    args = [_alibi_coef(), z, z, z]
    if has_prev:
        in_specs += [prev(col0 + 4), prev(col0 + 8)]
        args += [z, z]
    out = jax.ShapeDtypeStruct((SEQ, ATT_WIDTH), F32)
    return pl.pallas_call(
        body, name=name, grid=(HEAD_PAIRS, n_slabs), in_specs=in_specs,
        out_specs=[cur(0), cur(0)], out_shape=[out, out],
        compiler_params=pltpu.CompilerParams(dimension_semantics=("parallel", "arbitrary")),
    )(*args)


def att_bwd(z, l, do, corr, g, name):
    B = ATT_BLOCK
    n_units, R, n_slabs, neighbours, col0, cur, prev, nxt, unit = _att_specs(g)

    def body(coef_ref, *refs):
        if neighbours:
            (q_ref, kc_ref, vc_ref, l_ref, do_ref, cr_ref, kp_ref, vp_ref, qn_ref, ln_ref, don_ref, crn_ref,
             dq_ref, dk_ref, dv_ref, dq_sc, dk_sc, dv_sc) = refs
        else:
            q_ref, kc_ref, vc_ref, l_ref, do_ref, cr_ref, dq_ref, dk_ref, dv_ref, dq_sc, dk_sc, dv_sc = refs
        hp, s = pl.program_id(0), pl.program_id(1)
        cf2 = _pair_coef(coef_ref, g, hp)
        sel0 = _head_lanes(0)
        own = slice(B, 2 * B) if neighbours else slice(0, B)

        def one(u):
            rows, inner_prev, prev_rows, first_key, inner_next, next_rows, first_key_n = unit(u, s)
            valid, dist = _band(neighbours, first_key)
            kc, vc = kc_ref[rows, :], vc_ref[rows, :]
            kk, vv = kc, vc
            if neighbours:
                k_from, v_from = (kc_ref, vc_ref) if inner_prev else (kp_ref, vp_ref)
                kk = jnp.concatenate([k_from[prev_rows, :], kc], axis=0)
                vv = jnp.concatenate([v_from[prev_rows, :], vc], axis=0)
            q2, do2 = _stack_heads(q_ref[rows, :], sel0), _stack_heads(do_ref[rows, :], sel0)
            keys = kk.shape[0]
            lse2, cr2 = _stack_values(l_ref[rows, :], sel0, keys), _stack_values(cr_ref[rows, :], sel0, keys)
            p = jnp.exp(jnp.where(valid, _dot(q2, kk, tb=True) * 0.125 - cf2 * dist, NEG_INF) - lse2)
            ds = p * (_dot(do2, vv, tb=True) + cr2)
            dq2 = _dot(ds, kk)
            dk = _dot(ds, q2, ta=True)[own]
            dv = _dot(p, do2, ta=True)[own]
            if neighbours:
                valid_n, dist_n = _band_next(first_key_n)
                q_from, l_from, do_from, cr_from = ((q_ref, l_ref, do_ref, cr_ref) if inner_next
                                                    else (qn_ref, ln_ref, don_ref, crn_ref))
                qn2, don2 = _stack_heads(q_from[next_rows, :], sel0), _stack_heads(do_from[next_rows, :], sel0)
                lse_n2 = _stack_values(l_from[next_rows, :], sel0, B)
                cr_n2 = _stack_values(cr_from[next_rows, :], sel0, B)
                p_n = jnp.exp(jnp.where(valid_n, _dot(qn2, kc, tb=True) * 0.125 - cf2 * dist_n, NEG_INF) - lse_n2)
                ds_n = p_n * (_dot(don2, vc, tb=True) + cr_n2)
                dk = dk + _dot(ds_n, qn2, ta=True)
                dv = dv + _dot(p_n, don2, ta=True)
            dq_sc[rows, :] = jnp.where(sel0, dq2[:B], dq2[B:]) * 0.125
            dk_sc[rows, :] = dk * 0.125
            dv_sc[rows, :] = dv

        _for_each_unit(n_units, one)
        dq_ref[...] = dq_sc[...].astype(dq_ref.dtype)
        dk_ref[...] = dk_sc[...].astype(dk_ref.dtype)
        dv_ref[...] = dv_sc[...].astype(dv_ref.dtype)

    in_specs = [pl.BlockSpec(memory_space=pltpu.SMEM), cur(col0), cur(col0 + 4), cur(col0 + 8), cur(0), cur(0), cur(0)]
    args = [_alibi_coef(), z, z, z, l, do, corr]
    if neighbours:
        in_specs += [prev(col0 + 4), prev(col0 + 8), nxt(col0), nxt(0), nxt(0), nxt(0)]
        args += [z, z, z, l, do, corr]
    out = jax.ShapeDtypeStruct((SEQ, ATT_WIDTH), MXU_DTYPE)
    return pl.pallas_call(
        body, name=name, grid=(HEAD_PAIRS, n_slabs), in_specs=in_specs,
        out_specs=[cur(0)] * 3, out_shape=[out] * 3,
        scratch_shapes=[pltpu.VMEM((R, 128), F32)] * 3,
        compiler_params=pltpu.CompilerParams(dimension_semantics=("parallel", "arbitrary"),
                                             vmem_limit_bytes=MATMUL_VMEM_BYTES),
    )(*args)


def _head_sum(x):
    i = lax.broadcasted_iota(jnp.int32, (128, 128), 0) // 64
    j = lax.broadcasted_iota(jnp.int32, (128, 128), 1) // 64
    return _dot_f32(x, (i == j).astype(F32), ones_on_right=True)


def _group_weights(l0, l1, l2):
    mx = jnp.maximum(jnp.maximum(l0, l1), l2)
    e0, e1, e2 = jnp.exp(l0 - mx), jnp.exp(l1 - mx), jnp.exp(l2 - mx)
    inv = 1.0 / (e0 + e1 + e2)
    return e0 * inv, e1 * inv, e2 * inv


def att_combine_fwd(o, l, name):
    def body(o0, o1, o2, l0, l1, l2, y_ref):
        w0, w1, w2 = _group_weights(l0[...], l1[...], l2[...])
        y_ref[...] = (o0[...] * w0 + o1[...] * w1 + o2[...] * w2).astype(y_ref.dtype)

    blk = pl.BlockSpec((ROW_TILE, ATT_WIDTH), lambda i: (i, 0))
    return pl.pallas_call(
        body, name=name, grid=(SEQ // ROW_TILE,), in_specs=[blk] * 6, out_specs=blk,
        out_shape=jax.ShapeDtypeStruct((SEQ, ATT_WIDTH), MXU_DTYPE),
    )(*o, *l)


def att_combine_bwd(o, l, dy, name):
    def body(o0, o1, o2, l0, l1, l2, dy_ref, do0, do1, do2, cr0, cr1, cr2):
        w = _group_weights(l0[...], l1[...], l2[...])
        dyv = dy_ref[...]
        dw = [_head_sum(dyv * o_ref[...]) for o_ref in (o0, o1, o2)]
        tot = w[0] * dw[0] + w[1] * dw[1] + w[2] * dw[2]
        for g, (do_ref, cr_ref) in enumerate(((do0, cr0), (do1, cr1), (do2, cr2))):
            do_ref[...] = dyv * w[g]
            cr_ref[...] = -w[g] * tot

    blk = pl.BlockSpec((ROW_TILE, 128), lambda i, j: (i, j))
    out = jax.ShapeDtypeStruct((SEQ, ATT_WIDTH), F32)
    res = pl.pallas_call(
        body, name=name, grid=(SEQ // ROW_TILE, HEAD_PAIRS), in_specs=[blk] * 7, out_specs=[blk] * 6, out_shape=[out] * 6,
    )(*o, *l, dy)
    return res[:N_GROUPS], res[N_GROUPS:]


SUM_ROW_TILES = (1024, 512, 256, 128, 64, 32, 16)
SUM_TILE_BYTES = 24 * 1024 * 1024
SUM_PARAMS = pltpu.CompilerParams(vmem_limit_bytes=MATMUL_VMEM_BYTES)


def _row_tile(rows, cols, operands):
    fit = [t for t in SUM_ROW_TILES if rows % t == 0]
    return next((t for t in fit if 2 * 4 * operands * t * cols <= SUM_TILE_BYTES), fit[-1])


def _shard_shape(rows, cols, axis):
    return (rows // N_CHIPS, cols) if axis == 0 else (rows, cols // N_CHIPS)


def _half_shape(rows, cols, axis):
    return (rows, cols // 2) if axis == 0 else (rows // 2, cols)


def _piece_shape(rows, cols, axis):
    return (rows // N_CHIPS, cols // 2) if axis == 0 else (rows // 2, cols // N_CHIPS)


def place_own_block(shard, chip, rows, cols, axis, name):
    sr, sc = _shard_shape(rows, cols, axis)
    tr = _row_tile(sr, sc, 2)

    def body(chip_ref, s_ref, o_ref):
        o_ref[...] = s_ref[...].astype(o_ref.dtype)

    if axis == 0:
        out_map = lambda i, chip_ref: (chip_ref[0] * (sr // tr) + i, 0)
    else:
        out_map = lambda i, chip_ref: (i, chip_ref[0])
    return pl.pallas_call(
        body, name=name, out_shape=jax.ShapeDtypeStruct((rows, cols), WEIGHT_COMM_DTYPE), compiler_params=SUM_PARAMS,
        grid_spec=pltpu.PrefetchScalarGridSpec(
            num_scalar_prefetch=1, grid=(sr // tr,), in_specs=[pl.BlockSpec((tr, sc), lambda i, chip_ref: (i, 0))],
            out_specs=pl.BlockSpec((tr, sc), out_map)),
    )(chip, shard)


def add_halves(g, theirs, core, rows, cols, axis, name):
    hr, hc = _half_shape(rows, cols, axis)
    tr = _row_tile(hr, hc, 3)

    def body(core_ref, g_ref, t_ref, o_ref):
        o_ref[...] = (g_ref[...].astype(F32) + t_ref[...].astype(F32)).astype(o_ref.dtype)

    if axis == 0:
        g_map = lambda i, core_ref: (i, core_ref[0])
    else:
        g_map = lambda i, core_ref: (core_ref[0] * (hr // tr) + i, 0)
    blk = pl.BlockSpec((tr, hc), lambda i, core_ref: (i, 0))
    return pl.pallas_call(
        body, name=name, out_shape=jax.ShapeDtypeStruct((hr, hc), GRAD_COMM_DTYPE), compiler_params=SUM_PARAMS,
        grid_spec=pltpu.PrefetchScalarGridSpec(
            num_scalar_prefetch=1, grid=(hr // tr,), in_specs=[pl.BlockSpec((tr, hc), g_map), blk], out_specs=blk),
    )(core, g, theirs)


def add_pieces(half, got, chip, rows, cols, axis, name):
    hr, _ = _half_shape(rows, cols, axis)
    pr, pc = _piece_shape(rows, cols, axis)
    tr = _row_tile(pr, pc, 5)

    def body(chip_ref, h_ref, got_ref, o_ref):
        o_ref[...] = (h_ref[...].astype(F32) + got_ref[0].astype(F32) + got_ref[1].astype(F32) + got_ref[2].astype(F32))

    if axis == 0:
        h_map = lambda i, chip_ref: (chip_ref[0] * (pr // tr) + i, 0)
    else:
        h_map = lambda i, chip_ref: (i, chip_ref[0])
    return pl.pallas_call(
        body, name=name, out_shape=jax.ShapeDtypeStruct((pr, pc), F32), compiler_params=SUM_PARAMS,
        grid_spec=pltpu.PrefetchScalarGridSpec(
            num_scalar_prefetch=1, grid=(pr // tr,),
            in_specs=[pl.BlockSpec((tr, pc), h_map), pl.BlockSpec((3, tr, pc), lambda i, chip_ref: (0, i, 0))],
            out_specs=pl.BlockSpec((tr, pc), lambda i, chip_ref: (i, 0))),
    )(chip, half, got)


def _adamw_math(w, g, m, v):
    nm = ADAM_B1 * m + (1.0 - ADAM_B1) * g
    nv = ADAM_B2 * v + (1.0 - ADAM_B2) * (g * g)
    m_hat = nm / (1.0 - ADAM_B1 ** ADAM_STEP)
    v_hat = nv / (1.0 - ADAM_B2 ** ADAM_STEP)
    return -ADAM_LR * (m_hat / (jnp.sqrt(v_hat) + ADAM_EPS) + ADAM_WD * w), nm, nv


def adamw_halves(w, mine, theirs, m, v, core, rows, cols, axis, name):
    sr, sc = _shard_shape(rows, cols, axis)
    pr, pc = _piece_shape(rows, cols, axis)
    tr = _row_tile(pr, pc, 9)
    nt = pr // tr

    def body(core_ref, w_ref, a_ref, b_ref, m_ref, v_ref, g_ref, d_ref, nm_ref, nv_ref):
        g = jnp.where(pl.program_id(0) == core_ref[0], a_ref[...], b_ref[...])
        g_ref[...] = g
        d_ref[...], nm_ref[...], nv_ref[...] = _adamw_math(w_ref[...], g, m_ref[...], v_ref[...])

    if axis == 0:
        full = pl.BlockSpec((tr, pc), lambda h, i, core_ref: (i, h))
    else:
        full = pl.BlockSpec((tr, pc), lambda h, i, core_ref: (h * nt + i, 0))
    part = pl.BlockSpec((tr, pc), lambda h, i, core_ref: (i, 0))
    out = jax.ShapeDtypeStruct((sr, sc), F32)
    return pl.pallas_call(
        body, name=name, out_shape=[out, out, out, out], compiler_params=SUM_PARAMS,
        grid_spec=pltpu.PrefetchScalarGridSpec(
            num_scalar_prefetch=1, grid=(2, nt), in_specs=[full, part, part, full, full], out_specs=[full] * 4),
    )(core, w, mine, theirs, m, v)


BIG = (
    ("ffn1_w_gate_up", D_MODEL, 2 * D_FF, 1),
    ("ffn1_w_down", D_FF, D_MODEL, 0),
    ("w_in", D_MODEL, IN_COLS, 1),
    ("w_branch_hg", HG_WIDTH, D_MODEL, 1),
    ("w_branch_att", ATT_WIDTH, D_MODEL, 1),
    ("w_out", D_MODEL, D_MODEL, 0),
    ("ffn2_w_gate_up", D_MODEL, 2 * D_FF, 1),
    ("ffn2_w_down", D_FF, D_MODEL, 0),
)
N_BIG = len(BIG)
ANY = pl.BlockSpec(memory_space=pl.ANY)


def _place():
    return lax.axis_index("x"), lax.axis_index("y"), lax.axis_index("c")


def _other_chips(x, y):
    return ((1 - x, y), (x, 1 - y), (1 - x, 1 - y))


MAX_COPY_CHUNKS = 16
CHUNK_ROW_ALIGN = 16


def _row_chunks(view):
    rows = view.shape[0]
    n = next(n for n in range(MAX_COPY_CHUNKS, 0, -1) if rows % (CHUNK_ROW_ALIGN * n) == 0 or n == 1)
    step = rows // n
    return [pl.ds(i * step, step) for i in range(n)]


def _remote(src, dst, send_sem, recv_sem, device):
    return pltpu.make_async_remote_copy(src_ref=src, dst_ref=dst, send_sem=send_sem, recv_sem=recv_sem,
                                        device_id=device, device_id_type=MESH)


def _start_remote(src, dst, send_sem, recv_sem, device):
    for rows in _row_chunks(src):
        _remote(src.at[rows, :], dst.at[rows, :], send_sem, recv_sem, device).start()
    return _remote(src, dst, send_sem, recv_sem, device)


HBM = pl.BlockSpec(memory_space=pltpu.HBM)
SEM = pl.BlockSpec(memory_space=pltpu.SEMAPHORE)
SPLIT_COPY_EFFECT = pltpu.SideEffectType.DATAFLOW_SIDE_EFFECTING
GROUPS = {"ffn1": (0, 1), "mix": (2, 3, 4, 5), "ffn2": (6, 7)}


def _in_hbm(a):
    return pltpu.with_memory_space_constraint(a, pltpu.HBM)


class _SemList:
    def __init__(self, refs):
        self.refs = refs
        self.at = self

    def __getitem__(self, index):
        w, k = index
        return self.refs[3 * w + k]


def _gather_piece(ref, rows, cols, axis, chip, c):
    sr, sc = _shard_shape(rows, cols, axis)
    j = 2 * chip[0] + chip[1]
    if axis == 0:
        return ref.at[pl.ds(j * sr + c * (sr // 2), sr // 2), :]
    return ref.at[pl.ds(c * (sr // 2), sr // 2), pl.ds(pl.multiple_of(j * sc, 128), sc)]


def _start_gather_sends(bufs, ws, send_sems, recv_sems):
    x, y, c = _place()
    for w, (_, r, cc, ax) in enumerate(ws):
        mine = _gather_piece(bufs[w], r, cc, ax, (x, y), c)
        for k, chip in enumerate(_other_chips(x, y)):
            _start_remote(mine, mine, send_sems.at[w, k], recv_sems.at[w, k], (*chip, c))


def _wait_gather_sends(bufs, ws, send_sems, recv_sems):
    x, y, c = _place()
    for w, (_, r, cc, ax) in enumerate(ws):
        for k, chip in enumerate(_other_chips(x, y)):
            got = _gather_piece(bufs[w], r, cc, ax, chip, c)
            _remote(got, got, send_sems.at[w, k], recv_sems.at[w, k], (x, y, c)).wait_recv()
    for w, (_, r, cc, ax) in enumerate(ws):
        mine = _gather_piece(bufs[w], r, cc, ax, (x, y), c)
        for k in range(3):
            _remote(mine, mine, send_sems.at[w, k], recv_sems.at[w, k], (x, y, c)).wait_send()


def _forward_halves(bufs, ws, send_sems, recv_sems):
    x, y, c = _place()
    passed = []
    for w, (_, r, cc, ax) in enumerate(ws):
        for k, chip in enumerate(_other_chips(x, y)):
            got = _gather_piece(bufs[w], r, cc, ax, chip, c)
            passed.append(_start_remote(got, got, send_sems.at[w, k], recv_sems.at[w, k], (x, y, 1 - c)))
    for w, (_, r, cc, ax) in enumerate(ws):
        for k, chip in enumerate(_other_chips(x, y)):
            got = _gather_piece(bufs[w], r, cc, ax, chip, 1 - c)
            _remote(got, got, send_sems.at[w, k], recv_sems.at[w, k], (x, y, c)).wait_recv()
    for cp in passed:
        cp.wait_send()


def gather_start(placed, after, group):
    ws = [BIG[i] for i in GROUPS[group]]
    n = len(ws)

    def body(*refs):
        bufs = refs[:n]
        send_sems, recv_sems = _SemList(refs[n + 1:4 * n + 1]), _SemList(refs[4 * n + 1:7 * n + 1])
        token = refs[-1]
        _start_gather_sends(bufs, ws, send_sems, recv_sems)
        token[...] = jnp.zeros_like(token)

    out = pl.pallas_call(
        body, name=f"gather_start_{group}", in_specs=[HBM] * n + [ANY],
        out_specs=[SEM] * (6 * n) + [HBM] * n + [pl.BlockSpec(memory_space=pltpu.VMEM)],
        out_shape=[pltpu.SemaphoreType.DMA(())] * (6 * n)
        + [pltpu.HBM((r, cc), WEIGHT_COMM_DTYPE) for _, r, cc, _ in ws] + [jax.ShapeDtypeStruct((8, 128), F32)],
        input_output_aliases={w: 6 * n + w for w in range(n)},
        compiler_params=pltpu.CompilerParams(has_side_effects=SPLIT_COPY_EFFECT),
    )(*[_in_hbm(p) for p in placed], after)
    return out[:3 * n], out[3 * n:6 * n], out[6 * n:7 * n], out[-1]


def gather_wait(bufs, send_sems, recv_sems, after, group):
    ws = [BIG[i] for i in GROUPS[group]]
    n = len(ws)

    def body(*refs):
        _wait_gather_sends(refs[:n], ws, _SemList(refs[n:n + 3 * n]), _SemList(refs[n + 3 * n:n + 6 * n]))

    return pl.pallas_call(
        body, name=f"gather_wait_{group}", in_specs=[HBM] * n + [SEM] * (6 * n) + [ANY] * len(after), out_specs=[HBM] * n,
        out_shape=[pltpu.HBM((r, cc), WEIGHT_COMM_DTYPE) for _, r, cc, _ in ws],
        input_output_aliases={w: w for w in range(n)},
        compiler_params=pltpu.CompilerParams(has_side_effects=SPLIT_COPY_EFFECT),
    )(*bufs, *send_sems, *recv_sems, *after)


def gather_forward(bufs, group):
    ws = [BIG[i] for i in GROUPS[group]]
    n = len(ws)

    def body(*refs):
        _forward_halves(refs[n:2 * n], ws, refs[2 * n], refs[2 * n + 1])

    return pl.pallas_call(
        body, name=f"gather_forward_{group}", in_specs=[ANY] * n, out_specs=[ANY] * n,
        out_shape=[jax.ShapeDtypeStruct((r, cc), WEIGHT_COMM_DTYPE) for _, r, cc, _ in ws],
        input_output_aliases={w: w for w in range(n)},
        scratch_shapes=[pltpu.SemaphoreType.DMA((n, 3))] * 2,
    )(*bufs)


def _half(ref, rows, cols, axis, c):
    if axis == 0:
        return ref.at[:, pl.ds(pl.multiple_of(c * (cols // 2), 128), cols // 2)]
    return ref.at[pl.ds(c * (rows // 2), rows // 2), :]


def _piece_of_half(ref, rows, cols, axis, chip):
    j = 2 * chip[0] + chip[1]
    pr, pc = _piece_shape(rows, cols, axis)
    if axis == 0:
        return ref.at[pl.ds(j * pr, pr), :]
    return ref.at[:, pl.ds(pl.multiple_of(j * pc, 128), pc)]


def sibling_exchange_start(srcs, view, landing_shapes, dtype, name):
    n = len(srcs)

    def body(*refs):
        ins, land, sems = refs[:n], refs[n:2 * n], refs[2 * n:4 * n]
        x, y, c = _place()
        for w in range(n):
            _start_remote(view(ins[w], w, c), land[w], sems[w], sems[n + w], (x, y, 1 - c))
        refs[-1][...] = jnp.zeros_like(refs[-1])

    landing = [lax.empty(shape, dtype) for shape in landing_shapes]
    out = pl.pallas_call(
        body, name=name, in_specs=[HBM] * (2 * n),
        out_specs=[SEM] * (2 * n) + [HBM] * (2 * n) + [pl.BlockSpec(memory_space=pltpu.VMEM)],
        out_shape=[pltpu.SemaphoreType.DMA(())] * (2 * n) + [pltpu.HBM(a.shape, a.dtype) for a in srcs]
        + [pltpu.HBM(shape, dtype) for shape in landing_shapes] + [jax.ShapeDtypeStruct((8, 128), F32)],
        input_output_aliases={i: 2 * n + i for i in range(2 * n)},
        compiler_params=pltpu.CompilerParams(has_side_effects=SPLIT_COPY_EFFECT),
    )(*[_in_hbm(a) for a in srcs], *[_in_hbm(b) for b in landing])
    return out[:n], out[n:2 * n], out[2 * n:3 * n], out[3 * n:4 * n], out[-1]


def sibling_exchange_wait(srcs, landing, send_sems, recv_sems, view, after, name):
    n = len(srcs)

    def body(*refs):
        ins, land, sems = refs[:n], refs[n:2 * n], refs[2 * n:4 * n]
        x, y, c = _place()
        for w in range(n):
            cp = _remote(view(ins[w], w, c), land[w], sems[w], sems[n + w], (x, y, c))
            cp.wait_send()
            cp.wait_recv()

    out = pl.pallas_call(
        body, name=name, in_specs=[HBM] * (2 * n) + [SEM] * (2 * n) + [ANY] * len(after), out_specs=[HBM] * (2 * n),
        out_shape=[pltpu.HBM(a.shape, a.dtype) for a in srcs] + [pltpu.HBM(b.shape, b.dtype) for b in landing],
        input_output_aliases={i: i for i in range(2 * n)},
        compiler_params=pltpu.CompilerParams(has_side_effects=SPLIT_COPY_EFFECT),
    )(*srcs, *landing, *send_sems, *recv_sems, *after)
    return out[:n], out[n:]


def _scatter_copies(halves, got, ws, send_sems, recv_sems, start):
    x, y, c = _place()
    copies = []
    for w, (_, r, cc, ax) in enumerate(ws):
        for k, chip in enumerate(_other_chips(x, y)):
            args = (_piece_of_half(halves[w], r, cc, ax, chip), got[w].at[k], send_sems.at[w, k], recv_sems.at[w, k], (*chip, c))
            copies.append(_start_remote(*args) if start else _remote(*args))
    return copies


def scatter_start(halves, group):
    ws = [BIG[i] for i in GROUPS[group]]
    n = len(ws)

    def body(*refs):
        sems = refs[2 * n:8 * n]
        _scatter_copies(refs[:n], refs[n:2 * n], ws, _SemList(sems[:3 * n]), _SemList(sems[3 * n:]), start=True)
        refs[-1][...] = jnp.zeros_like(refs[-1])

    landing = [lax.empty((3,) + _piece_shape(r, cc, ax), GRAD_COMM_DTYPE) for _, r, cc, ax in ws]
    out = pl.pallas_call(
        body, name=f"scatter_start_{group}", in_specs=[HBM] * (2 * n),
        out_specs=[SEM] * (6 * n) + [HBM] * (2 * n) + [pl.BlockSpec(memory_space=pltpu.VMEM)],
        out_shape=[pltpu.SemaphoreType.DMA(())] * (6 * n)
        + [pltpu.HBM(_half_shape(r, cc, ax), GRAD_COMM_DTYPE) for _, r, cc, ax in ws]
        + [pltpu.HBM((3,) + _piece_shape(r, cc, ax), GRAD_COMM_DTYPE) for _, r, cc, ax in ws]
        + [jax.ShapeDtypeStruct((8, 128), F32)],
        input_output_aliases={i: 6 * n + i for i in range(2 * n)},
        compiler_params=pltpu.CompilerParams(has_side_effects=SPLIT_COPY_EFFECT),
    )(*[_in_hbm(h) for h in halves], *[_in_hbm(b) for b in landing])
    return out[:3 * n], out[3 * n:6 * n], out[6 * n:7 * n], out[7 * n:8 * n], out[-1]


def scatter_wait(halves, got, send_sems, recv_sems, after, group):
    ws = [BIG[i] for i in GROUPS[group]]
    n = len(ws)

    def body(*refs):
        sems = refs[2 * n:8 * n]
        for cp in _scatter_copies(refs[:n], refs[n:2 * n], ws, _SemList(sems[:3 * n]), _SemList(sems[3 * n:]), start=False):
            cp.wait_send()
            cp.wait_recv()

    out = pl.pallas_call(
        body, name=f"scatter_wait_{group}", in_specs=[HBM] * (2 * n) + [SEM] * (6 * n) + [ANY] * len(after),
        out_specs=[HBM] * (2 * n),
        out_shape=[pltpu.HBM(_half_shape(r, cc, ax), GRAD_COMM_DTYPE) for _, r, cc, ax in ws]
        + [pltpu.HBM((3,) + _piece_shape(r, cc, ax), GRAD_COMM_DTYPE) for _, r, cc, ax in ws],
        input_output_aliases={i: i for i in range(2 * n)},
        compiler_params=pltpu.CompilerParams(has_side_effects=SPLIT_COPY_EFFECT),
    )(*halves, *got, *send_sems, *recv_sems, *after)
    return out[:n], out[n:]


N_DEV = 8
SMALL = ("ffn1_norm", "mix_norm", "hg_lower_bounds", "hg_out_norm", "ffn2_norm", "final_norm")
SMALL_STAGE_ROWS = 8


def small_step(loss, grads, w, m, v, behind):
    n = len(SMALL)
    shapes = [g.shape for g in grads]
    first_row = [sum(s[0] for s in shapes[:i]) for i in range(n + 1)]
    assert first_row[n] < SMALL_STAGE_ROWS
    loss_row = (pl.ds(first_row[n], 1), pl.ds(0, loss.shape[1]))

    def body(*refs):
        loss_ref, g_refs, w_refs, m_refs, v_refs = refs[0], refs[1:1 + n], refs[1 + n:1 + 2 * n], refs[1 + 2 * n:1 + 3 * n], refs[1 + 3 * n:1 + 4 * n]
        outs = refs[2 + 4 * n:3 + 8 * n]
        loss_out, dg_refs, d_refs, nm_refs, nv_refs = outs[0], outs[1:1 + n], outs[1 + n:1 + 2 * n], outs[1 + 2 * n:1 + 3 * n], outs[1 + 3 * n:]
        stage, gathered, send_sems, recv_sems = refs[3 + 8 * n:]
        x, y, c = _place()
        me = 4 * x + 2 * y + c

        def slot(i, shape):
            return pl.ds(first_row[i], shape[0]), pl.ds(0, shape[1])

        stage[...] = jnp.zeros_like(stage)
        for i, g_ref in enumerate(g_refs):
            stage[slot(i, shapes[i])] = g_ref[...]
        stage[loss_row] = loss_ref[pl.ds(0, 1), :]
        gathered[me] = stage[...]
        copies = []
        for k in range(1, N_DEV):
            peer = (x ^ (k >> 2), y ^ ((k >> 1) & 1), c ^ (k & 1))
            cp = pltpu.make_async_remote_copy(
                src_ref=stage, dst_ref=gathered.at[me], send_sem=send_sems.at[k - 1], recv_sem=recv_sems.at[k - 1],
                device_id=peer, device_id_type=MESH)
            cp.start()
            copies.append(cp)
        for cp in copies:
            cp.wait()
        acc = gathered[0]
        for k in range(1, N_DEV):
            acc = acc + gathered[k]
        stage[...] = acc
        loss_out[...] = jnp.broadcast_to(stage[loss_row], loss_out.shape)
        for i in range(n):
            g = stage[slot(i, shapes[i])]
            dg_refs[i][...] = g
            d_refs[i][...], nm_refs[i][...], nv_refs[i][...] = _adamw_math(w_refs[i][...], g, m_refs[i][...], v_refs[i][...])

    vm = pl.BlockSpec(memory_space=pltpu.VMEM)
    per_param = [jax.ShapeDtypeStruct(s, F32) for s in shapes]
    out = pl.pallas_call(
        body, name="small_step", in_specs=[vm] * (1 + 4 * n) + [ANY], out_specs=[vm] * (1 + 4 * n),
        out_shape=[jax.ShapeDtypeStruct(loss.shape, F32)] + per_param * 4,
        scratch_shapes=[pltpu.VMEM((SMALL_STAGE_ROWS, D_MODEL), F32),
                        pltpu.VMEM((N_DEV, SMALL_STAGE_ROWS, D_MODEL), F32),
                        pltpu.SemaphoreType.DMA((N_DEV - 1,)), pltpu.SemaphoreType.DMA((N_DEV - 1,))],
    )(loss, *grads, *w, *m, *v, behind)
    return out[0], out[1:1 + n], out[1 + n:1 + 2 * n], out[1 + 2 * n:1 + 3 * n], out[1 + 3 * n:]


def _swiglu_block_fwd(h, norm_g, w_gu, w_down, tag, behind=()):
    n = rmsnorm_fwd(h, norm_g, f"{tag}_norm", behind=behind)
    gu = matmul(n, w_gu, out_dtype=ACT_DTYPE, name=f"{tag}_gate_up")
    s = swiglu_fwd(gu, f"{tag}_swiglu")
    h_out = matmul(s, w_down, res=h, scale=0.5, name=f"{tag}_down")
    return h_out, (n, gu, s)


def _swiglu_block_bwd(h, norm_g, w_gu, w_down, saved, dh_out, df, tag, exchange, behind=()):
    n, gu, s = saved
    d_down = matmul(s, df, ta=True, scale=0.5, out_dtype=GRAD_COMM_DTYPE, name=f"{tag}_d_w_down")
    ds = matmul(df, w_down, tb=True, scale=0.5, out_dtype=ACT_DTYPE, behind=behind, name=f"{tag}_d_s")
    dgu = swiglu_bwd(gu, ds, f"{tag}_swiglu_bwd")
    d_gu = matmul(n, dgu, ta=True, out_dtype=GRAD_COMM_DTYPE, name=f"{tag}_d_w_gate_up")
    tokens = exchange.gradients_ready(tag, {f"{tag}_w_gate_up": d_gu, f"{tag}_w_down": d_down})
    dn = matmul(dgu, w_gu, tb=True, behind=tokens, name=f"{tag}_d_n")
    dh, dh_m, dg = rmsnorm_bwd(h, norm_g, dn, dh_out, f"{tag}_norm_bwd")
    return dh, dh_m, dg


def local_step(x, target, small, exchange):
    big = {}
    token, big_ffn1 = exchange.weights("ffn1", x)
    big.update(big_ffn1)
    h1, saved1 = _swiglu_block_fwd(x, small["ffn1_norm"], big["ffn1_w_gate_up"], big["ffn1_w_down"], "ffn1", token)
    token, big_mix = exchange.weights("mix", h1)
    big.update(big_mix)
    u = rmsnorm_fwd(h1, small["mix_norm"], "mix_norm", behind=token)
    z = matmul(u, big["w_in"], name="w_in")
    p = small["hg_lower_bounds"]
    lb = 1.0 / (1.0 + jnp.exp(p[1:2] - p[0:1]))
    y_hg, o_raw, states = hgrn_fwd(z, lb, small["hg_out_norm"], "hgrn_fwd")
    o_att, l_att = zip(*[att_fwd(z, g, f"att_fwd_{g}") for g in range(N_GROUPS)])
    y_att = att_combine_fwd(o_att, l_att, "att_combine")
    bh = matmul(y_hg, big["w_branch_hg"], name="branch_hg")
    ba = matmul(y_att, big["w_branch_att"], name="branch_att")
    merged = merge_fwd(z, bh, ba, "merge")
    h2 = matmul(merged, big["w_out"], res=h1, name="w_out")
    token, big_ffn2 = exchange.weights("ffn2", h2)
    big.update(big_ffn2)
    h3, saved2 = _swiglu_block_fwd(h2, small["ffn2_norm"], big["ffn2_w_gate_up"], big["ffn2_w_down"], "ffn2", token)
    dh3, dh3_m, d_final, loss = final_norm_loss(h3, small["final_norm"], target, "final_norm_loss")

    gs, gb = {"final_norm": d_final}, {}
    dh2, dh2_m, gs["ffn2_norm"] = _swiglu_block_bwd(
        h2, small["ffn2_norm"], big["ffn2_w_gate_up"], big["ffn2_w_down"], saved2, dh3, dh3_m, "ffn2", exchange)
    token = exchange.backward_done("ffn2", dh2)
    gb["w_out"] = matmul(merged, dh2_m, ta=True, out_dtype=GRAD_COMM_DTYPE, name="d_w_out")
    dmerged = matmul(dh2_m, big["w_out"], tb=True, behind=token, name="d_merged")
    dbh, dba, dgh, dga = merge_bwd(z, bh, ba, dmerged, "merge_bwd")
    gb["w_branch_hg"] = matmul(y_hg, dbh, ta=True, out_dtype=GRAD_COMM_DTYPE, name="d_w_branch_hg")
    gb["w_branch_att"] = matmul(y_att, dba, ta=True, out_dtype=GRAD_COMM_DTYPE, name="d_w_branch_att")
    dy_hg = matmul(dbh, big["w_branch_hg"], tb=True, name="d_y_hg")
    dy_att = matmul(dba, big["w_branch_att"], tb=True, name="d_y_att")
    dq, dfp, di, dog, d_lb, gs["hg_out_norm"] = hgrn_bwd(z, lb, small["hg_out_norm"], o_raw, states, dy_hg, "hgrn_bwd")
    do_att, corr = att_combine_bwd(o_att, l_att, dy_att, "att_combine_bwd")
    d_att = [part for g in range(N_GROUPS) for part in att_bwd(z, l_att[g], do_att[g], corr[g], g, f"att_bwd_{g}")]
    dz = jnp.concatenate([dq, dfp, di, dog, *d_att, dgh, dga], axis=1)
    gb["w_in"] = matmul(u, dz, ta=True, out_dtype=GRAD_COMM_DTYPE, name="d_w_in")
    token = exchange.gradients_ready("mix", gb)
    du = matmul(dz, big["w_in"], tb=True, behind=token, name="d_u")
    dh1, dh1_m, gs["mix_norm"] = rmsnorm_bwd(h1, small["mix_norm"], du, dh2, "mix_norm_bwd")
    token = exchange.backward_done("mix", dh1)
    dp0 = d_lb * lb * (1.0 - lb)
    gs["hg_lower_bounds"] = jnp.concatenate([dp0, -dp0], axis=0)
    dx, _, gs["ffn1_norm"] = _swiglu_block_bwd(
        x, small["ffn1_norm"], big["ffn1_w_gate_up"], big["ffn1_w_down"], saved1, dh1, dh1_m, "ffn1", exchange, token)
    exchange.backward_done("ffn1", dx)
    return loss, dx, gs


WEIGHTS = ("ffn1_norm", "ffn1_w_gate_up", "ffn1_w_down", "mix_norm", "w_in", "hg_lower_bounds", "hg_out_norm",
           "w_branch_hg", "w_branch_att", "w_out", "ffn2_norm", "ffn2_w_gate_up", "ffn2_w_down", "final_norm")


class WeightExchange:
    ORDER = ("ffn1", "mix", "ffn2")

    def __init__(self, shards, core, chip):
        self.core, self.chip = core, chip
        self.halving = None
        self.scattering = None
        self.reducing = {}
        first = self.ORDER[0]
        self.placed = {BIG[i][0]: place_own_block(shards[BIG[i][0]], chip, *BIG[i][1:], f"place_{BIG[i][0]}")
                       for i in GROUPS[first]}
        self._start_gather(first, self.placed[self._names(first)[0]])
        chip_behind = chip + self.token[0, :1].astype(jnp.int32)
        for group in self.ORDER[1:]:
            for i in GROUPS[group]:
                n, r, cc, ax = BIG[i]
                self.placed[n] = place_own_block(shards[n], chip_behind, r, cc, ax, f"place_{n}")
        self.placed_behind = [self.placed[n] for group in self.ORDER[1:] for n in self._names(group)]

    def _names(self, group):
        return [BIG[i][0] for i in GROUPS[group]]

    def _start_gather(self, group, after):
        send_sems, recv_sems, bufs, self.token = gather_start([self.placed[n] for n in self._names(group)], after, group)
        self.gathering = (group, send_sems, recv_sems, bufs)

    def weights(self, group, h):
        pending, send_sems, recv_sems, bufs = self.gathering
        assert pending == group
        after = self.placed_behind if group == self.ORDER[0] else [h]
        whole = gather_forward(gather_wait(bufs, send_sems, recv_sems, after, group), group)
        later = self.ORDER.index(group) + 1
        behind = []
        if later < len(self.ORDER):
            self._start_gather(self.ORDER[later], whole[0])
            behind = [self.token]
        return behind, dict(zip(self._names(group), whole))

    @staticmethod
    def _half_to_sibling(ws):
        return lambda ref, w, c: _half(ref, *ws[w][1:], 1 - c)

    def gradients_ready(self, group, grads):
        ws = [BIG[i] for i in GROUPS[group]]
        send_sems, recv_sems, own, theirs, token = sibling_exchange_start(
            [grads[n] for n, *_ in ws], self._half_to_sibling(ws), [_half_shape(r, cc, ax) for _, r, cc, ax in ws],
            GRAD_COMM_DTYPE, f"halves_start_{group}")
        self.halving = (group, send_sems, recv_sems, own, theirs)
        return [token]

    def backward_done(self, group, dh):
        behind = [self._finish_scatter([dh])] if self.scattering is not None else []
        pending, send_sems, recv_sems, own, theirs = self.halving
        assert pending == group
        ws = [BIG[i] for i in GROUPS[group]]
        own, theirs = sibling_exchange_wait(own, theirs, send_sems, recv_sems, self._half_to_sibling(ws), [dh],
                                            f"halves_wait_{group}")
        halves = [add_halves(g, t, self.core, r, cc, ax, f"add_halves_{n}") for (n, r, cc, ax), g, t in zip(ws, own, theirs)]
        send_sems, recv_sems, halves, got, self.token = scatter_start(halves, group)
        self.scattering = (group, send_sems, recv_sems, halves, got)
        return behind + [self.token]

    def _finish_scatter(self, after):
        group, send_sems, recv_sems, halves, got = self.scattering
        halves, got = scatter_wait(halves, got, send_sems, recv_sems, after, group)
        ws = [BIG[i] for i in GROUPS[group]]
        mine = [add_pieces(h, g, self.chip, r, cc, ax, f"add_pieces_{n}") for (n, r, cc, ax), h, g in zip(ws, halves, got)]
        send_sems, recv_sems, mine, theirs, token = sibling_exchange_start(
            mine, lambda ref, w, c: ref, [_piece_shape(r, cc, ax) for _, r, cc, ax in ws], F32, f"reduced_start_{group}")
        self.reducing[group] = (send_sems, recv_sems, mine, theirs)
        self.scattering = None
        return token

    def finish(self, after):
        return self._finish_scatter(after)

    def reduced_halves(self, group, after):
        send_sems, recv_sems, mine, theirs = self.reducing.pop(group)
        mine, theirs = sibling_exchange_wait(mine, theirs, send_sems, recv_sems, lambda ref, w, c: ref, after,
                                             f"reduced_wait_{group}")
        return {BIG[i][0]: (a, b) for i, a, b in zip(GROUPS[group], mine, theirs)}


def kernel(x, ffn1_norm, ffn1_w_gate_up, ffn1_w_down, mix_norm, w_in, hg_lower_bounds, hg_out_norm, w_branch_hg, w_branch_att, w_out, ffn2_norm, ffn2_w_gate_up, ffn2_w_down, final_norm, loss_target, m_ffn1_norm, m_ffn1_w_gate_up, m_ffn1_w_down, m_mix_norm, m_w_in, m_hg_lower_bounds, m_hg_out_norm, m_w_branch_hg, m_w_branch_att, m_w_out, m_ffn2_norm, m_ffn2_w_gate_up, m_ffn2_w_down, m_final_norm, v_ffn1_norm, v_ffn1_w_gate_up, v_ffn1_w_down, v_mix_norm, v_w_in, v_hg_lower_bounds, v_hg_out_norm, v_w_branch_hg, v_w_branch_att, v_w_out, v_ffn2_norm, v_ffn2_w_gate_up, v_ffn2_w_down, v_final_norm):
    w = dict(ffn1_norm=ffn1_norm, ffn1_w_gate_up=ffn1_w_gate_up, ffn1_w_down=ffn1_w_down, mix_norm=mix_norm, w_in=w_in,
             hg_lower_bounds=hg_lower_bounds, hg_out_norm=hg_out_norm, w_branch_hg=w_branch_hg, w_branch_att=w_branch_att,
             w_out=w_out, ffn2_norm=ffn2_norm, ffn2_w_gate_up=ffn2_w_gate_up, ffn2_w_down=ffn2_w_down, final_norm=final_norm)
    m = dict(ffn1_norm=m_ffn1_norm, ffn1_w_gate_up=m_ffn1_w_gate_up, ffn1_w_down=m_ffn1_w_down, mix_norm=m_mix_norm,
             w_in=m_w_in, hg_lower_bounds=m_hg_lower_bounds, hg_out_norm=m_hg_out_norm, w_branch_hg=m_w_branch_hg,
             w_branch_att=m_w_branch_att, w_out=m_w_out, ffn2_norm=m_ffn2_norm, ffn2_w_gate_up=m_ffn2_w_gate_up,
             ffn2_w_down=m_ffn2_w_down, final_norm=m_final_norm)
    v = dict(ffn1_norm=v_ffn1_norm, ffn1_w_gate_up=v_ffn1_w_gate_up, ffn1_w_down=v_ffn1_w_down, mix_norm=v_mix_norm,
             w_in=v_w_in, hg_lower_bounds=v_hg_lower_bounds, hg_out_norm=v_hg_out_norm, w_branch_hg=v_w_branch_hg,
             w_branch_att=v_w_branch_att, w_out=v_w_out, ffn2_norm=v_ffn2_norm, ffn2_w_gate_up=v_ffn2_w_gate_up,
             ffn2_w_down=v_ffn2_w_down, final_norm=v_final_norm)

    core = lax.axis_index("c").astype(jnp.int32).reshape(1)
    chip = (2 * lax.axis_index("x") + lax.axis_index("y")).astype(jnp.int32).reshape(1)
    exchange = WeightExchange({n: w[n][0] for n, *_ in BIG}, core, chip)
    small = {n: w[n] for n in SMALL}
    small["final_norm"] = final_norm.reshape(1, D_MODEL)

    loss, dx, gs = local_step(x[0], loss_target[0], small, exchange)

    grads, delta, new_m, new_v = {}, {}, {}, {}

    def update(group, core, after):
        reduced = exchange.reduced_halves(group, after)
        for i in GROUPS[group]:
            n, r, cc, ax = BIG[i]
            a, b = reduced[n]
            g, d, nm, nv = adamw_halves(w[n][0], a, b, m[n][0], v[n][0], core, r, cc, ax, f"adamw_{n}")
            grads[n], delta[n], new_m[n], new_v[n] = g[None], d[None], nm[None], nv[None]

    core_behind = core + exchange.token[0, :1].astype(jnp.int32)
    update("ffn2", core_behind, [exchange.token])
    update("mix", core_behind, [delta["ffn2_w_down"]])
    token = exchange.finish(after=[delta[BIG[i][0]] for group in ("ffn2", "mix") for i in GROUPS[group]])
    two_d = lambda a: a.reshape(1, D_MODEL) if a.ndim == 1 else a
    loss_sum, *small_out = small_step(loss, [gs[n] for n in SMALL], *[[two_d(p[n]) for n in SMALL] for p in (w, m, v)],
                                      behind=token)
    for result, parts in zip((grads, delta, new_m, new_v), small_out):
        result.update({n: a.reshape(w[n].shape) for n, a in zip(SMALL, parts)})
    update("ffn1", core, [loss_sum])

    return (loss_sum[0, 0], dx[None], *[grads[n] for n in WEIGHTS], *[delta[n] for n in WEIGHTS],
            *[new_m[n] for n in WEIGHTS], *[new_v[n] for n in WEIGHTS])
```

```python
import numpy as np
import jax
import jax.numpy as jnp
from jax import lax
from jax.experimental import pallas as pl
from jax.experimental.pallas import tpu as pltpu

SEQ = 2048
D_MODEL = 1024
D_FF = 2816
HG_HEADS = 4
HG_DIM = 128
HG_WIDTH = 512
HG_CHUNK = 64
ATT_GROUPS = ((128, 1), (512, 4), (2048, 16))
ATT_HEADS = 8
ATT_WIDTH = 512
ATT_BLOCK = 128
ALIBI_MAX = 8.0
IN_COLS = 8704
EPS = 1e-6
NEG_INF = -1e30
ADAM_LR = 0.001
ADAM_B1 = 0.9
ADAM_B2 = 0.999
ADAM_EPS = 1e-08
ADAM_WD = 0.01
ADAM_STEP = 10

N_CHIPS = 4
MXU_DTYPE = jnp.bfloat16
WEIGHT_COMM_DTYPE = jnp.bfloat16
GRAD_COMM_DTYPE = jnp.bfloat16
ACT_DTYPE = jnp.bfloat16
MESH = pl.DeviceIdType.MESH
F32 = jnp.float32
HIGHEST = lax.Precision.HIGHEST


def _sigmoid(x):
    return 1.0 / (1.0 + jnp.exp(-x))


def _dot(a, b, ta=False, tb=False):
    dn = (((0 if ta else 1,), (1 if tb else 0,)), ((), ()))
    return lax.dot_general(a.astype(MXU_DTYPE), b.astype(MXU_DTYPE), dn, preferred_element_type=F32)


def _dot_f32(a, b, ones_on_right=False):
    x = a if ones_on_right else b
    hi = x.astype(jnp.bfloat16)
    rest = x - hi.astype(F32)
    mid = rest.astype(jnp.bfloat16)
    lo = (rest - mid.astype(F32)).astype(jnp.bfloat16)
    if ones_on_right:
        dot = lambda q: jnp.dot(q, b.astype(jnp.bfloat16), preferred_element_type=F32)
    else:
        dot = lambda q: jnp.dot(a.astype(jnp.bfloat16), q, preferred_element_type=F32)
    return dot(hi) + (dot(mid) + dot(lo))


def _split_bf16(x):
    hi = x.astype(jnp.bfloat16)
    return hi, (x - hi.astype(F32)).astype(jnp.bfloat16)


def _hdot(a, b, ta=False, tb=False):
    dn =(((0 if ta else 1,), (1 if tb else 0,)), ((), ()))
    (a_hi, a_lo), (b_hi, b_lo) = _split_bf16(a), _split_bf16(b)
    dot = lambda p, q: lax.dot_general(p, q, dn, preferred_element_type=F32)
    return dot(a_hi, b_hi) + (dot(a_lo, b_hi) + dot(a_hi, b_lo))


MATMUL_VMEM_BYTES = 48 * 1024 * 1024
MATMUL_TILE_BYTES = 36 * 1024 * 1024
MXU_ALIGN = 128


def _divisors(n, most):
    return [t for t in range(min(n, most), 0, -MXU_ALIGN) if n % t == 0 and t % MXU_ALIGN == 0]


def _matmul_tiles(M, N, K, in_bytes, out_bytes, has_res):
    best = None
    for tk in _divisors(K, K):
        nk = K // tk
        for tm in _divisors(M, 2048):
            for tn in _divisors(N, 512):
                tiles = 2 * in_bytes * (tm * tk + tk * tn) + 2 * out_bytes * tm * tn
                tiles += 4 * tm * tn * ((nk > 1) + 2 * has_res)
                if tiles > MATMUL_TILE_BYTES:
                    continue
                traffic = in_bytes * (M * K * (1 if nk == 1 else N // tn) + K * N * (M // tm))
                key = (traffic, -tm * tn * tk)
                if best is None or key < best[0]:
                    best = (key, (tm, tn, tk))
    return best[1]


def matmul(a, b, *, ta=False, tb=False, out_dtype=F32, res=None, scale=1.0, behind=(), name):
    if ta:
        K, M = a.shape
    else:
        M, K = a.shape
    if tb:
        N, K2 = b.shape
    else:
        K2, N = b.shape
    assert K == K2 and a.dtype == b.dtype
    tm, tn, tk = _matmul_tiles(M, N, K, a.dtype.itemsize, jnp.dtype(out_dtype).itemsize, res is not None)
    nk = K // tk

    def finish(r, r_ref, o_ref):
        if scale != 1.0:
            r = r * scale
        if res is not None:
            r = r_ref[...] + r
        o_ref[...] = r.astype(out_dtype)

    def body(*refs):
        a_ref, b_ref = refs[:2]
        r_ref = refs[2] if res is not None else None
        o_ref = refs[2 + (res is not None) + len(behind)]
        if nk == 1:
            finish(_dot(a_ref[...], b_ref[...], ta, tb), r_ref, o_ref)
            return
        acc = refs[-1]
        k = pl.program_id(2)

        @pl.when(k == 0)
        def _():
            acc[...] = jnp.zeros_like(acc)

        acc[...] += _dot(a_ref[...], b_ref[...], ta, tb)

        @pl.when(k == nk - 1)
        def _():
            finish(acc[...], r_ref, o_ref)

    a_spec = pl.BlockSpec((tk, tm), lambda i, j, k: (k, i)) if ta else pl.BlockSpec((tm, tk), lambda i, j, k: (i, k))
    b_spec = pl.BlockSpec((tn, tk), lambda i, j, k: (j, k)) if tb else pl.BlockSpec((tk, tn), lambda i, j, k: (k, j))
    in_specs = [a_spec, b_spec]
    args = [a, b]
    if res is not None:
        in_specs.append(pl.BlockSpec((tm, tn), lambda i, j, k: (i, j)))
        args.append(res)
    for earlier in behind:
        in_specs.append(pl.BlockSpec(memory_space=pl.ANY))
        args.append(earlier)
    return pl.pallas_call(
        body, name=name, grid=(M // tm, N // tn, nk), in_specs=in_specs,
        out_specs=pl.BlockSpec((tm, tn), lambda i, j, k: (i, j)),
        out_shape=jax.ShapeDtypeStruct((M, N), out_dtype),
        scratch_shapes=[pltpu.VMEM((tm, tn), F32)] if nk > 1 else [],
        compiler_params=pltpu.CompilerParams(dimension_semantics=("parallel", "parallel", "arbitrary"),
                                             vmem_limit_bytes=MATMUL_VMEM_BYTES),
    )(*args)


ROW_TILE = 256


def rmsnorm_fwd(x, g, name, behind=()):
    def body(x_ref, g_ref, *refs):
        n_ref = refs[-1]
        xv = x_ref[...]
        r = lax.rsqrt(jnp.mean(xv * xv, axis=-1, keepdims=True) + EPS)
        n_ref[...] = ((xv * r) * g_ref[...]).astype(n_ref.dtype)

    order = list(behind)
    return pl.pallas_call(
        body, name=name, grid=(SEQ // ROW_TILE,),
        in_specs=[pl.BlockSpec((ROW_TILE, D_MODEL), lambda i: (i, 0)), pl.BlockSpec((1, D_MODEL), lambda i: (0, 0))]
        + [pl.BlockSpec(memory_space=pl.ANY)] * len(order),
        out_specs=pl.BlockSpec((ROW_TILE, D_MODEL), lambda i: (i, 0)),
        out_shape=jax.ShapeDtypeStruct((SEQ, D_MODEL), MXU_DTYPE),
    )(x, g, *order)


def rmsnorm_bwd(x, g, dn, dres, name):
    def body(x_ref, g_ref, dn_ref, dr_ref, dx_ref, dxm_ref, dg_ref):
        xv = x_ref[...]
        r = lax.rsqrt(jnp.mean(xv * xv, axis=-1, keepdims=True) + EPS)
        xh = xv * r
        dnv = dn_ref[...]

        @pl.when(pl.program_id(0) == 0)
        def _():
            dg_ref[...] = jnp.zeros_like(dg_ref)

        dg_ref[...] += jnp.sum(dnv * xh, axis=0, keepdims=True)
        dxh = dnv * g_ref[...]
        dx = dr_ref[...] + r * (dxh - xh * jnp.mean(dxh * xh, axis=-1, keepdims=True))
        dx_ref[...] = dx
        dxm_ref[...] = dx.astype(dxm_ref.dtype)

    row = pl.BlockSpec((ROW_TILE, D_MODEL), lambda i: (i, 0))
    vec = pl.BlockSpec((1, D_MODEL), lambda i: (0, 0))
    return pl.pallas_call(
        body, name=name, grid=(SEQ // ROW_TILE,), in_specs=[row, vec, row, row], out_specs=[row, row, vec],
        out_shape=[jax.ShapeDtypeStruct((SEQ, D_MODEL), F32), jax.ShapeDtypeStruct((SEQ, D_MODEL), MXU_DTYPE),
                   jax.ShapeDtypeStruct((1, D_MODEL), F32)],
        compiler_params=pltpu.CompilerParams(dimension_semantics=("arbitrary",)),
    )(x, g, dn, dres)


def final_norm_loss(h, g, target, name):
    def body(h_ref, g_ref, t_ref, dh_ref, dhm_ref, dg_ref, loss_ref):
        xv = h_ref[...]
        r = lax.rsqrt(jnp.mean(xv * xv, axis=-1, keepdims=True) + EPS)
        xh = xv * r
        gv = g_ref[...]
        e = xh * gv - t_ref[...]

        @pl.when(pl.program_id(0) == 0)
        def _():
            dg_ref[...] = jnp.zeros_like(dg_ref)
            loss_ref[...] = jnp.zeros_like(loss_ref)

        part = 0.5 * jnp.sum(jnp.sum(e * e, axis=-1, keepdims=True) * (1.0 / D_MODEL), axis=0, keepdims=True)
        loss_ref[...] += jnp.broadcast_to(part, loss_ref.shape)
        dout = e * (1.0 / D_MODEL)
        dg_ref[...] += jnp.sum(dout * xh, axis=0, keepdims=True)
        dxh = dout * gv
        dh = r * (dxh - xh * jnp.mean(dxh * xh, axis=-1, keepdims=True))
        dh_ref[...] = dh
        dhm_ref[...] = dh.astype(dhm_ref.dtype)

    row = pl.BlockSpec((ROW_TILE, D_MODEL), lambda i: (i, 0))
    vec = pl.BlockSpec((1, D_MODEL), lambda i: (0, 0))
    return pl.pallas_call(
        body, name=name, grid=(SEQ // ROW_TILE,), in_specs=[row, vec, row],
        out_specs=[row, row, vec, pl.BlockSpec((8, 128), lambda i: (0, 0))],
        out_shape=[jax.ShapeDtypeStruct((SEQ, D_MODEL), F32), jax.ShapeDtypeStruct((SEQ, D_MODEL), MXU_DTYPE),
                   jax.ShapeDtypeStruct((1, D_MODEL), F32), jax.ShapeDtypeStruct((8, 128), F32)],
        compiler_params=pltpu.CompilerParams(dimension_semantics=("arbitrary",)),
    )(h, g, target)


FF_TILE = D_FF // 2


def swiglu_fwd(gu, name):
    def body(a_ref, b_ref, s_ref):
        a = a_ref[...].astype(F32)
        s_ref[...] = (a * _sigmoid(a) * b_ref[...].astype(F32)).astype(s_ref.dtype)

    return pl.pallas_call(
        body, name=name, grid=(SEQ // ROW_TILE, 2),
        in_specs=[pl.BlockSpec((ROW_TILE, FF_TILE), lambda i, j: (i, j)),
                  pl.BlockSpec((ROW_TILE, FF_TILE), lambda i, j: (i, j + 2))],
        out_specs=pl.BlockSpec((ROW_TILE, FF_TILE), lambda i, j: (i, j)),
        out_shape=jax.ShapeDtypeStruct((SEQ, D_FF), MXU_DTYPE),
    )(gu, gu)


def swiglu_bwd(gu, ds, name):
    rows = ROW_TILE // 2

    def body(a_ref, b_ref, ds_ref, o_ref):
        a = a_ref[...].astype(F32)
        sg = _sigmoid(a)
        dsv = ds_ref[...].astype(F32)
        o_ref[:, :D_FF] = (dsv * b_ref[...].astype(F32) * (sg * (1.0 + a * (1.0 - sg)))).astype(o_ref.dtype)
        o_ref[:, D_FF:] = (dsv * a * sg).astype(o_ref.dtype)

    return pl.pallas_call(
        body, name=name, grid=(SEQ // rows,),
        in_specs=[pl.BlockSpec((rows, D_FF), lambda i: (i, 0)), pl.BlockSpec((rows, D_FF), lambda i: (i, 1)),
                  pl.BlockSpec((rows, D_FF), lambda i: (i, 0))],
        out_specs=pl.BlockSpec((rows, 2 * D_FF), lambda i: (i, 0)),
        out_shape=jax.ShapeDtypeStruct((SEQ, 2 * D_FF), MXU_DTYPE), compiler_params=SUM_PARAMS,
    )(gu, gu, ds)


GATE_HG_BLK = 6656 // 512
GATE_ATT_BLK = 7680 // 512


def merge_fwd(z, bh, ba, name):
    def body(gh_ref, ga_ref, bh_ref, ba_ref, o_ref):
        o_ref[...] = (_sigmoid(gh_ref[...]) * bh_ref[...] + _sigmoid(ga_ref[...]) * ba_ref[...]).astype(o_ref.dtype)

    blk = pl.BlockSpec((ROW_TILE, 512), lambda i, j: (i, j))
    return pl.pallas_call(
        body, name=name, grid=(SEQ // ROW_TILE, 2),
        in_specs=[pl.BlockSpec((ROW_TILE, 512), lambda i, j: (i, GATE_HG_BLK + j)),
                  pl.BlockSpec((ROW_TILE, 512), lambda i, j: (i, GATE_ATT_BLK + j)), blk, blk],
        out_specs=blk, out_shape=jax.ShapeDtypeStruct((SEQ, D_MODEL), MXU_DTYPE),
    )(z, z, bh, ba)


def merge_bwd(z, bh, ba, dm, name):
    def body(gh_ref, ga_ref, bh_ref, ba_ref, dm_ref, dbh_ref, dba_ref, dgh_ref, dga_ref):
        dmv = dm_ref[...]
        sh = _sigmoid(gh_ref[...])
        sa = _sigmoid(ga_ref[...])
        dbh_ref[...] = (dmv * sh).astype(dbh_ref.dtype)
        dba_ref[...] = (dmv * sa).astype(dba_ref.dtype)
        dgh_ref[...] = (dmv * bh_ref[...] * (sh * (1.0 - sh))).astype(dgh_ref.dtype)
        dga_ref[...] = (dmv * ba_ref[...] * (sa * (1.0 - sa))).astype(dga_ref.dtype)

    blk = pl.BlockSpec((ROW_TILE, 512), lambda i, j: (i, j))
    out = jax.ShapeDtypeStruct((SEQ, D_MODEL), MXU_DTYPE)
    return pl.pallas_call(
        body, name=name, grid=(SEQ // ROW_TILE, 2),
        in_specs=[pl.BlockSpec((ROW_TILE, 512), lambda i, j: (i, GATE_HG_BLK + j)),
                  pl.BlockSpec((ROW_TILE, 512), lambda i, j: (i, GATE_ATT_BLK + j)), blk, blk, blk],
        out_specs=[blk, blk, blk, blk], out_shape=[out, out, out, out],
    )(z, z, bh, ba, dm)


N_CHUNKS = SEQ // HG_CHUNK
HG_STEP_CHUNKS = 4


def _hgrn_gates(q, fp, lb):
    C = HG_CHUNK
    sg = _sigmoid(fp)
    f = lb + (1.0 - lb) * sg
    lf = jnp.log(f)
    row = lax.broadcasted_iota(jnp.int32, (C, C), 0)
    col = lax.broadcasted_iota(jnp.int32, (C, C), 1)
    causal = row >= col
    G = _dot_f32(causal.astype(F32), lf)
    eG = jnp.exp(G)
    enG = jnp.exp(-G)
    qg = q * eG
    kg = (1.0 - f) * enG
    A = jnp.where(causal, _hdot(qg, kg, tb=True), 0.0)
    egl = jnp.exp(jnp.sum(lf, axis=0, keepdims=True))
    return sg, f, causal, eG, enG, qg, kg, A, egl


def hgrn_fwd(z, lb, gain, name):
    C, K = HG_CHUNK, HG_DIM

    def body(q_ref, f_ref, v_ref, og_ref, p_ref, g_ref, y_ref, o_ref, st_ref, state):
        @pl.when(pl.program_id(0) == 0)
        def _():
            state[...] = jnp.zeros_like(state)

        for cc in range(HG_STEP_CHUNKS):
            rows = pl.ds(cc * C, C)
            for h in range(HG_HEADS):
                hd = pl.ds(h * K, K)
                v = v_ref[rows, hd]
                _, _, _, _, _, qg, kg, A, egl = _hgrn_gates(q_ref[rows, hd], f_ref[rows, hd], p_ref[:, hd])
                st = state[h]
                st_ref[h, cc] = st
                o = _hdot(A, v) + _hdot(qg, st, tb=True)
                state[h] = st * egl + _hdot(v, kg * egl, ta=True)
                o_ref[rows, hd] = o
                rs = lax.rsqrt(jnp.mean(o * o, axis=-1, keepdims=True) + EPS)
                og = og_ref[rows, hd]
                y_ref[rows, hd] = (((o * rs) * g_ref[:, hd]) * (og * _sigmoid(og))).astype(y_ref.dtype)

    R = HG_STEP_CHUNKS * C

    def zcol(section):
        return pl.BlockSpec((R, HG_WIDTH), lambda c: (c, section))

    vec = pl.BlockSpec((1, HG_WIDTH), lambda c: (0, 0))
    blk = pl.BlockSpec((R, HG_WIDTH), lambda c: (c, 0))
    return pl.pallas_call(
        body, name=name, grid=(N_CHUNKS // HG_STEP_CHUNKS,),
        in_specs=[zcol(0), zcol(1), zcol(2), zcol(3), vec, vec],
        out_specs=[blk, blk, pl.BlockSpec((HG_HEADS, HG_STEP_CHUNKS, K, K), lambda c: (0, c, 0, 0))],
        out_shape=[jax.ShapeDtypeStruct((SEQ, HG_WIDTH), MXU_DTYPE), jax.ShapeDtypeStruct((SEQ, HG_WIDTH), F32),
                   jax.ShapeDtypeStruct((HG_HEADS, N_CHUNKS, K, K), F32)],
        scratch_shapes=[pltpu.VMEM((HG_HEADS, K, K), F32)],
        compiler_params=pltpu.CompilerParams(dimension_semantics=("arbitrary",)),
    )(z, z, z, z, lb, gain)


def hgrn_bwd(z, lb, gain, o_raw, states, dy, name):
    C, K = HG_CHUNK, HG_DIM

    def body(q_ref, f_ref, v_ref, og_ref, p_ref, g_ref, o_ref, st_ref, dy_ref,
             dq_ref, dfp_ref, dv_ref, dog_ref, dlb_ref, dgain_ref, dstate):
        @pl.when(pl.program_id(0) == 0)
        def _():
            dstate[...] = jnp.zeros_like(dstate)
            dlb_ref[...] = jnp.zeros_like(dlb_ref)
            dgain_ref[...] = jnp.zeros_like(dgain_ref)

        last = lax.broadcasted_iota(jnp.int32, (C, K), 0) == C - 1
        row = lax.broadcasted_iota(jnp.int32, (C, C), 0)
        col = lax.broadcasted_iota(jnp.int32, (C, C), 1)
        anti_causal = (col >= row).astype(F32)
        for cc in reversed(range(HG_STEP_CHUNKS)):
            rows = pl.ds(cc * C, C)
            for h in range(HG_HEADS):
                hd = pl.ds(h * K, K)
                v = v_ref[rows, hd]
                lb = p_ref[:, hd]
                sg, f, causal, eG, enG, qg, kg, A, egl = _hgrn_gates(q_ref[rows, hd], f_ref[rows, hd], lb)
                kd = kg * egl
                st = st_ref[h, cc]
                dst = dstate[h]
                o = o_ref[rows, hd]
                og = og_ref[rows, hd]
                gain_v = g_ref[:, hd]
                dyv = dy_ref[rows, hd]
                rs = lax.rsqrt(jnp.mean(o * o, axis=-1, keepdims=True) + EPS)
                on = o * rs
                sgo = _sigmoid(og)
                silu = og * sgo
                dog_ref[rows, hd] = (dyv * (on * gain_v) * (sgo * (1.0 + og * (1.0 - sgo)))).astype(dog_ref.dtype)
                dgain_ref[:, hd] += jnp.sum(dyv * silu * on, axis=0, keepdims=True)
                don = dyv * gain_v * silu
                do = rs * (don - on * jnp.mean(don * on, axis=-1, keepdims=True))
                dA = jnp.where(causal, _hdot(do, v, tb=True), 0.0)
                dv_ref[rows, hd] = (_hdot(A, do, ta=True) + _hdot(kd, dst, tb=True)).astype(dv_ref.dtype)
                dqg = _hdot(dA, kg) + _hdot(do, st)
                dkg = _hdot(dA, qg, ta=True)
                dkd = _hdot(v, dst)
                dstate[h] = dst * egl + _hdot(do, qg, ta=True)
                dgl = jnp.sum(st * dst, axis=0, keepdims=True) * egl
                dq_ref[rows, hd] = (dqg * eG).astype(dq_ref.dtype)
                dk = dkg * enG + dkd * (enG * egl)
                dG = dqg * qg - dkg * kg - dkd * kd
                extra = jnp.sum(dkd * kd, axis=0, keepdims=True) + dgl
                dG = dG + jnp.where(last, extra, 0.0)
                dlf = _dot_f32(anti_causal, dG)
                df = dlf / f - dk
                dfp_ref[rows, hd] = (df * (1.0 - lb) * (sg * (1.0 - sg))).astype(dfp_ref.dtype)
                dlb_ref[:, hd] += jnp.sum(df * (1.0 - sg), axis=0, keepdims=True)

    R = HG_STEP_CHUNKS * C
    n_steps = N_CHUNKS // HG_STEP_CHUNKS

    def rc(c):
        return n_steps - 1 - c

    def zcol(section):
        return pl.BlockSpec((R, HG_WIDTH), lambda c: (rc(c), section))

    vec = pl.BlockSpec((1, HG_WIDTH), lambda c: (0, 0))
    blk = pl.BlockSpec((R, HG_WIDTH), lambda c: (rc(c), 0))
    out = jax.ShapeDtypeStruct((SEQ, HG_WIDTH), MXU_DTYPE)
    small = jax.ShapeDtypeStruct((1, HG_WIDTH), F32)
    return pl.pallas_call(
        body, name=name, grid=(n_steps,),
        in_specs=[zcol(0), zcol(1), zcol(2), zcol(3), vec, vec, blk,
                  pl.BlockSpec((HG_HEADS, HG_STEP_CHUNKS, K, K), lambda c: (0, rc(c), 0, 0)), blk],
        out_specs=[blk, blk, blk, blk, vec, vec],
        out_shape=[out, out, out, out, small, small],
        scratch_shapes=[pltpu.VMEM((HG_HEADS, K, K), F32)],
        compiler_params=pltpu.CompilerParams(dimension_semantics=("arbitrary",)),
    )(z, z, z, z, lb, gain, o_raw, states, dy)


N_GROUPS = len(ATT_GROUPS)
HEAD_PAIRS = ATT_WIDTH // 128
ATT_COL0 = 4 * HG_WIDTH
UNROLLED_UNITS = 4
ATT_SLAB_BLOCKS = 4


def _alibi_coef():
    n = N_GROUPS * ATT_HEADS
    slopes = np.exp2(-ALIBI_MAX * np.arange(1, n + 1, dtype=np.float32) / n).astype(np.float32)
    dil = np.repeat(np.array([d for _, d in ATT_GROUPS], np.float32), ATT_HEADS)
    return jnp.asarray(slopes * dil, F32)


def _for_each_unit(n, fn):
    if n <= UNROLLED_UNITS:
        for u in range(n):
            fn(u)
    else:
        def group(i, carry):
            for j in range(UNROLLED_UNITS):
                fn(i * UNROLLED_UNITS + j)
            return carry
        lax.fori_loop(0, n // UNROLLED_UNITS, group, 0)


def _att_specs(g):
    B = ATT_BLOCK
    d = ATT_GROUPS[g][1]
    blocks = ATT_SLAB_BLOCKS if d == 1 else 1
    R = B * d * blocks
    n_slabs = SEQ // R
    multi = SEQ // d > B
    col0 = (ATT_COL0 + g * 3 * ATT_WIDTH) // 128

    def cur(col):
        return pl.BlockSpec((R, 128), lambda hp, s: (s, col + hp))

    def prev(col):
        return pl.BlockSpec((R, 128), lambda hp, s: (jnp.maximum(s - 1, 0), col + hp))

    def nxt(col):
        return pl.BlockSpec((R, 128), lambda hp, s: (jnp.minimum(s + 1, n_slabs - 1), col + hp))

    def unit(u, s):
        if d > 1:
            rows = pl.ds(u, B, stride=d)
            return rows, False, rows, jnp.where(s == 0, B, 0), False, rows, jnp.where(s == n_slabs - 1, B, 0)
        rows = pl.ds(u * B, B)
        inner_prev, inner_next = u > 0, u < blocks - 1
        return (rows, inner_prev, pl.ds((u - 1) * B if inner_prev else (blocks - 1) * B, B),
                0 if inner_prev else jnp.where(s == 0, B, 0),
                inner_next, pl.ds((u + 1) * B if inner_next else 0, B),
                0 if inner_next else jnp.where(s == n_slabs - 1, B, 0))

    return d * blocks, R, n_slabs, multi, col0, cur, prev, nxt, unit


def _head_lanes(j):
    lane = lax.broadcasted_iota(jnp.int32, (ATT_BLOCK, 128), 1)
    return (lane >= 64 * j) & (lane < 64 * (j + 1))


def _lane_value(x, sel):
    return jnp.max(jnp.where(sel, x, -3e38), axis=-1, keepdims=True)


def _stack_heads(x, sel0):
    return jnp.concatenate([jnp.where(sel0, x, 0.0), jnp.where(sel0, 0.0, x)], axis=0)


def _stack_values(x, sel0, lanes):
    swapped = pltpu.roll(x, 64, 1)
    stacked = jnp.concatenate([jnp.where(sel0, x, swapped), jnp.where(sel0, swapped, x)], axis=0)
    return stacked if lanes == 128 else jnp.concatenate([stacked] * (lanes // 128), axis=1)


def _pair_coef(coef_ref, g, hp):
    row = lax.broadcasted_iota(jnp.int32, (2 * ATT_BLOCK, 1), 0)
    first = g * ATT_HEADS + hp * 2
    return jnp.where(row < ATT_BLOCK, coef_ref[first], coef_ref[first + 1])


def _band(with_prev, first_key):
    B = ATT_BLOCK
    keys = 2 * B if with_prev else B
    qi = jnp.bitwise_and(lax.broadcasted_iota(jnp.int32, (2 * B, keys), 0), B - 1)
    kj = lax.broadcasted_iota(jnp.int32, (2 * B, keys), 1)
    delta = qi + (B if with_prev else 0) - kj
    valid = (delta >= 0) & (delta <= B)
    if with_prev:
        valid = valid & (kj >= first_key)
    return valid, delta.astype(F32)


def _band_next(first_key):
    B = ATT_BLOCK
    qi = jnp.bitwise_and(lax.broadcasted_iota(jnp.int32, (2 * B, B), 0), B - 1)
    kj = lax.broadcasted_iota(jnp.int32, (2 * B, B), 1)
    delta = qi + B - kj
    return (delta <= B) & (kj >= first_key), delta.astype(F32)


def att_fwd(z, g, name):
    B = ATT_BLOCK
    n_units, R, n_slabs, has_prev, col0, cur, prev, _, unit = _att_specs(g)

    def body(coef_ref, *refs):
        if has_prev:
            q_ref, kc_ref, vc_ref, kp_ref, vp_ref, o_ref, l_ref = refs
        else:
            q_ref, kc_ref, vc_ref, o_ref, l_ref = refs
        hp, s = pl.program_id(0), pl.program_id(1)
        cf2 = _pair_coef(coef_ref, g, hp)
        sel0 = _head_lanes(0)

        def one(u):
            rows, inner_prev, prev_rows, first_key, _, _, _ = unit(u, s)
            valid, dist = _band(has_prev, first_key)
            q2 = _stack_heads(q_ref[rows, :], sel0)
            kk, vv = kc_ref[rows, :], vc_ref[rows, :]
            if has_prev:
                k_from, v_from = (kc_ref, vc_ref) if inner_prev else (kp_ref, vp_ref)
                kk = jnp.concatenate([k_from[prev_rows, :], kk], axis=0)
                vv = jnp.concatenate([v_from[prev_rows, :], vv], axis=0)
            sc = jnp.where(valid, _dot(q2, kk, tb=True) * 0.125 - cf2 * dist, NEG_INF)
            mx = jnp.max(sc, axis=-1, keepdims=True)
            e = jnp.exp(sc - mx)
            den = jnp.sum(e, axis=-1, keepdims=True)
            o2 = _dot(e * (1.0 / den), vv)
            lse2 = mx + jnp.log(den)
            o_ref[rows, :] = jnp.where(sel0, o2[:B], o2[B:])
            l_ref[rows, :] = jnp.where(sel0, lse2[:B], lse2[B:])

        _for_each_unit(n_units, one)

    in_specs = [pl.BlockSpec(memory_space=pltpu.SMEM), cur(col0), cur(col0 + 4), cur(col0 + 8)]
    args = [_alibi_coef(), z, z, z]
    if has_prev:
        in_specs += [prev(col0 + 4), prev(col0 + 8)]
        args += [z, z]
    out = jax.ShapeDtypeStruct((SEQ, ATT_WIDTH), F32)
    return pl.pallas_call(
        body, name=name, grid=(HEAD_PAIRS, n_slabs), in_specs=in_specs,
        out_specs=[cur(0), cur(0)], out_shape=[out, out],
        compiler_params=pltpu.CompilerParams(dimension_semantics=("parallel", "arbitrary")),
    )(*args)


def att_bwd(z, l, do, corr, g, name):
    B = ATT_BLOCK
    d = ATT_GROUPS[g][1]
    n_blocks = SEQ // (d * B)
    multi = n_blocks > 1
    col0 = (ATT_COL0 + g * 3 * ATT_WIDTH) // 128
    own = slice(B, 2 * B) if multi else slice(0, B)

    def body(coef_ref, q_ref, k_ref, v_ref, l_ref, do_ref, cr_ref, dq_ref, dk_ref, dv_ref, dq_sc, dk_sc, dv_sc):
        hp = pl.program_id(0)
        cf2 = _pair_coef(coef_ref, g, hp)
        sel0 = _head_lanes(0)

        def block_rows(b, r):
            return pl.ds(b * (B * d) + r, B, stride=d) if d > 1 else pl.ds(pl.multiple_of(b * B, B), B)

        def one(u):
            b, r = (u, 0) if d == 1 else (u // d, u % d)
            rows = block_rows(b, r)
            valid, dist = _band(multi, jnp.where(b == 0, B, 0))
            kk, vv = k_ref[rows, :], v_ref[rows, :]
            if multi:
                prev_rows = block_rows(jnp.maximum(b - 1, 0), r)
                kk = jnp.concatenate([k_ref[prev_rows, :], kk], axis=0)
                vv = jnp.concatenate([v_ref[prev_rows, :], vv], axis=0)
            q2, do2 = _stack_heads(q_ref[rows, :], sel0), _stack_heads(do_ref[rows, :], sel0)
            keys = kk.shape[0]
            lse2, cr2 = _stack_values(l_ref[rows, :], sel0, keys), _stack_values(cr_ref[rows, :], sel0, keys)
            p = jnp.exp(jnp.where(valid, _dot(q2, kk, tb=True) * 0.125 - cf2 * dist, NEG_INF) - lse2)
            ds = p * (_dot(do2, vv, tb=True) + cr2)
            dq2 = _dot(ds, kk)
            dkk = _dot(ds, q2, ta=True) * 0.125
            dvv = _dot(p, do2, ta=True)
            dq_sc[rows, :] = jnp.where(sel0, dq2[:B], dq2[B:]) * 0.125
            dk_sc[rows, :] = dkk[own]
            dv_sc[rows, :] = dvv[own]
            if multi:
                dk_sc[prev_rows, :] += dkk[:B]
                dv_sc[prev_rows, :] += dvv[:B]

        _for_each_unit(d * n_blocks, one)
        dq_ref[...] = dq_sc[...].astype(dq_ref.dtype)
        dk_ref[...] = dk_sc[...].astype(dk_ref.dtype)
        dv_ref[...] = dv_sc[...].astype(dv_ref.dtype)

    def col(c):
        return pl.BlockSpec((SEQ, 128), lambda hp: (0, c + hp))

    out = jax.ShapeDtypeStruct((SEQ, ATT_WIDTH), MXU_DTYPE)
    return pl.pallas_call(
        body, name=name, grid=(HEAD_PAIRS,),
        in_specs=[pl.BlockSpec(memory_space=pltpu.SMEM), col(col0), col(col0 + 4), col(col0 + 8), col(0), col(0), col(0)],
        out_specs=[col(0)] * 3, out_shape=[out] * 3,
        scratch_shapes=[pltpu.VMEM((SEQ, 128), F32)] * 3,
        compiler_params=pltpu.CompilerParams(dimension_semantics=("parallel",), vmem_limit_bytes=MATMUL_VMEM_BYTES),
    )(_alibi_coef(), z, z, z, l, do, corr)


def _head_sum(x):
    i = lax.broadcasted_iota(jnp.int32, (128, 128), 0) // 64
    j = lax.broadcasted_iota(jnp.int32, (128, 128), 1) // 64
    return _dot_f32(x, (i == j).astype(F32), ones_on_right=True)


def _group_weights(l0, l1, l2):
    mx = jnp.maximum(jnp.maximum(l0, l1), l2)
    e0, e1, e2 = jnp.exp(l0 - mx), jnp.exp(l1 - mx), jnp.exp(l2 - mx)
    inv = 1.0 / (e0 + e1 + e2)
    return e0 * inv, e1 * inv, e2 * inv


def att_combine_fwd(o, l, name):
    def body(o0, o1, o2, l0, l1, l2, y_ref):
        w0, w1, w2 = _group_weights(l0[...], l1[...], l2[...])
        y_ref[...] = (o0[...] * w0 + o1[...] * w1 + o2[...] * w2).astype(y_ref.dtype)

    blk = pl.BlockSpec((ROW_TILE, ATT_WIDTH), lambda i: (i, 0))
    return pl.pallas_call(
        body, name=name, grid=(SEQ // ROW_TILE,), in_specs=[blk] * 6, out_specs=blk,
        out_shape=jax.ShapeDtypeStruct((SEQ, ATT_WIDTH), MXU_DTYPE),
    )(*o, *l)


def att_combine_bwd(o, l, dy, name):
    def body(o0, o1, o2, l0, l1, l2, dy_ref, do0, do1, do2, cr0, cr1, cr2):
        w = _group_weights(l0[...], l1[...], l2[...])
        dyv = dy_ref[...]
        tot = _head_sum(dyv * (w[0] * o0[...] + w[1] * o1[...] + w[2] * o2[...]))
        for g, (do_ref, cr_ref) in enumerate(((do0, cr0), (do1, cr1), (do2, cr2))):
            do_ref[...] = dyv * w[g]
            cr_ref[...] = -w[g] * tot

    blk = pl.BlockSpec((ROW_TILE, 128), lambda i, j: (i, j))
    out = jax.ShapeDtypeStruct((SEQ, ATT_WIDTH), F32)
    res = pl.pallas_call(
        body, name=name, grid=(SEQ // ROW_TILE, HEAD_PAIRS), in_specs=[blk] * 7, out_specs=[blk] * 6, out_shape=[out] * 6,
    )(*o, *l, dy)
    return res[:N_GROUPS], res[N_GROUPS:]


SUM_ROW_TILES = (1024, 512, 256, 128, 64, 32, 16)
SUM_TILE_BYTES = 24 * 1024 * 1024
SUM_PARAMS = pltpu.CompilerParams(vmem_limit_bytes=MATMUL_VMEM_BYTES)


def _row_tile(rows, cols, operands):
    fit = [t for t in SUM_ROW_TILES if rows % t == 0]
    return next((t for t in fit if 2 * 4 * operands * t * cols <= SUM_TILE_BYTES), fit[-1])


def _shard_shape(rows, cols, axis):
    return (rows // N_CHIPS, cols) if axis == 0 else (rows, cols // N_CHIPS)


def _half_shape(rows, cols, axis):
    return (rows, cols // 2) if axis == 0 else (rows // 2, cols)


def _piece_shape(rows, cols, axis):
    return (rows // N_CHIPS, cols // 2) if axis == 0 else (rows // 2, cols // N_CHIPS)


def place_own_block(shard, chip, rows, cols, axis, name):
    sr, sc = _shard_shape(rows, cols, axis)
    tr = _row_tile(sr, sc, 2)

    def body(chip_ref, s_ref, o_ref):
        o_ref[...] = s_ref[...].astype(o_ref.dtype)

    if axis == 0:
        out_map = lambda i, chip_ref: (chip_ref[0] * (sr // tr) + i, 0)
    else:
        out_map = lambda i, chip_ref: (i, chip_ref[0])
    return pl.pallas_call(
        body, name=name, out_shape=jax.ShapeDtypeStruct((rows, cols), WEIGHT_COMM_DTYPE), compiler_params=SUM_PARAMS,
        grid_spec=pltpu.PrefetchScalarGridSpec(
            num_scalar_prefetch=1, grid=(sr // tr,), in_specs=[pl.BlockSpec((tr, sc), lambda i, chip_ref: (i, 0))],
            out_specs=pl.BlockSpec((tr, sc), out_map)),
    )(chip, shard)


def add_halves(g, theirs, core, rows, cols, axis, name):
    hr, hc = _half_shape(rows, cols, axis)
    tr = _row_tile(hr, hc, 3)

    def body(core_ref, g_ref, t_ref, o_ref):
        o_ref[...] = (g_ref[...].astype(F32) + t_ref[...].astype(F32)).astype(o_ref.dtype)

    if axis == 0:
        g_map = lambda i, core_ref: (i, core_ref[0])
    else:
        g_map = lambda i, core_ref: (core_ref[0] * (hr // tr) + i, 0)
    blk = pl.BlockSpec((tr, hc), lambda i, core_ref: (i, 0))
    return pl.pallas_call(
        body, name=name, out_shape=jax.ShapeDtypeStruct((hr, hc), GRAD_COMM_DTYPE), compiler_params=SUM_PARAMS,
        grid_spec=pltpu.PrefetchScalarGridSpec(
            num_scalar_prefetch=1, grid=(hr // tr,), in_specs=[pl.BlockSpec((tr, hc), g_map), blk], out_specs=blk),
    )(core, g, theirs)


def add_pieces(half, got, chip, rows, cols, axis, name):
    hr, _ = _half_shape(rows, cols, axis)
    pr, pc = _piece_shape(rows, cols, axis)
    tr = _row_tile(pr, pc, 5)

    def body(chip_ref, h_ref, got_ref, o_ref):
        o_ref[...] = (h_ref[...].astype(F32) + got_ref[0].astype(F32) + got_ref[1].astype(F32) + got_ref[2].astype(F32))

    if axis == 0:
        h_map = lambda i, chip_ref: (chip_ref[0] * (pr // tr) + i, 0)
    else:
        h_map = lambda i, chip_ref: (i, chip_ref[0])
    return pl.pallas_call(
        body, name=name, out_shape=jax.ShapeDtypeStruct((pr, pc), F32), compiler_params=SUM_PARAMS,
        grid_spec=pltpu.PrefetchScalarGridSpec(
            num_scalar_prefetch=1, grid=(pr // tr,),
            in_specs=[pl.BlockSpec((tr, pc), h_map), pl.BlockSpec((3, tr, pc), lambda i, chip_ref: (0, i, 0))],
            out_specs=pl.BlockSpec((tr, pc), lambda i, chip_ref: (i, 0))),
    )(chip, half, got)


def _adamw_math(w, g, m, v):
    nm = ADAM_B1 * m + (1.0 - ADAM_B1) * g
    nv = ADAM_B2 * v + (1.0 - ADAM_B2) * (g * g)
    m_hat = nm / (1.0 - ADAM_B1 ** ADAM_STEP)
    v_hat = nv / (1.0 - ADAM_B2 ** ADAM_STEP)
    return -ADAM_LR * (m_hat / (jnp.sqrt(v_hat) + ADAM_EPS) + ADAM_WD * w), nm, nv


def adamw_halves(w, mine, theirs, m, v, core, rows, cols, axis, name):
    sr, sc = _shard_shape(rows, cols, axis)
    pr, pc = _piece_shape(rows, cols, axis)
    tr = _row_tile(pr, pc, 9)
    nt = pr // tr

    def body(core_ref, w_ref, a_ref, b_ref, m_ref, v_ref, g_ref, d_ref, nm_ref, nv_ref):
        g = jnp.where(pl.program_id(0) == core_ref[0], a_ref[...], b_ref[...])
        g_ref[...] = g
        d_ref[...], nm_ref[...], nv_ref[...] = _adamw_math(w_ref[...], g, m_ref[...], v_ref[...])

    if axis == 0:
        full = pl.BlockSpec((tr, pc), lambda h, i, core_ref: (i, h))
    else:
        full = pl.BlockSpec((tr, pc), lambda h, i, core_ref: (h * nt + i, 0))
    part = pl.BlockSpec((tr, pc), lambda h, i, core_ref: (i, 0))
    out = jax.ShapeDtypeStruct((sr, sc), F32)
    return pl.pallas_call(
        body, name=name, out_shape=[out, out, out, out], compiler_params=SUM_PARAMS,
        grid_spec=pltpu.PrefetchScalarGridSpec(
            num_scalar_prefetch=1, grid=(2, nt), in_specs=[full, part, part, full, full], out_specs=[full] * 4),
    )(core, w, mine, theirs, m, v)


BIG = (
    ("ffn1_w_gate_up", D_MODEL, 2 * D_FF, 1),
    ("ffn1_w_down", D_FF, D_MODEL, 0),
    ("w_in", D_MODEL, IN_COLS, 1),
    ("w_branch_hg", HG_WIDTH, D_MODEL, 1),
    ("w_branch_att", ATT_WIDTH, D_MODEL, 1),
    ("w_out", D_MODEL, D_MODEL, 0),
    ("ffn2_w_gate_up", D_MODEL, 2 * D_FF, 1),
    ("ffn2_w_down", D_FF, D_MODEL, 0),
)
N_BIG = len(BIG)
ANY = pl.BlockSpec(memory_space=pl.ANY)


def _place():
    return lax.axis_index("x"), lax.axis_index("y"), lax.axis_index("c")


def _other_chips(x, y):
    return ((1 - x, y), (x, 1 - y), (1 - x, 1 - y))


MAX_COPY_CHUNKS = 16
CHUNK_ROW_ALIGN = 16


def _row_chunks(view):
    rows = view.shape[0]
    n = next(n for n in range(MAX_COPY_CHUNKS, 0, -1) if rows % (CHUNK_ROW_ALIGN * n) == 0 or n == 1)
    step = rows // n
    return [pl.ds(i * step, step) for i in range(n)]


def _remote(src, dst, send_sem, recv_sem, device):
    return pltpu.make_async_remote_copy(src_ref=src, dst_ref=dst, send_sem=send_sem, recv_sem=recv_sem,
                                        device_id=device, device_id_type=MESH)


def _start_remote(src, dst, send_sem, recv_sem, device):
    for rows in _row_chunks(src):
        _remote(src.at[rows, :], dst.at[rows, :], send_sem, recv_sem, device).start()
    return _remote(src, dst, send_sem, recv_sem, device)


HBM = pl.BlockSpec(memory_space=pltpu.HBM)
SEM = pl.BlockSpec(memory_space=pltpu.SEMAPHORE)
SPLIT_COPY_EFFECT = pltpu.SideEffectType.DATAFLOW_SIDE_EFFECTING
GROUPS = {"ffn1": (0, 1), "mix": (2, 3, 4, 5), "ffn2": (6, 7)}


def _in_hbm(a):
    return pltpu.with_memory_space_constraint(a, pltpu.HBM)


class _SemList:
    def __init__(self, refs):
        self.refs = refs
        self.at = self

    def __getitem__(self, index):
        w, k = index
        return self.refs[3 * w + k]


def _gather_piece(ref, rows, cols, axis, chip, c):
    sr, sc = _shard_shape(rows, cols, axis)
    j = 2 * chip[0] + chip[1]
    if axis == 0:
        return ref.at[pl.ds(j * sr + c * (sr // 2), sr // 2), :]
    return ref.at[pl.ds(c * (sr // 2), sr // 2), pl.ds(pl.multiple_of(j * sc, 128), sc)]


def _start_gather_sends(bufs, ws, send_sems, recv_sems):
    x, y, c = _place()
    for w, (_, r, cc, ax) in enumerate(ws):
        mine = _gather_piece(bufs[w], r, cc, ax, (x, y), c)
        for k, chip in enumerate(_other_chips(x, y)):
            _start_remote(mine, mine, send_sems.at[w, k], recv_sems.at[w, k], (*chip, c))


def _wait_gather_sends(bufs, ws, send_sems, recv_sems):
    x, y, c = _place()
    for w, (_, r, cc, ax) in enumerate(ws):
        for k, chip in enumerate(_other_chips(x, y)):
            got = _gather_piece(bufs[w], r, cc, ax, chip, c)
            _remote(got, got, send_sems.at[w, k], recv_sems.at[w, k], (x, y, c)).wait_recv()
    for w, (_, r, cc, ax) in enumerate(ws):
        mine = _gather_piece(bufs[w], r, cc, ax, (x, y), c)
        for k in range(3):
            _remote(mine, mine, send_sems.at[w, k], recv_sems.at[w, k], (x, y, c)).wait_send()


def _forward_halves(bufs, ws, send_sems, recv_sems):
    x, y, c = _place()
    passed = []
    for w, (_, r, cc, ax) in enumerate(ws):
        for k, chip in enumerate(_other_chips(x, y)):
            got = _gather_piece(bufs[w], r, cc, ax, chip, c)
            passed.append(_start_remote(got, got, send_sems.at[w, k], recv_sems.at[w, k], (x, y, 1 - c)))
    for w, (_, r, cc, ax) in enumerate(ws):
        for k, chip in enumerate(_other_chips(x, y)):
            got = _gather_piece(bufs[w], r, cc, ax, chip, 1 - c)
            _remote(got, got, send_sems.at[w, k], recv_sems.at[w, k], (x, y, c)).wait_recv()
    for cp in passed:
        cp.wait_send()


def gather_start(placed, after, group):
    ws = [BIG[i] for i in GROUPS[group]]
    n = len(ws)

    def body(*refs):
        bufs = refs[:n]
        send_sems, recv_sems = _SemList(refs[n + 1:4 * n + 1]), _SemList(refs[4 * n + 1:7 * n + 1])
        token = refs[-1]
        _start_gather_sends(bufs, ws, send_sems, recv_sems)
        token[...] = jnp.zeros_like(token)

    out = pl.pallas_call(
        body, name=f"gather_start_{group}", in_specs=[HBM] * n + [ANY],
        out_specs=[SEM] * (6 * n) + [HBM] * n + [pl.BlockSpec(memory_space=pltpu.VMEM)],
        out_shape=[pltpu.SemaphoreType.DMA(())] * (6 * n)
        + [pltpu.HBM((r, cc), WEIGHT_COMM_DTYPE) for _, r, cc, _ in ws] + [jax.ShapeDtypeStruct((8, 128), F32)],
        input_output_aliases={w: 6 * n + w for w in range(n)},
        compiler_params=pltpu.CompilerParams(has_side_effects=SPLIT_COPY_EFFECT),
    )(*[_in_hbm(p) for p in placed], after)
    return out[:3 * n], out[3 * n:6 * n], out[6 * n:7 * n], out[-1]


def gather_wait(bufs, send_sems, recv_sems, after, group):
    ws = [BIG[i] for i in GROUPS[group]]
    n = len(ws)

    def body(*refs):
        _wait_gather_sends(refs[:n], ws, _SemList(refs[n:n + 3 * n]), _SemList(refs[n + 3 * n:n + 6 * n]))

    return pl.pallas_call(
        body, name=f"gather_wait_{group}", in_specs=[HBM] * n + [SEM] * (6 * n) + [ANY] * len(after), out_specs=[HBM] * n,
        out_shape=[pltpu.HBM((r, cc), WEIGHT_COMM_DTYPE) for _, r, cc, _ in ws],
        input_output_aliases={w: w for w in range(n)},
        compiler_params=pltpu.CompilerParams(has_side_effects=SPLIT_COPY_EFFECT),
    )(*bufs, *send_sems, *recv_sems, *after)


def gather_forward(bufs, group):
    ws = [BIG[i] for i in GROUPS[group]]
    n = len(ws)

    def body(*refs):
        _forward_halves(refs[n:2 * n], ws, refs[2 * n], refs[2 * n + 1])

    return pl.pallas_call(
        body, name=f"gather_forward_{group}", in_specs=[ANY] * n, out_specs=[ANY] * n,
        out_shape=[jax.ShapeDtypeStruct((r, cc), WEIGHT_COMM_DTYPE) for _, r, cc, _ in ws],
        input_output_aliases={w: w for w in range(n)},
        scratch_shapes=[pltpu.SemaphoreType.DMA((n, 3))] * 2,
    )(*bufs)


def _half(ref, rows, cols, axis, c):
    if axis == 0:
        return ref.at[:, pl.ds(pl.multiple_of(c * (cols // 2), 128), cols // 2)]
    return ref.at[pl.ds(c * (rows // 2), rows // 2), :]


def _piece_of_half(ref, rows, cols, axis, chip):
    j = 2 * chip[0] + chip[1]
    pr, pc = _piece_shape(rows, cols, axis)
    if axis == 0:
        return ref.at[pl.ds(j * pr, pr), :]
    return ref.at[:, pl.ds(pl.multiple_of(j * pc, 128), pc)]


def sibling_exchange_start(srcs, view, landing_shapes, dtype, name):
    n = len(srcs)

    def body(*refs):
        ins, land, sems = refs[:n], refs[n:2 * n], refs[2 * n:4 * n]
        x, y, c = _place()
        for w in range(n):
            _start_remote(view(ins[w], w, c), land[w], sems[w], sems[n + w], (x, y, 1 - c))
        refs[-1][...] = jnp.zeros_like(refs[-1])

    landing = [lax.empty(shape, dtype) for shape in landing_shapes]
    out = pl.pallas_call(
        body, name=name, in_specs=[HBM] * (2 * n),
        out_specs=[SEM] * (2 * n) + [HBM] * (2 * n) + [pl.BlockSpec(memory_space=pltpu.VMEM)],
        out_shape=[pltpu.SemaphoreType.DMA(())] * (2 * n) + [pltpu.HBM(a.shape, a.dtype) for a in srcs]
        + [pltpu.HBM(shape, dtype) for shape in landing_shapes] + [jax.ShapeDtypeStruct((8, 128), F32)],
        input_output_aliases={i: 2 * n + i for i in range(2 * n)},
        compiler_params=pltpu.CompilerParams(has_side_effects=SPLIT_COPY_EFFECT),
    )(*[_in_hbm(a) for a in srcs], *[_in_hbm(b) for b in landing])
    return out[:n], out[n:2 * n], out[2 * n:3 * n], out[3 * n:4 * n], out[-1]


def sibling_exchange_wait(srcs, landing, send_sems, recv_sems, view, after, name):
    n = len(srcs)

    def body(*refs):
        ins, land, sems = refs[:n], refs[n:2 * n], refs[2 * n:4 * n]
        x, y, c = _place()
        for w in range(n):
            cp = _remote(view(ins[w], w, c), land[w], sems[w], sems[n + w], (x, y, c))
            cp.wait_send()
            cp.wait_recv()

    out = pl.pallas_call(
        body, name=name, in_specs=[HBM] * (2 * n) + [SEM] * (2 * n) + [ANY] * len(after), out_specs=[HBM] * (2 * n),
        out_shape=[pltpu.HBM(a.shape, a.dtype) for a in srcs] + [pltpu.HBM(b.shape, b.dtype) for b in landing],
        input_output_aliases={i: i for i in range(2 * n)},
        compiler_params=pltpu.CompilerParams(has_side_effects=SPLIT_COPY_EFFECT),
    )(*srcs, *landing, *send_sems, *recv_sems, *after)
    return out[:n], out[n:]


def _scatter_copies(halves, got, ws, send_sems, recv_sems, start):
    x, y, c = _place()
    copies = []
    for w, (_, r, cc, ax) in enumerate(ws):
        for k, chip in enumerate(_other_chips(x, y)):
            args = (_piece_of_half(halves[w], r, cc, ax, chip), got[w].at[k], send_sems.at[w, k], recv_sems.at[w, k], (*chip, c))
            copies.append(_start_remote(*args) if start else _remote(*args))
    return copies


def scatter_start(halves, group):
    ws = [BIG[i] for i in GROUPS[group]]
    n = len(ws)

    def body(*refs):
        sems = refs[2 * n:8 * n]
        _scatter_copies(refs[:n], refs[n:2 * n], ws, _SemList(sems[:3 * n]), _SemList(sems[3 * n:]), start=True)
        refs[-1][...] = jnp.zeros_like(refs[-1])

    landing = [lax.empty((3,) + _piece_shape(r, cc, ax), GRAD_COMM_DTYPE) for _, r, cc, ax in ws]
    out = pl.pallas_call(
        body, name=f"scatter_start_{group}", in_specs=[HBM] * (2 * n),
        out_specs=[SEM] * (6 * n) + [HBM] * (2 * n) + [pl.BlockSpec(memory_space=pltpu.VMEM)],
        out_shape=[pltpu.SemaphoreType.DMA(())] * (6 * n)
        + [pltpu.HBM(_half_shape(r, cc, ax), GRAD_COMM_DTYPE) for _, r, cc, ax in ws]
        + [pltpu.HBM((3,) + _piece_shape(r, cc, ax), GRAD_COMM_DTYPE) for _, r, cc, ax in ws]
        + [jax.ShapeDtypeStruct((8, 128), F32)],
        input_output_aliases={i: 6 * n + i for i in range(2 * n)},
        compiler_params=pltpu.CompilerParams(has_side_effects=SPLIT_COPY_EFFECT),
    )(*[_in_hbm(h) for h in halves], *[_in_hbm(b) for b in landing])
    return out[:3 * n], out[3 * n:6 * n], out[6 * n:7 * n], out[7 * n:8 * n], out[-1]


def scatter_wait(halves, got, send_sems, recv_sems, after, group):
    ws = [BIG[i] for i in GROUPS[group]]
    n = len(ws)

    def body(*refs):
        sems = refs[2 * n:8 * n]
        for cp in _scatter_copies(refs[:n], refs[n:2 * n], ws, _SemList(sems[:3 * n]), _SemList(sems[3 * n:]), start=False):
            cp.wait_send()
            cp.wait_recv()

    out = pl.pallas_call(
        body, name=f"scatter_wait_{group}", in_specs=[HBM] * (2 * n) + [SEM] * (6 * n) + [ANY] * len(after),
        out_specs=[HBM] * (2 * n),
        out_shape=[pltpu.HBM(_half_shape(r, cc, ax), GRAD_COMM_DTYPE) for _, r, cc, ax in ws]
        + [pltpu.HBM((3,) + _piece_shape(r, cc, ax), GRAD_COMM_DTYPE) for _, r, cc, ax in ws],
        input_output_aliases={i: i for i in range(2 * n)},
        compiler_params=pltpu.CompilerParams(has_side_effects=SPLIT_COPY_EFFECT),
    )(*halves, *got, *send_sems, *recv_sems, *after)
    return out[:n], out[n:]


N_DEV = 8
SMALL = ("ffn1_norm", "mix_norm", "hg_lower_bounds", "hg_out_norm", "ffn2_norm", "final_norm")
SMALL_STAGE_ROWS = 8


def small_step(loss, grads, w, m, v, behind):
    n = len(SMALL)
    shapes = [g.shape for g in grads]
    first_row = [sum(s[0] for s in shapes[:i]) for i in range(n + 1)]
    assert first_row[n] < SMALL_STAGE_ROWS
    loss_row = (pl.ds(first_row[n], 1), pl.ds(0, loss.shape[1]))

    def body(*refs):
        loss_ref, g_refs, w_refs, m_refs, v_refs = refs[0], refs[1:1 + n], refs[1 + n:1 + 2 * n], refs[1 + 2 * n:1 + 3 * n], refs[1 + 3 * n:1 + 4 * n]
        outs = refs[2 + 4 * n:3 + 8 * n]
        loss_out, dg_refs, d_refs, nm_refs, nv_refs = outs[0], outs[1:1 + n], outs[1 + n:1 + 2 * n], outs[1 + 2 * n:1 + 3 * n], outs[1 + 3 * n:]
        stage, gathered, send_sems, recv_sems = refs[3 + 8 * n:]
        x, y, c = _place()
        me = 4 * x + 2 * y + c

        def slot(i, shape):
            return pl.ds(first_row[i], shape[0]), pl.ds(0, shape[1])

        stage[...] = jnp.zeros_like(stage)
        for i, g_ref in enumerate(g_refs):
            stage[slot(i, shapes[i])] = g_ref[...]
        stage[loss_row] = loss_ref[pl.ds(0, 1), :]
        gathered[me] = stage[...]
        copies = []
        for k in range(1, N_DEV):
            peer = (x ^ (k >> 2), y ^ ((k >> 1) & 1), c ^ (k & 1))
            cp = pltpu.make_async_remote_copy(
                src_ref=stage, dst_ref=gathered.at[me], send_sem=send_sems.at[k - 1], recv_sem=recv_sems.at[k - 1],
                device_id=peer, device_id_type=MESH)
            cp.start()
            copies.append(cp)
        for cp in copies:
            cp.wait()
        acc = gathered[0]
        for k in range(1, N_DEV):
            acc = acc + gathered[k]
        stage[...] = acc
        loss_out[...] = jnp.broadcast_to(stage[loss_row], loss_out.shape)
        for i in range(n):
            g = stage[slot(i, shapes[i])]
            dg_refs[i][...] = g
            d_refs[i][...], nm_refs[i][...], nv_refs[i][...] = _adamw_math(w_refs[i][...], g, m_refs[i][...], v_refs[i][...])

    vm = pl.BlockSpec(memory_space=pltpu.VMEM)
    per_param = [jax.ShapeDtypeStruct(s, F32) for s in shapes]
    out = pl.pallas_call(
        body, name="small_step", in_specs=[vm] * (1 + 4 * n) + [ANY], out_specs=[vm] * (1 + 4 * n),
        out_shape=[jax.ShapeDtypeStruct(loss.shape, F32)] + per_param * 4,
        scratch_shapes=[pltpu.VMEM((SMALL_STAGE_ROWS, D_MODEL), F32),
                        pltpu.VMEM((N_DEV, SMALL_STAGE_ROWS, D_MODEL), F32),
                        pltpu.SemaphoreType.DMA((N_DEV - 1,)), pltpu.SemaphoreType.DMA((N_DEV - 1,))],
    )(loss, *grads, *w, *m, *v, behind)
    return out[0], out[1:1 + n], out[1 + n:1 + 2 * n], out[1 + 2 * n:1 + 3 * n], out[1 + 3 * n:]


def _swiglu_block_fwd(h, norm_g, w_gu, w_down, tag, behind=()):
    n = rmsnorm_fwd(h, norm_g, f"{tag}_norm", behind=behind)
    gu = matmul(n, w_gu, out_dtype=ACT_DTYPE, name=f"{tag}_gate_up")
    s = swiglu_fwd(gu, f"{tag}_swiglu")
    h_out = matmul(s, w_down, res=h, scale=0.5, name=f"{tag}_down")
    return h_out, (n, gu, s)


def _swiglu_block_bwd(h, norm_g, w_gu, w_down, saved, dh_out, df, tag, exchange, behind=()):
    n, gu, s = saved
    d_down = matmul(s, df, ta=True, scale=0.5, out_dtype=GRAD_COMM_DTYPE, name=f"{tag}_d_w_down")
    ds = matmul(df, w_down, tb=True, scale=0.5, out_dtype=ACT_DTYPE, behind=behind, name=f"{tag}_d_s")
    dgu = swiglu_bwd(gu, ds, f"{tag}_swiglu_bwd")
    d_gu = matmul(n, dgu, ta=True, out_dtype=GRAD_COMM_DTYPE, name=f"{tag}_d_w_gate_up")
    tokens = exchange.gradients_ready(tag, {f"{tag}_w_gate_up": d_gu, f"{tag}_w_down": d_down})
    dn = matmul(dgu, w_gu, tb=True, behind=tokens, name=f"{tag}_d_n")
    dh, dh_m, dg = rmsnorm_bwd(h, norm_g, dn, dh_out, f"{tag}_norm_bwd")
    return dh, dh_m, dg


def local_step(x, target, small, exchange):
    big = {}
    token, big_ffn1 = exchange.weights("ffn1", x)
    big.update(big_ffn1)
    h1, saved1 = _swiglu_block_fwd(x, small["ffn1_norm"], big["ffn1_w_gate_up"], big["ffn1_w_down"], "ffn1", token)
    token, big_mix = exchange.weights("mix", h1)
    big.update(big_mix)
    u = rmsnorm_fwd(h1, small["mix_norm"], "mix_norm", behind=token)
    z = matmul(u, big["w_in"], name="w_in")
    p = small["hg_lower_bounds"]
    lb = 1.0 / (1.0 + jnp.exp(p[1:2] - p[0:1]))
    y_hg, o_raw, states = hgrn_fwd(z, lb, small["hg_out_norm"], "hgrn_fwd")
    o_att, l_att = zip(*[att_fwd(z, g, f"att_fwd_{g}") for g in range(N_GROUPS)])
    y_att = att_combine_fwd(o_att, l_att, "att_combine")
    bh = matmul(y_hg, big["w_branch_hg"], name="branch_hg")
    ba = matmul(y_att, big["w_branch_att"], name="branch_att")
    merged = merge_fwd(z, bh, ba, "merge")
    h2 = matmul(merged, big["w_out"], res=h1, name="w_out")
    token, big_ffn2 = exchange.weights("ffn2", h2)
    big.update(big_ffn2)
    h3, saved2 = _swiglu_block_fwd(h2, small["ffn2_norm"], big["ffn2_w_gate_up"], big["ffn2_w_down"], "ffn2", token)
    dh3, dh3_m, d_final, loss = final_norm_loss(h3, small["final_norm"], target, "final_norm_loss")

    gs, gb = {"final_norm": d_final}, {}
    dh2, dh2_m, gs["ffn2_norm"] = _swiglu_block_bwd(
        h2, small["ffn2_norm"], big["ffn2_w_gate_up"], big["ffn2_w_down"], saved2, dh3, dh3_m, "ffn2", exchange)
    token = exchange.backward_done("ffn2", dh2)
    gb["w_out"] = matmul(merged, dh2_m, ta=True, out_dtype=GRAD_COMM_DTYPE, name="d_w_out")
    dmerged = matmul(dh2_m, big["w_out"], tb=True, behind=token, name="d_merged")
    dbh, dba, dgh, dga = merge_bwd(z, bh, ba, dmerged, "merge_bwd")
    gb["w_branch_hg"] = matmul(y_hg, dbh, ta=True, out_dtype=GRAD_COMM_DTYPE, name="d_w_branch_hg")
    gb["w_branch_att"] = matmul(y_att, dba, ta=True, out_dtype=GRAD_COMM_DTYPE, name="d_w_branch_att")
    dy_hg = matmul(dbh, big["w_branch_hg"], tb=True, name="d_y_hg")
    dy_att = matmul(dba, big["w_branch_att"], tb=True, name="d_y_att")
    dq, dfp, di, dog, d_lb, gs["hg_out_norm"] = hgrn_bwd(z, lb, small["hg_out_norm"], o_raw, states, dy_hg, "hgrn_bwd")
    do_att, corr = att_combine_bwd(o_att, l_att, dy_att, "att_combine_bwd")
    d_att = [part for g in range(N_GROUPS) for part in att_bwd(z, l_att[g], do_att[g], corr[g], g, f"att_bwd_{g}")]
    dz = jnp.concatenate([dq, dfp, di, dog, *d_att, dgh, dga], axis=1)
    gb["w_in"] = matmul(u, dz, ta=True, out_dtype=GRAD_COMM_DTYPE, name="d_w_in")
    token = exchange.gradients_ready("mix", gb)
    du = matmul(dz, big["w_in"], tb=True, behind=token, name="d_u")
    dh1, dh1_m, gs["mix_norm"] = rmsnorm_bwd(h1, small["mix_norm"], du, dh2, "mix_norm_bwd")
    token = exchange.backward_done("mix", dh1)
    dp0 = d_lb * lb * (1.0 - lb)
    gs["hg_lower_bounds"] = jnp.concatenate([dp0, -dp0], axis=0)
    dx, _, gs["ffn1_norm"] = _swiglu_block_bwd(
        x, small["ffn1_norm"], big["ffn1_w_gate_up"], big["ffn1_w_down"], saved1, dh1, dh1_m, "ffn1", exchange, token)
    exchange.backward_done("ffn1", dx)
    return loss, dx, gs


WEIGHTS = ("ffn1_norm", "ffn1_w_gate_up", "ffn1_w_down", "mix_norm", "w_in", "hg_lower_bounds", "hg_out_norm",
           "w_branch_hg", "w_branch_att", "w_out", "ffn2_norm", "ffn2_w_gate_up", "ffn2_w_down", "final_norm")


class WeightExchange:
    ORDER = ("ffn1", "mix", "ffn2")

    def __init__(self, shards, core, chip):
        self.core, self.chip = core, chip
        self.halving = None
        self.scattering = None
        self.reducing = {}
        first = self.ORDER[0]
        self.placed = {BIG[i][0]: place_own_block(shards[BIG[i][0]], chip, *BIG[i][1:], f"place_{BIG[i][0]}")
                       for i in GROUPS[first]}
        self._start_gather(first, self.placed[self._names(first)[0]])
        chip_behind = chip + self.token[0, :1].astype(jnp.int32)
        for group in self.ORDER[1:]:
            for i in GROUPS[group]:
                n, r, cc, ax = BIG[i]
                self.placed[n] = place_own_block(shards[n], chip_behind, r, cc, ax, f"place_{n}")
        self.placed_behind = [self.placed[n] for group in self.ORDER[1:] for n in self._names(group)]

    def _names(self, group):
        return [BIG[i][0] for i in GROUPS[group]]

    def _start_gather(self, group, after):
        send_sems, recv_sems, bufs, self.token = gather_start([self.placed[n] for n in self._names(group)], after, group)
        self.gathering = (group, send_sems, recv_sems, bufs)

    def weights(self, group, h):
        pending, send_sems, recv_sems, bufs = self.gathering
        assert pending == group
        after = self.placed_behind if group == self.ORDER[0] else [h]
        whole = gather_forward(gather_wait(bufs, send_sems, recv_sems, after, group), group)
        later = self.ORDER.index(group) + 1
        behind = []
        if later < len(self.ORDER):
            self._start_gather(self.ORDER[later], whole[0])
            behind = [self.token]
        return behind, dict(zip(self._names(group), whole))

    @staticmethod
    def _half_to_sibling(ws):
        return lambda ref, w, c: _half(ref, *ws[w][1:], 1 - c)

    def gradients_ready(self, group, grads):
        ws = [BIG[i] for i in GROUPS[group]]
        send_sems, recv_sems, own, theirs, token = sibling_exchange_start(
            [grads[n] for n, *_ in ws], self._half_to_sibling(ws), [_half_shape(r, cc, ax) for _, r, cc, ax in ws],
            GRAD_COMM_DTYPE, f"halves_start_{group}")
        self.halving = (group, send_sems, recv_sems, own, theirs)
        return [token]

    def backward_done(self, group, dh):
        behind = [self._finish_scatter([dh])] if self.scattering is not None else []
        pending, send_sems, recv_sems, own, theirs = self.halving
        assert pending == group
        ws = [BIG[i] for i in GROUPS[group]]
        own, theirs = sibling_exchange_wait(own, theirs, send_sems, recv_sems, self._half_to_sibling(ws), [dh],
                                            f"halves_wait_{group}")
        halves = [add_halves(g, t, self.core, r, cc, ax, f"add_halves_{n}") for (n, r, cc, ax), g, t in zip(ws, own, theirs)]
        send_sems, recv_sems, halves, got, self.token = scatter_start(halves, group)
        self.scattering = (group, send_sems, recv_sems, halves, got)
        return behind + [self.token]

    def _finish_scatter(self, after):
        group, send_sems, recv_sems, halves, got = self.scattering
        halves, got = scatter_wait(halves, got, send_sems, recv_sems, after, group)
        ws = [BIG[i] for i in GROUPS[group]]
        mine = [add_pieces(h, g, self.chip, r, cc, ax, f"add_pieces_{n}") for (n, r, cc, ax), h, g in zip(ws, halves, got)]
        send_sems, recv_sems, mine, theirs, token = sibling_exchange_start(
            mine, lambda ref, w, c: ref, [_piece_shape(r, cc, ax) for _, r, cc, ax in ws], F32, f"reduced_start_{group}")
        self.reducing[group] = (send_sems, recv_sems, mine, theirs)
        self.scattering = None
        return token

    def finish(self, after):
        return self._finish_scatter(after)

    def reduced_halves(self, group, after):
        send_sems, recv_sems, mine, theirs = self.reducing.pop(group)
        mine, theirs = sibling_exchange_wait(mine, theirs, send_sems, recv_sems, lambda ref, w, c: ref, after,
                                             f"reduced_wait_{group}")
        return {BIG[i][0]: (a, b) for i, a, b in zip(GROUPS[group], mine, theirs)}


def kernel(x, ffn1_norm, ffn1_w_gate_up, ffn1_w_down, mix_norm, w_in, hg_lower_bounds, hg_out_norm, w_branch_hg, w_branch_att, w_out, ffn2_norm, ffn2_w_gate_up, ffn2_w_down, final_norm, loss_target, m_ffn1_norm, m_ffn1_w_gate_up, m_ffn1_w_down, m_mix_norm, m_w_in, m_hg_lower_bounds, m_hg_out_norm, m_w_branch_hg, m_w_branch_att, m_w_out, m_ffn2_norm, m_ffn2_w_gate_up, m_ffn2_w_down, m_final_norm, v_ffn1_norm, v_ffn1_w_gate_up, v_ffn1_w_down, v_mix_norm, v_w_in, v_hg_lower_bounds, v_hg_out_norm, v_w_branch_hg, v_w_branch_att, v_w_out, v_ffn2_norm, v_ffn2_w_gate_up, v_ffn2_w_down, v_final_norm):
    w = dict(ffn1_norm=ffn1_norm, ffn1_w_gate_up=ffn1_w_gate_up, ffn1_w_down=ffn1_w_down, mix_norm=mix_norm, w_in=w_in,
             hg_lower_bounds=hg_lower_bounds, hg_out_norm=hg_out_norm, w_branch_hg=w_branch_hg, w_branch_att=w_branch_att,
             w_out=w_out, ffn2_norm=ffn2_norm, ffn2_w_gate_up=ffn2_w_gate_up, ffn2_w_down=ffn2_w_down, final_norm=final_norm)
    m = dict(ffn1_norm=m_ffn1_norm, ffn1_w_gate_up=m_ffn1_w_gate_up, ffn1_w_down=m_ffn1_w_down, mix_norm=m_mix_norm,
             w_in=m_w_in, hg_lower_bounds=m_hg_lower_bounds, hg_out_norm=m_hg_out_norm, w_branch_hg=m_w_branch_hg,
             w_branch_att=m_w_branch_att, w_out=m_w_out, ffn2_norm=m_ffn2_norm, ffn2_w_gate_up=m_ffn2_w_gate_up,
             ffn2_w_down=m_ffn2_w_down, final_norm=m_final_norm)
    v = dict(ffn1_norm=v_ffn1_norm, ffn1_w_gate_up=v_ffn1_w_gate_up, ffn1_w_down=v_ffn1_w_down, mix_norm=v_mix_norm,
             w_in=v_w_in, hg_lower_bounds=v_hg_lower_bounds, hg_out_norm=v_hg_out_norm, w_branch_hg=v_w_branch_hg,
             w_branch_att=v_w_branch_att, w_out=v_w_out, ffn2_norm=v_ffn2_norm, ffn2_w_gate_up=v_ffn2_w_gate_up,
             ffn2_w_down=v_ffn2_w_down, final_norm=v_final_norm)

    core = lax.axis_index("c").astype(jnp.int32).reshape(1)
    chip = (2 * lax.axis_index("x") + lax.axis_index("y")).astype(jnp.int32).reshape(1)
    exchange = WeightExchange({n: w[n][0] for n, *_ in BIG}, core, chip)
    small = {n: w[n] for n in SMALL}
    small["final_norm"] = final_norm.reshape(1, D_MODEL)

    loss, dx, gs = local_step(x[0], loss_target[0], small, exchange)

    grads, delta, new_m, new_v = {}, {}, {}, {}

    def update(group, core, after):
        reduced = exchange.reduced_halves(group, after)
        for i in GROUPS[group]:
            n, r, cc, ax = BIG[i]
            a, b = reduced[n]
            g, d, nm, nv = adamw_halves(w[n][0], a, b, m[n][0], v[n][0], core, r, cc, ax, f"adamw_{n}")
            grads[n], delta[n], new_m[n], new_v[n] = g[None], d[None], nm[None], nv[None]

    core_behind = core + exchange.token[0, :1].astype(jnp.int32)
    update("ffn2", core_behind, [exchange.token])
    update("mix", core_behind, [delta["ffn2_w_down"]])
    token = exchange.finish(after=[delta[BIG[i][0]] for group in ("ffn2", "mix") for i in GROUPS[group]])
    two_d = lambda a: a.reshape(1, D_MODEL) if a.ndim == 1 else a
    loss_sum, *small_out = small_step(loss, [gs[n] for n in SMALL], *[[two_d(p[n]) for n in SMALL] for p in (w, m, v)],
                                      behind=token)
    for result, parts in zip((grads, delta, new_m, new_v), small_out):
        result.update({n: a.reshape(w[n].shape) for n, a in zip(SMALL, parts)})
    update("ffn1", core, [loss_sum])

    return (loss_sum[0, 0], dx[None], *[grads[n] for n in WEIGHTS], *[delta[n] for n in WEIGHTS],
            *[new_m[n] for n in WEIGHTS], *[new_v[n] for n in WEIGHTS])
```

```python
import numpy as np
import jax
import jax.numpy as jnp
from jax import lax
from jax.experimental import pallas as pl
from jax.experimental.pallas import tpu as pltpu

SEQ = 2048
D_MODEL = 1024
D_FF = 2816
HG_HEADS = 4
HG_DIM = 128
HG_WIDTH = 512
HG_CHUNK = 64
ATT_GROUPS = ((128, 1), (512, 4), (2048, 16))
ATT_HEADS = 8
ATT_WIDTH = 512
ATT_BLOCK = 128
ALIBI_MAX = 8.0
IN_COLS = 8704
EPS = 1e-6
NEG_INF = -1e30
ADAM_LR = 0.001
ADAM_B1 = 0.9
ADAM_B2 = 0.999
ADAM_EPS = 1e-08
ADAM_WD = 0.01
ADAM_STEP = 10

N_CHIPS = 4
MXU_DTYPE = jnp.bfloat16
WEIGHT_COMM_DTYPE = jnp.bfloat16
GRAD_COMM_DTYPE = jnp.bfloat16
ACT_DTYPE = jnp.bfloat16
MESH = pl.DeviceIdType.MESH
F32 = jnp.float32
HIGHEST = lax.Precision.HIGHEST


def _sigmoid(x):
    return 1.0 / (1.0 + jnp.exp(-x))


def _dot(a, b, ta=False, tb=False):
    dn = (((0 if ta else 1,), (1 if tb else 0,)), ((), ()))
    return lax.dot_general(a.astype(MXU_DTYPE), b.astype(MXU_DTYPE), dn, preferred_element_type=F32)


def _dot_f32(a, b, ones_on_right=False):
    x = a if ones_on_right else b
    hi = x.astype(jnp.bfloat16)
    rest = x - hi.astype(F32)
    mid = rest.astype(jnp.bfloat16)
    lo = (rest - mid.astype(F32)).astype(jnp.bfloat16)
    if ones_on_right:
        dot = lambda q: jnp.dot(q, b.astype(jnp.bfloat16), preferred_element_type=F32)
    else:
        dot = lambda q: jnp.dot(a.astype(jnp.bfloat16), q, preferred_element_type=F32)
    return dot(hi) + (dot(mid) + dot(lo))


def _split_bf16(x):
    hi = x.astype(jnp.bfloat16)
    return hi, (x - hi.astype(F32)).astype(jnp.bfloat16)


def _hdot(a, b, ta=False, tb=False):
    dn =(((0 if ta else 1,), (1 if tb else 0,)), ((), ()))
    (a_hi, a_lo), (b_hi, b_lo) = _split_bf16(a), _split_bf16(b)
    dot = lambda p, q: lax.dot_general(p, q, dn, preferred_element_type=F32)
    return dot(a_hi, b_hi) + (dot(a_lo, b_hi) + dot(a_hi, b_lo))


MATMUL_VMEM_BYTES = 48 * 1024 * 1024
MATMUL_TILE_BYTES = 36 * 1024 * 1024
MXU_ALIGN = 128


def _divisors(n, most):
    return [t for t in range(min(n, most), 0, -MXU_ALIGN) if n % t == 0 and t % MXU_ALIGN == 0]


def _matmul_tiles(M, N, K, in_bytes, out_bytes, has_res):
    best = None
    for tk in _divisors(K, K):
        nk = K // tk
        for tm in _divisors(M, 2048):
            for tn in _divisors(N, 512):
                tiles = 2 * in_bytes * (tm * tk + tk * tn) + 2 * out_bytes * tm * tn
                tiles += 4 * tm * tn * ((nk > 1) + 2 * has_res)
                if tiles > MATMUL_TILE_BYTES:
                    continue
                traffic = in_bytes * (M * K * (1 if nk == 1 else N // tn) + K * N * (M // tm))
                key = (traffic, -tm * tn * tk)
                if best is None or key < best[0]:
                    best = (key, (tm, tn, tk))
    return best[1]


def matmul(a, b, *, ta=False, tb=False, out_dtype=F32, res=None, scale=1.0, behind=(), name):
    if ta:
        K, M = a.shape
    else:
        M, K = a.shape
    if tb:
        N, K2 = b.shape
    else:
        K2, N = b.shape
    assert K == K2 and a.dtype == b.dtype
    tm, tn, tk = _matmul_tiles(M, N, K, a.dtype.itemsize, jnp.dtype(out_dtype).itemsize, res is not None)
    nk = K // tk

    def finish(r, r_ref, o_ref):
        if scale != 1.0:
            r = r * scale
        if res is not None:
            r = r_ref[...] + r
        o_ref[...] = r.astype(out_dtype)

    def body(*refs):
        a_ref, b_ref = refs[:2]
        r_ref = refs[2] if res is not None else None
        o_ref = refs[2 + (res is not None) + len(behind)]
        if nk == 1:
            finish(_dot(a_ref[...], b_ref[...], ta, tb), r_ref, o_ref)
            return
        acc = refs[-1]
        k = pl.program_id(2)

        @pl.when(k == 0)
        def _():
            acc[...] = jnp.zeros_like(acc)

        acc[...] += _dot(a_ref[...], b_ref[...], ta, tb)

        @pl.when(k == nk - 1)
        def _():
            finish(acc[...], r_ref, o_ref)

    a_spec = pl.BlockSpec((tk, tm), lambda i, j, k: (k, i)) if ta else pl.BlockSpec((tm, tk), lambda i, j, k: (i, k))
    b_spec = pl.BlockSpec((tn, tk), lambda i, j, k: (j, k)) if tb else pl.BlockSpec((tk, tn), lambda i, j, k: (k, j))
    in_specs = [a_spec, b_spec]
    args = [a, b]
    if res is not None:
        in_specs.append(pl.BlockSpec((tm, tn), lambda i, j, k: (i, j)))
        args.append(res)
    for earlier in behind:
        in_specs.append(pl.BlockSpec(memory_space=pl.ANY))
        args.append(earlier)
    return pl.pallas_call(
        body, name=name, grid=(M // tm, N // tn, nk), in_specs=in_specs,
        out_specs=pl.BlockSpec((tm, tn), lambda i, j, k: (i, j)),
        out_shape=jax.ShapeDtypeStruct((M, N), out_dtype),
        scratch_shapes=[pltpu.VMEM((tm, tn), F32)] if nk > 1 else [],
        compiler_params=pltpu.CompilerParams(dimension_semantics=("parallel", "parallel", "arbitrary"),
                                             vmem_limit_bytes=MATMUL_VMEM_BYTES),
    )(*args)


ROW_TILE = 256


def rmsnorm_fwd(x, g, name, behind=()):
    def body(x_ref, g_ref, *refs):
        n_ref = refs[-1]
        xv = x_ref[...]
        r = lax.rsqrt(jnp.mean(xv * xv, axis=-1, keepdims=True) + EPS)
        n_ref[...] = ((xv * r) * g_ref[...]).astype(n_ref.dtype)

    order = list(behind)
    return pl.pallas_call(
        body, name=name, grid=(SEQ // ROW_TILE,),
        in_specs=[pl.BlockSpec((ROW_TILE, D_MODEL), lambda i: (i, 0)), pl.BlockSpec((1, D_MODEL), lambda i: (0, 0))]
        + [pl.BlockSpec(memory_space=pl.ANY)] * len(order),
        out_specs=pl.BlockSpec((ROW_TILE, D_MODEL), lambda i: (i, 0)),
        out_shape=jax.ShapeDtypeStruct((SEQ, D_MODEL), MXU_DTYPE),
    )(x, g, *order)


def rmsnorm_bwd(x, g, dn, dres, name):
    def body(x_ref, g_ref, dn_ref, dr_ref, dx_ref, dxm_ref, dg_ref):
        xv = x_ref[...]
        r = lax.rsqrt(jnp.mean(xv * xv, axis=-1, keepdims=True) + EPS)
        xh = xv * r
        dnv = dn_ref[...]

        @pl.when(pl.program_id(0) == 0)
        def _():
            dg_ref[...] = jnp.zeros_like(dg_ref)

        dg_ref[...] += jnp.sum(dnv * xh, axis=0, keepdims=True)
        dxh = dnv * g_ref[...]
        dx = dr_ref[...] + r * (dxh - xh * jnp.mean(dxh * xh, axis=-1, keepdims=True))
        dx_ref[...] = dx
        dxm_ref[...] = dx.astype(dxm_ref.dtype)

    row = pl.BlockSpec((ROW_TILE, D_MODEL), lambda i: (i, 0))
    vec = pl.BlockSpec((1, D_MODEL), lambda i: (0, 0))
    return pl.pallas_call(
        body, name=name, grid=(SEQ // ROW_TILE,), in_specs=[row, vec, row, row], out_specs=[row, row, vec],
        out_shape=[jax.ShapeDtypeStruct((SEQ, D_MODEL), F32), jax.ShapeDtypeStruct((SEQ, D_MODEL), MXU_DTYPE),
                   jax.ShapeDtypeStruct((1, D_MODEL), F32)],
        compiler_params=pltpu.CompilerParams(dimension_semantics=("arbitrary",)),
    )(x, g, dn, dres)


def final_norm_loss(h, g, target, name):
    def body(h_ref, g_ref, t_ref, dh_ref, dhm_ref, dg_ref, loss_ref):
        xv = h_ref[...]
        r = lax.rsqrt(jnp.mean(xv * xv, axis=-1, keepdims=True) + EPS)
        xh = xv * r
        gv = g_ref[...]
        e = xh * gv - t_ref[...]

        @pl.when(pl.program_id(0) == 0)
        def _():
            dg_ref[...] = jnp.zeros_like(dg_ref)
            loss_ref[...] = jnp.zeros_like(loss_ref)

        part = 0.5 * jnp.sum(jnp.sum(e * e, axis=-1, keepdims=True) * (1.0 / D_MODEL), axis=0, keepdims=True)
        loss_ref[...] += jnp.broadcast_to(part, loss_ref.shape)
        dout = e * (1.0 / D_MODEL)
        dg_ref[...] += jnp.sum(dout * xh, axis=0, keepdims=True)
        dxh = dout * gv
        dh = r * (dxh - xh * jnp.mean(dxh * xh, axis=-1, keepdims=True))
        dh_ref[...] = dh
        dhm_ref[...] = dh.astype(dhm_ref.dtype)

    row = pl.BlockSpec((ROW_TILE, D_MODEL), lambda i: (i, 0))
    vec = pl.BlockSpec((1, D_MODEL), lambda i: (0, 0))
    return pl.pallas_call(
        body, name=name, grid=(SEQ // ROW_TILE,), in_specs=[row, vec, row],
        out_specs=[row, row, vec, pl.BlockSpec((8, 128), lambda i: (0, 0))],
        out_shape=[jax.ShapeDtypeStruct((SEQ, D_MODEL), F32), jax.ShapeDtypeStruct((SEQ, D_MODEL), MXU_DTYPE),
                   jax.ShapeDtypeStruct((1, D_MODEL), F32), jax.ShapeDtypeStruct((8, 128), F32)],
        compiler_params=pltpu.CompilerParams(dimension_semantics=("arbitrary",)),
    )(h, g, target)


FFN_TILE = 256
FFN_TILES = D_FF // FFN_TILE


def gate_up_swiglu(n, w_gu, name):
    def body(n_ref, wa_ref, wb_ref, a_ref, b_ref, s_ref):
        nv = n_ref[...]
        a = _dot(nv, wa_ref[...])
        b = _dot(nv, wb_ref[...])
        a_ref[...] = a.astype(a_ref.dtype)
        b_ref[...] = b.astype(b_ref.dtype)
        s_ref[...] = (a * _sigmoid(a) * b).astype(s_ref.dtype)

    tile = pl.BlockSpec((SEQ, FFN_TILE), lambda j: (0, j))
    act = jax.ShapeDtypeStruct((SEQ, D_FF), ACT_DTYPE)
    return pl.pallas_call(
        body, name=name, grid=(FFN_TILES,),
        in_specs=[pl.BlockSpec((SEQ, D_MODEL), lambda j: (0, 0)), pl.BlockSpec((D_MODEL, FFN_TILE), lambda j: (0, j)),
                  pl.BlockSpec((D_MODEL, FFN_TILE), lambda j: (0, j + FFN_TILES))],
        out_specs=[tile, tile, tile], out_shape=[act, act, jax.ShapeDtypeStruct((SEQ, D_FF), MXU_DTYPE)],
        compiler_params=pltpu.CompilerParams(dimension_semantics=("parallel",), vmem_limit_bytes=MATMUL_VMEM_BYTES),
    )(n, w_gu, w_gu)


def d_gate_up(df, w_down, a, b, scale, behind, name):
    def body(df_ref, w_ref, a_ref, b_ref, *refs):
        o_ref, db_sc = refs[-2:]

        @pl.when(pl.program_id(1) == 0)
        def _():
            ds = _dot(df_ref[...], w_ref[...], tb=True) * scale
            av = a_ref[...].astype(F32)
            sg = _sigmoid(av)
            o_ref[...] = (ds * b_ref[...].astype(F32) * (sg * (1.0 + av * (1.0 - sg)))).astype(o_ref.dtype)
            db_sc[...] = (ds * av * sg).astype(db_sc.dtype)

        @pl.when(pl.program_id(1) == 1)
        def _():
            o_ref[...] = db_sc[...]

    tile = pl.BlockSpec((SEQ, FFN_TILE), lambda j, t: (0, j))
    return pl.pallas_call(
        body, name=name, grid=(FFN_TILES, 2),
        in_specs=[pl.BlockSpec((SEQ, D_MODEL), lambda j, t: (0, 0)), pl.BlockSpec((FFN_TILE, D_MODEL), lambda j, t: (j, 0)),
                  tile, tile] + [pl.BlockSpec(memory_space=pl.ANY)] * len(behind),
        out_specs=pl.BlockSpec((SEQ, FFN_TILE), lambda j, t: (0, j + FFN_TILES * t)),
        out_shape=jax.ShapeDtypeStruct((SEQ, 2 * D_FF), MXU_DTYPE),
        scratch_shapes=[pltpu.VMEM((SEQ, FFN_TILE), MXU_DTYPE)],
        compiler_params=pltpu.CompilerParams(dimension_semantics=("arbitrary", "arbitrary"),
                                             vmem_limit_bytes=MATMUL_VMEM_BYTES),
    )(df, w_down, a, b, *behind)


GATE_HG_BLK = 6656 // 512
GATE_ATT_BLK = 7680 // 512


def merge_fwd(z, bh, ba, name):
    def body(gh_ref, ga_ref, bh_ref, ba_ref, o_ref):
        o_ref[...] = (_sigmoid(gh_ref[...]) * bh_ref[...] + _sigmoid(ga_ref[...]) * ba_ref[...]).astype(o_ref.dtype)

    blk = pl.BlockSpec((ROW_TILE, 512), lambda i, j: (i, j))
    return pl.pallas_call(
        body, name=name, grid=(SEQ // ROW_TILE, 2),
        in_specs=[pl.BlockSpec((ROW_TILE, 512), lambda i, j: (i, GATE_HG_BLK + j)),
                  pl.BlockSpec((ROW_TILE, 512), lambda i, j: (i, GATE_ATT_BLK + j)), blk, blk],
        out_specs=blk, out_shape=jax.ShapeDtypeStruct((SEQ, D_MODEL), MXU_DTYPE),
    )(z, z, bh, ba)


def merge_bwd(z, bh, ba, dm, name):
    def body(gh_ref, ga_ref, bh_ref, ba_ref, dm_ref, dbh_ref, dba_ref, dgh_ref, dga_ref):
        dmv = dm_ref[...]
        sh = _sigmoid(gh_ref[...])
        sa = _sigmoid(ga_ref[...])
        dbh_ref[...] = (dmv * sh).astype(dbh_ref.dtype)
        dba_ref[...] = (dmv * sa).astype(dba_ref.dtype)
        dgh_ref[...] = (dmv * bh_ref[...] * (sh * (1.0 - sh))).astype(dgh_ref.dtype)
        dga_ref[...] = (dmv * ba_ref[...] * (sa * (1.0 - sa))).astype(dga_ref.dtype)

    blk = pl.BlockSpec((ROW_TILE, 512), lambda i, j: (i, j))
    out = jax.ShapeDtypeStruct((SEQ, D_MODEL), MXU_DTYPE)
    return pl.pallas_call(
        body, name=name, grid=(SEQ // ROW_TILE, 2),
        in_specs=[pl.BlockSpec((ROW_TILE, 512), lambda i, j: (i, GATE_HG_BLK + j)),
                  pl.BlockSpec((ROW_TILE, 512), lambda i, j: (i, GATE_ATT_BLK + j)), blk, blk, blk],
        out_specs=[blk, blk, blk, blk], out_shape=[out, out, out, out],
    )(z, z, bh, ba, dm)


N_CHUNKS = SEQ // HG_CHUNK
HG_STEP_CHUNKS = 4


def _hgrn_gates(q, fp, lb):
    C = HG_CHUNK
    sg = _sigmoid(fp)
    f = lb + (1.0 - lb) * sg
    lf = jnp.log(f)
    row = lax.broadcasted_iota(jnp.int32, (C, C), 0)
    col = lax.broadcasted_iota(jnp.int32, (C, C), 1)
    causal = row >= col
    G = _dot_f32(causal.astype(F32), lf)
    eG = jnp.exp(G)
    enG = jnp.exp(-G)
    qg = q * eG
    kg = (1.0 - f) * enG
    A = jnp.where(causal, _hdot(qg, kg, tb=True), 0.0)
    egl = jnp.exp(jnp.sum(lf, axis=0, keepdims=True))
    return sg, f, causal, eG, enG, qg, kg, A, egl


def hgrn_fwd(z, lb, gain, name):
    C, K = HG_CHUNK, HG_DIM

    def body(q_ref, f_ref, v_ref, og_ref, p_ref, g_ref, y_ref, o_ref, st_ref, state):
        @pl.when(pl.program_id(0) == 0)
        def _():
            state[...] = jnp.zeros_like(state)

        for cc in range(HG_STEP_CHUNKS):
            rows = pl.ds(cc * C, C)
            for h in range(HG_HEADS):
                hd = pl.ds(h * K, K)
                v = v_ref[rows, hd]
                _, _, _, _, _, qg, kg, A, egl = _hgrn_gates(q_ref[rows, hd], f_ref[rows, hd], p_ref[:, hd])
                st = state[h]
                st_ref[h, cc] = st
                o = _hdot(A, v) + _hdot(qg, st, tb=True)
                state[h] = st * egl + _hdot(v, kg * egl, ta=True)
                o_ref[rows, hd] = o
                rs = lax.rsqrt(jnp.mean(o * o, axis=-1, keepdims=True) + EPS)
                og = og_ref[rows, hd]
                y_ref[rows, hd] = (((o * rs) * g_ref[:, hd]) * (og * _sigmoid(og))).astype(y_ref.dtype)

    R = HG_STEP_CHUNKS * C

    def zcol(section):
        return pl.BlockSpec((R, HG_WIDTH), lambda c: (c, section))

    vec = pl.BlockSpec((1, HG_WIDTH), lambda c: (0, 0))
    blk = pl.BlockSpec((R, HG_WIDTH), lambda c: (c, 0))
    return pl.pallas_call(
        body, name=name, grid=(N_CHUNKS // HG_STEP_CHUNKS,),
        in_specs=[zcol(0), zcol(1), zcol(2), zcol(3), vec, vec],
        out_specs=[blk, blk, pl.BlockSpec((HG_HEADS, HG_STEP_CHUNKS, K, K), lambda c: (0, c, 0, 0))],
        out_shape=[jax.ShapeDtypeStruct((SEQ, HG_WIDTH), MXU_DTYPE), jax.ShapeDtypeStruct((SEQ, HG_WIDTH), F32),
                   jax.ShapeDtypeStruct((HG_HEADS, N_CHUNKS, K, K), F32)],
        scratch_shapes=[pltpu.VMEM((HG_HEADS, K, K), F32)],
        compiler_params=pltpu.CompilerParams(dimension_semantics=("arbitrary",)),
    )(z, z, z, z, lb, gain)


def hgrn_bwd(z, lb, gain, o_raw, states, dy, name):
    C, K = HG_CHUNK, HG_DIM

    def body(q_ref, f_ref, v_ref, og_ref, p_ref, g_ref, o_ref, st_ref, dy_ref,
             dq_ref, dfp_ref, dv_ref, dog_ref, dlb_ref, dgain_ref, dstate):
        @pl.when(pl.program_id(0) == 0)
        def _():
            dstate[...] = jnp.zeros_like(dstate)
            dlb_ref[...] = jnp.zeros_like(dlb_ref)
            dgain_ref[...] = jnp.zeros_like(dgain_ref)

        last = lax.broadcasted_iota(jnp.int32, (C, K), 0) == C - 1
        row = lax.broadcasted_iota(jnp.int32, (C, C), 0)
        col = lax.broadcasted_iota(jnp.int32, (C, C), 1)
        anti_causal = (col >= row).astype(F32)
        for cc in reversed(range(HG_STEP_CHUNKS)):
            rows = pl.ds(cc * C, C)
            for h in range(HG_HEADS):
                hd = pl.ds(h * K, K)
                v = v_ref[rows, hd]
                lb = p_ref[:, hd]
                sg, f, causal, eG, enG, qg, kg, A, egl = _hgrn_gates(q_ref[rows, hd], f_ref[rows, hd], lb)
                kd = kg * egl
                st = st_ref[h, cc]
                dst = dstate[h]
                o = o_ref[rows, hd]
                og = og_ref[rows, hd]
                gain_v = g_ref[:, hd]
                dyv = dy_ref[rows, hd]
                rs = lax.rsqrt(jnp.mean(o * o, axis=-1, keepdims=True) + EPS)
                on = o * rs
                sgo = _sigmoid(og)
                silu = og * sgo
                dog_ref[rows, hd] = (dyv * (on * gain_v) * (sgo * (1.0 + og * (1.0 - sgo)))).astype(dog_ref.dtype)
                dgain_ref[:, hd] += jnp.sum(dyv * silu * on, axis=0, keepdims=True)
                don = dyv * gain_v * silu
                do = rs * (don - on * jnp.mean(don * on, axis=-1, keepdims=True))
                dA = jnp.where(causal, _hdot(do, v, tb=True), 0.0)
                dv_ref[rows, hd] = (_hdot(A, do, ta=True) + _hdot(kd, dst, tb=True)).astype(dv_ref.dtype)
                dqg = _hdot(dA, kg) + _hdot(do, st)
                dkg = _hdot(dA, qg, ta=True)
                dkd = _hdot(v, dst)
                dstate[h] = dst * egl + _hdot(do, qg, ta=True)
                dgl = jnp.sum(st * dst, axis=0, keepdims=True) * egl
                dq_ref[rows, hd] = (dqg * eG).astype(dq_ref.dtype)
                dk = dkg * enG + dkd * (enG * egl)
                dG = dqg * qg - dkg * kg - dkd * kd
                extra = jnp.sum(dkd * kd, axis=0, keepdims=True) + dgl
                dG = dG + jnp.where(last, extra, 0.0)
                dlf = _dot_f32(anti_causal, dG)
                df = dlf / f - dk
                dfp_ref[rows, hd] = (df * (1.0 - lb) * (sg * (1.0 - sg))).astype(dfp_ref.dtype)
                dlb_ref[:, hd] += jnp.sum(df * (1.0 - sg), axis=0, keepdims=True)

    R = HG_STEP_CHUNKS * C
    n_steps = N_CHUNKS // HG_STEP_CHUNKS

    def rc(c):
        return n_steps - 1 - c

    def zcol(section):
        return pl.BlockSpec((R, HG_WIDTH), lambda c: (rc(c), section))

    vec = pl.BlockSpec((1, HG_WIDTH), lambda c: (0, 0))
    blk = pl.BlockSpec((R, HG_WIDTH), lambda c: (rc(c), 0))
    out = jax.ShapeDtypeStruct((SEQ, HG_WIDTH), MXU_DTYPE)
    small = jax.ShapeDtypeStruct((1, HG_WIDTH), F32)
    return pl.pallas_call(
        body, name=name, grid=(n_steps,),
        in_specs=[zcol(0), zcol(1), zcol(2), zcol(3), vec, vec, blk,
                  pl.BlockSpec((HG_HEADS, HG_STEP_CHUNKS, K, K), lambda c: (0, rc(c), 0, 0)), blk],
        out_specs=[blk, blk, blk, blk, vec, vec],
        out_shape=[out, out, out, out, small, small],
        scratch_shapes=[pltpu.VMEM((HG_HEADS, K, K), F32)],
        compiler_params=pltpu.CompilerParams(dimension_semantics=("arbitrary",)),
    )(z, z, z, z, lb, gain, o_raw, states, dy)


N_GROUPS = len(ATT_GROUPS)
HEAD_PAIRS = ATT_WIDTH // 128
ATT_COL0 = 4 * HG_WIDTH
UNROLLED_UNITS = 4
ATT_SLAB_BLOCKS = 4


def _alibi_coef():
    n = N_GROUPS * ATT_HEADS
    slopes = np.exp2(-ALIBI_MAX * np.arange(1, n + 1, dtype=np.float32) / n).astype(np.float32)
    dil = np.repeat(np.array([d for _, d in ATT_GROUPS], np.float32), ATT_HEADS)
    return jnp.asarray(slopes * dil, F32)


def _for_each_unit(n, fn):
    if n <= UNROLLED_UNITS:
        for u in range(n):
            fn(u)
    else:
        def group(i, carry):
            for j in range(UNROLLED_UNITS):
                fn(i * UNROLLED_UNITS + j)
            return carry
        lax.fori_loop(0, n // UNROLLED_UNITS, group, 0)


def _att_specs(g):
    B = ATT_BLOCK
    d = ATT_GROUPS[g][1]
    blocks = ATT_SLAB_BLOCKS if d == 1 else 1
    R = B * d * blocks
    n_slabs = SEQ // R
    multi = SEQ // d > B
    col0 = (ATT_COL0 + g * 3 * ATT_WIDTH) // 128

    def cur(col):
        return pl.BlockSpec((R, 128), lambda hp, s: (s, col + hp))

    def prev(col):
        return pl.BlockSpec((R, 128), lambda hp, s: (jnp.maximum(s - 1, 0), col + hp))

    def nxt(col):
        return pl.BlockSpec((R, 128), lambda hp, s: (jnp.minimum(s + 1, n_slabs - 1), col + hp))

    def unit(u, s):
        if d > 1:
            rows = pl.ds(u, B, stride=d)
            return rows, False, rows, jnp.where(s == 0, B, 0), False, rows, jnp.where(s == n_slabs - 1, B, 0)
        rows = pl.ds(u * B, B)
        inner_prev, inner_next = u > 0, u < blocks - 1
        return (rows, inner_prev, pl.ds((u - 1) * B if inner_prev else (blocks - 1) * B, B),
                0 if inner_prev else jnp.where(s == 0, B, 0),
                inner_next, pl.ds((u + 1) * B if inner_next else 0, B),
                0 if inner_next else jnp.where(s == n_slabs - 1, B, 0))

    return d * blocks, R, n_slabs, multi, col0, cur, prev, nxt, unit


def _head_lanes(j):
    lane = lax.broadcasted_iota(jnp.int32, (ATT_BLOCK, 128), 1)
    return (lane >= 64 * j) & (lane < 64 * (j + 1))


def _lane_value(x, sel):
    return jnp.max(jnp.where(sel, x, -3e38), axis=-1, keepdims=True)


def _stack_heads(x, sel0):
    return jnp.concatenate([jnp.where(sel0, x, 0.0), jnp.where(sel0, 0.0, x)], axis=0)


def _stack_values(x, sel0, lanes):
    swapped = pltpu.roll(x, 64, 1)
    stacked = jnp.concatenate([jnp.where(sel0, x, swapped), jnp.where(sel0, swapped, x)], axis=0)
    return stacked if lanes == 128 else jnp.concatenate([stacked] * (lanes // 128), axis=1)


def _pair_coef(coef_ref, g, hp):
    row = lax.broadcasted_iota(jnp.int32, (2 * ATT_BLOCK, 1), 0)
    first = g * ATT_HEADS + hp * 2
    return jnp.where(row < ATT_BLOCK, coef_ref[first], coef_ref[first + 1])


def _band(with_prev, first_key):
    B = ATT_BLOCK
    keys = 2 * B if with_prev else B
    qi = jnp.bitwise_and(lax.broadcasted_iota(jnp.int32, (2 * B, keys), 0), B - 1)
    kj = lax.broadcasted_iota(jnp.int32, (2 * B, keys), 1)
    delta = qi + (B if with_prev else 0) - kj
    valid = (delta >= 0) & (delta <= B)
    if with_prev:
        valid = valid & (kj >= first_key)
    return valid, delta.astype(F32)


def _band_next(first_key):
    B = ATT_BLOCK
    qi = jnp.bitwise_and(lax.broadcasted_iota(jnp.int32, (2 * B, B), 0), B - 1)
    kj = lax.broadcasted_iota(jnp.int32, (2 * B, B), 1)
    delta = qi + B - kj
    return (delta <= B) & (kj >= first_key), delta.astype(F32)


def att_fwd(z, g, name):
    B = ATT_BLOCK
    n_units, R, n_slabs, has_prev, col0, cur, prev, _, unit = _att_specs(g)

    def body(coef_ref, *refs):
        if has_prev:
            q_ref, kc_ref, vc_ref, kp_ref, vp_ref, o_ref, l_ref = refs
        else:
            q_ref, kc_ref, vc_ref, o_ref, l_ref = refs
        hp, s = pl.program_id(0), pl.program_id(1)
        cf2 = _pair_coef(coef_ref, g, hp)
        sel0 = _head_lanes(0)

        def one(u):
            rows, inner_prev, prev_rows, first_key, _, _, _ = unit(u, s)
            valid, dist = _band(has_prev, first_key)
            q2 = _stack_heads(q_ref[rows, :], sel0)
            kk, vv = kc_ref[rows, :], vc_ref[rows, :]
            if has_prev:
                k_from, v_from = (kc_ref, vc_ref) if inner_prev else (kp_ref, vp_ref)
                kk = jnp.concatenate([k_from[prev_rows, :], kk], axis=0)
                vv = jnp.concatenate([v_from[prev_rows, :], vv], axis=0)
            sc = jnp.where(valid, _dot(q2, kk, tb=True) * 0.125 - cf2 * dist, NEG_INF)
            mx = jnp.max(sc, axis=-1, keepdims=True)
            e = jnp.exp(sc - mx)
            den = jnp.sum(e, axis=-1, keepdims=True)
            o2 = _dot(e * (1.0 / den), vv)
            lse2 = mx + jnp.log(den)
            o_ref[rows, :] = jnp.where(sel0, o2[:B], o2[B:])
            l_ref[rows, :] = jnp.where(sel0, lse2[:B], lse2[B:])

        _for_each_unit(n_units, one)

    in_specs = [pl.BlockSpec(memory_space=pltpu.SMEM), cur(col0), cur(col0 + 4), cur(col0 + 8)]
    args = [_alibi_coef(), z, z, z]
    if has_prev:
        in_specs += [prev(col0 + 4), prev(col0 + 8)]
        args += [z, z]
    out = jax.ShapeDtypeStruct((SEQ, ATT_WIDTH), F32)
    return pl.pallas_call(
        body, name=name, grid=(HEAD_PAIRS, n_slabs), in_specs=in_specs,
        out_specs=[cur(0), cur(0)], out_shape=[out, out],
        compiler_params=pltpu.CompilerParams(dimension_semantics=("parallel", "arbitrary")),
    )(*args)


def att_bwd(z, l, do, corr, g, name):
    B = ATT_BLOCK
    d = ATT_GROUPS[g][1]
    n_blocks = SEQ // (d * B)
    multi = n_blocks > 1
    col0 = (ATT_COL0 + g * 3 * ATT_WIDTH) // 128
    own = slice(B, 2 * B) if multi else slice(0, B)

    def body(coef_ref, q_ref, k_ref, v_ref, l_ref, do_ref, cr_ref, dq_ref, dk_ref, dv_ref, dq_sc, dk_sc, dv_sc):
        hp = pl.program_id(0)
        cf2 = _pair_coef(coef_ref, g, hp)
        sel0 = _head_lanes(0)

        def block_rows(b, r):
            return pl.ds(b * (B * d) + r, B, stride=d) if d > 1 else pl.ds(pl.multiple_of(b * B, B), B)

        def one(u):
            b, r = (u, 0) if d == 1 else (u // d, u % d)
            rows = block_rows(b, r)
            valid, dist = _band(multi, jnp.where(b == 0, B, 0))
            kk, vv = k_ref[rows, :], v_ref[rows, :]
            if multi:
                prev_rows = block_rows(jnp.maximum(b - 1, 0), r)
                kk = jnp.concatenate([k_ref[prev_rows, :], kk], axis=0)
                vv = jnp.concatenate([v_ref[prev_rows, :], vv], axis=0)
            q2, do2 = _stack_heads(q_ref[rows, :], sel0), _stack_heads(do_ref[rows, :], sel0)
            keys = kk.shape[0]
            lse2, cr2 = _stack_values(l_ref[rows, :], sel0, keys), _stack_values(cr_ref[rows, :], sel0, keys)
            p = jnp.exp(jnp.where(valid, _dot(q2, kk, tb=True) * 0.125 - cf2 * dist, NEG_INF) - lse2)
            ds = p * (_dot(do2, vv, tb=True) + cr2)
            dq2 = _dot(ds, kk)
            dkk = _dot(ds, q2, ta=True) * 0.125
            dvv = _dot(p, do2, ta=True)
            dq_sc[rows, :] = jnp.where(sel0, dq2[:B], dq2[B:]) * 0.125
            dk_sc[rows, :] = dkk[own]
            dv_sc[rows, :] = dvv[own]
            if multi:
                dk_sc[prev_rows, :] += dkk[:B]
                dv_sc[prev_rows, :] += dvv[:B]

        _for_each_unit(d * n_blocks, one)
        dq_ref[...] = dq_sc[...].astype(dq_ref.dtype)
        dk_ref[...] = dk_sc[...].astype(dk_ref.dtype)
        dv_ref[...] = dv_sc[...].astype(dv_ref.dtype)

    def col(c):
        return pl.BlockSpec((SEQ, 128), lambda hp: (0, c + hp))

    out = jax.ShapeDtypeStruct((SEQ, ATT_WIDTH), MXU_DTYPE)
    return pl.pallas_call(
        body, name=name, grid=(HEAD_PAIRS,),
        in_specs=[pl.BlockSpec(memory_space=pltpu.SMEM), col(col0), col(col0 + 4), col(col0 + 8), col(0), col(0), col(0)],
        out_specs=[col(0)] * 3, out_shape=[out] * 3,
        scratch_shapes=[pltpu.VMEM((SEQ, 128), F32)] * 3,
        compiler_params=pltpu.CompilerParams(dimension_semantics=("parallel",), vmem_limit_bytes=MATMUL_VMEM_BYTES),
    )(_alibi_coef(), z, z, z, l, do, corr)


def _head_sum(x):
    i = lax.broadcasted_iota(jnp.int32, (128, 128), 0) // 64
    j = lax.broadcasted_iota(jnp.int32, (128, 128), 1) // 64
    return _dot_f32(x, (i == j).astype(F32), ones_on_right=True)


def _group_weights(l0, l1, l2):
    mx = jnp.maximum(jnp.maximum(l0, l1), l2)
    e0, e1, e2 = jnp.exp(l0 - mx), jnp.exp(l1 - mx), jnp.exp(l2 - mx)
    inv = 1.0 / (e0 + e1 + e2)
    return e0 * inv, e1 * inv, e2 * inv


def att_combine_fwd(o, l, name):
    def body(o0, o1, o2, l0, l1, l2, y_ref):
        w0, w1, w2 = _group_weights(l0[...], l1[...], l2[...])
        y_ref[...] = (o0[...] * w0 + o1[...] * w1 + o2[...] * w2).astype(y_ref.dtype)

    blk = pl.BlockSpec((ROW_TILE, ATT_WIDTH), lambda i: (i, 0))
    return pl.pallas_call(
        body, name=name, grid=(SEQ // ROW_TILE,), in_specs=[blk] * 6, out_specs=blk,
        out_shape=jax.ShapeDtypeStruct((SEQ, ATT_WIDTH), MXU_DTYPE),
    )(*o, *l)


def att_combine_bwd(o, l, dy, name):
    def body(o0, o1, o2, l0, l1, l2, dy_ref, do0, do1, do2, cr0, cr1, cr2):
        w = _group_weights(l0[...], l1[...], l2[...])
        dyv = dy_ref[...]
        tot = _head_sum(dyv * (w[0] * o0[...] + w[1] * o1[...] + w[2] * o2[...]))
        for g, (do_ref, cr_ref) in enumerate(((do0, cr0), (do1, cr1), (do2, cr2))):
            do_ref[...] = dyv * w[g]
            cr_ref[...] = -w[g] * tot

    blk = pl.BlockSpec((ROW_TILE, 128), lambda i, j: (i, j))
    out = jax.ShapeDtypeStruct((SEQ, ATT_WIDTH), F32)
    res = pl.pallas_call(
        body, name=name, grid=(SEQ // ROW_TILE, HEAD_PAIRS), in_specs=[blk] * 7, out_specs=[blk] * 6, out_shape=[out] * 6,
    )(*o, *l, dy)
    return res[:N_GROUPS], res[N_GROUPS:]


SUM_ROW_TILES = (1024, 512, 256, 128, 64, 32, 16)
SUM_TILE_BYTES = 24 * 1024 * 1024
SUM_PARAMS = pltpu.CompilerParams(vmem_limit_bytes=MATMUL_VMEM_BYTES)


def _row_tile(rows, cols, operands):
    fit = [t for t in SUM_ROW_TILES if rows % t == 0]
    return next((t for t in fit if 2 * 4 * operands * t * cols <= SUM_TILE_BYTES), fit[-1])


def _shard_shape(rows, cols, axis):
    return (rows // N_CHIPS, cols) if axis == 0 else (rows, cols // N_CHIPS)


def _half_shape(rows, cols, axis):
    return (rows, cols // 2) if axis == 0 else (rows // 2, cols)


def _piece_shape(rows, cols, axis):
    return (rows // N_CHIPS, cols // 2) if axis == 0 else (rows // 2, cols // N_CHIPS)


def place_own_block(shard, chip, rows, cols, axis, name):
    sr, sc = _shard_shape(rows, cols, axis)
    tr = _row_tile(sr, sc, 2)

    def body(chip_ref, s_ref, o_ref):
        o_ref[...] = s_ref[...].astype(o_ref.dtype)

    if axis == 0:
        out_map = lambda i, chip_ref: (chip_ref[0] * (sr // tr) + i, 0)
    else:
        out_map = lambda i, chip_ref: (i, chip_ref[0])
    return pl.pallas_call(
        body, name=name, out_shape=jax.ShapeDtypeStruct((rows, cols), WEIGHT_COMM_DTYPE), compiler_params=SUM_PARAMS,
        grid_spec=pltpu.PrefetchScalarGridSpec(
            num_scalar_prefetch=1, grid=(sr // tr,), in_specs=[pl.BlockSpec((tr, sc), lambda i, chip_ref: (i, 0))],
            out_specs=pl.BlockSpec((tr, sc), out_map)),
    )(chip, shard)


def add_halves(g, theirs, core, rows, cols, axis, name):
    hr, hc = _half_shape(rows, cols, axis)
    tr = _row_tile(hr, hc, 3)

    def body(core_ref, g_ref, t_ref, o_ref):
        o_ref[...] = (g_ref[...].astype(F32) + t_ref[...].astype(F32)).astype(o_ref.dtype)

    if axis == 0:
        g_map = lambda i, core_ref: (i, core_ref[0])
    else:
        g_map = lambda i, core_ref: (core_ref[0] * (hr // tr) + i, 0)
    blk = pl.BlockSpec((tr, hc), lambda i, core_ref: (i, 0))
    return pl.pallas_call(
        body, name=name, out_shape=jax.ShapeDtypeStruct((hr, hc), GRAD_COMM_DTYPE), compiler_params=SUM_PARAMS,
        grid_spec=pltpu.PrefetchScalarGridSpec(
            num_scalar_prefetch=1, grid=(hr // tr,), in_specs=[pl.BlockSpec((tr, hc), g_map), blk], out_specs=blk),
    )(core, g, theirs)


def add_pieces(half, got, chip, rows, cols, axis, name):
    hr, _ = _half_shape(rows, cols, axis)
    pr, pc = _piece_shape(rows, cols, axis)
    tr = _row_tile(pr, pc, 5)

    def body(chip_ref, h_ref, got_ref, o_ref):
        o_ref[...] = (h_ref[...].astype(F32) + got_ref[0].astype(F32) + got_ref[1].astype(F32) + got_ref[2].astype(F32))

    if axis == 0:
        h_map = lambda i, chip_ref: (chip_ref[0] * (pr // tr) + i, 0)
    else:
        h_map = lambda i, chip_ref: (i, chip_ref[0])
    return pl.pallas_call(
        body, name=name, out_shape=jax.ShapeDtypeStruct((pr, pc), F32), compiler_params=SUM_PARAMS,
        grid_spec=pltpu.PrefetchScalarGridSpec(
            num_scalar_prefetch=1, grid=(pr // tr,),
            in_specs=[pl.BlockSpec((tr, pc), h_map), pl.BlockSpec((3, tr, pc), lambda i, chip_ref: (0, i, 0))],
            out_specs=pl.BlockSpec((tr, pc), lambda i, chip_ref: (i, 0))),
    )(chip, half, got)


def _adamw_math(w, g, m, v):
    nm = ADAM_B1 * m + (1.0 - ADAM_B1) * g
    nv = ADAM_B2 * v + (1.0 - ADAM_B2) * (g * g)
    m_hat = nm / (1.0 - ADAM_B1 ** ADAM_STEP)
    v_hat = nv / (1.0 - ADAM_B2 ** ADAM_STEP)
    return -ADAM_LR * (m_hat / (jnp.sqrt(v_hat) + ADAM_EPS) + ADAM_WD * w), nm, nv


def adamw_halves(w, mine, theirs, m, v, core, rows, cols, axis, name):
    sr, sc = _shard_shape(rows, cols, axis)
    pr, pc = _piece_shape(rows, cols, axis)
    tr = _row_tile(pr, pc, 9)
    nt = pr // tr

    def body(core_ref, w_ref, a_ref, b_ref, m_ref, v_ref, g_ref, d_ref, nm_ref, nv_ref):
        g = jnp.where(pl.program_id(0) == core_ref[0], a_ref[...], b_ref[...])
        g_ref[...] = g
        d_ref[...], nm_ref[...], nv_ref[...] = _adamw_math(w_ref[...], g, m_ref[...], v_ref[...])

    if axis == 0:
        full = pl.BlockSpec((tr, pc), lambda h, i, core_ref: (i, h))
    else:
        full = pl.BlockSpec((tr, pc), lambda h, i, core_ref: (h * nt + i, 0))
    part = pl.BlockSpec((tr, pc), lambda h, i, core_ref: (i, 0))
    out = jax.ShapeDtypeStruct((sr, sc), F32)
    return pl.pallas_call(
        body, name=name, out_shape=[out, out, out, out], compiler_params=SUM_PARAMS,
        grid_spec=pltpu.PrefetchScalarGridSpec(
            num_scalar_prefetch=1, grid=(2, nt), in_specs=[full, part, part, full, full], out_specs=[full] * 4),
    )(core, w, mine, theirs, m, v)


BIG = (
    ("ffn1_w_gate_up", D_MODEL, 2 * D_FF, 1),
    ("ffn1_w_down", D_FF, D_MODEL, 0),
    ("w_in", D_MODEL, IN_COLS, 1),
    ("w_branch_hg", HG_WIDTH, D_MODEL, 1),
    ("w_branch_att", ATT_WIDTH, D_MODEL, 1),
    ("w_out", D_MODEL, D_MODEL, 0),
    ("ffn2_w_gate_up", D_MODEL, 2 * D_FF, 1),
    ("ffn2_w_down", D_FF, D_MODEL, 0),
)
N_BIG = len(BIG)
ANY = pl.BlockSpec(memory_space=pl.ANY)


def _place():
    return lax.axis_index("x"), lax.axis_index("y"), lax.axis_index("c")


def _other_chips(x, y):
    return ((1 - x, y), (x, 1 - y), (1 - x, 1 - y))


MAX_COPY_CHUNKS = 16
CHUNK_ROW_ALIGN = 16


def _row_chunks(view):
    rows = view.shape[0]
    n = next(n for n in range(MAX_COPY_CHUNKS, 0, -1) if rows % (CHUNK_ROW_ALIGN * n) == 0 or n == 1)
    step = rows // n
    return [pl.ds(i * step, step) for i in range(n)]


def _remote(src, dst, send_sem, recv_sem, device):
    return pltpu.make_async_remote_copy(src_ref=src, dst_ref=dst, send_sem=send_sem, recv_sem=recv_sem,
                                        device_id=device, device_id_type=MESH)


def _start_remote(src, dst, send_sem, recv_sem, device):
    for rows in _row_chunks(src):
        _remote(src.at[rows, :], dst.at[rows, :], send_sem, recv_sem, device).start()
    return _remote(src, dst, send_sem, recv_sem, device)


HBM = pl.BlockSpec(memory_space=pltpu.HBM)
SEM = pl.BlockSpec(memory_space=pltpu.SEMAPHORE)
SPLIT_COPY_EFFECT = pltpu.SideEffectType.DATAFLOW_SIDE_EFFECTING
GROUPS = {"ffn1": (0, 1), "mix": (2, 3, 4, 5), "ffn2": (6, 7)}


def _in_hbm(a):
    return pltpu.with_memory_space_constraint(a, pltpu.HBM)


class _SemList:
    def __init__(self, refs):
        self.refs = refs
        self.at = self

    def __getitem__(self, index):
        w, k = index
        return self.refs[3 * w + k]


def _gather_piece(ref, rows, cols, axis, chip, c):
    sr, sc = _shard_shape(rows, cols, axis)
    j = 2 * chip[0] + chip[1]
    if axis == 0:
        return ref.at[pl.ds(j * sr + c * (sr // 2), sr // 2), :]
    return ref.at[pl.ds(c * (sr // 2), sr // 2), pl.ds(pl.multiple_of(j * sc, 128), sc)]


def _start_gather_sends(bufs, ws, send_sems, recv_sems):
    x, y, c = _place()
    for w, (_, r, cc, ax) in enumerate(ws):
        mine = _gather_piece(bufs[w], r, cc, ax, (x, y), c)
        for k, chip in enumerate(_other_chips(x, y)):
            _start_remote(mine, mine, send_sems.at[w, k], recv_sems.at[w, k], (*chip, c))


def _wait_gather_sends(bufs, ws, send_sems, recv_sems):
    x, y, c = _place()
    for w, (_, r, cc, ax) in enumerate(ws):
        for k, chip in enumerate(_other_chips(x, y)):
            got = _gather_piece(bufs[w], r, cc, ax, chip, c)
            _remote(got, got, send_sems.at[w, k], recv_sems.at[w, k], (x, y, c)).wait_recv()
    for w, (_, r, cc, ax) in enumerate(ws):
        mine = _gather_piece(bufs[w], r, cc, ax, (x, y), c)
        for k in range(3):
            _remote(mine, mine, send_sems.at[w, k], recv_sems.at[w, k], (x, y, c)).wait_send()


def _forward_halves(bufs, ws, send_sems, recv_sems):
    x, y, c = _place()
    passed = []
    for w, (_, r, cc, ax) in enumerate(ws):
        for k, chip in enumerate(_other_chips(x, y)):
            got = _gather_piece(bufs[w], r, cc, ax, chip, c)
            passed.append(_start_remote(got, got, send_sems.at[w, k], recv_sems.at[w, k], (x, y, 1 - c)))
    for w, (_, r, cc, ax) in enumerate(ws):
        for k, chip in enumerate(_other_chips(x, y)):
            got = _gather_piece(bufs[w], r, cc, ax, chip, 1 - c)
            _remote(got, got, send_sems.at[w, k], recv_sems.at[w, k], (x, y, c)).wait_recv()
    for cp in passed:
        cp.wait_send()


def gather_start(placed, after, group):
    ws = [BIG[i] for i in GROUPS[group]]
    n = len(ws)

    def body(*refs):
        bufs = refs[:n]
        send_sems, recv_sems = _SemList(refs[n + 1:4 * n + 1]), _SemList(refs[4 * n + 1:7 * n + 1])
        token = refs[-1]
        _start_gather_sends(bufs, ws, send_sems, recv_sems)
        token[...] = jnp.zeros_like(token)

    out = pl.pallas_call(
        body, name=f"gather_start_{group}", in_specs=[HBM] * n + [ANY],
        out_specs=[SEM] * (6 * n) + [HBM] * n + [pl.BlockSpec(memory_space=pltpu.VMEM)],
        out_shape=[pltpu.SemaphoreType.DMA(())] * (6 * n)
        + [pltpu.HBM((r, cc), WEIGHT_COMM_DTYPE) for _, r, cc, _ in ws] + [jax.ShapeDtypeStruct((8, 128), F32)],
        input_output_aliases={w: 6 * n + w for w in range(n)},
        compiler_params=pltpu.CompilerParams(has_side_effects=SPLIT_COPY_EFFECT),
    )(*[_in_hbm(p) for p in placed], after)
    return out[:3 * n], out[3 * n:6 * n], out[6 * n:7 * n], out[-1]


def gather_wait(bufs, send_sems, recv_sems, after, group):
    ws = [BIG[i] for i in GROUPS[group]]
    n = len(ws)

    def body(*refs):
        _wait_gather_sends(refs[:n], ws, _SemList(refs[n:n + 3 * n]), _SemList(refs[n + 3 * n:n + 6 * n]))

    return pl.pallas_call(
        body, name=f"gather_wait_{group}", in_specs=[HBM] * n + [SEM] * (6 * n) + [ANY] * len(after), out_specs=[HBM] * n,
        out_shape=[pltpu.HBM((r, cc), WEIGHT_COMM_DTYPE) for _, r, cc, _ in ws],
        input_output_aliases={w: w for w in range(n)},
        compiler_params=pltpu.CompilerParams(has_side_effects=SPLIT_COPY_EFFECT),
    )(*bufs, *send_sems, *recv_sems, *after)


def gather_forward(bufs, group):
    ws = [BIG[i] for i in GROUPS[group]]
    n = len(ws)

    def body(*refs):
        _forward_halves(refs[n:2 * n], ws, refs[2 * n], refs[2 * n + 1])

    return pl.pallas_call(
        body, name=f"gather_forward_{group}", in_specs=[ANY] * n, out_specs=[ANY] * n,
        out_shape=[jax.ShapeDtypeStruct((r, cc), WEIGHT_COMM_DTYPE) for _, r, cc, _ in ws],
        input_output_aliases={w: w for w in range(n)},
        scratch_shapes=[pltpu.SemaphoreType.DMA((n, 3))] * 2,
    )(*bufs)


def _half(ref, rows, cols, axis, c):
    if axis == 0:
        return ref.at[:, pl.ds(pl.multiple_of(c * (cols // 2), 128), cols // 2)]
    return ref.at[pl.ds(c * (rows // 2), rows // 2), :]


def _piece_of_half(ref, rows, cols, axis, chip):
    j = 2 * chip[0] + chip[1]
    pr, pc = _piece_shape(rows, cols, axis)
    if axis == 0:
        return ref.at[pl.ds(j * pr, pr), :]
    return ref.at[:, pl.ds(pl.multiple_of(j * pc, 128), pc)]


def sibling_exchange_start(srcs, view, landing_shapes, dtype, name):
    n = len(srcs)

    def body(*refs):
        ins, land, sems = refs[:n], refs[n:2 * n], refs[2 * n:4 * n]
        x, y, c = _place()
        for w in range(n):
            _start_remote(view(ins[w], w, c), land[w], sems[w], sems[n + w], (x, y, 1 - c))
        refs[-1][...] = jnp.zeros_like(refs[-1])

    landing = [lax.empty(shape, dtype) for shape in landing_shapes]
    out = pl.pallas_call(
        body, name=name, in_specs=[HBM] * (2 * n),
        out_specs=[SEM] * (2 * n) + [HBM] * (2 * n) + [pl.BlockSpec(memory_space=pltpu.VMEM)],
        out_shape=[pltpu.SemaphoreType.DMA(())] * (2 * n) + [pltpu.HBM(a.shape, a.dtype) for a in srcs]
        + [pltpu.HBM(shape, dtype) for shape in landing_shapes] + [jax.ShapeDtypeStruct((8, 128), F32)],
        input_output_aliases={i: 2 * n + i for i in range(2 * n)},
        compiler_params=pltpu.CompilerParams(has_side_effects=SPLIT_COPY_EFFECT),
    )(*[_in_hbm(a) for a in srcs], *[_in_hbm(b) for b in landing])
    return out[:n], out[n:2 * n], out[2 * n:3 * n], out[3 * n:4 * n], out[-1]


def sibling_exchange_wait(srcs, landing, send_sems, recv_sems, view, after, name):
    n = len(srcs)

    def body(*refs):
        ins, land, sems = refs[:n], refs[n:2 * n], refs[2 * n:4 * n]
        x, y, c = _place()
        for w in range(n):
            cp = _remote(view(ins[w], w, c), land[w], sems[w], sems[n + w], (x, y, c))
            cp.wait_send()
            cp.wait_recv()

    out = pl.pallas_call(
        body, name=name, in_specs=[HBM] * (2 * n) + [SEM] * (2 * n) + [ANY] * len(after), out_specs=[HBM] * (2 * n),
        out_shape=[pltpu.HBM(a.shape, a.dtype) for a in srcs] + [pltpu.HBM(b.shape, b.dtype) for b in landing],
        input_output_aliases={i: i for i in range(2 * n)},
        compiler_params=pltpu.CompilerParams(has_side_effects=SPLIT_COPY_EFFECT),
    )(*srcs, *landing, *send_sems, *recv_sems, *after)
    return out[:n], out[n:]


def _scatter_copies(halves, got, ws, send_sems, recv_sems, start):
    x, y, c = _place()
    copies = []
    for w, (_, r, cc, ax) in enumerate(ws):
        for k, chip in enumerate(_other_chips(x, y)):
            args = (_piece_of_half(halves[w], r, cc, ax, chip), got[w].at[k], send_sems.at[w, k], recv_sems.at[w, k], (*chip, c))
            copies.append(_start_remote(*args) if start else _remote(*args))
    return copies


def scatter_start(halves, group):
    ws = [BIG[i] for i in GROUPS[group]]
    n = len(ws)

    def body(*refs):
        sems = refs[2 * n:8 * n]
        _scatter_copies(refs[:n], refs[n:2 * n], ws, _SemList(sems[:3 * n]), _SemList(sems[3 * n:]), start=True)
        refs[-1][...] = jnp.zeros_like(refs[-1])

    landing = [lax.empty((3,) + _piece_shape(r, cc, ax), GRAD_COMM_DTYPE) for _, r, cc, ax in ws]
    out = pl.pallas_call(
        body, name=f"scatter_start_{group}", in_specs=[HBM] * (2 * n),
        out_specs=[SEM] * (6 * n) + [HBM] * (2 * n) + [pl.BlockSpec(memory_space=pltpu.VMEM)],
        out_shape=[pltpu.SemaphoreType.DMA(())] * (6 * n)
        + [pltpu.HBM(_half_shape(r, cc, ax), GRAD_COMM_DTYPE) for _, r, cc, ax in ws]
        + [pltpu.HBM((3,) + _piece_shape(r, cc, ax), GRAD_COMM_DTYPE) for _, r, cc, ax in ws]
        + [jax.ShapeDtypeStruct((8, 128), F32)],
        input_output_aliases={i: 6 * n + i for i in range(2 * n)},
        compiler_params=pltpu.CompilerParams(has_side_effects=SPLIT_COPY_EFFECT),
    )(*[_in_hbm(h) for h in halves], *[_in_hbm(b) for b in landing])
    return out[:3 * n], out[3 * n:6 * n], out[6 * n:7 * n], out[7 * n:8 * n], out[-1]


def scatter_wait(halves, got, send_sems, recv_sems, after, group):
    ws = [BIG[i] for i in GROUPS[group]]
    n = len(ws)

    def body(*refs):
        sems = refs[2 * n:8 * n]
        for cp in _scatter_copies(refs[:n], refs[n:2 * n], ws, _SemList(sems[:3 * n]), _SemList(sems[3 * n:]), start=False):
            cp.wait_send()
            cp.wait_recv()

    out = pl.pallas_call(
        body, name=f"scatter_wait_{group}", in_specs=[HBM] * (2 * n) + [SEM] * (6 * n) + [ANY] * len(after),
        out_specs=[HBM] * (2 * n),
        out_shape=[pltpu.HBM(_half_shape(r, cc, ax), GRAD_COMM_DTYPE) for _, r, cc, ax in ws]
        + [pltpu.HBM((3,) + _piece_shape(r, cc, ax), GRAD_COMM_DTYPE) for _, r, cc, ax in ws],
        input_output_aliases={i: i for i in range(2 * n)},
        compiler_params=pltpu.CompilerParams(has_side_effects=SPLIT_COPY_EFFECT),
    )(*halves, *got, *send_sems, *recv_sems, *after)
    return out[:n], out[n:]


N_DEV = 8
SMALL = ("ffn1_norm", "mix_norm", "hg_lower_bounds", "hg_out_norm", "ffn2_norm", "final_norm")
SMALL_STAGE_ROWS = 8


def small_step(loss, grads, w, m, v, behind):
    n = len(SMALL)
    shapes = [g.shape for g in grads]
    first_row = [sum(s[0] for s in shapes[:i]) for i in range(n + 1)]
    assert first_row[n] < SMALL_STAGE_ROWS
    loss_row = (pl.ds(first_row[n], 1), pl.ds(0, loss.shape[1]))

    def body(*refs):
        loss_ref, g_refs, w_refs, m_refs, v_refs = refs[0], refs[1:1 + n], refs[1 + n:1 + 2 * n], refs[1 + 2 * n:1 + 3 * n], refs[1 + 3 * n:1 + 4 * n]
        outs = refs[2 + 4 * n:3 + 8 * n]
        loss_out, dg_refs, d_refs, nm_refs, nv_refs = outs[0], outs[1:1 + n], outs[1 + n:1 + 2 * n], outs[1 + 2 * n:1 + 3 * n], outs[1 + 3 * n:]
        stage, gathered, send_sems, recv_sems = refs[3 + 8 * n:]
        x, y, c = _place()
        me = 4 * x + 2 * y + c

        def slot(i, shape):
            return pl.ds(first_row[i], shape[0]), pl.ds(0, shape[1])

        stage[...] = jnp.zeros_like(stage)
        for i, g_ref in enumerate(g_refs):
            stage[slot(i, shapes[i])] = g_ref[...]
        stage[loss_row] = loss_ref[pl.ds(0, 1), :]
        gathered[me] = stage[...]
        copies = []
        for k in range(1, N_DEV):
            peer = (x ^ (k >> 2), y ^ ((k >> 1) & 1), c ^ (k & 1))
            cp = pltpu.make_async_remote_copy(
                src_ref=stage, dst_ref=gathered.at[me], send_sem=send_sems.at[k - 1], recv_sem=recv_sems.at[k - 1],
                device_id=peer, device_id_type=MESH)
            cp.start()
            copies.append(cp)
        for cp in copies:
            cp.wait()
        acc = gathered[0]
        for k in range(1, N_DEV):
            acc = acc + gathered[k]
        stage[...] = acc
        loss_out[...] = jnp.broadcast_to(stage[loss_row], loss_out.shape)
        for i in range(n):
            g = stage[slot(i, shapes[i])]
            dg_refs[i][...] = g
            d_refs[i][...], nm_refs[i][...], nv_refs[i][...] = _adamw_math(w_refs[i][...], g, m_refs[i][...], v_refs[i][...])

    vm = pl.BlockSpec(memory_space=pltpu.VMEM)
    per_param = [jax.ShapeDtypeStruct(s, F32) for s in shapes]
    out = pl.pallas_call(
        body, name="small_step", in_specs=[vm] * (1 + 4 * n) + [ANY], out_specs=[vm] * (1 + 4 * n),
        out_shape=[jax.ShapeDtypeStruct(loss.shape, F32)] + per_param * 4,
        scratch_shapes=[pltpu.VMEM((SMALL_STAGE_ROWS, D_MODEL), F32),
                        pltpu.VMEM((N_DEV, SMALL_STAGE_ROWS, D_MODEL), F32),
                        pltpu.SemaphoreType.DMA((N_DEV - 1,)), pltpu.SemaphoreType.DMA((N_DEV - 1,))],
    )(loss, *grads, *w, *m, *v, behind)
    return out[0], out[1:1 + n], out[1 + n:1 + 2 * n], out[1 + 2 * n:1 + 3 * n], out[1 + 3 * n:]


def _swiglu_block_fwd(h, norm_g, w_gu, w_down, tag, behind=()):
    n = rmsnorm_fwd(h, norm_g, f"{tag}_norm", behind=behind)
    a, b, s = gate_up_swiglu(n, w_gu, f"{tag}_gate_up")
    h_out = matmul(s, w_down, res=h, scale=0.5, name=f"{tag}_down")
    return h_out, (n, a, b, s)


def _swiglu_block_bwd(h, norm_g, w_gu, w_down, saved, dh_out, df, tag, exchange, behind=()):
    n, a, b, s = saved
    d_down = matmul(s, df, ta=True, scale=0.5, out_dtype=GRAD_COMM_DTYPE, name=f"{tag}_d_w_down")
    dgu = d_gate_up(df, w_down, a, b, 0.5, behind, f"{tag}_d_gate_up")
    d_gu = matmul(n, dgu, ta=True, out_dtype=GRAD_COMM_DTYPE, name=f"{tag}_d_w_gate_up")
    tokens = exchange.gradients_ready(tag, {f"{tag}_w_gate_up": d_gu, f"{tag}_w_down": d_down})
    dn = matmul(dgu, w_gu, tb=True, behind=tokens, name=f"{tag}_d_n")
    dh, dh_m, dg = rmsnorm_bwd(h, norm_g, dn, dh_out, f"{tag}_norm_bwd")
    return dh, dh_m, dg


def local_step(x, target, small, exchange):
    big = {}
    token, big_ffn1 = exchange.weights("ffn1", x)
    big.update(big_ffn1)
    h1, saved1 = _swiglu_block_fwd(x, small["ffn1_norm"], big["ffn1_w_gate_up"], big["ffn1_w_down"], "ffn1", token)
    token, big_mix = exchange.weights("mix", h1)
    big.update(big_mix)
    u = rmsnorm_fwd(h1, small["mix_norm"], "mix_norm", behind=token)
    z = matmul(u, big["w_in"], name="w_in")
    p = small["hg_lower_bounds"]
    lb = 1.0 / (1.0 + jnp.exp(p[1:2] - p[0:1]))
    y_hg, o_raw, states = hgrn_fwd(z, lb, small["hg_out_norm"], "hgrn_fwd")
    o_att, l_att = zip(*[att_fwd(z, g, f"att_fwd_{g}") for g in range(N_GROUPS)])
    y_att = att_combine_fwd(o_att, l_att, "att_combine")
    bh = matmul(y_hg, big["w_branch_hg"], name="branch_hg")
    ba = matmul(y_att, big["w_branch_att"], name="branch_att")
    merged = merge_fwd(z, bh, ba, "merge")
    h2 = matmul(merged, big["w_out"], res=h1, name="w_out")
    token, big_ffn2 = exchange.weights("ffn2", h2)
    big.update(big_ffn2)
    h3, saved2 = _swiglu_block_fwd(h2, small["ffn2_norm"], big["ffn2_w_gate_up"], big["ffn2_w_down"], "ffn2", token)
    dh3, dh3_m, d_final, loss = final_norm_loss(h3, small["final_norm"], target, "final_norm_loss")

    gs, gb = {"final_norm": d_final}, {}
    dh2, dh2_m, gs["ffn2_norm"] = _swiglu_block_bwd(
        h2, small["ffn2_norm"], big["ffn2_w_gate_up"], big["ffn2_w_down"], saved2, dh3, dh3_m, "ffn2", exchange)
    token = exchange.backward_done("ffn2", dh2)
    gb["w_out"] = matmul(merged, dh2_m, ta=True, out_dtype=GRAD_COMM_DTYPE, name="d_w_out")
    dmerged = matmul(dh2_m, big["w_out"], tb=True, behind=token, name="d_merged")
    dbh, dba, dgh, dga = merge_bwd(z, bh, ba, dmerged, "merge_bwd")
    gb["w_branch_hg"] = matmul(y_hg, dbh, ta=True, out_dtype=GRAD_COMM_DTYPE, name="d_w_branch_hg")
    gb["w_branch_att"] = matmul(y_att, dba, ta=True, out_dtype=GRAD_COMM_DTYPE, name="d_w_branch_att")
    dy_hg = matmul(dbh, big["w_branch_hg"], tb=True, name="d_y_hg")
    dy_att = matmul(dba, big["w_branch_att"], tb=True, name="d_y_att")
    dq, dfp, di, dog, d_lb, gs["hg_out_norm"] = hgrn_bwd(z, lb, small["hg_out_norm"], o_raw, states, dy_hg, "hgrn_bwd")
    do_att, corr = att_combine_bwd(o_att, l_att, dy_att, "att_combine_bwd")
    d_att = [part for g in range(N_GROUPS) for part in att_bwd(z, l_att[g], do_att[g], corr[g], g, f"att_bwd_{g}")]
    dz = jnp.concatenate([dq, dfp, di, dog, *d_att, dgh, dga], axis=1)
    gb["w_in"] = matmul(u, dz, ta=True, out_dtype=GRAD_COMM_DTYPE, name="d_w_in")
    token = exchange.gradients_ready("mix", gb)
    du = matmul(dz, big["w_in"], tb=True, behind=token, name="d_u")
    dh1, dh1_m, gs["mix_norm"] = rmsnorm_bwd(h1, small["mix_norm"], du, dh2, "mix_norm_bwd")
    token = exchange.backward_done("mix", dh1)
    dp0 = d_lb * lb * (1.0 - lb)
    gs["hg_lower_bounds"] = jnp.concatenate([dp0, -dp0], axis=0)
    dx, _, gs["ffn1_norm"] = _swiglu_block_bwd(
        x, small["ffn1_norm"], big["ffn1_w_gate_up"], big["ffn1_w_down"], saved1, dh1, dh1_m, "ffn1", exchange, token)
    exchange.backward_done("ffn1", dx)
    return loss, dx, gs


WEIGHTS = ("ffn1_norm", "ffn1_w_gate_up", "ffn1_w_down", "mix_norm", "w_in", "hg_lower_bounds", "hg_out_norm",
           "w_branch_hg", "w_branch_att", "w_out", "ffn2_norm", "ffn2_w_gate_up", "ffn2_w_down", "final_norm")


class WeightExchange:
    ORDER = ("ffn1", "mix", "ffn2")

    def __init__(self, shards, core, chip):
        self.core, self.chip = core, chip
        self.halving = None
        self.scattering = None
        self.reducing = {}
        first = self.ORDER[0]
        self.placed = {BIG[i][0]: place_own_block(shards[BIG[i][0]], chip, *BIG[i][1:], f"place_{BIG[i][0]}")
                       for i in GROUPS[first]}
        self._start_gather(first, self.placed[self._names(first)[0]])
        chip_behind = chip + self.token[0, :1].astype(jnp.int32)
        for group in self.ORDER[1:]:
            for i in GROUPS[group]:
                n, r, cc, ax = BIG[i]
                self.placed[n] = place_own_block(shards[n], chip_behind, r, cc, ax, f"place_{n}")
        self.placed_behind = [self.placed[n] for group in self.ORDER[1:] for n in self._names(group)]

    def _names(self, group):
        return [BIG[i][0] for i in GROUPS[group]]

    def _start_gather(self, group, after):
        send_sems, recv_sems, bufs, self.token = gather_start([self.placed[n] for n in self._names(group)], after, group)
        self.gathering = (group, send_sems, recv_sems, bufs)

    def weights(self, group, h):
        pending, send_sems, recv_sems, bufs = self.gathering
        assert pending == group
        after = self.placed_behind if group == self.ORDER[0] else [h]
        whole = gather_forward(gather_wait(bufs, send_sems, recv_sems, after, group), group)
        later = self.ORDER.index(group) + 1
        behind = []
        if later < len(self.ORDER):
            self._start_gather(self.ORDER[later], whole[0])
            behind = [self.token]
        return behind, dict(zip(self._names(group), whole))

    @staticmethod
    def _half_to_sibling(ws):
        return lambda ref, w, c: _half(ref, *ws[w][1:], 1 - c)

    def gradients_ready(self, group, grads):
        ws = [BIG[i] for i in GROUPS[group]]
        send_sems, recv_sems, own, theirs, token = sibling_exchange_start(
            [grads[n] for n, *_ in ws], self._half_to_sibling(ws), [_half_shape(r, cc, ax) for _, r, cc, ax in ws],
            GRAD_COMM_DTYPE, f"halves_start_{group}")
        self.halving = (group, send_sems, recv_sems, own, theirs)
        return [token]

    def backward_done(self, group, dh):
        behind = [self._finish_scatter([dh])] if self.scattering is not None else []
        pending, send_sems, recv_sems, own, theirs = self.halving
        assert pending == group
        ws = [BIG[i] for i in GROUPS[group]]
        own, theirs = sibling_exchange_wait(own, theirs, send_sems, recv_sems, self._half_to_sibling(ws), [dh],
                                            f"halves_wait_{group}")
        halves = [add_halves(g, t, self.core, r, cc, ax, f"add_halves_{n}") for (n, r, cc, ax), g, t in zip(ws, own, theirs)]
        send_sems, recv_sems, halves, got, self.token = scatter_start(halves, group)
        self.scattering = (group, send_sems, recv_sems, halves, got)
        return behind + [self.token]

    def _finish_scatter(self, after):
        group, send_sems, recv_sems, halves, got = self.scattering
        halves, got = scatter_wait(halves, got, send_sems, recv_sems, after, group)
        ws = [BIG[i] for i in GROUPS[group]]
        mine = [add_pieces(h, g, self.chip, r, cc, ax, f"add_pieces_{n}") for (n, r, cc, ax), h, g in zip(ws, halves, got)]
        send_sems, recv_sems, mine, theirs, token = sibling_exchange_start(
            mine, lambda ref, w, c: ref, [_piece_shape(r, cc, ax) for _, r, cc, ax in ws], F32, f"reduced_start_{group}")
        self.reducing[group] = (send_sems, recv_sems, mine, theirs)
        self.scattering = None
        return token

    def finish(self, after):
        return self._finish_scatter(after)

    def reduced_halves(self, group, after):
        send_sems, recv_sems, mine, theirs = self.reducing.pop(group)
        mine, theirs = sibling_exchange_wait(mine, theirs, send_sems, recv_sems, lambda ref, w, c: ref, after,
                                             f"reduced_wait_{group}")
        return {BIG[i][0]: (a, b) for i, a, b in zip(GROUPS[group], mine, theirs)}


def kernel(x, ffn1_norm, ffn1_w_gate_up, ffn1_w_down, mix_norm, w_in, hg_lower_bounds, hg_out_norm, w_branch_hg, w_branch_att, w_out, ffn2_norm, ffn2_w_gate_up, ffn2_w_down, final_norm, loss_target, m_ffn1_norm, m_ffn1_w_gate_up, m_ffn1_w_down, m_mix_norm, m_w_in, m_hg_lower_bounds, m_hg_out_norm, m_w_branch_hg, m_w_branch_att, m_w_out, m_ffn2_norm, m_ffn2_w_gate_up, m_ffn2_w_down, m_final_norm, v_ffn1_norm, v_ffn1_w_gate_up, v_ffn1_w_down, v_mix_norm, v_w_in, v_hg_lower_bounds, v_hg_out_norm, v_w_branch_hg, v_w_branch_att, v_w_out, v_ffn2_norm, v_ffn2_w_gate_up, v_ffn2_w_down, v_final_norm):
    w = dict(ffn1_norm=ffn1_norm, ffn1_w_gate_up=ffn1_w_gate_up, ffn1_w_down=ffn1_w_down, mix_norm=mix_norm, w_in=w_in,
             hg_lower_bounds=hg_lower_bounds, hg_out_norm=hg_out_norm, w_branch_hg=w_branch_hg, w_branch_att=w_branch_att,
             w_out=w_out, ffn2_norm=ffn2_norm, ffn2_w_gate_up=ffn2_w_gate_up, ffn2_w_down=ffn2_w_down, final_norm=final_norm)
    m = dict(ffn1_norm=m_ffn1_norm, ffn1_w_gate_up=m_ffn1_w_gate_up, ffn1_w_down=m_ffn1_w_down, mix_norm=m_mix_norm,
             w_in=m_w_in, hg_lower_bounds=m_hg_lower_bounds, hg_out_norm=m_hg_out_norm, w_branch_hg=m_w_branch_hg,
             w_branch_att=m_w_branch_att, w_out=m_w_out, ffn2_norm=m_ffn2_norm, ffn2_w_gate_up=m_ffn2_w_gate_up,
             ffn2_w_down=m_ffn2_w_down, final_norm=m_final_norm)
    v = dict(ffn1_norm=v_ffn1_norm, ffn1_w_gate_up=v_ffn1_w_gate_up, ffn1_w_down=v_ffn1_w_down, mix_norm=v_mix_norm,
             w_in=v_w_in, hg_lower_bounds=v_hg_lower_bounds, hg_out_norm=v_hg_out_norm, w_branch_hg=v_w_branch_hg,
             w_branch_att=v_w_branch_att, w_out=v_w_out, ffn2_norm=v_ffn2_norm, ffn2_w_gate_up=v_ffn2_w_gate_up,
             ffn2_w_down=v_ffn2_w_down, final_norm=v_final_norm)

    core = lax.axis_index("c").astype(jnp.int32).reshape(1)
    chip = (2 * lax.axis_index("x") + lax.axis_index("y")).astype(jnp.int32).reshape(1)
    exchange = WeightExchange({n: w[n][0] for n, *_ in BIG}, core, chip)
    small = {n: w[n] for n in SMALL}
    small["final_norm"] = final_norm.reshape(1, D_MODEL)

    loss, dx, gs = local_step(x[0], loss_target[0], small, exchange)

    grads, delta, new_m, new_v = {}, {}, {}, {}

    def update(group, core, after):
        reduced = exchange.reduced_halves(group, after)
        for i in GROUPS[group]:
            n, r, cc, ax = BIG[i]
            a, b = reduced[n]
            g, d, nm, nv = adamw_halves(w[n][0], a, b, m[n][0], v[n][0], core, r, cc, ax, f"adamw_{n}")
            grads[n], delta[n], new_m[n], new_v[n] = g[None], d[None], nm[None], nv[None]

    core_behind = core + exchange.token[0, :1].astype(jnp.int32)
    update("ffn2", core_behind, [exchange.token])
    update("mix", core_behind, [delta["ffn2_w_down"]])
    token = exchange.finish(after=[delta[BIG[i][0]] for group in ("ffn2", "mix") for i in GROUPS[group]])
    two_d = lambda a: a.reshape(1, D_MODEL) if a.ndim == 1 else a
    loss_sum, *small_out = small_step(loss, [gs[n] for n in SMALL], *[[two_d(p[n]) for n in SMALL] for p in (w, m, v)],
                                      behind=token)
    for result, parts in zip((grads, delta, new_m, new_v), small_out):
        result.update({n: a.reshape(w[n].shape) for n, a in zip(SMALL, parts)})
    update("ffn1", core, [loss_sum])

    return (loss_sum[0, 0], dx[None], *[grads[n] for n in WEIGHTS], *[delta[n] for n in WEIGHTS],
            *[new_m[n] for n in WEIGHTS], *[new_v[n] for n in WEIGHTS])
```

```python
import numpy as np
import jax
import jax.numpy as jnp
from jax import lax
from jax.experimental import pallas as pl
from jax.experimental.pallas import tpu as pltpu

SEQ = 2048
D_MODEL = 1024
D_FF = 2816
HG_HEADS = 4
HG_DIM = 128
HG_WIDTH = 512
HG_CHUNK = 64
ATT_GROUPS = ((128, 1), (512, 4), (2048, 16))
ATT_HEADS = 8
ATT_WIDTH = 512
ATT_BLOCK = 128
ALIBI_MAX = 8.0
IN_COLS = 8704
EPS = 1e-6
NEG_INF = -1e30
ADAM_LR = 0.001
ADAM_B1 = 0.9
ADAM_B2 = 0.999
ADAM_EPS = 1e-08
ADAM_WD = 0.01
ADAM_STEP = 10

N_CHIPS = 4
MXU_DTYPE = jnp.bfloat16
WEIGHT_COMM_DTYPE = jnp.bfloat16
GRAD_COMM_DTYPE = jnp.bfloat16
ACT_DTYPE = jnp.bfloat16
MESH = pl.DeviceIdType.MESH
F32 = jnp.float32
HIGHEST = lax.Precision.HIGHEST


def _sigmoid(x):
    return 1.0 / (1.0 + jnp.exp(-x))


def _dot(a, b, ta=False, tb=False):
    dn = (((0 if ta else 1,), (1 if tb else 0,)), ((), ()))
    return lax.dot_general(a.astype(MXU_DTYPE), b.astype(MXU_DTYPE), dn, preferred_element_type=F32)


def _dot_f32(a, b, ones_on_right=False):
    x = a if ones_on_right else b
    hi = x.astype(jnp.bfloat16)
    rest = x - hi.astype(F32)
    mid = rest.astype(jnp.bfloat16)
    lo = (rest - mid.astype(F32)).astype(jnp.bfloat16)
    if ones_on_right:
        dot = lambda q: jnp.dot(q, b.astype(jnp.bfloat16), preferred_element_type=F32)
    else:
        dot = lambda q: jnp.dot(a.astype(jnp.bfloat16), q, preferred_element_type=F32)
    return dot(hi) + (dot(mid) + dot(lo))


def _split_bf16(x):
    hi = x.astype(jnp.bfloat16)
    return hi, (x - hi.astype(F32)).astype(jnp.bfloat16)


def _hdot(a, b, ta=False, tb=False):
    dn =(((0 if ta else 1,), (1 if tb else 0,)), ((), ()))
    (a_hi, a_lo), (b_hi, b_lo) = _split_bf16(a), _split_bf16(b)
    dot = lambda p, q: lax.dot_general(p, q, dn, preferred_element_type=F32)
    return dot(a_hi, b_hi) + (dot(a_lo, b_hi) + dot(a_hi, b_lo))


MATMUL_VMEM_BYTES = 48 * 1024 * 1024
MATMUL_TILE_BYTES = 36 * 1024 * 1024
MXU_ALIGN = 128


def _divisors(n, most):
    return [t for t in range(min(n, most), 0, -MXU_ALIGN) if n % t == 0 and t % MXU_ALIGN == 0]


def _matmul_tiles(M, N, K, in_bytes, out_bytes, has_res):
    best = None
    for tk in _divisors(K, K):
        nk = K // tk
        for tm in _divisors(M, 2048):
            for tn in _divisors(N, 512):
                tiles = 2 * in_bytes * (tm * tk + tk * tn) + 2 * out_bytes * tm * tn
                tiles += 4 * tm * tn * ((nk > 1) + 2 * has_res)
                if tiles > MATMUL_TILE_BYTES:
                    continue
                traffic = in_bytes * (M * K * (1 if nk == 1 else N // tn) + K * N * (M // tm))
                key = (traffic, -tm * tn * tk)
                if best is None or key < best[0]:
                    best = (key, (tm, tn, tk))
    return best[1]


def matmul(a, b, *, ta=False, tb=False, out_dtype=F32, res=None, scale=1.0, behind=(), name):
    if ta:
        K, M = a.shape
    else:
        M, K = a.shape
    if tb:
        N, K2 = b.shape
    else:
        K2, N = b.shape
    assert K == K2 and a.dtype == b.dtype
    tm, tn, tk = _matmul_tiles(M, N, K, a.dtype.itemsize, jnp.dtype(out_dtype).itemsize, res is not None)
    nk = K // tk

    def finish(r, r_ref, o_ref):
        if scale != 1.0:
            r = r * scale
        if res is not None:
            r = r_ref[...] + r
        o_ref[...] = r.astype(out_dtype)

    def body(*refs):
        a_ref, b_ref = refs[:2]
        r_ref = refs[2] if res is not None else None
        o_ref = refs[2 + (res is not None) + len(behind)]
        if nk == 1:
            finish(_dot(a_ref[...], b_ref[...], ta, tb), r_ref, o_ref)
            return
        acc = refs[-1]
        k = pl.program_id(2)

        @pl.when(k == 0)
        def _():
            acc[...] = jnp.zeros_like(acc)

        acc[...] += _dot(a_ref[...], b_ref[...], ta, tb)

        @pl.when(k == nk - 1)
        def _():
            finish(acc[...], r_ref, o_ref)

    a_spec = pl.BlockSpec((tk, tm), lambda i, j, k: (k, i)) if ta else pl.BlockSpec((tm, tk), lambda i, j, k: (i, k))
    b_spec = pl.BlockSpec((tn, tk), lambda i, j, k: (j, k)) if tb else pl.BlockSpec((tk, tn), lambda i, j, k: (k, j))
    in_specs = [a_spec, b_spec]
    args = [a, b]
    if res is not None:
        in_specs.append(pl.BlockSpec((tm, tn), lambda i, j, k: (i, j)))
        args.append(res)
    for earlier in behind:
        in_specs.append(pl.BlockSpec(memory_space=pl.ANY))
        args.append(earlier)
    return pl.pallas_call(
        body, name=name, grid=(M // tm, N // tn, nk), in_specs=in_specs,
        out_specs=pl.BlockSpec((tm, tn), lambda i, j, k: (i, j)),
        out_shape=jax.ShapeDtypeStruct((M, N), out_dtype),
        scratch_shapes=[pltpu.VMEM((tm, tn), F32)] if nk > 1 else [],
        compiler_params=pltpu.CompilerParams(dimension_semantics=("parallel", "parallel", "arbitrary"),
                                             vmem_limit_bytes=MATMUL_VMEM_BYTES),
    )(*args)


ROW_TILE = 256


def rmsnorm_fwd(x, g, name, behind=()):
    def body(x_ref, g_ref, *refs):
        n_ref = refs[-1]
        xv = x_ref[...]
        r = lax.rsqrt(jnp.mean(xv * xv, axis=-1, keepdims=True) + EPS)
        n_ref[...] = ((xv * r) * g_ref[...]).astype(n_ref.dtype)

    order = list(behind)
    return pl.pallas_call(
        body, name=name, grid=(SEQ // ROW_TILE,),
        in_specs=[pl.BlockSpec((ROW_TILE, D_MODEL), lambda i: (i, 0)), pl.BlockSpec((1, D_MODEL), lambda i: (0, 0))]
        + [pl.BlockSpec(memory_space=pl.ANY)] * len(order),
        out_specs=pl.BlockSpec((ROW_TILE, D_MODEL), lambda i: (i, 0)),
        out_shape=jax.ShapeDtypeStruct((SEQ, D_MODEL), MXU_DTYPE),
    )(x, g, *order)


def rmsnorm_bwd(x, g, dn, dres, name):
    def body(x_ref, g_ref, dn_ref, dr_ref, dx_ref, dxm_ref, dg_ref):
        xv = x_ref[...]
        r = lax.rsqrt(jnp.mean(xv * xv, axis=-1, keepdims=True) + EPS)
        xh = xv * r
        dnv = dn_ref[...]

        @pl.when(pl.program_id(0) == 0)
        def _():
            dg_ref[...] = jnp.zeros_like(dg_ref)

        dg_ref[...] += jnp.sum(dnv * xh, axis=0, keepdims=True)
        dxh = dnv * g_ref[...]
        dx = dr_ref[...] + r * (dxh - xh * jnp.mean(dxh * xh, axis=-1, keepdims=True))
        dx_ref[...] = dx
        dxm_ref[...] = dx.astype(dxm_ref.dtype)

    row = pl.BlockSpec((ROW_TILE, D_MODEL), lambda i: (i, 0))
    vec = pl.BlockSpec((1, D_MODEL), lambda i: (0, 0))
    return pl.pallas_call(
        body, name=name, grid=(SEQ // ROW_TILE,), in_specs=[row, vec, row, row], out_specs=[row, row, vec],
        out_shape=[jax.ShapeDtypeStruct((SEQ, D_MODEL), F32), jax.ShapeDtypeStruct((SEQ, D_MODEL), MXU_DTYPE),
                   jax.ShapeDtypeStruct((1, D_MODEL), F32)],
        compiler_params=pltpu.CompilerParams(dimension_semantics=("arbitrary",)),
    )(x, g, dn, dres)


def final_norm_loss(h, g, target, name):
    def body(h_ref, g_ref, t_ref, dh_ref, dhm_ref, dg_ref, loss_ref):
        xv = h_ref[...]
        r = lax.rsqrt(jnp.mean(xv * xv, axis=-1, keepdims=True) + EPS)
        xh = xv * r
        gv = g_ref[...]
        e = xh * gv - t_ref[...]

        @pl.when(pl.program_id(0) == 0)
        def _():
            dg_ref[...] = jnp.zeros_like(dg_ref)
            loss_ref[...] = jnp.zeros_like(loss_ref)

        part = 0.5 * jnp.sum(jnp.sum(e * e, axis=-1, keepdims=True) * (1.0 / D_MODEL), axis=0, keepdims=True)
        loss_ref[...] += jnp.broadcast_to(part, loss_ref.shape)
        dout = e * (1.0 / D_MODEL)
        dg_ref[...] += jnp.sum(dout * xh, axis=0, keepdims=True)
        dxh = dout * gv
        dh = r * (dxh - xh * jnp.mean(dxh * xh, axis=-1, keepdims=True))
        dh_ref[...] = dh
        dhm_ref[...] = dh.astype(dhm_ref.dtype)

    row = pl.BlockSpec((ROW_TILE, D_MODEL), lambda i: (i, 0))
    vec = pl.BlockSpec((1, D_MODEL), lambda i: (0, 0))
    return pl.pallas_call(
        body, name=name, grid=(SEQ // ROW_TILE,), in_specs=[row, vec, row],
        out_specs=[row, row, vec, pl.BlockSpec((8, 128), lambda i: (0, 0))],
        out_shape=[jax.ShapeDtypeStruct((SEQ, D_MODEL), F32), jax.ShapeDtypeStruct((SEQ, D_MODEL), MXU_DTYPE),
                   jax.ShapeDtypeStruct((1, D_MODEL), F32), jax.ShapeDtypeStruct((8, 128), F32)],
        compiler_params=pltpu.CompilerParams(dimension_semantics=("arbitrary",)),
    )(h, g, target)


FFN_TILE = 256
FFN_TILES = D_FF // FFN_TILE


def gate_up_swiglu(n, w_gu, name):
    def body(n_ref, wa_ref, wb_ref, a_ref, b_ref, s_ref):
        nv = n_ref[...]
        a = _dot(nv, wa_ref[...])
        b = _dot(nv, wb_ref[...])
        a_ref[...] = a.astype(a_ref.dtype)
        b_ref[...] = b.astype(b_ref.dtype)
        s_ref[...] = (a * _sigmoid(a) * b).astype(s_ref.dtype)

    tile = pl.BlockSpec((SEQ, FFN_TILE), lambda j: (0, j))
    act = jax.ShapeDtypeStruct((SEQ, D_FF), ACT_DTYPE)
    return pl.pallas_call(
        body, name=name, grid=(FFN_TILES,),
        in_specs=[pl.BlockSpec((SEQ, D_MODEL), lambda j: (0, 0)), pl.BlockSpec((D_MODEL, FFN_TILE), lambda j: (0, j)),
                  pl.BlockSpec((D_MODEL, FFN_TILE), lambda j: (0, j + FFN_TILES))],
        out_specs=[tile, tile, tile], out_shape=[act, act, jax.ShapeDtypeStruct((SEQ, D_FF), MXU_DTYPE)],
        compiler_params=pltpu.CompilerParams(dimension_semantics=("parallel",), vmem_limit_bytes=MATMUL_VMEM_BYTES),
    )(n, w_gu, w_gu)


def d_gate_up(df, w_down, a, b, scale, behind, name):
    half = FFN_TILE // 2

    def body(df_ref, w_ref, a_ref, b_ref, *refs):
        o_ref, db_sc = refs[-2:]

        @pl.when(pl.program_id(1) == 0)
        def _():
            dfv = df_ref[...]
            ds = [_dot(dfv, w_ref[pl.ds(i * half, half), :], tb=True) * scale for i in range(2)]
            for i in range(2):
                cols = pl.ds(i * half, half)
                av = a_ref[:, cols].astype(F32)
                sg = _sigmoid(av)
                o_ref[:, cols] = (ds[i] * b_ref[:, cols].astype(F32) * (sg * (1.0 + av * (1.0 - sg)))).astype(o_ref.dtype)
                db_sc[:, cols] = (ds[i] * av * sg).astype(db_sc.dtype)

        @pl.when(pl.program_id(1) == 1)
        def _():
            o_ref[...] = db_sc[...]

    tile = pl.BlockSpec((SEQ, FFN_TILE), lambda j, t: (0, j))
    return pl.pallas_call(
        body, name=name, grid=(FFN_TILES, 2),
        in_specs=[pl.BlockSpec((SEQ, D_MODEL), lambda j, t: (0, 0)), pl.BlockSpec((FFN_TILE, D_MODEL), lambda j, t: (j, 0)),
                  tile, tile] + [pl.BlockSpec(memory_space=pl.ANY)] * len(behind),
        out_specs=pl.BlockSpec((SEQ, FFN_TILE), lambda j, t: (0, j + FFN_TILES * t)),
        out_shape=jax.ShapeDtypeStruct((SEQ, 2 * D_FF), MXU_DTYPE),
        scratch_shapes=[pltpu.VMEM((SEQ, FFN_TILE), MXU_DTYPE)],
        compiler_params=pltpu.CompilerParams(dimension_semantics=("arbitrary", "arbitrary"),
                                             vmem_limit_bytes=MATMUL_VMEM_BYTES),
    )(df, w_down, a, b, *behind)


GATE_HG_BLK = 6656 // 512
GATE_ATT_BLK = 7680 // 512


def merge_fwd(z, bh, ba, name):
    def body(gh_ref, ga_ref, bh_ref, ba_ref, o_ref):
        o_ref[...] = (_sigmoid(gh_ref[...]) * bh_ref[...] + _sigmoid(ga_ref[...]) * ba_ref[...]).astype(o_ref.dtype)

    blk = pl.BlockSpec((ROW_TILE, 512), lambda i, j: (i, j))
    return pl.pallas_call(
        body, name=name, grid=(SEQ // ROW_TILE, 2),
        in_specs=[pl.BlockSpec((ROW_TILE, 512), lambda i, j: (i, GATE_HG_BLK + j)),
                  pl.BlockSpec((ROW_TILE, 512), lambda i, j: (i, GATE_ATT_BLK + j)), blk, blk],
        out_specs=blk, out_shape=jax.ShapeDtypeStruct((SEQ, D_MODEL), MXU_DTYPE),
    )(z, z, bh, ba)


def merge_bwd(z, bh, ba, dm, name):
    def body(gh_ref, ga_ref, bh_ref, ba_ref, dm_ref, dbh_ref, dba_ref, dgh_ref, dga_ref):
        dmv = dm_ref[...]
        sh = _sigmoid(gh_ref[...])
        sa = _sigmoid(ga_ref[...])
        dbh_ref[...] = (dmv * sh).astype(dbh_ref.dtype)
        dba_ref[...] = (dmv * sa).astype(dba_ref.dtype)
        dgh_ref[...] = (dmv * bh_ref[...] * (sh * (1.0 - sh))).astype(dgh_ref.dtype)
        dga_ref[...] = (dmv * ba_ref[...] * (sa * (1.0 - sa))).astype(dga_ref.dtype)

    blk = pl.BlockSpec((ROW_TILE, 512), lambda i, j: (i, j))
    out = jax.ShapeDtypeStruct((SEQ, D_MODEL), MXU_DTYPE)
    return pl.pallas_call(
        body, name=name, grid=(SEQ // ROW_TILE, 2),
        in_specs=[pl.BlockSpec((ROW_TILE, 512), lambda i, j: (i, GATE_HG_BLK + j)),
                  pl.BlockSpec((ROW_TILE, 512), lambda i, j: (i, GATE_ATT_BLK + j)), blk, blk, blk],
        out_specs=[blk, blk, blk, blk], out_shape=[out, out, out, out],
    )(z, z, bh, ba, dm)


N_CHUNKS = SEQ // HG_CHUNK
HG_STEP_CHUNKS = 4


def _hgrn_gates(q, fp, lb):
    C = HG_CHUNK
    sg = _sigmoid(fp)
    f = lb + (1.0 - lb) * sg
    lf = jnp.log(f)
    row = lax.broadcasted_iota(jnp.int32, (C, C), 0)
    col = lax.broadcasted_iota(jnp.int32, (C, C), 1)
    causal = row >= col
    G = _dot_f32(causal.astype(F32), lf)
    eG = jnp.exp(G)
    enG = jnp.exp(-G)
    qg = q * eG
    kg = (1.0 - f) * enG
    A = jnp.where(causal, _hdot(qg, kg, tb=True), 0.0)
    egl = jnp.exp(jnp.sum(lf, axis=0, keepdims=True))
    return sg, f, causal, eG, enG, qg, kg, A, egl


def hgrn_fwd(z, lb, gain, name):
    C, K = HG_CHUNK, HG_DIM

    def body(q_ref, f_ref, v_ref, og_ref, p_ref, g_ref, y_ref, o_ref, st_ref, state):
        @pl.when(pl.program_id(0) == 0)
        def _():
            state[...] = jnp.zeros_like(state)

        for cc in range(HG_STEP_CHUNKS):
            rows = pl.ds(cc * C, C)
            for h in range(HG_HEADS):
                hd = pl.ds(h * K, K)
                v = v_ref[rows, hd]
                _, _, _, _, _, qg, kg, A, egl = _hgrn_gates(q_ref[rows, hd], f_ref[rows, hd], p_ref[:, hd])
                st = state[h]
                st_ref[h, cc] = st
                o = _hdot(A, v) + _hdot(qg, st, tb=True)
                state[h] = st * egl + _hdot(v, kg * egl, ta=True)
                o_ref[rows, hd] = o
                rs = lax.rsqrt(jnp.mean(o * o, axis=-1, keepdims=True) + EPS)
                og = og_ref[rows, hd]
                y_ref[rows, hd] = (((o * rs) * g_ref[:, hd]) * (og * _sigmoid(og))).astype(y_ref.dtype)

    R = HG_STEP_CHUNKS * C

    def zcol(section):
        return pl.BlockSpec((R, HG_WIDTH), lambda c: (c, section))

    vec = pl.BlockSpec((1, HG_WIDTH), lambda c: (0, 0))
    blk = pl.BlockSpec((R, HG_WIDTH), lambda c: (c, 0))
    return pl.pallas_call(
        body, name=name, grid=(N_CHUNKS // HG_STEP_CHUNKS,),
        in_specs=[zcol(0), zcol(1), zcol(2), zcol(3), vec, vec],
        out_specs=[blk, blk, pl.BlockSpec((HG_HEADS, HG_STEP_CHUNKS, K, K), lambda c: (0, c, 0, 0))],
        out_shape=[jax.ShapeDtypeStruct((SEQ, HG_WIDTH), MXU_DTYPE), jax.ShapeDtypeStruct((SEQ, HG_WIDTH), F32),
                   jax.ShapeDtypeStruct((HG_HEADS, N_CHUNKS, K, K), F32)],
        scratch_shapes=[pltpu.VMEM((HG_HEADS, K, K), F32)],
        compiler_params=pltpu.CompilerParams(dimension_semantics=("arbitrary",)),
    )(z, z, z, z, lb, gain)


def hgrn_bwd(z, lb, gain, o_raw, states, dy, name):
    C, K = HG_CHUNK, HG_DIM

    def body(q_ref, f_ref, v_ref, og_ref, p_ref, g_ref, o_ref, st_ref, dy_ref,
             dq_ref, dfp_ref, dv_ref, dog_ref, dlb_ref, dgain_ref, dstate):
        @pl.when(pl.program_id(0) == 0)
        def _():
            dstate[...] = jnp.zeros_like(dstate)
            dlb_ref[...] = jnp.zeros_like(dlb_ref)
            dgain_ref[...] = jnp.zeros_like(dgain_ref)

        last = lax.broadcasted_iota(jnp.int32, (C, K), 0) == C - 1
        row = lax.broadcasted_iota(jnp.int32, (C, C), 0)
        col = lax.broadcasted_iota(jnp.int32, (C, C), 1)
        anti_causal = (col >= row).astype(F32)
        for cc in reversed(range(HG_STEP_CHUNKS)):
            rows = pl.ds(cc * C, C)
            for h in range(HG_HEADS):
                hd = pl.ds(h * K, K)
                v = v_ref[rows, hd]
                lb = p_ref[:, hd]
                sg, f, causal, eG, enG, qg, kg, A, egl = _hgrn_gates(q_ref[rows, hd], f_ref[rows, hd], lb)
                kd = kg * egl
                st = st_ref[h, cc]
                dst = dstate[h]
                o = o_ref[rows, hd]
                og = og_ref[rows, hd]
                gain_v = g_ref[:, hd]
                dyv = dy_ref[rows, hd]
                rs = lax.rsqrt(jnp.mean(o * o, axis=-1, keepdims=True) + EPS)
                on = o * rs
                sgo = _sigmoid(og)
                silu = og * sgo
                dog_ref[rows, hd] = (dyv * (on * gain_v) * (sgo * (1.0 + og * (1.0 - sgo)))).astype(dog_ref.dtype)
                dgain_ref[:, hd] += jnp.sum(dyv * silu * on, axis=0, keepdims=True)
                don = dyv * gain_v * silu
                do = rs * (don - on * jnp.mean(don * on, axis=-1, keepdims=True))
                dA = jnp.where(causal, _hdot(do, v, tb=True), 0.0)
                dv_ref[rows, hd] = (_hdot(A, do, ta=True) + _hdot(kd, dst, tb=True)).astype(dv_ref.dtype)
                dqg = _hdot(dA, kg) + _hdot(do, st)
                dkg = _hdot(dA, qg, ta=True)
                dkd = _hdot(v, dst)
                dstate[h] = dst * egl + _hdot(do, qg, ta=True)
                dgl = jnp.sum(st * dst, axis=0, keepdims=True) * egl
                dq_ref[rows, hd] = (dqg * eG).astype(dq_ref.dtype)
                dk = dkg * enG + dkd * (enG * egl)
                dG = dqg * qg - dkg * kg - dkd * kd
                extra = jnp.sum(dkd * kd, axis=0, keepdims=True) + dgl
                dG = dG + jnp.where(last, extra, 0.0)
                dlf = _dot_f32(anti_causal, dG)
                df = dlf / f - dk
                dfp_ref[rows, hd] = (df * (1.0 - lb) * (sg * (1.0 - sg))).astype(dfp_ref.dtype)
                dlb_ref[:, hd] += jnp.sum(df * (1.0 - sg), axis=0, keepdims=True)

    R = HG_STEP_CHUNKS * C
    n_steps = N_CHUNKS // HG_STEP_CHUNKS

    def rc(c):
        return n_steps - 1 - c

    def zcol(section):
        return pl.BlockSpec((R, HG_WIDTH), lambda c: (rc(c), section))

    vec = pl.BlockSpec((1, HG_WIDTH), lambda c: (0, 0))
    blk = pl.BlockSpec((R, HG_WIDTH), lambda c: (rc(c), 0))
    out = jax.ShapeDtypeStruct((SEQ, HG_WIDTH), MXU_DTYPE)
    small = jax.ShapeDtypeStruct((1, HG_WIDTH), F32)
    return pl.pallas_call(
        body, name=name, grid=(n_steps,),
        in_specs=[zcol(0), zcol(1), zcol(2), zcol(3), vec, vec, blk,
                  pl.BlockSpec((HG_HEADS, HG_STEP_CHUNKS, K, K), lambda c: (0, rc(c), 0, 0)), blk],
        out_specs=[blk, blk, blk, blk, vec, vec],
        out_shape=[out, out, out, out, small, small],
        scratch_shapes=[pltpu.VMEM((HG_HEADS, K, K), F32)],
        compiler_params=pltpu.CompilerParams(dimension_semantics=("arbitrary",)),
    )(z, z, z, z, lb, gain, o_raw, states, dy)


N_GROUPS = len(ATT_GROUPS)
HEAD_PAIRS = ATT_WIDTH // 128
ATT_COL0 = 4 * HG_WIDTH
UNROLLED_UNITS = 4
ATT_SLAB_BLOCKS = 4


def _alibi_coef():
    n = N_GROUPS * ATT_HEADS
    slopes = np.exp2(-ALIBI_MAX * np.arange(1, n + 1, dtype=np.float32) / n).astype(np.float32)
    dil = np.repeat(np.array([d for _, d in ATT_GROUPS], np.float32), ATT_HEADS)
    return jnp.asarray(slopes * dil, F32)


def _for_each_unit(n, fn):
    if n <= UNROLLED_UNITS:
        for u in range(n):
            fn(u)
    else:
        def group(i, carry):
            for j in range(UNROLLED_UNITS):
                fn(i * UNROLLED_UNITS + j)
            return carry
        lax.fori_loop(0, n // UNROLLED_UNITS, group, 0)


def _att_specs(g):
    B = ATT_BLOCK
    d = ATT_GROUPS[g][1]
    blocks = ATT_SLAB_BLOCKS if d == 1 else 1
    R = B * d * blocks
    n_slabs = SEQ // R
    multi = SEQ // d > B
    col0 = (ATT_COL0 + g * 3 * ATT_WIDTH) // 128

    def cur(col):
        return pl.BlockSpec((R, 128), lambda hp, s: (s, col + hp))

    def prev(col):
        return pl.BlockSpec((R, 128), lambda hp, s: (jnp.maximum(s - 1, 0), col + hp))

    def nxt(col):
        return pl.BlockSpec((R, 128), lambda hp, s: (jnp.minimum(s + 1, n_slabs - 1), col + hp))

    def unit(u, s):
        if d > 1:
            rows = pl.ds(u, B, stride=d)
            return rows, False, rows, jnp.where(s == 0, B, 0), False, rows, jnp.where(s == n_slabs - 1, B, 0)
        rows = pl.ds(u * B, B)
        inner_prev, inner_next = u > 0, u < blocks - 1
        return (rows, inner_prev, pl.ds((u - 1) * B if inner_prev else (blocks - 1) * B, B),
                0 if inner_prev else jnp.where(s == 0, B, 0),
                inner_next, pl.ds((u + 1) * B if inner_next else 0, B),
                0 if inner_next else jnp.where(s == n_slabs - 1, B, 0))

    return d * blocks, R, n_slabs, multi, col0, cur, prev, nxt, unit


def _head_lanes(j):
    lane = lax.broadcasted_iota(jnp.int32, (ATT_BLOCK, 128), 1)
    return (lane >= 64 * j) & (lane < 64 * (j + 1))


def _lane_value(x, sel):
    return jnp.max(jnp.where(sel, x, -3e38), axis=-1, keepdims=True)


def _stack_heads(x, sel0):
    return jnp.concatenate([jnp.where(sel0, x, 0.0), jnp.where(sel0, 0.0, x)], axis=0)


def _stack_values(x, sel0, lanes):
    swapped = pltpu.roll(x, 64, 1)
    stacked = jnp.concatenate([jnp.where(sel0, x, swapped), jnp.where(sel0, swapped, x)], axis=0)
    return stacked if lanes == 128 else jnp.concatenate([stacked] * (lanes // 128), axis=1)


def _pair_coef(coef_ref, g, hp):
    row = lax.broadcasted_iota(jnp.int32, (2 * ATT_BLOCK, 1), 0)
    first = g * ATT_HEADS + hp * 2
    return jnp.where(row < ATT_BLOCK, coef_ref[first], coef_ref[first + 1])


def _band(with_prev, first_key):
    B = ATT_BLOCK
    keys = 2 * B if with_prev else B
    qi = jnp.bitwise_and(lax.broadcasted_iota(jnp.int32, (2 * B, keys), 0), B - 1)
    kj = lax.broadcasted_iota(jnp.int32, (2 * B, keys), 1)
    delta = qi + (B if with_prev else 0) - kj
    valid = (delta >= 0) & (delta <= B)
    if with_prev:
        valid = valid & (kj >= first_key)
    return valid, delta.astype(F32)


def _band_next(first_key):
    B = ATT_BLOCK
    qi = jnp.bitwise_and(lax.broadcasted_iota(jnp.int32, (2 * B, B), 0), B - 1)
    kj = lax.broadcasted_iota(jnp.int32, (2 * B, B), 1)
    delta = qi + B - kj
    return (delta <= B) & (kj >= first_key), delta.astype(F32)


def att_fwd(z, g, name):
    B = ATT_BLOCK
    n_units, R, n_slabs, has_prev, col0, cur, prev, _, unit = _att_specs(g)

    def body(coef_ref, *refs):
        if has_prev:
            q_ref, kc_ref, vc_ref, kp_ref, vp_ref, o_ref, l_ref = refs
        else:
            q_ref, kc_ref, vc_ref, o_ref, l_ref = refs
        hp, s = pl.program_id(0), pl.program_id(1)
        cf2 = _pair_coef(coef_ref, g, hp)
        sel0 = _head_lanes(0)

        def one(u):
            rows, inner_prev, prev_rows, first_key, _, _, _ = unit(u, s)
            valid, dist = _band(has_prev, first_key)
            q2 = _stack_heads(q_ref[rows, :], sel0)
            kk, vv = kc_ref[rows, :], vc_ref[rows, :]
            if has_prev:
                k_from, v_from = (kc_ref, vc_ref) if inner_prev else (kp_ref, vp_ref)
                kk = jnp.concatenate([k_from[prev_rows, :], kk], axis=0)
                vv = jnp.concatenate([v_from[prev_rows, :], vv], axis=0)
            sc = jnp.where(valid, _dot(q2, kk, tb=True) * 0.125 - cf2 * dist, NEG_INF)
            mx = jnp.max(sc, axis=-1, keepdims=True)
            e = jnp.exp(sc - mx)
            den = jnp.sum(e, axis=-1, keepdims=True)
            o2 = _dot(e * (1.0 / den), vv)
            lse2 = mx + jnp.log(den)
            o_ref[rows, :] = jnp.where(sel0, o2[:B], o2[B:])
            l_ref[rows, :] = jnp.where(sel0, lse2[:B], lse2[B:])

        _for_each_unit(n_units, one)

    in_specs = [pl.BlockSpec(memory_space=pltpu.SMEM), cur(col0), cur(col0 + 4), cur(col0 + 8)]
    args = [_alibi_coef(), z, z, z]
    if has_prev:
        in_specs += [prev(col0 + 4), prev(col0 + 8)]
        args += [z, z]
    out = jax.ShapeDtypeStruct((SEQ, ATT_WIDTH), F32)
    return pl.pallas_call(
        body, name=name, grid=(HEAD_PAIRS, n_slabs), in_specs=in_specs,
        out_specs=[cur(0), cur(0)], out_shape=[out, out],
        compiler_params=pltpu.CompilerParams(dimension_semantics=("parallel", "arbitrary")),
    )(*args)


def att_bwd(z, l, do, corr, g, name):
    B = ATT_BLOCK
    d = ATT_GROUPS[g][1]
    n_blocks = SEQ // (d * B)
    multi = n_blocks > 1
    col0 = (ATT_COL0 + g * 3 * ATT_WIDTH) // 128
    own = slice(B, 2 * B) if multi else slice(0, B)

    def body(coef_ref, q_ref, k_ref, v_ref, l_ref, do_ref, cr_ref, dq_ref, dk_ref, dv_ref, dq_sc, dk_sc, dv_sc):
        hp = pl.program_id(0)
        cf2 = _pair_coef(coef_ref, g, hp)
        sel0 = _head_lanes(0)

        def block_rows(b, r):
            return pl.ds(b * (B * d) + r, B, stride=d) if d > 1 else pl.ds(pl.multiple_of(b * B, B), B)

        def one(u):
            b, r = (u, 0) if d == 1 else (u // d, u % d)
            rows = block_rows(b, r)
            valid, dist = _band(multi, jnp.where(b == 0, B, 0))
            kk, vv = k_ref[rows, :], v_ref[rows, :]
            if multi:
                prev_rows = block_rows(jnp.maximum(b - 1, 0), r)
                kk = jnp.concatenate([k_ref[prev_rows, :], kk], axis=0)
                vv = jnp.concatenate([v_ref[prev_rows, :], vv], axis=0)
            q2, do2 = _stack_heads(q_ref[rows, :], sel0), _stack_heads(do_ref[rows, :], sel0)
            keys = kk.shape[0]
            lse2, cr2 = _stack_values(l_ref[rows, :], sel0, keys), _stack_values(cr_ref[rows, :], sel0, keys)
            p = jnp.exp(jnp.where(valid, _dot(q2, kk, tb=True) * 0.125 - cf2 * dist, NEG_INF) - lse2)
            ds = p * (_dot(do2, vv, tb=True) + cr2)
            dq2 = _dot(ds, kk)
            dkk = _dot(ds, q2, ta=True) * 0.125
            dvv = _dot(p, do2, ta=True)
            dq_sc[rows, :] = jnp.where(sel0, dq2[:B], dq2[B:]) * 0.125
            dk_sc[rows, :] = dkk[own]
            dv_sc[rows, :] = dvv[own]
            if multi:
                dk_sc[prev_rows, :] += dkk[:B]
                dv_sc[prev_rows, :] += dvv[:B]

        _for_each_unit(d * n_blocks, one)
        dq_ref[...] = dq_sc[...].astype(dq_ref.dtype)
        dk_ref[...] = dk_sc[...].astype(dk_ref.dtype)
        dv_ref[...] = dv_sc[...].astype(dv_ref.dtype)

    def col(c):
        return pl.BlockSpec((SEQ, 128), lambda hp: (0, c + hp))

    out = jax.ShapeDtypeStruct((SEQ, ATT_WIDTH), MXU_DTYPE)
    return pl.pallas_call(
        body, name=name, grid=(HEAD_PAIRS,),
        in_specs=[pl.BlockSpec(memory_space=pltpu.SMEM), col(col0), col(col0 + 4), col(col0 + 8), col(0), col(0), col(0)],
        out_specs=[col(0)] * 3, out_shape=[out] * 3,
        scratch_shapes=[pltpu.VMEM((SEQ, 128), F32)] * 3,
        compiler_params=pltpu.CompilerParams(dimension_semantics=("parallel",), vmem_limit_bytes=MATMUL_VMEM_BYTES),
    )(_alibi_coef(), z, z, z, l, do, corr)


def _head_sum(x):
    i = lax.broadcasted_iota(jnp.int32, (128, 128), 0) // 64
    j = lax.broadcasted_iota(jnp.int32, (128, 128), 1) // 64
    return _dot_f32(x, (i == j).astype(F32), ones_on_right=True)


def _group_weights(l0, l1, l2):
    mx = jnp.maximum(jnp.maximum(l0, l1), l2)
    e0, e1, e2 = jnp.exp(l0 - mx), jnp.exp(l1 - mx), jnp.exp(l2 - mx)
    inv = 1.0 / (e0 + e1 + e2)
    return e0 * inv, e1 * inv, e2 * inv


def att_combine_fwd(o, l, name):
    def body(o0, o1, o2, l0, l1, l2, y_ref):
        w0, w1, w2 = _group_weights(l0[...], l1[...], l2[...])
        y_ref[...] = (o0[...] * w0 + o1[...] * w1 + o2[...] * w2).astype(y_ref.dtype)

    blk = pl.BlockSpec((ROW_TILE, ATT_WIDTH), lambda i: (i, 0))
    return pl.pallas_call(
        body, name=name, grid=(SEQ // ROW_TILE,), in_specs=[blk] * 6, out_specs=blk,
        out_shape=jax.ShapeDtypeStruct((SEQ, ATT_WIDTH), MXU_DTYPE),
    )(*o, *l)


def att_combine_bwd(o, l, dy, name):
    def body(o0, o1, o2, l0, l1, l2, dy_ref, do0, do1, do2, cr0, cr1, cr2):
        w = _group_weights(l0[...], l1[...], l2[...])
        dyv = dy_ref[...]
        tot = _head_sum(dyv * (w[0] * o0[...] + w[1] * o1[...] + w[2] * o2[...]))
        for g, (do_ref, cr_ref) in enumerate(((do0, cr0), (do1, cr1), (do2, cr2))):
            do_ref[...] = dyv * w[g]
            cr_ref[...] = -w[g] * tot

    blk = pl.BlockSpec((ROW_TILE, 128), lambda i, j: (i, j))
    out = jax.ShapeDtypeStruct((SEQ, ATT_WIDTH), F32)
    res = pl.pallas_call(
        body, name=name, grid=(SEQ // ROW_TILE, HEAD_PAIRS), in_specs=[blk] * 7, out_specs=[blk] * 6, out_shape=[out] * 6,
    )(*o, *l, dy)
    return res[:N_GROUPS], res[N_GROUPS:]


SUM_ROW_TILES = (1024, 512, 256, 128, 64, 32, 16)
SUM_TILE_BYTES = 24 * 1024 * 1024
SUM_PARAMS = pltpu.CompilerParams(vmem_limit_bytes=MATMUL_VMEM_BYTES)


def _row_tile(rows, cols, operands):
    fit = [t for t in SUM_ROW_TILES if rows % t == 0]
    return next((t for t in fit if 2 * 4 * operands * t * cols <= SUM_TILE_BYTES), fit[-1])


def _shard_shape(rows, cols, axis):
    return (rows // N_CHIPS, cols) if axis == 0 else (rows, cols // N_CHIPS)


def _half_shape(rows, cols, axis):
    return (rows, cols // 2) if axis == 0 else (rows // 2, cols)


def _piece_shape(rows, cols, axis):
    return (rows // N_CHIPS, cols // 2) if axis == 0 else (rows // 2, cols // N_CHIPS)


def place_own_block(shard, chip, rows, cols, axis, name):
    sr, sc = _shard_shape(rows, cols, axis)
    tr = _row_tile(sr, sc, 2)

    def body(chip_ref, s_ref, o_ref):
        o_ref[...] = s_ref[...].astype(o_ref.dtype)

    if axis == 0:
        out_map = lambda i, chip_ref: (chip_ref[0] * (sr // tr) + i, 0)
    else:
        out_map = lambda i, chip_ref: (i, chip_ref[0])
    return pl.pallas_call(
        body, name=name, out_shape=jax.ShapeDtypeStruct((rows, cols), WEIGHT_COMM_DTYPE), compiler_params=SUM_PARAMS,
        grid_spec=pltpu.PrefetchScalarGridSpec(
            num_scalar_prefetch=1, grid=(sr // tr,), in_specs=[pl.BlockSpec((tr, sc), lambda i, chip_ref: (i, 0))],
            out_specs=pl.BlockSpec((tr, sc), out_map)),
    )(chip, shard)


def add_halves(g, theirs, core, rows, cols, axis, name):
    hr, hc = _half_shape(rows, cols, axis)
    tr = _row_tile(hr, hc, 3)

    def body(core_ref, g_ref, t_ref, o_ref):
        o_ref[...] = (g_ref[...].astype(F32) + t_ref[...].astype(F32)).astype(o_ref.dtype)

    if axis == 0:
        g_map = lambda i, core_ref: (i, core_ref[0])
    else:
        g_map = lambda i, core_ref: (core_ref[0] * (hr // tr) + i, 0)
    blk = pl.BlockSpec((tr, hc), lambda i, core_ref: (i, 0))
    return pl.pallas_call(
        body, name=name, out_shape=jax.ShapeDtypeStruct((hr, hc), GRAD_COMM_DTYPE), compiler_params=SUM_PARAMS,
        grid_spec=pltpu.PrefetchScalarGridSpec(
            num_scalar_prefetch=1, grid=(hr // tr,), in_specs=[pl.BlockSpec((tr, hc), g_map), blk], out_specs=blk),
    )(core, g, theirs)


def add_pieces(half, got, chip, rows, cols, axis, name):
    hr, _ = _half_shape(rows, cols, axis)
    pr, pc = _piece_shape(rows, cols, axis)
    tr = _row_tile(pr, pc, 5)

    def body(chip_ref, h_ref, got_ref, o_ref):
        o_ref[...] = (h_ref[...].astype(F32) + got_ref[0].astype(F32) + got_ref[1].astype(F32) + got_ref[2].astype(F32))

    if axis == 0:
        h_map = lambda i, chip_ref: (chip_ref[0] * (pr // tr) + i, 0)
    else:
        h_map = lambda i, chip_ref: (i, chip_ref[0])
    return pl.pallas_call(
        body, name=name, out_shape=jax.ShapeDtypeStruct((pr, pc), F32), compiler_params=SUM_PARAMS,
        grid_spec=pltpu.PrefetchScalarGridSpec(
            num_scalar_prefetch=1, grid=(pr // tr,),
            in_specs=[pl.BlockSpec((tr, pc), h_map), pl.BlockSpec((3, tr, pc), lambda i, chip_ref: (0, i, 0))],
            out_specs=pl.BlockSpec((tr, pc), lambda i, chip_ref: (i, 0))),
    )(chip, half, got)


def _adamw_math(w, g, m, v):
    nm = ADAM_B1 * m + (1.0 - ADAM_B1) * g
    nv = ADAM_B2 * v + (1.0 - ADAM_B2) * (g * g)
    m_hat = nm / (1.0 - ADAM_B1 ** ADAM_STEP)
    v_hat = nv / (1.0 - ADAM_B2 ** ADAM_STEP)
    return -ADAM_LR * (m_hat / (jnp.sqrt(v_hat) + ADAM_EPS) + ADAM_WD * w), nm, nv


def adamw_halves(w, mine, theirs, m, v, core, rows, cols, axis, name):
    sr, sc = _shard_shape(rows, cols, axis)
    pr, pc = _piece_shape(rows, cols, axis)
    tr = _row_tile(pr, pc, 9)
    nt = pr // tr

    def body(core_ref, w_ref, a_ref, b_ref, m_ref, v_ref, g_ref, d_ref, nm_ref, nv_ref):
        g = jnp.where(pl.program_id(0) == core_ref[0], a_ref[...], b_ref[...])
        g_ref[...] = g
        d_ref[...], nm_ref[...], nv_ref[...] = _adamw_math(w_ref[...], g, m_ref[...], v_ref[...])

    if axis == 0:
        full = pl.BlockSpec((tr, pc), lambda h, i, core_ref: (i, h))
    else:
        full = pl.BlockSpec((tr, pc), lambda h, i, core_ref: (h * nt + i, 0))
    part = pl.BlockSpec((tr, pc), lambda h, i, core_ref: (i, 0))
    out = jax.ShapeDtypeStruct((sr, sc), F32)
    return pl.pallas_call(
        body, name=name, out_shape=[out, out, out, out], compiler_params=SUM_PARAMS,
        grid_spec=pltpu.PrefetchScalarGridSpec(
            num_scalar_prefetch=1, grid=(2, nt), in_specs=[full, part, part, full, full], out_specs=[full] * 4),
    )(core, w, mine, theirs, m, v)


BIG = (
    ("ffn1_w_gate_up", D_MODEL, 2 * D_FF, 1),
    ("ffn1_w_down", D_FF, D_MODEL, 0),
    ("w_in", D_MODEL, IN_COLS, 1),
    ("w_branch_hg", HG_WIDTH, D_MODEL, 1),
    ("w_branch_att", ATT_WIDTH, D_MODEL, 1),
    ("w_out", D_MODEL, D_MODEL, 0),
    ("ffn2_w_gate_up", D_MODEL, 2 * D_FF, 1),
    ("ffn2_w_down", D_FF, D_MODEL, 0),
)
N_BIG = len(BIG)
ANY = pl.BlockSpec(memory_space=pl.ANY)


def _place():
    return lax.axis_index("x"), lax.axis_index("y"), lax.axis_index("c")


def _other_chips(x, y):
    return ((1 - x, y), (x, 1 - y), (1 - x, 1 - y))


MAX_COPY_CHUNKS = 16
CHUNK_ROW_ALIGN = 16


def _row_chunks(view):
    rows = view.shape[0]
    n = next(n for n in range(MAX_COPY_CHUNKS, 0, -1) if rows % (CHUNK_ROW_ALIGN * n) == 0 or n == 1)
    step = rows // n
    return [pl.ds(i * step, step) for i in range(n)]


def _remote(src, dst, send_sem, recv_sem, device):
    return pltpu.make_async_remote_copy(src_ref=src, dst_ref=dst, send_sem=send_sem, recv_sem=recv_sem,
                                        device_id=device, device_id_type=MESH)


def _start_remote(src, dst, send_sem, recv_sem, device):
    for rows in _row_chunks(src):
        _remote(src.at[rows, :], dst.at[rows, :], send_sem, recv_sem, device).start()
    return _remote(src, dst, send_sem, recv_sem, device)


HBM = pl.BlockSpec(memory_space=pltpu.HBM)
SEM = pl.BlockSpec(memory_space=pltpu.SEMAPHORE)
SPLIT_COPY_EFFECT = pltpu.SideEffectType.DATAFLOW_SIDE_EFFECTING
GROUPS = {"ffn1": (0, 1), "mix": (2, 3, 4, 5), "ffn2": (6, 7)}


def _in_hbm(a):
    return pltpu.with_memory_space_constraint(a, pltpu.HBM)


class _SemList:
    def __init__(self, refs):
        self.refs = refs
        self.at = self

    def __getitem__(self, index):
        w, k = index
        return self.refs[3 * w + k]


def _gather_piece(ref, rows, cols, axis, chip, c):
    sr, sc = _shard_shape(rows, cols, axis)
    j = 2 * chip[0] + chip[1]
    if axis == 0:
        return ref.at[pl.ds(j * sr + c * (sr // 2), sr // 2), :]
    return ref.at[pl.ds(c * (sr // 2), sr // 2), pl.ds(pl.multiple_of(j * sc, 128), sc)]


def _start_gather_sends(bufs, ws, send_sems, recv_sems):
    x, y, c = _place()
    for w, (_, r, cc, ax) in enumerate(ws):
        mine = _gather_piece(bufs[w], r, cc, ax, (x, y), c)
        for k, chip in enumerate(_other_chips(x, y)):
            _start_remote(mine, mine, send_sems.at[w, k], recv_sems.at[w, k], (*chip, c))


def _wait_gather_sends(bufs, ws, send_sems, recv_sems):
    x, y, c = _place()
    for w, (_, r, cc, ax) in enumerate(ws):
        for k, chip in enumerate(_other_chips(x, y)):
            got = _gather_piece(bufs[w], r, cc, ax, chip, c)
            _remote(got, got, send_sems.at[w, k], recv_sems.at[w, k], (x, y, c)).wait_recv()
    for w, (_, r, cc, ax) in enumerate(ws):
        mine = _gather_piece(bufs[w], r, cc, ax, (x, y), c)
        for k in range(3):
            _remote(mine, mine, send_sems.at[w, k], recv_sems.at[w, k], (x, y, c)).wait_send()


def _forward_halves(bufs, ws, send_sems, recv_sems):
    x, y, c = _place()
    passed = []
    for w, (_, r, cc, ax) in enumerate(ws):
        for k, chip in enumerate(_other_chips(x, y)):
            got = _gather_piece(bufs[w], r, cc, ax, chip, c)
            passed.append(_start_remote(got, got, send_sems.at[w, k], recv_sems.at[w, k], (x, y, 1 - c)))
    for w, (_, r, cc, ax) in enumerate(ws):
        for k, chip in enumerate(_other_chips(x, y)):
            got = _gather_piece(bufs[w], r, cc, ax, chip, 1 - c)
            _remote(got, got, send_sems.at[w, k], recv_sems.at[w, k], (x, y, c)).wait_recv()
    for cp in passed:
        cp.wait_send()


def gather_start(placed, after, group):
    ws = [BIG[i] for i in GROUPS[group]]
    n = len(ws)

    def body(*refs):
        bufs = refs[:n]
        send_sems, recv_sems = _SemList(refs[n + 1:4 * n + 1]), _SemList(refs[4 * n + 1:7 * n + 1])
        token = refs[-1]
        _start_gather_sends(bufs, ws, send_sems, recv_sems)
        token[...] = jnp.zeros_like(token)

    out = pl.pallas_call(
        body, name=f"gather_start_{group}", in_specs=[HBM] * n + [ANY],
        out_specs=[SEM] * (6 * n) + [HBM] * n + [pl.BlockSpec(memory_space=pltpu.VMEM)],
        out_shape=[pltpu.SemaphoreType.DMA(())] * (6 * n)
        + [pltpu.HBM((r, cc), WEIGHT_COMM_DTYPE) for _, r, cc, _ in ws] + [jax.ShapeDtypeStruct((8, 128), F32)],
        input_output_aliases={w: 6 * n + w for w in range(n)},
        compiler_params=pltpu.CompilerParams(has_side_effects=SPLIT_COPY_EFFECT),
    )(*[_in_hbm(p) for p in placed], after)
    return out[:3 * n], out[3 * n:6 * n], out[6 * n:7 * n], out[-1]


def gather_wait(bufs, send_sems, recv_sems, after, group):
    ws = [BIG[i] for i in GROUPS[group]]
    n = len(ws)

    def body(*refs):
        _wait_gather_sends(refs[:n], ws, _SemList(refs[n:n + 3 * n]), _SemList(refs[n + 3 * n:n + 6 * n]))

    return pl.pallas_call(
        body, name=f"gather_wait_{group}", in_specs=[HBM] * n + [SEM] * (6 * n) + [ANY] * len(after), out_specs=[HBM] * n,
        out_shape=[pltpu.HBM((r, cc), WEIGHT_COMM_DTYPE) for _, r, cc, _ in ws],
        input_output_aliases={w: w for w in range(n)},
        compiler_params=pltpu.CompilerParams(has_side_effects=SPLIT_COPY_EFFECT),
    )(*bufs, *send_sems, *recv_sems, *after)


def gather_forward(bufs, group):
    ws = [BIG[i] for i in GROUPS[group]]
    n = len(ws)

    def body(*refs):
        _forward_halves(refs[n:2 * n], ws, refs[2 * n], refs[2 * n + 1])

    return pl.pallas_call(
        body, name=f"gather_forward_{group}", in_specs=[ANY] * n, out_specs=[ANY] * n,
        out_shape=[jax.ShapeDtypeStruct((r, cc), WEIGHT_COMM_DTYPE) for _, r, cc, _ in ws],
        input_output_aliases={w: w for w in range(n)},
        scratch_shapes=[pltpu.SemaphoreType.DMA((n, 3))] * 2,
    )(*bufs)


def _half(ref, rows, cols, axis, c):
    if axis == 0:
        return ref.at[:, pl.ds(pl.multiple_of(c * (cols // 2), 128), cols // 2)]
    return ref.at[pl.ds(c * (rows // 2), rows // 2), :]


def _piece_of_half(ref, rows, cols, axis, chip):
    j = 2 * chip[0] + chip[1]
    pr, pc = _piece_shape(rows, cols, axis)
    if axis == 0:
        return ref.at[pl.ds(j * pr, pr), :]
    return ref.at[:, pl.ds(pl.multiple_of(j * pc, 128), pc)]


def sibling_exchange_start(srcs, view, landing_shapes, dtype, name):
    n = len(srcs)

    def body(*refs):
        ins, land, sems = refs[:n], refs[n:2 * n], refs[2 * n:4 * n]
        x, y, c = _place()
        for w in range(n):
            _start_remote(view(ins[w], w, c), land[w], sems[w], sems[n + w], (x, y, 1 - c))
        refs[-1][...] = jnp.zeros_like(refs[-1])

    landing = [lax.empty(shape, dtype) for shape in landing_shapes]
    out = pl.pallas_call(
        body, name=name, in_specs=[HBM] * (2 * n),
        out_specs=[SEM] * (2 * n) + [HBM] * (2 * n) + [pl.BlockSpec(memory_space=pltpu.VMEM)],
        out_shape=[pltpu.SemaphoreType.DMA(())] * (2 * n) + [pltpu.HBM(a.shape, a.dtype) for a in srcs]
        + [pltpu.HBM(shape, dtype) for shape in landing_shapes] + [jax.ShapeDtypeStruct((8, 128), F32)],
        input_output_aliases={i: 2 * n + i for i in range(2 * n)},
        compiler_params=pltpu.CompilerParams(has_side_effects=SPLIT_COPY_EFFECT),
    )(*[_in_hbm(a) for a in srcs], *[_in_hbm(b) for b in landing])
    return out[:n], out[n:2 * n], out[2 * n:3 * n], out[3 * n:4 * n], out[-1]


def sibling_exchange_wait(srcs, landing, send_sems, recv_sems, view, after, name):
    n = len(srcs)

    def body(*refs):
        ins, land, sems = refs[:n], refs[n:2 * n], refs[2 * n:4 * n]
        x, y, c = _place()
        for w in range(n):
            cp = _remote(view(ins[w], w, c), land[w], sems[w], sems[n + w], (x, y, c))
            cp.wait_send()
            cp.wait_recv()

    out = pl.pallas_call(
        body, name=name, in_specs=[HBM] * (2 * n) + [SEM] * (2 * n) + [ANY] * len(after), out_specs=[HBM] * (2 * n),
        out_shape=[pltpu.HBM(a.shape, a.dtype) for a in srcs] + [pltpu.HBM(b.shape, b.dtype) for b in landing],
        input_output_aliases={i: i for i in range(2 * n)},
        compiler_params=pltpu.CompilerParams(has_side_effects=SPLIT_COPY_EFFECT),
    )(*srcs, *landing, *send_sems, *recv_sems, *after)
    return out[:n], out[n:]


def _scatter_copies(halves, got, ws, send_sems, recv_sems, start):
    x, y, c = _place()
    copies = []
    for w, (_, r, cc, ax) in enumerate(ws):
        for k, chip in enumerate(_other_chips(x, y)):
            args = (_piece_of_half(halves[w], r, cc, ax, chip), got[w].at[k], send_sems.at[w, k], recv_sems.at[w, k], (*chip, c))
            copies.append(_start_remote(*args) if start else _remote(*args))
    return copies


def scatter_start(halves, group):
    ws = [BIG[i] for i in GROUPS[group]]
    n = len(ws)

    def body(*refs):
        sems = refs[2 * n:8 * n]
        _scatter_copies(refs[:n], refs[n:2 * n], ws, _SemList(sems[:3 * n]), _SemList(sems[3 * n:]), start=True)
        refs[-1][...] = jnp.zeros_like(refs[-1])

    landing = [lax.empty((3,) + _piece_shape(r, cc, ax), GRAD_COMM_DTYPE) for _, r, cc, ax in ws]
    out = pl.pallas_call(
        body, name=f"scatter_start_{group}", in_specs=[HBM] * (2 * n),
        out_specs=[SEM] * (6 * n) + [HBM] * (2 * n) + [pl.BlockSpec(memory_space=pltpu.VMEM)],
        out_shape=[pltpu.SemaphoreType.DMA(())] * (6 * n)
        + [pltpu.HBM(_half_shape(r, cc, ax), GRAD_COMM_DTYPE) for _, r, cc, ax in ws]
        + [pltpu.HBM((3,) + _piece_shape(r, cc, ax), GRAD_COMM_DTYPE) for _, r, cc, ax in ws]
        + [jax.ShapeDtypeStruct((8, 128), F32)],
        input_output_aliases={i: 6 * n + i for i in range(2 * n)},
        compiler_params=pltpu.CompilerParams(has_side_effects=SPLIT_COPY_EFFECT),
    )(*[_in_hbm(h) for h in halves], *[_in_hbm(b) for b in landing])
    return out[:3 * n], out[3 * n:6 * n], out[6 * n:7 * n], out[7 * n:8 * n], out[-1]


def scatter_wait(halves, got, send_sems, recv_sems, after, group):
    ws = [BIG[i] for i in GROUPS[group]]
    n = len(ws)

    def body(*refs):
        sems = refs[2 * n:8 * n]
        for cp in _scatter_copies(refs[:n], refs[n:2 * n], ws, _SemList(sems[:3 * n]), _SemList(sems[3 * n:]), start=False):
            cp.wait_send()
            cp.wait_recv()

    out = pl.pallas_call(
        body, name=f"scatter_wait_{group}", in_specs=[HBM] * (2 * n) + [SEM] * (6 * n) + [ANY] * len(after),
        out_specs=[HBM] * (2 * n),
        out_shape=[pltpu.HBM(_half_shape(r, cc, ax), GRAD_COMM_DTYPE) for _, r, cc, ax in ws]
        + [pltpu.HBM((3,) + _piece_shape(r, cc, ax), GRAD_COMM_DTYPE) for _, r, cc, ax in ws],
        input_output_aliases={i: i for i in range(2 * n)},
        compiler_params=pltpu.CompilerParams(has_side_effects=SPLIT_COPY_EFFECT),
    )(*halves, *got, *send_sems, *recv_sems, *after)
    return out[:n], out[n:]


N_DEV = 8
SMALL = ("ffn1_norm", "mix_norm", "hg_lower_bounds", "hg_out_norm", "ffn2_norm", "final_norm")
SMALL_STAGE_ROWS = 8


def small_step(loss, grads, w, m, v, behind):
    n = len(SMALL)
    shapes = [g.shape for g in grads]
    first_row = [sum(s[0] for s in shapes[:i]) for i in range(n + 1)]
    assert first_row[n] < SMALL_STAGE_ROWS
    loss_row = (pl.ds(first_row[n], 1), pl.ds(0, loss.shape[1]))

    def body(*refs):
        loss_ref, g_refs, w_refs, m_refs, v_refs = refs[0], refs[1:1 + n], refs[1 + n:1 + 2 * n], refs[1 + 2 * n:1 + 3 * n], refs[1 + 3 * n:1 + 4 * n]
        outs = refs[2 + 4 * n:3 + 8 * n]
        loss_out, dg_refs, d_refs, nm_refs, nv_refs = outs[0], outs[1:1 + n], outs[1 + n:1 + 2 * n], outs[1 + 2 * n:1 + 3 * n], outs[1 + 3 * n:]
        stage, gathered, send_sems, recv_sems = refs[3 + 8 * n:]
        x, y, c = _place()
        me = 4 * x + 2 * y + c

        def slot(i, shape):
            return pl.ds(first_row[i], shape[0]), pl.ds(0, shape[1])

        stage[...] = jnp.zeros_like(stage)
        for i, g_ref in enumerate(g_refs):
            stage[slot(i, shapes[i])] = g_ref[...]
        stage[loss_row] = loss_ref[pl.ds(0, 1), :]
        gathered[me] = stage[...]
        copies = []
        for k in range(1, N_DEV):
            peer = (x ^ (k >> 2), y ^ ((k >> 1) & 1), c ^ (k & 1))
            cp = pltpu.make_async_remote_copy(
                src_ref=stage, dst_ref=gathered.at[me], send_sem=send_sems.at[k - 1], recv_sem=recv_sems.at[k - 1],
                device_id=peer, device_id_type=MESH)
            cp.start()
            copies.append(cp)
        for cp in copies:
            cp.wait()
        acc = gathered[0]
        for k in range(1, N_DEV):
            acc = acc + gathered[k]
        stage[...] = acc
        loss_out[...] = jnp.broadcast_to(stage[loss_row], loss_out.shape)
        for i in range(n):
            g = stage[slot(i, shapes[i])]
            dg_refs[i][...] = g
            d_refs[i][...], nm_refs[i][...], nv_refs[i][...] = _adamw_math(w_refs[i][...], g, m_refs[i][...], v_refs[i][...])

    vm = pl.BlockSpec(memory_space=pltpu.VMEM)
    per_param = [jax.ShapeDtypeStruct(s, F32) for s in shapes]
    out = pl.pallas_call(
        body, name="small_step", in_specs=[vm] * (1 + 4 * n) + [ANY], out_specs=[vm] * (1 + 4 * n),
        out_shape=[jax.ShapeDtypeStruct(loss.shape, F32)] + per_param * 4,
        scratch_shapes=[pltpu.VMEM((SMALL_STAGE_ROWS, D_MODEL), F32),
                        pltpu.VMEM((N_DEV, SMALL_STAGE_ROWS, D_MODEL), F32),
                        pltpu.SemaphoreType.DMA((N_DEV - 1,)), pltpu.SemaphoreType.DMA((N_DEV - 1,))],
    )(loss, *grads, *w, *m, *v, behind)
    return out[0], out[1:1 + n], out[1 + n:1 + 2 * n], out[1 + 2 * n:1 + 3 * n], out[1 + 3 * n:]


def _swiglu_block_fwd(h, norm_g, w_gu, w_down, tag, behind=()):
    n = rmsnorm_fwd(h, norm_g, f"{tag}_norm", behind=behind)
    a, b, s = gate_up_swiglu(n, w_gu, f"{tag}_gate_up")
    h_out = matmul(s, w_down, res=h, scale=0.5, name=f"{tag}_down")
    return h_out, (n, a, b, s)


def _swiglu_block_bwd(h, norm_g, w_gu, w_down, saved, dh_out, df, tag, exchange, behind=()):
    n, a, b, s = saved
    d_down = matmul(s, df, ta=True, scale=0.5, out_dtype=GRAD_COMM_DTYPE, name=f"{tag}_d_w_down")
    dgu = d_gate_up(df, w_down, a, b, 0.5, behind, f"{tag}_d_gate_up")
    d_gu = matmul(n, dgu, ta=True, out_dtype=GRAD_COMM_DTYPE, name=f"{tag}_d_w_gate_up")
    tokens = exchange.gradients_ready(tag, {f"{tag}_w_gate_up": d_gu, f"{tag}_w_down": d_down})
    dn = matmul(dgu, w_gu, tb=True, behind=tokens, name=f"{tag}_d_n")
    dh, dh_m, dg = rmsnorm_bwd(h, norm_g, dn, dh_out, f"{tag}_norm_bwd")
    return dh, dh_m, dg


def local_step(x, target, small, exchange):
    big = {}
    token, big_ffn1 = exchange.weights("ffn1", x)
    big.update(big_ffn1)
    h1, saved1 = _swiglu_block_fwd(x, small["ffn1_norm"], big["ffn1_w_gate_up"], big["ffn1_w_down"], "ffn1", token)
    token, big_mix = exchange.weights("mix", h1)
    big.update(big_mix)
    u = rmsnorm_fwd(h1, small["mix_norm"], "mix_norm", behind=token)
    z = matmul(u, big["w_in"], name="w_in")
    p = small["hg_lower_bounds"]
    lb = 1.0 / (1.0 + jnp.exp(p[1:2] - p[0:1]))
    y_hg, o_raw, states = hgrn_fwd(z, lb, small["hg_out_norm"], "hgrn_fwd")
    o_att, l_att = zip(*[att_fwd(z, g, f"att_fwd_{g}") for g in range(N_GROUPS)])
    y_att = att_combine_fwd(o_att, l_att, "att_combine")
    bh = matmul(y_hg, big["w_branch_hg"], name="branch_hg")
    ba = matmul(y_att, big["w_branch_att"], name="branch_att")
    merged = merge_fwd(z, bh, ba, "merge")
    h2 = matmul(merged, big["w_out"], res=h1, name="w_out")
    token, big_ffn2 = exchange.weights("ffn2", h2)
    big.update(big_ffn2)
    h3, saved2 = _swiglu_block_fwd(h2, small["ffn2_norm"], big["ffn2_w_gate_up"], big["ffn2_w_down"], "ffn2", token)
    dh3, dh3_m, d_final, loss = final_norm_loss(h3, small["final_norm"], target, "final_norm_loss")

    gs, gb = {"final_norm": d_final}, {}
    dh2, dh2_m, gs["ffn2_norm"] = _swiglu_block_bwd(
        h2, small["ffn2_norm"], big["ffn2_w_gate_up"], big["ffn2_w_down"], saved2, dh3, dh3_m, "ffn2", exchange)
    token = exchange.backward_done("ffn2", dh2)
    gb["w_out"] = matmul(merged, dh2_m, ta=True, out_dtype=GRAD_COMM_DTYPE, name="d_w_out")
    dmerged = matmul(dh2_m, big["w_out"], tb=True, behind=token, name="d_merged")
    dbh, dba, dgh, dga = merge_bwd(z, bh, ba, dmerged, "merge_bwd")
    gb["w_branch_hg"] = matmul(y_hg, dbh, ta=True, out_dtype=GRAD_COMM_DTYPE, name="d_w_branch_hg")
    gb["w_branch_att"] = matmul(y_att, dba, ta=True, out_dtype=GRAD_COMM_DTYPE, name="d_w_branch_att")
    dy_hg = matmul(dbh, big["w_branch_hg"], tb=True, name="d_y_hg")
    dy_att = matmul(dba, big["w_branch_att"], tb=True, name="d_y_att")
    dq, dfp, di, dog, d_lb, gs["hg_out_norm"] = hgrn_bwd(z, lb, small["hg_out_norm"], o_raw, states, dy_hg, "hgrn_bwd")
    do_att, corr = att_combine_bwd(o_att, l_att, dy_att, "att_combine_bwd")
    d_att = [part for g in range(N_GROUPS) for part in att_bwd(z, l_att[g], do_att[g], corr[g], g, f"att_bwd_{g}")]
    dz = jnp.concatenate([dq, dfp, di, dog, *d_att, dgh, dga], axis=1)
    gb["w_in"] = matmul(u, dz, ta=True, out_dtype=GRAD_COMM_DTYPE, name="d_w_in")
    token = exchange.gradients_ready("mix", gb)
    du = matmul(dz, big["w_in"], tb=True, behind=token, name="d_u")
    dh1, dh1_m, gs["mix_norm"] = rmsnorm_bwd(h1, small["mix_norm"], du, dh2, "mix_norm_bwd")
    token = exchange.backward_done("mix", dh1)
    dp0 = d_lb * lb * (1.0 - lb)
    gs["hg_lower_bounds"] = jnp.concatenate([dp0, -dp0], axis=0)
    dx, _, gs["ffn1_norm"] = _swiglu_block_bwd(
        x, small["ffn1_norm"], big["ffn1_w_gate_up"], big["ffn1_w_down"], saved1, dh1, dh1_m, "ffn1", exchange, token)
    exchange.backward_done("ffn1", dx)
    return loss, dx, gs


WEIGHTS = ("ffn1_norm", "ffn1_w_gate_up", "ffn1_w_down", "mix_norm", "w_in", "hg_lower_bounds", "hg_out_norm",
           "w_branch_hg", "w_branch_att", "w_out", "ffn2_norm", "ffn2_w_gate_up", "ffn2_w_down", "final_norm")


class WeightExchange:
    ORDER = ("ffn1", "mix", "ffn2")

    def __init__(self, shards, core, chip):
        self.core, self.chip = core, chip
        self.halving = None
        self.scattering = None
        self.reducing = {}
        first = self.ORDER[0]
        self.placed = {BIG[i][0]: place_own_block(shards[BIG[i][0]], chip, *BIG[i][1:], f"place_{BIG[i][0]}")
                       for i in GROUPS[first]}
        self._start_gather(first, self.placed[self._names(first)[0]])
        chip_behind = chip + self.token[0, :1].astype(jnp.int32)
        for group in self.ORDER[1:]:
            for i in GROUPS[group]:
                n, r, cc, ax = BIG[i]
                self.placed[n] = place_own_block(shards[n], chip_behind, r, cc, ax, f"place_{n}")
        self.placed_behind = [self.placed[n] for group in self.ORDER[1:] for n in self._names(group)]

    def _names(self, group):
        return [BIG[i][0] for i in GROUPS[group]]

    def _start_gather(self, group, after):
        send_sems, recv_sems, bufs, self.token = gather_start([self.placed[n] for n in self._names(group)], after, group)
        self.gathering = (group, send_sems, recv_sems, bufs)

    def weights(self, group, h):
        pending, send_sems, recv_sems, bufs = self.gathering
        assert pending == group
        after = self.placed_behind if group == self.ORDER[0] else [h]
        whole = gather_forward(gather_wait(bufs, send_sems, recv_sems, after, group), group)
        later = self.ORDER.index(group) + 1
        behind = []
        if later < len(self.ORDER):
            self._start_gather(self.ORDER[later], whole[0])
            behind = [self.token]
        return behind, dict(zip(self._names(group), whole))

    @staticmethod
    def _half_to_sibling(ws):
        return lambda ref, w, c: _half(ref, *ws[w][1:], 1 - c)

    def gradients_ready(self, group, grads):
        ws = [BIG[i] for i in GROUPS[group]]
        send_sems, recv_sems, own, theirs, token = sibling_exchange_start(
            [grads[n] for n, *_ in ws], self._half_to_sibling(ws), [_half_shape(r, cc, ax) for _, r, cc, ax in ws],
            GRAD_COMM_DTYPE, f"halves_start_{group}")
        self.halving = (group, send_sems, recv_sems, own, theirs)
        return [token]

    def backward_done(self, group, dh):
        behind = [self._finish_scatter([dh])] if self.scattering is not None else []
        pending, send_sems, recv_sems, own, theirs = self.halving
        assert pending == group
        ws = [BIG[i] for i in GROUPS[group]]
        own, theirs = sibling_exchange_wait(own, theirs, send_sems, recv_sems, self._half_to_sibling(ws), [dh],
                                            f"halves_wait_{group}")
        halves = [add_halves(g, t, self.core, r, cc, ax, f"add_halves_{n}") for (n, r, cc, ax), g, t in zip(ws, own, theirs)]
        send_sems, recv_sems, halves, got, self.token = scatter_start(halves, group)
        self.scattering = (group, send_sems, recv_sems, halves, got)
        return behind + [self.token]

    def _finish_scatter(self, after):
        group, send_sems, recv_sems, halves, got = self.scattering
        halves, got = scatter_wait(halves, got, send_sems, recv_sems, after, group)
        ws = [BIG[i] for i in GROUPS[group]]
        mine = [add_pieces(h, g, self.chip, r, cc, ax, f"add_pieces_{n}") for (n, r, cc, ax), h, g in zip(ws, halves, got)]
        send_sems, recv_sems, mine, theirs, token = sibling_exchange_start(
            mine, lambda ref, w, c: ref, [_piece_shape(r, cc, ax) for _, r, cc, ax in ws], F32, f"reduced_start_{group}")
        self.reducing[group] = (send_sems, recv_sems, mine, theirs)
        self.scattering = None
        return token

    def finish(self, after):
        return self._finish_scatter(after)

    def reduced_halves(self, group, after):
        send_sems, recv_sems, mine, theirs = self.reducing.pop(group)
        mine, theirs = sibling_exchange_wait(mine, theirs, send_sems, recv_sems, lambda ref, w, c: ref, after,
                                             f"reduced_wait_{group}")
        return {BIG[i][0]: (a, b) for i, a, b in zip(GROUPS[group], mine, theirs)}


def kernel(x, ffn1_norm, ffn1_w_gate_up, ffn1_w_down, mix_norm, w_in, hg_lower_bounds, hg_out_norm, w_branch_hg, w_branch_att, w_out, ffn2_norm, ffn2_w_gate_up, ffn2_w_down, final_norm, loss_target, m_ffn1_norm, m_ffn1_w_gate_up, m_ffn1_w_down, m_mix_norm, m_w_in, m_hg_lower_bounds, m_hg_out_norm, m_w_branch_hg, m_w_branch_att, m_w_out, m_ffn2_norm, m_ffn2_w_gate_up, m_ffn2_w_down, m_final_norm, v_ffn1_norm, v_ffn1_w_gate_up, v_ffn1_w_down, v_mix_norm, v_w_in, v_hg_lower_bounds, v_hg_out_norm, v_w_branch_hg, v_w_branch_att, v_w_out, v_ffn2_norm, v_ffn2_w_gate_up, v_ffn2_w_down, v_final_norm):
    w = dict(ffn1_norm=ffn1_norm, ffn1_w_gate_up=ffn1_w_gate_up, ffn1_w_down=ffn1_w_down, mix_norm=mix_norm, w_in=w_in,
             hg_lower_bounds=hg_lower_bounds, hg_out_norm=hg_out_norm, w_branch_hg=w_branch_hg, w_branch_att=w_branch_att,
             w_out=w_out, ffn2_norm=ffn2_norm, ffn2_w_gate_up=ffn2_w_gate_up, ffn2_w_down=ffn2_w_down, final_norm=final_norm)
    m = dict(ffn1_norm=m_ffn1_norm, ffn1_w_gate_up=m_ffn1_w_gate_up, ffn1_w_down=m_ffn1_w_down, mix_norm=m_mix_norm,
             w_in=m_w_in, hg_lower_bounds=m_hg_lower_bounds, hg_out_norm=m_hg_out_norm, w_branch_hg=m_w_branch_hg,
             w_branch_att=m_w_branch_att, w_out=m_w_out, ffn2_norm=m_ffn2_norm, ffn2_w_gate_up=m_ffn2_w_gate_up,
             ffn2_w_down=m_ffn2_w_down, final_norm=m_final_norm)
    v = dict(ffn1_norm=v_ffn1_norm, ffn1_w_gate_up=v_ffn1_w_gate_up, ffn1_w_down=v_ffn1_w_down, mix_norm=v_mix_norm,
             w_in=v_w_in, hg_lower_bounds=v_hg_lower_bounds, hg_out_norm=v_hg_out_norm, w_branch_hg=v_w_branch_hg,
             w_branch_att=v_w_branch_att, w_out=v_w_out, ffn2_norm=v_ffn2_norm, ffn2_w_gate_up=v_ffn2_w_gate_up,
             ffn2_w_down=v_ffn2_w_down, final_norm=v_final_norm)

    core = lax.axis_index("c").astype(jnp.int32).reshape(1)
    chip = (2 * lax.axis_index("x") + lax.axis_index("y")).astype(jnp.int32).reshape(1)
    exchange = WeightExchange({n: w[n][0] for n, *_ in BIG}, core, chip)
    small = {n: w[n] for n in SMALL}
    small["final_norm"] = final_norm.reshape(1, D_MODEL)

    loss, dx, gs = local_step(x[0], loss_target[0], small, exchange)

    grads, delta, new_m, new_v = {}, {}, {}, {}

    def update(group, core, after):
        reduced = exchange.reduced_halves(group, after)
        for i in GROUPS[group]:
            n, r, cc, ax = BIG[i]
            a, b = reduced[n]
            g, d, nm, nv = adamw_halves(w[n][0], a, b, m[n][0], v[n][0], core, r, cc, ax, f"adamw_{n}")
            grads[n], delta[n], new_m[n], new_v[n] = g[None], d[None], nm[None], nv[None]

    core_behind = core + exchange.token[0, :1].astype(jnp.int32)
    update("ffn2", core_behind, [exchange.token])
    update("mix", core_behind, [delta["ffn2_w_down"]])
    token = exchange.finish(after=[delta[BIG[i][0]] for group in ("ffn2", "mix") for i in GROUPS[group]])
    two_d = lambda a: a.reshape(1, D_MODEL) if a.ndim == 1 else a
    loss_sum, *small_out = small_step(loss, [gs[n] for n in SMALL], *[[two_d(p[n]) for n in SMALL] for p in (w, m, v)],
                                      behind=token)
    for result, parts in zip((grads, delta, new_m, new_v), small_out):
        result.update({n: a.reshape(w[n].shape) for n, a in zip(SMALL, parts)})
    update("ffn1", core, [loss_sum])

    return (loss_sum[0, 0], dx[None], *[grads[n] for n in WEIGHTS], *[delta[n] for n in WEIGHTS],
            *[new_m[n] for n in WEIGHTS], *[new_v[n] for n in WEIGHTS])
```

```python
import numpy as np
import jax
import jax.numpy as jnp
from jax import lax
from jax.experimental import pallas as pl
from jax.experimental.pallas import tpu as pltpu

SEQ = 2048
D_MODEL = 1024
D_FF = 2816
HG_HEADS = 4
HG_DIM = 128
HG_WIDTH = 512
HG_CHUNK = 64
ATT_GROUPS = ((128, 1), (512, 4), (2048, 16))
ATT_HEADS = 8
ATT_WIDTH = 512
ATT_BLOCK = 128
ALIBI_MAX = 8.0
IN_COLS = 8704
EPS = 1e-6
NEG_INF = -1e30
ADAM_LR = 0.001
ADAM_B1 = 0.9
ADAM_B2 = 0.999
ADAM_EPS = 1e-08
ADAM_WD = 0.01
ADAM_STEP = 10

N_CHIPS = 4
MXU_DTYPE = jnp.bfloat16
WEIGHT_COMM_DTYPE = jnp.bfloat16
GRAD_COMM_DTYPE = jnp.bfloat16
ACT_DTYPE = jnp.bfloat16
MESH = pl.DeviceIdType.MESH
F32 = jnp.float32


def _sigmoid(x):
    return 1.0 / (1.0 + jnp.exp(-x))


def _dot(a, b, ta=False, tb=False):
    dn = (((0 if ta else 1,), (1 if tb else 0,)), ((), ()))
    return lax.dot_general(a.astype(MXU_DTYPE), b.astype(MXU_DTYPE), dn, preferred_element_type=F32)


def _dot_f32(a, b, ones_on_right=False):
    x = a if ones_on_right else b
    hi = x.astype(jnp.bfloat16)
    rest = x - hi.astype(F32)
    mid = rest.astype(jnp.bfloat16)
    lo = (rest - mid.astype(F32)).astype(jnp.bfloat16)
    if ones_on_right:
        dot = lambda q: jnp.dot(q, b.astype(jnp.bfloat16), preferred_element_type=F32)
    else:
        dot = lambda q: jnp.dot(a.astype(jnp.bfloat16), q, preferred_element_type=F32)
    return dot(hi) + (dot(mid) + dot(lo))


def _split_bf16(x):
    hi = x.astype(jnp.bfloat16)
    return hi, (x - hi.astype(F32)).astype(jnp.bfloat16)


def _hdot(a, b, ta=False, tb=False):
    dn =(((0 if ta else 1,), (1 if tb else 0,)), ((), ()))
    (a_hi, a_lo), (b_hi, b_lo) = _split_bf16(a), _split_bf16(b)
    dot = lambda p, q: lax.dot_general(p, q, dn, preferred_element_type=F32)
    return dot(a_hi, b_hi) + (dot(a_lo, b_hi) + dot(a_hi, b_lo))


MATMUL_VMEM_BYTES = 48 * 1024 * 1024
MATMUL_TILE_BYTES = 36 * 1024 * 1024
MXU_ALIGN = 128


def _divisors(n, most):
    return [t for t in range(min(n, most), 0, -MXU_ALIGN) if n % t == 0 and t % MXU_ALIGN == 0]


def _matmul_tiles(M, N, K, in_bytes, out_bytes, has_res):
    best = None
    for tk in _divisors(K, K):
        nk = K // tk
        for tm in _divisors(M, 2048):
            for tn in _divisors(N, 512):
                tiles = 2 * in_bytes * (tm * tk + tk * tn) + 2 * out_bytes * tm * tn
                tiles += 4 * tm * tn * ((nk > 1) + 2 * has_res)
                if tiles > MATMUL_TILE_BYTES:
                    continue
                traffic = in_bytes * (M * K * (1 if nk == 1 else N // tn) + K * N * (M // tm))
                key = (traffic, -tm * tn * tk)
                if best is None or key < best[0]:
                    best = (key, (tm, tn, tk))
    return best[1]


def matmul(a, b, *, ta=False, tb=False, out_dtype=F32, res=None, scale=1.0, behind=(), name):
    if ta:
        K, M = a.shape
    else:
        M, K = a.shape
    if tb:
        N, K2 = b.shape
    else:
        K2, N = b.shape
    assert K == K2 and a.dtype == b.dtype
    tm, tn, tk = _matmul_tiles(M, N, K, a.dtype.itemsize, jnp.dtype(out_dtype).itemsize, res is not None)
    nk = K // tk

    def finish(r, r_ref, o_ref):
        if scale != 1.0:
            r = r * scale
        if res is not None:
            r = r_ref[...] + r
        o_ref[...] = r.astype(out_dtype)

    def body(*refs):
        a_ref, b_ref = refs[:2]
        r_ref = refs[2] if res is not None else None
        o_ref = refs[2 + (res is not None) + len(behind)]
        if nk == 1:
            finish(_dot(a_ref[...], b_ref[...], ta, tb), r_ref, o_ref)
            return
        acc = refs[-1]
        k = pl.program_id(2)

        @pl.when(k == 0)
        def _():
            acc[...] = jnp.zeros_like(acc)

        acc[...] += _dot(a_ref[...], b_ref[...], ta, tb)

        @pl.when(k == nk - 1)
        def _():
            finish(acc[...], r_ref, o_ref)

    a_spec = pl.BlockSpec((tk, tm), lambda i, j, k: (k, i)) if ta else pl.BlockSpec((tm, tk), lambda i, j, k: (i, k))
    b_spec = pl.BlockSpec((tn, tk), lambda i, j, k: (j, k)) if tb else pl.BlockSpec((tk, tn), lambda i, j, k: (k, j))
    in_specs = [a_spec, b_spec]
    args = [a, b]
    if res is not None:
        in_specs.append(pl.BlockSpec((tm, tn), lambda i, j, k: (i, j)))
        args.append(res)
    for earlier in behind:
        in_specs.append(pl.BlockSpec(memory_space=pl.ANY))
        args.append(earlier)
    return pl.pallas_call(
        body, name=name, grid=(M // tm, N // tn, nk), in_specs=in_specs,
        out_specs=pl.BlockSpec((tm, tn), lambda i, j, k: (i, j)),
        out_shape=jax.ShapeDtypeStruct((M, N), out_dtype),
        scratch_shapes=[pltpu.VMEM((tm, tn), F32)] if nk > 1 else [],
        compiler_params=pltpu.CompilerParams(dimension_semantics=("parallel", "parallel", "arbitrary"),
                                             vmem_limit_bytes=MATMUL_VMEM_BYTES),
    )(*args)


ROW_TILE = 256


def rmsnorm_fwd(x, g, name, behind=()):
    def body(x_ref, g_ref, *refs):
        n_ref = refs[-1]
        xv = x_ref[...]
        r = lax.rsqrt(jnp.mean(xv * xv, axis=-1, keepdims=True) + EPS)
        n_ref[...] = ((xv * r) * g_ref[...]).astype(n_ref.dtype)

    order = list(behind)
    return pl.pallas_call(
        body, name=name, grid=(SEQ // ROW_TILE,),
        in_specs=[pl.BlockSpec((ROW_TILE, D_MODEL), lambda i: (i, 0)), pl.BlockSpec((1, D_MODEL), lambda i: (0, 0))]
        + [pl.BlockSpec(memory_space=pl.ANY)] * len(order),
        out_specs=pl.BlockSpec((ROW_TILE, D_MODEL), lambda i: (i, 0)),
        out_shape=jax.ShapeDtypeStruct((SEQ, D_MODEL), MXU_DTYPE),
    )(x, g, *order)


def rmsnorm_bwd(x, g, dn, dres, name):
    def body(x_ref, g_ref, dn_ref, dr_ref, dx_ref, dxm_ref, dg_ref):
        xv = x_ref[...]
        r = lax.rsqrt(jnp.mean(xv * xv, axis=-1, keepdims=True) + EPS)
        xh = xv * r
        dnv = dn_ref[...]

        @pl.when(pl.program_id(0) == 0)
        def _():
            dg_ref[...] = jnp.zeros_like(dg_ref)

        dg_ref[...] += jnp.sum(dnv * xh, axis=0, keepdims=True)
        dxh = dnv * g_ref[...]
        dx = dr_ref[...] + r * (dxh - xh * jnp.mean(dxh * xh, axis=-1, keepdims=True))
        dx_ref[...] = dx
        dxm_ref[...] = dx.astype(dxm_ref.dtype)

    row = pl.BlockSpec((ROW_TILE, D_MODEL), lambda i: (i, 0))
    vec = pl.BlockSpec((1, D_MODEL), lambda i: (0, 0))
    return pl.pallas_call(
        body, name=name, grid=(SEQ // ROW_TILE,), in_specs=[row, vec, row, row], out_specs=[row, row, vec],
        out_shape=[jax.ShapeDtypeStruct((SEQ, D_MODEL), F32), jax.ShapeDtypeStruct((SEQ, D_MODEL), MXU_DTYPE),
                   jax.ShapeDtypeStruct((1, D_MODEL), F32)],
        compiler_params=pltpu.CompilerParams(dimension_semantics=("arbitrary",)),
    )(x, g, dn, dres)


def final_norm_loss(h, g, target, name):
    def body(h_ref, g_ref, t_ref, dh_ref, dhm_ref, dg_ref, loss_ref):
        xv = h_ref[...]
        r = lax.rsqrt(jnp.mean(xv * xv, axis=-1, keepdims=True) + EPS)
        xh = xv * r
        gv = g_ref[...]
        e = xh * gv - t_ref[...]

        @pl.when(pl.program_id(0) == 0)
        def _():
            dg_ref[...] = jnp.zeros_like(dg_ref)
            loss_ref[...] = jnp.zeros_like(loss_ref)

        part = 0.5 * jnp.sum(jnp.sum(e * e, axis=-1, keepdims=True) * (1.0 / D_MODEL), axis=0, keepdims=True)
        loss_ref[...] += jnp.broadcast_to(part, loss_ref.shape)
        dout = e * (1.0 / D_MODEL)
        dg_ref[...] += jnp.sum(dout * xh, axis=0, keepdims=True)
        dxh = dout * gv
        dh = r * (dxh - xh * jnp.mean(dxh * xh, axis=-1, keepdims=True))
        dh_ref[...] = dh
        dhm_ref[...] = dh.astype(dhm_ref.dtype)

    row = pl.BlockSpec((ROW_TILE, D_MODEL), lambda i: (i, 0))
    vec = pl.BlockSpec((1, D_MODEL), lambda i: (0, 0))
    return pl.pallas_call(
        body, name=name, grid=(SEQ // ROW_TILE,), in_specs=[row, vec, row],
        out_specs=[row, row, vec, pl.BlockSpec((8, 128), lambda i: (0, 0))],
        out_shape=[jax.ShapeDtypeStruct((SEQ, D_MODEL), F32), jax.ShapeDtypeStruct((SEQ, D_MODEL), MXU_DTYPE),
                   jax.ShapeDtypeStruct((1, D_MODEL), F32), jax.ShapeDtypeStruct((8, 128), F32)],
        compiler_params=pltpu.CompilerParams(dimension_semantics=("arbitrary",)),
    )(h, g, target)


FFN_TILE = 256
FFN_TILES = D_FF // FFN_TILE


def gate_up_swiglu(n, w_gu, name):
    def body(n_ref, wa_ref, wb_ref, a_ref, b_ref, s_ref):
        nv = n_ref[...]
        a = _dot(nv, wa_ref[...])
        b = _dot(nv, wb_ref[...])
        a_ref[...] = a.astype(a_ref.dtype)
        b_ref[...] = b.astype(b_ref.dtype)
        s_ref[...] = (a * _sigmoid(a) * b).astype(s_ref.dtype)

    tile = pl.BlockSpec((SEQ, FFN_TILE), lambda j: (0, j))
    act = jax.ShapeDtypeStruct((SEQ, D_FF), ACT_DTYPE)
    return pl.pallas_call(
        body, name=name, grid=(FFN_TILES,),
        in_specs=[pl.BlockSpec((SEQ, D_MODEL), lambda j: (0, 0)), pl.BlockSpec((D_MODEL, FFN_TILE), lambda j: (0, j)),
                  pl.BlockSpec((D_MODEL, FFN_TILE), lambda j: (0, j + FFN_TILES))],
        out_specs=[tile, tile, tile], out_shape=[act, act, jax.ShapeDtypeStruct((SEQ, D_FF), MXU_DTYPE)],
        compiler_params=pltpu.CompilerParams(dimension_semantics=("parallel",), vmem_limit_bytes=MATMUL_VMEM_BYTES),
    )(n, w_gu, w_gu)


def swiglu_bwd(a, b, ds, name):
    rows = ROW_TILE // 2

    def body(a_ref, b_ref, ds_ref, o_ref):
        av = a_ref[...].astype(F32)
        sg = _sigmoid(av)
        dsv = ds_ref[...].astype(F32)
        o_ref[:, :D_FF] = (dsv * b_ref[...].astype(F32) * (sg * (1.0 + av * (1.0 - sg)))).astype(o_ref.dtype)
        o_ref[:, D_FF:] = (dsv * av * sg).astype(o_ref.dtype)

    blk = pl.BlockSpec((rows, D_FF), lambda i: (i, 0))
    return pl.pallas_call(
        body, name=name, grid=(SEQ // rows,), in_specs=[blk, blk, blk],
        out_specs=pl.BlockSpec((rows, 2 * D_FF), lambda i: (i, 0)),
        out_shape=jax.ShapeDtypeStruct((SEQ, 2 * D_FF), MXU_DTYPE), compiler_params=SUM_PARAMS,
    )(a, b, ds)


GATE_HG_BLK = 6656 // 512
GATE_ATT_BLK = 7680 // 512


def merge_fwd(z, bh, ba, name):
    def body(gh_ref, ga_ref, bh_ref, ba_ref, o_ref):
        o_ref[...] = (_sigmoid(gh_ref[...]) * bh_ref[...] + _sigmoid(ga_ref[...]) * ba_ref[...]).astype(o_ref.dtype)

    blk = pl.BlockSpec((ROW_TILE, 512), lambda i, j: (i, j))
    return pl.pallas_call(
        body, name=name, grid=(SEQ // ROW_TILE, 2),
        in_specs=[pl.BlockSpec((ROW_TILE, 512), lambda i, j: (i, GATE_HG_BLK + j)),
                  pl.BlockSpec((ROW_TILE, 512), lambda i, j: (i, GATE_ATT_BLK + j)), blk, blk],
        out_specs=blk, out_shape=jax.ShapeDtypeStruct((SEQ, D_MODEL), MXU_DTYPE),
    )(z, z, bh, ba)


def merge_bwd(z, bh, ba, dm, name):
    def body(gh_ref, ga_ref, bh_ref, ba_ref, dm_ref, dbh_ref, dba_ref, dgh_ref, dga_ref):
        dmv = dm_ref[...]
        sh = _sigmoid(gh_ref[...])
        sa = _sigmoid(ga_ref[...])
        dbh_ref[...] = (dmv * sh).astype(dbh_ref.dtype)
        dba_ref[...] = (dmv * sa).astype(dba_ref.dtype)
        dgh_ref[...] = (dmv * bh_ref[...] * (sh * (1.0 - sh))).astype(dgh_ref.dtype)
        dga_ref[...] = (dmv * ba_ref[...] * (sa * (1.0 - sa))).astype(dga_ref.dtype)

    blk = pl.BlockSpec((ROW_TILE, 512), lambda i, j: (i, j))
    out = jax.ShapeDtypeStruct((SEQ, D_MODEL), MXU_DTYPE)
    return pl.pallas_call(
        body, name=name, grid=(SEQ // ROW_TILE, 2),
        in_specs=[pl.BlockSpec((ROW_TILE, 512), lambda i, j: (i, GATE_HG_BLK + j)),
                  pl.BlockSpec((ROW_TILE, 512), lambda i, j: (i, GATE_ATT_BLK + j)), blk, blk, blk],
        out_specs=[blk, blk, blk, blk], out_shape=[out, out, out, out],
    )(z, z, bh, ba, dm)


N_CHUNKS = SEQ // HG_CHUNK
HG_STEP_CHUNKS = 4


def _hgrn_gates(q, fp, lb):
    C = HG_CHUNK
    sg = _sigmoid(fp)
    f = lb + (1.0 - lb) * sg
    lf = jnp.log(f)
    row = lax.broadcasted_iota(jnp.int32, (C, C), 0)
    col = lax.broadcasted_iota(jnp.int32, (C, C), 1)
    causal = row >= col
    G = _dot_f32(causal.astype(F32), lf)
    eG = jnp.exp(G)
    enG = jnp.exp(-G)
    qg = q * eG
    kg = (1.0 - f) * enG
    A = jnp.where(causal, _hdot(qg, kg, tb=True), 0.0)
    egl = jnp.exp(jnp.sum(lf, axis=0, keepdims=True))
    return sg, f, causal, eG, enG, qg, kg, A, egl


def hgrn_fwd(z, lb, gain, name):
    C, K = HG_CHUNK, HG_DIM

    def body(q_ref, f_ref, v_ref, og_ref, p_ref, g_ref, y_ref, o_ref, st_ref, state):
        @pl.when(pl.program_id(0) == 0)
        def _():
            state[...] = jnp.zeros_like(state)

        for cc in range(HG_STEP_CHUNKS):
            rows = pl.ds(cc * C, C)
            for h in range(HG_HEADS):
                hd = pl.ds(h * K, K)
                v = v_ref[rows, hd]
                _, _, _, _, _, qg, kg, A, egl = _hgrn_gates(q_ref[rows, hd], f_ref[rows, hd], p_ref[:, hd])
                st = state[h]
                st_ref[h, cc] = st
                o = _hdot(A, v) + _hdot(qg, st, tb=True)
                state[h] = st * egl + _hdot(v, kg * egl, ta=True)
                o_ref[rows, hd] = o
                rs = lax.rsqrt(jnp.mean(o * o, axis=-1, keepdims=True) + EPS)
                og = og_ref[rows, hd]
                y_ref[rows, hd] = (((o * rs) * g_ref[:, hd]) * (og * _sigmoid(og))).astype(y_ref.dtype)

    R = HG_STEP_CHUNKS * C

    def zcol(section):
        return pl.BlockSpec((R, HG_WIDTH), lambda c: (c, section))

    vec = pl.BlockSpec((1, HG_WIDTH), lambda c: (0, 0))
    blk = pl.BlockSpec((R, HG_WIDTH), lambda c: (c, 0))
    return pl.pallas_call(
        body, name=name, grid=(N_CHUNKS // HG_STEP_CHUNKS,),
        in_specs=[zcol(0), zcol(1), zcol(2), zcol(3), vec, vec],
        out_specs=[blk, blk, pl.BlockSpec((HG_HEADS, HG_STEP_CHUNKS, K, K), lambda c: (0, c, 0, 0))],
        out_shape=[jax.ShapeDtypeStruct((SEQ, HG_WIDTH), MXU_DTYPE), jax.ShapeDtypeStruct((SEQ, HG_WIDTH), F32),
                   jax.ShapeDtypeStruct((HG_HEADS, N_CHUNKS, K, K), F32)],
        scratch_shapes=[pltpu.VMEM((HG_HEADS, K, K), F32)],
        compiler_params=pltpu.CompilerParams(dimension_semantics=("arbitrary",)),
    )(z, z, z, z, lb, gain)


def hgrn_bwd(z, lb, gain, o_raw, states, dy, name):
    C, K = HG_CHUNK, HG_DIM

    def body(q_ref, f_ref, v_ref, og_ref, p_ref, g_ref, o_ref, st_ref, dy_ref,
             dq_ref, dfp_ref, dv_ref, dog_ref, dlb_ref, dgain_ref, dstate):
        @pl.when(pl.program_id(0) == 0)
        def _():
            dstate[...] = jnp.zeros_like(dstate)
            dlb_ref[...] = jnp.zeros_like(dlb_ref)
            dgain_ref[...] = jnp.zeros_like(dgain_ref)

        last = lax.broadcasted_iota(jnp.int32, (C, K), 0) == C - 1
        row = lax.broadcasted_iota(jnp.int32, (C, C), 0)
        col = lax.broadcasted_iota(jnp.int32, (C, C), 1)
        anti_causal = (col >= row).astype(F32)
        for cc in reversed(range(HG_STEP_CHUNKS)):
            rows = pl.ds(cc * C, C)
            for h in range(HG_HEADS):
                hd = pl.ds(h * K, K)
                v = v_ref[rows, hd]
                lb = p_ref[:, hd]
                sg, f, causal, eG, enG, qg, kg, A, egl = _hgrn_gates(q_ref[rows, hd], f_ref[rows, hd], lb)
                kd = kg * egl
                st = st_ref[h, cc]
                dst = dstate[h]
                o = o_ref[rows, hd]
                og = og_ref[rows, hd]
                gain_v = g_ref[:, hd]
                dyv = dy_ref[rows, hd]
                rs = lax.rsqrt(jnp.mean(o * o, axis=-1, keepdims=True) + EPS)
                on = o * rs
                sgo = _sigmoid(og)
                silu = og * sgo
                dog_ref[rows, hd] = (dyv * (on * gain_v) * (sgo * (1.0 + og * (1.0 - sgo)))).astype(dog_ref.dtype)
                dgain_ref[:, hd] += jnp.sum(dyv * silu * on, axis=0, keepdims=True)
                don = dyv * gain_v * silu
                do = rs * (don - on * jnp.mean(don * on, axis=-1, keepdims=True))
                dA = jnp.where(causal, _hdot(do, v, tb=True), 0.0)
                dv_ref[rows, hd] = (_hdot(A, do, ta=True) + _hdot(kd, dst, tb=True)).astype(dv_ref.dtype)
                dqg = _hdot(dA, kg) + _hdot(do, st)
                dkg = _hdot(dA, qg, ta=True)
                dkd = _hdot(v, dst)
                dstate[h] = dst * egl + _hdot(do, qg, ta=True)
                dgl = jnp.sum(st * dst, axis=0, keepdims=True) * egl
                dq_ref[rows, hd] = (dqg * eG).astype(dq_ref.dtype)
                dk = dkg * enG + dkd * (enG * egl)
                dG = dqg * qg - dkg * kg - dkd * kd
                extra = jnp.sum(dkd * kd, axis=0, keepdims=True) + dgl
                dG = dG + jnp.where(last, extra, 0.0)
                dlf = _dot_f32(anti_causal, dG)
                df = dlf / f - dk
                dfp_ref[rows, hd] = (df * (1.0 - lb) * (sg * (1.0 - sg))).astype(dfp_ref.dtype)
                dlb_ref[:, hd] += jnp.sum(df * (1.0 - sg), axis=0, keepdims=True)

    R = HG_STEP_CHUNKS * C
    n_steps = N_CHUNKS // HG_STEP_CHUNKS

    def rc(c):
        return n_steps - 1 - c

    def zcol(section):
        return pl.BlockSpec((R, HG_WIDTH), lambda c: (rc(c), section))

    vec = pl.BlockSpec((1, HG_WIDTH), lambda c: (0, 0))
    blk = pl.BlockSpec((R, HG_WIDTH), lambda c: (rc(c), 0))
    out = jax.ShapeDtypeStruct((SEQ, HG_WIDTH), MXU_DTYPE)
    small = jax.ShapeDtypeStruct((1, HG_WIDTH), F32)
    return pl.pallas_call(
        body, name=name, grid=(n_steps,),
        in_specs=[zcol(0), zcol(1), zcol(2), zcol(3), vec, vec, blk,
                  pl.BlockSpec((HG_HEADS, HG_STEP_CHUNKS, K, K), lambda c: (0, rc(c), 0, 0)), blk],
        out_specs=[blk, blk, blk, blk, vec, vec],
        out_shape=[out, out, out, out, small, small],
        scratch_shapes=[pltpu.VMEM((HG_HEADS, K, K), F32)],
        compiler_params=pltpu.CompilerParams(dimension_semantics=("arbitrary",)),
    )(z, z, z, z, lb, gain, o_raw, states, dy)


N_GROUPS = len(ATT_GROUPS)
HEAD_PAIRS = ATT_WIDTH // 128
ATT_COL0 = 4 * HG_WIDTH
UNROLLED_UNITS = 4
ATT_SLAB_BLOCKS = 4


def _alibi_coef():
    n = N_GROUPS * ATT_HEADS
    slopes = np.exp2(-ALIBI_MAX * np.arange(1, n + 1, dtype=np.float32) / n).astype(np.float32)
    dil = np.repeat(np.array([d for _, d in ATT_GROUPS], np.float32), ATT_HEADS)
    return jnp.asarray(slopes * dil, F32)


def _for_each_unit(n, fn):
    if n <= UNROLLED_UNITS:
        for u in range(n):
            fn(u)
    else:
        def group(i, carry):
            for j in range(UNROLLED_UNITS):
                fn(i * UNROLLED_UNITS + j)
            return carry
        lax.fori_loop(0, n // UNROLLED_UNITS, group, 0)


def _att_specs(g):
    B = ATT_BLOCK
    d = ATT_GROUPS[g][1]
    blocks = ATT_SLAB_BLOCKS if d == 1 else 1
    R = B * d * blocks
    n_slabs = SEQ // R
    multi = SEQ // d > B
    col0 = (ATT_COL0 + g * 3 * ATT_WIDTH) // 128

    def cur(col):
        return pl.BlockSpec((R, 128), lambda hp, s: (s, col + hp))

    def prev(col):
        return pl.BlockSpec((R, 128), lambda hp, s: (jnp.maximum(s - 1, 0), col + hp))

    def unit(u, s):
        if d > 1:
            rows = pl.ds(u, B, stride=d)
            return rows, False, rows, jnp.where(s == 0, B, 0)
        inner_prev = u > 0
        return (pl.ds(u * B, B), inner_prev, pl.ds((u - 1) * B if inner_prev else (blocks - 1) * B, B),
                0 if inner_prev else jnp.where(s == 0, B, 0))

    return d * blocks, R, n_slabs, multi, col0, cur, prev, unit


def _head_lanes(j):
    lane = lax.broadcasted_iota(jnp.int32, (ATT_BLOCK, 128), 1)
    return (lane >= 64 * j) & (lane < 64 * (j + 1))


def _stack_heads(x, sel0):
    return jnp.concatenate([jnp.where(sel0, x, 0.0), jnp.where(sel0, 0.0, x)], axis=0)


def _stack_values(x, sel0, lanes):
    swapped = pltpu.roll(x, 64, 1)
    stacked = jnp.concatenate([jnp.where(sel0, x, swapped), jnp.where(sel0, swapped, x)], axis=0)
    return stacked if lanes == 128 else jnp.concatenate([stacked] * (lanes // 128), axis=1)


def _pair_coef(coef_ref, g, hp):
    row = lax.broadcasted_iota(jnp.int32, (2 * ATT_BLOCK, 1), 0)
    first = g * ATT_HEADS + hp * 2
    return jnp.where(row < ATT_BLOCK, coef_ref[first], coef_ref[first + 1])


def _band(with_prev, first_key):
    B = ATT_BLOCK
    keys = 2 * B if with_prev else B
    qi = jnp.bitwise_and(lax.broadcasted_iota(jnp.int32, (2 * B, keys), 0), B - 1)
    kj = lax.broadcasted_iota(jnp.int32, (2 * B, keys), 1)
    delta = qi + (B if with_prev else 0) - kj
    valid = (delta >= 0) & (delta <= B)
    if with_prev:
        valid = valid & (kj >= first_key)
    return valid, delta.astype(F32)


def att_fwd(z, g, name):
    B = ATT_BLOCK
    n_units, R, n_slabs, has_prev, col0, cur, prev, unit = _att_specs(g)

    def body(coef_ref, *refs):
        if has_prev:
            q_ref, kc_ref, vc_ref, kp_ref, vp_ref, o_ref, l_ref = refs
        else:
            q_ref, kc_ref, vc_ref, o_ref, l_ref = refs
        hp, s = pl.program_id(0), pl.program_id(1)
        cf2 = _pair_coef(coef_ref, g, hp)
        sel0 = _head_lanes(0)

        def one(u):
            rows, inner_prev, prev_rows, first_key = unit(u, s)
            valid, dist = _band(has_prev, first_key)
            q2 = _stack_heads(q_ref[rows, :], sel0)
            kk, vv = kc_ref[rows, :], vc_ref[rows, :]
            if has_prev:
                k_from, v_from = (kc_ref, vc_ref) if inner_prev else (kp_ref, vp_ref)
                kk = jnp.concatenate([k_from[prev_rows, :], kk], axis=0)
                vv = jnp.concatenate([v_from[prev_rows, :], vv], axis=0)
            sc = jnp.where(valid, _dot(q2, kk, tb=True) * 0.125 - cf2 * dist, NEG_INF)
            mx = jnp.max(sc, axis=-1, keepdims=True)
            e = jnp.exp(sc - mx)
            den = jnp.sum(e, axis=-1, keepdims=True)
            o2 = _dot(e * (1.0 / den), vv)
            lse2 = mx + jnp.log(den)
            o_ref[rows, :] = jnp.where(sel0, o2[:B], o2[B:])
            l_ref[rows, :] = jnp.where(sel0, lse2[:B], lse2[B:])

        _for_each_unit(n_units, one)

    in_specs = [pl.BlockSpec(memory_space=pltpu.SMEM), cur(col0), cur(col0 + 4), cur(col0 + 8)]
    args = [_alibi_coef(), z, z, z]
    if has_prev:
        in_specs += [prev(col0 + 4), prev(col0 + 8)]
        args += [z, z]
    out = jax.ShapeDtypeStruct((SEQ, ATT_WIDTH), F32)
    return pl.pallas_call(
        body, name=name, grid=(HEAD_PAIRS, n_slabs), in_specs=in_specs,
        out_specs=[cur(0), cur(0)], out_shape=[out, out],
        compiler_params=pltpu.CompilerParams(dimension_semantics=("parallel", "arbitrary")),
    )(*args)


def att_bwd(z, l, do, corr, g, name):
    B = ATT_BLOCK
    d = ATT_GROUPS[g][1]
    n_blocks = SEQ // (d * B)
    multi = n_blocks > 1
    col0 = (ATT_COL0 + g * 3 * ATT_WIDTH) // 128
    own = slice(B, 2 * B) if multi else slice(0, B)

    def body(coef_ref, q_ref, k_ref, v_ref, l_ref, do_ref, cr_ref, dq_ref, dk_ref, dv_ref, dq_sc, dk_sc, dv_sc):
        hp = pl.program_id(0)
        cf2 = _pair_coef(coef_ref, g, hp)
        sel0 = _head_lanes(0)

        def block_rows(b, r):
            return pl.ds(b * (B * d) + r, B, stride=d) if d > 1 else pl.ds(pl.multiple_of(b * B, B), B)

        def one(u):
            b, r = (u, 0) if d == 1 else (u // d, u % d)
            rows = block_rows(b, r)
            valid, dist = _band(multi, jnp.where(b == 0, B, 0))
            kk, vv = k_ref[rows, :], v_ref[rows, :]
            if multi:
                prev_rows = block_rows(jnp.maximum(b - 1, 0), r)
                kk = jnp.concatenate([k_ref[prev_rows, :], kk], axis=0)
                vv = jnp.concatenate([v_ref[prev_rows, :], vv], axis=0)
            q2, do2 = _stack_heads(q_ref[rows, :], sel0), _stack_heads(do_ref[rows, :], sel0)
            keys = kk.shape[0]
            lse2, cr2 = _stack_values(l_ref[rows, :], sel0, keys), _stack_values(cr_ref[rows, :], sel0, keys)
            p = jnp.exp(jnp.where(valid, _dot(q2, kk, tb=True) * 0.125 - cf2 * dist, NEG_INF) - lse2)
            ds = p * (_dot(do2, vv, tb=True) + cr2)
            dq2 = _dot(ds, kk)
            dkk = _dot(ds, q2, ta=True) * 0.125
            dvv = _dot(p, do2, ta=True)
            dq_sc[rows, :] = jnp.where(sel0, dq2[:B], dq2[B:]) * 0.125
            dk_sc[rows, :] = dkk[own]
            dv_sc[rows, :] = dvv[own]
            if multi:
                dk_sc[prev_rows, :] += dkk[:B]
                dv_sc[prev_rows, :] += dvv[:B]

        _for_each_unit(d * n_blocks, one)
        dq_ref[...] = dq_sc[...].astype(dq_ref.dtype)
        dk_ref[...] = dk_sc[...].astype(dk_ref.dtype)
        dv_ref[...] = dv_sc[...].astype(dv_ref.dtype)

    def col(c):
        return pl.BlockSpec((SEQ, 128), lambda hp: (0, c + hp))

    out = jax.ShapeDtypeStruct((SEQ, ATT_WIDTH), MXU_DTYPE)
    return pl.pallas_call(
        body, name=name, grid=(HEAD_PAIRS,),
        in_specs=[pl.BlockSpec(memory_space=pltpu.SMEM), col(col0), col(col0 + 4), col(col0 + 8), col(0), col(0), col(0)],
        out_specs=[col(0)] * 3, out_shape=[out] * 3,
        scratch_shapes=[pltpu.VMEM((SEQ, 128), F32)] * 3,
        compiler_params=pltpu.CompilerParams(dimension_semantics=("parallel",), vmem_limit_bytes=MATMUL_VMEM_BYTES),
    )(_alibi_coef(), z, z, z, l, do, corr)


def _head_sum(x):
    i = lax.broadcasted_iota(jnp.int32, (128, 128), 0) // 64
    j = lax.broadcasted_iota(jnp.int32, (128, 128), 1) // 64
    return _dot_f32(x, (i == j).astype(F32), ones_on_right=True)


def _group_weights(l0, l1, l2):
    mx = jnp.maximum(jnp.maximum(l0, l1), l2)
    e0, e1, e2 = jnp.exp(l0 - mx), jnp.exp(l1 - mx), jnp.exp(l2 - mx)
    inv = 1.0 / (e0 + e1 + e2)
    return e0 * inv, e1 * inv, e2 * inv


def att_combine_fwd(o, l, name):
    def body(o0, o1, o2, l0, l1, l2, y_ref):
        w0, w1, w2 = _group_weights(l0[...], l1[...], l2[...])
        y_ref[...] = (o0[...] * w0 + o1[...] * w1 + o2[...] * w2).astype(y_ref.dtype)

    blk = pl.BlockSpec((ROW_TILE, ATT_WIDTH), lambda i: (i, 0))
    return pl.pallas_call(
        body, name=name, grid=(SEQ // ROW_TILE,), in_specs=[blk] * 6, out_specs=blk,
        out_shape=jax.ShapeDtypeStruct((SEQ, ATT_WIDTH), MXU_DTYPE),
    )(*o, *l)


def att_combine_bwd(o, l, dy, name):
    def body(o0, o1, o2, l0, l1, l2, dy_ref, do0, do1, do2, cr0, cr1, cr2):
        w = _group_weights(l0[...], l1[...], l2[...])
        dyv = dy_ref[...]
        tot = _head_sum(dyv * (w[0] * o0[...] + w[1] * o1[...] + w[2] * o2[...]))
        for g, (do_ref, cr_ref) in enumerate(((do0, cr0), (do1, cr1), (do2, cr2))):
            do_ref[...] = dyv * w[g]
            cr_ref[...] = -w[g] * tot

    blk = pl.BlockSpec((ROW_TILE, 128), lambda i, j: (i, j))
    out = jax.ShapeDtypeStruct((SEQ, ATT_WIDTH), F32)
    res = pl.pallas_call(
        body, name=name, grid=(SEQ // ROW_TILE, HEAD_PAIRS), in_specs=[blk] * 7, out_specs=[blk] * 6, out_shape=[out] * 6,
    )(*o, *l, dy)
    return res[:N_GROUPS], res[N_GROUPS:]


SUM_ROW_TILES = (1024, 512, 256, 128, 64, 32, 16)
SUM_TILE_BYTES = 24 * 1024 * 1024
SUM_PARAMS = pltpu.CompilerParams(vmem_limit_bytes=MATMUL_VMEM_BYTES)


def _row_tile(rows, cols, operands):
    fit = [t for t in SUM_ROW_TILES if rows % t == 0]
    return next((t for t in fit if 2 * 4 * operands * t * cols <= SUM_TILE_BYTES), fit[-1])


def _shard_shape(rows, cols, axis):
    return (rows // N_CHIPS, cols) if axis == 0 else (rows, cols // N_CHIPS)


def _half_shape(rows, cols, axis):
    return (rows, cols // 2) if axis == 0 else (rows // 2, cols)


def _piece_shape(rows, cols, axis):
    return (rows // N_CHIPS, cols // 2) if axis == 0 else (rows // 2, cols // N_CHIPS)


def place_own_block(shard, chip, rows, cols, axis, name):
    sr, sc = _shard_shape(rows, cols, axis)
    tr = _row_tile(sr, sc, 2)

    def body(chip_ref, s_ref, o_ref):
        o_ref[...] = s_ref[...].astype(o_ref.dtype)

    if axis == 0:
        out_map = lambda i, chip_ref: (chip_ref[0] * (sr // tr) + i, 0)
    else:
        out_map = lambda i, chip_ref: (i, chip_ref[0])
    return pl.pallas_call(
        body, name=name, out_shape=jax.ShapeDtypeStruct((rows, cols), WEIGHT_COMM_DTYPE), compiler_params=SUM_PARAMS,
        grid_spec=pltpu.PrefetchScalarGridSpec(
            num_scalar_prefetch=1, grid=(sr // tr,), in_specs=[pl.BlockSpec((tr, sc), lambda i, chip_ref: (i, 0))],
            out_specs=pl.BlockSpec((tr, sc), out_map)),
    )(chip, shard)


def add_halves(g, theirs, core, rows, cols, axis, name):
    hr, hc = _half_shape(rows, cols, axis)
    tr = _row_tile(hr, hc, 3)

    def body(core_ref, g_ref, t_ref, o_ref):
        o_ref[...] = (g_ref[...].astype(F32) + t_ref[...].astype(F32)).astype(o_ref.dtype)

    if axis == 0:
        g_map = lambda i, core_ref: (i, core_ref[0])
    else:
        g_map = lambda i, core_ref: (core_ref[0] * (hr // tr) + i, 0)
    blk = pl.BlockSpec((tr, hc), lambda i, core_ref: (i, 0))
    return pl.pallas_call(
        body, name=name, out_shape=jax.ShapeDtypeStruct((hr, hc), GRAD_COMM_DTYPE), compiler_params=SUM_PARAMS,
        grid_spec=pltpu.PrefetchScalarGridSpec(
            num_scalar_prefetch=1, grid=(hr // tr,), in_specs=[pl.BlockSpec((tr, hc), g_map), blk], out_specs=blk),
    )(core, g, theirs)


def add_pieces(half, got, chip, rows, cols, axis, name):
    hr, _ = _half_shape(rows, cols, axis)
    pr, pc = _piece_shape(rows, cols, axis)
    tr = _row_tile(pr, pc, 5)

    def body(chip_ref, h_ref, got_ref, o_ref):
        o_ref[...] = (h_ref[...].astype(F32) + got_ref[0].astype(F32) + got_ref[1].astype(F32) + got_ref[2].astype(F32))

    if axis == 0:
        h_map = lambda i, chip_ref: (chip_ref[0] * (pr // tr) + i, 0)
    else:
        h_map = lambda i, chip_ref: (i, chip_ref[0])
    return pl.pallas_call(
        body, name=name, out_shape=jax.ShapeDtypeStruct((pr, pc), F32), compiler_params=SUM_PARAMS,
        grid_spec=pltpu.PrefetchScalarGridSpec(
            num_scalar_prefetch=1, grid=(pr // tr,),
            in_specs=[pl.BlockSpec((tr, pc), h_map), pl.BlockSpec((3, tr, pc), lambda i, chip_ref: (0, i, 0))],
            out_specs=pl.BlockSpec((tr, pc), lambda i, chip_ref: (i, 0))),
    )(chip, half, got)


def _adamw_math(w, g, m, v):
    nm = ADAM_B1 * m + (1.0 - ADAM_B1) * g
    nv = ADAM_B2 * v + (1.0 - ADAM_B2) * (g * g)
    m_hat = nm / (1.0 - ADAM_B1 ** ADAM_STEP)
    v_hat = nv / (1.0 - ADAM_B2 ** ADAM_STEP)
    return -ADAM_LR * (m_hat / (jnp.sqrt(v_hat) + ADAM_EPS) + ADAM_WD * w), nm, nv


def adamw_halves(w, mine, theirs, m, v, core, rows, cols, axis, name):
    sr, sc = _shard_shape(rows, cols, axis)
    pr, pc = _piece_shape(rows, cols, axis)
    tr = _row_tile(pr, pc, 9)
    nt = pr // tr

    def body(core_ref, w_ref, a_ref, b_ref, m_ref, v_ref, g_ref, d_ref, nm_ref, nv_ref):
        g = jnp.where(pl.program_id(0) == core_ref[0], a_ref[...], b_ref[...])
        g_ref[...] = g
        d_ref[...], nm_ref[...], nv_ref[...] = _adamw_math(w_ref[...], g, m_ref[...], v_ref[...])

    if axis == 0:
        full = pl.BlockSpec((tr, pc), lambda h, i, core_ref: (i, h))
    else:
        full = pl.BlockSpec((tr, pc), lambda h, i, core_ref: (h * nt + i, 0))
    part = pl.BlockSpec((tr, pc), lambda h, i, core_ref: (i, 0))
    out = jax.ShapeDtypeStruct((sr, sc), F32)
    return pl.pallas_call(
        body, name=name, out_shape=[out, out, out, out], compiler_params=SUM_PARAMS,
        grid_spec=pltpu.PrefetchScalarGridSpec(
            num_scalar_prefetch=1, grid=(2, nt), in_specs=[full, part, part, full, full], out_specs=[full] * 4),
    )(core, w, mine, theirs, m, v)


BIG = (
    ("ffn1_w_gate_up", D_MODEL, 2 * D_FF, 1),
    ("ffn1_w_down", D_FF, D_MODEL, 0),
    ("w_in", D_MODEL, IN_COLS, 1),
    ("w_branch_hg", HG_WIDTH, D_MODEL, 1),
    ("w_branch_att", ATT_WIDTH, D_MODEL, 1),
    ("w_out", D_MODEL, D_MODEL, 0),
    ("ffn2_w_gate_up", D_MODEL, 2 * D_FF, 1),
    ("ffn2_w_down", D_FF, D_MODEL, 0),
)
N_BIG = len(BIG)
ANY = pl.BlockSpec(memory_space=pl.ANY)


def _place():
    return lax.axis_index("x"), lax.axis_index("y"), lax.axis_index("c")


def _other_chips(x, y):
    return ((1 - x, y), (x, 1 - y), (1 - x, 1 - y))


MAX_COPY_CHUNKS = 16
CHUNK_ROW_ALIGN = 16


def _row_chunks(view):
    rows = view.shape[0]
    n = next(n for n in range(MAX_COPY_CHUNKS, 0, -1) if rows % (CHUNK_ROW_ALIGN * n) == 0 or n == 1)
    step = rows // n
    return [pl.ds(i * step, step) for i in range(n)]


def _remote(src, dst, send_sem, recv_sem, device):
    return pltpu.make_async_remote_copy(src_ref=src, dst_ref=dst, send_sem=send_sem, recv_sem=recv_sem,
                                        device_id=device, device_id_type=MESH)


def _start_remote(src, dst, send_sem, recv_sem, device):
    for rows in _row_chunks(src):
        _remote(src.at[rows, :], dst.at[rows, :], send_sem, recv_sem, device).start()
    return _remote(src, dst, send_sem, recv_sem, device)


HBM = pl.BlockSpec(memory_space=pltpu.HBM)
SEM = pl.BlockSpec(memory_space=pltpu.SEMAPHORE)
SPLIT_COPY_EFFECT = pltpu.SideEffectType.DATAFLOW_SIDE_EFFECTING
GROUPS = {"ffn1": (0, 1), "mix": (2, 3, 4, 5), "ffn2": (6, 7)}


def _in_hbm(a):
    return pltpu.with_memory_space_constraint(a, pltpu.HBM)


class _SemList:
    def __init__(self, refs):
        self.refs = refs
        self.at = self

    def __getitem__(self, index):
        w, k = index
        return self.refs[3 * w + k]


def _gather_piece(ref, rows, cols, axis, chip, c):
    sr, sc = _shard_shape(rows, cols, axis)
    j = 2 * chip[0] + chip[1]
    if axis == 0:
        return ref.at[pl.ds(j * sr + c * (sr // 2), sr // 2), :]
    return ref.at[pl.ds(c * (sr // 2), sr // 2), pl.ds(pl.multiple_of(j * sc, 128), sc)]


def _start_gather_sends(bufs, ws, send_sems, recv_sems):
    x, y, c = _place()
    for w, (_, r, cc, ax) in enumerate(ws):
        mine = _gather_piece(bufs[w], r, cc, ax, (x, y), c)
        for k, chip in enumerate(_other_chips(x, y)):
            _start_remote(mine, mine, send_sems.at[w, k], recv_sems.at[w, k], (*chip, c))


def _wait_gather_sends(bufs, ws, send_sems, recv_sems):
    x, y, c = _place()
    for w, (_, r, cc, ax) in enumerate(ws):
        for k, chip in enumerate(_other_chips(x, y)):
            got = _gather_piece(bufs[w], r, cc, ax, chip, c)
            _remote(got, got, send_sems.at[w, k], recv_sems.at[w, k], (x, y, c)).wait_recv()
    for w, (_, r, cc, ax) in enumerate(ws):
        mine = _gather_piece(bufs[w], r, cc, ax, (x, y), c)
        for k in range(3):
            _remote(mine, mine, send_sems.at[w, k], recv_sems.at[w, k], (x, y, c)).wait_send()


def _forward_halves(bufs, ws, send_sems, recv_sems):
    x, y, c = _place()
    passed = []
    for w, (_, r, cc, ax) in enumerate(ws):
        for k, chip in enumerate(_other_chips(x, y)):
            got = _gather_piece(bufs[w], r, cc, ax, chip, c)
            passed.append(_start_remote(got, got, send_sems.at[w, k], recv_sems.at[w, k], (x, y, 1 - c)))
    for w, (_, r, cc, ax) in enumerate(ws):
        for k, chip in enumerate(_other_chips(x, y)):
            got = _gather_piece(bufs[w], r, cc, ax, chip, 1 - c)
            _remote(got, got, send_sems.at[w, k], recv_sems.at[w, k], (x, y, c)).wait_recv()
    for cp in passed:
        cp.wait_send()


def gather_start(placed, after, group):
    ws = [BIG[i] for i in GROUPS[group]]
    n = len(ws)

    def body(*refs):
        bufs = refs[:n]
        send_sems, recv_sems = _SemList(refs[n + 1:4 * n + 1]), _SemList(refs[4 * n + 1:7 * n + 1])
        token = refs[-1]
        _start_gather_sends(bufs, ws, send_sems, recv_sems)
        token[...] = jnp.zeros_like(token)

    out = pl.pallas_call(
        body, name=f"gather_start_{group}", in_specs=[HBM] * n + [ANY],
        out_specs=[SEM] * (6 * n) + [HBM] * n + [pl.BlockSpec(memory_space=pltpu.VMEM)],
        out_shape=[pltpu.SemaphoreType.DMA(())] * (6 * n)
        + [pltpu.HBM((r, cc), WEIGHT_COMM_DTYPE) for _, r, cc, _ in ws] + [jax.ShapeDtypeStruct((8, 128), F32)],
        input_output_aliases={w: 6 * n + w for w in range(n)},
        compiler_params=pltpu.CompilerParams(has_side_effects=SPLIT_COPY_EFFECT),
    )(*[_in_hbm(p) for p in placed], after)
    return out[:3 * n], out[3 * n:6 * n], out[6 * n:7 * n], out[-1]


def gather_wait(bufs, send_sems, recv_sems, after, group):
    ws = [BIG[i] for i in GROUPS[group]]
    n = len(ws)

    def body(*refs):
        _wait_gather_sends(refs[:n], ws, _SemList(refs[n:n + 3 * n]), _SemList(refs[n + 3 * n:n + 6 * n]))

    return pl.pallas_call(
        body, name=f"gather_wait_{group}", in_specs=[HBM] * n + [SEM] * (6 * n) + [ANY] * len(after), out_specs=[HBM] * n,
        out_shape=[pltpu.HBM((r, cc), WEIGHT_COMM_DTYPE) for _, r, cc, _ in ws],
        input_output_aliases={w: w for w in range(n)},
        compiler_params=pltpu.CompilerParams(has_side_effects=SPLIT_COPY_EFFECT),
    )(*bufs, *send_sems, *recv_sems, *after)


def gather_forward(bufs, group):
    ws = [BIG[i] for i in GROUPS[group]]
    n = len(ws)

    def body(*refs):
        _forward_halves(refs[n:2 * n], ws, refs[2 * n], refs[2 * n + 1])

    return pl.pallas_call(
        body, name=f"gather_forward_{group}", in_specs=[ANY] * n, out_specs=[ANY] * n,
        out_shape=[jax.ShapeDtypeStruct((r, cc), WEIGHT_COMM_DTYPE) for _, r, cc, _ in ws],
        input_output_aliases={w: w for w in range(n)},
        scratch_shapes=[pltpu.SemaphoreType.DMA((n, 3))] * 2,
    )(*bufs)


def _half(ref, rows, cols, axis, c):
    if axis == 0:
        return ref.at[:, pl.ds(pl.multiple_of(c * (cols // 2), 128), cols // 2)]
    return ref.at[pl.ds(c * (rows // 2), rows // 2), :]


def _piece_of_half(ref, rows, cols, axis, chip):
    j = 2 * chip[0] + chip[1]
    pr, pc = _piece_shape(rows, cols, axis)
    if axis == 0:
        return ref.at[pl.ds(j * pr, pr), :]
    return ref.at[:, pl.ds(pl.multiple_of(j * pc, 128), pc)]


def sibling_exchange_start(srcs, view, landing_shapes, dtype, name):
    n = len(srcs)

    def body(*refs):
        ins, land, sems = refs[:n], refs[n:2 * n], refs[2 * n:4 * n]
        x, y, c = _place()
        for w in range(n):
            _start_remote(view(ins[w], w, c), land[w], sems[w], sems[n + w], (x, y, 1 - c))
        refs[-1][...] = jnp.zeros_like(refs[-1])

    landing = [lax.empty(shape, dtype) for shape in landing_shapes]
    out = pl.pallas_call(
        body, name=name, in_specs=[HBM] * (2 * n),
        out_specs=[SEM] * (2 * n) + [HBM] * (2 * n) + [pl.BlockSpec(memory_space=pltpu.VMEM)],
        out_shape=[pltpu.SemaphoreType.DMA(())] * (2 * n) + [pltpu.HBM(a.shape, a.dtype) for a in srcs]
        + [pltpu.HBM(shape, dtype) for shape in landing_shapes] + [jax.ShapeDtypeStruct((8, 128), F32)],
        input_output_aliases={i: 2 * n + i for i in range(2 * n)},
        compiler_params=pltpu.CompilerParams(has_side_effects=SPLIT_COPY_EFFECT),
    )(*[_in_hbm(a) for a in srcs], *[_in_hbm(b) for b in landing])
    return out[:n], out[n:2 * n], out[2 * n:3 * n], out[3 * n:4 * n], out[-1]


def sibling_exchange_wait(srcs, landing, send_sems, recv_sems, view, after, name):
    n = len(srcs)

    def body(*refs):
        ins, land, sems = refs[:n], refs[n:2 * n], refs[2 * n:4 * n]
        x, y, c = _place()
        for w in range(n):
            cp = _remote(view(ins[w], w, c), land[w], sems[w], sems[n + w], (x, y, c))
            cp.wait_send()
            cp.wait_recv()

    out = pl.pallas_call(
        body, name=name, in_specs=[HBM] * (2 * n) + [SEM] * (2 * n) + [ANY] * len(after), out_specs=[HBM] * (2 * n),
        out_shape=[pltpu.HBM(a.shape, a.dtype) for a in srcs] + [pltpu.HBM(b.shape, b.dtype) for b in landing],
        input_output_aliases={i: i for i in range(2 * n)},
        compiler_params=pltpu.CompilerParams(has_side_effects=SPLIT_COPY_EFFECT),
    )(*srcs, *landing, *send_sems, *recv_sems, *after)
    return out[:n], out[n:]


def _scatter_copies(halves, got, ws, send_sems, recv_sems, start):
    x, y, c = _place()
    copies = []
    for w, (_, r, cc, ax) in enumerate(ws):
        for k, chip in enumerate(_other_chips(x, y)):
            args = (_piece_of_half(halves[w], r, cc, ax, chip), got[w].at[k], send_sems.at[w, k], recv_sems.at[w, k], (*chip, c))
            copies.append(_start_remote(*args) if start else _remote(*args))
    return copies


def scatter_start(halves, group):
    ws = [BIG[i] for i in GROUPS[group]]
    n = len(ws)

    def body(*refs):
        sems = refs[2 * n:8 * n]
        _scatter_copies(refs[:n], refs[n:2 * n], ws, _SemList(sems[:3 * n]), _SemList(sems[3 * n:]), start=True)
        refs[-1][...] = jnp.zeros_like(refs[-1])

    landing = [lax.empty((3,) + _piece_shape(r, cc, ax), GRAD_COMM_DTYPE) for _, r, cc, ax in ws]
    out = pl.pallas_call(
        body, name=f"scatter_start_{group}", in_specs=[HBM] * (2 * n),
        out_specs=[SEM] * (6 * n) + [HBM] * (2 * n) + [pl.BlockSpec(memory_space=pltpu.VMEM)],
        out_shape=[pltpu.SemaphoreType.DMA(())] * (6 * n)
        + [pltpu.HBM(_half_shape(r, cc, ax), GRAD_COMM_DTYPE) for _, r, cc, ax in ws]
        + [pltpu.HBM((3,) + _piece_shape(r, cc, ax), GRAD_COMM_DTYPE) for _, r, cc, ax in ws]
        + [jax.ShapeDtypeStruct((8, 128), F32)],
        input_output_aliases={i: 6 * n + i for i in range(2 * n)},
        compiler_params=pltpu.CompilerParams(has_side_effects=SPLIT_COPY_EFFECT),
    )(*[_in_hbm(h) for h in halves], *[_in_hbm(b) for b in landing])
    return out[:3 * n], out[3 * n:6 * n], out[6 * n:7 * n], out[7 * n:8 * n], out[-1]


def scatter_wait(halves, got, send_sems, recv_sems, after, group):
    ws = [BIG[i] for i in GROUPS[group]]
    n = len(ws)

    def body(*refs):
        sems = refs[2 * n:8 * n]
        for cp in _scatter_copies(refs[:n], refs[n:2 * n], ws, _SemList(sems[:3 * n]), _SemList(sems[3 * n:]), start=False):
            cp.wait_send()
            cp.wait_recv()

    out = pl.pallas_call(
        body, name=f"scatter_wait_{group}", in_specs=[HBM] * (2 * n) + [SEM] * (6 * n) + [ANY] * len(after),
        out_specs=[HBM] * (2 * n),
        out_shape=[pltpu.HBM(_half_shape(r, cc, ax), GRAD_COMM_DTYPE) for _, r, cc, ax in ws]
        + [pltpu.HBM((3,) + _piece_shape(r, cc, ax), GRAD_COMM_DTYPE) for _, r, cc, ax in ws],
        input_output_aliases={i: i for i in range(2 * n)},
        compiler_params=pltpu.CompilerParams(has_side_effects=SPLIT_COPY_EFFECT),
    )(*halves, *got, *send_sems, *recv_sems, *after)
    return out[:n], out[n:]


N_DEV = 8
SMALL = ("ffn1_norm", "mix_norm", "hg_lower_bounds", "hg_out_norm", "ffn2_norm", "final_norm")
SMALL_STAGE_ROWS = 8


def small_step(loss, grads, w, m, v, behind):
    n = len(SMALL)
    shapes = [g.shape for g in grads]
    first_row = [sum(s[0] for s in shapes[:i]) for i in range(n + 1)]
    assert first_row[n] < SMALL_STAGE_ROWS
    loss_row = (pl.ds(first_row[n], 1), pl.ds(0, loss.shape[1]))

    def body(*refs):
        loss_ref, g_refs, w_refs, m_refs, v_refs = refs[0], refs[1:1 + n], refs[1 + n:1 + 2 * n], refs[1 + 2 * n:1 + 3 * n], refs[1 + 3 * n:1 + 4 * n]
        outs = refs[2 + 4 * n:3 + 8 * n]
        loss_out, dg_refs, d_refs, nm_refs, nv_refs = outs[0], outs[1:1 + n], outs[1 + n:1 + 2 * n], outs[1 + 2 * n:1 + 3 * n], outs[1 + 3 * n:]
        stage, gathered, send_sems, recv_sems = refs[3 + 8 * n:]
        x, y, c = _place()
        me = 4 * x + 2 * y + c

        def slot(i, shape):
            return pl.ds(first_row[i], shape[0]), pl.ds(0, shape[1])

        stage[...] = jnp.zeros_like(stage)
        for i, g_ref in enumerate(g_refs):
            stage[slot(i, shapes[i])] = g_ref[...]
        stage[loss_row] = loss_ref[pl.ds(0, 1), :]
        gathered[me] = stage[...]
        copies = []
        for k in range(1, N_DEV):
            peer = (x ^ (k >> 2), y ^ ((k >> 1) & 1), c ^ (k & 1))
            cp = pltpu.make_async_remote_copy(
                src_ref=stage, dst_ref=gathered.at[me], send_sem=send_sems.at[k - 1], recv_sem=recv_sems.at[k - 1],
                device_id=peer, device_id_type=MESH)
            cp.start()
            copies.append(cp)
        for cp in copies:
            cp.wait()
        acc = gathered[0]
        for k in range(1, N_DEV):
            acc = acc + gathered[k]
        stage[...] = acc
        loss_out[...] = jnp.broadcast_to(stage[loss_row], loss_out.shape)
        for i in range(n):
            g = stage[slot(i, shapes[i])]
            dg_refs[i][...] = g
            d_refs[i][...], nm_refs[i][...], nv_refs[i][...] = _adamw_math(w_refs[i][...], g, m_refs[i][...], v_refs[i][...])

    vm = pl.BlockSpec(memory_space=pltpu.VMEM)
    per_param = [jax.ShapeDtypeStruct(s, F32) for s in shapes]
    out = pl.pallas_call(
        body, name="small_step", in_specs=[vm] * (1 + 4 * n) + [ANY], out_specs=[vm] * (1 + 4 * n),
        out_shape=[jax.ShapeDtypeStruct(loss.shape, F32)] + per_param * 4,
        scratch_shapes=[pltpu.VMEM((SMALL_STAGE_ROWS, D_MODEL), F32),
                        pltpu.VMEM((N_DEV, SMALL_STAGE_ROWS, D_MODEL), F32),
                        pltpu.SemaphoreType.DMA((N_DEV - 1,)), pltpu.SemaphoreType.DMA((N_DEV - 1,))],
    )(loss, *grads, *w, *m, *v, behind)
    return out[0], out[1:1 + n], out[1 + n:1 + 2 * n], out[1 + 2 * n:1 + 3 * n], out[1 + 3 * n:]


def _swiglu_block_fwd(h, norm_g, w_gu, w_down, tag, behind=()):
    n = rmsnorm_fwd(h, norm_g, f"{tag}_norm", behind=behind)
    a, b, s = gate_up_swiglu(n, w_gu, f"{tag}_gate_up")
    h_out = matmul(s, w_down, res=h, scale=0.5, name=f"{tag}_down")
    return h_out, (n, a, b, s)


def _swiglu_block_bwd(h, norm_g, w_gu, w_down, saved, dh_out, df, tag, exchange, behind=()):
    n, a, b, s = saved
    d_down = matmul(s, df, ta=True, scale=0.5, out_dtype=GRAD_COMM_DTYPE, name=f"{tag}_d_w_down")
    ds = matmul(df, w_down, tb=True, scale=0.5, out_dtype=ACT_DTYPE, behind=behind, name=f"{tag}_d_s")
    dgu = swiglu_bwd(a, b, ds, f"{tag}_swiglu_bwd")
    d_gu = matmul(n, dgu, ta=True, out_dtype=GRAD_COMM_DTYPE, name=f"{tag}_d_w_gate_up")
    tokens = exchange.gradients_ready(tag, {f"{tag}_w_gate_up": d_gu, f"{tag}_w_down": d_down})
    dn = matmul(dgu, w_gu, tb=True, behind=tokens, name=f"{tag}_d_n")
    dh, dh_m, dg = rmsnorm_bwd(h, norm_g, dn, dh_out, f"{tag}_norm_bwd")
    return dh, dh_m, dg


def local_step(x, target, small, exchange):
    big = {}
    token, big_ffn1 = exchange.weights("ffn1", x)
    big.update(big_ffn1)
    h1, saved1 = _swiglu_block_fwd(x, small["ffn1_norm"], big["ffn1_w_gate_up"], big["ffn1_w_down"], "ffn1", token)
    token, big_mix = exchange.weights("mix", h1)
    big.update(big_mix)
    u = rmsnorm_fwd(h1, small["mix_norm"], "mix_norm", behind=token)
    z = matmul(u, big["w_in"], name="w_in")
    p = small["hg_lower_bounds"]
    lb = 1.0 / (1.0 + jnp.exp(p[1:2] - p[0:1]))
    y_hg, o_raw, states = hgrn_fwd(z, lb, small["hg_out_norm"], "hgrn_fwd")
    o_att, l_att = zip(*[att_fwd(z, g, f"att_fwd_{g}") for g in range(N_GROUPS)])
    y_att = att_combine_fwd(o_att, l_att, "att_combine")
    bh = matmul(y_hg, big["w_branch_hg"], name="branch_hg")
    ba = matmul(y_att, big["w_branch_att"], name="branch_att")
    merged = merge_fwd(z, bh, ba, "merge")
    h2 = matmul(merged, big["w_out"], res=h1, name="w_out")
    token, big_ffn2 = exchange.weights("ffn2", h2)
    big.update(big_ffn2)
    h3, saved2 = _swiglu_block_fwd(h2, small["ffn2_norm"], big["ffn2_w_gate_up"], big["ffn2_w_down"], "ffn2", token)
    dh3, dh3_m, d_final, loss = final_norm_loss(h3, small["final_norm"], target, "final_norm_loss")

    gs, gb = {"final_norm": d_final}, {}
    dh2, dh2_m, gs["ffn2_norm"] = _swiglu_block_bwd(
        h2, small["ffn2_norm"], big["ffn2_w_gate_up"], big["ffn2_w_down"], saved2, dh3, dh3_m, "ffn2", exchange)
    token = exchange.backward_done("ffn2", dh2)
    gb["w_out"] = matmul(merged, dh2_m, ta=True, out_dtype=GRAD_COMM_DTYPE, name="d_w_out")
    dmerged = matmul(dh2_m, big["w_out"], tb=True, behind=token, name="d_merged")
    dbh, dba, dgh, dga = merge_bwd(z, bh, ba, dmerged, "merge_bwd")
    gb["w_branch_hg"] = matmul(y_hg, dbh, ta=True, out_dtype=GRAD_COMM_DTYPE, name="d_w_branch_hg")
    gb["w_branch_att"] = matmul(y_att, dba, ta=True, out_dtype=GRAD_COMM_DTYPE, name="d_w_branch_att")
    dy_hg = matmul(dbh, big["w_branch_hg"], tb=True, name="d_y_hg")
    dy_att = matmul(dba, big["w_branch_att"], tb=True, name="d_y_att")
    dq, dfp, di, dog, d_lb, gs["hg_out_norm"] = hgrn_bwd(z, lb, small["hg_out_norm"], o_raw, states, dy_hg, "hgrn_bwd")
    do_att, corr = att_combine_bwd(o_att, l_att, dy_att, "att_combine_bwd")
    d_att = [part for g in range(N_GROUPS) for part in att_bwd(z, l_att[g], do_att[g], corr[g], g, f"att_bwd_{g}")]
    dz = jnp.concatenate([dq, dfp, di, dog, *d_att, dgh, dga], axis=1)
    gb["w_in"] = matmul(u, dz, ta=True, out_dtype=GRAD_COMM_DTYPE, name="d_w_in")
    token = exchange.gradients_ready("mix", gb)
    du = matmul(dz, big["w_in"], tb=True, behind=token, name="d_u")
    dh1, dh1_m, gs["mix_norm"] = rmsnorm_bwd(h1, small["mix_norm"], du, dh2, "mix_norm_bwd")
    token = exchange.backward_done("mix", dh1)
    dp0 = d_lb * lb * (1.0 - lb)
    gs["hg_lower_bounds"] = jnp.concatenate([dp0, -dp0], axis=0)
    dx, _, gs["ffn1_norm"] = _swiglu_block_bwd(
        x, small["ffn1_norm"], big["ffn1_w_gate_up"], big["ffn1_w_down"], saved1, dh1, dh1_m, "ffn1", exchange, token)
    exchange.backward_done("ffn1", dx)
    return loss, dx, gs


WEIGHTS = ("ffn1_norm", "ffn1_w_gate_up", "ffn1_w_down", "mix_norm", "w_in", "hg_lower_bounds", "hg_out_norm",
           "w_branch_hg", "w_branch_att", "w_out", "ffn2_norm", "ffn2_w_gate_up", "ffn2_w_down", "final_norm")


class WeightExchange:
    ORDER = ("ffn1", "mix", "ffn2")

    def __init__(self, shards, core, chip):
        self.core, self.chip = core, chip
        self.halving = None
        self.scattering = None
        self.reducing = {}
        first = self.ORDER[0]
        self.placed = {BIG[i][0]: place_own_block(shards[BIG[i][0]], chip, *BIG[i][1:], f"place_{BIG[i][0]}")
                       for i in GROUPS[first]}
        self._start_gather(first, self.placed[self._names(first)[0]])
        chip_behind = chip + self.token[0, :1].astype(jnp.int32)
        for group in self.ORDER[1:]:
            for i in GROUPS[group]:
                n, r, cc, ax = BIG[i]
                self.placed[n] = place_own_block(shards[n], chip_behind, r, cc, ax, f"place_{n}")
        self.placed_behind = [self.placed[n] for group in self.ORDER[1:] for n in self._names(group)]

    def _names(self, group):
        return [BIG[i][0] for i in GROUPS[group]]

    def _start_gather(self, group, after):
        send_sems, recv_sems, bufs, self.token = gather_start([self.placed[n] for n in self._names(group)], after, group)
        self.gathering = (group, send_sems, recv_sems, bufs)

    def weights(self, group, h):
        pending, send_sems, recv_sems, bufs = self.gathering
        assert pending == group
        after = self.placed_behind if group == self.ORDER[0] else [h]
        whole = gather_forward(gather_wait(bufs, send_sems, recv_sems, after, group), group)
        later = self.ORDER.index(group) + 1
        behind = []
        if later < len(self.ORDER):
            self._start_gather(self.ORDER[later], whole[0])
            behind = [self.token]
        return behind, dict(zip(self._names(group), whole))

    @staticmethod
    def _half_to_sibling(ws):
        return lambda ref, w, c: _half(ref, *ws[w][1:], 1 - c)

    def gradients_ready(self, group, grads):
        ws = [BIG[i] for i in GROUPS[group]]
        send_sems, recv_sems, own, theirs, token = sibling_exchange_start(
            [grads[n] for n, *_ in ws], self._half_to_sibling(ws), [_half_shape(r, cc, ax) for _, r, cc, ax in ws],
            GRAD_COMM_DTYPE, f"halves_start_{group}")
        self.halving = (group, send_sems, recv_sems, own, theirs)
        return [token]

    def backward_done(self, group, dh):
        behind = [self._finish_scatter([dh])] if self.scattering is not None else []
        pending, send_sems, recv_sems, own, theirs = self.halving
        assert pending == group
        ws = [BIG[i] for i in GROUPS[group]]
        own, theirs = sibling_exchange_wait(own, theirs, send_sems, recv_sems, self._half_to_sibling(ws), [dh],
                                            f"halves_wait_{group}")
        halves = [add_halves(g, t, self.core, r, cc, ax, f"add_halves_{n}") for (n, r, cc, ax), g, t in zip(ws, own, theirs)]
        send_sems, recv_sems, halves, got, self.token = scatter_start(halves, group)
        self.scattering = (group, send_sems, recv_sems, halves, got)
        return behind + [self.token]

    def _finish_scatter(self, after):
        group, send_sems, recv_sems, halves, got = self.scattering
        halves, got = scatter_wait(halves, got, send_sems, recv_sems, after, group)
        ws = [BIG[i] for i in GROUPS[group]]
        mine = [add_pieces(h, g, self.chip, r, cc, ax, f"add_pieces_{n}") for (n, r, cc, ax), h, g in zip(ws, halves, got)]
        send_sems, recv_sems, mine, theirs, token = sibling_exchange_start(
            mine, lambda ref, w, c: ref, [_piece_shape(r, cc, ax) for _, r, cc, ax in ws], F32, f"reduced_start_{group}")
        self.reducing[group] = (send_sems, recv_sems, mine, theirs)
        self.scattering = None
        return token

    def finish(self, after):
        return self._finish_scatter(after)

    def reduced_halves(self, group, after):
        send_sems, recv_sems, mine, theirs = self.reducing.pop(group)
        mine, theirs = sibling_exchange_wait(mine, theirs, send_sems, recv_sems, lambda ref, w, c: ref, after,
                                             f"reduced_wait_{group}")
        return {BIG[i][0]: (a, b) for i, a, b in zip(GROUPS[group], mine, theirs)}


def kernel(x, ffn1_norm, ffn1_w_gate_up, ffn1_w_down, mix_norm, w_in, hg_lower_bounds, hg_out_norm, w_branch_hg, w_branch_att, w_out, ffn2_norm, ffn2_w_gate_up, ffn2_w_down, final_norm, loss_target, m_ffn1_norm, m_ffn1_w_gate_up, m_ffn1_w_down, m_mix_norm, m_w_in, m_hg_lower_bounds, m_hg_out_norm, m_w_branch_hg, m_w_branch_att, m_w_out, m_ffn2_norm, m_ffn2_w_gate_up, m_ffn2_w_down, m_final_norm, v_ffn1_norm, v_ffn1_w_gate_up, v_ffn1_w_down, v_mix_norm, v_w_in, v_hg_lower_bounds, v_hg_out_norm, v_w_branch_hg, v_w_branch_att, v_w_out, v_ffn2_norm, v_ffn2_w_gate_up, v_ffn2_w_down, v_final_norm):
    w = dict(ffn1_norm=ffn1_norm, ffn1_w_gate_up=ffn1_w_gate_up, ffn1_w_down=ffn1_w_down, mix_norm=mix_norm, w_in=w_in,
             hg_lower_bounds=hg_lower_bounds, hg_out_norm=hg_out_norm, w_branch_hg=w_branch_hg, w_branch_att=w_branch_att,
             w_out=w_out, ffn2_norm=ffn2_norm, ffn2_w_gate_up=ffn2_w_gate_up, ffn2_w_down=ffn2_w_down, final_norm=final_norm)
    m = dict(ffn1_norm=m_ffn1_norm, ffn1_w_gate_up=m_ffn1_w_gate_up, ffn1_w_down=m_ffn1_w_down, mix_norm=m_mix_norm,
             w_in=m_w_in, hg_lower_bounds=m_hg_lower_bounds, hg_out_norm=m_hg_out_norm, w_branch_hg=m_w_branch_hg,
             w_branch_att=m_w_branch_att, w_out=m_w_out, ffn2_norm=m_ffn2_norm, ffn2_w_gate_up=m_ffn2_w_gate_up,
             ffn2_w_down=m_ffn2_w_down, final_norm=m_final_norm)
    v = dict(ffn1_norm=v_ffn1_norm, ffn1_w_gate_up=v_ffn1_w_gate_up, ffn1_w_down=v_ffn1_w_down, mix_norm=v_mix_norm,
             w_in=v_w_in, hg_lower_bounds=v_hg_lower_bounds, hg_out_norm=v_hg_out_norm, w_branch_hg=v_w_branch_hg,
             w_branch_att=v_w_branch_att, w_out=v_w_out, ffn2_norm=v_ffn2_norm, ffn2_w_gate_up=v_ffn2_w_gate_up,
             ffn2_w_down=v_ffn2_w_down, final_norm=v_final_norm)

    core = lax.axis_index("c").astype(jnp.int32).reshape(1)
    chip = (2 * lax.axis_index("x") + lax.axis_index("y")).astype(jnp.int32).reshape(1)
    exchange = WeightExchange({n: w[n][0] for n, *_ in BIG}, core, chip)
    small = {n: w[n] for n in SMALL}
    small["final_norm"] = final_norm.reshape(1, D_MODEL)

    loss, dx, gs = local_step(x[0], loss_target[0], small, exchange)

    grads, delta, new_m, new_v = {}, {}, {}, {}

    def update(group, core, after):
        reduced = exchange.reduced_halves(group, after)
        for i in GROUPS[group]:
            n, r, cc, ax = BIG[i]
            a, b = reduced[n]
            g, d, nm, nv = adamw_halves(w[n][0], a, b, m[n][0], v[n][0], core, r, cc, ax, f"adamw_{n}")
            grads[n], delta[n], new_m[n], new_v[n] = g[None], d[None], nm[None], nv[None]

    core_behind = core + exchange.token[0, :1].astype(jnp.int32)
    update("ffn2", core_behind, [exchange.token])
    update("mix", core_behind, [delta["ffn2_w_down"]])
    token = exchange.finish(after=[delta[BIG[i][0]] for group in ("ffn2", "mix") for i in GROUPS[group]])
    two_d = lambda a: a.reshape(1, D_MODEL) if a.ndim == 1 else a
    loss_sum, *small_out = small_step(loss, [gs[n] for n in SMALL], *[[two_d(p[n]) for n in SMALL] for p in (w, m, v)],
                                      behind=token)
    for result, parts in zip((grads, delta, new_m, new_v), small_out):
        result.update({n: a.reshape(w[n].shape) for n, a in zip(SMALL, parts)})
    update("ffn1", core, [loss_sum])

    return (loss_sum[0, 0], dx[None], *[grads[n] for n in WEIGHTS], *[delta[n] for n in WEIGHTS],
            *[new_m[n] for n in WEIGHTS], *[new_v[n] for n in WEIGHTS])
```

```python
import numpy as np
import jax
import jax.numpy as jnp
from jax import lax
from jax.experimental import pallas as pl
from jax.experimental.pallas import tpu as pltpu

SEQ = 2048
D_MODEL = 1024
D_FF = 2816
HG_HEADS = 4
HG_DIM = 128
HG_WIDTH = 512
HG_CHUNK = 64
ATT_GROUPS = ((128, 1), (512, 4), (2048, 16))
ATT_HEADS = 8
ATT_WIDTH = 512
ATT_BLOCK = 128
ALIBI_MAX = 8.0
IN_COLS = 8704
EPS = 1e-6
NEG_INF = -1e30
ADAM_LR = 0.001
ADAM_B1 = 0.9
ADAM_B2 = 0.999
ADAM_EPS = 1e-08
ADAM_WD = 0.01
ADAM_STEP = 10

N_CHIPS = 4
MXU_DTYPE = jnp.bfloat16
WEIGHT_COMM_DTYPE = jnp.bfloat16
GRAD_COMM_DTYPE = jnp.bfloat16
ACT_DTYPE = jnp.bfloat16
MESH = pl.DeviceIdType.MESH
F32 = jnp.float32


def _sigmoid(x):
    return 1.0 / (1.0 + jnp.exp(-x))


def _dot(a, b, ta=False, tb=False):
    dn = (((0 if ta else 1,), (1 if tb else 0,)), ((), ()))
    return lax.dot_general(a.astype(MXU_DTYPE), b.astype(MXU_DTYPE), dn, preferred_element_type=F32)


def _dot_f32(a, b, ones_on_right=False):
    x = a if ones_on_right else b
    hi = x.astype(jnp.bfloat16)
    rest = x - hi.astype(F32)
    mid = rest.astype(jnp.bfloat16)
    lo = (rest - mid.astype(F32)).astype(jnp.bfloat16)
    if ones_on_right:
        dot = lambda q: jnp.dot(q, b.astype(jnp.bfloat16), preferred_element_type=F32)
    else:
        dot = lambda q: jnp.dot(a.astype(jnp.bfloat16), q, preferred_element_type=F32)
    return dot(hi) + (dot(mid) + dot(lo))


def _split_bf16(x):
    hi = x.astype(jnp.bfloat16)
    return hi, (x - hi.astype(F32)).astype(jnp.bfloat16)


def _hdot(a, b, ta=False, tb=False):
    dn =(((0 if ta else 1,), (1 if tb else 0,)), ((), ()))
    (a_hi, a_lo), (b_hi, b_lo) = _split_bf16(a), _split_bf16(b)
    dot = lambda p, q: lax.dot_general(p, q, dn, preferred_element_type=F32)
    return dot(a_hi, b_hi) + (dot(a_lo, b_hi) + dot(a_hi, b_lo))


MATMUL_VMEM_BYTES = 48 * 1024 * 1024
MATMUL_TILE_BYTES = 36 * 1024 * 1024
MXU_ALIGN = 128


def _divisors(n, most):
    return [t for t in range(min(n, most), 0, -MXU_ALIGN) if n % t == 0 and t % MXU_ALIGN == 0]


def _matmul_tiles(M, N, K, in_bytes, out_bytes, has_res):
    best = None
    for tk in _divisors(K, K):
        nk = K // tk
        for tm in _divisors(M, 2048):
            for tn in _divisors(N, 512):
                tiles = 2 * in_bytes * (tm * tk + tk * tn) + 2 * out_bytes * tm * tn
                tiles += 4 * tm * tn * ((nk > 1) + 2 * has_res)
                if tiles > MATMUL_TILE_BYTES:
                    continue
                traffic = in_bytes * (M * K * (1 if nk == 1 else N // tn) + K * N * (M // tm))
                key = (traffic, -tm * tn * tk)
                if best is None or key < best[0]:
                    best = (key, (tm, tn, tk))
    return best[1]


def matmul(a, b, *, ta=False, tb=False, out_dtype=F32, res=None, scale=1.0, behind=(), name):
    if ta:
        K, M = a.shape
    else:
        M, K = a.shape
    if tb:
        N, K2 = b.shape
    else:
        K2, N = b.shape
    assert K == K2 and a.dtype == b.dtype
    tm, tn, tk = _matmul_tiles(M, N, K, a.dtype.itemsize, jnp.dtype(out_dtype).itemsize, res is not None)
    nk = K // tk

    def finish(r, r_ref, o_ref):
        if scale != 1.0:
            r = r * scale
        if res is not None:
            r = r_ref[...] + r
        o_ref[...] = r.astype(out_dtype)

    def body(*refs):
        a_ref, b_ref = refs[:2]
        r_ref = refs[2] if res is not None else None
        o_ref = refs[2 + (res is not None) + len(behind)]
        if nk == 1:
            finish(_dot(a_ref[...], b_ref[...], ta, tb), r_ref, o_ref)
            return
        acc = refs[-1]
        k = pl.program_id(2)

        @pl.when(k == 0)
        def _():
            acc[...] = jnp.zeros_like(acc)

        acc[...] += _dot(a_ref[...], b_ref[...], ta, tb)

        @pl.when(k == nk - 1)
        def _():
            finish(acc[...], r_ref, o_ref)

    a_spec = pl.BlockSpec((tk, tm), lambda i, j, k: (k, i)) if ta else pl.BlockSpec((tm, tk), lambda i, j, k: (i, k))
    b_spec = pl.BlockSpec((tn, tk), lambda i, j, k: (j, k)) if tb else pl.BlockSpec((tk, tn), lambda i, j, k: (k, j))
    in_specs = [a_spec, b_spec]
    args = [a, b]
    if res is not None:
        in_specs.append(pl.BlockSpec((tm, tn), lambda i, j, k: (i, j)))
        args.append(res)
    for earlier in behind:
        in_specs.append(pl.BlockSpec(memory_space=pl.ANY))
        args.append(earlier)
    return pl.pallas_call(
        body, name=name, grid=(M // tm, N // tn, nk), in_specs=in_specs,
        out_specs=pl.BlockSpec((tm, tn), lambda i, j, k: (i, j)),
        out_shape=jax.ShapeDtypeStruct((M, N), out_dtype),
        scratch_shapes=[pltpu.VMEM((tm, tn), F32)] if nk > 1 else [],
        compiler_params=pltpu.CompilerParams(dimension_semantics=("parallel", "parallel", "arbitrary"),
                                             vmem_limit_bytes=MATMUL_VMEM_BYTES),
    )(*args)


ROW_TILE = 256


def rmsnorm_fwd(x, g, name, behind=()):
    def body(x_ref, g_ref, *refs):
        n_ref = refs[-1]
        xv = x_ref[...]
        r = lax.rsqrt(jnp.mean(xv * xv, axis=-1, keepdims=True) + EPS)
        n_ref[...] = ((xv * r) * g_ref[...]).astype(n_ref.dtype)

    order = list(behind)
    return pl.pallas_call(
        body, name=name, grid=(SEQ // ROW_TILE,),
        in_specs=[pl.BlockSpec((ROW_TILE, D_MODEL), lambda i: (i, 0)), pl.BlockSpec((1, D_MODEL), lambda i: (0, 0))]
        + [pl.BlockSpec(memory_space=pl.ANY)] * len(order),
        out_specs=pl.BlockSpec((ROW_TILE, D_MODEL), lambda i: (i, 0)),
        out_shape=jax.ShapeDtypeStruct((SEQ, D_MODEL), MXU_DTYPE),
    )(x, g, *order)


def rmsnorm_bwd(x, g, dn, dres, name):
    def body(x_ref, g_ref, dn_ref, dr_ref, dx_ref, dxm_ref, dg_ref):
        xv = x_ref[...]
        r = lax.rsqrt(jnp.mean(xv * xv, axis=-1, keepdims=True) + EPS)
        xh = xv * r
        dnv = dn_ref[...]

        @pl.when(pl.program_id(0) == 0)
        def _():
            dg_ref[...] = jnp.zeros_like(dg_ref)

        dg_ref[...] += jnp.sum(dnv * xh, axis=0, keepdims=True)
        dxh = dnv * g_ref[...]
        dx = dr_ref[...] + r * (dxh - xh * jnp.mean(dxh * xh, axis=-1, keepdims=True))
        dx_ref[...] = dx
        dxm_ref[...] = dx.astype(dxm_ref.dtype)

    row = pl.BlockSpec((ROW_TILE, D_MODEL), lambda i: (i, 0))
    vec = pl.BlockSpec((1, D_MODEL), lambda i: (0, 0))
    return pl.pallas_call(
        body, name=name, grid=(SEQ // ROW_TILE,), in_specs=[row, vec, row, row], out_specs=[row, row, vec],
        out_shape=[jax.ShapeDtypeStruct((SEQ, D_MODEL), F32), jax.ShapeDtypeStruct((SEQ, D_MODEL), MXU_DTYPE),
                   jax.ShapeDtypeStruct((1, D_MODEL), F32)],
        compiler_params=pltpu.CompilerParams(dimension_semantics=("arbitrary",)),
    )(x, g, dn, dres)


def final_norm_loss(h, g, target, name):
    def body(h_ref, g_ref, t_ref, dh_ref, dhm_ref, dg_ref, loss_ref):
        xv = h_ref[...]
        r = lax.rsqrt(jnp.mean(xv * xv, axis=-1, keepdims=True) + EPS)
        xh = xv * r
        gv = g_ref[...]
        e = xh * gv - t_ref[...]

        @pl.when(pl.program_id(0) == 0)
        def _():
            dg_ref[...] = jnp.zeros_like(dg_ref)
            loss_ref[...] = jnp.zeros_like(loss_ref)

        part = 0.5 * jnp.sum(jnp.sum(e * e, axis=-1, keepdims=True) * (1.0 / D_MODEL), axis=0, keepdims=True)
        loss_ref[...] += jnp.broadcast_to(part, loss_ref.shape)
        dout = e * (1.0 / D_MODEL)
        dg_ref[...] += jnp.sum(dout * xh, axis=0, keepdims=True)
        dxh = dout * gv
        dh = r * (dxh - xh * jnp.mean(dxh * xh, axis=-1, keepdims=True))
        dh_ref[...] = dh
        dhm_ref[...] = dh.astype(dhm_ref.dtype)

    row = pl.BlockSpec((ROW_TILE, D_MODEL), lambda i: (i, 0))
    vec = pl.BlockSpec((1, D_MODEL), lambda i: (0, 0))
    return pl.pallas_call(
        body, name=name, grid=(SEQ // ROW_TILE,), in_specs=[row, vec, row],
        out_specs=[row, row, vec, pl.BlockSpec((8, 128), lambda i: (0, 0))],
        out_shape=[jax.ShapeDtypeStruct((SEQ, D_MODEL), F32), jax.ShapeDtypeStruct((SEQ, D_MODEL), MXU_DTYPE),
                   jax.ShapeDtypeStruct((1, D_MODEL), F32), jax.ShapeDtypeStruct((8, 128), F32)],
        compiler_params=pltpu.CompilerParams(dimension_semantics=("arbitrary",)),
    )(h, g, target)


FFN_TILE = 256
FFN_TILES = D_FF // FFN_TILE


def gate_up_swiglu(n, w_gu, name):
    def body(n_ref, wa_ref, wb_ref, a_ref, b_ref, s_ref):
        nv = n_ref[...]
        a = _dot(nv, wa_ref[...])
        b = _dot(nv, wb_ref[...])
        a_ref[...] = a.astype(a_ref.dtype)
        b_ref[...] = b.astype(b_ref.dtype)
        s_ref[...] = (a * _sigmoid(a) * b).astype(s_ref.dtype)

    tile = pl.BlockSpec((SEQ, FFN_TILE), lambda j: (0, j))
    act = jax.ShapeDtypeStruct((SEQ, D_FF), ACT_DTYPE)
    return pl.pallas_call(
        body, name=name, grid=(FFN_TILES,),
        in_specs=[pl.BlockSpec((SEQ, D_MODEL), lambda j: (0, 0)), pl.BlockSpec((D_MODEL, FFN_TILE), lambda j: (0, j)),
                  pl.BlockSpec((D_MODEL, FFN_TILE), lambda j: (0, j + FFN_TILES))],
        out_specs=[tile, tile, tile], out_shape=[act, act, jax.ShapeDtypeStruct((SEQ, D_FF), MXU_DTYPE)],
        compiler_params=pltpu.CompilerParams(dimension_semantics=("parallel",), vmem_limit_bytes=MATMUL_VMEM_BYTES),
    )(n, w_gu, w_gu)


def swiglu_bwd(a, b, ds, name):
    rows = ROW_TILE // 2

    def body(a_ref, b_ref, ds_ref, o_ref):
        av = a_ref[...].astype(F32)
        sg = _sigmoid(av)
        dsv = ds_ref[...].astype(F32)
        o_ref[:, :D_FF] = (dsv * b_ref[...].astype(F32) * (sg * (1.0 + av * (1.0 - sg)))).astype(o_ref.dtype)
        o_ref[:, D_FF:] = (dsv * av * sg).astype(o_ref.dtype)

    blk = pl.BlockSpec((rows, D_FF), lambda i: (i, 0))
    return pl.pallas_call(
        body, name=name, grid=(SEQ // rows,), in_specs=[blk, blk, blk],
        out_specs=pl.BlockSpec((rows, 2 * D_FF), lambda i: (i, 0)),
        out_shape=jax.ShapeDtypeStruct((SEQ, 2 * D_FF), MXU_DTYPE), compiler_params=SUM_PARAMS,
    )(a, b, ds)


GATE_HG_BLK = 6656 // 512
GATE_ATT_BLK = 7680 // 512


def merge_fwd(z, bh, ba, name):
    def body(gh_ref, ga_ref, bh_ref, ba_ref, o_ref):
        o_ref[...] = (_sigmoid(gh_ref[...]) * bh_ref[...] + _sigmoid(ga_ref[...]) * ba_ref[...]).astype(o_ref.dtype)

    blk = pl.BlockSpec((ROW_TILE, 512), lambda i, j: (i, j))
    return pl.pallas_call(
        body, name=name, grid=(SEQ // ROW_TILE, 2),
        in_specs=[pl.BlockSpec((ROW_TILE, 512), lambda i, j: (i, GATE_HG_BLK + j)),
                  pl.BlockSpec((ROW_TILE, 512), lambda i, j: (i, GATE_ATT_BLK + j)), blk, blk],
        out_specs=blk, out_shape=jax.ShapeDtypeStruct((SEQ, D_MODEL), MXU_DTYPE),
    )(z, z, bh, ba)


def merge_bwd(z, bh, ba, dm, name):
    def body(gh_ref, ga_ref, bh_ref, ba_ref, dm_ref, dbh_ref, dba_ref, dgh_ref, dga_ref):
        dmv = dm_ref[...]
        sh = _sigmoid(gh_ref[...])
        sa = _sigmoid(ga_ref[...])
        dbh_ref[...] = (dmv * sh).astype(dbh_ref.dtype)
        dba_ref[...] = (dmv * sa).astype(dba_ref.dtype)
        dgh_ref[...] = (dmv * bh_ref[...] * (sh * (1.0 - sh))).astype(dgh_ref.dtype)
        dga_ref[...] = (dmv * ba_ref[...] * (sa * (1.0 - sa))).astype(dga_ref.dtype)

    blk = pl.BlockSpec((ROW_TILE, 512), lambda i, j: (i, j))
    out = jax.ShapeDtypeStruct((SEQ, D_MODEL), MXU_DTYPE)
    return pl.pallas_call(
        body, name=name, grid=(SEQ // ROW_TILE, 2),
        in_specs=[pl.BlockSpec((ROW_TILE, 512), lambda i, j: (i, GATE_HG_BLK + j)),
                  pl.BlockSpec((ROW_TILE, 512), lambda i, j: (i, GATE_ATT_BLK + j)), blk, blk, blk],
        out_specs=[blk, blk, blk, blk], out_shape=[out, out, out, out],
    )(z, z, bh, ba, dm)


N_CHUNKS = SEQ // HG_CHUNK
HG_STEP_CHUNKS = 4


def _hgrn_gates(q, fp, lb):
    C = HG_CHUNK
    sg = _sigmoid(fp)
    f = lb + (1.0 - lb) * sg
    lf = jnp.log(f)
    row = lax.broadcasted_iota(jnp.int32, (C, C), 0)
    col = lax.broadcasted_iota(jnp.int32, (C, C), 1)
    causal = row >= col
    G = _dot_f32(causal.astype(F32), lf)
    eG = jnp.exp(G)
    enG = jnp.exp(-G)
    qg = q * eG
    kg = (1.0 - f) * enG
    A = jnp.where(causal, _hdot(qg, kg, tb=True), 0.0)
    egl = jnp.exp(jnp.sum(lf, axis=0, keepdims=True))
    return sg, f, causal, eG, enG, qg, kg, A, egl


def hgrn_fwd(z, lb, gain, name):
    C, K = HG_CHUNK, HG_DIM

    def body(q_ref, f_ref, v_ref, og_ref, p_ref, g_ref, y_ref, o_ref, st_ref, state):
        @pl.when(pl.program_id(0) == 0)
        def _():
            state[...] = jnp.zeros_like(state)

        for cc in range(HG_STEP_CHUNKS):
            rows = pl.ds(cc * C, C)
            for h in range(HG_HEADS):
                hd = pl.ds(h * K, K)
                v = v_ref[rows, hd]
                _, _, _, _, _, qg, kg, A, egl = _hgrn_gates(q_ref[rows, hd], f_ref[rows, hd], p_ref[:, hd])
                st = state[h]
                st_ref[h, cc] = st
                o = _hdot(A, v) + _hdot(qg, st, tb=True)
                state[h] = st * egl + _hdot(v, kg * egl, ta=True)
                o_ref[rows, hd] = o
                rs = lax.rsqrt(jnp.mean(o * o, axis=-1, keepdims=True) + EPS)
                og = og_ref[rows, hd]
                y_ref[rows, hd] = (((o * rs) * g_ref[:, hd]) * (og * _sigmoid(og))).astype(y_ref.dtype)

    R = HG_STEP_CHUNKS * C

    def zcol(section):
        return pl.BlockSpec((R, HG_WIDTH), lambda c: (c, section))

    vec = pl.BlockSpec((1, HG_WIDTH), lambda c: (0, 0))
    blk = pl.BlockSpec((R, HG_WIDTH), lambda c: (c, 0))
    return pl.pallas_call(
        body, name=name, grid=(N_CHUNKS // HG_STEP_CHUNKS,),
        in_specs=[zcol(0), zcol(1), zcol(2), zcol(3), vec, vec],
        out_specs=[blk, blk, pl.BlockSpec((HG_HEADS, HG_STEP_CHUNKS, K, K), lambda c: (0, c, 0, 0))],
        out_shape=[jax.ShapeDtypeStruct((SEQ, HG_WIDTH), MXU_DTYPE), jax.ShapeDtypeStruct((SEQ, HG_WIDTH), F32),
                   jax.ShapeDtypeStruct((HG_HEADS, N_CHUNKS, K, K), F32)],
        scratch_shapes=[pltpu.VMEM((HG_HEADS, K, K), F32)],
        compiler_params=pltpu.CompilerParams(dimension_semantics=("arbitrary",)),
    )(z, z, z, z, lb, gain)


def hgrn_bwd(z, lb, gain, o_raw, states, dy, name):
    C, K = HG_CHUNK, HG_DIM

    def body(q_ref, f_ref, v_ref, og_ref, p_ref, g_ref, o_ref, st_ref, dy_ref,
             dq_ref, dfp_ref, dv_ref, dog_ref, dlb_ref, dgain_ref, dstate):
        @pl.when(pl.program_id(0) == 0)
        def _():
            dstate[...] = jnp.zeros_like(dstate)
            dlb_ref[...] = jnp.zeros_like(dlb_ref)
            dgain_ref[...] = jnp.zeros_like(dgain_ref)

        last = lax.broadcasted_iota(jnp.int32, (C, K), 0) == C - 1
        row = lax.broadcasted_iota(jnp.int32, (C, C), 0)
        col = lax.broadcasted_iota(jnp.int32, (C, C), 1)
        anti_causal = (col >= row).astype(F32)
        for cc in reversed(range(HG_STEP_CHUNKS)):
            rows = pl.ds(cc * C, C)
            for h in range(HG_HEADS):
                hd = pl.ds(h * K, K)
                v = v_ref[rows, hd]
                lb = p_ref[:, hd]
                sg, f, causal, eG, enG, qg, kg, A, egl = _hgrn_gates(q_ref[rows, hd], f_ref[rows, hd], lb)
                kd = kg * egl
                st = st_ref[h, cc]
                dst = dstate[h]
                o = o_ref[rows, hd]
                og = og_ref[rows, hd]
                gain_v = g_ref[:, hd]
                dyv = dy_ref[rows, hd]
                rs = lax.rsqrt(jnp.mean(o * o, axis=-1, keepdims=True) + EPS)
                on = o * rs
                sgo = _sigmoid(og)
                silu = og * sgo
                dog_ref[rows, hd] = (dyv * (on * gain_v) * (sgo * (1.0 + og * (1.0 - sgo)))).astype(dog_ref.dtype)
                dgain_ref[:, hd] += jnp.sum(dyv * silu * on, axis=0, keepdims=True)
                don = dyv * gain_v * silu
                do = rs * (don - on * jnp.mean(don * on, axis=-1, keepdims=True))
                dA = jnp.where(causal, _hdot(do, v, tb=True), 0.0)
                dv_ref[rows, hd] = (_hdot(A, do, ta=True) + _hdot(kd, dst, tb=True)).astype(dv_ref.dtype)
                dqg = _hdot(dA, kg) + _hdot(do, st)
                dkg = _hdot(dA, qg, ta=True)
                dkd = _hdot(v, dst)
                dstate[h] = dst * egl + _hdot(do, qg, ta=True)
                dgl = jnp.sum(st * dst, axis=0, keepdims=True) * egl
                dq_ref[rows, hd] = (dqg * eG).astype(dq_ref.dtype)
                dk = dkg * enG + dkd * (enG * egl)
                dG = dqg * qg - dkg * kg - dkd * kd
                extra = jnp.sum(dkd * kd, axis=0, keepdims=True) + dgl
                dG = dG + jnp.where(last, extra, 0.0)
                dlf = _dot_f32(anti_causal, dG)
                df = dlf / f - dk
                dfp_ref[rows, hd] = (df * (1.0 - lb) * (sg * (1.0 - sg))).astype(dfp_ref.dtype)
                dlb_ref[:, hd] += jnp.sum(df * (1.0 - sg), axis=0, keepdims=True)

    R = HG_STEP_CHUNKS * C
    n_steps = N_CHUNKS // HG_STEP_CHUNKS

    def rc(c):
        return n_steps - 1 - c

    def zcol(section):
        return pl.BlockSpec((R, HG_WIDTH), lambda c: (rc(c), section))

    vec = pl.BlockSpec((1, HG_WIDTH), lambda c: (0, 0))
    blk = pl.BlockSpec((R, HG_WIDTH), lambda c: (rc(c), 0))
    out = jax.ShapeDtypeStruct((SEQ, HG_WIDTH), MXU_DTYPE)
    small = jax.ShapeDtypeStruct((1, HG_WIDTH), F32)
    return pl.pallas_call(
        body, name=name, grid=(n_steps,),
        in_specs=[zcol(0), zcol(1), zcol(2), zcol(3), vec, vec, blk,
                  pl.BlockSpec((HG_HEADS, HG_STEP_CHUNKS, K, K), lambda c: (0, rc(c), 0, 0)), blk],
        out_specs=[blk, blk, blk, blk, vec, vec],
        out_shape=[out, out, out, out, small, small],
        scratch_shapes=[pltpu.VMEM((HG_HEADS, K, K), F32)],
        compiler_params=pltpu.CompilerParams(dimension_semantics=("arbitrary",)),
    )(z, z, z, z, lb, gain, o_raw, states, dy)


N_GROUPS = len(ATT_GROUPS)
HEAD_PAIRS = ATT_WIDTH // 128
ATT_COL0 = 4 * HG_WIDTH
UNROLLED_UNITS = 4


def _alibi_coef():
    n = N_GROUPS * ATT_HEADS
    slopes = np.exp2(-ALIBI_MAX * np.arange(1, n + 1, dtype=np.float32) / n).astype(np.float32)
    dil = np.repeat(np.array([d for _, d in ATT_GROUPS], np.float32), ATT_HEADS)
    return jnp.asarray(slopes * dil, F32)


def _for_each_unit(n, fn):
    if n <= UNROLLED_UNITS:
        for u in range(n):
            fn(u)
    else:
        def group(i, carry):
            for j in range(UNROLLED_UNITS):
                fn(i * UNROLLED_UNITS + j)
            return carry
        lax.fori_loop(0, n // UNROLLED_UNITS, group, 0)


def _att_geometry(g):
    B = ATT_BLOCK
    d = ATT_GROUPS[g][1]
    n_blocks = SEQ // (d * B)
    col0 = (ATT_COL0 + g * 3 * ATT_WIDTH) // 128

    def block_rows(b, r):
        return pl.ds(b * (B * d) + r, B, stride=d) if d > 1 else pl.ds(pl.multiple_of(b * B, B), B)

    def block_of(u):
        return (u, 0) if d == 1 else (u // d, u % d)

    return d, n_blocks, col0, block_rows, block_of


def _att_column(c):
    return pl.BlockSpec((SEQ, 128), lambda hp: (0, c + hp))


def _head_lanes(j):
    lane = lax.broadcasted_iota(jnp.int32, (ATT_BLOCK, 128), 1)
    return (lane >= 64 * j) & (lane < 64 * (j + 1))


def _stack_heads(x, sel0):
    return jnp.concatenate([jnp.where(sel0, x, 0.0), jnp.where(sel0, 0.0, x)], axis=0)


def _stack_values(x, sel0, lanes):
    swapped = pltpu.roll(x, 64, 1)
    stacked = jnp.concatenate([jnp.where(sel0, x, swapped), jnp.where(sel0, swapped, x)], axis=0)
    return stacked if lanes == 128 else jnp.concatenate([stacked] * (lanes // 128), axis=1)


def _pair_coef(coef_ref, g, hp):
    row = lax.broadcasted_iota(jnp.int32, (2 * ATT_BLOCK, 1), 0)
    first = g * ATT_HEADS + hp * 2
    return jnp.where(row < ATT_BLOCK, coef_ref[first], coef_ref[first + 1])


def _band(with_prev, first_key):
    B = ATT_BLOCK
    keys = 2 * B if with_prev else B
    qi = jnp.bitwise_and(lax.broadcasted_iota(jnp.int32, (2 * B, keys), 0), B - 1)
    kj = lax.broadcasted_iota(jnp.int32, (2 * B, keys), 1)
    delta = qi + (B if with_prev else 0) - kj
    valid = (delta >= 0) & (delta <= B)
    if with_prev:
        valid = valid & (kj >= first_key)
    return valid, delta.astype(F32)


def att_fwd(z, g, name):
    B = ATT_BLOCK
    d, n_blocks, col0, block_rows, block_of = _att_geometry(g)
    multi = n_blocks > 1

    def body(coef_ref, q_ref, k_ref, v_ref, o_ref, l_ref):
        cf2 = _pair_coef(coef_ref, g, pl.program_id(0))
        sel0 = _head_lanes(0)

        def one(u):
            b, r = block_of(u)
            rows = block_rows(b, r)
            valid, dist = _band(multi, jnp.where(b == 0, B, 0))
            q2 = _stack_heads(q_ref[rows, :], sel0)
            kk, vv = k_ref[rows, :], v_ref[rows, :]
            if multi:
                prev_rows = block_rows(jnp.maximum(b - 1, 0), r)
                kk = jnp.concatenate([k_ref[prev_rows, :], kk], axis=0)
                vv = jnp.concatenate([v_ref[prev_rows, :], vv], axis=0)
            sc = jnp.where(valid, _dot(q2, kk, tb=True) * 0.125 - cf2 * dist, NEG_INF)
            mx = jnp.max(sc, axis=-1, keepdims=True)
            e = jnp.exp(sc - mx)
            den = jnp.sum(e, axis=-1, keepdims=True)
            o2 = _dot(e * (1.0 / den), vv)
            lse2 = mx + jnp.log(den)
            o_ref[rows, :] = jnp.where(sel0, o2[:B], o2[B:])
            l_ref[rows, :] = jnp.where(sel0, lse2[:B], lse2[B:])

        _for_each_unit(d * n_blocks, one)

    out = jax.ShapeDtypeStruct((SEQ, ATT_WIDTH), F32)
    return pl.pallas_call(
        body, name=name, grid=(HEAD_PAIRS,),
        in_specs=[pl.BlockSpec(memory_space=pltpu.SMEM), _att_column(col0), _att_column(col0 + 4), _att_column(col0 + 8)],
        out_specs=[_att_column(0), _att_column(0)], out_shape=[out, out],
        compiler_params=pltpu.CompilerParams(dimension_semantics=("parallel",)),
    )(_alibi_coef(), z, z, z)


def att_bwd(z, l, do, corr, g, name):
    B = ATT_BLOCK
    d, n_blocks, col0, block_rows, block_of = _att_geometry(g)
    multi = n_blocks > 1
    own = slice(B, 2 * B) if multi else slice(0, B)

    def body(coef_ref, q_ref, k_ref, v_ref, l_ref, do_ref, cr_ref, dq_ref, dk_ref, dv_ref, dq_sc, dk_sc, dv_sc):
        cf2 = _pair_coef(coef_ref, g, pl.program_id(0))
        sel0 = _head_lanes(0)

        def one(u):
            b, r = block_of(u)
            rows = block_rows(b, r)
            valid, dist = _band(multi, jnp.where(b == 0, B, 0))
            kk, vv = k_ref[rows, :], v_ref[rows, :]
            if multi:
                prev_rows = block_rows(jnp.maximum(b - 1, 0), r)
                kk = jnp.concatenate([k_ref[prev_rows, :], kk], axis=0)
                vv = jnp.concatenate([v_ref[prev_rows, :], vv], axis=0)
            q2, do2 = _stack_heads(q_ref[rows, :], sel0), _stack_heads(do_ref[rows, :], sel0)
            keys = kk.shape[0]
            lse2, cr2 = _stack_values(l_ref[rows, :], sel0, keys), _stack_values(cr_ref[rows, :], sel0, keys)
            p = jnp.exp(jnp.where(valid, _dot(q2, kk, tb=True) * 0.125 - cf2 * dist, NEG_INF) - lse2)
            ds = p * (_dot(do2, vv, tb=True) + cr2)
            dq2 = _dot(ds, kk)
            dkk = _dot(ds, q2, ta=True) * 0.125
            dvv = _dot(p, do2, ta=True)
            dq_sc[rows, :] = jnp.where(sel0, dq2[:B], dq2[B:]) * 0.125
            dk_sc[rows, :] = dkk[own]
            dv_sc[rows, :] = dvv[own]
            if multi:
                dk_sc[prev_rows, :] += dkk[:B]
                dv_sc[prev_rows, :] += dvv[:B]

        _for_each_unit(d * n_blocks, one)
        dq_ref[...] = dq_sc[...].astype(dq_ref.dtype)
        dk_ref[...] = dk_sc[...].astype(dk_ref.dtype)
        dv_ref[...] = dv_sc[...].astype(dv_ref.dtype)

    col = _att_column
    out = jax.ShapeDtypeStruct((SEQ, ATT_WIDTH), MXU_DTYPE)
    return pl.pallas_call(
        body, name=name, grid=(HEAD_PAIRS,),
        in_specs=[pl.BlockSpec(memory_space=pltpu.SMEM), col(col0), col(col0 + 4), col(col0 + 8), col(0), col(0), col(0)],
        out_specs=[col(0)] * 3, out_shape=[out] * 3,
        scratch_shapes=[pltpu.VMEM((SEQ, 128), F32)] * 3,
        compiler_params=pltpu.CompilerParams(dimension_semantics=("parallel",), vmem_limit_bytes=MATMUL_VMEM_BYTES),
    )(_alibi_coef(), z, z, z, l, do, corr)


def _head_sum(x):
    i = lax.broadcasted_iota(jnp.int32, (128, 128), 0) // 64
    j = lax.broadcasted_iota(jnp.int32, (128, 128), 1) // 64
    return _dot_f32(x, (i == j).astype(F32), ones_on_right=True)


def _group_weights(l0, l1, l2):
    mx = jnp.maximum(jnp.maximum(l0, l1), l2)
    e0, e1, e2 = jnp.exp(l0 - mx), jnp.exp(l1 - mx), jnp.exp(l2 - mx)
    inv = 1.0 / (e0 + e1 + e2)
    return e0 * inv, e1 * inv, e2 * inv


def att_combine_fwd(o, l, name):
    def body(o0, o1, o2, l0, l1, l2, y_ref):
        w0, w1, w2 = _group_weights(l0[...], l1[...], l2[...])
        y_ref[...] = (o0[...] * w0 + o1[...] * w1 + o2[...] * w2).astype(y_ref.dtype)

    blk = pl.BlockSpec((ROW_TILE, ATT_WIDTH), lambda i: (i, 0))
    return pl.pallas_call(
        body, name=name, grid=(SEQ // ROW_TILE,), in_specs=[blk] * 6, out_specs=blk,
        out_shape=jax.ShapeDtypeStruct((SEQ, ATT_WIDTH), MXU_DTYPE),
    )(*o, *l)


def att_combine_bwd(o, l, dy, name):
    def body(o0, o1, o2, l0, l1, l2, dy_ref, do0, do1, do2, cr0, cr1, cr2):
        w = _group_weights(l0[...], l1[...], l2[...])
        dyv = dy_ref[...]
        tot = _head_sum(dyv * (w[0] * o0[...] + w[1] * o1[...] + w[2] * o2[...]))
        for g, (do_ref, cr_ref) in enumerate(((do0, cr0), (do1, cr1), (do2, cr2))):
            do_ref[...] = dyv * w[g]
            cr_ref[...] = -w[g] * tot

    blk = pl.BlockSpec((ROW_TILE, 128), lambda i, j: (i, j))
    out = jax.ShapeDtypeStruct((SEQ, ATT_WIDTH), F32)
    res = pl.pallas_call(
        body, name=name, grid=(SEQ // ROW_TILE, HEAD_PAIRS), in_specs=[blk] * 7, out_specs=[blk] * 6, out_shape=[out] * 6,
    )(*o, *l, dy)
    return res[:N_GROUPS], res[N_GROUPS:]


SUM_ROW_TILES = (1024, 512, 256, 128, 64, 32, 16)
SUM_TILE_BYTES = 24 * 1024 * 1024
SUM_PARAMS = pltpu.CompilerParams(vmem_limit_bytes=MATMUL_VMEM_BYTES)


def _row_tile(rows, cols, operands):
    fit = [t for t in SUM_ROW_TILES if rows % t == 0]
    return next((t for t in fit if 2 * 4 * operands * t * cols <= SUM_TILE_BYTES), fit[-1])


def _shard_shape(rows, cols, axis):
    return (rows // N_CHIPS, cols) if axis == 0 else (rows, cols // N_CHIPS)


def _half_shape(rows, cols, axis):
    return (rows, cols // 2) if axis == 0 else (rows // 2, cols)


def _piece_shape(rows, cols, axis):
    return (rows // N_CHIPS, cols // 2) if axis == 0 else (rows // 2, cols // N_CHIPS)


def place_own_block(shard, chip, rows, cols, axis, name):
    sr, sc = _shard_shape(rows, cols, axis)
    tr = _row_tile(sr, sc, 2)

    def body(chip_ref, s_ref, o_ref):
        o_ref[...] = s_ref[...].astype(o_ref.dtype)

    if axis == 0:
        out_map = lambda i, chip_ref: (chip_ref[0] * (sr // tr) + i, 0)
    else:
        out_map = lambda i, chip_ref: (i, chip_ref[0])
    return pl.pallas_call(
        body, name=name, out_shape=jax.ShapeDtypeStruct((rows, cols), WEIGHT_COMM_DTYPE), compiler_params=SUM_PARAMS,
        grid_spec=pltpu.PrefetchScalarGridSpec(
            num_scalar_prefetch=1, grid=(sr // tr,), in_specs=[pl.BlockSpec((tr, sc), lambda i, chip_ref: (i, 0))],
            out_specs=pl.BlockSpec((tr, sc), out_map)),
    )(chip, shard)


def add_halves(g, theirs, core, rows, cols, axis, name):
    hr, hc = _half_shape(rows, cols, axis)
    tr = _row_tile(hr, hc, 3)

    def body(core_ref, g_ref, t_ref, o_ref):
        o_ref[...] = (g_ref[...].astype(F32) + t_ref[...].astype(F32)).astype(o_ref.dtype)

    if axis == 0:
        g_map = lambda i, core_ref: (i, core_ref[0])
    else:
        g_map = lambda i, core_ref: (core_ref[0] * (hr // tr) + i, 0)
    blk = pl.BlockSpec((tr, hc), lambda i, core_ref: (i, 0))
    return pl.pallas_call(
        body, name=name, out_shape=jax.ShapeDtypeStruct((hr, hc), GRAD_COMM_DTYPE), compiler_params=SUM_PARAMS,
        grid_spec=pltpu.PrefetchScalarGridSpec(
            num_scalar_prefetch=1, grid=(hr // tr,), in_specs=[pl.BlockSpec((tr, hc), g_map), blk], out_specs=blk),
    )(core, g, theirs)


def add_pieces(half, got, chip, rows, cols, axis, name):
    hr, _ = _half_shape(rows, cols, axis)
    pr, pc = _piece_shape(rows, cols, axis)
    tr = _row_tile(pr, pc, 5)

    def body(chip_ref, h_ref, got_ref, o_ref):
        o_ref[...] = (h_ref[...].astype(F32) + got_ref[0].astype(F32) + got_ref[1].astype(F32) + got_ref[2].astype(F32))

    if axis == 0:
        h_map = lambda i, chip_ref: (chip_ref[0] * (pr // tr) + i, 0)
    else:
        h_map = lambda i, chip_ref: (i, chip_ref[0])
    return pl.pallas_call(
        body, name=name, out_shape=jax.ShapeDtypeStruct((pr, pc), F32), compiler_params=SUM_PARAMS,
        grid_spec=pltpu.PrefetchScalarGridSpec(
            num_scalar_prefetch=1, grid=(pr // tr,),
            in_specs=[pl.BlockSpec((tr, pc), h_map), pl.BlockSpec((3, tr, pc), lambda i, chip_ref: (0, i, 0))],
            out_specs=pl.BlockSpec((tr, pc), lambda i, chip_ref: (i, 0))),
    )(chip, half, got)


def _adamw_math(w, g, m, v):
    nm = ADAM_B1 * m + (1.0 - ADAM_B1) * g
    nv = ADAM_B2 * v + (1.0 - ADAM_B2) * (g * g)
    m_hat = nm / (1.0 - ADAM_B1 ** ADAM_STEP)
    v_hat = nv / (1.0 - ADAM_B2 ** ADAM_STEP)
    return -ADAM_LR * (m_hat / (jnp.sqrt(v_hat) + ADAM_EPS) + ADAM_WD * w), nm, nv


def adamw_halves(w, mine, theirs, m, v, core, rows, cols, axis, name):
    sr, sc = _shard_shape(rows, cols, axis)
    pr, pc = _piece_shape(rows, cols, axis)
    tr = _row_tile(pr, pc, 9)
    nt = pr // tr

    def body(core_ref, w_ref, a_ref, b_ref, m_ref, v_ref, g_ref, d_ref, nm_ref, nv_ref):
        g = jnp.where(pl.program_id(0) == core_ref[0], a_ref[...], b_ref[...])
        g_ref[...] = g
        d_ref[...], nm_ref[...], nv_ref[...] = _adamw_math(w_ref[...], g, m_ref[...], v_ref[...])

    if axis == 0:
        full = pl.BlockSpec((tr, pc), lambda h, i, core_ref: (i, h))
    else:
        full = pl.BlockSpec((tr, pc), lambda h, i, core_ref: (h * nt + i, 0))
    part = pl.BlockSpec((tr, pc), lambda h, i, core_ref: (i, 0))
    out = jax.ShapeDtypeStruct((sr, sc), F32)
    return pl.pallas_call(
        body, name=name, out_shape=[out, out, out, out], compiler_params=SUM_PARAMS,
        grid_spec=pltpu.PrefetchScalarGridSpec(
            num_scalar_prefetch=1, grid=(2, nt), in_specs=[full, part, part, full, full], out_specs=[full] * 4),
    )(core, w, mine, theirs, m, v)


BIG = (
    ("ffn1_w_gate_up", D_MODEL, 2 * D_FF, 1),
    ("ffn1_w_down", D_FF, D_MODEL, 0),
    ("w_in", D_MODEL, IN_COLS, 1),
    ("w_branch_hg", HG_WIDTH, D_MODEL, 1),
    ("w_branch_att", ATT_WIDTH, D_MODEL, 1),
    ("w_out", D_MODEL, D_MODEL, 0),
    ("ffn2_w_gate_up", D_MODEL, 2 * D_FF, 1),
    ("ffn2_w_down", D_FF, D_MODEL, 0),
)
N_BIG = len(BIG)
ANY = pl.BlockSpec(memory_space=pl.ANY)


def _place():
    return lax.axis_index("x"), lax.axis_index("y"), lax.axis_index("c")


def _other_chips(x, y):
    return ((1 - x, y), (x, 1 - y), (1 - x, 1 - y))


MAX_COPY_CHUNKS = 16
CHUNK_ROW_ALIGN = 16


def _row_chunks(view):
    rows = view.shape[0]
    n = next(n for n in range(MAX_COPY_CHUNKS, 0, -1) if rows % (CHUNK_ROW_ALIGN * n) == 0 or n == 1)
    step = rows // n
    return [pl.ds(i * step, step) for i in range(n)]


def _remote(src, dst, send_sem, recv_sem, device):
    return pltpu.make_async_remote_copy(src_ref=src, dst_ref=dst, send_sem=send_sem, recv_sem=recv_sem,
                                        device_id=device, device_id_type=MESH)


def _start_remote(src, dst, send_sem, recv_sem, device):
    for rows in _row_chunks(src):
        _remote(src.at[rows, :], dst.at[rows, :], send_sem, recv_sem, device).start()
    return _remote(src, dst, send_sem, recv_sem, device)


HBM = pl.BlockSpec(memory_space=pltpu.HBM)
SEM = pl.BlockSpec(memory_space=pltpu.SEMAPHORE)
SPLIT_COPY_EFFECT = pltpu.SideEffectType.DATAFLOW_SIDE_EFFECTING
GROUPS = {"ffn1": (0, 1), "mix": (2, 3, 4, 5), "ffn2": (6, 7)}


def _in_hbm(a):
    return pltpu.with_memory_space_constraint(a, pltpu.HBM)


class _SemList:
    def __init__(self, refs):
        self.refs = refs
        self.at = self

    def __getitem__(self, index):
        w, k = index
        return self.refs[3 * w + k]


def _gather_piece(ref, rows, cols, axis, chip, c):
    sr, sc = _shard_shape(rows, cols, axis)
    j = 2 * chip[0] + chip[1]
    if axis == 0:
        return ref.at[pl.ds(j * sr + c * (sr // 2), sr // 2), :]
    return ref.at[pl.ds(c * (sr // 2), sr // 2), pl.ds(pl.multiple_of(j * sc, 128), sc)]


def _start_gather_sends(bufs, ws, send_sems, recv_sems):
    x, y, c = _place()
    for w, (_, r, cc, ax) in enumerate(ws):
        mine = _gather_piece(bufs[w], r, cc, ax, (x, y), c)
        for k, chip in enumerate(_other_chips(x, y)):
            _start_remote(mine, mine, send_sems.at[w, k], recv_sems.at[w, k], (*chip, c))


def _wait_gather_sends(bufs, ws, send_sems, recv_sems):
    x, y, c = _place()
    for w, (_, r, cc, ax) in enumerate(ws):
        for k, chip in enumerate(_other_chips(x, y)):
            got = _gather_piece(bufs[w], r, cc, ax, chip, c)
            _remote(got, got, send_sems.at[w, k], recv_sems.at[w, k], (x, y, c)).wait_recv()
    for w, (_, r, cc, ax) in enumerate(ws):
        mine = _gather_piece(bufs[w], r, cc, ax, (x, y), c)
        for k in range(3):
            _remote(mine, mine, send_sems.at[w, k], recv_sems.at[w, k], (x, y, c)).wait_send()


def _forward_halves(bufs, ws, send_sems, recv_sems):
    x, y, c = _place()
    passed = []
    for w, (_, r, cc, ax) in enumerate(ws):
        for k, chip in enumerate(_other_chips(x, y)):
            got = _gather_piece(bufs[w], r, cc, ax, chip, c)
            passed.append(_start_remote(got, got, send_sems.at[w, k], recv_sems.at[w, k], (x, y, 1 - c)))
    for w, (_, r, cc, ax) in enumerate(ws):
        for k, chip in enumerate(_other_chips(x, y)):
            got = _gather_piece(bufs[w], r, cc, ax, chip, 1 - c)
            _remote(got, got, send_sems.at[w, k], recv_sems.at[w, k], (x, y, c)).wait_recv()
    for cp in passed:
        cp.wait_send()


def gather_start(placed, after, group):
    ws = [BIG[i] for i in GROUPS[group]]
    n = len(ws)

    def body(*refs):
        bufs = refs[:n]
        send_sems, recv_sems = _SemList(refs[n + 1:4 * n + 1]), _SemList(refs[4 * n + 1:7 * n + 1])
        token = refs[-1]
        _start_gather_sends(bufs, ws, send_sems, recv_sems)
        token[...] = jnp.zeros_like(token)

    out = pl.pallas_call(
        body, name=f"gather_start_{group}", in_specs=[HBM] * n + [ANY],
        out_specs=[SEM] * (6 * n) + [HBM] * n + [pl.BlockSpec(memory_space=pltpu.VMEM)],
        out_shape=[pltpu.SemaphoreType.DMA(())] * (6 * n)
        + [pltpu.HBM((r, cc), WEIGHT_COMM_DTYPE) for _, r, cc, _ in ws] + [jax.ShapeDtypeStruct((8, 128), F32)],
        input_output_aliases={w: 6 * n + w for w in range(n)},
        compiler_params=pltpu.CompilerParams(has_side_effects=SPLIT_COPY_EFFECT),
    )(*[_in_hbm(p) for p in placed], after)
    return out[:3 * n], out[3 * n:6 * n], out[6 * n:7 * n], out[-1]


def gather_wait(bufs, send_sems, recv_sems, after, group):
    ws = [BIG[i] for i in GROUPS[group]]
    n = len(ws)

    def body(*refs):
        _wait_gather_sends(refs[:n], ws, _SemList(refs[n:n + 3 * n]), _SemList(refs[n + 3 * n:n + 6 * n]))

    return pl.pallas_call(
        body, name=f"gather_wait_{group}", in_specs=[HBM] * n + [SEM] * (6 * n) + [ANY] * len(after), out_specs=[HBM] * n,
        out_shape=[pltpu.HBM((r, cc), WEIGHT_COMM_DTYPE) for _, r, cc, _ in ws],
        input_output_aliases={w: w for w in range(n)},
        compiler_params=pltpu.CompilerParams(has_side_effects=SPLIT_COPY_EFFECT),
    )(*bufs, *send_sems, *recv_sems, *after)


def gather_forward(bufs, group):
    ws = [BIG[i] for i in GROUPS[group]]
    n = len(ws)

    def body(*refs):
        _forward_halves(refs[n:2 * n], ws, refs[2 * n], refs[2 * n + 1])

    return pl.pallas_call(
        body, name=f"gather_forward_{group}", in_specs=[ANY] * n, out_specs=[ANY] * n,
        out_shape=[jax.ShapeDtypeStruct((r, cc), WEIGHT_COMM_DTYPE) for _, r, cc, _ in ws],
        input_output_aliases={w: w for w in range(n)},
        scratch_shapes=[pltpu.SemaphoreType.DMA((n, 3))] * 2,
    )(*bufs)


def _half(ref, rows, cols, axis, c):
    if axis == 0:
        return ref.at[:, pl.ds(pl.multiple_of(c * (cols // 2), 128), cols // 2)]
    return ref.at[pl.ds(c * (rows // 2), rows // 2), :]


def _piece_of_half(ref, rows, cols, axis, chip):
    j = 2 * chip[0] + chip[1]
    pr, pc = _piece_shape(rows, cols, axis)
    if axis == 0:
        return ref.at[pl.ds(j * pr, pr), :]
    return ref.at[:, pl.ds(pl.multiple_of(j * pc, 128), pc)]


def sibling_exchange_start(srcs, view, landing_shapes, dtype, name):
    n = len(srcs)

    def body(*refs):
        ins, land, sems = refs[:n], refs[n:2 * n], refs[2 * n:4 * n]
        x, y, c = _place()
        for w in range(n):
            _start_remote(view(ins[w], w, c), land[w], sems[w], sems[n + w], (x, y, 1 - c))
        refs[-1][...] = jnp.zeros_like(refs[-1])

    landing = [lax.empty(shape, dtype) for shape in landing_shapes]
    out = pl.pallas_call(
        body, name=name, in_specs=[HBM] * (2 * n),
        out_specs=[SEM] * (2 * n) + [HBM] * (2 * n) + [pl.BlockSpec(memory_space=pltpu.VMEM)],
        out_shape=[pltpu.SemaphoreType.DMA(())] * (2 * n) + [pltpu.HBM(a.shape, a.dtype) for a in srcs]
        + [pltpu.HBM(shape, dtype) for shape in landing_shapes] + [jax.ShapeDtypeStruct((8, 128), F32)],
        input_output_aliases={i: 2 * n + i for i in range(2 * n)},
        compiler_params=pltpu.CompilerParams(has_side_effects=SPLIT_COPY_EFFECT),
    )(*[_in_hbm(a) for a in srcs], *[_in_hbm(b) for b in landing])
    return out[:n], out[n:2 * n], out[2 * n:3 * n], out[3 * n:4 * n], out[-1]


def sibling_exchange_wait(srcs, landing, send_sems, recv_sems, view, after, name):
    n = len(srcs)

    def body(*refs):
        ins, land, sems = refs[:n], refs[n:2 * n], refs[2 * n:4 * n]
        x, y, c = _place()
        for w in range(n):
            cp = _remote(view(ins[w], w, c), land[w], sems[w], sems[n + w], (x, y, c))
            cp.wait_send()
            cp.wait_recv()

    out = pl.pallas_call(
        body, name=name, in_specs=[HBM] * (2 * n) + [SEM] * (2 * n) + [ANY] * len(after), out_specs=[HBM] * (2 * n),
        out_shape=[pltpu.HBM(a.shape, a.dtype) for a in srcs] + [pltpu.HBM(b.shape, b.dtype) for b in landing],
        input_output_aliases={i: i for i in range(2 * n)},
        compiler_params=pltpu.CompilerParams(has_side_effects=SPLIT_COPY_EFFECT),
    )(*srcs, *landing, *send_sems, *recv_sems, *after)
    return out[:n], out[n:]


def _scatter_copies(halves, got, ws, send_sems, recv_sems, start):
    x, y, c = _place()
    copies = []
    for w, (_, r, cc, ax) in enumerate(ws):
        for k, chip in enumerate(_other_chips(x, y)):
            args = (_piece_of_half(halves[w], r, cc, ax, chip), got[w].at[k], send_sems.at[w, k], recv_sems.at[w, k], (*chip, c))
            copies.append(_start_remote(*args) if start else _remote(*args))
    return copies


def scatter_start(halves, group):
    ws = [BIG[i] for i in GROUPS[group]]
    n = len(ws)

    def body(*refs):
        sems = refs[2 * n:8 * n]
        _scatter_copies(refs[:n], refs[n:2 * n], ws, _SemList(sems[:3 * n]), _SemList(sems[3 * n:]), start=True)
        refs[-1][...] = jnp.zeros_like(refs[-1])

    landing = [lax.empty((3,) + _piece_shape(r, cc, ax), GRAD_COMM_DTYPE) for _, r, cc, ax in ws]
    out = pl.pallas_call(
        body, name=f"scatter_start_{group}", in_specs=[HBM] * (2 * n),
        out_specs=[SEM] * (6 * n) + [HBM] * (2 * n) + [pl.BlockSpec(memory_space=pltpu.VMEM)],
        out_shape=[pltpu.SemaphoreType.DMA(())] * (6 * n)
        + [pltpu.HBM(_half_shape(r, cc, ax), GRAD_COMM_DTYPE) for _, r, cc, ax in ws]
        + [pltpu.HBM((3,) + _piece_shape(r, cc, ax), GRAD_COMM_DTYPE) for _, r, cc, ax in ws]
        + [jax.ShapeDtypeStruct((8, 128), F32)],
        input_output_aliases={i: 6 * n + i for i in range(2 * n)},
        compiler_params=pltpu.CompilerParams(has_side_effects=SPLIT_COPY_EFFECT),
    )(*[_in_hbm(h) for h in halves], *[_in_hbm(b) for b in landing])
    return out[:3 * n], out[3 * n:6 * n], out[6 * n:7 * n], out[7 * n:8 * n], out[-1]


def scatter_wait(halves, got, send_sems, recv_sems, after, group):
    ws = [BIG[i] for i in GROUPS[group]]
    n = len(ws)

    def body(*refs):
        sems = refs[2 * n:8 * n]
        for cp in _scatter_copies(refs[:n], refs[n:2 * n], ws, _SemList(sems[:3 * n]), _SemList(sems[3 * n:]), start=False):
            cp.wait_send()
            cp.wait_recv()

    out = pl.pallas_call(
        body, name=f"scatter_wait_{group}", in_specs=[HBM] * (2 * n) + [SEM] * (6 * n) + [ANY] * len(after),
        out_specs=[HBM] * (2 * n),
        out_shape=[pltpu.HBM(_half_shape(r, cc, ax), GRAD_COMM_DTYPE) for _, r, cc, ax in ws]
        + [pltpu.HBM((3,) + _piece_shape(r, cc, ax), GRAD_COMM_DTYPE) for _, r, cc, ax in ws],
        input_output_aliases={i: i for i in range(2 * n)},
        compiler_params=pltpu.CompilerParams(has_side_effects=SPLIT_COPY_EFFECT),
    )(*halves, *got, *send_sems, *recv_sems, *after)
    return out[:n], out[n:]


N_DEV = 8
SMALL = ("ffn1_norm", "mix_norm", "hg_lower_bounds", "hg_out_norm", "ffn2_norm", "final_norm")
SMALL_STAGE_ROWS = 8


def small_step(loss, grads, w, m, v, behind):
    n = len(SMALL)
    shapes = [g.shape for g in grads]
    first_row = [sum(s[0] for s in shapes[:i]) for i in range(n + 1)]
    assert first_row[n] < SMALL_STAGE_ROWS
    loss_row = (pl.ds(first_row[n], 1), pl.ds(0, loss.shape[1]))

    def body(*refs):
        loss_ref, g_refs, w_refs, m_refs, v_refs = refs[0], refs[1:1 + n], refs[1 + n:1 + 2 * n], refs[1 + 2 * n:1 + 3 * n], refs[1 + 3 * n:1 + 4 * n]
        outs = refs[2 + 4 * n:3 + 8 * n]
        loss_out, dg_refs, d_refs, nm_refs, nv_refs = outs[0], outs[1:1 + n], outs[1 + n:1 + 2 * n], outs[1 + 2 * n:1 + 3 * n], outs[1 + 3 * n:]
        stage, gathered, send_sems, recv_sems = refs[3 + 8 * n:]
        x, y, c = _place()
        me = 4 * x + 2 * y + c

        def slot(i, shape):
            return pl.ds(first_row[i], shape[0]), pl.ds(0, shape[1])

        stage[...] = jnp.zeros_like(stage)
        for i, g_ref in enumerate(g_refs):
            stage[slot(i, shapes[i])] = g_ref[...]
        stage[loss_row] = loss_ref[pl.ds(0, 1), :]
        gathered[me] = stage[...]
        copies = []
        for k in range(1, N_DEV):
            peer = (x ^ (k >> 2), y ^ ((k >> 1) & 1), c ^ (k & 1))
            cp = pltpu.make_async_remote_copy(
                src_ref=stage, dst_ref=gathered.at[me], send_sem=send_sems.at[k - 1], recv_sem=recv_sems.at[k - 1],
                device_id=peer, device_id_type=MESH)
            cp.start()
            copies.append(cp)
        for cp in copies:
            cp.wait()
        acc = gathered[0]
        for k in range(1, N_DEV):
            acc = acc + gathered[k]
        stage[...] = acc
        loss_out[...] = jnp.broadcast_to(stage[loss_row], loss_out.shape)
        for i in range(n):
            g = stage[slot(i, shapes[i])]
            dg_refs[i][...] = g
            d_refs[i][...], nm_refs[i][...], nv_refs[i][...] = _adamw_math(w_refs[i][...], g, m_refs[i][...], v_refs[i][...])

    vm = pl.BlockSpec(memory_space=pltpu.VMEM)
    per_param = [jax.ShapeDtypeStruct(s, F32) for s in shapes]
    out = pl.pallas_call(
        body, name="small_step", in_specs=[vm] * (1 + 4 * n) + [ANY], out_specs=[vm] * (1 + 4 * n),
        out_shape=[jax.ShapeDtypeStruct(loss.shape, F32)] + per_param * 4,
        scratch_shapes=[pltpu.VMEM((SMALL_STAGE_ROWS, D_MODEL), F32),
                        pltpu.VMEM((N_DEV, SMALL_STAGE_ROWS, D_MODEL), F32),
                        pltpu.SemaphoreType.DMA((N_DEV - 1,)), pltpu.SemaphoreType.DMA((N_DEV - 1,))],
    )(loss, *grads, *w, *m, *v, behind)
    return out[0], out[1:1 + n], out[1 + n:1 + 2 * n], out[1 + 2 * n:1 + 3 * n], out[1 + 3 * n:]


def _swiglu_block_fwd(h, norm_g, w_gu, w_down, tag, behind=()):
    n = rmsnorm_fwd(h, norm_g, f"{tag}_norm", behind=behind)
    a, b, s = gate_up_swiglu(n, w_gu, f"{tag}_gate_up")
    h_out = matmul(s, w_down, res=h, scale=0.5, name=f"{tag}_down")
    return h_out, (n, a, b, s)


def _swiglu_block_bwd(h, norm_g, w_gu, w_down, saved, dh_out, df, tag, exchange, behind=()):
    n, a, b, s = saved
    d_down = matmul(s, df, ta=True, scale=0.5, out_dtype=GRAD_COMM_DTYPE, name=f"{tag}_d_w_down")
    ds = matmul(df, w_down, tb=True, scale=0.5, out_dtype=ACT_DTYPE, behind=behind, name=f"{tag}_d_s")
    dgu = swiglu_bwd(a, b, ds, f"{tag}_swiglu_bwd")
    d_gu = matmul(n, dgu, ta=True, out_dtype=GRAD_COMM_DTYPE, name=f"{tag}_d_w_gate_up")
    tokens = exchange.gradients_ready(tag, {f"{tag}_w_gate_up": d_gu, f"{tag}_w_down": d_down})
    dn = matmul(dgu, w_gu, tb=True, behind=tokens, name=f"{tag}_d_n")
    dh, dh_m, dg = rmsnorm_bwd(h, norm_g, dn, dh_out, f"{tag}_norm_bwd")
    return dh, dh_m, dg


def local_step(x, target, small, exchange):
    big = {}
    token, big_ffn1 = exchange.weights("ffn1", x)
    big.update(big_ffn1)
    h1, saved1 = _swiglu_block_fwd(x, small["ffn1_norm"], big["ffn1_w_gate_up"], big["ffn1_w_down"], "ffn1", token)
    token, big_mix = exchange.weights("mix", h1)
    big.update(big_mix)
    u = rmsnorm_fwd(h1, small["mix_norm"], "mix_norm", behind=token)
    z = matmul(u, big["w_in"], name="w_in")
    p = small["hg_lower_bounds"]
    lb = 1.0 / (1.0 + jnp.exp(p[1:2] - p[0:1]))
    y_hg, o_raw, states = hgrn_fwd(z, lb, small["hg_out_norm"], "hgrn_fwd")
    o_att, l_att = zip(*[att_fwd(z, g, f"att_fwd_{g}") for g in range(N_GROUPS)])
    y_att = att_combine_fwd(o_att, l_att, "att_combine")
    bh = matmul(y_hg, big["w_branch_hg"], name="branch_hg")
    ba = matmul(y_att, big["w_branch_att"], name="branch_att")
    merged = merge_fwd(z, bh, ba, "merge")
    h2 = matmul(merged, big["w_out"], res=h1, name="w_out")
    token, big_ffn2 = exchange.weights("ffn2", h2)
    big.update(big_ffn2)
    h3, saved2 = _swiglu_block_fwd(h2, small["ffn2_norm"], big["ffn2_w_gate_up"], big["ffn2_w_down"], "ffn2", token)
    dh3, dh3_m, d_final, loss = final_norm_loss(h3, small["final_norm"], target, "final_norm_loss")

    gs, gb = {"final_norm": d_final}, {}
    dh2, dh2_m, gs["ffn2_norm"] = _swiglu_block_bwd(
        h2, small["ffn2_norm"], big["ffn2_w_gate_up"], big["ffn2_w_down"], saved2, dh3, dh3_m, "ffn2", exchange)
    token = exchange.backward_done("ffn2", dh2)
    gb["w_out"] = matmul(merged, dh2_m, ta=True, out_dtype=GRAD_COMM_DTYPE, name="d_w_out")
    dmerged = matmul(dh2_m, big["w_out"], tb=True, behind=token, name="d_merged")
    dbh, dba, dgh, dga = merge_bwd(z, bh, ba, dmerged, "merge_bwd")
    gb["w_branch_hg"] = matmul(y_hg, dbh, ta=True, out_dtype=GRAD_COMM_DTYPE, name="d_w_branch_hg")
    gb["w_branch_att"] = matmul(y_att, dba, ta=True, out_dtype=GRAD_COMM_DTYPE, name="d_w_branch_att")
    dy_hg = matmul(dbh, big["w_branch_hg"], tb=True, name="d_y_hg")
    dy_att = matmul(dba, big["w_branch_att"], tb=True, name="d_y_att")
    dq, dfp, di, dog, d_lb, gs["hg_out_norm"] = hgrn_bwd(z, lb, small["hg_out_norm"], o_raw, states, dy_hg, "hgrn_bwd")
    do_att, corr = att_combine_bwd(o_att, l_att, dy_att, "att_combine_bwd")
    d_att = [part for g in range(N_GROUPS) for part in att_bwd(z, l_att[g], do_att[g], corr[g], g, f"att_bwd_{g}")]
    dz = jnp.concatenate([dq, dfp, di, dog, *d_att, dgh, dga], axis=1)
    gb["w_in"] = matmul(u, dz, ta=True, out_dtype=GRAD_COMM_DTYPE, name="d_w_in")
    token = exchange.gradients_ready("mix", gb)
    du = matmul(dz, big["w_in"], tb=True, behind=token, name="d_u")
    dh1, dh1_m, gs["mix_norm"] = rmsnorm_bwd(h1, small["mix_norm"], du, dh2, "mix_norm_bwd")
    token = exchange.backward_done("mix", dh1)
    dp0 = d_lb * lb * (1.0 - lb)
    gs["hg_lower_bounds"] = jnp.concatenate([dp0, -dp0], axis=0)
    dx, _, gs["ffn1_norm"] = _swiglu_block_bwd(
        x, small["ffn1_norm"], big["ffn1_w_gate_up"], big["ffn1_w_down"], saved1, dh1, dh1_m, "ffn1", exchange, token)
    exchange.backward_done("ffn1", dx)
    return loss, dx, gs


WEIGHTS = ("ffn1_norm", "ffn1_w_gate_up", "ffn1_w_down", "mix_norm", "w_in", "hg_lower_bounds", "hg_out_norm",
           "w_branch_hg", "w_branch_att", "w_out", "ffn2_norm", "ffn2_w_gate_up", "ffn2_w_down", "final_norm")


class WeightExchange:
    ORDER = ("ffn1", "mix", "ffn2")

    def __init__(self, shards, core, chip):
        self.core, self.chip = core, chip
        self.halving = None
        self.scattering = None
        self.reducing = {}
        first = self.ORDER[0]
        self.placed = {BIG[i][0]: place_own_block(shards[BIG[i][0]], chip, *BIG[i][1:], f"place_{BIG[i][0]}")
                       for i in GROUPS[first]}
        self._start_gather(first, self.placed[self._names(first)[0]])
        chip_behind = chip + self.token[0, :1].astype(jnp.int32)
        for group in self.ORDER[1:]:
            for i in GROUPS[group]:
                n, r, cc, ax = BIG[i]
                self.placed[n] = place_own_block(shards[n], chip_behind, r, cc, ax, f"place_{n}")
        self.placed_behind = [self.placed[n] for group in self.ORDER[1:] for n in self._names(group)]

    def _names(self, group):
        return [BIG[i][0] for i in GROUPS[group]]

    def _start_gather(self, group, after):
        send_sems, recv_sems, bufs, self.token = gather_start([self.placed[n] for n in self._names(group)], after, group)
        self.gathering = (group, send_sems, recv_sems, bufs)

    def weights(self, group, h):
        pending, send_sems, recv_sems, bufs = self.gathering
        assert pending == group
        after = self.placed_behind if group == self.ORDER[0] else [h]
        whole = gather_forward(gather_wait(bufs, send_sems, recv_sems, after, group), group)
        later = self.ORDER.index(group) + 1
        behind = []
        if later < len(self.ORDER):
            self._start_gather(self.ORDER[later], whole[0])
            behind = [self.token]
        return behind, dict(zip(self._names(group), whole))

    @staticmethod
    def _half_to_sibling(ws):
        return lambda ref, w, c: _half(ref, *ws[w][1:], 1 - c)

    def gradients_ready(self, group, grads):
        ws = [BIG[i] for i in GROUPS[group]]
        send_sems, recv_sems, own, theirs, token = sibling_exchange_start(
            [grads[n] for n, *_ in ws], self._half_to_sibling(ws), [_half_shape(r, cc, ax) for _, r, cc, ax in ws],
            GRAD_COMM_DTYPE, f"halves_start_{group}")
        self.halving = (group, send_sems, recv_sems, own, theirs)
        return [token]

    def backward_done(self, group, dh):
        behind = [self._finish_scatter([dh])] if self.scattering is not None else []
        pending, send_sems, recv_sems, own, theirs = self.halving
        assert pending == group
        ws = [BIG[i] for i in GROUPS[group]]
        own, theirs = sibling_exchange_wait(own, theirs, send_sems, recv_sems, self._half_to_sibling(ws), [dh],
                                            f"halves_wait_{group}")
        halves = [add_halves(g, t, self.core, r, cc, ax, f"add_halves_{n}") for (n, r, cc, ax), g, t in zip(ws, own, theirs)]
        send_sems, recv_sems, halves, got, self.token = scatter_start(halves, group)
        self.scattering = (group, send_sems, recv_sems, halves, got)
        return behind + [self.token]

    def _finish_scatter(self, after):
        group, send_sems, recv_sems, halves, got = self.scattering
        halves, got = scatter_wait(halves, got, send_sems, recv_sems, after, group)
        ws = [BIG[i] for i in GROUPS[group]]
        mine = [add_pieces(h, g, self.chip, r, cc, ax, f"add_pieces_{n}") for (n, r, cc, ax), h, g in zip(ws, halves, got)]
        send_sems, recv_sems, mine, theirs, token = sibling_exchange_start(
            mine, lambda ref, w, c: ref, [_piece_shape(r, cc, ax) for _, r, cc, ax in ws], F32, f"reduced_start_{group}")
        self.reducing[group] = (send_sems, recv_sems, mine, theirs)
        self.scattering = None
        return token

    def finish(self, after):
        return self._finish_scatter(after)

    def reduced_halves(self, group, after):
        send_sems, recv_sems, mine, theirs = self.reducing.pop(group)
        mine, theirs = sibling_exchange_wait(mine, theirs, send_sems, recv_sems, lambda ref, w, c: ref, after,
                                             f"reduced_wait_{group}")
        return {BIG[i][0]: (a, b) for i, a, b in zip(GROUPS[group], mine, theirs)}


def kernel(x, ffn1_norm, ffn1_w_gate_up, ffn1_w_down, mix_norm, w_in, hg_lower_bounds, hg_out_norm, w_branch_hg, w_branch_att, w_out, ffn2_norm, ffn2_w_gate_up, ffn2_w_down, final_norm, loss_target, m_ffn1_norm, m_ffn1_w_gate_up, m_ffn1_w_down, m_mix_norm, m_w_in, m_hg_lower_bounds, m_hg_out_norm, m_w_branch_hg, m_w_branch_att, m_w_out, m_ffn2_norm, m_ffn2_w_gate_up, m_ffn2_w_down, m_final_norm, v_ffn1_norm, v_ffn1_w_gate_up, v_ffn1_w_down, v_mix_norm, v_w_in, v_hg_lower_bounds, v_hg_out_norm, v_w_branch_hg, v_w_branch_att, v_w_out, v_ffn2_norm, v_ffn2_w_gate_up, v_ffn2_w_down, v_final_norm):
    w = dict(ffn1_norm=ffn1_norm, ffn1_w_gate_up=ffn1_w_gate_up, ffn1_w_down=ffn1_w_down, mix_norm=mix_norm, w_in=w_in,
             hg_lower_bounds=hg_lower_bounds, hg_out_norm=hg_out_norm, w_branch_hg=w_branch_hg, w_branch_att=w_branch_att,
             w_out=w_out, ffn2_norm=ffn2_norm, ffn2_w_gate_up=ffn2_w_gate_up, ffn2_w_down=ffn2_w_down, final_norm=final_norm)
    m = dict(ffn1_norm=m_ffn1_norm, ffn1_w_gate_up=m_ffn1_w_gate_up, ffn1_w_down=m_ffn1_w_down, mix_norm=m_mix_norm,
             w_in=m_w_in, hg_lower_bounds=m_hg_lower_bounds, hg_out_norm=m_hg_out_norm, w_branch_hg=m_w_branch_hg,
             w_branch_att=m_w_branch_att, w_out=m_w_out, ffn2_norm=m_ffn2_norm, ffn2_w_gate_up=m_ffn2_w_gate_up,
             ffn2_w_down=m_ffn2_w_down, final_norm=m_final_norm)
    v = dict(ffn1_norm=v_ffn1_norm, ffn1_w_gate_up=v_ffn1_w_gate_up, ffn1_w_down=v_ffn1_w_down, mix_norm=v_mix_norm,
             w_in=v_w_in, hg_lower_bounds=v_hg_lower_bounds, hg_out_norm=v_hg_out_norm, w_branch_hg=v_w_branch_hg,
             w_branch_att=v_w_branch_att, w_out=v_w_out, ffn2_norm=v_ffn2_norm, ffn2_w_gate_up=v_ffn2_w_gate_up,
             ffn2_w_down=v_ffn2_w_down, final_norm=v_final_norm)

    core = lax.axis_index("c").astype(jnp.int32).reshape(1)
    chip = (2 * lax.axis_index("x") + lax.axis_index("y")).astype(jnp.int32).reshape(1)
    exchange = WeightExchange({n: w[n][0] for n, *_ in BIG}, core, chip)
    small = {n: w[n] for n in SMALL}
    small["final_norm"] = final_norm.reshape(1, D_MODEL)

    loss, dx, gs = local_step(x[0], loss_target[0], small, exchange)

    grads, delta, new_m, new_v = {}, {}, {}, {}

    def update(group, core, after):
        reduced = exchange.reduced_halves(group, after)
        for i in GROUPS[group]:
            n, r, cc, ax = BIG[i]
            a, b = reduced[n]
            g, d, nm, nv = adamw_halves(w[n][0], a, b, m[n][0], v[n][0], core, r, cc, ax, f"adamw_{n}")
            grads[n], delta[n], new_m[n], new_v[n] = g[None], d[None], nm[None], nv[None]

    core_behind = core + exchange.token[0, :1].astype(jnp.int32)
    update("ffn2", core_behind, [exchange.token])
    update("mix", core_behind, [delta["ffn2_w_down"]])
    token = exchange.finish(after=[delta[BIG[i][0]] for group in ("ffn2", "mix") for i in GROUPS[group]])
    two_d = lambda a: a.reshape(1, D_MODEL) if a.ndim == 1 else a
    loss_sum, *small_out = small_step(loss, [gs[n] for n in SMALL], *[[two_d(p[n]) for n in SMALL] for p in (w, m, v)],
                                      behind=token)
    for result, parts in zip((grads, delta, new_m, new_v), small_out):
        result.update({n: a.reshape(w[n].shape) for n, a in zip(SMALL, parts)})
    update("ffn1", core, [loss_sum])

    return (loss_sum[0, 0], dx[None], *[grads[n] for n in WEIGHTS], *[delta[n] for n in WEIGHTS],
            *[new_m[n] for n in WEIGHTS], *[new_v[n] for n in WEIGHTS])
```

```python
import numpy as np
import jax
import jax.numpy as jnp
from jax import lax
from jax.experimental import pallas as pl
from jax.experimental.pallas import tpu as pltpu

SEQ = 2048
D_MODEL = 1024
D_FF = 2816
HG_HEADS = 4
HG_DIM = 128
HG_WIDTH = 512
HG_CHUNK = 64
ATT_GROUPS = ((128, 1), (512, 4), (2048, 16))
ATT_HEADS = 8
ATT_WIDTH = 512
ATT_BLOCK = 128
ALIBI_MAX = 8.0
IN_COLS = 8704
EPS = 1e-6
NEG_INF = -1e30
ADAM_LR = 0.001
ADAM_B1 = 0.9
ADAM_B2 = 0.999
ADAM_EPS = 1e-08
ADAM_WD = 0.01
ADAM_STEP = 10

N_CHIPS = 4
MXU_DTYPE = jnp.bfloat16
WEIGHT_COMM_DTYPE = jnp.bfloat16
GRAD_COMM_DTYPE = jnp.bfloat16
ACT_DTYPE = jnp.bfloat16
MESH = pl.DeviceIdType.MESH
F32 = jnp.float32


def _sigmoid(x):
    return 1.0 / (1.0 + jnp.exp(-x))


def _dot(a, b, ta=False, tb=False):
    dn = (((0 if ta else 1,), (1 if tb else 0,)), ((), ()))
    return lax.dot_general(a.astype(MXU_DTYPE), b.astype(MXU_DTYPE), dn, preferred_element_type=F32)


def _dot_f32(a, b, ones_on_right=False):
    x = a if ones_on_right else b
    hi = x.astype(jnp.bfloat16)
    rest = x - hi.astype(F32)
    mid = rest.astype(jnp.bfloat16)
    lo = (rest - mid.astype(F32)).astype(jnp.bfloat16)
    if ones_on_right:
        dot = lambda q: jnp.dot(q, b.astype(jnp.bfloat16), preferred_element_type=F32)
    else:
        dot = lambda q: jnp.dot(a.astype(jnp.bfloat16), q, preferred_element_type=F32)
    return dot(hi) + (dot(mid) + dot(lo))


def _split_bf16(x):
    hi = x.astype(jnp.bfloat16)
    return hi, (x - hi.astype(F32)).astype(jnp.bfloat16)


def _hdot(a, b, ta=False, tb=False):
    dn =(((0 if ta else 1,), (1 if tb else 0,)), ((), ()))
    (a_hi, a_lo), (b_hi, b_lo) = _split_bf16(a), _split_bf16(b)
    dot = lambda p, q: lax.dot_general(p, q, dn, preferred_element_type=F32)
    return dot(a_hi, b_hi) + (dot(a_lo, b_hi) + dot(a_hi, b_lo))


MATMUL_VMEM_BYTES = 48 * 1024 * 1024
MATMUL_TILE_BYTES = 36 * 1024 * 1024
MXU_ALIGN = 128


def _divisors(n, most):
    return [t for t in range(min(n, most), 0, -MXU_ALIGN) if n % t == 0 and t % MXU_ALIGN == 0]


def _matmul_tiles(M, N, K, in_bytes, out_bytes, has_res):
    best = None
    for tk in _divisors(K, K):
        nk = K // tk
        for tm in _divisors(M, 2048):
            for tn in _divisors(N, 512):
                tiles = 2 * in_bytes * (tm * tk + tk * tn) + 2 * out_bytes * tm * tn
                tiles += 4 * tm * tn * ((nk > 1) + 2 * has_res)
                if tiles > MATMUL_TILE_BYTES:
                    continue
                traffic = in_bytes * (M * K * (1 if nk == 1 else N // tn) + K * N * (M // tm))
                key = (traffic, -tm * tn * tk)
                if best is None or key < best[0]:
                    best = (key, (tm, tn, tk))
    return best[1]


def matmul(a, b, *, ta=False, tb=False, out_dtype=F32, res=None, scale=1.0, behind=(), name):
    if ta:
        K, M = a.shape
    else:
        M, K = a.shape
    if tb:
        N, K2 = b.shape
    else:
        K2, N = b.shape
    assert K == K2 and a.dtype == b.dtype
    tm, tn, tk = _matmul_tiles(M, N, K, a.dtype.itemsize, jnp.dtype(out_dtype).itemsize, res is not None)
    nk = K // tk

    def finish(r, r_ref, o_ref):
        if scale != 1.0:
            r = r * scale
        if res is not None:
            r = r_ref[...] + r
        o_ref[...] = r.astype(out_dtype)

    def body(*refs):
        a_ref, b_ref = refs[:2]
        r_ref = refs[2] if res is not None else None
        o_ref = refs[2 + (res is not None) + len(behind)]
        if nk == 1:
            finish(_dot(a_ref[...], b_ref[...], ta, tb), r_ref, o_ref)
            return
        acc = refs[-1]
        k = pl.program_id(2)

        @pl.when(k == 0)
        def _():
            acc[...] = jnp.zeros_like(acc)

        acc[...] += _dot(a_ref[...], b_ref[...], ta, tb)

        @pl.when(k == nk - 1)
        def _():
            finish(acc[...], r_ref, o_ref)

    a_spec = pl.BlockSpec((tk, tm), lambda i, j, k: (k, i)) if ta else pl.BlockSpec((tm, tk), lambda i, j, k: (i, k))
    b_spec = pl.BlockSpec((tn, tk), lambda i, j, k: (j, k)) if tb else pl.BlockSpec((tk, tn), lambda i, j, k: (k, j))
    in_specs = [a_spec, b_spec]
    args = [a, b]
    if res is not None:
        in_specs.append(pl.BlockSpec((tm, tn), lambda i, j, k: (i, j)))
        args.append(res)
    for earlier in behind:
        in_specs.append(pl.BlockSpec(memory_space=pl.ANY))
        args.append(earlier)
    return pl.pallas_call(
        body, name=name, grid=(M // tm, N // tn, nk), in_specs=in_specs,
        out_specs=pl.BlockSpec((tm, tn), lambda i, j, k: (i, j)),
        out_shape=jax.ShapeDtypeStruct((M, N), out_dtype),
        scratch_shapes=[pltpu.VMEM((tm, tn), F32)] if nk > 1 else [],
        compiler_params=pltpu.CompilerParams(dimension_semantics=("parallel", "parallel", "arbitrary"),
                                             vmem_limit_bytes=MATMUL_VMEM_BYTES),
    )(*args)


ROW_TILE = 256


def rmsnorm_fwd(x, g, name, behind=()):
    def body(x_ref, g_ref, *refs):
        n_ref = refs[-1]
        xv = x_ref[...]
        r = lax.rsqrt(jnp.mean(xv * xv, axis=-1, keepdims=True) + EPS)
        n_ref[...] = ((xv * r) * g_ref[...]).astype(n_ref.dtype)

    order = list(behind)
    return pl.pallas_call(
        body, name=name, grid=(SEQ // ROW_TILE,),
        in_specs=[pl.BlockSpec((ROW_TILE, D_MODEL), lambda i: (i, 0)), pl.BlockSpec((1, D_MODEL), lambda i: (0, 0))]
        + [pl.BlockSpec(memory_space=pl.ANY)] * len(order),
        out_specs=pl.BlockSpec((ROW_TILE, D_MODEL), lambda i: (i, 0)),
        out_shape=jax.ShapeDtypeStruct((SEQ, D_MODEL), MXU_DTYPE),
    )(x, g, *order)


def rmsnorm_bwd(x, g, dn, dres, name):
    def body(x_ref, g_ref, dn_ref, dr_ref, dx_ref, dxm_ref, dg_ref):
        xv = x_ref[...]
        r = lax.rsqrt(jnp.mean(xv * xv, axis=-1, keepdims=True) + EPS)
        xh = xv * r
        dnv = dn_ref[...]

        @pl.when(pl.program_id(0) == 0)
        def _():
            dg_ref[...] = jnp.zeros_like(dg_ref)

        dg_ref[...] += jnp.sum(dnv * xh, axis=0, keepdims=True)
        dxh = dnv * g_ref[...]
        dx = dr_ref[...] + r * (dxh - xh * jnp.mean(dxh * xh, axis=-1, keepdims=True))
        dx_ref[...] = dx
        dxm_ref[...] = dx.astype(dxm_ref.dtype)

    row = pl.BlockSpec((ROW_TILE, D_MODEL), lambda i: (i, 0))
    vec = pl.BlockSpec((1, D_MODEL), lambda i: (0, 0))
    return pl.pallas_call(
        body, name=name, grid=(SEQ // ROW_TILE,), in_specs=[row, vec, row, row], out_specs=[row, row, vec],
        out_shape=[jax.ShapeDtypeStruct((SEQ, D_MODEL), F32), jax.ShapeDtypeStruct((SEQ, D_MODEL), MXU_DTYPE),
                   jax.ShapeDtypeStruct((1, D_MODEL), F32)],
        compiler_params=pltpu.CompilerParams(dimension_semantics=("arbitrary",)),
    )(x, g, dn, dres)


def final_norm_loss(h, g, target, name):
    def body(h_ref, g_ref, t_ref, dh_ref, dhm_ref, dg_ref, loss_ref):
        xv = h_ref[...]
        r = lax.rsqrt(jnp.mean(xv * xv, axis=-1, keepdims=True) + EPS)
        xh = xv * r
        gv = g_ref[...]
        e = xh * gv - t_ref[...]

        @pl.when(pl.program_id(0) == 0)
        def _():
            dg_ref[...] = jnp.zeros_like(dg_ref)
            loss_ref[...] = jnp.zeros_like(loss_ref)

        part = 0.5 * jnp.sum(jnp.sum(e * e, axis=-1, keepdims=True) * (1.0 / D_MODEL), axis=0, keepdims=True)
        loss_ref[...] += jnp.broadcast_to(part, loss_ref.shape)
        dout = e * (1.0 / D_MODEL)
        dg_ref[...] += jnp.sum(dout * xh, axis=0, keepdims=True)
        dxh = dout * gv
        dh = r * (dxh - xh * jnp.mean(dxh * xh, axis=-1, keepdims=True))
        dh_ref[...] = dh
        dhm_ref[...] = dh.astype(dhm_ref.dtype)

    row = pl.BlockSpec((ROW_TILE, D_MODEL), lambda i: (i, 0))
    vec = pl.BlockSpec((1, D_MODEL), lambda i: (0, 0))
    return pl.pallas_call(
        body, name=name, grid=(SEQ // ROW_TILE,), in_specs=[row, vec, row],
        out_specs=[row, row, vec, pl.BlockSpec((8, 128), lambda i: (0, 0))],
        out_shape=[jax.ShapeDtypeStruct((SEQ, D_MODEL), F32), jax.ShapeDtypeStruct((SEQ, D_MODEL), MXU_DTYPE),
                   jax.ShapeDtypeStruct((1, D_MODEL), F32), jax.ShapeDtypeStruct((8, 128), F32)],
        compiler_params=pltpu.CompilerParams(dimension_semantics=("arbitrary",)),
    )(h, g, target)


FFN_TILE = 256
FFN_TILES = D_FF // FFN_TILE


def gate_up_swiglu(n, w_gu, name):
    def body(n_ref, wa_ref, wb_ref, a_ref, b_ref, s_ref):
        nv = n_ref[...]
        a = _dot(nv, wa_ref[...])
        b = _dot(nv, wb_ref[...])
        a_ref[...] = a.astype(a_ref.dtype)
        b_ref[...] = b.astype(b_ref.dtype)
        s_ref[...] = (a * _sigmoid(a) * b).astype(s_ref.dtype)

    tile = pl.BlockSpec((SEQ, FFN_TILE), lambda j: (0, j))
    act = jax.ShapeDtypeStruct((SEQ, D_FF), ACT_DTYPE)
    return pl.pallas_call(
        body, name=name, grid=(FFN_TILES,),
        in_specs=[pl.BlockSpec((SEQ, D_MODEL), lambda j: (0, 0)), pl.BlockSpec((D_MODEL, FFN_TILE), lambda j: (0, j)),
                  pl.BlockSpec((D_MODEL, FFN_TILE), lambda j: (0, j + FFN_TILES))],
        out_specs=[tile, tile, tile], out_shape=[act, act, jax.ShapeDtypeStruct((SEQ, D_FF), MXU_DTYPE)],
        compiler_params=pltpu.CompilerParams(dimension_semantics=("parallel",), vmem_limit_bytes=MATMUL_VMEM_BYTES),
    )(n, w_gu, w_gu)


def swiglu_bwd(a, b, ds, name):
    rows = ROW_TILE // 2

    def body(a_ref, b_ref, ds_ref, o_ref):
        av = a_ref[...].astype(F32)
        sg = _sigmoid(av)
        dsv = ds_ref[...].astype(F32)
        o_ref[:, :D_FF] = (dsv * b_ref[...].astype(F32) * (sg * (1.0 + av * (1.0 - sg)))).astype(o_ref.dtype)
        o_ref[:, D_FF:] = (dsv * av * sg).astype(o_ref.dtype)

    blk = pl.BlockSpec((rows, D_FF), lambda i: (i, 0))
    return pl.pallas_call(
        body, name=name, grid=(SEQ // rows,), in_specs=[blk, blk, blk],
        out_specs=pl.BlockSpec((rows, 2 * D_FF), lambda i: (i, 0)),
        out_shape=jax.ShapeDtypeStruct((SEQ, 2 * D_FF), MXU_DTYPE), compiler_params=SUM_PARAMS,
    )(a, b, ds)


GATE_HG_BLK = 6656 // 512
GATE_ATT_BLK = 7680 // 512


def merge_fwd(z, bh, ba, name):
    def body(gh_ref, ga_ref, bh_ref, ba_ref, o_ref):
        o_ref[...] = (_sigmoid(gh_ref[...]) * bh_ref[...] + _sigmoid(ga_ref[...]) * ba_ref[...]).astype(o_ref.dtype)

    blk = pl.BlockSpec((ROW_TILE, 512), lambda i, j: (i, j))
    return pl.pallas_call(
        body, name=name, grid=(SEQ // ROW_TILE, 2),
        in_specs=[pl.BlockSpec((ROW_TILE, 512), lambda i, j: (i, GATE_HG_BLK + j)),
                  pl.BlockSpec((ROW_TILE, 512), lambda i, j: (i, GATE_ATT_BLK + j)), blk, blk],
        out_specs=blk, out_shape=jax.ShapeDtypeStruct((SEQ, D_MODEL), MXU_DTYPE),
    )(z, z, bh, ba)


def merge_bwd(z, bh, ba, dm, name):
    assert GATE_ATT_BLK == GATE_HG_BLK + 2

    def body(gh_ref, ga_ref, bh_ref, ba_ref, dm_ref, dbh_ref, dba_ref, dz_ref, dga_sc):
        @pl.when(pl.program_id(2) == 0)
        def _():
            dmv = dm_ref[...]
            sh = _sigmoid(gh_ref[...])
            sa = _sigmoid(ga_ref[...])
            dbh_ref[...] = (dmv * sh).astype(dbh_ref.dtype)
            dba_ref[...] = (dmv * sa).astype(dba_ref.dtype)
            dz_ref[...] = (dmv * bh_ref[...] * (sh * (1.0 - sh))).astype(dz_ref.dtype)
            dga_sc[...] = (dmv * ba_ref[...] * (sa * (1.0 - sa))).astype(dga_sc.dtype)

        @pl.when(pl.program_id(2) == 1)
        def _():
            dz_ref[...] = dga_sc[...]

    blk = pl.BlockSpec((ROW_TILE, 512), lambda i, j, t: (i, j))
    out = jax.ShapeDtypeStruct((SEQ, D_MODEL), MXU_DTYPE)
    return pl.pallas_call(
        body, name=name, grid=(SEQ // ROW_TILE, 2, 2),
        in_specs=[pl.BlockSpec((ROW_TILE, 512), lambda i, j, t: (i, GATE_HG_BLK + j)),
                  pl.BlockSpec((ROW_TILE, 512), lambda i, j, t: (i, GATE_ATT_BLK + j)), blk, blk, blk],
        out_specs=[blk, blk, pl.BlockSpec((ROW_TILE, 512), lambda i, j, t: (i, GATE_HG_BLK + j + 2 * t))],
        out_shape=[out, out, jax.ShapeDtypeStruct((SEQ, IN_COLS), MXU_DTYPE)],
        scratch_shapes=[pltpu.VMEM((ROW_TILE, 512), MXU_DTYPE)],
        compiler_params=pltpu.CompilerParams(dimension_semantics=("arbitrary", "arbitrary", "arbitrary")),
    )(z, z, bh, ba, dm)


N_CHUNKS = SEQ // HG_CHUNK
HG_STEP_CHUNKS = 4


def _hgrn_gates(q, fp, lb):
    C = HG_CHUNK
    sg = _sigmoid(fp)
    f = lb + (1.0 - lb) * sg
    lf = jnp.log(f)
    row = lax.broadcasted_iota(jnp.int32, (C, C), 0)
    col = lax.broadcasted_iota(jnp.int32, (C, C), 1)
    causal = row >= col
    G = _dot_f32(causal.astype(F32), lf)
    eG = jnp.exp(G)
    enG = jnp.exp(-G)
    qg = q * eG
    kg = (1.0 - f) * enG
    A = jnp.where(causal, _hdot(qg, kg, tb=True), 0.0)
    egl = jnp.exp(jnp.sum(lf, axis=0, keepdims=True))
    return sg, f, causal, eG, enG, qg, kg, A, egl


def hgrn_fwd(z, lb, gain, name):
    C, K = HG_CHUNK, HG_DIM

    def body(q_ref, f_ref, v_ref, og_ref, p_ref, g_ref, y_ref, o_ref, st_ref, state):
        @pl.when(pl.program_id(0) == 0)
        def _():
            state[...] = jnp.zeros_like(state)

        for cc in range(HG_STEP_CHUNKS):
            rows = pl.ds(cc * C, C)
            for h in range(HG_HEADS):
                hd = pl.ds(h * K, K)
                v = v_ref[rows, hd]
                _, _, _, _, _, qg, kg, A, egl = _hgrn_gates(q_ref[rows, hd], f_ref[rows, hd], p_ref[:, hd])
                st = state[h]
                st_ref[h, cc] = st
                o = _hdot(A, v) + _hdot(qg, st, tb=True)
                state[h] = st * egl + _hdot(v, kg * egl, ta=True)
                o_ref[rows, hd] = o
                rs = lax.rsqrt(jnp.mean(o * o, axis=-1, keepdims=True) + EPS)
                og = og_ref[rows, hd]
                y_ref[rows, hd] = (((o * rs) * g_ref[:, hd]) * (og * _sigmoid(og))).astype(y_ref.dtype)

    R = HG_STEP_CHUNKS * C

    def zcol(section):
        return pl.BlockSpec((R, HG_WIDTH), lambda c: (c, section))

    vec = pl.BlockSpec((1, HG_WIDTH), lambda c: (0, 0))
    blk = pl.BlockSpec((R, HG_WIDTH), lambda c: (c, 0))
    return pl.pallas_call(
        body, name=name, grid=(N_CHUNKS // HG_STEP_CHUNKS,),
        in_specs=[zcol(0), zcol(1), zcol(2), zcol(3), vec, vec],
        out_specs=[blk, blk, pl.BlockSpec((HG_HEADS, HG_STEP_CHUNKS, K, K), lambda c: (0, c, 0, 0))],
        out_shape=[jax.ShapeDtypeStruct((SEQ, HG_WIDTH), MXU_DTYPE), jax.ShapeDtypeStruct((SEQ, HG_WIDTH), F32),
                   jax.ShapeDtypeStruct((HG_HEADS, N_CHUNKS, K, K), F32)],
        scratch_shapes=[pltpu.VMEM((HG_HEADS, K, K), F32)],
        compiler_params=pltpu.CompilerParams(dimension_semantics=("arbitrary",)),
    )(z, z, z, z, lb, gain)


def hgrn_bwd(z, lb, gain, o_raw, states, dy, dz, name):
    C, K = HG_CHUNK, HG_DIM

    def compute(first_block, q_ref, f_ref, v_ref, og_ref, p_ref, g_ref, o_ref, st_ref, dy_ref,
                dq_ref, dfp_ref, dv_ref, dog_ref, dlb_ref, dgain_ref, dstate):
        @pl.when(first_block)
        def _():
            dstate[...] = jnp.zeros_like(dstate)
            dlb_ref[...] = jnp.zeros_like(dlb_ref)
            dgain_ref[...] = jnp.zeros_like(dgain_ref)

        last = lax.broadcasted_iota(jnp.int32, (C, K), 0) == C - 1
        row = lax.broadcasted_iota(jnp.int32, (C, C), 0)
        col = lax.broadcasted_iota(jnp.int32, (C, C), 1)
        anti_causal = (col >= row).astype(F32)
        for cc in reversed(range(HG_STEP_CHUNKS)):
            rows = pl.ds(cc * C, C)
            for h in range(HG_HEADS):
                hd = pl.ds(h * K, K)
                v = v_ref[rows, hd]
                lb = p_ref[:, hd]
                sg, f, causal, eG, enG, qg, kg, A, egl = _hgrn_gates(q_ref[rows, hd], f_ref[rows, hd], lb)
                kd = kg * egl
                st = st_ref[h, cc]
                dst = dstate[h]
                o = o_ref[rows, hd]
                og = og_ref[rows, hd]
                gain_v = g_ref[:, hd]
                dyv = dy_ref[rows, hd]
                rs = lax.rsqrt(jnp.mean(o * o, axis=-1, keepdims=True) + EPS)
                on = o * rs
                sgo = _sigmoid(og)
                silu = og * sgo
                dog_ref[rows, hd] = (dyv * (on * gain_v) * (sgo * (1.0 + og * (1.0 - sgo)))).astype(dog_ref.dtype)
                dgain_ref[:, hd] += jnp.sum(dyv * silu * on, axis=0, keepdims=True)
                don = dyv * gain_v * silu
                do = rs * (don - on * jnp.mean(don * on, axis=-1, keepdims=True))
                dA = jnp.where(causal, _hdot(do, v, tb=True), 0.0)
                dv_ref[rows, hd] = (_hdot(A, do, ta=True) + _hdot(kd, dst, tb=True)).astype(dv_ref.dtype)
                dqg = _hdot(dA, kg) + _hdot(do, st)
                dkg = _hdot(dA, qg, ta=True)
                dkd = _hdot(v, dst)
                dstate[h] = dst * egl + _hdot(do, qg, ta=True)
                dgl = jnp.sum(st * dst, axis=0, keepdims=True) * egl
                dq_ref[rows, hd] = (dqg * eG).astype(dq_ref.dtype)
                dk = dkg * enG + dkd * (enG * egl)
                dG = dqg * qg - dkg * kg - dkd * kd
                extra = jnp.sum(dkd * kd, axis=0, keepdims=True) + dgl
                dG = dG + jnp.where(last, extra, 0.0)
                dlf = _dot_f32(anti_causal, dG)
                df = dlf / f - dk
                dfp_ref[rows, hd] = (df * (1.0 - lb) * (sg * (1.0 - sg))).astype(dfp_ref.dtype)
                dlb_ref[:, hd] += jnp.sum(df * (1.0 - sg), axis=0, keepdims=True)

    R = HG_STEP_CHUNKS * C
    n_steps = N_CHUNKS // HG_STEP_CHUNKS

    def body(q_ref, f_ref, v_ref, og_ref, p_ref, g_ref, o_ref, st_ref, dy_ref, dz_in_ref,
             dz_ref, dlb_ref, dgain_ref, dstate, parts):
        first_block, t = pl.program_id(0) == 0, pl.program_id(1)

        @pl.when(t == 0)
        def _():
            compute(first_block, q_ref, f_ref, v_ref, og_ref, p_ref, g_ref, o_ref, st_ref, dy_ref,
                    parts.at[0], parts.at[1], parts.at[2], parts.at[3], dlb_ref, dgain_ref, dstate)

        dz_ref[...] = parts[t]

    def rc(c):
        return n_steps - 1 - c

    def zcol(section):
        return pl.BlockSpec((R, HG_WIDTH), lambda c, t: (rc(c), section))

    vec = pl.BlockSpec((1, HG_WIDTH), lambda c, t: (0, 0))
    blk = pl.BlockSpec((R, HG_WIDTH), lambda c, t: (rc(c), 0))
    small = jax.ShapeDtypeStruct((1, HG_WIDTH), F32)
    return pl.pallas_call(
        body, name=name, grid=(n_steps, 4),
        in_specs=[zcol(0), zcol(1), zcol(2), zcol(3), vec, vec, blk,
                  pl.BlockSpec((HG_HEADS, HG_STEP_CHUNKS, K, K), lambda c, t: (0, rc(c), 0, 0)), blk,
                  pl.BlockSpec(memory_space=pl.ANY)],
        out_specs=[pl.BlockSpec((R, HG_WIDTH), lambda c, t: (rc(c), t)), vec, vec],
        out_shape=[jax.ShapeDtypeStruct(dz.shape, dz.dtype), small, small],
        input_output_aliases={9: 0},
        scratch_shapes=[pltpu.VMEM((HG_HEADS, K, K), F32), pltpu.VMEM((4, R, HG_WIDTH), MXU_DTYPE)],
        compiler_params=pltpu.CompilerParams(dimension_semantics=("arbitrary", "arbitrary")),
    )(z, z, z, z, lb, gain, o_raw, states, dy, dz)


N_GROUPS = len(ATT_GROUPS)
HEAD_PAIRS = ATT_WIDTH // 128
ATT_COL0 = 4 * HG_WIDTH
UNROLLED_UNITS = 4


def _alibi_coef():
    n = N_GROUPS * ATT_HEADS
    slopes = np.exp2(-ALIBI_MAX * np.arange(1, n + 1, dtype=np.float32) / n).astype(np.float32)
    dil = np.repeat(np.array([d for _, d in ATT_GROUPS], np.float32), ATT_HEADS)
    return jnp.asarray(slopes * dil, F32)


def _for_each_unit(n, fn):
    if n <= UNROLLED_UNITS:
        for u in range(n):
            fn(u)
    else:
        def group(i, carry):
            for j in range(UNROLLED_UNITS):
                fn(i * UNROLLED_UNITS + j)
            return carry
        lax.fori_loop(0, n // UNROLLED_UNITS, group, 0)


def _att_geometry(g):
    B = ATT_BLOCK
    d = ATT_GROUPS[g][1]
    n_blocks = SEQ // (d * B)
    col0 = (ATT_COL0 + g * 3 * ATT_WIDTH) // 128

    def block_rows(b, r):
        return pl.ds(b * (B * d) + r, B, stride=d) if d > 1 else pl.ds(pl.multiple_of(b * B, B), B)

    def block_of(u):
        return (u, 0) if d == 1 else (u // d, u % d)

    return d, n_blocks, col0, block_rows, block_of


def _att_column(c):
    return pl.BlockSpec((SEQ, 128), lambda hp: (0, c + hp))


def _head_lanes(j):
    lane = lax.broadcasted_iota(jnp.int32, (ATT_BLOCK, 128), 1)
    return (lane >= 64 * j) & (lane < 64 * (j + 1))


def _stack_heads(x, sel0):
    return jnp.concatenate([jnp.where(sel0, x, 0.0), jnp.where(sel0, 0.0, x)], axis=0)


def _stack_values(x, sel0, lanes):
    swapped = pltpu.roll(x, 64, 1)
    stacked = jnp.concatenate([jnp.where(sel0, x, swapped), jnp.where(sel0, swapped, x)], axis=0)
    return stacked if lanes == 128 else jnp.concatenate([stacked] * (lanes // 128), axis=1)


def _pair_coef(coef_ref, g, hp):
    row = lax.broadcasted_iota(jnp.int32, (2 * ATT_BLOCK, 1), 0)
    first = g * ATT_HEADS + hp * 2
    return jnp.where(row < ATT_BLOCK, coef_ref[first], coef_ref[first + 1])


def _band(with_prev, first_key):
    B = ATT_BLOCK
    keys = 2 * B if with_prev else B
    qi = jnp.bitwise_and(lax.broadcasted_iota(jnp.int32, (2 * B, keys), 0), B - 1)
    kj = lax.broadcasted_iota(jnp.int32, (2 * B, keys), 1)
    delta = qi + (B if with_prev else 0) - kj
    valid = (delta >= 0) & (delta <= B)
    if with_prev:
        valid = valid & (kj >= first_key)
    return valid, delta.astype(F32)


def att_fwd(z, g, name):
    B = ATT_BLOCK
    d, n_blocks, col0, block_rows, block_of = _att_geometry(g)
    multi = n_blocks > 1

    def body(coef_ref, q_ref, k_ref, v_ref, o_ref, l_ref):
        cf2 = _pair_coef(coef_ref, g, pl.program_id(0))
        sel0 = _head_lanes(0)

        def one(u):
            b, r = block_of(u)
            rows = block_rows(b, r)
            valid, dist = _band(multi, jnp.where(b == 0, B, 0))
            q2 = _stack_heads(q_ref[rows, :], sel0)
            kk, vv = k_ref[rows, :], v_ref[rows, :]
            if multi:
                prev_rows = block_rows(jnp.maximum(b - 1, 0), r)
                kk = jnp.concatenate([k_ref[prev_rows, :], kk], axis=0)
                vv = jnp.concatenate([v_ref[prev_rows, :], vv], axis=0)
            sc = jnp.where(valid, _dot(q2, kk, tb=True) * 0.125 - cf2 * dist, NEG_INF)
            mx = jnp.max(sc, axis=-1, keepdims=True)
            e = jnp.exp(sc - mx)
            den = jnp.sum(e, axis=-1, keepdims=True)
            o2 = _dot(e * (1.0 / den), vv)
            lse2 = mx + jnp.log(den)
            o_ref[rows, :] = jnp.where(sel0, o2[:B], o2[B:])
            l_ref[rows, :] = jnp.where(sel0, lse2[:B], lse2[B:])

        _for_each_unit(d * n_blocks, one)

    out = jax.ShapeDtypeStruct((SEQ, ATT_WIDTH), F32)
    return pl.pallas_call(
        body, name=name, grid=(HEAD_PAIRS,),
        in_specs=[pl.BlockSpec(memory_space=pltpu.SMEM), _att_column(col0), _att_column(col0 + 4), _att_column(col0 + 8)],
        out_specs=[_att_column(0), _att_column(0)], out_shape=[out, out],
        compiler_params=pltpu.CompilerParams(dimension_semantics=("parallel",)),
    )(_alibi_coef(), z, z, z)


def att_bwd(z, l, do, corr, dz, g, name):
    B = ATT_BLOCK
    d, n_blocks, col0, block_rows, block_of = _att_geometry(g)
    multi = n_blocks > 1
    own = slice(B, 2 * B) if multi else slice(0, B)

    def body(coef_ref, q_ref, k_ref, v_ref, l_ref, do_ref, cr_ref, dz_in_ref, dz_ref, parts):
        hp, t = pl.program_id(0), pl.program_id(1)

        @pl.when(t == 0)
        def _():
            compute(hp, coef_ref, q_ref, k_ref, v_ref, l_ref, do_ref, cr_ref, parts.at[0], parts.at[1], parts.at[2])

        dz_ref[...] = parts[t].astype(dz_ref.dtype)

    def compute(hp, coef_ref, q_ref, k_ref, v_ref, l_ref, do_ref, cr_ref, dq_sc, dk_sc, dv_sc):
        cf2 = _pair_coef(coef_ref, g, hp)
        sel0 = _head_lanes(0)

        def one(u):
            b, r = block_of(u)
            rows = block_rows(b, r)
            valid, dist = _band(multi, jnp.where(b == 0, B, 0))
            kk, vv = k_ref[rows, :], v_ref[rows, :]
            if multi:
                prev_rows = block_rows(jnp.maximum(b - 1, 0), r)
                kk = jnp.concatenate([k_ref[prev_rows, :], kk], axis=0)
                vv = jnp.concatenate([v_ref[prev_rows, :], vv], axis=0)
            q2, do2 = _stack_heads(q_ref[rows, :], sel0), _stack_heads(do_ref[rows, :], sel0)
            keys = kk.shape[0]
            lse2, cr2 = _stack_values(l_ref[rows, :], sel0, keys), _stack_values(cr_ref[rows, :], sel0, keys)
            p = jnp.exp(jnp.where(valid, _dot(q2, kk, tb=True) * 0.125 - cf2 * dist, NEG_INF) - lse2)
            ds = p * (_dot(do2, vv, tb=True) + cr2)
            dq2 = _dot(ds, kk)
            dkk = _dot(ds, q2, ta=True) * 0.125
            dvv = _dot(p, do2, ta=True)
            dq_sc[rows, :] = jnp.where(sel0, dq2[:B], dq2[B:]) * 0.125
            dk_sc[rows, :] = dkk[own]
            dv_sc[rows, :] = dvv[own]
            if multi:
                dk_sc[prev_rows, :] += dkk[:B]
                dv_sc[prev_rows, :] += dvv[:B]

        _for_each_unit(d * n_blocks, one)

    def col(c):
        return pl.BlockSpec((SEQ, 128), lambda hp, t: (0, c + hp))

    return pl.pallas_call(
        body, name=name, grid=(HEAD_PAIRS, 3),
        in_specs=[pl.BlockSpec(memory_space=pltpu.SMEM), col(col0), col(col0 + 4), col(col0 + 8), col(0), col(0), col(0),
                  pl.BlockSpec(memory_space=pl.ANY)],
        out_specs=pl.BlockSpec((SEQ, 128), lambda hp, t: (0, col0 + 4 * t + hp)),
        out_shape=jax.ShapeDtypeStruct(dz.shape, dz.dtype), input_output_aliases={7: 0},
        scratch_shapes=[pltpu.VMEM((3, SEQ, 128), F32)],
        compiler_params=pltpu.CompilerParams(dimension_semantics=("arbitrary", "arbitrary"),
                                             vmem_limit_bytes=MATMUL_VMEM_BYTES),
    )(_alibi_coef(), z, z, z, l, do, corr, dz)


def _head_sum(x):
    i = lax.broadcasted_iota(jnp.int32, (128, 128), 0) // 64
    j = lax.broadcasted_iota(jnp.int32, (128, 128), 1) // 64
    return _dot_f32(x, (i == j).astype(F32), ones_on_right=True)


def _group_weights(l0, l1, l2):
    mx = jnp.maximum(jnp.maximum(l0, l1), l2)
    e0, e1, e2 = jnp.exp(l0 - mx), jnp.exp(l1 - mx), jnp.exp(l2 - mx)
    inv = 1.0 / (e0 + e1 + e2)
    return e0 * inv, e1 * inv, e2 * inv


def att_combine_fwd(o, l, name):
    def body(o0, o1, o2, l0, l1, l2, y_ref):
        w0, w1, w2 = _group_weights(l0[...], l1[...], l2[...])
        y_ref[...] = (o0[...] * w0 + o1[...] * w1 + o2[...] * w2).astype(y_ref.dtype)

    blk = pl.BlockSpec((ROW_TILE, ATT_WIDTH), lambda i: (i, 0))
    return pl.pallas_call(
        body, name=name, grid=(SEQ // ROW_TILE,), in_specs=[blk] * 6, out_specs=blk,
        out_shape=jax.ShapeDtypeStruct((SEQ, ATT_WIDTH), MXU_DTYPE),
    )(*o, *l)


def att_combine_bwd(o, l, dy, name):
    def body(o0, o1, o2, l0, l1, l2, dy_ref, do0, do1, do2, cr0, cr1, cr2):
        w = _group_weights(l0[...], l1[...], l2[...])
        dyv = dy_ref[...]
        tot = _head_sum(dyv * (w[0] * o0[...] + w[1] * o1[...] + w[2] * o2[...]))
        for g, (do_ref, cr_ref) in enumerate(((do0, cr0), (do1, cr1), (do2, cr2))):
            do_ref[...] = dyv * w[g]
            cr_ref[...] = -w[g] * tot

    blk = pl.BlockSpec((ROW_TILE, 128), lambda i, j: (i, j))
    out = jax.ShapeDtypeStruct((SEQ, ATT_WIDTH), F32)
    res = pl.pallas_call(
        body, name=name, grid=(SEQ // ROW_TILE, HEAD_PAIRS), in_specs=[blk] * 7, out_specs=[blk] * 6, out_shape=[out] * 6,
    )(*o, *l, dy)
    return res[:N_GROUPS], res[N_GROUPS:]


SUM_ROW_TILES = (1024, 512, 256, 128, 64, 32, 16)
SUM_TILE_BYTES = 24 * 1024 * 1024
SUM_PARAMS = pltpu.CompilerParams(vmem_limit_bytes=MATMUL_VMEM_BYTES)


def _row_tile(rows, cols, operands):
    fit = [t for t in SUM_ROW_TILES if rows % t == 0]
    return next((t for t in fit if 2 * 4 * operands * t * cols <= SUM_TILE_BYTES), fit[-1])


def _shard_shape(rows, cols, axis):
    return (rows // N_CHIPS, cols) if axis == 0 else (rows, cols // N_CHIPS)


def _half_shape(rows, cols, axis):
    return (rows, cols // 2) if axis == 0 else (rows // 2, cols)


def _piece_shape(rows, cols, axis):
    return (rows // N_CHIPS, cols // 2) if axis == 0 else (rows // 2, cols // N_CHIPS)


def place_own_block(shard, chip, rows, cols, axis, name):
    sr, sc = _shard_shape(rows, cols, axis)
    tr = _row_tile(sr, sc, 2)

    def body(chip_ref, s_ref, o_ref):
        o_ref[...] = s_ref[...].astype(o_ref.dtype)

    if axis == 0:
        out_map = lambda i, chip_ref: (chip_ref[0] * (sr // tr) + i, 0)
    else:
        out_map = lambda i, chip_ref: (i, chip_ref[0])
    return pl.pallas_call(
        body, name=name, out_shape=jax.ShapeDtypeStruct((rows, cols), WEIGHT_COMM_DTYPE), compiler_params=SUM_PARAMS,
        grid_spec=pltpu.PrefetchScalarGridSpec(
            num_scalar_prefetch=1, grid=(sr // tr,), in_specs=[pl.BlockSpec((tr, sc), lambda i, chip_ref: (i, 0))],
            out_specs=pl.BlockSpec((tr, sc), out_map)),
    )(chip, shard)


def add_halves(g, theirs, core, rows, cols, axis, name):
    hr, hc = _half_shape(rows, cols, axis)
    tr = _row_tile(hr, hc, 3)

    def body(core_ref, g_ref, t_ref, o_ref):
        o_ref[...] = (g_ref[...].astype(F32) + t_ref[...].astype(F32)).astype(o_ref.dtype)

    if axis == 0:
        g_map = lambda i, core_ref: (i, core_ref[0])
    else:
        g_map = lambda i, core_ref: (core_ref[0] * (hr // tr) + i, 0)
    blk = pl.BlockSpec((tr, hc), lambda i, core_ref: (i, 0))
    return pl.pallas_call(
        body, name=name, out_shape=jax.ShapeDtypeStruct((hr, hc), GRAD_COMM_DTYPE), compiler_params=SUM_PARAMS,
        grid_spec=pltpu.PrefetchScalarGridSpec(
            num_scalar_prefetch=1, grid=(hr // tr,), in_specs=[pl.BlockSpec((tr, hc), g_map), blk], out_specs=blk),
    )(core, g, theirs)


def add_pieces(half, got, chip, rows, cols, axis, name):
    hr, _ = _half_shape(rows, cols, axis)
    pr, pc = _piece_shape(rows, cols, axis)
    tr = _row_tile(pr, pc, 5)

    def body(chip_ref, h_ref, got_ref, o_ref):
        o_ref[...] = (h_ref[...].astype(F32) + got_ref[0].astype(F32) + got_ref[1].astype(F32) + got_ref[2].astype(F32))

    if axis == 0:
        h_map = lambda i, chip_ref: (chip_ref[0] * (pr // tr) + i, 0)
    else:
        h_map = lambda i, chip_ref: (i, chip_ref[0])
    return pl.pallas_call(
        body, name=name, out_shape=jax.ShapeDtypeStruct((pr, pc), F32), compiler_params=SUM_PARAMS,
        grid_spec=pltpu.PrefetchScalarGridSpec(
            num_scalar_prefetch=1, grid=(pr // tr,),
            in_specs=[pl.BlockSpec((tr, pc), h_map), pl.BlockSpec((3, tr, pc), lambda i, chip_ref: (0, i, 0))],
            out_specs=pl.BlockSpec((tr, pc), lambda i, chip_ref: (i, 0))),
    )(chip, half, got)


def _adamw_math(w, g, m, v):
    nm = ADAM_B1 * m + (1.0 - ADAM_B1) * g
    nv = ADAM_B2 * v + (1.0 - ADAM_B2) * (g * g)
    m_hat = nm / (1.0 - ADAM_B1 ** ADAM_STEP)
    v_hat = nv / (1.0 - ADAM_B2 ** ADAM_STEP)
    return -ADAM_LR * (m_hat / (jnp.sqrt(v_hat) + ADAM_EPS) + ADAM_WD * w), nm, nv


def adamw_halves(w, mine, theirs, m, v, core, rows, cols, axis, name):
    sr, sc = _shard_shape(rows, cols, axis)
    pr, pc = _piece_shape(rows, cols, axis)
    tr = _row_tile(pr, pc, 9)
    nt = pr // tr

    def body(core_ref, w_ref, a_ref, b_ref, m_ref, v_ref, g_ref, d_ref, nm_ref, nv_ref):
        g = jnp.where(pl.program_id(0) == core_ref[0], a_ref[...], b_ref[...])
        g_ref[...] = g
        d_ref[...], nm_ref[...], nv_ref[...] = _adamw_math(w_ref[...], g, m_ref[...], v_ref[...])

    if axis == 0:
        full = pl.BlockSpec((tr, pc), lambda h, i, core_ref: (i, h))
    else:
        full = pl.BlockSpec((tr, pc), lambda h, i, core_ref: (h * nt + i, 0))
    part = pl.BlockSpec((tr, pc), lambda h, i, core_ref: (i, 0))
    out = jax.ShapeDtypeStruct((sr, sc), F32)
    return pl.pallas_call(
        body, name=name, out_shape=[out, out, out, out], compiler_params=SUM_PARAMS,
        grid_spec=pltpu.PrefetchScalarGridSpec(
            num_scalar_prefetch=1, grid=(2, nt), in_specs=[full, part, part, full, full], out_specs=[full] * 4),
    )(core, w, mine, theirs, m, v)


BIG = (
    ("ffn1_w_gate_up", D_MODEL, 2 * D_FF, 1),
    ("ffn1_w_down", D_FF, D_MODEL, 0),
    ("w_in", D_MODEL, IN_COLS, 1),
    ("w_branch_hg", HG_WIDTH, D_MODEL, 1),
    ("w_branch_att", ATT_WIDTH, D_MODEL, 1),
    ("w_out", D_MODEL, D_MODEL, 0),
    ("ffn2_w_gate_up", D_MODEL, 2 * D_FF, 1),
    ("ffn2_w_down", D_FF, D_MODEL, 0),
)
N_BIG = len(BIG)
ANY = pl.BlockSpec(memory_space=pl.ANY)


def _place():
    return lax.axis_index("x"), lax.axis_index("y"), lax.axis_index("c")


def _other_chips(x, y):
    return ((1 - x, y), (x, 1 - y), (1 - x, 1 - y))


MAX_COPY_CHUNKS = 16
CHUNK_ROW_ALIGN = 16


def _row_chunks(view):
    rows = view.shape[0]
    n = next(n for n in range(MAX_COPY_CHUNKS, 0, -1) if rows % (CHUNK_ROW_ALIGN * n) == 0 or n == 1)
    step = rows // n
    return [pl.ds(i * step, step) for i in range(n)]


def _remote(src, dst, send_sem, recv_sem, device):
    return pltpu.make_async_remote_copy(src_ref=src, dst_ref=dst, send_sem=send_sem, recv_sem=recv_sem,
                                        device_id=device, device_id_type=MESH)


def _start_remote(src, dst, send_sem, recv_sem, device):
    for rows in _row_chunks(src):
        _remote(src.at[rows, :], dst.at[rows, :], send_sem, recv_sem, device).start()
    return _remote(src, dst, send_sem, recv_sem, device)


HBM = pl.BlockSpec(memory_space=pltpu.HBM)
SEM = pl.BlockSpec(memory_space=pltpu.SEMAPHORE)
SPLIT_COPY_EFFECT = pltpu.SideEffectType.DATAFLOW_SIDE_EFFECTING
GROUPS = {"ffn1": (0, 1), "mix": (2, 3, 4, 5), "ffn2": (6, 7)}


def _in_hbm(a):
    return pltpu.with_memory_space_constraint(a, pltpu.HBM)


class _SemList:
    def __init__(self, refs):
        self.refs = refs
        self.at = self

    def __getitem__(self, index):
        w, k = index
        return self.refs[3 * w + k]


def _gather_piece(ref, rows, cols, axis, chip, c):
    sr, sc = _shard_shape(rows, cols, axis)
    j = 2 * chip[0] + chip[1]
    if axis == 0:
        return ref.at[pl.ds(j * sr + c * (sr // 2), sr // 2), :]
    return ref.at[pl.ds(c * (sr // 2), sr // 2), pl.ds(pl.multiple_of(j * sc, 128), sc)]


def _start_gather_sends(bufs, ws, send_sems, recv_sems):
    x, y, c = _place()
    for w, (_, r, cc, ax) in enumerate(ws):
        mine = _gather_piece(bufs[w], r, cc, ax, (x, y), c)
        for k, chip in enumerate(_other_chips(x, y)):
            _start_remote(mine, mine, send_sems.at[w, k], recv_sems.at[w, k], (*chip, c))


def _wait_gather_sends(bufs, ws, send_sems, recv_sems):
    x, y, c = _place()
    for w, (_, r, cc, ax) in enumerate(ws):
        for k, chip in enumerate(_other_chips(x, y)):
            got = _gather_piece(bufs[w], r, cc, ax, chip, c)
            _remote(got, got, send_sems.at[w, k], recv_sems.at[w, k], (x, y, c)).wait_recv()
    for w, (_, r, cc, ax) in enumerate(ws):
        mine = _gather_piece(bufs[w], r, cc, ax, (x, y), c)
        for k in range(3):
            _remote(mine, mine, send_sems.at[w, k], recv_sems.at[w, k], (x, y, c)).wait_send()


def _forward_halves(bufs, ws, send_sems, recv_sems):
    x, y, c = _place()
    passed = []
    for w, (_, r, cc, ax) in enumerate(ws):
        for k, chip in enumerate(_other_chips(x, y)):
            got = _gather_piece(bufs[w], r, cc, ax, chip, c)
            passed.append(_start_remote(got, got, send_sems.at[w, k], recv_sems.at[w, k], (x, y, 1 - c)))
    for w, (_, r, cc, ax) in enumerate(ws):
        for k, chip in enumerate(_other_chips(x, y)):
            got = _gather_piece(bufs[w], r, cc, ax, chip, 1 - c)
            _remote(got, got, send_sems.at[w, k], recv_sems.at[w, k], (x, y, c)).wait_recv()
    for cp in passed:
        cp.wait_send()


def gather_start(placed, after, group):
    ws = [BIG[i] for i in GROUPS[group]]
    n = len(ws)

    def body(*refs):
        bufs = refs[:n]
        send_sems, recv_sems = _SemList(refs[n + 1:4 * n + 1]), _SemList(refs[4 * n + 1:7 * n + 1])
        token = refs[-1]
        _start_gather_sends(bufs, ws, send_sems, recv_sems)
        token[...] = jnp.zeros_like(token)

    out = pl.pallas_call(
        body, name=f"gather_start_{group}", in_specs=[HBM] * n + [ANY],
        out_specs=[SEM] * (6 * n) + [HBM] * n + [pl.BlockSpec(memory_space=pltpu.VMEM)],
        out_shape=[pltpu.SemaphoreType.DMA(())] * (6 * n)
        + [pltpu.HBM((r, cc), WEIGHT_COMM_DTYPE) for _, r, cc, _ in ws] + [jax.ShapeDtypeStruct((8, 128), F32)],
        input_output_aliases={w: 6 * n + w for w in range(n)},
        compiler_params=pltpu.CompilerParams(has_side_effects=SPLIT_COPY_EFFECT),
    )(*[_in_hbm(p) for p in placed], after)
    return out[:3 * n], out[3 * n:6 * n], out[6 * n:7 * n], out[-1]


def gather_wait(bufs, send_sems, recv_sems, after, group):
    ws = [BIG[i] for i in GROUPS[group]]
    n = len(ws)

    def body(*refs):
        _wait_gather_sends(refs[:n], ws, _SemList(refs[n:n + 3 * n]), _SemList(refs[n + 3 * n:n + 6 * n]))

    return pl.pallas_call(
        body, name=f"gather_wait_{group}", in_specs=[HBM] * n + [SEM] * (6 * n) + [ANY] * len(after), out_specs=[HBM] * n,
        out_shape=[pltpu.HBM((r, cc), WEIGHT_COMM_DTYPE) for _, r, cc, _ in ws],
        input_output_aliases={w: w for w in range(n)},
        compiler_params=pltpu.CompilerParams(has_side_effects=SPLIT_COPY_EFFECT),
    )(*bufs, *send_sems, *recv_sems, *after)


def gather_forward(bufs, group):
    ws = [BIG[i] for i in GROUPS[group]]
    n = len(ws)

    def body(*refs):
        _forward_halves(refs[n:2 * n], ws, refs[2 * n], refs[2 * n + 1])

    return pl.pallas_call(
        body, name=f"gather_forward_{group}", in_specs=[ANY] * n, out_specs=[ANY] * n,
        out_shape=[jax.ShapeDtypeStruct((r, cc), WEIGHT_COMM_DTYPE) for _, r, cc, _ in ws],
        input_output_aliases={w: w for w in range(n)},
        scratch_shapes=[pltpu.SemaphoreType.DMA((n, 3))] * 2,
    )(*bufs)


def _half(ref, rows, cols, axis, c):
    if axis == 0:
        return ref.at[:, pl.ds(pl.multiple_of(c * (cols // 2), 128), cols // 2)]
    return ref.at[pl.ds(c * (rows // 2), rows // 2), :]


def _piece_of_half(ref, rows, cols, axis, chip):
    j = 2 * chip[0] + chip[1]
    pr, pc = _piece_shape(rows, cols, axis)
    if axis == 0:
        return ref.at[pl.ds(j * pr, pr), :]
    return ref.at[:, pl.ds(pl.multiple_of(j * pc, 128), pc)]


def sibling_exchange_start(srcs, view, landing_shapes, dtype, name):
    n = len(srcs)

    def body(*refs):
        ins, land, sems = refs[:n], refs[n:2 * n], refs[2 * n:4 * n]
        x, y, c = _place()
        for w in range(n):
            _start_remote(view(ins[w], w, c), land[w], sems[w], sems[n + w], (x, y, 1 - c))
        refs[-1][...] = jnp.zeros_like(refs[-1])

    landing = [lax.empty(shape, dtype) for shape in landing_shapes]
    out = pl.pallas_call(
        body, name=name, in_specs=[HBM] * (2 * n),
        out_specs=[SEM] * (2 * n) + [HBM] * (2 * n) + [pl.BlockSpec(memory_space=pltpu.VMEM)],
        out_shape=[pltpu.SemaphoreType.DMA(())] * (2 * n) + [pltpu.HBM(a.shape, a.dtype) for a in srcs]
        + [pltpu.HBM(shape, dtype) for shape in landing_shapes] + [jax.ShapeDtypeStruct((8, 128), F32)],
        input_output_aliases={i: 2 * n + i for i in range(2 * n)},
        compiler_params=pltpu.CompilerParams(has_side_effects=SPLIT_COPY_EFFECT),
    )(*[_in_hbm(a) for a in srcs], *[_in_hbm(b) for b in landing])
    return out[:n], out[n:2 * n], out[2 * n:3 * n], out[3 * n:4 * n], out[-1]


def sibling_exchange_wait(srcs, landing, send_sems, recv_sems, view, after, name):
    n = len(srcs)

    def body(*refs):
        ins, land, sems = refs[:n], refs[n:2 * n], refs[2 * n:4 * n]
        x, y, c = _place()
        for w in range(n):
            cp = _remote(view(ins[w], w, c), land[w], sems[w], sems[n + w], (x, y, c))
            cp.wait_send()
            cp.wait_recv()

    out = pl.pallas_call(
        body, name=name, in_specs=[HBM] * (2 * n) + [SEM] * (2 * n) + [ANY] * len(after), out_specs=[HBM] * (2 * n),
        out_shape=[pltpu.HBM(a.shape, a.dtype) for a in srcs] + [pltpu.HBM(b.shape, b.dtype) for b in landing],
        input_output_aliases={i: i for i in range(2 * n)},
        compiler_params=pltpu.CompilerParams(has_side_effects=SPLIT_COPY_EFFECT),
    )(*srcs, *landing, *send_sems, *recv_sems, *after)
    return out[:n], out[n:]


def _scatter_copies(halves, got, ws, send_sems, recv_sems, start):
    x, y, c = _place()
    copies = []
    for w, (_, r, cc, ax) in enumerate(ws):
        for k, chip in enumerate(_other_chips(x, y)):
            args = (_piece_of_half(halves[w], r, cc, ax, chip), got[w].at[k], send_sems.at[w, k], recv_sems.at[w, k], (*chip, c))
            copies.append(_start_remote(*args) if start else _remote(*args))
    return copies


def scatter_start(halves, group):
    ws = [BIG[i] for i in GROUPS[group]]
    n = len(ws)

    def body(*refs):
        sems = refs[2 * n:8 * n]
        _scatter_copies(refs[:n], refs[n:2 * n], ws, _SemList(sems[:3 * n]), _SemList(sems[3 * n:]), start=True)
        refs[-1][...] = jnp.zeros_like(refs[-1])

    landing = [lax.empty((3,) + _piece_shape(r, cc, ax), GRAD_COMM_DTYPE) for _, r, cc, ax in ws]
    out = pl.pallas_call(
        body, name=f"scatter_start_{group}", in_specs=[HBM] * (2 * n),
        out_specs=[SEM] * (6 * n) + [HBM] * (2 * n) + [pl.BlockSpec(memory_space=pltpu.VMEM)],
        out_shape=[pltpu.SemaphoreType.DMA(())] * (6 * n)
        + [pltpu.HBM(_half_shape(r, cc, ax), GRAD_COMM_DTYPE) for _, r, cc, ax in ws]
        + [pltpu.HBM((3,) + _piece_shape(r, cc, ax), GRAD_COMM_DTYPE) for _, r, cc, ax in ws]
        + [jax.ShapeDtypeStruct((8, 128), F32)],
        input_output_aliases={i: 6 * n + i for i in range(2 * n)},
        compiler_params=pltpu.CompilerParams(has_side_effects=SPLIT_COPY_EFFECT),
    )(*[_in_hbm(h) for h in halves], *[_in_hbm(b) for b in landing])
    return out[:3 * n], out[3 * n:6 * n], out[6 * n:7 * n], out[7 * n:8 * n], out[-1]


def scatter_wait(halves, got, send_sems, recv_sems, after, group):
    ws = [BIG[i] for i in GROUPS[group]]
    n = len(ws)

    def body(*refs):
        sems = refs[2 * n:8 * n]
        for cp in _scatter_copies(refs[:n], refs[n:2 * n], ws, _SemList(sems[:3 * n]), _SemList(sems[3 * n:]), start=False):
            cp.wait_send()
            cp.wait_recv()

    out = pl.pallas_call(
        body, name=f"scatter_wait_{group}", in_specs=[HBM] * (2 * n) + [SEM] * (6 * n) + [ANY] * len(after),
        out_specs=[HBM] * (2 * n),
        out_shape=[pltpu.HBM(_half_shape(r, cc, ax), GRAD_COMM_DTYPE) for _, r, cc, ax in ws]
        + [pltpu.HBM((3,) + _piece_shape(r, cc, ax), GRAD_COMM_DTYPE) for _, r, cc, ax in ws],
        input_output_aliases={i: i for i in range(2 * n)},
        compiler_params=pltpu.CompilerParams(has_side_effects=SPLIT_COPY_EFFECT),
    )(*halves, *got, *send_sems, *recv_sems, *after)
    return out[:n], out[n:]


N_DEV = 8
SMALL = ("ffn1_norm", "mix_norm", "hg_lower_bounds", "hg_out_norm", "ffn2_norm", "final_norm")
SMALL_STAGE_ROWS = 8


def small_step(loss, grads, w, m, v, behind):
    n = len(SMALL)
    shapes = [g.shape for g in grads]
    first_row = [sum(s[0] for s in shapes[:i]) for i in range(n + 1)]
    assert first_row[n] < SMALL_STAGE_ROWS
    loss_row = (pl.ds(first_row[n], 1), pl.ds(0, loss.shape[1]))

    def body(*refs):
        loss_ref, g_refs, w_refs, m_refs, v_refs = refs[0], refs[1:1 + n], refs[1 + n:1 + 2 * n], refs[1 + 2 * n:1 + 3 * n], refs[1 + 3 * n:1 + 4 * n]
        outs = refs[2 + 4 * n:3 + 8 * n]
        loss_out, dg_refs, d_refs, nm_refs, nv_refs = outs[0], outs[1:1 + n], outs[1 + n:1 + 2 * n], outs[1 + 2 * n:1 + 3 * n], outs[1 + 3 * n:]
        stage, gathered, send_sems, recv_sems = refs[3 + 8 * n:]
        x, y, c = _place()
        me = 4 * x + 2 * y + c

        def slot(i, shape):
            return pl.ds(first_row[i], shape[0]), pl.ds(0, shape[1])

        stage[...] = jnp.zeros_like(stage)
        for i, g_ref in enumerate(g_refs):
            stage[slot(i, shapes[i])] = g_ref[...]
        stage[loss_row] = loss_ref[pl.ds(0, 1), :]
        gathered[me] = stage[...]
        copies = []
        for k in range(1, N_DEV):
            peer = (x ^ (k >> 2), y ^ ((k >> 1) & 1), c ^ (k & 1))
            cp = pltpu.make_async_remote_copy(
                src_ref=stage, dst_ref=gathered.at[me], send_sem=send_sems.at[k - 1], recv_sem=recv_sems.at[k - 1],
                device_id=peer, device_id_type=MESH)
            cp.start()
            copies.append(cp)
        for cp in copies:
            cp.wait()
        acc = gathered[0]
        for k in range(1, N_DEV):
            acc = acc + gathered[k]
        stage[...] = acc
        loss_out[...] = jnp.broadcast_to(stage[loss_row], loss_out.shape)
        for i in range(n):
            g = stage[slot(i, shapes[i])]
            dg_refs[i][...] = g
            d_refs[i][...], nm_refs[i][...], nv_refs[i][...] = _adamw_math(w_refs[i][...], g, m_refs[i][...], v_refs[i][...])

    vm = pl.BlockSpec(memory_space=pltpu.VMEM)
    per_param = [jax.ShapeDtypeStruct(s, F32) for s in shapes]
    out = pl.pallas_call(
        body, name="small_step", in_specs=[vm] * (1 + 4 * n) + [ANY], out_specs=[vm] * (1 + 4 * n),
        out_shape=[jax.ShapeDtypeStruct(loss.shape, F32)] + per_param * 4,
        scratch_shapes=[pltpu.VMEM((SMALL_STAGE_ROWS, D_MODEL), F32),
                        pltpu.VMEM((N_DEV, SMALL_STAGE_ROWS, D_MODEL), F32),
                        pltpu.SemaphoreType.DMA((N_DEV - 1,)), pltpu.SemaphoreType.DMA((N_DEV - 1,))],
    )(loss, *grads, *w, *m, *v, behind)
    return out[0], out[1:1 + n], out[1 + n:1 + 2 * n], out[1 + 2 * n:1 + 3 * n], out[1 + 3 * n:]


def _swiglu_block_fwd(h, norm_g, w_gu, w_down, tag, behind=()):
    n = rmsnorm_fwd(h, norm_g, f"{tag}_norm", behind=behind)
    a, b, s = gate_up_swiglu(n, w_gu, f"{tag}_gate_up")
    h_out = matmul(s, w_down, res=h, scale=0.5, name=f"{tag}_down")
    return h_out, (n, a, b, s)


def _swiglu_block_bwd(h, norm_g, w_gu, w_down, saved, dh_out, df, tag, exchange, behind=()):
    n, a, b, s = saved
    d_down = matmul(s, df, ta=True, scale=0.5, out_dtype=GRAD_COMM_DTYPE, name=f"{tag}_d_w_down")
    ds = matmul(df, w_down, tb=True, scale=0.5, out_dtype=ACT_DTYPE, behind=behind, name=f"{tag}_d_s")
    dgu = swiglu_bwd(a, b, ds, f"{tag}_swiglu_bwd")
    d_gu = matmul(n, dgu, ta=True, out_dtype=GRAD_COMM_DTYPE, name=f"{tag}_d_w_gate_up")
    tokens = exchange.gradients_ready(tag, {f"{tag}_w_gate_up": d_gu, f"{tag}_w_down": d_down})
    dn = matmul(dgu, w_gu, tb=True, behind=tokens, name=f"{tag}_d_n")
    dh, dh_m, dg = rmsnorm_bwd(h, norm_g, dn, dh_out, f"{tag}_norm_bwd")
    return dh, dh_m, dg


def local_step(x, target, small, exchange):
    big = {}
    token, big_ffn1 = exchange.weights("ffn1", x)
    big.update(big_ffn1)
    h1, saved1 = _swiglu_block_fwd(x, small["ffn1_norm"], big["ffn1_w_gate_up"], big["ffn1_w_down"], "ffn1", token)
    token, big_mix = exchange.weights("mix", h1)
    big.update(big_mix)
    u = rmsnorm_fwd(h1, small["mix_norm"], "mix_norm", behind=token)
    z = matmul(u, big["w_in"], name="w_in")
    p = small["hg_lower_bounds"]
    lb = 1.0 / (1.0 + jnp.exp(p[1:2] - p[0:1]))
    y_hg, o_raw, states = hgrn_fwd(z, lb, small["hg_out_norm"], "hgrn_fwd")
    o_att, l_att = zip(*[att_fwd(z, g, f"att_fwd_{g}") for g in range(N_GROUPS)])
    y_att = att_combine_fwd(o_att, l_att, "att_combine")
    bh = matmul(y_hg, big["w_branch_hg"], name="branch_hg")
    ba = matmul(y_att, big["w_branch_att"], name="branch_att")
    merged = merge_fwd(z, bh, ba, "merge")
    h2 = matmul(merged, big["w_out"], res=h1, name="w_out")
    token, big_ffn2 = exchange.weights("ffn2", h2)
    big.update(big_ffn2)
    h3, saved2 = _swiglu_block_fwd(h2, small["ffn2_norm"], big["ffn2_w_gate_up"], big["ffn2_w_down"], "ffn2", token)
    dh3, dh3_m, d_final, loss = final_norm_loss(h3, small["final_norm"], target, "final_norm_loss")

    gs, gb = {"final_norm": d_final}, {}
    dh2, dh2_m, gs["ffn2_norm"] = _swiglu_block_bwd(
        h2, small["ffn2_norm"], big["ffn2_w_gate_up"], big["ffn2_w_down"], saved2, dh3, dh3_m, "ffn2", exchange)
    token = exchange.backward_done("ffn2", dh2)
    gb["w_out"] = matmul(merged, dh2_m, ta=True, out_dtype=GRAD_COMM_DTYPE, name="d_w_out")
    dmerged = matmul(dh2_m, big["w_out"], tb=True, behind=token, name="d_merged")
    dbh, dba, dz = merge_bwd(z, bh, ba, dmerged, "merge_bwd")
    gb["w_branch_hg"] = matmul(y_hg, dbh, ta=True, out_dtype=GRAD_COMM_DTYPE, name="d_w_branch_hg")
    gb["w_branch_att"] = matmul(y_att, dba, ta=True, out_dtype=GRAD_COMM_DTYPE, name="d_w_branch_att")
    dy_hg = matmul(dbh, big["w_branch_hg"], tb=True, name="d_y_hg")
    dy_att = matmul(dba, big["w_branch_att"], tb=True, name="d_y_att")
    dz, d_lb, gs["hg_out_norm"] = hgrn_bwd(z, lb, small["hg_out_norm"], o_raw, states, dy_hg, dz, "hgrn_bwd")
    do_att, corr = att_combine_bwd(o_att, l_att, dy_att, "att_combine_bwd")
    for g in range(N_GROUPS):
        dz = att_bwd(z, l_att[g], do_att[g], corr[g], dz, g, f"att_bwd_{g}")
    gb["w_in"] = matmul(u, dz, ta=True, out_dtype=GRAD_COMM_DTYPE, name="d_w_in")
    token = exchange.gradients_ready("mix", gb)
    du = matmul(dz, big["w_in"], tb=True, behind=token, name="d_u")
    dh1, dh1_m, gs["mix_norm"] = rmsnorm_bwd(h1, small["mix_norm"], du, dh2, "mix_norm_bwd")
    token = exchange.backward_done("mix", dh1)
    dp0 = d_lb * lb * (1.0 - lb)
    gs["hg_lower_bounds"] = jnp.concatenate([dp0, -dp0], axis=0)
    dx, _, gs["ffn1_norm"] = _swiglu_block_bwd(
        x, small["ffn1_norm"], big["ffn1_w_gate_up"], big["ffn1_w_down"], saved1, dh1, dh1_m, "ffn1", exchange, token)
    exchange.backward_done("ffn1", dx)
    return loss, dx, gs


WEIGHTS = ("ffn1_norm", "ffn1_w_gate_up", "ffn1_w_down", "mix_norm", "w_in", "hg_lower_bounds", "hg_out_norm",
           "w_branch_hg", "w_branch_att", "w_out", "ffn2_norm", "ffn2_w_gate_up", "ffn2_w_down", "final_norm")


class WeightExchange:
    ORDER = ("ffn1", "mix", "ffn2")

    def __init__(self, shards, core, chip):
        self.core, self.chip = core, chip
        self.halving = None
        self.scattering = None
        self.reducing = {}
        first = self.ORDER[0]
        self.placed = {BIG[i][0]: place_own_block(shards[BIG[i][0]], chip, *BIG[i][1:], f"place_{BIG[i][0]}")
                       for i in GROUPS[first]}
        self._start_gather(first, self.placed[self._names(first)[0]])
        chip_behind = chip + self.token[0, :1].astype(jnp.int32)
        for group in self.ORDER[1:]:
            for i in GROUPS[group]:
                n, r, cc, ax = BIG[i]
                self.placed[n] = place_own_block(shards[n], chip_behind, r, cc, ax, f"place_{n}")
        self.placed_behind = [self.placed[n] for group in self.ORDER[1:] for n in self._names(group)]

    def _names(self, group):
        return [BIG[i][0] for i in GROUPS[group]]

    def _start_gather(self, group, after):
        send_sems, recv_sems, bufs, self.token = gather_start([self.placed[n] for n in self._names(group)], after, group)
        self.gathering = (group, send_sems, recv_sems, bufs)

    def weights(self, group, h):
        pending, send_sems, recv_sems, bufs = self.gathering
        assert pending == group
        after = self.placed_behind if group == self.ORDER[0] else [h]
        whole = gather_forward(gather_wait(bufs, send_sems, recv_sems, after, group), group)
        later = self.ORDER.index(group) + 1
        behind = []
        if later < len(self.ORDER):
            self._start_gather(self.ORDER[later], whole[0])
            behind = [self.token]
        return behind, dict(zip(self._names(group), whole))

    @staticmethod
    def _half_to_sibling(ws):
        return lambda ref, w, c: _half(ref, *ws[w][1:], 1 - c)

    def gradients_ready(self, group, grads):
        ws = [BIG[i] for i in GROUPS[group]]
        send_sems, recv_sems, own, theirs, token = sibling_exchange_start(
            [grads[n] for n, *_ in ws], self._half_to_sibling(ws), [_half_shape(r, cc, ax) for _, r, cc, ax in ws],
            GRAD_COMM_DTYPE, f"halves_start_{group}")
        self.halving = (group, send_sems, recv_sems, own, theirs)
        return [token]

    def backward_done(self, group, dh):
        behind = [self._finish_scatter([dh])] if self.scattering is not None else []
        pending, send_sems, recv_sems, own, theirs = self.halving
        assert pending == group
        ws = [BIG[i] for i in GROUPS[group]]
        own, theirs = sibling_exchange_wait(own, theirs, send_sems, recv_sems, self._half_to_sibling(ws), [dh],
                                            f"halves_wait_{group}")
        halves = [add_halves(g, t, self.core, r, cc, ax, f"add_halves_{n}") for (n, r, cc, ax), g, t in zip(ws, own, theirs)]
        send_sems, recv_sems, halves, got, self.token = scatter_start(halves, group)
        self.scattering = (group, send_sems, recv_sems, halves, got)
        return behind + [self.token]

    def _finish_scatter(self, after):
        group, send_sems, recv_sems, halves, got = self.scattering
        halves, got = scatter_wait(halves, got, send_sems, recv_sems, after, group)
        ws = [BIG[i] for i in GROUPS[group]]
        mine = [add_pieces(h, g, self.chip, r, cc, ax, f"add_pieces_{n}") for (n, r, cc, ax), h, g in zip(ws, halves, got)]
        send_sems, recv_sems, mine, theirs, token = sibling_exchange_start(
            mine, lambda ref, w, c: ref, [_piece_shape(r, cc, ax) for _, r, cc, ax in ws], F32, f"reduced_start_{group}")
        self.reducing[group] = (send_sems, recv_sems, mine, theirs)
        self.scattering = None
        return token

    def finish(self, after):
        return self._finish_scatter(after)

    def reduced_halves(self, group, after):
        send_sems, recv_sems, mine, theirs = self.reducing.pop(group)
        mine, theirs = sibling_exchange_wait(mine, theirs, send_sems, recv_sems, lambda ref, w, c: ref, after,
                                             f"reduced_wait_{group}")
        return {BIG[i][0]: (a, b) for i, a, b in zip(GROUPS[group], mine, theirs)}


def kernel(x, ffn1_norm, ffn1_w_gate_up, ffn1_w_down, mix_norm, w_in, hg_lower_bounds, hg_out_norm, w_branch_hg, w_branch_att, w_out, ffn2_norm, ffn2_w_gate_up, ffn2_w_down, final_norm, loss_target, m_ffn1_norm, m_ffn1_w_gate_up, m_ffn1_w_down, m_mix_norm, m_w_in, m_hg_lower_bounds, m_hg_out_norm, m_w_branch_hg, m_w_branch_att, m_w_out, m_ffn2_norm, m_ffn2_w_gate_up, m_ffn2_w_down, m_final_norm, v_ffn1_norm, v_ffn1_w_gate_up, v_ffn1_w_down, v_mix_norm, v_w_in, v_hg_lower_bounds, v_hg_out_norm, v_w_branch_hg, v_w_branch_att, v_w_out, v_ffn2_norm, v_ffn2_w_gate_up, v_ffn2_w_down, v_final_norm):
    w = dict(ffn1_norm=ffn1_norm, ffn1_w_gate_up=ffn1_w_gate_up, ffn1_w_down=ffn1_w_down, mix_norm=mix_norm, w_in=w_in,
             hg_lower_bounds=hg_lower_bounds, hg_out_norm=hg_out_norm, w_branch_hg=w_branch_hg, w_branch_att=w_branch_att,
             w_out=w_out, ffn2_norm=ffn2_norm, ffn2_w_gate_up=ffn2_w_gate_up, ffn2_w_down=ffn2_w_down, final_norm=final_norm)
    m = dict(ffn1_norm=m_ffn1_norm, ffn1_w_gate_up=m_ffn1_w_gate_up, ffn1_w_down=m_ffn1_w_down, mix_norm=m_mix_norm,
             w_in=m_w_in, hg_lower_bounds=m_hg_lower_bounds, hg_out_norm=m_hg_out_norm, w_branch_hg=m_w_branch_hg,
             w_branch_att=m_w_branch_att, w_out=m_w_out, ffn2_norm=m_ffn2_norm, ffn2_w_gate_up=m_ffn2_w_gate_up,
             ffn2_w_down=m_ffn2_w_down, final_norm=m_final_norm)
    v = dict(ffn1_norm=v_ffn1_norm, ffn1_w_gate_up=v_ffn1_w_gate_up, ffn1_w_down=v_ffn1_w_down, mix_norm=v_mix_norm,
             w_in=v_w_in, hg_lower_bounds=v_hg_lower_bounds, hg_out_norm=v_hg_out_norm, w_branch_hg=v_w_branch_hg,
             w_branch_att=v_w_branch_att, w_out=v_w_out, ffn2_norm=v_ffn2_norm, ffn2_w_gate_up=v_ffn2_w_gate_up,
             ffn2_w_down=v_ffn2_w_down, final_norm=v_final_norm)

    core = lax.axis_index("c").astype(jnp.int32).reshape(1)
    chip = (2 * lax.axis_index("x") + lax.axis_index("y")).astype(jnp.int32).reshape(1)
    exchange = WeightExchange({n: w[n][0] for n, *_ in BIG}, core, chip)
    small = {n: w[n] for n in SMALL}
    small["final_norm"] = final_norm.reshape(1, D_MODEL)

    loss, dx, gs = local_step(x[0], loss_target[0], small, exchange)

    grads, delta, new_m, new_v = {}, {}, {}, {}

    def update(group, core, after):
        reduced = exchange.reduced_halves(group, after)
        for i in GROUPS[group]:
            n, r, cc, ax = BIG[i]
            a, b = reduced[n]
            g, d, nm, nv = adamw_halves(w[n][0], a, b, m[n][0], v[n][0], core, r, cc, ax, f"adamw_{n}")
            grads[n], delta[n], new_m[n], new_v[n] = g[None], d[None], nm[None], nv[None]

    core_behind = core + exchange.token[0, :1].astype(jnp.int32)
    update("ffn2", core_behind, [exchange.token])
    update("mix", core_behind, [delta["ffn2_w_down"]])
    token = exchange.finish(after=[delta[BIG[i][0]] for group in ("ffn2", "mix") for i in GROUPS[group]])
    two_d = lambda a: a.reshape(1, D_MODEL) if a.ndim == 1 else a
    loss_sum, *small_out = small_step(loss, [gs[n] for n in SMALL], *[[two_d(p[n]) for n in SMALL] for p in (w, m, v)],
                                      behind=token)
    for result, parts in zip((grads, delta, new_m, new_v), small_out):
        result.update({n: a.reshape(w[n].shape) for n, a in zip(SMALL, parts)})
    update("ffn1", core, [loss_sum])

    return (loss_sum[0, 0], dx[None], *[grads[n] for n in WEIGHTS], *[delta[n] for n in WEIGHTS],
            *[new_m[n] for n in WEIGHTS], *[new_v[n] for n in WEIGHTS])
```

```python
import numpy as np
import jax
import jax.numpy as jnp
from jax import lax
from jax.experimental import pallas as pl
from jax.experimental.pallas import tpu as pltpu

SEQ = 2048
D_MODEL = 1024
D_FF = 2816
HG_HEADS = 4
HG_DIM = 128
HG_WIDTH = 512
HG_CHUNK = 64
ATT_GROUPS = ((128, 1), (512, 4), (2048, 16))
ATT_HEADS = 8
ATT_WIDTH = 512
ATT_BLOCK = 128
ALIBI_MAX = 8.0
IN_COLS = 8704
EPS = 1e-6
NEG_INF = -1e30
ADAM_LR = 0.001
ADAM_B1 = 0.9
ADAM_B2 = 0.999
ADAM_EPS = 1e-08
ADAM_WD = 0.01
ADAM_STEP = 10

N_CHIPS = 4
MXU_DTYPE = jnp.bfloat16
WEIGHT_COMM_DTYPE = jnp.bfloat16
GRAD_COMM_DTYPE = jnp.bfloat16
ACT_DTYPE = jnp.bfloat16
MESH = pl.DeviceIdType.MESH
F32 = jnp.float32


def _sigmoid(x):
    return 1.0 / (1.0 + jnp.exp(-x))


def _dot(a, b, ta=False, tb=False):
    dn = (((0 if ta else 1,), (1 if tb else 0,)), ((), ()))
    return lax.dot_general(a.astype(MXU_DTYPE), b.astype(MXU_DTYPE), dn, preferred_element_type=F32)


def _dot_f32(a, b, ones_on_right=False):
    x = a if ones_on_right else b
    hi = x.astype(jnp.bfloat16)
    rest = x - hi.astype(F32)
    mid = rest.astype(jnp.bfloat16)
    lo = (rest - mid.astype(F32)).astype(jnp.bfloat16)
    if ones_on_right:
        dot = lambda q: jnp.dot(q, b.astype(jnp.bfloat16), preferred_element_type=F32)
    else:
        dot = lambda q: jnp.dot(a.astype(jnp.bfloat16), q, preferred_element_type=F32)
    return dot(hi) + (dot(mid) + dot(lo))


def _split_bf16(x):
    hi = x.astype(jnp.bfloat16)
    return hi, (x - hi.astype(F32)).astype(jnp.bfloat16)


def _hdot(a, b, ta=False, tb=False):
    dn =(((0 if ta else 1,), (1 if tb else 0,)), ((), ()))
    (a_hi, a_lo), (b_hi, b_lo) = _split_bf16(a), _split_bf16(b)
    dot = lambda p, q: lax.dot_general(p, q, dn, preferred_element_type=F32)
    return dot(a_hi, b_hi) + (dot(a_lo, b_hi) + dot(a_hi, b_lo))


MATMUL_VMEM_BYTES = 48 * 1024 * 1024
MATMUL_TILE_BYTES = 36 * 1024 * 1024
MXU_ALIGN = 128


def _divisors(n, most):
    return [t for t in range(min(n, most), 0, -MXU_ALIGN) if n % t == 0 and t % MXU_ALIGN == 0]


def _matmul_tiles(M, N, K, in_bytes, out_bytes, has_res):
    best = None
    for tk in _divisors(K, K):
        nk = K // tk
        for tm in _divisors(M, 2048):
            for tn in _divisors(N, 512):
                tiles = 2 * in_bytes * (tm * tk + tk * tn) + 2 * out_bytes * tm * tn
                tiles += 4 * tm * tn * ((nk > 1) + 2 * has_res)
                if tiles > MATMUL_TILE_BYTES:
                    continue
                traffic = in_bytes * (M * K * (1 if nk == 1 else N // tn) + K * N * (M // tm))
                key = (traffic, -tm * tn * tk)
                if best is None or key < best[0]:
                    best = (key, (tm, tn, tk))
    return best[1]


def matmul(a, b, *, ta=False, tb=False, out_dtype=F32, res=None, scale=1.0, behind=(), name):
    if ta:
        K, M = a.shape
    else:
        M, K = a.shape
    if tb:
        N, K2 = b.shape
    else:
        K2, N = b.shape
    assert K == K2 and a.dtype == b.dtype
    tm, tn, tk = _matmul_tiles(M, N, K, a.dtype.itemsize, jnp.dtype(out_dtype).itemsize, res is not None)
    nk = K // tk

    def finish(r, r_ref, o_ref):
        if scale != 1.0:
            r = r * scale
        if res is not None:
            r = r_ref[...] + r
        o_ref[...] = r.astype(out_dtype)

    def body(*refs):
        a_ref, b_ref = refs[:2]
        r_ref = refs[2] if res is not None else None
        o_ref = refs[2 + (res is not None) + len(behind)]
        if nk == 1:
            finish(_dot(a_ref[...], b_ref[...], ta, tb), r_ref, o_ref)
            return
        acc = refs[-1]
        k = pl.program_id(2)

        @pl.when(k == 0)
        def _():
            acc[...] = jnp.zeros_like(acc)

        acc[...] += _dot(a_ref[...], b_ref[...], ta, tb)

        @pl.when(k == nk - 1)
        def _():
            finish(acc[...], r_ref, o_ref)

    a_spec = pl.BlockSpec((tk, tm), lambda i, j, k: (k, i)) if ta else pl.BlockSpec((tm, tk), lambda i, j, k: (i, k))
    b_spec = pl.BlockSpec((tn, tk), lambda i, j, k: (j, k)) if tb else pl.BlockSpec((tk, tn), lambda i, j, k: (k, j))
    in_specs = [a_spec, b_spec]
    args = [a, b]
    if res is not None:
        in_specs.append(pl.BlockSpec((tm, tn), lambda i, j, k: (i, j)))
        args.append(res)
    for earlier in behind:
        in_specs.append(pl.BlockSpec(memory_space=pl.ANY))
        args.append(earlier)
    return pl.pallas_call(
        body, name=name, grid=(M // tm, N // tn, nk), in_specs=in_specs,
        out_specs=pl.BlockSpec((tm, tn), lambda i, j, k: (i, j)),
        out_shape=jax.ShapeDtypeStruct((M, N), out_dtype),
        scratch_shapes=[pltpu.VMEM((tm, tn), F32)] if nk > 1 else [],
        compiler_params=pltpu.CompilerParams(dimension_semantics=("parallel", "parallel", "arbitrary"),
                                             vmem_limit_bytes=MATMUL_VMEM_BYTES),
    )(*args)


ROW_TILE = 256


def rmsnorm_fwd(x, g, name, behind=()):
    def body(x_ref, g_ref, *refs):
        n_ref = refs[-1]
        xv = x_ref[...]
        r = lax.rsqrt(jnp.mean(xv * xv, axis=-1, keepdims=True) + EPS)
        n_ref[...] = ((xv * r) * g_ref[...]).astype(n_ref.dtype)

    order = list(behind)
    return pl.pallas_call(
        body, name=name, grid=(SEQ // ROW_TILE,),
        in_specs=[pl.BlockSpec((ROW_TILE, D_MODEL), lambda i: (i, 0)), pl.BlockSpec((1, D_MODEL), lambda i: (0, 0))]
        + [pl.BlockSpec(memory_space=pl.ANY)] * len(order),
        out_specs=pl.BlockSpec((ROW_TILE, D_MODEL), lambda i: (i, 0)),
        out_shape=jax.ShapeDtypeStruct((SEQ, D_MODEL), MXU_DTYPE),
    )(x, g, *order)


def rmsnorm_bwd(x, g, dn, dres, name):
    def body(x_ref, g_ref, dn_ref, dr_ref, dx_ref, dxm_ref, dg_ref):
        xv = x_ref[...]
        r = lax.rsqrt(jnp.mean(xv * xv, axis=-1, keepdims=True) + EPS)
        xh = xv * r
        dnv = dn_ref[...]

        @pl.when(pl.program_id(0) == 0)
        def _():
            dg_ref[...] = jnp.zeros_like(dg_ref)

        dg_ref[...] += jnp.sum(dnv * xh, axis=0, keepdims=True)
        dxh = dnv * g_ref[...]
        dx = dr_ref[...] + r * (dxh - xh * jnp.mean(dxh * xh, axis=-1, keepdims=True))
        dx_ref[...] = dx
        dxm_ref[...] = dx.astype(dxm_ref.dtype)

    row = pl.BlockSpec((ROW_TILE, D_MODEL), lambda i: (i, 0))
    vec = pl.BlockSpec((1, D_MODEL), lambda i: (0, 0))
    return pl.pallas_call(
        body, name=name, grid=(SEQ // ROW_TILE,), in_specs=[row, vec, row, row], out_specs=[row, row, vec],
        out_shape=[jax.ShapeDtypeStruct((SEQ, D_MODEL), F32), jax.ShapeDtypeStruct((SEQ, D_MODEL), MXU_DTYPE),
                   jax.ShapeDtypeStruct((1, D_MODEL), F32)],
        compiler_params=pltpu.CompilerParams(dimension_semantics=("arbitrary",)),
    )(x, g, dn, dres)


def final_norm_loss(h, g, target, name):
    def body(h_ref, g_ref, t_ref, dh_ref, dhm_ref, dg_ref, loss_ref):
        xv = h_ref[...]
        r = lax.rsqrt(jnp.mean(xv * xv, axis=-1, keepdims=True) + EPS)
        xh = xv * r
        gv = g_ref[...]
        e = xh * gv - t_ref[...]

        @pl.when(pl.program_id(0) == 0)
        def _():
            dg_ref[...] = jnp.zeros_like(dg_ref)
            loss_ref[...] = jnp.zeros_like(loss_ref)

        part = 0.5 * jnp.sum(jnp.sum(e * e, axis=-1, keepdims=True) * (1.0 / D_MODEL), axis=0, keepdims=True)
        loss_ref[...] += jnp.broadcast_to(part, loss_ref.shape)
        dout = e * (1.0 / D_MODEL)
        dg_ref[...] += jnp.sum(dout * xh, axis=0, keepdims=True)
        dxh = dout * gv
        dh = r * (dxh - xh * jnp.mean(dxh * xh, axis=-1, keepdims=True))
        dh_ref[...] = dh
        dhm_ref[...] = dh.astype(dhm_ref.dtype)

    row = pl.BlockSpec((ROW_TILE, D_MODEL), lambda i: (i, 0))
    vec = pl.BlockSpec((1, D_MODEL), lambda i: (0, 0))
    return pl.pallas_call(
        body, name=name, grid=(SEQ // ROW_TILE,), in_specs=[row, vec, row],
        out_specs=[row, row, vec, pl.BlockSpec((8, 128), lambda i: (0, 0))],
        out_shape=[jax.ShapeDtypeStruct((SEQ, D_MODEL), F32), jax.ShapeDtypeStruct((SEQ, D_MODEL), MXU_DTYPE),
                   jax.ShapeDtypeStruct((1, D_MODEL), F32), jax.ShapeDtypeStruct((8, 128), F32)],
        compiler_params=pltpu.CompilerParams(dimension_semantics=("arbitrary",)),
    )(h, g, target)


FFN_TILE = 256
FFN_TILES = D_FF // FFN_TILE


def gate_up_swiglu(n, w_gu, name):
    def body(n_ref, wa_ref, wb_ref, a_ref, b_ref, s_ref):
        nv = n_ref[...]
        a = _dot(nv, wa_ref[...])
        b = _dot(nv, wb_ref[...])
        a_ref[...] = a.astype(a_ref.dtype)
        b_ref[...] = b.astype(b_ref.dtype)
        s_ref[...] = (a * _sigmoid(a) * b).astype(s_ref.dtype)

    tile = pl.BlockSpec((SEQ, FFN_TILE), lambda j: (0, j))
    act = jax.ShapeDtypeStruct((SEQ, D_FF), ACT_DTYPE)
    return pl.pallas_call(
        body, name=name, grid=(FFN_TILES,),
        in_specs=[pl.BlockSpec((SEQ, D_MODEL), lambda j: (0, 0)), pl.BlockSpec((D_MODEL, FFN_TILE), lambda j: (0, j)),
                  pl.BlockSpec((D_MODEL, FFN_TILE), lambda j: (0, j + FFN_TILES))],
        out_specs=[tile, tile, tile], out_shape=[act, act, jax.ShapeDtypeStruct((SEQ, D_FF), MXU_DTYPE)],
        compiler_params=pltpu.CompilerParams(dimension_semantics=("parallel",), vmem_limit_bytes=MATMUL_VMEM_BYTES),
    )(n, w_gu, w_gu)


def swiglu_bwd(a, b, ds, name):
    rows = ROW_TILE // 2

    def body(a_ref, b_ref, ds_ref, o_ref):
        av = a_ref[...].astype(F32)
        sg = _sigmoid(av)
        dsv = ds_ref[...].astype(F32)
        o_ref[:, :D_FF] = (dsv * b_ref[...].astype(F32) * (sg * (1.0 + av * (1.0 - sg)))).astype(o_ref.dtype)
        o_ref[:, D_FF:] = (dsv * av * sg).astype(o_ref.dtype)

    blk = pl.BlockSpec((rows, D_FF), lambda i: (i, 0))
    return pl.pallas_call(
        body, name=name, grid=(SEQ // rows,), in_specs=[blk, blk, blk],
        out_specs=pl.BlockSpec((rows, 2 * D_FF), lambda i: (i, 0)),
        out_shape=jax.ShapeDtypeStruct((SEQ, 2 * D_FF), MXU_DTYPE), compiler_params=SUM_PARAMS,
    )(a, b, ds)


GATE_HG_BLK = 6656 // 512
GATE_ATT_BLK = 7680 // 512


def merge_fwd(z, bh, ba, name):
    def body(gh_ref, ga_ref, bh_ref, ba_ref, o_ref):
        o_ref[...] = (_sigmoid(gh_ref[...]) * bh_ref[...] + _sigmoid(ga_ref[...]) * ba_ref[...]).astype(o_ref.dtype)

    blk = pl.BlockSpec((ROW_TILE, 512), lambda i, j: (i, j))
    return pl.pallas_call(
        body, name=name, grid=(SEQ // ROW_TILE, 2),
        in_specs=[pl.BlockSpec((ROW_TILE, 512), lambda i, j: (i, GATE_HG_BLK + j)),
                  pl.BlockSpec((ROW_TILE, 512), lambda i, j: (i, GATE_ATT_BLK + j)), blk, blk],
        out_specs=blk, out_shape=jax.ShapeDtypeStruct((SEQ, D_MODEL), MXU_DTYPE),
    )(z, z, bh, ba)


def merge_bwd(z, bh, ba, dm, name):
    def body(gh_ref, ga_ref, bh_ref, ba_ref, dm_ref, dbh_ref, dba_ref, dgh_ref, dga_ref):
        dmv = dm_ref[...]
        sh = _sigmoid(gh_ref[...])
        sa = _sigmoid(ga_ref[...])
        dbh_ref[...] = (dmv * sh).astype(dbh_ref.dtype)
        dba_ref[...] = (dmv * sa).astype(dba_ref.dtype)
        dgh_ref[...] = (dmv * bh_ref[...] * (sh * (1.0 - sh))).astype(dgh_ref.dtype)
        dga_ref[...] = (dmv * ba_ref[...] * (sa * (1.0 - sa))).astype(dga_ref.dtype)

    blk = pl.BlockSpec((ROW_TILE, 512), lambda i, j: (i, j))
    out = jax.ShapeDtypeStruct((SEQ, D_MODEL), MXU_DTYPE)
    return pl.pallas_call(
        body, name=name, grid=(SEQ // ROW_TILE, 2),
        in_specs=[pl.BlockSpec((ROW_TILE, 512), lambda i, j: (i, GATE_HG_BLK + j)),
                  pl.BlockSpec((ROW_TILE, 512), lambda i, j: (i, GATE_ATT_BLK + j)), blk, blk, blk],
        out_specs=[blk, blk, blk, blk], out_shape=[out, out, out, out],
    )(z, z, bh, ba, dm)


N_CHUNKS = SEQ // HG_CHUNK
HG_STEP_CHUNKS = 4


def _hgrn_gates(q, fp, lb):
    C = HG_CHUNK
    sg = _sigmoid(fp)
    f = lb + (1.0 - lb) * sg
    lf = jnp.log(f)
    row = lax.broadcasted_iota(jnp.int32, (C, C), 0)
    col = lax.broadcasted_iota(jnp.int32, (C, C), 1)
    causal = row >= col
    G = _dot_f32(causal.astype(F32), lf)
    eG = jnp.exp(G)
    enG = jnp.exp(-G)
    qg = q * eG
    kg = (1.0 - f) * enG
    A = jnp.where(causal, _hdot(qg, kg, tb=True), 0.0)
    egl = jnp.exp(jnp.sum(lf, axis=0, keepdims=True))
    return sg, f, causal, eG, enG, qg, kg, A, egl


def hgrn_fwd(z, lb, gain, name):
    C, K = HG_CHUNK, HG_DIM

    def body(q_ref, f_ref, v_ref, og_ref, p_ref, g_ref, y_ref, o_ref, st_ref, state):
        @pl.when(pl.program_id(0) == 0)
        def _():
            state[...] = jnp.zeros_like(state)

        for cc in range(HG_STEP_CHUNKS):
            rows = pl.ds(cc * C, C)
            for h in range(HG_HEADS):
                hd = pl.ds(h * K, K)
                v = v_ref[rows, hd]
                _, _, _, _, _, qg, kg, A, egl = _hgrn_gates(q_ref[rows, hd], f_ref[rows, hd], p_ref[:, hd])
                st = state[h]
                st_ref[h, cc] = st
                o = _hdot(A, v) + _hdot(qg, st, tb=True)
                state[h] = st * egl + _hdot(v, kg * egl, ta=True)
                o_ref[rows, hd] = o
                rs = lax.rsqrt(jnp.mean(o * o, axis=-1, keepdims=True) + EPS)
                og = og_ref[rows, hd]
                y_ref[rows, hd] = (((o * rs) * g_ref[:, hd]) * (og * _sigmoid(og))).astype(y_ref.dtype)

    R = HG_STEP_CHUNKS * C

    def zcol(section):
        return pl.BlockSpec((R, HG_WIDTH), lambda c: (c, section))

    vec = pl.BlockSpec((1, HG_WIDTH), lambda c: (0, 0))
    blk = pl.BlockSpec((R, HG_WIDTH), lambda c: (c, 0))
    return pl.pallas_call(
        body, name=name, grid=(N_CHUNKS // HG_STEP_CHUNKS,),
        in_specs=[zcol(0), zcol(1), zcol(2), zcol(3), vec, vec],
        out_specs=[blk, blk, pl.BlockSpec((HG_HEADS, HG_STEP_CHUNKS, K, K), lambda c: (0, c, 0, 0))],
        out_shape=[jax.ShapeDtypeStruct((SEQ, HG_WIDTH), MXU_DTYPE), jax.ShapeDtypeStruct((SEQ, HG_WIDTH), F32),
                   jax.ShapeDtypeStruct((HG_HEADS, N_CHUNKS, K, K), F32)],
        scratch_shapes=[pltpu.VMEM((HG_HEADS, K, K), F32)],
        compiler_params=pltpu.CompilerParams(dimension_semantics=("arbitrary",)),
    )(z, z, z, z, lb, gain)


def hgrn_bwd(z, lb, gain, o_raw, states, dy, name):
    C, K = HG_CHUNK, HG_DIM

    def body(q_ref, f_ref, v_ref, og_ref, p_ref, g_ref, o_ref, st_ref, dy_ref,
             dq_ref, dfp_ref, dv_ref, dog_ref, dlb_ref, dgain_ref, dstate):
        @pl.when(pl.program_id(0) == 0)
        def _():
            dstate[...] = jnp.zeros_like(dstate)
            dlb_ref[...] = jnp.zeros_like(dlb_ref)
            dgain_ref[...] = jnp.zeros_like(dgain_ref)

        last = lax.broadcasted_iota(jnp.int32, (C, K), 0) == C - 1
        row = lax.broadcasted_iota(jnp.int32, (C, C), 0)
        col = lax.broadcasted_iota(jnp.int32, (C, C), 1)
        anti_causal = (col >= row).astype(F32)
        for cc in reversed(range(HG_STEP_CHUNKS)):
            rows = pl.ds(cc * C, C)
            for h in range(HG_HEADS):
                hd = pl.ds(h * K, K)
                v = v_ref[rows, hd]
                lb = p_ref[:, hd]
                sg, f, causal, eG, enG, qg, kg, A, egl = _hgrn_gates(q_ref[rows, hd], f_ref[rows, hd], lb)
                kd = kg * egl
                st = st_ref[h, cc]
                dst = dstate[h]
                o = o_ref[rows, hd]
                og = og_ref[rows, hd]
                gain_v = g_ref[:, hd]
                dyv = dy_ref[rows, hd]
                rs = lax.rsqrt(jnp.mean(o * o, axis=-1, keepdims=True) + EPS)
                on = o * rs
                sgo = _sigmoid(og)
                silu = og * sgo
                dog_ref[rows, hd] = (dyv * (on * gain_v) * (sgo * (1.0 + og * (1.0 - sgo)))).astype(dog_ref.dtype)
                dgain_ref[:, hd] += jnp.sum(dyv * silu * on, axis=0, keepdims=True)
                don = dyv * gain_v * silu
                do = rs * (don - on * jnp.mean(don * on, axis=-1, keepdims=True))
                dA = jnp.where(causal, _hdot(do, v, tb=True), 0.0)
                dv_ref[rows, hd] = (_hdot(A, do, ta=True) + _hdot(kd, dst, tb=True)).astype(dv_ref.dtype)
                dqg = _hdot(dA, kg) + _hdot(do, st)
                dkg = _hdot(dA, qg, ta=True)
                dkd = _hdot(v, dst)
                dstate[h] = dst * egl + _hdot(do, qg, ta=True)
                dgl = jnp.sum(st * dst, axis=0, keepdims=True) * egl
                dq_ref[rows, hd] = (dqg * eG).astype(dq_ref.dtype)
                dk = dkg * enG + dkd * (enG * egl)
                dG = dqg * qg - dkg * kg - dkd * kd
                extra = jnp.sum(dkd * kd, axis=0, keepdims=True) + dgl
                dG = dG + jnp.where(last, extra, 0.0)
                dlf = _dot_f32(anti_causal, dG)
                df = dlf / f - dk
                dfp_ref[rows, hd] = (df * (1.0 - lb) * (sg * (1.0 - sg))).astype(dfp_ref.dtype)
                dlb_ref[:, hd] += jnp.sum(df * (1.0 - sg), axis=0, keepdims=True)

    R = HG_STEP_CHUNKS * C
    n_steps = N_CHUNKS // HG_STEP_CHUNKS

    def rc(c):
        return n_steps - 1 - c

    def zcol(section):
        return pl.BlockSpec((R, HG_WIDTH), lambda c: (rc(c), section))

    vec = pl.BlockSpec((1, HG_WIDTH), lambda c: (0, 0))
    blk = pl.BlockSpec((R, HG_WIDTH), lambda c: (rc(c), 0))
    out = jax.ShapeDtypeStruct((SEQ, HG_WIDTH), MXU_DTYPE)
    small = jax.ShapeDtypeStruct((1, HG_WIDTH), F32)
    return pl.pallas_call(
        body, name=name, grid=(n_steps,),
        in_specs=[zcol(0), zcol(1), zcol(2), zcol(3), vec, vec, blk,
                  pl.BlockSpec((HG_HEADS, HG_STEP_CHUNKS, K, K), lambda c: (0, rc(c), 0, 0)), blk],
        out_specs=[blk, blk, blk, blk, vec, vec],
        out_shape=[out, out, out, out, small, small],
        scratch_shapes=[pltpu.VMEM((HG_HEADS, K, K), F32)],
        compiler_params=pltpu.CompilerParams(dimension_semantics=("arbitrary",)),
    )(z, z, z, z, lb, gain, o_raw, states, dy)


N_GROUPS = len(ATT_GROUPS)
HEAD_PAIRS = ATT_WIDTH // 128
ATT_COL0 = 4 * HG_WIDTH
UNROLLED_UNITS = 4


def _alibi_coef():
    n = N_GROUPS * ATT_HEADS
    slopes = np.exp2(-ALIBI_MAX * np.arange(1, n + 1, dtype=np.float32) / n).astype(np.float32)
    dil = np.repeat(np.array([d for _, d in ATT_GROUPS], np.float32), ATT_HEADS)
    return jnp.asarray(slopes * dil, F32)


def _for_each_unit(n, fn):
    if n <= UNROLLED_UNITS:
        for u in range(n):
            fn(u)
    else:
        def group(i, carry):
            for j in range(UNROLLED_UNITS):
                fn(i * UNROLLED_UNITS + j)
            return carry
        lax.fori_loop(0, n // UNROLLED_UNITS, group, 0)


def _att_geometry(g):
    B = ATT_BLOCK
    d = ATT_GROUPS[g][1]
    n_blocks = SEQ // (d * B)
    col0 = (ATT_COL0 + g * 3 * ATT_WIDTH) // 128

    def block_rows(b, r):
        return pl.ds(b * (B * d) + r, B, stride=d) if d > 1 else pl.ds(pl.multiple_of(b * B, B), B)

    def block_of(u):
        return (u, 0) if d == 1 else (u // d, u % d)

    return d, n_blocks, col0, block_rows, block_of


def _att_column(c):
    return pl.BlockSpec((SEQ, 128), lambda hp: (0, c + hp))


def _head_lanes(j):
    lane = lax.broadcasted_iota(jnp.int32, (ATT_BLOCK, 128), 1)
    return (lane >= 64 * j) & (lane < 64 * (j + 1))


def _stack_heads(x, sel0):
    return jnp.concatenate([jnp.where(sel0, x, 0.0), jnp.where(sel0, 0.0, x)], axis=0)


def _stack_values(x, sel0, lanes):
    swapped = pltpu.roll(x, 64, 1)
    stacked = jnp.concatenate([jnp.where(sel0, x, swapped), jnp.where(sel0, swapped, x)], axis=0)
    return stacked if lanes == 128 else jnp.concatenate([stacked] * (lanes // 128), axis=1)


def _pair_coef(coef_ref, g, hp):
    row = lax.broadcasted_iota(jnp.int32, (2 * ATT_BLOCK, 1), 0)
    first = g * ATT_HEADS + hp * 2
    return jnp.where(row < ATT_BLOCK, coef_ref[first], coef_ref[first + 1])


def _band(with_prev, first_key):
    B = ATT_BLOCK
    keys = 2 * B if with_prev else B
    qi = jnp.bitwise_and(lax.broadcasted_iota(jnp.int32, (2 * B, keys), 0), B - 1)
    kj = lax.broadcasted_iota(jnp.int32, (2 * B, keys), 1)
    delta = qi + (B if with_prev else 0) - kj
    valid = (delta >= 0) & (delta <= B)
    if with_prev:
        valid = valid & (kj >= first_key)
    return valid, delta.astype(F32)


def att_fwd(z, g, name):
    B = ATT_BLOCK
    d, n_blocks, col0, block_rows, block_of = _att_geometry(g)
    multi = n_blocks > 1

    def body(coef_ref, q_ref, k_ref, v_ref, o_ref, l_ref):
        cf2 = _pair_coef(coef_ref, g, pl.program_id(0))
        sel0 = _head_lanes(0)

        def one(u):
            b, r = block_of(u)
            rows = block_rows(b, r)
            valid, dist = _band(multi, jnp.where(b == 0, B, 0))
            q2 = _stack_heads(q_ref[rows, :], sel0)
            kk, vv = k_ref[rows, :], v_ref[rows, :]
            if multi:
                prev_rows = block_rows(jnp.maximum(b - 1, 0), r)
                kk = jnp.concatenate([k_ref[prev_rows, :], kk], axis=0)
                vv = jnp.concatenate([v_ref[prev_rows, :], vv], axis=0)
            sc = jnp.where(valid, _dot(q2, kk, tb=True) * 0.125 - cf2 * dist, NEG_INF)
            mx = jnp.max(sc, axis=-1, keepdims=True)
            e = jnp.exp(sc - mx)
            den = jnp.sum(e, axis=-1, keepdims=True)
            o2 = _dot(e * (1.0 / den), vv)
            lse2 = mx + jnp.log(den)
            o_ref[rows, :] = jnp.where(sel0, o2[:B], o2[B:])
            l_ref[rows, :] = jnp.where(sel0, lse2[:B], lse2[B:])

        _for_each_unit(d * n_blocks, one)

    out = jax.ShapeDtypeStruct((SEQ, ATT_WIDTH), F32)
    return pl.pallas_call(
        body, name=name, grid=(HEAD_PAIRS,),
        in_specs=[pl.BlockSpec(memory_space=pltpu.SMEM), _att_column(col0), _att_column(col0 + 4), _att_column(col0 + 8)],
        out_specs=[_att_column(0), _att_column(0)], out_shape=[out, out],
        compiler_params=pltpu.CompilerParams(dimension_semantics=("parallel",)),
    )(_alibi_coef(), z, z, z)


def att_bwd(z, l, do, corr, g, name):
    B = ATT_BLOCK
    d, n_blocks, col0, block_rows, block_of = _att_geometry(g)
    multi = n_blocks > 1
    own = slice(B, 2 * B) if multi else slice(0, B)

    def body(coef_ref, q_ref, k_ref, v_ref, l_ref, do_ref, cr_ref, dq_ref, dk_ref, dv_ref, dq_sc, dk_sc, dv_sc):
        cf2 = _pair_coef(coef_ref, g, pl.program_id(0))
        sel0 = _head_lanes(0)

        def one(u):
            b, r = block_of(u)
            rows = block_rows(b, r)
            valid, dist = _band(multi, jnp.where(b == 0, B, 0))
            kk, vv = k_ref[rows, :], v_ref[rows, :]
            if multi:
                prev_rows = block_rows(jnp.maximum(b - 1, 0), r)
                kk = jnp.concatenate([k_ref[prev_rows, :], kk], axis=0)
                vv = jnp.concatenate([v_ref[prev_rows, :], vv], axis=0)
            q2, do2 = _stack_heads(q_ref[rows, :], sel0), _stack_heads(do_ref[rows, :], sel0)
            keys = kk.shape[0]
            lse2, cr2 = _stack_values(l_ref[rows, :], sel0, keys), _stack_values(cr_ref[rows, :], sel0, keys)
            p = jnp.exp(jnp.where(valid, _dot(q2, kk, tb=True) * 0.125 - cf2 * dist, NEG_INF) - lse2)
            ds = p * (_dot(do2, vv, tb=True) + cr2)
            dq2 = _dot(ds, kk)
            dkk = _dot(ds, q2, ta=True) * 0.125
            dvv = _dot(p, do2, ta=True)
            dq_sc[rows, :] = jnp.where(sel0, dq2[:B], dq2[B:]) * 0.125
            dk_sc[rows, :] = dkk[own]
            dv_sc[rows, :] = dvv[own]
            if multi:
                dk_sc[prev_rows, :] += dkk[:B]
                dv_sc[prev_rows, :] += dvv[:B]

        _for_each_unit(d * n_blocks, one)
        dq_ref[...] = dq_sc[...].astype(dq_ref.dtype)
        dk_ref[...] = dk_sc[...].astype(dk_ref.dtype)
        dv_ref[...] = dv_sc[...].astype(dv_ref.dtype)

    col = _att_column
    out = jax.ShapeDtypeStruct((SEQ, ATT_WIDTH), MXU_DTYPE)
    return pl.pallas_call(
        body, name=name, grid=(HEAD_PAIRS,),
        in_specs=[pl.BlockSpec(memory_space=pltpu.SMEM), col(col0), col(col0 + 4), col(col0 + 8), col(0), col(0), col(0)],
        out_specs=[col(0)] * 3, out_shape=[out] * 3,
        scratch_shapes=[pltpu.VMEM((SEQ, 128), F32)] * 3,
        compiler_params=pltpu.CompilerParams(dimension_semantics=("parallel",), vmem_limit_bytes=MATMUL_VMEM_BYTES),
    )(_alibi_coef(), z, z, z, l, do, corr)


def _head_sum(x):
    i = lax.broadcasted_iota(jnp.int32, (128, 128), 0) // 64
    j = lax.broadcasted_iota(jnp.int32, (128, 128), 1) // 64
    return _dot_f32(x, (i == j).astype(F32), ones_on_right=True)


def _group_weights(l0, l1, l2):
    mx = jnp.maximum(jnp.maximum(l0, l1), l2)
    e0, e1, e2 = jnp.exp(l0 - mx), jnp.exp(l1 - mx), jnp.exp(l2 - mx)
    inv = 1.0 / (e0 + e1 + e2)
    return e0 * inv, e1 * inv, e2 * inv


def att_combine_fwd(o, l, name):
    def body(o0, o1, o2, l0, l1, l2, y_ref):
        w0, w1, w2 = _group_weights(l0[...], l1[...], l2[...])
        y_ref[...] = (o0[...] * w0 + o1[...] * w1 + o2[...] * w2).astype(y_ref.dtype)

    blk = pl.BlockSpec((ROW_TILE, ATT_WIDTH), lambda i: (i, 0))
    return pl.pallas_call(
        body, name=name, grid=(SEQ // ROW_TILE,), in_specs=[blk] * 6, out_specs=blk,
        out_shape=jax.ShapeDtypeStruct((SEQ, ATT_WIDTH), MXU_DTYPE),
    )(*o, *l)


def att_combine_bwd(o, l, dy, name):
    def body(o0, o1, o2, l0, l1, l2, dy_ref, do0, do1, do2, cr0, cr1, cr2):
        w = _group_weights(l0[...], l1[...], l2[...])
        dyv = dy_ref[...]
        tot = _head_sum(dyv * (w[0] * o0[...] + w[1] * o1[...] + w[2] * o2[...]))
        for g, (do_ref, cr_ref) in enumerate(((do0, cr0), (do1, cr1), (do2, cr2))):
            do_ref[...] = dyv * w[g]
            cr_ref[...] = -w[g] * tot

    blk = pl.BlockSpec((ROW_TILE, 128), lambda i, j: (i, j))
    out = jax.ShapeDtypeStruct((SEQ, ATT_WIDTH), F32)
    res = pl.pallas_call(
        body, name=name, grid=(SEQ // ROW_TILE, HEAD_PAIRS), in_specs=[blk] * 7, out_specs=[blk] * 6, out_shape=[out] * 6,
    )(*o, *l, dy)
    return res[:N_GROUPS], res[N_GROUPS:]


SUM_ROW_TILES = (1024, 512, 256, 128, 64, 32, 16)
SUM_TILE_BYTES = 24 * 1024 * 1024
SUM_PARAMS = pltpu.CompilerParams(vmem_limit_bytes=MATMUL_VMEM_BYTES)


def _row_tile(rows, cols, operands):
    fit = [t for t in SUM_ROW_TILES if rows % t == 0]
    return next((t for t in fit if 2 * 4 * operands * t * cols <= SUM_TILE_BYTES), fit[-1])


def _shard_shape(rows, cols, axis):
    return (rows // N_CHIPS, cols) if axis == 0 else (rows, cols // N_CHIPS)


def _half_shape(rows, cols, axis):
    return (rows, cols // 2) if axis == 0 else (rows // 2, cols)


def _piece_shape(rows, cols, axis):
    return (rows // N_CHIPS, cols // 2) if axis == 0 else (rows // 2, cols // N_CHIPS)


def place_own_block(shard, chip, rows, cols, axis, name):
    sr, sc = _shard_shape(rows, cols, axis)
    tr = _row_tile(sr, sc, 2)

    def body(chip_ref, s_ref, o_ref):
        o_ref[...] = s_ref[...].astype(o_ref.dtype)

    if axis == 0:
        out_map = lambda i, chip_ref: (chip_ref[0] * (sr // tr) + i, 0)
    else:
        out_map = lambda i, chip_ref: (i, chip_ref[0])
    return pl.pallas_call(
        body, name=name, out_shape=jax.ShapeDtypeStruct((rows, cols), WEIGHT_COMM_DTYPE), compiler_params=SUM_PARAMS,
        grid_spec=pltpu.PrefetchScalarGridSpec(
            num_scalar_prefetch=1, grid=(sr // tr,), in_specs=[pl.BlockSpec((tr, sc), lambda i, chip_ref: (i, 0))],
            out_specs=pl.BlockSpec((tr, sc), out_map)),
    )(chip, shard)


def add_halves(g, theirs, core, rows, cols, axis, name):
    hr, hc = _half_shape(rows, cols, axis)
    tr = _row_tile(hr, hc, 3)

    def body(core_ref, g_ref, t_ref, o_ref):
        o_ref[...] = (g_ref[...].astype(F32) + t_ref[...].astype(F32)).astype(o_ref.dtype)

    if axis == 0:
        g_map = lambda i, core_ref: (i, core_ref[0])
    else:
        g_map = lambda i, core_ref: (core_ref[0] * (hr // tr) + i, 0)
    blk = pl.BlockSpec((tr, hc), lambda i, core_ref: (i, 0))
    return pl.pallas_call(
        body, name=name, out_shape=jax.ShapeDtypeStruct((hr, hc), GRAD_COMM_DTYPE), compiler_params=SUM_PARAMS,
        grid_spec=pltpu.PrefetchScalarGridSpec(
            num_scalar_prefetch=1, grid=(hr // tr,), in_specs=[pl.BlockSpec((tr, hc), g_map), blk], out_specs=blk),
    )(core, g, theirs)


def add_pieces(half, got, chip, rows, cols, axis, name):
    hr, _ = _half_shape(rows, cols, axis)
    pr, pc = _piece_shape(rows, cols, axis)
    tr = _row_tile(pr, pc, 5)

    def body(chip_ref, h_ref, got_ref, o_ref):
        o_ref[...] = (h_ref[...].astype(F32) + got_ref[0].astype(F32) + got_ref[1].astype(F32) + got_ref[2].astype(F32))

    if axis == 0:
        h_map = lambda i, chip_ref: (chip_ref[0] * (pr // tr) + i, 0)
    else:
        h_map = lambda i, chip_ref: (i, chip_ref[0])
    return pl.pallas_call(
        body, name=name, out_shape=jax.ShapeDtypeStruct((pr, pc), F32), compiler_params=SUM_PARAMS,
        grid_spec=pltpu.PrefetchScalarGridSpec(
            num_scalar_prefetch=1, grid=(pr // tr,),
            in_specs=[pl.BlockSpec((tr, pc), h_map), pl.BlockSpec((3, tr, pc), lambda i, chip_ref: (0, i, 0))],
            out_specs=pl.BlockSpec((tr, pc), lambda i, chip_ref: (i, 0))),
    )(chip, half, got)


def _adamw_math(w, g, m, v):
    nm = ADAM_B1 * m + (1.0 - ADAM_B1) * g
    nv = ADAM_B2 * v + (1.0 - ADAM_B2) * (g * g)
    m_hat = nm / (1.0 - ADAM_B1 ** ADAM_STEP)
    v_hat = nv / (1.0 - ADAM_B2 ** ADAM_STEP)
    return -ADAM_LR * (m_hat / (jnp.sqrt(v_hat) + ADAM_EPS) + ADAM_WD * w), nm, nv


def adamw_halves(w, mine, theirs, m, v, core, rows, cols, axis, name):
    sr, sc = _shard_shape(rows, cols, axis)
    pr, pc = _piece_shape(rows, cols, axis)
    tr = _row_tile(pr, pc, 9)
    nt = pr // tr

    def body(core_ref, w_ref, a_ref, b_ref, m_ref, v_ref, g_ref, d_ref, nm_ref, nv_ref):
        g = jnp.where(pl.program_id(0) == core_ref[0], a_ref[...], b_ref[...])
        g_ref[...] = g
        d_ref[...], nm_ref[...], nv_ref[...] = _adamw_math(w_ref[...], g, m_ref[...], v_ref[...])

    if axis == 0:
        full = pl.BlockSpec((tr, pc), lambda h, i, core_ref: (i, h))
    else:
        full = pl.BlockSpec((tr, pc), lambda h, i, core_ref: (h * nt + i, 0))
    part = pl.BlockSpec((tr, pc), lambda h, i, core_ref: (i, 0))
    out = jax.ShapeDtypeStruct((sr, sc), F32)
    return pl.pallas_call(
        body, name=name, out_shape=[out, out, out, out], compiler_params=SUM_PARAMS,
        grid_spec=pltpu.PrefetchScalarGridSpec(
            num_scalar_prefetch=1, grid=(2, nt), in_specs=[full, part, part, full, full], out_specs=[full] * 4),
    )(core, w, mine, theirs, m, v)


BIG = (
    ("ffn1_w_gate_up", D_MODEL, 2 * D_FF, 1),
    ("ffn1_w_down", D_FF, D_MODEL, 0),
    ("w_in", D_MODEL, IN_COLS, 1),
    ("w_branch_hg", HG_WIDTH, D_MODEL, 1),
    ("w_branch_att", ATT_WIDTH, D_MODEL, 1),
    ("w_out", D_MODEL, D_MODEL, 0),
    ("ffn2_w_gate_up", D_MODEL, 2 * D_FF, 1),
    ("ffn2_w_down", D_FF, D_MODEL, 0),
)
N_BIG = len(BIG)
ANY = pl.BlockSpec(memory_space=pl.ANY)


def _place():
    return lax.axis_index("x"), lax.axis_index("y"), lax.axis_index("c")


def _other_chips(x, y):
    return ((1 - x, y), (x, 1 - y), (1 - x, 1 - y))


MAX_COPY_CHUNKS = 16
CHUNK_ROW_ALIGN = 16


def _row_chunks(view):
    rows = view.shape[0]
    n = next(n for n in range(MAX_COPY_CHUNKS, 0, -1) if rows % (CHUNK_ROW_ALIGN * n) == 0 or n == 1)
    step = rows // n
    return [pl.ds(i * step, step) for i in range(n)]


def _remote(src, dst, send_sem, recv_sem, device):
    return pltpu.make_async_remote_copy(src_ref=src, dst_ref=dst, send_sem=send_sem, recv_sem=recv_sem,
                                        device_id=device, device_id_type=MESH)


def _start_remote(src, dst, send_sem, recv_sem, device):
    for rows in _row_chunks(src):
        _remote(src.at[rows, :], dst.at[rows, :], send_sem, recv_sem, device).start()
    return _remote(src, dst, send_sem, recv_sem, device)


HBM = pl.BlockSpec(memory_space=pltpu.HBM)
SEM = pl.BlockSpec(memory_space=pltpu.SEMAPHORE)
SPLIT_COPY_EFFECT = pltpu.SideEffectType.DATAFLOW_SIDE_EFFECTING
GROUPS = {"ffn1": (0, 1), "mix": (2, 3, 4, 5), "ffn2": (6, 7)}


def _in_hbm(a):
    return pltpu.with_memory_space_constraint(a, pltpu.HBM)


class _SemList:
    def __init__(self, refs):
        self.refs = refs
        self.at = self

    def __getitem__(self, index):
        w, k = index
        return self.refs[3 * w + k]


def _gather_piece(ref, rows, cols, axis, chip, c):
    sr, sc = _shard_shape(rows, cols, axis)
    j = 2 * chip[0] + chip[1]
    if axis == 0:
        return ref.at[pl.ds(j * sr + c * (sr // 2), sr // 2), :]
    return ref.at[pl.ds(c * (sr // 2), sr // 2), pl.ds(pl.multiple_of(j * sc, 128), sc)]


def _start_gather_sends(bufs, ws, send_sems, recv_sems):
    x, y, c = _place()
    for w, (_, r, cc, ax) in enumerate(ws):
        mine = _gather_piece(bufs[w], r, cc, ax, (x, y), c)
        for k, chip in enumerate(_other_chips(x, y)):
            _start_remote(mine, mine, send_sems.at[w, k], recv_sems.at[w, k], (*chip, c))


def _wait_gather_sends(bufs, ws, send_sems, recv_sems):
    x, y, c = _place()
    for w, (_, r, cc, ax) in enumerate(ws):
        for k, chip in enumerate(_other_chips(x, y)):
            got = _gather_piece(bufs[w], r, cc, ax, chip, c)
            _remote(got, got, send_sems.at[w, k], recv_sems.at[w, k], (x, y, c)).wait_recv()
    for w, (_, r, cc, ax) in enumerate(ws):
        mine = _gather_piece(bufs[w], r, cc, ax, (x, y), c)
        for k in range(3):
            _remote(mine, mine, send_sems.at[w, k], recv_sems.at[w, k], (x, y, c)).wait_send()


def _forward_halves(bufs, ws, send_sems, recv_sems):
    x, y, c = _place()
    passed = []
    for w, (_, r, cc, ax) in enumerate(ws):
        for k, chip in enumerate(_other_chips(x, y)):
            got = _gather_piece(bufs[w], r, cc, ax, chip, c)
            passed.append(_start_remote(got, got, send_sems.at[w, k], recv_sems.at[w, k], (x, y, 1 - c)))
    for w, (_, r, cc, ax) in enumerate(ws):
        for k, chip in enumerate(_other_chips(x, y)):
            got = _gather_piece(bufs[w], r, cc, ax, chip, 1 - c)
            _remote(got, got, send_sems.at[w, k], recv_sems.at[w, k], (x, y, c)).wait_recv()
    for cp in passed:
        cp.wait_send()


def gather_start(placed, after, group):
    ws = [BIG[i] for i in GROUPS[group]]
    n = len(ws)

    def body(*refs):
        bufs = refs[:n]
        send_sems, recv_sems = _SemList(refs[n + 1:4 * n + 1]), _SemList(refs[4 * n + 1:7 * n + 1])
        token = refs[-1]
        _start_gather_sends(bufs, ws, send_sems, recv_sems)
        token[...] = jnp.zeros_like(token)

    out = pl.pallas_call(
        body, name=f"gather_start_{group}", in_specs=[HBM] * n + [ANY],
        out_specs=[SEM] * (6 * n) + [HBM] * n + [pl.BlockSpec(memory_space=pltpu.VMEM)],
        out_shape=[pltpu.SemaphoreType.DMA(())] * (6 * n)
        + [pltpu.HBM((r, cc), WEIGHT_COMM_DTYPE) for _, r, cc, _ in ws] + [jax.ShapeDtypeStruct((8, 128), F32)],
        input_output_aliases={w: 6 * n + w for w in range(n)},
        compiler_params=pltpu.CompilerParams(has_side_effects=SPLIT_COPY_EFFECT),
    )(*[_in_hbm(p) for p in placed], after)
    return out[:3 * n], out[3 * n:6 * n], out[6 * n:7 * n], out[-1]


def gather_wait(bufs, send_sems, recv_sems, after, group):
    ws = [BIG[i] for i in GROUPS[group]]
    n = len(ws)

    def body(*refs):
        _wait_gather_sends(refs[:n], ws, _SemList(refs[n:n + 3 * n]), _SemList(refs[n + 3 * n:n + 6 * n]))

    return pl.pallas_call(
        body, name=f"gather_wait_{group}", in_specs=[HBM] * n + [SEM] * (6 * n) + [ANY] * len(after), out_specs=[HBM] * n,
        out_shape=[pltpu.HBM((r, cc), WEIGHT_COMM_DTYPE) for _, r, cc, _ in ws],
        input_output_aliases={w: w for w in range(n)},
        compiler_params=pltpu.CompilerParams(has_side_effects=SPLIT_COPY_EFFECT),
    )(*bufs, *send_sems, *recv_sems, *after)


def gather_forward(bufs, group):
    ws = [BIG[i] for i in GROUPS[group]]
    n = len(ws)

    def body(*refs):
        _forward_halves(refs[n:2 * n], ws, refs[2 * n], refs[2 * n + 1])

    return pl.pallas_call(
        body, name=f"gather_forward_{group}", in_specs=[ANY] * n, out_specs=[ANY] * n,
        out_shape=[jax.ShapeDtypeStruct((r, cc), WEIGHT_COMM_DTYPE) for _, r, cc, _ in ws],
        input_output_aliases={w: w for w in range(n)},
        scratch_shapes=[pltpu.SemaphoreType.DMA((n, 3))] * 2,
    )(*bufs)


def _half(ref, rows, cols, axis, c):
    if axis == 0:
        return ref.at[:, pl.ds(pl.multiple_of(c * (cols // 2), 128), cols // 2)]
    return ref.at[pl.ds(c * (rows // 2), rows // 2), :]


def _piece_of_half(ref, rows, cols, axis, chip):
    j = 2 * chip[0] + chip[1]
    pr, pc = _piece_shape(rows, cols, axis)
    if axis == 0:
        return ref.at[pl.ds(j * pr, pr), :]
    return ref.at[:, pl.ds(pl.multiple_of(j * pc, 128), pc)]


def sibling_exchange_start(srcs, view, landing_shapes, dtype, name):
    n = len(srcs)

    def body(*refs):
        ins, land, sems = refs[:n], refs[n:2 * n], refs[2 * n:4 * n]
        x, y, c = _place()
        for w in range(n):
            _start_remote(view(ins[w], w, c), land[w], sems[w], sems[n + w], (x, y, 1 - c))
        refs[-1][...] = jnp.zeros_like(refs[-1])

    landing = [lax.empty(shape, dtype) for shape in landing_shapes]
    out = pl.pallas_call(
        body, name=name, in_specs=[HBM] * (2 * n),
        out_specs=[SEM] * (2 * n) + [HBM] * (2 * n) + [pl.BlockSpec(memory_space=pltpu.VMEM)],
        out_shape=[pltpu.SemaphoreType.DMA(())] * (2 * n) + [pltpu.HBM(a.shape, a.dtype) for a in srcs]
        + [pltpu.HBM(shape, dtype) for shape in landing_shapes] + [jax.ShapeDtypeStruct((8, 128), F32)],
        input_output_aliases={i: 2 * n + i for i in range(2 * n)},
        compiler_params=pltpu.CompilerParams(has_side_effects=SPLIT_COPY_EFFECT),
    )(*[_in_hbm(a) for a in srcs], *[_in_hbm(b) for b in landing])
    return out[:n], out[n:2 * n], out[2 * n:3 * n], out[3 * n:4 * n], out[-1]


def sibling_exchange_wait(srcs, landing, send_sems, recv_sems, view, after, name):
    n = len(srcs)

    def body(*refs):
        ins, land, sems = refs[:n], refs[n:2 * n], refs[2 * n:4 * n]
        x, y, c = _place()
        for w in range(n):
            cp = _remote(view(ins[w], w, c), land[w], sems[w], sems[n + w], (x, y, c))
            cp.wait_send()
            cp.wait_recv()

    out = pl.pallas_call(
        body, name=name, in_specs=[HBM] * (2 * n) + [SEM] * (2 * n) + [ANY] * len(after), out_specs=[HBM] * (2 * n),
        out_shape=[pltpu.HBM(a.shape, a.dtype) for a in srcs] + [pltpu.HBM(b.shape, b.dtype) for b in landing],
        input_output_aliases={i: i for i in range(2 * n)},
        compiler_params=pltpu.CompilerParams(has_side_effects=SPLIT_COPY_EFFECT),
    )(*srcs, *landing, *send_sems, *recv_sems, *after)
    return out[:n], out[n:]


def _scatter_copies(halves, got, ws, send_sems, recv_sems, start):
    x, y, c = _place()
    copies = []
    for w, (_, r, cc, ax) in enumerate(ws):
        for k, chip in enumerate(_other_chips(x, y)):
            args = (_piece_of_half(halves[w], r, cc, ax, chip), got[w].at[k], send_sems.at[w, k], recv_sems.at[w, k], (*chip, c))
            copies.append(_start_remote(*args) if start else _remote(*args))
    return copies


def scatter_start(halves, group):
    ws = [BIG[i] for i in GROUPS[group]]
    n = len(ws)

    def body(*refs):
        sems = refs[2 * n:8 * n]
        _scatter_copies(refs[:n], refs[n:2 * n], ws, _SemList(sems[:3 * n]), _SemList(sems[3 * n:]), start=True)
        refs[-1][...] = jnp.zeros_like(refs[-1])

    landing = [lax.empty((3,) + _piece_shape(r, cc, ax), GRAD_COMM_DTYPE) for _, r, cc, ax in ws]
    out = pl.pallas_call(
        body, name=f"scatter_start_{group}", in_specs=[HBM] * (2 * n),
        out_specs=[SEM] * (6 * n) + [HBM] * (2 * n) + [pl.BlockSpec(memory_space=pltpu.VMEM)],
        out_shape=[pltpu.SemaphoreType.DMA(())] * (6 * n)
        + [pltpu.HBM(_half_shape(r, cc, ax), GRAD_COMM_DTYPE) for _, r, cc, ax in ws]
        + [pltpu.HBM((3,) + _piece_shape(r, cc, ax), GRAD_COMM_DTYPE) for _, r, cc, ax in ws]
        + [jax.ShapeDtypeStruct((8, 128), F32)],
        input_output_aliases={i: 6 * n + i for i in range(2 * n)},
        compiler_params=pltpu.CompilerParams(has_side_effects=SPLIT_COPY_EFFECT),
    )(*[_in_hbm(h) for h in halves], *[_in_hbm(b) for b in landing])
    return out[:3 * n], out[3 * n:6 * n], out[6 * n:7 * n], out[7 * n:8 * n], out[-1]


def scatter_wait(halves, got, send_sems, recv_sems, after, group):
    ws = [BIG[i] for i in GROUPS[group]]
    n = len(ws)

    def body(*refs):
        sems = refs[2 * n:8 * n]
        for cp in _scatter_copies(refs[:n], refs[n:2 * n], ws, _SemList(sems[:3 * n]), _SemList(sems[3 * n:]), start=False):
            cp.wait_send()
            cp.wait_recv()

    out = pl.pallas_call(
        body, name=f"scatter_wait_{group}", in_specs=[HBM] * (2 * n) + [SEM] * (6 * n) + [ANY] * len(after),
        out_specs=[HBM] * (2 * n),
        out_shape=[pltpu.HBM(_half_shape(r, cc, ax), GRAD_COMM_DTYPE) for _, r, cc, ax in ws]
        + [pltpu.HBM((3,) + _piece_shape(r, cc, ax), GRAD_COMM_DTYPE) for _, r, cc, ax in ws],
        input_output_aliases={i: i for i in range(2 * n)},
        compiler_params=pltpu.CompilerParams(has_side_effects=SPLIT_COPY_EFFECT),
    )(*halves, *got, *send_sems, *recv_sems, *after)
    return out[:n], out[n:]


N_DEV = 8
SMALL = ("ffn1_norm", "mix_norm", "hg_lower_bounds", "hg_out_norm", "ffn2_norm", "final_norm")
SMALL_STAGE_ROWS = 8


def small_step(loss, grads, w, m, v, behind):
    n = len(SMALL)
    shapes = [g.shape for g in grads]
    first_row = [sum(s[0] for s in shapes[:i]) for i in range(n + 1)]
    assert first_row[n] < SMALL_STAGE_ROWS
    loss_row = (pl.ds(first_row[n], 1), pl.ds(0, loss.shape[1]))

    def body(*refs):
        loss_ref, g_refs, w_refs, m_refs, v_refs = refs[0], refs[1:1 + n], refs[1 + n:1 + 2 * n], refs[1 + 2 * n:1 + 3 * n], refs[1 + 3 * n:1 + 4 * n]
        outs = refs[2 + 4 * n:3 + 8 * n]
        loss_out, dg_refs, d_refs, nm_refs, nv_refs = outs[0], outs[1:1 + n], outs[1 + n:1 + 2 * n], outs[1 + 2 * n:1 + 3 * n], outs[1 + 3 * n:]
        stage, gathered, send_sems, recv_sems = refs[3 + 8 * n:]
        x, y, c = _place()
        me = 4 * x + 2 * y + c

        def slot(i, shape):
            return pl.ds(first_row[i], shape[0]), pl.ds(0, shape[1])

        stage[...] = jnp.zeros_like(stage)
        for i, g_ref in enumerate(g_refs):
            stage[slot(i, shapes[i])] = g_ref[...]
        stage[loss_row] = loss_ref[pl.ds(0, 1), :]
        gathered[me] = stage[...]
        copies = []
        for k in range(1, N_DEV):
            peer = (x ^ (k >> 2), y ^ ((k >> 1) & 1), c ^ (k & 1))
            cp = pltpu.make_async_remote_copy(
                src_ref=stage, dst_ref=gathered.at[me], send_sem=send_sems.at[k - 1], recv_sem=recv_sems.at[k - 1],
                device_id=peer, device_id_type=MESH)
            cp.start()
            copies.append(cp)
        for cp in copies:
            cp.wait()
        acc = gathered[0]
        for k in range(1, N_DEV):
            acc = acc + gathered[k]
        stage[...] = acc
        loss_out[...] = jnp.broadcast_to(stage[loss_row], loss_out.shape)
        for i in range(n):
            g = stage[slot(i, shapes[i])]
            dg_refs[i][...] = g
            d_refs[i][...], nm_refs[i][...], nv_refs[i][...] = _adamw_math(w_refs[i][...], g, m_refs[i][...], v_refs[i][...])

    vm = pl.BlockSpec(memory_space=pltpu.VMEM)
    per_param = [jax.ShapeDtypeStruct(s, F32) for s in shapes]
    out = pl.pallas_call(
        body, name="small_step", in_specs=[vm] * (1 + 4 * n) + [ANY], out_specs=[vm] * (1 + 4 * n),
        out_shape=[jax.ShapeDtypeStruct(loss.shape, F32)] + per_param * 4,
        scratch_shapes=[pltpu.VMEM((SMALL_STAGE_ROWS, D_MODEL), F32),
                        pltpu.VMEM((N_DEV, SMALL_STAGE_ROWS, D_MODEL), F32),
                        pltpu.SemaphoreType.DMA((N_DEV - 1,)), pltpu.SemaphoreType.DMA((N_DEV - 1,))],
    )(loss, *grads, *w, *m, *v, behind)
    return out[0], out[1:1 + n], out[1 + n:1 + 2 * n], out[1 + 2 * n:1 + 3 * n], out[1 + 3 * n:]


def _swiglu_block_fwd(h, norm_g, w_gu, w_down, tag, behind=()):
    n = rmsnorm_fwd(h, norm_g, f"{tag}_norm", behind=behind)
    a, b, s = gate_up_swiglu(n, w_gu, f"{tag}_gate_up")
    h_out = matmul(s, w_down, res=h, scale=0.5, name=f"{tag}_down")
    return h_out, (n, a, b, s)


def _swiglu_block_bwd(h, norm_g, w_gu, w_down, saved, dh_out, df, tag, exchange, behind=()):
    n, a, b, s = saved
    d_down = matmul(s, df, ta=True, scale=0.5, out_dtype=GRAD_COMM_DTYPE, name=f"{tag}_d_w_down")
    ds = matmul(df, w_down, tb=True, scale=0.5, out_dtype=ACT_DTYPE, behind=behind, name=f"{tag}_d_s")
    dgu = swiglu_bwd(a, b, ds, f"{tag}_swiglu_bwd")
    d_gu = matmul(n, dgu, ta=True, out_dtype=GRAD_COMM_DTYPE, name=f"{tag}_d_w_gate_up")
    tokens = exchange.gradients_ready(tag, {f"{tag}_w_gate_up": d_gu, f"{tag}_w_down": d_down})
    dn = matmul(dgu, w_gu, tb=True, behind=tokens, name=f"{tag}_d_n")
    dh, dh_m, dg = rmsnorm_bwd(h, norm_g, dn, dh_out, f"{tag}_norm_bwd")
    return dh, dh_m, dg


def local_step(x, target, small, exchange):
    big = {}
    token, big_ffn1 = exchange.weights("ffn1", x)
    big.update(big_ffn1)
    h1, saved1 = _swiglu_block_fwd(x, small["ffn1_norm"], big["ffn1_w_gate_up"], big["ffn1_w_down"], "ffn1", token)
    token, big_mix = exchange.weights("mix", h1)
    big.update(big_mix)
    u = rmsnorm_fwd(h1, small["mix_norm"], "mix_norm", behind=token)
    z = matmul(u, big["w_in"], name="w_in")
    p = small["hg_lower_bounds"]
    lb = 1.0 / (1.0 + jnp.exp(p[1:2] - p[0:1]))
    y_hg, o_raw, states = hgrn_fwd(z, lb, small["hg_out_norm"], "hgrn_fwd")
    o_att, l_att = zip(*[att_fwd(z, g, f"att_fwd_{g}") for g in range(N_GROUPS)])
    y_att = att_combine_fwd(o_att, l_att, "att_combine")
    bh = matmul(y_hg, big["w_branch_hg"], name="branch_hg")
    ba = matmul(y_att, big["w_branch_att"], name="branch_att")
    merged = merge_fwd(z, bh, ba, "merge")
    h2 = matmul(merged, big["w_out"], res=h1, name="w_out")
    token, big_ffn2 = exchange.weights("ffn2", h2)
    big.update(big_ffn2)
    h3, saved2 = _swiglu_block_fwd(h2, small["ffn2_norm"], big["ffn2_w_gate_up"], big["ffn2_w_down"], "ffn2", token)
    dh3, dh3_m, d_final, loss = final_norm_loss(h3, small["final_norm"], target, "final_norm_loss")

    gs, gb = {"final_norm": d_final}, {}
    dh2, dh2_m, gs["ffn2_norm"] = _swiglu_block_bwd(
        h2, small["ffn2_norm"], big["ffn2_w_gate_up"], big["ffn2_w_down"], saved2, dh3, dh3_m, "ffn2", exchange)
    token = exchange.backward_done("ffn2", dh2)
    gb["w_out"] = matmul(merged, dh2_m, ta=True, out_dtype=GRAD_COMM_DTYPE, name="d_w_out")
    dmerged = matmul(dh2_m, big["w_out"], tb=True, behind=token, name="d_merged")
    dbh, dba, dgh, dga = merge_bwd(z, bh, ba, dmerged, "merge_bwd")
    gb["w_branch_hg"] = matmul(y_hg, dbh, ta=True, out_dtype=GRAD_COMM_DTYPE, name="d_w_branch_hg")
    gb["w_branch_att"] = matmul(y_att, dba, ta=True, out_dtype=GRAD_COMM_DTYPE, name="d_w_branch_att")
    dy_hg = matmul(dbh, big["w_branch_hg"], tb=True, name="d_y_hg")
    dy_att = matmul(dba, big["w_branch_att"], tb=True, name="d_y_att")
    dq, dfp, di, dog, d_lb, gs["hg_out_norm"] = hgrn_bwd(z, lb, small["hg_out_norm"], o_raw, states, dy_hg, "hgrn_bwd")
    do_att, corr = att_combine_bwd(o_att, l_att, dy_att, "att_combine_bwd")
    d_att = [part for g in range(N_GROUPS) for part in att_bwd(z, l_att[g], do_att[g], corr[g], g, f"att_bwd_{g}")]
    dz = jnp.concatenate([dq, dfp, di, dog, *d_att, dgh, dga], axis=1)
    gb["w_in"] = matmul(u, dz, ta=True, out_dtype=GRAD_COMM_DTYPE, name="d_w_in")
    token = exchange.gradients_ready("mix", gb)
    du = matmul(dz, big["w_in"], tb=True, behind=token, name="d_u")
    dh1, dh1_m, gs["mix_norm"] = rmsnorm_bwd(h1, small["mix_norm"], du, dh2, "mix_norm_bwd")
    token = exchange.backward_done("mix", dh1)
    dp0 = d_lb * lb * (1.0 - lb)
    gs["hg_lower_bounds"] = jnp.concatenate([dp0, -dp0], axis=0)
    dx, _, gs["ffn1_norm"] = _swiglu_block_bwd(
        x, small["ffn1_norm"], big["ffn1_w_gate_up"], big["ffn1_w_down"], saved1, dh1, dh1_m, "ffn1", exchange, token)
    exchange.backward_done("ffn1", dx)
    return loss, dx, gs


WEIGHTS = ("ffn1_norm", "ffn1_w_gate_up", "ffn1_w_down", "mix_norm", "w_in", "hg_lower_bounds", "hg_out_norm",
           "w_branch_hg", "w_branch_att", "w_out", "ffn2_norm", "ffn2_w_gate_up", "ffn2_w_down", "final_norm")


class WeightExchange:
    ORDER = ("ffn1", "mix", "ffn2")

    def __init__(self, shards, core, chip):
        self.core, self.chip = core, chip
        self.halving = None
        self.scattering = None
        self.reducing = {}
        first = self.ORDER[0]
        self.placed = {BIG[i][0]: place_own_block(shards[BIG[i][0]], chip, *BIG[i][1:], f"place_{BIG[i][0]}")
                       for i in GROUPS[first]}
        self._start_gather(first, self.placed[self._names(first)[0]])
        chip_behind = chip + self.token[0, :1].astype(jnp.int32)
        for group in self.ORDER[1:]:
            for i in GROUPS[group]:
                n, r, cc, ax = BIG[i]
                self.placed[n] = place_own_block(shards[n], chip_behind, r, cc, ax, f"place_{n}")
        self.placed_behind = [self.placed[n] for group in self.ORDER[1:] for n in self._names(group)]

    def _names(self, group):
        return [BIG[i][0] for i in GROUPS[group]]

    def _start_gather(self, group, after):
        send_sems, recv_sems, bufs, self.token = gather_start([self.placed[n] for n in self._names(group)], after, group)
        self.gathering = (group, send_sems, recv_sems, bufs)

    def weights(self, group, h):
        pending, send_sems, recv_sems, bufs = self.gathering
        assert pending == group
        after = self.placed_behind if group == self.ORDER[0] else [h]
        whole = gather_forward(gather_wait(bufs, send_sems, recv_sems, after, group), group)
        later = self.ORDER.index(group) + 1
        behind = []
        if later < len(self.ORDER):
            self._start_gather(self.ORDER[later], whole[0])
            behind = [self.token]
        return behind, dict(zip(self._names(group), whole))

    @staticmethod
    def _half_to_sibling(ws):
        return lambda ref, w, c: _half(ref, *ws[w][1:], 1 - c)

    def gradients_ready(self, group, grads):
        ws = [BIG[i] for i in GROUPS[group]]
        send_sems, recv_sems, own, theirs, token = sibling_exchange_start(
            [grads[n] for n, *_ in ws], self._half_to_sibling(ws), [_half_shape(r, cc, ax) for _, r, cc, ax in ws],
            GRAD_COMM_DTYPE, f"halves_start_{group}")
        self.halving = (group, send_sems, recv_sems, own, theirs)
        return [token]

    def backward_done(self, group, dh):
        pending, send_sems, recv_sems, own, theirs = self.halving
        assert pending == group
        ws = [BIG[i] for i in GROUPS[group]]
        own, theirs = sibling_exchange_wait(own, theirs, send_sems, recv_sems, self._half_to_sibling(ws), [dh],
                                            f"halves_wait_{group}")
        halves = [add_halves(g, t, self.core, r, cc, ax, f"add_halves_{n}") for (n, r, cc, ax), g, t in zip(ws, own, theirs)]
        previous = self.scattering
        send_sems, recv_sems, halves, got, self.token = scatter_start(halves, group)
        self.scattering = (group, send_sems, recv_sems, halves, got)
        behind = [self._finish_scatter(previous, [self.token])] if previous is not None else []
        return behind + [self.token]

    def _finish_scatter(self, scattering, after):
        group, send_sems, recv_sems, halves, got = scattering
        halves, got = scatter_wait(halves, got, send_sems, recv_sems, after, group)
        ws = [BIG[i] for i in GROUPS[group]]
        mine = [add_pieces(h, g, self.chip, r, cc, ax, f"add_pieces_{n}") for (n, r, cc, ax), h, g in zip(ws, halves, got)]
        send_sems, recv_sems, mine, theirs, token = sibling_exchange_start(
            mine, lambda ref, w, c: ref, [_piece_shape(r, cc, ax) for _, r, cc, ax in ws], F32, f"reduced_start_{group}")
        self.reducing[group] = (send_sems, recv_sems, mine, theirs)
        return token

    def finish(self, after):
        return self._finish_scatter(self.scattering, after)

    def reduced_halves(self, group, after):
        send_sems, recv_sems, mine, theirs = self.reducing.pop(group)
        mine, theirs = sibling_exchange_wait(mine, theirs, send_sems, recv_sems, lambda ref, w, c: ref, after,
                                             f"reduced_wait_{group}")
        return {BIG[i][0]: (a, b) for i, a, b in zip(GROUPS[group], mine, theirs)}


def kernel(x, ffn1_norm, ffn1_w_gate_up, ffn1_w_down, mix_norm, w_in, hg_lower_bounds, hg_out_norm, w_branch_hg, w_branch_att, w_out, ffn2_norm, ffn2_w_gate_up, ffn2_w_down, final_norm, loss_target, m_ffn1_norm, m_ffn1_w_gate_up, m_ffn1_w_down, m_mix_norm, m_w_in, m_hg_lower_bounds, m_hg_out_norm, m_w_branch_hg, m_w_branch_att, m_w_out, m_ffn2_norm, m_ffn2_w_gate_up, m_ffn2_w_down, m_final_norm, v_ffn1_norm, v_ffn1_w_gate_up, v_ffn1_w_down, v_mix_norm, v_w_in, v_hg_lower_bounds, v_hg_out_norm, v_w_branch_hg, v_w_branch_att, v_w_out, v_ffn2_norm, v_ffn2_w_gate_up, v_ffn2_w_down, v_final_norm):
    w = dict(ffn1_norm=ffn1_norm, ffn1_w_gate_up=ffn1_w_gate_up, ffn1_w_down=ffn1_w_down, mix_norm=mix_norm, w_in=w_in,
             hg_lower_bounds=hg_lower_bounds, hg_out_norm=hg_out_norm, w_branch_hg=w_branch_hg, w_branch_att=w_branch_att,
             w_out=w_out, ffn2_norm=ffn2_norm, ffn2_w_gate_up=ffn2_w_gate_up, ffn2_w_down=ffn2_w_down, final_norm=final_norm)
    m = dict(ffn1_norm=m_ffn1_norm, ffn1_w_gate_up=m_ffn1_w_gate_up, ffn1_w_down=m_ffn1_w_down, mix_norm=m_mix_norm,
             w_in=m_w_in, hg_lower_bounds=m_hg_lower_bounds, hg_out_norm=m_hg_out_norm, w_branch_hg=m_w_branch_hg,
             w_branch_att=m_w_branch_att, w_out=m_w_out, ffn2_norm=m_ffn2_norm, ffn2_w_gate_up=m_ffn2_w_gate_up,
             ffn2_w_down=m_ffn2_w_down, final_norm=m_final_norm)
    v = dict(ffn1_norm=v_ffn1_norm, ffn1_w_gate_up=v_ffn1_w_gate_up, ffn1_w_down=v_ffn1_w_down, mix_norm=v_mix_norm,
             w_in=v_w_in, hg_lower_bounds=v_hg_lower_bounds, hg_out_norm=v_hg_out_norm, w_branch_hg=v_w_branch_hg,
             w_branch_att=v_w_branch_att, w_out=v_w_out, ffn2_norm=v_ffn2_norm, ffn2_w_gate_up=v_ffn2_w_gate_up,
             ffn2_w_down=v_ffn2_w_down, final_norm=v_final_norm)

    core = lax.axis_index("c").astype(jnp.int32).reshape(1)
    chip = (2 * lax.axis_index("x") + lax.axis_index("y")).astype(jnp.int32).reshape(1)
    exchange = WeightExchange({n: w[n][0] for n, *_ in BIG}, core, chip)
    small = {n: w[n] for n in SMALL}
    small["final_norm"] = final_norm.reshape(1, D_MODEL)

    loss, dx, gs = local_step(x[0], loss_target[0], small, exchange)

    grads, delta, new_m, new_v = {}, {}, {}, {}

    def update(group, core, after):
        reduced = exchange.reduced_halves(group, after)
        for i in GROUPS[group]:
            n, r, cc, ax = BIG[i]
            a, b = reduced[n]
            g, d, nm, nv = adamw_halves(w[n][0], a, b, m[n][0], v[n][0], core, r, cc, ax, f"adamw_{n}")
            grads[n], delta[n], new_m[n], new_v[n] = g[None], d[None], nm[None], nv[None]

    core_behind = core + exchange.token[0, :1].astype(jnp.int32)
    update("ffn2", core_behind, [exchange.token])
    update("mix", core_behind, [delta["ffn2_w_down"]])
    token = exchange.finish(after=[delta[BIG[i][0]] for group in ("ffn2", "mix") for i in GROUPS[group]])
    two_d = lambda a: a.reshape(1, D_MODEL) if a.ndim == 1 else a
    loss_sum, *small_out = small_step(loss, [gs[n] for n in SMALL], *[[two_d(p[n]) for n in SMALL] for p in (w, m, v)],
                                      behind=token)
    for result, parts in zip((grads, delta, new_m, new_v), small_out):
        result.update({n: a.reshape(w[n].shape) for n, a in zip(SMALL, parts)})
    update("ffn1", core, [loss_sum])

    return (loss_sum[0, 0], dx[None], *[grads[n] for n in WEIGHTS], *[delta[n] for n in WEIGHTS],
            *[new_m[n] for n in WEIGHTS], *[new_v[n] for n in WEIGHTS])
```

```python
import numpy as np
import jax
import jax.numpy as jnp
from jax import lax
from jax.experimental import pallas as pl
from jax.experimental.pallas import tpu as pltpu

SEQ = 2048
D_MODEL = 1024
D_FF = 2816
HG_HEADS = 4
HG_DIM = 128
HG_WIDTH = 512
HG_CHUNK = 64
ATT_GROUPS = ((128, 1), (512, 4), (2048, 16))
ATT_HEADS = 8
ATT_WIDTH = 512
ATT_BLOCK = 128
ALIBI_MAX = 8.0
IN_COLS = 8704
EPS = 1e-6
NEG_INF = -1e30
ADAM_LR = 0.001
ADAM_B1 = 0.9
ADAM_B2 = 0.999
ADAM_EPS = 1e-08
ADAM_WD = 0.01
ADAM_STEP = 10

N_CHIPS = 4
MXU_DTYPE = jnp.bfloat16
WEIGHT_COMM_DTYPE = jnp.bfloat16
GRAD_COMM_DTYPE = jnp.bfloat16
ACT_DTYPE = jnp.bfloat16
MESH = pl.DeviceIdType.MESH
F32 = jnp.float32


def _sigmoid(x):
    return 1.0 / (1.0 + jnp.exp(-x))


def _dot(a, b, ta=False, tb=False):
    dn = (((0 if ta else 1,), (1 if tb else 0,)), ((), ()))
    return lax.dot_general(a.astype(MXU_DTYPE), b.astype(MXU_DTYPE), dn, preferred_element_type=F32)


def _dot_f32(a, b, ones_on_right=False):
    x = a if ones_on_right else b
    hi = x.astype(jnp.bfloat16)
    rest = x - hi.astype(F32)
    mid = rest.astype(jnp.bfloat16)
    lo = (rest - mid.astype(F32)).astype(jnp.bfloat16)
    if ones_on_right:
        dot = lambda q: jnp.dot(q, b.astype(jnp.bfloat16), preferred_element_type=F32)
    else:
        dot = lambda q: jnp.dot(a.astype(jnp.bfloat16), q, preferred_element_type=F32)
    return dot(hi) + (dot(mid) + dot(lo))


def _split_bf16(x):
    hi = x.astype(jnp.bfloat16)
    return hi, (x - hi.astype(F32)).astype(jnp.bfloat16)


def _hdot(a, b, ta=False, tb=False):
    dn =(((0 if ta else 1,), (1 if tb else 0,)), ((), ()))
    (a_hi, a_lo), (b_hi, b_lo) = _split_bf16(a), _split_bf16(b)
    dot = lambda p, q: lax.dot_general(p, q, dn, preferred_element_type=F32)
    return dot(a_hi, b_hi) + (dot(a_lo, b_hi) + dot(a_hi, b_lo))


def _in_hbm(a):
    return pltpu.with_memory_space_constraint(a, pltpu.HBM)


def pallas_call(body, **kw):
    grid_spec = kw.get("grid_spec")
    specs = list(kw["in_specs"] if grid_spec is None else grid_spec.in_specs)
    n_prefetch = 0 if grid_spec is None else grid_spec.num_scalar_prefetch
    out_specs = kw["out_specs"] if grid_spec is None else grid_spec.out_specs
    one = not isinstance(kw["out_shape"], (list, tuple))
    shapes = [kw["out_shape"]] if one else list(kw["out_shape"])
    out_specs = [out_specs] if one else list(out_specs)
    shapes = [pltpu.HBM(a.shape, a.dtype) if s.memory_space is None and isinstance(a, jax.ShapeDtypeStruct) else a
              for a, s in zip(shapes, out_specs)]
    kw["out_shape"] = shapes[0] if one else shapes
    call = pl.pallas_call(body, **kw)

    def run(*args):
        assert len(args) == n_prefetch + len(specs)
        pinned = [_in_hbm(a) if s.memory_space is None else a for a, s in zip(args[n_prefetch:], specs)]
        return call(*args[:n_prefetch], *pinned)

    return run


MATMUL_VMEM_BYTES = 48 * 1024 * 1024
MATMUL_TILE_BYTES = 36 * 1024 * 1024
MXU_ALIGN = 128


def _divisors(n, most):
    return [t for t in range(min(n, most), 0, -MXU_ALIGN) if n % t == 0 and t % MXU_ALIGN == 0]


def _matmul_tiles(M, N, K, in_bytes, out_bytes, has_res):
    best = None
    for tk in _divisors(K, K):
        nk = K // tk
        for tm in _divisors(M, 2048):
            for tn in _divisors(N, 512):
                tiles = 2 * in_bytes * (tm * tk + tk * tn) + 2 * out_bytes * tm * tn
                tiles += 4 * tm * tn * ((nk > 1) + 2 * has_res)
                if tiles > MATMUL_TILE_BYTES:
                    continue
                traffic = in_bytes * (M * K * (1 if nk == 1 else N // tn) + K * N * (M // tm))
                key = (traffic, -tm * tn * tk)
                if best is None or key < best[0]:
                    best = (key, (tm, tn, tk))
    return best[1]


def matmul(a, b, *, ta=False, tb=False, out_dtype=F32, res=None, scale=1.0, behind=(), name):
    if ta:
        K, M = a.shape
    else:
        M, K = a.shape
    if tb:
        N, K2 = b.shape
    else:
        K2, N = b.shape
    assert K == K2 and a.dtype == b.dtype
    tm, tn, tk = _matmul_tiles(M, N, K, a.dtype.itemsize, jnp.dtype(out_dtype).itemsize, res is not None)
    nk = K // tk

    def finish(r, r_ref, o_ref):
        if scale != 1.0:
            r = r * scale
        if res is not None:
            r = r_ref[...] + r
        o_ref[...] = r.astype(out_dtype)

    def body(*refs):
        a_ref, b_ref = refs[:2]
        r_ref = refs[2] if res is not None else None
        o_ref = refs[2 + (res is not None) + len(behind)]
        if nk == 1:
            finish(_dot(a_ref[...], b_ref[...], ta, tb), r_ref, o_ref)
            return
        acc = refs[-1]
        k = pl.program_id(2)

        @pl.when(k == 0)
        def _():
            acc[...] = jnp.zeros_like(acc)

        acc[...] += _dot(a_ref[...], b_ref[...], ta, tb)

        @pl.when(k == nk - 1)
        def _():
            finish(acc[...], r_ref, o_ref)

    a_spec = pl.BlockSpec((tk, tm), lambda i, j, k: (k, i)) if ta else pl.BlockSpec((tm, tk), lambda i, j, k: (i, k))
    b_spec = pl.BlockSpec((tn, tk), lambda i, j, k: (j, k)) if tb else pl.BlockSpec((tk, tn), lambda i, j, k: (k, j))
    in_specs = [a_spec, b_spec]
    args = [a, b]
    if res is not None:
        in_specs.append(pl.BlockSpec((tm, tn), lambda i, j, k: (i, j)))
        args.append(res)
    for earlier in behind:
        in_specs.append(pl.BlockSpec(memory_space=pl.ANY))
        args.append(earlier)
    return pallas_call(
        body, name=name, grid=(M // tm, N // tn, nk), in_specs=in_specs,
        out_specs=pl.BlockSpec((tm, tn), lambda i, j, k: (i, j)),
        out_shape=jax.ShapeDtypeStruct((M, N), out_dtype),
        scratch_shapes=[pltpu.VMEM((tm, tn), F32)] if nk > 1 else [],
        compiler_params=pltpu.CompilerParams(dimension_semantics=("parallel", "parallel", "arbitrary"),
                                             vmem_limit_bytes=MATMUL_VMEM_BYTES),
    )(*args)


ROW_TILE = 256


def rmsnorm_fwd(x, g, name, behind=()):
    def body(x_ref, g_ref, *refs):
        n_ref = refs[-1]
        xv = x_ref[...]
        r = lax.rsqrt(jnp.mean(xv * xv, axis=-1, keepdims=True) + EPS)
        n_ref[...] = ((xv * r) * g_ref[...]).astype(n_ref.dtype)

    order = list(behind)
    return pallas_call(
        body, name=name, grid=(SEQ // ROW_TILE,),
        in_specs=[pl.BlockSpec((ROW_TILE, D_MODEL), lambda i: (i, 0)), pl.BlockSpec((1, D_MODEL), lambda i: (0, 0))]
        + [pl.BlockSpec(memory_space=pl.ANY)] * len(order),
        out_specs=pl.BlockSpec((ROW_TILE, D_MODEL), lambda i: (i, 0)),
        out_shape=jax.ShapeDtypeStruct((SEQ, D_MODEL), MXU_DTYPE),
    )(x, g, *order)


def rmsnorm_bwd(x, g, dn, dres, name):
    def body(x_ref, g_ref, dn_ref, dr_ref, dx_ref, dxm_ref, dg_ref):
        xv = x_ref[...]
        r = lax.rsqrt(jnp.mean(xv * xv, axis=-1, keepdims=True) + EPS)
        xh = xv * r
        dnv = dn_ref[...]

        @pl.when(pl.program_id(0) == 0)
        def _():
            dg_ref[...] = jnp.zeros_like(dg_ref)

        dg_ref[...] += jnp.sum(dnv * xh, axis=0, keepdims=True)
        dxh = dnv * g_ref[...]
        dx = dr_ref[...] + r * (dxh - xh * jnp.mean(dxh * xh, axis=-1, keepdims=True))
        dx_ref[...] = dx
        dxm_ref[...] = dx.astype(dxm_ref.dtype)

    row = pl.BlockSpec((ROW_TILE, D_MODEL), lambda i: (i, 0))
    vec = pl.BlockSpec((1, D_MODEL), lambda i: (0, 0))
    return pallas_call(
        body, name=name, grid=(SEQ // ROW_TILE,), in_specs=[row, vec, row, row], out_specs=[row, row, vec],
        out_shape=[jax.ShapeDtypeStruct((SEQ, D_MODEL), F32), jax.ShapeDtypeStruct((SEQ, D_MODEL), MXU_DTYPE),
                   jax.ShapeDtypeStruct((1, D_MODEL), F32)],
        compiler_params=pltpu.CompilerParams(dimension_semantics=("arbitrary",)),
    )(x, g, dn, dres)


def final_norm_loss(h, g, target, name):
    def body(h_ref, g_ref, t_ref, dh_ref, dhm_ref, dg_ref, loss_ref):
        xv = h_ref[...]
        r = lax.rsqrt(jnp.mean(xv * xv, axis=-1, keepdims=True) + EPS)
        xh = xv * r
        gv = g_ref[...]
        e = xh * gv - t_ref[...]

        @pl.when(pl.program_id(0) == 0)
        def _():
            dg_ref[...] = jnp.zeros_like(dg_ref)
            loss_ref[...] = jnp.zeros_like(loss_ref)

        part = 0.5 * jnp.sum(jnp.sum(e * e, axis=-1, keepdims=True) * (1.0 / D_MODEL), axis=0, keepdims=True)
        loss_ref[...] += jnp.broadcast_to(part, loss_ref.shape)
        dout = e * (1.0 / D_MODEL)
        dg_ref[...] += jnp.sum(dout * xh, axis=0, keepdims=True)
        dxh = dout * gv
        dh = r * (dxh - xh * jnp.mean(dxh * xh, axis=-1, keepdims=True))
        dh_ref[...] = dh
        dhm_ref[...] = dh.astype(dhm_ref.dtype)

    row = pl.BlockSpec((ROW_TILE, D_MODEL), lambda i: (i, 0))
    vec = pl.BlockSpec((1, D_MODEL), lambda i: (0, 0))
    return pallas_call(
        body, name=name, grid=(SEQ // ROW_TILE,), in_specs=[row, vec, row],
        out_specs=[row, row, vec, pl.BlockSpec((8, 128), lambda i: (0, 0))],
        out_shape=[jax.ShapeDtypeStruct((SEQ, D_MODEL), F32), jax.ShapeDtypeStruct((SEQ, D_MODEL), MXU_DTYPE),
                   jax.ShapeDtypeStruct((1, D_MODEL), F32), jax.ShapeDtypeStruct((8, 128), F32)],
        compiler_params=pltpu.CompilerParams(dimension_semantics=("arbitrary",)),
    )(h, g, target)


FFN_TILE = 256
FFN_TILES = D_FF // FFN_TILE


def gate_up_swiglu(n, w_gu, name):
    def body(n_ref, wa_ref, wb_ref, a_ref, b_ref, s_ref):
        nv = n_ref[...]
        a = _dot(nv, wa_ref[...])
        b = _dot(nv, wb_ref[...])
        a_ref[...] = a.astype(a_ref.dtype)
        b_ref[...] = b.astype(b_ref.dtype)
        s_ref[...] = (a * _sigmoid(a) * b).astype(s_ref.dtype)

    tile = pl.BlockSpec((SEQ, FFN_TILE), lambda j: (0, j))
    act = jax.ShapeDtypeStruct((SEQ, D_FF), ACT_DTYPE)
    return pallas_call(
        body, name=name, grid=(FFN_TILES,),
        in_specs=[pl.BlockSpec((SEQ, D_MODEL), lambda j: (0, 0)), pl.BlockSpec((D_MODEL, FFN_TILE), lambda j: (0, j)),
                  pl.BlockSpec((D_MODEL, FFN_TILE), lambda j: (0, j + FFN_TILES))],
        out_specs=[tile, tile, tile], out_shape=[act, act, jax.ShapeDtypeStruct((SEQ, D_FF), MXU_DTYPE)],
        compiler_params=pltpu.CompilerParams(dimension_semantics=("parallel",), vmem_limit_bytes=MATMUL_VMEM_BYTES),
    )(n, w_gu, w_gu)


def swiglu_bwd(a, b, ds, name):
    rows = ROW_TILE // 2

    def body(a_ref, b_ref, ds_ref, o_ref):
        av = a_ref[...].astype(F32)
        sg = _sigmoid(av)
        dsv = ds_ref[...].astype(F32)
        o_ref[:, :D_FF] = (dsv * b_ref[...].astype(F32) * (sg * (1.0 + av * (1.0 - sg)))).astype(o_ref.dtype)
        o_ref[:, D_FF:] = (dsv * av * sg).astype(o_ref.dtype)

    blk = pl.BlockSpec((rows, D_FF), lambda i: (i, 0))
    return pallas_call(
        body, name=name, grid=(SEQ // rows,), in_specs=[blk, blk, blk],
        out_specs=pl.BlockSpec((rows, 2 * D_FF), lambda i: (i, 0)),
        out_shape=jax.ShapeDtypeStruct((SEQ, 2 * D_FF), MXU_DTYPE), compiler_params=SUM_PARAMS,
    )(a, b, ds)


GATE_HG_BLK = 6656 // 512
GATE_ATT_BLK = 7680 // 512


def merge_fwd(z, bh, ba, name):
    def body(gh_ref, ga_ref, bh_ref, ba_ref, o_ref):
        o_ref[...] = (_sigmoid(gh_ref[...]) * bh_ref[...] + _sigmoid(ga_ref[...]) * ba_ref[...]).astype(o_ref.dtype)

    blk = pl.BlockSpec((ROW_TILE, 512), lambda i, j: (i, j))
    return pallas_call(
        body, name=name, grid=(SEQ // ROW_TILE, 2),
        in_specs=[pl.BlockSpec((ROW_TILE, 512), lambda i, j: (i, GATE_HG_BLK + j)),
                  pl.BlockSpec((ROW_TILE, 512), lambda i, j: (i, GATE_ATT_BLK + j)), blk, blk],
        out_specs=blk, out_shape=jax.ShapeDtypeStruct((SEQ, D_MODEL), MXU_DTYPE),
    )(z, z, bh, ba)


def merge_bwd(z, bh, ba, dm, name):
    def body(gh_ref, ga_ref, bh_ref, ba_ref, dm_ref, dbh_ref, dba_ref, dgh_ref, dga_ref):
        dmv = dm_ref[...]
        sh = _sigmoid(gh_ref[...])
        sa = _sigmoid(ga_ref[...])
        dbh_ref[...] = (dmv * sh).astype(dbh_ref.dtype)
        dba_ref[...] = (dmv * sa).astype(dba_ref.dtype)
        dgh_ref[...] = (dmv * bh_ref[...] * (sh * (1.0 - sh))).astype(dgh_ref.dtype)
        dga_ref[...] = (dmv * ba_ref[...] * (sa * (1.0 - sa))).astype(dga_ref.dtype)

    blk = pl.BlockSpec((ROW_TILE, 512), lambda i, j: (i, j))
    out = jax.ShapeDtypeStruct((SEQ, D_MODEL), MXU_DTYPE)
    return pallas_call(
        body, name=name, grid=(SEQ // ROW_TILE, 2),
        in_specs=[pl.BlockSpec((ROW_TILE, 512), lambda i, j: (i, GATE_HG_BLK + j)),
                  pl.BlockSpec((ROW_TILE, 512), lambda i, j: (i, GATE_ATT_BLK + j)), blk, blk, blk],
        out_specs=[blk, blk, blk, blk], out_shape=[out, out, out, out],
    )(z, z, bh, ba, dm)


N_CHUNKS = SEQ // HG_CHUNK
HG_STEP_CHUNKS = 4


def _hgrn_gates(q, fp, lb):
    C = HG_CHUNK
    sg = _sigmoid(fp)
    f = lb + (1.0 - lb) * sg
    lf = jnp.log(f)
    row = lax.broadcasted_iota(jnp.int32, (C, C), 0)
    col = lax.broadcasted_iota(jnp.int32, (C, C), 1)
    causal = row >= col
    G = _dot_f32(causal.astype(F32), lf)
    eG = jnp.exp(G)
    enG = jnp.exp(-G)
    qg = q * eG
    kg = (1.0 - f) * enG
    A = jnp.where(causal, _hdot(qg, kg, tb=True), 0.0)
    egl = jnp.exp(jnp.sum(lf, axis=0, keepdims=True))
    return sg, f, causal, eG, enG, qg, kg, A, egl


def hgrn_fwd(z, lb, gain, name):
    C, K = HG_CHUNK, HG_DIM

    def body(q_ref, f_ref, v_ref, og_ref, p_ref, g_ref, y_ref, o_ref, st_ref, state):
        @pl.when(pl.program_id(0) == 0)
        def _():
            state[...] = jnp.zeros_like(state)

        for cc in range(HG_STEP_CHUNKS):
            rows = pl.ds(cc * C, C)
            for h in range(HG_HEADS):
                hd = pl.ds(h * K, K)
                v = v_ref[rows, hd]
                _, _, _, _, _, qg, kg, A, egl = _hgrn_gates(q_ref[rows, hd], f_ref[rows, hd], p_ref[:, hd])
                st = state[h]
                st_ref[h, cc] = st
                o = _hdot(A, v) + _hdot(qg, st, tb=True)
                state[h] = st * egl + _hdot(v, kg * egl, ta=True)
                o_ref[rows, hd] = o
                rs = lax.rsqrt(jnp.mean(o * o, axis=-1, keepdims=True) + EPS)
                og = og_ref[rows, hd]
                y_ref[rows, hd] = (((o * rs) * g_ref[:, hd]) * (og * _sigmoid(og))).astype(y_ref.dtype)

    R = HG_STEP_CHUNKS * C

    def zcol(section):
        return pl.BlockSpec((R, HG_WIDTH), lambda c: (c, section))

    vec = pl.BlockSpec((1, HG_WIDTH), lambda c: (0, 0))
    blk = pl.BlockSpec((R, HG_WIDTH), lambda c: (c, 0))
    return pallas_call(
        body, name=name, grid=(N_CHUNKS // HG_STEP_CHUNKS,),
        in_specs=[zcol(0), zcol(1), zcol(2), zcol(3), vec, vec],
        out_specs=[blk, blk, pl.BlockSpec((HG_HEADS, HG_STEP_CHUNKS, K, K), lambda c: (0, c, 0, 0))],
        out_shape=[jax.ShapeDtypeStruct((SEQ, HG_WIDTH), MXU_DTYPE), jax.ShapeDtypeStruct((SEQ, HG_WIDTH), F32),
                   jax.ShapeDtypeStruct((HG_HEADS, N_CHUNKS, K, K), F32)],
        scratch_shapes=[pltpu.VMEM((HG_HEADS, K, K), F32)],
        compiler_params=pltpu.CompilerParams(dimension_semantics=("arbitrary",)),
    )(z, z, z, z, lb, gain)


def hgrn_bwd(z, lb, gain, o_raw, states, dy, name):
    C, K = HG_CHUNK, HG_DIM

    def body(q_ref, f_ref, v_ref, og_ref, p_ref, g_ref, o_ref, st_ref, dy_ref,
             dq_ref, dfp_ref, dv_ref, dog_ref, dlb_ref, dgain_ref, dstate):
        @pl.when(pl.program_id(0) == 0)
        def _():
            dstate[...] = jnp.zeros_like(dstate)
            dlb_ref[...] = jnp.zeros_like(dlb_ref)
            dgain_ref[...] = jnp.zeros_like(dgain_ref)

        last = lax.broadcasted_iota(jnp.int32, (C, K), 0) == C - 1
        row = lax.broadcasted_iota(jnp.int32, (C, C), 0)
        col = lax.broadcasted_iota(jnp.int32, (C, C), 1)
        anti_causal = (col >= row).astype(F32)
        for cc in reversed(range(HG_STEP_CHUNKS)):
            rows = pl.ds(cc * C, C)
            for h in range(HG_HEADS):
                hd = pl.ds(h * K, K)
                v = v_ref[rows, hd]
                lb = p_ref[:, hd]
                sg, f, causal, eG, enG, qg, kg, A, egl = _hgrn_gates(q_ref[rows, hd], f_ref[rows, hd], lb)
                kd = kg * egl
                st = st_ref[h, cc]
                dst = dstate[h]
                o = o_ref[rows, hd]
                og = og_ref[rows, hd]
                gain_v = g_ref[:, hd]
                dyv = dy_ref[rows, hd]
                rs = lax.rsqrt(jnp.mean(o * o, axis=-1, keepdims=True) + EPS)
                on = o * rs
                sgo = _sigmoid(og)
                silu = og * sgo
                dog_ref[rows, hd] = (dyv * (on * gain_v) * (sgo * (1.0 + og * (1.0 - sgo)))).astype(dog_ref.dtype)
                dgain_ref[:, hd] += jnp.sum(dyv * silu * on, axis=0, keepdims=True)
                don = dyv * gain_v * silu
                do = rs * (don - on * jnp.mean(don * on, axis=-1, keepdims=True))
                dA = jnp.where(causal, _hdot(do, v, tb=True), 0.0)
                dv_ref[rows, hd] = (_hdot(A, do, ta=True) + _hdot(kd, dst, tb=True)).astype(dv_ref.dtype)
                dqg = _hdot(dA, kg) + _hdot(do, st)
                dkg = _hdot(dA, qg, ta=True)
                dkd = _hdot(v, dst)
                dstate[h] = dst * egl + _hdot(do, qg, ta=True)
                dgl = jnp.sum(st * dst, axis=0, keepdims=True) * egl
                dq_ref[rows, hd] = (dqg * eG).astype(dq_ref.dtype)
                dk = dkg * enG + dkd * (enG * egl)
                dG = dqg * qg - dkg * kg - dkd * kd
                extra = jnp.sum(dkd * kd, axis=0, keepdims=True) + dgl
                dG = dG + jnp.where(last, extra, 0.0)
                dlf = _dot_f32(anti_causal, dG)
                df = dlf / f - dk
                dfp_ref[rows, hd] = (df * (1.0 - lb) * (sg * (1.0 - sg))).astype(dfp_ref.dtype)
                dlb_ref[:, hd] += jnp.sum(df * (1.0 - sg), axis=0, keepdims=True)

    R = HG_STEP_CHUNKS * C
    n_steps = N_CHUNKS // HG_STEP_CHUNKS

    def rc(c):
        return n_steps - 1 - c

    def zcol(section):
        return pl.BlockSpec((R, HG_WIDTH), lambda c: (rc(c), section))

    vec = pl.BlockSpec((1, HG_WIDTH), lambda c: (0, 0))
    blk = pl.BlockSpec((R, HG_WIDTH), lambda c: (rc(c), 0))
    out = jax.ShapeDtypeStruct((SEQ, HG_WIDTH), MXU_DTYPE)
    small = jax.ShapeDtypeStruct((1, HG_WIDTH), F32)
    return pallas_call(
        body, name=name, grid=(n_steps,),
        in_specs=[zcol(0), zcol(1), zcol(2), zcol(3), vec, vec, blk,
                  pl.BlockSpec((HG_HEADS, HG_STEP_CHUNKS, K, K), lambda c: (0, rc(c), 0, 0)), blk],
        out_specs=[blk, blk, blk, blk, vec, vec],
        out_shape=[out, out, out, out, small, small],
        scratch_shapes=[pltpu.VMEM((HG_HEADS, K, K), F32)],
        compiler_params=pltpu.CompilerParams(dimension_semantics=("arbitrary",)),
    )(z, z, z, z, lb, gain, o_raw, states, dy)


N_GROUPS = len(ATT_GROUPS)
HEAD_PAIRS = ATT_WIDTH // 128
ATT_COL0 = 4 * HG_WIDTH
UNROLLED_UNITS = 4


def _alibi_coef():
    n = N_GROUPS * ATT_HEADS
    slopes = np.exp2(-ALIBI_MAX * np.arange(1, n + 1, dtype=np.float32) / n).astype(np.float32)
    dil = np.repeat(np.array([d for _, d in ATT_GROUPS], np.float32), ATT_HEADS)
    return jnp.asarray(slopes * dil, F32)


def _for_each_unit(n, fn):
    if n <= UNROLLED_UNITS:
        for u in range(n):
            fn(u)
    else:
        def group(i, carry):
            for j in range(UNROLLED_UNITS):
                fn(i * UNROLLED_UNITS + j)
            return carry
        lax.fori_loop(0, n // UNROLLED_UNITS, group, 0)


def _att_geometry(g):
    B = ATT_BLOCK
    d = ATT_GROUPS[g][1]
    n_blocks = SEQ // (d * B)
    col0 = (ATT_COL0 + g * 3 * ATT_WIDTH) // 128

    def block_rows(b, r):
        return pl.ds(b * (B * d) + r, B, stride=d) if d > 1 else pl.ds(pl.multiple_of(b * B, B), B)

    def block_of(u):
        return (u, 0) if d == 1 else (u // d, u % d)

    return d, n_blocks, col0, block_rows, block_of


def _att_column(c):
    return pl.BlockSpec((SEQ, 128), lambda hp: (0, c + hp))


def _head_lanes(j):
    lane = lax.broadcasted_iota(jnp.int32, (ATT_BLOCK, 128), 1)
    return (lane >= 64 * j) & (lane < 64 * (j + 1))


def _stack_heads(x, sel0):
    return jnp.concatenate([jnp.where(sel0, x, 0.0), jnp.where(sel0, 0.0, x)], axis=0)


def _stack_values(x, sel0, lanes):
    swapped = pltpu.roll(x, 64, 1)
    stacked = jnp.concatenate([jnp.where(sel0, x, swapped), jnp.where(sel0, swapped, x)], axis=0)
    return stacked if lanes == 128 else jnp.concatenate([stacked] * (lanes // 128), axis=1)


def _pair_coef(coef_ref, g, hp):
    row = lax.broadcasted_iota(jnp.int32, (2 * ATT_BLOCK, 1), 0)
    first = g * ATT_HEADS + hp * 2
    return jnp.where(row < ATT_BLOCK, coef_ref[first], coef_ref[first + 1])


def _band(with_prev, first_key):
    B = ATT_BLOCK
    keys = 2 * B if with_prev else B
    qi = jnp.bitwise_and(lax.broadcasted_iota(jnp.int32, (2 * B, keys), 0), B - 1)
    kj = lax.broadcasted_iota(jnp.int32, (2 * B, keys), 1)
    delta = qi + (B if with_prev else 0) - kj
    valid = (delta >= 0) & (delta <= B)
    if with_prev:
        valid = valid & (kj >= first_key)
    return valid, delta.astype(F32)


def att_fwd(z, g, name):
    B = ATT_BLOCK
    d, n_blocks, col0, block_rows, block_of = _att_geometry(g)
    multi = n_blocks > 1

    def body(coef_ref, q_ref, k_ref, v_ref, o_ref, l_ref):
        cf2 = _pair_coef(coef_ref, g, pl.program_id(0))
        sel0 = _head_lanes(0)

        def one(u):
            b, r = block_of(u)
            rows = block_rows(b, r)
            valid, dist = _band(multi, jnp.where(b == 0, B, 0))
            q2 = _stack_heads(q_ref[rows, :], sel0)
            kk, vv = k_ref[rows, :], v_ref[rows, :]
            if multi:
                prev_rows = block_rows(jnp.maximum(b - 1, 0), r)
                kk = jnp.concatenate([k_ref[prev_rows, :], kk], axis=0)
                vv = jnp.concatenate([v_ref[prev_rows, :], vv], axis=0)
            sc = jnp.where(valid, _dot(q2, kk, tb=True) * 0.125 - cf2 * dist, NEG_INF)
            mx = jnp.max(sc, axis=-1, keepdims=True)
            e = jnp.exp(sc - mx)
            den = jnp.sum(e, axis=-1, keepdims=True)
            o2 = _dot(e * (1.0 / den), vv)
            lse2 = mx + jnp.log(den)
            o_ref[rows, :] = jnp.where(sel0, o2[:B], o2[B:])
            l_ref[rows, :] = jnp.where(sel0, lse2[:B], lse2[B:])

        _for_each_unit(d * n_blocks, one)

    out = jax.ShapeDtypeStruct((SEQ, ATT_WIDTH), F32)
    return pallas_call(
        body, name=name, grid=(HEAD_PAIRS,),
        in_specs=[pl.BlockSpec(memory_space=pltpu.SMEM), _att_column(col0), _att_column(col0 + 4), _att_column(col0 + 8)],
        out_specs=[_att_column(0), _att_column(0)], out_shape=[out, out],
        compiler_params=pltpu.CompilerParams(dimension_semantics=("parallel",)),
    )(_alibi_coef(), z, z, z)


def att_bwd(z, l, do, corr, g, name):
    B = ATT_BLOCK
    d, n_blocks, col0, block_rows, block_of = _att_geometry(g)
    multi = n_blocks > 1
    own = slice(B, 2 * B) if multi else slice(0, B)

    def body(coef_ref, q_ref, k_ref, v_ref, l_ref, do_ref, cr_ref, dq_ref, dk_ref, dv_ref, dq_sc, dk_sc, dv_sc):
        cf2 = _pair_coef(coef_ref, g, pl.program_id(0))
        sel0 = _head_lanes(0)

        def one(u):
            b, r = block_of(u)
            rows = block_rows(b, r)
            valid, dist = _band(multi, jnp.where(b == 0, B, 0))
            kk, vv = k_ref[rows, :], v_ref[rows, :]
            if multi:
                prev_rows = block_rows(jnp.maximum(b - 1, 0), r)
                kk = jnp.concatenate([k_ref[prev_rows, :], kk], axis=0)
                vv = jnp.concatenate([v_ref[prev_rows, :], vv], axis=0)
            q2, do2 = _stack_heads(q_ref[rows, :], sel0), _stack_heads(do_ref[rows, :], sel0)
            keys = kk.shape[0]
            lse2, cr2 = _stack_values(l_ref[rows, :], sel0, keys), _stack_values(cr_ref[rows, :], sel0, keys)
            p = jnp.exp(jnp.where(valid, _dot(q2, kk, tb=True) * 0.125 - cf2 * dist, NEG_INF) - lse2)
            ds = p * (_dot(do2, vv, tb=True) + cr2)
            dq2 = _dot(ds, kk)
            dkk = _dot(ds, q2, ta=True) * 0.125
            dvv = _dot(p, do2, ta=True)
            dq_sc[rows, :] = jnp.where(sel0, dq2[:B], dq2[B:]) * 0.125
            dk_sc[rows, :] = dkk[own]
            dv_sc[rows, :] = dvv[own]
            if multi:
                dk_sc[prev_rows, :] += dkk[:B]
                dv_sc[prev_rows, :] += dvv[:B]

        _for_each_unit(d * n_blocks, one)
        dq_ref[...] = dq_sc[...].astype(dq_ref.dtype)
        dk_ref[...] = dk_sc[...].astype(dk_ref.dtype)
        dv_ref[...] = dv_sc[...].astype(dv_ref.dtype)

    col = _att_column
    out = jax.ShapeDtypeStruct((SEQ, ATT_WIDTH), MXU_DTYPE)
    return pallas_call(
        body, name=name, grid=(HEAD_PAIRS,),
        in_specs=[pl.BlockSpec(memory_space=pltpu.SMEM), col(col0), col(col0 + 4), col(col0 + 8), col(0), col(0), col(0)],
        out_specs=[col(0)] * 3, out_shape=[out] * 3,
        scratch_shapes=[pltpu.VMEM((SEQ, 128), F32)] * 3,
        compiler_params=pltpu.CompilerParams(dimension_semantics=("parallel",), vmem_limit_bytes=MATMUL_VMEM_BYTES),
    )(_alibi_coef(), z, z, z, l, do, corr)


def _head_sum(x):
    i = lax.broadcasted_iota(jnp.int32, (128, 128), 0) // 64
    j = lax.broadcasted_iota(jnp.int32, (128, 128), 1) // 64
    return _dot_f32(x, (i == j).astype(F32), ones_on_right=True)


def _group_weights(l0, l1, l2):
    mx = jnp.maximum(jnp.maximum(l0, l1), l2)
    e0, e1, e2 = jnp.exp(l0 - mx), jnp.exp(l1 - mx), jnp.exp(l2 - mx)
    inv = 1.0 / (e0 + e1 + e2)
    return e0 * inv, e1 * inv, e2 * inv


def att_combine_fwd(o, l, name):
    def body(o0, o1, o2, l0, l1, l2, y_ref):
        w0, w1, w2 = _group_weights(l0[...], l1[...], l2[...])
        y_ref[...] = (o0[...] * w0 + o1[...] * w1 + o2[...] * w2).astype(y_ref.dtype)

    blk = pl.BlockSpec((ROW_TILE, ATT_WIDTH), lambda i: (i, 0))
    return pallas_call(
        body, name=name, grid=(SEQ // ROW_TILE,), in_specs=[blk] * 6, out_specs=blk,
        out_shape=jax.ShapeDtypeStruct((SEQ, ATT_WIDTH), MXU_DTYPE),
    )(*o, *l)


def att_combine_bwd(o, l, dy, name):
    def body(o0, o1, o2, l0, l1, l2, dy_ref, do0, do1, do2, cr0, cr1, cr2):
        w = _group_weights(l0[...], l1[...], l2[...])
        dyv = dy_ref[...]
        tot = _head_sum(dyv * (w[0] * o0[...] + w[1] * o1[...] + w[2] * o2[...]))
        for g, (do_ref, cr_ref) in enumerate(((do0, cr0), (do1, cr1), (do2, cr2))):
            do_ref[...] = dyv * w[g]
            cr_ref[...] = -w[g] * tot

    blk = pl.BlockSpec((ROW_TILE, 128), lambda i, j: (i, j))
    out = jax.ShapeDtypeStruct((SEQ, ATT_WIDTH), F32)
    res = pallas_call(
        body, name=name, grid=(SEQ // ROW_TILE, HEAD_PAIRS), in_specs=[blk] * 7, out_specs=[blk] * 6, out_shape=[out] * 6,
    )(*o, *l, dy)
    return res[:N_GROUPS], res[N_GROUPS:]


SUM_ROW_TILES = (1024, 512, 256, 128, 64, 32, 16)
SUM_TILE_BYTES = 24 * 1024 * 1024
SUM_PARAMS = pltpu.CompilerParams(vmem_limit_bytes=MATMUL_VMEM_BYTES)


def _row_tile(rows, cols, operands):
    fit = [t for t in SUM_ROW_TILES if rows % t == 0]
    return next((t for t in fit if 2 * 4 * operands * t * cols <= SUM_TILE_BYTES), fit[-1])


def _shard_shape(rows, cols, axis):
    return (rows // N_CHIPS, cols) if axis == 0 else (rows, cols // N_CHIPS)


def _half_shape(rows, cols, axis):
    return (rows, cols // 2) if axis == 0 else (rows // 2, cols)


def _piece_shape(rows, cols, axis):
    return (rows // N_CHIPS, cols // 2) if axis == 0 else (rows // 2, cols // N_CHIPS)


def place_own_block(shard, chip, rows, cols, axis, name):
    sr, sc = _shard_shape(rows, cols, axis)
    tr = _row_tile(sr, sc, 2)

    def body(chip_ref, s_ref, o_ref):
        o_ref[...] = s_ref[...].astype(o_ref.dtype)

    if axis == 0:
        out_map = lambda i, chip_ref: (chip_ref[0] * (sr // tr) + i, 0)
    else:
        out_map = lambda i, chip_ref: (i, chip_ref[0])
    return pallas_call(
        body, name=name, out_shape=jax.ShapeDtypeStruct((rows, cols), WEIGHT_COMM_DTYPE), compiler_params=SUM_PARAMS,
        grid_spec=pltpu.PrefetchScalarGridSpec(
            num_scalar_prefetch=1, grid=(sr // tr,), in_specs=[pl.BlockSpec((tr, sc), lambda i, chip_ref: (i, 0))],
            out_specs=pl.BlockSpec((tr, sc), out_map)),
    )(chip, shard)


def add_halves(g, theirs, core, rows, cols, axis, name):
    hr, hc = _half_shape(rows, cols, axis)
    tr = _row_tile(hr, hc, 3)

    def body(core_ref, g_ref, t_ref, o_ref):
        o_ref[...] = (g_ref[...].astype(F32) + t_ref[...].astype(F32)).astype(o_ref.dtype)

    if axis == 0:
        g_map = lambda i, core_ref: (i, core_ref[0])
    else:
        g_map = lambda i, core_ref: (core_ref[0] * (hr // tr) + i, 0)
    blk = pl.BlockSpec((tr, hc), lambda i, core_ref: (i, 0))
    return pallas_call(
        body, name=name, out_shape=jax.ShapeDtypeStruct((hr, hc), GRAD_COMM_DTYPE), compiler_params=SUM_PARAMS,
        grid_spec=pltpu.PrefetchScalarGridSpec(
            num_scalar_prefetch=1, grid=(hr // tr,), in_specs=[pl.BlockSpec((tr, hc), g_map), blk], out_specs=blk),
    )(core, g, theirs)


def add_pieces(half, got, chip, rows, cols, axis, name):
    hr, _ = _half_shape(rows, cols, axis)
    pr, pc = _piece_shape(rows, cols, axis)
    tr = _row_tile(pr, pc, 5)

    def body(chip_ref, h_ref, got_ref, o_ref):
        o_ref[...] = (h_ref[...].astype(F32) + got_ref[0].astype(F32) + got_ref[1].astype(F32) + got_ref[2].astype(F32))

    if axis == 0:
        h_map = lambda i, chip_ref: (chip_ref[0] * (pr // tr) + i, 0)
    else:
        h_map = lambda i, chip_ref: (i, chip_ref[0])
    return pallas_call(
        body, name=name, out_shape=jax.ShapeDtypeStruct((pr, pc), F32), compiler_params=SUM_PARAMS,
        grid_spec=pltpu.PrefetchScalarGridSpec(
            num_scalar_prefetch=1, grid=(pr // tr,),
            in_specs=[pl.BlockSpec((tr, pc), h_map), pl.BlockSpec((3, tr, pc), lambda i, chip_ref: (0, i, 0))],
            out_specs=pl.BlockSpec((tr, pc), lambda i, chip_ref: (i, 0))),
    )(chip, half, got)


def _adamw_math(w, g, m, v):
    nm = ADAM_B1 * m + (1.0 - ADAM_B1) * g
    nv = ADAM_B2 * v + (1.0 - ADAM_B2) * (g * g)
    m_hat = nm / (1.0 - ADAM_B1 ** ADAM_STEP)
    v_hat = nv / (1.0 - ADAM_B2 ** ADAM_STEP)
    return -ADAM_LR * (m_hat / (jnp.sqrt(v_hat) + ADAM_EPS) + ADAM_WD * w), nm, nv


def adamw_halves(w, mine, theirs, m, v, core, rows, cols, axis, name):
    sr, sc = _shard_shape(rows, cols, axis)
    pr, pc = _piece_shape(rows, cols, axis)
    tr = _row_tile(pr, pc, 9)
    nt = pr // tr

    def body(core_ref, w_ref, a_ref, b_ref, m_ref, v_ref, g_ref, d_ref, nm_ref, nv_ref):
        g = jnp.where(pl.program_id(0) == core_ref[0], a_ref[...], b_ref[...])
        g_ref[...] = g
        d_ref[...], nm_ref[...], nv_ref[...] = _adamw_math(w_ref[...], g, m_ref[...], v_ref[...])

    if axis == 0:
        full = pl.BlockSpec((tr, pc), lambda h, i, core_ref: (i, h))
    else:
        full = pl.BlockSpec((tr, pc), lambda h, i, core_ref: (h * nt + i, 0))
    part = pl.BlockSpec((tr, pc), lambda h, i, core_ref: (i, 0))
    out = jax.ShapeDtypeStruct((sr, sc), F32)
    return pallas_call(
        body, name=name, out_shape=[out, out, out, out], compiler_params=SUM_PARAMS,
        grid_spec=pltpu.PrefetchScalarGridSpec(
            num_scalar_prefetch=1, grid=(2, nt), in_specs=[full, part, part, full, full], out_specs=[full] * 4),
    )(core, w, mine, theirs, m, v)


BIG = (
    ("ffn1_w_gate_up", D_MODEL, 2 * D_FF, 1),
    ("ffn1_w_down", D_FF, D_MODEL, 0),
    ("w_in", D_MODEL, IN_COLS, 1),
    ("w_branch_hg", HG_WIDTH, D_MODEL, 1),
    ("w_branch_att", ATT_WIDTH, D_MODEL, 1),
    ("w_out", D_MODEL, D_MODEL, 0),
    ("ffn2_w_gate_up", D_MODEL, 2 * D_FF, 1),
    ("ffn2_w_down", D_FF, D_MODEL, 0),
)
N_BIG = len(BIG)
ANY = pl.BlockSpec(memory_space=pl.ANY)


def _place():
    return lax.axis_index("x"), lax.axis_index("y"), lax.axis_index("c")


def _other_chips(x, y):
    return ((1 - x, y), (x, 1 - y), (1 - x, 1 - y))


MAX_COPY_CHUNKS = 16
CHUNK_ROW_ALIGN = 16


def _row_chunks(view):
    rows = view.shape[0]
    n = next(n for n in range(MAX_COPY_CHUNKS, 0, -1) if rows % (CHUNK_ROW_ALIGN * n) == 0 or n == 1)
    step = rows // n
    return [pl.ds(i * step, step) for i in range(n)]


def _remote(src, dst, send_sem, recv_sem, device):
    return pltpu.make_async_remote_copy(src_ref=src, dst_ref=dst, send_sem=send_sem, recv_sem=recv_sem,
                                        device_id=device, device_id_type=MESH)


def _start_remote(src, dst, send_sem, recv_sem, device):
    for rows in _row_chunks(src):
        _remote(src.at[rows, :], dst.at[rows, :], send_sem, recv_sem, device).start()
    return _remote(src, dst, send_sem, recv_sem, device)


HBM = pl.BlockSpec(memory_space=pltpu.HBM)
SEM = pl.BlockSpec(memory_space=pltpu.SEMAPHORE)
SPLIT_COPY_EFFECT = pltpu.SideEffectType.DATAFLOW_SIDE_EFFECTING
GROUPS = {"ffn1": (0, 1), "mix": (2, 3, 4, 5), "ffn2": (6, 7)}


class _SemList:
    def __init__(self, refs):
        self.refs = refs
        self.at = self

    def __getitem__(self, index):
        w, k = index
        return self.refs[3 * w + k]


def _gather_piece(ref, rows, cols, axis, chip, c):
    sr, sc = _shard_shape(rows, cols, axis)
    j = 2 * chip[0] + chip[1]
    if axis == 0:
        return ref.at[pl.ds(j * sr + c * (sr // 2), sr // 2), :]
    return ref.at[pl.ds(c * (sr // 2), sr // 2), pl.ds(pl.multiple_of(j * sc, 128), sc)]


def _start_gather_sends(bufs, ws, send_sems, recv_sems):
    x, y, c = _place()
    for w, (_, r, cc, ax) in enumerate(ws):
        mine = _gather_piece(bufs[w], r, cc, ax, (x, y), c)
        for k, chip in enumerate(_other_chips(x, y)):
            _start_remote(mine, mine, send_sems.at[w, k], recv_sems.at[w, k], (*chip, c))


def _wait_gather_sends(bufs, ws, send_sems, recv_sems):
    x, y, c = _place()
    for w, (_, r, cc, ax) in enumerate(ws):
        for k, chip in enumerate(_other_chips(x, y)):
            got = _gather_piece(bufs[w], r, cc, ax, chip, c)
            _remote(got, got, send_sems.at[w, k], recv_sems.at[w, k], (x, y, c)).wait_recv()
    for w, (_, r, cc, ax) in enumerate(ws):
        mine = _gather_piece(bufs[w], r, cc, ax, (x, y), c)
        for k in range(3):
            _remote(mine, mine, send_sems.at[w, k], recv_sems.at[w, k], (x, y, c)).wait_send()


def _forward_halves(bufs, ws, send_sems, recv_sems):
    x, y, c = _place()
    passed = []
    for w, (_, r, cc, ax) in enumerate(ws):
        for k, chip in enumerate(_other_chips(x, y)):
            got = _gather_piece(bufs[w], r, cc, ax, chip, c)
            passed.append(_start_remote(got, got, send_sems.at[w, k], recv_sems.at[w, k], (x, y, 1 - c)))
    for w, (_, r, cc, ax) in enumerate(ws):
        for k, chip in enumerate(_other_chips(x, y)):
            got = _gather_piece(bufs[w], r, cc, ax, chip, 1 - c)
            _remote(got, got, send_sems.at[w, k], recv_sems.at[w, k], (x, y, c)).wait_recv()
    for cp in passed:
        cp.wait_send()


def gather_start(placed, after, group):
    ws = [BIG[i] for i in GROUPS[group]]
    n = len(ws)

    def body(*refs):
        bufs = refs[:n]
        send_sems, recv_sems = _SemList(refs[n + 1:4 * n + 1]), _SemList(refs[4 * n + 1:7 * n + 1])
        token = refs[-1]
        _start_gather_sends(bufs, ws, send_sems, recv_sems)
        token[...] = jnp.zeros_like(token)

    out = pallas_call(
        body, name=f"gather_start_{group}", in_specs=[HBM] * n + [ANY],
        out_specs=[SEM] * (6 * n) + [HBM] * n + [pl.BlockSpec(memory_space=pltpu.VMEM)],
        out_shape=[pltpu.SemaphoreType.DMA(())] * (6 * n)
        + [pltpu.HBM((r, cc), WEIGHT_COMM_DTYPE) for _, r, cc, _ in ws] + [jax.ShapeDtypeStruct((8, 128), F32)],
        input_output_aliases={w: 6 * n + w for w in range(n)},
        compiler_params=pltpu.CompilerParams(has_side_effects=SPLIT_COPY_EFFECT),
    )(*[_in_hbm(p) for p in placed], after)
    return out[:3 * n], out[3 * n:6 * n], out[6 * n:7 * n], out[-1]


def gather_wait(bufs, send_sems, recv_sems, after, group):
    ws = [BIG[i] for i in GROUPS[group]]
    n = len(ws)

    def body(*refs):
        _wait_gather_sends(refs[:n], ws, _SemList(refs[n:n + 3 * n]), _SemList(refs[n + 3 * n:n + 6 * n]))

    return pallas_call(
        body, name=f"gather_wait_{group}", in_specs=[HBM] * n + [SEM] * (6 * n) + [ANY] * len(after), out_specs=[HBM] * n,
        out_shape=[pltpu.HBM((r, cc), WEIGHT_COMM_DTYPE) for _, r, cc, _ in ws],
        input_output_aliases={w: w for w in range(n)},
        compiler_params=pltpu.CompilerParams(has_side_effects=SPLIT_COPY_EFFECT),
    )(*bufs, *send_sems, *recv_sems, *after)


def gather_forward(bufs, group):
    ws = [BIG[i] for i in GROUPS[group]]
    n = len(ws)

    def body(*refs):
        _forward_halves(refs[n:2 * n], ws, refs[2 * n], refs[2 * n + 1])

    return pallas_call(
        body, name=f"gather_forward_{group}", in_specs=[ANY] * n, out_specs=[ANY] * n,
        out_shape=[jax.ShapeDtypeStruct((r, cc), WEIGHT_COMM_DTYPE) for _, r, cc, _ in ws],
        input_output_aliases={w: w for w in range(n)},
        scratch_shapes=[pltpu.SemaphoreType.DMA((n, 3))] * 2,
    )(*bufs)


def _half(ref, rows, cols, axis, c):
    if axis == 0:
        return ref.at[:, pl.ds(pl.multiple_of(c * (cols // 2), 128), cols // 2)]
    return ref.at[pl.ds(c * (rows // 2), rows // 2), :]


def _piece_of_half(ref, rows, cols, axis, chip):
    j = 2 * chip[0] + chip[1]
    pr, pc = _piece_shape(rows, cols, axis)
    if axis == 0:
        return ref.at[pl.ds(j * pr, pr), :]
    return ref.at[:, pl.ds(pl.multiple_of(j * pc, 128), pc)]


def sibling_exchange_start(srcs, view, landing_shapes, dtype, name):
    n = len(srcs)

    def body(*refs):
        ins, land, sems = refs[:n], refs[n:2 * n], refs[2 * n:4 * n]
        x, y, c = _place()
        for w in range(n):
            _start_remote(view(ins[w], w, c), land[w], sems[w], sems[n + w], (x, y, 1 - c))
        refs[-1][...] = jnp.zeros_like(refs[-1])

    landing = [lax.empty(shape, dtype) for shape in landing_shapes]
    out = pallas_call(
        body, name=name, in_specs=[HBM] * (2 * n),
        out_specs=[SEM] * (2 * n) + [HBM] * (2 * n) + [pl.BlockSpec(memory_space=pltpu.VMEM)],
        out_shape=[pltpu.SemaphoreType.DMA(())] * (2 * n) + [pltpu.HBM(a.shape, a.dtype) for a in srcs]
        + [pltpu.HBM(shape, dtype) for shape in landing_shapes] + [jax.ShapeDtypeStruct((8, 128), F32)],
        input_output_aliases={i: 2 * n + i for i in range(2 * n)},
        compiler_params=pltpu.CompilerParams(has_side_effects=SPLIT_COPY_EFFECT),
    )(*[_in_hbm(a) for a in srcs], *[_in_hbm(b) for b in landing])
    return out[:n], out[n:2 * n], out[2 * n:3 * n], out[3 * n:4 * n], out[-1]


def sibling_exchange_wait(srcs, landing, send_sems, recv_sems, view, after, name):
    n = len(srcs)

    def body(*refs):
        ins, land, sems = refs[:n], refs[n:2 * n], refs[2 * n:4 * n]
        x, y, c = _place()
        for w in range(n):
            cp = _remote(view(ins[w], w, c), land[w], sems[w], sems[n + w], (x, y, c))
            cp.wait_send()
            cp.wait_recv()

    out = pallas_call(
        body, name=name, in_specs=[HBM] * (2 * n) + [SEM] * (2 * n) + [ANY] * len(after), out_specs=[HBM] * (2 * n),
        out_shape=[pltpu.HBM(a.shape, a.dtype) for a in srcs] + [pltpu.HBM(b.shape, b.dtype) for b in landing],
        input_output_aliases={i: i for i in range(2 * n)},
        compiler_params=pltpu.CompilerParams(has_side_effects=SPLIT_COPY_EFFECT),
    )(*srcs, *landing, *send_sems, *recv_sems, *after)
    return out[:n], out[n:]


def _scatter_copies(halves, got, ws, send_sems, recv_sems, start):
    x, y, c = _place()
    copies = []
    for w, (_, r, cc, ax) in enumerate(ws):
        for k, chip in enumerate(_other_chips(x, y)):
            args = (_piece_of_half(halves[w], r, cc, ax, chip), got[w].at[k], send_sems.at[w, k], recv_sems.at[w, k], (*chip, c))
            copies.append(_start_remote(*args) if start else _remote(*args))
    return copies


def scatter_start(halves, group):
    ws = [BIG[i] for i in GROUPS[group]]
    n = len(ws)

    def body(*refs):
        sems = refs[2 * n:8 * n]
        _scatter_copies(refs[:n], refs[n:2 * n], ws, _SemList(sems[:3 * n]), _SemList(sems[3 * n:]), start=True)
        refs[-1][...] = jnp.zeros_like(refs[-1])

    landing = [lax.empty((3,) + _piece_shape(r, cc, ax), GRAD_COMM_DTYPE) for _, r, cc, ax in ws]
    out = pallas_call(
        body, name=f"scatter_start_{group}", in_specs=[HBM] * (2 * n),
        out_specs=[SEM] * (6 * n) + [HBM] * (2 * n) + [pl.BlockSpec(memory_space=pltpu.VMEM)],
        out_shape=[pltpu.SemaphoreType.DMA(())] * (6 * n)
        + [pltpu.HBM(_half_shape(r, cc, ax), GRAD_COMM_DTYPE) for _, r, cc, ax in ws]
        + [pltpu.HBM((3,) + _piece_shape(r, cc, ax), GRAD_COMM_DTYPE) for _, r, cc, ax in ws]
        + [jax.ShapeDtypeStruct((8, 128), F32)],
        input_output_aliases={i: 6 * n + i for i in range(2 * n)},
        compiler_params=pltpu.CompilerParams(has_side_effects=SPLIT_COPY_EFFECT),
    )(*[_in_hbm(h) for h in halves], *[_in_hbm(b) for b in landing])
    return out[:3 * n], out[3 * n:6 * n], out[6 * n:7 * n], out[7 * n:8 * n], out[-1]


def scatter_wait(halves, got, send_sems, recv_sems, after, group):
    ws = [BIG[i] for i in GROUPS[group]]
    n = len(ws)

    def body(*refs):
        sems = refs[2 * n:8 * n]
        for cp in _scatter_copies(refs[:n], refs[n:2 * n], ws, _SemList(sems[:3 * n]), _SemList(sems[3 * n:]), start=False):
            cp.wait_send()
            cp.wait_recv()

    out = pallas_call(
        body, name=f"scatter_wait_{group}", in_specs=[HBM] * (2 * n) + [SEM] * (6 * n) + [ANY] * len(after),
        out_specs=[HBM] * (2 * n),
        out_shape=[pltpu.HBM(_half_shape(r, cc, ax), GRAD_COMM_DTYPE) for _, r, cc, ax in ws]
        + [pltpu.HBM((3,) + _piece_shape(r, cc, ax), GRAD_COMM_DTYPE) for _, r, cc, ax in ws],
        input_output_aliases={i: i for i in range(2 * n)},
        compiler_params=pltpu.CompilerParams(has_side_effects=SPLIT_COPY_EFFECT),
    )(*halves, *got, *send_sems, *recv_sems, *after)
    return out[:n], out[n:]


N_DEV = 8
SMALL = ("ffn1_norm", "mix_norm", "hg_lower_bounds", "hg_out_norm", "ffn2_norm", "final_norm")
SMALL_STAGE_ROWS = 8


def small_step(loss, grads, w, m, v, behind):
    n = len(SMALL)
    shapes = [g.shape for g in grads]
    first_row = [sum(s[0] for s in shapes[:i]) for i in range(n + 1)]
    assert first_row[n] < SMALL_STAGE_ROWS
    loss_row = (pl.ds(first_row[n], 1), pl.ds(0, loss.shape[1]))

    def body(*refs):
        loss_ref, g_refs, w_refs, m_refs, v_refs = refs[0], refs[1:1 + n], refs[1 + n:1 + 2 * n], refs[1 + 2 * n:1 + 3 * n], refs[1 + 3 * n:1 + 4 * n]
        outs = refs[2 + 4 * n:3 + 8 * n]
        loss_out, dg_refs, d_refs, nm_refs, nv_refs = outs[0], outs[1:1 + n], outs[1 + n:1 + 2 * n], outs[1 + 2 * n:1 + 3 * n], outs[1 + 3 * n:]
        stage, gathered, send_sems, recv_sems = refs[3 + 8 * n:]
        x, y, c = _place()
        me = 4 * x + 2 * y + c

        def slot(i, shape):
            return pl.ds(first_row[i], shape[0]), pl.ds(0, shape[1])

        stage[...] = jnp.zeros_like(stage)
        for i, g_ref in enumerate(g_refs):
            stage[slot(i, shapes[i])] = g_ref[...]
        stage[loss_row] = loss_ref[pl.ds(0, 1), :]
        gathered[me] = stage[...]
        copies = []
        for k in range(1, N_DEV):
            peer = (x ^ (k >> 2), y ^ ((k >> 1) & 1), c ^ (k & 1))
            cp = pltpu.make_async_remote_copy(
                src_ref=stage, dst_ref=gathered.at[me], send_sem=send_sems.at[k - 1], recv_sem=recv_sems.at[k - 1],
                device_id=peer, device_id_type=MESH)
            cp.start()
            copies.append(cp)
        for cp in copies:
            cp.wait()
        acc = gathered[0]
        for k in range(1, N_DEV):
            acc = acc + gathered[k]
        stage[...] = acc
        loss_out[...] = jnp.broadcast_to(stage[loss_row], loss_out.shape)
        for i in range(n):
            g = stage[slot(i, shapes[i])]
            dg_refs[i][...] = g
            d_refs[i][...], nm_refs[i][...], nv_refs[i][...] = _adamw_math(w_refs[i][...], g, m_refs[i][...], v_refs[i][...])

    vm = pl.BlockSpec(memory_space=pltpu.VMEM)
    per_param = [jax.ShapeDtypeStruct(s, F32) for s in shapes]
    out = pallas_call(
        body, name="small_step", in_specs=[vm] * (1 + 4 * n) + [ANY], out_specs=[vm] * (1 + 4 * n),
        out_shape=[jax.ShapeDtypeStruct(loss.shape, F32)] + per_param * 4,
        scratch_shapes=[pltpu.VMEM((SMALL_STAGE_ROWS, D_MODEL), F32),
                        pltpu.VMEM((N_DEV, SMALL_STAGE_ROWS, D_MODEL), F32),
                        pltpu.SemaphoreType.DMA((N_DEV - 1,)), pltpu.SemaphoreType.DMA((N_DEV - 1,))],
    )(loss, *grads, *w, *m, *v, behind)
    return out[0], out[1:1 + n], out[1 + n:1 + 2 * n], out[1 + 2 * n:1 + 3 * n], out[1 + 3 * n:]


def _swiglu_block_fwd(h, norm_g, w_gu, w_down, tag, behind=()):
    n = rmsnorm_fwd(h, norm_g, f"{tag}_norm", behind=behind)
    a, b, s = gate_up_swiglu(n, w_gu, f"{tag}_gate_up")
    h_out = matmul(s, w_down, res=h, scale=0.5, name=f"{tag}_down")
    return h_out, (n, a, b, s)


def _swiglu_block_bwd(h, norm_g, w_gu, w_down, saved, dh_out, df, tag, exchange, behind=()):
    n, a, b, s = saved
    d_down = matmul(s, df, ta=True, scale=0.5, out_dtype=GRAD_COMM_DTYPE, name=f"{tag}_d_w_down")
    ds = matmul(df, w_down, tb=True, scale=0.5, out_dtype=ACT_DTYPE, behind=behind, name=f"{tag}_d_s")
    dgu = swiglu_bwd(a, b, ds, f"{tag}_swiglu_bwd")
    d_gu = matmul(n, dgu, ta=True, out_dtype=GRAD_COMM_DTYPE, name=f"{tag}_d_w_gate_up")
    tokens = exchange.gradients_ready(tag, {f"{tag}_w_gate_up": d_gu, f"{tag}_w_down": d_down})
    dn = matmul(dgu, w_gu, tb=True, behind=tokens, name=f"{tag}_d_n")
    dh, dh_m, dg = rmsnorm_bwd(h, norm_g, dn, dh_out, f"{tag}_norm_bwd")
    return dh, dh_m, dg


def local_step(x, target, small, exchange):
    big = {}
    token, big_ffn1 = exchange.weights("ffn1", x)
    big.update(big_ffn1)
    h1, saved1 = _swiglu_block_fwd(x, small["ffn1_norm"], big["ffn1_w_gate_up"], big["ffn1_w_down"], "ffn1", token)
    token, big_mix = exchange.weights("mix", h1)
    big.update(big_mix)
    u = rmsnorm_fwd(h1, small["mix_norm"], "mix_norm", behind=token)
    z = matmul(u, big["w_in"], name="w_in")
    p = small["hg_lower_bounds"]
    lb = 1.0 / (1.0 + jnp.exp(p[1:2] - p[0:1]))
    y_hg, o_raw, states = hgrn_fwd(z, lb, small["hg_out_norm"], "hgrn_fwd")
    o_att, l_att = zip(*[att_fwd(z, g, f"att_fwd_{g}") for g in range(N_GROUPS)])
    y_att = att_combine_fwd(o_att, l_att, "att_combine")
    bh = matmul(y_hg, big["w_branch_hg"], name="branch_hg")
    ba = matmul(y_att, big["w_branch_att"], name="branch_att")
    merged = merge_fwd(z, bh, ba, "merge")
    h2 = matmul(merged, big["w_out"], res=h1, name="w_out")
    token, big_ffn2 = exchange.weights("ffn2", h2)
    big.update(big_ffn2)
    h3, saved2 = _swiglu_block_fwd(h2, small["ffn2_norm"], big["ffn2_w_gate_up"], big["ffn2_w_down"], "ffn2", token)
    dh3, dh3_m, d_final, loss = final_norm_loss(h3, small["final_norm"], target, "final_norm_loss")

    gs, gb = {"final_norm": d_final}, {}
    dh2, dh2_m, gs["ffn2_norm"] = _swiglu_block_bwd(
        h2, small["ffn2_norm"], big["ffn2_w_gate_up"], big["ffn2_w_down"], saved2, dh3, dh3_m, "ffn2", exchange)
    token = exchange.backward_done("ffn2", dh2)
    gb["w_out"] = matmul(merged, dh2_m, ta=True, out_dtype=GRAD_COMM_DTYPE, name="d_w_out")
    dmerged = matmul(dh2_m, big["w_out"], tb=True, behind=token, name="d_merged")
    dbh, dba, dgh, dga = merge_bwd(z, bh, ba, dmerged, "merge_bwd")
    gb["w_branch_hg"] = matmul(y_hg, dbh, ta=True, out_dtype=GRAD_COMM_DTYPE, name="d_w_branch_hg")
    gb["w_branch_att"] = matmul(y_att, dba, ta=True, out_dtype=GRAD_COMM_DTYPE, name="d_w_branch_att")
    dy_hg = matmul(dbh, big["w_branch_hg"], tb=True, name="d_y_hg")
    dy_att = matmul(dba, big["w_branch_att"], tb=True, name="d_y_att")
    dq, dfp, di, dog, d_lb, gs["hg_out_norm"] = hgrn_bwd(z, lb, small["hg_out_norm"], o_raw, states, dy_hg, "hgrn_bwd")
    do_att, corr = att_combine_bwd(o_att, l_att, dy_att, "att_combine_bwd")
    d_att = [part for g in range(N_GROUPS) for part in att_bwd(z, l_att[g], do_att[g], corr[g], g, f"att_bwd_{g}")]
    dz = jnp.concatenate([dq, dfp, di, dog, *d_att, dgh, dga], axis=1)
    gb["w_in"] = matmul(u, dz, ta=True, out_dtype=GRAD_COMM_DTYPE, name="d_w_in")
    token = exchange.gradients_ready("mix", gb)
    du = matmul(dz, big["w_in"], tb=True, behind=token, name="d_u")
    dh1, dh1_m, gs["mix_norm"] = rmsnorm_bwd(h1, small["mix_norm"], du, dh2, "mix_norm_bwd")
    token = exchange.backward_done("mix", dh1)
    dp0 = d_lb * lb * (1.0 - lb)
    gs["hg_lower_bounds"] = jnp.concatenate([dp0, -dp0], axis=0)
    dx, _, gs["ffn1_norm"] = _swiglu_block_bwd(
        x, small["ffn1_norm"], big["ffn1_w_gate_up"], big["ffn1_w_down"], saved1, dh1, dh1_m, "ffn1", exchange, token)
    exchange.backward_done("ffn1", dx)
    return loss, dx, gs


WEIGHTS = ("ffn1_norm", "ffn1_w_gate_up", "ffn1_w_down", "mix_norm", "w_in", "hg_lower_bounds", "hg_out_norm",
           "w_branch_hg", "w_branch_att", "w_out", "ffn2_norm", "ffn2_w_gate_up", "ffn2_w_down", "final_norm")


class WeightExchange:
    ORDER = ("ffn1", "mix", "ffn2")

    def __init__(self, shards, core, chip):
        self.core, self.chip = core, chip
        self.halving = None
        self.scattering = None
        self.reducing = {}
        first = self.ORDER[0]
        self.placed = {BIG[i][0]: place_own_block(shards[BIG[i][0]], chip, *BIG[i][1:], f"place_{BIG[i][0]}")
                       for i in GROUPS[first]}
        self._start_gather(first, self.placed[self._names(first)[0]])
        chip_behind = chip + self.token[0, :1].astype(jnp.int32)
        for group in self.ORDER[1:]:
            for i in GROUPS[group]:
                n, r, cc, ax = BIG[i]
                self.placed[n] = place_own_block(shards[n], chip_behind, r, cc, ax, f"place_{n}")
        self.placed_behind = [self.placed[n] for group in self.ORDER[1:] for n in self._names(group)]

    def _names(self, group):
        return [BIG[i][0] for i in GROUPS[group]]

    def _start_gather(self, group, after):
        send_sems, recv_sems, bufs, self.token = gather_start([self.placed[n] for n in self._names(group)], after, group)
        self.gathering = (group, send_sems, recv_sems, bufs)

    def weights(self, group, h):
        pending, send_sems, recv_sems, bufs = self.gathering
        assert pending == group
        after = self.placed_behind if group == self.ORDER[0] else [h]
        whole = gather_forward(gather_wait(bufs, send_sems, recv_sems, after, group), group)
        later = self.ORDER.index(group) + 1
        behind = []
        if later < len(self.ORDER):
            self._start_gather(self.ORDER[later], whole[0])
            behind = [self.token]
        return behind, dict(zip(self._names(group), whole))

    @staticmethod
    def _half_to_sibling(ws):
        return lambda ref, w, c: _half(ref, *ws[w][1:], 1 - c)

    def gradients_ready(self, group, grads):
        ws = [BIG[i] for i in GROUPS[group]]
        send_sems, recv_sems, own, theirs, token = sibling_exchange_start(
            [grads[n] for n, *_ in ws], self._half_to_sibling(ws), [_half_shape(r, cc, ax) for _, r, cc, ax in ws],
            GRAD_COMM_DTYPE, f"halves_start_{group}")
        self.halving = (group, send_sems, recv_sems, own, theirs)
        return [token]

    def backward_done(self, group, dh):
        pending, send_sems, recv_sems, own, theirs = self.halving
        assert pending == group
        ws = [BIG[i] for i in GROUPS[group]]
        own, theirs = sibling_exchange_wait(own, theirs, send_sems, recv_sems, self._half_to_sibling(ws), [dh],
                                            f"halves_wait_{group}")
        halves = [add_halves(g, t, self.core, r, cc, ax, f"add_halves_{n}") for (n, r, cc, ax), g, t in zip(ws, own, theirs)]
        previous = self.scattering
        send_sems, recv_sems, halves, got, self.token = scatter_start(halves, group)
        self.scattering = (group, send_sems, recv_sems, halves, got)
        behind = [self._finish_scatter(previous, [self.token])] if previous is not None else []
        return behind + [self.token]

    def _finish_scatter(self, scattering, after):
        group, send_sems, recv_sems, halves, got = scattering
        halves, got = scatter_wait(halves, got, send_sems, recv_sems, after, group)
        ws = [BIG[i] for i in GROUPS[group]]
        mine = [add_pieces(h, g, self.chip, r, cc, ax, f"add_pieces_{n}") for (n, r, cc, ax), h, g in zip(ws, halves, got)]
        send_sems, recv_sems, mine, theirs, token = sibling_exchange_start(
            mine, lambda ref, w, c: ref, [_piece_shape(r, cc, ax) for _, r, cc, ax in ws], F32, f"reduced_start_{group}")
        self.reducing[group] = (send_sems, recv_sems, mine, theirs)
        return token

    def finish(self, after):
        return self._finish_scatter(self.scattering, after)

    def reduced_halves(self, group, after):
        send_sems, recv_sems, mine, theirs = self.reducing.pop(group)
        mine, theirs = sibling_exchange_wait(mine, theirs, send_sems, recv_sems, lambda ref, w, c: ref, after,
                                             f"reduced_wait_{group}")
        return {BIG[i][0]: (a, b) for i, a, b in zip(GROUPS[group], mine, theirs)}


def kernel(x, ffn1_norm, ffn1_w_gate_up, ffn1_w_down, mix_norm, w_in, hg_lower_bounds, hg_out_norm, w_branch_hg, w_branch_att, w_out, ffn2_norm, ffn2_w_gate_up, ffn2_w_down, final_norm, loss_target, m_ffn1_norm, m_ffn1_w_gate_up, m_ffn1_w_down, m_mix_norm, m_w_in, m_hg_lower_bounds, m_hg_out_norm, m_w_branch_hg, m_w_branch_att, m_w_out, m_ffn2_norm, m_ffn2_w_gate_up, m_ffn2_w_down, m_final_norm, v_ffn1_norm, v_ffn1_w_gate_up, v_ffn1_w_down, v_mix_norm, v_w_in, v_hg_lower_bounds, v_hg_out_norm, v_w_branch_hg, v_w_branch_att, v_w_out, v_ffn2_norm, v_ffn2_w_gate_up, v_ffn2_w_down, v_final_norm):
    w = dict(ffn1_norm=ffn1_norm, ffn1_w_gate_up=ffn1_w_gate_up, ffn1_w_down=ffn1_w_down, mix_norm=mix_norm, w_in=w_in,
             hg_lower_bounds=hg_lower_bounds, hg_out_norm=hg_out_norm, w_branch_hg=w_branch_hg, w_branch_att=w_branch_att,
             w_out=w_out, ffn2_norm=ffn2_norm, ffn2_w_gate_up=ffn2_w_gate_up, ffn2_w_down=ffn2_w_down, final_norm=final_norm)
    m = dict(ffn1_norm=m_ffn1_norm, ffn1_w_gate_up=m_ffn1_w_gate_up, ffn1_w_down=m_ffn1_w_down, mix_norm=m_mix_norm,
             w_in=m_w_in, hg_lower_bounds=m_hg_lower_bounds, hg_out_norm=m_hg_out_norm, w_branch_hg=m_w_branch_hg,
             w_branch_att=m_w_branch_att, w_out=m_w_out, ffn2_norm=m_ffn2_norm, ffn2_w_gate_up=m_ffn2_w_gate_up,
             ffn2_w_down=m_ffn2_w_down, final_norm=m_final_norm)
    v = dict(ffn1_norm=v_ffn1_norm, ffn1_w_gate_up=v_ffn1_w_gate_up, ffn1_w_down=v_ffn1_w_down, mix_norm=v_mix_norm,
             w_in=v_w_in, hg_lower_bounds=v_hg_lower_bounds, hg_out_norm=v_hg_out_norm, w_branch_hg=v_w_branch_hg,
             w_branch_att=v_w_branch_att, w_out=v_w_out, ffn2_norm=v_ffn2_norm, ffn2_w_gate_up=v_ffn2_w_gate_up,
             ffn2_w_down=v_ffn2_w_down, final_norm=v_final_norm)

    core = lax.axis_index("c").astype(jnp.int32).reshape(1)
    chip = (2 * lax.axis_index("x") + lax.axis_index("y")).astype(jnp.int32).reshape(1)
    exchange = WeightExchange({n: w[n][0] for n, *_ in BIG}, core, chip)
    small = {n: w[n] for n in SMALL}
    small["final_norm"] = final_norm.reshape(1, D_MODEL)

    loss, dx, gs = local_step(x[0], loss_target[0], small, exchange)

    grads, delta, new_m, new_v = {}, {}, {}, {}

    def update(group, core, after):
        reduced = exchange.reduced_halves(group, after)
        for i in GROUPS[group]:
            n, r, cc, ax = BIG[i]
            a, b = reduced[n]
            g, d, nm, nv = adamw_halves(w[n][0], a, b, m[n][0], v[n][0], core, r, cc, ax, f"adamw_{n}")
            grads[n], delta[n], new_m[n], new_v[n] = g[None], d[None], nm[None], nv[None]

    core_behind = core + exchange.token[0, :1].astype(jnp.int32)
    update("ffn2", core_behind, [exchange.token])
    update("mix", core_behind, [delta["ffn2_w_down"]])
    token = exchange.finish(after=[delta[BIG[i][0]] for group in ("ffn2", "mix") for i in GROUPS[group]])
    two_d = lambda a: a.reshape(1, D_MODEL) if a.ndim == 1 else a
    loss_sum, *small_out = small_step(loss, [gs[n] for n in SMALL], *[[two_d(p[n]) for n in SMALL] for p in (w, m, v)],
                                      behind=token)
    for result, parts in zip((grads, delta, new_m, new_v), small_out):
        result.update({n: a.reshape(w[n].shape) for n, a in zip(SMALL, parts)})
    update("ffn1", core, [loss_sum])

    return (loss_sum[0, 0], dx[None], *[grads[n] for n in WEIGHTS], *[delta[n] for n in WEIGHTS],
            *[new_m[n] for n in WEIGHTS], *[new_v[n] for n in WEIGHTS])
```

```python
import numpy as np
import jax
import jax.numpy as jnp
from jax import lax
from jax.experimental import pallas as pl
from jax.experimental.pallas import tpu as pltpu

SEQ = 2048
D_MODEL = 1024
D_FF = 2816
HG_HEADS = 4
HG_DIM = 128
HG_WIDTH = 512
HG_CHUNK = 64
ATT_GROUPS = ((128, 1), (512, 4), (2048, 16))
ATT_HEADS = 8
ATT_WIDTH = 512
ATT_BLOCK = 128
ALIBI_MAX = 8.0
IN_COLS = 8704
EPS = 1e-6
NEG_INF = -1e30
ADAM_LR = 0.001
ADAM_B1 = 0.9
ADAM_B2 = 0.999
ADAM_EPS = 1e-08
ADAM_WD = 0.01
ADAM_STEP = 10

N_CHIPS = 4
MXU_DTYPE = jnp.bfloat16
WEIGHT_COMM_DTYPE = jnp.bfloat16
GRAD_COMM_DTYPE = jnp.bfloat16
ACT_DTYPE = jnp.bfloat16
MESH = pl.DeviceIdType.MESH
F32 = jnp.float32


def _sigmoid(x):
    return 1.0 / (1.0 + jnp.exp(-x))


def _dot(a, b, ta=False, tb=False):
    dn = (((0 if ta else 1,), (1 if tb else 0,)), ((), ()))
    return lax.dot_general(a.astype(MXU_DTYPE), b.astype(MXU_DTYPE), dn, preferred_element_type=F32)


def _dot_f32(a, b, ones_on_right=False):
    x = a if ones_on_right else b
    hi = x.astype(jnp.bfloat16)
    rest = x - hi.astype(F32)
    mid = rest.astype(jnp.bfloat16)
    lo = (rest - mid.astype(F32)).astype(jnp.bfloat16)
    if ones_on_right:
        dot = lambda q: jnp.dot(q, b.astype(jnp.bfloat16), preferred_element_type=F32)
    else:
        dot = lambda q: jnp.dot(a.astype(jnp.bfloat16), q, preferred_element_type=F32)
    return dot(hi) + (dot(mid) + dot(lo))


def _split_bf16(x):
    hi = x.astype(jnp.bfloat16)
    return hi, (x - hi.astype(F32)).astype(jnp.bfloat16)


def _hdot(a, b, ta=False, tb=False):
    dn =(((0 if ta else 1,), (1 if tb else 0,)), ((), ()))
    (a_hi, a_lo), (b_hi, b_lo) = _split_bf16(a), _split_bf16(b)
    dot = lambda p, q: lax.dot_general(p, q, dn, preferred_element_type=F32)
    return dot(a_hi, b_hi) + (dot(a_lo, b_hi) + dot(a_hi, b_lo))


def _in_hbm(a):
    return pltpu.with_memory_space_constraint(a, pltpu.HBM)


def pallas_call(body, **kw):
    grid_spec = kw.get("grid_spec")
    specs = list(kw["in_specs"] if grid_spec is None else grid_spec.in_specs)
    n_prefetch = 0 if grid_spec is None else grid_spec.num_scalar_prefetch
    out_specs = kw["out_specs"] if grid_spec is None else grid_spec.out_specs
    one = not isinstance(kw["out_shape"], (list, tuple))
    shapes = [kw["out_shape"]] if one else list(kw["out_shape"])
    out_specs = [out_specs] if one else list(out_specs)
    shapes = [pltpu.HBM(a.shape, a.dtype) if s.memory_space is None and isinstance(a, jax.ShapeDtypeStruct) else a
              for a, s in zip(shapes, out_specs)]
    kw["out_shape"] = shapes[0] if one else shapes
    call = pl.pallas_call(body, **kw)

    def run(*args):
        assert len(args) == n_prefetch + len(specs)
        pinned = [_in_hbm(a) if s.memory_space is None else a for a, s in zip(args[n_prefetch:], specs)]
        return call(*args[:n_prefetch], *pinned)

    return run


MATMUL_VMEM_BYTES = 48 * 1024 * 1024
MATMUL_TILE_BYTES = 36 * 1024 * 1024
MXU_ALIGN = 128


def _divisors(n, most):
    return [t for t in range(min(n, most), 0, -MXU_ALIGN) if n % t == 0 and t % MXU_ALIGN == 0]


def _matmul_tiles(M, N, K, in_bytes, out_bytes, has_res):
    best = None
    for tk in _divisors(K, K):
        nk = K // tk
        for tm in _divisors(M, 2048):
            for tn in _divisors(N, 512):
                tiles = 2 * in_bytes * (tm * tk + tk * tn) + 2 * out_bytes * tm * tn
                tiles += 4 * tm * tn * ((nk > 1) + 2 * has_res)
                if tiles > MATMUL_TILE_BYTES:
                    continue
                traffic = in_bytes * (M * K * (1 if nk == 1 else N // tn) + K * N * (M // tm))
                key = (traffic, -tm * tn * tk)
                if best is None or key < best[0]:
                    best = (key, (tm, tn, tk))
    return best[1]


def matmul(a, b, *, ta=False, tb=False, out_dtype=F32, res=None, scale=1.0, behind=(), name):
    if ta:
        K, M = a.shape
    else:
        M, K = a.shape
    if tb:
        N, K2 = b.shape
    else:
        K2, N = b.shape
    assert K == K2 and a.dtype == b.dtype
    tm, tn, tk = _matmul_tiles(M, N, K, a.dtype.itemsize, jnp.dtype(out_dtype).itemsize, res is not None)
    nk = K // tk

    def finish(r, r_ref, o_ref):
        if scale != 1.0:
            r = r * scale
        if res is not None:
            r = r_ref[...] + r
        o_ref[...] = r.astype(out_dtype)

    def body(*refs):
        a_ref, b_ref = refs[:2]
        r_ref = refs[2] if res is not None else None
        o_ref = refs[2 + (res is not None) + len(behind)]
        if nk == 1:
            finish(_dot(a_ref[...], b_ref[...], ta, tb), r_ref, o_ref)
            return
        acc = refs[-1]
        k = pl.program_id(2)

        @pl.when(k == 0)
        def _():
            acc[...] = jnp.zeros_like(acc)

        acc[...] += _dot(a_ref[...], b_ref[...], ta, tb)

        @pl.when(k == nk - 1)
        def _():
            finish(acc[...], r_ref, o_ref)

    a_spec = pl.BlockSpec((tk, tm), lambda i, j, k: (k, i)) if ta else pl.BlockSpec((tm, tk), lambda i, j, k: (i, k))
    b_spec = pl.BlockSpec((tn, tk), lambda i, j, k: (j, k)) if tb else pl.BlockSpec((tk, tn), lambda i, j, k: (k, j))
    in_specs = [a_spec, b_spec]
    args = [a, b]
    if res is not None:
        in_specs.append(pl.BlockSpec((tm, tn), lambda i, j, k: (i, j)))
        args.append(res)
    for earlier in behind:
        in_specs.append(pl.BlockSpec(memory_space=pl.ANY))
        args.append(earlier)
    return pallas_call(
        body, name=name, grid=(M // tm, N // tn, nk), in_specs=in_specs,
        out_specs=pl.BlockSpec((tm, tn), lambda i, j, k: (i, j)),
        out_shape=jax.ShapeDtypeStruct((M, N), out_dtype),
        scratch_shapes=[pltpu.VMEM((tm, tn), F32)] if nk > 1 else [],
        compiler_params=pltpu.CompilerParams(dimension_semantics=("parallel", "parallel", "arbitrary"),
                                             vmem_limit_bytes=MATMUL_VMEM_BYTES),
    )(*args)


ROW_TILE = 256


def rmsnorm_fwd(x, g, name, behind=()):
    def body(x_ref, g_ref, *refs):
        n_ref = refs[-1]
        xv = x_ref[...]
        r = lax.rsqrt(jnp.mean(xv * xv, axis=-1, keepdims=True) + EPS)
        n_ref[...] = ((xv * r) * g_ref[...]).astype(n_ref.dtype)

    order = list(behind)
    return pallas_call(
        body, name=name, grid=(SEQ // ROW_TILE,),
        in_specs=[pl.BlockSpec((ROW_TILE, D_MODEL), lambda i: (i, 0)), pl.BlockSpec((1, D_MODEL), lambda i: (0, 0))]
        + [pl.BlockSpec(memory_space=pl.ANY)] * len(order),
        out_specs=pl.BlockSpec((ROW_TILE, D_MODEL), lambda i: (i, 0)),
        out_shape=jax.ShapeDtypeStruct((SEQ, D_MODEL), MXU_DTYPE),
    )(x, g, *order)


def rmsnorm_bwd(x, g, dn, dres, name):
    def body(x_ref, g_ref, dn_ref, dr_ref, dx_ref, dxm_ref, dg_ref):
        xv = x_ref[...]
        r = lax.rsqrt(jnp.mean(xv * xv, axis=-1, keepdims=True) + EPS)
        xh = xv * r
        dnv = dn_ref[...]

        @pl.when(pl.program_id(0) == 0)
        def _():
            dg_ref[...] = jnp.zeros_like(dg_ref)

        dg_ref[...] += jnp.sum(dnv * xh, axis=0, keepdims=True)
        dxh = dnv * g_ref[...]
        dx = dr_ref[...] + r * (dxh - xh * jnp.mean(dxh * xh, axis=-1, keepdims=True))
        dx_ref[...] = dx
        dxm_ref[...] = dx.astype(dxm_ref.dtype)

    row = pl.BlockSpec((ROW_TILE, D_MODEL), lambda i: (i, 0))
    vec = pl.BlockSpec((1, D_MODEL), lambda i: (0, 0))
    return pallas_call(
        body, name=name, grid=(SEQ // ROW_TILE,), in_specs=[row, vec, row, row], out_specs=[row, row, vec],
        out_shape=[jax.ShapeDtypeStruct((SEQ, D_MODEL), F32), jax.ShapeDtypeStruct((SEQ, D_MODEL), MXU_DTYPE),
                   jax.ShapeDtypeStruct((1, D_MODEL), F32)],
        compiler_params=pltpu.CompilerParams(dimension_semantics=("arbitrary",)),
    )(x, g, dn, dres)


def final_norm_loss(h, g, target, name):
    def body(h_ref, g_ref, t_ref, dh_ref, dhm_ref, dg_ref, loss_ref):
        xv = h_ref[...]
        r = lax.rsqrt(jnp.mean(xv * xv, axis=-1, keepdims=True) + EPS)
        xh = xv * r
        gv = g_ref[...]
        e = xh * gv - t_ref[...]

        @pl.when(pl.program_id(0) == 0)
        def _():
            dg_ref[...] = jnp.zeros_like(dg_ref)
            loss_ref[...] = jnp.zeros_like(loss_ref)

        part = 0.5 * jnp.sum(jnp.sum(e * e, axis=-1, keepdims=True) * (1.0 / D_MODEL), axis=0, keepdims=True)
        loss_ref[...] += jnp.broadcast_to(part, loss_ref.shape)
        dout = e * (1.0 / D_MODEL)
        dg_ref[...] += jnp.sum(dout * xh, axis=0, keepdims=True)
        dxh = dout * gv
        dh = r * (dxh - xh * jnp.mean(dxh * xh, axis=-1, keepdims=True))
        dh_ref[...] = dh
        dhm_ref[...] = dh.astype(dhm_ref.dtype)

    row = pl.BlockSpec((ROW_TILE, D_MODEL), lambda i: (i, 0))
    vec = pl.BlockSpec((1, D_MODEL), lambda i: (0, 0))
    return pallas_call(
        body, name=name, grid=(SEQ // ROW_TILE,), in_specs=[row, vec, row],
        out_specs=[row, row, vec, pl.BlockSpec((8, 128), lambda i: (0, 0))],
        out_shape=[jax.ShapeDtypeStruct((SEQ, D_MODEL), F32), jax.ShapeDtypeStruct((SEQ, D_MODEL), MXU_DTYPE),
                   jax.ShapeDtypeStruct((1, D_MODEL), F32), jax.ShapeDtypeStruct((8, 128), F32)],
        compiler_params=pltpu.CompilerParams(dimension_semantics=("arbitrary",)),
    )(h, g, target)


FFN_TILE = 256
FFN_TILES = D_FF // FFN_TILE


def gate_up_swiglu(n, w_gu, name, behind=()):
    def body(n_ref, wa_ref, wb_ref, *refs):
        a_ref, b_ref, s_ref = refs[len(behind):]
        nv = n_ref[...]
        a = _dot(nv, wa_ref[...])
        b = _dot(nv, wb_ref[...])
        a_ref[...] = a.astype(a_ref.dtype)
        b_ref[...] = b.astype(b_ref.dtype)
        s_ref[...] = (a * _sigmoid(a) * b).astype(s_ref.dtype)

    tile = pl.BlockSpec((SEQ, FFN_TILE), lambda j: (0, j))
    act = jax.ShapeDtypeStruct((SEQ, D_FF), ACT_DTYPE)
    return pallas_call(
        body, name=name, grid=(FFN_TILES,),
        in_specs=[pl.BlockSpec((SEQ, D_MODEL), lambda j: (0, 0)), pl.BlockSpec((D_MODEL, FFN_TILE), lambda j: (0, j)),
                  pl.BlockSpec((D_MODEL, FFN_TILE), lambda j: (0, j + FFN_TILES))]
        + [pl.BlockSpec(memory_space=pl.ANY)] * len(behind),
        out_specs=[tile, tile, tile], out_shape=[act, act, jax.ShapeDtypeStruct((SEQ, D_FF), MXU_DTYPE)],
        compiler_params=pltpu.CompilerParams(dimension_semantics=("parallel",), vmem_limit_bytes=MATMUL_VMEM_BYTES),
    )(n, w_gu, w_gu, *behind)


def swiglu_bwd(a, b, ds, name):
    rows = ROW_TILE // 2

    def body(a_ref, b_ref, ds_ref, o_ref):
        av = a_ref[...].astype(F32)
        sg = _sigmoid(av)
        dsv = ds_ref[...].astype(F32)
        o_ref[:, :D_FF] = (dsv * b_ref[...].astype(F32) * (sg * (1.0 + av * (1.0 - sg)))).astype(o_ref.dtype)
        o_ref[:, D_FF:] = (dsv * av * sg).astype(o_ref.dtype)

    blk = pl.BlockSpec((rows, D_FF), lambda i: (i, 0))
    return pallas_call(
        body, name=name, grid=(SEQ // rows,), in_specs=[blk, blk, blk],
        out_specs=pl.BlockSpec((rows, 2 * D_FF), lambda i: (i, 0)),
        out_shape=jax.ShapeDtypeStruct((SEQ, 2 * D_FF), MXU_DTYPE), compiler_params=SUM_PARAMS,
    )(a, b, ds)


GATE_HG_BLK = 6656 // 512
GATE_ATT_BLK = 7680 // 512


def merge_fwd(z, bh, ba, name):
    def body(gh_ref, ga_ref, bh_ref, ba_ref, o_ref):
        o_ref[...] = (_sigmoid(gh_ref[...]) * bh_ref[...] + _sigmoid(ga_ref[...]) * ba_ref[...]).astype(o_ref.dtype)

    blk = pl.BlockSpec((ROW_TILE, 512), lambda i, j: (i, j))
    return pallas_call(
        body, name=name, grid=(SEQ // ROW_TILE, 2),
        in_specs=[pl.BlockSpec((ROW_TILE, 512), lambda i, j: (i, GATE_HG_BLK + j)),
                  pl.BlockSpec((ROW_TILE, 512), lambda i, j: (i, GATE_ATT_BLK + j)), blk, blk],
        out_specs=blk, out_shape=jax.ShapeDtypeStruct((SEQ, D_MODEL), MXU_DTYPE),
    )(z, z, bh, ba)


def merge_bwd(z, bh, ba, dm, name):
    def body(gh_ref, ga_ref, bh_ref, ba_ref, dm_ref, dbh_ref, dba_ref, dgh_ref, dga_ref):
        dmv = dm_ref[...]
        sh = _sigmoid(gh_ref[...])
        sa = _sigmoid(ga_ref[...])
        dbh_ref[...] = (dmv * sh).astype(dbh_ref.dtype)
        dba_ref[...] = (dmv * sa).astype(dba_ref.dtype)
        dgh_ref[...] = (dmv * bh_ref[...] * (sh * (1.0 - sh))).astype(dgh_ref.dtype)
        dga_ref[...] = (dmv * ba_ref[...] * (sa * (1.0 - sa))).astype(dga_ref.dtype)

    blk = pl.BlockSpec((ROW_TILE, 512), lambda i, j: (i, j))
    out = jax.ShapeDtypeStruct((SEQ, D_MODEL), MXU_DTYPE)
    return pallas_call(
        body, name=name, grid=(SEQ // ROW_TILE, 2),
        in_specs=[pl.BlockSpec((ROW_TILE, 512), lambda i, j: (i, GATE_HG_BLK + j)),
                  pl.BlockSpec((ROW_TILE, 512), lambda i, j: (i, GATE_ATT_BLK + j)), blk, blk, blk],
        out_specs=[blk, blk, blk, blk], out_shape=[out, out, out, out],
    )(z, z, bh, ba, dm)


N_CHUNKS = SEQ // HG_CHUNK
HG_STEP_CHUNKS = 4


def _hgrn_gates(q, fp, lb):
    C = HG_CHUNK
    sg = _sigmoid(fp)
    f = lb + (1.0 - lb) * sg
    lf = jnp.log(f)
    row = lax.broadcasted_iota(jnp.int32, (C, C), 0)
    col = lax.broadcasted_iota(jnp.int32, (C, C), 1)
    causal = row >= col
    G = _dot_f32(causal.astype(F32), lf)
    eG = jnp.exp(G)
    enG = jnp.exp(-G)
    qg = q * eG
    kg = (1.0 - f) * enG
    A = jnp.where(causal, _hdot(qg, kg, tb=True), 0.0)
    egl = jnp.exp(jnp.sum(lf, axis=0, keepdims=True))
    return sg, f, causal, eG, enG, qg, kg, A, egl


def hgrn_fwd(z, lb, gain, name):
    C, K = HG_CHUNK, HG_DIM

    def body(q_ref, f_ref, v_ref, og_ref, p_ref, g_ref, y_ref, o_ref, st_ref, state):
        @pl.when(pl.program_id(0) == 0)
        def _():
            state[...] = jnp.zeros_like(state)

        for cc in range(HG_STEP_CHUNKS):
            rows = pl.ds(cc * C, C)
            for h in range(HG_HEADS):
                hd = pl.ds(h * K, K)
                v = v_ref[rows, hd]
                _, _, _, _, _, qg, kg, A, egl = _hgrn_gates(q_ref[rows, hd], f_ref[rows, hd], p_ref[:, hd])
                st = state[h]
                st_ref[h, cc] = st
                o = _hdot(A, v) + _hdot(qg, st, tb=True)
                state[h] = st * egl + _hdot(v, kg * egl, ta=True)
                o_ref[rows, hd] = o
                rs = lax.rsqrt(jnp.mean(o * o, axis=-1, keepdims=True) + EPS)
                og = og_ref[rows, hd]
                y_ref[rows, hd] = (((o * rs) * g_ref[:, hd]) * (og * _sigmoid(og))).astype(y_ref.dtype)

    R = HG_STEP_CHUNKS * C

    def zcol(section):
        return pl.BlockSpec((R, HG_WIDTH), lambda c: (c, section))

    vec = pl.BlockSpec((1, HG_WIDTH), lambda c: (0, 0))
    blk = pl.BlockSpec((R, HG_WIDTH), lambda c: (c, 0))
    return pallas_call(
        body, name=name, grid=(N_CHUNKS // HG_STEP_CHUNKS,),
        in_specs=[zcol(0), zcol(1), zcol(2), zcol(3), vec, vec],
        out_specs=[blk, blk, pl.BlockSpec((HG_HEADS, HG_STEP_CHUNKS, K, K), lambda c: (0, c, 0, 0))],
        out_shape=[jax.ShapeDtypeStruct((SEQ, HG_WIDTH), MXU_DTYPE), jax.ShapeDtypeStruct((SEQ, HG_WIDTH), F32),
                   jax.ShapeDtypeStruct((HG_HEADS, N_CHUNKS, K, K), F32)],
        scratch_shapes=[pltpu.VMEM((HG_HEADS, K, K), F32)],
        compiler_params=pltpu.CompilerParams(dimension_semantics=("arbitrary",)),
    )(z, z, z, z, lb, gain)


def hgrn_bwd(z, lb, gain, o_raw, states, dy, name):
    C, K = HG_CHUNK, HG_DIM

    def body(q_ref, f_ref, v_ref, og_ref, p_ref, g_ref, o_ref, st_ref, dy_ref,
             dq_ref, dfp_ref, dv_ref, dog_ref, dlb_ref, dgain_ref, dstate):
        @pl.when(pl.program_id(0) == 0)
        def _():
            dstate[...] = jnp.zeros_like(dstate)
            dlb_ref[...] = jnp.zeros_like(dlb_ref)
            dgain_ref[...] = jnp.zeros_like(dgain_ref)

        last = lax.broadcasted_iota(jnp.int32, (C, K), 0) == C - 1
        row = lax.broadcasted_iota(jnp.int32, (C, C), 0)
        col = lax.broadcasted_iota(jnp.int32, (C, C), 1)
        anti_causal = (col >= row).astype(F32)
        for cc in reversed(range(HG_STEP_CHUNKS)):
            rows = pl.ds(cc * C, C)
            for h in range(HG_HEADS):
                hd = pl.ds(h * K, K)
                v = v_ref[rows, hd]
                lb = p_ref[:, hd]
                sg, f, causal, eG, enG, qg, kg, A, egl = _hgrn_gates(q_ref[rows, hd], f_ref[rows, hd], lb)
                kd = kg * egl
                st = st_ref[h, cc]
                dst = dstate[h]
                o = o_ref[rows, hd]
                og = og_ref[rows, hd]
                gain_v = g_ref[:, hd]
                dyv = dy_ref[rows, hd]
                rs = lax.rsqrt(jnp.mean(o * o, axis=-1, keepdims=True) + EPS)
                on = o * rs
                sgo = _sigmoid(og)
                silu = og * sgo
                dog_ref[rows, hd] = (dyv * (on * gain_v) * (sgo * (1.0 + og * (1.0 - sgo)))).astype(dog_ref.dtype)
                dgain_ref[:, hd] += jnp.sum(dyv * silu * on, axis=0, keepdims=True)
                don = dyv * gain_v * silu
                do = rs * (don - on * jnp.mean(don * on, axis=-1, keepdims=True))
                dA = jnp.where(causal, _hdot(do, v, tb=True), 0.0)
                dv_ref[rows, hd] = (_hdot(A, do, ta=True) + _hdot(kd, dst, tb=True)).astype(dv_ref.dtype)
                dqg = _hdot(dA, kg) + _hdot(do, st)
                dkg = _hdot(dA, qg, ta=True)
                dkd = _hdot(v, dst)
                dstate[h] = dst * egl + _hdot(do, qg, ta=True)
                dgl = jnp.sum(st * dst, axis=0, keepdims=True) * egl
                dq_ref[rows, hd] = (dqg * eG).astype(dq_ref.dtype)
                dk = dkg * enG + dkd * (enG * egl)
                dG = dqg * qg - dkg * kg - dkd * kd
                extra = jnp.sum(dkd * kd, axis=0, keepdims=True) + dgl
                dG = dG + jnp.where(last, extra, 0.0)
                dlf = _dot_f32(anti_causal, dG)
                df = dlf / f - dk
                dfp_ref[rows, hd] = (df * (1.0 - lb) * (sg * (1.0 - sg))).astype(dfp_ref.dtype)
                dlb_ref[:, hd] += jnp.sum(df * (1.0 - sg), axis=0, keepdims=True)

    R = HG_STEP_CHUNKS * C
    n_steps = N_CHUNKS // HG_STEP_CHUNKS

    def rc(c):
        return n_steps - 1 - c

    def zcol(section):
        return pl.BlockSpec((R, HG_WIDTH), lambda c: (rc(c), section))

    vec = pl.BlockSpec((1, HG_WIDTH), lambda c: (0, 0))
    blk = pl.BlockSpec((R, HG_WIDTH), lambda c: (rc(c), 0))
    out = jax.ShapeDtypeStruct((SEQ, HG_WIDTH), MXU_DTYPE)
    small = jax.ShapeDtypeStruct((1, HG_WIDTH), F32)
    return pallas_call(
        body, name=name, grid=(n_steps,),
        in_specs=[zcol(0), zcol(1), zcol(2), zcol(3), vec, vec, blk,
                  pl.BlockSpec((HG_HEADS, HG_STEP_CHUNKS, K, K), lambda c: (0, rc(c), 0, 0)), blk],
        out_specs=[blk, blk, blk, blk, vec, vec],
        out_shape=[out, out, out, out, small, small],
        scratch_shapes=[pltpu.VMEM((HG_HEADS, K, K), F32)],
        compiler_params=pltpu.CompilerParams(dimension_semantics=("arbitrary",)),
    )(z, z, z, z, lb, gain, o_raw, states, dy)


N_GROUPS = len(ATT_GROUPS)
HEAD_PAIRS = ATT_WIDTH // 128
ATT_COL0 = 4 * HG_WIDTH
UNROLLED_UNITS = 4


def _alibi_coef():
    n = N_GROUPS * ATT_HEADS
    slopes = np.exp2(-ALIBI_MAX * np.arange(1, n + 1, dtype=np.float32) / n).astype(np.float32)
    dil = np.repeat(np.array([d for _, d in ATT_GROUPS], np.float32), ATT_HEADS)
    return jnp.asarray(slopes * dil, F32)


def _for_each_unit(n, fn):
    if n <= UNROLLED_UNITS:
        for u in range(n):
            fn(u)
    else:
        def group(i, carry):
            for j in range(UNROLLED_UNITS):
                fn(i * UNROLLED_UNITS + j)
            return carry
        lax.fori_loop(0, n // UNROLLED_UNITS, group, 0)


def _att_geometry(g):
    B = ATT_BLOCK
    d = ATT_GROUPS[g][1]
    n_blocks = SEQ // (d * B)
    col0 = (ATT_COL0 + g * 3 * ATT_WIDTH) // 128

    def block_rows(b, r):
        return pl.ds(b * (B * d) + r, B, stride=d) if d > 1 else pl.ds(pl.multiple_of(b * B, B), B)

    def block_of(u):
        return (u, 0) if d == 1 else (u // d, u % d)

    return d, n_blocks, col0, block_rows, block_of


def _att_column(c):
    return pl.BlockSpec((SEQ, 128), lambda hp: (0, c + hp))


def _head_lanes(j):
    lane = lax.broadcasted_iota(jnp.int32, (ATT_BLOCK, 128), 1)
    return (lane >= 64 * j) & (lane < 64 * (j + 1))


def _stack_heads(x, sel0):
    return jnp.concatenate([jnp.where(sel0, x, 0.0), jnp.where(sel0, 0.0, x)], axis=0)


def _stack_values(x, sel0, lanes):
    swapped = pltpu.roll(x, 64, 1)
    stacked = jnp.concatenate([jnp.where(sel0, x, swapped), jnp.where(sel0, swapped, x)], axis=0)
    return stacked if lanes == 128 else jnp.concatenate([stacked] * (lanes // 128), axis=1)


def _pair_coef(coef_ref, g, hp):
    row = lax.broadcasted_iota(jnp.int32, (2 * ATT_BLOCK, 1), 0)
    first = g * ATT_HEADS + hp * 2
    return jnp.where(row < ATT_BLOCK, coef_ref[first], coef_ref[first + 1])


def _band(with_prev, first_key):
    B = ATT_BLOCK
    keys = 2 * B if with_prev else B
    qi = jnp.bitwise_and(lax.broadcasted_iota(jnp.int32, (2 * B, keys), 0), B - 1)
    kj = lax.broadcasted_iota(jnp.int32, (2 * B, keys), 1)
    delta = qi + (B if with_prev else 0) - kj
    valid = (delta >= 0) & (delta <= B)
    if with_prev:
        valid = valid & (kj >= first_key)
    return valid, delta.astype(F32)


def att_fwd(z, g, name):
    B = ATT_BLOCK
    d, n_blocks, col0, block_rows, block_of = _att_geometry(g)
    multi = n_blocks > 1

    def body(coef_ref, q_ref, k_ref, v_ref, o_ref, l_ref):
        cf2 = _pair_coef(coef_ref, g, pl.program_id(0))
        sel0 = _head_lanes(0)

        def one(u):
            b, r = block_of(u)
            rows = block_rows(b, r)
            valid, dist = _band(multi, jnp.where(b == 0, B, 0))
            q2 = _stack_heads(q_ref[rows, :], sel0)
            kk, vv = k_ref[rows, :], v_ref[rows, :]
            if multi:
                prev_rows = block_rows(jnp.maximum(b - 1, 0), r)
                kk = jnp.concatenate([k_ref[prev_rows, :], kk], axis=0)
                vv = jnp.concatenate([v_ref[prev_rows, :], vv], axis=0)
            sc = jnp.where(valid, _dot(q2, kk, tb=True) * 0.125 - cf2 * dist, NEG_INF)
            mx = jnp.max(sc, axis=-1, keepdims=True)
            e = jnp.exp(sc - mx)
            den = jnp.sum(e, axis=-1, keepdims=True)
            o2 = _dot(e * (1.0 / den), vv)
            lse2 = mx + jnp.log(den)
            o_ref[rows, :] = jnp.where(sel0, o2[:B], o2[B:])
            l_ref[rows, :] = jnp.where(sel0, lse2[:B], lse2[B:])

        _for_each_unit(d * n_blocks, one)

    out = jax.ShapeDtypeStruct((SEQ, ATT_WIDTH), F32)
    return pallas_call(
        body, name=name, grid=(HEAD_PAIRS,),
        in_specs=[pl.BlockSpec(memory_space=pltpu.SMEM), _att_column(col0), _att_column(col0 + 4), _att_column(col0 + 8)],
        out_specs=[_att_column(0), _att_column(0)], out_shape=[out, out],
        compiler_params=pltpu.CompilerParams(dimension_semantics=("parallel",)),
    )(_alibi_coef(), z, z, z)


def att_bwd(z, l, do, corr, g, name):
    B = ATT_BLOCK
    d, n_blocks, col0, block_rows, block_of = _att_geometry(g)
    multi = n_blocks > 1
    own = slice(B, 2 * B) if multi else slice(0, B)

    def body(coef_ref, q_ref, k_ref, v_ref, l_ref, do_ref, cr_ref, dq_ref, dk_ref, dv_ref, dq_sc, dk_sc, dv_sc):
        cf2 = _pair_coef(coef_ref, g, pl.program_id(0))
        sel0 = _head_lanes(0)

        def one(u):
            b, r = block_of(u)
            rows = block_rows(b, r)
            valid, dist = _band(multi, jnp.where(b == 0, B, 0))
            kk, vv = k_ref[rows, :], v_ref[rows, :]
            if multi:
                prev_rows = block_rows(jnp.maximum(b - 1, 0), r)
                kk = jnp.concatenate([k_ref[prev_rows, :], kk], axis=0)
                vv = jnp.concatenate([v_ref[prev_rows, :], vv], axis=0)
            q2, do2 = _stack_heads(q_ref[rows, :], sel0), _stack_heads(do_ref[rows, :], sel0)
            keys = kk.shape[0]
            lse2, cr2 = _stack_values(l_ref[rows, :], sel0, keys), _stack_values(cr_ref[rows, :], sel0, keys)
            p = jnp.exp(jnp.where(valid, _dot(q2, kk, tb=True) * 0.125 - cf2 * dist, NEG_INF) - lse2)
            ds = p * (_dot(do2, vv, tb=True) + cr2)
            dq2 = _dot(ds, kk)
            dkk = _dot(ds, q2, ta=True) * 0.125
            dvv = _dot(p, do2, ta=True)
            dq_sc[rows, :] = jnp.where(sel0, dq2[:B], dq2[B:]) * 0.125
            dk_sc[rows, :] = dkk[own]
            dv_sc[rows, :] = dvv[own]
            if multi:
                dk_sc[prev_rows, :] += dkk[:B]
                dv_sc[prev_rows, :] += dvv[:B]

        _for_each_unit(d * n_blocks, one)
        dq_ref[...] = dq_sc[...].astype(dq_ref.dtype)
        dk_ref[...] = dk_sc[...].astype(dk_ref.dtype)
        dv_ref[...] = dv_sc[...].astype(dv_ref.dtype)

    col = _att_column
    out = jax.ShapeDtypeStruct((SEQ, ATT_WIDTH), MXU_DTYPE)
    return pallas_call(
        body, name=name, grid=(HEAD_PAIRS,),
        in_specs=[pl.BlockSpec(memory_space=pltpu.SMEM), col(col0), col(col0 + 4), col(col0 + 8), col(0), col(0), col(0)],
        out_specs=[col(0)] * 3, out_shape=[out] * 3,
        scratch_shapes=[pltpu.VMEM((SEQ, 128), F32)] * 3,
        compiler_params=pltpu.CompilerParams(dimension_semantics=("parallel",), vmem_limit_bytes=MATMUL_VMEM_BYTES),
    )(_alibi_coef(), z, z, z, l, do, corr)


def _head_sum(x):
    i = lax.broadcasted_iota(jnp.int32, (128, 128), 0) // 64
    j = lax.broadcasted_iota(jnp.int32, (128, 128), 1) // 64
    return _dot_f32(x, (i == j).astype(F32), ones_on_right=True)


def _group_weights(l0, l1, l2):
    mx = jnp.maximum(jnp.maximum(l0, l1), l2)
    e0, e1, e2 = jnp.exp(l0 - mx), jnp.exp(l1 - mx), jnp.exp(l2 - mx)
    inv = 1.0 / (e0 + e1 + e2)
    return e0 * inv, e1 * inv, e2 * inv


def att_combine_fwd(o, l, name):
    def body(o0, o1, o2, l0, l1, l2, y_ref):
        w0, w1, w2 = _group_weights(l0[...], l1[...], l2[...])
        y_ref[...] = (o0[...] * w0 + o1[...] * w1 + o2[...] * w2).astype(y_ref.dtype)

    blk = pl.BlockSpec((ROW_TILE, ATT_WIDTH), lambda i: (i, 0))
    return pallas_call(
        body, name=name, grid=(SEQ // ROW_TILE,), in_specs=[blk] * 6, out_specs=blk,
        out_shape=jax.ShapeDtypeStruct((SEQ, ATT_WIDTH), MXU_DTYPE),
    )(*o, *l)


def att_combine_bwd(o, l, dy, name):
    def body(o0, o1, o2, l0, l1, l2, dy_ref, do0, do1, do2, cr0, cr1, cr2):
        w = _group_weights(l0[...], l1[...], l2[...])
        dyv = dy_ref[...]
        tot = _head_sum(dyv * (w[0] * o0[...] + w[1] * o1[...] + w[2] * o2[...]))
        for g, (do_ref, cr_ref) in enumerate(((do0, cr0), (do1, cr1), (do2, cr2))):
            do_ref[...] = dyv * w[g]
            cr_ref[...] = -w[g] * tot

    blk = pl.BlockSpec((ROW_TILE, 128), lambda i, j: (i, j))
    out = jax.ShapeDtypeStruct((SEQ, ATT_WIDTH), F32)
    res = pallas_call(
        body, name=name, grid=(SEQ // ROW_TILE, HEAD_PAIRS), in_specs=[blk] * 7, out_specs=[blk] * 6, out_shape=[out] * 6,
    )(*o, *l, dy)
    return res[:N_GROUPS], res[N_GROUPS:]


SUM_ROW_TILES = (1024, 512, 256, 128, 64, 32, 16)
SUM_TILE_BYTES = 24 * 1024 * 1024
SUM_PARAMS = pltpu.CompilerParams(vmem_limit_bytes=MATMUL_VMEM_BYTES)


def _row_tile(rows, cols, operands):
    fit = [t for t in SUM_ROW_TILES if rows % t == 0]
    return next((t for t in fit if 2 * 4 * operands * t * cols <= SUM_TILE_BYTES), fit[-1])


def _shard_shape(rows, cols, axis):
    return (rows // N_CHIPS, cols) if axis == 0 else (rows, cols // N_CHIPS)


def _half_shape(rows, cols, axis):
    return (rows, cols // 2) if axis == 0 else (rows // 2, cols)


def _piece_shape(rows, cols, axis):
    return (rows // N_CHIPS, cols // 2) if axis == 0 else (rows // 2, cols // N_CHIPS)


def place_own_block(shard, chip, rows, cols, axis, name):
    sr, sc = _shard_shape(rows, cols, axis)
    tr = _row_tile(sr, sc, 2)

    def body(chip_ref, s_ref, o_ref):
        o_ref[...] = s_ref[...].astype(o_ref.dtype)

    if axis == 0:
        out_map = lambda i, chip_ref: (chip_ref[0] * (sr // tr) + i, 0)
    else:
        out_map = lambda i, chip_ref: (i, chip_ref[0])
    return pallas_call(
        body, name=name, out_shape=jax.ShapeDtypeStruct((rows, cols), WEIGHT_COMM_DTYPE), compiler_params=SUM_PARAMS,
        grid_spec=pltpu.PrefetchScalarGridSpec(
            num_scalar_prefetch=1, grid=(sr // tr,), in_specs=[pl.BlockSpec((tr, sc), lambda i, chip_ref: (i, 0))],
            out_specs=pl.BlockSpec((tr, sc), out_map)),
    )(chip, shard)


def add_halves(g, theirs, core, rows, cols, axis, name):
    hr, hc = _half_shape(rows, cols, axis)
    tr = _row_tile(hr, hc, 3)

    def body(core_ref, g_ref, t_ref, o_ref):
        o_ref[...] = (g_ref[...].astype(F32) + t_ref[...].astype(F32)).astype(o_ref.dtype)

    if axis == 0:
        g_map = lambda i, core_ref: (i, core_ref[0])
    else:
        g_map = lambda i, core_ref: (core_ref[0] * (hr // tr) + i, 0)
    blk = pl.BlockSpec((tr, hc), lambda i, core_ref: (i, 0))
    return pallas_call(
        body, name=name, out_shape=jax.ShapeDtypeStruct((hr, hc), GRAD_COMM_DTYPE), compiler_params=SUM_PARAMS,
        grid_spec=pltpu.PrefetchScalarGridSpec(
            num_scalar_prefetch=1, grid=(hr // tr,), in_specs=[pl.BlockSpec((tr, hc), g_map), blk], out_specs=blk),
    )(core, g, theirs)


def add_pieces(half, got, chip, rows, cols, axis, name):
    hr, _ = _half_shape(rows, cols, axis)
    pr, pc = _piece_shape(rows, cols, axis)
    tr = _row_tile(pr, pc, 5)

    def body(chip_ref, h_ref, got_ref, o_ref):
        o_ref[...] = (h_ref[...].astype(F32) + got_ref[0].astype(F32) + got_ref[1].astype(F32) + got_ref[2].astype(F32))

    if axis == 0:
        h_map = lambda i, chip_ref: (chip_ref[0] * (pr // tr) + i, 0)
    else:
        h_map = lambda i, chip_ref: (i, chip_ref[0])
    return pallas_call(
        body, name=name, out_shape=jax.ShapeDtypeStruct((pr, pc), F32), compiler_params=SUM_PARAMS,
        grid_spec=pltpu.PrefetchScalarGridSpec(
            num_scalar_prefetch=1, grid=(pr // tr,),
            in_specs=[pl.BlockSpec((tr, pc), h_map), pl.BlockSpec((3, tr, pc), lambda i, chip_ref: (0, i, 0))],
            out_specs=pl.BlockSpec((tr, pc), lambda i, chip_ref: (i, 0))),
    )(chip, half, got)


def _adamw_math(w, g, m, v):
    nm = ADAM_B1 * m + (1.0 - ADAM_B1) * g
    nv = ADAM_B2 * v + (1.0 - ADAM_B2) * (g * g)
    m_hat = nm / (1.0 - ADAM_B1 ** ADAM_STEP)
    v_hat = nv / (1.0 - ADAM_B2 ** ADAM_STEP)
    return -ADAM_LR * (m_hat / (jnp.sqrt(v_hat) + ADAM_EPS) + ADAM_WD * w), nm, nv


def adamw_halves(w, mine, theirs, m, v, core, rows, cols, axis, name):
    sr, sc = _shard_shape(rows, cols, axis)
    pr, pc = _piece_shape(rows, cols, axis)
    tr = _row_tile(pr, pc, 9)
    nt = pr // tr

    def body(core_ref, w_ref, a_ref, b_ref, m_ref, v_ref, g_ref, d_ref, nm_ref, nv_ref):
        g = jnp.where(pl.program_id(0) == core_ref[0], a_ref[...], b_ref[...])
        g_ref[...] = g
        d_ref[...], nm_ref[...], nv_ref[...] = _adamw_math(w_ref[...], g, m_ref[...], v_ref[...])

    if axis == 0:
        full = pl.BlockSpec((tr, pc), lambda h, i, core_ref: (i, h))
    else:
        full = pl.BlockSpec((tr, pc), lambda h, i, core_ref: (h * nt + i, 0))
    part = pl.BlockSpec((tr, pc), lambda h, i, core_ref: (i, 0))
    out = jax.ShapeDtypeStruct((sr, sc), F32)
    return pallas_call(
        body, name=name, out_shape=[out, out, out, out], compiler_params=SUM_PARAMS,
        grid_spec=pltpu.PrefetchScalarGridSpec(
            num_scalar_prefetch=1, grid=(2, nt), in_specs=[full, part, part, full, full], out_specs=[full] * 4),
    )(core, w, mine, theirs, m, v)


BIG = (
    ("ffn1_w_gate_up", D_MODEL, 2 * D_FF, 1),
    ("ffn1_w_down", D_FF, D_MODEL, 0),
    ("w_in", D_MODEL, IN_COLS, 1),
    ("w_branch_hg", HG_WIDTH, D_MODEL, 1),
    ("w_branch_att", ATT_WIDTH, D_MODEL, 1),
    ("w_out", D_MODEL, D_MODEL, 0),
    ("ffn2_w_gate_up", D_MODEL, 2 * D_FF, 1),
    ("ffn2_w_down", D_FF, D_MODEL, 0),
)
N_BIG = len(BIG)
ANY = pl.BlockSpec(memory_space=pl.ANY)


def _place():
    return lax.axis_index("x"), lax.axis_index("y"), lax.axis_index("c")


def _other_chips(x, y):
    return ((1 - x, y), (x, 1 - y), (1 - x, 1 - y))


MAX_COPY_CHUNKS = 16
CHUNK_ROW_ALIGN = 16


def _row_chunks(view):
    rows = view.shape[0]
    n = next(n for n in range(MAX_COPY_CHUNKS, 0, -1) if rows % (CHUNK_ROW_ALIGN * n) == 0 or n == 1)
    step = rows // n
    return [pl.ds(i * step, step) for i in range(n)]


def _remote(src, dst, send_sem, recv_sem, device):
    return pltpu.make_async_remote_copy(src_ref=src, dst_ref=dst, send_sem=send_sem, recv_sem=recv_sem,
                                        device_id=device, device_id_type=MESH)


def _start_remote(src, dst, send_sem, recv_sem, device):
    for rows in _row_chunks(src):
        _remote(src.at[rows, :], dst.at[rows, :], send_sem, recv_sem, device).start()
    return _remote(src, dst, send_sem, recv_sem, device)


HBM = pl.BlockSpec(memory_space=pltpu.HBM)
SEM = pl.BlockSpec(memory_space=pltpu.SEMAPHORE)
SPLIT_COPY_EFFECT = pltpu.SideEffectType.DATAFLOW_SIDE_EFFECTING
GROUPS = {"ffn1": (0, 1), "mix": (2, 3, 4, 5), "ffn2": (6, 7)}


class _SemList:
    def __init__(self, refs):
        self.refs = refs
        self.at = self

    def __getitem__(self, index):
        w, k = index
        return self.refs[3 * w + k]


def _gather_piece(ref, rows, cols, axis, chip, c):
    sr, sc = _shard_shape(rows, cols, axis)
    j = 2 * chip[0] + chip[1]
    if axis == 0:
        return ref.at[pl.ds(j * sr + c * (sr // 2), sr // 2), :]
    return ref.at[pl.ds(c * (sr // 2), sr // 2), pl.ds(pl.multiple_of(j * sc, 128), sc)]


def _start_gather_sends(bufs, ws, send_sems, recv_sems):
    x, y, c = _place()
    for w, (_, r, cc, ax) in enumerate(ws):
        mine = _gather_piece(bufs[w], r, cc, ax, (x, y), c)
        for k, chip in enumerate(_other_chips(x, y)):
            _start_remote(mine, mine, send_sems.at[w, k], recv_sems.at[w, k], (*chip, c))


def _wait_gather_sends(bufs, ws, send_sems, recv_sems):
    x, y, c = _place()
    for w, (_, r, cc, ax) in enumerate(ws):
        for k, chip in enumerate(_other_chips(x, y)):
            got = _gather_piece(bufs[w], r, cc, ax, chip, c)
            _remote(got, got, send_sems.at[w, k], recv_sems.at[w, k], (x, y, c)).wait_recv()
    for w, (_, r, cc, ax) in enumerate(ws):
        mine = _gather_piece(bufs[w], r, cc, ax, (x, y), c)
        for k in range(3):
            _remote(mine, mine, send_sems.at[w, k], recv_sems.at[w, k], (x, y, c)).wait_send()


def _forward_halves(bufs, ws, send_sems, recv_sems):
    x, y, c = _place()
    passed = []
    for w, (_, r, cc, ax) in enumerate(ws):
        for k, chip in enumerate(_other_chips(x, y)):
            got = _gather_piece(bufs[w], r, cc, ax, chip, c)
            passed.append(_start_remote(got, got, send_sems.at[w, k], recv_sems.at[w, k], (x, y, 1 - c)))
    for w, (_, r, cc, ax) in enumerate(ws):
        for k, chip in enumerate(_other_chips(x, y)):
            got = _gather_piece(bufs[w], r, cc, ax, chip, 1 - c)
            _remote(got, got, send_sems.at[w, k], recv_sems.at[w, k], (x, y, c)).wait_recv()
    for cp in passed:
        cp.wait_send()


def gather_start(placed, after, group):
    ws = [BIG[i] for i in GROUPS[group]]
    n = len(ws)

    def body(*refs):
        bufs = refs[:n]
        send_sems, recv_sems = _SemList(refs[n + 1:4 * n + 1]), _SemList(refs[4 * n + 1:7 * n + 1])
        token = refs[-1]
        _start_gather_sends(bufs, ws, send_sems, recv_sems)
        token[...] = jnp.zeros_like(token)

    out = pallas_call(
        body, name=f"gather_start_{group}", in_specs=[HBM] * n + [ANY],
        out_specs=[SEM] * (6 * n) + [HBM] * n + [pl.BlockSpec(memory_space=pltpu.VMEM)],
        out_shape=[pltpu.SemaphoreType.DMA(())] * (6 * n)
        + [pltpu.HBM((r, cc), WEIGHT_COMM_DTYPE) for _, r, cc, _ in ws] + [jax.ShapeDtypeStruct((8, 128), F32)],
        input_output_aliases={w: 6 * n + w for w in range(n)},
        compiler_params=pltpu.CompilerParams(has_side_effects=SPLIT_COPY_EFFECT),
    )(*[_in_hbm(p) for p in placed], after)
    return out[:3 * n], out[3 * n:6 * n], out[6 * n:7 * n], out[-1]


def gather_wait(bufs, send_sems, recv_sems, after, group):
    ws = [BIG[i] for i in GROUPS[group]]
    n = len(ws)

    def body(*refs):
        _wait_gather_sends(refs[:n], ws, _SemList(refs[n:n + 3 * n]), _SemList(refs[n + 3 * n:n + 6 * n]))

    return pallas_call(
        body, name=f"gather_wait_{group}", in_specs=[HBM] * n + [SEM] * (6 * n) + [ANY] * len(after), out_specs=[HBM] * n,
        out_shape=[pltpu.HBM((r, cc), WEIGHT_COMM_DTYPE) for _, r, cc, _ in ws],
        input_output_aliases={w: w for w in range(n)},
        compiler_params=pltpu.CompilerParams(has_side_effects=SPLIT_COPY_EFFECT),
    )(*bufs, *send_sems, *recv_sems, *after)


def gather_forward(bufs, group):
    ws = [BIG[i] for i in GROUPS[group]]
    n = len(ws)

    def body(*refs):
        _forward_halves(refs[n:2 * n], ws, refs[2 * n], refs[2 * n + 1])

    return pallas_call(
        body, name=f"gather_forward_{group}", in_specs=[ANY] * n, out_specs=[ANY] * n,
        out_shape=[jax.ShapeDtypeStruct((r, cc), WEIGHT_COMM_DTYPE) for _, r, cc, _ in ws],
        input_output_aliases={w: w for w in range(n)},
        scratch_shapes=[pltpu.SemaphoreType.DMA((n, 3))] * 2,
    )(*bufs)


def _half(ref, rows, cols, axis, c):
    if axis == 0:
        return ref.at[:, pl.ds(pl.multiple_of(c * (cols // 2), 128), cols // 2)]
    return ref.at[pl.ds(c * (rows // 2), rows // 2), :]


def _piece_of_half(ref, rows, cols, axis, chip):
    j = 2 * chip[0] + chip[1]
    pr, pc = _piece_shape(rows, cols, axis)
    if axis == 0:
        return ref.at[pl.ds(j * pr, pr), :]
    return ref.at[:, pl.ds(pl.multiple_of(j * pc, 128), pc)]


def sibling_exchange_start(srcs, view, landing_shapes, dtype, name):
    n = len(srcs)

    def body(*refs):
        ins, land, sems = refs[:n], refs[n:2 * n], refs[2 * n:4 * n]
        x, y, c = _place()
        for w in range(n):
            _start_remote(view(ins[w], w, c), land[w], sems[w], sems[n + w], (x, y, 1 - c))
        refs[-1][...] = jnp.zeros_like(refs[-1])

    landing = [lax.empty(shape, dtype) for shape in landing_shapes]
    out = pallas_call(
        body, name=name, in_specs=[HBM] * (2 * n),
        out_specs=[SEM] * (2 * n) + [HBM] * (2 * n) + [pl.BlockSpec(memory_space=pltpu.VMEM)],
        out_shape=[pltpu.SemaphoreType.DMA(())] * (2 * n) + [pltpu.HBM(a.shape, a.dtype) for a in srcs]
        + [pltpu.HBM(shape, dtype) for shape in landing_shapes] + [jax.ShapeDtypeStruct((8, 128), F32)],
        input_output_aliases={i: 2 * n + i for i in range(2 * n)},
        compiler_params=pltpu.CompilerParams(has_side_effects=SPLIT_COPY_EFFECT),
    )(*[_in_hbm(a) for a in srcs], *[_in_hbm(b) for b in landing])
    return out[:n], out[n:2 * n], out[2 * n:3 * n], out[3 * n:4 * n], out[-1]


def sibling_exchange_wait(srcs, landing, send_sems, recv_sems, view, after, name):
    n = len(srcs)

    def body(*refs):
        ins, land, sems = refs[:n], refs[n:2 * n], refs[2 * n:4 * n]
        x, y, c = _place()
        for w in range(n):
            cp = _remote(view(ins[w], w, c), land[w], sems[w], sems[n + w], (x, y, c))
            cp.wait_send()
            cp.wait_recv()

    out = pallas_call(
        body, name=name, in_specs=[HBM] * (2 * n) + [SEM] * (2 * n) + [ANY] * len(after), out_specs=[HBM] * (2 * n),
        out_shape=[pltpu.HBM(a.shape, a.dtype) for a in srcs] + [pltpu.HBM(b.shape, b.dtype) for b in landing],
        input_output_aliases={i: i for i in range(2 * n)},
        compiler_params=pltpu.CompilerParams(has_side_effects=SPLIT_COPY_EFFECT),
    )(*srcs, *landing, *send_sems, *recv_sems, *after)
    return out[:n], out[n:]


def _scatter_copies(halves, got, ws, send_sems, recv_sems, start):
    x, y, c = _place()
    copies = []
    for w, (_, r, cc, ax) in enumerate(ws):
        for k, chip in enumerate(_other_chips(x, y)):
            args = (_piece_of_half(halves[w], r, cc, ax, chip), got[w].at[k], send_sems.at[w, k], recv_sems.at[w, k], (*chip, c))
            copies.append(_start_remote(*args) if start else _remote(*args))
    return copies


def scatter_start(halves, group):
    ws = [BIG[i] for i in GROUPS[group]]
    n = len(ws)

    def body(*refs):
        sems = refs[2 * n:8 * n]
        _scatter_copies(refs[:n], refs[n:2 * n], ws, _SemList(sems[:3 * n]), _SemList(sems[3 * n:]), start=True)
        refs[-1][...] = jnp.zeros_like(refs[-1])

    landing = [lax.empty((3,) + _piece_shape(r, cc, ax), GRAD_COMM_DTYPE) for _, r, cc, ax in ws]
    out = pallas_call(
        body, name=f"scatter_start_{group}", in_specs=[HBM] * (2 * n),
        out_specs=[SEM] * (6 * n) + [HBM] * (2 * n) + [pl.BlockSpec(memory_space=pltpu.VMEM)],
        out_shape=[pltpu.SemaphoreType.DMA(())] * (6 * n)
        + [pltpu.HBM(_half_shape(r, cc, ax), GRAD_COMM_DTYPE) for _, r, cc, ax in ws]
        + [pltpu.HBM((3,) + _piece_shape(r, cc, ax), GRAD_COMM_DTYPE) for _, r, cc, ax in ws]
        + [jax.ShapeDtypeStruct((8, 128), F32)],
        input_output_aliases={i: 6 * n + i for i in range(2 * n)},
        compiler_params=pltpu.CompilerParams(has_side_effects=SPLIT_COPY_EFFECT),
    )(*[_in_hbm(h) for h in halves], *[_in_hbm(b) for b in landing])
    return out[:3 * n], out[3 * n:6 * n], out[6 * n:7 * n], out[7 * n:8 * n], out[-1]


def scatter_wait(halves, got, send_sems, recv_sems, after, group):
    ws = [BIG[i] for i in GROUPS[group]]
    n = len(ws)

    def body(*refs):
        sems = refs[2 * n:8 * n]
        for cp in _scatter_copies(refs[:n], refs[n:2 * n], ws, _SemList(sems[:3 * n]), _SemList(sems[3 * n:]), start=False):
            cp.wait_send()
            cp.wait_recv()

    out = pallas_call(
        body, name=f"scatter_wait_{group}", in_specs=[HBM] * (2 * n) + [SEM] * (6 * n) + [ANY] * len(after),
        out_specs=[HBM] * (2 * n),
        out_shape=[pltpu.HBM(_half_shape(r, cc, ax), GRAD_COMM_DTYPE) for _, r, cc, ax in ws]
        + [pltpu.HBM((3,) + _piece_shape(r, cc, ax), GRAD_COMM_DTYPE) for _, r, cc, ax in ws],
        input_output_aliases={i: i for i in range(2 * n)},
        compiler_params=pltpu.CompilerParams(has_side_effects=SPLIT_COPY_EFFECT),
    )(*halves, *got, *send_sems, *recv_sems, *after)
    return out[:n], out[n:]


N_DEV = 8
SMALL = ("ffn1_norm", "mix_norm", "hg_lower_bounds", "hg_out_norm", "ffn2_norm", "final_norm")
SMALL_STAGE_ROWS = 8


def small_step(loss, grads, w, m, v, behind):
    n = len(SMALL)
    shapes = [g.shape for g in grads]
    first_row = [sum(s[0] for s in shapes[:i]) for i in range(n + 1)]
    assert first_row[n] < SMALL_STAGE_ROWS
    loss_row = (pl.ds(first_row[n], 1), pl.ds(0, loss.shape[1]))

    def body(*refs):
        loss_ref, g_refs, w_refs, m_refs, v_refs = refs[0], refs[1:1 + n], refs[1 + n:1 + 2 * n], refs[1 + 2 * n:1 + 3 * n], refs[1 + 3 * n:1 + 4 * n]
        outs = refs[2 + 4 * n:3 + 8 * n]
        loss_out, dg_refs, d_refs, nm_refs, nv_refs = outs[0], outs[1:1 + n], outs[1 + n:1 + 2 * n], outs[1 + 2 * n:1 + 3 * n], outs[1 + 3 * n:]
        stage, gathered, send_sems, recv_sems = refs[3 + 8 * n:]
        x, y, c = _place()
        me = 4 * x + 2 * y + c

        def slot(i, shape):
            return pl.ds(first_row[i], shape[0]), pl.ds(0, shape[1])

        stage[...] = jnp.zeros_like(stage)
        for i, g_ref in enumerate(g_refs):
            stage[slot(i, shapes[i])] = g_ref[...]
        stage[loss_row] = loss_ref[pl.ds(0, 1), :]
        gathered[me] = stage[...]
        copies = []
        for k in range(1, N_DEV):
            peer = (x ^ (k >> 2), y ^ ((k >> 1) & 1), c ^ (k & 1))
            cp = pltpu.make_async_remote_copy(
                src_ref=stage, dst_ref=gathered.at[me], send_sem=send_sems.at[k - 1], recv_sem=recv_sems.at[k - 1],
                device_id=peer, device_id_type=MESH)
            cp.start()
            copies.append(cp)
        for cp in copies:
            cp.wait()
        acc = gathered[0]
        for k in range(1, N_DEV):
            acc = acc + gathered[k]
        stage[...] = acc
        loss_out[...] = jnp.broadcast_to(stage[loss_row], loss_out.shape)
        for i in range(n):
            g = stage[slot(i, shapes[i])]
            dg_refs[i][...] = g
            d_refs[i][...], nm_refs[i][...], nv_refs[i][...] = _adamw_math(w_refs[i][...], g, m_refs[i][...], v_refs[i][...])

    vm = pl.BlockSpec(memory_space=pltpu.VMEM)
    per_param = [jax.ShapeDtypeStruct(s, F32) for s in shapes]
    out = pallas_call(
        body, name="small_step", in_specs=[vm] * (1 + 4 * n) + [ANY], out_specs=[vm] * (1 + 4 * n),
        out_shape=[jax.ShapeDtypeStruct(loss.shape, F32)] + per_param * 4,
        scratch_shapes=[pltpu.VMEM((SMALL_STAGE_ROWS, D_MODEL), F32),
                        pltpu.VMEM((N_DEV, SMALL_STAGE_ROWS, D_MODEL), F32),
                        pltpu.SemaphoreType.DMA((N_DEV - 1,)), pltpu.SemaphoreType.DMA((N_DEV - 1,))],
    )(loss, *grads, *w, *m, *v, behind)
    return out[0], out[1:1 + n], out[1 + n:1 + 2 * n], out[1 + 2 * n:1 + 3 * n], out[1 + 3 * n:]


def _swiglu_block_fwd(h, n, w_gu, w_down, tag, behind=()):
    a, b, s = gate_up_swiglu(n, w_gu, f"{tag}_gate_up", behind=behind)
    h_out = matmul(s, w_down, res=h, scale=0.5, name=f"{tag}_down")
    return h_out, (n, a, b, s)


def _swiglu_block_bwd(h, norm_g, w_gu, w_down, saved, dh_out, df, tag, exchange, behind=()):
    n, a, b, s = saved
    d_down = matmul(s, df, ta=True, scale=0.5, out_dtype=GRAD_COMM_DTYPE, name=f"{tag}_d_w_down")
    ds = matmul(df, w_down, tb=True, scale=0.5, out_dtype=ACT_DTYPE, behind=behind, name=f"{tag}_d_s")
    dgu = swiglu_bwd(a, b, ds, f"{tag}_swiglu_bwd")
    d_gu = matmul(n, dgu, ta=True, out_dtype=GRAD_COMM_DTYPE, name=f"{tag}_d_w_gate_up")
    tokens = exchange.gradients_ready(tag, {f"{tag}_w_gate_up": d_gu, f"{tag}_w_down": d_down})
    dn = matmul(dgu, w_gu, tb=True, behind=tokens, name=f"{tag}_d_n")
    dh, dh_m, dg = rmsnorm_bwd(h, norm_g, dn, dh_out, f"{tag}_norm_bwd")
    return dh, dh_m, dg


def local_step(x, target, small, exchange):
    big = {}
    n1 = rmsnorm_fwd(x, small["ffn1_norm"], "ffn1_norm", behind=exchange.started)
    token, big_ffn1 = exchange.weights("ffn1", n1)
    big.update(big_ffn1)
    h1, saved1 = _swiglu_block_fwd(x, n1, big["ffn1_w_gate_up"], big["ffn1_w_down"], "ffn1", token)
    u = rmsnorm_fwd(h1, small["mix_norm"], "mix_norm")
    token, big_mix = exchange.weights("mix", u)
    big.update(big_mix)
    z = matmul(u, big["w_in"], behind=token, name="w_in")
    p = small["hg_lower_bounds"]
    lb = 1.0 / (1.0 + jnp.exp(p[1:2] - p[0:1]))
    y_hg, o_raw, states = hgrn_fwd(z, lb, small["hg_out_norm"], "hgrn_fwd")
    o_att, l_att = zip(*[att_fwd(z, g, f"att_fwd_{g}") for g in range(N_GROUPS)])
    y_att = att_combine_fwd(o_att, l_att, "att_combine")
    bh = matmul(y_hg, big["w_branch_hg"], name="branch_hg")
    ba = matmul(y_att, big["w_branch_att"], name="branch_att")
    merged = merge_fwd(z, bh, ba, "merge")
    h2 = matmul(merged, big["w_out"], res=h1, name="w_out")
    n2 = rmsnorm_fwd(h2, small["ffn2_norm"], "ffn2_norm")
    token, big_ffn2 = exchange.weights("ffn2", n2)
    big.update(big_ffn2)
    h3, saved2 = _swiglu_block_fwd(h2, n2, big["ffn2_w_gate_up"], big["ffn2_w_down"], "ffn2", token)
    dh3, dh3_m, d_final, loss = final_norm_loss(h3, small["final_norm"], target, "final_norm_loss")

    gs, gb = {"final_norm": d_final}, {}
    dh2, dh2_m, gs["ffn2_norm"] = _swiglu_block_bwd(
        h2, small["ffn2_norm"], big["ffn2_w_gate_up"], big["ffn2_w_down"], saved2, dh3, dh3_m, "ffn2", exchange)
    token = exchange.backward_done("ffn2", dh2)
    gb["w_out"] = matmul(merged, dh2_m, ta=True, out_dtype=GRAD_COMM_DTYPE, name="d_w_out")
    dmerged = matmul(dh2_m, big["w_out"], tb=True, behind=token, name="d_merged")
    dbh, dba, dgh, dga = merge_bwd(z, bh, ba, dmerged, "merge_bwd")
    gb["w_branch_hg"] = matmul(y_hg, dbh, ta=True, out_dtype=GRAD_COMM_DTYPE, name="d_w_branch_hg")
    gb["w_branch_att"] = matmul(y_att, dba, ta=True, out_dtype=GRAD_COMM_DTYPE, name="d_w_branch_att")
    dy_hg = matmul(dbh, big["w_branch_hg"], tb=True, name="d_y_hg")
    dy_att = matmul(dba, big["w_branch_att"], tb=True, name="d_y_att")
    dq, dfp, di, dog, d_lb, gs["hg_out_norm"] = hgrn_bwd(z, lb, small["hg_out_norm"], o_raw, states, dy_hg, "hgrn_bwd")
    do_att, corr = att_combine_bwd(o_att, l_att, dy_att, "att_combine_bwd")
    d_att = [part for g in range(N_GROUPS) for part in att_bwd(z, l_att[g], do_att[g], corr[g], g, f"att_bwd_{g}")]
    dz = jnp.concatenate([dq, dfp, di, dog, *d_att, dgh, dga], axis=1)
    gb["w_in"] = matmul(u, dz, ta=True, out_dtype=GRAD_COMM_DTYPE, name="d_w_in")
    token = exchange.gradients_ready("mix", gb)
    du = matmul(dz, big["w_in"], tb=True, behind=token, name="d_u")
    dh1, dh1_m, gs["mix_norm"] = rmsnorm_bwd(h1, small["mix_norm"], du, dh2, "mix_norm_bwd")
    token = exchange.backward_done("mix", dh1)
    dp0 = d_lb * lb * (1.0 - lb)
    gs["hg_lower_bounds"] = jnp.concatenate([dp0, -dp0], axis=0)
    dx, _, gs["ffn1_norm"] = _swiglu_block_bwd(
        x, small["ffn1_norm"], big["ffn1_w_gate_up"], big["ffn1_w_down"], saved1, dh1, dh1_m, "ffn1", exchange, token)
    exchange.backward_done("ffn1", dx)
    return loss, dx, gs


WEIGHTS = ("ffn1_norm", "ffn1_w_gate_up", "ffn1_w_down", "mix_norm", "w_in", "hg_lower_bounds", "hg_out_norm",
           "w_branch_hg", "w_branch_att", "w_out", "ffn2_norm", "ffn2_w_gate_up", "ffn2_w_down", "final_norm")


class WeightExchange:
    ORDER = ("ffn1", "mix", "ffn2")

    def __init__(self, shards, core, chip):
        self.core, self.chip = core, chip
        self.halving = None
        self.scattering = None
        self.reducing = {}
        first = self.ORDER[0]
        self.placed = {BIG[i][0]: place_own_block(shards[BIG[i][0]], chip, *BIG[i][1:], f"place_{BIG[i][0]}")
                       for i in GROUPS[first]}
        self._start_gather(first, chip)
        self.started = [self.token]
        chip_behind = chip + self.token[0, :1].astype(jnp.int32)
        for group in self.ORDER[1:]:
            for i in GROUPS[group]:
                n, r, cc, ax = BIG[i]
                self.placed[n] = place_own_block(shards[n], chip_behind, r, cc, ax, f"place_{n}")
        self.placed_behind = [self.placed[n] for group in self.ORDER[1:] for n in self._names(group)]

    def _names(self, group):
        return [BIG[i][0] for i in GROUPS[group]]

    def _start_gather(self, group, after):
        send_sems, recv_sems, bufs, self.token = gather_start([self.placed[n] for n in self._names(group)], after, group)
        self.gathering = (group, send_sems, recv_sems, bufs)

    def weights(self, group, h):
        pending, send_sems, recv_sems, bufs = self.gathering
        assert pending == group
        after = [h] + (self.placed_behind if group == self.ORDER[0] else [])
        whole = gather_forward(gather_wait(bufs, send_sems, recv_sems, after, group), group)
        later = self.ORDER.index(group) + 1
        behind = []
        if later < len(self.ORDER):
            self._start_gather(self.ORDER[later], whole[0])
            behind = [self.token]
        return behind, dict(zip(self._names(group), whole))

    @staticmethod
    def _half_to_sibling(ws):
        return lambda ref, w, c: _half(ref, *ws[w][1:], 1 - c)

    def gradients_ready(self, group, grads):
        ws = [BIG[i] for i in GROUPS[group]]
        send_sems, recv_sems, own, theirs, token = sibling_exchange_start(
            [grads[n] for n, *_ in ws], self._half_to_sibling(ws), [_half_shape(r, cc, ax) for _, r, cc, ax in ws],
            GRAD_COMM_DTYPE, f"halves_start_{group}")
        self.halving = (group, send_sems, recv_sems, own, theirs)
        return [token]

    def backward_done(self, group, dh):
        pending, send_sems, recv_sems, own, theirs = self.halving
        assert pending == group
        ws = [BIG[i] for i in GROUPS[group]]
        own, theirs = sibling_exchange_wait(own, theirs, send_sems, recv_sems, self._half_to_sibling(ws), [dh],
                                            f"halves_wait_{group}")
        halves = [add_halves(g, t, self.core, r, cc, ax, f"add_halves_{n}") for (n, r, cc, ax), g, t in zip(ws, own, theirs)]
        previous = self.scattering
        send_sems, recv_sems, halves, got, self.token = scatter_start(halves, group)
        self.scattering = (group, send_sems, recv_sems, halves, got)
        behind = [self._finish_scatter(previous, [self.token])] if previous is not None else []
        return behind + [self.token]

    def _finish_scatter(self, scattering, after):
        group, send_sems, recv_sems, halves, got = scattering
        halves, got = scatter_wait(halves, got, send_sems, recv_sems, after, group)
        ws = [BIG[i] for i in GROUPS[group]]
        mine = [add_pieces(h, g, self.chip, r, cc, ax, f"add_pieces_{n}") for (n, r, cc, ax), h, g in zip(ws, halves, got)]
        send_sems, recv_sems, mine, theirs, token = sibling_exchange_start(
            mine, lambda ref, w, c: ref, [_piece_shape(r, cc, ax) for _, r, cc, ax in ws], F32, f"reduced_start_{group}")
        self.reducing[group] = (send_sems, recv_sems, mine, theirs)
        return token

    def finish(self, after):
        return self._finish_scatter(self.scattering, after)

    def reduced_halves(self, group, after):
        send_sems, recv_sems, mine, theirs = self.reducing.pop(group)
        mine, theirs = sibling_exchange_wait(mine, theirs, send_sems, recv_sems, lambda ref, w, c: ref, after,
                                             f"reduced_wait_{group}")
        return {BIG[i][0]: (a, b) for i, a, b in zip(GROUPS[group], mine, theirs)}


def kernel(x, ffn1_norm, ffn1_w_gate_up, ffn1_w_down, mix_norm, w_in, hg_lower_bounds, hg_out_norm, w_branch_hg, w_branch_att, w_out, ffn2_norm, ffn2_w_gate_up, ffn2_w_down, final_norm, loss_target, m_ffn1_norm, m_ffn1_w_gate_up, m_ffn1_w_down, m_mix_norm, m_w_in, m_hg_lower_bounds, m_hg_out_norm, m_w_branch_hg, m_w_branch_att, m_w_out, m_ffn2_norm, m_ffn2_w_gate_up, m_ffn2_w_down, m_final_norm, v_ffn1_norm, v_ffn1_w_gate_up, v_ffn1_w_down, v_mix_norm, v_w_in, v_hg_lower_bounds, v_hg_out_norm, v_w_branch_hg, v_w_branch_att, v_w_out, v_ffn2_norm, v_ffn2_w_gate_up, v_ffn2_w_down, v_final_norm):
    w = dict(ffn1_norm=ffn1_norm, ffn1_w_gate_up=ffn1_w_gate_up, ffn1_w_down=ffn1_w_down, mix_norm=mix_norm, w_in=w_in,
             hg_lower_bounds=hg_lower_bounds, hg_out_norm=hg_out_norm, w_branch_hg=w_branch_hg, w_branch_att=w_branch_att,
             w_out=w_out, ffn2_norm=ffn2_norm, ffn2_w_gate_up=ffn2_w_gate_up, ffn2_w_down=ffn2_w_down, final_norm=final_norm)
    m = dict(ffn1_norm=m_ffn1_norm, ffn1_w_gate_up=m_ffn1_w_gate_up, ffn1_w_down=m_ffn1_w_down, mix_norm=m_mix_norm,
             w_in=m_w_in, hg_lower_bounds=m_hg_lower_bounds, hg_out_norm=m_hg_out_norm, w_branch_hg=m_w_branch_hg,
             w_branch_att=m_w_branch_att, w_out=m_w_out, ffn2_norm=m_ffn2_norm, ffn2_w_gate_up=m_ffn2_w_gate_up,
             ffn2_w_down=m_ffn2_w_down, final_norm=m_final_norm)
    v = dict(ffn1_norm=v_ffn1_norm, ffn1_w_gate_up=v_ffn1_w_gate_up, ffn1_w_down=v_ffn1_w_down, mix_norm=v_mix_norm,
             w_in=v_w_in, hg_lower_bounds=v_hg_lower_bounds, hg_out_norm=v_hg_out_norm, w_branch_hg=v_w_branch_hg,
             w_branch_att=v_w_branch_att, w_out=v_w_out, ffn2_norm=v_ffn2_norm, ffn2_w_gate_up=v_ffn2_w_gate_up,
             ffn2_w_down=v_ffn2_w_down, final_norm=v_final_norm)

    core = lax.axis_index("c").astype(jnp.int32).reshape(1)
    chip = (2 * lax.axis_index("x") + lax.axis_index("y")).astype(jnp.int32).reshape(1)
    exchange = WeightExchange({n: w[n][0] for n, *_ in BIG}, core, chip)
    small = {n: w[n] for n in SMALL}
    small["final_norm"] = final_norm.reshape(1, D_MODEL)

    loss, dx, gs = local_step(x[0], loss_target[0], small, exchange)

    grads, delta, new_m, new_v = {}, {}, {}, {}

    def update(group, core, after):
        reduced = exchange.reduced_halves(group, after)
        for i in GROUPS[group]:
            n, r, cc, ax = BIG[i]
            a, b = reduced[n]
            g, d, nm, nv = adamw_halves(w[n][0], a, b, m[n][0], v[n][0], core, r, cc, ax, f"adamw_{n}")
            grads[n], delta[n], new_m[n], new_v[n] = g[None], d[None], nm[None], nv[None]

    core_behind = core + exchange.token[0, :1].astype(jnp.int32)
    update("ffn2", core_behind, [exchange.token])
    update("mix", core_behind, [delta["ffn2_w_down"]])
    token = exchange.finish(after=[delta[BIG[i][0]] for group in ("ffn2", "mix") for i in GROUPS[group]])
    two_d = lambda a: a.reshape(1, D_MODEL) if a.ndim == 1 else a
    loss_sum, *small_out = small_step(loss, [gs[n] for n in SMALL], *[[two_d(p[n]) for n in SMALL] for p in (w, m, v)],
                                      behind=token)
    for result, parts in zip((grads, delta, new_m, new_v), small_out):
        result.update({n: a.reshape(w[n].shape) for n, a in zip(SMALL, parts)})
    update("ffn1", core, [loss_sum])

    return (loss_sum[0, 0], dx[None], *[grads[n] for n in WEIGHTS], *[delta[n] for n in WEIGHTS],
            *[new_m[n] for n in WEIGHTS], *[new_v[n] for n in WEIGHTS])
```

```python
import numpy as np
import jax
import jax.numpy as jnp
from jax import lax
from jax.experimental import pallas as pl
from jax.experimental.pallas import tpu as pltpu

SEQ = 2048
D_MODEL = 1024
D_FF = 2816
HG_HEADS = 4
HG_DIM = 128
HG_WIDTH = 512
HG_CHUNK = 64
ATT_GROUPS = ((128, 1), (512, 4), (2048, 16))
ATT_HEADS = 8
ATT_WIDTH = 512
ATT_BLOCK = 128
ALIBI_MAX = 8.0
IN_COLS = 8704
EPS = 1e-6
NEG_INF = -1e30
ADAM_LR = 0.001
ADAM_B1 = 0.9
ADAM_B2 = 0.999
ADAM_EPS = 1e-08
ADAM_WD = 0.01
ADAM_STEP = 10

N_CHIPS = 4
MXU_DTYPE = jnp.bfloat16
WEIGHT_COMM_DTYPE = jnp.bfloat16
GRAD_COMM_DTYPE = jnp.bfloat16
ACT_DTYPE = jnp.bfloat16
MESH = pl.DeviceIdType.MESH
F32 = jnp.float32


def _sigmoid(x):
    return 1.0 / (1.0 + jnp.exp(-x))


def _dot(a, b, ta=False, tb=False):
    dn = (((0 if ta else 1,), (1 if tb else 0,)), ((), ()))
    return lax.dot_general(a.astype(MXU_DTYPE), b.astype(MXU_DTYPE), dn, preferred_element_type=F32)


def _dot_f32(a, b, ones_on_right=False):
    x = a if ones_on_right else b
    hi = x.astype(jnp.bfloat16)
    rest = x - hi.astype(F32)
    mid = rest.astype(jnp.bfloat16)
    lo = (rest - mid.astype(F32)).astype(jnp.bfloat16)
    if ones_on_right:
        dot = lambda q: jnp.dot(q, b.astype(jnp.bfloat16), preferred_element_type=F32)
    else:
        dot = lambda q: jnp.dot(a.astype(jnp.bfloat16), q, preferred_element_type=F32)
    return dot(hi) + (dot(mid) + dot(lo))


def _split_bf16(x):
    hi = x.astype(jnp.bfloat16)
    return hi, (x - hi.astype(F32)).astype(jnp.bfloat16)


def _hdot(a, b, ta=False, tb=False):
    dn =(((0 if ta else 1,), (1 if tb else 0,)), ((), ()))
    (a_hi, a_lo), (b_hi, b_lo) = _split_bf16(a), _split_bf16(b)
    dot = lambda p, q: lax.dot_general(p, q, dn, preferred_element_type=F32)
    return dot(a_hi, b_hi) + (dot(a_lo, b_hi) + dot(a_hi, b_lo))


def _in_hbm(a):
    return pltpu.with_memory_space_constraint(a, pltpu.HBM)


def pallas_call(body, **kw):
    grid_spec = kw.get("grid_spec")
    specs = list(kw["in_specs"] if grid_spec is None else grid_spec.in_specs)
    n_prefetch = 0 if grid_spec is None else grid_spec.num_scalar_prefetch
    out_specs = kw["out_specs"] if grid_spec is None else grid_spec.out_specs
    one = not isinstance(kw["out_shape"], (list, tuple))
    shapes = [kw["out_shape"]] if one else list(kw["out_shape"])
    out_specs = [out_specs] if one else list(out_specs)
    shapes = [pltpu.HBM(a.shape, a.dtype) if s.memory_space is None and isinstance(a, jax.ShapeDtypeStruct) else a
              for a, s in zip(shapes, out_specs)]
    kw["out_shape"] = shapes[0] if one else shapes
    call = pl.pallas_call(body, **kw)

    def run(*args):
        assert len(args) == n_prefetch + len(specs)
        pinned = [_in_hbm(a) if s.memory_space is None else a for a, s in zip(args[n_prefetch:], specs)]
        return call(*args[:n_prefetch], *pinned)

    return run


MATMUL_VMEM_BYTES = 48 * 1024 * 1024
MATMUL_TILE_BYTES = 36 * 1024 * 1024
MXU_ALIGN = 128


def _divisors(n, most):
    return [t for t in range(min(n, most), 0, -MXU_ALIGN) if n % t == 0 and t % MXU_ALIGN == 0]


def _matmul_tiles(M, N, K, in_bytes, out_bytes, has_res):
    best = None
    for tk in _divisors(K, K):
        nk = K // tk
        for tm in _divisors(M, 2048):
            for tn in _divisors(N, 512):
                tiles = 2 * in_bytes * (tm * tk + tk * tn) + 2 * out_bytes * tm * tn
                tiles += 4 * tm * tn * ((nk > 1) + 2 * has_res)
                if tiles > MATMUL_TILE_BYTES:
                    continue
                traffic = in_bytes * (M * K * (1 if nk == 1 else N // tn) + K * N * (M // tm))
                key = (traffic, -tm * tn * tk)
                if best is None or key < best[0]:
                    best = (key, (tm, tn, tk))
    return best[1]


def matmul(a, b, *, ta=False, tb=False, out_dtype=F32, res=None, scale=1.0, behind=(), name):
    if ta:
        K, M = a.shape
    else:
        M, K = a.shape
    if tb:
        N, K2 = b.shape
    else:
        K2, N = b.shape
    assert K == K2 and a.dtype == b.dtype
    tm, tn, tk = _matmul_tiles(M, N, K, a.dtype.itemsize, jnp.dtype(out_dtype).itemsize, res is not None)
    nk = K // tk

    def finish(r, r_ref, o_ref):
        if scale != 1.0:
            r = r * scale
        if res is not None:
            r = r_ref[...] + r
        o_ref[...] = r.astype(out_dtype)

    def body(*refs):
        a_ref, b_ref = refs[:2]
        r_ref = refs[2] if res is not None else None
        o_ref = refs[2 + (res is not None) + len(behind)]
        if nk == 1:
            finish(_dot(a_ref[...], b_ref[...], ta, tb), r_ref, o_ref)
            return
        acc = refs[-1]
        k = pl.program_id(2)

        @pl.when(k == 0)
        def _():
            acc[...] = jnp.zeros_like(acc)

        acc[...] += _dot(a_ref[...], b_ref[...], ta, tb)

        @pl.when(k == nk - 1)
        def _():
            finish(acc[...], r_ref, o_ref)

    a_spec = pl.BlockSpec((tk, tm), lambda i, j, k: (k, i)) if ta else pl.BlockSpec((tm, tk), lambda i, j, k: (i, k))
    b_spec = pl.BlockSpec((tn, tk), lambda i, j, k: (j, k)) if tb else pl.BlockSpec((tk, tn), lambda i, j, k: (k, j))
    in_specs = [a_spec, b_spec]
    args = [a, b]
    if res is not None:
        in_specs.append(pl.BlockSpec((tm, tn), lambda i, j, k: (i, j)))
        args.append(res)
    for earlier in behind:
        in_specs.append(pl.BlockSpec(memory_space=pl.ANY))
        args.append(earlier)
    return pallas_call(
        body, name=name, grid=(M // tm, N // tn, nk), in_specs=in_specs,
        out_specs=pl.BlockSpec((tm, tn), lambda i, j, k: (i, j)),
        out_shape=jax.ShapeDtypeStruct((M, N), out_dtype),
        scratch_shapes=[pltpu.VMEM((tm, tn), F32)] if nk > 1 else [],
        compiler_params=pltpu.CompilerParams(dimension_semantics=("parallel", "parallel", "arbitrary"),
                                             vmem_limit_bytes=MATMUL_VMEM_BYTES),
    )(*args)


ROW_TILE = 256


def rmsnorm_fwd(x, g, name, behind=()):
    def body(x_ref, g_ref, *refs):
        n_ref = refs[-1]
        xv = x_ref[...]
        r = lax.rsqrt(jnp.mean(xv * xv, axis=-1, keepdims=True) + EPS)
        n_ref[...] = ((xv * r) * g_ref[...]).astype(n_ref.dtype)

    order = list(behind)
    return pallas_call(
        body, name=name, grid=(SEQ // ROW_TILE,),
        in_specs=[pl.BlockSpec((ROW_TILE, D_MODEL), lambda i: (i, 0)), pl.BlockSpec((1, D_MODEL), lambda i: (0, 0))]
        + [pl.BlockSpec(memory_space=pl.ANY)] * len(order),
        out_specs=pl.BlockSpec((ROW_TILE, D_MODEL), lambda i: (i, 0)),
        out_shape=jax.ShapeDtypeStruct((SEQ, D_MODEL), MXU_DTYPE),
    )(x, g, *order)


def rmsnorm_bwd(x, g, dn, dres, name):
    def body(x_ref, g_ref, dn_ref, dr_ref, dx_ref, dxm_ref, dg_ref):
        xv = x_ref[...]
        r = lax.rsqrt(jnp.mean(xv * xv, axis=-1, keepdims=True) + EPS)
        xh = xv * r
        dnv = dn_ref[...]

        @pl.when(pl.program_id(0) == 0)
        def _():
            dg_ref[...] = jnp.zeros_like(dg_ref)

        dg_ref[...] += jnp.sum(dnv * xh, axis=0, keepdims=True)
        dxh = dnv * g_ref[...]
        dx = dr_ref[...] + r * (dxh - xh * jnp.mean(dxh * xh, axis=-1, keepdims=True))
        dx_ref[...] = dx
        dxm_ref[...] = dx.astype(dxm_ref.dtype)

    row = pl.BlockSpec((ROW_TILE, D_MODEL), lambda i: (i, 0))
    vec = pl.BlockSpec((1, D_MODEL), lambda i: (0, 0))
    return pallas_call(
        body, name=name, grid=(SEQ // ROW_TILE,), in_specs=[row, vec, row, row], out_specs=[row, row, vec],
        out_shape=[jax.ShapeDtypeStruct((SEQ, D_MODEL), F32), jax.ShapeDtypeStruct((SEQ, D_MODEL), MXU_DTYPE),
                   jax.ShapeDtypeStruct((1, D_MODEL), F32)],
        compiler_params=pltpu.CompilerParams(dimension_semantics=("arbitrary",)),
    )(x, g, dn, dres)


def final_norm_loss(h, g, target, name):
    def body(h_ref, g_ref, t_ref, dh_ref, dhm_ref, dg_ref, loss_ref):
        xv = h_ref[...]
        r = lax.rsqrt(jnp.mean(xv * xv, axis=-1, keepdims=True) + EPS)
        xh = xv * r
        gv = g_ref[...]
        e = xh * gv - t_ref[...]

        @pl.when(pl.program_id(0) == 0)
        def _():
            dg_ref[...] = jnp.zeros_like(dg_ref)
            loss_ref[...] = jnp.zeros_like(loss_ref)

        part = 0.5 * jnp.sum(jnp.sum(e * e, axis=-1, keepdims=True) * (1.0 / D_MODEL), axis=0, keepdims=True)
        loss_ref[...] += jnp.broadcast_to(part, loss_ref.shape)
        dout = e * (1.0 / D_MODEL)
        dg_ref[...] += jnp.sum(dout * xh, axis=0, keepdims=True)
        dxh = dout * gv
        dh = r * (dxh - xh * jnp.mean(dxh * xh, axis=-1, keepdims=True))
        dh_ref[...] = dh
        dhm_ref[...] = dh.astype(dhm_ref.dtype)

    row = pl.BlockSpec((ROW_TILE, D_MODEL), lambda i: (i, 0))
    vec = pl.BlockSpec((1, D_MODEL), lambda i: (0, 0))
    return pallas_call(
        body, name=name, grid=(SEQ // ROW_TILE,), in_specs=[row, vec, row],
        out_specs=[row, row, vec, pl.BlockSpec((8, 128), lambda i: (0, 0))],
        out_shape=[jax.ShapeDtypeStruct((SEQ, D_MODEL), F32), jax.ShapeDtypeStruct((SEQ, D_MODEL), MXU_DTYPE),
                   jax.ShapeDtypeStruct((1, D_MODEL), F32), jax.ShapeDtypeStruct((8, 128), F32)],
        compiler_params=pltpu.CompilerParams(dimension_semantics=("arbitrary",)),
    )(h, g, target)


FFN_TILE = 256
FFN_TILES = D_FF // FFN_TILE


def gate_up_swiglu(n, w_gu, name, behind=()):
    def body(n_ref, wa_ref, wb_ref, *refs):
        a_ref, b_ref, s_ref = refs[len(behind):]
        nv = n_ref[...]
        a = _dot(nv, wa_ref[...])
        b = _dot(nv, wb_ref[...])
        a_ref[...] = a.astype(a_ref.dtype)
        b_ref[...] = b.astype(b_ref.dtype)
        s_ref[...] = (a * _sigmoid(a) * b).astype(s_ref.dtype)

    tile = pl.BlockSpec((SEQ, FFN_TILE), lambda j: (0, j))
    act = jax.ShapeDtypeStruct((SEQ, D_FF), ACT_DTYPE)
    return pallas_call(
        body, name=name, grid=(FFN_TILES,),
        in_specs=[pl.BlockSpec((SEQ, D_MODEL), lambda j: (0, 0)), pl.BlockSpec((D_MODEL, FFN_TILE), lambda j: (0, j)),
                  pl.BlockSpec((D_MODEL, FFN_TILE), lambda j: (0, j + FFN_TILES))]
        + [pl.BlockSpec(memory_space=pl.ANY)] * len(behind),
        out_specs=[tile, tile, tile], out_shape=[act, act, jax.ShapeDtypeStruct((SEQ, D_FF), MXU_DTYPE)],
        compiler_params=pltpu.CompilerParams(dimension_semantics=("parallel",), vmem_limit_bytes=MATMUL_VMEM_BYTES),
    )(n, w_gu, w_gu, *behind)


def swiglu_bwd(a, b, ds, name):
    rows = ROW_TILE // 2

    def body(a_ref, b_ref, ds_ref, o_ref):
        av = a_ref[...].astype(F32)
        sg = _sigmoid(av)
        dsv = ds_ref[...].astype(F32)
        o_ref[:, :D_FF] = (dsv * b_ref[...].astype(F32) * (sg * (1.0 + av * (1.0 - sg)))).astype(o_ref.dtype)
        o_ref[:, D_FF:] = (dsv * av * sg).astype(o_ref.dtype)

    blk = pl.BlockSpec((rows, D_FF), lambda i: (i, 0))
    return pallas_call(
        body, name=name, grid=(SEQ // rows,), in_specs=[blk, blk, blk],
        out_specs=pl.BlockSpec((rows, 2 * D_FF), lambda i: (i, 0)),
        out_shape=jax.ShapeDtypeStruct((SEQ, 2 * D_FF), MXU_DTYPE), compiler_params=SUM_PARAMS,
    )(a, b, ds)


GATE_HG_BLK = 6656 // 512
GATE_ATT_BLK = 7680 // 512


def merge_fwd(z, bh, ba, name):
    def body(gh_ref, ga_ref, bh_ref, ba_ref, o_ref):
        o_ref[...] = (_sigmoid(gh_ref[...]) * bh_ref[...] + _sigmoid(ga_ref[...]) * ba_ref[...]).astype(o_ref.dtype)

    blk = pl.BlockSpec((ROW_TILE, 512), lambda i, j: (i, j))
    return pallas_call(
        body, name=name, grid=(SEQ // ROW_TILE, 2),
        in_specs=[pl.BlockSpec((ROW_TILE, 512), lambda i, j: (i, GATE_HG_BLK + j)),
                  pl.BlockSpec((ROW_TILE, 512), lambda i, j: (i, GATE_ATT_BLK + j)), blk, blk],
        out_specs=blk, out_shape=jax.ShapeDtypeStruct((SEQ, D_MODEL), MXU_DTYPE),
    )(z, z, bh, ba)


def merge_bwd(z, bh, ba, dm, name):
    def body(gh_ref, ga_ref, bh_ref, ba_ref, dm_ref, dbh_ref, dba_ref, dgh_ref, dga_ref):
        dmv = dm_ref[...]
        sh = _sigmoid(gh_ref[...])
        sa = _sigmoid(ga_ref[...])
        dbh_ref[...] = (dmv * sh).astype(dbh_ref.dtype)
        dba_ref[...] = (dmv * sa).astype(dba_ref.dtype)
        dgh_ref[...] = (dmv * bh_ref[...] * (sh * (1.0 - sh))).astype(dgh_ref.dtype)
        dga_ref[...] = (dmv * ba_ref[...] * (sa * (1.0 - sa))).astype(dga_ref.dtype)

    blk = pl.BlockSpec((ROW_TILE, 512), lambda i, j: (i, j))
    out = jax.ShapeDtypeStruct((SEQ, D_MODEL), MXU_DTYPE)
    return pallas_call(
        body, name=name, grid=(SEQ // ROW_TILE, 2),
        in_specs=[pl.BlockSpec((ROW_TILE, 512), lambda i, j: (i, GATE_HG_BLK + j)),
                  pl.BlockSpec((ROW_TILE, 512), lambda i, j: (i, GATE_ATT_BLK + j)), blk, blk, blk],
        out_specs=[blk, blk, blk, blk], out_shape=[out, out, out, out],
    )(z, z, bh, ba, dm)


N_CHUNKS = SEQ // HG_CHUNK
HG_STEP_CHUNKS = 4


def _hgrn_gates(q, fp, lb):
    C = HG_CHUNK
    sg = _sigmoid(fp)
    f = lb + (1.0 - lb) * sg
    lf = jnp.log(f)
    row = lax.broadcasted_iota(jnp.int32, (C, C), 0)
    col = lax.broadcasted_iota(jnp.int32, (C, C), 1)
    causal = row >= col
    G = _dot_f32(causal.astype(F32), lf)
    eG = jnp.exp(G)
    enG = jnp.exp(-G)
    qg = q * eG
    kg = (1.0 - f) * enG
    A = jnp.where(causal, _hdot(qg, kg, tb=True), 0.0)
    egl = jnp.exp(jnp.sum(lf, axis=0, keepdims=True))
    return sg, f, causal, eG, enG, qg, kg, A, egl


def hgrn_fwd(z, lb, gain, name):
    C, K = HG_CHUNK, HG_DIM

    def body(q_ref, f_ref, v_ref, og_ref, p_ref, g_ref, y_ref, o_ref, st_ref, state):
        @pl.when(pl.program_id(0) == 0)
        def _():
            state[...] = jnp.zeros_like(state)

        for cc in range(HG_STEP_CHUNKS):
            rows = pl.ds(cc * C, C)
            for h in range(HG_HEADS):
                hd = pl.ds(h * K, K)
                v = v_ref[rows, hd]
                _, _, _, _, _, qg, kg, A, egl = _hgrn_gates(q_ref[rows, hd], f_ref[rows, hd], p_ref[:, hd])
                st = state[h]
                st_ref[h, cc] = st
                o = _hdot(A, v) + _hdot(qg, st, tb=True)
                state[h] = st * egl + _hdot(v, kg * egl, ta=True)
                o_ref[rows, hd] = o
                rs = lax.rsqrt(jnp.mean(o * o, axis=-1, keepdims=True) + EPS)
                og = og_ref[rows, hd]
                y_ref[rows, hd] = (((o * rs) * g_ref[:, hd]) * (og * _sigmoid(og))).astype(y_ref.dtype)

    R = HG_STEP_CHUNKS * C

    def zcol(section):
        return pl.BlockSpec((R, HG_WIDTH), lambda c: (c, section))

    vec = pl.BlockSpec((1, HG_WIDTH), lambda c: (0, 0))
    blk = pl.BlockSpec((R, HG_WIDTH), lambda c: (c, 0))
    return pallas_call(
        body, name=name, grid=(N_CHUNKS // HG_STEP_CHUNKS,),
        in_specs=[zcol(0), zcol(1), zcol(2), zcol(3), vec, vec],
        out_specs=[blk, blk, pl.BlockSpec((HG_HEADS, HG_STEP_CHUNKS, K, K), lambda c: (0, c, 0, 0))],
        out_shape=[jax.ShapeDtypeStruct((SEQ, HG_WIDTH), MXU_DTYPE), jax.ShapeDtypeStruct((SEQ, HG_WIDTH), F32),
                   jax.ShapeDtypeStruct((HG_HEADS, N_CHUNKS, K, K), F32)],
        scratch_shapes=[pltpu.VMEM((HG_HEADS, K, K), F32)],
        compiler_params=pltpu.CompilerParams(dimension_semantics=("arbitrary",)),
    )(z, z, z, z, lb, gain)


def hgrn_bwd(z, lb, gain, o_raw, states, dy, name):
    C, K = HG_CHUNK, HG_DIM

    def body(q_ref, f_ref, v_ref, og_ref, p_ref, g_ref, o_ref, st_ref, dy_ref,
             dq_ref, dfp_ref, dv_ref, dog_ref, dlb_ref, dgain_ref, dstate):
        @pl.when(pl.program_id(0) == 0)
        def _():
            dstate[...] = jnp.zeros_like(dstate)
            dlb_ref[...] = jnp.zeros_like(dlb_ref)
            dgain_ref[...] = jnp.zeros_like(dgain_ref)

        last = lax.broadcasted_iota(jnp.int32, (C, K), 0) == C - 1
        row = lax.broadcasted_iota(jnp.int32, (C, C), 0)
        col = lax.broadcasted_iota(jnp.int32, (C, C), 1)
        anti_causal = (col >= row).astype(F32)
        for cc in reversed(range(HG_STEP_CHUNKS)):
            rows = pl.ds(cc * C, C)
            for h in range(HG_HEADS):
                hd = pl.ds(h * K, K)
                v = v_ref[rows, hd]
                lb = p_ref[:, hd]
                sg, f, causal, eG, enG, qg, kg, A, egl = _hgrn_gates(q_ref[rows, hd], f_ref[rows, hd], lb)
                kd = kg * egl
                st = st_ref[h, cc]
                dst = dstate[h]
                o = o_ref[rows, hd]
                og = og_ref[rows, hd]
                gain_v = g_ref[:, hd]
                dyv = dy_ref[rows, hd]
                rs = lax.rsqrt(jnp.mean(o * o, axis=-1, keepdims=True) + EPS)
                on = o * rs
                sgo = _sigmoid(og)
                silu = og * sgo
                dog_ref[rows, hd] = (dyv * (on * gain_v) * (sgo * (1.0 + og * (1.0 - sgo)))).astype(dog_ref.dtype)
                dgain_ref[:, hd] += jnp.sum(dyv * silu * on, axis=0, keepdims=True)
                don = dyv * gain_v * silu
                do = rs * (don - on * jnp.mean(don * on, axis=-1, keepdims=True))
                dA = jnp.where(causal, _hdot(do, v, tb=True), 0.0)
                dv_ref[rows, hd] = (_hdot(A, do, ta=True) + _hdot(kd, dst, tb=True)).astype(dv_ref.dtype)
                dqg = _hdot(dA, kg) + _hdot(do, st)
                dkg = _hdot(dA, qg, ta=True)
                dkd = _hdot(v, dst)
                dstate[h] = dst * egl + _hdot(do, qg, ta=True)
                dgl = jnp.sum(st * dst, axis=0, keepdims=True) * egl
                dq_ref[rows, hd] = (dqg * eG).astype(dq_ref.dtype)
                dk = dkg * enG + dkd * (enG * egl)
                dG = dqg * qg - dkg * kg - dkd * kd
                extra = jnp.sum(dkd * kd, axis=0, keepdims=True) + dgl
                dG = dG + jnp.where(last, extra, 0.0)
                dlf = _dot_f32(anti_causal, dG)
                df = dlf / f - dk
                dfp_ref[rows, hd] = (df * (1.0 - lb) * (sg * (1.0 - sg))).astype(dfp_ref.dtype)
                dlb_ref[:, hd] += jnp.sum(df * (1.0 - sg), axis=0, keepdims=True)

    R = HG_STEP_CHUNKS * C
    n_steps = N_CHUNKS // HG_STEP_CHUNKS

    def rc(c):
        return n_steps - 1 - c

    def zcol(section):
        return pl.BlockSpec((R, HG_WIDTH), lambda c: (rc(c), section))

    vec = pl.BlockSpec((1, HG_WIDTH), lambda c: (0, 0))
    blk = pl.BlockSpec((R, HG_WIDTH), lambda c: (rc(c), 0))
    out = jax.ShapeDtypeStruct((SEQ, HG_WIDTH), MXU_DTYPE)
    small = jax.ShapeDtypeStruct((1, HG_WIDTH), F32)
    return pallas_call(
        body, name=name, grid=(n_steps,),
        in_specs=[zcol(0), zcol(1), zcol(2), zcol(3), vec, vec, blk,
                  pl.BlockSpec((HG_HEADS, HG_STEP_CHUNKS, K, K), lambda c: (0, rc(c), 0, 0)), blk],
        out_specs=[blk, blk, blk, blk, vec, vec],
        out_shape=[out, out, out, out, small, small],
        scratch_shapes=[pltpu.VMEM((HG_HEADS, K, K), F32)],
        compiler_params=pltpu.CompilerParams(dimension_semantics=("arbitrary",)),
    )(z, z, z, z, lb, gain, o_raw, states, dy)


N_GROUPS = len(ATT_GROUPS)
HEAD_PAIRS = ATT_WIDTH // 128
ATT_COL0 = 4 * HG_WIDTH
UNROLLED_UNITS = 4


def _alibi_coef():
    n = N_GROUPS * ATT_HEADS
    slopes = np.exp2(-ALIBI_MAX * np.arange(1, n + 1, dtype=np.float32) / n).astype(np.float32)
    dil = np.repeat(np.array([d for _, d in ATT_GROUPS], np.float32), ATT_HEADS)
    return jnp.asarray(slopes * dil, F32)


def _for_each_unit(n, fn):
    if n <= UNROLLED_UNITS:
        for u in range(n):
            fn(u)
    else:
        def group(i, carry):
            for j in range(UNROLLED_UNITS):
                fn(i * UNROLLED_UNITS + j)
            return carry
        lax.fori_loop(0, n // UNROLLED_UNITS, group, 0)


def _att_geometry(g):
    B = ATT_BLOCK
    d = ATT_GROUPS[g][1]
    n_blocks = SEQ // (d * B)
    col0 = (ATT_COL0 + g * 3 * ATT_WIDTH) // 128

    def block_rows(b, r):
        return pl.ds(b * (B * d) + r, B, stride=d) if d > 1 else pl.ds(pl.multiple_of(b * B, B), B)

    def block_of(u):
        return (u, 0) if d == 1 else (u // d, u % d)

    return d, n_blocks, col0, block_rows, block_of


def _att_column(c):
    return pl.BlockSpec((SEQ, 128), lambda hp: (0, c + hp))


def _head_lanes(j):
    lane = lax.broadcasted_iota(jnp.int32, (ATT_BLOCK, 128), 1)
    return (lane >= 64 * j) & (lane < 64 * (j + 1))


def _stack_heads(x, sel0):
    return jnp.concatenate([jnp.where(sel0, x, 0.0), jnp.where(sel0, 0.0, x)], axis=0)


def _stack_values(x, sel0, lanes):
    swapped = pltpu.roll(x, 64, 1)
    stacked = jnp.concatenate([jnp.where(sel0, x, swapped), jnp.where(sel0, swapped, x)], axis=0)
    return stacked if lanes == 128 else jnp.concatenate([stacked] * (lanes // 128), axis=1)


def _pair_coef(coef_ref, g, hp):
    row = lax.broadcasted_iota(jnp.int32, (2 * ATT_BLOCK, 1), 0)
    first = g * ATT_HEADS + hp * 2
    return jnp.where(row < ATT_BLOCK, coef_ref[first], coef_ref[first + 1])


def _band(with_prev, first_key):
    B = ATT_BLOCK
    keys = 2 * B if with_prev else B
    qi = jnp.bitwise_and(lax.broadcasted_iota(jnp.int32, (2 * B, keys), 0), B - 1)
    kj = lax.broadcasted_iota(jnp.int32, (2 * B, keys), 1)
    delta = qi + (B if with_prev else 0) - kj
    valid = (delta >= 0) & (delta <= B)
    if with_prev:
        valid = valid & (kj >= first_key)
    return valid, delta.astype(F32)


def att_fwd(z, g, name):
    B = ATT_BLOCK
    d, n_blocks, col0, block_rows, block_of = _att_geometry(g)
    multi = n_blocks > 1

    def body(coef_ref, q_ref, k_ref, v_ref, o_ref, l_ref):
        cf2 = _pair_coef(coef_ref, g, pl.program_id(0))
        sel0 = _head_lanes(0)

        def one(u):
            b, r = block_of(u)
            rows = block_rows(b, r)
            valid, dist = _band(multi, jnp.where(b == 0, B, 0))
            q2 = _stack_heads(q_ref[rows, :], sel0)
            kk, vv = k_ref[rows, :], v_ref[rows, :]
            if multi:
                prev_rows = block_rows(jnp.maximum(b - 1, 0), r)
                kk = jnp.concatenate([k_ref[prev_rows, :], kk], axis=0)
                vv = jnp.concatenate([v_ref[prev_rows, :], vv], axis=0)
            sc = jnp.where(valid, _dot(q2, kk, tb=True) * 0.125 - cf2 * dist, NEG_INF)
            mx = jnp.max(sc, axis=-1, keepdims=True)
            e = jnp.exp(sc - mx)
            den = jnp.sum(e, axis=-1, keepdims=True)
            o2 = _dot(e * (1.0 / den), vv)
            lse2 = mx + jnp.log(den)
            o_ref[rows, :] = jnp.where(sel0, o2[:B], o2[B:])
            l_ref[rows, :] = jnp.where(sel0, lse2[:B], lse2[B:])

        _for_each_unit(d * n_blocks, one)

    out = jax.ShapeDtypeStruct((SEQ, ATT_WIDTH), F32)
    return pallas_call(
        body, name=name, grid=(HEAD_PAIRS,),
        in_specs=[pl.BlockSpec(memory_space=pltpu.SMEM), _att_column(col0), _att_column(col0 + 4), _att_column(col0 + 8)],
        out_specs=[_att_column(0), _att_column(0)], out_shape=[out, out],
        compiler_params=pltpu.CompilerParams(dimension_semantics=("parallel",)),
    )(_alibi_coef(), z, z, z)


def att_bwd(z, l, do, corr, g, name):
    B = ATT_BLOCK
    d, n_blocks, col0, block_rows, block_of = _att_geometry(g)
    multi = n_blocks > 1
    own = slice(B, 2 * B) if multi else slice(0, B)

    def body(coef_ref, q_ref, k_ref, v_ref, l_ref, do_ref, cr_ref, dq_ref, dk_ref, dv_ref, dq_sc, dk_sc, dv_sc):
        cf2 = _pair_coef(coef_ref, g, pl.program_id(0))
        sel0 = _head_lanes(0)

        def one(u):
            b, r = block_of(u)
            rows = block_rows(b, r)
            valid, dist = _band(multi, jnp.where(b == 0, B, 0))
            kk, vv = k_ref[rows, :], v_ref[rows, :]
            if multi:
                prev_rows = block_rows(jnp.maximum(b - 1, 0), r)
                kk = jnp.concatenate([k_ref[prev_rows, :], kk], axis=0)
                vv = jnp.concatenate([v_ref[prev_rows, :], vv], axis=0)
            q2, do2 = _stack_heads(q_ref[rows, :], sel0), _stack_heads(do_ref[rows, :], sel0)
            keys = kk.shape[0]
            lse2, cr2 = _stack_values(l_ref[rows, :], sel0, keys), _stack_values(cr_ref[rows, :], sel0, keys)
            p = jnp.exp(jnp.where(valid, _dot(q2, kk, tb=True) * 0.125 - cf2 * dist, NEG_INF) - lse2)
            ds = p * (_dot(do2, vv, tb=True) + cr2)
            dq2 = _dot(ds, kk)
            dkk = _dot(ds, q2, ta=True) * 0.125
            dvv = _dot(p, do2, ta=True)
            dq_sc[rows, :] = jnp.where(sel0, dq2[:B], dq2[B:]) * 0.125
            dk_sc[rows, :] = dkk[own]
            dv_sc[rows, :] = dvv[own]
            if multi:
                dk_sc[prev_rows, :] += dkk[:B]
                dv_sc[prev_rows, :] += dvv[:B]

        _for_each_unit(d * n_blocks, one)
        dq_ref[...] = dq_sc[...].astype(dq_ref.dtype)
        dk_ref[...] = dk_sc[...].astype(dk_ref.dtype)
        dv_ref[...] = dv_sc[...].astype(dv_ref.dtype)

    col = _att_column
    out = jax.ShapeDtypeStruct((SEQ, ATT_WIDTH), MXU_DTYPE)
    return pallas_call(
        body, name=name, grid=(HEAD_PAIRS,),
        in_specs=[pl.BlockSpec(memory_space=pltpu.SMEM), col(col0), col(col0 + 4), col(col0 + 8), col(0), col(0), col(0)],
        out_specs=[col(0)] * 3, out_shape=[out] * 3,
        scratch_shapes=[pltpu.VMEM((SEQ, 128), F32)] * 3,
        compiler_params=pltpu.CompilerParams(dimension_semantics=("parallel",), vmem_limit_bytes=MATMUL_VMEM_BYTES),
    )(_alibi_coef(), z, z, z, l, do, corr)


def _head_sum(x):
    i = lax.broadcasted_iota(jnp.int32, (128, 128), 0) // 64
    j = lax.broadcasted_iota(jnp.int32, (128, 128), 1) // 64
    return _dot_f32(x, (i == j).astype(F32), ones_on_right=True)


def _group_weights(l0, l1, l2):
    mx = jnp.maximum(jnp.maximum(l0, l1), l2)
    e0, e1, e2 = jnp.exp(l0 - mx), jnp.exp(l1 - mx), jnp.exp(l2 - mx)
    inv = 1.0 / (e0 + e1 + e2)
    return e0 * inv, e1 * inv, e2 * inv


def att_combine_fwd(o, l, name):
    def body(o0, o1, o2, l0, l1, l2, y_ref):
        w0, w1, w2 = _group_weights(l0[...], l1[...], l2[...])
        y_ref[...] = (o0[...] * w0 + o1[...] * w1 + o2[...] * w2).astype(y_ref.dtype)

    blk = pl.BlockSpec((ROW_TILE, ATT_WIDTH), lambda i: (i, 0))
    return pallas_call(
        body, name=name, grid=(SEQ // ROW_TILE,), in_specs=[blk] * 6, out_specs=blk,
        out_shape=jax.ShapeDtypeStruct((SEQ, ATT_WIDTH), MXU_DTYPE),
    )(*o, *l)


def att_combine_bwd(o, l, dy, name):
    def body(o0, o1, o2, l0, l1, l2, dy_ref, do0, do1, do2, cr0, cr1, cr2):
        w = _group_weights(l0[...], l1[...], l2[...])
        dyv = dy_ref[...]
        tot = _head_sum(dyv * (w[0] * o0[...] + w[1] * o1[...] + w[2] * o2[...]))
        for g, (do_ref, cr_ref) in enumerate(((do0, cr0), (do1, cr1), (do2, cr2))):
            do_ref[...] = dyv * w[g]
            cr_ref[...] = -w[g] * tot

    blk = pl.BlockSpec((ROW_TILE, 128), lambda i, j: (i, j))
    out = jax.ShapeDtypeStruct((SEQ, ATT_WIDTH), F32)
    res = pallas_call(
        body, name=name, grid=(SEQ // ROW_TILE, HEAD_PAIRS), in_specs=[blk] * 7, out_specs=[blk] * 6, out_shape=[out] * 6,
    )(*o, *l, dy)
    return res[:N_GROUPS], res[N_GROUPS:]


SUM_MAX_ROWS = 1024
SUM_ROW_ALIGN = 16
SUM_TILE_BYTES = 24 * 1024 * 1024
SUM_PARAMS = pltpu.CompilerParams(vmem_limit_bytes=MATMUL_VMEM_BYTES)


def _row_tile(rows, cols, operands):
    most = min(rows, SUM_MAX_ROWS) // SUM_ROW_ALIGN * SUM_ROW_ALIGN
    fit = [t for t in range(most, 0, -SUM_ROW_ALIGN) if rows % t == 0]
    return next((t for t in fit if 2 * 4 * operands * t * cols <= SUM_TILE_BYTES), fit[-1])


def _shard_shape(rows, cols, axis):
    return (rows // N_CHIPS, cols) if axis == 0 else (rows, cols // N_CHIPS)


def _half_shape(rows, cols, axis):
    return (rows, cols // 2) if axis == 0 else (rows // 2, cols)


def _piece_shape(rows, cols, axis):
    return (rows // N_CHIPS, cols // 2) if axis == 0 else (rows // 2, cols // N_CHIPS)


def place_own_block(shard, chip, rows, cols, axis, name):
    sr, sc = _shard_shape(rows, cols, axis)
    tr = _row_tile(sr, sc, 2)

    def body(chip_ref, s_ref, o_ref):
        o_ref[...] = s_ref[...].astype(o_ref.dtype)

    if axis == 0:
        out_map = lambda i, chip_ref: (chip_ref[0] * (sr // tr) + i, 0)
    else:
        out_map = lambda i, chip_ref: (i, chip_ref[0])
    return pallas_call(
        body, name=name, out_shape=jax.ShapeDtypeStruct((rows, cols), WEIGHT_COMM_DTYPE), compiler_params=SUM_PARAMS,
        grid_spec=pltpu.PrefetchScalarGridSpec(
            num_scalar_prefetch=1, grid=(sr // tr,), in_specs=[pl.BlockSpec((tr, sc), lambda i, chip_ref: (i, 0))],
            out_specs=pl.BlockSpec((tr, sc), out_map)),
    )(chip, shard)


def add_halves(g, theirs, core, rows, cols, axis, name):
    hr, hc = _half_shape(rows, cols, axis)
    tr = _row_tile(hr, hc, 3)

    def body(core_ref, g_ref, t_ref, o_ref):
        o_ref[...] = (g_ref[...].astype(F32) + t_ref[...].astype(F32)).astype(o_ref.dtype)

    if axis == 0:
        g_map = lambda i, core_ref: (i, core_ref[0])
    else:
        g_map = lambda i, core_ref: (core_ref[0] * (hr // tr) + i, 0)
    blk = pl.BlockSpec((tr, hc), lambda i, core_ref: (i, 0))
    return pallas_call(
        body, name=name, out_shape=jax.ShapeDtypeStruct((hr, hc), GRAD_COMM_DTYPE), compiler_params=SUM_PARAMS,
        grid_spec=pltpu.PrefetchScalarGridSpec(
            num_scalar_prefetch=1, grid=(hr // tr,), in_specs=[pl.BlockSpec((tr, hc), g_map), blk], out_specs=blk),
    )(core, g, theirs)


def add_pieces(half, got, chip, rows, cols, axis, name):
    hr, _ = _half_shape(rows, cols, axis)
    pr, pc = _piece_shape(rows, cols, axis)
    tr = _row_tile(pr, pc, 5)

    def body(chip_ref, h_ref, got_ref, o_ref):
        o_ref[...] = (h_ref[...].astype(F32) + got_ref[0].astype(F32) + got_ref[1].astype(F32) + got_ref[2].astype(F32))

    if axis == 0:
        h_map = lambda i, chip_ref: (chip_ref[0] * (pr // tr) + i, 0)
    else:
        h_map = lambda i, chip_ref: (i, chip_ref[0])
    return pallas_call(
        body, name=name, out_shape=jax.ShapeDtypeStruct((pr, pc), F32), compiler_params=SUM_PARAMS,
        grid_spec=pltpu.PrefetchScalarGridSpec(
            num_scalar_prefetch=1, grid=(pr // tr,),
            in_specs=[pl.BlockSpec((tr, pc), h_map), pl.BlockSpec((3, tr, pc), lambda i, chip_ref: (0, i, 0))],
            out_specs=pl.BlockSpec((tr, pc), lambda i, chip_ref: (i, 0))),
    )(chip, half, got)


def _adamw_math(w, g, m, v):
    nm = ADAM_B1 * m + (1.0 - ADAM_B1) * g
    nv = ADAM_B2 * v + (1.0 - ADAM_B2) * (g * g)
    m_hat = nm / (1.0 - ADAM_B1 ** ADAM_STEP)
    v_hat = nv / (1.0 - ADAM_B2 ** ADAM_STEP)
    return -ADAM_LR * (m_hat / (jnp.sqrt(v_hat) + ADAM_EPS) + ADAM_WD * w), nm, nv


def adamw_halves(w, mine, theirs, m, v, core, rows, cols, axis, name):
    sr, sc = _shard_shape(rows, cols, axis)
    pr, pc = _piece_shape(rows, cols, axis)
    tr = _row_tile(pr, pc, 9)
    nt = pr // tr

    def body(core_ref, w_ref, a_ref, b_ref, m_ref, v_ref, g_ref, d_ref, nm_ref, nv_ref):
        g = jnp.where(pl.program_id(0) == core_ref[0], a_ref[...], b_ref[...])
        g_ref[...] = g
        d_ref[...], nm_ref[...], nv_ref[...] = _adamw_math(w_ref[...], g, m_ref[...], v_ref[...])

    if axis == 0:
        full = pl.BlockSpec((tr, pc), lambda h, i, core_ref: (i, h))
    else:
        full = pl.BlockSpec((tr, pc), lambda h, i, core_ref: (h * nt + i, 0))
    part = pl.BlockSpec((tr, pc), lambda h, i, core_ref: (i, 0))
    out = jax.ShapeDtypeStruct((sr, sc), F32)
    return pallas_call(
        body, name=name, out_shape=[out, out, out, out], compiler_params=SUM_PARAMS,
        grid_spec=pltpu.PrefetchScalarGridSpec(
            num_scalar_prefetch=1, grid=(2, nt), in_specs=[full, part, part, full, full], out_specs=[full] * 4),
    )(core, w, mine, theirs, m, v)


BIG = (
    ("ffn1_w_gate_up", D_MODEL, 2 * D_FF, 1),
    ("ffn1_w_down", D_FF, D_MODEL, 0),
    ("w_in", D_MODEL, IN_COLS, 1),
    ("w_branch_hg", HG_WIDTH, D_MODEL, 1),
    ("w_branch_att", ATT_WIDTH, D_MODEL, 1),
    ("w_out", D_MODEL, D_MODEL, 0),
    ("ffn2_w_gate_up", D_MODEL, 2 * D_FF, 1),
    ("ffn2_w_down", D_FF, D_MODEL, 0),
)
N_BIG = len(BIG)
ANY = pl.BlockSpec(memory_space=pl.ANY)


def _place():
    return lax.axis_index("x"), lax.axis_index("y"), lax.axis_index("c")


def _other_chips(x, y):
    return ((1 - x, y), (x, 1 - y), (1 - x, 1 - y))


MAX_COPY_CHUNKS = 16
CHUNK_ROW_ALIGN = 16


def _row_chunks(view):
    rows = view.shape[0]
    n = next(n for n in range(MAX_COPY_CHUNKS, 0, -1) if rows % (CHUNK_ROW_ALIGN * n) == 0 or n == 1)
    step = rows // n
    return [pl.ds(i * step, step) for i in range(n)]


def _remote(src, dst, send_sem, recv_sem, device):
    return pltpu.make_async_remote_copy(src_ref=src, dst_ref=dst, send_sem=send_sem, recv_sem=recv_sem,
                                        device_id=device, device_id_type=MESH)


def _start_remote(src, dst, send_sem, recv_sem, device):
    for rows in _row_chunks(src):
        _remote(src.at[rows, :], dst.at[rows, :], send_sem, recv_sem, device).start()
    return _remote(src, dst, send_sem, recv_sem, device)


HBM = pl.BlockSpec(memory_space=pltpu.HBM)
SEM = pl.BlockSpec(memory_space=pltpu.SEMAPHORE)
SPLIT_COPY_EFFECT = pltpu.SideEffectType.DATAFLOW_SIDE_EFFECTING
GROUPS = {"ffn1": (0, 1), "mix": (2, 3, 4, 5), "ffn2": (6, 7)}


class _SemList:
    def __init__(self, refs):
        self.refs = refs
        self.at = self

    def __getitem__(self, index):
        w, k = index
        return self.refs[3 * w + k]


def _gather_piece(ref, rows, cols, axis, chip, c):
    sr, sc = _shard_shape(rows, cols, axis)
    j = 2 * chip[0] + chip[1]
    if axis == 0:
        return ref.at[pl.ds(j * sr + c * (sr // 2), sr // 2), :]
    return ref.at[pl.ds(c * (sr // 2), sr // 2), pl.ds(pl.multiple_of(j * sc, 128), sc)]


def _start_gather_sends(bufs, ws, send_sems, recv_sems):
    x, y, c = _place()
    for w, (_, r, cc, ax) in enumerate(ws):
        mine = _gather_piece(bufs[w], r, cc, ax, (x, y), c)
        for k, chip in enumerate(_other_chips(x, y)):
            _start_remote(mine, mine, send_sems.at[w, k], recv_sems.at[w, k], (*chip, c))


def _wait_gather_sends(bufs, ws, send_sems, recv_sems):
    x, y, c = _place()
    for w, (_, r, cc, ax) in enumerate(ws):
        for k, chip in enumerate(_other_chips(x, y)):
            got = _gather_piece(bufs[w], r, cc, ax, chip, c)
            _remote(got, got, send_sems.at[w, k], recv_sems.at[w, k], (x, y, c)).wait_recv()
    for w, (_, r, cc, ax) in enumerate(ws):
        mine = _gather_piece(bufs[w], r, cc, ax, (x, y), c)
        for k in range(3):
            _remote(mine, mine, send_sems.at[w, k], recv_sems.at[w, k], (x, y, c)).wait_send()


def _forward_halves(bufs, ws, send_sems, recv_sems):
    x, y, c = _place()
    passed = []
    for w, (_, r, cc, ax) in enumerate(ws):
        for k, chip in enumerate(_other_chips(x, y)):
            got = _gather_piece(bufs[w], r, cc, ax, chip, c)
            passed.append(_start_remote(got, got, send_sems.at[w, k], recv_sems.at[w, k], (x, y, 1 - c)))
    for w, (_, r, cc, ax) in enumerate(ws):
        for k, chip in enumerate(_other_chips(x, y)):
            got = _gather_piece(bufs[w], r, cc, ax, chip, 1 - c)
            _remote(got, got, send_sems.at[w, k], recv_sems.at[w, k], (x, y, c)).wait_recv()
    for cp in passed:
        cp.wait_send()


def gather_start(placed, after, group):
    ws = [BIG[i] for i in GROUPS[group]]
    n = len(ws)

    def body(*refs):
        bufs = refs[:n]
        send_sems, recv_sems = _SemList(refs[n + 1:4 * n + 1]), _SemList(refs[4 * n + 1:7 * n + 1])
        token = refs[-1]
        _start_gather_sends(bufs, ws, send_sems, recv_sems)
        token[...] = jnp.zeros_like(token)

    out = pallas_call(
        body, name=f"gather_start_{group}", in_specs=[HBM] * n + [ANY],
        out_specs=[SEM] * (6 * n) + [HBM] * n + [pl.BlockSpec(memory_space=pltpu.VMEM)],
        out_shape=[pltpu.SemaphoreType.DMA(())] * (6 * n)
        + [pltpu.HBM((r, cc), WEIGHT_COMM_DTYPE) for _, r, cc, _ in ws] + [jax.ShapeDtypeStruct((8, 128), F32)],
        input_output_aliases={w: 6 * n + w for w in range(n)},
        compiler_params=pltpu.CompilerParams(has_side_effects=SPLIT_COPY_EFFECT),
    )(*[_in_hbm(p) for p in placed], after)
    return out[:3 * n], out[3 * n:6 * n], out[6 * n:7 * n], out[-1]


def gather_wait(bufs, send_sems, recv_sems, after, group):
    ws = [BIG[i] for i in GROUPS[group]]
    n = len(ws)

    def body(*refs):
        _wait_gather_sends(refs[:n], ws, _SemList(refs[n:n + 3 * n]), _SemList(refs[n + 3 * n:n + 6 * n]))

    return pallas_call(
        body, name=f"gather_wait_{group}", in_specs=[HBM] * n + [SEM] * (6 * n) + [ANY] * len(after), out_specs=[HBM] * n,
        out_shape=[pltpu.HBM((r, cc), WEIGHT_COMM_DTYPE) for _, r, cc, _ in ws],
        input_output_aliases={w: w for w in range(n)},
        compiler_params=pltpu.CompilerParams(has_side_effects=SPLIT_COPY_EFFECT),
    )(*bufs, *send_sems, *recv_sems, *after)


def gather_forward(bufs, group):
    ws = [BIG[i] for i in GROUPS[group]]
    n = len(ws)

    def body(*refs):
        _forward_halves(refs[n:2 * n], ws, refs[2 * n], refs[2 * n + 1])

    return pallas_call(
        body, name=f"gather_forward_{group}", in_specs=[ANY] * n, out_specs=[ANY] * n,
        out_shape=[jax.ShapeDtypeStruct((r, cc), WEIGHT_COMM_DTYPE) for _, r, cc, _ in ws],
        input_output_aliases={w: w for w in range(n)},
        scratch_shapes=[pltpu.SemaphoreType.DMA((n, 3))] * 2,
    )(*bufs)


def _half(ref, rows, cols, axis, c):
    if axis == 0:
        return ref.at[:, pl.ds(pl.multiple_of(c * (cols // 2), 128), cols // 2)]
    return ref.at[pl.ds(c * (rows // 2), rows // 2), :]


def _piece_of_half(ref, rows, cols, axis, chip):
    j = 2 * chip[0] + chip[1]
    pr, pc = _piece_shape(rows, cols, axis)
    if axis == 0:
        return ref.at[pl.ds(j * pr, pr), :]
    return ref.at[:, pl.ds(pl.multiple_of(j * pc, 128), pc)]


def sibling_exchange_start(srcs, view, landing_shapes, dtype, name):
    n = len(srcs)

    def body(*refs):
        ins, land, sems = refs[:n], refs[n:2 * n], refs[2 * n:4 * n]
        x, y, c = _place()
        for w in range(n):
            _start_remote(view(ins[w], w, c), land[w], sems[w], sems[n + w], (x, y, 1 - c))
        refs[-1][...] = jnp.zeros_like(refs[-1])

    landing = [lax.empty(shape, dtype) for shape in landing_shapes]
    out = pallas_call(
        body, name=name, in_specs=[HBM] * (2 * n),
        out_specs=[SEM] * (2 * n) + [HBM] * (2 * n) + [pl.BlockSpec(memory_space=pltpu.VMEM)],
        out_shape=[pltpu.SemaphoreType.DMA(())] * (2 * n) + [pltpu.HBM(a.shape, a.dtype) for a in srcs]
        + [pltpu.HBM(shape, dtype) for shape in landing_shapes] + [jax.ShapeDtypeStruct((8, 128), F32)],
        input_output_aliases={i: 2 * n + i for i in range(2 * n)},
        compiler_params=pltpu.CompilerParams(has_side_effects=SPLIT_COPY_EFFECT),
    )(*[_in_hbm(a) for a in srcs], *[_in_hbm(b) for b in landing])
    return out[:n], out[n:2 * n], out[2 * n:3 * n], out[3 * n:4 * n], out[-1]


def sibling_exchange_wait(srcs, landing, send_sems, recv_sems, view, after, name):
    n = len(srcs)

    def body(*refs):
        ins, land, sems = refs[:n], refs[n:2 * n], refs[2 * n:4 * n]
        x, y, c = _place()
        for w in range(n):
            cp = _remote(view(ins[w], w, c), land[w], sems[w], sems[n + w], (x, y, c))
            cp.wait_send()
            cp.wait_recv()

    out = pallas_call(
        body, name=name, in_specs=[HBM] * (2 * n) + [SEM] * (2 * n) + [ANY] * len(after), out_specs=[HBM] * (2 * n),
        out_shape=[pltpu.HBM(a.shape, a.dtype) for a in srcs] + [pltpu.HBM(b.shape, b.dtype) for b in landing],
        input_output_aliases={i: i for i in range(2 * n)},
        compiler_params=pltpu.CompilerParams(has_side_effects=SPLIT_COPY_EFFECT),
    )(*srcs, *landing, *send_sems, *recv_sems, *after)
    return out[:n], out[n:]


def _scatter_copies(halves, got, ws, send_sems, recv_sems, start):
    x, y, c = _place()
    copies = []
    for w, (_, r, cc, ax) in enumerate(ws):
        for k, chip in enumerate(_other_chips(x, y)):
            args = (_piece_of_half(halves[w], r, cc, ax, chip), got[w].at[k], send_sems.at[w, k], recv_sems.at[w, k], (*chip, c))
            copies.append(_start_remote(*args) if start else _remote(*args))
    return copies


def scatter_start(halves, group):
    ws = [BIG[i] for i in GROUPS[group]]
    n = len(ws)

    def body(*refs):
        sems = refs[2 * n:8 * n]
        _scatter_copies(refs[:n], refs[n:2 * n], ws, _SemList(sems[:3 * n]), _SemList(sems[3 * n:]), start=True)
        refs[-1][...] = jnp.zeros_like(refs[-1])

    landing = [lax.empty((3,) + _piece_shape(r, cc, ax), GRAD_COMM_DTYPE) for _, r, cc, ax in ws]
    out = pallas_call(
        body, name=f"scatter_start_{group}", in_specs=[HBM] * (2 * n),
        out_specs=[SEM] * (6 * n) + [HBM] * (2 * n) + [pl.BlockSpec(memory_space=pltpu.VMEM)],
        out_shape=[pltpu.SemaphoreType.DMA(())] * (6 * n)
        + [pltpu.HBM(_half_shape(r, cc, ax), GRAD_COMM_DTYPE) for _, r, cc, ax in ws]
        + [pltpu.HBM((3,) + _piece_shape(r, cc, ax), GRAD_COMM_DTYPE) for _, r, cc, ax in ws]
        + [jax.ShapeDtypeStruct((8, 128), F32)],
        input_output_aliases={i: 6 * n + i for i in range(2 * n)},
        compiler_params=pltpu.CompilerParams(has_side_effects=SPLIT_COPY_EFFECT),
    )(*[_in_hbm(h) for h in halves], *[_in_hbm(b) for b in landing])
    return out[:3 * n], out[3 * n:6 * n], out[6 * n:7 * n], out[7 * n:8 * n], out[-1]


def scatter_wait(halves, got, send_sems, recv_sems, after, group):
    ws = [BIG[i] for i in GROUPS[group]]
    n = len(ws)

    def body(*refs):
        sems = refs[2 * n:8 * n]
        for cp in _scatter_copies(refs[:n], refs[n:2 * n], ws, _SemList(sems[:3 * n]), _SemList(sems[3 * n:]), start=False):
            cp.wait_send()
            cp.wait_recv()

    out = pallas_call(
        body, name=f"scatter_wait_{group}", in_specs=[HBM] * (2 * n) + [SEM] * (6 * n) + [ANY] * len(after),
        out_specs=[HBM] * (2 * n),
        out_shape=[pltpu.HBM(_half_shape(r, cc, ax), GRAD_COMM_DTYPE) for _, r, cc, ax in ws]
        + [pltpu.HBM((3,) + _piece_shape(r, cc, ax), GRAD_COMM_DTYPE) for _, r, cc, ax in ws],
        input_output_aliases={i: i for i in range(2 * n)},
        compiler_params=pltpu.CompilerParams(has_side_effects=SPLIT_COPY_EFFECT),
    )(*halves, *got, *send_sems, *recv_sems, *after)
    return out[:n], out[n:]


N_DEV = 8
SMALL = ("ffn1_norm", "mix_norm", "hg_lower_bounds", "hg_out_norm", "ffn2_norm", "final_norm")
SMALL_STAGE_ROWS = 8


def small_step(loss, grads, w, m, v, behind):
    n = len(SMALL)
    shapes = [g.shape for g in grads]
    first_row = [sum(s[0] for s in shapes[:i]) for i in range(n + 1)]
    assert first_row[n] < SMALL_STAGE_ROWS
    loss_row = (pl.ds(first_row[n], 1), pl.ds(0, loss.shape[1]))

    def body(*refs):
        loss_ref, g_refs, w_refs, m_refs, v_refs = refs[0], refs[1:1 + n], refs[1 + n:1 + 2 * n], refs[1 + 2 * n:1 + 3 * n], refs[1 + 3 * n:1 + 4 * n]
        outs = refs[2 + 4 * n:3 + 8 * n]
        loss_out, dg_refs, d_refs, nm_refs, nv_refs = outs[0], outs[1:1 + n], outs[1 + n:1 + 2 * n], outs[1 + 2 * n:1 + 3 * n], outs[1 + 3 * n:]
        stage, gathered, send_sems, recv_sems = refs[3 + 8 * n:]
        x, y, c = _place()
        me = 4 * x + 2 * y + c

        def slot(i, shape):
            return pl.ds(first_row[i], shape[0]), pl.ds(0, shape[1])

        stage[...] = jnp.zeros_like(stage)
        for i, g_ref in enumerate(g_refs):
            stage[slot(i, shapes[i])] = g_ref[...]
        stage[loss_row] = loss_ref[pl.ds(0, 1), :]
        gathered[me] = stage[...]
        copies = []
        for k in range(1, N_DEV):
            peer = (x ^ (k >> 2), y ^ ((k >> 1) & 1), c ^ (k & 1))
            cp = pltpu.make_async_remote_copy(
                src_ref=stage, dst_ref=gathered.at[me], send_sem=send_sems.at[k - 1], recv_sem=recv_sems.at[k - 1],
                device_id=peer, device_id_type=MESH)
            cp.start()
            copies.append(cp)
        for cp in copies:
            cp.wait()
        acc = gathered[0]
        for k in range(1, N_DEV):
            acc = acc + gathered[k]
        stage[...] = acc
        loss_out[...] = jnp.broadcast_to(stage[loss_row], loss_out.shape)
        for i in range(n):
            g = stage[slot(i, shapes[i])]
            dg_refs[i][...] = g
            d_refs[i][...], nm_refs[i][...], nv_refs[i][...] = _adamw_math(w_refs[i][...], g, m_refs[i][...], v_refs[i][...])

    vm = pl.BlockSpec(memory_space=pltpu.VMEM)
    per_param = [jax.ShapeDtypeStruct(s, F32) for s in shapes]
    out = pallas_call(
        body, name="small_step", in_specs=[vm] * (1 + 4 * n) + [ANY], out_specs=[vm] * (1 + 4 * n),
        out_shape=[jax.ShapeDtypeStruct(loss.shape, F32)] + per_param * 4,
        scratch_shapes=[pltpu.VMEM((SMALL_STAGE_ROWS, D_MODEL), F32),
                        pltpu.VMEM((N_DEV, SMALL_STAGE_ROWS, D_MODEL), F32),
                        pltpu.SemaphoreType.DMA((N_DEV - 1,)), pltpu.SemaphoreType.DMA((N_DEV - 1,))],
    )(loss, *grads, *w, *m, *v, behind)
    return out[0], out[1:1 + n], out[1 + n:1 + 2 * n], out[1 + 2 * n:1 + 3 * n], out[1 + 3 * n:]


def _swiglu_block_fwd(h, n, w_gu, w_down, tag, behind=()):
    a, b, s = gate_up_swiglu(n, w_gu, f"{tag}_gate_up", behind=behind)
    h_out = matmul(s, w_down, res=h, scale=0.5, name=f"{tag}_down")
    return h_out, (n, a, b, s)


def _swiglu_block_bwd(h, norm_g, w_gu, w_down, saved, dh_out, df, tag, exchange, behind=()):
    n, a, b, s = saved
    d_down = matmul(s, df, ta=True, scale=0.5, out_dtype=GRAD_COMM_DTYPE, name=f"{tag}_d_w_down")
    ds = matmul(df, w_down, tb=True, scale=0.5, out_dtype=ACT_DTYPE, behind=behind, name=f"{tag}_d_s")
    dgu = swiglu_bwd(a, b, ds, f"{tag}_swiglu_bwd")
    d_gu = matmul(n, dgu, ta=True, out_dtype=GRAD_COMM_DTYPE, name=f"{tag}_d_w_gate_up")
    tokens = exchange.gradients_ready(tag, {f"{tag}_w_gate_up": d_gu, f"{tag}_w_down": d_down})
    dn = matmul(dgu, w_gu, tb=True, behind=tokens, name=f"{tag}_d_n")
    dh, dh_m, dg = rmsnorm_bwd(h, norm_g, dn, dh_out, f"{tag}_norm_bwd")
    return dh, dh_m, dg


def local_step(x, target, small, exchange):
    big = {}
    n1 = rmsnorm_fwd(x, small["ffn1_norm"], "ffn1_norm", behind=exchange.started)
    token, big_ffn1 = exchange.weights("ffn1", n1)
    big.update(big_ffn1)
    h1, saved1 = _swiglu_block_fwd(x, n1, big["ffn1_w_gate_up"], big["ffn1_w_down"], "ffn1", token)
    u = rmsnorm_fwd(h1, small["mix_norm"], "mix_norm")
    token, big_mix = exchange.weights("mix", u)
    big.update(big_mix)
    z = matmul(u, big["w_in"], behind=token, name="w_in")
    p = small["hg_lower_bounds"]
    lb = 1.0 / (1.0 + jnp.exp(p[1:2] - p[0:1]))
    y_hg, o_raw, states = hgrn_fwd(z, lb, small["hg_out_norm"], "hgrn_fwd")
    o_att, l_att = zip(*[att_fwd(z, g, f"att_fwd_{g}") for g in range(N_GROUPS)])
    y_att = att_combine_fwd(o_att, l_att, "att_combine")
    bh = matmul(y_hg, big["w_branch_hg"], name="branch_hg")
    ba = matmul(y_att, big["w_branch_att"], name="branch_att")
    merged = merge_fwd(z, bh, ba, "merge")
    h2 = matmul(merged, big["w_out"], res=h1, name="w_out")
    n2 = rmsnorm_fwd(h2, small["ffn2_norm"], "ffn2_norm")
    token, big_ffn2 = exchange.weights("ffn2", n2)
    big.update(big_ffn2)
    h3, saved2 = _swiglu_block_fwd(h2, n2, big["ffn2_w_gate_up"], big["ffn2_w_down"], "ffn2", token)
    dh3, dh3_m, d_final, loss = final_norm_loss(h3, small["final_norm"], target, "final_norm_loss")

    gs, gb = {"final_norm": d_final}, {}
    dh2, dh2_m, gs["ffn2_norm"] = _swiglu_block_bwd(
        h2, small["ffn2_norm"], big["ffn2_w_gate_up"], big["ffn2_w_down"], saved2, dh3, dh3_m, "ffn2", exchange)
    token = exchange.backward_done("ffn2", dh2)
    gb["w_out"] = matmul(merged, dh2_m, ta=True, out_dtype=GRAD_COMM_DTYPE, name="d_w_out")
    dmerged = matmul(dh2_m, big["w_out"], tb=True, behind=token, name="d_merged")
    dbh, dba, dgh, dga = merge_bwd(z, bh, ba, dmerged, "merge_bwd")
    gb["w_branch_hg"] = matmul(y_hg, dbh, ta=True, out_dtype=GRAD_COMM_DTYPE, name="d_w_branch_hg")
    gb["w_branch_att"] = matmul(y_att, dba, ta=True, out_dtype=GRAD_COMM_DTYPE, name="d_w_branch_att")
    dy_hg = matmul(dbh, big["w_branch_hg"], tb=True, name="d_y_hg")
    dy_att = matmul(dba, big["w_branch_att"], tb=True, name="d_y_att")
    dq, dfp, di, dog, d_lb, gs["hg_out_norm"] = hgrn_bwd(z, lb, small["hg_out_norm"], o_raw, states, dy_hg, "hgrn_bwd")
    do_att, corr = att_combine_bwd(o_att, l_att, dy_att, "att_combine_bwd")
    d_att = [part for g in range(N_GROUPS) for part in att_bwd(z, l_att[g], do_att[g], corr[g], g, f"att_bwd_{g}")]
    dz = jnp.concatenate([dq, dfp, di, dog, *d_att, dgh, dga], axis=1)
    gb["w_in"] = matmul(u, dz, ta=True, out_dtype=GRAD_COMM_DTYPE, name="d_w_in")
    token = exchange.gradients_ready("mix", gb)
    du = matmul(dz, big["w_in"], tb=True, behind=token, name="d_u")
    dh1, dh1_m, gs["mix_norm"] = rmsnorm_bwd(h1, small["mix_norm"], du, dh2, "mix_norm_bwd")
    token = exchange.backward_done("mix", dh1)
    dp0 = d_lb * lb * (1.0 - lb)
    gs["hg_lower_bounds"] = jnp.concatenate([dp0, -dp0], axis=0)
    dx, _, gs["ffn1_norm"] = _swiglu_block_bwd(
        x, small["ffn1_norm"], big["ffn1_w_gate_up"], big["ffn1_w_down"], saved1, dh1, dh1_m, "ffn1", exchange, token)
    exchange.backward_done("ffn1", dx)
    return loss, dx, gs


WEIGHTS = ("ffn1_norm", "ffn1_w_gate_up", "ffn1_w_down", "mix_norm", "w_in", "hg_lower_bounds", "hg_out_norm",
           "w_branch_hg", "w_branch_att", "w_out", "ffn2_norm", "ffn2_w_gate_up", "ffn2_w_down", "final_norm")


class WeightExchange:
    ORDER = ("ffn1", "mix", "ffn2")

    def __init__(self, shards, core, chip):
        self.core, self.chip = core, chip
        self.halving = None
        self.scattering = None
        self.reducing = {}
        first = self.ORDER[0]
        self.placed = {BIG[i][0]: place_own_block(shards[BIG[i][0]], chip, *BIG[i][1:], f"place_{BIG[i][0]}")
                       for i in GROUPS[first]}
        self._start_gather(first, chip)
        self.started = [self.token]
        chip_behind = chip + self.token[0, :1].astype(jnp.int32)
        for group in self.ORDER[1:]:
            for i in GROUPS[group]:
                n, r, cc, ax = BIG[i]
                self.placed[n] = place_own_block(shards[n], chip_behind, r, cc, ax, f"place_{n}")
        self.placed_behind = [self.placed[n] for group in self.ORDER[1:] for n in self._names(group)]

    def _names(self, group):
        return [BIG[i][0] for i in GROUPS[group]]

    def _start_gather(self, group, after):
        send_sems, recv_sems, bufs, self.token = gather_start([self.placed[n] for n in self._names(group)], after, group)
        self.gathering = (group, send_sems, recv_sems, bufs)

    def weights(self, group, h):
        pending, send_sems, recv_sems, bufs = self.gathering
        assert pending == group
        after = [h] + (self.placed_behind if group == self.ORDER[0] else [])
        whole = gather_forward(gather_wait(bufs, send_sems, recv_sems, after, group), group)
        later = self.ORDER.index(group) + 1
        behind = []
        if later < len(self.ORDER):
            self._start_gather(self.ORDER[later], whole[0])
            behind = [self.token]
        return behind, dict(zip(self._names(group), whole))

    @staticmethod
    def _half_to_sibling(ws):
        return lambda ref, w, c: _half(ref, *ws[w][1:], 1 - c)

    def gradients_ready(self, group, grads):
        ws = [BIG[i] for i in GROUPS[group]]
        send_sems, recv_sems, own, theirs, token = sibling_exchange_start(
            [grads[n] for n, *_ in ws], self._half_to_sibling(ws), [_half_shape(r, cc, ax) for _, r, cc, ax in ws],
            GRAD_COMM_DTYPE, f"halves_start_{group}")
        self.halving = (group, send_sems, recv_sems, own, theirs)
        return [token]

    def backward_done(self, group, dh):
        pending, send_sems, recv_sems, own, theirs = self.halving
        assert pending == group
        ws = [BIG[i] for i in GROUPS[group]]
        own, theirs = sibling_exchange_wait(own, theirs, send_sems, recv_sems, self._half_to_sibling(ws), [dh],
                                            f"halves_wait_{group}")
        halves = [add_halves(g, t, self.core, r, cc, ax, f"add_halves_{n}") for (n, r, cc, ax), g, t in zip(ws, own, theirs)]
        previous = self.scattering
        send_sems, recv_sems, halves, got, self.token = scatter_start(halves, group)
        self.scattering = (group, send_sems, recv_sems, halves, got)
        behind = [self._finish_scatter(previous, [self.token])] if previous is not None else []
        return behind + [self.token]

    def _finish_scatter(self, scattering, after):
        group, send_sems, recv_sems, halves, got = scattering
        halves, got = scatter_wait(halves, got, send_sems, recv_sems, after, group)
        ws = [BIG[i] for i in GROUPS[group]]
        mine = [add_pieces(h, g, self.chip, r, cc, ax, f"add_pieces_{n}") for (n, r, cc, ax), h, g in zip(ws, halves, got)]
        send_sems, recv_sems, mine, theirs, token = sibling_exchange_start(
            mine, lambda ref, w, c: ref, [_piece_shape(r, cc, ax) for _, r, cc, ax in ws], F32, f"reduced_start_{group}")
        self.reducing[group] = (send_sems, recv_sems, mine, theirs)
        return token

    def finish(self, after):
        return self._finish_scatter(self.scattering, after)

    def reduced_halves(self, group, after):
        send_sems, recv_sems, mine, theirs = self.reducing.pop(group)
        mine, theirs = sibling_exchange_wait(mine, theirs, send_sems, recv_sems, lambda ref, w, c: ref, after,
                                             f"reduced_wait_{group}")
        return {BIG[i][0]: (a, b) for i, a, b in zip(GROUPS[group], mine, theirs)}


def kernel(x, ffn1_norm, ffn1_w_gate_up, ffn1_w_down, mix_norm, w_in, hg_lower_bounds, hg_out_norm, w_branch_hg, w_branch_att, w_out, ffn2_norm, ffn2_w_gate_up, ffn2_w_down, final_norm, loss_target, m_ffn1_norm, m_ffn1_w_gate_up, m_ffn1_w_down, m_mix_norm, m_w_in, m_hg_lower_bounds, m_hg_out_norm, m_w_branch_hg, m_w_branch_att, m_w_out, m_ffn2_norm, m_ffn2_w_gate_up, m_ffn2_w_down, m_final_norm, v_ffn1_norm, v_ffn1_w_gate_up, v_ffn1_w_down, v_mix_norm, v_w_in, v_hg_lower_bounds, v_hg_out_norm, v_w_branch_hg, v_w_branch_att, v_w_out, v_ffn2_norm, v_ffn2_w_gate_up, v_ffn2_w_down, v_final_norm):
    w = dict(ffn1_norm=ffn1_norm, ffn1_w_gate_up=ffn1_w_gate_up, ffn1_w_down=ffn1_w_down, mix_norm=mix_norm, w_in=w_in,
             hg_lower_bounds=hg_lower_bounds, hg_out_norm=hg_out_norm, w_branch_hg=w_branch_hg, w_branch_att=w_branch_att,
             w_out=w_out, ffn2_norm=ffn2_norm, ffn2_w_gate_up=ffn2_w_gate_up, ffn2_w_down=ffn2_w_down, final_norm=final_norm)
    m = dict(ffn1_norm=m_ffn1_norm, ffn1_w_gate_up=m_ffn1_w_gate_up, ffn1_w_down=m_ffn1_w_down, mix_norm=m_mix_norm,
             w_in=m_w_in, hg_lower_bounds=m_hg_lower_bounds, hg_out_norm=m_hg_out_norm, w_branch_hg=m_w_branch_hg,
             w_branch_att=m_w_branch_att, w_out=m_w_out, ffn2_norm=m_ffn2_norm, ffn2_w_gate_up=m_ffn2_w_gate_up,
             ffn2_w_down=m_ffn2_w_down, final_norm=m_final_norm)
    v = dict(ffn1_norm=v_ffn1_norm, ffn1_w_gate_up=v_ffn1_w_gate_up, ffn1_w_down=v_ffn1_w_down, mix_norm=v_mix_norm,
             w_in=v_w_in, hg_lower_bounds=v_hg_lower_bounds, hg_out_norm=v_hg_out_norm, w_branch_hg=v_w_branch_hg,
             w_branch_att=v_w_branch_att, w_out=v_w_out, ffn2_norm=v_ffn2_norm, ffn2_w_gate_up=v_ffn2_w_gate_up,
             ffn2_w_down=v_ffn2_w_down, final_norm=v_final_norm)

    core = lax.axis_index("c").astype(jnp.int32).reshape(1)
    chip = (2 * lax.axis_index("x") + lax.axis_index("y")).astype(jnp.int32).reshape(1)
    exchange = WeightExchange({n: w[n][0] for n, *_ in BIG}, core, chip)
    small = {n: w[n] for n in SMALL}
    small["final_norm"] = final_norm.reshape(1, D_MODEL)

    loss, dx, gs = local_step(x[0], loss_target[0], small, exchange)

    grads, delta, new_m, new_v = {}, {}, {}, {}

    def update(group, core, after):
        reduced = exchange.reduced_halves(group, after)
        for i in GROUPS[group]:
            n, r, cc, ax = BIG[i]
            a, b = reduced[n]
            g, d, nm, nv = adamw_halves(w[n][0], a, b, m[n][0], v[n][0], core, r, cc, ax, f"adamw_{n}")
            grads[n], delta[n], new_m[n], new_v[n] = g[None], d[None], nm[None], nv[None]

    core_behind = core + exchange.token[0, :1].astype(jnp.int32)
    update("ffn2", core_behind, [exchange.token])
    update("mix", core_behind, [delta["ffn2_w_down"]])
    token = exchange.finish(after=[delta[BIG[i][0]] for group in ("ffn2", "mix") for i in GROUPS[group]])
    two_d = lambda a: a.reshape(1, D_MODEL) if a.ndim == 1 else a
    loss_sum, *small_out = small_step(loss, [gs[n] for n in SMALL], *[[two_d(p[n]) for n in SMALL] for p in (w, m, v)],
                                      behind=token)
    for result, parts in zip((grads, delta, new_m, new_v), small_out):
        result.update({n: a.reshape(w[n].shape) for n, a in zip(SMALL, parts)})
    update("ffn1", core, [loss_sum])

    return (loss_sum[0, 0], dx[None], *[grads[n] for n in WEIGHTS], *[delta[n] for n in WEIGHTS],
            *[new_m[n] for n in WEIGHTS], *[new_v[n] for n in WEIGHTS])
```

```python
import numpy as np
import jax
import jax.numpy as jnp
from jax import lax
from jax.experimental import pallas as pl
from jax.experimental.pallas import tpu as pltpu

SEQ = 2048
D_MODEL = 1024
D_FF = 2816
HG_HEADS = 4
HG_DIM = 128
HG_WIDTH = 512
HG_CHUNK = 64
ATT_GROUPS = ((128, 1), (512, 4), (2048, 16))
ATT_HEADS = 8
ATT_WIDTH = 512
ATT_BLOCK = 128
ALIBI_MAX = 8.0
IN_COLS = 8704
EPS = 1e-6
NEG_INF = -1e30
ADAM_LR = 0.001
ADAM_B1 = 0.9
ADAM_B2 = 0.999
ADAM_EPS = 1e-08
ADAM_WD = 0.01
ADAM_STEP = 10

N_CHIPS = 4
MXU_DTYPE = jnp.bfloat16
WEIGHT_COMM_DTYPE = jnp.bfloat16
GRAD_COMM_DTYPE = jnp.bfloat16
ACT_DTYPE = jnp.bfloat16
MESH = pl.DeviceIdType.MESH
F32 = jnp.float32


def _sigmoid(x):
    return 1.0 / (1.0 + jnp.exp(-x))


def _dot(a, b, ta=False, tb=False):
    dn = (((0 if ta else 1,), (1 if tb else 0,)), ((), ()))
    return lax.dot_general(a.astype(MXU_DTYPE), b.astype(MXU_DTYPE), dn, preferred_element_type=F32)


def _dot_f32(a, b, ones_on_right=False):
    x = a if ones_on_right else b
    hi = x.astype(jnp.bfloat16)
    rest = x - hi.astype(F32)
    mid = rest.astype(jnp.bfloat16)
    lo = (rest - mid.astype(F32)).astype(jnp.bfloat16)
    if ones_on_right:
        dot = lambda q: jnp.dot(q, b.astype(jnp.bfloat16), preferred_element_type=F32)
    else:
        dot = lambda q: jnp.dot(a.astype(jnp.bfloat16), q, preferred_element_type=F32)
    return dot(hi) + (dot(mid) + dot(lo))


def _split_bf16(x):
    hi = x.astype(jnp.bfloat16)
    return hi, (x - hi.astype(F32)).astype(jnp.bfloat16)


def _hdot(a, b, ta=False, tb=False):
    dn =(((0 if ta else 1,), (1 if tb else 0,)), ((), ()))
    (a_hi, a_lo), (b_hi, b_lo) = _split_bf16(a), _split_bf16(b)
    dot = lambda p, q: lax.dot_general(p, q, dn, preferred_element_type=F32)
    return dot(a_hi, b_hi) + (dot(a_lo, b_hi) + dot(a_hi, b_lo))


def _in_hbm(a):
    return pltpu.with_memory_space_constraint(a, pltpu.HBM)


def pallas_call(body, **kw):
    grid_spec = kw.get("grid_spec")
    specs = list(kw["in_specs"] if grid_spec is None else grid_spec.in_specs)
    n_prefetch = 0 if grid_spec is None else grid_spec.num_scalar_prefetch
    out_specs = kw["out_specs"] if grid_spec is None else grid_spec.out_specs
    one = not isinstance(kw["out_shape"], (list, tuple))
    shapes = [kw["out_shape"]] if one else list(kw["out_shape"])
    out_specs = [out_specs] if one else list(out_specs)
    shapes = [pltpu.HBM(a.shape, a.dtype) if s.memory_space is None and isinstance(a, jax.ShapeDtypeStruct) else a
              for a, s in zip(shapes, out_specs)]
    kw["out_shape"] = shapes[0] if one else shapes
    call = pl.pallas_call(body, **kw)

    def run(*args):
        assert len(args) == n_prefetch + len(specs)
        pinned = [_in_hbm(a) if s.memory_space is None else a for a, s in zip(args[n_prefetch:], specs)]
        return call(*args[:n_prefetch], *pinned)

    return run


MATMUL_VMEM_BYTES = 48 * 1024 * 1024
MATMUL_TILE_BYTES = 36 * 1024 * 1024
MXU_ALIGN = 128
MXU_FLOPS_PER_SECOND = 900e12
HBM_BYTES_PER_SECOND = 3.0e12
GRID_STEP_SECONDS = 0.35e-6


def _divisors(n, most):
    return [t for t in range(min(n, most), 0, -MXU_ALIGN) if n % t == 0 and t % MXU_ALIGN == 0]


def _matmul_tiles(M, N, K, in_bytes, out_bytes, has_res):
    best = None
    for tk in _divisors(K, K):
        nk = K // tk
        for tm in _divisors(M, 2048):
            for tn in _divisors(N, 512):
                tiles = 2 * in_bytes * (tm * tk + tk * tn) + 2 * out_bytes * tm * tn
                tiles += 4 * tm * tn * ((nk > 1) + 2 * has_res)
                if tiles > MATMUL_TILE_BYTES:
                    continue
                traffic = in_bytes * (M * K * (1 if nk == 1 else N // tn) + K * N * (M // tm)) + out_bytes * M * N
                exposed = in_bytes * (tm * tk + tk * tn) + out_bytes * tm * tn
                seconds = (max(2 * M * N * K / MXU_FLOPS_PER_SECOND, traffic / HBM_BYTES_PER_SECOND)
                           + exposed / HBM_BYTES_PER_SECOND + (M // tm) * (N // tn) * nk * GRID_STEP_SECONDS)
                key = (seconds, -tm * tn * tk)
                if best is None or key < best[0]:
                    best = (key, (tm, tn, tk))
    return best[1]


def matmul(a, b, *, ta=False, tb=False, out_dtype=F32, res=None, scale=1.0, behind=(), name):
    if ta:
        K, M = a.shape
    else:
        M, K = a.shape
    if tb:
        N, K2 = b.shape
    else:
        K2, N = b.shape
    assert K == K2 and a.dtype == b.dtype
    tm, tn, tk = _matmul_tiles(M, N, K, a.dtype.itemsize, jnp.dtype(out_dtype).itemsize, res is not None)
    nk = K // tk

    def finish(r, r_ref, o_ref):
        if scale != 1.0:
            r = r * scale
        if res is not None:
            r = r_ref[...] + r
        o_ref[...] = r.astype(out_dtype)

    def body(*refs):
        a_ref, b_ref = refs[:2]
        r_ref = refs[2] if res is not None else None
        o_ref = refs[2 + (res is not None) + len(behind)]
        if nk == 1:
            finish(_dot(a_ref[...], b_ref[...], ta, tb), r_ref, o_ref)
            return
        acc = refs[-1]
        k = pl.program_id(2)

        @pl.when(k == 0)
        def _():
            acc[...] = jnp.zeros_like(acc)

        acc[...] += _dot(a_ref[...], b_ref[...], ta, tb)

        @pl.when(k == nk - 1)
        def _():
            finish(acc[...], r_ref, o_ref)

    a_spec = pl.BlockSpec((tk, tm), lambda i, j, k: (k, i)) if ta else pl.BlockSpec((tm, tk), lambda i, j, k: (i, k))
    b_spec = pl.BlockSpec((tn, tk), lambda i, j, k: (j, k)) if tb else pl.BlockSpec((tk, tn), lambda i, j, k: (k, j))
    in_specs = [a_spec, b_spec]
    args = [a, b]
    if res is not None:
        in_specs.append(pl.BlockSpec((tm, tn), lambda i, j, k: (i, j)))
        args.append(res)
    for earlier in behind:
        in_specs.append(pl.BlockSpec(memory_space=pl.ANY))
        args.append(earlier)
    return pallas_call(
        body, name=name, grid=(M // tm, N // tn, nk), in_specs=in_specs,
        out_specs=pl.BlockSpec((tm, tn), lambda i, j, k: (i, j)),
        out_shape=jax.ShapeDtypeStruct((M, N), out_dtype),
        scratch_shapes=[pltpu.VMEM((tm, tn), F32)] if nk > 1 else [],
        compiler_params=pltpu.CompilerParams(dimension_semantics=("parallel", "parallel", "arbitrary"),
                                             vmem_limit_bytes=MATMUL_VMEM_BYTES),
    )(*args)


ROW_TILE = 256


def rmsnorm_fwd(x, g, name, behind=()):
    def body(x_ref, g_ref, *refs):
        n_ref = refs[-1]
        xv = x_ref[...]
        r = lax.rsqrt(jnp.mean(xv * xv, axis=-1, keepdims=True) + EPS)
        n_ref[...] = ((xv * r) * g_ref[...]).astype(n_ref.dtype)

    order = list(behind)
    return pallas_call(
        body, name=name, grid=(SEQ // ROW_TILE,),
        in_specs=[pl.BlockSpec((ROW_TILE, D_MODEL), lambda i: (i, 0)), pl.BlockSpec((1, D_MODEL), lambda i: (0, 0))]
        + [pl.BlockSpec(memory_space=pl.ANY)] * len(order),
        out_specs=pl.BlockSpec((ROW_TILE, D_MODEL), lambda i: (i, 0)),
        out_shape=jax.ShapeDtypeStruct((SEQ, D_MODEL), MXU_DTYPE),
    )(x, g, *order)


def rmsnorm_bwd(x, g, dn, dres, name):
    def body(x_ref, g_ref, dn_ref, dr_ref, dx_ref, dxm_ref, dg_ref):
        xv = x_ref[...]
        r = lax.rsqrt(jnp.mean(xv * xv, axis=-1, keepdims=True) + EPS)
        xh = xv * r
        dnv = dn_ref[...]

        @pl.when(pl.program_id(0) == 0)
        def _():
            dg_ref[...] = jnp.zeros_like(dg_ref)

        dg_ref[...] += jnp.sum(dnv * xh, axis=0, keepdims=True)
        dxh = dnv * g_ref[...]
        dx = dr_ref[...] + r * (dxh - xh * jnp.mean(dxh * xh, axis=-1, keepdims=True))
        dx_ref[...] = dx
        dxm_ref[...] = dx.astype(dxm_ref.dtype)

    row = pl.BlockSpec((ROW_TILE, D_MODEL), lambda i: (i, 0))
    vec = pl.BlockSpec((1, D_MODEL), lambda i: (0, 0))
    return pallas_call(
        body, name=name, grid=(SEQ // ROW_TILE,), in_specs=[row, vec, row, row], out_specs=[row, row, vec],
        out_shape=[jax.ShapeDtypeStruct((SEQ, D_MODEL), F32), jax.ShapeDtypeStruct((SEQ, D_MODEL), MXU_DTYPE),
                   jax.ShapeDtypeStruct((1, D_MODEL), F32)],
        compiler_params=pltpu.CompilerParams(dimension_semantics=("arbitrary",)),
    )(x, g, dn, dres)


def final_norm_loss(h, g, target, name):
    def body(h_ref, g_ref, t_ref, dh_ref, dhm_ref, dg_ref, loss_ref):
        xv = h_ref[...]
        r = lax.rsqrt(jnp.mean(xv * xv, axis=-1, keepdims=True) + EPS)
        xh = xv * r
        gv = g_ref[...]
        e = xh * gv - t_ref[...]

        @pl.when(pl.program_id(0) == 0)
        def _():
            dg_ref[...] = jnp.zeros_like(dg_ref)
            loss_ref[...] = jnp.zeros_like(loss_ref)

        part = 0.5 * jnp.sum(jnp.sum(e * e, axis=-1, keepdims=True) * (1.0 / D_MODEL), axis=0, keepdims=True)
        loss_ref[...] += jnp.broadcast_to(part, loss_ref.shape)
        dout = e * (1.0 / D_MODEL)
        dg_ref[...] += jnp.sum(dout * xh, axis=0, keepdims=True)
        dxh = dout * gv
        dh = r * (dxh - xh * jnp.mean(dxh * xh, axis=-1, keepdims=True))
        dh_ref[...] = dh
        dhm_ref[...] = dh.astype(dhm_ref.dtype)

    row = pl.BlockSpec((ROW_TILE, D_MODEL), lambda i: (i, 0))
    vec = pl.BlockSpec((1, D_MODEL), lambda i: (0, 0))
    return pallas_call(
        body, name=name, grid=(SEQ // ROW_TILE,), in_specs=[row, vec, row],
        out_specs=[row, row, vec, pl.BlockSpec((8, 128), lambda i: (0, 0))],
        out_shape=[jax.ShapeDtypeStruct((SEQ, D_MODEL), F32), jax.ShapeDtypeStruct((SEQ, D_MODEL), MXU_DTYPE),
                   jax.ShapeDtypeStruct((1, D_MODEL), F32), jax.ShapeDtypeStruct((8, 128), F32)],
        compiler_params=pltpu.CompilerParams(dimension_semantics=("arbitrary",)),
    )(h, g, target)


FFN_TILE = 256
FFN_TILES = D_FF // FFN_TILE


def gate_up_swiglu(n, w_gu, name, behind=()):
    def body(n_ref, wa_ref, wb_ref, *refs):
        a_ref, b_ref, s_ref = refs[len(behind):]
        nv = n_ref[...]
        a = _dot(nv, wa_ref[...])
        b = _dot(nv, wb_ref[...])
        a_ref[...] = a.astype(a_ref.dtype)
        b_ref[...] = b.astype(b_ref.dtype)
        s_ref[...] = (a * _sigmoid(a) * b).astype(s_ref.dtype)

    tile = pl.BlockSpec((SEQ, FFN_TILE), lambda j: (0, j))
    act = jax.ShapeDtypeStruct((SEQ, D_FF), ACT_DTYPE)
    return pallas_call(
        body, name=name, grid=(FFN_TILES,),
        in_specs=[pl.BlockSpec((SEQ, D_MODEL), lambda j: (0, 0)), pl.BlockSpec((D_MODEL, FFN_TILE), lambda j: (0, j)),
                  pl.BlockSpec((D_MODEL, FFN_TILE), lambda j: (0, j + FFN_TILES))]
        + [pl.BlockSpec(memory_space=pl.ANY)] * len(behind),
        out_specs=[tile, tile, tile], out_shape=[act, act, jax.ShapeDtypeStruct((SEQ, D_FF), MXU_DTYPE)],
        compiler_params=pltpu.CompilerParams(dimension_semantics=("parallel",), vmem_limit_bytes=MATMUL_VMEM_BYTES),
    )(n, w_gu, w_gu, *behind)


def swiglu_bwd(a, b, ds, name):
    rows = ROW_TILE // 2

    def body(a_ref, b_ref, ds_ref, o_ref):
        av = a_ref[...].astype(F32)
        sg = _sigmoid(av)
        dsv = ds_ref[...].astype(F32)
        o_ref[:, :D_FF] = (dsv * b_ref[...].astype(F32) * (sg * (1.0 + av * (1.0 - sg)))).astype(o_ref.dtype)
        o_ref[:, D_FF:] = (dsv * av * sg).astype(o_ref.dtype)

    blk = pl.BlockSpec((rows, D_FF), lambda i: (i, 0))
    return pallas_call(
        body, name=name, grid=(SEQ // rows,), in_specs=[blk, blk, blk],
        out_specs=pl.BlockSpec((rows, 2 * D_FF), lambda i: (i, 0)),
        out_shape=jax.ShapeDtypeStruct((SEQ, 2 * D_FF), MXU_DTYPE), compiler_params=SUM_PARAMS,
    )(a, b, ds)


GATE_HG_BLK = 6656 // 512
GATE_ATT_BLK = 7680 // 512


def merge_fwd(z, bh, ba, name):
    def body(gh_ref, ga_ref, bh_ref, ba_ref, o_ref):
        o_ref[...] = (_sigmoid(gh_ref[...]) * bh_ref[...] + _sigmoid(ga_ref[...]) * ba_ref[...]).astype(o_ref.dtype)

    blk = pl.BlockSpec((ROW_TILE, 512), lambda i, j: (i, j))
    return pallas_call(
        body, name=name, grid=(SEQ // ROW_TILE, 2),
        in_specs=[pl.BlockSpec((ROW_TILE, 512), lambda i, j: (i, GATE_HG_BLK + j)),
                  pl.BlockSpec((ROW_TILE, 512), lambda i, j: (i, GATE_ATT_BLK + j)), blk, blk],
        out_specs=blk, out_shape=jax.ShapeDtypeStruct((SEQ, D_MODEL), MXU_DTYPE),
    )(z, z, bh, ba)


def merge_bwd(z, bh, ba, dm, name):
    def body(gh_ref, ga_ref, bh_ref, ba_ref, dm_ref, dbh_ref, dba_ref, dgh_ref, dga_ref):
        dmv = dm_ref[...]
        sh = _sigmoid(gh_ref[...])
        sa = _sigmoid(ga_ref[...])
        dbh_ref[...] = (dmv * sh).astype(dbh_ref.dtype)
        dba_ref[...] = (dmv * sa).astype(dba_ref.dtype)
        dgh_ref[...] = (dmv * bh_ref[...] * (sh * (1.0 - sh))).astype(dgh_ref.dtype)
        dga_ref[...] = (dmv * ba_ref[...] * (sa * (1.0 - sa))).astype(dga_ref.dtype)

    blk = pl.BlockSpec((ROW_TILE, 512), lambda i, j: (i, j))
    out = jax.ShapeDtypeStruct((SEQ, D_MODEL), MXU_DTYPE)
    return pallas_call(
        body, name=name, grid=(SEQ // ROW_TILE, 2),
        in_specs=[pl.BlockSpec((ROW_TILE, 512), lambda i, j: (i, GATE_HG_BLK + j)),
                  pl.BlockSpec((ROW_TILE, 512), lambda i, j: (i, GATE_ATT_BLK + j)), blk, blk, blk],
        out_specs=[blk, blk, blk, blk], out_shape=[out, out, out, out],
    )(z, z, bh, ba, dm)


N_CHUNKS = SEQ // HG_CHUNK
HG_STEP_CHUNKS = 4


def _hgrn_gates(q, fp, lb):
    C = HG_CHUNK
    sg = _sigmoid(fp)
    f = lb + (1.0 - lb) * sg
    lf = jnp.log(f)
    row = lax.broadcasted_iota(jnp.int32, (C, C), 0)
    col = lax.broadcasted_iota(jnp.int32, (C, C), 1)
    causal = row >= col
    G = _dot_f32(causal.astype(F32), lf)
    eG = jnp.exp(G)
    enG = jnp.exp(-G)
    qg = q * eG
    kg = (1.0 - f) * enG
    A = jnp.where(causal, _hdot(qg, kg, tb=True), 0.0)
    egl = jnp.exp(jnp.sum(lf, axis=0, keepdims=True))
    return sg, f, causal, eG, enG, qg, kg, A, egl


def hgrn_fwd(z, lb, gain, name):
    C, K = HG_CHUNK, HG_DIM

    def body(q_ref, f_ref, v_ref, og_ref, p_ref, g_ref, y_ref, o_ref, st_ref, state):
        @pl.when(pl.program_id(0) == 0)
        def _():
            state[...] = jnp.zeros_like(state)

        for cc in range(HG_STEP_CHUNKS):
            rows = pl.ds(cc * C, C)
            for h in range(HG_HEADS):
                hd = pl.ds(h * K, K)
                v = v_ref[rows, hd]
                _, _, _, _, _, qg, kg, A, egl = _hgrn_gates(q_ref[rows, hd], f_ref[rows, hd], p_ref[:, hd])
                st = state[h]
                st_ref[h, cc] = st
                o = _hdot(A, v) + _hdot(qg, st, tb=True)
                state[h] = st * egl + _hdot(v, kg * egl, ta=True)
                o_ref[rows, hd] = o
                rs = lax.rsqrt(jnp.mean(o * o, axis=-1, keepdims=True) + EPS)
                og = og_ref[rows, hd]
                y_ref[rows, hd] = (((o * rs) * g_ref[:, hd]) * (og * _sigmoid(og))).astype(y_ref.dtype)

    R = HG_STEP_CHUNKS * C

    def zcol(section):
        return pl.BlockSpec((R, HG_WIDTH), lambda c: (c, section))

    vec = pl.BlockSpec((1, HG_WIDTH), lambda c: (0, 0))
    blk = pl.BlockSpec((R, HG_WIDTH), lambda c: (c, 0))
    return pallas_call(
        body, name=name, grid=(N_CHUNKS // HG_STEP_CHUNKS,),
        in_specs=[zcol(0), zcol(1), zcol(2), zcol(3), vec, vec],
        out_specs=[blk, blk, pl.BlockSpec((HG_HEADS, HG_STEP_CHUNKS, K, K), lambda c: (0, c, 0, 0))],
        out_shape=[jax.ShapeDtypeStruct((SEQ, HG_WIDTH), MXU_DTYPE), jax.ShapeDtypeStruct((SEQ, HG_WIDTH), F32),
                   jax.ShapeDtypeStruct((HG_HEADS, N_CHUNKS, K, K), F32)],
        scratch_shapes=[pltpu.VMEM((HG_HEADS, K, K), F32)],
        compiler_params=pltpu.CompilerParams(dimension_semantics=("arbitrary",)),
    )(z, z, z, z, lb, gain)


def hgrn_bwd(z, lb, gain, o_raw, states, dy, name):
    C, K = HG_CHUNK, HG_DIM

    def body(q_ref, f_ref, v_ref, og_ref, p_ref, g_ref, o_ref, st_ref, dy_ref,
             dq_ref, dfp_ref, dv_ref, dog_ref, dlb_ref, dgain_ref, dstate):
        @pl.when(pl.program_id(0) == 0)
        def _():
            dstate[...] = jnp.zeros_like(dstate)
            dlb_ref[...] = jnp.zeros_like(dlb_ref)
            dgain_ref[...] = jnp.zeros_like(dgain_ref)

        last = lax.broadcasted_iota(jnp.int32, (C, K), 0) == C - 1
        row = lax.broadcasted_iota(jnp.int32, (C, C), 0)
        col = lax.broadcasted_iota(jnp.int32, (C, C), 1)
        anti_causal = (col >= row).astype(F32)
        for cc in reversed(range(HG_STEP_CHUNKS)):
            rows = pl.ds(cc * C, C)
            for h in range(HG_HEADS):
                hd = pl.ds(h * K, K)
                v = v_ref[rows, hd]
                lb = p_ref[:, hd]
                sg, f, causal, eG, enG, qg, kg, A, egl = _hgrn_gates(q_ref[rows, hd], f_ref[rows, hd], lb)
                kd = kg * egl
                st = st_ref[h, cc]
                dst = dstate[h]
                o = o_ref[rows, hd]
                og = og_ref[rows, hd]
                gain_v = g_ref[:, hd]
                dyv = dy_ref[rows, hd]
                rs = lax.rsqrt(jnp.mean(o * o, axis=-1, keepdims=True) + EPS)
                on = o * rs
                sgo = _sigmoid(og)
                silu = og * sgo
                dog_ref[rows, hd] = (dyv * (on * gain_v) * (sgo * (1.0 + og * (1.0 - sgo)))).astype(dog_ref.dtype)
                dgain_ref[:, hd] += jnp.sum(dyv * silu * on, axis=0, keepdims=True)
                don = dyv * gain_v * silu
                do = rs * (don - on * jnp.mean(don * on, axis=-1, keepdims=True))
                dA = jnp.where(causal, _hdot(do, v, tb=True), 0.0)
                dv_ref[rows, hd] = (_hdot(A, do, ta=True) + _hdot(kd, dst, tb=True)).astype(dv_ref.dtype)
                dqg = _hdot(dA, kg) + _hdot(do, st)
                dkg = _hdot(dA, qg, ta=True)
                dkd = _hdot(v, dst)
                dstate[h] = dst * egl + _hdot(do, qg, ta=True)
                dgl = jnp.sum(st * dst, axis=0, keepdims=True) * egl
                dq_ref[rows, hd] = (dqg * eG).astype(dq_ref.dtype)
                dk = dkg * enG + dkd * (enG * egl)
                dG = dqg * qg - dkg * kg - dkd * kd
                extra = jnp.sum(dkd * kd, axis=0, keepdims=True) + dgl
                dG = dG + jnp.where(last, extra, 0.0)
                dlf = _dot_f32(anti_causal, dG)
                df = dlf / f - dk
                dfp_ref[rows, hd] = (df * (1.0 - lb) * (sg * (1.0 - sg))).astype(dfp_ref.dtype)
                dlb_ref[:, hd] += jnp.sum(df * (1.0 - sg), axis=0, keepdims=True)

    R = HG_STEP_CHUNKS * C
    n_steps = N_CHUNKS // HG_STEP_CHUNKS

    def rc(c):
        return n_steps - 1 - c

    def zcol(section):
        return pl.BlockSpec((R, HG_WIDTH), lambda c: (rc(c), section))

    vec = pl.BlockSpec((1, HG_WIDTH), lambda c: (0, 0))
    blk = pl.BlockSpec((R, HG_WIDTH), lambda c: (rc(c), 0))
    out = jax.ShapeDtypeStruct((SEQ, HG_WIDTH), MXU_DTYPE)
    small = jax.ShapeDtypeStruct((1, HG_WIDTH), F32)
    return pallas_call(
        body, name=name, grid=(n_steps,),
        in_specs=[zcol(0), zcol(1), zcol(2), zcol(3), vec, vec, blk,
                  pl.BlockSpec((HG_HEADS, HG_STEP_CHUNKS, K, K), lambda c: (0, rc(c), 0, 0)), blk],
        out_specs=[blk, blk, blk, blk, vec, vec],
        out_shape=[out, out, out, out, small, small],
        scratch_shapes=[pltpu.VMEM((HG_HEADS, K, K), F32)],
        compiler_params=pltpu.CompilerParams(dimension_semantics=("arbitrary",)),
    )(z, z, z, z, lb, gain, o_raw, states, dy)


N_GROUPS = len(ATT_GROUPS)
HEAD_PAIRS = ATT_WIDTH // 128
ATT_COL0 = 4 * HG_WIDTH
UNROLLED_UNITS = 4


def _alibi_coef():
    n = N_GROUPS * ATT_HEADS
    slopes = np.exp2(-ALIBI_MAX * np.arange(1, n + 1, dtype=np.float32) / n).astype(np.float32)
    dil = np.repeat(np.array([d for _, d in ATT_GROUPS], np.float32), ATT_HEADS)
    return jnp.asarray(slopes * dil, F32)


def _for_each_unit(n, fn):
    if n <= UNROLLED_UNITS:
        for u in range(n):
            fn(u)
    else:
        def group(i, carry):
            for j in range(UNROLLED_UNITS):
                fn(i * UNROLLED_UNITS + j)
            return carry
        lax.fori_loop(0, n // UNROLLED_UNITS, group, 0)


def _att_geometry(g):
    B = ATT_BLOCK
    d = ATT_GROUPS[g][1]
    n_blocks = SEQ // (d * B)
    col0 = (ATT_COL0 + g * 3 * ATT_WIDTH) // 128

    def block_rows(b, r):
        return pl.ds(b * (B * d) + r, B, stride=d) if d > 1 else pl.ds(pl.multiple_of(b * B, B), B)

    def block_of(u):
        return (u, 0) if d == 1 else (u // d, u % d)

    return d, n_blocks, col0, block_rows, block_of


def _att_column(c):
    return pl.BlockSpec((SEQ, 128), lambda hp: (0, c + hp))


def _head_lanes(j):
    lane = lax.broadcasted_iota(jnp.int32, (ATT_BLOCK, 128), 1)
    return (lane >= 64 * j) & (lane < 64 * (j + 1))


def _stack_heads(x, sel0):
    return jnp.concatenate([jnp.where(sel0, x, 0.0), jnp.where(sel0, 0.0, x)], axis=0)


def _stack_values(x, sel0, lanes):
    swapped = pltpu.roll(x, 64, 1)
    stacked = jnp.concatenate([jnp.where(sel0, x, swapped), jnp.where(sel0, swapped, x)], axis=0)
    return stacked if lanes == 128 else jnp.concatenate([stacked] * (lanes // 128), axis=1)


def _pair_coef(coef_ref, g, hp):
    row = lax.broadcasted_iota(jnp.int32, (2 * ATT_BLOCK, 1), 0)
    first = g * ATT_HEADS + hp * 2
    return jnp.where(row < ATT_BLOCK, coef_ref[first], coef_ref[first + 1])


def _band(with_prev, first_key):
    B = ATT_BLOCK
    keys = 2 * B if with_prev else B
    qi = jnp.bitwise_and(lax.broadcasted_iota(jnp.int32, (2 * B, keys), 0), B - 1)
    kj = lax.broadcasted_iota(jnp.int32, (2 * B, keys), 1)
    delta = qi + (B if with_prev else 0) - kj
    valid = (delta >= 0) & (delta <= B)
    if with_prev:
        valid = valid & (kj >= first_key)
    return valid, delta.astype(F32)


def att_fwd(z, g, name):
    B = ATT_BLOCK
    d, n_blocks, col0, block_rows, block_of = _att_geometry(g)
    multi = n_blocks > 1

    def body(coef_ref, q_ref, k_ref, v_ref, o_ref, l_ref):
        cf2 = _pair_coef(coef_ref, g, pl.program_id(0))
        sel0 = _head_lanes(0)

        def one(u):
            b, r = block_of(u)
            rows = block_rows(b, r)
            valid, dist = _band(multi, jnp.where(b == 0, B, 0))
            q2 = _stack_heads(q_ref[rows, :], sel0)
            kk, vv = k_ref[rows, :], v_ref[rows, :]
            if multi:
                prev_rows = block_rows(jnp.maximum(b - 1, 0), r)
                kk = jnp.concatenate([k_ref[prev_rows, :], kk], axis=0)
                vv = jnp.concatenate([v_ref[prev_rows, :], vv], axis=0)
            sc = jnp.where(valid, _dot(q2, kk, tb=True) * 0.125 - cf2 * dist, NEG_INF)
            mx = jnp.max(sc, axis=-1, keepdims=True)
            e = jnp.exp(sc - mx)
            den = jnp.sum(e, axis=-1, keepdims=True)
            o2 = _dot(e * (1.0 / den), vv)
            lse2 = mx + jnp.log(den)
            o_ref[rows, :] = jnp.where(sel0, o2[:B], o2[B:])
            l_ref[rows, :] = jnp.where(sel0, lse2[:B], lse2[B:])

        _for_each_unit(d * n_blocks, one)

    out = jax.ShapeDtypeStruct((SEQ, ATT_WIDTH), F32)
    return pallas_call(
        body, name=name, grid=(HEAD_PAIRS,),
        in_specs=[pl.BlockSpec(memory_space=pltpu.SMEM), _att_column(col0), _att_column(col0 + 4), _att_column(col0 + 8)],
        out_specs=[_att_column(0), _att_column(0)], out_shape=[out, out],
        compiler_params=pltpu.CompilerParams(dimension_semantics=("parallel",)),
    )(_alibi_coef(), z, z, z)


def att_bwd(z, l, do, corr, g, name):
    B = ATT_BLOCK
    d, n_blocks, col0, block_rows, block_of = _att_geometry(g)
    multi = n_blocks > 1
    own = slice(B, 2 * B) if multi else slice(0, B)

    def body(coef_ref, q_ref, k_ref, v_ref, l_ref, do_ref, cr_ref, dq_ref, dk_ref, dv_ref, dq_sc, dk_sc, dv_sc):
        cf2 = _pair_coef(coef_ref, g, pl.program_id(0))
        sel0 = _head_lanes(0)

        def one(u):
            b, r = block_of(u)
            rows = block_rows(b, r)
            valid, dist = _band(multi, jnp.where(b == 0, B, 0))
            kk, vv = k_ref[rows, :], v_ref[rows, :]
            if multi:
                prev_rows = block_rows(jnp.maximum(b - 1, 0), r)
                kk = jnp.concatenate([k_ref[prev_rows, :], kk], axis=0)
                vv = jnp.concatenate([v_ref[prev_rows, :], vv], axis=0)
            q2, do2 = _stack_heads(q_ref[rows, :], sel0), _stack_heads(do_ref[rows, :], sel0)
            keys = kk.shape[0]
            lse2, cr2 = _stack_values(l_ref[rows, :], sel0, keys), _stack_values(cr_ref[rows, :], sel0, keys)
            p = jnp.exp(jnp.where(valid, _dot(q2, kk, tb=True) * 0.125 - cf2 * dist, NEG_INF) - lse2)
            ds = p * (_dot(do2, vv, tb=True) + cr2)
            dq2 = _dot(ds, kk)
            dkk = _dot(ds, q2, ta=True) * 0.125
            dvv = _dot(p, do2, ta=True)
            dq_sc[rows, :] = jnp.where(sel0, dq2[:B], dq2[B:]) * 0.125
            dk_sc[rows, :] = dkk[own]
            dv_sc[rows, :] = dvv[own]
            if multi:
                dk_sc[prev_rows, :] += dkk[:B]
                dv_sc[prev_rows, :] += dvv[:B]

        _for_each_unit(d * n_blocks, one)
        dq_ref[...] = dq_sc[...].astype(dq_ref.dtype)
        dk_ref[...] = dk_sc[...].astype(dk_ref.dtype)
        dv_ref[...] = dv_sc[...].astype(dv_ref.dtype)

    col = _att_column
    out = jax.ShapeDtypeStruct((SEQ, ATT_WIDTH), MXU_DTYPE)
    return pallas_call(
        body, name=name, grid=(HEAD_PAIRS,),
        in_specs=[pl.BlockSpec(memory_space=pltpu.SMEM), col(col0), col(col0 + 4), col(col0 + 8), col(0), col(0), col(0)],
        out_specs=[col(0)] * 3, out_shape=[out] * 3,
        scratch_shapes=[pltpu.VMEM((SEQ, 128), F32)] * 3,
        compiler_params=pltpu.CompilerParams(dimension_semantics=("parallel",), vmem_limit_bytes=MATMUL_VMEM_BYTES),
    )(_alibi_coef(), z, z, z, l, do, corr)


def _head_sum(x):
    i = lax.broadcasted_iota(jnp.int32, (128, 128), 0) // 64
    j = lax.broadcasted_iota(jnp.int32, (128, 128), 1) // 64
    return _dot_f32(x, (i == j).astype(F32), ones_on_right=True)


def _group_weights(l0, l1, l2):
    mx = jnp.maximum(jnp.maximum(l0, l1), l2)
    e0, e1, e2 = jnp.exp(l0 - mx), jnp.exp(l1 - mx), jnp.exp(l2 - mx)
    inv = 1.0 / (e0 + e1 + e2)
    return e0 * inv, e1 * inv, e2 * inv


def att_combine_fwd(o, l, name):
    def body(o0, o1, o2, l0, l1, l2, y_ref):
        w0, w1, w2 = _group_weights(l0[...], l1[...], l2[...])
        y_ref[...] = (o0[...] * w0 + o1[...] * w1 + o2[...] * w2).astype(y_ref.dtype)

    blk = pl.BlockSpec((ROW_TILE, ATT_WIDTH), lambda i: (i, 0))
    return pallas_call(
        body, name=name, grid=(SEQ // ROW_TILE,), in_specs=[blk] * 6, out_specs=blk,
        out_shape=jax.ShapeDtypeStruct((SEQ, ATT_WIDTH), MXU_DTYPE),
    )(*o, *l)


def att_combine_bwd(o, l, dy, name):
    def body(o0, o1, o2, l0, l1, l2, dy_ref, do0, do1, do2, cr0, cr1, cr2):
        w = _group_weights(l0[...], l1[...], l2[...])
        dyv = dy_ref[...]
        tot = _head_sum(dyv * (w[0] * o0[...] + w[1] * o1[...] + w[2] * o2[...]))
        for g, (do_ref, cr_ref) in enumerate(((do0, cr0), (do1, cr1), (do2, cr2))):
            do_ref[...] = dyv * w[g]
            cr_ref[...] = -w[g] * tot

    blk = pl.BlockSpec((ROW_TILE, 128), lambda i, j: (i, j))
    out = jax.ShapeDtypeStruct((SEQ, ATT_WIDTH), F32)
    res = pallas_call(
        body, name=name, grid=(SEQ // ROW_TILE, HEAD_PAIRS), in_specs=[blk] * 7, out_specs=[blk] * 6, out_shape=[out] * 6,
    )(*o, *l, dy)
    return res[:N_GROUPS], res[N_GROUPS:]


SUM_MAX_ROWS = 1024
SUM_ROW_ALIGN = 16
SUM_TILE_BYTES = 24 * 1024 * 1024
SUM_PARAMS = pltpu.CompilerParams(vmem_limit_bytes=MATMUL_VMEM_BYTES)


def _row_tile(rows, cols, operands):
    most = min(rows, SUM_MAX_ROWS) // SUM_ROW_ALIGN * SUM_ROW_ALIGN
    fit = [t for t in range(most, 0, -SUM_ROW_ALIGN) if rows % t == 0]
    return next((t for t in fit if 2 * 4 * operands * t * cols <= SUM_TILE_BYTES), fit[-1])


def _shard_shape(rows, cols, axis):
    return (rows // N_CHIPS, cols) if axis == 0 else (rows, cols // N_CHIPS)


def _half_shape(rows, cols, axis):
    return (rows, cols // 2) if axis == 0 else (rows // 2, cols)


def _piece_shape(rows, cols, axis):
    return (rows // N_CHIPS, cols // 2) if axis == 0 else (rows // 2, cols // N_CHIPS)


def place_own_block(shard, chip, rows, cols, axis, name):
    sr, sc = _shard_shape(rows, cols, axis)
    tr = _row_tile(sr, sc, 2)

    def body(chip_ref, s_ref, o_ref):
        o_ref[...] = s_ref[...].astype(o_ref.dtype)

    if axis == 0:
        out_map = lambda i, chip_ref: (chip_ref[0] * (sr // tr) + i, 0)
    else:
        out_map = lambda i, chip_ref: (i, chip_ref[0])
    return pallas_call(
        body, name=name, out_shape=jax.ShapeDtypeStruct((rows, cols), WEIGHT_COMM_DTYPE), compiler_params=SUM_PARAMS,
        grid_spec=pltpu.PrefetchScalarGridSpec(
            num_scalar_prefetch=1, grid=(sr // tr,), in_specs=[pl.BlockSpec((tr, sc), lambda i, chip_ref: (i, 0))],
            out_specs=pl.BlockSpec((tr, sc), out_map)),
    )(chip, shard)


def add_halves(g, theirs, core, rows, cols, axis, name):
    hr, hc = _half_shape(rows, cols, axis)
    tr = _row_tile(hr, hc, 3)

    def body(core_ref, g_ref, t_ref, o_ref):
        o_ref[...] = (g_ref[...].astype(F32) + t_ref[...].astype(F32)).astype(o_ref.dtype)

    if axis == 0:
        g_map = lambda i, core_ref: (i, core_ref[0])
    else:
        g_map = lambda i, core_ref: (core_ref[0] * (hr // tr) + i, 0)
    blk = pl.BlockSpec((tr, hc), lambda i, core_ref: (i, 0))
    return pallas_call(
        body, name=name, out_shape=jax.ShapeDtypeStruct((hr, hc), GRAD_COMM_DTYPE), compiler_params=SUM_PARAMS,
        grid_spec=pltpu.PrefetchScalarGridSpec(
            num_scalar_prefetch=1, grid=(hr // tr,), in_specs=[pl.BlockSpec((tr, hc), g_map), blk], out_specs=blk),
    )(core, g, theirs)


def add_pieces(half, got, chip, rows, cols, axis, name):
    hr, _ = _half_shape(rows, cols, axis)
    pr, pc = _piece_shape(rows, cols, axis)
    tr = _row_tile(pr, pc, 5)

    def body(chip_ref, h_ref, got_ref, o_ref):
        o_ref[...] = (h_ref[...].astype(F32) + got_ref[0].astype(F32) + got_ref[1].astype(F32) + got_ref[2].astype(F32))

    if axis == 0:
        h_map = lambda i, chip_ref: (chip_ref[0] * (pr // tr) + i, 0)
    else:
        h_map = lambda i, chip_ref: (i, chip_ref[0])
    return pallas_call(
        body, name=name, out_shape=jax.ShapeDtypeStruct((pr, pc), F32), compiler_params=SUM_PARAMS,
        grid_spec=pltpu.PrefetchScalarGridSpec(
            num_scalar_prefetch=1, grid=(pr // tr,),
            in_specs=[pl.BlockSpec((tr, pc), h_map), pl.BlockSpec((3, tr, pc), lambda i, chip_ref: (0, i, 0))],
            out_specs=pl.BlockSpec((tr, pc), lambda i, chip_ref: (i, 0))),
    )(chip, half, got)


def _adamw_math(w, g, m, v):
    nm = ADAM_B1 * m + (1.0 - ADAM_B1) * g
    nv = ADAM_B2 * v + (1.0 - ADAM_B2) * (g * g)
    m_hat = nm / (1.0 - ADAM_B1 ** ADAM_STEP)
    v_hat = nv / (1.0 - ADAM_B2 ** ADAM_STEP)
    return -ADAM_LR * (m_hat / (jnp.sqrt(v_hat) + ADAM_EPS) + ADAM_WD * w), nm, nv


def adamw_halves(w, mine, theirs, m, v, core, rows, cols, axis, name):
    sr, sc = _shard_shape(rows, cols, axis)
    pr, pc = _piece_shape(rows, cols, axis)
    tr = _row_tile(pr, pc, 9)
    nt = pr // tr

    def body(core_ref, w_ref, a_ref, b_ref, m_ref, v_ref, g_ref, d_ref, nm_ref, nv_ref):
        g = jnp.where(pl.program_id(0) == core_ref[0], a_ref[...], b_ref[...])
        g_ref[...] = g
        d_ref[...], nm_ref[...], nv_ref[...] = _adamw_math(w_ref[...], g, m_ref[...], v_ref[...])

    if axis == 0:
        full = pl.BlockSpec((tr, pc), lambda h, i, core_ref: (i, h))
    else:
        full = pl.BlockSpec((tr, pc), lambda h, i, core_ref: (h * nt + i, 0))
    part = pl.BlockSpec((tr, pc), lambda h, i, core_ref: (i, 0))
    out = jax.ShapeDtypeStruct((sr, sc), F32)
    return pallas_call(
        body, name=name, out_shape=[out, out, out, out], compiler_params=SUM_PARAMS,
        grid_spec=pltpu.PrefetchScalarGridSpec(
            num_scalar_prefetch=1, grid=(2, nt), in_specs=[full, part, part, full, full], out_specs=[full] * 4),
    )(core, w, mine, theirs, m, v)


BIG = (
    ("ffn1_w_gate_up", D_MODEL, 2 * D_FF, 1),
    ("ffn1_w_down", D_FF, D_MODEL, 0),
    ("w_in", D_MODEL, IN_COLS, 1),
    ("w_branch_hg", HG_WIDTH, D_MODEL, 1),
    ("w_branch_att", ATT_WIDTH, D_MODEL, 1),
    ("w_out", D_MODEL, D_MODEL, 0),
    ("ffn2_w_gate_up", D_MODEL, 2 * D_FF, 1),
    ("ffn2_w_down", D_FF, D_MODEL, 0),
)
N_BIG = len(BIG)
ANY = pl.BlockSpec(memory_space=pl.ANY)


def _place():
    return lax.axis_index("x"), lax.axis_index("y"), lax.axis_index("c")


def _other_chips(x, y):
    return ((1 - x, y), (x, 1 - y), (1 - x, 1 - y))


MAX_COPY_CHUNKS = 16
CHUNK_ROW_ALIGN = 16


def _row_chunks(view):
    rows = view.shape[0]
    n = next(n for n in range(MAX_COPY_CHUNKS, 0, -1) if rows % (CHUNK_ROW_ALIGN * n) == 0 or n == 1)
    step = rows // n
    return [pl.ds(i * step, step) for i in range(n)]


def _remote(src, dst, send_sem, recv_sem, device):
    return pltpu.make_async_remote_copy(src_ref=src, dst_ref=dst, send_sem=send_sem, recv_sem=recv_sem,
                                        device_id=device, device_id_type=MESH)


def _start_remote(src, dst, send_sem, recv_sem, device):
    for rows in _row_chunks(src):
        _remote(src.at[rows, :], dst.at[rows, :], send_sem, recv_sem, device).start()
    return _remote(src, dst, send_sem, recv_sem, device)


HBM = pl.BlockSpec(memory_space=pltpu.HBM)
SEM = pl.BlockSpec(memory_space=pltpu.SEMAPHORE)
SPLIT_COPY_EFFECT = pltpu.SideEffectType.DATAFLOW_SIDE_EFFECTING
GROUPS = {"ffn1": (0, 1), "mix": (2, 3, 4, 5), "ffn2": (6, 7)}


class _SemList:
    def __init__(self, refs):
        self.refs = refs
        self.at = self

    def __getitem__(self, index):
        w, k = index
        return self.refs[3 * w + k]


def _gather_piece(ref, rows, cols, axis, chip, c):
    sr, sc = _shard_shape(rows, cols, axis)
    j = 2 * chip[0] + chip[1]
    if axis == 0:
        return ref.at[pl.ds(j * sr + c * (sr // 2), sr // 2), :]
    return ref.at[pl.ds(c * (sr // 2), sr // 2), pl.ds(pl.multiple_of(j * sc, 128), sc)]


def _start_gather_sends(bufs, ws, send_sems, recv_sems):
    x, y, c = _place()
    for w, (_, r, cc, ax) in enumerate(ws):
        mine = _gather_piece(bufs[w], r, cc, ax, (x, y), c)
        for k, chip in enumerate(_other_chips(x, y)):
            _start_remote(mine, mine, send_sems.at[w, k], recv_sems.at[w, k], (*chip, c))


def _wait_gather_sends(bufs, ws, send_sems, recv_sems):
    x, y, c = _place()
    for w, (_, r, cc, ax) in enumerate(ws):
        for k, chip in enumerate(_other_chips(x, y)):
            got = _gather_piece(bufs[w], r, cc, ax, chip, c)
            _remote(got, got, send_sems.at[w, k], recv_sems.at[w, k], (x, y, c)).wait_recv()
    for w, (_, r, cc, ax) in enumerate(ws):
        mine = _gather_piece(bufs[w], r, cc, ax, (x, y), c)
        for k in range(3):
            _remote(mine, mine, send_sems.at[w, k], recv_sems.at[w, k], (x, y, c)).wait_send()


def _forward_halves(bufs, ws, send_sems, recv_sems):
    x, y, c = _place()
    passed = []
    for w, (_, r, cc, ax) in enumerate(ws):
        for k, chip in enumerate(_other_chips(x, y)):
            got = _gather_piece(bufs[w], r, cc, ax, chip, c)
            passed.append(_start_remote(got, got, send_sems.at[w, k], recv_sems.at[w, k], (x, y, 1 - c)))
    for w, (_, r, cc, ax) in enumerate(ws):
        for k, chip in enumerate(_other_chips(x, y)):
            got = _gather_piece(bufs[w], r, cc, ax, chip, 1 - c)
            _remote(got, got, send_sems.at[w, k], recv_sems.at[w, k], (x, y, c)).wait_recv()
    for cp in passed:
        cp.wait_send()


def gather_start(placed, after, group):
    ws = [BIG[i] for i in GROUPS[group]]
    n = len(ws)

    def body(*refs):
        bufs = refs[:n]
        send_sems, recv_sems = _SemList(refs[n + 1:4 * n + 1]), _SemList(refs[4 * n + 1:7 * n + 1])
        token = refs[-1]
        _start_gather_sends(bufs, ws, send_sems, recv_sems)
        token[...] = jnp.zeros_like(token)

    out = pallas_call(
        body, name=f"gather_start_{group}", in_specs=[HBM] * n + [ANY],
        out_specs=[SEM] * (6 * n) + [HBM] * n + [pl.BlockSpec(memory_space=pltpu.VMEM)],
        out_shape=[pltpu.SemaphoreType.DMA(())] * (6 * n)
        + [pltpu.HBM((r, cc), WEIGHT_COMM_DTYPE) for _, r, cc, _ in ws] + [jax.ShapeDtypeStruct((8, 128), F32)],
        input_output_aliases={w: 6 * n + w for w in range(n)},
        compiler_params=pltpu.CompilerParams(has_side_effects=SPLIT_COPY_EFFECT),
    )(*[_in_hbm(p) for p in placed], after)
    return out[:3 * n], out[3 * n:6 * n], out[6 * n:7 * n], out[-1]


def gather_wait(bufs, send_sems, recv_sems, after, group):
    ws = [BIG[i] for i in GROUPS[group]]
    n = len(ws)

    def body(*refs):
        _wait_gather_sends(refs[:n], ws, _SemList(refs[n:n + 3 * n]), _SemList(refs[n + 3 * n:n + 6 * n]))

    return pallas_call(
        body, name=f"gather_wait_{group}", in_specs=[HBM] * n + [SEM] * (6 * n) + [ANY] * len(after), out_specs=[HBM] * n,
        out_shape=[pltpu.HBM((r, cc), WEIGHT_COMM_DTYPE) for _, r, cc, _ in ws],
        input_output_aliases={w: w for w in range(n)},
        compiler_params=pltpu.CompilerParams(has_side_effects=SPLIT_COPY_EFFECT),
    )(*bufs, *send_sems, *recv_sems, *after)


def gather_forward(bufs, group):
    ws = [BIG[i] for i in GROUPS[group]]
    n = len(ws)

    def body(*refs):
        _forward_halves(refs[n:2 * n], ws, refs[2 * n], refs[2 * n + 1])

    return pallas_call(
        body, name=f"gather_forward_{group}", in_specs=[ANY] * n, out_specs=[ANY] * n,
        out_shape=[jax.ShapeDtypeStruct((r, cc), WEIGHT_COMM_DTYPE) for _, r, cc, _ in ws],
        input_output_aliases={w: w for w in range(n)},
        scratch_shapes=[pltpu.SemaphoreType.DMA((n, 3))] * 2,
    )(*bufs)


def _half(ref, rows, cols, axis, c):
    if axis == 0:
        return ref.at[:, pl.ds(pl.multiple_of(c * (cols // 2), 128), cols // 2)]
    return ref.at[pl.ds(c * (rows // 2), rows // 2), :]


def _piece_of_half(ref, rows, cols, axis, chip):
    j = 2 * chip[0] + chip[1]
    pr, pc = _piece_shape(rows, cols, axis)
    if axis == 0:
        return ref.at[pl.ds(j * pr, pr), :]
    return ref.at[:, pl.ds(pl.multiple_of(j * pc, 128), pc)]


def sibling_exchange_start(srcs, view, landing_shapes, dtype, name):
    n = len(srcs)

    def body(*refs):
        ins, land, sems = refs[:n], refs[n:2 * n], refs[2 * n:4 * n]
        x, y, c = _place()
        for w in range(n):
            _start_remote(view(ins[w], w, c), land[w], sems[w], sems[n + w], (x, y, 1 - c))
        refs[-1][...] = jnp.zeros_like(refs[-1])

    landing = [lax.empty(shape, dtype) for shape in landing_shapes]
    out = pallas_call(
        body, name=name, in_specs=[HBM] * (2 * n),
        out_specs=[SEM] * (2 * n) + [HBM] * (2 * n) + [pl.BlockSpec(memory_space=pltpu.VMEM)],
        out_shape=[pltpu.SemaphoreType.DMA(())] * (2 * n) + [pltpu.HBM(a.shape, a.dtype) for a in srcs]
        + [pltpu.HBM(shape, dtype) for shape in landing_shapes] + [jax.ShapeDtypeStruct((8, 128), F32)],
        input_output_aliases={i: 2 * n + i for i in range(2 * n)},
        compiler_params=pltpu.CompilerParams(has_side_effects=SPLIT_COPY_EFFECT),
    )(*[_in_hbm(a) for a in srcs], *[_in_hbm(b) for b in landing])
    return out[:n], out[n:2 * n], out[2 * n:3 * n], out[3 * n:4 * n], out[-1]


def sibling_exchange_wait(srcs, landing, send_sems, recv_sems, view, after, name):
    n = len(srcs)

    def body(*refs):
        ins, land, sems = refs[:n], refs[n:2 * n], refs[2 * n:4 * n]
        x, y, c = _place()
        for w in range(n):
            cp = _remote(view(ins[w], w, c), land[w], sems[w], sems[n + w], (x, y, c))
            cp.wait_send()
            cp.wait_recv()

    out = pallas_call(
        body, name=name, in_specs=[HBM] * (2 * n) + [SEM] * (2 * n) + [ANY] * len(after), out_specs=[HBM] * (2 * n),
        out_shape=[pltpu.HBM(a.shape, a.dtype) for a in srcs] + [pltpu.HBM(b.shape, b.dtype) for b in landing],
        input_output_aliases={i: i for i in range(2 * n)},
        compiler_params=pltpu.CompilerParams(has_side_effects=SPLIT_COPY_EFFECT),
    )(*srcs, *landing, *send_sems, *recv_sems, *after)
    return out[:n], out[n:]


def _scatter_copies(halves, got, ws, send_sems, recv_sems, start):
    x, y, c = _place()
    copies = []
    for w, (_, r, cc, ax) in enumerate(ws):
        for k, chip in enumerate(_other_chips(x, y)):
            args = (_piece_of_half(halves[w], r, cc, ax, chip), got[w].at[k], send_sems.at[w, k], recv_sems.at[w, k], (*chip, c))
            copies.append(_start_remote(*args) if start else _remote(*args))
    return copies


def scatter_start(halves, group):
    ws = [BIG[i] for i in GROUPS[group]]
    n = len(ws)

    def body(*refs):
        sems = refs[2 * n:8 * n]
        _scatter_copies(refs[:n], refs[n:2 * n], ws, _SemList(sems[:3 * n]), _SemList(sems[3 * n:]), start=True)
        refs[-1][...] = jnp.zeros_like(refs[-1])

    landing = [lax.empty((3,) + _piece_shape(r, cc, ax), GRAD_COMM_DTYPE) for _, r, cc, ax in ws]
    out = pallas_call(
        body, name=f"scatter_start_{group}", in_specs=[HBM] * (2 * n),
        out_specs=[SEM] * (6 * n) + [HBM] * (2 * n) + [pl.BlockSpec(memory_space=pltpu.VMEM)],
        out_shape=[pltpu.SemaphoreType.DMA(())] * (6 * n)
        + [pltpu.HBM(_half_shape(r, cc, ax), GRAD_COMM_DTYPE) for _, r, cc, ax in ws]
        + [pltpu.HBM((3,) + _piece_shape(r, cc, ax), GRAD_COMM_DTYPE) for _, r, cc, ax in ws]
        + [jax.ShapeDtypeStruct((8, 128), F32)],
        input_output_aliases={i: 6 * n + i for i in range(2 * n)},
        compiler_params=pltpu.CompilerParams(has_side_effects=SPLIT_COPY_EFFECT),
    )(*[_in_hbm(h) for h in halves], *[_in_hbm(b) for b in landing])
    return out[:3 * n], out[3 * n:6 * n], out[6 * n:7 * n], out[7 * n:8 * n], out[-1]


def scatter_wait(halves, got, send_sems, recv_sems, after, group):
    ws = [BIG[i] for i in GROUPS[group]]
    n = len(ws)

    def body(*refs):
        sems = refs[2 * n:8 * n]
        for cp in _scatter_copies(refs[:n], refs[n:2 * n], ws, _SemList(sems[:3 * n]), _SemList(sems[3 * n:]), start=False):
            cp.wait_send()
            cp.wait_recv()

    out = pallas_call(
        body, name=f"scatter_wait_{group}", in_specs=[HBM] * (2 * n) + [SEM] * (6 * n) + [ANY] * len(after),
        out_specs=[HBM] * (2 * n),
        out_shape=[pltpu.HBM(_half_shape(r, cc, ax), GRAD_COMM_DTYPE) for _, r, cc, ax in ws]
        + [pltpu.HBM((3,) + _piece_shape(r, cc, ax), GRAD_COMM_DTYPE) for _, r, cc, ax in ws],
        input_output_aliases={i: i for i in range(2 * n)},
        compiler_params=pltpu.CompilerParams(has_side_effects=SPLIT_COPY_EFFECT),
    )(*halves, *got, *send_sems, *recv_sems, *after)
    return out[:n], out[n:]


N_DEV = 8
SMALL = ("ffn1_norm", "mix_norm", "hg_lower_bounds", "hg_out_norm", "ffn2_norm", "final_norm")
SMALL_STAGE_ROWS = 8


def small_step(loss, grads, w, m, v, behind):
    n = len(SMALL)
    shapes = [g.shape for g in grads]
    first_row = [sum(s[0] for s in shapes[:i]) for i in range(n + 1)]
    assert first_row[n] < SMALL_STAGE_ROWS
    loss_row = (pl.ds(first_row[n], 1), pl.ds(0, loss.shape[1]))

    def body(*refs):
        loss_ref, g_refs, w_refs, m_refs, v_refs = refs[0], refs[1:1 + n], refs[1 + n:1 + 2 * n], refs[1 + 2 * n:1 + 3 * n], refs[1 + 3 * n:1 + 4 * n]
        outs = refs[2 + 4 * n:3 + 8 * n]
        loss_out, dg_refs, d_refs, nm_refs, nv_refs = outs[0], outs[1:1 + n], outs[1 + n:1 + 2 * n], outs[1 + 2 * n:1 + 3 * n], outs[1 + 3 * n:]
        stage, gathered, send_sems, recv_sems = refs[3 + 8 * n:]
        x, y, c = _place()
        me = 4 * x + 2 * y + c

        def slot(i, shape):
            return pl.ds(first_row[i], shape[0]), pl.ds(0, shape[1])

        stage[...] = jnp.zeros_like(stage)
        for i, g_ref in enumerate(g_refs):
            stage[slot(i, shapes[i])] = g_ref[...]
        stage[loss_row] = loss_ref[pl.ds(0, 1), :]
        gathered[me] = stage[...]
        copies = []
        for k in range(1, N_DEV):
            peer = (x ^ (k >> 2), y ^ ((k >> 1) & 1), c ^ (k & 1))
            cp = pltpu.make_async_remote_copy(
                src_ref=stage, dst_ref=gathered.at[me], send_sem=send_sems.at[k - 1], recv_sem=recv_sems.at[k - 1],
                device_id=peer, device_id_type=MESH)
            cp.start()
            copies.append(cp)
        for cp in copies:
            cp.wait()
        acc = gathered[0]
        for k in range(1, N_DEV):
            acc = acc + gathered[k]
        stage[...] = acc
        loss_out[...] = jnp.broadcast_to(stage[loss_row], loss_out.shape)
        for i in range(n):
            g = stage[slot(i, shapes[i])]
            dg_refs[i][...] = g
            d_refs[i][...], nm_refs[i][...], nv_refs[i][...] = _adamw_math(w_refs[i][...], g, m_refs[i][...], v_refs[i][...])

    vm = pl.BlockSpec(memory_space=pltpu.VMEM)
    per_param = [jax.ShapeDtypeStruct(s, F32) for s in shapes]
    out = pallas_call(
        body, name="small_step", in_specs=[vm] * (1 + 4 * n) + [ANY], out_specs=[vm] * (1 + 4 * n),
        out_shape=[jax.ShapeDtypeStruct(loss.shape, F32)] + per_param * 4,
        scratch_shapes=[pltpu.VMEM((SMALL_STAGE_ROWS, D_MODEL), F32),
                        pltpu.VMEM((N_DEV, SMALL_STAGE_ROWS, D_MODEL), F32),
                        pltpu.SemaphoreType.DMA((N_DEV - 1,)), pltpu.SemaphoreType.DMA((N_DEV - 1,))],
    )(loss, *grads, *w, *m, *v, behind)
    return out[0], out[1:1 + n], out[1 + n:1 + 2 * n], out[1 + 2 * n:1 + 3 * n], out[1 + 3 * n:]


def _swiglu_block_fwd(h, n, w_gu, w_down, tag, behind=()):
    a, b, s = gate_up_swiglu(n, w_gu, f"{tag}_gate_up", behind=behind)
    h_out = matmul(s, w_down, res=h, scale=0.5, name=f"{tag}_down")
    return h_out, (n, a, b, s)


def _swiglu_block_bwd(h, norm_g, w_gu, w_down, saved, dh_out, df, tag, exchange, behind=()):
    n, a, b, s = saved
    d_down = matmul(s, df, ta=True, scale=0.5, out_dtype=GRAD_COMM_DTYPE, name=f"{tag}_d_w_down")
    ds = matmul(df, w_down, tb=True, scale=0.5, out_dtype=ACT_DTYPE, behind=behind, name=f"{tag}_d_s")
    dgu = swiglu_bwd(a, b, ds, f"{tag}_swiglu_bwd")
    d_gu = matmul(n, dgu, ta=True, out_dtype=GRAD_COMM_DTYPE, name=f"{tag}_d_w_gate_up")
    tokens = exchange.gradients_ready(tag, {f"{tag}_w_gate_up": d_gu, f"{tag}_w_down": d_down})
    dn = matmul(dgu, w_gu, tb=True, behind=tokens, name=f"{tag}_d_n")
    dh, dh_m, dg = rmsnorm_bwd(h, norm_g, dn, dh_out, f"{tag}_norm_bwd")
    return dh, dh_m, dg


def local_step(x, target, small, exchange):
    big = {}
    n1 = rmsnorm_fwd(x, small["ffn1_norm"], "ffn1_norm", behind=exchange.started)
    token, big_ffn1 = exchange.weights("ffn1", n1)
    big.update(big_ffn1)
    h1, saved1 = _swiglu_block_fwd(x, n1, big["ffn1_w_gate_up"], big["ffn1_w_down"], "ffn1", token)
    u = rmsnorm_fwd(h1, small["mix_norm"], "mix_norm")
    token, big_mix = exchange.weights("mix", u)
    big.update(big_mix)
    z = matmul(u, big["w_in"], behind=token, name="w_in")
    p = small["hg_lower_bounds"]
    lb = 1.0 / (1.0 + jnp.exp(p[1:2] - p[0:1]))
    y_hg, o_raw, states = hgrn_fwd(z, lb, small["hg_out_norm"], "hgrn_fwd")
    o_att, l_att = zip(*[att_fwd(z, g, f"att_fwd_{g}") for g in range(N_GROUPS)])
    y_att = att_combine_fwd(o_att, l_att, "att_combine")
    bh = matmul(y_hg, big["w_branch_hg"], name="branch_hg")
    ba = matmul(y_att, big["w_branch_att"], name="branch_att")
    merged = merge_fwd(z, bh, ba, "merge")
    h2 = matmul(merged, big["w_out"], res=h1, name="w_out")
    n2 = rmsnorm_fwd(h2, small["ffn2_norm"], "ffn2_norm")
    token, big_ffn2 = exchange.weights("ffn2", n2)
    big.update(big_ffn2)
    h3, saved2 = _swiglu_block_fwd(h2, n2, big["ffn2_w_gate_up"], big["ffn2_w_down"], "ffn2", token)
    dh3, dh3_m, d_final, loss = final_norm_loss(h3, small["final_norm"], target, "final_norm_loss")

    gs, gb = {"final_norm": d_final}, {}
    dh2, dh2_m, gs["ffn2_norm"] = _swiglu_block_bwd(
        h2, small["ffn2_norm"], big["ffn2_w_gate_up"], big["ffn2_w_down"], saved2, dh3, dh3_m, "ffn2", exchange)
    token = exchange.backward_done("ffn2", dh2)
    gb["w_out"] = matmul(merged, dh2_m, ta=True, out_dtype=GRAD_COMM_DTYPE, name="d_w_out")
    dmerged = matmul(dh2_m, big["w_out"], tb=True, behind=token, name="d_merged")
    dbh, dba, dgh, dga = merge_bwd(z, bh, ba, dmerged, "merge_bwd")
    gb["w_branch_hg"] = matmul(y_hg, dbh, ta=True, out_dtype=GRAD_COMM_DTYPE, name="d_w_branch_hg")
    gb["w_branch_att"] = matmul(y_att, dba, ta=True, out_dtype=GRAD_COMM_DTYPE, name="d_w_branch_att")
    dy_hg = matmul(dbh, big["w_branch_hg"], tb=True, name="d_y_hg")
    dy_att = matmul(dba, big["w_branch_att"], tb=True, name="d_y_att")
    dq, dfp, di, dog, d_lb, gs["hg_out_norm"] = hgrn_bwd(z, lb, small["hg_out_norm"], o_raw, states, dy_hg, "hgrn_bwd")
    do_att, corr = att_combine_bwd(o_att, l_att, dy_att, "att_combine_bwd")
    d_att = [part for g in range(N_GROUPS) for part in att_bwd(z, l_att[g], do_att[g], corr[g], g, f"att_bwd_{g}")]
    dz = jnp.concatenate([dq, dfp, di, dog, *d_att, dgh, dga], axis=1)
    gb["w_in"] = matmul(u, dz, ta=True, out_dtype=GRAD_COMM_DTYPE, name="d_w_in")
    token = exchange.gradients_ready("mix", gb)
    du = matmul(dz, big["w_in"], tb=True, behind=token, name="d_u")
    dh1, dh1_m, gs["mix_norm"] = rmsnorm_bwd(h1, small["mix_norm"], du, dh2, "mix_norm_bwd")
    token = exchange.backward_done("mix", dh1)
    dp0 = d_lb * lb * (1.0 - lb)
    gs["hg_lower_bounds"] = jnp.concatenate([dp0, -dp0], axis=0)
    dx, _, gs["ffn1_norm"] = _swiglu_block_bwd(
        x, small["ffn1_norm"], big["ffn1_w_gate_up"], big["ffn1_w_down"], saved1, dh1, dh1_m, "ffn1", exchange, token)
    exchange.backward_done("ffn1", dx)
    return loss, dx, gs


WEIGHTS = ("ffn1_norm", "ffn1_w_gate_up", "ffn1_w_down", "mix_norm", "w_in", "hg_lower_bounds", "hg_out_norm",
           "w_branch_hg", "w_branch_att", "w_out", "ffn2_norm", "ffn2_w_gate_up", "ffn2_w_down", "final_norm")


class WeightExchange:
    ORDER = ("ffn1", "mix", "ffn2")

    def __init__(self, shards, core, chip):
        self.core, self.chip = core, chip
        self.halving = None
        self.scattering = None
        self.reducing = {}
        first = self.ORDER[0]
        self.placed = {BIG[i][0]: place_own_block(shards[BIG[i][0]], chip, *BIG[i][1:], f"place_{BIG[i][0]}")
                       for i in GROUPS[first]}
        self._start_gather(first, chip)
        self.started = [self.token]
        chip_behind = chip + self.token[0, :1].astype(jnp.int32)
        for group in self.ORDER[1:]:
            for i in GROUPS[group]:
                n, r, cc, ax = BIG[i]
                self.placed[n] = place_own_block(shards[n], chip_behind, r, cc, ax, f"place_{n}")
        self.placed_behind = [self.placed[n] for group in self.ORDER[1:] for n in self._names(group)]

    def _names(self, group):
        return [BIG[i][0] for i in GROUPS[group]]

    def _start_gather(self, group, after):
        send_sems, recv_sems, bufs, self.token = gather_start([self.placed[n] for n in self._names(group)], after, group)
        self.gathering = (group, send_sems, recv_sems, bufs)

    def weights(self, group, h):
        pending, send_sems, recv_sems, bufs = self.gathering
        assert pending == group
        after = [h] + (self.placed_behind if group == self.ORDER[0] else [])
        whole = gather_forward(gather_wait(bufs, send_sems, recv_sems, after, group), group)
        later = self.ORDER.index(group) + 1
        behind = []
        if later < len(self.ORDER):
            self._start_gather(self.ORDER[later], whole[0])
            behind = [self.token]
        return behind, dict(zip(self._names(group), whole))

    @staticmethod
    def _half_to_sibling(ws):
        return lambda ref, w, c: _half(ref, *ws[w][1:], 1 - c)

    def gradients_ready(self, group, grads):
        ws = [BIG[i] for i in GROUPS[group]]
        send_sems, recv_sems, own, theirs, token = sibling_exchange_start(
            [grads[n] for n, *_ in ws], self._half_to_sibling(ws), [_half_shape(r, cc, ax) for _, r, cc, ax in ws],
            GRAD_COMM_DTYPE, f"halves_start_{group}")
        self.halving = (group, send_sems, recv_sems, own, theirs)
        return [token]

    def backward_done(self, group, dh):
        pending, send_sems, recv_sems, own, theirs = self.halving
        assert pending == group
        ws = [BIG[i] for i in GROUPS[group]]
        own, theirs = sibling_exchange_wait(own, theirs, send_sems, recv_sems, self._half_to_sibling(ws), [dh],
                                            f"halves_wait_{group}")
        halves = [add_halves(g, t, self.core, r, cc, ax, f"add_halves_{n}") for (n, r, cc, ax), g, t in zip(ws, own, theirs)]
        previous = self.scattering
        send_sems, recv_sems, halves, got, self.token = scatter_start(halves, group)
        self.scattering = (group, send_sems, recv_sems, halves, got)
        behind = [self._finish_scatter(previous, [self.token])] if previous is not None else []
        return behind + [self.token]

    def _finish_scatter(self, scattering, after):
        group, send_sems, recv_sems, halves, got = scattering
        halves, got = scatter_wait(halves, got, send_sems, recv_sems, after, group)
        ws = [BIG[i] for i in GROUPS[group]]
        mine = [add_pieces(h, g, self.chip, r, cc, ax, f"add_pieces_{n}") for (n, r, cc, ax), h, g in zip(ws, halves, got)]
        send_sems, recv_sems, mine, theirs, token = sibling_exchange_start(
            mine, lambda ref, w, c: ref, [_piece_shape(r, cc, ax) for _, r, cc, ax in ws], F32, f"reduced_start_{group}")
        self.reducing[group] = (send_sems, recv_sems, mine, theirs)
        return token

    def finish(self, after):
        return self._finish_scatter(self.scattering, after)

    def reduced_halves(self, group, after):
        send_sems, recv_sems, mine, theirs = self.reducing.pop(group)
        mine, theirs = sibling_exchange_wait(mine, theirs, send_sems, recv_sems, lambda ref, w, c: ref, after,
                                             f"reduced_wait_{group}")
        return {BIG[i][0]: (a, b) for i, a, b in zip(GROUPS[group], mine, theirs)}


def kernel(x, ffn1_norm, ffn1_w_gate_up, ffn1_w_down, mix_norm, w_in, hg_lower_bounds, hg_out_norm, w_branch_hg, w_branch_att, w_out, ffn2_norm, ffn2_w_gate_up, ffn2_w_down, final_norm, loss_target, m_ffn1_norm, m_ffn1_w_gate_up, m_ffn1_w_down, m_mix_norm, m_w_in, m_hg_lower_bounds, m_hg_out_norm, m_w_branch_hg, m_w_branch_att, m_w_out, m_ffn2_norm, m_ffn2_w_gate_up, m_ffn2_w_down, m_final_norm, v_ffn1_norm, v_ffn1_w_gate_up, v_ffn1_w_down, v_mix_norm, v_w_in, v_hg_lower_bounds, v_hg_out_norm, v_w_branch_hg, v_w_branch_att, v_w_out, v_ffn2_norm, v_ffn2_w_gate_up, v_ffn2_w_down, v_final_norm):
    w = dict(ffn1_norm=ffn1_norm, ffn1_w_gate_up=ffn1_w_gate_up, ffn1_w_down=ffn1_w_down, mix_norm=mix_norm, w_in=w_in,
             hg_lower_bounds=hg_lower_bounds, hg_out_norm=hg_out_norm, w_branch_hg=w_branch_hg, w_branch_att=w_branch_att,
             w_out=w_out, ffn2_norm=ffn2_norm, ffn2_w_gate_up=ffn2_w_gate_up, ffn2_w_down=ffn2_w_down, final_norm=final_norm)
    m = dict(ffn1_norm=m_ffn1_norm, ffn1_w_gate_up=m_ffn1_w_gate_up, ffn1_w_down=m_ffn1_w_down, mix_norm=m_mix_norm,
             w_in=m_w_in, hg_lower_bounds=m_hg_lower_bounds, hg_out_norm=m_hg_out_norm, w_branch_hg=m_w_branch_hg,
             w_branch_att=m_w_branch_att, w_out=m_w_out, ffn2_norm=m_ffn2_norm, ffn2_w_gate_up=m_ffn2_w_gate_up,
             ffn2_w_down=m_ffn2_w_down, final_norm=m_final_norm)
    v = dict(ffn1_norm=v_ffn1_norm, ffn1_w_gate_up=v_ffn1_w_gate_up, ffn1_w_down=v_ffn1_w_down, mix_norm=v_mix_norm,
             w_in=v_w_in, hg_lower_bounds=v_hg_lower_bounds, hg_out_norm=v_hg_out_norm, w_branch_hg=v_w_branch_hg,
             w_branch_att=v_w_branch_att, w_out=v_w_out, ffn2_norm=v_ffn2_norm, ffn2_w_gate_up=v_ffn2_w_gate_up,
             ffn2_w_down=v_ffn2_w_down, final_norm=v_final_norm)

    core = lax.axis_index("c").astype(jnp.int32).reshape(1)
    chip = (2 * lax.axis_index("x") + lax.axis_index("y")).astype(jnp.int32).reshape(1)
    exchange = WeightExchange({n: w[n][0] for n, *_ in BIG}, core, chip)
    small = {n: w[n] for n in SMALL}
    small["final_norm"] = final_norm.reshape(1, D_MODEL)

    loss, dx, gs = local_step(x[0], loss_target[0], small, exchange)

    grads, delta, new_m, new_v = {}, {}, {}, {}

    def update(group, core, after):
        reduced = exchange.reduced_halves(group, after)
        for i in GROUPS[group]:
            n, r, cc, ax = BIG[i]
            a, b = reduced[n]
            g, d, nm, nv = adamw_halves(w[n][0], a, b, m[n][0], v[n][0], core, r, cc, ax, f"adamw_{n}")
            grads[n], delta[n], new_m[n], new_v[n] = g[None], d[None], nm[None], nv[None]

    core_behind = core + exchange.token[0, :1].astype(jnp.int32)
    update("ffn2", core_behind, [exchange.token])
    update("mix", core_behind, [delta["ffn2_w_down"]])
    token = exchange.finish(after=[delta[BIG[i][0]] for group in ("ffn2", "mix") for i in GROUPS[group]])
    two_d = lambda a: a.reshape(1, D_MODEL) if a.ndim == 1 else a
    loss_sum, *small_out = small_step(loss, [gs[n] for n in SMALL], *[[two_d(p[n]) for n in SMALL] for p in (w, m, v)],
                                      behind=token)
    for result, parts in zip((grads, delta, new_m, new_v), small_out):
        result.update({n: a.reshape(w[n].shape) for n, a in zip(SMALL, parts)})
    update("ffn1", core, [loss_sum])

    return (loss_sum[0, 0], dx[None], *[grads[n] for n in WEIGHTS], *[delta[n] for n in WEIGHTS],
            *[new_m[n] for n in WEIGHTS], *[new_v[n] for n in WEIGHTS])
```

```python
import numpy as np
import jax
import jax.numpy as jnp
from jax import lax
from jax.experimental import pallas as pl
from jax.experimental.pallas import tpu as pltpu

SEQ = 2048
D_MODEL = 1024
D_FF = 2816
HG_HEADS = 4
HG_DIM = 128
HG_WIDTH = 512
HG_CHUNK = 64
ATT_GROUPS = ((128, 1), (512, 4), (2048, 16))
ATT_HEADS = 8
ATT_WIDTH = 512
ATT_BLOCK = 128
ALIBI_MAX = 8.0
IN_COLS = 8704
EPS = 1e-6
NEG_INF = -1e30
ADAM_LR = 0.001
ADAM_B1 = 0.9
ADAM_B2 = 0.999
ADAM_EPS = 1e-08
ADAM_WD = 0.01
ADAM_STEP = 10

N_CHIPS = 4
MXU_DTYPE = jnp.bfloat16
WEIGHT_COMM_DTYPE = jnp.bfloat16
GRAD_COMM_DTYPE = jnp.bfloat16
ACT_DTYPE = jnp.bfloat16
MESH = pl.DeviceIdType.MESH
F32 = jnp.float32


def _sigmoid(x):
    return 1.0 / (1.0 + jnp.exp(-x))


def _dot(a, b, ta=False, tb=False):
    dn = (((0 if ta else 1,), (1 if tb else 0,)), ((), ()))
    return lax.dot_general(a.astype(MXU_DTYPE), b.astype(MXU_DTYPE), dn, preferred_element_type=F32)


def _dot_f32(a, b, ones_on_right=False):
    x = a if ones_on_right else b
    hi = x.astype(jnp.bfloat16)
    rest = x - hi.astype(F32)
    mid = rest.astype(jnp.bfloat16)
    lo = (rest - mid.astype(F32)).astype(jnp.bfloat16)
    if ones_on_right:
        dot = lambda q: jnp.dot(q, b.astype(jnp.bfloat16), preferred_element_type=F32)
    else:
        dot = lambda q: jnp.dot(a.astype(jnp.bfloat16), q, preferred_element_type=F32)
    return dot(hi) + (dot(mid) + dot(lo))


def _split_bf16(x):
    hi = x.astype(jnp.bfloat16)
    return hi, (x - hi.astype(F32)).astype(jnp.bfloat16)


def _hdot(a, b, ta=False, tb=False):
    dn =(((0 if ta else 1,), (1 if tb else 0,)), ((), ()))
    (a_hi, a_lo), (b_hi, b_lo) = _split_bf16(a), _split_bf16(b)
    dot = lambda p, q: lax.dot_general(p, q, dn, preferred_element_type=F32)
    return dot(a_hi, b_hi) + (dot(a_lo, b_hi) + dot(a_hi, b_lo))


def _in_hbm(a):
    return pltpu.with_memory_space_constraint(a, pltpu.HBM)


def pallas_call(body, **kw):
    grid_spec = kw.get("grid_spec")
    specs = list(kw["in_specs"] if grid_spec is None else grid_spec.in_specs)
    n_prefetch = 0 if grid_spec is None else grid_spec.num_scalar_prefetch
    out_specs = kw["out_specs"] if grid_spec is None else grid_spec.out_specs
    one = not isinstance(kw["out_shape"], (list, tuple))
    shapes = [kw["out_shape"]] if one else list(kw["out_shape"])
    out_specs = [out_specs] if one else list(out_specs)
    shapes = [pltpu.HBM(a.shape, a.dtype) if s.memory_space is None and isinstance(a, jax.ShapeDtypeStruct) else a
              for a, s in zip(shapes, out_specs)]
    kw["out_shape"] = shapes[0] if one else shapes
    call = pl.pallas_call(body, **kw)

    def run(*args):
        assert len(args) == n_prefetch + len(specs)
        pinned = [_in_hbm(a) if s.memory_space is None else a for a, s in zip(args[n_prefetch:], specs)]
        return call(*args[:n_prefetch], *pinned)

    return run


MATMUL_VMEM_BYTES = 48 * 1024 * 1024
MATMUL_TILE_BYTES = 36 * 1024 * 1024
MXU_ALIGN = 128
MXU_FLOPS_PER_SECOND = 900e12
HBM_BYTES_PER_SECOND = 3.0e12
EXPOSED_BYTES_PER_SECOND = 1.3e12
ACCUMULATOR_BYTES_PER_SECOND = 4.5e12
GRID_STEP_SECONDS = 0.35e-6


def _divisors(n, most):
    return [t for t in range(min(n, most), 0, -MXU_ALIGN) if n % t == 0 and t % MXU_ALIGN == 0]


def _matmul_tiles(M, N, K, in_bytes, out_bytes, has_res):
    best = None
    for tk in _divisors(K, K):
        nk = K // tk
        for tm in _divisors(M, 2048):
            for tn in _divisors(N, 512):
                tiles = 2 * in_bytes * (tm * tk + tk * tn) + 2 * out_bytes * tm * tn
                tiles += 4 * tm * tn * ((nk > 1) + 2 * has_res)
                if tiles > MATMUL_TILE_BYTES:
                    continue
                steps = (M // tm) * (N // tn) * nk
                result_bytes = out_bytes + 4 * has_res
                traffic = in_bytes * (M * K * (1 if nk == 1 else N // tn) + K * N * (M // tm)) + result_bytes * M * N
                exposed = in_bytes * (tm * tk + tk * tn) + result_bytes * tm * tn
                seconds = (max(2 * M * N * K / MXU_FLOPS_PER_SECOND, traffic / HBM_BYTES_PER_SECOND)
                           + exposed / EXPOSED_BYTES_PER_SECOND + steps * GRID_STEP_SECONDS
                           + (nk > 1) * steps * 8 * tm * tn / ACCUMULATOR_BYTES_PER_SECOND)
                key = (seconds, -tm * tn * tk)
                if best is None or key < best[0]:
                    best = (key, (tm, tn, tk))
    return best[1]


def matmul(a, b, *, ta=False, tb=False, out_dtype=F32, res=None, scale=1.0, behind=(), name):
    if ta:
        K, M = a.shape
    else:
        M, K = a.shape
    if tb:
        N, K2 = b.shape
    else:
        K2, N = b.shape
    assert K == K2 and a.dtype == b.dtype
    tm, tn, tk = _matmul_tiles(M, N, K, a.dtype.itemsize, jnp.dtype(out_dtype).itemsize, res is not None)
    nk = K // tk

    def finish(r, r_ref, o_ref):
        if scale != 1.0:
            r = r * scale
        if res is not None:
            r = r_ref[...] + r
        o_ref[...] = r.astype(out_dtype)

    def body(*refs):
        a_ref, b_ref = refs[:2]
        r_ref = refs[2] if res is not None else None
        o_ref = refs[2 + (res is not None) + len(behind)]
        if nk == 1:
            finish(_dot(a_ref[...], b_ref[...], ta, tb), r_ref, o_ref)
            return
        acc = refs[-1]
        k = pl.program_id(2)

        @pl.when(k == 0)
        def _():
            acc[...] = jnp.zeros_like(acc)

        acc[...] += _dot(a_ref[...], b_ref[...], ta, tb)

        @pl.when(k == nk - 1)
        def _():
            finish(acc[...], r_ref, o_ref)

    a_spec = pl.BlockSpec((tk, tm), lambda i, j, k: (k, i)) if ta else pl.BlockSpec((tm, tk), lambda i, j, k: (i, k))
    b_spec = pl.BlockSpec((tn, tk), lambda i, j, k: (j, k)) if tb else pl.BlockSpec((tk, tn), lambda i, j, k: (k, j))
    in_specs = [a_spec, b_spec]
    args = [a, b]
    if res is not None:
        in_specs.append(pl.BlockSpec((tm, tn), lambda i, j, k: (i, j)))
        args.append(res)
    for earlier in behind:
        in_specs.append(pl.BlockSpec(memory_space=pl.ANY))
        args.append(earlier)
    return pallas_call(
        body, name=name, grid=(M // tm, N // tn, nk), in_specs=in_specs,
        out_specs=pl.BlockSpec((tm, tn), lambda i, j, k: (i, j)),
        out_shape=jax.ShapeDtypeStruct((M, N), out_dtype),
        scratch_shapes=[pltpu.VMEM((tm, tn), F32)] if nk > 1 else [],
        compiler_params=pltpu.CompilerParams(dimension_semantics=("parallel", "parallel", "arbitrary"),
                                             vmem_limit_bytes=MATMUL_VMEM_BYTES),
    )(*args)


ROW_TILE = 256


def rmsnorm_fwd(x, g, name, behind=()):
    def body(x_ref, g_ref, *refs):
        n_ref = refs[-1]
        xv = x_ref[...]
        r = lax.rsqrt(jnp.mean(xv * xv, axis=-1, keepdims=True) + EPS)
        n_ref[...] = ((xv * r) * g_ref[...]).astype(n_ref.dtype)

    order = list(behind)
    return pallas_call(
        body, name=name, grid=(SEQ // ROW_TILE,),
        in_specs=[pl.BlockSpec((ROW_TILE, D_MODEL), lambda i: (i, 0)), pl.BlockSpec((1, D_MODEL), lambda i: (0, 0))]
        + [pl.BlockSpec(memory_space=pl.ANY)] * len(order),
        out_specs=pl.BlockSpec((ROW_TILE, D_MODEL), lambda i: (i, 0)),
        out_shape=jax.ShapeDtypeStruct((SEQ, D_MODEL), MXU_DTYPE),
    )(x, g, *order)


def rmsnorm_bwd(x, g, dn, dres, name):
    def body(x_ref, g_ref, dn_ref, dr_ref, dx_ref, dxm_ref, dg_ref):
        xv = x_ref[...]
        r = lax.rsqrt(jnp.mean(xv * xv, axis=-1, keepdims=True) + EPS)
        xh = xv * r
        dnv = dn_ref[...]

        @pl.when(pl.program_id(0) == 0)
        def _():
            dg_ref[...] = jnp.zeros_like(dg_ref)

        dg_ref[...] += jnp.sum(dnv * xh, axis=0, keepdims=True)
        dxh = dnv * g_ref[...]
        dx = dr_ref[...] + r * (dxh - xh * jnp.mean(dxh * xh, axis=-1, keepdims=True))
        dx_ref[...] = dx
        dxm_ref[...] = dx.astype(dxm_ref.dtype)

    row = pl.BlockSpec((ROW_TILE, D_MODEL), lambda i: (i, 0))
    vec = pl.BlockSpec((1, D_MODEL), lambda i: (0, 0))
    return pallas_call(
        body, name=name, grid=(SEQ // ROW_TILE,), in_specs=[row, vec, row, row], out_specs=[row, row, vec],
        out_shape=[jax.ShapeDtypeStruct((SEQ, D_MODEL), F32), jax.ShapeDtypeStruct((SEQ, D_MODEL), MXU_DTYPE),
                   jax.ShapeDtypeStruct((1, D_MODEL), F32)],
        compiler_params=pltpu.CompilerParams(dimension_semantics=("arbitrary",)),
    )(x, g, dn, dres)


def final_norm_loss(h, g, target, name):
    def body(h_ref, g_ref, t_ref, dh_ref, dhm_ref, dg_ref, loss_ref):
        xv = h_ref[...]
        r = lax.rsqrt(jnp.mean(xv * xv, axis=-1, keepdims=True) + EPS)
        xh = xv * r
        gv = g_ref[...]
        e = xh * gv - t_ref[...]

        @pl.when(pl.program_id(0) == 0)
        def _():
            dg_ref[...] = jnp.zeros_like(dg_ref)
            loss_ref[...] = jnp.zeros_like(loss_ref)

        part = 0.5 * jnp.sum(jnp.sum(e * e, axis=-1, keepdims=True) * (1.0 / D_MODEL), axis=0, keepdims=True)
        loss_ref[...] += jnp.broadcast_to(part, loss_ref.shape)
        dout = e * (1.0 / D_MODEL)
        dg_ref[...] += jnp.sum(dout * xh, axis=0, keepdims=True)
        dxh = dout * gv
        dh = r * (dxh - xh * jnp.mean(dxh * xh, axis=-1, keepdims=True))
        dh_ref[...] = dh
        dhm_ref[...] = dh.astype(dhm_ref.dtype)

    row = pl.BlockSpec((ROW_TILE, D_MODEL), lambda i: (i, 0))
    vec = pl.BlockSpec((1, D_MODEL), lambda i: (0, 0))
    return pallas_call(
        body, name=name, grid=(SEQ // ROW_TILE,), in_specs=[row, vec, row],
        out_specs=[row, row, vec, pl.BlockSpec((8, 128), lambda i: (0, 0))],
        out_shape=[jax.ShapeDtypeStruct((SEQ, D_MODEL), F32), jax.ShapeDtypeStruct((SEQ, D_MODEL), MXU_DTYPE),
                   jax.ShapeDtypeStruct((1, D_MODEL), F32), jax.ShapeDtypeStruct((8, 128), F32)],
        compiler_params=pltpu.CompilerParams(dimension_semantics=("arbitrary",)),
    )(h, g, target)


FFN_TILE = 256
FFN_TILES = D_FF // FFN_TILE


def gate_up_swiglu(n, w_gu, name, behind=()):
    def body(n_ref, wa_ref, wb_ref, *refs):
        a_ref, b_ref, s_ref = refs[len(behind):]
        nv = n_ref[...]
        a = _dot(nv, wa_ref[...])
        b = _dot(nv, wb_ref[...])
        a_ref[...] = a.astype(a_ref.dtype)
        b_ref[...] = b.astype(b_ref.dtype)
        s_ref[...] = (a * _sigmoid(a) * b).astype(s_ref.dtype)

    tile = pl.BlockSpec((SEQ, FFN_TILE), lambda j: (0, j))
    act = jax.ShapeDtypeStruct((SEQ, D_FF), ACT_DTYPE)
    return pallas_call(
        body, name=name, grid=(FFN_TILES,),
        in_specs=[pl.BlockSpec((SEQ, D_MODEL), lambda j: (0, 0)), pl.BlockSpec((D_MODEL, FFN_TILE), lambda j: (0, j)),
                  pl.BlockSpec((D_MODEL, FFN_TILE), lambda j: (0, j + FFN_TILES))]
        + [pl.BlockSpec(memory_space=pl.ANY)] * len(behind),
        out_specs=[tile, tile, tile], out_shape=[act, act, jax.ShapeDtypeStruct((SEQ, D_FF), MXU_DTYPE)],
        compiler_params=pltpu.CompilerParams(dimension_semantics=("parallel",), vmem_limit_bytes=MATMUL_VMEM_BYTES),
    )(n, w_gu, w_gu, *behind)


def swiglu_bwd(a, b, ds, name):
    rows = ROW_TILE // 2

    def body(a_ref, b_ref, ds_ref, o_ref):
        av = a_ref[...].astype(F32)
        sg = _sigmoid(av)
        dsv = ds_ref[...].astype(F32)
        o_ref[:, :D_FF] = (dsv * b_ref[...].astype(F32) * (sg * (1.0 + av * (1.0 - sg)))).astype(o_ref.dtype)
        o_ref[:, D_FF:] = (dsv * av * sg).astype(o_ref.dtype)

    blk = pl.BlockSpec((rows, D_FF), lambda i: (i, 0))
    return pallas_call(
        body, name=name, grid=(SEQ // rows,), in_specs=[blk, blk, blk],
        out_specs=pl.BlockSpec((rows, 2 * D_FF), lambda i: (i, 0)),
        out_shape=jax.ShapeDtypeStruct((SEQ, 2 * D_FF), MXU_DTYPE), compiler_params=SUM_PARAMS,
    )(a, b, ds)


GATE_HG_BLK = 6656 // 512
GATE_ATT_BLK = 7680 // 512


def merge_fwd(z, bh, ba, name):
    def body(gh_ref, ga_ref, bh_ref, ba_ref, o_ref):
        o_ref[...] = (_sigmoid(gh_ref[...]) * bh_ref[...] + _sigmoid(ga_ref[...]) * ba_ref[...]).astype(o_ref.dtype)

    blk = pl.BlockSpec((ROW_TILE, 512), lambda i, j: (i, j))
    return pallas_call(
        body, name=name, grid=(SEQ // ROW_TILE, 2),
        in_specs=[pl.BlockSpec((ROW_TILE, 512), lambda i, j: (i, GATE_HG_BLK + j)),
                  pl.BlockSpec((ROW_TILE, 512), lambda i, j: (i, GATE_ATT_BLK + j)), blk, blk],
        out_specs=blk, out_shape=jax.ShapeDtypeStruct((SEQ, D_MODEL), MXU_DTYPE),
    )(z, z, bh, ba)


def merge_bwd(z, bh, ba, dm, name):
    def body(gh_ref, ga_ref, bh_ref, ba_ref, dm_ref, dbh_ref, dba_ref, dgh_ref, dga_ref):
        dmv = dm_ref[...]
        sh = _sigmoid(gh_ref[...])
        sa = _sigmoid(ga_ref[...])
        dbh_ref[...] = (dmv * sh).astype(dbh_ref.dtype)
        dba_ref[...] = (dmv * sa).astype(dba_ref.dtype)
        dgh_ref[...] = (dmv * bh_ref[...] * (sh * (1.0 - sh))).astype(dgh_ref.dtype)
        dga_ref[...] = (dmv * ba_ref[...] * (sa * (1.0 - sa))).astype(dga_ref.dtype)

    blk = pl.BlockSpec((ROW_TILE, 512), lambda i, j: (i, j))
    out = jax.ShapeDtypeStruct((SEQ, D_MODEL), MXU_DTYPE)
    return pallas_call(
        body, name=name, grid=(SEQ // ROW_TILE, 2),
        in_specs=[pl.BlockSpec((ROW_TILE, 512), lambda i, j: (i, GATE_HG_BLK + j)),
                  pl.BlockSpec((ROW_TILE, 512), lambda i, j: (i, GATE_ATT_BLK + j)), blk, blk, blk],
        out_specs=[blk, blk, blk, blk], out_shape=[out, out, out, out],
    )(z, z, bh, ba, dm)


N_CHUNKS = SEQ // HG_CHUNK
HG_STEP_CHUNKS = 4


def _hgrn_gates(q, fp, lb):
    C = HG_CHUNK
    sg = _sigmoid(fp)
    f = lb + (1.0 - lb) * sg
    lf = jnp.log(f)
    row = lax.broadcasted_iota(jnp.int32, (C, C), 0)
    col = lax.broadcasted_iota(jnp.int32, (C, C), 1)
    causal = row >= col
    G = _dot_f32(causal.astype(F32), lf)
    eG = jnp.exp(G)
    enG = jnp.exp(-G)
    qg = q * eG
    kg = (1.0 - f) * enG
    A = jnp.where(causal, _hdot(qg, kg, tb=True), 0.0)
    egl = jnp.exp(jnp.sum(lf, axis=0, keepdims=True))
    return sg, f, causal, eG, enG, qg, kg, A, egl


def hgrn_fwd(z, lb, gain, name):
    C, K = HG_CHUNK, HG_DIM

    def body(q_ref, f_ref, v_ref, og_ref, p_ref, g_ref, y_ref, o_ref, st_ref, state):
        @pl.when(pl.program_id(0) == 0)
        def _():
            state[...] = jnp.zeros_like(state)

        for cc in range(HG_STEP_CHUNKS):
            rows = pl.ds(cc * C, C)
            for h in range(HG_HEADS):
                hd = pl.ds(h * K, K)
                v = v_ref[rows, hd]
                _, _, _, _, _, qg, kg, A, egl = _hgrn_gates(q_ref[rows, hd], f_ref[rows, hd], p_ref[:, hd])
                st = state[h]
                st_ref[h, cc] = st
                o = _hdot(A, v) + _hdot(qg, st, tb=True)
                state[h] = st * egl + _hdot(v, kg * egl, ta=True)
                o_ref[rows, hd] = o
                rs = lax.rsqrt(jnp.mean(o * o, axis=-1, keepdims=True) + EPS)
                og = og_ref[rows, hd]
                y_ref[rows, hd] = (((o * rs) * g_ref[:, hd]) * (og * _sigmoid(og))).astype(y_ref.dtype)

    R = HG_STEP_CHUNKS * C

    def zcol(section):
        return pl.BlockSpec((R, HG_WIDTH), lambda c: (c, section))

    vec = pl.BlockSpec((1, HG_WIDTH), lambda c: (0, 0))
    blk = pl.BlockSpec((R, HG_WIDTH), lambda c: (c, 0))
    return pallas_call(
        body, name=name, grid=(N_CHUNKS // HG_STEP_CHUNKS,),
        in_specs=[zcol(0), zcol(1), zcol(2), zcol(3), vec, vec],
        out_specs=[blk, blk, pl.BlockSpec((HG_HEADS, HG_STEP_CHUNKS, K, K), lambda c: (0, c, 0, 0))],
        out_shape=[jax.ShapeDtypeStruct((SEQ, HG_WIDTH), MXU_DTYPE), jax.ShapeDtypeStruct((SEQ, HG_WIDTH), F32),
                   jax.ShapeDtypeStruct((HG_HEADS, N_CHUNKS, K, K), F32)],
        scratch_shapes=[pltpu.VMEM((HG_HEADS, K, K), F32)],
        compiler_params=pltpu.CompilerParams(dimension_semantics=("arbitrary",)),
    )(z, z, z, z, lb, gain)


def hgrn_bwd(z, lb, gain, o_raw, states, dy, name):
    C, K = HG_CHUNK, HG_DIM

    def body(q_ref, f_ref, v_ref, og_ref, p_ref, g_ref, o_ref, st_ref, dy_ref,
             dq_ref, dfp_ref, dv_ref, dog_ref, dlb_ref, dgain_ref, dstate):
        @pl.when(pl.program_id(0) == 0)
        def _():
            dstate[...] = jnp.zeros_like(dstate)
            dlb_ref[...] = jnp.zeros_like(dlb_ref)
            dgain_ref[...] = jnp.zeros_like(dgain_ref)

        last = lax.broadcasted_iota(jnp.int32, (C, K), 0) == C - 1
        row = lax.broadcasted_iota(jnp.int32, (C, C), 0)
        col = lax.broadcasted_iota(jnp.int32, (C, C), 1)
        anti_causal = (col >= row).astype(F32)
        for cc in reversed(range(HG_STEP_CHUNKS)):
            rows = pl.ds(cc * C, C)
            for h in range(HG_HEADS):
                hd = pl.ds(h * K, K)
                v = v_ref[rows, hd]
                lb = p_ref[:, hd]
                sg, f, causal, eG, enG, qg, kg, A, egl = _hgrn_gates(q_ref[rows, hd], f_ref[rows, hd], lb)
                kd = kg * egl
                st = st_ref[h, cc]
                dst = dstate[h]
                o = o_ref[rows, hd]
                og = og_ref[rows, hd]
                gain_v = g_ref[:, hd]
                dyv = dy_ref[rows, hd]
                rs = lax.rsqrt(jnp.mean(o * o, axis=-1, keepdims=True) + EPS)
                on = o * rs
                sgo = _sigmoid(og)
                silu = og * sgo
                dog_ref[rows, hd] = (dyv * (on * gain_v) * (sgo * (1.0 + og * (1.0 - sgo)))).astype(dog_ref.dtype)
                dgain_ref[:, hd] += jnp.sum(dyv * silu * on, axis=0, keepdims=True)
                don = dyv * gain_v * silu
                do = rs * (don - on * jnp.mean(don * on, axis=-1, keepdims=True))
                dA = jnp.where(causal, _hdot(do, v, tb=True), 0.0)
                dv_ref[rows, hd] = (_hdot(A, do, ta=True) + _hdot(kd, dst, tb=True)).astype(dv_ref.dtype)
                dqg = _hdot(dA, kg) + _hdot(do, st)
                dkg = _hdot(dA, qg, ta=True)
                dkd = _hdot(v, dst)
                dstate[h] = dst * egl + _hdot(do, qg, ta=True)
                dgl = jnp.sum(st * dst, axis=0, keepdims=True) * egl
                dq_ref[rows, hd] = (dqg * eG).astype(dq_ref.dtype)
                dk = dkg * enG + dkd * (enG * egl)
                dG = dqg * qg - dkg * kg - dkd * kd
                extra = jnp.sum(dkd * kd, axis=0, keepdims=True) + dgl
                dG = dG + jnp.where(last, extra, 0.0)
                dlf = _dot_f32(anti_causal, dG)
                df = dlf / f - dk
                dfp_ref[rows, hd] = (df * (1.0 - lb) * (sg * (1.0 - sg))).astype(dfp_ref.dtype)
                dlb_ref[:, hd] += jnp.sum(df * (1.0 - sg), axis=0, keepdims=True)

    R = HG_STEP_CHUNKS * C
    n_steps = N_CHUNKS // HG_STEP_CHUNKS

    def rc(c):
        return n_steps - 1 - c

    def zcol(section):
        return pl.BlockSpec((R, HG_WIDTH), lambda c: (rc(c), section))

    vec = pl.BlockSpec((1, HG_WIDTH), lambda c: (0, 0))
    blk = pl.BlockSpec((R, HG_WIDTH), lambda c: (rc(c), 0))
    out = jax.ShapeDtypeStruct((SEQ, HG_WIDTH), MXU_DTYPE)
    small = jax.ShapeDtypeStruct((1, HG_WIDTH), F32)
    return pallas_call(
        body, name=name, grid=(n_steps,),
        in_specs=[zcol(0), zcol(1), zcol(2), zcol(3), vec, vec, blk,
                  pl.BlockSpec((HG_HEADS, HG_STEP_CHUNKS, K, K), lambda c: (0, rc(c), 0, 0)), blk],
        out_specs=[blk, blk, blk, blk, vec, vec],
        out_shape=[out, out, out, out, small, small],
        scratch_shapes=[pltpu.VMEM((HG_HEADS, K, K), F32)],
        compiler_params=pltpu.CompilerParams(dimension_semantics=("arbitrary",)),
    )(z, z, z, z, lb, gain, o_raw, states, dy)


N_GROUPS = len(ATT_GROUPS)
HEAD_PAIRS = ATT_WIDTH // 128
ATT_COL0 = 4 * HG_WIDTH
UNROLLED_UNITS = 4


def _alibi_coef():
    n = N_GROUPS * ATT_HEADS
    slopes = np.exp2(-ALIBI_MAX * np.arange(1, n + 1, dtype=np.float32) / n).astype(np.float32)
    dil = np.repeat(np.array([d for _, d in ATT_GROUPS], np.float32), ATT_HEADS)
    return jnp.asarray(slopes * dil, F32)


def _for_each_unit(n, fn):
    if n <= UNROLLED_UNITS:
        for u in range(n):
            fn(u)
    else:
        def group(i, carry):
            for j in range(UNROLLED_UNITS):
                fn(i * UNROLLED_UNITS + j)
            return carry
        lax.fori_loop(0, n // UNROLLED_UNITS, group, 0)


def _att_geometry(g):
    B = ATT_BLOCK
    d = ATT_GROUPS[g][1]
    n_blocks = SEQ // (d * B)
    col0 = (ATT_COL0 + g * 3 * ATT_WIDTH) // 128

    def block_rows(b, r):
        return pl.ds(b * (B * d) + r, B, stride=d) if d > 1 else pl.ds(pl.multiple_of(b * B, B), B)

    def block_of(u):
        return (u, 0) if d == 1 else (u // d, u % d)

    return d, n_blocks, col0, block_rows, block_of


def _att_column(c):
    return pl.BlockSpec((SEQ, 128), lambda hp: (0, c + hp))


def _head_lanes(j):
    lane = lax.broadcasted_iota(jnp.int32, (ATT_BLOCK, 128), 1)
    return (lane >= 64 * j) & (lane < 64 * (j + 1))


def _stack_heads(x, sel0):
    return jnp.concatenate([jnp.where(sel0, x, 0.0), jnp.where(sel0, 0.0, x)], axis=0)


def _stack_values(x, sel0, lanes):
    swapped = pltpu.roll(x, 64, 1)
    stacked = jnp.concatenate([jnp.where(sel0, x, swapped), jnp.where(sel0, swapped, x)], axis=0)
    return stacked if lanes == 128 else jnp.concatenate([stacked] * (lanes // 128), axis=1)


def _pair_coef(coef_ref, g, hp):
    row = lax.broadcasted_iota(jnp.int32, (2 * ATT_BLOCK, 1), 0)
    first = g * ATT_HEADS + hp * 2
    return jnp.where(row < ATT_BLOCK, coef_ref[first], coef_ref[first + 1])


def _band(with_prev, first_key):
    B = ATT_BLOCK
    keys = 2 * B if with_prev else B
    qi = jnp.bitwise_and(lax.broadcasted_iota(jnp.int32, (2 * B, keys), 0), B - 1)
    kj = lax.broadcasted_iota(jnp.int32, (2 * B, keys), 1)
    delta = qi + (B if with_prev else 0) - kj
    valid = (delta >= 0) & (delta <= B)
    if with_prev:
        valid = valid & (kj >= first_key)
    return valid, delta.astype(F32)


def att_fwd(z, g, name):
    B = ATT_BLOCK
    d, n_blocks, col0, block_rows, block_of = _att_geometry(g)
    multi = n_blocks > 1

    def body(coef_ref, q_ref, k_ref, v_ref, o_ref, l_ref):
        cf2 = _pair_coef(coef_ref, g, pl.program_id(0))
        sel0 = _head_lanes(0)

        def one(u):
            b, r = block_of(u)
            rows = block_rows(b, r)
            valid, dist = _band(multi, jnp.where(b == 0, B, 0))
            q2 = _stack_heads(q_ref[rows, :], sel0)
            kk, vv = k_ref[rows, :], v_ref[rows, :]
            if multi:
                prev_rows = block_rows(jnp.maximum(b - 1, 0), r)
                kk = jnp.concatenate([k_ref[prev_rows, :], kk], axis=0)
                vv = jnp.concatenate([v_ref[prev_rows, :], vv], axis=0)
            sc = jnp.where(valid, _dot(q2, kk, tb=True) * 0.125 - cf2 * dist, NEG_INF)
            mx = jnp.max(sc, axis=-1, keepdims=True)
            e = jnp.exp(sc - mx)
            den = jnp.sum(e, axis=-1, keepdims=True)
            o2 = _dot(e * (1.0 / den), vv)
            lse2 = mx + jnp.log(den)
            o_ref[rows, :] = jnp.where(sel0, o2[:B], o2[B:])
            l_ref[rows, :] = jnp.where(sel0, lse2[:B], lse2[B:])

        _for_each_unit(d * n_blocks, one)

    out = jax.ShapeDtypeStruct((SEQ, ATT_WIDTH), F32)
    return pallas_call(
        body, name=name, grid=(HEAD_PAIRS,),
        in_specs=[pl.BlockSpec(memory_space=pltpu.SMEM), _att_column(col0), _att_column(col0 + 4), _att_column(col0 + 8)],
        out_specs=[_att_column(0), _att_column(0)], out_shape=[out, out],
        compiler_params=pltpu.CompilerParams(dimension_semantics=("parallel",)),
    )(_alibi_coef(), z, z, z)


def att_bwd(z, l, do, corr, g, name):
    B = ATT_BLOCK
    d, n_blocks, col0, block_rows, block_of = _att_geometry(g)
    multi = n_blocks > 1
    own = slice(B, 2 * B) if multi else slice(0, B)

    def body(coef_ref, q_ref, k_ref, v_ref, l_ref, do_ref, cr_ref, dq_ref, dk_ref, dv_ref, dq_sc, dk_sc, dv_sc):
        cf2 = _pair_coef(coef_ref, g, pl.program_id(0))
        sel0 = _head_lanes(0)

        def one(u):
            b, r = block_of(u)
            rows = block_rows(b, r)
            valid, dist = _band(multi, jnp.where(b == 0, B, 0))
            kk, vv = k_ref[rows, :], v_ref[rows, :]
            if multi:
                prev_rows = block_rows(jnp.maximum(b - 1, 0), r)
                kk = jnp.concatenate([k_ref[prev_rows, :], kk], axis=0)
                vv = jnp.concatenate([v_ref[prev_rows, :], vv], axis=0)
            q2, do2 = _stack_heads(q_ref[rows, :], sel0), _stack_heads(do_ref[rows, :], sel0)
            keys = kk.shape[0]
            lse2, cr2 = _stack_values(l_ref[rows, :], sel0, keys), _stack_values(cr_ref[rows, :], sel0, keys)
            p = jnp.exp(jnp.where(valid, _dot(q2, kk, tb=True) * 0.125 - cf2 * dist, NEG_INF) - lse2)
            ds = p * (_dot(do2, vv, tb=True) + cr2)
            dq2 = _dot(ds, kk)
            dkk = _dot(ds, q2, ta=True) * 0.125
            dvv = _dot(p, do2, ta=True)
            dq_sc[rows, :] = jnp.where(sel0, dq2[:B], dq2[B:]) * 0.125
            dk_sc[rows, :] = dkk[own]
            dv_sc[rows, :] = dvv[own]
            if multi:
                dk_sc[prev_rows, :] += dkk[:B]
                dv_sc[prev_rows, :] += dvv[:B]

        _for_each_unit(d * n_blocks, one)
        dq_ref[...] = dq_sc[...].astype(dq_ref.dtype)
        dk_ref[...] = dk_sc[...].astype(dk_ref.dtype)
        dv_ref[...] = dv_sc[...].astype(dv_ref.dtype)

    col = _att_column
    out = jax.ShapeDtypeStruct((SEQ, ATT_WIDTH), MXU_DTYPE)
    return pallas_call(
        body, name=name, grid=(HEAD_PAIRS,),
        in_specs=[pl.BlockSpec(memory_space=pltpu.SMEM), col(col0), col(col0 + 4), col(col0 + 8), col(0), col(0), col(0)],
        out_specs=[col(0)] * 3, out_shape=[out] * 3,
        scratch_shapes=[pltpu.VMEM((SEQ, 128), F32)] * 3,
        compiler_params=pltpu.CompilerParams(dimension_semantics=("parallel",), vmem_limit_bytes=MATMUL_VMEM_BYTES),
    )(_alibi_coef(), z, z, z, l, do, corr)


def _head_sum(x):
    i = lax.broadcasted_iota(jnp.int32, (128, 128), 0) // 64
    j = lax.broadcasted_iota(jnp.int32, (128, 128), 1) // 64
    return _dot_f32(x, (i == j).astype(F32), ones_on_right=True)


def _group_weights(l0, l1, l2):
    mx = jnp.maximum(jnp.maximum(l0, l1), l2)
    e0, e1, e2 = jnp.exp(l0 - mx), jnp.exp(l1 - mx), jnp.exp(l2 - mx)
    inv = 1.0 / (e0 + e1 + e2)
    return e0 * inv, e1 * inv, e2 * inv


def att_combine_fwd(o, l, name):
    def body(o0, o1, o2, l0, l1, l2, y_ref):
        w0, w1, w2 = _group_weights(l0[...], l1[...], l2[...])
        y_ref[...] = (o0[...] * w0 + o1[...] * w1 + o2[...] * w2).astype(y_ref.dtype)

    blk = pl.BlockSpec((ROW_TILE, ATT_WIDTH), lambda i: (i, 0))
    return pallas_call(
        body, name=name, grid=(SEQ // ROW_TILE,), in_specs=[blk] * 6, out_specs=blk,
        out_shape=jax.ShapeDtypeStruct((SEQ, ATT_WIDTH), MXU_DTYPE),
    )(*o, *l)


def att_combine_bwd(o, l, dy, name):
    def body(o0, o1, o2, l0, l1, l2, dy_ref, do0, do1, do2, cr0, cr1, cr2):
        w = _group_weights(l0[...], l1[...], l2[...])
        dyv = dy_ref[...]
        tot = _head_sum(dyv * (w[0] * o0[...] + w[1] * o1[...] + w[2] * o2[...]))
        for g, (do_ref, cr_ref) in enumerate(((do0, cr0), (do1, cr1), (do2, cr2))):
            do_ref[...] = dyv * w[g]
            cr_ref[...] = -w[g] * tot

    blk = pl.BlockSpec((ROW_TILE, 128), lambda i, j: (i, j))
    out = jax.ShapeDtypeStruct((SEQ, ATT_WIDTH), F32)
    res = pallas_call(
        body, name=name, grid=(SEQ // ROW_TILE, HEAD_PAIRS), in_specs=[blk] * 7, out_specs=[blk] * 6, out_shape=[out] * 6,
    )(*o, *l, dy)
    return res[:N_GROUPS], res[N_GROUPS:]


SUM_MAX_ROWS = 1024
SUM_ROW_ALIGN = 16
SUM_TILE_BYTES = 24 * 1024 * 1024
SUM_PARAMS = pltpu.CompilerParams(vmem_limit_bytes=MATMUL_VMEM_BYTES)


def _row_tile(rows, cols, operands):
    most = min(rows, SUM_MAX_ROWS) // SUM_ROW_ALIGN * SUM_ROW_ALIGN
    fit = [t for t in range(most, 0, -SUM_ROW_ALIGN) if rows % t == 0]
    return next((t for t in fit if 2 * 4 * operands * t * cols <= SUM_TILE_BYTES), fit[-1])


def _shard_shape(rows, cols, axis):
    return (rows // N_CHIPS, cols) if axis == 0 else (rows, cols // N_CHIPS)


def _half_shape(rows, cols, axis):
    return (rows, cols // 2) if axis == 0 else (rows // 2, cols)


def _piece_shape(rows, cols, axis):
    return (rows // N_CHIPS, cols // 2) if axis == 0 else (rows // 2, cols // N_CHIPS)


def place_own_block(shard, chip, rows, cols, axis, name):
    sr, sc = _shard_shape(rows, cols, axis)
    tr = _row_tile(sr, sc, 2)

    def body(chip_ref, s_ref, o_ref):
        o_ref[...] = s_ref[...].astype(o_ref.dtype)

    if axis == 0:
        out_map = lambda i, chip_ref: (chip_ref[0] * (sr // tr) + i, 0)
    else:
        out_map = lambda i, chip_ref: (i, chip_ref[0])
    return pallas_call(
        body, name=name, out_shape=jax.ShapeDtypeStruct((rows, cols), WEIGHT_COMM_DTYPE), compiler_params=SUM_PARAMS,
        grid_spec=pltpu.PrefetchScalarGridSpec(
            num_scalar_prefetch=1, grid=(sr // tr,), in_specs=[pl.BlockSpec((tr, sc), lambda i, chip_ref: (i, 0))],
            out_specs=pl.BlockSpec((tr, sc), out_map)),
    )(chip, shard)


def add_halves(g, theirs, core, rows, cols, axis, name):
    hr, hc = _half_shape(rows, cols, axis)
    tr = _row_tile(hr, hc, 3)

    def body(core_ref, g_ref, t_ref, o_ref):
        o_ref[...] = (g_ref[...].astype(F32) + t_ref[...].astype(F32)).astype(o_ref.dtype)

    if axis == 0:
        g_map = lambda i, core_ref: (i, core_ref[0])
    else:
        g_map = lambda i, core_ref: (core_ref[0] * (hr // tr) + i, 0)
    blk = pl.BlockSpec((tr, hc), lambda i, core_ref: (i, 0))
    return pallas_call(
        body, name=name, out_shape=jax.ShapeDtypeStruct((hr, hc), GRAD_COMM_DTYPE), compiler_params=SUM_PARAMS,
        grid_spec=pltpu.PrefetchScalarGridSpec(
            num_scalar_prefetch=1, grid=(hr // tr,), in_specs=[pl.BlockSpec((tr, hc), g_map), blk], out_specs=blk),
    )(core, g, theirs)


def add_pieces(half, got, chip, rows, cols, axis, name):
    hr, _ = _half_shape(rows, cols, axis)
    pr, pc = _piece_shape(rows, cols, axis)
    tr = _row_tile(pr, pc, 5)

    def body(chip_ref, h_ref, got_ref, o_ref):
        o_ref[...] = (h_ref[...].astype(F32) + got_ref[0].astype(F32) + got_ref[1].astype(F32) + got_ref[2].astype(F32))

    if axis == 0:
        h_map = lambda i, chip_ref: (chip_ref[0] * (pr // tr) + i, 0)
    else:
        h_map = lambda i, chip_ref: (i, chip_ref[0])
    return pallas_call(
        body, name=name, out_shape=jax.ShapeDtypeStruct((pr, pc), F32), compiler_params=SUM_PARAMS,
        grid_spec=pltpu.PrefetchScalarGridSpec(
            num_scalar_prefetch=1, grid=(pr // tr,),
            in_specs=[pl.BlockSpec((tr, pc), h_map), pl.BlockSpec((3, tr, pc), lambda i, chip_ref: (0, i, 0))],
            out_specs=pl.BlockSpec((tr, pc), lambda i, chip_ref: (i, 0))),
    )(chip, half, got)


def _adamw_math(w, g, m, v):
    nm = ADAM_B1 * m + (1.0 - ADAM_B1) * g
    nv = ADAM_B2 * v + (1.0 - ADAM_B2) * (g * g)
    m_hat = nm / (1.0 - ADAM_B1 ** ADAM_STEP)
    v_hat = nv / (1.0 - ADAM_B2 ** ADAM_STEP)
    return -ADAM_LR * (m_hat / (jnp.sqrt(v_hat) + ADAM_EPS) + ADAM_WD * w), nm, nv


def adamw_halves(w, mine, theirs, m, v, core, rows, cols, axis, name):
    sr, sc = _shard_shape(rows, cols, axis)
    pr, pc = _piece_shape(rows, cols, axis)
    tr = _row_tile(pr, pc, 9)
    nt = pr // tr

    def body(core_ref, w_ref, a_ref, b_ref, m_ref, v_ref, g_ref, d_ref, nm_ref, nv_ref):
        g = jnp.where(pl.program_id(0) == core_ref[0], a_ref[...], b_ref[...])
        g_ref[...] = g
        d_ref[...], nm_ref[...], nv_ref[...] = _adamw_math(w_ref[...], g, m_ref[...], v_ref[...])

    if axis == 0:
        full = pl.BlockSpec((tr, pc), lambda h, i, core_ref: (i, h))
    else:
        full = pl.BlockSpec((tr, pc), lambda h, i, core_ref: (h * nt + i, 0))
    part = pl.BlockSpec((tr, pc), lambda h, i, core_ref: (i, 0))
    out = jax.ShapeDtypeStruct((sr, sc), F32)
    return pallas_call(
        body, name=name, out_shape=[out, out, out, out], compiler_params=SUM_PARAMS,
        grid_spec=pltpu.PrefetchScalarGridSpec(
            num_scalar_prefetch=1, grid=(2, nt), in_specs=[full, part, part, full, full], out_specs=[full] * 4),
    )(core, w, mine, theirs, m, v)


BIG = (
    ("ffn1_w_gate_up", D_MODEL, 2 * D_FF, 1),
    ("ffn1_w_down", D_FF, D_MODEL, 0),
    ("w_in", D_MODEL, IN_COLS, 1),
    ("w_branch_hg", HG_WIDTH, D_MODEL, 1),
    ("w_branch_att", ATT_WIDTH, D_MODEL, 1),
    ("w_out", D_MODEL, D_MODEL, 0),
    ("ffn2_w_gate_up", D_MODEL, 2 * D_FF, 1),
    ("ffn2_w_down", D_FF, D_MODEL, 0),
)
N_BIG = len(BIG)
ANY = pl.BlockSpec(memory_space=pl.ANY)


def _place():
    return lax.axis_index("x"), lax.axis_index("y"), lax.axis_index("c")


def _other_chips(x, y):
    return ((1 - x, y), (x, 1 - y), (1 - x, 1 - y))


MAX_COPY_CHUNKS = 16
CHUNK_ROW_ALIGN = 16


def _row_chunks(view):
    rows = view.shape[0]
    n = next(n for n in range(MAX_COPY_CHUNKS, 0, -1) if rows % (CHUNK_ROW_ALIGN * n) == 0 or n == 1)
    step = rows // n
    return [pl.ds(i * step, step) for i in range(n)]


def _remote(src, dst, send_sem, recv_sem, device):
    return pltpu.make_async_remote_copy(src_ref=src, dst_ref=dst, send_sem=send_sem, recv_sem=recv_sem,
                                        device_id=device, device_id_type=MESH)


def _start_remote(src, dst, send_sem, recv_sem, device):
    for rows in _row_chunks(src):
        _remote(src.at[rows, :], dst.at[rows, :], send_sem, recv_sem, device).start()
    return _remote(src, dst, send_sem, recv_sem, device)


HBM = pl.BlockSpec(memory_space=pltpu.HBM)
SEM = pl.BlockSpec(memory_space=pltpu.SEMAPHORE)
SPLIT_COPY_EFFECT = pltpu.SideEffectType.DATAFLOW_SIDE_EFFECTING
GROUPS = {"ffn1": (0, 1), "mix": (2, 3, 4, 5), "ffn2": (6, 7)}


class _SemList:
    def __init__(self, refs):
        self.refs = refs
        self.at = self

    def __getitem__(self, index):
        w, k = index
        return self.refs[3 * w + k]


def _gather_piece(ref, rows, cols, axis, chip, c):
    sr, sc = _shard_shape(rows, cols, axis)
    j = 2 * chip[0] + chip[1]
    if axis == 0:
        return ref.at[pl.ds(j * sr + c * (sr // 2), sr // 2), :]
    return ref.at[pl.ds(c * (sr // 2), sr // 2), pl.ds(pl.multiple_of(j * sc, 128), sc)]


def _start_gather_sends(bufs, ws, send_sems, recv_sems):
    x, y, c = _place()
    for w, (_, r, cc, ax) in enumerate(ws):
        mine = _gather_piece(bufs[w], r, cc, ax, (x, y), c)
        for k, chip in enumerate(_other_chips(x, y)):
            _start_remote(mine, mine, send_sems.at[w, k], recv_sems.at[w, k], (*chip, c))


def _wait_gather_sends(bufs, ws, send_sems, recv_sems):
    x, y, c = _place()
    for w, (_, r, cc, ax) in enumerate(ws):
        for k, chip in enumerate(_other_chips(x, y)):
            got = _gather_piece(bufs[w], r, cc, ax, chip, c)
            _remote(got, got, send_sems.at[w, k], recv_sems.at[w, k], (x, y, c)).wait_recv()
    for w, (_, r, cc, ax) in enumerate(ws):
        mine = _gather_piece(bufs[w], r, cc, ax, (x, y), c)
        for k in range(3):
            _remote(mine, mine, send_sems.at[w, k], recv_sems.at[w, k], (x, y, c)).wait_send()


def _forward_halves(bufs, ws, send_sems, recv_sems):
    x, y, c = _place()
    passed = []
    for w, (_, r, cc, ax) in enumerate(ws):
        for k, chip in enumerate(_other_chips(x, y)):
            got = _gather_piece(bufs[w], r, cc, ax, chip, c)
            passed.append(_start_remote(got, got, send_sems.at[w, k], recv_sems.at[w, k], (x, y, 1 - c)))
    for w, (_, r, cc, ax) in enumerate(ws):
        for k, chip in enumerate(_other_chips(x, y)):
            got = _gather_piece(bufs[w], r, cc, ax, chip, 1 - c)
            _remote(got, got, send_sems.at[w, k], recv_sems.at[w, k], (x, y, c)).wait_recv()
    for cp in passed:
        cp.wait_send()


def gather_start(placed, after, group):
    ws = [BIG[i] for i in GROUPS[group]]
    n = len(ws)

    def body(*refs):
        bufs = refs[:n]
        send_sems, recv_sems = _SemList(refs[n + 1:4 * n + 1]), _SemList(refs[4 * n + 1:7 * n + 1])
        token = refs[-1]
        _start_gather_sends(bufs, ws, send_sems, recv_sems)
        token[...] = jnp.zeros_like(token)

    out = pallas_call(
        body, name=f"gather_start_{group}", in_specs=[HBM] * n + [ANY],
        out_specs=[SEM] * (6 * n) + [HBM] * n + [pl.BlockSpec(memory_space=pltpu.VMEM)],
        out_shape=[pltpu.SemaphoreType.DMA(())] * (6 * n)
        + [pltpu.HBM((r, cc), WEIGHT_COMM_DTYPE) for _, r, cc, _ in ws] + [jax.ShapeDtypeStruct((8, 128), F32)],
        input_output_aliases={w: 6 * n + w for w in range(n)},
        compiler_params=pltpu.CompilerParams(has_side_effects=SPLIT_COPY_EFFECT),
    )(*[_in_hbm(p) for p in placed], after)
    return out[:3 * n], out[3 * n:6 * n], out[6 * n:7 * n], out[-1]


def gather_wait(bufs, send_sems, recv_sems, after, group):
    ws = [BIG[i] for i in GROUPS[group]]
    n = len(ws)

    def body(*refs):
        _wait_gather_sends(refs[:n], ws, _SemList(refs[n:n + 3 * n]), _SemList(refs[n + 3 * n:n + 6 * n]))

    return pallas_call(
        body, name=f"gather_wait_{group}", in_specs=[HBM] * n + [SEM] * (6 * n) + [ANY] * len(after), out_specs=[HBM] * n,
        out_shape=[pltpu.HBM((r, cc), WEIGHT_COMM_DTYPE) for _, r, cc, _ in ws],
        input_output_aliases={w: w for w in range(n)},
        compiler_params=pltpu.CompilerParams(has_side_effects=SPLIT_COPY_EFFECT),
    )(*bufs, *send_sems, *recv_sems, *after)


def gather_forward(bufs, group):
    ws = [BIG[i] for i in GROUPS[group]]
    n = len(ws)

    def body(*refs):
        _forward_halves(refs[n:2 * n], ws, refs[2 * n], refs[2 * n + 1])

    return pallas_call(
        body, name=f"gather_forward_{group}", in_specs=[ANY] * n, out_specs=[ANY] * n,
        out_shape=[jax.ShapeDtypeStruct((r, cc), WEIGHT_COMM_DTYPE) for _, r, cc, _ in ws],
        input_output_aliases={w: w for w in range(n)},
        scratch_shapes=[pltpu.SemaphoreType.DMA((n, 3))] * 2,
    )(*bufs)


def _half(ref, rows, cols, axis, c):
    if axis == 0:
        return ref.at[:, pl.ds(pl.multiple_of(c * (cols // 2), 128), cols // 2)]
    return ref.at[pl.ds(c * (rows // 2), rows // 2), :]


def _piece_of_half(ref, rows, cols, axis, chip):
    j = 2 * chip[0] + chip[1]
    pr, pc = _piece_shape(rows, cols, axis)
    if axis == 0:
        return ref.at[pl.ds(j * pr, pr), :]
    return ref.at[:, pl.ds(pl.multiple_of(j * pc, 128), pc)]


def sibling_exchange_start(srcs, view, landing_shapes, dtype, name):
    n = len(srcs)

    def body(*refs):
        ins, land, sems = refs[:n], refs[n:2 * n], refs[2 * n:4 * n]
        x, y, c = _place()
        for w in range(n):
            _start_remote(view(ins[w], w, c), land[w], sems[w], sems[n + w], (x, y, 1 - c))
        refs[-1][...] = jnp.zeros_like(refs[-1])

    landing = [lax.empty(shape, dtype) for shape in landing_shapes]
    out = pallas_call(
        body, name=name, in_specs=[HBM] * (2 * n),
        out_specs=[SEM] * (2 * n) + [HBM] * (2 * n) + [pl.BlockSpec(memory_space=pltpu.VMEM)],
        out_shape=[pltpu.SemaphoreType.DMA(())] * (2 * n) + [pltpu.HBM(a.shape, a.dtype) for a in srcs]
        + [pltpu.HBM(shape, dtype) for shape in landing_shapes] + [jax.ShapeDtypeStruct((8, 128), F32)],
        input_output_aliases={i: 2 * n + i for i in range(2 * n)},
        compiler_params=pltpu.CompilerParams(has_side_effects=SPLIT_COPY_EFFECT),
    )(*[_in_hbm(a) for a in srcs], *[_in_hbm(b) for b in landing])
    return out[:n], out[n:2 * n], out[2 * n:3 * n], out[3 * n:4 * n], out[-1]


def sibling_exchange_wait(srcs, landing, send_sems, recv_sems, view, after, name):
    n = len(srcs)

    def body(*refs):
        ins, land, sems = refs[:n], refs[n:2 * n], refs[2 * n:4 * n]
        x, y, c = _place()
        for w in range(n):
            cp = _remote(view(ins[w], w, c), land[w], sems[w], sems[n + w], (x, y, c))
            cp.wait_send()
            cp.wait_recv()

    out = pallas_call(
        body, name=name, in_specs=[HBM] * (2 * n) + [SEM] * (2 * n) + [ANY] * len(after), out_specs=[HBM] * (2 * n),
        out_shape=[pltpu.HBM(a.shape, a.dtype) for a in srcs] + [pltpu.HBM(b.shape, b.dtype) for b in landing],
        input_output_aliases={i: i for i in range(2 * n)},
        compiler_params=pltpu.CompilerParams(has_side_effects=SPLIT_COPY_EFFECT),
    )(*srcs, *landing, *send_sems, *recv_sems, *after)
    return out[:n], out[n:]


def _scatter_copies(halves, got, ws, send_sems, recv_sems, start):
    x, y, c = _place()
    copies = []
    for w, (_, r, cc, ax) in enumerate(ws):
        for k, chip in enumerate(_other_chips(x, y)):
            args = (_piece_of_half(halves[w], r, cc, ax, chip), got[w].at[k], send_sems.at[w, k], recv_sems.at[w, k], (*chip, c))
            copies.append(_start_remote(*args) if start else _remote(*args))
    return copies


def scatter_start(halves, group):
    ws = [BIG[i] for i in GROUPS[group]]
    n = len(ws)

    def body(*refs):
        sems = refs[2 * n:8 * n]
        _scatter_copies(refs[:n], refs[n:2 * n], ws, _SemList(sems[:3 * n]), _SemList(sems[3 * n:]), start=True)
        refs[-1][...] = jnp.zeros_like(refs[-1])

    landing = [lax.empty((3,) + _piece_shape(r, cc, ax), GRAD_COMM_DTYPE) for _, r, cc, ax in ws]
    out = pallas_call(
        body, name=f"scatter_start_{group}", in_specs=[HBM] * (2 * n),
        out_specs=[SEM] * (6 * n) + [HBM] * (2 * n) + [pl.BlockSpec(memory_space=pltpu.VMEM)],
        out_shape=[pltpu.SemaphoreType.DMA(())] * (6 * n)
        + [pltpu.HBM(_half_shape(r, cc, ax), GRAD_COMM_DTYPE) for _, r, cc, ax in ws]
        + [pltpu.HBM((3,) + _piece_shape(r, cc, ax), GRAD_COMM_DTYPE) for _, r, cc, ax in ws]
        + [jax.ShapeDtypeStruct((8, 128), F32)],
        input_output_aliases={i: 6 * n + i for i in range(2 * n)},
        compiler_params=pltpu.CompilerParams(has_side_effects=SPLIT_COPY_EFFECT),
    )(*[_in_hbm(h) for h in halves], *[_in_hbm(b) for b in landing])
    return out[:3 * n], out[3 * n:6 * n], out[6 * n:7 * n], out[7 * n:8 * n], out[-1]


def scatter_wait(halves, got, send_sems, recv_sems, after, group):
    ws = [BIG[i] for i in GROUPS[group]]
    n = len(ws)

    def body(*refs):
        sems = refs[2 * n:8 * n]
        for cp in _scatter_copies(refs[:n], refs[n:2 * n], ws, _SemList(sems[:3 * n]), _SemList(sems[3 * n:]), start=False):
            cp.wait_send()
            cp.wait_recv()

    out = pallas_call(
        body, name=f"scatter_wait_{group}", in_specs=[HBM] * (2 * n) + [SEM] * (6 * n) + [ANY] * len(after),
        out_specs=[HBM] * (2 * n),
        out_shape=[pltpu.HBM(_half_shape(r, cc, ax), GRAD_COMM_DTYPE) for _, r, cc, ax in ws]
        + [pltpu.HBM((3,) + _piece_shape(r, cc, ax), GRAD_COMM_DTYPE) for _, r, cc, ax in ws],
        input_output_aliases={i: i for i in range(2 * n)},
        compiler_params=pltpu.CompilerParams(has_side_effects=SPLIT_COPY_EFFECT),
    )(*halves, *got, *send_sems, *recv_sems, *after)
    return out[:n], out[n:]


N_DEV = 8
SMALL = ("ffn1_norm", "mix_norm", "hg_lower_bounds", "hg_out_norm", "ffn2_norm", "final_norm")
SMALL_STAGE_ROWS = 8


def small_step(loss, grads, w, m, v, behind):
    n = len(SMALL)
    shapes = [g.shape for g in grads]
    first_row = [sum(s[0] for s in shapes[:i]) for i in range(n + 1)]
    assert first_row[n] < SMALL_STAGE_ROWS
    loss_row = (pl.ds(first_row[n], 1), pl.ds(0, loss.shape[1]))

    def body(*refs):
        loss_ref, g_refs, w_refs, m_refs, v_refs = refs[0], refs[1:1 + n], refs[1 + n:1 + 2 * n], refs[1 + 2 * n:1 + 3 * n], refs[1 + 3 * n:1 + 4 * n]
        outs = refs[2 + 4 * n:3 + 8 * n]
        loss_out, dg_refs, d_refs, nm_refs, nv_refs = outs[0], outs[1:1 + n], outs[1 + n:1 + 2 * n], outs[1 + 2 * n:1 + 3 * n], outs[1 + 3 * n:]
        stage, gathered, send_sems, recv_sems = refs[3 + 8 * n:]
        x, y, c = _place()
        me = 4 * x + 2 * y + c

        def slot(i, shape):
            return pl.ds(first_row[i], shape[0]), pl.ds(0, shape[1])

        stage[...] = jnp.zeros_like(stage)
        for i, g_ref in enumerate(g_refs):
            stage[slot(i, shapes[i])] = g_ref[...]
        stage[loss_row] = loss_ref[pl.ds(0, 1), :]
        gathered[me] = stage[...]
        copies = []
        for k in range(1, N_DEV):
            peer = (x ^ (k >> 2), y ^ ((k >> 1) & 1), c ^ (k & 1))
            cp = pltpu.make_async_remote_copy(
                src_ref=stage, dst_ref=gathered.at[me], send_sem=send_sems.at[k - 1], recv_sem=recv_sems.at[k - 1],
                device_id=peer, device_id_type=MESH)
            cp.start()
            copies.append(cp)
        for cp in copies:
            cp.wait()
        acc = gathered[0]
        for k in range(1, N_DEV):
            acc = acc + gathered[k]
        stage[...] = acc
        loss_out[...] = jnp.broadcast_to(stage[loss_row], loss_out.shape)
        for i in range(n):
            g = stage[slot(i, shapes[i])]
            dg_refs[i][...] = g
            d_refs[i][...], nm_refs[i][...], nv_refs[i][...] = _adamw_math(w_refs[i][...], g, m_refs[i][...], v_refs[i][...])

    vm = pl.BlockSpec(memory_space=pltpu.VMEM)
    per_param = [jax.ShapeDtypeStruct(s, F32) for s in shapes]
    out = pallas_call(
        body, name="small_step", in_specs=[vm] * (1 + 4 * n) + [ANY], out_specs=[vm] * (1 + 4 * n),
        out_shape=[jax.ShapeDtypeStruct(loss.shape, F32)] + per_param * 4,
        scratch_shapes=[pltpu.VMEM((SMALL_STAGE_ROWS, D_MODEL), F32),
                        pltpu.VMEM((N_DEV, SMALL_STAGE_ROWS, D_MODEL), F32),
                        pltpu.SemaphoreType.DMA((N_DEV - 1,)), pltpu.SemaphoreType.DMA((N_DEV - 1,))],
    )(loss, *grads, *w, *m, *v, behind)
    return out[0], out[1:1 + n], out[1 + n:1 + 2 * n], out[1 + 2 * n:1 + 3 * n], out[1 + 3 * n:]


def _swiglu_block_fwd(h, n, w_gu, w_down, tag, behind=()):
    a, b, s = gate_up_swiglu(n, w_gu, f"{tag}_gate_up", behind=behind)
    h_out = matmul(s, w_down, res=h, scale=0.5, name=f"{tag}_down")
    return h_out, (n, a, b, s)


def _swiglu_block_bwd(h, norm_g, w_gu, w_down, saved, dh_out, df, tag, exchange, behind=()):
    n, a, b, s = saved
    d_down = matmul(s, df, ta=True, scale=0.5, out_dtype=GRAD_COMM_DTYPE, name=f"{tag}_d_w_down")
    ds = matmul(df, w_down, tb=True, scale=0.5, out_dtype=ACT_DTYPE, behind=behind, name=f"{tag}_d_s")
    dgu = swiglu_bwd(a, b, ds, f"{tag}_swiglu_bwd")
    d_gu = matmul(n, dgu, ta=True, out_dtype=GRAD_COMM_DTYPE, name=f"{tag}_d_w_gate_up")
    tokens = exchange.gradients_ready(tag, {f"{tag}_w_gate_up": d_gu, f"{tag}_w_down": d_down})
    dn = matmul(dgu, w_gu, tb=True, behind=tokens, name=f"{tag}_d_n")
    dh, dh_m, dg = rmsnorm_bwd(h, norm_g, dn, dh_out, f"{tag}_norm_bwd")
    return dh, dh_m, dg


def local_step(x, target, small, exchange):
    big = {}
    n1 = rmsnorm_fwd(x, small["ffn1_norm"], "ffn1_norm", behind=exchange.started)
    token, big_ffn1 = exchange.weights("ffn1", n1)
    big.update(big_ffn1)
    h1, saved1 = _swiglu_block_fwd(x, n1, big["ffn1_w_gate_up"], big["ffn1_w_down"], "ffn1", token)
    u = rmsnorm_fwd(h1, small["mix_norm"], "mix_norm")
    token, big_mix = exchange.weights("mix", u)
    big.update(big_mix)
    z = matmul(u, big["w_in"], behind=token, name="w_in")
    p = small["hg_lower_bounds"]
    lb = 1.0 / (1.0 + jnp.exp(p[1:2] - p[0:1]))
    y_hg, o_raw, states = hgrn_fwd(z, lb, small["hg_out_norm"], "hgrn_fwd")
    o_att, l_att = zip(*[att_fwd(z, g, f"att_fwd_{g}") for g in range(N_GROUPS)])
    y_att = att_combine_fwd(o_att, l_att, "att_combine")
    bh = matmul(y_hg, big["w_branch_hg"], name="branch_hg")
    ba = matmul(y_att, big["w_branch_att"], name="branch_att")
    merged = merge_fwd(z, bh, ba, "merge")
    h2 = matmul(merged, big["w_out"], res=h1, name="w_out")
    n2 = rmsnorm_fwd(h2, small["ffn2_norm"], "ffn2_norm")
    token, big_ffn2 = exchange.weights("ffn2", n2)
    big.update(big_ffn2)
    h3, saved2 = _swiglu_block_fwd(h2, n2, big["ffn2_w_gate_up"], big["ffn2_w_down"], "ffn2", token)
    dh3, dh3_m, d_final, loss = final_norm_loss(h3, small["final_norm"], target, "final_norm_loss")

    gs, gb = {"final_norm": d_final}, {}
    dh2, dh2_m, gs["ffn2_norm"] = _swiglu_block_bwd(
        h2, small["ffn2_norm"], big["ffn2_w_gate_up"], big["ffn2_w_down"], saved2, dh3, dh3_m, "ffn2", exchange)
    token = exchange.backward_done("ffn2", dh2)
    gb["w_out"] = matmul(merged, dh2_m, ta=True, out_dtype=GRAD_COMM_DTYPE, name="d_w_out")
    dmerged = matmul(dh2_m, big["w_out"], tb=True, behind=token, name="d_merged")
    dbh, dba, dgh, dga = merge_bwd(z, bh, ba, dmerged, "merge_bwd")
    gb["w_branch_hg"] = matmul(y_hg, dbh, ta=True, out_dtype=GRAD_COMM_DTYPE, name="d_w_branch_hg")
    gb["w_branch_att"] = matmul(y_att, dba, ta=True, out_dtype=GRAD_COMM_DTYPE, name="d_w_branch_att")
    dy_hg = matmul(dbh, big["w_branch_hg"], tb=True, name="d_y_hg")
    dy_att = matmul(dba, big["w_branch_att"], tb=True, name="d_y_att")
    dq, dfp, di, dog, d_lb, gs["hg_out_norm"] = hgrn_bwd(z, lb, small["hg_out_norm"], o_raw, states, dy_hg, "hgrn_bwd")
    do_att, corr = att_combine_bwd(o_att, l_att, dy_att, "att_combine_bwd")
    d_att = [part for g in range(N_GROUPS) for part in att_bwd(z, l_att[g], do_att[g], corr[g], g, f"att_bwd_{g}")]
    dz = jnp.concatenate([dq, dfp, di, dog, *d_att, dgh, dga], axis=1)
    gb["w_in"] = matmul(u, dz, ta=True, out_dtype=GRAD_COMM_DTYPE, name="d_w_in")
    token = exchange.gradients_ready("mix", gb)
    du = matmul(dz, big["w_in"], tb=True, behind=token, name="d_u")
    dh1, dh1_m, gs["mix_norm"] = rmsnorm_bwd(h1, small["mix_norm"], du, dh2, "mix_norm_bwd")
    token = exchange.backward_done("mix", dh1)
    dp0 = d_lb * lb * (1.0 - lb)
    gs["hg_lower_bounds"] = jnp.concatenate([dp0, -dp0], axis=0)
    dx, _, gs["ffn1_norm"] = _swiglu_block_bwd(
        x, small["ffn1_norm"], big["ffn1_w_gate_up"], big["ffn1_w_down"], saved1, dh1, dh1_m, "ffn1", exchange, token)
    exchange.backward_done("ffn1", dx)
    return loss, dx, gs


WEIGHTS = ("ffn1_norm", "ffn1_w_gate_up", "ffn1_w_down", "mix_norm", "w_in", "hg_lower_bounds", "hg_out_norm",
           "w_branch_hg", "w_branch_att", "w_out", "ffn2_norm", "ffn2_w_gate_up", "ffn2_w_down", "final_norm")


class WeightExchange:
    ORDER = ("ffn1", "mix", "ffn2")

    def __init__(self, shards, core, chip):
        self.core, self.chip = core, chip
        self.halving = None
        self.scattering = None
        self.reducing = {}
        first = self.ORDER[0]
        self.placed = {BIG[i][0]: place_own_block(shards[BIG[i][0]], chip, *BIG[i][1:], f"place_{BIG[i][0]}")
                       for i in GROUPS[first]}
        self._start_gather(first, chip)
        self.started = [self.token]
        chip_behind = chip + self.token[0, :1].astype(jnp.int32)
        for group in self.ORDER[1:]:
            for i in GROUPS[group]:
                n, r, cc, ax = BIG[i]
                self.placed[n] = place_own_block(shards[n], chip_behind, r, cc, ax, f"place_{n}")
        self.placed_behind = [self.placed[n] for group in self.ORDER[1:] for n in self._names(group)]

    def _names(self, group):
        return [BIG[i][0] for i in GROUPS[group]]

    def _start_gather(self, group, after):
        send_sems, recv_sems, bufs, self.token = gather_start([self.placed[n] for n in self._names(group)], after, group)
        self.gathering = (group, send_sems, recv_sems, bufs)

    def weights(self, group, h):
        pending, send_sems, recv_sems, bufs = self.gathering
        assert pending == group
        after = [h] + (self.placed_behind if group == self.ORDER[0] else [])
        whole = gather_forward(gather_wait(bufs, send_sems, recv_sems, after, group), group)
        later = self.ORDER.index(group) + 1
        behind = []
        if later < len(self.ORDER):
            self._start_gather(self.ORDER[later], whole[0])
            behind = [self.token]
        return behind, dict(zip(self._names(group), whole))

    @staticmethod
    def _half_to_sibling(ws):
        return lambda ref, w, c: _half(ref, *ws[w][1:], 1 - c)

    def gradients_ready(self, group, grads):
        ws = [BIG[i] for i in GROUPS[group]]
        send_sems, recv_sems, own, theirs, token = sibling_exchange_start(
            [grads[n] for n, *_ in ws], self._half_to_sibling(ws), [_half_shape(r, cc, ax) for _, r, cc, ax in ws],
            GRAD_COMM_DTYPE, f"halves_start_{group}")
        self.halving = (group, send_sems, recv_sems, own, theirs)
        return [token]

    def backward_done(self, group, dh):
        pending, send_sems, recv_sems, own, theirs = self.halving
        assert pending == group
        ws = [BIG[i] for i in GROUPS[group]]
        own, theirs = sibling_exchange_wait(own, theirs, send_sems, recv_sems, self._half_to_sibling(ws), [dh],
                                            f"halves_wait_{group}")
        halves = [add_halves(g, t, self.core, r, cc, ax, f"add_halves_{n}") for (n, r, cc, ax), g, t in zip(ws, own, theirs)]
        previous = self.scattering
        send_sems, recv_sems, halves, got, self.token = scatter_start(halves, group)
        self.scattering = (group, send_sems, recv_sems, halves, got)
        behind = [self._finish_scatter(previous, [self.token])] if previous is not None else []
        return behind + [self.token]

    def _finish_scatter(self, scattering, after):
        group, send_sems, recv_sems, halves, got = scattering
        halves, got = scatter_wait(halves, got, send_sems, recv_sems, after, group)
        ws = [BIG[i] for i in GROUPS[group]]
        mine = [add_pieces(h, g, self.chip, r, cc, ax, f"add_pieces_{n}") for (n, r, cc, ax), h, g in zip(ws, halves, got)]
        send_sems, recv_sems, mine, theirs, token = sibling_exchange_start(
            mine, lambda ref, w, c: ref, [_piece_shape(r, cc, ax) for _, r, cc, ax in ws], F32, f"reduced_start_{group}")
        self.reducing[group] = (send_sems, recv_sems, mine, theirs)
        return token

    def finish(self, after):
        return self._finish_scatter(self.scattering, after)

    def reduced_halves(self, group, after):
        send_sems, recv_sems, mine, theirs = self.reducing.pop(group)
        mine, theirs = sibling_exchange_wait(mine, theirs, send_sems, recv_sems, lambda ref, w, c: ref, after,
                                             f"reduced_wait_{group}")
        return {BIG[i][0]: (a, b) for i, a, b in zip(GROUPS[group], mine, theirs)}


def kernel(x, ffn1_norm, ffn1_w_gate_up, ffn1_w_down, mix_norm, w_in, hg_lower_bounds, hg_out_norm, w_branch_hg, w_branch_att, w_out, ffn2_norm, ffn2_w_gate_up, ffn2_w_down, final_norm, loss_target, m_ffn1_norm, m_ffn1_w_gate_up, m_ffn1_w_down, m_mix_norm, m_w_in, m_hg_lower_bounds, m_hg_out_norm, m_w_branch_hg, m_w_branch_att, m_w_out, m_ffn2_norm, m_ffn2_w_gate_up, m_ffn2_w_down, m_final_norm, v_ffn1_norm, v_ffn1_w_gate_up, v_ffn1_w_down, v_mix_norm, v_w_in, v_hg_lower_bounds, v_hg_out_norm, v_w_branch_hg, v_w_branch_att, v_w_out, v_ffn2_norm, v_ffn2_w_gate_up, v_ffn2_w_down, v_final_norm):
    w = dict(ffn1_norm=ffn1_norm, ffn1_w_gate_up=ffn1_w_gate_up, ffn1_w_down=ffn1_w_down, mix_norm=mix_norm, w_in=w_in,
             hg_lower_bounds=hg_lower_bounds, hg_out_norm=hg_out_norm, w_branch_hg=w_branch_hg, w_branch_att=w_branch_att,
             w_out=w_out, ffn2_norm=ffn2_norm, ffn2_w_gate_up=ffn2_w_gate_up, ffn2_w_down=ffn2_w_down, final_norm=final_norm)
    m = dict(ffn1_norm=m_ffn1_norm, ffn1_w_gate_up=m_ffn1_w_gate_up, ffn1_w_down=m_ffn1_w_down, mix_norm=m_mix_norm,
             w_in=m_w_in, hg_lower_bounds=m_hg_lower_bounds, hg_out_norm=m_hg_out_norm, w_branch_hg=m_w_branch_hg,
             w_branch_att=m_w_branch_att, w_out=m_w_out, ffn2_norm=m_ffn2_norm, ffn2_w_gate_up=m_ffn2_w_gate_up,
             ffn2_w_down=m_ffn2_w_down, final_norm=m_final_norm)
    v = dict(ffn1_norm=v_ffn1_norm, ffn1_w_gate_up=v_ffn1_w_gate_up, ffn1_w_down=v_ffn1_w_down, mix_norm=v_mix_norm,
             w_in=v_w_in, hg_lower_bounds=v_hg_lower_bounds, hg_out_norm=v_hg_out_norm, w_branch_hg=v_w_branch_hg,
             w_branch_att=v_w_branch_att, w_out=v_w_out, ffn2_norm=v_ffn2_norm, ffn2_w_gate_up=v_ffn2_w_gate_up,
             ffn2_w_down=v_ffn2_w_down, final_norm=v_final_norm)

    core = lax.axis_index("c").astype(jnp.int32).reshape(1)
    chip = (2 * lax.axis_index("x") + lax.axis_index("y")).astype(jnp.int32).reshape(1)
    exchange = WeightExchange({n: w[n][0] for n, *_ in BIG}, core, chip)
    small = {n: w[n] for n in SMALL}
    small["final_norm"] = final_norm.reshape(1, D_MODEL)

    loss, dx, gs = local_step(x[0], loss_target[0], small, exchange)

    grads, delta, new_m, new_v = {}, {}, {}, {}

    def update(group, core, after):
        reduced = exchange.reduced_halves(group, after)
        for i in GROUPS[group]:
            n, r, cc, ax = BIG[i]
            a, b = reduced[n]
            g, d, nm, nv = adamw_halves(w[n][0], a, b, m[n][0], v[n][0], core, r, cc, ax, f"adamw_{n}")
            grads[n], delta[n], new_m[n], new_v[n] = g[None], d[None], nm[None], nv[None]

    core_behind = core + exchange.token[0, :1].astype(jnp.int32)
    update("ffn2", core_behind, [exchange.token])
    update("mix", core_behind, [delta["ffn2_w_down"]])
    token = exchange.finish(after=[delta[BIG[i][0]] for group in ("ffn2", "mix") for i in GROUPS[group]])
    two_d = lambda a: a.reshape(1, D_MODEL) if a.ndim == 1 else a
    loss_sum, *small_out = small_step(loss, [gs[n] for n in SMALL], *[[two_d(p[n]) for n in SMALL] for p in (w, m, v)],
                                      behind=token)
    for result, parts in zip((grads, delta, new_m, new_v), small_out):
        result.update({n: a.reshape(w[n].shape) for n, a in zip(SMALL, parts)})
    update("ffn1", core, [loss_sum])

    return (loss_sum[0, 0], dx[None], *[grads[n] for n in WEIGHTS], *[delta[n] for n in WEIGHTS],
            *[new_m[n] for n in WEIGHTS], *[new_v[n] for n in WEIGHTS])
```

```python
import numpy as np
import jax
import jax.numpy as jnp
from jax import lax
from jax.experimental import pallas as pl
from jax.experimental.pallas import tpu as pltpu

SEQ = 2048
D_MODEL = 1024
D_FF = 2816
HG_HEADS = 4
HG_DIM = 128
HG_WIDTH = 512
HG_CHUNK = 64
ATT_GROUPS = ((128, 1), (512, 4), (2048, 16))
ATT_HEADS = 8
ATT_WIDTH = 512
ATT_BLOCK = 128
ALIBI_MAX = 8.0
IN_COLS = 8704
EPS = 1e-6
NEG_INF = -1e30
ADAM_LR = 0.001
ADAM_B1 = 0.9
ADAM_B2 = 0.999
ADAM_EPS = 1e-08
ADAM_WD = 0.01
ADAM_STEP = 10

N_CHIPS = 4
MXU_DTYPE = jnp.bfloat16
WEIGHT_COMM_DTYPE = jnp.bfloat16
GRAD_COMM_DTYPE = jnp.bfloat16
ACT_DTYPE = jnp.bfloat16
MESH = pl.DeviceIdType.MESH
F32 = jnp.float32


def _sigmoid(x):
    return 1.0 / (1.0 + jnp.exp(-x))


def _dot(a, b, ta=False, tb=False):
    dn = (((0 if ta else 1,), (1 if tb else 0,)), ((), ()))
    return lax.dot_general(a.astype(MXU_DTYPE), b.astype(MXU_DTYPE), dn, preferred_element_type=F32)


def _dot_f32(a, b, ones_on_right=False):
    x = a if ones_on_right else b
    hi = x.astype(jnp.bfloat16)
    rest = x - hi.astype(F32)
    mid = rest.astype(jnp.bfloat16)
    lo = (rest - mid.astype(F32)).astype(jnp.bfloat16)
    if ones_on_right:
        dot = lambda q: jnp.dot(q, b.astype(jnp.bfloat16), preferred_element_type=F32)
    else:
        dot = lambda q: jnp.dot(a.astype(jnp.bfloat16), q, preferred_element_type=F32)
    return dot(hi) + (dot(mid) + dot(lo))


def _split_bf16(x):
    hi = x.astype(jnp.bfloat16)
    return hi, (x - hi.astype(F32)).astype(jnp.bfloat16)


def _hdot(a, b, ta=False, tb=False):
    dn =(((0 if ta else 1,), (1 if tb else 0,)), ((), ()))
    (a_hi, a_lo), (b_hi, b_lo) = _split_bf16(a), _split_bf16(b)
    dot = lambda p, q: lax.dot_general(p, q, dn, preferred_element_type=F32)
    return dot(a_hi, b_hi) + (dot(a_lo, b_hi) + dot(a_hi, b_lo))


def _in_hbm(a):
    return pltpu.with_memory_space_constraint(a, pltpu.HBM)


def pallas_call(body, **kw):
    grid_spec = kw.get("grid_spec")
    specs = list(kw["in_specs"] if grid_spec is None else grid_spec.in_specs)
    n_prefetch = 0 if grid_spec is None else grid_spec.num_scalar_prefetch
    out_specs = kw["out_specs"] if grid_spec is None else grid_spec.out_specs
    one = not isinstance(kw["out_shape"], (list, tuple))
    shapes = [kw["out_shape"]] if one else list(kw["out_shape"])
    out_specs = [out_specs] if one else list(out_specs)
    shapes = [pltpu.HBM(a.shape, a.dtype) if s.memory_space is None and isinstance(a, jax.ShapeDtypeStruct) else a
              for a, s in zip(shapes, out_specs)]
    kw["out_shape"] = shapes[0] if one else shapes
    call = pl.pallas_call(body, **kw)

    def run(*args):
        assert len(args) == n_prefetch + len(specs)
        pinned = list(args[n_prefetch:])
        return call(*args[:n_prefetch], *pinned)

    return run


MATMUL_VMEM_BYTES = 48 * 1024 * 1024
MATMUL_TILE_BYTES = 36 * 1024 * 1024
MXU_ALIGN = 128
MXU_FLOPS_PER_SECOND = 900e12
HBM_BYTES_PER_SECOND = 3.0e12
GRID_STEP_SECONDS = 0.35e-6


def _divisors(n, most):
    return [t for t in range(min(n, most), 0, -MXU_ALIGN) if n % t == 0 and t % MXU_ALIGN == 0]


def _matmul_tiles(M, N, K, in_bytes, out_bytes, has_res):
    best = None
    for tk in _divisors(K, K):
        nk = K // tk
        for tm in _divisors(M, 2048):
            for tn in _divisors(N, 512):
                tiles = 2 * in_bytes * (tm * tk + tk * tn) + 2 * out_bytes * tm * tn
                tiles += 4 * tm * tn * ((nk > 1) + 2 * has_res)
                if tiles > MATMUL_TILE_BYTES:
                    continue
                traffic = in_bytes * (M * K * (1 if nk == 1 else N // tn) + K * N * (M // tm)) + out_bytes * M * N
                exposed = in_bytes * (tm * tk + tk * tn) + out_bytes * tm * tn
                seconds = (max(2 * M * N * K / MXU_FLOPS_PER_SECOND, traffic / HBM_BYTES_PER_SECOND)
                           + exposed / HBM_BYTES_PER_SECOND + (M // tm) * (N // tn) * nk * GRID_STEP_SECONDS)
                key = (seconds, -tm * tn * tk)
                if best is None or key < best[0]:
                    best = (key, (tm, tn, tk))
    return best[1]


def matmul(a, b, *, ta=False, tb=False, out_dtype=F32, res=None, scale=1.0, behind=(), name):
    if ta:
        K, M = a.shape
    else:
        M, K = a.shape
    if tb:
        N, K2 = b.shape
    else:
        K2, N = b.shape
    assert K == K2 and a.dtype == b.dtype
    tm, tn, tk = _matmul_tiles(M, N, K, a.dtype.itemsize, jnp.dtype(out_dtype).itemsize, res is not None)
    nk = K // tk

    def finish(r, r_ref, o_ref):
        if scale != 1.0:
            r = r * scale
        if res is not None:
            r = r_ref[...] + r
        o_ref[...] = r.astype(out_dtype)

    def body(*refs):
        a_ref, b_ref = refs[:2]
        r_ref = refs[2] if res is not None else None
        o_ref = refs[2 + (res is not None) + len(behind)]
        if nk == 1:
            finish(_dot(a_ref[...], b_ref[...], ta, tb), r_ref, o_ref)
            return
        acc = refs[-1]
        k = pl.program_id(2)

        @pl.when(k == 0)
        def _():
            acc[...] = jnp.zeros_like(acc)

        acc[...] += _dot(a_ref[...], b_ref[...], ta, tb)

        @pl.when(k == nk - 1)
        def _():
            finish(acc[...], r_ref, o_ref)

    a_spec = pl.BlockSpec((tk, tm), lambda i, j, k: (k, i)) if ta else pl.BlockSpec((tm, tk), lambda i, j, k: (i, k))
    b_spec = pl.BlockSpec((tn, tk), lambda i, j, k: (j, k)) if tb else pl.BlockSpec((tk, tn), lambda i, j, k: (k, j))
    in_specs = [a_spec, b_spec]
    args = [a, b]
    if res is not None:
        in_specs.append(pl.BlockSpec((tm, tn), lambda i, j, k: (i, j)))
        args.append(res)
    for earlier in behind:
        in_specs.append(pl.BlockSpec(memory_space=pl.ANY))
        args.append(earlier)
    return pallas_call(
        body, name=name, grid=(M // tm, N // tn, nk), in_specs=in_specs,
        out_specs=pl.BlockSpec((tm, tn), lambda i, j, k: (i, j)),
        out_shape=jax.ShapeDtypeStruct((M, N), out_dtype),
        scratch_shapes=[pltpu.VMEM((tm, tn), F32)] if nk > 1 else [],
        compiler_params=pltpu.CompilerParams(dimension_semantics=("parallel", "parallel", "arbitrary"),
                                             vmem_limit_bytes=MATMUL_VMEM_BYTES),
    )(*args)


ROW_TILE = 256


def rmsnorm_fwd(x, g, name, behind=()):
    def body(x_ref, g_ref, *refs):
        n_ref = refs[-1]
        xv = x_ref[...]
        r = lax.rsqrt(jnp.mean(xv * xv, axis=-1, keepdims=True) + EPS)
        n_ref[...] = ((xv * r) * g_ref[...]).astype(n_ref.dtype)

    order = list(behind)
    return pallas_call(
        body, name=name, grid=(SEQ // ROW_TILE,),
        in_specs=[pl.BlockSpec((ROW_TILE, D_MODEL), lambda i: (i, 0)), pl.BlockSpec((1, D_MODEL), lambda i: (0, 0))]
        + [pl.BlockSpec(memory_space=pl.ANY)] * len(order),
        out_specs=pl.BlockSpec((ROW_TILE, D_MODEL), lambda i: (i, 0)),
        out_shape=jax.ShapeDtypeStruct((SEQ, D_MODEL), MXU_DTYPE),
    )(x, g, *order)


def rmsnorm_bwd(x, g, dn, dres, name):
    def body(x_ref, g_ref, dn_ref, dr_ref, dx_ref, dxm_ref, dg_ref):
        xv = x_ref[...]
        r = lax.rsqrt(jnp.mean(xv * xv, axis=-1, keepdims=True) + EPS)
        xh = xv * r
        dnv = dn_ref[...]

        @pl.when(pl.program_id(0) == 0)
        def _():
            dg_ref[...] = jnp.zeros_like(dg_ref)

        dg_ref[...] += jnp.sum(dnv * xh, axis=0, keepdims=True)
        dxh = dnv * g_ref[...]
        dx = dr_ref[...] + r * (dxh - xh * jnp.mean(dxh * xh, axis=-1, keepdims=True))
        dx_ref[...] = dx
        dxm_ref[...] = dx.astype(dxm_ref.dtype)

    row = pl.BlockSpec((ROW_TILE, D_MODEL), lambda i: (i, 0))
    vec = pl.BlockSpec((1, D_MODEL), lambda i: (0, 0))
    return pallas_call(
        body, name=name, grid=(SEQ // ROW_TILE,), in_specs=[row, vec, row, row], out_specs=[row, row, vec],
        out_shape=[jax.ShapeDtypeStruct((SEQ, D_MODEL), F32), jax.ShapeDtypeStruct((SEQ, D_MODEL), MXU_DTYPE),
                   jax.ShapeDtypeStruct((1, D_MODEL), F32)],
        compiler_params=pltpu.CompilerParams(dimension_semantics=("arbitrary",)),
    )(x, g, dn, dres)


def final_norm_loss(h, g, target, name):
    def body(h_ref, g_ref, t_ref, dh_ref, dhm_ref, dg_ref, loss_ref):
        xv = h_ref[...]
        r = lax.rsqrt(jnp.mean(xv * xv, axis=-1, keepdims=True) + EPS)
        xh = xv * r
        gv = g_ref[...]
        e = xh * gv - t_ref[...]

        @pl.when(pl.program_id(0) == 0)
        def _():
            dg_ref[...] = jnp.zeros_like(dg_ref)
            loss_ref[...] = jnp.zeros_like(loss_ref)

        part = 0.5 * jnp.sum(jnp.sum(e * e, axis=-1, keepdims=True) * (1.0 / D_MODEL), axis=0, keepdims=True)
        loss_ref[...] += jnp.broadcast_to(part, loss_ref.shape)
        dout = e * (1.0 / D_MODEL)
        dg_ref[...] += jnp.sum(dout * xh, axis=0, keepdims=True)
        dxh = dout * gv
        dh = r * (dxh - xh * jnp.mean(dxh * xh, axis=-1, keepdims=True))
        dh_ref[...] = dh
        dhm_ref[...] = dh.astype(dhm_ref.dtype)

    row = pl.BlockSpec((ROW_TILE, D_MODEL), lambda i: (i, 0))
    vec = pl.BlockSpec((1, D_MODEL), lambda i: (0, 0))
    return pallas_call(
        body, name=name, grid=(SEQ // ROW_TILE,), in_specs=[row, vec, row],
        out_specs=[row, row, vec, pl.BlockSpec((8, 128), lambda i: (0, 0))],
        out_shape=[jax.ShapeDtypeStruct((SEQ, D_MODEL), F32), jax.ShapeDtypeStruct((SEQ, D_MODEL), MXU_DTYPE),
                   jax.ShapeDtypeStruct((1, D_MODEL), F32), jax.ShapeDtypeStruct((8, 128), F32)],
        compiler_params=pltpu.CompilerParams(dimension_semantics=("arbitrary",)),
    )(h, g, target)


FFN_TILE = 256
FFN_TILES = D_FF // FFN_TILE


def gate_up_swiglu(n, w_gu, name, behind=()):
    def body(n_ref, wa_ref, wb_ref, *refs):
        a_ref, b_ref, s_ref = refs[len(behind):]
        nv = n_ref[...]
        a = _dot(nv, wa_ref[...])
        b = _dot(nv, wb_ref[...])
        a_ref[...] = a.astype(a_ref.dtype)
        b_ref[...] = b.astype(b_ref.dtype)
        s_ref[...] = (a * _sigmoid(a) * b).astype(s_ref.dtype)

    tile = pl.BlockSpec((SEQ, FFN_TILE), lambda j: (0, j))
    act = jax.ShapeDtypeStruct((SEQ, D_FF), ACT_DTYPE)
    return pallas_call(
        body, name=name, grid=(FFN_TILES,),
        in_specs=[pl.BlockSpec((SEQ, D_MODEL), lambda j: (0, 0)), pl.BlockSpec((D_MODEL, FFN_TILE), lambda j: (0, j)),
                  pl.BlockSpec((D_MODEL, FFN_TILE), lambda j: (0, j + FFN_TILES))]
        + [pl.BlockSpec(memory_space=pl.ANY)] * len(behind),
        out_specs=[tile, tile, tile], out_shape=[act, act, jax.ShapeDtypeStruct((SEQ, D_FF), MXU_DTYPE)],
        compiler_params=pltpu.CompilerParams(dimension_semantics=("parallel",), vmem_limit_bytes=MATMUL_VMEM_BYTES),
    )(n, w_gu, w_gu, *behind)


def swiglu_bwd(a, b, ds, name):
    rows = ROW_TILE // 2

    def body(a_ref, b_ref, ds_ref, o_ref):
        av = a_ref[...].astype(F32)
        sg = _sigmoid(av)
        dsv = ds_ref[...].astype(F32)
        o_ref[:, :D_FF] = (dsv * b_ref[...].astype(F32) * (sg * (1.0 + av * (1.0 - sg)))).astype(o_ref.dtype)
        o_ref[:, D_FF:] = (dsv * av * sg).astype(o_ref.dtype)

    blk = pl.BlockSpec((rows, D_FF), lambda i: (i, 0))
    return pallas_call(
        body, name=name, grid=(SEQ // rows,), in_specs=[blk, blk, blk],
        out_specs=pl.BlockSpec((rows, 2 * D_FF), lambda i: (i, 0)),
        out_shape=jax.ShapeDtypeStruct((SEQ, 2 * D_FF), MXU_DTYPE), compiler_params=SUM_PARAMS,
    )(a, b, ds)


GATE_HG_BLK = 6656 // 512
GATE_ATT_BLK = 7680 // 512


def merge_fwd(z, bh, ba, name):
    def body(gh_ref, ga_ref, bh_ref, ba_ref, o_ref):
        o_ref[...] = (_sigmoid(gh_ref[...]) * bh_ref[...] + _sigmoid(ga_ref[...]) * ba_ref[...]).astype(o_ref.dtype)

    blk = pl.BlockSpec((ROW_TILE, 512), lambda i, j: (i, j))
    return pallas_call(
        body, name=name, grid=(SEQ // ROW_TILE, 2),
        in_specs=[pl.BlockSpec((ROW_TILE, 512), lambda i, j: (i, GATE_HG_BLK + j)),
                  pl.BlockSpec((ROW_TILE, 512), lambda i, j: (i, GATE_ATT_BLK + j)), blk, blk],
        out_specs=blk, out_shape=jax.ShapeDtypeStruct((SEQ, D_MODEL), MXU_DTYPE),
    )(z, z, bh, ba)


def merge_bwd(z, bh, ba, dm, name):
    def body(gh_ref, ga_ref, bh_ref, ba_ref, dm_ref, dbh_ref, dba_ref, dgh_ref, dga_ref):
        dmv = dm_ref[...]
        sh = _sigmoid(gh_ref[...])
        sa = _sigmoid(ga_ref[...])
        dbh_ref[...] = (dmv * sh).astype(dbh_ref.dtype)
        dba_ref[...] = (dmv * sa).astype(dba_ref.dtype)
        dgh_ref[...] = (dmv * bh_ref[...] * (sh * (1.0 - sh))).astype(dgh_ref.dtype)
        dga_ref[...] = (dmv * ba_ref[...] * (sa * (1.0 - sa))).astype(dga_ref.dtype)

    blk = pl.BlockSpec((ROW_TILE, 512), lambda i, j: (i, j))
    out = jax.ShapeDtypeStruct((SEQ, D_MODEL), MXU_DTYPE)
    return pallas_call(
        body, name=name, grid=(SEQ // ROW_TILE, 2),
        in_specs=[pl.BlockSpec((ROW_TILE, 512), lambda i, j: (i, GATE_HG_BLK + j)),
                  pl.BlockSpec((ROW_TILE, 512), lambda i, j: (i, GATE_ATT_BLK + j)), blk, blk, blk],
        out_specs=[blk, blk, blk, blk], out_shape=[out, out, out, out],
    )(z, z, bh, ba, dm)


N_CHUNKS = SEQ // HG_CHUNK
HG_STEP_CHUNKS = 4


def _hgrn_gates(q, fp, lb):
    C = HG_CHUNK
    sg = _sigmoid(fp)
    f = lb + (1.0 - lb) * sg
    lf = jnp.log(f)
    row = lax.broadcasted_iota(jnp.int32, (C, C), 0)
    col = lax.broadcasted_iota(jnp.int32, (C, C), 1)
    causal = row >= col
    G = _dot_f32(causal.astype(F32), lf)
    eG = jnp.exp(G)
    enG = jnp.exp(-G)
    qg = q * eG
    kg = (1.0 - f) * enG
    A = jnp.where(causal, _hdot(qg, kg, tb=True), 0.0)
    egl = jnp.exp(jnp.sum(lf, axis=0, keepdims=True))
    return sg, f, causal, eG, enG, qg, kg, A, egl


def hgrn_fwd(z, lb, gain, name):
    C, K = HG_CHUNK, HG_DIM

    def body(q_ref, f_ref, v_ref, og_ref, p_ref, g_ref, y_ref, o_ref, st_ref, state):
        @pl.when(pl.program_id(0) == 0)
        def _():
            state[...] = jnp.zeros_like(state)

        for cc in range(HG_STEP_CHUNKS):
            rows = pl.ds(cc * C, C)
            for h in range(HG_HEADS):
                hd = pl.ds(h * K, K)
                v = v_ref[rows, hd]
                _, _, _, _, _, qg, kg, A, egl = _hgrn_gates(q_ref[rows, hd], f_ref[rows, hd], p_ref[:, hd])
                st = state[h]
                st_ref[h, cc] = st
                o = _hdot(A, v) + _hdot(qg, st, tb=True)
                state[h] = st * egl + _hdot(v, kg * egl, ta=True)
                o_ref[rows, hd] = o
                rs = lax.rsqrt(jnp.mean(o * o, axis=-1, keepdims=True) + EPS)
                og = og_ref[rows, hd]
                y_ref[rows, hd] = (((o * rs) * g_ref[:, hd]) * (og * _sigmoid(og))).astype(y_ref.dtype)

    R = HG_STEP_CHUNKS * C

    def zcol(section):
        return pl.BlockSpec((R, HG_WIDTH), lambda c: (c, section))

    vec = pl.BlockSpec((1, HG_WIDTH), lambda c: (0, 0))
    blk = pl.BlockSpec((R, HG_WIDTH), lambda c: (c, 0))
    return pallas_call(
        body, name=name, grid=(N_CHUNKS // HG_STEP_CHUNKS,),
        in_specs=[zcol(0), zcol(1), zcol(2), zcol(3), vec, vec],
        out_specs=[blk, blk, pl.BlockSpec((HG_HEADS, HG_STEP_CHUNKS, K, K), lambda c: (0, c, 0, 0))],
        out_shape=[jax.ShapeDtypeStruct((SEQ, HG_WIDTH), MXU_DTYPE), jax.ShapeDtypeStruct((SEQ, HG_WIDTH), F32),
                   jax.ShapeDtypeStruct((HG_HEADS, N_CHUNKS, K, K), F32)],
        scratch_shapes=[pltpu.VMEM((HG_HEADS, K, K), F32)],
        compiler_params=pltpu.CompilerParams(dimension_semantics=("arbitrary",)),
    )(z, z, z, z, lb, gain)


def hgrn_bwd(z, lb, gain, o_raw, states, dy, name):
    C, K = HG_CHUNK, HG_DIM

    def body(q_ref, f_ref, v_ref, og_ref, p_ref, g_ref, o_ref, st_ref, dy_ref,
             dq_ref, dfp_ref, dv_ref, dog_ref, dlb_ref, dgain_ref, dstate):
        @pl.when(pl.program_id(0) == 0)
        def _():
            dstate[...] = jnp.zeros_like(dstate)
            dlb_ref[...] = jnp.zeros_like(dlb_ref)
            dgain_ref[...] = jnp.zeros_like(dgain_ref)

        last = lax.broadcasted_iota(jnp.int32, (C, K), 0) == C - 1
        row = lax.broadcasted_iota(jnp.int32, (C, C), 0)
        col = lax.broadcasted_iota(jnp.int32, (C, C), 1)
        anti_causal = (col >= row).astype(F32)
        for cc in reversed(range(HG_STEP_CHUNKS)):
            rows = pl.ds(cc * C, C)
            for h in range(HG_HEADS):
                hd = pl.ds(h * K, K)
                v = v_ref[rows, hd]
                lb = p_ref[:, hd]
                sg, f, causal, eG, enG, qg, kg, A, egl = _hgrn_gates(q_ref[rows, hd], f_ref[rows, hd], lb)
                kd = kg * egl
                st = st_ref[h, cc]
                dst = dstate[h]
                o = o_ref[rows, hd]
                og = og_ref[rows, hd]
                gain_v = g_ref[:, hd]
                dyv = dy_ref[rows, hd]
                rs = lax.rsqrt(jnp.mean(o * o, axis=-1, keepdims=True) + EPS)
                on = o * rs
                sgo = _sigmoid(og)
                silu = og * sgo
                dog_ref[rows, hd] = (dyv * (on * gain_v) * (sgo * (1.0 + og * (1.0 - sgo)))).astype(dog_ref.dtype)
                dgain_ref[:, hd] += jnp.sum(dyv * silu * on, axis=0, keepdims=True)
                don = dyv * gain_v * silu
                do = rs * (don - on * jnp.mean(don * on, axis=-1, keepdims=True))
                dA = jnp.where(causal, _hdot(do, v, tb=True), 0.0)
                dv_ref[rows, hd] = (_hdot(A, do, ta=True) + _hdot(kd, dst, tb=True)).astype(dv_ref.dtype)
                dqg = _hdot(dA, kg) + _hdot(do, st)
                dkg = _hdot(dA, qg, ta=True)
                dkd = _hdot(v, dst)
                dstate[h] = dst * egl + _hdot(do, qg, ta=True)
                dgl = jnp.sum(st * dst, axis=0, keepdims=True) * egl
                dq_ref[rows, hd] = (dqg * eG).astype(dq_ref.dtype)
                dk = dkg * enG + dkd * (enG * egl)
                dG = dqg * qg - dkg * kg - dkd * kd
                extra = jnp.sum(dkd * kd, axis=0, keepdims=True) + dgl
                dG = dG + jnp.where(last, extra, 0.0)
                dlf = _dot_f32(anti_causal, dG)
                df = dlf / f - dk
                dfp_ref[rows, hd] = (df * (1.0 - lb) * (sg * (1.0 - sg))).astype(dfp_ref.dtype)
                dlb_ref[:, hd] += jnp.sum(df * (1.0 - sg), axis=0, keepdims=True)

    R = HG_STEP_CHUNKS * C
    n_steps = N_CHUNKS // HG_STEP_CHUNKS

    def rc(c):
        return n_steps - 1 - c

    def zcol(section):
        return pl.BlockSpec((R, HG_WIDTH), lambda c: (rc(c), section))

    vec = pl.BlockSpec((1, HG_WIDTH), lambda c: (0, 0))
    blk = pl.BlockSpec((R, HG_WIDTH), lambda c: (rc(c), 0))
    out = jax.ShapeDtypeStruct((SEQ, HG_WIDTH), MXU_DTYPE)
    small = jax.ShapeDtypeStruct((1, HG_WIDTH), F32)
    return pallas_call(
        body, name=name, grid=(n_steps,),
        in_specs=[zcol(0), zcol(1), zcol(2), zcol(3), vec, vec, blk,
                  pl.BlockSpec((HG_HEADS, HG_STEP_CHUNKS, K, K), lambda c: (0, rc(c), 0, 0)), blk],
        out_specs=[blk, blk, blk, blk, vec, vec],
        out_shape=[out, out, out, out, small, small],
        scratch_shapes=[pltpu.VMEM((HG_HEADS, K, K), F32)],
        compiler_params=pltpu.CompilerParams(dimension_semantics=("arbitrary",)),
    )(z, z, z, z, lb, gain, o_raw, states, dy)


N_GROUPS = len(ATT_GROUPS)
HEAD_PAIRS = ATT_WIDTH // 128
ATT_COL0 = 4 * HG_WIDTH
UNROLLED_UNITS = 4


def _alibi_coef():
    n = N_GROUPS * ATT_HEADS
    slopes = np.exp2(-ALIBI_MAX * np.arange(1, n + 1, dtype=np.float32) / n).astype(np.float32)
    dil = np.repeat(np.array([d for _, d in ATT_GROUPS], np.float32), ATT_HEADS)
    return jnp.asarray(slopes * dil, F32)


def _for_each_unit(n, fn):
    if n <= UNROLLED_UNITS:
        for u in range(n):
            fn(u)
    else:
        def group(i, carry):
            for j in range(UNROLLED_UNITS):
                fn(i * UNROLLED_UNITS + j)
            return carry
        lax.fori_loop(0, n // UNROLLED_UNITS, group, 0)


def _att_geometry(g):
    B = ATT_BLOCK
    d = ATT_GROUPS[g][1]
    n_blocks = SEQ // (d * B)
    col0 = (ATT_COL0 + g * 3 * ATT_WIDTH) // 128

    def block_rows(b, r):
        return pl.ds(b * (B * d) + r, B, stride=d) if d > 1 else pl.ds(pl.multiple_of(b * B, B), B)

    def block_of(u):
        return (u, 0) if d == 1 else (u // d, u % d)

    return d, n_blocks, col0, block_rows, block_of


def _att_column(c):
    return pl.BlockSpec((SEQ, 128), lambda hp: (0, c + hp))


def _head_lanes(j):
    lane = lax.broadcasted_iota(jnp.int32, (ATT_BLOCK, 128), 1)
    return (lane >= 64 * j) & (lane < 64 * (j + 1))


def _stack_heads(x, sel0):
    return jnp.concatenate([jnp.where(sel0, x, 0.0), jnp.where(sel0, 0.0, x)], axis=0)


def _stack_values(x, sel0, lanes):
    swapped = pltpu.roll(x, 64, 1)
    stacked = jnp.concatenate([jnp.where(sel0, x, swapped), jnp.where(sel0, swapped, x)], axis=0)
    return stacked if lanes == 128 else jnp.concatenate([stacked] * (lanes // 128), axis=1)


def _pair_coef(coef_ref, g, hp):
    row = lax.broadcasted_iota(jnp.int32, (2 * ATT_BLOCK, 1), 0)
    first = g * ATT_HEADS + hp * 2
    return jnp.where(row < ATT_BLOCK, coef_ref[first], coef_ref[first + 1])


def _band(with_prev, first_key):
    B = ATT_BLOCK
    keys = 2 * B if with_prev else B
    qi = jnp.bitwise_and(lax.broadcasted_iota(jnp.int32, (2 * B, keys), 0), B - 1)
    kj = lax.broadcasted_iota(jnp.int32, (2 * B, keys), 1)
    delta = qi + (B if with_prev else 0) - kj
    valid = (delta >= 0) & (delta <= B)
    if with_prev:
        valid = valid & (kj >= first_key)
    return valid, delta.astype(F32)


def att_fwd(z, g, name):
    B = ATT_BLOCK
    d, n_blocks, col0, block_rows, block_of = _att_geometry(g)
    multi = n_blocks > 1

    def body(coef_ref, q_ref, k_ref, v_ref, o_ref, l_ref):
        cf2 = _pair_coef(coef_ref, g, pl.program_id(0))
        sel0 = _head_lanes(0)

        def one(u):
            b, r = block_of(u)
            rows = block_rows(b, r)
            valid, dist = _band(multi, jnp.where(b == 0, B, 0))
            q2 = _stack_heads(q_ref[rows, :], sel0)
            kk, vv = k_ref[rows, :], v_ref[rows, :]
            if multi:
                prev_rows = block_rows(jnp.maximum(b - 1, 0), r)
                kk = jnp.concatenate([k_ref[prev_rows, :], kk], axis=0)
                vv = jnp.concatenate([v_ref[prev_rows, :], vv], axis=0)
            sc = jnp.where(valid, _dot(q2, kk, tb=True) * 0.125 - cf2 * dist, NEG_INF)
            mx = jnp.max(sc, axis=-1, keepdims=True)
            e = jnp.exp(sc - mx)
            den = jnp.sum(e, axis=-1, keepdims=True)
            o2 = _dot(e * (1.0 / den), vv)
            lse2 = mx + jnp.log(den)
            o_ref[rows, :] = jnp.where(sel0, o2[:B], o2[B:])
            l_ref[rows, :] = jnp.where(sel0, lse2[:B], lse2[B:])

        _for_each_unit(d * n_blocks, one)

    out = jax.ShapeDtypeStruct((SEQ, ATT_WIDTH), F32)
    return pallas_call(
        body, name=name, grid=(HEAD_PAIRS,),
        in_specs=[pl.BlockSpec(memory_space=pltpu.SMEM), _att_column(col0), _att_column(col0 + 4), _att_column(col0 + 8)],
        out_specs=[_att_column(0), _att_column(0)], out_shape=[out, out],
        compiler_params=pltpu.CompilerParams(dimension_semantics=("parallel",)),
    )(_alibi_coef(), z, z, z)


def att_bwd(z, l, do, corr, g, name):
    B = ATT_BLOCK
    d, n_blocks, col0, block_rows, block_of = _att_geometry(g)
    multi = n_blocks > 1
    own = slice(B, 2 * B) if multi else slice(0, B)

    def body(coef_ref, q_ref, k_ref, v_ref, l_ref, do_ref, cr_ref, dq_ref, dk_ref, dv_ref, dq_sc, dk_sc, dv_sc):
        cf2 = _pair_coef(coef_ref, g, pl.program_id(0))
        sel0 = _head_lanes(0)

        def one(u):
            b, r = block_of(u)
            rows = block_rows(b, r)
            valid, dist = _band(multi, jnp.where(b == 0, B, 0))
            kk, vv = k_ref[rows, :], v_ref[rows, :]
            if multi:
                prev_rows = block_rows(jnp.maximum(b - 1, 0), r)
                kk = jnp.concatenate([k_ref[prev_rows, :], kk], axis=0)
                vv = jnp.concatenate([v_ref[prev_rows, :], vv], axis=0)
            q2, do2 = _stack_heads(q_ref[rows, :], sel0), _stack_heads(do_ref[rows, :], sel0)
            keys = kk.shape[0]
            lse2, cr2 = _stack_values(l_ref[rows, :], sel0, keys), _stack_values(cr_ref[rows, :], sel0, keys)
            p = jnp.exp(jnp.where(valid, _dot(q2, kk, tb=True) * 0.125 - cf2 * dist, NEG_INF) - lse2)
            ds = p * (_dot(do2, vv, tb=True) + cr2)
            dq2 = _dot(ds, kk)
            dkk = _dot(ds, q2, ta=True) * 0.125
            dvv = _dot(p, do2, ta=True)
            dq_sc[rows, :] = jnp.where(sel0, dq2[:B], dq2[B:]) * 0.125
            dk_sc[rows, :] = dkk[own]
            dv_sc[rows, :] = dvv[own]
            if multi:
                dk_sc[prev_rows, :] += dkk[:B]
                dv_sc[prev_rows, :] += dvv[:B]

        _for_each_unit(d * n_blocks, one)
        dq_ref[...] = dq_sc[...].astype(dq_ref.dtype)
        dk_ref[...] = dk_sc[...].astype(dk_ref.dtype)
        dv_ref[...] = dv_sc[...].astype(dv_ref.dtype)

    col = _att_column
    out = jax.ShapeDtypeStruct((SEQ, ATT_WIDTH), MXU_DTYPE)
    return pallas_call(
        body, name=name, grid=(HEAD_PAIRS,),
        in_specs=[pl.BlockSpec(memory_space=pltpu.SMEM), col(col0), col(col0 + 4), col(col0 + 8), col(0), col(0), col(0)],
        out_specs=[col(0)] * 3, out_shape=[out] * 3,
        scratch_shapes=[pltpu.VMEM((SEQ, 128), F32)] * 3,
        compiler_params=pltpu.CompilerParams(dimension_semantics=("parallel",), vmem_limit_bytes=MATMUL_VMEM_BYTES),
    )(_alibi_coef(), z, z, z, l, do, corr)


def _head_sum(x):
    i = lax.broadcasted_iota(jnp.int32, (128, 128), 0) // 64
    j = lax.broadcasted_iota(jnp.int32, (128, 128), 1) // 64
    return _dot_f32(x, (i == j).astype(F32), ones_on_right=True)


def _group_weights(l0, l1, l2):
    mx = jnp.maximum(jnp.maximum(l0, l1), l2)
    e0, e1, e2 = jnp.exp(l0 - mx), jnp.exp(l1 - mx), jnp.exp(l2 - mx)
    inv = 1.0 / (e0 + e1 + e2)
    return e0 * inv, e1 * inv, e2 * inv


def att_combine_fwd(o, l, name):
    def body(o0, o1, o2, l0, l1, l2, y_ref):
        w0, w1, w2 = _group_weights(l0[...], l1[...], l2[...])
        y_ref[...] = (o0[...] * w0 + o1[...] * w1 + o2[...] * w2).astype(y_ref.dtype)

    blk = pl.BlockSpec((ROW_TILE, ATT_WIDTH), lambda i: (i, 0))
    return pallas_call(
        body, name=name, grid=(SEQ // ROW_TILE,), in_specs=[blk] * 6, out_specs=blk,
        out_shape=jax.ShapeDtypeStruct((SEQ, ATT_WIDTH), MXU_DTYPE),
    )(*o, *l)


def att_combine_bwd(o, l, dy, name):
    def body(o0, o1, o2, l0, l1, l2, dy_ref, do0, do1, do2, cr0, cr1, cr2):
        w = _group_weights(l0[...], l1[...], l2[...])
        dyv = dy_ref[...]
        tot = _head_sum(dyv * (w[0] * o0[...] + w[1] * o1[...] + w[2] * o2[...]))
        for g, (do_ref, cr_ref) in enumerate(((do0, cr0), (do1, cr1), (do2, cr2))):
            do_ref[...] = dyv * w[g]
            cr_ref[...] = -w[g] * tot

    blk = pl.BlockSpec((ROW_TILE, 128), lambda i, j: (i, j))
    out = jax.ShapeDtypeStruct((SEQ, ATT_WIDTH), F32)
    res = pallas_call(
        body, name=name, grid=(SEQ // ROW_TILE, HEAD_PAIRS), in_specs=[blk] * 7, out_specs=[blk] * 6, out_shape=[out] * 6,
    )(*o, *l, dy)
    return res[:N_GROUPS], res[N_GROUPS:]


SUM_MAX_ROWS = 1024
SUM_ROW_ALIGN = 16
SUM_TILE_BYTES = 24 * 1024 * 1024
SUM_PARAMS = pltpu.CompilerParams(vmem_limit_bytes=MATMUL_VMEM_BYTES)


def _row_tile(rows, cols, operands):
    most = min(rows, SUM_MAX_ROWS) // SUM_ROW_ALIGN * SUM_ROW_ALIGN
    fit = [t for t in range(most, 0, -SUM_ROW_ALIGN) if rows % t == 0]
    return next((t for t in fit if 2 * 4 * operands * t * cols <= SUM_TILE_BYTES), fit[-1])


def _shard_shape(rows, cols, axis):
    return (rows // N_CHIPS, cols) if axis == 0 else (rows, cols // N_CHIPS)


def _half_shape(rows, cols, axis):
    return (rows, cols // 2) if axis == 0 else (rows // 2, cols)


def _piece_shape(rows, cols, axis):
    return (rows // N_CHIPS, cols // 2) if axis == 0 else (rows // 2, cols // N_CHIPS)


def place_own_block(shard, chip, rows, cols, axis, name):
    sr, sc = _shard_shape(rows, cols, axis)
    tr = _row_tile(sr, sc, 2)

    def body(chip_ref, s_ref, o_ref):
        o_ref[...] = s_ref[...].astype(o_ref.dtype)

    if axis == 0:
        out_map = lambda i, chip_ref: (chip_ref[0] * (sr // tr) + i, 0)
    else:
        out_map = lambda i, chip_ref: (i, chip_ref[0])
    return pallas_call(
        body, name=name, out_shape=jax.ShapeDtypeStruct((rows, cols), WEIGHT_COMM_DTYPE), compiler_params=SUM_PARAMS,
        grid_spec=pltpu.PrefetchScalarGridSpec(
            num_scalar_prefetch=1, grid=(sr // tr,), in_specs=[pl.BlockSpec((tr, sc), lambda i, chip_ref: (i, 0))],
            out_specs=pl.BlockSpec((tr, sc), out_map)),
    )(chip, shard)


def add_halves(g, theirs, core, rows, cols, axis, name):
    hr, hc = _half_shape(rows, cols, axis)
    tr = _row_tile(hr, hc, 3)

    def body(core_ref, g_ref, t_ref, o_ref):
        o_ref[...] = (g_ref[...].astype(F32) + t_ref[...].astype(F32)).astype(o_ref.dtype)

    if axis == 0:
        g_map = lambda i, core_ref: (i, core_ref[0])
    else:
        g_map = lambda i, core_ref: (core_ref[0] * (hr // tr) + i, 0)
    blk = pl.BlockSpec((tr, hc), lambda i, core_ref: (i, 0))
    return pallas_call(
        body, name=name, out_shape=jax.ShapeDtypeStruct((hr, hc), GRAD_COMM_DTYPE), compiler_params=SUM_PARAMS,
        grid_spec=pltpu.PrefetchScalarGridSpec(
            num_scalar_prefetch=1, grid=(hr // tr,), in_specs=[pl.BlockSpec((tr, hc), g_map), blk], out_specs=blk),
    )(core, g, theirs)


def add_pieces(half, got, chip, rows, cols, axis, name):
    hr, _ = _half_shape(rows, cols, axis)
    pr, pc = _piece_shape(rows, cols, axis)
    tr = _row_tile(pr, pc, 5)

    def body(chip_ref, h_ref, got_ref, o_ref):
        o_ref[...] = (h_ref[...].astype(F32) + got_ref[0].astype(F32) + got_ref[1].astype(F32) + got_ref[2].astype(F32))

    if axis == 0:
        h_map = lambda i, chip_ref: (chip_ref[0] * (pr // tr) + i, 0)
    else:
        h_map = lambda i, chip_ref: (i, chip_ref[0])
    return pallas_call(
        body, name=name, out_shape=jax.ShapeDtypeStruct((pr, pc), F32), compiler_params=SUM_PARAMS,
        grid_spec=pltpu.PrefetchScalarGridSpec(
            num_scalar_prefetch=1, grid=(pr // tr,),
            in_specs=[pl.BlockSpec((tr, pc), h_map), pl.BlockSpec((3, tr, pc), lambda i, chip_ref: (0, i, 0))],
            out_specs=pl.BlockSpec((tr, pc), lambda i, chip_ref: (i, 0))),
    )(chip, half, got)


def _adamw_math(w, g, m, v):
    nm = ADAM_B1 * m + (1.0 - ADAM_B1) * g
    nv = ADAM_B2 * v + (1.0 - ADAM_B2) * (g * g)
    m_hat = nm / (1.0 - ADAM_B1 ** ADAM_STEP)
    v_hat = nv / (1.0 - ADAM_B2 ** ADAM_STEP)
    return -ADAM_LR * (m_hat / (jnp.sqrt(v_hat) + ADAM_EPS) + ADAM_WD * w), nm, nv


def adamw_halves(w, mine, theirs, m, v, core, rows, cols, axis, name):
    sr, sc = _shard_shape(rows, cols, axis)
    pr, pc = _piece_shape(rows, cols, axis)
    tr = _row_tile(pr, pc, 9)
    nt = pr // tr

    def body(core_ref, w_ref, a_ref, b_ref, m_ref, v_ref, g_ref, d_ref, nm_ref, nv_ref):
        g = jnp.where(pl.program_id(0) == core_ref[0], a_ref[...], b_ref[...])
        g_ref[...] = g
        d_ref[...], nm_ref[...], nv_ref[...] = _adamw_math(w_ref[...], g, m_ref[...], v_ref[...])

    if axis == 0:
        full = pl.BlockSpec((tr, pc), lambda h, i, core_ref: (i, h))
    else:
        full = pl.BlockSpec((tr, pc), lambda h, i, core_ref: (h * nt + i, 0))
    part = pl.BlockSpec((tr, pc), lambda h, i, core_ref: (i, 0))
    out = jax.ShapeDtypeStruct((sr, sc), F32)
    return pallas_call(
        body, name=name, out_shape=[out, out, out, out], compiler_params=SUM_PARAMS,
        grid_spec=pltpu.PrefetchScalarGridSpec(
            num_scalar_prefetch=1, grid=(2, nt), in_specs=[full, part, part, full, full], out_specs=[full] * 4),
    )(core, w, mine, theirs, m, v)


BIG = (
    ("ffn1_w_gate_up", D_MODEL, 2 * D_FF, 1),
    ("ffn1_w_down", D_FF, D_MODEL, 0),
    ("w_in", D_MODEL, IN_COLS, 1),
    ("w_branch_hg", HG_WIDTH, D_MODEL, 1),
    ("w_branch_att", ATT_WIDTH, D_MODEL, 1),
    ("w_out", D_MODEL, D_MODEL, 0),
    ("ffn2_w_gate_up", D_MODEL, 2 * D_FF, 1),
    ("ffn2_w_down", D_FF, D_MODEL, 0),
)
N_BIG = len(BIG)
ANY = pl.BlockSpec(memory_space=pl.ANY)


def _place():
    return lax.axis_index("x"), lax.axis_index("y"), lax.axis_index("c")


def _other_chips(x, y):
    return ((1 - x, y), (x, 1 - y), (1 - x, 1 - y))


MAX_COPY_CHUNKS = 16
CHUNK_ROW_ALIGN = 16


def _row_chunks(view):
    rows = view.shape[0]
    n = next(n for n in range(MAX_COPY_CHUNKS, 0, -1) if rows % (CHUNK_ROW_ALIGN * n) == 0 or n == 1)
    step = rows // n
    return [pl.ds(i * step, step) for i in range(n)]


def _remote(src, dst, send_sem, recv_sem, device):
    return pltpu.make_async_remote_copy(src_ref=src, dst_ref=dst, send_sem=send_sem, recv_sem=recv_sem,
                                        device_id=device, device_id_type=MESH)


def _start_remote(src, dst, send_sem, recv_sem, device):
    for rows in _row_chunks(src):
        _remote(src.at[rows, :], dst.at[rows, :], send_sem, recv_sem, device).start()
    return _remote(src, dst, send_sem, recv_sem, device)


HBM = pl.BlockSpec(memory_space=pltpu.HBM)
SEM = pl.BlockSpec(memory_space=pltpu.SEMAPHORE)
SPLIT_COPY_EFFECT = pltpu.SideEffectType.DATAFLOW_SIDE_EFFECTING
GROUPS = {"ffn1": (0, 1), "mix": (2, 3, 4, 5), "ffn2": (6, 7)}


class _SemList:
    def __init__(self, refs):
        self.refs = refs
        self.at = self

    def __getitem__(self, index):
        w, k = index
        return self.refs[3 * w + k]


def _gather_piece(ref, rows, cols, axis, chip, c):
    sr, sc = _shard_shape(rows, cols, axis)
    j = 2 * chip[0] + chip[1]
    if axis == 0:
        return ref.at[pl.ds(j * sr + c * (sr // 2), sr // 2), :]
    return ref.at[pl.ds(c * (sr // 2), sr // 2), pl.ds(pl.multiple_of(j * sc, 128), sc)]


def _start_gather_sends(bufs, ws, send_sems, recv_sems):
    x, y, c = _place()
    for w, (_, r, cc, ax) in enumerate(ws):
        mine = _gather_piece(bufs[w], r, cc, ax, (x, y), c)
        for k, chip in enumerate(_other_chips(x, y)):
            _start_remote(mine, mine, send_sems.at[w, k], recv_sems.at[w, k], (*chip, c))


def _wait_gather_sends(bufs, ws, send_sems, recv_sems):
    x, y, c = _place()
    for w, (_, r, cc, ax) in enumerate(ws):
        for k, chip in enumerate(_other_chips(x, y)):
            got = _gather_piece(bufs[w], r, cc, ax, chip, c)
            _remote(got, got, send_sems.at[w, k], recv_sems.at[w, k], (x, y, c)).wait_recv()
    for w, (_, r, cc, ax) in enumerate(ws):
        mine = _gather_piece(bufs[w], r, cc, ax, (x, y), c)
        for k in range(3):
            _remote(mine, mine, send_sems.at[w, k], recv_sems.at[w, k], (x, y, c)).wait_send()


def _forward_halves(bufs, ws, send_sems, recv_sems):
    x, y, c = _place()
    passed = []
    for w, (_, r, cc, ax) in enumerate(ws):
        for k, chip in enumerate(_other_chips(x, y)):
            got = _gather_piece(bufs[w], r, cc, ax, chip, c)
            passed.append(_start_remote(got, got, send_sems.at[w, k], recv_sems.at[w, k], (x, y, 1 - c)))
    for w, (_, r, cc, ax) in enumerate(ws):
        for k, chip in enumerate(_other_chips(x, y)):
            got = _gather_piece(bufs[w], r, cc, ax, chip, 1 - c)
            _remote(got, got, send_sems.at[w, k], recv_sems.at[w, k], (x, y, c)).wait_recv()
    for cp in passed:
        cp.wait_send()


def gather_start(placed, after, group):
    ws = [BIG[i] for i in GROUPS[group]]
    n = len(ws)

    def body(*refs):
        bufs = refs[:n]
        send_sems, recv_sems = _SemList(refs[n + 1:4 * n + 1]), _SemList(refs[4 * n + 1:7 * n + 1])
        token = refs[-1]
        _start_gather_sends(bufs, ws, send_sems, recv_sems)
        token[...] = jnp.zeros_like(token)

    out = pallas_call(
        body, name=f"gather_start_{group}", in_specs=[HBM] * n + [ANY],
        out_specs=[SEM] * (6 * n) + [HBM] * n + [pl.BlockSpec(memory_space=pltpu.VMEM)],
        out_shape=[pltpu.SemaphoreType.DMA(())] * (6 * n)
        + [pltpu.HBM((r, cc), WEIGHT_COMM_DTYPE) for _, r, cc, _ in ws] + [jax.ShapeDtypeStruct((8, 128), F32)],
        input_output_aliases={w: 6 * n + w for w in range(n)},
        compiler_params=pltpu.CompilerParams(has_side_effects=SPLIT_COPY_EFFECT),
    )(*[_in_hbm(p) for p in placed], after)
    return out[:3 * n], out[3 * n:6 * n], out[6 * n:7 * n], out[-1]


def gather_wait(bufs, send_sems, recv_sems, after, group):
    ws = [BIG[i] for i in GROUPS[group]]
    n = len(ws)

    def body(*refs):
        _wait_gather_sends(refs[:n], ws, _SemList(refs[n:n + 3 * n]), _SemList(refs[n + 3 * n:n + 6 * n]))

    return pallas_call(
        body, name=f"gather_wait_{group}", in_specs=[HBM] * n + [SEM] * (6 * n) + [ANY] * len(after), out_specs=[HBM] * n,
        out_shape=[pltpu.HBM((r, cc), WEIGHT_COMM_DTYPE) for _, r, cc, _ in ws],
        input_output_aliases={w: w for w in range(n)},
        compiler_params=pltpu.CompilerParams(has_side_effects=SPLIT_COPY_EFFECT),
    )(*bufs, *send_sems, *recv_sems, *after)


def gather_forward(bufs, group):
    ws = [BIG[i] for i in GROUPS[group]]
    n = len(ws)

    def body(*refs):
        _forward_halves(refs[n:2 * n], ws, refs[2 * n], refs[2 * n + 1])

    return pallas_call(
        body, name=f"gather_forward_{group}", in_specs=[ANY] * n, out_specs=[ANY] * n,
        out_shape=[jax.ShapeDtypeStruct((r, cc), WEIGHT_COMM_DTYPE) for _, r, cc, _ in ws],
        input_output_aliases={w: w for w in range(n)},
        scratch_shapes=[pltpu.SemaphoreType.DMA((n, 3))] * 2,
    )(*bufs)


def _half(ref, rows, cols, axis, c):
    if axis == 0:
        return ref.at[:, pl.ds(pl.multiple_of(c * (cols // 2), 128), cols // 2)]
    return ref.at[pl.ds(c * (rows // 2), rows // 2), :]


def _piece_of_half(ref, rows, cols, axis, chip):
    j = 2 * chip[0] + chip[1]
    pr, pc = _piece_shape(rows, cols, axis)
    if axis == 0:
        return ref.at[pl.ds(j * pr, pr), :]
    return ref.at[:, pl.ds(pl.multiple_of(j * pc, 128), pc)]


def sibling_exchange_start(srcs, view, landing_shapes, dtype, name):
    n = len(srcs)

    def body(*refs):
        ins, land, sems = refs[:n], refs[n:2 * n], refs[2 * n:4 * n]
        x, y, c = _place()
        for w in range(n):
            _start_remote(view(ins[w], w, c), land[w], sems[w], sems[n + w], (x, y, 1 - c))
        refs[-1][...] = jnp.zeros_like(refs[-1])

    landing = [lax.empty(shape, dtype) for shape in landing_shapes]
    out = pallas_call(
        body, name=name, in_specs=[HBM] * (2 * n),
        out_specs=[SEM] * (2 * n) + [HBM] * (2 * n) + [pl.BlockSpec(memory_space=pltpu.VMEM)],
        out_shape=[pltpu.SemaphoreType.DMA(())] * (2 * n) + [pltpu.HBM(a.shape, a.dtype) for a in srcs]
        + [pltpu.HBM(shape, dtype) for shape in landing_shapes] + [jax.ShapeDtypeStruct((8, 128), F32)],
        input_output_aliases={i: 2 * n + i for i in range(2 * n)},
        compiler_params=pltpu.CompilerParams(has_side_effects=SPLIT_COPY_EFFECT),
    )(*[_in_hbm(a) for a in srcs], *[_in_hbm(b) for b in landing])
    return out[:n], out[n:2 * n], out[2 * n:3 * n], out[3 * n:4 * n], out[-1]


def sibling_exchange_wait(srcs, landing, send_sems, recv_sems, view, after, name):
    n = len(srcs)

    def body(*refs):
        ins, land, sems = refs[:n], refs[n:2 * n], refs[2 * n:4 * n]
        x, y, c = _place()
        for w in range(n):
            cp = _remote(view(ins[w], w, c), land[w], sems[w], sems[n + w], (x, y, c))
            cp.wait_send()
            cp.wait_recv()

    out = pallas_call(
        body, name=name, in_specs=[HBM] * (2 * n) + [SEM] * (2 * n) + [ANY] * len(after), out_specs=[HBM] * (2 * n),
        out_shape=[pltpu.HBM(a.shape, a.dtype) for a in srcs] + [pltpu.HBM(b.shape, b.dtype) for b in landing],
        input_output_aliases={i: i for i in range(2 * n)},
        compiler_params=pltpu.CompilerParams(has_side_effects=SPLIT_COPY_EFFECT),
    )(*srcs, *landing, *send_sems, *recv_sems, *after)
    return out[:n], out[n:]


def _scatter_copies(halves, got, ws, send_sems, recv_sems, start):
    x, y, c = _place()
    copies = []
    for w, (_, r, cc, ax) in enumerate(ws):
        for k, chip in enumerate(_other_chips(x, y)):
            args = (_piece_of_half(halves[w], r, cc, ax, chip), got[w].at[k], send_sems.at[w, k], recv_sems.at[w, k], (*chip, c))
            copies.append(_start_remote(*args) if start else _remote(*args))
    return copies


def scatter_start(halves, group):
    ws = [BIG[i] for i in GROUPS[group]]
    n = len(ws)

    def body(*refs):
        sems = refs[2 * n:8 * n]
        _scatter_copies(refs[:n], refs[n:2 * n], ws, _SemList(sems[:3 * n]), _SemList(sems[3 * n:]), start=True)
        refs[-1][...] = jnp.zeros_like(refs[-1])

    landing = [lax.empty((3,) + _piece_shape(r, cc, ax), GRAD_COMM_DTYPE) for _, r, cc, ax in ws]
    out = pallas_call(
        body, name=f"scatter_start_{group}", in_specs=[HBM] * (2 * n),
        out_specs=[SEM] * (6 * n) + [HBM] * (2 * n) + [pl.BlockSpec(memory_space=pltpu.VMEM)],
        out_shape=[pltpu.SemaphoreType.DMA(())] * (6 * n)
        + [pltpu.HBM(_half_shape(r, cc, ax), GRAD_COMM_DTYPE) for _, r, cc, ax in ws]
        + [pltpu.HBM((3,) + _piece_shape(r, cc, ax), GRAD_COMM_DTYPE) for _, r, cc, ax in ws]
        + [jax.ShapeDtypeStruct((8, 128), F32)],
        input_output_aliases={i: 6 * n + i for i in range(2 * n)},
        compiler_params=pltpu.CompilerParams(has_side_effects=SPLIT_COPY_EFFECT),
    )(*[_in_hbm(h) for h in halves], *[_in_hbm(b) for b in landing])
    return out[:3 * n], out[3 * n:6 * n], out[6 * n:7 * n], out[7 * n:8 * n], out[-1]


def scatter_wait(halves, got, send_sems, recv_sems, after, group):
    ws = [BIG[i] for i in GROUPS[group]]
    n = len(ws)

    def body(*refs):
        sems = refs[2 * n:8 * n]
        for cp in _scatter_copies(refs[:n], refs[n:2 * n], ws, _SemList(sems[:3 * n]), _SemList(sems[3 * n:]), start=False):
            cp.wait_send()
            cp.wait_recv()

    out = pallas_call(
        body, name=f"scatter_wait_{group}", in_specs=[HBM] * (2 * n) + [SEM] * (6 * n) + [ANY] * len(after),
        out_specs=[HBM] * (2 * n),
        out_shape=[pltpu.HBM(_half_shape(r, cc, ax), GRAD_COMM_DTYPE) for _, r, cc, ax in ws]
        + [pltpu.HBM((3,) + _piece_shape(r, cc, ax), GRAD_COMM_DTYPE) for _, r, cc, ax in ws],
        input_output_aliases={i: i for i in range(2 * n)},
        compiler_params=pltpu.CompilerParams(has_side_effects=SPLIT_COPY_EFFECT),
    )(*halves, *got, *send_sems, *recv_sems, *after)
    return out[:n], out[n:]


N_DEV = 8
SMALL = ("ffn1_norm", "mix_norm", "hg_lower_bounds", "hg_out_norm", "ffn2_norm", "final_norm")
SMALL_STAGE_ROWS = 8


def small_step(loss, grads, w, m, v, behind):
    n = len(SMALL)
    shapes = [g.shape for g in grads]
    first_row = [sum(s[0] for s in shapes[:i]) for i in range(n + 1)]
    assert first_row[n] < SMALL_STAGE_ROWS
    loss_row = (pl.ds(first_row[n], 1), pl.ds(0, loss.shape[1]))

    def body(*refs):
        loss_ref, g_refs, w_refs, m_refs, v_refs = refs[0], refs[1:1 + n], refs[1 + n:1 + 2 * n], refs[1 + 2 * n:1 + 3 * n], refs[1 + 3 * n:1 + 4 * n]
        outs = refs[2 + 4 * n:3 + 8 * n]
        loss_out, dg_refs, d_refs, nm_refs, nv_refs = outs[0], outs[1:1 + n], outs[1 + n:1 + 2 * n], outs[1 + 2 * n:1 + 3 * n], outs[1 + 3 * n:]
        stage, gathered, send_sems, recv_sems = refs[3 + 8 * n:]
        x, y, c = _place()
        me = 4 * x + 2 * y + c

        def slot(i, shape):
            return pl.ds(first_row[i], shape[0]), pl.ds(0, shape[1])

        stage[...] = jnp.zeros_like(stage)
        for i, g_ref in enumerate(g_refs):
            stage[slot(i, shapes[i])] = g_ref[...]
        stage[loss_row] = loss_ref[pl.ds(0, 1), :]
        gathered[me] = stage[...]
        copies = []
        for k in range(1, N_DEV):
            peer = (x ^ (k >> 2), y ^ ((k >> 1) & 1), c ^ (k & 1))
            cp = pltpu.make_async_remote_copy(
                src_ref=stage, dst_ref=gathered.at[me], send_sem=send_sems.at[k - 1], recv_sem=recv_sems.at[k - 1],
                device_id=peer, device_id_type=MESH)
            cp.start()
            copies.append(cp)
        for cp in copies:
            cp.wait()
        acc = gathered[0]
        for k in range(1, N_DEV):
            acc = acc + gathered[k]
        stage[...] = acc
        loss_out[...] = jnp.broadcast_to(stage[loss_row], loss_out.shape)
        for i in range(n):
            g = stage[slot(i, shapes[i])]
            dg_refs[i][...] = g
            d_refs[i][...], nm_refs[i][...], nv_refs[i][...] = _adamw_math(w_refs[i][...], g, m_refs[i][...], v_refs[i][...])

    vm = pl.BlockSpec(memory_space=pltpu.VMEM)
    per_param = [jax.ShapeDtypeStruct(s, F32) for s in shapes]
    out = pallas_call(
        body, name="small_step", in_specs=[vm] * (1 + 4 * n) + [ANY], out_specs=[vm] * (1 + 4 * n),
        out_shape=[jax.ShapeDtypeStruct(loss.shape, F32)] + per_param * 4,
        scratch_shapes=[pltpu.VMEM((SMALL_STAGE_ROWS, D_MODEL), F32),
                        pltpu.VMEM((N_DEV, SMALL_STAGE_ROWS, D_MODEL), F32),
                        pltpu.SemaphoreType.DMA((N_DEV - 1,)), pltpu.SemaphoreType.DMA((N_DEV - 1,))],
    )(loss, *grads, *w, *m, *v, behind)
    return out[0], out[1:1 + n], out[1 + n:1 + 2 * n], out[1 + 2 * n:1 + 3 * n], out[1 + 3 * n:]


def _swiglu_block_fwd(h, n, w_gu, w_down, tag, behind=()):
    a, b, s = gate_up_swiglu(n, w_gu, f"{tag}_gate_up", behind=behind)
    h_out = matmul(s, w_down, res=h, scale=0.5, name=f"{tag}_down")
    return h_out, (n, a, b, s)


def _swiglu_block_bwd(h, norm_g, w_gu, w_down, saved, dh_out, df, tag, exchange, behind=()):
    n, a, b, s = saved
    d_down = matmul(s, df, ta=True, scale=0.5, out_dtype=GRAD_COMM_DTYPE, name=f"{tag}_d_w_down")
    ds = matmul(df, w_down, tb=True, scale=0.5, out_dtype=ACT_DTYPE, behind=behind, name=f"{tag}_d_s")
    dgu = swiglu_bwd(a, b, ds, f"{tag}_swiglu_bwd")
    d_gu = matmul(n, dgu, ta=True, out_dtype=GRAD_COMM_DTYPE, name=f"{tag}_d_w_gate_up")
    tokens = exchange.gradients_ready(tag, {f"{tag}_w_gate_up": d_gu, f"{tag}_w_down": d_down})
    dn = matmul(dgu, w_gu, tb=True, behind=tokens, name=f"{tag}_d_n")
    dh, dh_m, dg = rmsnorm_bwd(h, norm_g, dn, dh_out, f"{tag}_norm_bwd")
    return dh, dh_m, dg


def local_step(x, target, small, exchange):
    big = {}
    n1 = rmsnorm_fwd(x, small["ffn1_norm"], "ffn1_norm", behind=exchange.started)
    token, big_ffn1 = exchange.weights("ffn1", n1)
    big.update(big_ffn1)
    h1, saved1 = _swiglu_block_fwd(x, n1, big["ffn1_w_gate_up"], big["ffn1_w_down"], "ffn1", token)
    u = rmsnorm_fwd(h1, small["mix_norm"], "mix_norm")
    token, big_mix = exchange.weights("mix", u)
    big.update(big_mix)
    z = matmul(u, big["w_in"], behind=token, name="w_in")
    p = small["hg_lower_bounds"]
    lb = 1.0 / (1.0 + jnp.exp(p[1:2] - p[0:1]))
    y_hg, o_raw, states = hgrn_fwd(z, lb, small["hg_out_norm"], "hgrn_fwd")
    o_att, l_att = zip(*[att_fwd(z, g, f"att_fwd_{g}") for g in range(N_GROUPS)])
    y_att = att_combine_fwd(o_att, l_att, "att_combine")
    bh = matmul(y_hg, big["w_branch_hg"], name="branch_hg")
    ba = matmul(y_att, big["w_branch_att"], name="branch_att")
    merged = merge_fwd(z, bh, ba, "merge")
    h2 = matmul(merged, big["w_out"], res=h1, name="w_out")
    n2 = rmsnorm_fwd(h2, small["ffn2_norm"], "ffn2_norm")
    token, big_ffn2 = exchange.weights("ffn2", n2)
    big.update(big_ffn2)
    h3, saved2 = _swiglu_block_fwd(h2, n2, big["ffn2_w_gate_up"], big["ffn2_w_down"], "ffn2", token)
    dh3, dh3_m, d_final, loss = final_norm_loss(h3, small["final_norm"], target, "final_norm_loss")

    gs, gb = {"final_norm": d_final}, {}
    dh2, dh2_m, gs["ffn2_norm"] = _swiglu_block_bwd(
        h2, small["ffn2_norm"], big["ffn2_w_gate_up"], big["ffn2_w_down"], saved2, dh3, dh3_m, "ffn2", exchange)
    token = exchange.backward_done("ffn2", dh2)
    gb["w_out"] = matmul(merged, dh2_m, ta=True, out_dtype=GRAD_COMM_DTYPE, name="d_w_out")
    dmerged = matmul(dh2_m, big["w_out"], tb=True, behind=token, name="d_merged")
    dbh, dba, dgh, dga = merge_bwd(z, bh, ba, dmerged, "merge_bwd")
    gb["w_branch_hg"] = matmul(y_hg, dbh, ta=True, out_dtype=GRAD_COMM_DTYPE, name="d_w_branch_hg")
    gb["w_branch_att"] = matmul(y_att, dba, ta=True, out_dtype=GRAD_COMM_DTYPE, name="d_w_branch_att")
    dy_hg = matmul(dbh, big["w_branch_hg"], tb=True, name="d_y_hg")
    dy_att = matmul(dba, big["w_branch_att"], tb=True, name="d_y_att")
    dq, dfp, di, dog, d_lb, gs["hg_out_norm"] = hgrn_bwd(z, lb, small["hg_out_norm"], o_raw, states, dy_hg, "hgrn_bwd")
    do_att, corr = att_combine_bwd(o_att, l_att, dy_att, "att_combine_bwd")
    d_att = [part for g in range(N_GROUPS) for part in att_bwd(z, l_att[g], do_att[g], corr[g], g, f"att_bwd_{g}")]
    dz = jnp.concatenate([dq, dfp, di, dog, *d_att, dgh, dga], axis=1)
    gb["w_in"] = matmul(u, dz, ta=True, out_dtype=GRAD_COMM_DTYPE, name="d_w_in")
    token = exchange.gradients_ready("mix", gb)
    du = matmul(dz, big["w_in"], tb=True, behind=token, name="d_u")
    dh1, dh1_m, gs["mix_norm"] = rmsnorm_bwd(h1, small["mix_norm"], du, dh2, "mix_norm_bwd")
    token = exchange.backward_done("mix", dh1)
    dp0 = d_lb * lb * (1.0 - lb)
    gs["hg_lower_bounds"] = jnp.concatenate([dp0, -dp0], axis=0)
    dx, _, gs["ffn1_norm"] = _swiglu_block_bwd(
        x, small["ffn1_norm"], big["ffn1_w_gate_up"], big["ffn1_w_down"], saved1, dh1, dh1_m, "ffn1", exchange, token)
    exchange.backward_done("ffn1", dx)
    return loss, dx, gs


WEIGHTS = ("ffn1_norm", "ffn1_w_gate_up", "ffn1_w_down", "mix_norm", "w_in", "hg_lower_bounds", "hg_out_norm",
           "w_branch_hg", "w_branch_att", "w_out", "ffn2_norm", "ffn2_w_gate_up", "ffn2_w_down", "final_norm")


class WeightExchange:
    ORDER = ("ffn1", "mix", "ffn2")

    def __init__(self, shards, core, chip):
        self.core, self.chip = core, chip
        self.halving = None
        self.scattering = None
        self.reducing = {}
        first = self.ORDER[0]
        self.placed = {BIG[i][0]: place_own_block(shards[BIG[i][0]], chip, *BIG[i][1:], f"place_{BIG[i][0]}")
                       for i in GROUPS[first]}
        self._start_gather(first, chip)
        self.started = [self.token]
        chip_behind = chip + self.token[0, :1].astype(jnp.int32)
        for group in self.ORDER[1:]:
            for i in GROUPS[group]:
                n, r, cc, ax = BIG[i]
                self.placed[n] = place_own_block(shards[n], chip_behind, r, cc, ax, f"place_{n}")
        self.placed_behind = [self.placed[n] for group in self.ORDER[1:] for n in self._names(group)]

    def _names(self, group):
        return [BIG[i][0] for i in GROUPS[group]]

    def _start_gather(self, group, after):
        send_sems, recv_sems, bufs, self.token = gather_start([self.placed[n] for n in self._names(group)], after, group)
        self.gathering = (group, send_sems, recv_sems, bufs)

    def weights(self, group, h):
        pending, send_sems, recv_sems, bufs = self.gathering
        assert pending == group
        after = [h] + (self.placed_behind if group == self.ORDER[0] else [])
        whole = gather_forward(gather_wait(bufs, send_sems, recv_sems, after, group), group)
        later = self.ORDER.index(group) + 1
        behind = []
        if later < len(self.ORDER):
            self._start_gather(self.ORDER[later], whole[0])
            behind = [self.token]
        return behind, dict(zip(self._names(group), whole))

    @staticmethod
    def _half_to_sibling(ws):
        return lambda ref, w, c: _half(ref, *ws[w][1:], 1 - c)

    def gradients_ready(self, group, grads):
        ws = [BIG[i] for i in GROUPS[group]]
        send_sems, recv_sems, own, theirs, token = sibling_exchange_start(
            [grads[n] for n, *_ in ws], self._half_to_sibling(ws), [_half_shape(r, cc, ax) for _, r, cc, ax in ws],
            GRAD_COMM_DTYPE, f"halves_start_{group}")
        self.halving = (group, send_sems, recv_sems, own, theirs)
        return [token]

    def backward_done(self, group, dh):
        pending, send_sems, recv_sems, own, theirs = self.halving
        assert pending == group
        ws = [BIG[i] for i in GROUPS[group]]
        own, theirs = sibling_exchange_wait(own, theirs, send_sems, recv_sems, self._half_to_sibling(ws), [dh],
                                            f"halves_wait_{group}")
        halves = [add_halves(g, t, self.core, r, cc, ax, f"add_halves_{n}") for (n, r, cc, ax), g, t in zip(ws, own, theirs)]
        previous = self.scattering
        send_sems, recv_sems, halves, got, self.token = scatter_start(halves, group)
        self.scattering = (group, send_sems, recv_sems, halves, got)
        behind = [self._finish_scatter(previous, [self.token])] if previous is not None else []
        return behind + [self.token]

    def _finish_scatter(self, scattering, after):
        group, send_sems, recv_sems, halves, got = scattering
        halves, got = scatter_wait(halves, got, send_sems, recv_sems, after, group)
        ws = [BIG[i] for i in GROUPS[group]]
        mine = [add_pieces(h, g, self.chip, r, cc, ax, f"add_pieces_{n}") for (n, r, cc, ax), h, g in zip(ws, halves, got)]
        send_sems, recv_sems, mine, theirs, token = sibling_exchange_start(
            mine, lambda ref, w, c: ref, [_piece_shape(r, cc, ax) for _, r, cc, ax in ws], F32, f"reduced_start_{group}")
        self.reducing[group] = (send_sems, recv_sems, mine, theirs)
        return token

    def finish(self, after):
        return self._finish_scatter(self.scattering, after)

    def reduced_halves(self, group, after):
        send_sems, recv_sems, mine, theirs = self.reducing.pop(group)
        mine, theirs = sibling_exchange_wait(mine, theirs, send_sems, recv_sems, lambda ref, w, c: ref, after,
                                             f"reduced_wait_{group}")
        return {BIG[i][0]: (a, b) for i, a, b in zip(GROUPS[group], mine, theirs)}


def kernel(x, ffn1_norm, ffn1_w_gate_up, ffn1_w_down, mix_norm, w_in, hg_lower_bounds, hg_out_norm, w_branch_hg, w_branch_att, w_out, ffn2_norm, ffn2_w_gate_up, ffn2_w_down, final_norm, loss_target, m_ffn1_norm, m_ffn1_w_gate_up, m_ffn1_w_down, m_mix_norm, m_w_in, m_hg_lower_bounds, m_hg_out_norm, m_w_branch_hg, m_w_branch_att, m_w_out, m_ffn2_norm, m_ffn2_w_gate_up, m_ffn2_w_down, m_final_norm, v_ffn1_norm, v_ffn1_w_gate_up, v_ffn1_w_down, v_mix_norm, v_w_in, v_hg_lower_bounds, v_hg_out_norm, v_w_branch_hg, v_w_branch_att, v_w_out, v_ffn2_norm, v_ffn2_w_gate_up, v_ffn2_w_down, v_final_norm):
    w = dict(ffn1_norm=ffn1_norm, ffn1_w_gate_up=ffn1_w_gate_up, ffn1_w_down=ffn1_w_down, mix_norm=mix_norm, w_in=w_in,
             hg_lower_bounds=hg_lower_bounds, hg_out_norm=hg_out_norm, w_branch_hg=w_branch_hg, w_branch_att=w_branch_att,
             w_out=w_out, ffn2_norm=ffn2_norm, ffn2_w_gate_up=ffn2_w_gate_up, ffn2_w_down=ffn2_w_down, final_norm=final_norm)
    m = dict(ffn1_norm=m_ffn1_norm, ffn1_w_gate_up=m_ffn1_w_gate_up, ffn1_w_down=m_ffn1_w_down, mix_norm=m_mix_norm,
             w_in=m_w_in, hg_lower_bounds=m_hg_lower_bounds, hg_out_norm=m_hg_out_norm, w_branch_hg=m_w_branch_hg,
             w_branch_att=m_w_branch_att, w_out=m_w_out, ffn2_norm=m_ffn2_norm, ffn2_w_gate_up=m_ffn2_w_gate_up,
             ffn2_w_down=m_ffn2_w_down, final_norm=m_final_norm)
    v = dict(ffn1_norm=v_ffn1_norm, ffn1_w_gate_up=v_ffn1_w_gate_up, ffn1_w_down=v_ffn1_w_down, mix_norm=v_mix_norm,
             w_in=v_w_in, hg_lower_bounds=v_hg_lower_bounds, hg_out_norm=v_hg_out_norm, w_branch_hg=v_w_branch_hg,
             w_branch_att=v_w_branch_att, w_out=v_w_out, ffn2_norm=v_ffn2_norm, ffn2_w_gate_up=v_ffn2_w_gate_up,
             ffn2_w_down=v_ffn2_w_down, final_norm=v_final_norm)

    core = lax.axis_index("c").astype(jnp.int32).reshape(1)
    chip = (2 * lax.axis_index("x") + lax.axis_index("y")).astype(jnp.int32).reshape(1)
    exchange = WeightExchange({n: w[n][0] for n, *_ in BIG}, core, chip)
    small = {n: w[n] for n in SMALL}
    small["final_norm"] = final_norm.reshape(1, D_MODEL)

    loss, dx, gs = local_step(x[0], loss_target[0], small, exchange)

    grads, delta, new_m, new_v = {}, {}, {}, {}

    def update(group, core, after):
        reduced = exchange.reduced_halves(group, after)
        for i in GROUPS[group]:
            n, r, cc, ax = BIG[i]
            a, b = reduced[n]
            g, d, nm, nv = adamw_halves(w[n][0], a, b, m[n][0], v[n][0], core, r, cc, ax, f"adamw_{n}")
            grads[n], delta[n], new_m[n], new_v[n] = g[None], d[None], nm[None], nv[None]

    core_behind = core + exchange.token[0, :1].astype(jnp.int32)
    update("ffn2", core_behind, [exchange.token])
    update("mix", core_behind, [delta["ffn2_w_down"]])
    token = exchange.finish(after=[delta[BIG[i][0]] for group in ("ffn2", "mix") for i in GROUPS[group]])
    two_d = lambda a: a.reshape(1, D_MODEL) if a.ndim == 1 else a
    loss_sum, *small_out = small_step(loss, [gs[n] for n in SMALL], *[[two_d(p[n]) for n in SMALL] for p in (w, m, v)],
                                      behind=token)
    for result, parts in zip((grads, delta, new_m, new_v), small_out):
        result.update({n: a.reshape(w[n].shape) for n, a in zip(SMALL, parts)})
    update("ffn1", core, [loss_sum])

    return (loss_sum[0, 0], dx[None], *[grads[n] for n in WEIGHTS], *[delta[n] for n in WEIGHTS],
            *[new_m[n] for n in WEIGHTS], *[new_v[n] for n in WEIGHTS])
```

```python
import numpy as np
import jax
import jax.numpy as jnp
from jax import lax
from jax.experimental import pallas as pl
from jax.experimental.pallas import tpu as pltpu

SEQ = 2048
D_MODEL = 1024
D_FF = 2816
HG_HEADS = 4
HG_DIM = 128
HG_WIDTH = 512
HG_CHUNK = 64
ATT_GROUPS = ((128, 1), (512, 4), (2048, 16))
ATT_HEADS = 8
ATT_WIDTH = 512
ATT_BLOCK = 128
ALIBI_MAX = 8.0
IN_COLS = 8704
EPS = 1e-6
NEG_INF = -1e30
ADAM_LR = 0.001
ADAM_B1 = 0.9
ADAM_B2 = 0.999
ADAM_EPS = 1e-08
ADAM_WD = 0.01
ADAM_STEP = 10

N_CHIPS = 4
MXU_DTYPE = jnp.bfloat16
WEIGHT_COMM_DTYPE = jnp.bfloat16
GRAD_COMM_DTYPE = jnp.bfloat16
ACT_DTYPE = jnp.bfloat16
MESH = pl.DeviceIdType.MESH
F32 = jnp.float32


def _sigmoid(x):
    return 1.0 / (1.0 + jnp.exp(-x))


def _dot(a, b, ta=False, tb=False):
    dn = (((0 if ta else 1,), (1 if tb else 0,)), ((), ()))
    return lax.dot_general(a.astype(MXU_DTYPE), b.astype(MXU_DTYPE), dn, preferred_element_type=F32)


def _dot_f32(a, b, ones_on_right=False):
    x = a if ones_on_right else b
    hi = x.astype(jnp.bfloat16)
    rest = x - hi.astype(F32)
    mid = rest.astype(jnp.bfloat16)
    lo = (rest - mid.astype(F32)).astype(jnp.bfloat16)
    if ones_on_right:
        dot = lambda q: jnp.dot(q, b.astype(jnp.bfloat16), preferred_element_type=F32)
    else:
        dot = lambda q: jnp.dot(a.astype(jnp.bfloat16), q, preferred_element_type=F32)
    return dot(hi) + (dot(mid) + dot(lo))


def _split_bf16(x):
    hi = x.astype(jnp.bfloat16)
    return hi, (x - hi.astype(F32)).astype(jnp.bfloat16)


def _hdot(a, b, ta=False, tb=False):
    dn =(((0 if ta else 1,), (1 if tb else 0,)), ((), ()))
    (a_hi, a_lo), (b_hi, b_lo) = _split_bf16(a), _split_bf16(b)
    dot = lambda p, q: lax.dot_general(p, q, dn, preferred_element_type=F32)
    return dot(a_hi, b_hi) + (dot(a_lo, b_hi) + dot(a_hi, b_lo))


def _in_hbm(a):
    return pltpu.with_memory_space_constraint(a, pltpu.HBM)


def pallas_call(body, **kw):
    grid_spec = kw.get("grid_spec")
    out_specs = kw["out_specs"] if grid_spec is None else grid_spec.out_specs
    one = not isinstance(kw["out_shape"], (list, tuple))
    shapes = [kw["out_shape"]] if one else list(kw["out_shape"])
    out_specs = [out_specs] if one else list(out_specs)
    shapes = [pltpu.HBM(a.shape, a.dtype) if s.memory_space is None and isinstance(a, jax.ShapeDtypeStruct) else a
              for a, s in zip(shapes, out_specs)]
    kw["out_shape"] = shapes[0] if one else shapes
    return pl.pallas_call(body, **kw)


MATMUL_VMEM_BYTES = 48 * 1024 * 1024
MATMUL_TILE_BYTES = 36 * 1024 * 1024
MXU_ALIGN = 128
MXU_FLOPS_PER_SECOND = 900e12
HBM_BYTES_PER_SECOND = 3.0e12
GRID_STEP_SECONDS = 0.35e-6


def _divisors(n, most):
    return [t for t in range(min(n, most), 0, -MXU_ALIGN) if n % t == 0 and t % MXU_ALIGN == 0]


def _matmul_tiles(M, N, K, in_bytes, out_bytes, has_res):
    best = None
    for tk in _divisors(K, K):
        nk = K // tk
        for tm in _divisors(M, 2048):
            for tn in _divisors(N, 512):
                tiles = 2 * in_bytes * (tm * tk + tk * tn) + 2 * out_bytes * tm * tn
                tiles += 4 * tm * tn * ((nk > 1) + 2 * has_res)
                if tiles > MATMUL_TILE_BYTES:
                    continue
                traffic = in_bytes * (M * K * (1 if nk == 1 else N // tn) + K * N * (M // tm)) + out_bytes * M * N
                exposed = in_bytes * (tm * tk + tk * tn) + out_bytes * tm * tn
                seconds = (max(2 * M * N * K / MXU_FLOPS_PER_SECOND, traffic / HBM_BYTES_PER_SECOND)
                           + exposed / HBM_BYTES_PER_SECOND + (M // tm) * (N // tn) * nk * GRID_STEP_SECONDS)
                key = (seconds, -tm * tn * tk)
                if best is None or key < best[0]:
                    best = (key, (tm, tn, tk))
    return best[1]


def matmul(a, b, *, ta=False, tb=False, out_dtype=F32, res=None, scale=1.0, behind=(), name):
    if ta:
        K, M = a.shape
    else:
        M, K = a.shape
    if tb:
        N, K2 = b.shape
    else:
        K2, N = b.shape
    assert K == K2 and a.dtype == b.dtype
    tm, tn, tk = _matmul_tiles(M, N, K, a.dtype.itemsize, jnp.dtype(out_dtype).itemsize, res is not None)
    nk = K // tk

    def finish(r, r_ref, o_ref):
        if scale != 1.0:
            r = r * scale
        if res is not None:
            r = r_ref[...] + r
        o_ref[...] = r.astype(out_dtype)

    def body(*refs):
        a_ref, b_ref = refs[:2]
        r_ref = refs[2] if res is not None else None
        o_ref = refs[2 + (res is not None) + len(behind)]
        if nk == 1:
            finish(_dot(a_ref[...], b_ref[...], ta, tb), r_ref, o_ref)
            return
        acc = refs[-1]
        k = pl.program_id(2)

        @pl.when(k == 0)
        def _():
            acc[...] = jnp.zeros_like(acc)

        acc[...] += _dot(a_ref[...], b_ref[...], ta, tb)

        @pl.when(k == nk - 1)
        def _():
            finish(acc[...], r_ref, o_ref)

    a_spec = pl.BlockSpec((tk, tm), lambda i, j, k: (k, i)) if ta else pl.BlockSpec((tm, tk), lambda i, j, k: (i, k))
    b_spec = pl.BlockSpec((tn, tk), lambda i, j, k: (j, k)) if tb else pl.BlockSpec((tk, tn), lambda i, j, k: (k, j))
    in_specs = [a_spec, b_spec]
    args = [a, b]
    if res is not None:
        in_specs.append(pl.BlockSpec((tm, tn), lambda i, j, k: (i, j)))
        args.append(res)
    for earlier in behind:
        in_specs.append(pl.BlockSpec(memory_space=pl.ANY))
        args.append(earlier)
    return pallas_call(
        body, name=name, grid=(M // tm, N // tn, nk), in_specs=in_specs,
        out_specs=pl.BlockSpec((tm, tn), lambda i, j, k: (i, j)),
        out_shape=jax.ShapeDtypeStruct((M, N), out_dtype),
        scratch_shapes=[pltpu.VMEM((tm, tn), F32)] if nk > 1 else [],
        compiler_params=pltpu.CompilerParams(dimension_semantics=("parallel", "parallel", "arbitrary"),
                                             vmem_limit_bytes=MATMUL_VMEM_BYTES),
    )(*args)


ROW_TILE = 256


def rmsnorm_fwd(x, g, name, behind=()):
    def body(x_ref, g_ref, *refs):
        n_ref = refs[-1]
        xv = x_ref[...]
        r = lax.rsqrt(jnp.mean(xv * xv, axis=-1, keepdims=True) + EPS)
        n_ref[...] = ((xv * r) * g_ref[...]).astype(n_ref.dtype)

    order = list(behind)
    return pallas_call(
        body, name=name, grid=(SEQ // ROW_TILE,),
        in_specs=[pl.BlockSpec((ROW_TILE, D_MODEL), lambda i: (i, 0)), pl.BlockSpec((1, D_MODEL), lambda i: (0, 0))]
        + [pl.BlockSpec(memory_space=pl.ANY)] * len(order),
        out_specs=pl.BlockSpec((ROW_TILE, D_MODEL), lambda i: (i, 0)),
        out_shape=jax.ShapeDtypeStruct((SEQ, D_MODEL), MXU_DTYPE),
    )(x, g, *order)


def rmsnorm_bwd(x, g, dn, dres, name):
    def body(x_ref, g_ref, dn_ref, dr_ref, dx_ref, dxm_ref, dg_ref):
        xv = x_ref[...]
        r = lax.rsqrt(jnp.mean(xv * xv, axis=-1, keepdims=True) + EPS)
        xh = xv * r
        dnv = dn_ref[...]

        @pl.when(pl.program_id(0) == 0)
        def _():
            dg_ref[...] = jnp.zeros_like(dg_ref)

        dg_ref[...] += jnp.sum(dnv * xh, axis=0, keepdims=True)
        dxh = dnv * g_ref[...]
        dx = dr_ref[...] + r * (dxh - xh * jnp.mean(dxh * xh, axis=-1, keepdims=True))
        dx_ref[...] = dx
        dxm_ref[...] = dx.astype(dxm_ref.dtype)

    row = pl.BlockSpec((ROW_TILE, D_MODEL), lambda i: (i, 0))
    vec = pl.BlockSpec((1, D_MODEL), lambda i: (0, 0))
    return pallas_call(
        body, name=name, grid=(SEQ // ROW_TILE,), in_specs=[row, vec, row, row], out_specs=[row, row, vec],
        out_shape=[jax.ShapeDtypeStruct((SEQ, D_MODEL), F32), jax.ShapeDtypeStruct((SEQ, D_MODEL), MXU_DTYPE),
                   jax.ShapeDtypeStruct((1, D_MODEL), F32)],
        compiler_params=pltpu.CompilerParams(dimension_semantics=("arbitrary",)),
    )(x, g, dn, dres)


def final_norm_loss(h, g, target, name):
    def body(h_ref, g_ref, t_ref, dh_ref, dhm_ref, dg_ref, loss_ref):
        xv = h_ref[...]
        r = lax.rsqrt(jnp.mean(xv * xv, axis=-1, keepdims=True) + EPS)
        xh = xv * r
        gv = g_ref[...]
        e = xh * gv - t_ref[...]

        @pl.when(pl.program_id(0) == 0)
        def _():
            dg_ref[...] = jnp.zeros_like(dg_ref)
            loss_ref[...] = jnp.zeros_like(loss_ref)

        part = 0.5 * jnp.sum(jnp.sum(e * e, axis=-1, keepdims=True) * (1.0 / D_MODEL), axis=0, keepdims=True)
        loss_ref[...] += jnp.broadcast_to(part, loss_ref.shape)
        dout = e * (1.0 / D_MODEL)
        dg_ref[...] += jnp.sum(dout * xh, axis=0, keepdims=True)
        dxh = dout * gv
        dh = r * (dxh - xh * jnp.mean(dxh * xh, axis=-1, keepdims=True))
        dh_ref[...] = dh
        dhm_ref[...] = dh.astype(dhm_ref.dtype)

    row = pl.BlockSpec((ROW_TILE, D_MODEL), lambda i: (i, 0))
    vec = pl.BlockSpec((1, D_MODEL), lambda i: (0, 0))
    return pallas_call(
        body, name=name, grid=(SEQ // ROW_TILE,), in_specs=[row, vec, row],
        out_specs=[row, row, vec, pl.BlockSpec((8, 128), lambda i: (0, 0))],
        out_shape=[jax.ShapeDtypeStruct((SEQ, D_MODEL), F32), jax.ShapeDtypeStruct((SEQ, D_MODEL), MXU_DTYPE),
                   jax.ShapeDtypeStruct((1, D_MODEL), F32), jax.ShapeDtypeStruct((8, 128), F32)],
        compiler_params=pltpu.CompilerParams(dimension_semantics=("arbitrary",)),
    )(h, g, target)


FFN_TILE = 256
FFN_TILES = D_FF // FFN_TILE


def gate_up_swiglu(n, w_gu, name, behind=()):
    def body(n_ref, wa_ref, wb_ref, *refs):
        a_ref, b_ref, s_ref = refs[len(behind):]
        nv = n_ref[...]
        a = _dot(nv, wa_ref[...])
        b = _dot(nv, wb_ref[...])
        a_ref[...] = a.astype(a_ref.dtype)
        b_ref[...] = b.astype(b_ref.dtype)
        s_ref[...] = (a * _sigmoid(a) * b).astype(s_ref.dtype)

    tile = pl.BlockSpec((SEQ, FFN_TILE), lambda j: (0, j))
    act = jax.ShapeDtypeStruct((SEQ, D_FF), ACT_DTYPE)
    return pallas_call(
        body, name=name, grid=(FFN_TILES,),
        in_specs=[pl.BlockSpec((SEQ, D_MODEL), lambda j: (0, 0)), pl.BlockSpec((D_MODEL, FFN_TILE), lambda j: (0, j)),
                  pl.BlockSpec((D_MODEL, FFN_TILE), lambda j: (0, j + FFN_TILES))]
        + [pl.BlockSpec(memory_space=pl.ANY)] * len(behind),
        out_specs=[tile, tile, tile], out_shape=[act, act, jax.ShapeDtypeStruct((SEQ, D_FF), MXU_DTYPE)],
        compiler_params=pltpu.CompilerParams(dimension_semantics=("parallel",), vmem_limit_bytes=MATMUL_VMEM_BYTES),
    )(n, w_gu, w_gu, *behind)


def swiglu_bwd(a, b, ds, name):
    rows = ROW_TILE // 2

    def body(a_ref, b_ref, ds_ref, o_ref):
        av = a_ref[...].astype(F32)
        sg = _sigmoid(av)
        dsv = ds_ref[...].astype(F32)
        o_ref[:, :D_FF] = (dsv * b_ref[...].astype(F32) * (sg * (1.0 + av * (1.0 - sg)))).astype(o_ref.dtype)
        o_ref[:, D_FF:] = (dsv * av * sg).astype(o_ref.dtype)

    blk = pl.BlockSpec((rows, D_FF), lambda i: (i, 0))
    return pallas_call(
        body, name=name, grid=(SEQ // rows,), in_specs=[blk, blk, blk],
        out_specs=pl.BlockSpec((rows, 2 * D_FF), lambda i: (i, 0)),
        out_shape=jax.ShapeDtypeStruct((SEQ, 2 * D_FF), MXU_DTYPE), compiler_params=SUM_PARAMS,
    )(a, b, ds)


GATE_HG_BLK = 6656 // 512
GATE_ATT_BLK = 7680 // 512


def merge_fwd(z, bh, ba, name):
    def body(gh_ref, ga_ref, bh_ref, ba_ref, o_ref):
        o_ref[...] = (_sigmoid(gh_ref[...]) * bh_ref[...] + _sigmoid(ga_ref[...]) * ba_ref[...]).astype(o_ref.dtype)

    blk = pl.BlockSpec((ROW_TILE, 512), lambda i, j: (i, j))
    return pallas_call(
        body, name=name, grid=(SEQ // ROW_TILE, 2),
        in_specs=[pl.BlockSpec((ROW_TILE, 512), lambda i, j: (i, GATE_HG_BLK + j)),
                  pl.BlockSpec((ROW_TILE, 512), lambda i, j: (i, GATE_ATT_BLK + j)), blk, blk],
        out_specs=blk, out_shape=jax.ShapeDtypeStruct((SEQ, D_MODEL), MXU_DTYPE),
    )(z, z, bh, ba)


def merge_bwd(z, bh, ba, dm, name):
    def body(gh_ref, ga_ref, bh_ref, ba_ref, dm_ref, dbh_ref, dba_ref, dgh_ref, dga_ref):
        dmv = dm_ref[...]
        sh = _sigmoid(gh_ref[...])
        sa = _sigmoid(ga_ref[...])
        dbh_ref[...] = (dmv * sh).astype(dbh_ref.dtype)
        dba_ref[...] = (dmv * sa).astype(dba_ref.dtype)
        dgh_ref[...] = (dmv * bh_ref[...] * (sh * (1.0 - sh))).astype(dgh_ref.dtype)
        dga_ref[...] = (dmv * ba_ref[...] * (sa * (1.0 - sa))).astype(dga_ref.dtype)

    blk = pl.BlockSpec((ROW_TILE, 512), lambda i, j: (i, j))
    out = jax.ShapeDtypeStruct((SEQ, D_MODEL), MXU_DTYPE)
    return pallas_call(
        body, name=name, grid=(SEQ // ROW_TILE, 2),
        in_specs=[pl.BlockSpec((ROW_TILE, 512), lambda i, j: (i, GATE_HG_BLK + j)),
                  pl.BlockSpec((ROW_TILE, 512), lambda i, j: (i, GATE_ATT_BLK + j)), blk, blk, blk],
        out_specs=[blk, blk, blk, blk], out_shape=[out, out, out, out],
    )(z, z, bh, ba, dm)


N_CHUNKS = SEQ // HG_CHUNK
HG_STEP_CHUNKS = 4


def _hgrn_gates(q, fp, lb):
    C = HG_CHUNK
    sg = _sigmoid(fp)
    f = lb + (1.0 - lb) * sg
    lf = jnp.log(f)
    row = lax.broadcasted_iota(jnp.int32, (C, C), 0)
    col = lax.broadcasted_iota(jnp.int32, (C, C), 1)
    causal = row >= col
    G = _dot_f32(causal.astype(F32), lf)
    eG = jnp.exp(G)
    enG = jnp.exp(-G)
    qg = q * eG
    kg = (1.0 - f) * enG
    A = jnp.where(causal, _hdot(qg, kg, tb=True), 0.0)
    egl = jnp.exp(jnp.sum(lf, axis=0, keepdims=True))
    return sg, f, causal, eG, enG, qg, kg, A, egl


def hgrn_fwd(z, lb, gain, name):
    C, K = HG_CHUNK, HG_DIM

    def body(q_ref, f_ref, v_ref, og_ref, p_ref, g_ref, y_ref, o_ref, st_ref, state):
        @pl.when(pl.program_id(0) == 0)
        def _():
            state[...] = jnp.zeros_like(state)

        for cc in range(HG_STEP_CHUNKS):
            rows = pl.ds(cc * C, C)
            for h in range(HG_HEADS):
                hd = pl.ds(h * K, K)
                v = v_ref[rows, hd]
                _, _, _, _, _, qg, kg, A, egl = _hgrn_gates(q_ref[rows, hd], f_ref[rows, hd], p_ref[:, hd])
                st = state[h]
                st_ref[h, cc] = st
                o = _hdot(A, v) + _hdot(qg, st, tb=True)
                state[h] = st * egl + _hdot(v, kg * egl, ta=True)
                o_ref[rows, hd] = o
                rs = lax.rsqrt(jnp.mean(o * o, axis=-1, keepdims=True) + EPS)
                og = og_ref[rows, hd]
                y_ref[rows, hd] = (((o * rs) * g_ref[:, hd]) * (og * _sigmoid(og))).astype(y_ref.dtype)

    R = HG_STEP_CHUNKS * C

    def zcol(section):
        return pl.BlockSpec((R, HG_WIDTH), lambda c: (c, section))

    vec = pl.BlockSpec((1, HG_WIDTH), lambda c: (0, 0))
    blk = pl.BlockSpec((R, HG_WIDTH), lambda c: (c, 0))
    return pallas_call(
        body, name=name, grid=(N_CHUNKS // HG_STEP_CHUNKS,),
        in_specs=[zcol(0), zcol(1), zcol(2), zcol(3), vec, vec],
        out_specs=[blk, blk, pl.BlockSpec((HG_HEADS, HG_STEP_CHUNKS, K, K), lambda c: (0, c, 0, 0))],
        out_shape=[jax.ShapeDtypeStruct((SEQ, HG_WIDTH), MXU_DTYPE), jax.ShapeDtypeStruct((SEQ, HG_WIDTH), F32),
                   jax.ShapeDtypeStruct((HG_HEADS, N_CHUNKS, K, K), F32)],
        scratch_shapes=[pltpu.VMEM((HG_HEADS, K, K), F32)],
        compiler_params=pltpu.CompilerParams(dimension_semantics=("arbitrary",)),
    )(z, z, z, z, lb, gain)


def hgrn_bwd(z, lb, gain, o_raw, states, dy, name):
    C, K = HG_CHUNK, HG_DIM

    def body(q_ref, f_ref, v_ref, og_ref, p_ref, g_ref, o_ref, st_ref, dy_ref,
             dq_ref, dfp_ref, dv_ref, dog_ref, dlb_ref, dgain_ref, dstate):
        @pl.when(pl.program_id(0) == 0)
        def _():
            dstate[...] = jnp.zeros_like(dstate)
            dlb_ref[...] = jnp.zeros_like(dlb_ref)
            dgain_ref[...] = jnp.zeros_like(dgain_ref)

        last = lax.broadcasted_iota(jnp.int32, (C, K), 0) == C - 1
        row = lax.broadcasted_iota(jnp.int32, (C, C), 0)
        col = lax.broadcasted_iota(jnp.int32, (C, C), 1)
        anti_causal = (col >= row).astype(F32)
        for cc in reversed(range(HG_STEP_CHUNKS)):
            rows = pl.ds(cc * C, C)
            for h in range(HG_HEADS):
                hd = pl.ds(h * K, K)
                v = v_ref[rows, hd]
                lb = p_ref[:, hd]
                sg, f, causal, eG, enG, qg, kg, A, egl = _hgrn_gates(q_ref[rows, hd], f_ref[rows, hd], lb)
                kd = kg * egl
                st = st_ref[h, cc]
                dst = dstate[h]
                o = o_ref[rows, hd]
                og = og_ref[rows, hd]
                gain_v = g_ref[:, hd]
                dyv = dy_ref[rows, hd]
                rs = lax.rsqrt(jnp.mean(o * o, axis=-1, keepdims=True) + EPS)
                on = o * rs
                sgo = _sigmoid(og)
                silu = og * sgo
                dog_ref[rows, hd] = (dyv * (on * gain_v) * (sgo * (1.0 + og * (1.0 - sgo)))).astype(dog_ref.dtype)
                dgain_ref[:, hd] += jnp.sum(dyv * silu * on, axis=0, keepdims=True)
                don = dyv * gain_v * silu
                do = rs * (don - on * jnp.mean(don * on, axis=-1, keepdims=True))
                dA = jnp.where(causal, _hdot(do, v, tb=True), 0.0)
                dv_ref[rows, hd] = (_hdot(A, do, ta=True) + _hdot(kd, dst, tb=True)).astype(dv_ref.dtype)
                dqg = _hdot(dA, kg) + _hdot(do, st)
                dkg = _hdot(dA, qg, ta=True)
                dkd = _hdot(v, dst)
                dstate[h] = dst * egl + _hdot(do, qg, ta=True)
                dgl = jnp.sum(st * dst, axis=0, keepdims=True) * egl
                dq_ref[rows, hd] = (dqg * eG).astype(dq_ref.dtype)
                dk = dkg * enG + dkd * (enG * egl)
                dG = dqg * qg - dkg * kg - dkd * kd
                extra = jnp.sum(dkd * kd, axis=0, keepdims=True) + dgl
                dG = dG + jnp.where(last, extra, 0.0)
                dlf = _dot_f32(anti_causal, dG)
                df = dlf / f - dk
                dfp_ref[rows, hd] = (df * (1.0 - lb) * (sg * (1.0 - sg))).astype(dfp_ref.dtype)
                dlb_ref[:, hd] += jnp.sum(df * (1.0 - sg), axis=0, keepdims=True)

    R = HG_STEP_CHUNKS * C
    n_steps = N_CHUNKS // HG_STEP_CHUNKS

    def rc(c):
        return n_steps - 1 - c

    def zcol(section):
        return pl.BlockSpec((R, HG_WIDTH), lambda c: (rc(c), section))

    vec = pl.BlockSpec((1, HG_WIDTH), lambda c: (0, 0))
    blk = pl.BlockSpec((R, HG_WIDTH), lambda c: (rc(c), 0))
    out = jax.ShapeDtypeStruct((SEQ, HG_WIDTH), MXU_DTYPE)
    small = jax.ShapeDtypeStruct((1, HG_WIDTH), F32)
    return pallas_call(
        body, name=name, grid=(n_steps,),
        in_specs=[zcol(0), zcol(1), zcol(2), zcol(3), vec, vec, blk,
                  pl.BlockSpec((HG_HEADS, HG_STEP_CHUNKS, K, K), lambda c: (0, rc(c), 0, 0)), blk],
        out_specs=[blk, blk, blk, blk, vec, vec],
        out_shape=[out, out, out, out, small, small],
        scratch_shapes=[pltpu.VMEM((HG_HEADS, K, K), F32)],
        compiler_params=pltpu.CompilerParams(dimension_semantics=("arbitrary",)),
    )(z, z, z, z, lb, gain, o_raw, states, dy)


N_GROUPS = len(ATT_GROUPS)
HEAD_PAIRS = ATT_WIDTH // 128
ATT_COL0 = 4 * HG_WIDTH
UNROLLED_UNITS = 4


def _alibi_coef():
    n = N_GROUPS * ATT_HEADS
    slopes = np.exp2(-ALIBI_MAX * np.arange(1, n + 1, dtype=np.float32) / n).astype(np.float32)
    dil = np.repeat(np.array([d for _, d in ATT_GROUPS], np.float32), ATT_HEADS)
    return jnp.asarray(slopes * dil, F32)


def _for_each_unit(n, fn):
    if n <= UNROLLED_UNITS:
        for u in range(n):
            fn(u)
    else:
        def group(i, carry):
            for j in range(UNROLLED_UNITS):
                fn(i * UNROLLED_UNITS + j)
            return carry
        lax.fori_loop(0, n // UNROLLED_UNITS, group, 0)


def _att_geometry(g):
    B = ATT_BLOCK
    d = ATT_GROUPS[g][1]
    n_blocks = SEQ // (d * B)
    col0 = (ATT_COL0 + g * 3 * ATT_WIDTH) // 128

    def block_rows(b, r):
        return pl.ds(b * (B * d) + r, B, stride=d) if d > 1 else pl.ds(pl.multiple_of(b * B, B), B)

    def block_of(u):
        return (u, 0) if d == 1 else (u // d, u % d)

    return d, n_blocks, col0, block_rows, block_of


def _att_column(c):
    return pl.BlockSpec((SEQ, 128), lambda hp: (0, c + hp))


def _head_lanes(j):
    lane = lax.broadcasted_iota(jnp.int32, (ATT_BLOCK, 128), 1)
    return (lane >= 64 * j) & (lane < 64 * (j + 1))


def _stack_heads(x, sel0):
    return jnp.concatenate([jnp.where(sel0, x, 0.0), jnp.where(sel0, 0.0, x)], axis=0)


def _stack_values(x, sel0, lanes):
    swapped = pltpu.roll(x, 64, 1)
    stacked = jnp.concatenate([jnp.where(sel0, x, swapped), jnp.where(sel0, swapped, x)], axis=0)
    return stacked if lanes == 128 else jnp.concatenate([stacked] * (lanes // 128), axis=1)


def _pair_coef(coef_ref, g, hp):
    row = lax.broadcasted_iota(jnp.int32, (2 * ATT_BLOCK, 1), 0)
    first = g * ATT_HEADS + hp * 2
    return jnp.where(row < ATT_BLOCK, coef_ref[first], coef_ref[first + 1])


def _band(with_prev, first_key):
    B = ATT_BLOCK
    keys = 2 * B if with_prev else B
    qi = jnp.bitwise_and(lax.broadcasted_iota(jnp.int32, (2 * B, keys), 0), B - 1)
    kj = lax.broadcasted_iota(jnp.int32, (2 * B, keys), 1)
    delta = qi + (B if with_prev else 0) - kj
    valid = (delta >= 0) & (delta <= B)
    if with_prev:
        valid = valid & (kj >= first_key)
    return valid, delta.astype(F32)


def att_fwd(z, g, name):
    B = ATT_BLOCK
    d, n_blocks, col0, block_rows, block_of = _att_geometry(g)
    multi = n_blocks > 1

    def body(coef_ref, q_ref, k_ref, v_ref, o_ref, l_ref):
        cf2 = _pair_coef(coef_ref, g, pl.program_id(0))
        sel0 = _head_lanes(0)

        def one(u):
            b, r = block_of(u)
            rows = block_rows(b, r)
            valid, dist = _band(multi, jnp.where(b == 0, B, 0))
            q2 = _stack_heads(q_ref[rows, :], sel0)
            kk, vv = k_ref[rows, :], v_ref[rows, :]
            if multi:
                prev_rows = block_rows(jnp.maximum(b - 1, 0), r)
                kk = jnp.concatenate([k_ref[prev_rows, :], kk], axis=0)
                vv = jnp.concatenate([v_ref[prev_rows, :], vv], axis=0)
            sc = jnp.where(valid, _dot(q2, kk, tb=True) * 0.125 - cf2 * dist, NEG_INF)
            mx = jnp.max(sc, axis=-1, keepdims=True)
            e = jnp.exp(sc - mx)
            den = jnp.sum(e, axis=-1, keepdims=True)
            o2 = _dot(e * (1.0 / den), vv)
            lse2 = mx + jnp.log(den)
            o_ref[rows, :] = jnp.where(sel0, o2[:B], o2[B:])
            l_ref[rows, :] = jnp.where(sel0, lse2[:B], lse2[B:])

        _for_each_unit(d * n_blocks, one)

    out = jax.ShapeDtypeStruct((SEQ, ATT_WIDTH), F32)
    return pallas_call(
        body, name=name, grid=(HEAD_PAIRS,),
        in_specs=[pl.BlockSpec(memory_space=pltpu.SMEM), _att_column(col0), _att_column(col0 + 4), _att_column(col0 + 8)],
        out_specs=[_att_column(0), _att_column(0)], out_shape=[out, out],
        compiler_params=pltpu.CompilerParams(dimension_semantics=("parallel",)),
    )(_alibi_coef(), z, z, z)


def att_bwd(z, l, do, corr, g, name):
    B = ATT_BLOCK
    d, n_blocks, col0, block_rows, block_of = _att_geometry(g)
    multi = n_blocks > 1
    own = slice(B, 2 * B) if multi else slice(0, B)

    def body(coef_ref, q_ref, k_ref, v_ref, l_ref, do_ref, cr_ref, dq_ref, dk_ref, dv_ref, dq_sc, dk_sc, dv_sc):
        cf2 = _pair_coef(coef_ref, g, pl.program_id(0))
        sel0 = _head_lanes(0)

        def one(u):
            b, r = block_of(u)
            rows = block_rows(b, r)
            valid, dist = _band(multi, jnp.where(b == 0, B, 0))
            kk, vv = k_ref[rows, :], v_ref[rows, :]
            if multi:
                prev_rows = block_rows(jnp.maximum(b - 1, 0), r)
                kk = jnp.concatenate([k_ref[prev_rows, :], kk], axis=0)
                vv = jnp.concatenate([v_ref[prev_rows, :], vv], axis=0)
            q2, do2 = _stack_heads(q_ref[rows, :], sel0), _stack_heads(do_ref[rows, :], sel0)
            keys = kk.shape[0]
            lse2, cr2 = _stack_values(l_ref[rows, :], sel0, keys), _stack_values(cr_ref[rows, :], sel0, keys)
            p = jnp.exp(jnp.where(valid, _dot(q2, kk, tb=True) * 0.125 - cf2 * dist, NEG_INF) - lse2)
            ds = p * (_dot(do2, vv, tb=True) + cr2)
            dq2 = _dot(ds, kk)
            dkk = _dot(ds, q2, ta=True) * 0.125
            dvv = _dot(p, do2, ta=True)
            dq_sc[rows, :] = jnp.where(sel0, dq2[:B], dq2[B:]) * 0.125
            dk_sc[rows, :] = dkk[own]
            dv_sc[rows, :] = dvv[own]
            if multi:
                dk_sc[prev_rows, :] += dkk[:B]
                dv_sc[prev_rows, :] += dvv[:B]

        _for_each_unit(d * n_blocks, one)
        dq_ref[...] = dq_sc[...].astype(dq_ref.dtype)
        dk_ref[...] = dk_sc[...].astype(dk_ref.dtype)
        dv_ref[...] = dv_sc[...].astype(dv_ref.dtype)

    col = _att_column
    out = jax.ShapeDtypeStruct((SEQ, ATT_WIDTH), MXU_DTYPE)
    return pallas_call(
        body, name=name, grid=(HEAD_PAIRS,),
        in_specs=[pl.BlockSpec(memory_space=pltpu.SMEM), col(col0), col(col0 + 4), col(col0 + 8), col(0), col(0), col(0)],
        out_specs=[col(0)] * 3, out_shape=[out] * 3,
        scratch_shapes=[pltpu.VMEM((SEQ, 128), F32)] * 3,
        compiler_params=pltpu.CompilerParams(dimension_semantics=("parallel",), vmem_limit_bytes=MATMUL_VMEM_BYTES),
    )(_alibi_coef(), z, z, z, l, do, corr)


def _head_sum(x):
    i = lax.broadcasted_iota(jnp.int32, (128, 128), 0) // 64
    j = lax.broadcasted_iota(jnp.int32, (128, 128), 1) // 64
    return _dot_f32(x, (i == j).astype(F32), ones_on_right=True)


def _group_weights(l0, l1, l2):
    mx = jnp.maximum(jnp.maximum(l0, l1), l2)
    e0, e1, e2 = jnp.exp(l0 - mx), jnp.exp(l1 - mx), jnp.exp(l2 - mx)
    inv = 1.0 / (e0 + e1 + e2)
    return e0 * inv, e1 * inv, e2 * inv


def att_combine_fwd(o, l, name):
    def body(o0, o1, o2, l0, l1, l2, y_ref):
        w0, w1, w2 = _group_weights(l0[...], l1[...], l2[...])
        y_ref[...] = (o0[...] * w0 + o1[...] * w1 + o2[...] * w2).astype(y_ref.dtype)

    blk = pl.BlockSpec((ROW_TILE, ATT_WIDTH), lambda i: (i, 0))
    return pallas_call(
        body, name=name, grid=(SEQ // ROW_TILE,), in_specs=[blk] * 6, out_specs=blk,
        out_shape=jax.ShapeDtypeStruct((SEQ, ATT_WIDTH), MXU_DTYPE),
    )(*o, *l)


def att_combine_bwd(o, l, dy, name):
    def body(o0, o1, o2, l0, l1, l2, dy_ref, do0, do1, do2, cr0, cr1, cr2):
        w = _group_weights(l0[...], l1[...], l2[...])
        dyv = dy_ref[...]
        tot = _head_sum(dyv * (w[0] * o0[...] + w[1] * o1[...] + w[2] * o2[...]))
        for g, (do_ref, cr_ref) in enumerate(((do0, cr0), (do1, cr1), (do2, cr2))):
            do_ref[...] = dyv * w[g]
            cr_ref[...] = -w[g] * tot

    blk = pl.BlockSpec((ROW_TILE, 128), lambda i, j: (i, j))
    out = jax.ShapeDtypeStruct((SEQ, ATT_WIDTH), F32)
    res = pallas_call(
        body, name=name, grid=(SEQ // ROW_TILE, HEAD_PAIRS), in_specs=[blk] * 7, out_specs=[blk] * 6, out_shape=[out] * 6,
    )(*o, *l, dy)
    return res[:N_GROUPS], res[N_GROUPS:]


SUM_MAX_ROWS = 1024
SUM_ROW_ALIGN = 16
SUM_TILE_BYTES = 24 * 1024 * 1024
SUM_PARAMS = pltpu.CompilerParams(vmem_limit_bytes=MATMUL_VMEM_BYTES)


def _row_tile(rows, cols, operands):
    most = min(rows, SUM_MAX_ROWS) // SUM_ROW_ALIGN * SUM_ROW_ALIGN
    fit = [t for t in range(most, 0, -SUM_ROW_ALIGN) if rows % t == 0]
    return next((t for t in fit if 2 * 4 * operands * t * cols <= SUM_TILE_BYTES), fit[-1])


def _shard_shape(rows, cols, axis):
    return (rows // N_CHIPS, cols) if axis == 0 else (rows, cols // N_CHIPS)


def _half_shape(rows, cols, axis):
    return (rows, cols // 2) if axis == 0 else (rows // 2, cols)


def _piece_shape(rows, cols, axis):
    return (rows // N_CHIPS, cols // 2) if axis == 0 else (rows // 2, cols // N_CHIPS)


def place_own_block(shard, chip, rows, cols, axis, name):
    sr, sc = _shard_shape(rows, cols, axis)
    tr = _row_tile(sr, sc, 2)

    def body(chip_ref, s_ref, o_ref):
        o_ref[...] = s_ref[...].astype(o_ref.dtype)

    if axis == 0:
        out_map = lambda i, chip_ref: (chip_ref[0] * (sr // tr) + i, 0)
    else:
        out_map = lambda i, chip_ref: (i, chip_ref[0])
    return pallas_call(
        body, name=name, out_shape=jax.ShapeDtypeStruct((rows, cols), WEIGHT_COMM_DTYPE), compiler_params=SUM_PARAMS,
        grid_spec=pltpu.PrefetchScalarGridSpec(
            num_scalar_prefetch=1, grid=(sr // tr,), in_specs=[pl.BlockSpec((tr, sc), lambda i, chip_ref: (i, 0))],
            out_specs=pl.BlockSpec((tr, sc), out_map)),
    )(chip, shard)


def add_halves(g, theirs, core, rows, cols, axis, name):
    hr, hc = _half_shape(rows, cols, axis)
    tr = _row_tile(hr, hc, 3)

    def body(core_ref, g_ref, t_ref, o_ref):
        o_ref[...] = (g_ref[...].astype(F32) + t_ref[...].astype(F32)).astype(o_ref.dtype)

    if axis == 0:
        g_map = lambda i, core_ref: (i, core_ref[0])
    else:
        g_map = lambda i, core_ref: (core_ref[0] * (hr // tr) + i, 0)
    blk = pl.BlockSpec((tr, hc), lambda i, core_ref: (i, 0))
    return pallas_call(
        body, name=name, out_shape=jax.ShapeDtypeStruct((hr, hc), GRAD_COMM_DTYPE), compiler_params=SUM_PARAMS,
        grid_spec=pltpu.PrefetchScalarGridSpec(
            num_scalar_prefetch=1, grid=(hr // tr,), in_specs=[pl.BlockSpec((tr, hc), g_map), blk], out_specs=blk),
    )(core, g, theirs)


def add_pieces(half, got, chip, rows, cols, axis, name):
    hr, _ = _half_shape(rows, cols, axis)
    pr, pc = _piece_shape(rows, cols, axis)
    tr = _row_tile(pr, pc, 5)

    def body(chip_ref, h_ref, got_ref, o_ref):
        o_ref[...] = (h_ref[...].astype(F32) + got_ref[0].astype(F32) + got_ref[1].astype(F32) + got_ref[2].astype(F32))

    if axis == 0:
        h_map = lambda i, chip_ref: (chip_ref[0] * (pr // tr) + i, 0)
    else:
        h_map = lambda i, chip_ref: (i, chip_ref[0])
    return pallas_call(
        body, name=name, out_shape=jax.ShapeDtypeStruct((pr, pc), F32), compiler_params=SUM_PARAMS,
        grid_spec=pltpu.PrefetchScalarGridSpec(
            num_scalar_prefetch=1, grid=(pr // tr,),
            in_specs=[pl.BlockSpec((tr, pc), h_map), pl.BlockSpec((3, tr, pc), lambda i, chip_ref: (0, i, 0))],
            out_specs=pl.BlockSpec((tr, pc), lambda i, chip_ref: (i, 0))),
    )(chip, half, got)


def _adamw_math(w, g, m, v):
    nm = ADAM_B1 * m + (1.0 - ADAM_B1) * g
    nv = ADAM_B2 * v + (1.0 - ADAM_B2) * (g * g)
    m_hat = nm / (1.0 - ADAM_B1 ** ADAM_STEP)
    v_hat = nv / (1.0 - ADAM_B2 ** ADAM_STEP)
    return -ADAM_LR * (m_hat / (jnp.sqrt(v_hat) + ADAM_EPS) + ADAM_WD * w), nm, nv


def adamw_halves(w, mine, theirs, m, v, core, rows, cols, axis, name):
    sr, sc = _shard_shape(rows, cols, axis)
    pr, pc = _piece_shape(rows, cols, axis)
    tr = _row_tile(pr, pc, 9)
    nt = pr // tr

    def body(core_ref, w_ref, a_ref, b_ref, m_ref, v_ref, g_ref, d_ref, nm_ref, nv_ref):
        g = jnp.where(pl.program_id(0) == core_ref[0], a_ref[...], b_ref[...])
        g_ref[...] = g
        d_ref[...], nm_ref[...], nv_ref[...] = _adamw_math(w_ref[...], g, m_ref[...], v_ref[...])

    if axis == 0:
        full = pl.BlockSpec((tr, pc), lambda h, i, core_ref: (i, h))
    else:
        full = pl.BlockSpec((tr, pc), lambda h, i, core_ref: (h * nt + i, 0))
    part = pl.BlockSpec((tr, pc), lambda h, i, core_ref: (i, 0))
    out = jax.ShapeDtypeStruct((sr, sc), F32)
    return pallas_call(
        body, name=name, out_shape=[out, out, out, out], compiler_params=SUM_PARAMS,
        grid_spec=pltpu.PrefetchScalarGridSpec(
            num_scalar_prefetch=1, grid=(2, nt), in_specs=[full, part, part, full, full], out_specs=[full] * 4),
    )(core, w, mine, theirs, m, v)


BIG = (
    ("ffn1_w_gate_up", D_MODEL, 2 * D_FF, 1),
    ("ffn1_w_down", D_FF, D_MODEL, 0),
    ("w_in", D_MODEL, IN_COLS, 1),
    ("w_branch_hg", HG_WIDTH, D_MODEL, 1),
    ("w_branch_att", ATT_WIDTH, D_MODEL, 1),
    ("w_out", D_MODEL, D_MODEL, 0),
    ("ffn2_w_gate_up", D_MODEL, 2 * D_FF, 1),
    ("ffn2_w_down", D_FF, D_MODEL, 0),
)
N_BIG = len(BIG)
ANY = pl.BlockSpec(memory_space=pl.ANY)


def _place():
    return lax.axis_index("x"), lax.axis_index("y"), lax.axis_index("c")


def _other_chips(x, y):
    return ((1 - x, y), (x, 1 - y), (1 - x, 1 - y))


MAX_COPY_CHUNKS = 16
CHUNK_ROW_ALIGN = 16


def _row_chunks(view):
    rows = view.shape[0]
    n = next(n for n in range(MAX_COPY_CHUNKS, 0, -1) if rows % (CHUNK_ROW_ALIGN * n) == 0 or n == 1)
    step = rows // n
    return [pl.ds(i * step, step) for i in range(n)]


def _remote(src, dst, send_sem, recv_sem, device):
    return pltpu.make_async_remote_copy(src_ref=src, dst_ref=dst, send_sem=send_sem, recv_sem=recv_sem,
                                        device_id=device, device_id_type=MESH)


def _start_remote(src, dst, send_sem, recv_sem, device):
    for rows in _row_chunks(src):
        _remote(src.at[rows, :], dst.at[rows, :], send_sem, recv_sem, device).start()
    return _remote(src, dst, send_sem, recv_sem, device)


HBM = pl.BlockSpec(memory_space=pltpu.HBM)
SEM = pl.BlockSpec(memory_space=pltpu.SEMAPHORE)
SPLIT_COPY_EFFECT = pltpu.SideEffectType.DATAFLOW_SIDE_EFFECTING
GROUPS = {"ffn1": (0, 1), "mix": (2, 3, 4, 5), "ffn2": (6, 7)}


class _SemList:
    def __init__(self, refs):
        self.refs = refs
        self.at = self

    def __getitem__(self, index):
        w, k = index
        return self.refs[3 * w + k]


def _gather_piece(ref, rows, cols, axis, chip, c):
    sr, sc = _shard_shape(rows, cols, axis)
    j = 2 * chip[0] + chip[1]
    if axis == 0:
        return ref.at[pl.ds(j * sr + c * (sr // 2), sr // 2), :]
    return ref.at[pl.ds(c * (sr // 2), sr // 2), pl.ds(pl.multiple_of(j * sc, 128), sc)]


def _start_gather_sends(bufs, ws, send_sems, recv_sems):
    x, y, c = _place()
    for w, (_, r, cc, ax) in enumerate(ws):
        mine = _gather_piece(bufs[w], r, cc, ax, (x, y), c)
        for k, chip in enumerate(_other_chips(x, y)):
            _start_remote(mine, mine, send_sems.at[w, k], recv_sems.at[w, k], (*chip, c))


def _wait_gather_sends(bufs, ws, send_sems, recv_sems):
    x, y, c = _place()
    for w, (_, r, cc, ax) in enumerate(ws):
        for k, chip in enumerate(_other_chips(x, y)):
            got = _gather_piece(bufs[w], r, cc, ax, chip, c)
            _remote(got, got, send_sems.at[w, k], recv_sems.at[w, k], (x, y, c)).wait_recv()
    for w, (_, r, cc, ax) in enumerate(ws):
        mine = _gather_piece(bufs[w], r, cc, ax, (x, y), c)
        for k in range(3):
            _remote(mine, mine, send_sems.at[w, k], recv_sems.at[w, k], (x, y, c)).wait_send()


def _start_forward_sends(bufs, ws, send_sems, recv_sems):
    x, y, c = _place()
    for w, (_, r, cc, ax) in enumerate(ws):
        for k, chip in enumerate(_other_chips(x, y)):
            got = _gather_piece(bufs[w], r, cc, ax, chip, c)
            _start_remote(got, got, send_sems.at[w, k], recv_sems.at[w, k], (x, y, 1 - c))


def _wait_forward_sends(bufs, ws, send_sems, recv_sems):
    x, y, c = _place()
    for w, (_, r, cc, ax) in enumerate(ws):
        for k, chip in enumerate(_other_chips(x, y)):
            got = _gather_piece(bufs[w], r, cc, ax, chip, 1 - c)
            _remote(got, got, send_sems.at[w, k], recv_sems.at[w, k], (x, y, c)).wait_recv()
    for w, (_, r, cc, ax) in enumerate(ws):
        for k, chip in enumerate(_other_chips(x, y)):
            sent = _gather_piece(bufs[w], r, cc, ax, chip, c)
            _remote(sent, sent, send_sems.at[w, k], recv_sems.at[w, k], (x, y, c)).wait_send()


GATHER_HOPS = {"gather": (_start_gather_sends, _wait_gather_sends), "forward": (_start_forward_sends, _wait_forward_sends)}


def gather_start(placed, after, group, hop):
    ws = [BIG[i] for i in GROUPS[group]]
    n = len(ws)

    def body(*refs):
        bufs = refs[:n]
        send_sems, recv_sems = _SemList(refs[n + 1:4 * n + 1]), _SemList(refs[4 * n + 1:7 * n + 1])
        token = refs[-1]
        GATHER_HOPS[hop][0](bufs, ws, send_sems, recv_sems)
        token[...] = jnp.zeros_like(token)

    out = pallas_call(
        body, name=f"{hop}_start_{group}", in_specs=[HBM] * n + [ANY],
        out_specs=[SEM] * (6 * n) + [HBM] * n + [pl.BlockSpec(memory_space=pltpu.VMEM)],
        out_shape=[pltpu.SemaphoreType.DMA(())] * (6 * n)
        + [pltpu.HBM((r, cc), WEIGHT_COMM_DTYPE) for _, r, cc, _ in ws] + [jax.ShapeDtypeStruct((8, 128), F32)],
        input_output_aliases={w: 6 * n + w for w in range(n)},
        compiler_params=pltpu.CompilerParams(has_side_effects=SPLIT_COPY_EFFECT),
    )(*[_in_hbm(p) for p in placed], after)
    return out[:3 * n], out[3 * n:6 * n], out[6 * n:7 * n], out[-1]


def gather_wait(bufs, send_sems, recv_sems, after, group, hop):
    ws = [BIG[i] for i in GROUPS[group]]
    n = len(ws)

    def body(*refs):
        GATHER_HOPS[hop][1](refs[:n], ws, _SemList(refs[n:n + 3 * n]), _SemList(refs[n + 3 * n:n + 6 * n]))

    return pallas_call(
        body, name=f"{hop}_wait_{group}", in_specs=[HBM] * n + [SEM] * (6 * n) + [ANY] * len(after), out_specs=[HBM] * n,
        out_shape=[pltpu.HBM((r, cc), WEIGHT_COMM_DTYPE) for _, r, cc, _ in ws],
        input_output_aliases={w: w for w in range(n)},
        compiler_params=pltpu.CompilerParams(has_side_effects=SPLIT_COPY_EFFECT),
    )(*bufs, *send_sems, *recv_sems, *after)


def _half(ref, rows, cols, axis, c):
    if axis == 0:
        return ref.at[:, pl.ds(pl.multiple_of(c * (cols // 2), 128), cols // 2)]
    return ref.at[pl.ds(c * (rows // 2), rows // 2), :]


def _piece_of_half(ref, rows, cols, axis, chip):
    j = 2 * chip[0] + chip[1]
    pr, pc = _piece_shape(rows, cols, axis)
    if axis == 0:
        return ref.at[pl.ds(j * pr, pr), :]
    return ref.at[:, pl.ds(pl.multiple_of(j * pc, 128), pc)]


def sibling_exchange_start(srcs, view, landing_shapes, dtype, name):
    n = len(srcs)

    def body(*refs):
        ins, land, sems = refs[:n], refs[n:2 * n], refs[2 * n:4 * n]
        x, y, c = _place()
        for w in range(n):
            _start_remote(view(ins[w], w, c), land[w], sems[w], sems[n + w], (x, y, 1 - c))
        refs[-1][...] = jnp.zeros_like(refs[-1])

    landing = [lax.empty(shape, dtype) for shape in landing_shapes]
    out = pallas_call(
        body, name=name, in_specs=[HBM] * (2 * n),
        out_specs=[SEM] * (2 * n) + [HBM] * (2 * n) + [pl.BlockSpec(memory_space=pltpu.VMEM)],
        out_shape=[pltpu.SemaphoreType.DMA(())] * (2 * n) + [pltpu.HBM(a.shape, a.dtype) for a in srcs]
        + [pltpu.HBM(shape, dtype) for shape in landing_shapes] + [jax.ShapeDtypeStruct((8, 128), F32)],
        input_output_aliases={i: 2 * n + i for i in range(2 * n)},
        compiler_params=pltpu.CompilerParams(has_side_effects=SPLIT_COPY_EFFECT),
    )(*[_in_hbm(a) for a in srcs], *[_in_hbm(b) for b in landing])
    return out[:n], out[n:2 * n], out[2 * n:3 * n], out[3 * n:4 * n], out[-1]


def sibling_exchange_wait(srcs, landing, send_sems, recv_sems, view, after, name):
    n = len(srcs)

    def body(*refs):
        ins, land, sems = refs[:n], refs[n:2 * n], refs[2 * n:4 * n]
        x, y, c = _place()
        for w in range(n):
            cp = _remote(view(ins[w], w, c), land[w], sems[w], sems[n + w], (x, y, c))
            cp.wait_send()
            cp.wait_recv()

    out = pallas_call(
        body, name=name, in_specs=[HBM] * (2 * n) + [SEM] * (2 * n) + [ANY] * len(after), out_specs=[HBM] * (2 * n),
        out_shape=[pltpu.HBM(a.shape, a.dtype) for a in srcs] + [pltpu.HBM(b.shape, b.dtype) for b in landing],
        input_output_aliases={i: i for i in range(2 * n)},
        compiler_params=pltpu.CompilerParams(has_side_effects=SPLIT_COPY_EFFECT),
    )(*srcs, *landing, *send_sems, *recv_sems, *after)
    return out[:n], out[n:]


def _scatter_copies(halves, got, ws, send_sems, recv_sems, start):
    x, y, c = _place()
    copies = []
    for w, (_, r, cc, ax) in enumerate(ws):
        for k, chip in enumerate(_other_chips(x, y)):
            args = (_piece_of_half(halves[w], r, cc, ax, chip), got[w].at[k], send_sems.at[w, k], recv_sems.at[w, k], (*chip, c))
            copies.append(_start_remote(*args) if start else _remote(*args))
    return copies


def scatter_start(halves, group):
    ws = [BIG[i] for i in GROUPS[group]]
    n = len(ws)

    def body(*refs):
        sems = refs[2 * n:8 * n]
        _scatter_copies(refs[:n], refs[n:2 * n], ws, _SemList(sems[:3 * n]), _SemList(sems[3 * n:]), start=True)
        refs[-1][...] = jnp.zeros_like(refs[-1])

    landing = [lax.empty((3,) + _piece_shape(r, cc, ax), GRAD_COMM_DTYPE) for _, r, cc, ax in ws]
    out = pallas_call(
        body, name=f"scatter_start_{group}", in_specs=[HBM] * (2 * n),
        out_specs=[SEM] * (6 * n) + [HBM] * (2 * n) + [pl.BlockSpec(memory_space=pltpu.VMEM)],
        out_shape=[pltpu.SemaphoreType.DMA(())] * (6 * n)
        + [pltpu.HBM(_half_shape(r, cc, ax), GRAD_COMM_DTYPE) for _, r, cc, ax in ws]
        + [pltpu.HBM((3,) + _piece_shape(r, cc, ax), GRAD_COMM_DTYPE) for _, r, cc, ax in ws]
        + [jax.ShapeDtypeStruct((8, 128), F32)],
        input_output_aliases={i: 6 * n + i for i in range(2 * n)},
        compiler_params=pltpu.CompilerParams(has_side_effects=SPLIT_COPY_EFFECT),
    )(*[_in_hbm(h) for h in halves], *[_in_hbm(b) for b in landing])
    return out[:3 * n], out[3 * n:6 * n], out[6 * n:7 * n], out[7 * n:8 * n], out[-1]


def scatter_wait(halves, got, send_sems, recv_sems, after, group):
    ws = [BIG[i] for i in GROUPS[group]]
    n = len(ws)

    def body(*refs):
        sems = refs[2 * n:8 * n]
        for cp in _scatter_copies(refs[:n], refs[n:2 * n], ws, _SemList(sems[:3 * n]), _SemList(sems[3 * n:]), start=False):
            cp.wait_send()
            cp.wait_recv()

    out = pallas_call(
        body, name=f"scatter_wait_{group}", in_specs=[HBM] * (2 * n) + [SEM] * (6 * n) + [ANY] * len(after),
        out_specs=[HBM] * (2 * n),
        out_shape=[pltpu.HBM(_half_shape(r, cc, ax), GRAD_COMM_DTYPE) for _, r, cc, ax in ws]
        + [pltpu.HBM((3,) + _piece_shape(r, cc, ax), GRAD_COMM_DTYPE) for _, r, cc, ax in ws],
        input_output_aliases={i: i for i in range(2 * n)},
        compiler_params=pltpu.CompilerParams(has_side_effects=SPLIT_COPY_EFFECT),
    )(*halves, *got, *send_sems, *recv_sems, *after)
    return out[:n], out[n:]


N_DEV = 8
SMALL = ("ffn1_norm", "mix_norm", "hg_lower_bounds", "hg_out_norm", "ffn2_norm", "final_norm")
SMALL_STAGE_ROWS = 8


def small_step(loss, grads, w, m, v, behind):
    n = len(SMALL)
    shapes = [g.shape for g in grads]
    first_row = [sum(s[0] for s in shapes[:i]) for i in range(n + 1)]
    assert first_row[n] < SMALL_STAGE_ROWS
    loss_row = (pl.ds(first_row[n], 1), pl.ds(0, loss.shape[1]))

    def body(*refs):
        loss_ref, g_refs, w_refs, m_refs, v_refs = refs[0], refs[1:1 + n], refs[1 + n:1 + 2 * n], refs[1 + 2 * n:1 + 3 * n], refs[1 + 3 * n:1 + 4 * n]
        outs = refs[2 + 4 * n:3 + 8 * n]
        loss_out, dg_refs, d_refs, nm_refs, nv_refs = outs[0], outs[1:1 + n], outs[1 + n:1 + 2 * n], outs[1 + 2 * n:1 + 3 * n], outs[1 + 3 * n:]
        stage, gathered, send_sems, recv_sems = refs[3 + 8 * n:]
        x, y, c = _place()
        me = 4 * x + 2 * y + c

        def slot(i, shape):
            return pl.ds(first_row[i], shape[0]), pl.ds(0, shape[1])

        stage[...] = jnp.zeros_like(stage)
        for i, g_ref in enumerate(g_refs):
            stage[slot(i, shapes[i])] = g_ref[...]
        stage[loss_row] = loss_ref[pl.ds(0, 1), :]
        gathered[me] = stage[...]
        copies = []
        for k in range(1, N_DEV):
            peer = (x ^ (k >> 2), y ^ ((k >> 1) & 1), c ^ (k & 1))
            cp = pltpu.make_async_remote_copy(
                src_ref=stage, dst_ref=gathered.at[me], send_sem=send_sems.at[k - 1], recv_sem=recv_sems.at[k - 1],
                device_id=peer, device_id_type=MESH)
            cp.start()
            copies.append(cp)
        for cp in copies:
            cp.wait()
        acc = gathered[0]
        for k in range(1, N_DEV):
            acc = acc + gathered[k]
        stage[...] = acc
        loss_out[...] = jnp.broadcast_to(stage[loss_row], loss_out.shape)
        for i in range(n):
            g = stage[slot(i, shapes[i])]
            dg_refs[i][...] = g
            d_refs[i][...], nm_refs[i][...], nv_refs[i][...] = _adamw_math(w_refs[i][...], g, m_refs[i][...], v_refs[i][...])

    vm = pl.BlockSpec(memory_space=pltpu.VMEM)
    per_param = [jax.ShapeDtypeStruct(s, F32) for s in shapes]
    out = pallas_call(
        body, name="small_step", in_specs=[vm] * (1 + 4 * n) + [ANY], out_specs=[vm] * (1 + 4 * n),
        out_shape=[jax.ShapeDtypeStruct(loss.shape, F32)] + per_param * 4,
        scratch_shapes=[pltpu.VMEM((SMALL_STAGE_ROWS, D_MODEL), F32),
                        pltpu.VMEM((N_DEV, SMALL_STAGE_ROWS, D_MODEL), F32),
                        pltpu.SemaphoreType.DMA((N_DEV - 1,)), pltpu.SemaphoreType.DMA((N_DEV - 1,))],
    )(loss, *grads, *w, *m, *v, behind)
    return out[0], out[1:1 + n], out[1 + n:1 + 2 * n], out[1 + 2 * n:1 + 3 * n], out[1 + 3 * n:]


def _swiglu_block_fwd(h, n, w_gu, w_down, tag, behind=()):
    a, b, s = gate_up_swiglu(n, w_gu, f"{tag}_gate_up", behind=behind)
    h_out = matmul(s, w_down, res=h, scale=0.5, name=f"{tag}_down")
    return h_out, (n, a, b, s)


def _swiglu_block_bwd(h, norm_g, w_gu, w_down, saved, dh_out, df, tag, exchange, behind=()):
    n, a, b, s = saved
    d_down = matmul(s, df, ta=True, scale=0.5, out_dtype=GRAD_COMM_DTYPE, name=f"{tag}_d_w_down")
    ds = matmul(df, w_down, tb=True, scale=0.5, out_dtype=ACT_DTYPE, behind=behind, name=f"{tag}_d_s")
    dgu = swiglu_bwd(a, b, ds, f"{tag}_swiglu_bwd")
    d_gu = matmul(n, dgu, ta=True, out_dtype=GRAD_COMM_DTYPE, name=f"{tag}_d_w_gate_up")
    tokens = exchange.gradients_ready(tag, {f"{tag}_w_gate_up": d_gu, f"{tag}_w_down": d_down})
    dn = matmul(dgu, w_gu, tb=True, behind=tokens, name=f"{tag}_d_n")
    dh, dh_m, dg = rmsnorm_bwd(h, norm_g, dn, dh_out, f"{tag}_norm_bwd")
    return dh, dh_m, dg


def local_step(x, target, small, exchange):
    big = {}
    n1 = rmsnorm_fwd(x, small["ffn1_norm"], "ffn1_norm", behind=exchange.arriving("ffn1", x))
    big.update(exchange.weights("ffn1", n1))
    h1, saved1 = _swiglu_block_fwd(x, n1, big["ffn1_w_gate_up"], big["ffn1_w_down"], "ffn1")
    u = rmsnorm_fwd(h1, small["mix_norm"], "mix_norm", behind=exchange.arriving("mix", h1))
    big.update(exchange.weights("mix", u))
    z = matmul(u, big["w_in"], name="w_in")
    p = small["hg_lower_bounds"]
    lb = 1.0 / (1.0 + jnp.exp(p[1:2] - p[0:1]))
    y_hg, o_raw, states = hgrn_fwd(z, lb, small["hg_out_norm"], "hgrn_fwd")
    o_att, l_att = zip(*[att_fwd(z, g, f"att_fwd_{g}") for g in range(N_GROUPS)])
    y_att = att_combine_fwd(o_att, l_att, "att_combine")
    bh = matmul(y_hg, big["w_branch_hg"], name="branch_hg")
    ba = matmul(y_att, big["w_branch_att"], name="branch_att")
    merged = merge_fwd(z, bh, ba, "merge")
    h2 = matmul(merged, big["w_out"], res=h1, behind=exchange.arriving("ffn2", merged), name="w_out")
    n2 = rmsnorm_fwd(h2, small["ffn2_norm"], "ffn2_norm")
    big.update(exchange.weights("ffn2", n2))
    h3, saved2 = _swiglu_block_fwd(h2, n2, big["ffn2_w_gate_up"], big["ffn2_w_down"], "ffn2")
    dh3, dh3_m, d_final, loss = final_norm_loss(h3, small["final_norm"], target, "final_norm_loss")

    gs, gb = {"final_norm": d_final}, {}
    dh2, dh2_m, gs["ffn2_norm"] = _swiglu_block_bwd(
        h2, small["ffn2_norm"], big["ffn2_w_gate_up"], big["ffn2_w_down"], saved2, dh3, dh3_m, "ffn2", exchange)
    token = exchange.backward_done("ffn2", dh2)
    gb["w_out"] = matmul(merged, dh2_m, ta=True, out_dtype=GRAD_COMM_DTYPE, name="d_w_out")
    dmerged = matmul(dh2_m, big["w_out"], tb=True, behind=token, name="d_merged")
    dbh, dba, dgh, dga = merge_bwd(z, bh, ba, dmerged, "merge_bwd")
    gb["w_branch_hg"] = matmul(y_hg, dbh, ta=True, out_dtype=GRAD_COMM_DTYPE, name="d_w_branch_hg")
    gb["w_branch_att"] = matmul(y_att, dba, ta=True, out_dtype=GRAD_COMM_DTYPE, name="d_w_branch_att")
    dy_hg = matmul(dbh, big["w_branch_hg"], tb=True, name="d_y_hg")
    dy_att = matmul(dba, big["w_branch_att"], tb=True, name="d_y_att")
    dq, dfp, di, dog, d_lb, gs["hg_out_norm"] = hgrn_bwd(z, lb, small["hg_out_norm"], o_raw, states, dy_hg, "hgrn_bwd")
    do_att, corr = att_combine_bwd(o_att, l_att, dy_att, "att_combine_bwd")
    d_att = [part for g in range(N_GROUPS) for part in att_bwd(z, l_att[g], do_att[g], corr[g], g, f"att_bwd_{g}")]
    dz = jnp.concatenate([dq, dfp, di, dog, *d_att, dgh, dga], axis=1)
    gb["w_in"] = matmul(u, dz, ta=True, out_dtype=GRAD_COMM_DTYPE, name="d_w_in")
    token = exchange.gradients_ready("mix", gb)
    du = matmul(dz, big["w_in"], tb=True, behind=token, name="d_u")
    dh1, dh1_m, gs["mix_norm"] = rmsnorm_bwd(h1, small["mix_norm"], du, dh2, "mix_norm_bwd")
    token = exchange.backward_done("mix", dh1)
    dp0 = d_lb * lb * (1.0 - lb)
    gs["hg_lower_bounds"] = jnp.concatenate([dp0, -dp0], axis=0)
    dx, _, gs["ffn1_norm"] = _swiglu_block_bwd(
        x, small["ffn1_norm"], big["ffn1_w_gate_up"], big["ffn1_w_down"], saved1, dh1, dh1_m, "ffn1", exchange, token)
    exchange.backward_done("ffn1", dx)
    return loss, dx, gs


WEIGHTS = ("ffn1_norm", "ffn1_w_gate_up", "ffn1_w_down", "mix_norm", "w_in", "hg_lower_bounds", "hg_out_norm",
           "w_branch_hg", "w_branch_att", "w_out", "ffn2_norm", "ffn2_w_gate_up", "ffn2_w_down", "final_norm")


class WeightExchange:
    ORDER = ("ffn1", "mix", "ffn2")

    def __init__(self, shards, core, chip):
        self.core, self.chip = core, chip
        self.halving = None
        self.scattering = None
        self.reducing = {}
        first = self.ORDER[0]
        self.placed = {BIG[i][0]: place_own_block(shards[BIG[i][0]], chip, *BIG[i][1:], f"place_{BIG[i][0]}")
                       for i in GROUPS[first]}
        self._start_gather(first, chip)
        chip_behind = chip + self.token[0, :1].astype(jnp.int32)
        for group in self.ORDER[1:]:
            for i in GROUPS[group]:
                n, r, cc, ax = BIG[i]
                self.placed[n] = place_own_block(shards[n], chip_behind, r, cc, ax, f"place_{n}")
        self.placed_behind = [self.placed[n] for group in self.ORDER[1:] for n in self._names(group)]

    def _names(self, group):
        return [BIG[i][0] for i in GROUPS[group]]

    def _start_gather(self, group, after):
        send_sems, recv_sems, bufs, self.token = gather_start([self.placed[n] for n in self._names(group)], after, group,
                                                              "gather")
        self.gathering = (group, send_sems, recv_sems, bufs)

    def arriving(self, group, h):
        pending, send_sems, recv_sems, bufs = self.gathering
        assert pending == group
        after = [h] + (self.placed_behind if group == self.ORDER[0] else [])
        bufs = gather_wait(bufs, send_sems, recv_sems, after, group, "gather")
        send_sems, recv_sems, bufs, token = gather_start(bufs, h, group, "forward")
        self.forwarding = (group, send_sems, recv_sems, bufs)
        later = self.ORDER.index(group) + 1
        if later == len(self.ORDER):
            return [token]
        self._start_gather(self.ORDER[later], token)
        return [token, self.token]

    def weights(self, group, h):
        pending, send_sems, recv_sems, bufs = self.forwarding
        assert pending == group
        whole = gather_wait(bufs, send_sems, recv_sems, [h], group, "forward")
        return dict(zip(self._names(group), whole))

    @staticmethod
    def _half_to_sibling(ws):
        return lambda ref, w, c: _half(ref, *ws[w][1:], 1 - c)

    def gradients_ready(self, group, grads):
        ws = [BIG[i] for i in GROUPS[group]]
        send_sems, recv_sems, own, theirs, token = sibling_exchange_start(
            [grads[n] for n, *_ in ws], self._half_to_sibling(ws), [_half_shape(r, cc, ax) for _, r, cc, ax in ws],
            GRAD_COMM_DTYPE, f"halves_start_{group}")
        self.halving = (group, send_sems, recv_sems, own, theirs)
        return [token]

    def backward_done(self, group, dh):
        pending, send_sems, recv_sems, own, theirs = self.halving
        assert pending == group
        ws = [BIG[i] for i in GROUPS[group]]
        own, theirs = sibling_exchange_wait(own, theirs, send_sems, recv_sems, self._half_to_sibling(ws), [dh],
                                            f"halves_wait_{group}")
        halves = [add_halves(g, t, self.core, r, cc, ax, f"add_halves_{n}") for (n, r, cc, ax), g, t in zip(ws, own, theirs)]
        previous = self.scattering
        send_sems, recv_sems, halves, got, self.token = scatter_start(halves, group)
        self.scattering = (group, send_sems, recv_sems, halves, got)
        behind = [self._finish_scatter(previous, [self.token])] if previous is not None else []
        return behind + [self.token]

    def _finish_scatter(self, scattering, after):
        group, send_sems, recv_sems, halves, got = scattering
        halves, got = scatter_wait(halves, got, send_sems, recv_sems, after, group)
        ws = [BIG[i] for i in GROUPS[group]]
        mine = [add_pieces(h, g, self.chip, r, cc, ax, f"add_pieces_{n}") for (n, r, cc, ax), h, g in zip(ws, halves, got)]
        send_sems, recv_sems, mine, theirs, token = sibling_exchange_start(
            mine, lambda ref, w, c: ref, [_piece_shape(r, cc, ax) for _, r, cc, ax in ws], F32, f"reduced_start_{group}")
        self.reducing[group] = (send_sems, recv_sems, mine, theirs)
        return token

    def finish(self, after):
        return self._finish_scatter(self.scattering, after)

    def reduced_halves(self, group, after):
        send_sems, recv_sems, mine, theirs = self.reducing.pop(group)
        mine, theirs = sibling_exchange_wait(mine, theirs, send_sems, recv_sems, lambda ref, w, c: ref, after,
                                             f"reduced_wait_{group}")
        return {BIG[i][0]: (a, b) for i, a, b in zip(GROUPS[group], mine, theirs)}


def kernel(x, ffn1_norm, ffn1_w_gate_up, ffn1_w_down, mix_norm, w_in, hg_lower_bounds, hg_out_norm, w_branch_hg, w_branch_att, w_out, ffn2_norm, ffn2_w_gate_up, ffn2_w_down, final_norm, loss_target, m_ffn1_norm, m_ffn1_w_gate_up, m_ffn1_w_down, m_mix_norm, m_w_in, m_hg_lower_bounds, m_hg_out_norm, m_w_branch_hg, m_w_branch_att, m_w_out, m_ffn2_norm, m_ffn2_w_gate_up, m_ffn2_w_down, m_final_norm, v_ffn1_norm, v_ffn1_w_gate_up, v_ffn1_w_down, v_mix_norm, v_w_in, v_hg_lower_bounds, v_hg_out_norm, v_w_branch_hg, v_w_branch_att, v_w_out, v_ffn2_norm, v_ffn2_w_gate_up, v_ffn2_w_down, v_final_norm):
    w = dict(ffn1_norm=ffn1_norm, ffn1_w_gate_up=ffn1_w_gate_up, ffn1_w_down=ffn1_w_down, mix_norm=mix_norm, w_in=w_in,
             hg_lower_bounds=hg_lower_bounds, hg_out_norm=hg_out_norm, w_branch_hg=w_branch_hg, w_branch_att=w_branch_att,
             w_out=w_out, ffn2_norm=ffn2_norm, ffn2_w_gate_up=ffn2_w_gate_up, ffn2_w_down=ffn2_w_down, final_norm=final_norm)
    m = dict(ffn1_norm=m_ffn1_norm, ffn1_w_gate_up=m_ffn1_w_gate_up, ffn1_w_down=m_ffn1_w_down, mix_norm=m_mix_norm,
             w_in=m_w_in, hg_lower_bounds=m_hg_lower_bounds, hg_out_norm=m_hg_out_norm, w_branch_hg=m_w_branch_hg,
             w_branch_att=m_w_branch_att, w_out=m_w_out, ffn2_norm=m_ffn2_norm, ffn2_w_gate_up=m_ffn2_w_gate_up,
             ffn2_w_down=m_ffn2_w_down, final_norm=m_final_norm)
    v = dict(ffn1_norm=v_ffn1_norm, ffn1_w_gate_up=v_ffn1_w_gate_up, ffn1_w_down=v_ffn1_w_down, mix_norm=v_mix_norm,
             w_in=v_w_in, hg_lower_bounds=v_hg_lower_bounds, hg_out_norm=v_hg_out_norm, w_branch_hg=v_w_branch_hg,
             w_branch_att=v_w_branch_att, w_out=v_w_out, ffn2_norm=v_ffn2_norm, ffn2_w_gate_up=v_ffn2_w_gate_up,
             ffn2_w_down=v_ffn2_w_down, final_norm=v_final_norm)

    core = lax.axis_index("c").astype(jnp.int32).reshape(1)
    chip = (2 * lax.axis_index("x") + lax.axis_index("y")).astype(jnp.int32).reshape(1)
    exchange = WeightExchange({n: w[n][0] for n, *_ in BIG}, core, chip)
    small = {n: w[n] for n in SMALL}
    small["final_norm"] = final_norm.reshape(1, D_MODEL)

    loss, dx, gs = local_step(x[0], loss_target[0], small, exchange)

    grads, delta, new_m, new_v = {}, {}, {}, {}

    def update(group, core, after):
        reduced = exchange.reduced_halves(group, after)
        for i in GROUPS[group]:
            n, r, cc, ax = BIG[i]
            a, b = reduced[n]
            g, d, nm, nv = adamw_halves(w[n][0], a, b, m[n][0], v[n][0], core, r, cc, ax, f"adamw_{n}")
            grads[n], delta[n], new_m[n], new_v[n] = g[None], d[None], nm[None], nv[None]

    core_behind = core + exchange.token[0, :1].astype(jnp.int32)
    update("ffn2", core_behind, [exchange.token])
    update("mix", core_behind, [delta["ffn2_w_down"]])
    token = exchange.finish(after=[delta[BIG[i][0]] for group in ("ffn2", "mix") for i in GROUPS[group]])
    two_d = lambda a: a.reshape(1, D_MODEL) if a.ndim == 1 else a
    loss_sum, *small_out = small_step(loss, [gs[n] for n in SMALL], *[[two_d(p[n]) for n in SMALL] for p in (w, m, v)],
                                      behind=token)
    for result, parts in zip((grads, delta, new_m, new_v), small_out):
        result.update({n: a.reshape(w[n].shape) for n, a in zip(SMALL, parts)})
    update("ffn1", core, [loss_sum])

    return (loss_sum[0, 0], dx[None], *[grads[n] for n in WEIGHTS], *[delta[n] for n in WEIGHTS],
            *[new_m[n] for n in WEIGHTS], *[new_v[n] for n in WEIGHTS])
```
